```python
import jax, jax.numpy as jnp
from jax import lax
import numpy as np

D_MODEL = 2048
BATCH = 8
SEQ = 8192
DEPTH = 1

GRID_W = 64
CTX_LEN = 256
HEAD_DIM = 128
N_Q_HEADS = 16
N_KV_HEADS = 4
Q_PER_KV = N_Q_HEADS // N_KV_HEADS
WINDOW = 128
BLOCK = 128
ROPE_THETA = 10000.0
GLA_HEADS = 4
GLA_DK = D_MODEL // 2 // GLA_HEADS
GLA_DV = D_MODEL // GLA_HEADS
GLA_LOWRANK = 16
GLA_GATE_NORM = 16.0
GLA_CHUNK = 64
D_FF = 5632
CONV_W = 3
EPS = 1e-6
ATTN_WIDTH = N_Q_HEADS * HEAD_DIM
KV_WIDTH = N_KV_HEADS * HEAD_DIM
GLA_K_WIDTH = GLA_HEADS * GLA_DK
GLA_V_WIDTH = GLA_HEADS * GLA_DV
IN_SPLITS = (ATTN_WIDTH, KV_WIDTH, KV_WIDTH, GLA_K_WIDTH, GLA_K_WIDTH, GLA_V_WIDTH, GLA_V_WIDTH,
             GLA_LOWRANK, GLA_LOWRANK, D_MODEL, D_MODEL)
IN_WIDTH = sum(IN_SPLITS)

kernel_name = "hybrid_swa_gla_convffn_prefix_dit"


def rms_norm(x, g):
    xf = x.astype(jnp.float32)
    y = xf * lax.rsqrt(jnp.mean(xf * xf, axis=-1, keepdims=True) + EPS)
    return (y * g.astype(jnp.float32)).astype(x.dtype)


def modulate(x, g, shift, scale):
    return rms_norm(x, g) * (1 + scale) + shift


def split_heads(a, n_heads):
    return a.reshape(*a.shape[:-1], n_heads, -1)


def axial_rope_angles(n):
    rows = n // GRID_W
    row = jnp.repeat(jnp.arange(rows), GRID_W)
    col = jnp.tile(jnp.arange(GRID_W), rows)
    n_freq = HEAD_DIM // 4
    inv = ROPE_THETA ** (-jnp.arange(n_freq, dtype=jnp.float32) / n_freq)
    ang = jnp.concatenate([row[:, None] * inv, col[:, None] * inv], axis=-1)
    return jnp.cos(ang), jnp.sin(ang)


def apply_rope(x, cos, sin):
    half = HEAD_DIM // 2
    x1, x2 = x[..., :half].astype(jnp.float32), x[..., half:].astype(jnp.float32)
    c, s = cos[None, :, None, :], sin[None, :, None, :]
    return jnp.concatenate([x1 * c - x2 * s, x2 * c + x1 * s], axis=-1).astype(x.dtype)


def gqa_scores(q5, k):
    return jnp.einsum("bqhgd,bkhd->bhgqk", q5, k).astype(jnp.float32) * (HEAD_DIM ** -0.5)


def attend_with_sink(parts, sink_g):
    sink_b = sink_g[None, :, :, None]
    m = sink_b
    for s, _ in parts:
        m = jnp.maximum(m, s.max(axis=-1))
    denom = jnp.exp(sink_b - m)
    out = None
    for s, val in parts:
        p = jnp.exp(s - m[..., None])
        denom = denom + p.sum(axis=-1)
        o = jnp.einsum("bhgqk,bkhd->bqhgd", p, val.astype(jnp.float32))
        out = o if out is None else out + o
    return out / jnp.transpose(denom, (0, 3, 1, 2))[..., None]


def windowed_attention(q, k, v, k_ctx, v_ctx, sink):
    B, n = q.shape[:2]
    nb = n // BLOCK
    span = BLOCK + 2 * WINDOW
    pad = ((0, 0), (WINDOW, WINDOW), (0, 0), (0, 0))
    kp, vp = jnp.pad(k, pad), jnp.pad(v, pad)
    qb = jnp.swapaxes(q.reshape(B, nb, BLOCK, N_KV_HEADS, Q_PER_KV, HEAD_DIM), 0, 1)
    sink_g = sink.reshape(N_KV_HEADS, Q_PER_KV).astype(jnp.float32)
    s_ctx_all = None

    def block(args):
        qi, i = args
        start = i * BLOCK
        kw = lax.dynamic_slice_in_dim(kp, start, span, axis=1)
        vw = lax.dynamic_slice_in_dim(vp, start, span, axis=1)
        qpos = start + jnp.arange(BLOCK)
        kpos = start - WINDOW + jnp.arange(span)
        valid = ((jnp.abs(qpos[:, None] - kpos[None, :]) <= WINDOW)
                 & (kpos >= 0)[None, :] & (kpos < n)[None, :])
        s_lat = jnp.where(valid, gqa_scores(qi, kw), -jnp.inf)
        s_ctx = gqa_scores(qi, k_ctx)
        return attend_with_sink([(s_lat, vw), (s_ctx, v_ctx)], sink_g)

    o = lax.map(block, (qb, jnp.arange(nb)))
    return jnp.swapaxes(o, 0, 1).reshape(B, n, ATTN_WIDTH).astype(q.dtype)


def context_attention(q, k, v, sink):
    B, L = q.shape[:2]
    q5 = q.reshape(B, L, N_KV_HEADS, Q_PER_KV, HEAD_DIM)
    sink_g = sink.reshape(N_KV_HEADS, Q_PER_KV).astype(jnp.float32)
    o = attend_with_sink([(gqa_scores(q5, k), v)], sink_g)
    return o.reshape(B, L, ATTN_WIDTH).astype(q.dtype)


def gla_chunked(q, k, v, g, s0):
    B, T, H, _ = q.shape
    dv = v.shape[-1]
    nc = T // GLA_CHUNK

    def to_chunks(a):
        return jnp.transpose(a.reshape(B, nc, GLA_CHUNK, H, a.shape[-1]), (1, 0, 3, 2, 4)).astype(jnp.float32)

    causal = jnp.tril(jnp.ones((GLA_CHUNK, GLA_CHUNK), dtype=bool))[..., None]

    def step(S, inp):
        qc, kc, vc, gc = inp
        b = jnp.cumsum(gc, axis=2)
        b_last = b[:, :, -1:]
        o_inter = jnp.einsum("bhcd,bhde->bhce", qc * jnp.exp(b), S)
        rel = jnp.where(causal, b[:, :, :, None, :] - b[:, :, None, :, :], -jnp.inf)
        A = jnp.einsum("bhtd,bhsd,bhtsd->bhts", qc, kc, jnp.exp(rel))
        o = o_inter + jnp.einsum("bhts,bhse->bhte", A, vc)
        S = (jnp.exp(b_last[:, :, 0])[..., None] * S
             + jnp.einsum("bhsd,bhse->bhde", kc * jnp.exp(b_last - b), vc))
        return S, o

    S, o = lax.scan(step, s0, (to_chunks(q), to_chunks(k), to_chunks(v), to_chunks(g)))
    o = jnp.transpose(o, (1, 0, 3, 2, 4)).reshape(B, T, H, dv)
    return o.astype(v.dtype), S


def gla_final_state(k, v, g):
    b = jnp.cumsum(g.astype(jnp.float32), axis=1)
    w = jnp.exp(b[:, -1:] - b)
    return jnp.einsum("bthd,bthe->bhde", k.astype(jnp.float32) * w, v.astype(jnp.float32))


def flip(a):
    return a[:, ::-1]


def context_gla(q, k, v, gf, gb, need_out):
    if need_out:
        s0 = jnp.zeros((q.shape[0], GLA_HEADS, GLA_DK, GLA_DV), jnp.float32)
        of, sf = gla_chunked(q, k, v, gf, s0)
        ob, sb = gla_chunked(flip(q), flip(k), flip(v), flip(gb), s0)
        return sf, sb, of + flip(ob)
    return gla_final_state(k, v, gf), gla_final_state(flip(k), flip(v), flip(gb)), None


def latent_gla(q, k, v, gf, gb, sf, sb):
    of, _ = gla_chunked(q, k, v, gf, sf)
    ob, _ = gla_chunked(flip(q), flip(k), flip(v), flip(gb), sb)
    return of + flip(ob)


def project_heads(h, w_in, q_norm, k_norm, w_gate_f, b_gate_f, w_gate_b, b_gate_b):
    offs = np.cumsum(IN_SPLITS)[:-1].tolist()
    qa, ka, va, qb, kb, vb, rb, lrf, lrb, gate_a, gate_b = jnp.split(h @ w_in, offs, axis=-1)
    qa = rms_norm(split_heads(qa, N_Q_HEADS), q_norm)
    ka = rms_norm(split_heads(ka, N_KV_HEADS), k_norm)
    va = split_heads(va, N_KV_HEADS)
    qb = split_heads(qb, GLA_HEADS) * (GLA_DK ** -0.5)
    kb = split_heads(kb, GLA_HEADS)
    vb = split_heads(vb, GLA_HEADS)
    gf = split_heads(jax.nn.log_sigmoid((lrf @ w_gate_f + b_gate_f).astype(jnp.float32)) / GLA_GATE_NORM, GLA_HEADS)
    gb = split_heads(jax.nn.log_sigmoid((lrb @ w_gate_b + b_gate_b).astype(jnp.float32)) / GLA_GATE_NORM, GLA_HEADS)
    return qa, ka, va, qb, kb, vb, rb, gf, gb, gate_a, gate_b


def merge_branches(o_attn, o_gla, rb, gate_a, gate_b, gla_norm, w_attn_o, w_gla_o, w_out):
    B, T = o_attn.shape[:2]
    y_att = o_attn @ w_attn_o
    y_gla = (rms_norm(o_gla, gla_norm).reshape(B, T, GLA_V_WIDTH) * jax.nn.silu(rb)) @ w_gla_o
    return (jax.nn.sigmoid(gate_a) * y_att + jax.nn.sigmoid(gate_b) * y_gla) @ w_out


def conv_ffn(h, w_up, conv_w, conv_b, w_down):
    T = h.shape[1]
    u = h @ w_up
    up = jnp.pad(u, ((0, 0), (CONV_W // 2, CONV_W // 2), (0, 0)))
    u = sum(up[:, j:j + T] * conv_w[j] for j in range(CONV_W)) + conv_b
    a, g = jnp.split(u, 2, axis=-1)
    return (jax.nn.silu(a) * g) @ w_down


def _fwd_setup_inputs(seed: int = 0) -> dict:
    key = jax.random.key(seed)
    ks = jax.random.split(key, 24)
    D, L = D_MODEL, DEPTH

    def nrm(k, shape, s):
        return jax.random.normal(k, shape, jnp.float32) * s

    return {
        "x": nrm(ks[0], (BATCH, SEQ, D), 1.0),
        "c": nrm(ks[1], (BATCH, D), 1.0),
        "ctx": nrm(ks[2], (BATCH, CTX_LEN, D), 1.0),
        "c_ctx": nrm(ks[3], (D,), 1.0),
        "w_mod": nrm(ks[4], (L, D, 6 * D), 0.5 * D ** -0.5),
        "b_mod": nrm(ks[5], (L, 6 * D), 0.02),
        "g_mix": 1.0 + nrm(ks[6], (L, D), 0.05),
        "w_in": nrm(ks[7], (L, D, IN_WIDTH), D ** -0.5),
        "q_norm": 1.0 + nrm(ks[8], (L, HEAD_DIM), 0.05),
        "k_norm": 1.0 + nrm(ks[9], (L, HEAD_DIM), 0.05),
        "attn_sink": nrm(ks[10], (L, N_Q_HEADS), 0.5),
        "w_gate_f": nrm(ks[11], (L, GLA_LOWRANK, GLA_K_WIDTH), GLA_LOWRANK ** -0.5),
        "b_gate_f": nrm(ks[12], (L, GLA_K_WIDTH), 0.1),
        "w_gate_b": nrm(ks[13], (L, GLA_LOWRANK, GLA_K_WIDTH), GLA_LOWRANK ** -0.5),
        "b_gate_b": nrm(ks[14], (L, GLA_K_WIDTH), 0.1),
        "gla_norm": 1.0 + nrm(ks[15], (L, GLA_DV), 0.05),
        "w_attn_o": nrm(ks[16], (L, ATTN_WIDTH, D), ATTN_WIDTH ** -0.5),
        "w_gla_o": nrm(ks[17], (L, GLA_V_WIDTH, D), GLA_V_WIDTH ** -0.5),
        "w_out": nrm(ks[18], (L, D, D), D ** -0.5),
        "g_ffn": 1.0 + nrm(ks[19], (L, D), 0.05),
        "w_up": nrm(ks[20], (L, D, 2 * D_FF), D ** -0.5),
        "conv_w": nrm(ks[21], (L, CONV_W, 2 * D_FF), CONV_W ** -0.5),
        "conv_b": nrm(ks[22], (L, 2 * D_FF), 0.02),
        "w_down": nrm(ks[23], (L, D_FF, D), D_FF ** -0.5),
    }


def _fwd_reference(x, c, ctx, c_ctx, w_mod, b_mod, g_mix, w_in, q_norm, k_norm, attn_sink,
              w_gate_f, b_gate_f, w_gate_b, b_gate_b, gla_norm, w_attn_o, w_gla_o, w_out,
              g_ffn, w_up, conv_w, conv_b, w_down):
    n = x.shape[1]
    cos, sin = axial_rope_angles(n)
    for l in range(DEPTH):
        last = l == DEPTH - 1
        mod_x = jnp.split((jax.nn.silu(c) @ w_mod[l] + b_mod[l])[:, None, :], 6, axis=-1)
        mod_c = jnp.split((jax.nn.silu(c_ctx) @ w_mod[l] + b_mod[l])[None, None, :], 6, axis=-1)
        proj = lambda h: project_heads(h, w_in[l], q_norm[l], k_norm[l],
                                       w_gate_f[l], b_gate_f[l], w_gate_b[l], b_gate_b[l])

        qa, ka, va, qb, kb, vb, rb, gf, gb, gate_a, gate_b = proj(modulate(x, g_mix[l], mod_x[0], mod_x[1]))
        qa, ka = apply_rope(qa, cos, sin), apply_rope(ka, cos, sin)
        cqa, cka, cva, cqb, ckb, cvb, crb, cgf, cgb, cgate_a, cgate_b = proj(
            modulate(ctx, g_mix[l], mod_c[0], mod_c[1]))

        o_attn = windowed_attention(qa, ka, va, cka, cva, attn_sink[l])
        sf, sb, o_gla_c = context_gla(cqb, ckb, cvb, cgf, cgb, need_out=not last)
        o_gla = latent_gla(qb, kb, vb, gf, gb, sf, sb)
        x = x + mod_x[2] * merge_branches(o_attn, o_gla, rb, gate_a, gate_b, gla_norm[l],
                                          w_attn_o[l], w_gla_o[l], w_out[l])
        x = x + mod_x[5] * conv_ffn(modulate(x, g_ffn[l], mod_x[3], mod_x[4]),
                                    w_up[l], conv_w[l], conv_b[l], w_down[l])

        if not last:
            o_attn_c = context_attention(cqa, cka, cva, attn_sink[l])
            ctx = ctx + mod_c[2] * merge_branches(o_attn_c, o_gla_c, crb, cgate_a, cgate_b, gla_norm[l],
                                                  w_attn_o[l], w_gla_o[l], w_out[l])
            ctx = ctx + mod_c[5] * conv_ffn(modulate(ctx, g_ffn[l], mod_c[3], mod_c[4]),
                                            w_up[l], conv_w[l], conv_b[l], w_down[l])
    return x


import jax as _jax
import jax.numpy as _jnp

TWIN_FORMAT = 'train_step'
FWD_PARAMS = ['x', 'c', 'ctx', 'c_ctx', 'w_mod', 'b_mod', 'g_mix', 'w_in', 'q_norm', 'k_norm', 'attn_sink', 'w_gate_f', 'b_gate_f', 'w_gate_b', 'b_gate_b', 'gla_norm', 'w_attn_o', 'w_gla_o', 'w_out', 'g_ffn', 'w_up', 'conv_w', 'conv_b', 'w_down']
TWIN_WEIGHTS = ['c_ctx', 'w_mod', 'b_mod', 'g_mix', 'w_in', 'q_norm', 'k_norm', 'attn_sink', 'w_gate_f', 'b_gate_f', 'w_gate_b', 'b_gate_b', 'gla_norm', 'w_attn_o', 'w_gla_o', 'w_out', 'g_ffn', 'w_up', 'conv_w', 'conv_b', 'w_down']
TWIN_DIFF_INPUT = 'x'
TWIN_INPUTS = ['x', 'c', 'ctx', 'c_ctx', 'w_mod', 'b_mod', 'g_mix', 'w_in', 'q_norm', 'k_norm', 'attn_sink', 'w_gate_f', 'b_gate_f', 'w_gate_b', 'b_gate_b', 'gla_norm', 'w_attn_o', 'w_gla_o', 'w_out', 'g_ffn', 'w_up', 'conv_w', 'conv_b', 'w_down', 'loss_target', 'm_c_ctx', 'm_w_mod', 'm_b_mod', 'm_g_mix', 'm_w_in', 'm_q_norm', 'm_k_norm', 'm_attn_sink', 'm_w_gate_f', 'm_b_gate_f', 'm_w_gate_b', 'm_b_gate_b', 'm_gla_norm', 'm_w_attn_o', 'm_w_gla_o', 'm_w_out', 'm_g_ffn', 'm_w_up', 'm_conv_w', 'm_conv_b', 'm_w_down', 'v_c_ctx', 'v_w_mod', 'v_b_mod', 'v_g_mix', 'v_w_in', 'v_q_norm', 'v_k_norm', 'v_attn_sink', 'v_w_gate_f', 'v_b_gate_f', 'v_w_gate_b', 'v_b_gate_b', 'v_gla_norm', 'v_w_attn_o', 'v_w_gla_o', 'v_w_out', 'v_g_ffn', 'v_w_up', 'v_conv_w', 'v_conv_b', 'v_w_down']
TWIN_OUTPUTS = ['loss', 'grad_x', 'grad_c_ctx', 'grad_w_mod', 'grad_b_mod', 'grad_g_mix', 'grad_w_in', 'grad_q_norm', 'grad_k_norm', 'grad_attn_sink', 'grad_w_gate_f', 'grad_b_gate_f', 'grad_w_gate_b', 'grad_b_gate_b', 'grad_gla_norm', 'grad_w_attn_o', 'grad_w_gla_o', 'grad_w_out', 'grad_g_ffn', 'grad_w_up', 'grad_conv_w', 'grad_conv_b', 'grad_w_down', 'delta_c_ctx', 'delta_w_mod', 'delta_b_mod', 'delta_g_mix', 'delta_w_in', 'delta_q_norm', 'delta_k_norm', 'delta_attn_sink', 'delta_w_gate_f', 'delta_b_gate_f', 'delta_w_gate_b', 'delta_b_gate_b', 'delta_gla_norm', 'delta_w_attn_o', 'delta_w_gla_o', 'delta_w_out', 'delta_g_ffn', 'delta_w_up', 'delta_conv_w', 'delta_conv_b', 'delta_w_down', 'new_m_c_ctx', 'new_m_w_mod', 'new_m_b_mod', 'new_m_g_mix', 'new_m_w_in', 'new_m_q_norm', 'new_m_k_norm', 'new_m_attn_sink', 'new_m_w_gate_f', 'new_m_b_gate_f', 'new_m_w_gate_b', 'new_m_b_gate_b', 'new_m_gla_norm', 'new_m_w_attn_o', 'new_m_w_gla_o', 'new_m_w_out', 'new_m_g_ffn', 'new_m_w_up', 'new_m_conv_w', 'new_m_conv_b', 'new_m_w_down', 'new_v_c_ctx', 'new_v_w_mod', 'new_v_b_mod', 'new_v_g_mix', 'new_v_w_in', 'new_v_q_norm', 'new_v_k_norm', 'new_v_attn_sink', 'new_v_w_gate_f', 'new_v_b_gate_f', 'new_v_w_gate_b', 'new_v_b_gate_b', 'new_v_gla_norm', 'new_v_w_attn_o', 'new_v_w_gla_o', 'new_v_w_out', 'new_v_g_ffn', 'new_v_w_up', 'new_v_conv_w', 'new_v_conv_b', 'new_v_w_down']
TWIN_LEAF_KINDS = {'loss': 'loss', 'grad_x': 'grad_x', 'grad_c_ctx': 'grad_w', 'grad_w_mod': 'grad_w', 'grad_b_mod': 'grad_w', 'grad_g_mix': 'grad_w', 'grad_w_in': 'grad_w', 'grad_q_norm': 'grad_w', 'grad_k_norm': 'grad_w', 'grad_attn_sink': 'grad_w', 'grad_w_gate_f': 'grad_w', 'grad_b_gate_f': 'grad_w', 'grad_w_gate_b': 'grad_w', 'grad_b_gate_b': 'grad_w', 'grad_gla_norm': 'grad_w', 'grad_w_attn_o': 'grad_w', 'grad_w_gla_o': 'grad_w', 'grad_w_out': 'grad_w', 'grad_g_ffn': 'grad_w', 'grad_w_up': 'grad_w', 'grad_conv_w': 'grad_w', 'grad_conv_b': 'grad_w', 'grad_w_down': 'grad_w', 'delta_c_ctx': 'delta_w', 'delta_w_mod': 'delta_w', 'delta_b_mod': 'delta_w', 'delta_g_mix': 'delta_w', 'delta_w_in': 'delta_w', 'delta_q_norm': 'delta_w', 'delta_k_norm': 'delta_w', 'delta_attn_sink': 'delta_w', 'delta_w_gate_f': 'delta_w', 'delta_b_gate_f': 'delta_w', 'delta_w_gate_b': 'delta_w', 'delta_b_gate_b': 'delta_w', 'delta_gla_norm': 'delta_w', 'delta_w_attn_o': 'delta_w', 'delta_w_gla_o': 'delta_w', 'delta_w_out': 'delta_w', 'delta_g_ffn': 'delta_w', 'delta_w_up': 'delta_w', 'delta_conv_w': 'delta_w', 'delta_conv_b': 'delta_w', 'delta_w_down': 'delta_w', 'new_m_c_ctx': 'new_m', 'new_m_w_mod': 'new_m', 'new_m_b_mod': 'new_m', 'new_m_g_mix': 'new_m', 'new_m_w_in': 'new_m', 'new_m_q_norm': 'new_m', 'new_m_k_norm': 'new_m', 'new_m_attn_sink': 'new_m', 'new_m_w_gate_f': 'new_m', 'new_m_b_gate_f': 'new_m', 'new_m_w_gate_b': 'new_m', 'new_m_b_gate_b': 'new_m', 'new_m_gla_norm': 'new_m', 'new_m_w_attn_o': 'new_m', 'new_m_w_gla_o': 'new_m', 'new_m_w_out': 'new_m', 'new_m_g_ffn': 'new_m', 'new_m_w_up': 'new_m', 'new_m_conv_w': 'new_m', 'new_m_conv_b': 'new_m', 'new_m_w_down': 'new_m', 'new_v_c_ctx': 'new_v', 'new_v_w_mod': 'new_v', 'new_v_b_mod': 'new_v', 'new_v_g_mix': 'new_v', 'new_v_w_in': 'new_v', 'new_v_q_norm': 'new_v', 'new_v_k_norm': 'new_v', 'new_v_attn_sink': 'new_v', 'new_v_w_gate_f': 'new_v', 'new_v_b_gate_f': 'new_v', 'new_v_w_gate_b': 'new_v', 'new_v_b_gate_b': 'new_v', 'new_v_gla_norm': 'new_v', 'new_v_w_attn_o': 'new_v', 'new_v_w_gla_o': 'new_v', 'new_v_w_out': 'new_v', 'new_v_g_ffn': 'new_v', 'new_v_w_up': 'new_v', 'new_v_conv_w': 'new_v', 'new_v_conv_b': 'new_v', 'new_v_w_down': 'new_v'}


def _forward(args):
    return _fwd_reference(*[args[k] for k in FWD_PARAMS])


def _output_shape():
    def fwd():
        inp = _fwd_setup_inputs(0)
        return _fwd_reference(*[inp[k] for k in FWD_PARAMS])
    out = _jax.eval_shape(fwd)
    return out.shape, out.dtype

N_MICROBATCH = 1
ADAM_LR = 0.001
ADAM_B1 = 0.9
ADAM_B2 = 0.999
ADAM_EPS = 1e-08
ADAM_WD = 0.01
ADAM_STEP = 10
PER_EXAMPLE_BATCH_AXIS = {'x': 0, 'c': 0, 'ctx': 0, 'loss_target': 0}
SHARED_INPUTS = []
_WEIGHT_DTYPES = {'c_ctx': _jnp.float32, 'w_mod': _jnp.float32, 'b_mod': _jnp.float32, 'g_mix': _jnp.float32, 'w_in': _jnp.float32, 'q_norm': _jnp.float32, 'k_norm': _jnp.float32, 'attn_sink': _jnp.float32, 'w_gate_f': _jnp.float32, 'b_gate_f': _jnp.float32, 'w_gate_b': _jnp.float32, 'b_gate_b': _jnp.float32, 'gla_norm': _jnp.float32, 'w_attn_o': _jnp.float32, 'w_gla_o': _jnp.float32, 'w_out': _jnp.float32, 'g_ffn': _jnp.float32, 'w_up': _jnp.float32, 'conv_w': _jnp.float32, 'conv_b': _jnp.float32, 'w_down': _jnp.float32}
MOMENT_SCALE = {'c_ctx': 6.425253e-02, 'w_mod': 5.909014e-01, 'b_mod': 1.647830e+00, 'g_mix': 4.237352e-01, 'w_in': 2.711691e-02, 'q_norm': 7.231045e-02, 'k_norm': 7.308432e-02, 'attn_sink': 8.880213e-03, 'w_gate_f': 3.683477e-03, 'b_gate_f': 9.825175e-03, 'w_gate_b': 3.431781e-03, 'b_gate_b': 9.652250e-03, 'gla_norm': 1.484431e+00, 'w_attn_o': 3.926706e-02, 'w_gla_o': 2.230808e-02, 'w_out': 3.298916e-02, 'g_ffn': 3.176731e+00, 'w_up': 7.025529e-02, 'conv_w': 4.653724e-01, 'conv_b': 3.773585e-01, 'w_down': 4.907548e-02}


def _to_microbatches(a, axis):
    t = _jnp.moveaxis(a, axis, 0)
    t = t.reshape((N_MICROBATCH, t.shape[0] // N_MICROBATCH) + t.shape[1:])
    return _jnp.moveaxis(t, 1, axis + 1)


def setup_inputs(seed: int = 0) -> dict:
    inp = _fwd_setup_inputs(seed)
    key = _jax.random.fold_in(_jax.random.key(seed), 7919)
    shape, _ = _output_shape()
    out = dict(inp)
    out["loss_target"] = _jax.random.normal(_jax.random.fold_in(key, 0), shape, _jnp.float32)
    for i, name in enumerate(TWIN_WEIGHTS):
        w = inp[name].astype(_jnp.float32)
        if MOMENT_SCALE is None:
            s = _jnp.sqrt(_jnp.mean(_jnp.square(w)) + 1e-30)
        else:
            s = MOMENT_SCALE[name]
        km, kv = _jax.random.split(_jax.random.fold_in(key, i + 1))
        out[name] = w
        out["m_" + name] = s * _jax.random.normal(km, w.shape, _jnp.float32)
        out["v_" + name] = (s * s) * _jax.random.uniform(kv, w.shape, _jnp.float32, 0.5, 1.5)
    if N_MICROBATCH > 1:
        for name, axis in PER_EXAMPLE_BATCH_AXIS.items():
            out[name] = _to_microbatches(out[name], axis)
    return {'x': out['x'], 'c': out['c'], 'ctx': out['ctx'], 'c_ctx': out['c_ctx'], 'w_mod': out['w_mod'], 'b_mod': out['b_mod'], 'g_mix': out['g_mix'], 'w_in': out['w_in'], 'q_norm': out['q_norm'], 'k_norm': out['k_norm'], 'attn_sink': out['attn_sink'], 'w_gate_f': out['w_gate_f'], 'b_gate_f': out['b_gate_f'], 'w_gate_b': out['w_gate_b'], 'b_gate_b': out['b_gate_b'], 'gla_norm': out['gla_norm'], 'w_attn_o': out['w_attn_o'], 'w_gla_o': out['w_gla_o'], 'w_out': out['w_out'], 'g_ffn': out['g_ffn'], 'w_up': out['w_up'], 'conv_w': out['conv_w'], 'conv_b': out['conv_b'], 'w_down': out['w_down'], 'loss_target': out['loss_target'], 'm_c_ctx': out['m_c_ctx'], 'm_w_mod': out['m_w_mod'], 'm_b_mod': out['m_b_mod'], 'm_g_mix': out['m_g_mix'], 'm_w_in': out['m_w_in'], 'm_q_norm': out['m_q_norm'], 'm_k_norm': out['m_k_norm'], 'm_attn_sink': out['m_attn_sink'], 'm_w_gate_f': out['m_w_gate_f'], 'm_b_gate_f': out['m_b_gate_f'], 'm_w_gate_b': out['m_w_gate_b'], 'm_b_gate_b': out['m_b_gate_b'], 'm_gla_norm': out['m_gla_norm'], 'm_w_attn_o': out['m_w_attn_o'], 'm_w_gla_o': out['m_w_gla_o'], 'm_w_out': out['m_w_out'], 'm_g_ffn': out['m_g_ffn'], 'm_w_up': out['m_w_up'], 'm_conv_w': out['m_conv_w'], 'm_conv_b': out['m_conv_b'], 'm_w_down': out['m_w_down'], 'v_c_ctx': out['v_c_ctx'], 'v_w_mod': out['v_w_mod'], 'v_b_mod': out['v_b_mod'], 'v_g_mix': out['v_g_mix'], 'v_w_in': out['v_w_in'], 'v_q_norm': out['v_q_norm'], 'v_k_norm': out['v_k_norm'], 'v_attn_sink': out['v_attn_sink'], 'v_w_gate_f': out['v_w_gate_f'], 'v_b_gate_f': out['v_b_gate_f'], 'v_w_gate_b': out['v_w_gate_b'], 'v_b_gate_b': out['v_b_gate_b'], 'v_gla_norm': out['v_gla_norm'], 'v_w_attn_o': out['v_w_attn_o'], 'v_w_gla_o': out['v_w_gla_o'], 'v_w_out': out['v_w_out'], 'v_g_ffn': out['v_g_ffn'], 'v_w_up': out['v_w_up'], 'v_conv_w': out['v_conv_w'], 'v_conv_b': out['v_conv_b'], 'v_w_down': out['v_w_down']}


def _loss(weights, diff, rest, loss_target):
    with _jax.named_scope("forward"):
        args = {**rest, TWIN_DIFF_INPUT: diff, **{k: w.astype(_WEIGHT_DTYPES[k]) for k, w in weights.items()}}
        y = _forward(args)
    with _jax.named_scope("loss_head"):
        err = _jnp.square(y.astype(_jnp.float32) - loss_target)
        return 0.5 * _jnp.sum(_jnp.mean(err, axis=-1)) if err.ndim else 0.5 * err


def _adamw(w, g, m, v):
    m = ADAM_B1 * m + (1.0 - ADAM_B1) * g
    v = ADAM_B2 * v + (1.0 - ADAM_B2) * _jnp.square(g)
    m_hat = m / (1.0 - ADAM_B1 ** ADAM_STEP)
    v_hat = v / (1.0 - ADAM_B2 ** ADAM_STEP)
    delta = -ADAM_LR * (m_hat / (_jnp.sqrt(v_hat) + ADAM_EPS) + ADAM_WD * w)
    return delta, m, v


def reference(x, c, ctx, c_ctx, w_mod, b_mod, g_mix, w_in, q_norm, k_norm, attn_sink, w_gate_f, b_gate_f, w_gate_b, b_gate_b, gla_norm, w_attn_o, w_gla_o, w_out, g_ffn, w_up, conv_w, conv_b, w_down, loss_target, m_c_ctx, m_w_mod, m_b_mod, m_g_mix, m_w_in, m_q_norm, m_k_norm, m_attn_sink, m_w_gate_f, m_b_gate_f, m_w_gate_b, m_b_gate_b, m_gla_norm, m_w_attn_o, m_w_gla_o, m_w_out, m_g_ffn, m_w_up, m_conv_w, m_conv_b, m_w_down, v_c_ctx, v_w_mod, v_b_mod, v_g_mix, v_w_in, v_q_norm, v_k_norm, v_attn_sink, v_w_gate_f, v_b_gate_f, v_w_gate_b, v_b_gate_b, v_gla_norm, v_w_attn_o, v_w_gla_o, v_w_out, v_g_ffn, v_w_up, v_conv_w, v_conv_b, v_w_down):
    given = dict(x=x, c=c, ctx=ctx, c_ctx=c_ctx, w_mod=w_mod, b_mod=b_mod, g_mix=g_mix, w_in=w_in, q_norm=q_norm, k_norm=k_norm, attn_sink=attn_sink, w_gate_f=w_gate_f, b_gate_f=b_gate_f, w_gate_b=w_gate_b, b_gate_b=b_gate_b, gla_norm=gla_norm, w_attn_o=w_attn_o, w_gla_o=w_gla_o, w_out=w_out, g_ffn=g_ffn, w_up=w_up, conv_w=conv_w, conv_b=conv_b, w_down=w_down, loss_target=loss_target, m_c_ctx=m_c_ctx, m_w_mod=m_w_mod, m_b_mod=m_b_mod, m_g_mix=m_g_mix, m_w_in=m_w_in, m_q_norm=m_q_norm, m_k_norm=m_k_norm, m_attn_sink=m_attn_sink, m_w_gate_f=m_w_gate_f, m_b_gate_f=m_b_gate_f, m_w_gate_b=m_w_gate_b, m_b_gate_b=m_b_gate_b, m_gla_norm=m_gla_norm, m_w_attn_o=m_w_attn_o, m_w_gla_o=m_w_gla_o, m_w_out=m_w_out, m_g_ffn=m_g_ffn, m_w_up=m_w_up, m_conv_w=m_conv_w, m_conv_b=m_conv_b, m_w_down=m_w_down, v_c_ctx=v_c_ctx, v_w_mod=v_w_mod, v_b_mod=v_b_mod, v_g_mix=v_g_mix, v_w_in=v_w_in, v_q_norm=v_q_norm, v_k_norm=v_k_norm, v_attn_sink=v_attn_sink, v_w_gate_f=v_w_gate_f, v_b_gate_f=v_b_gate_f, v_w_gate_b=v_w_gate_b, v_b_gate_b=v_b_gate_b, v_gla_norm=v_gla_norm, v_w_attn_o=v_w_attn_o, v_w_gla_o=v_w_gla_o, v_w_out=v_w_out, v_g_ffn=v_g_ffn, v_w_up=v_w_up, v_conv_w=v_conv_w, v_conv_b=v_conv_b, v_w_down=v_w_down)
    weights = {n: given[n] for n in TWIN_WEIGHTS}
    shared = {n: given[n] for n in SHARED_INPUTS}
    per_example = {n: given[n] for n in ['x', 'c', 'ctx']}
    grad_fn = _jax.value_and_grad(_loss, argnums=(0, 1))

    def one_microbatch(ex, loss_target):
        ex = dict(ex)
        diff = ex.pop(TWIN_DIFF_INPUT)
        return grad_fn(weights, diff, {**shared, **ex}, loss_target)

    if N_MICROBATCH == 1:
        loss, (grad_w, grad_x) = one_microbatch(per_example, given["loss_target"])
    else:
        def body(carry, xs):
            loss_sum, grad_sum = carry
            l_k, (gw_k, gx_k) = one_microbatch(xs[0], xs[1])
            with _jax.named_scope("update"):
                return (loss_sum + l_k, _jax.tree.map(_jnp.add, grad_sum, gw_k)), gx_k

        init = (_jnp.zeros((), _jnp.float32), _jax.tree.map(_jnp.zeros_like, weights))
        (loss, grad_w), grad_x = _jax.lax.scan(body, init, (per_example, given["loss_target"]))
    with _jax.named_scope("update"):
        delta_w, new_m, new_v = {}, {}, {}
        for n in TWIN_WEIGHTS:
            delta_w[n], new_m[n], new_v[n] = _adamw(weights[n], grad_w[n], given["m_" + n], given["v_" + n])
    return (loss, grad_x, *[grad_w[n] for n in TWIN_WEIGHTS], *[delta_w[n] for n in TWIN_WEIGHTS],
            *[new_m[n] for n in TWIN_WEIGHTS], *[new_v[n] for n in TWIN_WEIGHTS])
```

```python
import functools
import math

import jax
import jax.numpy as jnp
from jax import lax
from jax.experimental import pallas as pl
from jax.experimental.pallas import tpu as pltpu

F32 = jnp.float32
BF16 = jnp.bfloat16
MESH = pl.DeviceIdType.MESH

EPS = 1e-6
HEAD_DIM = 128
N_Q_HEADS = 16
N_KV_HEADS = 4
Q_PER_KV = N_Q_HEADS // N_KV_HEADS
WINDOW = 128
GLA_HEADS = 4
GLA_LOWRANK = 16
GLA_GATE_NORM = 16.0
GLA_CHUNK = 64
GRID_W = 64
ROPE_THETA = 10000.0
GLA_LEVELS = (32, 16, 8, 4, 2, 1)
LANES = 128

ADAM_LR = 0.001
ADAM_B1 = 0.9
ADAM_B2 = 0.999
ADAM_EPS = 1e-08
ADAM_WD = 0.01
ADAM_STEP = 10

VMEM_LIMIT = 52 * 1024 * 1024


def _cparams(*sem):
    return pltpu.CompilerParams(dimension_semantics=sem, vmem_limit_bytes=VMEM_LIMIT)


def _pick(n, target, mult=LANES):
    best = None
    d = mult
    while d <= min(n, target):
        if n % d == 0:
            best = d
        d += mult
    return n if best is None else best


def _sigmoid(x):
    return 1.0 / (1.0 + jnp.exp(-x))


def _silu(x):
    return x * _sigmoid(x)


def _dsilu(x):
    s = _sigmoid(x)
    return s * (1.0 + x * (1.0 - s))


def _dot(a, b, dims):
    return lax.dot_general(a, b, (dims, ((), ())), preferred_element_type=F32)


NN = ((1,), (0,))
NT = ((1,), (1,))
TN = ((0,), (0,))


def _matmul(a, b, mode, out_dtype, name, tm=768, tn=1024, tk=2048):
    if mode == "nn":
        (M, K), (K2, N) = a.shape, b.shape
    elif mode == "nt":
        (M, K), (N, K2) = a.shape, b.shape
    else:
        (K, M), (K2, N) = a.shape, b.shape
    assert K == K2, (name, a.shape, b.shape)
    tm, tn, tk = _pick(M, tm), _pick(N, tn), _pick(K, tk)
    nk = K // tk
    dims = {"nn": NN, "nt": NT, "tn": TN}[mode]

    def body(a_ref, b_ref, o_ref, acc_ref):
        k = pl.program_id(2)

        @pl.when(k == 0)
        def _():
            acc_ref[...] = jnp.zeros_like(acc_ref)

        acc_ref[...] += _dot(a_ref[...].astype(BF16), b_ref[...].astype(BF16), dims)

        @pl.when(k == nk - 1)
        def _():
            o_ref[...] = acc_ref[...].astype(out_dtype)

    if mode == "tn":
        a_spec = pl.BlockSpec((tk, tm), lambda i, j, k: (k, i))
    else:
        a_spec = pl.BlockSpec((tm, tk), lambda i, j, k: (i, k))
    if mode == "nt":
        b_spec = pl.BlockSpec((tn, tk), lambda i, j, k: (j, k))
    else:
        b_spec = pl.BlockSpec((tk, tn), lambda i, j, k: (k, j))
    return pl.pallas_call(
        body, name=name, grid=(M // tm, N // tn, nk),
        in_specs=[a_spec, b_spec],
        out_specs=pl.BlockSpec((tm, tn), lambda i, j, k: (i, j)),
        out_shape=jax.ShapeDtypeStruct((M, N), out_dtype),
        scratch_shapes=[pltpu.VMEM((tm, tn), F32)],
        compiler_params=_cparams("parallel", "parallel", "arbitrary"),
    )(a, b)


def _modnorm_fwd(xall, g, sc, sh, n_ctx, name):
    R, D = xall.shape
    tm = _pick(n_ctx, 256, 8)
    cb = n_ctx // tm

    def body(x_ref, g_ref, sc_ref, sh_ref, h_ref):
        x = x_ref[...]
        r = lax.rsqrt(jnp.mean(x * x, axis=-1, keepdims=True) + EPS)
        n = x * r * g_ref[...]
        h_ref[...] = (n * (1.0 + sc_ref[0]) + sh_ref[0]).astype(BF16)

    sel = lambda i: (jnp.where(i < cb, 0, 1), 0, 0)
    return pl.pallas_call(
        body, name=name, grid=(R // tm,),
        in_specs=[pl.BlockSpec((tm, D), lambda i: (i, 0)), pl.BlockSpec((1, D), lambda i: (0, 0)),
                  pl.BlockSpec((1, 1, D), sel), pl.BlockSpec((1, 1, D), sel)],
        out_specs=pl.BlockSpec((tm, D), lambda i: (i, 0)),
        out_shape=jax.ShapeDtypeStruct((R, D), BF16),
        compiler_params=_cparams("parallel"),
    )(xall, g, sc, sh)


def _modnorm_bwd(x, dh, g, sc, resid, name):
    N, D = x.shape
    tm = _pick(N, 256, 8)
    want_dx = resid is not None

    def body(*refs):
        if want_dx:
            x_ref, dh_ref, g_ref, sc_ref, res_ref, dx_ref, acc_ref = refs
        else:
            x_ref, dh_ref, g_ref, sc_ref, acc_ref = refs
        i = pl.program_id(0)

        @pl.when(i == 0)
        def _():
            acc_ref[...] = jnp.zeros_like(acc_ref)

        xv, dhv, gv = x_ref[...], dh_ref[...], g_ref[...]
        r = lax.rsqrt(jnp.mean(xv * xv, axis=-1, keepdims=True) + EPS)
        xh = xv * r
        dn = dhv * (1.0 + sc_ref[...])
        acc_ref[0:1, :] += jnp.sum(dhv, axis=0, keepdims=True)
        acc_ref[1:2, :] += jnp.sum(dhv * xh * gv, axis=0, keepdims=True)
        acc_ref[2:3, :] += jnp.sum(dn * xh, axis=0, keepdims=True)
        if want_dx:
            dxh = dn * gv
            dx_ref[...] = res_ref[...] + r * (dxh - xh * jnp.mean(dxh * xh, axis=-1, keepdims=True))

    row = pl.BlockSpec((tm, D), lambda i: (i, 0))
    vec = pl.BlockSpec((1, D), lambda i: (0, 0))
    acc = pl.BlockSpec((8, D), lambda i: (0, 0))
    acc_shape = jax.ShapeDtypeStruct((8, D), F32)
    if want_dx:
        return pl.pallas_call(
            body, name=name, grid=(N // tm,), in_specs=[row, row, vec, vec, row],
            out_specs=[row, acc], out_shape=[jax.ShapeDtypeStruct((N, D), F32), acc_shape],
            compiler_params=_cparams("arbitrary"))(x, dh, g, sc, resid)
    sums = pl.pallas_call(
        body, name=name, grid=(N // tm,), in_specs=[row, row, vec, vec],
        out_specs=acc, out_shape=acc_shape, compiler_params=_cparams("arbitrary"))(x, dh, g, sc)
    return None, sums


def _qknorm_fwd(z, cblk, nrows, roff, w, cos2, sin2, nh, name):
    W = nh * HEAD_DIM
    tm = _pick(math.gcd(nrows, roff), 256, 8)
    ro = roff // tm
    assert roff % tm == 0

    def body(z_ref, w_ref, c_ref, s_ref, o_ref):
        c, s, wv = c_ref[...], s_ref[...], w_ref[...]
        for h in range(nh):
            x = z_ref[:, h * HEAD_DIM:(h + 1) * HEAD_DIM]
            r = lax.rsqrt(jnp.mean(x * x, axis=-1, keepdims=True) + EPS)
            y = x * r * wv
            o_ref[:, h * HEAD_DIM:(h + 1) * HEAD_DIM] = (y * c + pltpu.roll(y, HEAD_DIM // 2, 1) * s).astype(BF16)

    return pl.pallas_call(
        body, name=name, grid=(nrows // tm,),
        in_specs=[pl.BlockSpec((tm, W), lambda i: (i + ro, cblk)), pl.BlockSpec((1, HEAD_DIM), lambda i: (0, 0)),
                  pl.BlockSpec((tm, HEAD_DIM), lambda i: (i + ro, 0)), pl.BlockSpec((tm, HEAD_DIM), lambda i: (i + ro, 0))],
        out_specs=pl.BlockSpec((tm, W), lambda i: (i, 0)),
        out_shape=jax.ShapeDtypeStruct((nrows, W), BF16),
        compiler_params=_cparams("parallel"),
    )(z, w, cos2, sin2)


def _qknorm_bwd(z, cblk, nrows, roff, w, cos2, sin2, dy, nh, name):
    W = nh * HEAD_DIM
    tm = _pick(math.gcd(nrows, roff), 256, 8)
    ro = roff // tm

    def body(z_ref, w_ref, c_ref, s_ref, dy_ref, dz_ref, acc_ref):
        i = pl.program_id(0)

        @pl.when(i == 0)
        def _():
            acc_ref[...] = jnp.zeros_like(acc_ref)

        c, s, wv = c_ref[...], s_ref[...], w_ref[...]
        dw = jnp.zeros((1, HEAD_DIM), F32)
        for h in range(nh):
            sl = slice(h * HEAD_DIM, (h + 1) * HEAD_DIM)
            x = z_ref[:, sl]
            d = dy_ref[:, sl]
            dyn = d * c + pltpu.roll(d * s, HEAD_DIM // 2, 1)
            r = lax.rsqrt(jnp.mean(x * x, axis=-1, keepdims=True) + EPS)
            xh = x * r
            dw = dw + jnp.sum(dyn * xh, axis=0, keepdims=True)
            dxh = dyn * wv
            dz_ref[:, sl] = (r * (dxh - xh * jnp.mean(dxh * xh, axis=-1, keepdims=True))).astype(BF16)
        acc_ref[0:1, :] += dw

    return pl.pallas_call(
        body, name=name, grid=(nrows // tm,),
        in_specs=[pl.BlockSpec((tm, W), lambda i: (i + ro, cblk)), pl.BlockSpec((1, HEAD_DIM), lambda i: (0, 0)),
                  pl.BlockSpec((tm, HEAD_DIM), lambda i: (i + ro, 0)), pl.BlockSpec((tm, HEAD_DIM), lambda i: (i + ro, 0)),
                  pl.BlockSpec((tm, W), lambda i: (i, 0))],
        out_specs=[pl.BlockSpec((tm, W), lambda i: (i, 0)), pl.BlockSpec((8, HEAD_DIM), lambda i: (0, 0))],
        out_shape=[jax.ShapeDtypeStruct((nrows, W), BF16), jax.ShapeDtypeStruct((8, HEAD_DIM), F32)],
        compiler_params=_cparams("arbitrary"),
    )(z, w, cos2, sin2, dy)


def _cast_seg(z, cblk, width, name):
    R = z.shape[0]
    tm = _pick(R, 512, 8)

    def body(z_ref, o_ref):
        o_ref[...] = z_ref[...].astype(BF16)

    return pl.pallas_call(
        body, name=name, grid=(R // tm,),
        in_specs=[pl.BlockSpec((tm, width), lambda i: (i, cblk))],
        out_specs=pl.BlockSpec((tm, width), lambda i: (i, 0)),
        out_shape=jax.ShapeDtypeStruct((R, width), BF16), compiler_params=_cparams("parallel"))(z)


NEG_BIG = -1e30


def _attn_specs(T, n_ctx):
    nb = T // WINDOW
    lb = n_ctx // WINDOW
    blk = lambda f: pl.BlockSpec((WINDOW, HEAD_DIM), f)
    win = [blk(lambda h, i: (lb + jnp.maximum(i - 1, 0), h)), blk(lambda h, i: (lb + i, h)),
           blk(lambda h, i: (lb + jnp.minimum(i + 1, nb - 1), h))]
    ctx = pl.BlockSpec((n_ctx, HEAD_DIM), lambda h, i: (0, h))
    qspec = pl.BlockSpec((WINDOW, Q_PER_KV * HEAD_DIM), lambda h, i: (i, h))
    sink = pl.BlockSpec((N_Q_HEADS, HEAD_DIM), lambda h, i: (0, 0))
    return nb, qspec, win, ctx, sink


def _attn_probs(q, kw, kctx, snk, valid):
    scale = HEAD_DIM ** -0.5
    s_lat = jnp.where(valid, _dot(q, kw, NT) * scale, NEG_BIG)
    s_ctx = _dot(q, kctx, NT) * scale
    m = jnp.maximum(jnp.maximum(jnp.max(s_lat, axis=-1, keepdims=True), jnp.max(s_ctx, axis=-1, keepdims=True)), snk)
    p_lat = jnp.exp(s_lat - m)
    p_ctx = jnp.exp(s_ctx - m)
    p_snk = jnp.exp(snk - m)
    den = p_snk + jnp.sum(p_lat, axis=-1, keepdims=True) + jnp.sum(p_ctx, axis=-1, keepdims=True)
    return p_lat, p_ctx, p_snk, den


def _attn_valid(i, T):
    qpos = i * WINDOW + lax.broadcasted_iota(jnp.int32, (WINDOW, 3 * WINDOW), 0)
    kpos = (i - 1) * WINDOW + lax.broadcasted_iota(jnp.int32, (WINDOW, 3 * WINDOW), 1)
    return (jnp.abs(qpos - kpos) <= WINDOW) & (kpos >= 0) & (kpos < T)


def _attn_fwd(qn, kn, vb, sink_rows, n_ctx, name):
    T = qn.shape[0]
    nb, qspec, win, ctx, sink = _attn_specs(T, n_ctx)

    def body(q_ref, kp, kc, kx, vp, vc, vx, kctx_ref, vctx_ref, sink_ref, o_ref):
        h, i = pl.program_id(0), pl.program_id(1)
        kw = jnp.concatenate([kp[...], kc[...], kx[...]], axis=0)
        vw = jnp.concatenate([vp[...], vc[...], vx[...]], axis=0)
        kctx, vctx = kctx_ref[...], vctx_ref[...]
        valid = _attn_valid(i, T)
        for g in range(Q_PER_KV):
            sl = slice(g * HEAD_DIM, (g + 1) * HEAD_DIM)
            snk = sink_ref[pl.ds(h * Q_PER_KV + g, 1), :][:, 0:1]
            p_lat, p_ctx, _, den = _attn_probs(q_ref[:, sl], kw, kctx, snk, valid)
            o = (_dot(p_lat.astype(BF16), vw, NN) + _dot(p_ctx.astype(BF16), vctx, NN)) / den
            o_ref[:, sl] = o.astype(BF16)

    return pl.pallas_call(
        body, name=name, grid=(N_KV_HEADS, nb),
        in_specs=[qspec] + win + win + [ctx, ctx, sink],
        out_specs=qspec, out_shape=jax.ShapeDtypeStruct(qn.shape, BF16),
        compiler_params=_cparams("parallel", "parallel"),
    )(qn, kn, kn, kn, vb, vb, vb, kn, vb, sink_rows)


def _attn_bwd(qn, kn, vb, sink_rows, do, n_ctx, name):
    T = qn.shape[0]
    nb, qspec, win, ctx, sink = _attn_specs(T, n_ctx)
    scale = HEAD_DIM ** -0.5
    TP = T + 2 * WINDOW

    def body(q_ref, kp, kc, kx, vp, vc, vx, kctx_ref, vctx_ref, sink_ref, do_ref,
             dq_ref, dkw_ref, dvw_ref, dkc_ref, dvc_ref, dsn_ref):
        h, i = pl.program_id(0), pl.program_id(1)

        @pl.when(i == 0)
        def _():
            dkw_ref[...] = jnp.zeros_like(dkw_ref)
            dvw_ref[...] = jnp.zeros_like(dvw_ref)
            dkc_ref[...] = jnp.zeros_like(dkc_ref)
            dvc_ref[...] = jnp.zeros_like(dvc_ref)
            dsn_ref[...] = jnp.zeros_like(dsn_ref)

        kw = jnp.concatenate([kp[...], kc[...], kx[...]], axis=0)
        vw = jnp.concatenate([vp[...], vc[...], vx[...]], axis=0)
        kctx, vctx = kctx_ref[...], vctx_ref[...]
        valid = _attn_valid(i, T)
        lane = lax.broadcasted_iota(jnp.int32, (8, HEAD_DIM), 1)
        dkw = jnp.zeros((3 * WINDOW, HEAD_DIM), F32)
        dvw = jnp.zeros((3 * WINDOW, HEAD_DIM), F32)
        dkc = jnp.zeros(kctx.shape, F32)
        dvc = jnp.zeros(kctx.shape, F32)
        dsn = jnp.zeros((8, HEAD_DIM), F32)
        for g in range(Q_PER_KV):
            sl = slice(g * HEAD_DIM, (g + 1) * HEAD_DIM)
            snk = sink_ref[pl.ds(h * Q_PER_KV + g, 1), :][:, 0:1]
            q, d_o = q_ref[:, sl], do_ref[:, sl]
            p_lat, p_ctx, p_snk, den = _attn_probs(q, kw, kctx, snk, valid)
            inv = 1.0 / den
            p_lat, p_ctx, p_snk = p_lat * inv, p_ctx * inv, p_snk * inv
            dp_lat = _dot(d_o, vw, NT)
            dp_ctx = _dot(d_o, vctx, NT)
            dr = jnp.sum(p_lat * dp_lat, axis=-1, keepdims=True) + jnp.sum(p_ctx * dp_ctx, axis=-1, keepdims=True)
            ds_lat = (p_lat * (dp_lat - dr) * scale).astype(BF16)
            ds_ctx = (p_ctx * (dp_ctx - dr) * scale).astype(BF16)
            dq_ref[:, sl] = _dot(ds_lat, kw, NN) + _dot(ds_ctx, kctx, NN)
            dkw = dkw + _dot(ds_lat, q, TN)
            dvw = dvw + _dot(p_lat.astype(BF16), d_o, TN)
            dkc = dkc + _dot(ds_ctx, q, TN)
            dvc = dvc + _dot(p_ctx.astype(BF16), d_o, TN)
            dsn = dsn + jnp.where(lane == g, -jnp.sum(p_snk * dr, axis=0, keepdims=True), 0.0)
        rows = pl.ds(pl.multiple_of(i * WINDOW, WINDOW), 3 * WINDOW)
        dkw_ref[rows, :] += dkw
        dvw_ref[rows, :] += dvw
        dkc_ref[...] += dkc
        dvc_ref[...] += dvc
        dsn_ref[0] += dsn

    wacc = pl.BlockSpec((TP, HEAD_DIM), lambda h, i: (0, h))
    return pl.pallas_call(
        body, name=name, grid=(N_KV_HEADS, nb),
        in_specs=[qspec] + win + win + [ctx, ctx, sink, qspec],
        out_specs=[qspec, wacc, wacc, ctx, ctx, pl.BlockSpec((1, 8, HEAD_DIM), lambda h, i: (h, 0, 0))],
        out_shape=[jax.ShapeDtypeStruct(qn.shape, F32),
                   jax.ShapeDtypeStruct((TP, N_KV_HEADS * HEAD_DIM), F32),
                   jax.ShapeDtypeStruct((TP, N_KV_HEADS * HEAD_DIM), F32),
                   jax.ShapeDtypeStruct((n_ctx, N_KV_HEADS * HEAD_DIM), F32),
                   jax.ShapeDtypeStruct((n_ctx, N_KV_HEADS * HEAD_DIM), F32),
                   jax.ShapeDtypeStruct((N_KV_HEADS, 8, HEAD_DIM), F32)],
        compiler_params=_cparams("arbitrary", "arbitrary"),
    )(qn, kn, kn, kn, vb, vb, vb, kn, vb, sink_rows, do)


def _gla_masks(dirv):
    C = GLA_CHUNK
    r = lax.broadcasted_iota(jnp.int32, (C, C), 0)
    c = lax.broadcasted_iota(jnp.int32, (C, C), 1)
    tt = jnp.where(dirv == 0, r, C - 1 - r)
    ss = jnp.where(dirv == 0, c, C - 1 - c)
    le = (ss <= tt).astype(jnp.int32)
    sums = [le == 1, le == 0]
    blocks = [ss == tt]
    for m in GLA_LEVELS:
        sh = m.bit_length() - 1
        same = (tt >> (sh + 1)) == (ss >> (sh + 1))
        ut = (tt >> sh) & 1
        us = (ss >> sh) & 1
        sums.append(same & (ut == us) & (ut == le))
        blocks.append(same & (ut == 1) & (us == 0))
    mall = jnp.concatenate([jnp.where(s, 1.0, 0.0) for s in sums], axis=0).astype(BF16)
    return mall, blocks


def _split3(x):
    hi = x.astype(BF16)
    r1 = x - hi.astype(F32)
    mid = r1.astype(BF16)
    lo = (r1 - mid.astype(F32)).astype(BF16)
    return hi, mid, lo


def _dot3(m_bf16, x, dims):
    hi, mid, lo = _split3(x)
    return _dot(m_bf16, hi, dims) + _dot(m_bf16, mid, dims) + _dot(m_bf16, lo, dims)


def _gla_chunk_of(dirv, j, lc, nc):
    return jnp.where(dirv == 0, j, jnp.where(j < lc, lc - 1 - j, nc + lc - 1 - j))


def _gla_gate(lr_ref, wg_ref, bg_ref):
    pre = _dot(lr_ref[...].astype(BF16), wg_ref[0].astype(BF16), NN) + bg_ref[0]
    g = (jnp.minimum(pre, 0.0) - jnp.log(1.0 + jnp.exp(-jnp.abs(pre)))) * (1.0 / GLA_GATE_NORM)
    return pre, g


def _gla_fwd(z, qblk, kblk, vblk, lrblk, wg, bg, DV, n_ctx, name):
    R = z.shape[0]
    C = GLA_CHUNK
    DK = wg.shape[2] // GLA_HEADS
    nc, lc = R // C, n_ctx // C
    qscale = DK ** -0.5

    def body(q_ref, k_ref, v_ref, lr_ref, wg_ref, bg_ref, o_ref, sp_ref, st_ref):
        dirv, j = pl.program_id(0), pl.program_id(2)

        @pl.when(j == 0)
        def _():
            st_ref[...] = jnp.zeros_like(st_ref)

        mall, blocks = _gla_masks(dirv)
        q, k, v = q_ref[...] * qscale, k_ref[...], v_ref[...].astype(BF16)
        _, g = _gla_gate(lr_ref, wg_ref, bg_ref)
        E = _dot3(mall, g, NN)
        st = st_ref[...]
        sp_ref[0, 0, 0] = st
        A = jnp.where(blocks[0], _dot(q.astype(BF16), k.astype(BF16), NT), 0.0)
        for l in range(len(GLA_LEVELS)):
            e = jnp.exp(E[(2 + l) * C:(3 + l) * C])
            A = A + jnp.where(blocks[l + 1], _dot((q * e).astype(BF16), (k * e).astype(BF16), NT), 0.0)
        o_ref[0] = _dot((q * jnp.exp(E[0:C])).astype(BF16), st.astype(BF16), NT) + _dot(A.astype(BF16), v, NN)
        decay = jnp.exp(jnp.sum(g, axis=0, keepdims=True))
        st_ref[...] = decay * st + _dot(v, (k * jnp.exp(E[C:2 * C])).astype(BF16), TN)

    chunk = functools.partial(_gla_chunk_of, lc=lc, nc=nc)
    return pl.pallas_call(
        body, name=name, grid=(2, GLA_HEADS, nc),
        in_specs=[pl.BlockSpec((C, DK), lambda d, h, j: (chunk(d, j), qblk + h)),
                  pl.BlockSpec((C, DK), lambda d, h, j: (chunk(d, j), kblk + h)),
                  pl.BlockSpec((C, DV), lambda d, h, j: (chunk(d, j), vblk + h)),
                  pl.BlockSpec((C, LANES), lambda d, h, j: (chunk(d, j), lrblk)),
                  pl.BlockSpec((1, LANES, DK), lambda d, h, j: (d, 0, h)),
                  pl.BlockSpec((1, 1, DK), lambda d, h, j: (d, 0, h))],
        out_specs=[pl.BlockSpec((1, C, DV), lambda d, h, j: (d, chunk(d, j), h)),
                   pl.BlockSpec((1, 1, 1, DV, DK), lambda d, h, j: (d, h, j, 0, 0))],
        out_shape=[jax.ShapeDtypeStruct((2, R, GLA_HEADS * DV), F32),
                   jax.ShapeDtypeStruct((2, GLA_HEADS, nc, DV, DK), F32)],
        scratch_shapes=[pltpu.VMEM((DV, DK), F32)],
        compiler_params=_cparams("parallel", "parallel", "arbitrary"),
    )(z, z, z, z, wg, bg)


def _gla_bwd(z, qblk, kblk, vblk, lrblk, wg, bg, sprev, do, n_ctx, name):
    R = z.shape[0]
    C = GLA_CHUNK
    DK, DV = wg.shape[2] // GLA_HEADS, do.shape[1] // GLA_HEADS
    nc, lc = R // C, n_ctx // C
    qscale = DK ** -0.5
    nl = len(GLA_LEVELS)

    def body(q_ref, k_ref, v_ref, lr_ref, wg_ref, bg_ref, sp_ref, do_ref,
             dq_ref, dk_ref, dv_ref, dpre_ref, dbg_ref, dst_ref):
        dirv, jr = pl.program_id(0), pl.program_id(2)

        @pl.when(jr == 0)
        def _():
            dst_ref[...] = jnp.zeros_like(dst_ref)
            dbg_ref[...] = jnp.zeros_like(dbg_ref)

        mall, blocks = _gla_masks(dirv)
        q, k, v = q_ref[...] * qscale, k_ref[...], v_ref[...].astype(BF16)
        pre, g = _gla_gate(lr_ref, wg_ref, bg_ref)
        E = _dot3(mall, g, NN)
        eb, er = jnp.exp(E[0:C]), jnp.exp(E[C:2 * C])
        decay = jnp.exp(jnp.sum(g, axis=0, keepdims=True))
        st = sp_ref[0, 0, 0]
        dst = dst_ref[...]
        d_o = do_ref[...]
        qe, kd = q * eb, k * er
        qb, kb = q.astype(BF16), k.astype(BF16)
        A = jnp.where(blocks[0], _dot(qb, kb, NT), 0.0)
        for l in range(nl):
            e = jnp.exp(E[(2 + l) * C:(3 + l) * C])
            A = A + jnp.where(blocks[l + 1], _dot((q * e).astype(BF16), (k * e).astype(BF16), NT), 0.0)
        dA = _dot(d_o, v, NT)
        dv_ref[0] = _dot(A.astype(BF16), d_o, TN) + _dot(kd.astype(BF16), dst.astype(BF16), NT)
        dqe = _dot(d_o, st.astype(BF16), NN)
        dkd = _dot(v, dst.astype(BF16), NN)
        G = jnp.where(blocks[0], dA, 0.0).astype(BF16)
        dq = dqe * eb + _dot(G, kb, NN)
        dk = dkd * er + _dot(G, qb, TN)
        dE = [dqe * qe, dkd * kd]
        for l in range(nl):
            e = jnp.exp(E[(2 + l) * C:(3 + l) * C])
            ql, kl = q * e, k * e
            G = jnp.where(blocks[l + 1], dA, 0.0).astype(BF16)
            dql = _dot(G, kl.astype(BF16), NN)
            dkl = _dot(G, ql.astype(BF16), TN)
            dq = dq + dql * e
            dk = dk + dkl * e
            dE.append(dql * ql + dkl * kl)
        dlast = jnp.sum(dst * st, axis=0, keepdims=True) * decay
        dg = _dot3(mall, jnp.concatenate(dE, axis=0), TN) + dlast
        dpre = dg * (1.0 / GLA_GATE_NORM) / (1.0 + jnp.exp(pre))
        dq_ref[0] = dq * qscale
        dk_ref[0] = dk
        dpre_ref[...] = dpre.astype(BF16)
        dbg_ref[0, 0] += jnp.sum(dpre, axis=0, keepdims=True)
        dst_ref[...] = decay * dst + _dot(d_o, qe.astype(BF16), TN)

    def chunk(d, jr):
        return _gla_chunk_of(d, nc - 1 - jr, lc, nc)

    return pl.pallas_call(
        body, name=name, grid=(2, GLA_HEADS, nc),
        in_specs=[pl.BlockSpec((C, DK), lambda d, h, j: (chunk(d, j), qblk + h)),
                  pl.BlockSpec((C, DK), lambda d, h, j: (chunk(d, j), kblk + h)),
                  pl.BlockSpec((C, DV), lambda d, h, j: (chunk(d, j), vblk + h)),
                  pl.BlockSpec((C, LANES), lambda d, h, j: (chunk(d, j), lrblk)),
                  pl.BlockSpec((1, LANES, DK), lambda d, h, j: (d, 0, h)),
                  pl.BlockSpec((1, 1, DK), lambda d, h, j: (d, 0, h)),
                  pl.BlockSpec((1, 1, 1, DV, DK), lambda d, h, j: (d, h, nc - 1 - j, 0, 0)),
                  pl.BlockSpec((C, DV), lambda d, h, j: (chunk(d, j), h))],
        out_specs=[pl.BlockSpec((1, C, DK), lambda d, h, j: (d, chunk(d, j), h)),
                   pl.BlockSpec((1, C, DK), lambda d, h, j: (d, chunk(d, j), h)),
                   pl.BlockSpec((1, C, DV), lambda d, h, j: (d, chunk(d, j), h)),
                   pl.BlockSpec((C, DK), lambda d, h, j: (chunk(d, j), d * GLA_HEADS + h)),
                   pl.BlockSpec((1, 1, 1, DK), lambda d, h, j: (d, h, 0, 0))],
        out_shape=[jax.ShapeDtypeStruct((2, R, GLA_HEADS * DK), F32),
                   jax.ShapeDtypeStruct((2, R, GLA_HEADS * DK), F32),
                   jax.ShapeDtypeStruct((2, R, GLA_HEADS * DV), F32),
                   jax.ShapeDtypeStruct((R, 2 * GLA_HEADS * DK), BF16),
                   jax.ShapeDtypeStruct((2, GLA_HEADS, 1, DK), F32)],
        scratch_shapes=[pltpu.VMEM((DV, DK), F32)],
        compiler_params=_cparams("arbitrary", "arbitrary", "arbitrary"),
    )(z, z, z, z, wg, bg, sprev, do)


def _glanorm_fwd(o, z, rbblk, gn, n_ctx, name):
    _, R, GV = o.shape
    T = R - n_ctx
    DV = GV // GLA_HEADS
    tm = _pick(n_ctx, 256, 8)
    ro = n_ctx // tm

    def body(o0_ref, o1_ref, rb_ref, gn_ref, p_ref):
        gnv = gn_ref[...]
        for h in range(GLA_HEADS):
            sl = slice(h * DV, (h + 1) * DV)
            og = o0_ref[0, :, sl] + o1_ref[0, :, sl]
            r = lax.rsqrt(jnp.mean(og * og, axis=-1, keepdims=True) + EPS)
            p_ref[:, sl] = (og * r * gnv * _silu(rb_ref[:, sl])).astype(BF16)

    return pl.pallas_call(
        body, name=name, grid=(T // tm,),
        in_specs=[pl.BlockSpec((1, tm, GV), lambda i: (0, i + ro, 0)), pl.BlockSpec((1, tm, GV), lambda i: (1, i + ro, 0)),
                  pl.BlockSpec((tm, GV), lambda i: (i + ro, rbblk)), pl.BlockSpec((1, DV), lambda i: (0, 0))],
        out_specs=pl.BlockSpec((tm, GV), lambda i: (i, 0)),
        out_shape=jax.ShapeDtypeStruct((T, GV), BF16), compiler_params=_cparams("parallel"))(o, o, z, gn)


def _glanorm_bwd(o, z, rbblk, gn, dp, n_ctx, name):
    _, R, GV = o.shape
    T = R - n_ctx
    DV = GV // GLA_HEADS
    tm = _pick(n_ctx, 256, 8)
    ro = n_ctx // tm

    def body(o0_ref, o1_ref, rb_ref, gn_ref, dp_ref, do_ref, drb_ref, acc_ref):
        i = pl.program_id(0)

        @pl.when(i == 0)
        def _():
            acc_ref[...] = jnp.zeros_like(acc_ref)

        gnv = gn_ref[...]
        dgn = jnp.zeros((1, DV), F32)
        for h in range(GLA_HEADS):
            sl = slice(h * DV, (h + 1) * DV)
            og = o0_ref[0, :, sl] + o1_ref[0, :, sl]
            rb = rb_ref[:, sl]
            d = dp_ref[:, sl]
            r = lax.rsqrt(jnp.mean(og * og, axis=-1, keepdims=True) + EPS)
            xh = og * r
            drb_ref[:, sl] = (d * xh * gnv * _dsilu(rb)).astype(BF16)
            dn = d * _silu(rb)
            dgn = dgn + jnp.sum(dn * xh, axis=0, keepdims=True)
            dxh = dn * gnv
            do_ref[:, sl] = (r * (dxh - xh * jnp.mean(dxh * xh, axis=-1, keepdims=True))).astype(BF16)
        acc_ref[0:1, :] += dgn

    row = pl.BlockSpec((tm, GV), lambda i: (i, 0))
    return pl.pallas_call(
        body, name=name, grid=(T // tm,),
        in_specs=[pl.BlockSpec((1, tm, GV), lambda i: (0, i + ro, 0)), pl.BlockSpec((1, tm, GV), lambda i: (1, i + ro, 0)),
                  pl.BlockSpec((tm, GV), lambda i: (i + ro, rbblk)), pl.BlockSpec((1, DV), lambda i: (0, 0)), row],
        out_specs=[row, row, pl.BlockSpec((8, DV), lambda i: (0, 0))],
        out_shape=[jax.ShapeDtypeStruct((T, GV), BF16), jax.ShapeDtypeStruct((T, GV), BF16),
                   jax.ShapeDtypeStruct((8, DV), F32)],
        compiler_params=_cparams("arbitrary"))(o, o, z, gn, dp)


def _gate_fwd(z, gablk, gbblk, ya, yg, n_ctx, name):
    T, D = ya.shape
    tm = _pick(n_ctx, 256, 8)
    ro = n_ctx // tm

    def body(ga_ref, gb_ref, ya_ref, yg_ref, m_ref):
        m_ref[...] = (_sigmoid(ga_ref[...]) * ya_ref[...] + _sigmoid(gb_ref[...]) * yg_ref[...]).astype(BF16)

    row = pl.BlockSpec((tm, D), lambda i: (i, 0))
    return pl.pallas_call(
        body, name=name, grid=(T // tm,),
        in_specs=[pl.BlockSpec((tm, D), lambda i: (i + ro, gablk)), pl.BlockSpec((tm, D), lambda i: (i + ro, gbblk)), row, row],
        out_specs=row, out_shape=jax.ShapeDtypeStruct((T, D), BF16), compiler_params=_cparams("parallel"))(z, z, ya, yg)


def _gate_bwd(z, gablk, gbblk, ya, yg, dm, n_ctx, name):
    T, D = ya.shape
    tm = _pick(n_ctx, 256, 8)
    ro = n_ctx // tm

    def body(ga_ref, gb_ref, ya_ref, yg_ref, dm_ref, dya_ref, dyg_ref, dga_ref, dgb_ref):
        d = dm_ref[...]
        sa, sb = _sigmoid(ga_ref[...]), _sigmoid(gb_ref[...])
        dya_ref[...] = (d * sa).astype(BF16)
        dyg_ref[...] = (d * sb).astype(BF16)
        dga_ref[...] = (d * ya_ref[...] * sa * (1.0 - sa)).astype(BF16)
        dgb_ref[...] = (d * yg_ref[...] * sb * (1.0 - sb)).astype(BF16)

    row = pl.BlockSpec((tm, D), lambda i: (i, 0))
    sh = jax.ShapeDtypeStruct((T, D), BF16)
    return pl.pallas_call(
        body, name=name, grid=(T // tm,),
        in_specs=[pl.BlockSpec((tm, D), lambda i: (i + ro, gablk)), pl.BlockSpec((tm, D), lambda i: (i + ro, gbblk)), row, row, row],
        out_specs=[row] * 4, out_shape=[sh] * 4, compiler_params=_cparams("parallel"))(z, z, ya, yg, dm)


def _resnorm_fwd(x, mix, gt, g, sc, sh, name):
    T, D = x.shape
    tm = _pick(T, 256, 8)

    def body(x_ref, mix_ref, gt_ref, g_ref, sc_ref, sh_ref, x1_ref, h_ref):
        x1 = x_ref[...] + gt_ref[...] * mix_ref[...]
        x1_ref[...] = x1
        r = lax.rsqrt(jnp.mean(x1 * x1, axis=-1, keepdims=True) + EPS)
        h_ref[...] = (x1 * r * g_ref[...] * (1.0 + sc_ref[...]) + sh_ref[...]).astype(BF16)

    row = pl.BlockSpec((tm, D), lambda i: (i, 0))
    vec = pl.BlockSpec((1, D), lambda i: (0, 0))
    return pl.pallas_call(
        body, name=name, grid=(T // tm,), in_specs=[row, row, vec, vec, vec, vec], out_specs=[row, row],
        out_shape=[jax.ShapeDtypeStruct((T, D), F32), jax.ShapeDtypeStruct((T, D), BF16)],
        compiler_params=_cparams("parallel"))(x, mix, gt, g, sc, sh)


def _gate_resid_bwd(dx, val, gt, name):
    T, D = dx.shape
    tm = _pick(T, 256, 8)

    def body(dx_ref, val_ref, gt_ref, d_ref, acc_ref):
        i = pl.program_id(0)

        @pl.when(i == 0)
        def _():
            acc_ref[...] = jnp.zeros_like(acc_ref)

        d = dx_ref[...]
        d_ref[...] = (d * gt_ref[...]).astype(BF16)
        acc_ref[0:1, :] += jnp.sum(d * val_ref[...], axis=0, keepdims=True)

    row = pl.BlockSpec((tm, D), lambda i: (i, 0))
    return pl.pallas_call(
        body, name=name, grid=(T // tm,), in_specs=[row, row, pl.BlockSpec((1, D), lambda i: (0, 0))],
        out_specs=[row, pl.BlockSpec((8, D), lambda i: (0, 0))],
        out_shape=[jax.ShapeDtypeStruct((T, D), BF16), jax.ShapeDtypeStruct((8, D), F32)],
        compiler_params=_cparams("arbitrary"))(dx, val, gt)


def _loss_head(d, x1, gt, target, name):
    T, D = d.shape
    tm = _pick(T, 256, 8)

    def body(d_ref, x1_ref, gt_ref, t_ref, dy_ref, acc_ref):
        i = pl.program_id(0)

        @pl.when(i == 0)
        def _():
            acc_ref[...] = jnp.zeros_like(acc_ref)

        e = x1_ref[...] + gt_ref[...] * d_ref[...] - t_ref[...]
        dy_ref[...] = e * (1.0 / D)
        acc_ref[0:1, :] += jnp.sum(e * e, axis=0, keepdims=True)

    row = pl.BlockSpec((tm, D), lambda i: (i, 0))
    return pl.pallas_call(
        body, name=name, grid=(T // tm,), in_specs=[row, row, pl.BlockSpec((1, D), lambda i: (0, 0)), row],
        out_specs=[row, pl.BlockSpec((8, D), lambda i: (0, 0))],
        out_shape=[jax.ShapeDtypeStruct((T, D), F32), jax.ShapeDtypeStruct((8, D), F32)],
        compiler_params=_cparams("arbitrary"))(d, x1, gt, target)


def _halo_specs(T, tm, tw, col_of, order):
    n8 = tm // 8
    if order == "ij":
        mid = lambda i, j: (i, col_of(j))
        prev = lambda i, j: (jnp.maximum(i * n8 - 1, 0), col_of(j))
        nxt = lambda i, j: (jnp.minimum((i + 1) * n8, T // 8 - 1), col_of(j))
    else:
        mid = lambda j, i: (i, col_of(j))
        prev = lambda j, i: (jnp.maximum(i * n8 - 1, 0), col_of(j))
        nxt = lambda j, i: (jnp.minimum((i + 1) * n8, T // 8 - 1), col_of(j))
    return [pl.BlockSpec((tm, tw), mid), pl.BlockSpec((8, tw), prev), pl.BlockSpec((8, tw), nxt)]


def _shifted(u_ref, up_ref, un_ref, i, nt):
    u = u_ref[...]
    tm = u.shape[0]
    row = lax.broadcasted_iota(jnp.int32, u.shape, 0)
    hp = jnp.where(i > 0, up_ref[7:8, :], 0.0)
    hn = jnp.where(i < nt - 1, un_ref[0:1, :], 0.0)
    u_prev = jnp.where(row == 0, hp, pltpu.roll(u, 1, 0))
    u_next = jnp.where(row == tm - 1, hn, pltpu.roll(u, tm - 1, 0))
    return u_prev, u, u_next


def _conv_fwd(u, cw, cb, name):
    T, F2 = u.shape
    F = F2 // 2
    tm, tw = _pick(T, 256, 8), _pick(F, 512)
    nt, nw = T // tm, F // tw

    def body(ua, uap, uan, ug, ugp, ugn, cwa, cwg, cba, cbg, f_ref):
        i = pl.program_id(0)

        def conv(u_ref, up_ref, un_ref, w_ref, b_ref):
            p, m, n = _shifted(u_ref, up_ref, un_ref, i, nt)
            return p * w_ref[0:1, :] + m * w_ref[1:2, :] + n * w_ref[2:3, :] + b_ref[...]

        a = conv(ua, uap, uan, cwa, cba)
        g = conv(ug, ugp, ugn, cwg, cbg)
        f_ref[...] = (_silu(a) * g).astype(BF16)

    wspec = lambda off: pl.BlockSpec((3, tw), lambda i, j: (0, j + off))
    bspec = lambda off: pl.BlockSpec((1, tw), lambda i, j: (0, j + off))
    return pl.pallas_call(
        body, name=name, grid=(nt, nw),
        in_specs=_halo_specs(T, tm, tw, lambda j: j, "ij") + _halo_specs(T, tm, tw, lambda j: j + nw, "ij")
        + [wspec(0), wspec(nw), bspec(0), bspec(nw)],
        out_specs=pl.BlockSpec((tm, tw), lambda i, j: (i, j)),
        out_shape=jax.ShapeDtypeStruct((T, F), BF16), compiler_params=_cparams("parallel", "parallel"),
    )(u, u, u, u, u, u, cw, cw, cb, cb)


def _conv_bwd_a(u, df, cw, cb, name):
    T, F2 = u.shape
    F = F2 // 2
    tm, tw = _pick(T, 256, 8), _pick(F, 512)
    nt, nw = T // tm, F // tw

    def body(ua, uap, uan, ug, ugp, ugn, cwa, cwg, cba, cbg, df_ref, da_ref, dg_ref, acca_ref, accg_ref):
        i = pl.program_id(1)

        @pl.when(i == 0)
        def _():
            acca_ref[...] = jnp.zeros_like(acca_ref)
            accg_ref[...] = jnp.zeros_like(accg_ref)

        sa = _shifted(ua, uap, uan, i, nt)
        sg = _shifted(ug, ugp, ugn, i, nt)
        a = sa[0] * cwa[0:1, :] + sa[1] * cwa[1:2, :] + sa[2] * cwa[2:3, :] + cba[...]
        g = sg[0] * cwg[0:1, :] + sg[1] * cwg[1:2, :] + sg[2] * cwg[2:3, :] + cbg[...]
        d = df_ref[...]
        da = d * g * _dsilu(a)
        dg = d * _silu(a)
        da_ref[...] = da
        dg_ref[...] = dg
        for t in range(3):
            acca_ref[t:t + 1, :] += jnp.sum(da * sa[t], axis=0, keepdims=True)
            accg_ref[t:t + 1, :] += jnp.sum(dg * sg[t], axis=0, keepdims=True)
        acca_ref[3:4, :] += jnp.sum(da, axis=0, keepdims=True)
        accg_ref[3:4, :] += jnp.sum(dg, axis=0, keepdims=True)

    wspec = lambda off: pl.BlockSpec((3, tw), lambda j, i: (0, j + off))
    bspec = lambda off: pl.BlockSpec((1, tw), lambda j, i: (0, j + off))
    row = pl.BlockSpec((tm, tw), lambda j, i: (i, j))
    acc = pl.BlockSpec((8, tw), lambda j, i: (0, j))
    return pl.pallas_call(
        body, name=name, grid=(nw, nt),
        in_specs=_halo_specs(T, tm, tw, lambda j: j, "ji") + _halo_specs(T, tm, tw, lambda j: j + nw, "ji")
        + [wspec(0), wspec(nw), bspec(0), bspec(nw), row],
        out_specs=[row, row, acc, acc],
        out_shape=[jax.ShapeDtypeStruct((T, F), F32), jax.ShapeDtypeStruct((T, F), F32),
                   jax.ShapeDtypeStruct((8, F), F32), jax.ShapeDtypeStruct((8, F), F32)],
        compiler_params=_cparams("parallel", "arbitrary"),
    )(u, u, u, u, u, u, cw, cw, cb, cb, df)


def _conv_bwd_b(dc, cw, woff, name):
    T, F = dc.shape
    tm, tw = _pick(T, 256, 8), _pick(F, 512)
    nt, nw = T // tm, F // tw

    def body(d_ref, dp_ref, dn_ref, w_ref, o_ref):
        i = pl.program_id(0)
        p, m, n = _shifted(d_ref, dp_ref, dn_ref, i, nt)
        o_ref[...] = (n * w_ref[0:1, :] + m * w_ref[1:2, :] + p * w_ref[2:3, :]).astype(BF16)

    return pl.pallas_call(
        body, name=name, grid=(nt, nw),
        in_specs=_halo_specs(T, tm, tw, lambda j: j, "ij") + [pl.BlockSpec((3, tw), lambda i, j: (0, j + woff * nw))],
        out_specs=pl.BlockSpec((tm, tw), lambda i, j: (i, j)),
        out_shape=jax.ShapeDtypeStruct((T, F), BF16), compiler_params=_cparams("parallel", "parallel"),
    )(dc, dc, dc, cw)


def _assemble_dz(lay, Z, n_ctx, dqa, drb, dga, dgb, dka, dva, dvg, dqg, dkg, dlr, name):
    T = dqa.shape[0]
    R = T + n_ctx
    tm = _pick(n_ctx, 128, 8)
    cb = n_ctx // tm

    def body(dqa_ref, drb_ref, dga_ref, dgb_ref, dka_ref, dva_ref, dvg0, dvg1, dqg0, dqg1, dkg0, dkg1, dlr_ref, o_ref):
        lat = pl.program_id(0) >= cb

        def put(seg, val):
            o_ref[:, lay[seg]:lay[seg] + val.shape[1]] = val.astype(BF16)

        def lat_only(ref):
            v = ref[...]
            return jnp.where(lat, v, jnp.zeros_like(v))

        put("qa", lat_only(dqa_ref))
        put("rb", lat_only(drb_ref))
        put("ga", lat_only(dga_ref))
        put("gb", lat_only(dgb_ref))
        put("ka", dka_ref[...])
        put("va", dva_ref[...])
        put("vb", dvg0[0] + dvg1[0])
        put("qb", dqg0[0] + dqg1[0])
        put("kb", dkg0[0] + dkg1[0])
        put("lr", dlr_ref[...])

    lat_spec = lambda a: pl.BlockSpec((tm, a.shape[1]), lambda i: (jnp.maximum(i - cb, 0), 0))
    all_spec = lambda a: pl.BlockSpec((tm, a.shape[1]), lambda i: (i, 0))
    dir_specs = lambda a: [pl.BlockSpec((1, tm, a.shape[2]), lambda i: (0, i, 0)),
                           pl.BlockSpec((1, tm, a.shape[2]), lambda i: (1, i, 0))]
    return pl.pallas_call(
        body, name=name, grid=(R // tm,),
        in_specs=[lat_spec(dqa), lat_spec(drb), lat_spec(dga), lat_spec(dgb), all_spec(dka), all_spec(dva)]
        + dir_specs(dvg) + dir_specs(dqg) + dir_specs(dkg) + [all_spec(dlr)],
        out_specs=pl.BlockSpec((tm, Z), lambda i: (i, 0)),
        out_shape=jax.ShapeDtypeStruct((R, Z), BF16), compiler_params=_cparams("parallel"),
    )(dqa, drb, dga, dgb, dka, dva, dvg, dvg, dqg, dqg, dkg, dkg, dlr)


def _mod_fwd(ca, w, b, name):
    n, D = ca.shape
    N = w.shape[1]
    tn = _pick(N, 512)

    def body(c_ref, w_ref, b_ref, o_ref, s_ref):
        s = _silu(c_ref[...])
        s_ref[...] = s
        o_ref[...] = _dot(s.astype(BF16), w_ref[...].astype(BF16), NN) + b_ref[...]

    return pl.pallas_call(
        body, name=name, grid=(N // tn,),
        in_specs=[pl.BlockSpec((n, D), lambda j: (0, 0)), pl.BlockSpec((D, tn), lambda j: (0, j)),
                  pl.BlockSpec((1, tn), lambda j: (0, j))],
        out_specs=[pl.BlockSpec((n, tn), lambda j: (0, j)), pl.BlockSpec((n, D), lambda j: (0, 0))],
        out_shape=[jax.ShapeDtypeStruct((n, N), F32), jax.ShapeDtypeStruct((n, D), F32)],
        compiler_params=_cparams("arbitrary"))(ca, w, b)


def _silu_bwd(dsil, ca, name):
    def body(d_ref, c_ref, o_ref):
        o_ref[...] = d_ref[...] * _dsilu(c_ref[...])

    return pl.pallas_call(body, name=name, out_shape=jax.ShapeDtypeStruct(ca.shape, F32))(dsil, ca)


def _adamw(w, g, m, v, name):
    Rw, Cw = w.shape
    tr = _pick(Rw, 128, 8)
    c1 = 1.0 - ADAM_B1 ** ADAM_STEP
    c2 = 1.0 - ADAM_B2 ** ADAM_STEP

    def body(w_ref, g_ref, m_ref, v_ref, d_ref, mo_ref, vo_ref):
        gv = g_ref[...]
        mn = ADAM_B1 * m_ref[...] + (1.0 - ADAM_B1) * gv
        vn = ADAM_B2 * v_ref[...] + (1.0 - ADAM_B2) * (gv * gv)
        mo_ref[...] = mn
        vo_ref[...] = vn
        d_ref[...] = -ADAM_LR * ((mn / c1) / (jnp.sqrt(vn / c2) + ADAM_EPS) + ADAM_WD * w_ref[...])

    row = pl.BlockSpec((tr, Cw), lambda i: (i, 0))
    sh = jax.ShapeDtypeStruct((Rw, Cw), F32)
    return pl.pallas_call(body, name=name, grid=(Rw // tr,), in_specs=[row] * 4, out_specs=[row] * 3,
                          out_shape=[sh] * 3, compiler_params=_cparams("parallel"))(w, g, m, v)


HBM_SPEC = pl.BlockSpec(memory_space=pltpu.HBM)


def _exchange(inputs, out_shapes, stages, name):
    n_in, n_out = len(inputs), len(out_shapes)
    n = sum(len(s) for s in stages)

    def body(*refs):
        ins, outs = refs[:n_in], refs[n_in:n_in + n_out]
        send_sems, recv_sems = refs[n_in + n_out:]
        me = (lax.axis_index("x"), lax.axis_index("y"), lax.axis_index("c"))
        k = 0
        for stage in stages:
            copies = []
            for (skind, sidx), sfn, didx, dfn, flip in stage:
                src = (ins if skind == "in" else outs)[sidx].at[sfn(*me)]
                dst = outs[didx].at[dfn(*me)]
                if flip == (0, 0, 0):
                    cp = pltpu.make_async_copy(src, dst, send_sems.at[k])
                else:
                    peer = tuple(1 - a if f else a for a, f in zip(me, flip))
                    cp = pltpu.make_async_remote_copy(src, dst, send_sems.at[k], recv_sems.at[k],
                                                      device_id=peer, device_id_type=MESH)
                cp.start()
                copies.append(cp)
                k += 1
            for cp in copies:
                cp.wait()

    return pl.pallas_call(
        body, name=name, in_specs=[HBM_SPEC] * n_in, out_specs=[HBM_SPEC] * n_out, out_shape=out_shapes,
        scratch_shapes=[pltpu.SemaphoreType.DMA((n,)), pltpu.SemaphoreType.DMA((n,))],
    )(*inputs)


FLIPS_ALL = [(0, 0, 1), (0, 1, 0), (0, 1, 1), (1, 0, 0), (1, 0, 1), (1, 1, 0), (1, 1, 1)]
FLIPS_CHIP = [(0, 1, 0), (1, 0, 0), (1, 1, 0)]


def _sum_slots(buf, name):
    n, r, w = buf.shape
    tr = _pick(r, 256, 8)

    def body(b_ref, o_ref):
        acc = b_ref[0]
        for s in range(1, n):
            acc = acc + b_ref[s]
        o_ref[...] = acc

    return pl.pallas_call(
        body, name=name, grid=(r // tr,), in_specs=[pl.BlockSpec((n, tr, w), lambda i: (0, i, 0))],
        out_specs=pl.BlockSpec((tr, w), lambda i: (i, 0)), out_shape=jax.ShapeDtypeStruct((r, w), F32),
        compiler_params=_cparams("parallel"))(buf)


def _allreduce(buf, name):
    r, w = buf.shape
    whole = lambda x, y, c: (slice(None), slice(None))
    slot = lambda x, y, c: (4 * x + 2 * y + c,)
    stage = [(("in", 0), whole, 0, slot, f) for f in [(0, 0, 0)] + FLIPS_ALL]
    (slots,) = _exchange([buf], [jax.ShapeDtypeStruct((8, r, w), F32)], [stage], name + "_x")
    return _sum_slots(slots, name + "_sum")


def _allgather_weights(shards, name):
    half = lambda a, c: pl.ds(c * (a.shape[0] // 2), a.shape[0] // 2)
    first, second = [], []
    for n, a in enumerate(shards):
        first.append((("in", n), lambda x, y, c: (slice(None), slice(None)), n,
                      lambda x, y, c: (2 * x + y,), (0, 0, 0)))
        for f in FLIPS_CHIP:
            first.append((("in", n), lambda x, y, c, a=a: (half(a, c), slice(None)), n,
                          lambda x, y, c, a=a: (2 * x + y, half(a, c), slice(None)), f))
            peer_slot = lambda x, y, c, a=a, f=f: (2 * (x ^ f[0]) + (y ^ f[1]), half(a, c), slice(None))
            second.append((("out", n), peer_slot, n, peer_slot, (0, 0, 1)))
    outs = [jax.ShapeDtypeStruct((4,) + a.shape, a.dtype) for a in shards]
    return _exchange(shards, outs, [first, second], name)


def _add_pair(G, bufA, cvec, name):
    _, Rs, Cs = G.shape
    Rh = Rs // 2
    tr = _pick(Rh, 128, 8)
    nb = Rh // tr

    def body(c_ref, g_ref, a_ref, o_ref):
        o_ref[...] = g_ref[...] + a_ref[...]

    grid_spec = pltpu.PrefetchScalarGridSpec(
        num_scalar_prefetch=1, grid=(4, nb),
        in_specs=[pl.BlockSpec((1, tr, Cs), lambda s, i, c_ref: (s, c_ref[0] * nb + i, 0)),
                  pl.BlockSpec((1, tr, Cs), lambda s, i, c_ref: (s, i, 0))],
        out_specs=pl.BlockSpec((1, tr, Cs), lambda s, i, c_ref: (s, i, 0)))
    return pl.pallas_call(body, name=name, grid_spec=grid_spec, out_shape=jax.ShapeDtypeStruct((4, Rh, Cs), F32),
                          compiler_params=_cparams("parallel", "parallel"))(cvec, G, bufA)


def _reduce_scatter(grads, cvec, name):
    ng = len(grads)
    Rh = [g.shape[1] // 2 for g in grads]
    st = [(("in", n), lambda x, y, c, n=n: (slice(None), pl.ds((1 - c) * Rh[n], Rh[n]), slice(None)), n,
           lambda x, y, c: (slice(None), slice(None), slice(None)), (0, 0, 1)) for n in range(ng)]
    bufA = _exchange(grads, [jax.ShapeDtypeStruct((4, Rh[n], g.shape[2]), F32) for n, g in enumerate(grads)],
                     [st], name + "_pair")
    P = [_add_pair(g, a, cvec, "%s_add%d" % (name, n)) for n, (g, a) in enumerate(zip(grads, bufA))]
    st = []
    for n in range(ng):
        mine = lambda x, y, c: (2 * x + y,)
        st.append((("in", n), mine, n, mine, (0, 0, 0)))
        for f in FLIPS_CHIP:
            st.append((("in", n), lambda x, y, c, f=f: (2 * (x ^ f[0]) + (y ^ f[1]),), n, mine, f))
    bufB = _exchange(P, [jax.ShapeDtypeStruct(p.shape, F32) for p in P], [st], name + "_chips")
    Rd = [_sum_slots(b, "%s_sum%d" % (name, n)) for n, b in enumerate(bufB)]
    st = []
    for n in range(ng):
        rows = lambda x, y, c, n=n: (pl.ds(c * Rh[n], Rh[n]), slice(None))
        whole = lambda x, y, c: (slice(None), slice(None))
        st.append((("in", n), whole, n, rows, (0, 0, 0)))
        st.append((("in", n), whole, n, rows, (0, 0, 1)))
    return _exchange(Rd, [jax.ShapeDtypeStruct((2 * Rh[n], g.shape[2]), F32) for n, g in enumerate(grads)],
                     [st], name + "_halves")


def _pack(arrays):
    flat = [a.reshape(-1).astype(F32) for a in arrays]
    meta, off = [], 0
    for a, f in zip(arrays, flat):
        meta.append((off, a.shape))
        off += f.shape[0]
    total = -(-off // (8 * LANES)) * (8 * LANES)
    flat.append(jnp.zeros((total - off,), F32))
    return jnp.concatenate(flat).reshape(total // LANES, LANES), meta


def _unpack(buf, meta):
    flat = buf.reshape(-1)
    out = []
    for off, shape in meta:
        size = 1
        for s in shape:
            size *= s
        out.append(flat[off:off + size].reshape(shape))
    return out


WEIGHT_NAMES = ["c_ctx", "w_mod", "b_mod", "g_mix", "w_in", "q_norm", "k_norm", "attn_sink", "w_gate_f", "b_gate_f",
                "w_gate_b", "b_gate_b", "gla_norm", "w_attn_o", "w_gla_o", "w_out", "g_ffn", "w_up", "conv_w",
                "conv_b", "w_down"]
BIG_NAMES = ["w_in", "w_attn_o", "w_gla_o", "w_out", "w_up", "w_down"]
SHARDED_SMALL = ["w_gate_f", "w_gate_b", "conv_w"]


def _layouts(D):
    aw, kvw, gk, gv = N_Q_HEADS * HEAD_DIM, N_KV_HEADS * HEAD_DIM, D // 2, D
    widths = {"qa": aw, "ka": kvw, "va": kvw, "qb": gk, "kb": gk, "vb": gv, "rb": gv, "lr": 2 * GLA_LOWRANK,
              "ga": D, "gb": D}
    orig, off = {}, 0
    for s in ["qa", "ka", "va", "qb", "kb", "vb", "rb", "lr", "ga", "gb"]:
        orig[s] = off
        off += widths[s]
    order = ["qa", "vb", "rb", "ga", "gb", "ka", "va", "qb", "kb", "lr"]
    lay, off = {}, 0
    for s in order:
        lay[s] = off
        off += LANES if s == "lr" else widths[s]
    align = {"qa": aw, "vb": D, "rb": D, "ga": D, "gb": D, "ka": kvw, "va": kvw, "qb": gk // GLA_HEADS,
             "kb": gk // GLA_HEADS, "lr": LANES}
    for s in order:
        assert lay[s] % align[s] == 0, (s, lay[s], align[s])
    return widths, orig, order, lay, off


def _rope_tables(T, L):
    t = jnp.arange(T)
    nf = HEAD_DIM // 4
    inv = ROPE_THETA ** (-jnp.arange(nf, dtype=F32) / nf)
    ang = jnp.concatenate([(t // GRID_W)[:, None] * inv, (t % GRID_W)[:, None] * inv], axis=-1)
    cos, sin = jnp.cos(ang), jnp.sin(ang)
    cos2 = jnp.concatenate([jnp.ones((L, HEAD_DIM), F32), jnp.concatenate([cos, cos], axis=-1)], axis=0)
    sin2 = jnp.concatenate([jnp.zeros((L, HEAD_DIM), F32), jnp.concatenate([-sin, sin], axis=-1)], axis=0)
    return cos2, sin2


def _step(x, c, ctx, loss_target, W, M, V):
    xi, yi, ci = lax.axis_index("x"), lax.axis_index("y"), lax.axis_index("c")
    chip = 2 * xi + yi
    dev = 2 * chip + ci
    south = (ci == 0).astype(F32)
    cvec = ci.reshape(1).astype(jnp.int32)
    T, D = x.shape[1], x.shape[2]
    L = ctx.shape[1]
    R = L + T
    F = 4 * W["w_down"].shape[1]
    GK, GV = D // 2, D
    DK, DV = GK // GLA_HEADS, GV // GLA_HEADS
    N6 = 6 * D
    N4 = N6 // 4
    widths, orig, order, lay, Z = _layouts(D)

    def place_cols(shard, full_cols):
        cols = shard.shape[-1]
        full = jnp.zeros(shard.shape[:-1] + (full_cols,), F32)
        return lax.dynamic_update_slice(full, shard * south, (0,) * (shard.ndim - 1) + (chip * cols,))

    c_rows = lax.dynamic_update_slice(jnp.zeros((8, D), F32), c, (dev, 0))
    bufa, meta = _pack([c_rows, place_cols(W["w_gate_f"][0], GK), place_cols(W["w_gate_b"][0], GK),
                        place_cols(W["conv_w"][0], 2 * F)])
    c_all, wgf, wgb, cw = _unpack(_allreduce(bufa, "gather_small"), meta)
    ca = jnp.concatenate([c_all, W["c_ctx"][None, :], jnp.zeros((7, D), F32)], axis=0)
    b_shard = lax.dynamic_slice(W["b_mod"], (0, chip * N4), (1, N4))
    mod_part, sil = _mod_fwd(ca, W["w_mod"][0], b_shard, "mod_fwd")
    slots = lax.dynamic_update_slice(jnp.zeros((4, 16, N4), F32), (mod_part * south)[None], (chip, 0, 0))
    mod_all = _allreduce(slots.reshape(64, N4), "gather_mod").reshape(4, 16, N4).transpose(1, 0, 2).reshape(16, N6)
    mx = lax.dynamic_slice(mod_all, (dev, 0), (1, N6)).reshape(6, 1, D)
    mc = mod_all[8].reshape(6, 1, D)

    shards = [W[n][0].astype(BF16) for n in BIG_NAMES]
    g_in, g_ao, g_go, g_out, g_up, g_dn = _allgather_weights(shards, "gather_weights")
    cols = lambda g: g.transpose(1, 0, 2).reshape(g.shape[1], 4 * g.shape[2])
    rows = lambda g: g.reshape(4 * g.shape[1], g.shape[2])
    w_in_f = cols(g_in)
    seg = lambda s: w_in_f[:, orig[s]:orig[s] + widths[s]]
    w_cat = jnp.concatenate([jnp.pad(seg(s), ((0, 0), (0, LANES - widths[s]))) if s == "lr" else seg(s)
                             for s in order], axis=1)
    w_ao, w_go, w_out, w_up, w_dn = rows(g_ao), rows(g_go), rows(g_out), cols(g_up), rows(g_dn)
    wg = jnp.zeros((2, LANES, GK), F32).at[0, :GLA_LOWRANK].set(wgf).at[1, GLA_LOWRANK:2 * GLA_LOWRANK].set(wgb)
    bg = jnp.stack([W["b_gate_f"], W["b_gate_b"]])
    cb = W["conv_b"]
    sink_rows = jnp.broadcast_to(W["attn_sink"][0][:, None], (N_Q_HEADS, HEAD_DIM))
    cos2, sin2 = _rope_tables(T, L)
    blk = lambda s, w: lay[s] // w

    xall = jnp.concatenate([ctx[0], x[0]], axis=0)
    sc1 = jnp.stack([mc[1], mx[1]])
    sh1 = jnp.stack([mc[0], mx[0]])
    h = _modnorm_fwd(xall, W["g_mix"], sc1, sh1, L, "modnorm1")
    z = _matmul(h, w_cat, "nn", F32, "proj_in")
    qn = _qknorm_fwd(z, blk("qa", widths["qa"]), T, L, W["q_norm"], cos2, sin2, N_Q_HEADS, "qnorm")
    kn = _qknorm_fwd(z, blk("ka", widths["ka"]), R, 0, W["k_norm"], cos2, sin2, N_KV_HEADS, "knorm")
    vb = _cast_seg(z, blk("va", widths["va"]), widths["va"], "vcast")
    o_attn = _attn_fwd(qn, kn, vb, sink_rows, L, "attn_fwd")
    gla_blks = (blk("qb", DK), blk("kb", DK), blk("vb", DV), blk("lr", LANES))
    o_g, sprev = _gla_fwd(z, *gla_blks, wg, bg, DV, L, "gla_fwd")
    p = _glanorm_fwd(o_g, z, blk("rb", D), W["gla_norm"], L, "glanorm")
    ya = _matmul(o_attn, w_ao, "nn", F32, "proj_attn_o")
    yg = _matmul(p, w_go, "nn", F32, "proj_gla_o")
    m = _gate_fwd(z, blk("ga", D), blk("gb", D), ya, yg, L, "gate")
    mix = _matmul(m, w_out, "nn", F32, "proj_out")
    x1, h2 = _resnorm_fwd(x[0], mix, mx[2], W["g_ffn"], mx[4], mx[3], "resnorm2")
    u = _matmul(h2, w_up, "nn", F32, "ffn_up")
    f = _conv_fwd(u, cw, cb, "conv_swiglu")
    d = _matmul(f, w_dn, "nn", F32, "ffn_down")
    dy, lacc = _loss_head(d, x1, mx[5], loss_target[0], "loss_head")
    loss = lax.psum((0.5 / D) * jnp.sum(lacc[0]), ("x", "y", "c"))

    dd, s_gt2 = _gate_resid_bwd(dy, d, mx[5], "gate2_bwd")
    gw_dn = _matmul(f, dd, "tn", F32, "ffn_down_dw")
    df = _matmul(dd, w_dn, "nt", F32, "ffn_down_dx")
    dca, dcg, acca, accg = _conv_bwd_a(u, df, cw, cb, "conv_swiglu_bwd")
    du = jnp.concatenate([_conv_bwd_b(dca, cw, 0, "conv_t_a"), _conv_bwd_b(dcg, cw, 1, "conv_t_g")], axis=1)
    gw_up = _matmul(h2, du, "tn", F32, "ffn_up_dw")
    dh2 = _matmul(du, w_up, "nt", F32, "ffn_up_dx")
    dx1, s2 = _modnorm_bwd(x1, dh2, W["g_ffn"], mx[4], dy, "resnorm2_bwd")
    dmix, s_gt1 = _gate_resid_bwd(dx1, mix, mx[2], "gate1_bwd")
    gw_out = _matmul(m, dmix, "tn", F32, "proj_out_dw")
    dm = _matmul(dmix, w_out, "nt", F32, "proj_out_dx")
    dya, dyg, dga, dgb = _gate_bwd(z, blk("ga", D), blk("gb", D), ya, yg, dm, L, "gate_bwd")
    gw_ao = _matmul(o_attn, dya, "tn", F32, "proj_attn_o_dw")
    do_attn = _matmul(dya, w_ao, "nt", BF16, "proj_attn_o_dx")
    gw_go = _matmul(p, dyg, "tn", F32, "proj_gla_o_dw")
    dp = _matmul(dyg, w_go, "nt", F32, "proj_gla_o_dx")
    do_gla, drb, s_gn = _glanorm_bwd(o_g, z, blk("rb", D), W["gla_norm"], dp, L, "glanorm_bwd")
    do_pad = jnp.concatenate([jnp.zeros((L, GV), BF16), do_gla], axis=0)
    dqg, dkg, dvg, dpre, dbg = _gla_bwd(z, *gla_blks, wg, bg, sprev, do_pad, L, "gla_bwd")
    wg_cat = jnp.concatenate([wg[0], wg[1]], axis=1)
    dlr = _matmul(dpre, wg_cat, "nt", BF16, "gla_gate_dx")
    dwg = _matmul(z[:, lay["lr"]:lay["lr"] + LANES], dpre, "tn", F32, "gla_gate_dw")
    dqn, dkw, dvw, dkc, dvc, dsn = _attn_bwd(qn, kn, vb, sink_rows, do_attn, L, "attn_bwd")
    dqa, s_qn = _qknorm_bwd(z, blk("qa", widths["qa"]), T, L, W["q_norm"], cos2, sin2, dqn, N_Q_HEADS, "qnorm_bwd")
    dk_all = jnp.concatenate([dkc, dkw[WINDOW:WINDOW + T]], axis=0)
    dv_all = jnp.concatenate([dvc, dvw[WINDOW:WINDOW + T]], axis=0)
    dka, s_kn = _qknorm_bwd(z, blk("ka", widths["ka"]), R, 0, W["k_norm"], cos2, sin2, dk_all, N_KV_HEADS, "knorm_bwd")
    dz = _assemble_dz(lay, Z, L, dqa, drb, dga, dgb, dka, dv_all, dvg, dqg, dkg, dlr, "assemble_dz")
    gw_cat = _matmul(h, dz, "tn", F32, "proj_in_dw")
    dh = _matmul(dz, w_cat, "nt", F32, "proj_in_dx")
    grad_x, s1 = _modnorm_bwd(x[0], dh[L:], W["g_mix"], mx[1], dx1, "modnorm1_bwd")
    _, s1c = _modnorm_bwd(ctx[0], dh[:L], W["g_mix"], mc[1], None, "modnorm1_ctx_bwd")

    dmod_x = jnp.concatenate([s1[0], s1[1], s_gt1[0], s2[0], s2[1], s_gt2[0]])
    dmod_c = jnp.concatenate([s1c[0], s1c[1], jnp.zeros((4 * D,), F32)])
    dmod_rows = lax.dynamic_update_slice(jnp.zeros((9, N6), F32).at[8].set(dmod_c), dmod_x[None], (dev, 0))
    small = [dmod_rows, dmod_x + dmod_c, s1[2] + s1c[2], s_qn[0], s_kn[0], dsn[:, 0, :Q_PER_KV].reshape(N_Q_HEADS),
             dwg[:GLA_LOWRANK, :GK], dbg[0].reshape(GK), dwg[GLA_LOWRANK:2 * GLA_LOWRANK, GK:], dbg[1].reshape(GK),
             s_gn[0], s2[2], jnp.concatenate([acca[0:3], accg[0:3]], axis=1), jnp.concatenate([acca[3], accg[3]])]
    bufc, meta = _pack(small)
    (dmod_sum, g_b_mod, g_g_mix, g_q_norm, g_k_norm, g_sink, g_wgf, g_bgf, g_wgb, g_bgb, g_gla_norm, g_g_ffn,
     g_conv_w, g_conv_b) = _unpack(_allreduce(bufc, "reduce_small"), meta)
    dmod16 = lax.dynamic_slice(jnp.concatenate([dmod_sum, jnp.zeros((7, N6), F32)], axis=0), (0, chip * N4), (16, N4))
    g_w_mod = _matmul(sil, dmod16, "tn", F32, "mod_dw")
    dsil = _matmul(dmod16, W["w_mod"][0], "nt", F32, "mod_dx")
    g_c_ctx = _silu_bwd(_allreduce(dsil * south, "reduce_cctx"), ca, "silu_bwd")[8]

    gw_in = jnp.concatenate([gw_cat[:, lay[s]:lay[s] + widths[s]] for s in ["qa", "ka", "va", "qb", "kb", "vb", "rb",
                                                                           "lr", "ga", "gb"]], axis=1)
    by_cols = lambda g: g.reshape(g.shape[0], 4, g.shape[1] // 4).transpose(1, 0, 2)
    by_rows = lambda g: g.reshape(4, g.shape[0] // 4, g.shape[1])
    big = _reduce_scatter([by_cols(gw_in), by_rows(gw_ao), by_rows(gw_go), by_rows(gw_out), by_cols(gw_up),
                           by_rows(gw_dn)], cvec, "reduce_big")
    cut = lambda g: lax.dynamic_slice(g, (0, chip * (g.shape[1] // 4)), (g.shape[0], g.shape[1] // 4))
    grads = {"c_ctx": g_c_ctx, "w_mod": g_w_mod[None], "b_mod": g_b_mod[None], "g_mix": g_g_mix[None],
             "q_norm": g_q_norm[None], "k_norm": g_k_norm[None], "attn_sink": g_sink[None],
             "w_gate_f": cut(g_wgf)[None], "b_gate_f": g_bgf[None], "w_gate_b": cut(g_wgb)[None],
             "b_gate_b": g_bgb[None], "gla_norm": g_gla_norm[None], "g_ffn": g_g_ffn[None],
             "conv_w": cut(g_conv_w)[None], "conv_b": g_conv_b[None]}
    for n, g in zip(BIG_NAMES, big):
        grads[n] = g[None]

    delta, new_m, new_v = {}, {}, {}
    for n in ["w_mod"] + BIG_NAMES:
        dl, mn, vn = _adamw(W[n][0], grads[n][0], M[n][0], V[n][0], "adamw_" + n)
        delta[n], new_m[n], new_v[n] = dl[None], mn[None], vn[None]
    small_names = [n for n in WEIGHT_NAMES if n not in delta]
    packs = [_pack([src[n] for n in small_names]) for src in (W, grads, M, V)]
    meta = packs[0][1]
    outs = _adamw(packs[0][0], packs[1][0], packs[2][0], packs[3][0], "adamw_small")
    for res, o in zip((delta, new_m, new_v), outs):
        for n, a in zip(small_names, _unpack(o, meta)):
            res[n] = a
    return (loss, grad_x[None], *[grads[n] for n in WEIGHT_NAMES], *[delta[n] for n in WEIGHT_NAMES],
            *[new_m[n] for n in WEIGHT_NAMES], *[new_v[n] for n in WEIGHT_NAMES])


def kernel(x, c, ctx, c_ctx, w_mod, b_mod, g_mix, w_in, q_norm, k_norm, attn_sink, w_gate_f, b_gate_f, w_gate_b, b_gate_b, gla_norm, w_attn_o, w_gla_o, w_out, g_ffn, w_up, conv_w, conv_b, w_down, loss_target, m_c_ctx, m_w_mod, m_b_mod, m_g_mix, m_w_in, m_q_norm, m_k_norm, m_attn_sink, m_w_gate_f, m_b_gate_f, m_w_gate_b, m_b_gate_b, m_gla_norm, m_w_attn_o, m_w_gla_o, m_w_out, m_g_ffn, m_w_up, m_conv_w, m_conv_b, m_w_down, v_c_ctx, v_w_mod, v_b_mod, v_g_mix, v_w_in, v_q_norm, v_k_norm, v_attn_sink, v_w_gate_f, v_b_gate_f, v_w_gate_b, v_b_gate_b, v_gla_norm, v_w_attn_o, v_w_gla_o, v_w_out, v_g_ffn, v_w_up, v_conv_w, v_conv_b, v_w_down):
    W = dict(zip(WEIGHT_NAMES, (c_ctx, w_mod, b_mod, g_mix, w_in, q_norm, k_norm, attn_sink, w_gate_f, b_gate_f,
                                w_gate_b, b_gate_b, gla_norm, w_attn_o, w_gla_o, w_out, g_ffn, w_up, conv_w, conv_b,
                                w_down)))
    M = dict(zip(WEIGHT_NAMES, (m_c_ctx, m_w_mod, m_b_mod, m_g_mix, m_w_in, m_q_norm, m_k_norm, m_attn_sink,
                                m_w_gate_f, m_b_gate_f, m_w_gate_b, m_b_gate_b, m_gla_norm, m_w_attn_o, m_w_gla_o,
                                m_w_out, m_g_ffn, m_w_up, m_conv_w, m_conv_b, m_w_down)))
    V = dict(zip(WEIGHT_NAMES, (v_c_ctx, v_w_mod, v_b_mod, v_g_mix, v_w_in, v_q_norm, v_k_norm, v_attn_sink,
                                v_w_gate_f, v_b_gate_f, v_w_gate_b, v_b_gate_b, v_gla_norm, v_w_attn_o, v_w_gla_o,
                                v_w_out, v_g_ffn, v_w_up, v_conv_w, v_conv_b, v_w_down)))
    return _step(x, c, ctx, loss_target, W, M, V)
```

```python
import functools
import math

import jax
import jax.numpy as jnp
from jax import lax
from jax.experimental import pallas as pl
from jax.experimental.pallas import tpu as pltpu

F32 = jnp.float32
BF16 = jnp.bfloat16
MESH = pl.DeviceIdType.MESH

EPS = 1e-6
HEAD_DIM = 128
N_Q_HEADS = 16
N_KV_HEADS = 4
Q_PER_KV = N_Q_HEADS // N_KV_HEADS
WINDOW = 128
GLA_HEADS = 4
GLA_LOWRANK = 16
GLA_GATE_NORM = 16.0
GLA_CHUNK = 64
GRID_W = 64
ROPE_THETA = 10000.0
GLA_LEVELS = (32, 16, 8, 4, 2, 1)
LANES = 128

ADAM_LR = 0.001
ADAM_B1 = 0.9
ADAM_B2 = 0.999
ADAM_EPS = 1e-08
ADAM_WD = 0.01
ADAM_STEP = 10

VMEM_LIMIT = 52 * 1024 * 1024


def _cparams(*sem):
    return pltpu.CompilerParams(dimension_semantics=sem, vmem_limit_bytes=VMEM_LIMIT)


def _pick(n, target, mult=LANES):
    best = None
    d = mult
    while d <= min(n, target):
        if n % d == 0:
            best = d
        d += mult
    return n if best is None else best


def _sigmoid(x):
    return 1.0 / (1.0 + jnp.exp(-x))


def _silu(x):
    return x * _sigmoid(x)


def _dsilu(x):
    s = _sigmoid(x)
    return s * (1.0 + x * (1.0 - s))


def _dot(a, b, dims):
    return lax.dot_general(a, b, (dims, ((), ())), preferred_element_type=F32)


NN = ((1,), (0,))
NT = ((1,), (1,))
TN = ((0,), (0,))


def _matmul(a, b, mode, out_dtype, name, tm=768, tn=1024, tk=2048):
    if mode == "nn":
        (M, K), (K2, N) = a.shape, b.shape
    elif mode == "nt":
        (M, K), (N, K2) = a.shape, b.shape
    else:
        (K, M), (K2, N) = a.shape, b.shape
    assert K == K2, (name, a.shape, b.shape)
    tm, tn, tk = _pick(M, tm), _pick(N, tn), _pick(K, tk)
    nk = K // tk
    dims = {"nn": NN, "nt": NT, "tn": TN}[mode]

    def body(a_ref, b_ref, o_ref, acc_ref):
        k = pl.program_id(2)

        @pl.when(k == 0)
        def _():
            acc_ref[...] = jnp.zeros_like(acc_ref)

        acc_ref[...] += _dot(a_ref[...].astype(BF16), b_ref[...].astype(BF16), dims)

        @pl.when(k == nk - 1)
        def _():
            o_ref[...] = acc_ref[...].astype(out_dtype)

    if mode == "tn":
        a_spec = pl.BlockSpec((tk, tm), lambda i, j, k: (k, i))
    else:
        a_spec = pl.BlockSpec((tm, tk), lambda i, j, k: (i, k))
    if mode == "nt":
        b_spec = pl.BlockSpec((tn, tk), lambda i, j, k: (j, k))
    else:
        b_spec = pl.BlockSpec((tk, tn), lambda i, j, k: (k, j))
    return pl.pallas_call(
        body, name=name, grid=(M // tm, N // tn, nk),
        in_specs=[a_spec, b_spec],
        out_specs=pl.BlockSpec((tm, tn), lambda i, j, k: (i, j)),
        out_shape=jax.ShapeDtypeStruct((M, N), out_dtype),
        scratch_shapes=[pltpu.VMEM((tm, tn), F32)],
        compiler_params=_cparams("parallel", "parallel", "arbitrary"),
    )(a, b)


def _modnorm_fwd(xall, g, sc, sh, n_ctx, name):
    R, D = xall.shape
    tm = _pick(n_ctx, 256, 8)
    cb = n_ctx // tm

    def body(x_ref, g_ref, sc_ref, sh_ref, h_ref):
        x = x_ref[...]
        r = lax.rsqrt(jnp.mean(x * x, axis=-1, keepdims=True) + EPS)
        n = x * r * g_ref[...]
        h_ref[...] = (n * (1.0 + sc_ref[0]) + sh_ref[0]).astype(BF16)

    sel = lambda i: (jnp.where(i < cb, 0, 1), 0, 0)
    return pl.pallas_call(
        body, name=name, grid=(R // tm,),
        in_specs=[pl.BlockSpec((tm, D), lambda i: (i, 0)), pl.BlockSpec((1, D), lambda i: (0, 0)),
                  pl.BlockSpec((1, 1, D), sel), pl.BlockSpec((1, 1, D), sel)],
        out_specs=pl.BlockSpec((tm, D), lambda i: (i, 0)),
        out_shape=jax.ShapeDtypeStruct((R, D), BF16),
        compiler_params=_cparams("parallel"),
    )(xall, g, sc, sh)


def _modnorm_bwd(x, dh, g, sc, resid, name):
    N, D = x.shape
    tm = _pick(N, 256, 8)
    want_dx = resid is not None

    def body(*refs):
        if want_dx:
            x_ref, dh_ref, g_ref, sc_ref, res_ref, dx_ref, acc_ref = refs
        else:
            x_ref, dh_ref, g_ref, sc_ref, acc_ref = refs
        i = pl.program_id(0)

        @pl.when(i == 0)
        def _():
            acc_ref[...] = jnp.zeros_like(acc_ref)

        xv, dhv, gv = x_ref[...], dh_ref[...], g_ref[...]
        r = lax.rsqrt(jnp.mean(xv * xv, axis=-1, keepdims=True) + EPS)
        xh = xv * r
        dn = dhv * (1.0 + sc_ref[...])
        acc_ref[0:1, :] += jnp.sum(dhv, axis=0, keepdims=True)
        acc_ref[1:2, :] += jnp.sum(dhv * xh * gv, axis=0, keepdims=True)
        acc_ref[2:3, :] += jnp.sum(dn * xh, axis=0, keepdims=True)
        if want_dx:
            dxh = dn * gv
            dx_ref[...] = res_ref[...] + r * (dxh - xh * jnp.mean(dxh * xh, axis=-1, keepdims=True))

    row = pl.BlockSpec((tm, D), lambda i: (i, 0))
    vec = pl.BlockSpec((1, D), lambda i: (0, 0))
    acc = pl.BlockSpec((8, D), lambda i: (0, 0))
    acc_shape = jax.ShapeDtypeStruct((8, D), F32)
    if want_dx:
        return pl.pallas_call(
            body, name=name, grid=(N // tm,), in_specs=[row, row, vec, vec, row],
            out_specs=[row, acc], out_shape=[jax.ShapeDtypeStruct((N, D), F32), acc_shape],
            compiler_params=_cparams("arbitrary"))(x, dh, g, sc, resid)
    sums = pl.pallas_call(
        body, name=name, grid=(N // tm,), in_specs=[row, row, vec, vec],
        out_specs=acc, out_shape=acc_shape, compiler_params=_cparams("arbitrary"))(x, dh, g, sc)
    return None, sums


def _qknorm_fwd(z, cblk, nrows, roff, w, cos2, sin2, nh, name):
    W = nh * HEAD_DIM
    tm = _pick(math.gcd(nrows, roff), 256, 8)
    ro = roff // tm
    assert roff % tm == 0

    def body(z_ref, w_ref, c_ref, s_ref, o_ref):
        c, s, wv = c_ref[...], s_ref[...], w_ref[...]
        for h in range(nh):
            x = z_ref[:, h * HEAD_DIM:(h + 1) * HEAD_DIM]
            r = lax.rsqrt(jnp.mean(x * x, axis=-1, keepdims=True) + EPS)
            y = x * r * wv
            o_ref[:, h * HEAD_DIM:(h + 1) * HEAD_DIM] = (y * c + pltpu.roll(y, HEAD_DIM // 2, 1) * s).astype(BF16)

    return pl.pallas_call(
        body, name=name, grid=(nrows // tm,),
        in_specs=[pl.BlockSpec((tm, W), lambda i: (i + ro, cblk)), pl.BlockSpec((1, HEAD_DIM), lambda i: (0, 0)),
                  pl.BlockSpec((tm, HEAD_DIM), lambda i: (i + ro, 0)), pl.BlockSpec((tm, HEAD_DIM), lambda i: (i + ro, 0))],
        out_specs=pl.BlockSpec((tm, W), lambda i: (i, 0)),
        out_shape=jax.ShapeDtypeStruct((nrows, W), BF16),
        compiler_params=_cparams("parallel"),
    )(z, w, cos2, sin2)


def _qknorm_bwd(z, cblk, nrows, roff, w, cos2, sin2, dy, nh, name):
    W = nh * HEAD_DIM
    tm = _pick(math.gcd(nrows, roff), 256, 8)
    ro = roff // tm

    def body(z_ref, w_ref, c_ref, s_ref, dy_ref, dz_ref, acc_ref):
        i = pl.program_id(0)

        @pl.when(i == 0)
        def _():
            acc_ref[...] = jnp.zeros_like(acc_ref)

        c, s, wv = c_ref[...], s_ref[...], w_ref[...]
        dw = jnp.zeros((1, HEAD_DIM), F32)
        for h in range(nh):
            sl = slice(h * HEAD_DIM, (h + 1) * HEAD_DIM)
            x = z_ref[:, sl]
            d = dy_ref[:, sl]
            dyn = d * c + pltpu.roll(d * s, HEAD_DIM // 2, 1)
            r = lax.rsqrt(jnp.mean(x * x, axis=-1, keepdims=True) + EPS)
            xh = x * r
            dw = dw + jnp.sum(dyn * xh, axis=0, keepdims=True)
            dxh = dyn * wv
            dz_ref[:, sl] = (r * (dxh - xh * jnp.mean(dxh * xh, axis=-1, keepdims=True))).astype(BF16)
        acc_ref[0:1, :] += dw

    return pl.pallas_call(
        body, name=name, grid=(nrows // tm,),
        in_specs=[pl.BlockSpec((tm, W), lambda i: (i + ro, cblk)), pl.BlockSpec((1, HEAD_DIM), lambda i: (0, 0)),
                  pl.BlockSpec((tm, HEAD_DIM), lambda i: (i + ro, 0)), pl.BlockSpec((tm, HEAD_DIM), lambda i: (i + ro, 0)),
                  pl.BlockSpec((tm, W), lambda i: (i, 0))],
        out_specs=[pl.BlockSpec((tm, W), lambda i: (i, 0)), pl.BlockSpec((8, HEAD_DIM), lambda i: (0, 0))],
        out_shape=[jax.ShapeDtypeStruct((nrows, W), BF16), jax.ShapeDtypeStruct((8, HEAD_DIM), F32)],
        compiler_params=_cparams("arbitrary"),
    )(z, w, cos2, sin2, dy)


def _cast_seg(z, cblk, width, name):
    R = z.shape[0]
    tm = _pick(R, 512, 8)

    def body(z_ref, o_ref):
        o_ref[...] = z_ref[...].astype(BF16)

    return pl.pallas_call(
        body, name=name, grid=(R // tm,),
        in_specs=[pl.BlockSpec((tm, width), lambda i: (i, cblk))],
        out_specs=pl.BlockSpec((tm, width), lambda i: (i, 0)),
        out_shape=jax.ShapeDtypeStruct((R, width), BF16), compiler_params=_cparams("parallel"))(z)


NEG_BIG = -1e30


def _attn_specs(T, n_ctx):
    nb = T // WINDOW
    lb = n_ctx // WINDOW
    blk = lambda f: pl.BlockSpec((WINDOW, HEAD_DIM), f)
    win = [blk(lambda h, i: (lb + jnp.maximum(i - 1, 0), h)), blk(lambda h, i: (lb + i, h)),
           blk(lambda h, i: (lb + jnp.minimum(i + 1, nb - 1), h))]
    ctx = pl.BlockSpec((n_ctx, HEAD_DIM), lambda h, i: (0, h))
    qspec = pl.BlockSpec((WINDOW, Q_PER_KV * HEAD_DIM), lambda h, i: (i, h))
    sink = pl.BlockSpec((N_Q_HEADS, HEAD_DIM), lambda h, i: (0, 0))
    return nb, qspec, win, ctx, sink


def _attn_probs(q, kw, kctx, snk, valid):
    scale = HEAD_DIM ** -0.5
    s_lat = jnp.where(valid, _dot(q, kw, NT) * scale, NEG_BIG)
    s_ctx = _dot(q, kctx, NT) * scale
    m = jnp.maximum(jnp.maximum(jnp.max(s_lat, axis=-1, keepdims=True), jnp.max(s_ctx, axis=-1, keepdims=True)), snk)
    p_lat = jnp.exp(s_lat - m)
    p_ctx = jnp.exp(s_ctx - m)
    p_snk = jnp.exp(snk - m)
    den = p_snk + jnp.sum(p_lat, axis=-1, keepdims=True) + jnp.sum(p_ctx, axis=-1, keepdims=True)
    return p_lat, p_ctx, p_snk, den


def _attn_valid(i, T):
    qpos = i * WINDOW + lax.broadcasted_iota(jnp.int32, (WINDOW, 3 * WINDOW), 0)
    kpos = (i - 1) * WINDOW + lax.broadcasted_iota(jnp.int32, (WINDOW, 3 * WINDOW), 1)
    return (jnp.abs(qpos - kpos) <= WINDOW) & (kpos >= 0) & (kpos < T)


def _attn_fwd(qn, kn, vb, sink_rows, n_ctx, name):
    T = qn.shape[0]
    nb, qspec, win, ctx, sink = _attn_specs(T, n_ctx)

    def body(q_ref, kp, kc, kx, vp, vc, vx, kctx_ref, vctx_ref, sink_ref, o_ref):
        h, i = pl.program_id(0), pl.program_id(1)
        kw = jnp.concatenate([kp[...], kc[...], kx[...]], axis=0)
        vw = jnp.concatenate([vp[...], vc[...], vx[...]], axis=0)
        kctx, vctx = kctx_ref[...], vctx_ref[...]
        valid = _attn_valid(i, T)
        for g in range(Q_PER_KV):
            sl = slice(g * HEAD_DIM, (g + 1) * HEAD_DIM)
            snk = sink_ref[pl.ds(h * Q_PER_KV + g, 1), :][:, 0:1]
            p_lat, p_ctx, _, den = _attn_probs(q_ref[:, sl], kw, kctx, snk, valid)
            o = (_dot(p_lat.astype(BF16), vw, NN) + _dot(p_ctx.astype(BF16), vctx, NN)) / den
            o_ref[:, sl] = o.astype(BF16)

    return pl.pallas_call(
        body, name=name, grid=(N_KV_HEADS, nb),
        in_specs=[qspec] + win + win + [ctx, ctx, sink],
        out_specs=qspec, out_shape=jax.ShapeDtypeStruct(qn.shape, BF16),
        compiler_params=_cparams("parallel", "parallel"),
    )(qn, kn, kn, kn, vb, vb, vb, kn, vb, sink_rows)


def _attn_bwd(qn, kn, vb, sink_rows, do, n_ctx, name):
    T = qn.shape[0]
    nb, qspec, win, ctx, sink = _attn_specs(T, n_ctx)
    scale = HEAD_DIM ** -0.5
    TP = T + 2 * WINDOW

    def body(q_ref, kp, kc, kx, vp, vc, vx, kctx_ref, vctx_ref, sink_ref, do_ref,
             dq_ref, dkw_ref, dvw_ref, dkc_ref, dvc_ref, dsn_ref):
        h, i = pl.program_id(0), pl.program_id(1)

        @pl.when(i == 0)
        def _():
            dkw_ref[...] = jnp.zeros_like(dkw_ref)
            dvw_ref[...] = jnp.zeros_like(dvw_ref)
            dkc_ref[...] = jnp.zeros_like(dkc_ref)
            dvc_ref[...] = jnp.zeros_like(dvc_ref)
            dsn_ref[...] = jnp.zeros_like(dsn_ref)

        kw = jnp.concatenate([kp[...], kc[...], kx[...]], axis=0)
        vw = jnp.concatenate([vp[...], vc[...], vx[...]], axis=0)
        kctx, vctx = kctx_ref[...], vctx_ref[...]
        valid = _attn_valid(i, T)
        lane = lax.broadcasted_iota(jnp.int32, (8, HEAD_DIM), 1)
        dkw = jnp.zeros((3 * WINDOW, HEAD_DIM), F32)
        dvw = jnp.zeros((3 * WINDOW, HEAD_DIM), F32)
        dkc = jnp.zeros(kctx.shape, F32)
        dvc = jnp.zeros(kctx.shape, F32)
        dsn = jnp.zeros((8, HEAD_DIM), F32)
        for g in range(Q_PER_KV):
            sl = slice(g * HEAD_DIM, (g + 1) * HEAD_DIM)
            snk = sink_ref[pl.ds(h * Q_PER_KV + g, 1), :][:, 0:1]
            q, d_o = q_ref[:, sl], do_ref[:, sl]
            p_lat, p_ctx, p_snk, den = _attn_probs(q, kw, kctx, snk, valid)
            inv = 1.0 / den
            p_lat, p_ctx, p_snk = p_lat * inv, p_ctx * inv, p_snk * inv
            dp_lat = _dot(d_o, vw, NT)
            dp_ctx = _dot(d_o, vctx, NT)
            dr = jnp.sum(p_lat * dp_lat, axis=-1, keepdims=True) + jnp.sum(p_ctx * dp_ctx, axis=-1, keepdims=True)
            ds_lat = (p_lat * (dp_lat - dr) * scale).astype(BF16)
            ds_ctx = (p_ctx * (dp_ctx - dr) * scale).astype(BF16)
            dq_ref[:, sl] = _dot(ds_lat, kw, NN) + _dot(ds_ctx, kctx, NN)
            dkw = dkw + _dot(ds_lat, q, TN)
            dvw = dvw + _dot(p_lat.astype(BF16), d_o, TN)
            dkc = dkc + _dot(ds_ctx, q, TN)
            dvc = dvc + _dot(p_ctx.astype(BF16), d_o, TN)
            dsn = dsn + jnp.where(lane == g, -jnp.sum(p_snk * dr, axis=0, keepdims=True), 0.0)
        rows = pl.ds(pl.multiple_of(i * WINDOW, WINDOW), 3 * WINDOW)
        dkw_ref[rows, :] += dkw
        dvw_ref[rows, :] += dvw
        dkc_ref[...] += dkc
        dvc_ref[...] += dvc
        dsn_ref[0] += dsn

    wacc = pl.BlockSpec((TP, HEAD_DIM), lambda h, i: (0, h))
    return pl.pallas_call(
        body, name=name, grid=(N_KV_HEADS, nb),
        in_specs=[qspec] + win + win + [ctx, ctx, sink, qspec],
        out_specs=[qspec, wacc, wacc, ctx, ctx, pl.BlockSpec((1, 8, HEAD_DIM), lambda h, i: (h, 0, 0))],
        out_shape=[jax.ShapeDtypeStruct(qn.shape, F32),
                   jax.ShapeDtypeStruct((TP, N_KV_HEADS * HEAD_DIM), F32),
                   jax.ShapeDtypeStruct((TP, N_KV_HEADS * HEAD_DIM), F32),
                   jax.ShapeDtypeStruct((n_ctx, N_KV_HEADS * HEAD_DIM), F32),
                   jax.ShapeDtypeStruct((n_ctx, N_KV_HEADS * HEAD_DIM), F32),
                   jax.ShapeDtypeStruct((N_KV_HEADS, 8, HEAD_DIM), F32)],
        compiler_params=_cparams("arbitrary", "arbitrary"),
    )(qn, kn, kn, kn, vb, vb, vb, kn, vb, sink_rows, do)


def _gla_masks(dirv):
    C = GLA_CHUNK
    r = lax.broadcasted_iota(jnp.int32, (C, C), 0)
    c = lax.broadcasted_iota(jnp.int32, (C, C), 1)
    tt = jnp.where(dirv == 0, r, C - 1 - r)
    ss = jnp.where(dirv == 0, c, C - 1 - c)
    le = (ss <= tt).astype(jnp.int32)
    sums = [le == 1, le == 0]
    blocks = [ss == tt]
    for m in GLA_LEVELS:
        sh = m.bit_length() - 1
        same = (tt >> (sh + 1)) == (ss >> (sh + 1))
        ut = (tt >> sh) & 1
        us = (ss >> sh) & 1
        sums.append(same & (ut == us) & (ut == le))
        blocks.append(same & (ut == 1) & (us == 0))
    mall = jnp.concatenate([jnp.where(s, 1.0, 0.0) for s in sums], axis=0).astype(BF16)
    return mall, blocks


def _split3(x):
    hi = x.astype(BF16)
    r1 = x - hi.astype(F32)
    mid = r1.astype(BF16)
    lo = (r1 - mid.astype(F32)).astype(BF16)
    return hi, mid, lo


def _dot3(m_bf16, x, dims):
    hi, mid, lo = _split3(x)
    return _dot(m_bf16, hi, dims) + _dot(m_bf16, mid, dims) + _dot(m_bf16, lo, dims)


def _gla_chunk_of(dirv, j, lc, nc):
    return jnp.where(dirv == 0, j, jnp.where(j < lc, lc - 1 - j, nc + lc - 1 - j))


def _gla_gate(lr_ref, wg_ref, bg_ref):
    pre = _dot(lr_ref[...].astype(BF16), wg_ref[0].astype(BF16), NN) + bg_ref[0]
    g = (jnp.minimum(pre, 0.0) - jnp.log(1.0 + jnp.exp(-jnp.abs(pre)))) * (1.0 / GLA_GATE_NORM)
    return pre, g


def _gla_fwd(z, qblk, kblk, vblk, lrblk, wg, bg, DV, n_ctx, name):
    R = z.shape[0]
    C = GLA_CHUNK
    DK = wg.shape[2] // GLA_HEADS
    nc, lc = R // C, n_ctx // C
    qscale = DK ** -0.5

    def body(q_ref, k_ref, v_ref, lr_ref, wg_ref, bg_ref, o_ref, sp_ref, st_ref):
        dirv, j = pl.program_id(0), pl.program_id(2)

        @pl.when(j == 0)
        def _():
            st_ref[...] = jnp.zeros_like(st_ref)

        mall, blocks = _gla_masks(dirv)
        q, k, v = q_ref[...] * qscale, k_ref[...], v_ref[...].astype(BF16)
        _, g = _gla_gate(lr_ref, wg_ref, bg_ref)
        E = _dot3(mall, g, NN)
        st = st_ref[...]
        sp_ref[0, 0, 0] = st
        A = jnp.where(blocks[0], _dot(q.astype(BF16), k.astype(BF16), NT), 0.0)
        for l in range(len(GLA_LEVELS)):
            e = jnp.exp(E[(2 + l) * C:(3 + l) * C])
            A = A + jnp.where(blocks[l + 1], _dot((q * e).astype(BF16), (k * e).astype(BF16), NT), 0.0)
        o_ref[0] = _dot((q * jnp.exp(E[0:C])).astype(BF16), st.astype(BF16), NT) + _dot(A.astype(BF16), v, NN)
        decay = jnp.exp(jnp.sum(g, axis=0, keepdims=True))
        st_ref[...] = decay * st + _dot(v, (k * jnp.exp(E[C:2 * C])).astype(BF16), TN)

    chunk = functools.partial(_gla_chunk_of, lc=lc, nc=nc)
    return pl.pallas_call(
        body, name=name, grid=(2, GLA_HEADS, nc),
        in_specs=[pl.BlockSpec((C, DK), lambda d, h, j: (chunk(d, j), qblk + h)),
                  pl.BlockSpec((C, DK), lambda d, h, j: (chunk(d, j), kblk + h)),
                  pl.BlockSpec((C, DV), lambda d, h, j: (chunk(d, j), vblk + h)),
                  pl.BlockSpec((C, LANES), lambda d, h, j: (chunk(d, j), lrblk)),
                  pl.BlockSpec((1, LANES, DK), lambda d, h, j: (d, 0, h)),
                  pl.BlockSpec((1, 1, DK), lambda d, h, j: (d, 0, h))],
        out_specs=[pl.BlockSpec((1, C, DV), lambda d, h, j: (d, chunk(d, j), h)),
                   pl.BlockSpec((1, 1, 1, DV, DK), lambda d, h, j: (d, h, j, 0, 0))],
        out_shape=[jax.ShapeDtypeStruct((2, R, GLA_HEADS * DV), F32),
                   jax.ShapeDtypeStruct((2, GLA_HEADS, nc, DV, DK), F32)],
        scratch_shapes=[pltpu.VMEM((DV, DK), F32)],
        compiler_params=_cparams("parallel", "parallel", "arbitrary"),
    )(z, z, z, z, wg, bg)


def _gla_bwd(z, qblk, kblk, vblk, lrblk, wg, bg, sprev, do, n_ctx, name):
    R = z.shape[0]
    C = GLA_CHUNK
    DK, DV = wg.shape[2] // GLA_HEADS, do.shape[1] // GLA_HEADS
    nc, lc = R // C, n_ctx // C
    qscale = DK ** -0.5
    nl = len(GLA_LEVELS)

    def body(q_ref, k_ref, v_ref, lr_ref, wg_ref, bg_ref, sp_ref, do_ref,
             dq_ref, dk_ref, dv_ref, dpre_ref, dbg_ref, dst_ref):
        dirv, jr = pl.program_id(0), pl.program_id(2)

        @pl.when(jr == 0)
        def _():
            dst_ref[...] = jnp.zeros_like(dst_ref)
            dbg_ref[...] = jnp.zeros_like(dbg_ref)

        mall, blocks = _gla_masks(dirv)
        q, k, v = q_ref[...] * qscale, k_ref[...], v_ref[...].astype(BF16)
        pre, g = _gla_gate(lr_ref, wg_ref, bg_ref)
        E = _dot3(mall, g, NN)
        eb, er = jnp.exp(E[0:C]), jnp.exp(E[C:2 * C])
        decay = jnp.exp(jnp.sum(g, axis=0, keepdims=True))
        st = sp_ref[0, 0, 0]
        dst = dst_ref[...]
        d_o = do_ref[...]
        qe, kd = q * eb, k * er
        qb, kb = q.astype(BF16), k.astype(BF16)
        A = jnp.where(blocks[0], _dot(qb, kb, NT), 0.0)
        for l in range(nl):
            e = jnp.exp(E[(2 + l) * C:(3 + l) * C])
            A = A + jnp.where(blocks[l + 1], _dot((q * e).astype(BF16), (k * e).astype(BF16), NT), 0.0)
        dA = _dot(d_o, v, NT)
        dv_ref[0] = _dot(A.astype(BF16), d_o, TN) + _dot(kd.astype(BF16), dst.astype(BF16), NT)
        dqe = _dot(d_o, st.astype(BF16), NN)
        dkd = _dot(v, dst.astype(BF16), NN)
        G = jnp.where(blocks[0], dA, 0.0).astype(BF16)
        dq = dqe * eb + _dot(G, kb, NN)
        dk = dkd * er + _dot(G, qb, TN)
        dE = [dqe * qe, dkd * kd]
        for l in range(nl):
            e = jnp.exp(E[(2 + l) * C:(3 + l) * C])
            ql, kl = q * e, k * e
            G = jnp.where(blocks[l + 1], dA, 0.0).astype(BF16)
            dql = _dot(G, kl.astype(BF16), NN)
            dkl = _dot(G, ql.astype(BF16), TN)
            dq = dq + dql * e
            dk = dk + dkl * e
            dE.append(dql * ql + dkl * kl)
        dlast = jnp.sum(dst * st, axis=0, keepdims=True) * decay
        dg = _dot3(mall, jnp.concatenate(dE, axis=0), TN) + dlast
        dpre = dg * (1.0 / GLA_GATE_NORM) / (1.0 + jnp.exp(pre))
        dq_ref[0] = dq * qscale
        dk_ref[0] = dk
        dpre_ref[...] = dpre.astype(BF16)
        dbg_ref[0, 0] += jnp.sum(dpre, axis=0, keepdims=True)
        dst_ref[...] = decay * dst + _dot(d_o, qe.astype(BF16), TN)

    def chunk(d, jr):
        return _gla_chunk_of(d, nc - 1 - jr, lc, nc)

    return pl.pallas_call(
        body, name=name, grid=(2, GLA_HEADS, nc),
        in_specs=[pl.BlockSpec((C, DK), lambda d, h, j: (chunk(d, j), qblk + h)),
                  pl.BlockSpec((C, DK), lambda d, h, j: (chunk(d, j), kblk + h)),
                  pl.BlockSpec((C, DV), lambda d, h, j: (chunk(d, j), vblk + h)),
                  pl.BlockSpec((C, LANES), lambda d, h, j: (chunk(d, j), lrblk)),
                  pl.BlockSpec((1, LANES, DK), lambda d, h, j: (d, 0, h)),
                  pl.BlockSpec((1, 1, DK), lambda d, h, j: (d, 0, h)),
                  pl.BlockSpec((1, 1, 1, DV, DK), lambda d, h, j: (d, h, nc - 1 - j, 0, 0)),
                  pl.BlockSpec((C, DV), lambda d, h, j: (chunk(d, j), h))],
        out_specs=[pl.BlockSpec((1, C, DK), lambda d, h, j: (d, chunk(d, j), h)),
                   pl.BlockSpec((1, C, DK), lambda d, h, j: (d, chunk(d, j), h)),
                   pl.BlockSpec((1, C, DV), lambda d, h, j: (d, chunk(d, j), h)),
                   pl.BlockSpec((C, DK), lambda d, h, j: (chunk(d, j), d * GLA_HEADS + h)),
                   pl.BlockSpec((1, 1, 1, DK), lambda d, h, j: (d, h, 0, 0))],
        out_shape=[jax.ShapeDtypeStruct((2, R, GLA_HEADS * DK), F32),
                   jax.ShapeDtypeStruct((2, R, GLA_HEADS * DK), F32),
                   jax.ShapeDtypeStruct((2, R, GLA_HEADS * DV), F32),
                   jax.ShapeDtypeStruct((R, 2 * GLA_HEADS * DK), BF16),
                   jax.ShapeDtypeStruct((2, GLA_HEADS, 1, DK), F32)],
        scratch_shapes=[pltpu.VMEM((DV, DK), F32)],
        compiler_params=_cparams("arbitrary", "arbitrary", "arbitrary"),
    )(z, z, z, z, wg, bg, sprev, do)


def _glanorm_fwd(o, z, rbblk, gn, n_ctx, name):
    _, R, GV = o.shape
    T = R - n_ctx
    DV = GV // GLA_HEADS
    tm = _pick(n_ctx, 256, 8)
    ro = n_ctx // tm

    def body(o0_ref, o1_ref, rb_ref, gn_ref, p_ref):
        gnv = gn_ref[...]
        for h in range(GLA_HEADS):
            sl = slice(h * DV, (h + 1) * DV)
            og = o0_ref[0, :, sl] + o1_ref[0, :, sl]
            r = lax.rsqrt(jnp.mean(og * og, axis=-1, keepdims=True) + EPS)
            p_ref[:, sl] = (og * r * gnv * _silu(rb_ref[:, sl])).astype(BF16)

    return pl.pallas_call(
        body, name=name, grid=(T // tm,),
        in_specs=[pl.BlockSpec((1, tm, GV), lambda i: (0, i + ro, 0)), pl.BlockSpec((1, tm, GV), lambda i: (1, i + ro, 0)),
                  pl.BlockSpec((tm, GV), lambda i: (i + ro, rbblk)), pl.BlockSpec((1, DV), lambda i: (0, 0))],
        out_specs=pl.BlockSpec((tm, GV), lambda i: (i, 0)),
        out_shape=jax.ShapeDtypeStruct((T, GV), BF16), compiler_params=_cparams("parallel"))(o, o, z, gn)


def _glanorm_bwd(o, z, rbblk, gn, dp, n_ctx, name):
    _, R, GV = o.shape
    T = R - n_ctx
    DV = GV // GLA_HEADS
    tm = _pick(n_ctx, 256, 8)
    ro = n_ctx // tm

    def body(o0_ref, o1_ref, rb_ref, gn_ref, dp_ref, do_ref, drb_ref, acc_ref):
        i = pl.program_id(0)

        @pl.when(i == 0)
        def _():
            acc_ref[...] = jnp.zeros_like(acc_ref)

        gnv = gn_ref[...]
        dgn = jnp.zeros((1, DV), F32)
        for h in range(GLA_HEADS):
            sl = slice(h * DV, (h + 1) * DV)
            og = o0_ref[0, :, sl] + o1_ref[0, :, sl]
            rb = rb_ref[:, sl]
            d = dp_ref[:, sl]
            r = lax.rsqrt(jnp.mean(og * og, axis=-1, keepdims=True) + EPS)
            xh = og * r
            drb_ref[:, sl] = (d * xh * gnv * _dsilu(rb)).astype(BF16)
            dn = d * _silu(rb)
            dgn = dgn + jnp.sum(dn * xh, axis=0, keepdims=True)
            dxh = dn * gnv
            do_ref[:, sl] = (r * (dxh - xh * jnp.mean(dxh * xh, axis=-1, keepdims=True))).astype(BF16)
        acc_ref[0:1, :] += dgn

    row = pl.BlockSpec((tm, GV), lambda i: (i, 0))
    return pl.pallas_call(
        body, name=name, grid=(T // tm,),
        in_specs=[pl.BlockSpec((1, tm, GV), lambda i: (0, i + ro, 0)), pl.BlockSpec((1, tm, GV), lambda i: (1, i + ro, 0)),
                  pl.BlockSpec((tm, GV), lambda i: (i + ro, rbblk)), pl.BlockSpec((1, DV), lambda i: (0, 0)), row],
        out_specs=[row, row, pl.BlockSpec((8, DV), lambda i: (0, 0))],
        out_shape=[jax.ShapeDtypeStruct((T, GV), BF16), jax.ShapeDtypeStruct((T, GV), BF16),
                   jax.ShapeDtypeStruct((8, DV), F32)],
        compiler_params=_cparams("arbitrary"))(o, o, z, gn, dp)


def _gate_fwd(z, gablk, gbblk, ya, yg, n_ctx, name):
    T, D = ya.shape
    tm = _pick(n_ctx, 256, 8)
    ro = n_ctx // tm

    def body(ga_ref, gb_ref, ya_ref, yg_ref, m_ref):
        m_ref[...] = (_sigmoid(ga_ref[...]) * ya_ref[...] + _sigmoid(gb_ref[...]) * yg_ref[...]).astype(BF16)

    row = pl.BlockSpec((tm, D), lambda i: (i, 0))
    return pl.pallas_call(
        body, name=name, grid=(T // tm,),
        in_specs=[pl.BlockSpec((tm, D), lambda i: (i + ro, gablk)), pl.BlockSpec((tm, D), lambda i: (i + ro, gbblk)), row, row],
        out_specs=row, out_shape=jax.ShapeDtypeStruct((T, D), BF16), compiler_params=_cparams("parallel"))(z, z, ya, yg)


def _gate_bwd(z, gablk, gbblk, ya, yg, dm, n_ctx, name):
    T, D = ya.shape
    tm = _pick(n_ctx, 256, 8)
    ro = n_ctx // tm

    def body(ga_ref, gb_ref, ya_ref, yg_ref, dm_ref, dya_ref, dyg_ref, dga_ref, dgb_ref):
        d = dm_ref[...]
        sa, sb = _sigmoid(ga_ref[...]), _sigmoid(gb_ref[...])
        dya_ref[...] = (d * sa).astype(BF16)
        dyg_ref[...] = (d * sb).astype(BF16)
        dga_ref[...] = (d * ya_ref[...] * sa * (1.0 - sa)).astype(BF16)
        dgb_ref[...] = (d * yg_ref[...] * sb * (1.0 - sb)).astype(BF16)

    row = pl.BlockSpec((tm, D), lambda i: (i, 0))
    sh = jax.ShapeDtypeStruct((T, D), BF16)
    return pl.pallas_call(
        body, name=name, grid=(T // tm,),
        in_specs=[pl.BlockSpec((tm, D), lambda i: (i + ro, gablk)), pl.BlockSpec((tm, D), lambda i: (i + ro, gbblk)), row, row, row],
        out_specs=[row] * 4, out_shape=[sh] * 4, compiler_params=_cparams("parallel"))(z, z, ya, yg, dm)


def _resnorm_fwd(x, mix, gt, g, sc, sh, name):
    T, D = x.shape
    tm = _pick(T, 256, 8)

    def body(x_ref, mix_ref, gt_ref, g_ref, sc_ref, sh_ref, x1_ref, h_ref):
        x1 = x_ref[...] + gt_ref[...] * mix_ref[...]
        x1_ref[...] = x1
        r = lax.rsqrt(jnp.mean(x1 * x1, axis=-1, keepdims=True) + EPS)
        h_ref[...] = (x1 * r * g_ref[...] * (1.0 + sc_ref[...]) + sh_ref[...]).astype(BF16)

    row = pl.BlockSpec((tm, D), lambda i: (i, 0))
    vec = pl.BlockSpec((1, D), lambda i: (0, 0))
    return pl.pallas_call(
        body, name=name, grid=(T // tm,), in_specs=[row, row, vec, vec, vec, vec], out_specs=[row, row],
        out_shape=[jax.ShapeDtypeStruct((T, D), F32), jax.ShapeDtypeStruct((T, D), BF16)],
        compiler_params=_cparams("parallel"))(x, mix, gt, g, sc, sh)


def _gate_resid_bwd(dx, val, gt, name):
    T, D = dx.shape
    tm = _pick(T, 256, 8)

    def body(dx_ref, val_ref, gt_ref, d_ref, acc_ref):
        i = pl.program_id(0)

        @pl.when(i == 0)
        def _():
            acc_ref[...] = jnp.zeros_like(acc_ref)

        d = dx_ref[...]
        d_ref[...] = (d * gt_ref[...]).astype(BF16)
        acc_ref[0:1, :] += jnp.sum(d * val_ref[...], axis=0, keepdims=True)

    row = pl.BlockSpec((tm, D), lambda i: (i, 0))
    return pl.pallas_call(
        body, name=name, grid=(T // tm,), in_specs=[row, row, pl.BlockSpec((1, D), lambda i: (0, 0))],
        out_specs=[row, pl.BlockSpec((8, D), lambda i: (0, 0))],
        out_shape=[jax.ShapeDtypeStruct((T, D), BF16), jax.ShapeDtypeStruct((8, D), F32)],
        compiler_params=_cparams("arbitrary"))(dx, val, gt)


def _loss_head(d, x1, gt, target, name):
    T, D = d.shape
    tm = _pick(T, 256, 8)

    def body(d_ref, x1_ref, gt_ref, t_ref, dy_ref, acc_ref):
        i = pl.program_id(0)

        @pl.when(i == 0)
        def _():
            acc_ref[...] = jnp.zeros_like(acc_ref)

        e = x1_ref[...] + gt_ref[...] * d_ref[...] - t_ref[...]
        dy_ref[...] = e * (1.0 / D)
        acc_ref[0:1, :] += jnp.sum(e * e, axis=0, keepdims=True)

    row = pl.BlockSpec((tm, D), lambda i: (i, 0))
    return pl.pallas_call(
        body, name=name, grid=(T // tm,), in_specs=[row, row, pl.BlockSpec((1, D), lambda i: (0, 0)), row],
        out_specs=[row, pl.BlockSpec((8, D), lambda i: (0, 0))],
        out_shape=[jax.ShapeDtypeStruct((T, D), F32), jax.ShapeDtypeStruct((8, D), F32)],
        compiler_params=_cparams("arbitrary"))(d, x1, gt, target)


def _halo_specs(T, tm, tw, col_of, order):
    n8 = tm // 8
    if order == "ij":
        mid = lambda i, j: (i, col_of(j))
        prev = lambda i, j: (jnp.maximum(i * n8 - 1, 0), col_of(j))
        nxt = lambda i, j: (jnp.minimum((i + 1) * n8, T // 8 - 1), col_of(j))
    else:
        mid = lambda j, i: (i, col_of(j))
        prev = lambda j, i: (jnp.maximum(i * n8 - 1, 0), col_of(j))
        nxt = lambda j, i: (jnp.minimum((i + 1) * n8, T // 8 - 1), col_of(j))
    return [pl.BlockSpec((tm, tw), mid), pl.BlockSpec((8, tw), prev), pl.BlockSpec((8, tw), nxt)]


def _shifted(u_ref, up_ref, un_ref, i, nt):
    u = u_ref[...]
    tm = u.shape[0]
    row = lax.broadcasted_iota(jnp.int32, u.shape, 0)
    hp = jnp.where(i > 0, up_ref[7:8, :], 0.0)
    hn = jnp.where(i < nt - 1, un_ref[0:1, :], 0.0)
    u_prev = jnp.where(row == 0, hp, pltpu.roll(u, 1, 0))
    u_next = jnp.where(row == tm - 1, hn, pltpu.roll(u, tm - 1, 0))
    return u_prev, u, u_next


def _conv_fwd(u, cw, cb, name):
    T, F2 = u.shape
    F = F2 // 2
    tm, tw = _pick(T, 256, 8), _pick(F, 512)
    nt, nw = T // tm, F // tw

    def body(ua, uap, uan, ug, ugp, ugn, cwa, cwg, cba, cbg, f_ref):
        i = pl.program_id(0)

        def conv(u_ref, up_ref, un_ref, w_ref, b_ref):
            p, m, n = _shifted(u_ref, up_ref, un_ref, i, nt)
            return p * w_ref[0:1, :] + m * w_ref[1:2, :] + n * w_ref[2:3, :] + b_ref[...]

        a = conv(ua, uap, uan, cwa, cba)
        g = conv(ug, ugp, ugn, cwg, cbg)
        f_ref[...] = (_silu(a) * g).astype(BF16)

    wspec = lambda off: pl.BlockSpec((3, tw), lambda i, j: (0, j + off))
    bspec = lambda off: pl.BlockSpec((1, tw), lambda i, j: (0, j + off))
    return pl.pallas_call(
        body, name=name, grid=(nt, nw),
        in_specs=_halo_specs(T, tm, tw, lambda j: j, "ij") + _halo_specs(T, tm, tw, lambda j: j + nw, "ij")
        + [wspec(0), wspec(nw), bspec(0), bspec(nw)],
        out_specs=pl.BlockSpec((tm, tw), lambda i, j: (i, j)),
        out_shape=jax.ShapeDtypeStruct((T, F), BF16), compiler_params=_cparams("parallel", "parallel"),
    )(u, u, u, u, u, u, cw, cw, cb, cb)


def _conv_bwd_a(u, df, cw, cb, name):
    T, F2 = u.shape
    F = F2 // 2
    tm, tw = _pick(T, 256, 8), _pick(F, 512)
    nt, nw = T // tm, F // tw

    def body(ua, uap, uan, ug, ugp, ugn, cwa, cwg, cba, cbg, df_ref, da_ref, dg_ref, acca_ref, accg_ref):
        i = pl.program_id(1)

        @pl.when(i == 0)
        def _():
            acca_ref[...] = jnp.zeros_like(acca_ref)
            accg_ref[...] = jnp.zeros_like(accg_ref)

        sa = _shifted(ua, uap, uan, i, nt)
        sg = _shifted(ug, ugp, ugn, i, nt)
        a = sa[0] * cwa[0:1, :] + sa[1] * cwa[1:2, :] + sa[2] * cwa[2:3, :] + cba[...]
        g = sg[0] * cwg[0:1, :] + sg[1] * cwg[1:2, :] + sg[2] * cwg[2:3, :] + cbg[...]
        d = df_ref[...]
        da = d * g * _dsilu(a)
        dg = d * _silu(a)
        da_ref[...] = da
        dg_ref[...] = dg
        for t in range(3):
            acca_ref[t:t + 1, :] += jnp.sum(da * sa[t], axis=0, keepdims=True)
            accg_ref[t:t + 1, :] += jnp.sum(dg * sg[t], axis=0, keepdims=True)
        acca_ref[3:4, :] += jnp.sum(da, axis=0, keepdims=True)
        accg_ref[3:4, :] += jnp.sum(dg, axis=0, keepdims=True)

    wspec = lambda off: pl.BlockSpec((3, tw), lambda j, i: (0, j + off))
    bspec = lambda off: pl.BlockSpec((1, tw), lambda j, i: (0, j + off))
    row = pl.BlockSpec((tm, tw), lambda j, i: (i, j))
    acc = pl.BlockSpec((8, tw), lambda j, i: (0, j))
    return pl.pallas_call(
        body, name=name, grid=(nw, nt),
        in_specs=_halo_specs(T, tm, tw, lambda j: j, "ji") + _halo_specs(T, tm, tw, lambda j: j + nw, "ji")
        + [wspec(0), wspec(nw), bspec(0), bspec(nw), row],
        out_specs=[row, row, acc, acc],
        out_shape=[jax.ShapeDtypeStruct((T, F), F32), jax.ShapeDtypeStruct((T, F), F32),
                   jax.ShapeDtypeStruct((8, F), F32), jax.ShapeDtypeStruct((8, F), F32)],
        compiler_params=_cparams("parallel", "arbitrary"),
    )(u, u, u, u, u, u, cw, cw, cb, cb, df)


def _conv_bwd_b(dc, cw, woff, name):
    T, F = dc.shape
    tm, tw = _pick(T, 256, 8), _pick(F, 512)
    nt, nw = T // tm, F // tw

    def body(d_ref, dp_ref, dn_ref, w_ref, o_ref):
        i = pl.program_id(0)
        p, m, n = _shifted(d_ref, dp_ref, dn_ref, i, nt)
        o_ref[...] = (n * w_ref[0:1, :] + m * w_ref[1:2, :] + p * w_ref[2:3, :]).astype(BF16)

    return pl.pallas_call(
        body, name=name, grid=(nt, nw),
        in_specs=_halo_specs(T, tm, tw, lambda j: j, "ij") + [pl.BlockSpec((3, tw), lambda i, j: (0, j + woff * nw))],
        out_specs=pl.BlockSpec((tm, tw), lambda i, j: (i, j)),
        out_shape=jax.ShapeDtypeStruct((T, F), BF16), compiler_params=_cparams("parallel", "parallel"),
    )(dc, dc, dc, cw)


def _assemble_dz(lay, Z, n_ctx, dqa, drb, dga, dgb, dka, dva, dvg, dqg, dkg, dlr, name):
    T = dqa.shape[0]
    R = T + n_ctx
    tm = _pick(n_ctx, 128, 8)
    cb = n_ctx // tm

    def body(dqa_ref, drb_ref, dga_ref, dgb_ref, dka_ref, dva_ref, dvg0, dvg1, dqg0, dqg1, dkg0, dkg1, dlr_ref, o_ref):
        lat = pl.program_id(0) >= cb

        def put(seg, val):
            o_ref[:, lay[seg]:lay[seg] + val.shape[1]] = val.astype(BF16)

        def lat_only(ref):
            v = ref[...]
            return jnp.where(lat, v, jnp.zeros_like(v))

        put("qa", lat_only(dqa_ref))
        put("rb", lat_only(drb_ref))
        put("ga", lat_only(dga_ref))
        put("gb", lat_only(dgb_ref))
        put("ka", dka_ref[...])
        put("va", dva_ref[...])
        put("vb", dvg0[0] + dvg1[0])
        put("qb", dqg0[0] + dqg1[0])
        put("kb", dkg0[0] + dkg1[0])
        put("lr", dlr_ref[...])

    lat_spec = lambda a: pl.BlockSpec((tm, a.shape[1]), lambda i: (jnp.maximum(i - cb, 0), 0))
    all_spec = lambda a: pl.BlockSpec((tm, a.shape[1]), lambda i: (i, 0))
    dir_specs = lambda a: [pl.BlockSpec((1, tm, a.shape[2]), lambda i: (0, i, 0)),
                           pl.BlockSpec((1, tm, a.shape[2]), lambda i: (1, i, 0))]
    return pl.pallas_call(
        body, name=name, grid=(R // tm,),
        in_specs=[lat_spec(dqa), lat_spec(drb), lat_spec(dga), lat_spec(dgb), all_spec(dka), all_spec(dva)]
        + dir_specs(dvg) + dir_specs(dqg) + dir_specs(dkg) + [all_spec(dlr)],
        out_specs=pl.BlockSpec((tm, Z), lambda i: (i, 0)),
        out_shape=jax.ShapeDtypeStruct((R, Z), BF16), compiler_params=_cparams("parallel"),
    )(dqa, drb, dga, dgb, dka, dva, dvg, dvg, dqg, dqg, dkg, dkg, dlr)


def _mod_fwd(ca, w, b, name):
    n, D = ca.shape
    N = w.shape[1]
    tn = _pick(N, 512)

    def body(c_ref, w_ref, b_ref, o_ref, s_ref):
        s = _silu(c_ref[...])
        s_ref[...] = s
        o_ref[...] = _dot(s.astype(BF16), w_ref[...].astype(BF16), NN) + b_ref[...]

    return pl.pallas_call(
        body, name=name, grid=(N // tn,),
        in_specs=[pl.BlockSpec((n, D), lambda j: (0, 0)), pl.BlockSpec((D, tn), lambda j: (0, j)),
                  pl.BlockSpec((1, tn), lambda j: (0, j))],
        out_specs=[pl.BlockSpec((n, tn), lambda j: (0, j)), pl.BlockSpec((n, D), lambda j: (0, 0))],
        out_shape=[jax.ShapeDtypeStruct((n, N), F32), jax.ShapeDtypeStruct((n, D), F32)],
        compiler_params=_cparams("arbitrary"))(ca, w, b)


def _silu_bwd(dsil, ca, name):
    def body(d_ref, c_ref, o_ref):
        o_ref[...] = d_ref[...] * _dsilu(c_ref[...])

    return pl.pallas_call(body, name=name, out_shape=jax.ShapeDtypeStruct(ca.shape, F32))(dsil, ca)


def _adam_math(w, g, m, v):
    c1 = 1.0 - ADAM_B1 ** ADAM_STEP
    c2 = 1.0 - ADAM_B2 ** ADAM_STEP
    mn = ADAM_B1 * m + (1.0 - ADAM_B1) * g
    vn = ADAM_B2 * v + (1.0 - ADAM_B2) * (g * g)
    return -ADAM_LR * ((mn / c1) / (jnp.sqrt(vn / c2) + ADAM_EPS) + ADAM_WD * w), mn, vn


def _adamw(w, g, m, v, name):
    Rw, Cw = w.shape
    tr = _pick(Rw, 128, 8)

    def body(w_ref, g_ref, m_ref, v_ref, d_ref, mo_ref, vo_ref):
        d_ref[...], mo_ref[...], vo_ref[...] = _adam_math(w_ref[...], g_ref[...], m_ref[...], v_ref[...])

    row = pl.BlockSpec((tr, Cw), lambda i: (i, 0))
    sh = jax.ShapeDtypeStruct((Rw, Cw), F32)
    return pl.pallas_call(body, name=name, grid=(Rw // tr,), in_specs=[row] * 4, out_specs=[row] * 3,
                          out_shape=[sh] * 3, compiler_params=_cparams("parallel"))(w, g, m, v)


HBM_SPEC = pl.BlockSpec(memory_space=pltpu.HBM)


def _exchange(inputs, out_shapes, stages, name):
    n_in, n_out = len(inputs), len(out_shapes)
    n = sum(len(s) for s in stages)

    def body(*refs):
        ins, outs = refs[:n_in], refs[n_in:n_in + n_out]
        send_sems, recv_sems = refs[n_in + n_out:]
        me = (lax.axis_index("x"), lax.axis_index("y"), lax.axis_index("c"))
        k = 0
        for stage in stages:
            copies = []
            for (skind, sidx), sfn, didx, dfn, flip in stage:
                src = (ins if skind == "in" else outs)[sidx].at[sfn(*me)]
                dst = outs[didx].at[dfn(*me)]
                if flip == (0, 0, 0):
                    cp = pltpu.make_async_copy(src, dst, send_sems.at[k])
                else:
                    peer = tuple(1 - a if f else a for a, f in zip(me, flip))
                    cp = pltpu.make_async_remote_copy(src, dst, send_sems.at[k], recv_sems.at[k],
                                                      device_id=peer, device_id_type=MESH)
                cp.start()
                copies.append(cp)
                k += 1
            for cp in copies:
                cp.wait()

    return pl.pallas_call(
        body, name=name, in_specs=[HBM_SPEC] * n_in, out_specs=[HBM_SPEC] * n_out, out_shape=out_shapes,
        scratch_shapes=[pltpu.SemaphoreType.DMA((n,)), pltpu.SemaphoreType.DMA((n,))],
    )(*inputs)


FLIPS_ALL = [(0, 0, 1), (0, 1, 0), (0, 1, 1), (1, 0, 0), (1, 0, 1), (1, 1, 0), (1, 1, 1)]
FLIPS_CHIP = [(0, 1, 0), (1, 0, 0), (1, 1, 0)]


def _sum_slots(buf, name):
    n, r, w = buf.shape
    tr = _pick(r, 256, 8)

    def body(b_ref, o_ref):
        acc = b_ref[0]
        for s in range(1, n):
            acc = acc + b_ref[s]
        o_ref[...] = acc

    return pl.pallas_call(
        body, name=name, grid=(r // tr,), in_specs=[pl.BlockSpec((n, tr, w), lambda i: (0, i, 0))],
        out_specs=pl.BlockSpec((tr, w), lambda i: (i, 0)), out_shape=jax.ShapeDtypeStruct((r, w), F32),
        compiler_params=_cparams("parallel"))(buf)


def _allreduce(buf, name):
    r, w = buf.shape
    whole = lambda x, y, c: (slice(None), slice(None))
    slot = lambda x, y, c: (4 * x + 2 * y + c,)
    stage = [(("in", 0), whole, 0, slot, f) for f in [(0, 0, 0)] + FLIPS_ALL]
    (slots,) = _exchange([buf], [jax.ShapeDtypeStruct((8, r, w), F32)], [stage], name + "_x")
    return _sum_slots(slots, name + "_sum")


def _allgather_weights(shards, name):
    half = lambda a, c: pl.ds(c * (a.shape[0] // 2), a.shape[0] // 2)
    first, second = [], []
    for n, a in enumerate(shards):
        for f in FLIPS_CHIP:
            first.append((("in", n), lambda x, y, c, a=a: (half(a, c), slice(None)), n,
                          lambda x, y, c, a=a: (2 * x + y, half(a, c), slice(None)), f))
            peer_slot = lambda x, y, c, a=a, f=f: (2 * (x ^ f[0]) + (y ^ f[1]), half(a, c), slice(None))
            second.append((("out", n), peer_slot, n, peer_slot, (0, 0, 1)))
    outs = [jax.ShapeDtypeStruct((4,) + a.shape, a.dtype) for a in shards]
    return _exchange(shards, outs, [first, second], name)


def _add_pair(G, bufA, cvec, name):
    _, Rs, Cs = G.shape
    Rh = Rs // 2
    tr = _pick(Rh, 128, 16)
    nb = Rh // tr

    def body(c_ref, g_ref, a_ref, o_ref):
        o_ref[...] = (g_ref[...] + a_ref[...]).astype(BF16)

    grid_spec = pltpu.PrefetchScalarGridSpec(
        num_scalar_prefetch=1, grid=(4, nb),
        in_specs=[pl.BlockSpec((1, tr, Cs), lambda s, i, c_ref: (s, c_ref[0] * nb + i, 0)),
                  pl.BlockSpec((1, tr, Cs), lambda s, i, c_ref: (s, i, 0))],
        out_specs=pl.BlockSpec((1, tr, Cs), lambda s, i, c_ref: (s, i, 0)))
    return pl.pallas_call(body, name=name, grid_spec=grid_spec, out_shape=jax.ShapeDtypeStruct((4, Rh, Cs), BF16),
                          compiler_params=_cparams("parallel", "parallel"))(cvec, G, bufA)


def _sum_chips(G, bufA, bufB, cvec, svec, name):
    _, Rs, Cs = G.shape
    Rh = Rs // 2
    tr = _pick(Rh, 128, 16)
    nb = Rh // tr

    def body(c_ref, s_ref, g_ref, a_ref, b_ref, o_ref):
        o_ref[...] = (g_ref[0] + a_ref[0]) + b_ref[0].astype(F32) + b_ref[1].astype(F32) + b_ref[2].astype(F32)

    grid_spec = pltpu.PrefetchScalarGridSpec(
        num_scalar_prefetch=2, grid=(nb,),
        in_specs=[pl.BlockSpec((1, tr, Cs), lambda i, c, s: (s[0], c[0] * nb + i, 0)),
                  pl.BlockSpec((1, tr, Cs), lambda i, c, s: (s[0], i, 0)),
                  pl.BlockSpec((3, tr, Cs), lambda i, c, s: (0, i, 0))],
        out_specs=pl.BlockSpec((tr, Cs), lambda i, c, s: (i, 0)))
    return pl.pallas_call(body, name=name, grid_spec=grid_spec, out_shape=jax.ShapeDtypeStruct((Rh, Cs), F32),
                          compiler_params=_cparams("parallel"))(cvec, svec, G, bufA, bufB)


def _reduce_scatter(grads, cvec, svec, name):
    ng = len(grads)
    Rh = [g.shape[1] // 2 for g in grads]
    whole3 = lambda x, y, c: (slice(None), slice(None), slice(None))
    whole2 = lambda x, y, c: (slice(None), slice(None))
    st = [(("in", n), lambda x, y, c, n=n: (slice(None), pl.ds((1 - c) * Rh[n], Rh[n]), slice(None)), n,
           whole3, (0, 0, 1)) for n in range(ng)]
    bufA = _exchange(grads, [jax.ShapeDtypeStruct((4, Rh[n], g.shape[2]), F32) for n, g in enumerate(grads)],
                     [st], name + "_pair")
    P = [_add_pair(g, a, cvec, "%s_add%d" % (name, n)) for n, (g, a) in enumerate(zip(grads, bufA))]
    st = [(("in", n), lambda x, y, c, f=f: (2 * (x ^ f[0]) + (y ^ f[1]),), n, lambda x, y, c, k=k: (k,), f)
          for n in range(ng) for k, f in enumerate(FLIPS_CHIP)]
    bufB = _exchange(P, [jax.ShapeDtypeStruct((3,) + p.shape[1:], BF16) for p in P], [st], name + "_chips")
    mine = [_sum_chips(g, a, b, cvec, svec, "%s_sum%d" % (name, n)) for n, (g, a, b) in enumerate(zip(grads, bufA, bufB))]
    st = [(("in", n), whole2, n, whole2, (0, 0, 1)) for n in range(ng)]
    other = _exchange(mine, [jax.ShapeDtypeStruct(r.shape, F32) for r in mine], [st], name + "_halves")
    return mine, other


def _adamw_halves(w, mine, other, m, v, cvec, name):
    Rs, Cs = w.shape
    Rh = Rs // 2
    tr = _pick(Rh, 128, 8)
    nb = Rh // tr

    def body(c_ref, w_ref, a_ref, b_ref, m_ref, v_ref, g_ref, d_ref, mo_ref, vo_ref):
        gv = jnp.where(pl.program_id(0) // nb == c_ref[0], a_ref[...], b_ref[...])
        g_ref[...] = gv
        d_ref[...], mo_ref[...], vo_ref[...] = _adam_math(w_ref[...], gv, m_ref[...], v_ref[...])

    row = pl.BlockSpec((tr, Cs), lambda i, c: (i, 0))
    hrow = pl.BlockSpec((tr, Cs), lambda i, c: (i % nb, 0))
    grid_spec = pltpu.PrefetchScalarGridSpec(num_scalar_prefetch=1, grid=(2 * nb,),
                                             in_specs=[row, hrow, hrow, row, row], out_specs=[row] * 4)
    return pl.pallas_call(body, name=name, grid_spec=grid_spec, out_shape=[jax.ShapeDtypeStruct((Rs, Cs), F32)] * 4,
                          compiler_params=_cparams("parallel"))(cvec, w, mine, other, m, v)


def _pack(arrays):
    flat = [a.reshape(-1).astype(F32) for a in arrays]
    meta, off = [], 0
    for a, f in zip(arrays, flat):
        meta.append((off, a.shape))
        off += f.shape[0]
    total = -(-off // (8 * LANES)) * (8 * LANES)
    flat.append(jnp.zeros((total - off,), F32))
    return jnp.concatenate(flat).reshape(total // LANES, LANES), meta


def _unpack(buf, meta):
    flat = buf.reshape(-1)
    out = []
    for off, shape in meta:
        size = 1
        for s in shape:
            size *= s
        out.append(flat[off:off + size].reshape(shape))
    return out


WEIGHT_NAMES = ["c_ctx", "w_mod", "b_mod", "g_mix", "w_in", "q_norm", "k_norm", "attn_sink", "w_gate_f", "b_gate_f",
                "w_gate_b", "b_gate_b", "gla_norm", "w_attn_o", "w_gla_o", "w_out", "g_ffn", "w_up", "conv_w",
                "conv_b", "w_down"]
BIG_NAMES = ["w_in", "w_attn_o", "w_gla_o", "w_out", "w_up", "w_down"]
SHARDED_SMALL = ["w_gate_f", "w_gate_b", "conv_w"]


def _layouts(D):
    aw, kvw, gk, gv = N_Q_HEADS * HEAD_DIM, N_KV_HEADS * HEAD_DIM, D // 2, D
    widths = {"qa": aw, "ka": kvw, "va": kvw, "qb": gk, "kb": gk, "vb": gv, "rb": gv, "lr": 2 * GLA_LOWRANK,
              "ga": D, "gb": D}
    orig, off = {}, 0
    for s in ["qa", "ka", "va", "qb", "kb", "vb", "rb", "lr", "ga", "gb"]:
        orig[s] = off
        off += widths[s]
    order = ["qa", "vb", "rb", "ga", "gb", "ka", "va", "qb", "kb", "lr"]
    lay, off = {}, 0
    for s in order:
        lay[s] = off
        off += LANES if s == "lr" else widths[s]
    align = {"qa": aw, "vb": D, "rb": D, "ga": D, "gb": D, "ka": kvw, "va": kvw, "qb": gk // GLA_HEADS,
             "kb": gk // GLA_HEADS, "lr": LANES}
    for s in order:
        assert lay[s] % align[s] == 0, (s, lay[s], align[s])
    return widths, orig, order, lay, off


def _rope_tables(T, L):
    t = jnp.arange(T)
    nf = HEAD_DIM // 4
    inv = ROPE_THETA ** (-jnp.arange(nf, dtype=F32) / nf)
    ang = jnp.concatenate([(t // GRID_W)[:, None] * inv, (t % GRID_W)[:, None] * inv], axis=-1)
    cos, sin = jnp.cos(ang), jnp.sin(ang)
    cos2 = jnp.concatenate([jnp.ones((L, HEAD_DIM), F32), jnp.concatenate([cos, cos], axis=-1)], axis=0)
    sin2 = jnp.concatenate([jnp.zeros((L, HEAD_DIM), F32), jnp.concatenate([-sin, sin], axis=-1)], axis=0)
    return cos2, sin2


def _step(x, c, ctx, loss_target, W, M, V):
    xi, yi, ci = lax.axis_index("x"), lax.axis_index("y"), lax.axis_index("c")
    chip = 2 * xi + yi
    dev = 2 * chip + ci
    south = (ci == 0).astype(F32)
    cvec = ci.reshape(1).astype(jnp.int32)
    T, D = x.shape[1], x.shape[2]
    L = ctx.shape[1]
    R = L + T
    F = 4 * W["w_down"].shape[1]
    GK, GV = D // 2, D
    DK, DV = GK // GLA_HEADS, GV // GLA_HEADS
    N6 = 6 * D
    N4 = N6 // 4
    widths, orig, order, lay, Z = _layouts(D)

    def place_cols(shard, full_cols):
        cols = shard.shape[-1]
        full = jnp.zeros(shard.shape[:-1] + (full_cols,), F32)
        return lax.dynamic_update_slice(full, shard * south, (0,) * (shard.ndim - 1) + (chip * cols,))

    c_rows = lax.dynamic_update_slice(jnp.zeros((8, D), F32), c, (dev, 0))
    bufa, meta = _pack([c_rows, place_cols(W["w_gate_f"][0], GK), place_cols(W["w_gate_b"][0], GK),
                        place_cols(W["conv_w"][0], 2 * F)])
    c_all, wgf, wgb, cw = _unpack(_allreduce(bufa, "gather_small"), meta)
    ca = jnp.concatenate([c_all, W["c_ctx"][None, :], jnp.zeros((7, D), F32)], axis=0)
    b_shard = lax.dynamic_slice(W["b_mod"], (0, chip * N4), (1, N4))
    mod_part, sil = _mod_fwd(ca, W["w_mod"][0], b_shard, "mod_fwd")
    slots = lax.dynamic_update_slice(jnp.zeros((4, 16, N4), F32), (mod_part * south)[None], (chip, 0, 0))
    mod_all = _allreduce(slots.reshape(64, N4), "gather_mod").reshape(4, 16, N4).transpose(1, 0, 2).reshape(16, N6)
    mx = lax.dynamic_slice(mod_all, (dev, 0), (1, N6)).reshape(6, 1, D)
    mc = mod_all[8].reshape(6, 1, D)

    shards = [W[n][0].astype(BF16) for n in BIG_NAMES]
    g_in, g_ao, g_go, g_out, g_up, g_dn = [
        lax.dynamic_update_slice(g, s[None], (chip, 0, 0))
        for g, s in zip(_allgather_weights(shards, "gather_weights"), shards)]
    cols = lambda g: g.transpose(1, 0, 2).reshape(g.shape[1], 4 * g.shape[2])
    rows = lambda g: g.reshape(4 * g.shape[1], g.shape[2])
    w_in_f = cols(g_in)
    seg = lambda s: w_in_f[:, orig[s]:orig[s] + widths[s]]
    w_cat = jnp.concatenate([jnp.pad(seg(s), ((0, 0), (0, LANES - widths[s]))) if s == "lr" else seg(s)
                             for s in order], axis=1)
    w_ao, w_go, w_out, w_up, w_dn = rows(g_ao), rows(g_go), rows(g_out), cols(g_up), rows(g_dn)
    wg = jnp.zeros((2, LANES, GK), F32).at[0, :GLA_LOWRANK].set(wgf).at[1, GLA_LOWRANK:2 * GLA_LOWRANK].set(wgb)
    bg = jnp.stack([W["b_gate_f"], W["b_gate_b"]])
    cb = W["conv_b"]
    sink_rows = jnp.broadcast_to(W["attn_sink"][0][:, None], (N_Q_HEADS, HEAD_DIM))
    cos2, sin2 = _rope_tables(T, L)
    blk = lambda s, w: lay[s] // w

    xall = jnp.concatenate([ctx[0], x[0]], axis=0)
    sc1 = jnp.stack([mc[1], mx[1]])
    sh1 = jnp.stack([mc[0], mx[0]])
    h = _modnorm_fwd(xall, W["g_mix"], sc1, sh1, L, "modnorm1")
    z = _matmul(h, w_cat, "nn", F32, "proj_in")
    qn = _qknorm_fwd(z, blk("qa", widths["qa"]), T, L, W["q_norm"], cos2, sin2, N_Q_HEADS, "qnorm")
    kn = _qknorm_fwd(z, blk("ka", widths["ka"]), R, 0, W["k_norm"], cos2, sin2, N_KV_HEADS, "knorm")
    vb = _cast_seg(z, blk("va", widths["va"]), widths["va"], "vcast")
    o_attn = _attn_fwd(qn, kn, vb, sink_rows, L, "attn_fwd")
    gla_blks = (blk("qb", DK), blk("kb", DK), blk("vb", DV), blk("lr", LANES))
    o_g, sprev = _gla_fwd(z, *gla_blks, wg, bg, DV, L, "gla_fwd")
    p = _glanorm_fwd(o_g, z, blk("rb", D), W["gla_norm"], L, "glanorm")
    ya = _matmul(o_attn, w_ao, "nn", F32, "proj_attn_o")
    yg = _matmul(p, w_go, "nn", F32, "proj_gla_o")
    m = _gate_fwd(z, blk("ga", D), blk("gb", D), ya, yg, L, "gate")
    mix = _matmul(m, w_out, "nn", F32, "proj_out")
    x1, h2 = _resnorm_fwd(x[0], mix, mx[2], W["g_ffn"], mx[4], mx[3], "resnorm2")
    u = _matmul(h2, w_up, "nn", F32, "ffn_up")
    f = _conv_fwd(u, cw, cb, "conv_swiglu")
    d = _matmul(f, w_dn, "nn", F32, "ffn_down")
    dy, lacc = _loss_head(d, x1, mx[5], loss_target[0], "loss_head")
    loss = lax.psum((0.5 / D) * jnp.sum(lacc[0]), ("x", "y", "c"))

    dd, s_gt2 = _gate_resid_bwd(dy, d, mx[5], "gate2_bwd")
    gw_dn = _matmul(f, dd, "tn", F32, "ffn_down_dw")
    df = _matmul(dd, w_dn, "nt", F32, "ffn_down_dx")
    dca, dcg, acca, accg = _conv_bwd_a(u, df, cw, cb, "conv_swiglu_bwd")
    du = jnp.concatenate([_conv_bwd_b(dca, cw, 0, "conv_t_a"), _conv_bwd_b(dcg, cw, 1, "conv_t_g")], axis=1)
    gw_up = _matmul(h2, du, "tn", F32, "ffn_up_dw")
    dh2 = _matmul(du, w_up, "nt", F32, "ffn_up_dx")
    dx1, s2 = _modnorm_bwd(x1, dh2, W["g_ffn"], mx[4], dy, "resnorm2_bwd")
    dmix, s_gt1 = _gate_resid_bwd(dx1, mix, mx[2], "gate1_bwd")
    gw_out = _matmul(m, dmix, "tn", F32, "proj_out_dw")
    dm = _matmul(dmix, w_out, "nt", F32, "proj_out_dx")
    dya, dyg, dga, dgb = _gate_bwd(z, blk("ga", D), blk("gb", D), ya, yg, dm, L, "gate_bwd")
    gw_ao = _matmul(o_attn, dya, "tn", F32, "proj_attn_o_dw")
    do_attn = _matmul(dya, w_ao, "nt", BF16, "proj_attn_o_dx")
    gw_go = _matmul(p, dyg, "tn", F32, "proj_gla_o_dw")
    dp = _matmul(dyg, w_go, "nt", F32, "proj_gla_o_dx")
    do_gla, drb, s_gn = _glanorm_bwd(o_g, z, blk("rb", D), W["gla_norm"], dp, L, "glanorm_bwd")
    do_pad = jnp.concatenate([jnp.zeros((L, GV), BF16), do_gla], axis=0)
    dqg, dkg, dvg, dpre, dbg = _gla_bwd(z, *gla_blks, wg, bg, sprev, do_pad, L, "gla_bwd")
    wg_cat = jnp.concatenate([wg[0], wg[1]], axis=1)
    dlr = _matmul(dpre, wg_cat, "nt", BF16, "gla_gate_dx")
    dwg = _matmul(z[:, lay["lr"]:lay["lr"] + LANES], dpre, "tn", F32, "gla_gate_dw")
    dqn, dkw, dvw, dkc, dvc, dsn = _attn_bwd(qn, kn, vb, sink_rows, do_attn, L, "attn_bwd")
    dqa, s_qn = _qknorm_bwd(z, blk("qa", widths["qa"]), T, L, W["q_norm"], cos2, sin2, dqn, N_Q_HEADS, "qnorm_bwd")
    dk_all = jnp.concatenate([dkc, dkw[WINDOW:WINDOW + T]], axis=0)
    dv_all = jnp.concatenate([dvc, dvw[WINDOW:WINDOW + T]], axis=0)
    dka, s_kn = _qknorm_bwd(z, blk("ka", widths["ka"]), R, 0, W["k_norm"], cos2, sin2, dk_all, N_KV_HEADS, "knorm_bwd")
    dz = _assemble_dz(lay, Z, L, dqa, drb, dga, dgb, dka, dv_all, dvg, dqg, dkg, dlr, "assemble_dz")
    gw_cat = _matmul(h, dz, "tn", F32, "proj_in_dw")
    dh = _matmul(dz, w_cat, "nt", F32, "proj_in_dx")
    grad_x, s1 = _modnorm_bwd(x[0], dh[L:], W["g_mix"], mx[1], dx1, "modnorm1_bwd")
    _, s1c = _modnorm_bwd(ctx[0], dh[:L], W["g_mix"], mc[1], None, "modnorm1_ctx_bwd")

    dmod_x = jnp.concatenate([s1[0], s1[1], s_gt1[0], s2[0], s2[1], s_gt2[0]])
    dmod_c = jnp.concatenate([s1c[0], s1c[1], jnp.zeros((4 * D,), F32)])
    dmod_rows = lax.dynamic_update_slice(jnp.zeros((9, N6), F32).at[8].set(dmod_c), dmod_x[None], (dev, 0))
    small = [dmod_rows, dmod_x + dmod_c, s1[2] + s1c[2], s_qn[0], s_kn[0], dsn[:, 0, :Q_PER_KV].reshape(N_Q_HEADS),
             dwg[:GLA_LOWRANK, :GK], dbg[0].reshape(GK), dwg[GLA_LOWRANK:2 * GLA_LOWRANK, GK:], dbg[1].reshape(GK),
             s_gn[0], s2[2], jnp.concatenate([acca[0:3], accg[0:3]], axis=1), jnp.concatenate([acca[3], accg[3]])]
    bufc, meta = _pack(small)
    (dmod_sum, g_b_mod, g_g_mix, g_q_norm, g_k_norm, g_sink, g_wgf, g_bgf, g_wgb, g_bgb, g_gla_norm, g_g_ffn,
     g_conv_w, g_conv_b) = _unpack(_allreduce(bufc, "reduce_small"), meta)
    dmod16 = lax.dynamic_slice(jnp.concatenate([dmod_sum, jnp.zeros((7, N6), F32)], axis=0), (0, chip * N4), (16, N4))
    g_w_mod = _matmul(sil, dmod16, "tn", F32, "mod_dw")
    dsil = _matmul(dmod16, W["w_mod"][0], "nt", F32, "mod_dx")
    g_c_ctx = _silu_bwd(_allreduce(dsil * south, "reduce_cctx"), ca, "silu_bwd")[8]

    gw_in = jnp.concatenate([gw_cat[:, lay[s]:lay[s] + widths[s]] for s in ["qa", "ka", "va", "qb", "kb", "vb", "rb",
                                                                           "lr", "ga", "gb"]], axis=1)
    by_cols = lambda g: g.reshape(g.shape[0], 4, g.shape[1] // 4).transpose(1, 0, 2)
    by_rows = lambda g: g.reshape(4, g.shape[0] // 4, g.shape[1])
    svec = chip.reshape(1).astype(jnp.int32)
    mine, other = _reduce_scatter([by_cols(gw_in), by_rows(gw_ao), by_rows(gw_go), by_rows(gw_out), by_cols(gw_up),
                                   by_rows(gw_dn)], cvec, svec, "reduce_big")
    cut = lambda g: lax.dynamic_slice(g, (0, chip * (g.shape[1] // 4)), (g.shape[0], g.shape[1] // 4))
    grads = {"c_ctx": g_c_ctx, "w_mod": g_w_mod[None], "b_mod": g_b_mod[None], "g_mix": g_g_mix[None],
             "q_norm": g_q_norm[None], "k_norm": g_k_norm[None], "attn_sink": g_sink[None],
             "w_gate_f": cut(g_wgf)[None], "b_gate_f": g_bgf[None], "w_gate_b": cut(g_wgb)[None],
             "b_gate_b": g_bgb[None], "gla_norm": g_gla_norm[None], "g_ffn": g_g_ffn[None],
             "conv_w": cut(g_conv_w)[None], "conv_b": g_conv_b[None]}

    delta, new_m, new_v = {}, {}, {}
    dl, mn, vn = _adamw(W["w_mod"][0], g_w_mod, M["w_mod"][0], V["w_mod"][0], "adamw_w_mod")
    delta["w_mod"], new_m["w_mod"], new_v["w_mod"] = dl[None], mn[None], vn[None]
    for n, a, b in zip(BIG_NAMES, mine, other):
        g, dl, mn, vn = _adamw_halves(W[n][0], a, b, M[n][0], V[n][0], cvec, "adamw_" + n)
        grads[n], delta[n], new_m[n], new_v[n] = g[None], dl[None], mn[None], vn[None]
    small_names = [n for n in WEIGHT_NAMES if n not in delta]
    packs = [_pack([src[n] for n in small_names]) for src in (W, grads, M, V)]
    meta = packs[0][1]
    outs = _adamw(packs[0][0], packs[1][0], packs[2][0], packs[3][0], "adamw_small")
    for res, o in zip((delta, new_m, new_v), outs):
        for n, a in zip(small_names, _unpack(o, meta)):
            res[n] = a
    return (loss, grad_x[None], *[grads[n] for n in WEIGHT_NAMES], *[delta[n] for n in WEIGHT_NAMES],
            *[new_m[n] for n in WEIGHT_NAMES], *[new_v[n] for n in WEIGHT_NAMES])


def kernel(x, c, ctx, c_ctx, w_mod, b_mod, g_mix, w_in, q_norm, k_norm, attn_sink, w_gate_f, b_gate_f, w_gate_b, b_gate_b, gla_norm, w_attn_o, w_gla_o, w_out, g_ffn, w_up, conv_w, conv_b, w_down, loss_target, m_c_ctx, m_w_mod, m_b_mod, m_g_mix, m_w_in, m_q_norm, m_k_norm, m_attn_sink, m_w_gate_f, m_b_gate_f, m_w_gate_b, m_b_gate_b, m_gla_norm, m_w_attn_o, m_w_gla_o, m_w_out, m_g_ffn, m_w_up, m_conv_w, m_conv_b, m_w_down, v_c_ctx, v_w_mod, v_b_mod, v_g_mix, v_w_in, v_q_norm, v_k_norm, v_attn_sink, v_w_gate_f, v_b_gate_f, v_w_gate_b, v_b_gate_b, v_gla_norm, v_w_attn_o, v_w_gla_o, v_w_out, v_g_ffn, v_w_up, v_conv_w, v_conv_b, v_w_down):
    W = dict(zip(WEIGHT_NAMES, (c_ctx, w_mod, b_mod, g_mix, w_in, q_norm, k_norm, attn_sink, w_gate_f, b_gate_f,
                                w_gate_b, b_gate_b, gla_norm, w_attn_o, w_gla_o, w_out, g_ffn, w_up, conv_w, conv_b,
                                w_down)))
    M = dict(zip(WEIGHT_NAMES, (m_c_ctx, m_w_mod, m_b_mod, m_g_mix, m_w_in, m_q_norm, m_k_norm, m_attn_sink,
                                m_w_gate_f, m_b_gate_f, m_w_gate_b, m_b_gate_b, m_gla_norm, m_w_attn_o, m_w_gla_o,
                                m_w_out, m_g_ffn, m_w_up, m_conv_w, m_conv_b, m_w_down)))
    V = dict(zip(WEIGHT_NAMES, (v_c_ctx, v_w_mod, v_b_mod, v_g_mix, v_w_in, v_q_norm, v_k_norm, v_attn_sink,
                                v_w_gate_f, v_b_gate_f, v_w_gate_b, v_b_gate_b, v_gla_norm, v_w_attn_o, v_w_gla_o,
                                v_w_out, v_g_ffn, v_w_up, v_conv_w, v_conv_b, v_w_down)))
    return _step(x, c, ctx, loss_target, W, M, V)
```

```python
import functools
import math

import jax
import jax.numpy as jnp
from jax import lax
from jax.experimental import pallas as pl
from jax.experimental.pallas import tpu as pltpu

F32 = jnp.float32
BF16 = jnp.bfloat16
MESH = pl.DeviceIdType.MESH

EPS = 1e-6
HEAD_DIM = 128
N_Q_HEADS = 16
N_KV_HEADS = 4
Q_PER_KV = N_Q_HEADS // N_KV_HEADS
WINDOW = 128
GLA_HEADS = 4
GLA_LOWRANK = 16
GLA_GATE_NORM = 16.0
GLA_CHUNK = 64
GRID_W = 64
ROPE_THETA = 10000.0
GLA_LEVELS = (32, 16, 8, 4, 2, 1)
LANES = 128

ADAM_LR = 0.001
ADAM_B1 = 0.9
ADAM_B2 = 0.999
ADAM_EPS = 1e-08
ADAM_WD = 0.01
ADAM_STEP = 10

VMEM_LIMIT = 52 * 1024 * 1024


def _cparams(*sem):
    return pltpu.CompilerParams(dimension_semantics=sem, vmem_limit_bytes=VMEM_LIMIT)


def _pick(n, target, mult=LANES):
    best = None
    d = mult
    while d <= min(n, target):
        if n % d == 0:
            best = d
        d += mult
    return n if best is None else best


def _sigmoid(x):
    return 1.0 / (1.0 + jnp.exp(-x))


def _silu(x):
    return x * _sigmoid(x)


def _dsilu(x):
    s = _sigmoid(x)
    return s * (1.0 + x * (1.0 - s))


def _dot(a, b, dims):
    return lax.dot_general(a, b, (dims, ((), ())), preferred_element_type=F32)


NN = ((1,), (0,))
NT = ((1,), (1,))
TN = ((0,), (0,))


def _matmul(a, b, mode, out_dtype, name, tm=768, tn=1024, tk=2048):
    if mode == "nn":
        (M, K), (K2, N) = a.shape, b.shape
    elif mode == "nt":
        (M, K), (N, K2) = a.shape, b.shape
    else:
        (K, M), (K2, N) = a.shape, b.shape
    assert K == K2, (name, a.shape, b.shape)
    tm, tn, tk = _pick(M, tm), _pick(N, tn), _pick(K, tk)
    nk = K // tk
    dims = {"nn": NN, "nt": NT, "tn": TN}[mode]

    def body(a_ref, b_ref, o_ref, acc_ref):
        k = pl.program_id(2)

        @pl.when(k == 0)
        def _():
            acc_ref[...] = jnp.zeros_like(acc_ref)

        acc_ref[...] += _dot(a_ref[...].astype(BF16), b_ref[...].astype(BF16), dims)

        @pl.when(k == nk - 1)
        def _():
            o_ref[...] = acc_ref[...].astype(out_dtype)

    if mode == "tn":
        a_spec = pl.BlockSpec((tk, tm), lambda i, j, k: (k, i))
    else:
        a_spec = pl.BlockSpec((tm, tk), lambda i, j, k: (i, k))
    if mode == "nt":
        b_spec = pl.BlockSpec((tn, tk), lambda i, j, k: (j, k))
    else:
        b_spec = pl.BlockSpec((tk, tn), lambda i, j, k: (k, j))
    return pl.pallas_call(
        body, name=name, grid=(M // tm, N // tn, nk),
        in_specs=[a_spec, b_spec],
        out_specs=pl.BlockSpec((tm, tn), lambda i, j, k: (i, j)),
        out_shape=jax.ShapeDtypeStruct((M, N), out_dtype),
        scratch_shapes=[pltpu.VMEM((tm, tn), F32)],
        compiler_params=_cparams("parallel", "parallel", "arbitrary"),
    )(a, b)


def _modnorm_fwd(xall, g, sc, sh, n_ctx, name):
    R, D = xall.shape
    tm = _pick(n_ctx, 256, 8)
    cb = n_ctx // tm

    def body(x_ref, g_ref, sc_ref, sh_ref, h_ref):
        x = x_ref[...]
        r = lax.rsqrt(jnp.mean(x * x, axis=-1, keepdims=True) + EPS)
        n = x * r * g_ref[...]
        h_ref[...] = (n * (1.0 + sc_ref[0]) + sh_ref[0]).astype(BF16)

    sel = lambda i: (jnp.where(i < cb, 0, 1), 0, 0)
    return pl.pallas_call(
        body, name=name, grid=(R // tm,),
        in_specs=[pl.BlockSpec((tm, D), lambda i: (i, 0)), pl.BlockSpec((1, D), lambda i: (0, 0)),
                  pl.BlockSpec((1, 1, D), sel), pl.BlockSpec((1, 1, D), sel)],
        out_specs=pl.BlockSpec((tm, D), lambda i: (i, 0)),
        out_shape=jax.ShapeDtypeStruct((R, D), BF16),
        compiler_params=_cparams("parallel"),
    )(xall, g, sc, sh)


def _modnorm_bwd(x, dh, g, sc, resid, name):
    N, D = x.shape
    tm = _pick(N, 256, 8)
    want_dx = resid is not None

    def body(*refs):
        if want_dx:
            x_ref, dh_ref, g_ref, sc_ref, res_ref, dx_ref, acc_ref = refs
        else:
            x_ref, dh_ref, g_ref, sc_ref, acc_ref = refs
        i = pl.program_id(0)

        @pl.when(i == 0)
        def _():
            acc_ref[...] = jnp.zeros_like(acc_ref)

        xv, dhv, gv = x_ref[...], dh_ref[...], g_ref[...]
        r = lax.rsqrt(jnp.mean(xv * xv, axis=-1, keepdims=True) + EPS)
        xh = xv * r
        dn = dhv * (1.0 + sc_ref[...])
        acc_ref[0:1, :] += jnp.sum(dhv, axis=0, keepdims=True)
        acc_ref[1:2, :] += jnp.sum(dhv * xh * gv, axis=0, keepdims=True)
        acc_ref[2:3, :] += jnp.sum(dn * xh, axis=0, keepdims=True)
        if want_dx:
            dxh = dn * gv
            dx_ref[...] = res_ref[...] + r * (dxh - xh * jnp.mean(dxh * xh, axis=-1, keepdims=True))

    row = pl.BlockSpec((tm, D), lambda i: (i, 0))
    vec = pl.BlockSpec((1, D), lambda i: (0, 0))
    acc = pl.BlockSpec((8, D), lambda i: (0, 0))
    acc_shape = jax.ShapeDtypeStruct((8, D), F32)
    if want_dx:
        return pl.pallas_call(
            body, name=name, grid=(N // tm,), in_specs=[row, row, vec, vec, row],
            out_specs=[row, acc], out_shape=[jax.ShapeDtypeStruct((N, D), F32), acc_shape],
            compiler_params=_cparams("arbitrary"))(x, dh, g, sc, resid)
    sums = pl.pallas_call(
        body, name=name, grid=(N // tm,), in_specs=[row, row, vec, vec],
        out_specs=acc, out_shape=acc_shape, compiler_params=_cparams("arbitrary"))(x, dh, g, sc)
    return None, sums


def _qknorm_fwd(z, cblk, nrows, roff, w, cos2, sin2, nh, name):
    W = nh * HEAD_DIM
    tm = _pick(math.gcd(nrows, roff), 256, 8)
    ro = roff // tm
    assert roff % tm == 0

    def body(z_ref, w_ref, c_ref, s_ref, o_ref):
        c, s, wv = c_ref[...], s_ref[...], w_ref[...]
        for h in range(nh):
            x = z_ref[:, h * HEAD_DIM:(h + 1) * HEAD_DIM]
            r = lax.rsqrt(jnp.mean(x * x, axis=-1, keepdims=True) + EPS)
            y = x * r * wv
            o_ref[:, h * HEAD_DIM:(h + 1) * HEAD_DIM] = (y * c + pltpu.roll(y, HEAD_DIM // 2, 1) * s).astype(BF16)

    return pl.pallas_call(
        body, name=name, grid=(nrows // tm,),
        in_specs=[pl.BlockSpec((tm, W), lambda i: (i + ro, cblk)), pl.BlockSpec((1, HEAD_DIM), lambda i: (0, 0)),
                  pl.BlockSpec((tm, HEAD_DIM), lambda i: (i + ro, 0)), pl.BlockSpec((tm, HEAD_DIM), lambda i: (i + ro, 0))],
        out_specs=pl.BlockSpec((tm, W), lambda i: (i, 0)),
        out_shape=jax.ShapeDtypeStruct((nrows, W), BF16),
        compiler_params=_cparams("parallel"),
    )(z, w, cos2, sin2)


def _qknorm_bwd(z, cblk, nrows, roff, w, cos2, sin2, dy, nh, name):
    W = nh * HEAD_DIM
    tm = _pick(math.gcd(nrows, roff), 256, 8)
    ro = roff // tm

    def body(z_ref, w_ref, c_ref, s_ref, dy_ref, dz_ref, acc_ref):
        i = pl.program_id(0)

        @pl.when(i == 0)
        def _():
            acc_ref[...] = jnp.zeros_like(acc_ref)

        c, s, wv = c_ref[...], s_ref[...], w_ref[...]
        dw = jnp.zeros((1, HEAD_DIM), F32)
        for h in range(nh):
            sl = slice(h * HEAD_DIM, (h + 1) * HEAD_DIM)
            x = z_ref[:, sl]
            d = dy_ref[:, sl]
            dyn = d * c + pltpu.roll(d * s, HEAD_DIM // 2, 1)
            r = lax.rsqrt(jnp.mean(x * x, axis=-1, keepdims=True) + EPS)
            xh = x * r
            dw = dw + jnp.sum(dyn * xh, axis=0, keepdims=True)
            dxh = dyn * wv
            dz_ref[:, sl] = (r * (dxh - xh * jnp.mean(dxh * xh, axis=-1, keepdims=True))).astype(BF16)
        acc_ref[0:1, :] += dw

    return pl.pallas_call(
        body, name=name, grid=(nrows // tm,),
        in_specs=[pl.BlockSpec((tm, W), lambda i: (i + ro, cblk)), pl.BlockSpec((1, HEAD_DIM), lambda i: (0, 0)),
                  pl.BlockSpec((tm, HEAD_DIM), lambda i: (i + ro, 0)), pl.BlockSpec((tm, HEAD_DIM), lambda i: (i + ro, 0)),
                  pl.BlockSpec((tm, W), lambda i: (i, 0))],
        out_specs=[pl.BlockSpec((tm, W), lambda i: (i, 0)), pl.BlockSpec((8, HEAD_DIM), lambda i: (0, 0))],
        out_shape=[jax.ShapeDtypeStruct((nrows, W), BF16), jax.ShapeDtypeStruct((8, HEAD_DIM), F32)],
        compiler_params=_cparams("arbitrary"),
    )(z, w, cos2, sin2, dy)


def _cast_seg(z, cblk, width, name):
    R = z.shape[0]
    tm = _pick(R, 512, 8)

    def body(z_ref, o_ref):
        o_ref[...] = z_ref[...].astype(BF16)

    return pl.pallas_call(
        body, name=name, grid=(R // tm,),
        in_specs=[pl.BlockSpec((tm, width), lambda i: (i, cblk))],
        out_specs=pl.BlockSpec((tm, width), lambda i: (i, 0)),
        out_shape=jax.ShapeDtypeStruct((R, width), BF16), compiler_params=_cparams("parallel"))(z)


NEG_BIG = -1e30


def _attn_specs(T, n_ctx):
    nb = T // WINDOW
    lb = n_ctx // WINDOW
    blk = lambda f: pl.BlockSpec((WINDOW, HEAD_DIM), f)
    win = [blk(lambda h, i: (lb + jnp.maximum(i - 1, 0), h)), blk(lambda h, i: (lb + i, h)),
           blk(lambda h, i: (lb + jnp.minimum(i + 1, nb - 1), h))]
    ctx = pl.BlockSpec((n_ctx, HEAD_DIM), lambda h, i: (0, h))
    qspec = pl.BlockSpec((WINDOW, Q_PER_KV * HEAD_DIM), lambda h, i: (i, h))
    sink = pl.BlockSpec((N_Q_HEADS, HEAD_DIM), lambda h, i: (0, 0))
    return nb, qspec, win, ctx, sink


def _attn_probs(q, kw, kctx, snk, valid):
    scale = HEAD_DIM ** -0.5
    s_lat = jnp.where(valid, _dot(q, kw, NT) * scale, NEG_BIG)
    s_ctx = _dot(q, kctx, NT) * scale
    m = jnp.maximum(jnp.maximum(jnp.max(s_lat, axis=-1, keepdims=True), jnp.max(s_ctx, axis=-1, keepdims=True)), snk)
    p_lat = jnp.exp(s_lat - m)
    p_ctx = jnp.exp(s_ctx - m)
    p_snk = jnp.exp(snk - m)
    den = p_snk + jnp.sum(p_lat, axis=-1, keepdims=True) + jnp.sum(p_ctx, axis=-1, keepdims=True)
    return p_lat, p_ctx, p_snk, den


def _attn_valid(i, T):
    qpos = i * WINDOW + lax.broadcasted_iota(jnp.int32, (WINDOW, 3 * WINDOW), 0)
    kpos = (i - 1) * WINDOW + lax.broadcasted_iota(jnp.int32, (WINDOW, 3 * WINDOW), 1)
    return (jnp.abs(qpos - kpos) <= WINDOW) & (kpos >= 0) & (kpos < T)


def _attn_fwd(qn, kn, vb, sink_rows, n_ctx, name):
    T = qn.shape[0]
    nb, qspec, win, ctx, sink = _attn_specs(T, n_ctx)

    def body(q_ref, kp, kc, kx, vp, vc, vx, kctx_ref, vctx_ref, sink_ref, o_ref):
        h, i = pl.program_id(0), pl.program_id(1)
        kw = jnp.concatenate([kp[...], kc[...], kx[...]], axis=0)
        vw = jnp.concatenate([vp[...], vc[...], vx[...]], axis=0)
        kctx, vctx = kctx_ref[...], vctx_ref[...]
        valid = _attn_valid(i, T)
        for g in range(Q_PER_KV):
            sl = slice(g * HEAD_DIM, (g + 1) * HEAD_DIM)
            snk = sink_ref[pl.ds(h * Q_PER_KV + g, 1), :][:, 0:1]
            p_lat, p_ctx, _, den = _attn_probs(q_ref[:, sl], kw, kctx, snk, valid)
            o = (_dot(p_lat.astype(BF16), vw, NN) + _dot(p_ctx.astype(BF16), vctx, NN)) / den
            o_ref[:, sl] = o.astype(BF16)

    return pl.pallas_call(
        body, name=name, grid=(N_KV_HEADS, nb),
        in_specs=[qspec] + win + win + [ctx, ctx, sink],
        out_specs=qspec, out_shape=jax.ShapeDtypeStruct(qn.shape, BF16),
        compiler_params=_cparams("parallel", "parallel"),
    )(qn, kn, kn, kn, vb, vb, vb, kn, vb, sink_rows)


def _attn_bwd(qn, kn, vb, sink_rows, do, n_ctx, name):
    T = qn.shape[0]
    nb, qspec, win, ctx, sink = _attn_specs(T, n_ctx)
    scale = HEAD_DIM ** -0.5
    TP = T + 2 * WINDOW

    def body(q_ref, kp, kc, kx, vp, vc, vx, kctx_ref, vctx_ref, sink_ref, do_ref,
             dq_ref, dkw_ref, dvw_ref, dkc_ref, dvc_ref, dsn_ref):
        h, i = pl.program_id(0), pl.program_id(1)

        @pl.when(i == 0)
        def _():
            dkw_ref[...] = jnp.zeros_like(dkw_ref)
            dvw_ref[...] = jnp.zeros_like(dvw_ref)
            dkc_ref[...] = jnp.zeros_like(dkc_ref)
            dvc_ref[...] = jnp.zeros_like(dvc_ref)
            dsn_ref[...] = jnp.zeros_like(dsn_ref)

        kw = jnp.concatenate([kp[...], kc[...], kx[...]], axis=0)
        vw = jnp.concatenate([vp[...], vc[...], vx[...]], axis=0)
        kctx, vctx = kctx_ref[...], vctx_ref[...]
        valid = _attn_valid(i, T)
        lane = lax.broadcasted_iota(jnp.int32, (8, HEAD_DIM), 1)
        dkw = jnp.zeros((3 * WINDOW, HEAD_DIM), F32)
        dvw = jnp.zeros((3 * WINDOW, HEAD_DIM), F32)
        dkc = jnp.zeros(kctx.shape, F32)
        dvc = jnp.zeros(kctx.shape, F32)
        dsn = jnp.zeros((8, HEAD_DIM), F32)
        for g in range(Q_PER_KV):
            sl = slice(g * HEAD_DIM, (g + 1) * HEAD_DIM)
            snk = sink_ref[pl.ds(h * Q_PER_KV + g, 1), :][:, 0:1]
            q, d_o = q_ref[:, sl], do_ref[:, sl]
            p_lat, p_ctx, p_snk, den = _attn_probs(q, kw, kctx, snk, valid)
            inv = 1.0 / den
            p_lat, p_ctx, p_snk = p_lat * inv, p_ctx * inv, p_snk * inv
            dp_lat = _dot(d_o, vw, NT)
            dp_ctx = _dot(d_o, vctx, NT)
            dr = jnp.sum(p_lat * dp_lat, axis=-1, keepdims=True) + jnp.sum(p_ctx * dp_ctx, axis=-1, keepdims=True)
            ds_lat = (p_lat * (dp_lat - dr) * scale).astype(BF16)
            ds_ctx = (p_ctx * (dp_ctx - dr) * scale).astype(BF16)
            dq_ref[:, sl] = _dot(ds_lat, kw, NN) + _dot(ds_ctx, kctx, NN)
            dkw = dkw + _dot(ds_lat, q, TN)
            dvw = dvw + _dot(p_lat.astype(BF16), d_o, TN)
            dkc = dkc + _dot(ds_ctx, q, TN)
            dvc = dvc + _dot(p_ctx.astype(BF16), d_o, TN)
            dsn = dsn + jnp.where(lane == g, -jnp.sum(p_snk * dr, axis=0, keepdims=True), 0.0)
        rows = pl.ds(pl.multiple_of(i * WINDOW, WINDOW), 3 * WINDOW)
        dkw_ref[rows, :] += dkw
        dvw_ref[rows, :] += dvw
        dkc_ref[...] += dkc
        dvc_ref[...] += dvc
        dsn_ref[0] += dsn

    wacc = pl.BlockSpec((TP, HEAD_DIM), lambda h, i: (0, h))
    return pl.pallas_call(
        body, name=name, grid=(N_KV_HEADS, nb),
        in_specs=[qspec] + win + win + [ctx, ctx, sink, qspec],
        out_specs=[qspec, wacc, wacc, ctx, ctx, pl.BlockSpec((1, 8, HEAD_DIM), lambda h, i: (h, 0, 0))],
        out_shape=[jax.ShapeDtypeStruct(qn.shape, F32),
                   jax.ShapeDtypeStruct((TP, N_KV_HEADS * HEAD_DIM), F32),
                   jax.ShapeDtypeStruct((TP, N_KV_HEADS * HEAD_DIM), F32),
                   jax.ShapeDtypeStruct((n_ctx, N_KV_HEADS * HEAD_DIM), F32),
                   jax.ShapeDtypeStruct((n_ctx, N_KV_HEADS * HEAD_DIM), F32),
                   jax.ShapeDtypeStruct((N_KV_HEADS, 8, HEAD_DIM), F32)],
        compiler_params=_cparams("arbitrary", "arbitrary"),
    )(qn, kn, kn, kn, vb, vb, vb, kn, vb, sink_rows, do)


def _gla_masks(dirv):
    C = GLA_CHUNK
    r = lax.broadcasted_iota(jnp.int32, (C, C), 0)
    c = lax.broadcasted_iota(jnp.int32, (C, C), 1)
    tt = jnp.where(dirv == 0, r, C - 1 - r)
    ss = jnp.where(dirv == 0, c, C - 1 - c)
    le = (ss <= tt).astype(jnp.int32)
    sums = [le == 1, le == 0]
    blocks = [ss == tt]
    for m in GLA_LEVELS:
        sh = m.bit_length() - 1
        same = (tt >> (sh + 1)) == (ss >> (sh + 1))
        ut = (tt >> sh) & 1
        us = (ss >> sh) & 1
        sums.append(same & (ut == us) & (ut == le))
        blocks.append(same & (ut == 1) & (us == 0))
    mall = jnp.concatenate([jnp.where(s, 1.0, 0.0) for s in sums], axis=0).astype(BF16)
    return mall, blocks


def _split3(x):
    hi = x.astype(BF16)
    r1 = x - hi.astype(F32)
    mid = r1.astype(BF16)
    lo = (r1 - mid.astype(F32)).astype(BF16)
    return hi, mid, lo


def _dot3(m_bf16, x, dims):
    hi, mid, lo = _split3(x)
    return _dot(m_bf16, hi, dims) + _dot(m_bf16, mid, dims) + _dot(m_bf16, lo, dims)


def _gla_chunk_of(dirv, j, lc, nc):
    return jnp.where(dirv == 0, j, jnp.where(j < lc, lc - 1 - j, nc + lc - 1 - j))


def _gla_gate(lr_ref, wg_ref, bg_ref):
    pre = _dot(lr_ref[...].astype(BF16), wg_ref[0].astype(BF16), NN) + bg_ref[0]
    g = (jnp.minimum(pre, 0.0) - jnp.log(1.0 + jnp.exp(-jnp.abs(pre)))) * (1.0 / GLA_GATE_NORM)
    return pre, g


def _gla_fwd(z, qblk, kblk, vblk, lrblk, wg, bg, DV, n_ctx, name):
    R = z.shape[0]
    C = GLA_CHUNK
    DK = wg.shape[2] // GLA_HEADS
    nc, lc = R // C, n_ctx // C
    qscale = DK ** -0.5

    GK, GV = GLA_HEADS * DK, GLA_HEADS * DV

    def body(q_ref, k_ref, v_ref, lr_ref, wg_ref, bg_ref, o_ref, sp_ref, st_ref):
        dirv, j = pl.program_id(0), pl.program_id(1)

        @pl.when(j == 0)
        def _():
            st_ref[...] = jnp.zeros_like(st_ref)

        mall, blocks = _gla_masks(dirv)
        _, g_all = _gla_gate(lr_ref, wg_ref, bg_ref)
        E_all = _dot3(mall, g_all, NN)
        for h in range(GLA_HEADS):
            ks, vs = slice(h * DK, (h + 1) * DK), slice(h * DV, (h + 1) * DV)
            q, k, v = q_ref[:, ks] * qscale, k_ref[:, ks], v_ref[:, vs].astype(BF16)
            g, E = g_all[:, ks], E_all[:, ks]
            st = st_ref[h]
            sp_ref[0, h, 0] = st
            A = jnp.where(blocks[0], _dot(q.astype(BF16), k.astype(BF16), NT), 0.0)
            for l in range(len(GLA_LEVELS)):
                e = jnp.exp(E[(2 + l) * C:(3 + l) * C])
                A = A + jnp.where(blocks[l + 1], _dot((q * e).astype(BF16), (k * e).astype(BF16), NT), 0.0)
            o_ref[0, :, vs] = (_dot((q * jnp.exp(E[0:C])).astype(BF16), st.astype(BF16), NT)
                               + _dot(A.astype(BF16), v, NN))
            decay = jnp.exp(jnp.sum(g, axis=0, keepdims=True))
            st_ref[h] = decay * st + _dot(v, (k * jnp.exp(E[C:2 * C])).astype(BF16), TN)

    chunk = functools.partial(_gla_chunk_of, lc=lc, nc=nc)
    return pl.pallas_call(
        body, name=name, grid=(2, nc),
        in_specs=[pl.BlockSpec((C, GK), lambda d, j: (chunk(d, j), qblk)),
                  pl.BlockSpec((C, GK), lambda d, j: (chunk(d, j), kblk)),
                  pl.BlockSpec((C, GV), lambda d, j: (chunk(d, j), vblk)),
                  pl.BlockSpec((C, LANES), lambda d, j: (chunk(d, j), lrblk)),
                  pl.BlockSpec((1, LANES, GK), lambda d, j: (d, 0, 0)),
                  pl.BlockSpec((1, 1, GK), lambda d, j: (d, 0, 0))],
        out_specs=[pl.BlockSpec((1, C, GV), lambda d, j: (d, chunk(d, j), 0)),
                   pl.BlockSpec((1, GLA_HEADS, 1, DV, DK), lambda d, j: (d, 0, j, 0, 0))],
        out_shape=[jax.ShapeDtypeStruct((2, R, GV), F32),
                   jax.ShapeDtypeStruct((2, GLA_HEADS, nc, DV, DK), F32)],
        scratch_shapes=[pltpu.VMEM((GLA_HEADS, DV, DK), F32)],
        compiler_params=_cparams("parallel", "arbitrary"),
    )(z, z, z, z, wg, bg)


def _gla_bwd(z, qblk, kblk, vblk, lrblk, wg, bg, sprev, do, n_ctx, name):
    R = z.shape[0]
    C = GLA_CHUNK
    DK, DV = wg.shape[2] // GLA_HEADS, do.shape[1] // GLA_HEADS
    nc, lc = R // C, n_ctx // C
    qscale = DK ** -0.5
    nl = len(GLA_LEVELS)

    GK, GV = GLA_HEADS * DK, GLA_HEADS * DV

    def body(q_ref, k_ref, v_ref, lr_ref, wg_ref, bg_ref, sp_ref, do_ref,
             dq_ref, dk_ref, dv_ref, dpre_ref, dbg_ref, dst_ref):
        dirv, jr = pl.program_id(0), pl.program_id(1)

        @pl.when(jr == 0)
        def _():
            dst_ref[...] = jnp.zeros_like(dst_ref)
            dbg_ref[...] = jnp.zeros_like(dbg_ref)

        mall, blocks = _gla_masks(dirv)
        pre_all, g_all = _gla_gate(lr_ref, wg_ref, bg_ref)
        E_all = _dot3(mall, g_all, NN)
        for h in range(GLA_HEADS):
            ks, vs = slice(h * DK, (h + 1) * DK), slice(h * DV, (h + 1) * DV)
            q, k, v = q_ref[:, ks] * qscale, k_ref[:, ks], v_ref[:, vs].astype(BF16)
            pre, g, E = pre_all[:, ks], g_all[:, ks], E_all[:, ks]
            eb, er = jnp.exp(E[0:C]), jnp.exp(E[C:2 * C])
            decay = jnp.exp(jnp.sum(g, axis=0, keepdims=True))
            st = sp_ref[0, h, 0]
            dst = dst_ref[h]
            d_o = do_ref[:, vs]
            qe, kd = q * eb, k * er
            qb, kb = q.astype(BF16), k.astype(BF16)
            A = jnp.where(blocks[0], _dot(qb, kb, NT), 0.0)
            for l in range(nl):
                e = jnp.exp(E[(2 + l) * C:(3 + l) * C])
                A = A + jnp.where(blocks[l + 1], _dot((q * e).astype(BF16), (k * e).astype(BF16), NT), 0.0)
            dA = _dot(d_o, v, NT)
            dv_ref[0, :, vs] = _dot(A.astype(BF16), d_o, TN) + _dot(kd.astype(BF16), dst.astype(BF16), NT)
            dqe = _dot(d_o, st.astype(BF16), NN)
            dkd = _dot(v, dst.astype(BF16), NN)
            G = jnp.where(blocks[0], dA, 0.0).astype(BF16)
            dq = dqe * eb + _dot(G, kb, NN)
            dk = dkd * er + _dot(G, qb, TN)
            dE = [dqe * qe, dkd * kd]
            for l in range(nl):
                e = jnp.exp(E[(2 + l) * C:(3 + l) * C])
                ql, kl = q * e, k * e
                G = jnp.where(blocks[l + 1], dA, 0.0).astype(BF16)
                dql = _dot(G, kl.astype(BF16), NN)
                dkl = _dot(G, ql.astype(BF16), TN)
                dq = dq + dql * e
                dk = dk + dkl * e
                dE.append(dql * ql + dkl * kl)
            dlast = jnp.sum(dst * st, axis=0, keepdims=True) * decay
            dg = _dot3(mall, jnp.concatenate(dE, axis=0), TN) + dlast
            dpre = dg * (1.0 / GLA_GATE_NORM) / (1.0 + jnp.exp(pre))
            dq_ref[0, :, ks] = dq * qscale
            dk_ref[0, :, ks] = dk
            dpre_ref[:, ks] = dpre.astype(BF16)
            dbg_ref[0, :, ks] += jnp.sum(dpre, axis=0, keepdims=True)
            dst_ref[h] = decay * dst + _dot(d_o, qe.astype(BF16), TN)

    def chunk(d, jr):
        return _gla_chunk_of(d, nc - 1 - jr, lc, nc)

    return pl.pallas_call(
        body, name=name, grid=(2, nc),
        in_specs=[pl.BlockSpec((C, GK), lambda d, j: (chunk(d, j), qblk)),
                  pl.BlockSpec((C, GK), lambda d, j: (chunk(d, j), kblk)),
                  pl.BlockSpec((C, GV), lambda d, j: (chunk(d, j), vblk)),
                  pl.BlockSpec((C, LANES), lambda d, j: (chunk(d, j), lrblk)),
                  pl.BlockSpec((1, LANES, GK), lambda d, j: (d, 0, 0)),
                  pl.BlockSpec((1, 1, GK), lambda d, j: (d, 0, 0)),
                  pl.BlockSpec((1, GLA_HEADS, 1, DV, DK), lambda d, j: (d, 0, nc - 1 - j, 0, 0)),
                  pl.BlockSpec((C, GV), lambda d, j: (chunk(d, j), 0))],
        out_specs=[pl.BlockSpec((1, C, GK), lambda d, j: (d, chunk(d, j), 0)),
                   pl.BlockSpec((1, C, GK), lambda d, j: (d, chunk(d, j), 0)),
                   pl.BlockSpec((1, C, GV), lambda d, j: (d, chunk(d, j), 0)),
                   pl.BlockSpec((C, GK), lambda d, j: (chunk(d, j), d)),
                   pl.BlockSpec((1, 1, GK), lambda d, j: (d, 0, 0))],
        out_shape=[jax.ShapeDtypeStruct((2, R, GK), F32),
                   jax.ShapeDtypeStruct((2, R, GK), F32),
                   jax.ShapeDtypeStruct((2, R, GV), F32),
                   jax.ShapeDtypeStruct((R, 2 * GK), BF16),
                   jax.ShapeDtypeStruct((2, 1, GK), F32)],
        scratch_shapes=[pltpu.VMEM((GLA_HEADS, DV, DK), F32)],
        compiler_params=_cparams("arbitrary", "arbitrary"),
    )(z, z, z, z, wg, bg, sprev, do)


def _glanorm_fwd(o, z, rbblk, gn, n_ctx, name):
    _, R, GV = o.shape
    T = R - n_ctx
    DV = GV // GLA_HEADS
    tm = _pick(n_ctx, 256, 8)
    ro = n_ctx // tm

    def body(o0_ref, o1_ref, rb_ref, gn_ref, p_ref):
        gnv = gn_ref[...]
        for h in range(GLA_HEADS):
            sl = slice(h * DV, (h + 1) * DV)
            og = o0_ref[0, :, sl] + o1_ref[0, :, sl]
            r = lax.rsqrt(jnp.mean(og * og, axis=-1, keepdims=True) + EPS)
            p_ref[:, sl] = (og * r * gnv * _silu(rb_ref[:, sl])).astype(BF16)

    return pl.pallas_call(
        body, name=name, grid=(T // tm,),
        in_specs=[pl.BlockSpec((1, tm, GV), lambda i: (0, i + ro, 0)), pl.BlockSpec((1, tm, GV), lambda i: (1, i + ro, 0)),
                  pl.BlockSpec((tm, GV), lambda i: (i + ro, rbblk)), pl.BlockSpec((1, DV), lambda i: (0, 0))],
        out_specs=pl.BlockSpec((tm, GV), lambda i: (i, 0)),
        out_shape=jax.ShapeDtypeStruct((T, GV), BF16), compiler_params=_cparams("parallel"))(o, o, z, gn)


def _glanorm_bwd(o, z, rbblk, gn, dp, n_ctx, name):
    _, R, GV = o.shape
    T = R - n_ctx
    DV = GV // GLA_HEADS
    tm = _pick(n_ctx, 256, 8)
    ro = n_ctx // tm

    def body(o0_ref, o1_ref, rb_ref, gn_ref, dp_ref, do_ref, drb_ref, acc_ref):
        i = pl.program_id(0)

        @pl.when(i == 0)
        def _():
            acc_ref[...] = jnp.zeros_like(acc_ref)

        gnv = gn_ref[...]
        dgn = jnp.zeros((1, DV), F32)
        for h in range(GLA_HEADS):
            sl = slice(h * DV, (h + 1) * DV)
            og = o0_ref[0, :, sl] + o1_ref[0, :, sl]
            rb = rb_ref[:, sl]
            d = dp_ref[:, sl]
            r = lax.rsqrt(jnp.mean(og * og, axis=-1, keepdims=True) + EPS)
            xh = og * r
            drb_ref[:, sl] = (d * xh * gnv * _dsilu(rb)).astype(BF16)
            dn = d * _silu(rb)
            dgn = dgn + jnp.sum(dn * xh, axis=0, keepdims=True)
            dxh = dn * gnv
            do_ref[:, sl] = (r * (dxh - xh * jnp.mean(dxh * xh, axis=-1, keepdims=True))).astype(BF16)
        acc_ref[0:1, :] += dgn

    row = pl.BlockSpec((tm, GV), lambda i: (i, 0))
    return pl.pallas_call(
        body, name=name, grid=(T // tm,),
        in_specs=[pl.BlockSpec((1, tm, GV), lambda i: (0, i + ro, 0)), pl.BlockSpec((1, tm, GV), lambda i: (1, i + ro, 0)),
                  pl.BlockSpec((tm, GV), lambda i: (i + ro, rbblk)), pl.BlockSpec((1, DV), lambda i: (0, 0)), row],
        out_specs=[row, row, pl.BlockSpec((8, DV), lambda i: (0, 0))],
        out_shape=[jax.ShapeDtypeStruct((T, GV), BF16), jax.ShapeDtypeStruct((T, GV), BF16),
                   jax.ShapeDtypeStruct((8, DV), F32)],
        compiler_params=_cparams("arbitrary"))(o, o, z, gn, dp)


def _gate_fwd(z, gablk, gbblk, ya, yg, n_ctx, name):
    T, D = ya.shape
    tm = _pick(n_ctx, 256, 8)
    ro = n_ctx // tm

    def body(ga_ref, gb_ref, ya_ref, yg_ref, m_ref):
        m_ref[...] = (_sigmoid(ga_ref[...]) * ya_ref[...] + _sigmoid(gb_ref[...]) * yg_ref[...]).astype(BF16)

    row = pl.BlockSpec((tm, D), lambda i: (i, 0))
    return pl.pallas_call(
        body, name=name, grid=(T // tm,),
        in_specs=[pl.BlockSpec((tm, D), lambda i: (i + ro, gablk)), pl.BlockSpec((tm, D), lambda i: (i + ro, gbblk)), row, row],
        out_specs=row, out_shape=jax.ShapeDtypeStruct((T, D), BF16), compiler_params=_cparams("parallel"))(z, z, ya, yg)


def _gate_bwd(z, gablk, gbblk, ya, yg, dm, n_ctx, name):
    T, D = ya.shape
    tm = _pick(n_ctx, 256, 8)
    ro = n_ctx // tm

    def body(ga_ref, gb_ref, ya_ref, yg_ref, dm_ref, dya_ref, dyg_ref, dga_ref, dgb_ref):
        d = dm_ref[...]
        sa, sb = _sigmoid(ga_ref[...]), _sigmoid(gb_ref[...])
        dya_ref[...] = (d * sa).astype(BF16)
        dyg_ref[...] = (d * sb).astype(BF16)
        dga_ref[...] = (d * ya_ref[...] * sa * (1.0 - sa)).astype(BF16)
        dgb_ref[...] = (d * yg_ref[...] * sb * (1.0 - sb)).astype(BF16)

    row = pl.BlockSpec((tm, D), lambda i: (i, 0))
    sh = jax.ShapeDtypeStruct((T, D), BF16)
    return pl.pallas_call(
        body, name=name, grid=(T // tm,),
        in_specs=[pl.BlockSpec((tm, D), lambda i: (i + ro, gablk)), pl.BlockSpec((tm, D), lambda i: (i + ro, gbblk)), row, row, row],
        out_specs=[row] * 4, out_shape=[sh] * 4, compiler_params=_cparams("parallel"))(z, z, ya, yg, dm)


def _resnorm_fwd(x, mix, gt, g, sc, sh, name):
    T, D = x.shape
    tm = _pick(T, 256, 8)

    def body(x_ref, mix_ref, gt_ref, g_ref, sc_ref, sh_ref, x1_ref, h_ref):
        x1 = x_ref[...] + gt_ref[...] * mix_ref[...]
        x1_ref[...] = x1
        r = lax.rsqrt(jnp.mean(x1 * x1, axis=-1, keepdims=True) + EPS)
        h_ref[...] = (x1 * r * g_ref[...] * (1.0 + sc_ref[...]) + sh_ref[...]).astype(BF16)

    row = pl.BlockSpec((tm, D), lambda i: (i, 0))
    vec = pl.BlockSpec((1, D), lambda i: (0, 0))
    return pl.pallas_call(
        body, name=name, grid=(T // tm,), in_specs=[row, row, vec, vec, vec, vec], out_specs=[row, row],
        out_shape=[jax.ShapeDtypeStruct((T, D), F32), jax.ShapeDtypeStruct((T, D), BF16)],
        compiler_params=_cparams("parallel"))(x, mix, gt, g, sc, sh)


def _gate_resid_bwd(dx, val, gt, name):
    T, D = dx.shape
    tm = _pick(T, 256, 8)

    def body(dx_ref, val_ref, gt_ref, d_ref, acc_ref):
        i = pl.program_id(0)

        @pl.when(i == 0)
        def _():
            acc_ref[...] = jnp.zeros_like(acc_ref)

        d = dx_ref[...]
        d_ref[...] = (d * gt_ref[...]).astype(BF16)
        acc_ref[0:1, :] += jnp.sum(d * val_ref[...], axis=0, keepdims=True)

    row = pl.BlockSpec((tm, D), lambda i: (i, 0))
    return pl.pallas_call(
        body, name=name, grid=(T // tm,), in_specs=[row, row, pl.BlockSpec((1, D), lambda i: (0, 0))],
        out_specs=[row, pl.BlockSpec((8, D), lambda i: (0, 0))],
        out_shape=[jax.ShapeDtypeStruct((T, D), BF16), jax.ShapeDtypeStruct((8, D), F32)],
        compiler_params=_cparams("arbitrary"))(dx, val, gt)


def _loss_head(d, x1, gt, target, name):
    T, D = d.shape
    tm = _pick(T, 256, 8)

    def body(d_ref, x1_ref, gt_ref, t_ref, dy_ref, acc_ref):
        i = pl.program_id(0)

        @pl.when(i == 0)
        def _():
            acc_ref[...] = jnp.zeros_like(acc_ref)

        e = x1_ref[...] + gt_ref[...] * d_ref[...] - t_ref[...]
        dy_ref[...] = e * (1.0 / D)
        acc_ref[0:1, :] += jnp.sum(e * e, axis=0, keepdims=True)

    row = pl.BlockSpec((tm, D), lambda i: (i, 0))
    return pl.pallas_call(
        body, name=name, grid=(T // tm,), in_specs=[row, row, pl.BlockSpec((1, D), lambda i: (0, 0)), row],
        out_specs=[row, pl.BlockSpec((8, D), lambda i: (0, 0))],
        out_shape=[jax.ShapeDtypeStruct((T, D), F32), jax.ShapeDtypeStruct((8, D), F32)],
        compiler_params=_cparams("arbitrary"))(d, x1, gt, target)


def _halo_specs(T, tm, tw, col_of, order):
    n8 = tm // 8
    if order == "ij":
        mid = lambda i, j: (i, col_of(j))
        prev = lambda i, j: (jnp.maximum(i * n8 - 1, 0), col_of(j))
        nxt = lambda i, j: (jnp.minimum((i + 1) * n8, T // 8 - 1), col_of(j))
    else:
        mid = lambda j, i: (i, col_of(j))
        prev = lambda j, i: (jnp.maximum(i * n8 - 1, 0), col_of(j))
        nxt = lambda j, i: (jnp.minimum((i + 1) * n8, T // 8 - 1), col_of(j))
    return [pl.BlockSpec((tm, tw), mid), pl.BlockSpec((8, tw), prev), pl.BlockSpec((8, tw), nxt)]


def _shifted(u_ref, up_ref, un_ref, i, nt):
    u = u_ref[...]
    tm = u.shape[0]
    row = lax.broadcasted_iota(jnp.int32, u.shape, 0)
    hp = jnp.where(i > 0, up_ref[7:8, :], 0.0)
    hn = jnp.where(i < nt - 1, un_ref[0:1, :], 0.0)
    u_prev = jnp.where(row == 0, hp, pltpu.roll(u, 1, 0))
    u_next = jnp.where(row == tm - 1, hn, pltpu.roll(u, tm - 1, 0))
    return u_prev, u, u_next


def _conv_fwd(u, cw, cb, name):
    T, F2 = u.shape
    F = F2 // 2
    tm, tw = _pick(T, 256, 8), _pick(F, 512)
    nt, nw = T // tm, F // tw

    def body(ua, uap, uan, ug, ugp, ugn, cwa, cwg, cba, cbg, f_ref):
        i = pl.program_id(0)

        def conv(u_ref, up_ref, un_ref, w_ref, b_ref):
            p, m, n = _shifted(u_ref, up_ref, un_ref, i, nt)
            return p * w_ref[0:1, :] + m * w_ref[1:2, :] + n * w_ref[2:3, :] + b_ref[...]

        a = conv(ua, uap, uan, cwa, cba)
        g = conv(ug, ugp, ugn, cwg, cbg)
        f_ref[...] = (_silu(a) * g).astype(BF16)

    wspec = lambda off: pl.BlockSpec((3, tw), lambda i, j: (0, j + off))
    bspec = lambda off: pl.BlockSpec((1, tw), lambda i, j: (0, j + off))
    return pl.pallas_call(
        body, name=name, grid=(nt, nw),
        in_specs=_halo_specs(T, tm, tw, lambda j: j, "ij") + _halo_specs(T, tm, tw, lambda j: j + nw, "ij")
        + [wspec(0), wspec(nw), bspec(0), bspec(nw)],
        out_specs=pl.BlockSpec((tm, tw), lambda i, j: (i, j)),
        out_shape=jax.ShapeDtypeStruct((T, F), BF16), compiler_params=_cparams("parallel", "parallel"),
    )(u, u, u, u, u, u, cw, cw, cb, cb)


def _conv_bwd_a(u, df, cw, cb, name):
    T, F2 = u.shape
    F = F2 // 2
    tm, tw = _pick(T, 256, 8), _pick(F, 512)
    nt, nw = T // tm, F // tw

    def body(ua, uap, uan, ug, ugp, ugn, cwa, cwg, cba, cbg, df_ref, da_ref, dg_ref, acca_ref, accg_ref):
        i = pl.program_id(1)

        @pl.when(i == 0)
        def _():
            acca_ref[...] = jnp.zeros_like(acca_ref)
            accg_ref[...] = jnp.zeros_like(accg_ref)

        sa = _shifted(ua, uap, uan, i, nt)
        sg = _shifted(ug, ugp, ugn, i, nt)
        a = sa[0] * cwa[0:1, :] + sa[1] * cwa[1:2, :] + sa[2] * cwa[2:3, :] + cba[...]
        g = sg[0] * cwg[0:1, :] + sg[1] * cwg[1:2, :] + sg[2] * cwg[2:3, :] + cbg[...]
        d = df_ref[...]
        da = d * g * _dsilu(a)
        dg = d * _silu(a)
        da_ref[...] = da
        dg_ref[...] = dg
        for t in range(3):
            acca_ref[t:t + 1, :] += jnp.sum(da * sa[t], axis=0, keepdims=True)
            accg_ref[t:t + 1, :] += jnp.sum(dg * sg[t], axis=0, keepdims=True)
        acca_ref[3:4, :] += jnp.sum(da, axis=0, keepdims=True)
        accg_ref[3:4, :] += jnp.sum(dg, axis=0, keepdims=True)

    wspec = lambda off: pl.BlockSpec((3, tw), lambda j, i: (0, j + off))
    bspec = lambda off: pl.BlockSpec((1, tw), lambda j, i: (0, j + off))
    row = pl.BlockSpec((tm, tw), lambda j, i: (i, j))
    acc = pl.BlockSpec((8, tw), lambda j, i: (0, j))
    return pl.pallas_call(
        body, name=name, grid=(nw, nt),
        in_specs=_halo_specs(T, tm, tw, lambda j: j, "ji") + _halo_specs(T, tm, tw, lambda j: j + nw, "ji")
        + [wspec(0), wspec(nw), bspec(0), bspec(nw), row],
        out_specs=[row, row, acc, acc],
        out_shape=[jax.ShapeDtypeStruct((T, F), F32), jax.ShapeDtypeStruct((T, F), F32),
                   jax.ShapeDtypeStruct((8, F), F32), jax.ShapeDtypeStruct((8, F), F32)],
        compiler_params=_cparams("parallel", "arbitrary"),
    )(u, u, u, u, u, u, cw, cw, cb, cb, df)


def _conv_bwd_b(dc, cw, woff, name):
    T, F = dc.shape
    tm, tw = _pick(T, 256, 8), _pick(F, 512)
    nt, nw = T // tm, F // tw

    def body(d_ref, dp_ref, dn_ref, w_ref, o_ref):
        i = pl.program_id(0)
        p, m, n = _shifted(d_ref, dp_ref, dn_ref, i, nt)
        o_ref[...] = (n * w_ref[0:1, :] + m * w_ref[1:2, :] + p * w_ref[2:3, :]).astype(BF16)

    return pl.pallas_call(
        body, name=name, grid=(nt, nw),
        in_specs=_halo_specs(T, tm, tw, lambda j: j, "ij") + [pl.BlockSpec((3, tw), lambda i, j: (0, j + woff * nw))],
        out_specs=pl.BlockSpec((tm, tw), lambda i, j: (i, j)),
        out_shape=jax.ShapeDtypeStruct((T, F), BF16), compiler_params=_cparams("parallel", "parallel"),
    )(dc, dc, dc, cw)


def _assemble_dz(lay, Z, n_ctx, dqa, drb, dga, dgb, dka, dva, dvg, dqg, dkg, dlr, name):
    T = dqa.shape[0]
    R = T + n_ctx
    tm = _pick(n_ctx, 128, 8)
    cb = n_ctx // tm

    def body(dqa_ref, drb_ref, dga_ref, dgb_ref, dka_ref, dva_ref, dvg0, dvg1, dqg0, dqg1, dkg0, dkg1, dlr_ref, o_ref):
        lat = pl.program_id(0) >= cb

        def put(seg, val):
            o_ref[:, lay[seg]:lay[seg] + val.shape[1]] = val.astype(BF16)

        def lat_only(ref):
            v = ref[...]
            return jnp.where(lat, v, jnp.zeros_like(v))

        put("qa", lat_only(dqa_ref))
        put("rb", lat_only(drb_ref))
        put("ga", lat_only(dga_ref))
        put("gb", lat_only(dgb_ref))
        put("ka", dka_ref[...])
        put("va", dva_ref[...])
        put("vb", dvg0[0] + dvg1[0])
        put("qb", dqg0[0] + dqg1[0])
        put("kb", dkg0[0] + dkg1[0])
        put("lr", dlr_ref[...])

    lat_spec = lambda a: pl.BlockSpec((tm, a.shape[1]), lambda i: (jnp.maximum(i - cb, 0), 0))
    all_spec = lambda a: pl.BlockSpec((tm, a.shape[1]), lambda i: (i, 0))
    dir_specs = lambda a: [pl.BlockSpec((1, tm, a.shape[2]), lambda i: (0, i, 0)),
                           pl.BlockSpec((1, tm, a.shape[2]), lambda i: (1, i, 0))]
    return pl.pallas_call(
        body, name=name, grid=(R // tm,),
        in_specs=[lat_spec(dqa), lat_spec(drb), lat_spec(dga), lat_spec(dgb), all_spec(dka), all_spec(dva)]
        + dir_specs(dvg) + dir_specs(dqg) + dir_specs(dkg) + [all_spec(dlr)],
        out_specs=pl.BlockSpec((tm, Z), lambda i: (i, 0)),
        out_shape=jax.ShapeDtypeStruct((R, Z), BF16), compiler_params=_cparams("parallel"),
    )(dqa, drb, dga, dgb, dka, dva, dvg, dvg, dqg, dqg, dkg, dkg, dlr)


def _mod_fwd(ca, w, b, name):
    n, D = ca.shape
    N = w.shape[1]
    tn = _pick(N, 512)

    def body(c_ref, w_ref, b_ref, o_ref, s_ref):
        s = _silu(c_ref[...])
        s_ref[...] = s
        o_ref[...] = _dot(s.astype(BF16), w_ref[...].astype(BF16), NN) + b_ref[...]

    return pl.pallas_call(
        body, name=name, grid=(N // tn,),
        in_specs=[pl.BlockSpec((n, D), lambda j: (0, 0)), pl.BlockSpec((D, tn), lambda j: (0, j)),
                  pl.BlockSpec((1, tn), lambda j: (0, j))],
        out_specs=[pl.BlockSpec((n, tn), lambda j: (0, j)), pl.BlockSpec((n, D), lambda j: (0, 0))],
        out_shape=[jax.ShapeDtypeStruct((n, N), F32), jax.ShapeDtypeStruct((n, D), F32)],
        compiler_params=_cparams("arbitrary"))(ca, w, b)


def _silu_bwd(dsil, ca, name):
    def body(d_ref, c_ref, o_ref):
        o_ref[...] = d_ref[...] * _dsilu(c_ref[...])

    return pl.pallas_call(body, name=name, out_shape=jax.ShapeDtypeStruct(ca.shape, F32))(dsil, ca)


def _adam_math(w, g, m, v):
    c1 = 1.0 - ADAM_B1 ** ADAM_STEP
    c2 = 1.0 - ADAM_B2 ** ADAM_STEP
    mn = ADAM_B1 * m + (1.0 - ADAM_B1) * g
    vn = ADAM_B2 * v + (1.0 - ADAM_B2) * (g * g)
    return -ADAM_LR * ((mn / c1) / (jnp.sqrt(vn / c2) + ADAM_EPS) + ADAM_WD * w), mn, vn


def _adamw(w, g, m, v, name):
    Rw, Cw = w.shape
    tr = _pick(Rw, 128, 8)

    def body(w_ref, g_ref, m_ref, v_ref, d_ref, mo_ref, vo_ref):
        d_ref[...], mo_ref[...], vo_ref[...] = _adam_math(w_ref[...], g_ref[...], m_ref[...], v_ref[...])

    row = pl.BlockSpec((tr, Cw), lambda i: (i, 0))
    sh = jax.ShapeDtypeStruct((Rw, Cw), F32)
    return pl.pallas_call(body, name=name, grid=(Rw // tr,), in_specs=[row] * 4, out_specs=[row] * 3,
                          out_shape=[sh] * 3, compiler_params=_cparams("parallel"))(w, g, m, v)


HBM_SPEC = pl.BlockSpec(memory_space=pltpu.HBM)


def _exchange(inputs, out_shapes, stages, name):
    n_in, n_out = len(inputs), len(out_shapes)
    n = sum(len(s) for s in stages)

    def body(*refs):
        ins, outs = refs[:n_in], refs[n_in:n_in + n_out]
        send_sems, recv_sems = refs[n_in + n_out:]
        me = (lax.axis_index("x"), lax.axis_index("y"), lax.axis_index("c"))
        k = 0
        for stage in stages:
            copies = []
            for (skind, sidx), sfn, didx, dfn, flip in stage:
                src = (ins if skind == "in" else outs)[sidx].at[sfn(*me)]
                dst = outs[didx].at[dfn(*me)]
                if flip == (0, 0, 0):
                    cp = pltpu.make_async_copy(src, dst, send_sems.at[k])
                else:
                    peer = tuple(1 - a if f else a for a, f in zip(me, flip))
                    cp = pltpu.make_async_remote_copy(src, dst, send_sems.at[k], recv_sems.at[k],
                                                      device_id=peer, device_id_type=MESH)
                cp.start()
                copies.append(cp)
                k += 1
            for cp in copies:
                cp.wait()

    return pl.pallas_call(
        body, name=name, in_specs=[HBM_SPEC] * n_in, out_specs=[HBM_SPEC] * n_out, out_shape=out_shapes,
        scratch_shapes=[pltpu.SemaphoreType.DMA((n,)), pltpu.SemaphoreType.DMA((n,))],
    )(*inputs)


FLIPS_ALL = [(0, 0, 1), (0, 1, 0), (0, 1, 1), (1, 0, 0), (1, 0, 1), (1, 1, 0), (1, 1, 1)]
FLIPS_CHIP = [(0, 1, 0), (1, 0, 0), (1, 1, 0)]


def _sum_slots(buf, name):
    n, r, w = buf.shape
    tr = _pick(r, 256, 8)

    def body(b_ref, o_ref):
        acc = b_ref[0]
        for s in range(1, n):
            acc = acc + b_ref[s]
        o_ref[...] = acc

    return pl.pallas_call(
        body, name=name, grid=(r // tr,), in_specs=[pl.BlockSpec((n, tr, w), lambda i: (0, i, 0))],
        out_specs=pl.BlockSpec((tr, w), lambda i: (i, 0)), out_shape=jax.ShapeDtypeStruct((r, w), F32),
        compiler_params=_cparams("parallel"))(buf)


def _allreduce(buf, name):
    r, w = buf.shape
    whole = lambda x, y, c: (slice(None), slice(None))
    slot = lambda x, y, c: (4 * x + 2 * y + c,)
    stage = [(("in", 0), whole, 0, slot, f) for f in [(0, 0, 0)] + FLIPS_ALL]
    (slots,) = _exchange([buf], [jax.ShapeDtypeStruct((8, r, w), F32)], [stage], name + "_x")
    return _sum_slots(slots, name + "_sum")


def _allgather_weights(shards, name):
    half = lambda a, c: pl.ds(c * (a.shape[0] // 2), a.shape[0] // 2)
    first, second = [], []
    for n, a in enumerate(shards):
        for f in FLIPS_CHIP:
            first.append((("in", n), lambda x, y, c, a=a: (half(a, c), slice(None)), n,
                          lambda x, y, c, a=a: (2 * x + y, half(a, c), slice(None)), f))
            peer_slot = lambda x, y, c, a=a, f=f: (2 * (x ^ f[0]) + (y ^ f[1]), half(a, c), slice(None))
            second.append((("out", n), peer_slot, n, peer_slot, (0, 0, 1)))
    outs = [jax.ShapeDtypeStruct((4,) + a.shape, a.dtype) for a in shards]
    return _exchange(shards, outs, [first, second], name)


def _add_pair(G, bufA, cvec, name):
    _, Rs, Cs = G.shape
    Rh = Rs // 2
    tr = _pick(Rh, 128, 16)
    nb = Rh // tr

    def body(c_ref, g_ref, a_ref, o_ref):
        o_ref[...] = (g_ref[...] + a_ref[...]).astype(BF16)

    grid_spec = pltpu.PrefetchScalarGridSpec(
        num_scalar_prefetch=1, grid=(4, nb),
        in_specs=[pl.BlockSpec((1, tr, Cs), lambda s, i, c_ref: (s, c_ref[0] * nb + i, 0)),
                  pl.BlockSpec((1, tr, Cs), lambda s, i, c_ref: (s, i, 0))],
        out_specs=pl.BlockSpec((1, tr, Cs), lambda s, i, c_ref: (s, i, 0)))
    return pl.pallas_call(body, name=name, grid_spec=grid_spec, out_shape=jax.ShapeDtypeStruct((4, Rh, Cs), BF16),
                          compiler_params=_cparams("parallel", "parallel"))(cvec, G, bufA)


def _sum_chips(G, bufA, bufB, cvec, svec, name):
    _, Rs, Cs = G.shape
    Rh = Rs // 2
    tr = _pick(Rh, 128, 16)
    nb = Rh // tr

    def body(c_ref, s_ref, g_ref, a_ref, b_ref, o_ref):
        o_ref[...] = (g_ref[0] + a_ref[0]) + b_ref[0].astype(F32) + b_ref[1].astype(F32) + b_ref[2].astype(F32)

    grid_spec = pltpu.PrefetchScalarGridSpec(
        num_scalar_prefetch=2, grid=(nb,),
        in_specs=[pl.BlockSpec((1, tr, Cs), lambda i, c, s: (s[0], c[0] * nb + i, 0)),
                  pl.BlockSpec((1, tr, Cs), lambda i, c, s: (s[0], i, 0)),
                  pl.BlockSpec((3, tr, Cs), lambda i, c, s: (0, i, 0))],
        out_specs=pl.BlockSpec((tr, Cs), lambda i, c, s: (i, 0)))
    return pl.pallas_call(body, name=name, grid_spec=grid_spec, out_shape=jax.ShapeDtypeStruct((Rh, Cs), F32),
                          compiler_params=_cparams("parallel"))(cvec, svec, G, bufA, bufB)


def _reduce_scatter(grads, cvec, svec, name):
    ng = len(grads)
    Rh = [g.shape[1] // 2 for g in grads]
    whole3 = lambda x, y, c: (slice(None), slice(None), slice(None))
    whole2 = lambda x, y, c: (slice(None), slice(None))
    st = [(("in", n), lambda x, y, c, n=n: (slice(None), pl.ds((1 - c) * Rh[n], Rh[n]), slice(None)), n,
           whole3, (0, 0, 1)) for n in range(ng)]
    bufA = _exchange(grads, [jax.ShapeDtypeStruct((4, Rh[n], g.shape[2]), F32) for n, g in enumerate(grads)],
                     [st], name + "_pair")
    P = [_add_pair(g, a, cvec, "%s_add%d" % (name, n)) for n, (g, a) in enumerate(zip(grads, bufA))]
    st = [(("in", n), lambda x, y, c, f=f: (2 * (x ^ f[0]) + (y ^ f[1]),), n, lambda x, y, c, k=k: (k,), f)
          for n in range(ng) for k, f in enumerate(FLIPS_CHIP)]
    bufB = _exchange(P, [jax.ShapeDtypeStruct((3,) + p.shape[1:], BF16) for p in P], [st], name + "_chips")
    mine = [_sum_chips(g, a, b, cvec, svec, "%s_sum%d" % (name, n)) for n, (g, a, b) in enumerate(zip(grads, bufA, bufB))]
    st = [(("in", n), whole2, n, whole2, (0, 0, 1)) for n in range(ng)]
    other = _exchange(mine, [jax.ShapeDtypeStruct(r.shape, F32) for r in mine], [st], name + "_halves")
    return mine, other


def _adamw_halves(w, mine, other, m, v, cvec, name):
    Rs, Cs = w.shape
    Rh = Rs // 2
    tr = _pick(Rh, 128, 8)
    nb = Rh // tr

    def body(c_ref, w_ref, a_ref, b_ref, m_ref, v_ref, g_ref, d_ref, mo_ref, vo_ref):
        gv = jnp.where(pl.program_id(0) // nb == c_ref[0], a_ref[...], b_ref[...])
        g_ref[...] = gv
        d_ref[...], mo_ref[...], vo_ref[...] = _adam_math(w_ref[...], gv, m_ref[...], v_ref[...])

    row = pl.BlockSpec((tr, Cs), lambda i, c: (i, 0))
    hrow = pl.BlockSpec((tr, Cs), lambda i, c: (i % nb, 0))
    grid_spec = pltpu.PrefetchScalarGridSpec(num_scalar_prefetch=1, grid=(2 * nb,),
                                             in_specs=[row, hrow, hrow, row, row], out_specs=[row] * 4)
    return pl.pallas_call(body, name=name, grid_spec=grid_spec, out_shape=[jax.ShapeDtypeStruct((Rs, Cs), F32)] * 4,
                          compiler_params=_cparams("parallel"))(cvec, w, mine, other, m, v)


def _pack(arrays):
    flat = [a.reshape(-1).astype(F32) for a in arrays]
    meta, off = [], 0
    for a, f in zip(arrays, flat):
        meta.append((off, a.shape))
        off += f.shape[0]
    total = -(-off // (8 * LANES)) * (8 * LANES)
    flat.append(jnp.zeros((total - off,), F32))
    return jnp.concatenate(flat).reshape(total // LANES, LANES), meta


def _unpack(buf, meta):
    flat = buf.reshape(-1)
    out = []
    for off, shape in meta:
        size = 1
        for s in shape:
            size *= s
        out.append(flat[off:off + size].reshape(shape))
    return out


WEIGHT_NAMES = ["c_ctx", "w_mod", "b_mod", "g_mix", "w_in", "q_norm", "k_norm", "attn_sink", "w_gate_f", "b_gate_f",
                "w_gate_b", "b_gate_b", "gla_norm", "w_attn_o", "w_gla_o", "w_out", "g_ffn", "w_up", "conv_w",
                "conv_b", "w_down"]
BIG_NAMES = ["w_in", "w_attn_o", "w_gla_o", "w_out", "w_up", "w_down"]
SHARDED_SMALL = ["w_gate_f", "w_gate_b", "conv_w"]


def _layouts(D):
    aw, kvw, gk, gv = N_Q_HEADS * HEAD_DIM, N_KV_HEADS * HEAD_DIM, D // 2, D
    widths = {"qa": aw, "ka": kvw, "va": kvw, "qb": gk, "kb": gk, "vb": gv, "rb": gv, "lr": 2 * GLA_LOWRANK,
              "ga": D, "gb": D}
    orig, off = {}, 0
    for s in ["qa", "ka", "va", "qb", "kb", "vb", "rb", "lr", "ga", "gb"]:
        orig[s] = off
        off += widths[s]
    order = ["qa", "vb", "rb", "ga", "gb", "ka", "va", "qb", "kb", "lr"]
    lay, off = {}, 0
    for s in order:
        lay[s] = off
        off += LANES if s == "lr" else widths[s]
    align = {"qa": aw, "vb": D, "rb": D, "ga": D, "gb": D, "ka": kvw, "va": kvw, "qb": gk, "kb": gk,
             "lr": LANES}
    for s in order:
        assert lay[s] % align[s] == 0, (s, lay[s], align[s])
    return widths, orig, order, lay, off


def _rope_tables(T, L):
    t = jnp.arange(T)
    nf = HEAD_DIM // 4
    inv = ROPE_THETA ** (-jnp.arange(nf, dtype=F32) / nf)
    ang = jnp.concatenate([(t // GRID_W)[:, None] * inv, (t % GRID_W)[:, None] * inv], axis=-1)
    cos, sin = jnp.cos(ang), jnp.sin(ang)
    cos2 = jnp.concatenate([jnp.ones((L, HEAD_DIM), F32), jnp.concatenate([cos, cos], axis=-1)], axis=0)
    sin2 = jnp.concatenate([jnp.zeros((L, HEAD_DIM), F32), jnp.concatenate([-sin, sin], axis=-1)], axis=0)
    return cos2, sin2


def _step(x, c, ctx, loss_target, W, M, V):
    xi, yi, ci = lax.axis_index("x"), lax.axis_index("y"), lax.axis_index("c")
    chip = 2 * xi + yi
    dev = 2 * chip + ci
    south = (ci == 0).astype(F32)
    cvec = ci.reshape(1).astype(jnp.int32)
    T, D = x.shape[1], x.shape[2]
    L = ctx.shape[1]
    R = L + T
    F = 4 * W["w_down"].shape[1]
    GK, GV = D // 2, D
    DK, DV = GK // GLA_HEADS, GV // GLA_HEADS
    N6 = 6 * D
    N4 = N6 // 4
    widths, orig, order, lay, Z = _layouts(D)

    def place_cols(shard, full_cols):
        cols = shard.shape[-1]
        full = jnp.zeros(shard.shape[:-1] + (full_cols,), F32)
        return lax.dynamic_update_slice(full, shard * south, (0,) * (shard.ndim - 1) + (chip * cols,))

    c_rows = lax.dynamic_update_slice(jnp.zeros((8, D), F32), c, (dev, 0))
    bufa, meta = _pack([c_rows, place_cols(W["w_gate_f"][0], GK), place_cols(W["w_gate_b"][0], GK),
                        place_cols(W["conv_w"][0], 2 * F)])
    c_all, wgf, wgb, cw = _unpack(_allreduce(bufa, "gather_small"), meta)
    ca = jnp.concatenate([c_all, W["c_ctx"][None, :], jnp.zeros((7, D), F32)], axis=0)
    b_shard = lax.dynamic_slice(W["b_mod"], (0, chip * N4), (1, N4))
    mod_part, sil = _mod_fwd(ca, W["w_mod"][0], b_shard, "mod_fwd")
    slots = lax.dynamic_update_slice(jnp.zeros((4, 16, N4), F32), (mod_part * south)[None], (chip, 0, 0))
    mod_all = _allreduce(slots.reshape(64, N4), "gather_mod").reshape(4, 16, N4).transpose(1, 0, 2).reshape(16, N6)
    mx = lax.dynamic_slice(mod_all, (dev, 0), (1, N6)).reshape(6, 1, D)
    mc = mod_all[8].reshape(6, 1, D)

    shards = [W[n][0].astype(BF16) for n in BIG_NAMES]
    g_in, g_ao, g_go, g_out, g_up, g_dn = [
        lax.dynamic_update_slice(g, s[None], (chip, 0, 0))
        for g, s in zip(_allgather_weights(shards, "gather_weights"), shards)]
    cols = lambda g: g.transpose(1, 0, 2).reshape(g.shape[1], 4 * g.shape[2])
    rows = lambda g: g.reshape(4 * g.shape[1], g.shape[2])
    w_in_f = cols(g_in)
    seg = lambda s: w_in_f[:, orig[s]:orig[s] + widths[s]]
    w_cat = jnp.concatenate([jnp.pad(seg(s), ((0, 0), (0, LANES - widths[s]))) if s == "lr" else seg(s)
                             for s in order], axis=1)
    w_ao, w_go, w_out, w_up, w_dn = rows(g_ao), rows(g_go), rows(g_out), cols(g_up), rows(g_dn)
    wg = jnp.zeros((2, LANES, GK), F32).at[0, :GLA_LOWRANK].set(wgf).at[1, GLA_LOWRANK:2 * GLA_LOWRANK].set(wgb)
    bg = jnp.stack([W["b_gate_f"], W["b_gate_b"]])
    cb = W["conv_b"]
    sink_rows = jnp.broadcast_to(W["attn_sink"][0][:, None], (N_Q_HEADS, HEAD_DIM))
    cos2, sin2 = _rope_tables(T, L)
    blk = lambda s, w: lay[s] // w

    xall = jnp.concatenate([ctx[0], x[0]], axis=0)
    sc1 = jnp.stack([mc[1], mx[1]])
    sh1 = jnp.stack([mc[0], mx[0]])
    h = _modnorm_fwd(xall, W["g_mix"], sc1, sh1, L, "modnorm1")
    z = _matmul(h, w_cat, "nn", F32, "proj_in")
    qn = _qknorm_fwd(z, blk("qa", widths["qa"]), T, L, W["q_norm"], cos2, sin2, N_Q_HEADS, "qnorm")
    kn = _qknorm_fwd(z, blk("ka", widths["ka"]), R, 0, W["k_norm"], cos2, sin2, N_KV_HEADS, "knorm")
    vb = _cast_seg(z, blk("va", widths["va"]), widths["va"], "vcast")
    o_attn = _attn_fwd(qn, kn, vb, sink_rows, L, "attn_fwd")
    gla_blks = (blk("qb", GK), blk("kb", GK), blk("vb", GV), blk("lr", LANES))
    o_g, sprev = _gla_fwd(z, *gla_blks, wg, bg, DV, L, "gla_fwd")
    p = _glanorm_fwd(o_g, z, blk("rb", D), W["gla_norm"], L, "glanorm")
    ya = _matmul(o_attn, w_ao, "nn", F32, "proj_attn_o")
    yg = _matmul(p, w_go, "nn", F32, "proj_gla_o")
    m = _gate_fwd(z, blk("ga", D), blk("gb", D), ya, yg, L, "gate")
    mix = _matmul(m, w_out, "nn", F32, "proj_out")
    x1, h2 = _resnorm_fwd(x[0], mix, mx[2], W["g_ffn"], mx[4], mx[3], "resnorm2")
    u = _matmul(h2, w_up, "nn", F32, "ffn_up")
    f = _conv_fwd(u, cw, cb, "conv_swiglu")
    d = _matmul(f, w_dn, "nn", F32, "ffn_down")
    dy, lacc = _loss_head(d, x1, mx[5], loss_target[0], "loss_head")
    loss = lax.psum((0.5 / D) * jnp.sum(lacc[0]), ("x", "y", "c"))

    dd, s_gt2 = _gate_resid_bwd(dy, d, mx[5], "gate2_bwd")
    gw_dn = _matmul(f, dd, "tn", F32, "ffn_down_dw")
    df = _matmul(dd, w_dn, "nt", F32, "ffn_down_dx")
    dca, dcg, acca, accg = _conv_bwd_a(u, df, cw, cb, "conv_swiglu_bwd")
    du = jnp.concatenate([_conv_bwd_b(dca, cw, 0, "conv_t_a"), _conv_bwd_b(dcg, cw, 1, "conv_t_g")], axis=1)
    gw_up = _matmul(h2, du, "tn", F32, "ffn_up_dw")
    dh2 = _matmul(du, w_up, "nt", F32, "ffn_up_dx")
    dx1, s2 = _modnorm_bwd(x1, dh2, W["g_ffn"], mx[4], dy, "resnorm2_bwd")
    dmix, s_gt1 = _gate_resid_bwd(dx1, mix, mx[2], "gate1_bwd")
    gw_out = _matmul(m, dmix, "tn", F32, "proj_out_dw")
    dm = _matmul(dmix, w_out, "nt", F32, "proj_out_dx")
    dya, dyg, dga, dgb = _gate_bwd(z, blk("ga", D), blk("gb", D), ya, yg, dm, L, "gate_bwd")
    gw_ao = _matmul(o_attn, dya, "tn", F32, "proj_attn_o_dw")
    do_attn = _matmul(dya, w_ao, "nt", BF16, "proj_attn_o_dx")
    gw_go = _matmul(p, dyg, "tn", F32, "proj_gla_o_dw")
    dp = _matmul(dyg, w_go, "nt", F32, "proj_gla_o_dx")
    do_gla, drb, s_gn = _glanorm_bwd(o_g, z, blk("rb", D), W["gla_norm"], dp, L, "glanorm_bwd")
    do_pad = jnp.concatenate([jnp.zeros((L, GV), BF16), do_gla], axis=0)
    dqg, dkg, dvg, dpre, dbg = _gla_bwd(z, *gla_blks, wg, bg, sprev, do_pad, L, "gla_bwd")
    wg_cat = jnp.concatenate([wg[0], wg[1]], axis=1)
    dlr = _matmul(dpre, wg_cat, "nt", BF16, "gla_gate_dx")
    dwg = _matmul(z[:, lay["lr"]:lay["lr"] + LANES], dpre, "tn", F32, "gla_gate_dw")
    dqn, dkw, dvw, dkc, dvc, dsn = _attn_bwd(qn, kn, vb, sink_rows, do_attn, L, "attn_bwd")
    dqa, s_qn = _qknorm_bwd(z, blk("qa", widths["qa"]), T, L, W["q_norm"], cos2, sin2, dqn, N_Q_HEADS, "qnorm_bwd")
    dk_all = jnp.concatenate([dkc, dkw[WINDOW:WINDOW + T]], axis=0)
    dv_all = jnp.concatenate([dvc, dvw[WINDOW:WINDOW + T]], axis=0)
    dka, s_kn = _qknorm_bwd(z, blk("ka", widths["ka"]), R, 0, W["k_norm"], cos2, sin2, dk_all, N_KV_HEADS, "knorm_bwd")
    dz = _assemble_dz(lay, Z, L, dqa, drb, dga, dgb, dka, dv_all, dvg, dqg, dkg, dlr, "assemble_dz")
    gw_cat = _matmul(h, dz, "tn", F32, "proj_in_dw")
    dh = _matmul(dz, w_cat, "nt", F32, "proj_in_dx")
    grad_x, s1 = _modnorm_bwd(x[0], dh[L:], W["g_mix"], mx[1], dx1, "modnorm1_bwd")
    _, s1c = _modnorm_bwd(ctx[0], dh[:L], W["g_mix"], mc[1], None, "modnorm1_ctx_bwd")

    dmod_x = jnp.concatenate([s1[0], s1[1], s_gt1[0], s2[0], s2[1], s_gt2[0]])
    dmod_c = jnp.concatenate([s1c[0], s1c[1], jnp.zeros((4 * D,), F32)])
    dmod_rows = lax.dynamic_update_slice(jnp.zeros((9, N6), F32).at[8].set(dmod_c), dmod_x[None], (dev, 0))
    small = [dmod_rows, dmod_x + dmod_c, s1[2] + s1c[2], s_qn[0], s_kn[0], dsn[:, 0, :Q_PER_KV].reshape(N_Q_HEADS),
             dwg[:GLA_LOWRANK, :GK], dbg[0].reshape(GK), dwg[GLA_LOWRANK:2 * GLA_LOWRANK, GK:], dbg[1].reshape(GK),
             s_gn[0], s2[2], jnp.concatenate([acca[0:3], accg[0:3]], axis=1), jnp.concatenate([acca[3], accg[3]])]
    bufc, meta = _pack(small)
    (dmod_sum, g_b_mod, g_g_mix, g_q_norm, g_k_norm, g_sink, g_wgf, g_bgf, g_wgb, g_bgb, g_gla_norm, g_g_ffn,
     g_conv_w, g_conv_b) = _unpack(_allreduce(bufc, "reduce_small"), meta)
    dmod16 = lax.dynamic_slice(jnp.concatenate([dmod_sum, jnp.zeros((7, N6), F32)], axis=0), (0, chip * N4), (16, N4))
    g_w_mod = _matmul(sil, dmod16, "tn", F32, "mod_dw")
    dsil = _matmul(dmod16, W["w_mod"][0], "nt", F32, "mod_dx")
    g_c_ctx = _silu_bwd(_allreduce(dsil * south, "reduce_cctx"), ca, "silu_bwd")[8]

    gw_in = jnp.concatenate([gw_cat[:, lay[s]:lay[s] + widths[s]] for s in ["qa", "ka", "va", "qb", "kb", "vb", "rb",
                                                                           "lr", "ga", "gb"]], axis=1)
    by_cols = lambda g: g.reshape(g.shape[0], 4, g.shape[1] // 4).transpose(1, 0, 2)
    by_rows = lambda g: g.reshape(4, g.shape[0] // 4, g.shape[1])
    svec = chip.reshape(1).astype(jnp.int32)
    mine, other = _reduce_scatter([by_cols(gw_in), by_rows(gw_ao), by_rows(gw_go), by_rows(gw_out), by_cols(gw_up),
                                   by_rows(gw_dn)], cvec, svec, "reduce_big")
    cut = lambda g: lax.dynamic_slice(g, (0, chip * (g.shape[1] // 4)), (g.shape[0], g.shape[1] // 4))
    grads = {"c_ctx": g_c_ctx, "w_mod": g_w_mod[None], "b_mod": g_b_mod[None], "g_mix": g_g_mix[None],
             "q_norm": g_q_norm[None], "k_norm": g_k_norm[None], "attn_sink": g_sink[None],
             "w_gate_f": cut(g_wgf)[None], "b_gate_f": g_bgf[None], "w_gate_b": cut(g_wgb)[None],
             "b_gate_b": g_bgb[None], "gla_norm": g_gla_norm[None], "g_ffn": g_g_ffn[None],
             "conv_w": cut(g_conv_w)[None], "conv_b": g_conv_b[None]}

    delta, new_m, new_v = {}, {}, {}
    dl, mn, vn = _adamw(W["w_mod"][0], g_w_mod, M["w_mod"][0], V["w_mod"][0], "adamw_w_mod")
    delta["w_mod"], new_m["w_mod"], new_v["w_mod"] = dl[None], mn[None], vn[None]
    for n, a, b in zip(BIG_NAMES, mine, other):
        g, dl, mn, vn = _adamw_halves(W[n][0], a, b, M[n][0], V[n][0], cvec, "adamw_" + n)
        grads[n], delta[n], new_m[n], new_v[n] = g[None], dl[None], mn[None], vn[None]
    small_names = [n for n in WEIGHT_NAMES if n not in delta]
    packs = [_pack([src[n] for n in small_names]) for src in (W, grads, M, V)]
    meta = packs[0][1]
    outs = _adamw(packs[0][0], packs[1][0], packs[2][0], packs[3][0], "adamw_small")
    for res, o in zip((delta, new_m, new_v), outs):
        for n, a in zip(small_names, _unpack(o, meta)):
            res[n] = a
    return (loss, grad_x[None], *[grads[n] for n in WEIGHT_NAMES], *[delta[n] for n in WEIGHT_NAMES],
            *[new_m[n] for n in WEIGHT_NAMES], *[new_v[n] for n in WEIGHT_NAMES])


def kernel(x, c, ctx, c_ctx, w_mod, b_mod, g_mix, w_in, q_norm, k_norm, attn_sink, w_gate_f, b_gate_f, w_gate_b, b_gate_b, gla_norm, w_attn_o, w_gla_o, w_out, g_ffn, w_up, conv_w, conv_b, w_down, loss_target, m_c_ctx, m_w_mod, m_b_mod, m_g_mix, m_w_in, m_q_norm, m_k_norm, m_attn_sink, m_w_gate_f, m_b_gate_f, m_w_gate_b, m_b_gate_b, m_gla_norm, m_w_attn_o, m_w_gla_o, m_w_out, m_g_ffn, m_w_up, m_conv_w, m_conv_b, m_w_down, v_c_ctx, v_w_mod, v_b_mod, v_g_mix, v_w_in, v_q_norm, v_k_norm, v_attn_sink, v_w_gate_f, v_b_gate_f, v_w_gate_b, v_b_gate_b, v_gla_norm, v_w_attn_o, v_w_gla_o, v_w_out, v_g_ffn, v_w_up, v_conv_w, v_conv_b, v_w_down):
    W = dict(zip(WEIGHT_NAMES, (c_ctx, w_mod, b_mod, g_mix, w_in, q_norm, k_norm, attn_sink, w_gate_f, b_gate_f,
                                w_gate_b, b_gate_b, gla_norm, w_attn_o, w_gla_o, w_out, g_ffn, w_up, conv_w, conv_b,
                                w_down)))
    M = dict(zip(WEIGHT_NAMES, (m_c_ctx, m_w_mod, m_b_mod, m_g_mix, m_w_in, m_q_norm, m_k_norm, m_attn_sink,
                                m_w_gate_f, m_b_gate_f, m_w_gate_b, m_b_gate_b, m_gla_norm, m_w_attn_o, m_w_gla_o,
                                m_w_out, m_g_ffn, m_w_up, m_conv_w, m_conv_b, m_w_down)))
    V = dict(zip(WEIGHT_NAMES, (v_c_ctx, v_w_mod, v_b_mod, v_g_mix, v_w_in, v_q_norm, v_k_norm, v_attn_sink,
                                v_w_gate_f, v_b_gate_f, v_w_gate_b, v_b_gate_b, v_gla_norm, v_w_attn_o, v_w_gla_o,
                                v_w_out, v_g_ffn, v_w_up, v_conv_w, v_conv_b, v_w_down)))
    return _step(x, c, ctx, loss_target, W, M, V)
```

```python
import functools
import math

import jax
import jax.numpy as jnp
from jax import lax
from jax.experimental import pallas as pl
from jax.experimental.pallas import tpu as pltpu

F32 = jnp.float32
BF16 = jnp.bfloat16
MESH = pl.DeviceIdType.MESH

EPS = 1e-6
HEAD_DIM = 128
N_Q_HEADS = 16
N_KV_HEADS = 4
Q_PER_KV = N_Q_HEADS // N_KV_HEADS
WINDOW = 128
GLA_HEADS = 4
GLA_LOWRANK = 16
GLA_GATE_NORM = 16.0
GLA_CHUNK = 64
GRID_W = 64
ROPE_THETA = 10000.0
GLA_LEVELS = (32, 16, 8, 4, 2, 1)
LANES = 128

ADAM_LR = 0.001
ADAM_B1 = 0.9
ADAM_B2 = 0.999
ADAM_EPS = 1e-08
ADAM_WD = 0.01
ADAM_STEP = 10

VMEM_LIMIT = 52 * 1024 * 1024


def _cparams(*sem):
    return pltpu.CompilerParams(dimension_semantics=sem, vmem_limit_bytes=VMEM_LIMIT)


def _pick(n, target, mult=LANES):
    best = None
    d = mult
    while d <= min(n, target):
        if n % d == 0:
            best = d
        d += mult
    return n if best is None else best


def _sigmoid(x):
    return 1.0 / (1.0 + jnp.exp(-x))


def _silu(x):
    return x * _sigmoid(x)


def _dsilu(x):
    s = _sigmoid(x)
    return s * (1.0 + x * (1.0 - s))


def _dot(a, b, dims):
    return lax.dot_general(a, b, (dims, ((), ())), preferred_element_type=F32)


NN = ((1,), (0,))
NT = ((1,), (1,))
TN = ((0,), (0,))


def _matmul(a, b, mode, out_dtype, name, tm=768, tn=1024, tk=2048):
    if mode == "nn":
        (M, K), (K2, N) = a.shape, b.shape
    elif mode == "nt":
        (M, K), (N, K2) = a.shape, b.shape
    else:
        (K, M), (K2, N) = a.shape, b.shape
    assert K == K2, (name, a.shape, b.shape)
    if mode == "tn":
        tm = max(tm, 1024)
    tm, tn, tk = _pick(M, tm), _pick(N, tn), _pick(K, tk)
    nk = K // tk
    dims = {"nn": NN, "nt": NT, "tn": TN}[mode]

    def body(a_ref, b_ref, o_ref, acc_ref):
        k = pl.program_id(2)

        @pl.when(k == 0)
        def _():
            acc_ref[...] = jnp.zeros_like(acc_ref)

        acc_ref[...] += _dot(a_ref[...].astype(BF16), b_ref[...].astype(BF16), dims)

        @pl.when(k == nk - 1)
        def _():
            o_ref[...] = acc_ref[...].astype(out_dtype)

    if mode == "tn":
        a_spec = pl.BlockSpec((tk, tm), lambda i, j, k: (k, i))
    else:
        a_spec = pl.BlockSpec((tm, tk), lambda i, j, k: (i, k))
    if mode == "nt":
        b_spec = pl.BlockSpec((tn, tk), lambda i, j, k: (j, k))
    else:
        b_spec = pl.BlockSpec((tk, tn), lambda i, j, k: (k, j))
    return pl.pallas_call(
        body, name=name, grid=(M // tm, N // tn, nk),
        in_specs=[a_spec, b_spec],
        out_specs=pl.BlockSpec((tm, tn), lambda i, j, k: (i, j)),
        out_shape=jax.ShapeDtypeStruct((M, N), out_dtype),
        scratch_shapes=[pltpu.VMEM((tm, tn), F32)],
        compiler_params=_cparams("parallel", "parallel", "arbitrary"),
    )(a, b)


def _modnorm_fwd(xall, g, sc, sh, n_ctx, name):
    R, D = xall.shape
    tm = _pick(n_ctx, 256, 8)
    cb = n_ctx // tm

    def body(x_ref, g_ref, sc_ref, sh_ref, h_ref):
        x = x_ref[...]
        r = lax.rsqrt(jnp.mean(x * x, axis=-1, keepdims=True) + EPS)
        n = x * r * g_ref[...]
        h_ref[...] = (n * (1.0 + sc_ref[0]) + sh_ref[0]).astype(BF16)

    sel = lambda i: (jnp.where(i < cb, 0, 1), 0, 0)
    return pl.pallas_call(
        body, name=name, grid=(R // tm,),
        in_specs=[pl.BlockSpec((tm, D), lambda i: (i, 0)), pl.BlockSpec((1, D), lambda i: (0, 0)),
                  pl.BlockSpec((1, 1, D), sel), pl.BlockSpec((1, 1, D), sel)],
        out_specs=pl.BlockSpec((tm, D), lambda i: (i, 0)),
        out_shape=jax.ShapeDtypeStruct((R, D), BF16),
        compiler_params=_cparams("parallel"),
    )(xall, g, sc, sh)


def _modnorm_bwd(x, dh, g, sc, resid, name, dh_roff=0):
    N, D = x.shape
    tm = _pick(math.gcd(N, dh_roff), 256, 8)
    ro = dh_roff // tm
    want_dx = resid is not None

    def body(*refs):
        if want_dx:
            x_ref, dh_ref, g_ref, sc_ref, res_ref, dx_ref, acc_ref = refs
        else:
            x_ref, dh_ref, g_ref, sc_ref, acc_ref = refs
        i = pl.program_id(0)

        @pl.when(i == 0)
        def _():
            acc_ref[...] = jnp.zeros_like(acc_ref)

        xv, dhv, gv = x_ref[...], dh_ref[...], g_ref[...]
        r = lax.rsqrt(jnp.mean(xv * xv, axis=-1, keepdims=True) + EPS)
        xh = xv * r
        dn = dhv * (1.0 + sc_ref[...])
        acc_ref[0:1, :] += jnp.sum(dhv, axis=0, keepdims=True)
        acc_ref[1:2, :] += jnp.sum(dhv * xh * gv, axis=0, keepdims=True)
        acc_ref[2:3, :] += jnp.sum(dn * xh, axis=0, keepdims=True)
        if want_dx:
            dxh = dn * gv
            dx_ref[...] = res_ref[...] + r * (dxh - xh * jnp.mean(dxh * xh, axis=-1, keepdims=True))

    row = pl.BlockSpec((tm, D), lambda i: (i, 0))
    drow = pl.BlockSpec((tm, D), lambda i: (i + ro, 0))
    vec = pl.BlockSpec((1, D), lambda i: (0, 0))
    acc = pl.BlockSpec((8, D), lambda i: (0, 0))
    acc_shape = jax.ShapeDtypeStruct((8, D), F32)
    if want_dx:
        return pl.pallas_call(
            body, name=name, grid=(N // tm,), in_specs=[row, drow, vec, vec, row],
            out_specs=[row, acc], out_shape=[jax.ShapeDtypeStruct((N, D), F32), acc_shape],
            compiler_params=_cparams("arbitrary"))(x, dh, g, sc, resid)
    sums = pl.pallas_call(
        body, name=name, grid=(N // tm,), in_specs=[row, drow, vec, vec],
        out_specs=acc, out_shape=acc_shape, compiler_params=_cparams("arbitrary"))(x, dh, g, sc)
    return None, sums


def _qknorm_fwd(z, cblk, nrows, roff, w, cos2, sin2, nh, name):
    W = nh * HEAD_DIM
    tm = _pick(math.gcd(nrows, roff), 256, 8)
    ro = roff // tm
    assert roff % tm == 0

    def body(z_ref, w_ref, c_ref, s_ref, o_ref):
        c, s, wv = c_ref[...], s_ref[...], w_ref[...]
        for h in range(nh):
            x = z_ref[:, h * HEAD_DIM:(h + 1) * HEAD_DIM]
            r = lax.rsqrt(jnp.mean(x * x, axis=-1, keepdims=True) + EPS)
            y = x * r * wv
            o_ref[:, h * HEAD_DIM:(h + 1) * HEAD_DIM] = (y * c + pltpu.roll(y, HEAD_DIM // 2, 1) * s).astype(BF16)

    return pl.pallas_call(
        body, name=name, grid=(nrows // tm,),
        in_specs=[pl.BlockSpec((tm, W), lambda i: (i + ro, cblk)), pl.BlockSpec((1, HEAD_DIM), lambda i: (0, 0)),
                  pl.BlockSpec((tm, HEAD_DIM), lambda i: (i + ro, 0)), pl.BlockSpec((tm, HEAD_DIM), lambda i: (i + ro, 0))],
        out_specs=pl.BlockSpec((tm, W), lambda i: (i, 0)),
        out_shape=jax.ShapeDtypeStruct((nrows, W), BF16),
        compiler_params=_cparams("parallel"),
    )(z, w, cos2, sin2)


def _qknorm_bwd(z, cblk, nrows, roff, w, cos2, sin2, dy, nh, name):
    W = nh * HEAD_DIM
    tm = _pick(math.gcd(nrows, roff), 256, 8)
    ro = roff // tm

    def body(z_ref, w_ref, c_ref, s_ref, dy_ref, dz_ref, acc_ref):
        i = pl.program_id(0)

        @pl.when(i == 0)
        def _():
            acc_ref[...] = jnp.zeros_like(acc_ref)

        c, s, wv = c_ref[...], s_ref[...], w_ref[...]
        dw = jnp.zeros((1, HEAD_DIM), F32)
        for h in range(nh):
            sl = slice(h * HEAD_DIM, (h + 1) * HEAD_DIM)
            x = z_ref[:, sl]
            d = dy_ref[:, sl]
            dyn = d * c + pltpu.roll(d * s, HEAD_DIM // 2, 1)
            r = lax.rsqrt(jnp.mean(x * x, axis=-1, keepdims=True) + EPS)
            xh = x * r
            dw = dw + jnp.sum(dyn * xh, axis=0, keepdims=True)
            dxh = dyn * wv
            dz_ref[:, sl] = (r * (dxh - xh * jnp.mean(dxh * xh, axis=-1, keepdims=True))).astype(BF16)
        acc_ref[0:1, :] += dw

    return pl.pallas_call(
        body, name=name, grid=(nrows // tm,),
        in_specs=[pl.BlockSpec((tm, W), lambda i: (i + ro, cblk)), pl.BlockSpec((1, HEAD_DIM), lambda i: (0, 0)),
                  pl.BlockSpec((tm, HEAD_DIM), lambda i: (i + ro, 0)), pl.BlockSpec((tm, HEAD_DIM), lambda i: (i + ro, 0)),
                  pl.BlockSpec((tm, W), lambda i: (i, 0))],
        out_specs=[pl.BlockSpec((tm, W), lambda i: (i, 0)), pl.BlockSpec((8, HEAD_DIM), lambda i: (0, 0))],
        out_shape=[jax.ShapeDtypeStruct((nrows, W), BF16), jax.ShapeDtypeStruct((8, HEAD_DIM), F32)],
        compiler_params=_cparams("arbitrary"),
    )(z, w, cos2, sin2, dy)


def _cast_seg(z, cblk, width, name):
    R = z.shape[0]
    tm = _pick(R, 512, 8)

    def body(z_ref, o_ref):
        o_ref[...] = z_ref[...].astype(BF16)

    return pl.pallas_call(
        body, name=name, grid=(R // tm,),
        in_specs=[pl.BlockSpec((tm, width), lambda i: (i, cblk))],
        out_specs=pl.BlockSpec((tm, width), lambda i: (i, 0)),
        out_shape=jax.ShapeDtypeStruct((R, width), BF16), compiler_params=_cparams("parallel"))(z)


NEG_BIG = -1e30


def _attn_specs(T, n_ctx):
    nb = T // WINDOW
    lb = n_ctx // WINDOW
    blk = lambda f: pl.BlockSpec((WINDOW, HEAD_DIM), f)
    win = [blk(lambda h, i: (lb + jnp.maximum(i - 1, 0), h)), blk(lambda h, i: (lb + i, h)),
           blk(lambda h, i: (lb + jnp.minimum(i + 1, nb - 1), h))]
    ctx = pl.BlockSpec((n_ctx, HEAD_DIM), lambda h, i: (0, h))
    qspec = pl.BlockSpec((WINDOW, Q_PER_KV * HEAD_DIM), lambda h, i: (i, h))
    sink = pl.BlockSpec((N_Q_HEADS, HEAD_DIM), lambda h, i: (0, 0))
    return nb, qspec, win, ctx, sink


def _attn_probs(q, kw, kctx, snk, valid):
    scale = HEAD_DIM ** -0.5
    s_lat = jnp.where(valid, _dot(q, kw, NT) * scale, NEG_BIG)
    s_ctx = _dot(q, kctx, NT) * scale
    m = jnp.maximum(jnp.maximum(jnp.max(s_lat, axis=-1, keepdims=True), jnp.max(s_ctx, axis=-1, keepdims=True)), snk)
    p_lat = jnp.exp(s_lat - m)
    p_ctx = jnp.exp(s_ctx - m)
    p_snk = jnp.exp(snk - m)
    den = p_snk + jnp.sum(p_lat, axis=-1, keepdims=True) + jnp.sum(p_ctx, axis=-1, keepdims=True)
    return p_lat, p_ctx, p_snk, den


def _attn_valid(i, T):
    qpos = i * WINDOW + lax.broadcasted_iota(jnp.int32, (WINDOW, 3 * WINDOW), 0)
    kpos = (i - 1) * WINDOW + lax.broadcasted_iota(jnp.int32, (WINDOW, 3 * WINDOW), 1)
    return (jnp.abs(qpos - kpos) <= WINDOW) & (kpos >= 0) & (kpos < T)


def _attn_fwd(qn, kn, vb, sink_rows, n_ctx, name):
    T = qn.shape[0]
    nb, qspec, win, ctx, sink = _attn_specs(T, n_ctx)

    def body(q_ref, kp, kc, kx, vp, vc, vx, kctx_ref, vctx_ref, sink_ref, o_ref):
        h, i = pl.program_id(0), pl.program_id(1)
        kw = jnp.concatenate([kp[...], kc[...], kx[...]], axis=0)
        vw = jnp.concatenate([vp[...], vc[...], vx[...]], axis=0)
        kctx, vctx = kctx_ref[...], vctx_ref[...]
        valid = _attn_valid(i, T)
        for g in range(Q_PER_KV):
            sl = slice(g * HEAD_DIM, (g + 1) * HEAD_DIM)
            snk = sink_ref[pl.ds(h * Q_PER_KV + g, 1), :][:, 0:1]
            p_lat, p_ctx, _, den = _attn_probs(q_ref[:, sl], kw, kctx, snk, valid)
            o = (_dot(p_lat.astype(BF16), vw, NN) + _dot(p_ctx.astype(BF16), vctx, NN)) / den
            o_ref[:, sl] = o.astype(BF16)

    return pl.pallas_call(
        body, name=name, grid=(N_KV_HEADS, nb),
        in_specs=[qspec] + win + win + [ctx, ctx, sink],
        out_specs=qspec, out_shape=jax.ShapeDtypeStruct(qn.shape, BF16),
        compiler_params=_cparams("parallel", "parallel"),
    )(qn, kn, kn, kn, vb, vb, vb, kn, vb, sink_rows)


def _attn_bwd(qn, kn, vb, sink_rows, do, n_ctx, name):
    T = qn.shape[0]
    nb, qspec, win, ctx, sink = _attn_specs(T, n_ctx)
    scale = HEAD_DIM ** -0.5
    TP = T + 2 * WINDOW

    def body(q_ref, kp, kc, kx, vp, vc, vx, kctx_ref, vctx_ref, sink_ref, do_ref,
             dq_ref, dkw_ref, dvw_ref, dkc_ref, dvc_ref, dsn_ref):
        h, i = pl.program_id(0), pl.program_id(1)

        @pl.when(i == 0)
        def _():
            dkw_ref[...] = jnp.zeros_like(dkw_ref)
            dvw_ref[...] = jnp.zeros_like(dvw_ref)
            dkc_ref[...] = jnp.zeros_like(dkc_ref)
            dvc_ref[...] = jnp.zeros_like(dvc_ref)
            dsn_ref[...] = jnp.zeros_like(dsn_ref)

        kw = jnp.concatenate([kp[...], kc[...], kx[...]], axis=0)
        vw = jnp.concatenate([vp[...], vc[...], vx[...]], axis=0)
        kctx, vctx = kctx_ref[...], vctx_ref[...]
        valid = _attn_valid(i, T)
        lane = lax.broadcasted_iota(jnp.int32, (8, HEAD_DIM), 1)
        dkw = jnp.zeros((3 * WINDOW, HEAD_DIM), F32)
        dvw = jnp.zeros((3 * WINDOW, HEAD_DIM), F32)
        dkc = jnp.zeros(kctx.shape, F32)
        dvc = jnp.zeros(kctx.shape, F32)
        dsn = jnp.zeros((8, HEAD_DIM), F32)
        for g in range(Q_PER_KV):
            sl = slice(g * HEAD_DIM, (g + 1) * HEAD_DIM)
            snk = sink_ref[pl.ds(h * Q_PER_KV + g, 1), :][:, 0:1]
            q, d_o = q_ref[:, sl], do_ref[:, sl]
            p_lat, p_ctx, p_snk, den = _attn_probs(q, kw, kctx, snk, valid)
            inv = 1.0 / den
            p_lat, p_ctx, p_snk = p_lat * inv, p_ctx * inv, p_snk * inv
            dp_lat = _dot(d_o, vw, NT)
            dp_ctx = _dot(d_o, vctx, NT)
            dr = jnp.sum(p_lat * dp_lat, axis=-1, keepdims=True) + jnp.sum(p_ctx * dp_ctx, axis=-1, keepdims=True)
            ds_lat = (p_lat * (dp_lat - dr) * scale).astype(BF16)
            ds_ctx = (p_ctx * (dp_ctx - dr) * scale).astype(BF16)
            dq_ref[:, sl] = _dot(ds_lat, kw, NN) + _dot(ds_ctx, kctx, NN)
            dkw = dkw + _dot(ds_lat, q, TN)
            dvw = dvw + _dot(p_lat.astype(BF16), d_o, TN)
            dkc = dkc + _dot(ds_ctx, q, TN)
            dvc = dvc + _dot(p_ctx.astype(BF16), d_o, TN)
            dsn = dsn + jnp.where(lane == g, -jnp.sum(p_snk * dr, axis=0, keepdims=True), 0.0)
        rows = pl.ds(pl.multiple_of(i * WINDOW, WINDOW), 3 * WINDOW)
        dkw_ref[rows, :] += dkw
        dvw_ref[rows, :] += dvw
        dkc_ref[...] += dkc
        dvc_ref[...] += dvc
        dsn_ref[0] += dsn

    wacc = pl.BlockSpec((TP, HEAD_DIM), lambda h, i: (0, h))
    return pl.pallas_call(
        body, name=name, grid=(N_KV_HEADS, nb),
        in_specs=[qspec] + win + win + [ctx, ctx, sink, qspec],
        out_specs=[qspec, wacc, wacc, ctx, ctx, pl.BlockSpec((1, 8, HEAD_DIM), lambda h, i: (h, 0, 0))],
        out_shape=[jax.ShapeDtypeStruct(qn.shape, F32),
                   jax.ShapeDtypeStruct((TP, N_KV_HEADS * HEAD_DIM), F32),
                   jax.ShapeDtypeStruct((TP, N_KV_HEADS * HEAD_DIM), F32),
                   jax.ShapeDtypeStruct((n_ctx, N_KV_HEADS * HEAD_DIM), F32),
                   jax.ShapeDtypeStruct((n_ctx, N_KV_HEADS * HEAD_DIM), F32),
                   jax.ShapeDtypeStruct((N_KV_HEADS, 8, HEAD_DIM), F32)],
        compiler_params=_cparams("arbitrary", "arbitrary"),
    )(qn, kn, kn, kn, vb, vb, vb, kn, vb, sink_rows, do)


def _gla_masks(dirv):
    C = GLA_CHUNK
    r = lax.broadcasted_iota(jnp.int32, (C, C), 0)
    c = lax.broadcasted_iota(jnp.int32, (C, C), 1)
    tt = jnp.where(dirv == 0, r, C - 1 - r)
    ss = jnp.where(dirv == 0, c, C - 1 - c)
    le = (ss <= tt).astype(jnp.int32)
    sums = [le == 1, le == 0]
    blocks = [ss == tt]
    for m in GLA_LEVELS:
        sh = m.bit_length() - 1
        same = (tt >> (sh + 1)) == (ss >> (sh + 1))
        ut = (tt >> sh) & 1
        us = (ss >> sh) & 1
        sums.append(same & (ut == us) & (ut == le))
        blocks.append(same & (ut == 1) & (us == 0))
    mall = jnp.concatenate([jnp.where(s, 1.0, 0.0) for s in sums], axis=0).astype(BF16)
    return mall, blocks


def _split3(x):
    hi = x.astype(BF16)
    r1 = x - hi.astype(F32)
    mid = r1.astype(BF16)
    lo = (r1 - mid.astype(F32)).astype(BF16)
    return hi, mid, lo


def _dot3(m_bf16, x, dims):
    hi, mid, lo = _split3(x)
    return _dot(m_bf16, hi, dims) + _dot(m_bf16, mid, dims) + _dot(m_bf16, lo, dims)


def _gla_chunk_of(dirv, j, lc, nc):
    return jnp.where(dirv == 0, j, jnp.where(j < lc, lc - 1 - j, nc + lc - 1 - j))


def _gla_gate(lr_ref, wg_ref, bg_ref):
    pre = _dot(lr_ref[...].astype(BF16), wg_ref[0].astype(BF16), NN) + bg_ref[0]
    g = (jnp.minimum(pre, 0.0) - jnp.log(1.0 + jnp.exp(-jnp.abs(pre)))) * (1.0 / GLA_GATE_NORM)
    return pre, g


def _gla_fwd(z, qblk, kblk, vblk, lrblk, wg, bg, DV, n_ctx, name):
    R = z.shape[0]
    C = GLA_CHUNK
    DK = wg.shape[2] // GLA_HEADS
    nc, lc = R // C, n_ctx // C
    qscale = DK ** -0.5

    GK, GV = GLA_HEADS * DK, GLA_HEADS * DV

    def body(q_ref, k_ref, v_ref, lr_ref, wg_ref, bg_ref, o_ref, sp_ref, st_ref):
        dirv, j = pl.program_id(0), pl.program_id(1)

        @pl.when(j == 0)
        def _():
            st_ref[...] = jnp.zeros_like(st_ref)

        mall, blocks = _gla_masks(dirv)
        _, g_all = _gla_gate(lr_ref, wg_ref, bg_ref)
        E_all = _dot3(mall, g_all, NN)
        for h in range(GLA_HEADS):
            ks, vs = slice(h * DK, (h + 1) * DK), slice(h * DV, (h + 1) * DV)
            q, k, v = q_ref[:, ks] * qscale, k_ref[:, ks], v_ref[:, vs].astype(BF16)
            g, E = g_all[:, ks], E_all[:, ks]
            st = st_ref[h]
            sp_ref[0, h, 0] = st
            A = jnp.where(blocks[0], _dot(q.astype(BF16), k.astype(BF16), NT), 0.0)
            for l in range(len(GLA_LEVELS)):
                e = jnp.exp(E[(2 + l) * C:(3 + l) * C])
                A = A + jnp.where(blocks[l + 1], _dot((q * e).astype(BF16), (k * e).astype(BF16), NT), 0.0)
            o_ref[0, :, vs] = (_dot((q * jnp.exp(E[0:C])).astype(BF16), st.astype(BF16), NT)
                               + _dot(A.astype(BF16), v, NN))
            decay = jnp.exp(jnp.sum(g, axis=0, keepdims=True))
            st_ref[h] = decay * st + _dot(v, (k * jnp.exp(E[C:2 * C])).astype(BF16), TN)

    chunk = functools.partial(_gla_chunk_of, lc=lc, nc=nc)
    return pl.pallas_call(
        body, name=name, grid=(2, nc),
        in_specs=[pl.BlockSpec((C, GK), lambda d, j: (chunk(d, j), qblk)),
                  pl.BlockSpec((C, GK), lambda d, j: (chunk(d, j), kblk)),
                  pl.BlockSpec((C, GV), lambda d, j: (chunk(d, j), vblk)),
                  pl.BlockSpec((C, LANES), lambda d, j: (chunk(d, j), lrblk)),
                  pl.BlockSpec((1, LANES, GK), lambda d, j: (d, 0, 0)),
                  pl.BlockSpec((1, 1, GK), lambda d, j: (d, 0, 0))],
        out_specs=[pl.BlockSpec((1, C, GV), lambda d, j: (d, chunk(d, j), 0)),
                   pl.BlockSpec((1, GLA_HEADS, 1, DV, DK), lambda d, j: (d, 0, j, 0, 0))],
        out_shape=[jax.ShapeDtypeStruct((2, R, GV), F32),
                   jax.ShapeDtypeStruct((2, GLA_HEADS, nc, DV, DK), F32)],
        scratch_shapes=[pltpu.VMEM((GLA_HEADS, DV, DK), F32)],
        compiler_params=_cparams("parallel", "arbitrary"),
    )(z, z, z, z, wg, bg)


def _gla_bwd(z, qblk, kblk, vblk, lrblk, wg, bg, sprev, do, n_ctx, name):
    R = z.shape[0]
    C = GLA_CHUNK
    DK, DV = wg.shape[2] // GLA_HEADS, do.shape[1] // GLA_HEADS
    nc, lc = R // C, n_ctx // C
    qscale = DK ** -0.5
    nl = len(GLA_LEVELS)

    GK, GV = GLA_HEADS * DK, GLA_HEADS * DV

    def body(q_ref, k_ref, v_ref, lr_ref, wg_ref, bg_ref, sp_ref, do_ref,
             dq_ref, dk_ref, dv_ref, dpre_ref, dbg_ref, dst_ref):
        dirv, jr = pl.program_id(0), pl.program_id(1)

        @pl.when(jr == 0)
        def _():
            dst_ref[...] = jnp.zeros_like(dst_ref)
            dbg_ref[...] = jnp.zeros_like(dbg_ref)

        mall, blocks = _gla_masks(dirv)
        pre_all, g_all = _gla_gate(lr_ref, wg_ref, bg_ref)
        E_all = _dot3(mall, g_all, NN)
        for h in range(GLA_HEADS):
            ks, vs = slice(h * DK, (h + 1) * DK), slice(h * DV, (h + 1) * DV)
            q, k, v = q_ref[:, ks] * qscale, k_ref[:, ks], v_ref[:, vs].astype(BF16)
            pre, g, E = pre_all[:, ks], g_all[:, ks], E_all[:, ks]
            eb, er = jnp.exp(E[0:C]), jnp.exp(E[C:2 * C])
            decay = jnp.exp(jnp.sum(g, axis=0, keepdims=True))
            st = sp_ref[0, h, 0]
            dst = dst_ref[h]
            d_o = do_ref[:, vs]
            qe, kd = q * eb, k * er
            qb, kb = q.astype(BF16), k.astype(BF16)
            A = jnp.where(blocks[0], _dot(qb, kb, NT), 0.0)
            for l in range(nl):
                e = jnp.exp(E[(2 + l) * C:(3 + l) * C])
                A = A + jnp.where(blocks[l + 1], _dot((q * e).astype(BF16), (k * e).astype(BF16), NT), 0.0)
            dA = _dot(d_o, v, NT)
            dv_ref[0, :, vs] = _dot(A.astype(BF16), d_o, TN) + _dot(kd.astype(BF16), dst.astype(BF16), NT)
            dqe = _dot(d_o, st.astype(BF16), NN)
            dkd = _dot(v, dst.astype(BF16), NN)
            G = jnp.where(blocks[0], dA, 0.0).astype(BF16)
            dq = dqe * eb + _dot(G, kb, NN)
            dk = dkd * er + _dot(G, qb, TN)
            dE = [dqe * qe, dkd * kd]
            for l in range(nl):
                e = jnp.exp(E[(2 + l) * C:(3 + l) * C])
                ql, kl = q * e, k * e
                G = jnp.where(blocks[l + 1], dA, 0.0).astype(BF16)
                dql = _dot(G, kl.astype(BF16), NN)
                dkl = _dot(G, ql.astype(BF16), TN)
                dq = dq + dql * e
                dk = dk + dkl * e
                dE.append(dql * ql + dkl * kl)
            dlast = jnp.sum(dst * st, axis=0, keepdims=True) * decay
            dg = _dot3(mall, jnp.concatenate(dE, axis=0), TN) + dlast
            dpre = dg * (1.0 / GLA_GATE_NORM) / (1.0 + jnp.exp(pre))
            dq_ref[0, :, ks] = dq * qscale
            dk_ref[0, :, ks] = dk
            dpre_ref[:, ks] = dpre.astype(BF16)
            dbg_ref[0, :, ks] += jnp.sum(dpre, axis=0, keepdims=True)
            dst_ref[h] = decay * dst + _dot(d_o, qe.astype(BF16), TN)

    def chunk(d, jr):
        return _gla_chunk_of(d, nc - 1 - jr, lc, nc)

    return pl.pallas_call(
        body, name=name, grid=(2, nc),
        in_specs=[pl.BlockSpec((C, GK), lambda d, j: (chunk(d, j), qblk)),
                  pl.BlockSpec((C, GK), lambda d, j: (chunk(d, j), kblk)),
                  pl.BlockSpec((C, GV), lambda d, j: (chunk(d, j), vblk)),
                  pl.BlockSpec((C, LANES), lambda d, j: (chunk(d, j), lrblk)),
                  pl.BlockSpec((1, LANES, GK), lambda d, j: (d, 0, 0)),
                  pl.BlockSpec((1, 1, GK), lambda d, j: (d, 0, 0)),
                  pl.BlockSpec((1, GLA_HEADS, 1, DV, DK), lambda d, j: (d, 0, nc - 1 - j, 0, 0)),
                  pl.BlockSpec((C, GV), lambda d, j: (chunk(d, j), 0))],
        out_specs=[pl.BlockSpec((1, C, GK), lambda d, j: (d, chunk(d, j), 0)),
                   pl.BlockSpec((1, C, GK), lambda d, j: (d, chunk(d, j), 0)),
                   pl.BlockSpec((1, C, GV), lambda d, j: (d, chunk(d, j), 0)),
                   pl.BlockSpec((C, GK), lambda d, j: (chunk(d, j), d)),
                   pl.BlockSpec((1, 1, GK), lambda d, j: (d, 0, 0))],
        out_shape=[jax.ShapeDtypeStruct((2, R, GK), F32),
                   jax.ShapeDtypeStruct((2, R, GK), F32),
                   jax.ShapeDtypeStruct((2, R, GV), F32),
                   jax.ShapeDtypeStruct((R, 2 * GK), BF16),
                   jax.ShapeDtypeStruct((2, 1, GK), F32)],
        scratch_shapes=[pltpu.VMEM((GLA_HEADS, DV, DK), F32)],
        compiler_params=_cparams("arbitrary", "arbitrary"),
    )(z, z, z, z, wg, bg, sprev, do)


def _glanorm_fwd(o, z, rbblk, gn, n_ctx, name):
    _, R, GV = o.shape
    T = R - n_ctx
    DV = GV // GLA_HEADS
    tm = _pick(n_ctx, 256, 8)
    ro = n_ctx // tm

    def body(o0_ref, o1_ref, rb_ref, gn_ref, p_ref):
        gnv = gn_ref[...]
        for h in range(GLA_HEADS):
            sl = slice(h * DV, (h + 1) * DV)
            og = o0_ref[0, :, sl] + o1_ref[0, :, sl]
            r = lax.rsqrt(jnp.mean(og * og, axis=-1, keepdims=True) + EPS)
            p_ref[:, sl] = (og * r * gnv * _silu(rb_ref[:, sl])).astype(BF16)

    return pl.pallas_call(
        body, name=name, grid=(T // tm,),
        in_specs=[pl.BlockSpec((1, tm, GV), lambda i: (0, i + ro, 0)), pl.BlockSpec((1, tm, GV), lambda i: (1, i + ro, 0)),
                  pl.BlockSpec((tm, GV), lambda i: (i + ro, rbblk)), pl.BlockSpec((1, DV), lambda i: (0, 0))],
        out_specs=pl.BlockSpec((tm, GV), lambda i: (i, 0)),
        out_shape=jax.ShapeDtypeStruct((T, GV), BF16), compiler_params=_cparams("parallel"))(o, o, z, gn)


def _glanorm_bwd(o, z, rbblk, gn, dp, n_ctx, name):
    _, R, GV = o.shape
    T = R - n_ctx
    DV = GV // GLA_HEADS
    tm = _pick(n_ctx, 256, 8)
    ro = n_ctx // tm

    def body(o0_ref, o1_ref, rb_ref, gn_ref, dp_ref, do_ref, drb_ref, acc_ref):
        i = pl.program_id(0)

        @pl.when(i == 0)
        def _():
            acc_ref[...] = jnp.zeros_like(acc_ref)

        gnv = gn_ref[...]
        dgn = jnp.zeros((1, DV), F32)
        for h in range(GLA_HEADS):
            sl = slice(h * DV, (h + 1) * DV)
            og = o0_ref[0, :, sl] + o1_ref[0, :, sl]
            rb = rb_ref[:, sl]
            d = dp_ref[:, sl]
            r = lax.rsqrt(jnp.mean(og * og, axis=-1, keepdims=True) + EPS)
            xh = og * r
            drb_ref[:, sl] = (d * xh * gnv * _dsilu(rb)).astype(BF16)
            dn = d * _silu(rb)
            dgn = dgn + jnp.sum(dn * xh, axis=0, keepdims=True)
            dxh = dn * gnv
            do_ref[:, sl] = (r * (dxh - xh * jnp.mean(dxh * xh, axis=-1, keepdims=True))).astype(BF16)
        acc_ref[0:1, :] += dgn

    row = pl.BlockSpec((tm, GV), lambda i: (i, 0))
    return pl.pallas_call(
        body, name=name, grid=(T // tm,),
        in_specs=[pl.BlockSpec((1, tm, GV), lambda i: (0, i + ro, 0)), pl.BlockSpec((1, tm, GV), lambda i: (1, i + ro, 0)),
                  pl.BlockSpec((tm, GV), lambda i: (i + ro, rbblk)), pl.BlockSpec((1, DV), lambda i: (0, 0)), row],
        out_specs=[row, row, pl.BlockSpec((8, DV), lambda i: (0, 0))],
        out_shape=[jax.ShapeDtypeStruct((T, GV), BF16), jax.ShapeDtypeStruct((T, GV), BF16),
                   jax.ShapeDtypeStruct((8, DV), F32)],
        compiler_params=_cparams("arbitrary"))(o, o, z, gn, dp)


def _gate_fwd(z, gablk, gbblk, ya, yg, n_ctx, name):
    T, D = ya.shape
    tm = _pick(n_ctx, 256, 8)
    ro = n_ctx // tm

    def body(ga_ref, gb_ref, ya_ref, yg_ref, m_ref):
        m_ref[...] = (_sigmoid(ga_ref[...]) * ya_ref[...] + _sigmoid(gb_ref[...]) * yg_ref[...]).astype(BF16)

    row = pl.BlockSpec((tm, D), lambda i: (i, 0))
    return pl.pallas_call(
        body, name=name, grid=(T // tm,),
        in_specs=[pl.BlockSpec((tm, D), lambda i: (i + ro, gablk)), pl.BlockSpec((tm, D), lambda i: (i + ro, gbblk)), row, row],
        out_specs=row, out_shape=jax.ShapeDtypeStruct((T, D), BF16), compiler_params=_cparams("parallel"))(z, z, ya, yg)


def _gate_bwd(z, gablk, gbblk, ya, yg, dm, n_ctx, name):
    T, D = ya.shape
    tm = _pick(n_ctx, 256, 8)
    ro = n_ctx // tm

    def body(ga_ref, gb_ref, ya_ref, yg_ref, dm_ref, dya_ref, dyg_ref, dga_ref, dgb_ref):
        d = dm_ref[...]
        sa, sb = _sigmoid(ga_ref[...]), _sigmoid(gb_ref[...])
        dya_ref[...] = (d * sa).astype(BF16)
        dyg_ref[...] = (d * sb).astype(BF16)
        dga_ref[...] = (d * ya_ref[...] * sa * (1.0 - sa)).astype(BF16)
        dgb_ref[...] = (d * yg_ref[...] * sb * (1.0 - sb)).astype(BF16)

    row = pl.BlockSpec((tm, D), lambda i: (i, 0))
    sh = jax.ShapeDtypeStruct((T, D), BF16)
    return pl.pallas_call(
        body, name=name, grid=(T // tm,),
        in_specs=[pl.BlockSpec((tm, D), lambda i: (i + ro, gablk)), pl.BlockSpec((tm, D), lambda i: (i + ro, gbblk)), row, row, row],
        out_specs=[row] * 4, out_shape=[sh] * 4, compiler_params=_cparams("parallel"))(z, z, ya, yg, dm)


def _resnorm_fwd(x, mix, gt, g, sc, sh, name):
    T, D = x.shape
    tm = _pick(T, 256, 8)

    def body(x_ref, mix_ref, gt_ref, g_ref, sc_ref, sh_ref, x1_ref, h_ref):
        x1 = x_ref[...] + gt_ref[...] * mix_ref[...]
        x1_ref[...] = x1
        r = lax.rsqrt(jnp.mean(x1 * x1, axis=-1, keepdims=True) + EPS)
        h_ref[...] = (x1 * r * g_ref[...] * (1.0 + sc_ref[...]) + sh_ref[...]).astype(BF16)

    row = pl.BlockSpec((tm, D), lambda i: (i, 0))
    vec = pl.BlockSpec((1, D), lambda i: (0, 0))
    return pl.pallas_call(
        body, name=name, grid=(T // tm,), in_specs=[row, row, vec, vec, vec, vec], out_specs=[row, row],
        out_shape=[jax.ShapeDtypeStruct((T, D), F32), jax.ShapeDtypeStruct((T, D), BF16)],
        compiler_params=_cparams("parallel"))(x, mix, gt, g, sc, sh)


def _gate_resid_bwd(dx, val, gt, name):
    T, D = dx.shape
    tm = _pick(T, 256, 8)

    def body(dx_ref, val_ref, gt_ref, d_ref, acc_ref):
        i = pl.program_id(0)

        @pl.when(i == 0)
        def _():
            acc_ref[...] = jnp.zeros_like(acc_ref)

        d = dx_ref[...]
        d_ref[...] = (d * gt_ref[...]).astype(BF16)
        acc_ref[0:1, :] += jnp.sum(d * val_ref[...], axis=0, keepdims=True)

    row = pl.BlockSpec((tm, D), lambda i: (i, 0))
    return pl.pallas_call(
        body, name=name, grid=(T // tm,), in_specs=[row, row, pl.BlockSpec((1, D), lambda i: (0, 0))],
        out_specs=[row, pl.BlockSpec((8, D), lambda i: (0, 0))],
        out_shape=[jax.ShapeDtypeStruct((T, D), BF16), jax.ShapeDtypeStruct((8, D), F32)],
        compiler_params=_cparams("arbitrary"))(dx, val, gt)


def _loss_head(d, x1, gt, target, name):
    T, D = d.shape
    tm = _pick(T, 256, 8)

    def body(d_ref, x1_ref, gt_ref, t_ref, dy_ref, acc_ref):
        i = pl.program_id(0)

        @pl.when(i == 0)
        def _():
            acc_ref[...] = jnp.zeros_like(acc_ref)

        e = x1_ref[...] + gt_ref[...] * d_ref[...] - t_ref[...]
        dy_ref[...] = e * (1.0 / D)
        acc_ref[0:1, :] += jnp.sum(e * e, axis=0, keepdims=True)

    row = pl.BlockSpec((tm, D), lambda i: (i, 0))
    return pl.pallas_call(
        body, name=name, grid=(T // tm,), in_specs=[row, row, pl.BlockSpec((1, D), lambda i: (0, 0)), row],
        out_specs=[row, pl.BlockSpec((8, D), lambda i: (0, 0))],
        out_shape=[jax.ShapeDtypeStruct((T, D), F32), jax.ShapeDtypeStruct((8, D), F32)],
        compiler_params=_cparams("arbitrary"))(d, x1, gt, target)


def _halo_specs(T, tm, tw, col_of, order):
    n8 = tm // 8
    if order == "ij":
        mid = lambda i, j: (i, col_of(j))
        prev = lambda i, j: (jnp.maximum(i * n8 - 1, 0), col_of(j))
        nxt = lambda i, j: (jnp.minimum((i + 1) * n8, T // 8 - 1), col_of(j))
    else:
        mid = lambda j, i: (i, col_of(j))
        prev = lambda j, i: (jnp.maximum(i * n8 - 1, 0), col_of(j))
        nxt = lambda j, i: (jnp.minimum((i + 1) * n8, T // 8 - 1), col_of(j))
    return [pl.BlockSpec((tm, tw), mid), pl.BlockSpec((8, tw), prev), pl.BlockSpec((8, tw), nxt)]


def _shifted(u_ref, up_ref, un_ref, i, nt):
    u = u_ref[...]
    tm = u.shape[0]
    row = lax.broadcasted_iota(jnp.int32, u.shape, 0)
    hp = jnp.where(i > 0, up_ref[7:8, :], 0.0)
    hn = jnp.where(i < nt - 1, un_ref[0:1, :], 0.0)
    u_prev = jnp.where(row == 0, hp, pltpu.roll(u, 1, 0))
    u_next = jnp.where(row == tm - 1, hn, pltpu.roll(u, tm - 1, 0))
    return u_prev, u, u_next


def _conv_fwd(u, cw, cb, name):
    T, F2 = u.shape
    F = F2 // 2
    tm, tw = _pick(T, 256, 8), _pick(F, 512)
    nt, nw = T // tm, F // tw

    def body(ua, uap, uan, ug, ugp, ugn, cwa, cwg, cba, cbg, f_ref):
        i = pl.program_id(0)

        def conv(u_ref, up_ref, un_ref, w_ref, b_ref):
            p, m, n = _shifted(u_ref, up_ref, un_ref, i, nt)
            return p * w_ref[0:1, :] + m * w_ref[1:2, :] + n * w_ref[2:3, :] + b_ref[...]

        a = conv(ua, uap, uan, cwa, cba)
        g = conv(ug, ugp, ugn, cwg, cbg)
        f_ref[...] = (_silu(a) * g).astype(BF16)

    wspec = lambda off: pl.BlockSpec((3, tw), lambda i, j: (0, j + off))
    bspec = lambda off: pl.BlockSpec((1, tw), lambda i, j: (0, j + off))
    return pl.pallas_call(
        body, name=name, grid=(nt, nw),
        in_specs=_halo_specs(T, tm, tw, lambda j: j, "ij") + _halo_specs(T, tm, tw, lambda j: j + nw, "ij")
        + [wspec(0), wspec(nw), bspec(0), bspec(nw)],
        out_specs=pl.BlockSpec((tm, tw), lambda i, j: (i, j)),
        out_shape=jax.ShapeDtypeStruct((T, F), BF16), compiler_params=_cparams("parallel", "parallel"),
    )(u, u, u, u, u, u, cw, cw, cb, cb)


def _shift_rows(x, before, after):
    tm = x.shape[0]
    row = lax.broadcasted_iota(jnp.int32, x.shape, 0)
    return (jnp.where(row == 0, before, pltpu.roll(x, 1, 0)),
            jnp.where(row == tm - 1, after, pltpu.roll(x, tm - 1, 0)))


def _conv_bwd(u, df, cw, cb, name):
    T, F2 = u.shape
    F = F2 // 2
    tm, tw = _pick(T, 256, 8), _pick(F, 512)
    nt, nw = T // tm, F // tw

    def body(ua, uap, uan, ug, ugp, ugn, cwa, cwg, cba, cbg, df_ref, dfp, dfn, dua_ref, dug_ref, acca_ref, accg_ref):
        i = pl.program_id(1)

        @pl.when(i == 0)
        def _():
            acca_ref[...] = jnp.zeros_like(acca_ref)
            accg_ref[...] = jnp.zeros_like(accg_ref)

        first, last = i == 0, i == nt - 1
        wa, wg, ba, bg = cwa[...], cwg[...], cba[...], cbg[...]

        def conv(p, m, n, w, b):
            return p * w[0:1] + m * w[1:2] + n * w[2:3] + b

        def grads(a, g, d):
            return d * g * _dsilu(a), d * _silu(a)

        xa, xg, d = ua[...], ug[...], df_ref[...]
        sa = _shift_rows(xa, jnp.where(first, 0.0, uap[7:8, :]), jnp.where(last, 0.0, uan[0:1, :]))
        sg = _shift_rows(xg, jnp.where(first, 0.0, ugp[7:8, :]), jnp.where(last, 0.0, ugn[0:1, :]))
        da, dg = grads(conv(sa[0], xa, sa[1], wa, ba), conv(sg[0], xg, sg[1], wg, bg), d)
        da_p, dg_p = grads(conv(uap[6:7, :], uap[7:8, :], xa[0:1], wa, ba),
                           conv(ugp[6:7, :], ugp[7:8, :], xg[0:1], wg, bg), dfp[7:8, :])
        da_n, dg_n = grads(conv(xa[tm - 1:tm], uan[0:1, :], uan[1:2, :], wa, ba),
                           conv(xg[tm - 1:tm], ugn[0:1, :], ugn[1:2, :], wg, bg), dfn[0:1, :])
        ta = _shift_rows(da, jnp.where(first, 0.0, da_p), jnp.where(last, 0.0, da_n))
        tg = _shift_rows(dg, jnp.where(first, 0.0, dg_p), jnp.where(last, 0.0, dg_n))
        dua_ref[...] = (ta[1] * wa[0:1] + da * wa[1:2] + ta[0] * wa[2:3]).astype(BF16)
        dug_ref[...] = (tg[1] * wg[0:1] + dg * wg[1:2] + tg[0] * wg[2:3]).astype(BF16)
        for t, (va, vg) in enumerate(((sa[0], sg[0]), (xa, xg), (sa[1], sg[1]))):
            acca_ref[t:t + 1, :] += jnp.sum(da * va, axis=0, keepdims=True)
            accg_ref[t:t + 1, :] += jnp.sum(dg * vg, axis=0, keepdims=True)
        acca_ref[3:4, :] += jnp.sum(da, axis=0, keepdims=True)
        accg_ref[3:4, :] += jnp.sum(dg, axis=0, keepdims=True)

    wspec = lambda off: pl.BlockSpec((3, tw), lambda j, i: (0, j + off))
    bspec = lambda off: pl.BlockSpec((1, tw), lambda j, i: (0, j + off))
    row = pl.BlockSpec((tm, tw), lambda j, i: (i, j))
    acc = pl.BlockSpec((8, tw), lambda j, i: (0, j))
    return pl.pallas_call(
        body, name=name, grid=(nw, nt),
        in_specs=_halo_specs(T, tm, tw, lambda j: j, "ji") + _halo_specs(T, tm, tw, lambda j: j + nw, "ji")
        + [wspec(0), wspec(nw), bspec(0), bspec(nw)] + _halo_specs(T, tm, tw, lambda j: j, "ji"),
        out_specs=[row, row, acc, acc],
        out_shape=[jax.ShapeDtypeStruct((T, F), BF16), jax.ShapeDtypeStruct((T, F), BF16),
                   jax.ShapeDtypeStruct((8, F), F32), jax.ShapeDtypeStruct((8, F), F32)],
        compiler_params=_cparams("parallel", "arbitrary"),
    )(u, u, u, u, u, u, cw, cw, cb, cb, df, df, df)


def _assemble_dz(lay, Z, n_ctx, dqa, drb, dga, dgb, dka, dva, dvg, dqg, dkg, dlr, name):
    T = dqa.shape[0]
    R = T + n_ctx
    tm = _pick(n_ctx, 128, 8)
    cb = n_ctx // tm

    def body(dqa_ref, drb_ref, dga_ref, dgb_ref, dka_ref, dva_ref, dvg0, dvg1, dqg0, dqg1, dkg0, dkg1, dlr_ref, o_ref):
        lat = pl.program_id(0) >= cb

        def put(seg, val):
            o_ref[:, lay[seg]:lay[seg] + val.shape[1]] = val.astype(BF16)

        def lat_only(ref):
            v = ref[...]
            return jnp.where(lat, v, jnp.zeros_like(v))

        put("qa", lat_only(dqa_ref))
        put("rb", lat_only(drb_ref))
        put("ga", lat_only(dga_ref))
        put("gb", lat_only(dgb_ref))
        put("ka", dka_ref[...])
        put("va", dva_ref[...])
        put("vb", dvg0[0] + dvg1[0])
        put("qb", dqg0[0] + dqg1[0])
        put("kb", dkg0[0] + dkg1[0])
        put("lr", dlr_ref[...])

    lat_spec = lambda a: pl.BlockSpec((tm, a.shape[1]), lambda i: (jnp.maximum(i - cb, 0), 0))
    all_spec = lambda a: pl.BlockSpec((tm, a.shape[1]), lambda i: (i, 0))
    dir_specs = lambda a: [pl.BlockSpec((1, tm, a.shape[2]), lambda i: (0, i, 0)),
                           pl.BlockSpec((1, tm, a.shape[2]), lambda i: (1, i, 0))]
    return pl.pallas_call(
        body, name=name, grid=(R // tm,),
        in_specs=[lat_spec(dqa), lat_spec(drb), lat_spec(dga), lat_spec(dgb), all_spec(dka), all_spec(dva)]
        + dir_specs(dvg) + dir_specs(dqg) + dir_specs(dkg) + [all_spec(dlr)],
        out_specs=pl.BlockSpec((tm, Z), lambda i: (i, 0)),
        out_shape=jax.ShapeDtypeStruct((R, Z), BF16), compiler_params=_cparams("parallel"),
    )(dqa, drb, dga, dgb, dka, dva, dvg, dvg, dqg, dqg, dkg, dkg, dlr)


def _mod_fwd(ca, w, b, name):
    n, D = ca.shape
    N = w.shape[1]
    tn = _pick(N, 512)

    def body(c_ref, w_ref, b_ref, o_ref, s_ref):
        s = _silu(c_ref[...])
        s_ref[...] = s
        o_ref[...] = _dot(s.astype(BF16), w_ref[...].astype(BF16), NN) + b_ref[...]

    return pl.pallas_call(
        body, name=name, grid=(N // tn,),
        in_specs=[pl.BlockSpec((n, D), lambda j: (0, 0)), pl.BlockSpec((D, tn), lambda j: (0, j)),
                  pl.BlockSpec((1, tn), lambda j: (0, j))],
        out_specs=[pl.BlockSpec((n, tn), lambda j: (0, j)), pl.BlockSpec((n, D), lambda j: (0, 0))],
        out_shape=[jax.ShapeDtypeStruct((n, N), F32), jax.ShapeDtypeStruct((n, D), F32)],
        compiler_params=_cparams("arbitrary"))(ca, w, b)


def _silu_bwd(dsil, ca, name):
    def body(d_ref, c_ref, o_ref):
        o_ref[...] = d_ref[...] * _dsilu(c_ref[...])

    return pl.pallas_call(body, name=name, out_shape=jax.ShapeDtypeStruct(ca.shape, F32))(dsil, ca)


def _adam_math(w, g, m, v):
    c1 = 1.0 - ADAM_B1 ** ADAM_STEP
    c2 = 1.0 - ADAM_B2 ** ADAM_STEP
    mn = ADAM_B1 * m + (1.0 - ADAM_B1) * g
    vn = ADAM_B2 * v + (1.0 - ADAM_B2) * (g * g)
    return -ADAM_LR * ((mn / c1) / (jnp.sqrt(vn / c2) + ADAM_EPS) + ADAM_WD * w), mn, vn


def _adamw(w, g, m, v, name):
    Rw, Cw = w.shape
    tr = _pick(Rw, 128, 8)

    def body(w_ref, g_ref, m_ref, v_ref, d_ref, mo_ref, vo_ref):
        d_ref[...], mo_ref[...], vo_ref[...] = _adam_math(w_ref[...], g_ref[...], m_ref[...], v_ref[...])

    row = pl.BlockSpec((tr, Cw), lambda i: (i, 0))
    sh = jax.ShapeDtypeStruct((Rw, Cw), F32)
    return pl.pallas_call(body, name=name, grid=(Rw // tr,), in_specs=[row] * 4, out_specs=[row] * 3,
                          out_shape=[sh] * 3, compiler_params=_cparams("parallel"))(w, g, m, v)


HBM_SPEC = pl.BlockSpec(memory_space=pltpu.HBM)


def _exchange(inputs, out_shapes, stages, name):
    n_in, n_out = len(inputs), len(out_shapes)
    n = sum(len(s) for s in stages)

    def body(*refs):
        ins, outs = refs[:n_in], refs[n_in:n_in + n_out]
        send_sems, recv_sems = refs[n_in + n_out:]
        me = (lax.axis_index("x"), lax.axis_index("y"), lax.axis_index("c"))
        k = 0
        for stage in stages:
            copies = []
            for (skind, sidx), sfn, didx, dfn, flip in stage:
                src = (ins if skind == "in" else outs)[sidx].at[sfn(*me)]
                dst = outs[didx].at[dfn(*me)]
                if flip == (0, 0, 0):
                    cp = pltpu.make_async_copy(src, dst, send_sems.at[k])
                else:
                    peer = tuple(1 - a if f else a for a, f in zip(me, flip))
                    cp = pltpu.make_async_remote_copy(src, dst, send_sems.at[k], recv_sems.at[k],
                                                      device_id=peer, device_id_type=MESH)
                cp.start()
                copies.append(cp)
                k += 1
            for cp in copies:
                cp.wait()

    return pl.pallas_call(
        body, name=name, in_specs=[HBM_SPEC] * n_in, out_specs=[HBM_SPEC] * n_out, out_shape=out_shapes,
        scratch_shapes=[pltpu.SemaphoreType.DMA((n,)), pltpu.SemaphoreType.DMA((n,))],
    )(*inputs)


FLIPS_ALL = [(0, 0, 1), (0, 1, 0), (0, 1, 1), (1, 0, 0), (1, 0, 1), (1, 1, 0), (1, 1, 1)]
FLIPS_CHIP = [(0, 1, 0), (1, 0, 0), (1, 1, 0)]


def _sum_slots(buf, name):
    n, r, w = buf.shape
    tr = _pick(r, 256, 8)

    def body(b_ref, o_ref):
        acc = b_ref[0]
        for s in range(1, n):
            acc = acc + b_ref[s]
        o_ref[...] = acc

    return pl.pallas_call(
        body, name=name, grid=(r // tr,), in_specs=[pl.BlockSpec((n, tr, w), lambda i: (0, i, 0))],
        out_specs=pl.BlockSpec((tr, w), lambda i: (i, 0)), out_shape=jax.ShapeDtypeStruct((r, w), F32),
        compiler_params=_cparams("parallel"))(buf)


def _allreduce(buf, name):
    r, w = buf.shape
    whole = lambda x, y, c: (slice(None), slice(None))
    slot = lambda x, y, c: (4 * x + 2 * y + c,)
    stage = [(("in", 0), whole, 0, slot, f) for f in [(0, 0, 0)] + FLIPS_ALL]
    (slots,) = _exchange([buf], [jax.ShapeDtypeStruct((8, r, w), F32)], [stage], name + "_x")
    return _sum_slots(slots, name + "_sum")


def _allgather_weights(shards, name):
    half = lambda a, c: pl.ds(c * (a.shape[0] // 2), a.shape[0] // 2)
    first, second = [], []
    for n, a in enumerate(shards):
        for f in FLIPS_CHIP:
            first.append((("in", n), lambda x, y, c, a=a: (half(a, c), slice(None)), n,
                          lambda x, y, c, a=a: (2 * x + y, half(a, c), slice(None)), f))
            peer_slot = lambda x, y, c, a=a, f=f: (2 * (x ^ f[0]) + (y ^ f[1]), half(a, c), slice(None))
            second.append((("out", n), peer_slot, n, peer_slot, (0, 0, 1)))
    outs = [jax.ShapeDtypeStruct((4,) + a.shape, a.dtype) for a in shards]
    return _exchange(shards, outs, [first, second], name)


def _add_pair(G, bufA, cvec, name):
    _, Rs, Cs = G.shape
    Rh = Rs // 2
    tr = _pick(Rh, 128, 16)
    nb = Rh // tr

    def body(c_ref, g_ref, a_ref, o_ref):
        o_ref[...] = (g_ref[...] + a_ref[...]).astype(BF16)

    grid_spec = pltpu.PrefetchScalarGridSpec(
        num_scalar_prefetch=1, grid=(4, nb),
        in_specs=[pl.BlockSpec((1, tr, Cs), lambda s, i, c_ref: (s, c_ref[0] * nb + i, 0)),
                  pl.BlockSpec((1, tr, Cs), lambda s, i, c_ref: (s, i, 0))],
        out_specs=pl.BlockSpec((1, tr, Cs), lambda s, i, c_ref: (s, i, 0)))
    return pl.pallas_call(body, name=name, grid_spec=grid_spec, out_shape=jax.ShapeDtypeStruct((4, Rh, Cs), BF16),
                          compiler_params=_cparams("parallel", "parallel"))(cvec, G, bufA)


def _sum_chips(G, bufA, bufB, cvec, svec, name):
    _, Rs, Cs = G.shape
    Rh = Rs // 2
    tr = _pick(Rh, 128, 16)
    nb = Rh // tr

    def body(c_ref, s_ref, g_ref, a_ref, b_ref, o_ref):
        o_ref[...] = (g_ref[0] + a_ref[0]) + b_ref[0].astype(F32) + b_ref[1].astype(F32) + b_ref[2].astype(F32)

    grid_spec = pltpu.PrefetchScalarGridSpec(
        num_scalar_prefetch=2, grid=(nb,),
        in_specs=[pl.BlockSpec((1, tr, Cs), lambda i, c, s: (s[0], c[0] * nb + i, 0)),
                  pl.BlockSpec((1, tr, Cs), lambda i, c, s: (s[0], i, 0)),
                  pl.BlockSpec((3, tr, Cs), lambda i, c, s: (0, i, 0))],
        out_specs=pl.BlockSpec((tr, Cs), lambda i, c, s: (i, 0)))
    return pl.pallas_call(body, name=name, grid_spec=grid_spec, out_shape=jax.ShapeDtypeStruct((Rh, Cs), F32),
                          compiler_params=_cparams("parallel"))(cvec, svec, G, bufA, bufB)


def _reduce_scatter(grads, cvec, svec, name):
    ng = len(grads)
    Rh = [g.shape[1] // 2 for g in grads]
    whole3 = lambda x, y, c: (slice(None), slice(None), slice(None))
    whole2 = lambda x, y, c: (slice(None), slice(None))
    st = [(("in", n), lambda x, y, c, n=n: (slice(None), pl.ds((1 - c) * Rh[n], Rh[n]), slice(None)), n,
           whole3, (0, 0, 1)) for n in range(ng)]
    bufA = _exchange(grads, [jax.ShapeDtypeStruct((4, Rh[n], g.shape[2]), F32) for n, g in enumerate(grads)],
                     [st], name + "_pair")
    P = [_add_pair(g, a, cvec, "%s_add%d" % (name, n)) for n, (g, a) in enumerate(zip(grads, bufA))]
    st = [(("in", n), lambda x, y, c, f=f: (2 * (x ^ f[0]) + (y ^ f[1]),), n, lambda x, y, c, k=k: (k,), f)
          for n in range(ng) for k, f in enumerate(FLIPS_CHIP)]
    bufB = _exchange(P, [jax.ShapeDtypeStruct((3,) + p.shape[1:], BF16) for p in P], [st], name + "_chips")
    mine = [_sum_chips(g, a, b, cvec, svec, "%s_sum%d" % (name, n)) for n, (g, a, b) in enumerate(zip(grads, bufA, bufB))]
    st = [(("in", n), whole2, n, whole2, (0, 0, 1)) for n in range(ng)]
    other = _exchange(mine, [jax.ShapeDtypeStruct(r.shape, F32) for r in mine], [st], name + "_halves")
    return mine, other


def _adamw_halves(w, mine, other, m, v, cvec, name):
    Rs, Cs = w.shape
    Rh = Rs // 2
    tr = _pick(Rh, 128, 8)
    nb = Rh // tr

    def body(c_ref, w_ref, a_ref, b_ref, m_ref, v_ref, g_ref, d_ref, mo_ref, vo_ref):
        gv = jnp.where(pl.program_id(0) // nb == c_ref[0], a_ref[...], b_ref[...])
        g_ref[...] = gv
        d_ref[...], mo_ref[...], vo_ref[...] = _adam_math(w_ref[...], gv, m_ref[...], v_ref[...])

    row = pl.BlockSpec((tr, Cs), lambda i, c: (i, 0))
    hrow = pl.BlockSpec((tr, Cs), lambda i, c: (i % nb, 0))
    grid_spec = pltpu.PrefetchScalarGridSpec(num_scalar_prefetch=1, grid=(2 * nb,),
                                             in_specs=[row, hrow, hrow, row, row], out_specs=[row] * 4)
    return pl.pallas_call(body, name=name, grid_spec=grid_spec, out_shape=[jax.ShapeDtypeStruct((Rs, Cs), F32)] * 4,
                          compiler_params=_cparams("parallel"))(cvec, w, mine, other, m, v)


def _pack(arrays):
    flat = [a.reshape(-1).astype(F32) for a in arrays]
    meta, off = [], 0
    for a, f in zip(arrays, flat):
        meta.append((off, a.shape))
        off += f.shape[0]
    total = -(-off // (8 * LANES)) * (8 * LANES)
    flat.append(jnp.zeros((total - off,), F32))
    return jnp.concatenate(flat).reshape(total // LANES, LANES), meta


def _unpack(buf, meta):
    flat = buf.reshape(-1)
    out = []
    for off, shape in meta:
        size = 1
        for s in shape:
            size *= s
        out.append(flat[off:off + size].reshape(shape))
    return out


WEIGHT_NAMES = ["c_ctx", "w_mod", "b_mod", "g_mix", "w_in", "q_norm", "k_norm", "attn_sink", "w_gate_f", "b_gate_f",
                "w_gate_b", "b_gate_b", "gla_norm", "w_attn_o", "w_gla_o", "w_out", "g_ffn", "w_up", "conv_w",
                "conv_b", "w_down"]
BIG_NAMES = ["w_in", "w_attn_o", "w_gla_o", "w_out", "w_up", "w_down"]
SHARDED_SMALL = ["w_gate_f", "w_gate_b", "conv_w"]


def _layouts(D):
    aw, kvw, gk, gv = N_Q_HEADS * HEAD_DIM, N_KV_HEADS * HEAD_DIM, D // 2, D
    widths = {"qa": aw, "ka": kvw, "va": kvw, "qb": gk, "kb": gk, "vb": gv, "rb": gv, "lr": 2 * GLA_LOWRANK,
              "ga": D, "gb": D}
    orig, off = {}, 0
    for s in ["qa", "ka", "va", "qb", "kb", "vb", "rb", "lr", "ga", "gb"]:
        orig[s] = off
        off += widths[s]
    order = ["qa", "vb", "rb", "ga", "gb", "ka", "va", "qb", "kb", "lr"]
    lay, off = {}, 0
    for s in order:
        lay[s] = off
        off += LANES if s == "lr" else widths[s]
    align = {"qa": aw, "vb": D, "rb": D, "ga": D, "gb": D, "ka": kvw, "va": kvw, "qb": gk, "kb": gk,
             "lr": LANES}
    for s in order:
        assert lay[s] % align[s] == 0, (s, lay[s], align[s])
    return widths, orig, order, lay, off


def _rope_tables(T, L):
    t = jnp.arange(T)
    nf = HEAD_DIM // 4
    inv = ROPE_THETA ** (-jnp.arange(nf, dtype=F32) / nf)
    ang = jnp.concatenate([(t // GRID_W)[:, None] * inv, (t % GRID_W)[:, None] * inv], axis=-1)
    cos, sin = jnp.cos(ang), jnp.sin(ang)
    cos2 = jnp.concatenate([jnp.ones((L, HEAD_DIM), F32), jnp.concatenate([cos, cos], axis=-1)], axis=0)
    sin2 = jnp.concatenate([jnp.zeros((L, HEAD_DIM), F32), jnp.concatenate([-sin, sin], axis=-1)], axis=0)
    return cos2, sin2


def _step(x, c, ctx, loss_target, W, M, V):
    xi, yi, ci = lax.axis_index("x"), lax.axis_index("y"), lax.axis_index("c")
    chip = 2 * xi + yi
    dev = 2 * chip + ci
    south = (ci == 0).astype(F32)
    cvec = ci.reshape(1).astype(jnp.int32)
    T, D = x.shape[1], x.shape[2]
    L = ctx.shape[1]
    R = L + T
    F = 4 * W["w_down"].shape[1]
    GK, GV = D // 2, D
    DK, DV = GK // GLA_HEADS, GV // GLA_HEADS
    N6 = 6 * D
    N4 = N6 // 4
    widths, orig, order, lay, Z = _layouts(D)

    def place_cols(shard, full_cols):
        cols = shard.shape[-1]
        full = jnp.zeros(shard.shape[:-1] + (full_cols,), F32)
        return lax.dynamic_update_slice(full, shard * south, (0,) * (shard.ndim - 1) + (chip * cols,))

    c_rows = lax.dynamic_update_slice(jnp.zeros((8, D), F32), c, (dev, 0))
    bufa, meta = _pack([c_rows, place_cols(W["w_gate_f"][0], GK), place_cols(W["w_gate_b"][0], GK),
                        place_cols(W["conv_w"][0], 2 * F)])
    c_all, wgf, wgb, cw = _unpack(_allreduce(bufa, "gather_small"), meta)
    ca = jnp.concatenate([c_all, W["c_ctx"][None, :], jnp.zeros((7, D), F32)], axis=0)
    b_shard = lax.dynamic_slice(W["b_mod"], (0, chip * N4), (1, N4))
    mod_part, sil = _mod_fwd(ca, W["w_mod"][0], b_shard, "mod_fwd")
    slots = lax.dynamic_update_slice(jnp.zeros((4, 16, N4), F32), (mod_part * south)[None], (chip, 0, 0))
    mod_all = _allreduce(slots.reshape(64, N4), "gather_mod").reshape(4, 16, N4).transpose(1, 0, 2).reshape(16, N6)
    mx = lax.dynamic_slice(mod_all, (dev, 0), (1, N6)).reshape(6, 1, D)
    mc = mod_all[8].reshape(6, 1, D)

    sq = lambda a: a.reshape(a.shape[1:])
    shards = [sq(W[n]).astype(BF16) for n in BIG_NAMES]
    g_in, g_ao, g_go, g_out, g_up, g_dn = [
        lax.dynamic_update_slice(g, s[None], (chip, 0, 0))
        for g, s in zip(_allgather_weights(shards, "gather_weights"), shards)]
    cols = lambda g: g.transpose(1, 0, 2).reshape(g.shape[1], 4 * g.shape[2])
    rows = lambda g: g.reshape(4 * g.shape[1], g.shape[2])
    w_in_f = cols(g_in)
    seg = lambda s: w_in_f[:, orig[s]:orig[s] + widths[s]]
    w_cat = jnp.concatenate([jnp.pad(seg(s), ((0, 0), (0, LANES - widths[s]))) if s == "lr" else seg(s)
                             for s in order], axis=1)
    w_ao, w_go, w_out, w_up, w_dn = rows(g_ao), rows(g_go), rows(g_out), cols(g_up), rows(g_dn)
    wg = jnp.zeros((2, LANES, GK), F32).at[0, :GLA_LOWRANK].set(wgf).at[1, GLA_LOWRANK:2 * GLA_LOWRANK].set(wgb)
    bg = jnp.stack([W["b_gate_f"], W["b_gate_b"]])
    cb = W["conv_b"]
    sink_rows = jnp.broadcast_to(W["attn_sink"][0][:, None], (N_Q_HEADS, HEAD_DIM))
    cos2, sin2 = _rope_tables(T, L)
    blk = lambda s, w: lay[s] // w

    xall = jnp.concatenate([ctx[0], x[0]], axis=0)
    sc1 = jnp.stack([mc[1], mx[1]])
    sh1 = jnp.stack([mc[0], mx[0]])
    h = _modnorm_fwd(xall, W["g_mix"], sc1, sh1, L, "modnorm1")
    z = _matmul(h, w_cat, "nn", F32, "proj_in")
    qn = _qknorm_fwd(z, blk("qa", widths["qa"]), T, L, W["q_norm"], cos2, sin2, N_Q_HEADS, "qnorm")
    kn = _qknorm_fwd(z, blk("ka", widths["ka"]), R, 0, W["k_norm"], cos2, sin2, N_KV_HEADS, "knorm")
    vb = _cast_seg(z, blk("va", widths["va"]), widths["va"], "vcast")
    o_attn = _attn_fwd(qn, kn, vb, sink_rows, L, "attn_fwd")
    gla_blks = (blk("qb", GK), blk("kb", GK), blk("vb", GV), blk("lr", LANES))
    o_g, sprev = _gla_fwd(z, *gla_blks, wg, bg, DV, L, "gla_fwd")
    p = _glanorm_fwd(o_g, z, blk("rb", D), W["gla_norm"], L, "glanorm")
    ya = _matmul(o_attn, w_ao, "nn", F32, "proj_attn_o")
    yg = _matmul(p, w_go, "nn", F32, "proj_gla_o")
    m = _gate_fwd(z, blk("ga", D), blk("gb", D), ya, yg, L, "gate")
    mix = _matmul(m, w_out, "nn", F32, "proj_out")
    x1, h2 = _resnorm_fwd(x[0], mix, mx[2], W["g_ffn"], mx[4], mx[3], "resnorm2")
    u = _matmul(h2, w_up, "nn", F32, "ffn_up")
    f = _conv_fwd(u, cw, cb, "conv_swiglu")
    d = _matmul(f, w_dn, "nn", F32, "ffn_down")
    dy, lacc = _loss_head(d, x1, mx[5], loss_target[0], "loss_head")
    loss = lax.psum((0.5 / D) * jnp.sum(lacc[0]), ("x", "y", "c"))

    dd, s_gt2 = _gate_resid_bwd(dy, d, mx[5], "gate2_bwd")
    gw_dn = _matmul(f, dd, "tn", F32, "ffn_down_dw")
    df = _matmul(dd, w_dn, "nt", F32, "ffn_down_dx")
    du_a, du_g, acca, accg = _conv_bwd(u, df, cw, cb, "conv_swiglu_bwd")
    du = jnp.concatenate([du_a, du_g], axis=1)
    gw_up = _matmul(h2, du, "tn", F32, "ffn_up_dw")
    dh2 = _matmul(du, w_up, "nt", F32, "ffn_up_dx")
    dx1, s2 = _modnorm_bwd(x1, dh2, W["g_ffn"], mx[4], dy, "resnorm2_bwd")
    dmix, s_gt1 = _gate_resid_bwd(dx1, mix, mx[2], "gate1_bwd")
    gw_out = _matmul(m, dmix, "tn", F32, "proj_out_dw")
    dm = _matmul(dmix, w_out, "nt", F32, "proj_out_dx")
    dya, dyg, dga, dgb = _gate_bwd(z, blk("ga", D), blk("gb", D), ya, yg, dm, L, "gate_bwd")
    gw_ao = _matmul(o_attn, dya, "tn", F32, "proj_attn_o_dw")
    do_attn = _matmul(dya, w_ao, "nt", BF16, "proj_attn_o_dx")
    gw_go = _matmul(p, dyg, "tn", F32, "proj_gla_o_dw")
    dp = _matmul(dyg, w_go, "nt", F32, "proj_gla_o_dx")
    do_gla, drb, s_gn = _glanorm_bwd(o_g, z, blk("rb", D), W["gla_norm"], dp, L, "glanorm_bwd")
    do_pad = jnp.concatenate([jnp.zeros((L, GV), BF16), do_gla], axis=0)
    dqg, dkg, dvg, dpre, dbg = _gla_bwd(z, *gla_blks, wg, bg, sprev, do_pad, L, "gla_bwd")
    wg_cat = jnp.concatenate([wg[0], wg[1]], axis=1)
    dlr = _matmul(dpre, wg_cat, "nt", BF16, "gla_gate_dx")
    dwg = _matmul(z[:, lay["lr"]:lay["lr"] + LANES], dpre, "tn", F32, "gla_gate_dw")
    dqn, dkw, dvw, dkc, dvc, dsn = _attn_bwd(qn, kn, vb, sink_rows, do_attn, L, "attn_bwd")
    dqa, s_qn = _qknorm_bwd(z, blk("qa", widths["qa"]), T, L, W["q_norm"], cos2, sin2, dqn, N_Q_HEADS, "qnorm_bwd")
    dk_all = jnp.concatenate([dkc, dkw[WINDOW:WINDOW + T]], axis=0)
    dv_all = jnp.concatenate([dvc, dvw[WINDOW:WINDOW + T]], axis=0)
    dka, s_kn = _qknorm_bwd(z, blk("ka", widths["ka"]), R, 0, W["k_norm"], cos2, sin2, dk_all, N_KV_HEADS, "knorm_bwd")
    dz = _assemble_dz(lay, Z, L, dqa, drb, dga, dgb, dka, dv_all, dvg, dqg, dkg, dlr, "assemble_dz")
    gw_cat = _matmul(h, dz, "tn", F32, "proj_in_dw")
    dh = _matmul(dz, w_cat, "nt", F32, "proj_in_dx")
    grad_x, s1 = _modnorm_bwd(x[0], dh, W["g_mix"], mx[1], dx1, "modnorm1_bwd", dh_roff=L)
    _, s1c = _modnorm_bwd(ctx[0], dh, W["g_mix"], mc[1], None, "modnorm1_ctx_bwd")

    dmod_x = jnp.concatenate([s1[0], s1[1], s_gt1[0], s2[0], s2[1], s_gt2[0]])
    dmod_c = jnp.concatenate([s1c[0], s1c[1], jnp.zeros((4 * D,), F32)])
    dmod_rows = lax.dynamic_update_slice(jnp.zeros((9, N6), F32).at[8].set(dmod_c), dmod_x[None], (dev, 0))
    small = [dmod_rows, dmod_x + dmod_c, s1[2] + s1c[2], s_qn[0], s_kn[0], dsn[:, 0, :Q_PER_KV].reshape(N_Q_HEADS),
             dwg[:GLA_LOWRANK, :GK], dbg[0].reshape(GK), dwg[GLA_LOWRANK:2 * GLA_LOWRANK, GK:], dbg[1].reshape(GK),
             s_gn[0], s2[2], jnp.concatenate([acca[0:3], accg[0:3]], axis=1), jnp.concatenate([acca[3], accg[3]])]
    bufc, meta = _pack(small)
    (dmod_sum, g_b_mod, g_g_mix, g_q_norm, g_k_norm, g_sink, g_wgf, g_bgf, g_wgb, g_bgb, g_gla_norm, g_g_ffn,
     g_conv_w, g_conv_b) = _unpack(_allreduce(bufc, "reduce_small"), meta)
    dmod16 = lax.dynamic_slice(jnp.concatenate([dmod_sum, jnp.zeros((7, N6), F32)], axis=0), (0, chip * N4), (16, N4))
    g_w_mod = _matmul(sil, dmod16, "tn", F32, "mod_dw")
    dsil = _matmul(dmod16, W["w_mod"][0], "nt", F32, "mod_dx")
    g_c_ctx = _silu_bwd(_allreduce(dsil * south, "reduce_cctx"), ca, "silu_bwd")[8]

    gw_in = jnp.concatenate([gw_cat[:, lay[s]:lay[s] + widths[s]] for s in ["qa", "ka", "va", "qb", "kb", "vb", "rb",
                                                                           "lr", "ga", "gb"]], axis=1)
    by_cols = lambda g: g.reshape(g.shape[0], 4, g.shape[1] // 4).transpose(1, 0, 2)
    by_rows = lambda g: g.reshape(4, g.shape[0] // 4, g.shape[1])
    svec = chip.reshape(1).astype(jnp.int32)
    mine, other = _reduce_scatter([by_cols(gw_in), by_rows(gw_ao), by_rows(gw_go), by_rows(gw_out), by_cols(gw_up),
                                   by_rows(gw_dn)], cvec, svec, "reduce_big")
    cut = lambda g: lax.dynamic_slice(g, (0, chip * (g.shape[1] // 4)), (g.shape[0], g.shape[1] // 4))
    grads = {"c_ctx": g_c_ctx, "w_mod": g_w_mod[None], "b_mod": g_b_mod[None], "g_mix": g_g_mix[None],
             "q_norm": g_q_norm[None], "k_norm": g_k_norm[None], "attn_sink": g_sink[None],
             "w_gate_f": cut(g_wgf)[None], "b_gate_f": g_bgf[None], "w_gate_b": cut(g_wgb)[None],
             "b_gate_b": g_bgb[None], "gla_norm": g_gla_norm[None], "g_ffn": g_g_ffn[None],
             "conv_w": cut(g_conv_w)[None], "conv_b": g_conv_b[None]}

    delta, new_m, new_v = {}, {}, {}
    dl, mn, vn = _adamw(W["w_mod"][0], g_w_mod, M["w_mod"][0], V["w_mod"][0], "adamw_w_mod")
    delta["w_mod"], new_m["w_mod"], new_v["w_mod"] = dl[None], mn[None], vn[None]
    for n, a, b in zip(BIG_NAMES, mine, other):
        g, dl, mn, vn = _adamw_halves(sq(W[n]), a, b, sq(M[n]), sq(V[n]), cvec, "adamw_" + n)
        grads[n], delta[n], new_m[n], new_v[n] = g[None], dl[None], mn[None], vn[None]
    small_names = [n for n in WEIGHT_NAMES if n not in delta]
    packs = [_pack([src[n] for n in small_names]) for src in (W, grads, M, V)]
    meta = packs[0][1]
    outs = _adamw(packs[0][0], packs[1][0], packs[2][0], packs[3][0], "adamw_small")
    for res, o in zip((delta, new_m, new_v), outs):
        for n, a in zip(small_names, _unpack(o, meta)):
            res[n] = a
    return (loss, grad_x[None], *[grads[n] for n in WEIGHT_NAMES], *[delta[n] for n in WEIGHT_NAMES],
            *[new_m[n] for n in WEIGHT_NAMES], *[new_v[n] for n in WEIGHT_NAMES])


def kernel(x, c, ctx, c_ctx, w_mod, b_mod, g_mix, w_in, q_norm, k_norm, attn_sink, w_gate_f, b_gate_f, w_gate_b, b_gate_b, gla_norm, w_attn_o, w_gla_o, w_out, g_ffn, w_up, conv_w, conv_b, w_down, loss_target, m_c_ctx, m_w_mod, m_b_mod, m_g_mix, m_w_in, m_q_norm, m_k_norm, m_attn_sink, m_w_gate_f, m_b_gate_f, m_w_gate_b, m_b_gate_b, m_gla_norm, m_w_attn_o, m_w_gla_o, m_w_out, m_g_ffn, m_w_up, m_conv_w, m_conv_b, m_w_down, v_c_ctx, v_w_mod, v_b_mod, v_g_mix, v_w_in, v_q_norm, v_k_norm, v_attn_sink, v_w_gate_f, v_b_gate_f, v_w_gate_b, v_b_gate_b, v_gla_norm, v_w_attn_o, v_w_gla_o, v_w_out, v_g_ffn, v_w_up, v_conv_w, v_conv_b, v_w_down):
    W = dict(zip(WEIGHT_NAMES, (c_ctx, w_mod, b_mod, g_mix, w_in, q_norm, k_norm, attn_sink, w_gate_f, b_gate_f,
                                w_gate_b, b_gate_b, gla_norm, w_attn_o, w_gla_o, w_out, g_ffn, w_up, conv_w, conv_b,
                                w_down)))
    M = dict(zip(WEIGHT_NAMES, (m_c_ctx, m_w_mod, m_b_mod, m_g_mix, m_w_in, m_q_norm, m_k_norm, m_attn_sink,
                                m_w_gate_f, m_b_gate_f, m_w_gate_b, m_b_gate_b, m_gla_norm, m_w_attn_o, m_w_gla_o,
                                m_w_out, m_g_ffn, m_w_up, m_conv_w, m_conv_b, m_w_down)))
    V = dict(zip(WEIGHT_NAMES, (v_c_ctx, v_w_mod, v_b_mod, v_g_mix, v_w_in, v_q_norm, v_k_norm, v_attn_sink,
                                v_w_gate_f, v_b_gate_f, v_w_gate_b, v_b_gate_b, v_gla_norm, v_w_attn_o, v_w_gla_o,
                                v_w_out, v_g_ffn, v_w_up, v_conv_w, v_conv_b, v_w_down)))
    return _step(x, c, ctx, loss_target, W, M, V)
```

```python
import functools
import math

import jax
import jax.numpy as jnp
from jax import lax
from jax.experimental import pallas as pl
from jax.experimental.pallas import tpu as pltpu

F32 = jnp.float32
BF16 = jnp.bfloat16
MESH = pl.DeviceIdType.MESH

EPS = 1e-6
HEAD_DIM = 128
N_Q_HEADS = 16
N_KV_HEADS = 4
Q_PER_KV = N_Q_HEADS // N_KV_HEADS
WINDOW = 128
GLA_HEADS = 4
GLA_LOWRANK = 16
GLA_GATE_NORM = 16.0
GLA_CHUNK = 64
GRID_W = 64
ROPE_THETA = 10000.0
GLA_LEVELS = (32, 16, 8, 4, 2, 1)
LANES = 128

ADAM_LR = 0.001
ADAM_B1 = 0.9
ADAM_B2 = 0.999
ADAM_EPS = 1e-08
ADAM_WD = 0.01
ADAM_STEP = 10

VMEM_LIMIT = 52 * 1024 * 1024


def _cparams(*sem):
    return pltpu.CompilerParams(dimension_semantics=sem, vmem_limit_bytes=VMEM_LIMIT)


def _pick(n, target, mult=LANES):
    best = None
    d = mult
    while d <= min(n, target):
        if n % d == 0:
            best = d
        d += mult
    return n if best is None else best


def _sigmoid(x):
    return 1.0 / (1.0 + jnp.exp(-x))


def _silu(x):
    return x * _sigmoid(x)


def _dsilu(x):
    s = _sigmoid(x)
    return s * (1.0 + x * (1.0 - s))


def _dot(a, b, dims):
    return lax.dot_general(a, b, (dims, ((), ())), preferred_element_type=F32)


NN = ((1,), (0,))
NT = ((1,), (1,))
TN = ((0,), (0,))


def _matmul(a, b, mode, out_dtype, name, tm=768, tn=1024, tk=2048, ride=None):
    if mode == "nn":
        (M, K), (K2, N) = a.shape, b.shape
    elif mode == "nt":
        (M, K), (N, K2) = a.shape, b.shape
    else:
        (K, M), (K2, N) = a.shape, b.shape
    assert K == K2, (name, a.shape, b.shape)
    if mode == "tn":
        tm = max(tm, 1024)
    tm, tn, tk = _pick(M, tm), _pick(N, tn), _pick(K, tk)
    nk = K // tk
    dims = {"nn": NN, "nt": NT, "tn": TN}[mode]

    def body(a_ref, b_ref, o_ref, acc_ref):
        k = pl.program_id(2)

        @pl.when(k == 0)
        def _():
            acc_ref[...] = jnp.zeros_like(acc_ref)

        acc_ref[...] += _dot(a_ref[...].astype(BF16), b_ref[...].astype(BF16), dims)

        @pl.when(k == nk - 1)
        def _():
            o_ref[...] = acc_ref[...].astype(out_dtype)

    if mode == "tn":
        a_spec = pl.BlockSpec((tk, tm), lambda i, j, k: (k, i))
    else:
        a_spec = pl.BlockSpec((tm, tk), lambda i, j, k: (i, k))
    if mode == "nt":
        b_spec = pl.BlockSpec((tn, tk), lambda i, j, k: (j, k))
    else:
        b_spec = pl.BlockSpec((tk, tn), lambda i, j, k: (k, j))
    return _pcall(
        body, name=name, grid=(M // tm, N // tn, nk),
        in_specs=[a_spec, b_spec],
        out_specs=pl.BlockSpec((tm, tn), lambda i, j, k: (i, j)),
        out_shape=jax.ShapeDtypeStruct((M, N), out_dtype),
        scratch_shapes=[pltpu.VMEM((tm, tn), F32)],
        sem=("parallel", "parallel", "arbitrary"), args=(a, b), ride=ride)


def _modnorm_fwd(xall, g, sc, sh, n_ctx, name):
    R, D = xall.shape
    tm = _pick(n_ctx, 256, 8)
    cb = n_ctx // tm

    def body(x_ref, g_ref, sc_ref, sh_ref, h_ref):
        x = x_ref[...]
        r = lax.rsqrt(jnp.mean(x * x, axis=-1, keepdims=True) + EPS)
        n = x * r * g_ref[...]
        h_ref[...] = (n * (1.0 + sc_ref[0]) + sh_ref[0]).astype(BF16)

    sel = lambda i: (jnp.where(i < cb, 0, 1), 0, 0)
    return pl.pallas_call(
        body, name=name, grid=(R // tm,),
        in_specs=[pl.BlockSpec((tm, D), lambda i: (i, 0)), pl.BlockSpec((1, D), lambda i: (0, 0)),
                  pl.BlockSpec((1, 1, D), sel), pl.BlockSpec((1, 1, D), sel)],
        out_specs=pl.BlockSpec((tm, D), lambda i: (i, 0)),
        out_shape=jax.ShapeDtypeStruct((R, D), BF16),
        compiler_params=_cparams("parallel"),
    )(xall, g, sc, sh)


def _modnorm_bwd(x, dh, g, sc, resid, name, dh_roff=0):
    N, D = x.shape
    tm = _pick(math.gcd(N, dh_roff), 256, 8)
    ro = dh_roff // tm
    want_dx = resid is not None

    def body(*refs):
        if want_dx:
            x_ref, dh_ref, g_ref, sc_ref, res_ref, dx_ref, acc_ref = refs
        else:
            x_ref, dh_ref, g_ref, sc_ref, acc_ref = refs
        i = pl.program_id(0)

        @pl.when(i == 0)
        def _():
            acc_ref[...] = jnp.zeros_like(acc_ref)

        xv, dhv, gv = x_ref[...], dh_ref[...], g_ref[...]
        r = lax.rsqrt(jnp.mean(xv * xv, axis=-1, keepdims=True) + EPS)
        xh = xv * r
        dn = dhv * (1.0 + sc_ref[...])
        acc_ref[0:1, :] += jnp.sum(dhv, axis=0, keepdims=True)
        acc_ref[1:2, :] += jnp.sum(dhv * xh * gv, axis=0, keepdims=True)
        acc_ref[2:3, :] += jnp.sum(dn * xh, axis=0, keepdims=True)
        if want_dx:
            dxh = dn * gv
            dx_ref[...] = res_ref[...] + r * (dxh - xh * jnp.mean(dxh * xh, axis=-1, keepdims=True))

    row = pl.BlockSpec((tm, D), lambda i: (i, 0))
    drow = pl.BlockSpec((tm, D), lambda i: (i + ro, 0))
    vec = pl.BlockSpec((1, D), lambda i: (0, 0))
    acc = pl.BlockSpec((8, D), lambda i: (0, 0))
    acc_shape = jax.ShapeDtypeStruct((8, D), F32)
    if want_dx:
        return pl.pallas_call(
            body, name=name, grid=(N // tm,), in_specs=[row, drow, vec, vec, row],
            out_specs=[row, acc], out_shape=[jax.ShapeDtypeStruct((N, D), F32), acc_shape],
            compiler_params=_cparams("arbitrary"))(x, dh, g, sc, resid)
    sums = pl.pallas_call(
        body, name=name, grid=(N // tm,), in_specs=[row, drow, vec, vec],
        out_specs=acc, out_shape=acc_shape, compiler_params=_cparams("arbitrary"))(x, dh, g, sc)
    return None, sums


def _qknorm_fwd(z, cblk, nrows, roff, w, cos2, sin2, nh, name):
    W = nh * HEAD_DIM
    tm = _pick(math.gcd(nrows, roff), 256, 8)
    ro = roff // tm
    assert roff % tm == 0

    def body(z_ref, w_ref, c_ref, s_ref, o_ref):
        c, s, wv = c_ref[...], s_ref[...], w_ref[...]
        for h in range(nh):
            x = z_ref[:, h * HEAD_DIM:(h + 1) * HEAD_DIM]
            r = lax.rsqrt(jnp.mean(x * x, axis=-1, keepdims=True) + EPS)
            y = x * r * wv
            o_ref[:, h * HEAD_DIM:(h + 1) * HEAD_DIM] = (y * c + pltpu.roll(y, HEAD_DIM // 2, 1) * s).astype(BF16)

    return pl.pallas_call(
        body, name=name, grid=(nrows // tm,),
        in_specs=[pl.BlockSpec((tm, W), lambda i: (i + ro, cblk)), pl.BlockSpec((1, HEAD_DIM), lambda i: (0, 0)),
                  pl.BlockSpec((tm, HEAD_DIM), lambda i: (i + ro, 0)), pl.BlockSpec((tm, HEAD_DIM), lambda i: (i + ro, 0))],
        out_specs=pl.BlockSpec((tm, W), lambda i: (i, 0)),
        out_shape=jax.ShapeDtypeStruct((nrows, W), BF16),
        compiler_params=_cparams("parallel"),
    )(z, w, cos2, sin2)


def _qknorm_bwd(z, cblk, nrows, roff, w, cos2, sin2, dy, nh, name):
    W = nh * HEAD_DIM
    tm = _pick(math.gcd(nrows, roff), 256, 8)
    ro = roff // tm

    def body(z_ref, w_ref, c_ref, s_ref, dy_ref, dz_ref, acc_ref):
        i = pl.program_id(0)

        @pl.when(i == 0)
        def _():
            acc_ref[...] = jnp.zeros_like(acc_ref)

        c, s, wv = c_ref[...], s_ref[...], w_ref[...]
        dw = jnp.zeros((1, HEAD_DIM), F32)
        for h in range(nh):
            sl = slice(h * HEAD_DIM, (h + 1) * HEAD_DIM)
            x = z_ref[:, sl]
            d = dy_ref[:, sl]
            dyn = d * c + pltpu.roll(d * s, HEAD_DIM // 2, 1)
            r = lax.rsqrt(jnp.mean(x * x, axis=-1, keepdims=True) + EPS)
            xh = x * r
            dw = dw + jnp.sum(dyn * xh, axis=0, keepdims=True)
            dxh = dyn * wv
            dz_ref[:, sl] = (r * (dxh - xh * jnp.mean(dxh * xh, axis=-1, keepdims=True))).astype(BF16)
        acc_ref[0:1, :] += dw

    return pl.pallas_call(
        body, name=name, grid=(nrows // tm,),
        in_specs=[pl.BlockSpec((tm, W), lambda i: (i + ro, cblk)), pl.BlockSpec((1, HEAD_DIM), lambda i: (0, 0)),
                  pl.BlockSpec((tm, HEAD_DIM), lambda i: (i + ro, 0)), pl.BlockSpec((tm, HEAD_DIM), lambda i: (i + ro, 0)),
                  pl.BlockSpec((tm, W), lambda i: (i, 0))],
        out_specs=[pl.BlockSpec((tm, W), lambda i: (i, 0)), pl.BlockSpec((8, HEAD_DIM), lambda i: (0, 0))],
        out_shape=[jax.ShapeDtypeStruct((nrows, W), BF16), jax.ShapeDtypeStruct((8, HEAD_DIM), F32)],
        compiler_params=_cparams("arbitrary"),
    )(z, w, cos2, sin2, dy)


def _cast_seg(z, cblk, width, name):
    R = z.shape[0]
    tm = _pick(R, 512, 8)

    def body(z_ref, o_ref):
        o_ref[...] = z_ref[...].astype(BF16)

    return pl.pallas_call(
        body, name=name, grid=(R // tm,),
        in_specs=[pl.BlockSpec((tm, width), lambda i: (i, cblk))],
        out_specs=pl.BlockSpec((tm, width), lambda i: (i, 0)),
        out_shape=jax.ShapeDtypeStruct((R, width), BF16), compiler_params=_cparams("parallel"))(z)


NEG_BIG = -1e30


def _attn_specs(T, n_ctx):
    nb = T // WINDOW
    lb = n_ctx // WINDOW
    blk = lambda f: pl.BlockSpec((WINDOW, HEAD_DIM), f)
    win = [blk(lambda h, i: (lb + jnp.maximum(i - 1, 0), h)), blk(lambda h, i: (lb + i, h)),
           blk(lambda h, i: (lb + jnp.minimum(i + 1, nb - 1), h))]
    ctx = pl.BlockSpec((n_ctx, HEAD_DIM), lambda h, i: (0, h))
    qspec = pl.BlockSpec((WINDOW, Q_PER_KV * HEAD_DIM), lambda h, i: (i, h))
    sink = pl.BlockSpec((N_Q_HEADS, HEAD_DIM), lambda h, i: (0, 0))
    return nb, qspec, win, ctx, sink


def _attn_probs(q, kw, kctx, snk, valid):
    scale = HEAD_DIM ** -0.5
    s_lat = jnp.where(valid, _dot(q, kw, NT) * scale, NEG_BIG)
    s_ctx = _dot(q, kctx, NT) * scale
    m = jnp.maximum(jnp.maximum(jnp.max(s_lat, axis=-1, keepdims=True), jnp.max(s_ctx, axis=-1, keepdims=True)), snk)
    p_lat = jnp.exp(s_lat - m)
    p_ctx = jnp.exp(s_ctx - m)
    p_snk = jnp.exp(snk - m)
    den = p_snk + jnp.sum(p_lat, axis=-1, keepdims=True) + jnp.sum(p_ctx, axis=-1, keepdims=True)
    return p_lat, p_ctx, p_snk, den


def _attn_valid(i, T):
    qpos = i * WINDOW + lax.broadcasted_iota(jnp.int32, (WINDOW, 3 * WINDOW), 0)
    kpos = (i - 1) * WINDOW + lax.broadcasted_iota(jnp.int32, (WINDOW, 3 * WINDOW), 1)
    return (jnp.abs(qpos - kpos) <= WINDOW) & (kpos >= 0) & (kpos < T)


def _attn_fwd(qn, kn, vb, sink_rows, n_ctx, name, ride=None):
    T = qn.shape[0]
    nb, qspec, win, ctx, sink = _attn_specs(T, n_ctx)

    def body(q_ref, kp, kc, kx, vp, vc, vx, kctx_ref, vctx_ref, sink_ref, o_ref):
        h, i = pl.program_id(0), pl.program_id(1)
        kw = jnp.concatenate([kp[...], kc[...], kx[...]], axis=0)
        vw = jnp.concatenate([vp[...], vc[...], vx[...]], axis=0)
        kctx, vctx = kctx_ref[...], vctx_ref[...]
        valid = _attn_valid(i, T)
        for g in range(Q_PER_KV):
            sl = slice(g * HEAD_DIM, (g + 1) * HEAD_DIM)
            snk = sink_ref[pl.ds(h * Q_PER_KV + g, 1), :][:, 0:1]
            p_lat, p_ctx, _, den = _attn_probs(q_ref[:, sl], kw, kctx, snk, valid)
            o = (_dot(p_lat.astype(BF16), vw, NN) + _dot(p_ctx.astype(BF16), vctx, NN)) / den
            o_ref[:, sl] = o.astype(BF16)

    return _pcall(
        body, name=name, grid=(N_KV_HEADS, nb),
        in_specs=[qspec] + win + win + [ctx, ctx, sink],
        out_specs=qspec, out_shape=jax.ShapeDtypeStruct(qn.shape, BF16),
        sem=("parallel", "parallel"), args=(qn, kn, kn, kn, vb, vb, vb, kn, vb, sink_rows), ride=ride)


def _attn_bwd(qn, kn, vb, sink_rows, do, n_ctx, name, ride=None):
    T = qn.shape[0]
    nb, qspec, win, ctx, sink = _attn_specs(T, n_ctx)
    scale = HEAD_DIM ** -0.5
    TP = T + 2 * WINDOW

    def body(q_ref, kp, kc, kx, vp, vc, vx, kctx_ref, vctx_ref, sink_ref, do_ref,
             dq_ref, dkw_ref, dvw_ref, dkc_ref, dvc_ref, dsn_ref):
        h, i = pl.program_id(0), pl.program_id(1)

        @pl.when(i == 0)
        def _():
            dkw_ref[...] = jnp.zeros_like(dkw_ref)
            dvw_ref[...] = jnp.zeros_like(dvw_ref)
            dkc_ref[...] = jnp.zeros_like(dkc_ref)
            dvc_ref[...] = jnp.zeros_like(dvc_ref)
            dsn_ref[...] = jnp.zeros_like(dsn_ref)

        kw = jnp.concatenate([kp[...], kc[...], kx[...]], axis=0)
        vw = jnp.concatenate([vp[...], vc[...], vx[...]], axis=0)
        kctx, vctx = kctx_ref[...], vctx_ref[...]
        valid = _attn_valid(i, T)
        lane = lax.broadcasted_iota(jnp.int32, (8, HEAD_DIM), 1)
        dkw = jnp.zeros((3 * WINDOW, HEAD_DIM), F32)
        dvw = jnp.zeros((3 * WINDOW, HEAD_DIM), F32)
        dkc = jnp.zeros(kctx.shape, F32)
        dvc = jnp.zeros(kctx.shape, F32)
        dsn = jnp.zeros((8, HEAD_DIM), F32)
        for g in range(Q_PER_KV):
            sl = slice(g * HEAD_DIM, (g + 1) * HEAD_DIM)
            snk = sink_ref[pl.ds(h * Q_PER_KV + g, 1), :][:, 0:1]
            q, d_o = q_ref[:, sl], do_ref[:, sl]
            p_lat, p_ctx, p_snk, den = _attn_probs(q, kw, kctx, snk, valid)
            inv = 1.0 / den
            p_lat, p_ctx, p_snk = p_lat * inv, p_ctx * inv, p_snk * inv
            dp_lat = _dot(d_o, vw, NT)
            dp_ctx = _dot(d_o, vctx, NT)
            dr = jnp.sum(p_lat * dp_lat, axis=-1, keepdims=True) + jnp.sum(p_ctx * dp_ctx, axis=-1, keepdims=True)
            ds_lat = (p_lat * (dp_lat - dr) * scale).astype(BF16)
            ds_ctx = (p_ctx * (dp_ctx - dr) * scale).astype(BF16)
            dq_ref[:, sl] = _dot(ds_lat, kw, NN) + _dot(ds_ctx, kctx, NN)
            dkw = dkw + _dot(ds_lat, q, TN)
            dvw = dvw + _dot(p_lat.astype(BF16), d_o, TN)
            dkc = dkc + _dot(ds_ctx, q, TN)
            dvc = dvc + _dot(p_ctx.astype(BF16), d_o, TN)
            dsn = dsn + jnp.where(lane == g, -jnp.sum(p_snk * dr, axis=0, keepdims=True), 0.0)
        rows = pl.ds(pl.multiple_of(i * WINDOW, WINDOW), 3 * WINDOW)
        dkw_ref[rows, :] += dkw
        dvw_ref[rows, :] += dvw
        dkc_ref[...] += dkc
        dvc_ref[...] += dvc
        dsn_ref[0] += dsn

    wacc = pl.BlockSpec((TP, HEAD_DIM), lambda h, i: (0, h))
    return _pcall(
        body, name=name, grid=(N_KV_HEADS, nb),
        in_specs=[qspec] + win + win + [ctx, ctx, sink, qspec],
        out_specs=[qspec, wacc, wacc, ctx, ctx, pl.BlockSpec((1, 8, HEAD_DIM), lambda h, i: (h, 0, 0))],
        out_shape=[jax.ShapeDtypeStruct(qn.shape, F32),
                   jax.ShapeDtypeStruct((TP, N_KV_HEADS * HEAD_DIM), F32),
                   jax.ShapeDtypeStruct((TP, N_KV_HEADS * HEAD_DIM), F32),
                   jax.ShapeDtypeStruct((n_ctx, N_KV_HEADS * HEAD_DIM), F32),
                   jax.ShapeDtypeStruct((n_ctx, N_KV_HEADS * HEAD_DIM), F32),
                   jax.ShapeDtypeStruct((N_KV_HEADS, 8, HEAD_DIM), F32)],
        sem=("arbitrary", "arbitrary"), args=(qn, kn, kn, kn, vb, vb, vb, kn, vb, sink_rows, do), ride=ride)


def _gla_masks(dirv):
    C = GLA_CHUNK
    r = lax.broadcasted_iota(jnp.int32, (C, C), 0)
    c = lax.broadcasted_iota(jnp.int32, (C, C), 1)
    tt = jnp.where(dirv == 0, r, C - 1 - r)
    ss = jnp.where(dirv == 0, c, C - 1 - c)
    le = (ss <= tt).astype(jnp.int32)
    sums = [le == 1, le == 0]
    blocks = [ss == tt]
    for m in GLA_LEVELS:
        sh = m.bit_length() - 1
        same = (tt >> (sh + 1)) == (ss >> (sh + 1))
        ut = (tt >> sh) & 1
        us = (ss >> sh) & 1
        sums.append(same & (ut == us) & (ut == le))
        blocks.append(same & (ut == 1) & (us == 0))
    mall = jnp.concatenate([jnp.where(s, 1.0, 0.0) for s in sums], axis=0).astype(BF16)
    return mall, blocks


def _split3(x):
    hi = x.astype(BF16)
    r1 = x - hi.astype(F32)
    mid = r1.astype(BF16)
    lo = (r1 - mid.astype(F32)).astype(BF16)
    return hi, mid, lo


def _dot3(m_bf16, x, dims):
    hi, mid, lo = _split3(x)
    return _dot(m_bf16, hi, dims) + _dot(m_bf16, mid, dims) + _dot(m_bf16, lo, dims)


def _gla_chunk_of(dirv, j, lc, nc):
    return jnp.where(dirv == 0, j, jnp.where(j < lc, lc - 1 - j, nc + lc - 1 - j))


def _gla_gate(lr_ref, wg_ref, bg_ref):
    pre = _dot(lr_ref[...].astype(BF16), wg_ref[0].astype(BF16), NN) + bg_ref[0]
    g = (jnp.minimum(pre, 0.0) - jnp.log(1.0 + jnp.exp(-jnp.abs(pre)))) * (1.0 / GLA_GATE_NORM)
    return pre, g


def _gla_fwd(z, qblk, kblk, vblk, lrblk, wg, bg, DV, n_ctx, name):
    R = z.shape[0]
    C = GLA_CHUNK
    DK = wg.shape[2] // GLA_HEADS
    nc, lc = R // C, n_ctx // C
    qscale = DK ** -0.5

    GK, GV = GLA_HEADS * DK, GLA_HEADS * DV

    def body(q_ref, k_ref, v_ref, lr_ref, wg_ref, bg_ref, o_ref, sp_ref, st_ref):
        dirv, j = pl.program_id(0), pl.program_id(1)

        @pl.when(j == 0)
        def _():
            st_ref[...] = jnp.zeros_like(st_ref)

        mall, blocks = _gla_masks(dirv)
        _, g_all = _gla_gate(lr_ref, wg_ref, bg_ref)
        E_all = _dot3(mall, g_all, NN)
        for h in range(GLA_HEADS):
            ks, vs = slice(h * DK, (h + 1) * DK), slice(h * DV, (h + 1) * DV)
            q, k, v = q_ref[:, ks] * qscale, k_ref[:, ks], v_ref[:, vs].astype(BF16)
            g, E = g_all[:, ks], E_all[:, ks]
            st = st_ref[h]
            sp_ref[0, h, 0] = st
            A = jnp.where(blocks[0], _dot(q.astype(BF16), k.astype(BF16), NT), 0.0)
            for l in range(len(GLA_LEVELS)):
                e = jnp.exp(E[(2 + l) * C:(3 + l) * C])
                A = A + jnp.where(blocks[l + 1], _dot((q * e).astype(BF16), (k * e).astype(BF16), NT), 0.0)
            o_ref[0, :, vs] = (_dot((q * jnp.exp(E[0:C])).astype(BF16), st.astype(BF16), NT)
                               + _dot(A.astype(BF16), v, NN))
            decay = jnp.exp(jnp.sum(g, axis=0, keepdims=True))
            st_ref[h] = decay * st + _dot(v, (k * jnp.exp(E[C:2 * C])).astype(BF16), TN)

    chunk = functools.partial(_gla_chunk_of, lc=lc, nc=nc)
    return pl.pallas_call(
        body, name=name, grid=(2, nc),
        in_specs=[pl.BlockSpec((C, GK), lambda d, j: (chunk(d, j), qblk)),
                  pl.BlockSpec((C, GK), lambda d, j: (chunk(d, j), kblk)),
                  pl.BlockSpec((C, GV), lambda d, j: (chunk(d, j), vblk)),
                  pl.BlockSpec((C, LANES), lambda d, j: (chunk(d, j), lrblk)),
                  pl.BlockSpec((1, LANES, GK), lambda d, j: (d, 0, 0)),
                  pl.BlockSpec((1, 1, GK), lambda d, j: (d, 0, 0))],
        out_specs=[pl.BlockSpec((1, C, GV), lambda d, j: (d, chunk(d, j), 0)),
                   pl.BlockSpec((1, GLA_HEADS, 1, DV, DK), lambda d, j: (d, 0, j, 0, 0))],
        out_shape=[jax.ShapeDtypeStruct((2, R, GV), F32),
                   jax.ShapeDtypeStruct((2, GLA_HEADS, nc, DV, DK), F32)],
        scratch_shapes=[pltpu.VMEM((GLA_HEADS, DV, DK), F32)],
        compiler_params=_cparams("parallel", "arbitrary"),
    )(z, z, z, z, wg, bg)


def _gla_bwd(z, qblk, kblk, vblk, lrblk, wg, bg, sprev, do, n_ctx, name, ride=None):
    R = z.shape[0]
    C = GLA_CHUNK
    DK, DV = wg.shape[2] // GLA_HEADS, do.shape[1] // GLA_HEADS
    nc, lc = R // C, n_ctx // C
    qscale = DK ** -0.5
    nl = len(GLA_LEVELS)

    GK, GV = GLA_HEADS * DK, GLA_HEADS * DV

    def body(q_ref, k_ref, v_ref, lr_ref, wg_ref, bg_ref, sp_ref, do_ref,
             dq_ref, dk_ref, dv_ref, dpre_ref, dbg_ref, dst_ref):
        dirv, jr = pl.program_id(0), pl.program_id(1)

        @pl.when(jr == 0)
        def _():
            dst_ref[...] = jnp.zeros_like(dst_ref)
            dbg_ref[...] = jnp.zeros_like(dbg_ref)

        mall, blocks = _gla_masks(dirv)
        pre_all, g_all = _gla_gate(lr_ref, wg_ref, bg_ref)
        E_all = _dot3(mall, g_all, NN)
        for h in range(GLA_HEADS):
            ks, vs = slice(h * DK, (h + 1) * DK), slice(h * DV, (h + 1) * DV)
            q, k, v = q_ref[:, ks] * qscale, k_ref[:, ks], v_ref[:, vs].astype(BF16)
            pre, g, E = pre_all[:, ks], g_all[:, ks], E_all[:, ks]
            eb, er = jnp.exp(E[0:C]), jnp.exp(E[C:2 * C])
            decay = jnp.exp(jnp.sum(g, axis=0, keepdims=True))
            st = sp_ref[0, h, 0]
            dst = dst_ref[h]
            d_o = do_ref[:, vs]
            qe, kd = q * eb, k * er
            qb, kb = q.astype(BF16), k.astype(BF16)
            A = jnp.where(blocks[0], _dot(qb, kb, NT), 0.0)
            for l in range(nl):
                e = jnp.exp(E[(2 + l) * C:(3 + l) * C])
                A = A + jnp.where(blocks[l + 1], _dot((q * e).astype(BF16), (k * e).astype(BF16), NT), 0.0)
            dA = _dot(d_o, v, NT)
            dv_ref[0, :, vs] = _dot(A.astype(BF16), d_o, TN) + _dot(kd.astype(BF16), dst.astype(BF16), NT)
            dqe = _dot(d_o, st.astype(BF16), NN)
            dkd = _dot(v, dst.astype(BF16), NN)
            G = jnp.where(blocks[0], dA, 0.0).astype(BF16)
            dq = dqe * eb + _dot(G, kb, NN)
            dk = dkd * er + _dot(G, qb, TN)
            dE = [dqe * qe, dkd * kd]
            for l in range(nl):
                e = jnp.exp(E[(2 + l) * C:(3 + l) * C])
                ql, kl = q * e, k * e
                G = jnp.where(blocks[l + 1], dA, 0.0).astype(BF16)
                dql = _dot(G, kl.astype(BF16), NN)
                dkl = _dot(G, ql.astype(BF16), TN)
                dq = dq + dql * e
                dk = dk + dkl * e
                dE.append(dql * ql + dkl * kl)
            dlast = jnp.sum(dst * st, axis=0, keepdims=True) * decay
            dg = _dot3(mall, jnp.concatenate(dE, axis=0), TN) + dlast
            dpre = dg * (1.0 / GLA_GATE_NORM) / (1.0 + jnp.exp(pre))
            dq_ref[0, :, ks] = dq * qscale
            dk_ref[0, :, ks] = dk
            dpre_ref[:, ks] = dpre.astype(BF16)
            dbg_ref[0, :, ks] += jnp.sum(dpre, axis=0, keepdims=True)
            dst_ref[h] = decay * dst + _dot(d_o, qe.astype(BF16), TN)

    def chunk(d, jr):
        return _gla_chunk_of(d, nc - 1 - jr, lc, nc)

    return _pcall(
        body, name=name, grid=(2, nc),
        in_specs=[pl.BlockSpec((C, GK), lambda d, j: (chunk(d, j), qblk)),
                  pl.BlockSpec((C, GK), lambda d, j: (chunk(d, j), kblk)),
                  pl.BlockSpec((C, GV), lambda d, j: (chunk(d, j), vblk)),
                  pl.BlockSpec((C, LANES), lambda d, j: (chunk(d, j), lrblk)),
                  pl.BlockSpec((1, LANES, GK), lambda d, j: (d, 0, 0)),
                  pl.BlockSpec((1, 1, GK), lambda d, j: (d, 0, 0)),
                  pl.BlockSpec((1, GLA_HEADS, 1, DV, DK), lambda d, j: (d, 0, nc - 1 - j, 0, 0)),
                  pl.BlockSpec((C, GV), lambda d, j: (chunk(d, j), 0))],
        out_specs=[pl.BlockSpec((1, C, GK), lambda d, j: (d, chunk(d, j), 0)),
                   pl.BlockSpec((1, C, GK), lambda d, j: (d, chunk(d, j), 0)),
                   pl.BlockSpec((1, C, GV), lambda d, j: (d, chunk(d, j), 0)),
                   pl.BlockSpec((C, GK), lambda d, j: (chunk(d, j), d)),
                   pl.BlockSpec((1, 1, GK), lambda d, j: (d, 0, 0))],
        out_shape=[jax.ShapeDtypeStruct((2, R, GK), F32),
                   jax.ShapeDtypeStruct((2, R, GK), F32),
                   jax.ShapeDtypeStruct((2, R, GV), F32),
                   jax.ShapeDtypeStruct((R, 2 * GK), BF16),
                   jax.ShapeDtypeStruct((2, 1, GK), F32)],
        scratch_shapes=[pltpu.VMEM((GLA_HEADS, DV, DK), F32)],
        sem=("arbitrary", "arbitrary"), args=(z, z, z, z, wg, bg, sprev, do), ride=ride)


def _glanorm_fwd(o, z, rbblk, gn, n_ctx, name):
    _, R, GV = o.shape
    T = R - n_ctx
    DV = GV // GLA_HEADS
    tm = _pick(n_ctx, 256, 8)
    ro = n_ctx // tm

    def body(o0_ref, o1_ref, rb_ref, gn_ref, p_ref):
        gnv = gn_ref[...]
        for h in range(GLA_HEADS):
            sl = slice(h * DV, (h + 1) * DV)
            og = o0_ref[0, :, sl] + o1_ref[0, :, sl]
            r = lax.rsqrt(jnp.mean(og * og, axis=-1, keepdims=True) + EPS)
            p_ref[:, sl] = (og * r * gnv * _silu(rb_ref[:, sl])).astype(BF16)

    return pl.pallas_call(
        body, name=name, grid=(T // tm,),
        in_specs=[pl.BlockSpec((1, tm, GV), lambda i: (0, i + ro, 0)), pl.BlockSpec((1, tm, GV), lambda i: (1, i + ro, 0)),
                  pl.BlockSpec((tm, GV), lambda i: (i + ro, rbblk)), pl.BlockSpec((1, DV), lambda i: (0, 0))],
        out_specs=pl.BlockSpec((tm, GV), lambda i: (i, 0)),
        out_shape=jax.ShapeDtypeStruct((T, GV), BF16), compiler_params=_cparams("parallel"))(o, o, z, gn)


def _glanorm_bwd(o, z, rbblk, gn, dp, n_ctx, name):
    _, R, GV = o.shape
    T = R - n_ctx
    DV = GV // GLA_HEADS
    tm = _pick(n_ctx, 256, 8)
    ro = n_ctx // tm

    def body(o0_ref, o1_ref, rb_ref, gn_ref, dp_ref, do_ref, drb_ref, acc_ref):
        i = pl.program_id(0)

        @pl.when(i == 0)
        def _():
            acc_ref[...] = jnp.zeros_like(acc_ref)

        gnv = gn_ref[...]
        dgn = jnp.zeros((1, DV), F32)
        for h in range(GLA_HEADS):
            sl = slice(h * DV, (h + 1) * DV)
            og = o0_ref[0, :, sl] + o1_ref[0, :, sl]
            rb = rb_ref[:, sl]
            d = dp_ref[:, sl]
            r = lax.rsqrt(jnp.mean(og * og, axis=-1, keepdims=True) + EPS)
            xh = og * r
            drb_ref[:, sl] = (d * xh * gnv * _dsilu(rb)).astype(BF16)
            dn = d * _silu(rb)
            dgn = dgn + jnp.sum(dn * xh, axis=0, keepdims=True)
            dxh = dn * gnv
            do_ref[:, sl] = (r * (dxh - xh * jnp.mean(dxh * xh, axis=-1, keepdims=True))).astype(BF16)
        acc_ref[0:1, :] += dgn

    row = pl.BlockSpec((tm, GV), lambda i: (i, 0))
    return pl.pallas_call(
        body, name=name, grid=(T // tm,),
        in_specs=[pl.BlockSpec((1, tm, GV), lambda i: (0, i + ro, 0)), pl.BlockSpec((1, tm, GV), lambda i: (1, i + ro, 0)),
                  pl.BlockSpec((tm, GV), lambda i: (i + ro, rbblk)), pl.BlockSpec((1, DV), lambda i: (0, 0)), row],
        out_specs=[row, row, pl.BlockSpec((8, DV), lambda i: (0, 0))],
        out_shape=[jax.ShapeDtypeStruct((T, GV), BF16), jax.ShapeDtypeStruct((T, GV), BF16),
                   jax.ShapeDtypeStruct((8, DV), F32)],
        compiler_params=_cparams("arbitrary"))(o, o, z, gn, dp)


def _gate_fwd(z, gablk, gbblk, ya, yg, n_ctx, name):
    T, D = ya.shape
    tm = _pick(n_ctx, 256, 8)
    ro = n_ctx // tm

    def body(ga_ref, gb_ref, ya_ref, yg_ref, m_ref):
        m_ref[...] = (_sigmoid(ga_ref[...]) * ya_ref[...] + _sigmoid(gb_ref[...]) * yg_ref[...]).astype(BF16)

    row = pl.BlockSpec((tm, D), lambda i: (i, 0))
    return pl.pallas_call(
        body, name=name, grid=(T // tm,),
        in_specs=[pl.BlockSpec((tm, D), lambda i: (i + ro, gablk)), pl.BlockSpec((tm, D), lambda i: (i + ro, gbblk)), row, row],
        out_specs=row, out_shape=jax.ShapeDtypeStruct((T, D), BF16), compiler_params=_cparams("parallel"))(z, z, ya, yg)


def _gate_bwd(z, gablk, gbblk, ya, yg, dm, n_ctx, name):
    T, D = ya.shape
    tm = _pick(n_ctx, 256, 8)
    ro = n_ctx // tm

    def body(ga_ref, gb_ref, ya_ref, yg_ref, dm_ref, dya_ref, dyg_ref, dga_ref, dgb_ref):
        d = dm_ref[...]
        sa, sb = _sigmoid(ga_ref[...]), _sigmoid(gb_ref[...])
        dya_ref[...] = (d * sa).astype(BF16)
        dyg_ref[...] = (d * sb).astype(BF16)
        dga_ref[...] = (d * ya_ref[...] * sa * (1.0 - sa)).astype(BF16)
        dgb_ref[...] = (d * yg_ref[...] * sb * (1.0 - sb)).astype(BF16)

    row = pl.BlockSpec((tm, D), lambda i: (i, 0))
    sh = jax.ShapeDtypeStruct((T, D), BF16)
    return pl.pallas_call(
        body, name=name, grid=(T // tm,),
        in_specs=[pl.BlockSpec((tm, D), lambda i: (i + ro, gablk)), pl.BlockSpec((tm, D), lambda i: (i + ro, gbblk)), row, row, row],
        out_specs=[row] * 4, out_shape=[sh] * 4, compiler_params=_cparams("parallel"))(z, z, ya, yg, dm)


def _resnorm_fwd(x, mix, gt, g, sc, sh, name):
    T, D = x.shape
    tm = _pick(T, 256, 8)

    def body(x_ref, mix_ref, gt_ref, g_ref, sc_ref, sh_ref, x1_ref, h_ref):
        x1 = x_ref[...] + gt_ref[...] * mix_ref[...]
        x1_ref[...] = x1
        r = lax.rsqrt(jnp.mean(x1 * x1, axis=-1, keepdims=True) + EPS)
        h_ref[...] = (x1 * r * g_ref[...] * (1.0 + sc_ref[...]) + sh_ref[...]).astype(BF16)

    row = pl.BlockSpec((tm, D), lambda i: (i, 0))
    vec = pl.BlockSpec((1, D), lambda i: (0, 0))
    return pl.pallas_call(
        body, name=name, grid=(T // tm,), in_specs=[row, row, vec, vec, vec, vec], out_specs=[row, row],
        out_shape=[jax.ShapeDtypeStruct((T, D), F32), jax.ShapeDtypeStruct((T, D), BF16)],
        compiler_params=_cparams("parallel"))(x, mix, gt, g, sc, sh)


def _gate_resid_bwd(dx, val, gt, name):
    T, D = dx.shape
    tm = _pick(T, 256, 8)

    def body(dx_ref, val_ref, gt_ref, d_ref, acc_ref):
        i = pl.program_id(0)

        @pl.when(i == 0)
        def _():
            acc_ref[...] = jnp.zeros_like(acc_ref)

        d = dx_ref[...]
        d_ref[...] = (d * gt_ref[...]).astype(BF16)
        acc_ref[0:1, :] += jnp.sum(d * val_ref[...], axis=0, keepdims=True)

    row = pl.BlockSpec((tm, D), lambda i: (i, 0))
    return pl.pallas_call(
        body, name=name, grid=(T // tm,), in_specs=[row, row, pl.BlockSpec((1, D), lambda i: (0, 0))],
        out_specs=[row, pl.BlockSpec((8, D), lambda i: (0, 0))],
        out_shape=[jax.ShapeDtypeStruct((T, D), BF16), jax.ShapeDtypeStruct((8, D), F32)],
        compiler_params=_cparams("arbitrary"))(dx, val, gt)


def _loss_head(d, x1, gt, target, name):
    T, D = d.shape
    tm = _pick(T, 256, 8)

    def body(d_ref, x1_ref, gt_ref, t_ref, dy_ref, acc_ref):
        i = pl.program_id(0)

        @pl.when(i == 0)
        def _():
            acc_ref[...] = jnp.zeros_like(acc_ref)

        e = x1_ref[...] + gt_ref[...] * d_ref[...] - t_ref[...]
        dy_ref[...] = e * (1.0 / D)
        acc_ref[0:1, :] += jnp.sum(e * e, axis=0, keepdims=True)

    row = pl.BlockSpec((tm, D), lambda i: (i, 0))
    return pl.pallas_call(
        body, name=name, grid=(T // tm,), in_specs=[row, row, pl.BlockSpec((1, D), lambda i: (0, 0)), row],
        out_specs=[row, pl.BlockSpec((8, D), lambda i: (0, 0))],
        out_shape=[jax.ShapeDtypeStruct((T, D), F32), jax.ShapeDtypeStruct((8, D), F32)],
        compiler_params=_cparams("arbitrary"))(d, x1, gt, target)


def _halo_specs(T, tm, tw, col_of, order):
    n8 = tm // 8
    if order == "ij":
        mid = lambda i, j: (i, col_of(j))
        prev = lambda i, j: (jnp.maximum(i * n8 - 1, 0), col_of(j))
        nxt = lambda i, j: (jnp.minimum((i + 1) * n8, T // 8 - 1), col_of(j))
    else:
        mid = lambda j, i: (i, col_of(j))
        prev = lambda j, i: (jnp.maximum(i * n8 - 1, 0), col_of(j))
        nxt = lambda j, i: (jnp.minimum((i + 1) * n8, T // 8 - 1), col_of(j))
    return [pl.BlockSpec((tm, tw), mid), pl.BlockSpec((8, tw), prev), pl.BlockSpec((8, tw), nxt)]


def _shifted(u_ref, up_ref, un_ref, i, nt):
    u = u_ref[...]
    tm = u.shape[0]
    row = lax.broadcasted_iota(jnp.int32, u.shape, 0)
    hp = jnp.where(i > 0, up_ref[7:8, :], 0.0)
    hn = jnp.where(i < nt - 1, un_ref[0:1, :], 0.0)
    u_prev = jnp.where(row == 0, hp, pltpu.roll(u, 1, 0))
    u_next = jnp.where(row == tm - 1, hn, pltpu.roll(u, tm - 1, 0))
    return u_prev, u, u_next


def _conv_fwd(u, cw, cb, name):
    T, F2 = u.shape
    F = F2 // 2
    tm, tw = _pick(T, 256, 8), _pick(F, 512)
    nt, nw = T // tm, F // tw

    def body(ua, uap, uan, ug, ugp, ugn, cwa, cwg, cba, cbg, f_ref):
        i = pl.program_id(0)

        def conv(u_ref, up_ref, un_ref, w_ref, b_ref):
            p, m, n = _shifted(u_ref, up_ref, un_ref, i, nt)
            return p * w_ref[0:1, :] + m * w_ref[1:2, :] + n * w_ref[2:3, :] + b_ref[...]

        a = conv(ua, uap, uan, cwa, cba)
        g = conv(ug, ugp, ugn, cwg, cbg)
        f_ref[...] = (_silu(a) * g).astype(BF16)

    wspec = lambda off: pl.BlockSpec((3, tw), lambda i, j: (0, j + off))
    bspec = lambda off: pl.BlockSpec((1, tw), lambda i, j: (0, j + off))
    return pl.pallas_call(
        body, name=name, grid=(nt, nw),
        in_specs=_halo_specs(T, tm, tw, lambda j: j, "ij") + _halo_specs(T, tm, tw, lambda j: j + nw, "ij")
        + [wspec(0), wspec(nw), bspec(0), bspec(nw)],
        out_specs=pl.BlockSpec((tm, tw), lambda i, j: (i, j)),
        out_shape=jax.ShapeDtypeStruct((T, F), BF16), compiler_params=_cparams("parallel", "parallel"),
    )(u, u, u, u, u, u, cw, cw, cb, cb)


def _shift_rows(x, before, after):
    tm = x.shape[0]
    row = lax.broadcasted_iota(jnp.int32, x.shape, 0)
    return (jnp.where(row == 0, before, pltpu.roll(x, 1, 0)),
            jnp.where(row == tm - 1, after, pltpu.roll(x, tm - 1, 0)))


def _conv_bwd(u, df, cw, cb, name):
    T, F2 = u.shape
    F = F2 // 2
    tm, tw = _pick(T, 256, 8), _pick(F, 512)
    nt, nw = T // tm, F // tw

    def body(ua, uap, uan, ug, ugp, ugn, cwa, cwg, cba, cbg, df_ref, dfp, dfn, dua_ref, dug_ref, acca_ref, accg_ref):
        i = pl.program_id(1)

        @pl.when(i == 0)
        def _():
            acca_ref[...] = jnp.zeros_like(acca_ref)
            accg_ref[...] = jnp.zeros_like(accg_ref)

        first, last = i == 0, i == nt - 1
        wa, wg, ba, bg = cwa[...], cwg[...], cba[...], cbg[...]

        def conv(p, m, n, w, b):
            return p * w[0:1] + m * w[1:2] + n * w[2:3] + b

        def grads(a, g, d):
            return d * g * _dsilu(a), d * _silu(a)

        xa, xg, d = ua[...], ug[...], df_ref[...]
        sa = _shift_rows(xa, jnp.where(first, 0.0, uap[7:8, :]), jnp.where(last, 0.0, uan[0:1, :]))
        sg = _shift_rows(xg, jnp.where(first, 0.0, ugp[7:8, :]), jnp.where(last, 0.0, ugn[0:1, :]))
        da, dg = grads(conv(sa[0], xa, sa[1], wa, ba), conv(sg[0], xg, sg[1], wg, bg), d)
        da_p, dg_p = grads(conv(uap[6:7, :], uap[7:8, :], xa[0:1], wa, ba),
                           conv(ugp[6:7, :], ugp[7:8, :], xg[0:1], wg, bg), dfp[7:8, :])
        da_n, dg_n = grads(conv(xa[tm - 1:tm], uan[0:1, :], uan[1:2, :], wa, ba),
                           conv(xg[tm - 1:tm], ugn[0:1, :], ugn[1:2, :], wg, bg), dfn[0:1, :])
        ta = _shift_rows(da, jnp.where(first, 0.0, da_p), jnp.where(last, 0.0, da_n))
        tg = _shift_rows(dg, jnp.where(first, 0.0, dg_p), jnp.where(last, 0.0, dg_n))
        dua_ref[...] = (ta[1] * wa[0:1] + da * wa[1:2] + ta[0] * wa[2:3]).astype(BF16)
        dug_ref[...] = (tg[1] * wg[0:1] + dg * wg[1:2] + tg[0] * wg[2:3]).astype(BF16)
        for t, (va, vg) in enumerate(((sa[0], sg[0]), (xa, xg), (sa[1], sg[1]))):
            acca_ref[t:t + 1, :] += jnp.sum(da * va, axis=0, keepdims=True)
            accg_ref[t:t + 1, :] += jnp.sum(dg * vg, axis=0, keepdims=True)
        acca_ref[3:4, :] += jnp.sum(da, axis=0, keepdims=True)
        accg_ref[3:4, :] += jnp.sum(dg, axis=0, keepdims=True)

    wspec = lambda off: pl.BlockSpec((3, tw), lambda j, i: (0, j + off))
    bspec = lambda off: pl.BlockSpec((1, tw), lambda j, i: (0, j + off))
    row = pl.BlockSpec((tm, tw), lambda j, i: (i, j))
    acc = pl.BlockSpec((8, tw), lambda j, i: (0, j))
    return pl.pallas_call(
        body, name=name, grid=(nw, nt),
        in_specs=_halo_specs(T, tm, tw, lambda j: j, "ji") + _halo_specs(T, tm, tw, lambda j: j + nw, "ji")
        + [wspec(0), wspec(nw), bspec(0), bspec(nw)] + _halo_specs(T, tm, tw, lambda j: j, "ji"),
        out_specs=[row, row, acc, acc],
        out_shape=[jax.ShapeDtypeStruct((T, F), BF16), jax.ShapeDtypeStruct((T, F), BF16),
                   jax.ShapeDtypeStruct((8, F), F32), jax.ShapeDtypeStruct((8, F), F32)],
        compiler_params=_cparams("parallel", "arbitrary"),
    )(u, u, u, u, u, u, cw, cw, cb, cb, df, df, df)


def _assemble_dz(lay, Z, n_ctx, dqa, drb, dga, dgb, dka, dva, dvg, dqg, dkg, dlr, name):
    T = dqa.shape[0]
    R = T + n_ctx
    tm = _pick(n_ctx, 128, 8)
    cb = n_ctx // tm

    def body(dqa_ref, drb_ref, dga_ref, dgb_ref, dka_ref, dva_ref, dvg0, dvg1, dqg0, dqg1, dkg0, dkg1, dlr_ref, o_ref):
        lat = pl.program_id(0) >= cb

        def put(seg, val):
            o_ref[:, lay[seg]:lay[seg] + val.shape[1]] = val.astype(BF16)

        def lat_only(ref):
            v = ref[...]
            return jnp.where(lat, v, jnp.zeros_like(v))

        put("qa", lat_only(dqa_ref))
        put("rb", lat_only(drb_ref))
        put("ga", lat_only(dga_ref))
        put("gb", lat_only(dgb_ref))
        put("ka", dka_ref[...])
        put("va", dva_ref[...])
        put("vb", dvg0[0] + dvg1[0])
        put("qb", dqg0[0] + dqg1[0])
        put("kb", dkg0[0] + dkg1[0])
        put("lr", dlr_ref[...])

    lat_spec = lambda a: pl.BlockSpec((tm, a.shape[1]), lambda i: (jnp.maximum(i - cb, 0), 0))
    all_spec = lambda a: pl.BlockSpec((tm, a.shape[1]), lambda i: (i, 0))
    dir_specs = lambda a: [pl.BlockSpec((1, tm, a.shape[2]), lambda i: (0, i, 0)),
                           pl.BlockSpec((1, tm, a.shape[2]), lambda i: (1, i, 0))]
    return pl.pallas_call(
        body, name=name, grid=(R // tm,),
        in_specs=[lat_spec(dqa), lat_spec(drb), lat_spec(dga), lat_spec(dgb), all_spec(dka), all_spec(dva)]
        + dir_specs(dvg) + dir_specs(dqg) + dir_specs(dkg) + [all_spec(dlr)],
        out_specs=pl.BlockSpec((tm, Z), lambda i: (i, 0)),
        out_shape=jax.ShapeDtypeStruct((R, Z), BF16), compiler_params=_cparams("parallel"),
    )(dqa, drb, dga, dgb, dka, dva, dvg, dvg, dqg, dqg, dkg, dkg, dlr)


def _mod_fwd(ca, w, b, name):
    n, D = ca.shape
    N = w.shape[1]
    tn = _pick(N, 512)

    def body(c_ref, w_ref, b_ref, o_ref, s_ref):
        s = _silu(c_ref[...])
        s_ref[...] = s
        o_ref[...] = _dot(s.astype(BF16), w_ref[...].astype(BF16), NN) + b_ref[...]

    return pl.pallas_call(
        body, name=name, grid=(N // tn,),
        in_specs=[pl.BlockSpec((n, D), lambda j: (0, 0)), pl.BlockSpec((D, tn), lambda j: (0, j)),
                  pl.BlockSpec((1, tn), lambda j: (0, j))],
        out_specs=[pl.BlockSpec((n, tn), lambda j: (0, j)), pl.BlockSpec((n, D), lambda j: (0, 0))],
        out_shape=[jax.ShapeDtypeStruct((n, N), F32), jax.ShapeDtypeStruct((n, D), F32)],
        compiler_params=_cparams("arbitrary"))(ca, w, b)


def _silu_bwd(dsil, ca, name):
    def body(d_ref, c_ref, o_ref):
        o_ref[...] = d_ref[...] * _dsilu(c_ref[...])

    return pl.pallas_call(body, name=name, out_shape=jax.ShapeDtypeStruct(ca.shape, F32))(dsil, ca)


def _adam_math(w, g, m, v):
    c1 = 1.0 - ADAM_B1 ** ADAM_STEP
    c2 = 1.0 - ADAM_B2 ** ADAM_STEP
    mn = ADAM_B1 * m + (1.0 - ADAM_B1) * g
    vn = ADAM_B2 * v + (1.0 - ADAM_B2) * (g * g)
    return -ADAM_LR * ((mn / c1) / (jnp.sqrt(vn / c2) + ADAM_EPS) + ADAM_WD * w), mn, vn


def _adamw(w, g, m, v, name):
    Rw, Cw = w.shape
    tr = _pick(Rw, 128, 8)

    def body(w_ref, g_ref, m_ref, v_ref, d_ref, mo_ref, vo_ref):
        d_ref[...], mo_ref[...], vo_ref[...] = _adam_math(w_ref[...], g_ref[...], m_ref[...], v_ref[...])

    row = pl.BlockSpec((tr, Cw), lambda i: (i, 0))
    sh = jax.ShapeDtypeStruct((Rw, Cw), F32)
    return pl.pallas_call(body, name=name, grid=(Rw // tr,), in_specs=[row] * 4, out_specs=[row] * 3,
                          out_shape=[sh] * 3, compiler_params=_cparams("parallel"))(w, g, m, v)


HBM_SPEC = pl.BlockSpec(memory_space=pltpu.HBM)


def _exchange(inputs, out_shapes, stages, name):
    n_in, n_out = len(inputs), len(out_shapes)
    n = sum(len(s) for s in stages)

    def body(*refs):
        ins, outs = refs[:n_in], refs[n_in:n_in + n_out]
        send_sems, recv_sems = refs[n_in + n_out:]
        k = 0
        for stage in stages:
            copies = _stage_copies(stage, ins, outs, send_sems, recv_sems, k)
            for cp in copies:
                cp.start()
            for cp in copies:
                cp.wait()
            k += len(stage)

    return pl.pallas_call(
        body, name=name, in_specs=[HBM_SPEC] * n_in, out_specs=[HBM_SPEC] * n_out, out_shape=out_shapes,
        scratch_shapes=[pltpu.SemaphoreType.DMA((n,)), pltpu.SemaphoreType.DMA((n,))],
    )(*inputs)


def _stage_copies(stage, ins, outs, send_sems, recv_sems, k0=0):
    me = (lax.axis_index("x"), lax.axis_index("y"), lax.axis_index("c"))
    copies = []
    for k, ((skind, sidx), sfn, didx, dfn, flip) in enumerate(stage):
        src = (ins if skind == "in" else outs)[sidx].at[sfn(*me)]
        dst = outs[didx].at[dfn(*me)]
        if flip == (0, 0, 0):
            copies.append(pltpu.make_async_copy(src, dst, send_sems.at[k0 + k]))
        else:
            peer = tuple(1 - a if f else a for a, f in zip(me, flip))
            copies.append(pltpu.make_async_remote_copy(src, dst, send_sems.at[k0 + k], recv_sems.at[k0 + k],
                                                       device_id=peer, device_id_type=MESH))
    return copies


def _pcall(body, *, name, grid, in_specs, out_specs, out_shape, scratch_shapes=(), sem, args, ride=None):
    many = isinstance(out_shape, (list, tuple))
    out_specs, out_shape = (list(out_specs), list(out_shape)) if many else ([out_specs], [out_shape])
    if ride is None:
        res = pl.pallas_call(body, name=name, grid=grid, in_specs=list(in_specs), out_specs=out_specs,
                             out_shape=out_shape, scratch_shapes=list(scratch_shapes),
                             compiler_params=_cparams(*sem))(*args)
        return res if many else res[0]
    x_in, x_out, stage, aliases = ride
    n_in, n_out, n_scr, n_xin, n_xout = len(in_specs), len(out_specs), len(scratch_shapes), len(x_in), len(x_out)

    def wrapped(*refs):
        ins, xins = refs[:n_in], refs[n_in:n_in + n_xin]
        o0 = n_in + n_xin
        outs, xouts = refs[o0:o0 + n_out], refs[o0 + n_out:o0 + n_out + n_xout]
        s0 = o0 + n_out + n_xout
        scr, (send_sems, recv_sems) = refs[s0:s0 + n_scr], refs[s0 + n_scr:]
        first = functools.reduce(jnp.logical_and, [pl.program_id(d) == 0 for d in range(len(grid))])
        last = functools.reduce(jnp.logical_and, [pl.program_id(d) == grid[d] - 1 for d in range(len(grid))])

        @pl.when(first)
        def _():
            for cp in _stage_copies(stage, xins, xouts, send_sems, recv_sems):
                cp.start()

        body(*ins, *outs, *scr)

        @pl.when(last)
        def _():
            for cp in _stage_copies(stage, xins, xouts, send_sems, recv_sems):
                cp.wait()

    res = pl.pallas_call(
        wrapped, name=name, grid=grid, in_specs=list(in_specs) + [HBM_SPEC] * n_xin,
        out_specs=out_specs + [HBM_SPEC] * n_xout, out_shape=out_shape + list(x_out),
        scratch_shapes=list(scratch_shapes) + [pltpu.SemaphoreType.DMA((len(stage),)),
                                               pltpu.SemaphoreType.DMA((len(stage),))],
        input_output_aliases={n_in + a: n_out + b for a, b in aliases.items()},
        compiler_params=_cparams(*(["arbitrary"] * len(grid))))(*args, *x_in)
    main = res[:n_out]
    return (main if many else main[0]), list(res[n_out:])


FLIPS_ALL = [(0, 0, 1), (0, 1, 0), (0, 1, 1), (1, 0, 0), (1, 0, 1), (1, 1, 0), (1, 1, 1)]
FLIPS_CHIP = [(0, 1, 0), (1, 0, 0), (1, 1, 0)]


def _sum_slots(buf, name):
    n, r, w = buf.shape
    tr = _pick(r, 256, 8)

    def body(b_ref, o_ref):
        acc = b_ref[0]
        for s in range(1, n):
            acc = acc + b_ref[s]
        o_ref[...] = acc

    return pl.pallas_call(
        body, name=name, grid=(r // tr,), in_specs=[pl.BlockSpec((n, tr, w), lambda i: (0, i, 0))],
        out_specs=pl.BlockSpec((tr, w), lambda i: (i, 0)), out_shape=jax.ShapeDtypeStruct((r, w), F32),
        compiler_params=_cparams("parallel"))(buf)


def _allreduce(buf, name):
    r, w = buf.shape
    whole = lambda x, y, c: (slice(None), slice(None))
    slot = lambda x, y, c: (4 * x + 2 * y + c,)
    stage = [(("in", 0), whole, 0, slot, f) for f in [(0, 0, 0)] + FLIPS_ALL]
    (slots,) = _exchange([buf], [jax.ShapeDtypeStruct((8, r, w), F32)], [stage], name + "_x")
    return _sum_slots(slots, name + "_sum")


def _gather_plan(shards, src):
    half = lambda a, c: pl.ds(c * (a.shape[0] // 2), a.shape[0] // 2)
    first, second = [], []
    for n, a in enumerate(shards):
        for f in FLIPS_CHIP:
            first.append((("in", n), lambda x, y, c, a=a: (half(a, c), slice(None)), n,
                          lambda x, y, c, a=a: (2 * x + y, half(a, c), slice(None)), f))
            peer_slot = lambda x, y, c, a=a, f=f: (2 * (x ^ f[0]) + (y ^ f[1]), half(a, c), slice(None))
            second.append(((src, n), peer_slot, n, peer_slot, (0, 0, 1)))
    outs = [jax.ShapeDtypeStruct((4,) + a.shape, a.dtype) for a in shards]
    return first, second, outs


def _allgather_weights(shards, name):
    first, second, outs = _gather_plan(shards, "out")
    return _exchange(shards, outs, [first, second], name)


def _add_pair(G, bufA, cvec, name):
    _, Rs, Cs = G.shape
    Rh = Rs // 2
    tr = _pick(Rh, 128, 16)
    nb = Rh // tr

    def body(c_ref, g_ref, a_ref, o_ref):
        o_ref[...] = (g_ref[...] + a_ref[...]).astype(BF16)

    grid_spec = pltpu.PrefetchScalarGridSpec(
        num_scalar_prefetch=1, grid=(4, nb),
        in_specs=[pl.BlockSpec((1, tr, Cs), lambda s, i, c_ref: (s, c_ref[0] * nb + i, 0)),
                  pl.BlockSpec((1, tr, Cs), lambda s, i, c_ref: (s, i, 0))],
        out_specs=pl.BlockSpec((1, tr, Cs), lambda s, i, c_ref: (s, i, 0)))
    return pl.pallas_call(body, name=name, grid_spec=grid_spec, out_shape=jax.ShapeDtypeStruct((4, Rh, Cs), BF16),
                          compiler_params=_cparams("parallel", "parallel"))(cvec, G, bufA)


def _sum_chips(G, bufA, bufB, cvec, svec, name):
    _, Rs, Cs = G.shape
    Rh = Rs // 2
    tr = _pick(Rh, 128, 16)
    nb = Rh // tr

    def body(c_ref, s_ref, g_ref, a_ref, b_ref, o_ref):
        o_ref[...] = (g_ref[0] + a_ref[0]) + b_ref[0].astype(F32) + b_ref[1].astype(F32) + b_ref[2].astype(F32)

    grid_spec = pltpu.PrefetchScalarGridSpec(
        num_scalar_prefetch=2, grid=(nb,),
        in_specs=[pl.BlockSpec((1, tr, Cs), lambda i, c, s: (s[0], c[0] * nb + i, 0)),
                  pl.BlockSpec((1, tr, Cs), lambda i, c, s: (s[0], i, 0)),
                  pl.BlockSpec((3, tr, Cs), lambda i, c, s: (0, i, 0))],
        out_specs=pl.BlockSpec((tr, Cs), lambda i, c, s: (i, 0)))
    return pl.pallas_call(body, name=name, grid_spec=grid_spec, out_shape=jax.ShapeDtypeStruct((Rh, Cs), F32),
                          compiler_params=_cparams("parallel"))(cvec, svec, G, bufA, bufB)


def _pair_plan(grads):
    Rh = [g.shape[1] // 2 for g in grads]
    whole3 = lambda x, y, c: (slice(None), slice(None), slice(None))
    stage = [(("in", n), lambda x, y, c, n=n: (slice(None), pl.ds((1 - c) * Rh[n], Rh[n]), slice(None)), n,
              whole3, (0, 0, 1)) for n in range(len(grads))]
    return [jax.ShapeDtypeStruct((4, Rh[n], g.shape[2]), F32) for n, g in enumerate(grads)], stage


def _chips_plan(P):
    stage = [(("in", n), lambda x, y, c, f=f: (2 * (x ^ f[0]) + (y ^ f[1]),), n, lambda x, y, c, k=k: (k,), f)
             for n in range(len(P)) for k, f in enumerate(FLIPS_CHIP)]
    return [jax.ShapeDtypeStruct((3,) + p.shape[1:], BF16) for p in P], stage


def _halves_plan(mine):
    whole2 = lambda x, y, c: (slice(None), slice(None))
    stage = [(("in", n), whole2, n, whole2, (0, 0, 1)) for n in range(len(mine))]
    return [jax.ShapeDtypeStruct(r.shape, F32) for r in mine], stage


def _adamw_halves(w, mine, other, m, v, cvec, name):
    Rs, Cs = w.shape
    Rh = Rs // 2
    tr = _pick(Rh, 128, 8)
    nb = Rh // tr

    def body(c_ref, w_ref, a_ref, b_ref, m_ref, v_ref, g_ref, d_ref, mo_ref, vo_ref):
        gv = jnp.where(pl.program_id(0) // nb == c_ref[0], a_ref[...], b_ref[...])
        g_ref[...] = gv
        d_ref[...], mo_ref[...], vo_ref[...] = _adam_math(w_ref[...], gv, m_ref[...], v_ref[...])

    row = pl.BlockSpec((tr, Cs), lambda i, c: (i, 0))
    hrow = pl.BlockSpec((tr, Cs), lambda i, c: (i % nb, 0))
    grid_spec = pltpu.PrefetchScalarGridSpec(num_scalar_prefetch=1, grid=(2 * nb,),
                                             in_specs=[row, hrow, hrow, row, row], out_specs=[row] * 4)
    return pl.pallas_call(body, name=name, grid_spec=grid_spec, out_shape=[jax.ShapeDtypeStruct((Rs, Cs), F32)] * 4,
                          compiler_params=_cparams("parallel"))(cvec, w, mine, other, m, v)


def _pack(arrays):
    flat = [a.reshape(-1).astype(F32) for a in arrays]
    meta, off = [], 0
    for a, f in zip(arrays, flat):
        meta.append((off, a.shape))
        off += f.shape[0]
    total = -(-off // (8 * LANES)) * (8 * LANES)
    flat.append(jnp.zeros((total - off,), F32))
    return jnp.concatenate(flat).reshape(total // LANES, LANES), meta


def _unpack(buf, meta):
    flat = buf.reshape(-1)
    out = []
    for off, shape in meta:
        size = 1
        for s in shape:
            size *= s
        out.append(flat[off:off + size].reshape(shape))
    return out


WEIGHT_NAMES = ["c_ctx", "w_mod", "b_mod", "g_mix", "w_in", "q_norm", "k_norm", "attn_sink", "w_gate_f", "b_gate_f",
                "w_gate_b", "b_gate_b", "gla_norm", "w_attn_o", "w_gla_o", "w_out", "g_ffn", "w_up", "conv_w",
                "conv_b", "w_down"]
BIG_NAMES = ["w_in", "w_attn_o", "w_gla_o", "w_out", "w_up", "w_down"]
SHARDED_SMALL = ["w_gate_f", "w_gate_b", "conv_w"]


def _layouts(D):
    aw, kvw, gk, gv = N_Q_HEADS * HEAD_DIM, N_KV_HEADS * HEAD_DIM, D // 2, D
    widths = {"qa": aw, "ka": kvw, "va": kvw, "qb": gk, "kb": gk, "vb": gv, "rb": gv, "lr": 2 * GLA_LOWRANK,
              "ga": D, "gb": D}
    orig, off = {}, 0
    for s in ["qa", "ka", "va", "qb", "kb", "vb", "rb", "lr", "ga", "gb"]:
        orig[s] = off
        off += widths[s]
    order = ["qa", "vb", "rb", "ga", "gb", "ka", "va", "qb", "kb", "lr"]
    lay, off = {}, 0
    for s in order:
        lay[s] = off
        off += LANES if s == "lr" else widths[s]
    align = {"qa": aw, "vb": D, "rb": D, "ga": D, "gb": D, "ka": kvw, "va": kvw, "qb": gk, "kb": gk,
             "lr": LANES}
    for s in order:
        assert lay[s] % align[s] == 0, (s, lay[s], align[s])
    return widths, orig, order, lay, off


def _rope_tables(T, L):
    t = jnp.arange(T)
    nf = HEAD_DIM // 4
    inv = ROPE_THETA ** (-jnp.arange(nf, dtype=F32) / nf)
    ang = jnp.concatenate([(t // GRID_W)[:, None] * inv, (t % GRID_W)[:, None] * inv], axis=-1)
    cos, sin = jnp.cos(ang), jnp.sin(ang)
    cos2 = jnp.concatenate([jnp.ones((L, HEAD_DIM), F32), jnp.concatenate([cos, cos], axis=-1)], axis=0)
    sin2 = jnp.concatenate([jnp.zeros((L, HEAD_DIM), F32), jnp.concatenate([-sin, sin], axis=-1)], axis=0)
    return cos2, sin2


def _step(x, c, ctx, loss_target, W, M, V):
    xi, yi, ci = lax.axis_index("x"), lax.axis_index("y"), lax.axis_index("c")
    chip = 2 * xi + yi
    dev = 2 * chip + ci
    south = (ci == 0).astype(F32)
    cvec = ci.reshape(1).astype(jnp.int32)
    T, D = x.shape[1], x.shape[2]
    L = ctx.shape[1]
    R = L + T
    F = 4 * W["w_down"].shape[1]
    GK, GV = D // 2, D
    DK, DV = GK // GLA_HEADS, GV // GLA_HEADS
    N6 = 6 * D
    N4 = N6 // 4
    widths, orig, order, lay, Z = _layouts(D)

    def place_cols(shard, full_cols):
        cols = shard.shape[-1]
        full = jnp.zeros(shard.shape[:-1] + (full_cols,), F32)
        return lax.dynamic_update_slice(full, shard * south, (0,) * (shard.ndim - 1) + (chip * cols,))

    c_rows = lax.dynamic_update_slice(jnp.zeros((8, D), F32), c, (dev, 0))
    bufa, meta = _pack([c_rows, place_cols(W["w_gate_f"][0], GK), place_cols(W["w_gate_b"][0], GK),
                        place_cols(W["conv_w"][0], 2 * F)])
    c_all, wgf, wgb, cw = _unpack(_allreduce(bufa, "gather_small"), meta)
    ca = jnp.concatenate([c_all, W["c_ctx"][None, :], jnp.zeros((7, D), F32)], axis=0)
    b_shard = lax.dynamic_slice(W["b_mod"], (0, chip * N4), (1, N4))
    mod_part, sil = _mod_fwd(ca, W["w_mod"][0], b_shard, "mod_fwd")
    slots = lax.dynamic_update_slice(jnp.zeros((4, 16, N4), F32), (mod_part * south)[None], (chip, 0, 0))
    mod_all = _allreduce(slots.reshape(64, N4), "gather_mod").reshape(4, 16, N4).transpose(1, 0, 2).reshape(16, N6)
    mx = lax.dynamic_slice(mod_all, (dev, 0), (1, N6)).reshape(6, 1, D)
    mc = mod_all[8].reshape(6, 1, D)

    sq = lambda a: a.reshape(a.shape[1:])
    shards = [sq(W[n]).astype(BF16) for n in BIG_NAMES]
    own = lambda g, s: lax.dynamic_update_slice(g, s[None], (chip, 0, 0))
    cols = lambda g: g.transpose(1, 0, 2).reshape(g.shape[1], 4 * g.shape[2])
    rows = lambda g: g.reshape(4 * g.shape[1], g.shape[2])
    w_in_f = cols(own(_allgather_weights(shards[:1], "gather_w_in")[0], shards[0]))
    seg = lambda s: w_in_f[:, orig[s]:orig[s] + widths[s]]
    w_cat = jnp.concatenate([jnp.pad(seg(s), ((0, 0), (0, LANES - widths[s]))) if s == "lr" else seg(s)
                             for s in order], axis=1)
    gather1, gather2, gather_outs = _gather_plan(shards[1:], "in")
    wg = jnp.zeros((2, LANES, GK), F32).at[0, :GLA_LOWRANK].set(wgf).at[1, GLA_LOWRANK:2 * GLA_LOWRANK].set(wgb)
    bg = jnp.stack([W["b_gate_f"], W["b_gate_b"]])
    cb = W["conv_b"]
    sink_rows = jnp.broadcast_to(W["attn_sink"][0][:, None], (N_Q_HEADS, HEAD_DIM))
    cos2, sin2 = _rope_tables(T, L)
    blk = lambda s, w: lay[s] // w

    xall = jnp.concatenate([ctx[0], x[0]], axis=0)
    sc1 = jnp.stack([mc[1], mx[1]])
    sh1 = jnp.stack([mc[0], mx[0]])
    h = _modnorm_fwd(xall, W["g_mix"], sc1, sh1, L, "modnorm1")
    z, landed = _matmul(h, w_cat, "nn", F32, "proj_in", ride=(shards[1:], gather_outs, gather1, {}))
    qn = _qknorm_fwd(z, blk("qa", widths["qa"]), T, L, W["q_norm"], cos2, sin2, N_Q_HEADS, "qnorm")
    kn = _qknorm_fwd(z, blk("ka", widths["ka"]), R, 0, W["k_norm"], cos2, sin2, N_KV_HEADS, "knorm")
    vb = _cast_seg(z, blk("va", widths["va"]), widths["va"], "vcast")
    o_attn, landed = _attn_fwd(qn, kn, vb, sink_rows, L, "attn_fwd",
                               ride=(landed, gather_outs, gather2, {n: n for n in range(len(landed))}))
    g_ao, g_go, g_out, g_up, g_dn = [own(g, s) for g, s in zip(landed, shards[1:])]
    w_ao, w_go, w_out, w_up, w_dn = rows(g_ao), rows(g_go), rows(g_out), cols(g_up), rows(g_dn)
    gla_blks = (blk("qb", GK), blk("kb", GK), blk("vb", GV), blk("lr", LANES))
    o_g, sprev = _gla_fwd(z, *gla_blks, wg, bg, DV, L, "gla_fwd")
    p = _glanorm_fwd(o_g, z, blk("rb", D), W["gla_norm"], L, "glanorm")
    ya = _matmul(o_attn, w_ao, "nn", F32, "proj_attn_o")
    yg = _matmul(p, w_go, "nn", F32, "proj_gla_o")
    m = _gate_fwd(z, blk("ga", D), blk("gb", D), ya, yg, L, "gate")
    mix = _matmul(m, w_out, "nn", F32, "proj_out")
    x1, h2 = _resnorm_fwd(x[0], mix, mx[2], W["g_ffn"], mx[4], mx[3], "resnorm2")
    u = _matmul(h2, w_up, "nn", F32, "ffn_up")
    f = _conv_fwd(u, cw, cb, "conv_swiglu")
    d = _matmul(f, w_dn, "nn", F32, "ffn_down")
    dy, lacc = _loss_head(d, x1, mx[5], loss_target[0], "loss_head")
    loss = lax.psum((0.5 / D) * jnp.sum(lacc[0]), ("x", "y", "c"))

    dd, s_gt2 = _gate_resid_bwd(dy, d, mx[5], "gate2_bwd")
    gw_dn = _matmul(f, dd, "tn", F32, "ffn_down_dw")
    df = _matmul(dd, w_dn, "nt", F32, "ffn_down_dx")
    du_a, du_g, acca, accg = _conv_bwd(u, df, cw, cb, "conv_swiglu_bwd")
    du = jnp.concatenate([du_a, du_g], axis=1)
    gw_up = _matmul(h2, du, "tn", F32, "ffn_up_dw")
    dh2 = _matmul(du, w_up, "nt", F32, "ffn_up_dx")
    dx1, s2 = _modnorm_bwd(x1, dh2, W["g_ffn"], mx[4], dy, "resnorm2_bwd")
    dmix, s_gt1 = _gate_resid_bwd(dx1, mix, mx[2], "gate1_bwd")
    gw_out = _matmul(m, dmix, "tn", F32, "proj_out_dw")
    dm = _matmul(dmix, w_out, "nt", F32, "proj_out_dx")
    dya, dyg, dga, dgb = _gate_bwd(z, blk("ga", D), blk("gb", D), ya, yg, dm, L, "gate_bwd")
    gw_ao = _matmul(o_attn, dya, "tn", F32, "proj_attn_o_dw")
    do_attn = _matmul(dya, w_ao, "nt", BF16, "proj_attn_o_dx")
    gw_go = _matmul(p, dyg, "tn", F32, "proj_gla_o_dw")
    dp = _matmul(dyg, w_go, "nt", F32, "proj_gla_o_dx")
    do_gla, drb, s_gn = _glanorm_bwd(o_g, z, blk("rb", D), W["gla_norm"], dp, L, "glanorm_bwd")
    do_pad = jnp.concatenate([jnp.zeros((L, GV), BF16), do_gla], axis=0)
    by_cols = lambda g: g.reshape(g.shape[0], 4, g.shape[1] // 4).transpose(1, 0, 2)
    by_rows = lambda g: g.reshape(4, g.shape[0] // 4, g.shape[1])
    svec = chip.reshape(1).astype(jnp.int32)
    early = [by_rows(gw_ao), by_rows(gw_go), by_rows(gw_out), by_cols(gw_up), by_rows(gw_dn)]
    (dqg, dkg, dvg, dpre, dbg), pair_e = _gla_bwd(z, *gla_blks, wg, bg, sprev, do_pad, L, "gla_bwd",
                                                  ride=(early, *_pair_plan(early), {}))
    sums_e = [_add_pair(g, a, cvec, "reduce_early_add%d" % n) for n, (g, a) in enumerate(zip(early, pair_e))]
    wg_cat = jnp.concatenate([wg[0], wg[1]], axis=1)
    dlr = _matmul(dpre, wg_cat, "nt", BF16, "gla_gate_dx")
    dwg = _matmul(z[:, lay["lr"]:lay["lr"] + LANES], dpre, "tn", F32, "gla_gate_dw")
    (dqn, dkw, dvw, dkc, dvc, dsn), chips_e = _attn_bwd(qn, kn, vb, sink_rows, do_attn, L, "attn_bwd",
                                                        ride=(sums_e, *_chips_plan(sums_e), {}))
    mine_e = [_sum_chips(g, a, b, cvec, svec, "reduce_early_sum%d" % n)
              for n, (g, a, b) in enumerate(zip(early, pair_e, chips_e))]
    dqa, s_qn = _qknorm_bwd(z, blk("qa", widths["qa"]), T, L, W["q_norm"], cos2, sin2, dqn, N_Q_HEADS, "qnorm_bwd")
    dk_all = jnp.concatenate([dkc, dkw[WINDOW:WINDOW + T]], axis=0)
    dv_all = jnp.concatenate([dvc, dvw[WINDOW:WINDOW + T]], axis=0)
    dka, s_kn = _qknorm_bwd(z, blk("ka", widths["ka"]), R, 0, W["k_norm"], cos2, sin2, dk_all, N_KV_HEADS, "knorm_bwd")
    dz = _assemble_dz(lay, Z, L, dqa, drb, dga, dgb, dka, dv_all, dvg, dqg, dkg, dlr, "assemble_dz")
    gw_cat, other_e = _matmul(h, dz, "tn", F32, "proj_in_dw", ride=(mine_e, *_halves_plan(mine_e), {}))
    gw_in = jnp.concatenate([gw_cat[:, lay[s]:lay[s] + widths[s]] for s in ["qa", "ka", "va", "qb", "kb", "vb", "rb",
                                                                           "lr", "ga", "gb"]], axis=1)
    late = [by_cols(gw_in)]
    dh, pair_l = _matmul(dz, w_cat, "nt", F32, "proj_in_dx", ride=(late, *_pair_plan(late), {}))
    sums_l = [_add_pair(late[0], pair_l[0], cvec, "reduce_late_add")]
    shapes, stage = _chips_plan(sums_l)
    chips_l = _exchange(sums_l, shapes, [stage], "reduce_late_chips")
    mine_l = [_sum_chips(late[0], pair_l[0], chips_l[0], cvec, svec, "reduce_late_sum")]
    shapes, stage = _halves_plan(mine_l)
    other_l = _exchange(mine_l, shapes, [stage], "reduce_late_halves")
    mine, other = mine_l + mine_e, list(other_l) + other_e
    grad_x, s1 = _modnorm_bwd(x[0], dh, W["g_mix"], mx[1], dx1, "modnorm1_bwd", dh_roff=L)
    _, s1c = _modnorm_bwd(ctx[0], dh, W["g_mix"], mc[1], None, "modnorm1_ctx_bwd")

    dmod_x = jnp.concatenate([s1[0], s1[1], s_gt1[0], s2[0], s2[1], s_gt2[0]])
    dmod_c = jnp.concatenate([s1c[0], s1c[1], jnp.zeros((4 * D,), F32)])
    dmod_rows = lax.dynamic_update_slice(jnp.zeros((9, N6), F32).at[8].set(dmod_c), dmod_x[None], (dev, 0))
    small = [dmod_rows, dmod_x + dmod_c, s1[2] + s1c[2], s_qn[0], s_kn[0], dsn[:, 0, :Q_PER_KV].reshape(N_Q_HEADS),
             dwg[:GLA_LOWRANK, :GK], dbg[0].reshape(GK), dwg[GLA_LOWRANK:2 * GLA_LOWRANK, GK:], dbg[1].reshape(GK),
             s_gn[0], s2[2], jnp.concatenate([acca[0:3], accg[0:3]], axis=1), jnp.concatenate([acca[3], accg[3]])]
    bufc, meta = _pack(small)
    (dmod_sum, g_b_mod, g_g_mix, g_q_norm, g_k_norm, g_sink, g_wgf, g_bgf, g_wgb, g_bgb, g_gla_norm, g_g_ffn,
     g_conv_w, g_conv_b) = _unpack(_allreduce(bufc, "reduce_small"), meta)
    dmod16 = lax.dynamic_slice(jnp.concatenate([dmod_sum, jnp.zeros((7, N6), F32)], axis=0), (0, chip * N4), (16, N4))
    g_w_mod = _matmul(sil, dmod16, "tn", F32, "mod_dw")
    dsil = _matmul(dmod16, W["w_mod"][0], "nt", F32, "mod_dx")
    g_c_ctx = _silu_bwd(_allreduce(dsil * south, "reduce_cctx"), ca, "silu_bwd")[8]

    cut = lambda g: lax.dynamic_slice(g, (0, chip * (g.shape[1] // 4)), (g.shape[0], g.shape[1] // 4))
    grads = {"c_ctx": g_c_ctx, "w_mod": g_w_mod[None], "b_mod": g_b_mod[None], "g_mix": g_g_mix[None],
             "q_norm": g_q_norm[None], "k_norm": g_k_norm[None], "attn_sink": g_sink[None],
             "w_gate_f": cut(g_wgf)[None], "b_gate_f": g_bgf[None], "w_gate_b": cut(g_wgb)[None],
             "b_gate_b": g_bgb[None], "gla_norm": g_gla_norm[None], "g_ffn": g_g_ffn[None],
             "conv_w": cut(g_conv_w)[None], "conv_b": g_conv_b[None]}

    delta, new_m, new_v = {}, {}, {}
    dl, mn, vn = _adamw(W["w_mod"][0], g_w_mod, M["w_mod"][0], V["w_mod"][0], "adamw_w_mod")
    delta["w_mod"], new_m["w_mod"], new_v["w_mod"] = dl[None], mn[None], vn[None]
    for n, a, b in zip(BIG_NAMES, mine, other):
        g, dl, mn, vn = _adamw_halves(sq(W[n]), a, b, sq(M[n]), sq(V[n]), cvec, "adamw_" + n)
        grads[n], delta[n], new_m[n], new_v[n] = g[None], dl[None], mn[None], vn[None]
    small_names = [n for n in WEIGHT_NAMES if n not in delta]
    packs = [_pack([src[n] for n in small_names]) for src in (W, grads, M, V)]
    meta = packs[0][1]
    outs = _adamw(packs[0][0], packs[1][0], packs[2][0], packs[3][0], "adamw_small")
    for res, o in zip((delta, new_m, new_v), outs):
        for n, a in zip(small_names, _unpack(o, meta)):
            res[n] = a
    return (loss, grad_x[None], *[grads[n] for n in WEIGHT_NAMES], *[delta[n] for n in WEIGHT_NAMES],
            *[new_m[n] for n in WEIGHT_NAMES], *[new_v[n] for n in WEIGHT_NAMES])


def kernel(x, c, ctx, c_ctx, w_mod, b_mod, g_mix, w_in, q_norm, k_norm, attn_sink, w_gate_f, b_gate_f, w_gate_b, b_gate_b, gla_norm, w_attn_o, w_gla_o, w_out, g_ffn, w_up, conv_w, conv_b, w_down, loss_target, m_c_ctx, m_w_mod, m_b_mod, m_g_mix, m_w_in, m_q_norm, m_k_norm, m_attn_sink, m_w_gate_f, m_b_gate_f, m_w_gate_b, m_b_gate_b, m_gla_norm, m_w_attn_o, m_w_gla_o, m_w_out, m_g_ffn, m_w_up, m_conv_w, m_conv_b, m_w_down, v_c_ctx, v_w_mod, v_b_mod, v_g_mix, v_w_in, v_q_norm, v_k_norm, v_attn_sink, v_w_gate_f, v_b_gate_f, v_w_gate_b, v_b_gate_b, v_gla_norm, v_w_attn_o, v_w_gla_o, v_w_out, v_g_ffn, v_w_up, v_conv_w, v_conv_b, v_w_down):
    W = dict(zip(WEIGHT_NAMES, (c_ctx, w_mod, b_mod, g_mix, w_in, q_norm, k_norm, attn_sink, w_gate_f, b_gate_f,
                                w_gate_b, b_gate_b, gla_norm, w_attn_o, w_gla_o, w_out, g_ffn, w_up, conv_w, conv_b,
                                w_down)))
    M = dict(zip(WEIGHT_NAMES, (m_c_ctx, m_w_mod, m_b_mod, m_g_mix, m_w_in, m_q_norm, m_k_norm, m_attn_sink,
                                m_w_gate_f, m_b_gate_f, m_w_gate_b, m_b_gate_b, m_gla_norm, m_w_attn_o, m_w_gla_o,
                                m_w_out, m_g_ffn, m_w_up, m_conv_w, m_conv_b, m_w_down)))
    V = dict(zip(WEIGHT_NAMES, (v_c_ctx, v_w_mod, v_b_mod, v_g_mix, v_w_in, v_q_norm, v_k_norm, v_attn_sink,
                                v_w_gate_f, v_b_gate_f, v_w_gate_b, v_b_gate_b, v_gla_norm, v_w_attn_o, v_w_gla_o,
                                v_w_out, v_g_ffn, v_w_up, v_conv_w, v_conv_b, v_w_down)))
    return _step(x, c, ctx, loss_target, W, M, V)
```

```python
import functools
import math

import jax
import jax.numpy as jnp
from jax import lax
from jax.experimental import pallas as pl
from jax.experimental.pallas import tpu as pltpu

F32 = jnp.float32
BF16 = jnp.bfloat16
MESH = pl.DeviceIdType.MESH

EPS = 1e-6
HEAD_DIM = 128
N_Q_HEADS = 16
N_KV_HEADS = 4
Q_PER_KV = N_Q_HEADS // N_KV_HEADS
WINDOW = 128
GLA_HEADS = 4
GLA_LOWRANK = 16
GLA_GATE_NORM = 16.0
GLA_CHUNK = 64
GRID_W = 64
ROPE_THETA = 10000.0
GLA_LEVELS = (32, 16, 8, 4, 2, 1)
LANES = 128

ADAM_LR = 0.001
ADAM_B1 = 0.9
ADAM_B2 = 0.999
ADAM_EPS = 1e-08
ADAM_WD = 0.01
ADAM_STEP = 10

VMEM_LIMIT = 52 * 1024 * 1024


def _cparams(*sem):
    return pltpu.CompilerParams(dimension_semantics=sem, vmem_limit_bytes=VMEM_LIMIT)


def _pick(n, target, mult=LANES):
    best = None
    d = mult
    while d <= min(n, target):
        if n % d == 0:
            best = d
        d += mult
    return n if best is None else best


def _sigmoid(x):
    return 1.0 / (1.0 + jnp.exp(-x))


def _silu(x):
    return x * _sigmoid(x)


def _dsilu(x):
    s = _sigmoid(x)
    return s * (1.0 + x * (1.0 - s))


def _dot(a, b, dims):
    return lax.dot_general(a, b, (dims, ((), ())), preferred_element_type=F32)


NN = ((1,), (0,))
NT = ((1,), (1,))
TN = ((0,), (0,))


def _matmul(a, b, mode, out_dtype, name, tm=768, tn=1024, tk=2048, ride=None):
    if mode == "nn":
        (M, K), (K2, N) = a.shape, b.shape
    elif mode == "nt":
        (M, K), (N, K2) = a.shape, b.shape
    else:
        (K, M), (K2, N) = a.shape, b.shape
    assert K == K2, (name, a.shape, b.shape)
    if mode == "tn":
        tm = max(tm, 1024)
    tm, tn, tk = _pick(M, tm), _pick(N, tn), _pick(K, tk)
    nk = K // tk
    dims = {"nn": NN, "nt": NT, "tn": TN}[mode]

    def body(a_ref, b_ref, o_ref, acc_ref):
        k = pl.program_id(2)

        @pl.when(k == 0)
        def _():
            acc_ref[...] = jnp.zeros_like(acc_ref)

        acc_ref[...] += _dot(a_ref[...].astype(BF16), b_ref[...].astype(BF16), dims)

        @pl.when(k == nk - 1)
        def _():
            o_ref[...] = acc_ref[...].astype(out_dtype)

    if mode == "tn":
        a_spec = pl.BlockSpec((tk, tm), lambda i, j, k: (k, i))
    else:
        a_spec = pl.BlockSpec((tm, tk), lambda i, j, k: (i, k))
    if mode == "nt":
        b_spec = pl.BlockSpec((tn, tk), lambda i, j, k: (j, k))
    else:
        b_spec = pl.BlockSpec((tk, tn), lambda i, j, k: (k, j))
    return _pcall(
        body, name=name, grid=(M // tm, N // tn, nk),
        in_specs=[a_spec, b_spec],
        out_specs=pl.BlockSpec((tm, tn), lambda i, j, k: (i, j)),
        out_shape=jax.ShapeDtypeStruct((M, N), out_dtype),
        scratch_shapes=[pltpu.VMEM((tm, tn), F32)],
        sem=("parallel", "parallel", "arbitrary"), args=(a, b), ride=ride)


def _modnorm_fwd(xall, g, sc, sh, n_ctx, name):
    R, D = xall.shape
    tm = _pick(n_ctx, 256, 8)
    cb = n_ctx // tm

    def body(x_ref, g_ref, sc_ref, sh_ref, h_ref):
        x = x_ref[...]
        r = lax.rsqrt(jnp.mean(x * x, axis=-1, keepdims=True) + EPS)
        n = x * r * g_ref[...]
        h_ref[...] = (n * (1.0 + sc_ref[0]) + sh_ref[0]).astype(BF16)

    sel = lambda i: (jnp.where(i < cb, 0, 1), 0, 0)
    return pl.pallas_call(
        body, name=name, grid=(R // tm,),
        in_specs=[pl.BlockSpec((tm, D), lambda i: (i, 0)), pl.BlockSpec((1, D), lambda i: (0, 0)),
                  pl.BlockSpec((1, 1, D), sel), pl.BlockSpec((1, 1, D), sel)],
        out_specs=pl.BlockSpec((tm, D), lambda i: (i, 0)),
        out_shape=jax.ShapeDtypeStruct((R, D), BF16),
        compiler_params=_cparams("parallel"),
    )(xall, g, sc, sh)


def _modnorm_bwd(x, dh, g, sc, resid, name, dh_roff=0):
    N, D = x.shape
    tm = _pick(math.gcd(N, dh_roff), 256, 8)
    ro = dh_roff // tm
    want_dx = resid is not None

    def body(*refs):
        if want_dx:
            x_ref, dh_ref, g_ref, sc_ref, res_ref, dx_ref, acc_ref = refs
        else:
            x_ref, dh_ref, g_ref, sc_ref, acc_ref = refs
        i = pl.program_id(0)

        @pl.when(i == 0)
        def _():
            acc_ref[...] = jnp.zeros_like(acc_ref)

        xv, dhv, gv = x_ref[...], dh_ref[...], g_ref[...]
        r = lax.rsqrt(jnp.mean(xv * xv, axis=-1, keepdims=True) + EPS)
        xh = xv * r
        dn = dhv * (1.0 + sc_ref[...])
        acc_ref[0:1, :] += jnp.sum(dhv, axis=0, keepdims=True)
        acc_ref[1:2, :] += jnp.sum(dhv * xh * gv, axis=0, keepdims=True)
        acc_ref[2:3, :] += jnp.sum(dn * xh, axis=0, keepdims=True)
        if want_dx:
            dxh = dn * gv
            dx_ref[...] = res_ref[...] + r * (dxh - xh * jnp.mean(dxh * xh, axis=-1, keepdims=True))

    row = pl.BlockSpec((tm, D), lambda i: (i, 0))
    drow = pl.BlockSpec((tm, D), lambda i: (i + ro, 0))
    vec = pl.BlockSpec((1, D), lambda i: (0, 0))
    acc = pl.BlockSpec((8, D), lambda i: (0, 0))
    acc_shape = jax.ShapeDtypeStruct((8, D), F32)
    if want_dx:
        return pl.pallas_call(
            body, name=name, grid=(N // tm,), in_specs=[row, drow, vec, vec, row],
            out_specs=[row, acc], out_shape=[jax.ShapeDtypeStruct((N, D), F32), acc_shape],
            compiler_params=_cparams("arbitrary"))(x, dh, g, sc, resid)
    sums = pl.pallas_call(
        body, name=name, grid=(N // tm,), in_specs=[row, drow, vec, vec],
        out_specs=acc, out_shape=acc_shape, compiler_params=_cparams("arbitrary"))(x, dh, g, sc)
    return None, sums


def _resnorm_bwd(x1, dh, g, sc, dy, mix, gt, name):
    N, D = x1.shape
    tm = _pick(N, 256, 8)

    def body(x_ref, dh_ref, g_ref, sc_ref, dy_ref, mix_ref, gt_ref, dx_ref, dm_ref, acc_ref):
        i = pl.program_id(0)

        @pl.when(i == 0)
        def _():
            acc_ref[...] = jnp.zeros_like(acc_ref)

        xv, dhv, gv = x_ref[...], dh_ref[...], g_ref[...]
        r = lax.rsqrt(jnp.mean(xv * xv, axis=-1, keepdims=True) + EPS)
        xh = xv * r
        dn = dhv * (1.0 + sc_ref[...])
        dxh = dn * gv
        dx = dy_ref[...] + r * (dxh - xh * jnp.mean(dxh * xh, axis=-1, keepdims=True))
        dx_ref[...] = dx
        dm_ref[...] = (dx * gt_ref[...]).astype(BF16)
        acc_ref[0:1, :] += jnp.sum(dhv, axis=0, keepdims=True)
        acc_ref[1:2, :] += jnp.sum(dhv * xh * gv, axis=0, keepdims=True)
        acc_ref[2:3, :] += jnp.sum(dn * xh, axis=0, keepdims=True)
        acc_ref[3:4, :] += jnp.sum(dx * mix_ref[...], axis=0, keepdims=True)

    row = pl.BlockSpec((tm, D), lambda i: (i, 0))
    vec = pl.BlockSpec((1, D), lambda i: (0, 0))
    return pl.pallas_call(
        body, name=name, grid=(N // tm,), in_specs=[row, row, vec, vec, row, row, vec],
        out_specs=[row, row, pl.BlockSpec((8, D), lambda i: (0, 0))],
        out_shape=[jax.ShapeDtypeStruct((N, D), F32), jax.ShapeDtypeStruct((N, D), BF16),
                   jax.ShapeDtypeStruct((8, D), F32)],
        compiler_params=_cparams("arbitrary"))(x1, dh, g, sc, dy, mix, gt)


def _qknorm_fwd(z, cblk, nrows, roff, w, cos2, sin2, nh, name):
    W = nh * HEAD_DIM
    tm = _pick(math.gcd(nrows, roff), 256, 8)
    ro = roff // tm
    assert roff % tm == 0

    def body(z_ref, w_ref, c_ref, s_ref, o_ref):
        c, s, wv = c_ref[...], s_ref[...], w_ref[...]
        for h in range(nh):
            x = z_ref[:, h * HEAD_DIM:(h + 1) * HEAD_DIM]
            r = lax.rsqrt(jnp.mean(x * x, axis=-1, keepdims=True) + EPS)
            y = x * r * wv
            o_ref[:, h * HEAD_DIM:(h + 1) * HEAD_DIM] = (y * c + pltpu.roll(y, HEAD_DIM // 2, 1) * s).astype(BF16)

    return pl.pallas_call(
        body, name=name, grid=(nrows // tm,),
        in_specs=[pl.BlockSpec((tm, W), lambda i: (i + ro, cblk)), pl.BlockSpec((1, HEAD_DIM), lambda i: (0, 0)),
                  pl.BlockSpec((tm, HEAD_DIM), lambda i: (i + ro, 0)), pl.BlockSpec((tm, HEAD_DIM), lambda i: (i + ro, 0))],
        out_specs=pl.BlockSpec((tm, W), lambda i: (i, 0)),
        out_shape=jax.ShapeDtypeStruct((nrows, W), BF16),
        compiler_params=_cparams("parallel"),
    )(z, w, cos2, sin2)


def _qknorm_bwd(z, cblk, nrows, roff, w, cos2, sin2, dy, nh, name):
    W = nh * HEAD_DIM
    tm = _pick(math.gcd(nrows, roff), 256, 8)
    ro = roff // tm

    def body(z_ref, w_ref, c_ref, s_ref, dy_ref, dz_ref, acc_ref):
        i = pl.program_id(0)

        @pl.when(i == 0)
        def _():
            acc_ref[...] = jnp.zeros_like(acc_ref)

        c, s, wv = c_ref[...], s_ref[...], w_ref[...]
        dw = jnp.zeros((1, HEAD_DIM), F32)
        for h in range(nh):
            sl = slice(h * HEAD_DIM, (h + 1) * HEAD_DIM)
            x = z_ref[:, sl]
            d = dy_ref[:, sl]
            dyn = d * c + pltpu.roll(d * s, HEAD_DIM // 2, 1)
            r = lax.rsqrt(jnp.mean(x * x, axis=-1, keepdims=True) + EPS)
            xh = x * r
            dw = dw + jnp.sum(dyn * xh, axis=0, keepdims=True)
            dxh = dyn * wv
            dz_ref[:, sl] = (r * (dxh - xh * jnp.mean(dxh * xh, axis=-1, keepdims=True))).astype(BF16)
        acc_ref[0:1, :] += dw

    return pl.pallas_call(
        body, name=name, grid=(nrows // tm,),
        in_specs=[pl.BlockSpec((tm, W), lambda i: (i + ro, cblk)), pl.BlockSpec((1, HEAD_DIM), lambda i: (0, 0)),
                  pl.BlockSpec((tm, HEAD_DIM), lambda i: (i + ro, 0)), pl.BlockSpec((tm, HEAD_DIM), lambda i: (i + ro, 0)),
                  pl.BlockSpec((tm, W), lambda i: (i, 0))],
        out_specs=[pl.BlockSpec((tm, W), lambda i: (i, 0)), pl.BlockSpec((8, HEAD_DIM), lambda i: (0, 0))],
        out_shape=[jax.ShapeDtypeStruct((nrows, W), BF16), jax.ShapeDtypeStruct((8, HEAD_DIM), F32)],
        compiler_params=_cparams("arbitrary"),
    )(z, w, cos2, sin2, dy)


def _cast_seg(z, cblk, width, name):
    R = z.shape[0]
    tm = _pick(R, 512, 8)

    def body(z_ref, o_ref):
        o_ref[...] = z_ref[...].astype(BF16)

    return pl.pallas_call(
        body, name=name, grid=(R // tm,),
        in_specs=[pl.BlockSpec((tm, width), lambda i: (i, cblk))],
        out_specs=pl.BlockSpec((tm, width), lambda i: (i, 0)),
        out_shape=jax.ShapeDtypeStruct((R, width), BF16), compiler_params=_cparams("parallel"))(z)


NEG_BIG = -1e30


def _attn_specs(T, n_ctx):
    nb = T // WINDOW
    lb = n_ctx // WINDOW
    blk = lambda f: pl.BlockSpec((WINDOW, HEAD_DIM), f)
    win = [blk(lambda h, i: (lb + jnp.maximum(i - 1, 0), h)), blk(lambda h, i: (lb + i, h)),
           blk(lambda h, i: (lb + jnp.minimum(i + 1, nb - 1), h))]
    ctx = pl.BlockSpec((n_ctx, HEAD_DIM), lambda h, i: (0, h))
    qspec = pl.BlockSpec((WINDOW, Q_PER_KV * HEAD_DIM), lambda h, i: (i, h))
    sink = pl.BlockSpec((N_Q_HEADS, HEAD_DIM), lambda h, i: (0, 0))
    return nb, qspec, win, ctx, sink


def _attn_probs(q, kw, kctx, snk, valid):
    scale = HEAD_DIM ** -0.5
    s_lat = jnp.where(valid, _dot(q, kw, NT) * scale, NEG_BIG)
    s_ctx = _dot(q, kctx, NT) * scale
    m = jnp.maximum(jnp.maximum(jnp.max(s_lat, axis=-1, keepdims=True), jnp.max(s_ctx, axis=-1, keepdims=True)), snk)
    p_lat = jnp.exp(s_lat - m)
    p_ctx = jnp.exp(s_ctx - m)
    p_snk = jnp.exp(snk - m)
    den = p_snk + jnp.sum(p_lat, axis=-1, keepdims=True) + jnp.sum(p_ctx, axis=-1, keepdims=True)
    return p_lat, p_ctx, p_snk, den


def _attn_valid(i, T, heads):
    rows = heads * WINDOW
    qpos = i * WINDOW + (lax.broadcasted_iota(jnp.int32, (rows, 3 * WINDOW), 0) & (WINDOW - 1))
    kpos = (i - 1) * WINDOW + lax.broadcasted_iota(jnp.int32, (rows, 3 * WINDOW), 1)
    return (jnp.abs(qpos - kpos) <= WINDOW) & (kpos >= 0) & (kpos < T)


def _stack_heads(ref):
    return jnp.concatenate([ref[:, g * HEAD_DIM:(g + 1) * HEAD_DIM] for g in range(Q_PER_KV)], axis=0)


def _stack_sinks(sink_ref, h):
    return jnp.concatenate([jnp.broadcast_to(sink_ref[pl.ds(h * Q_PER_KV + g, 1), :][:, 0:1], (WINDOW, 1))
                            for g in range(Q_PER_KV)], axis=0)


def _attn_fwd(qn, kn, vb, sink_rows, n_ctx, name, ride=None):
    T = qn.shape[0]
    nb, qspec, win, ctx, sink = _attn_specs(T, n_ctx)

    def body(q_ref, kp, kc, kx, vp, vc, vx, kctx_ref, vctx_ref, sink_ref, o_ref):
        h, i = pl.program_id(0), pl.program_id(1)
        kw = jnp.concatenate([kp[...], kc[...], kx[...]], axis=0)
        vw = jnp.concatenate([vp[...], vc[...], vx[...]], axis=0)
        kctx, vctx = kctx_ref[...], vctx_ref[...]
        valid = _attn_valid(i, T, 1)
        for g in range(Q_PER_KV):
            sl = slice(g * HEAD_DIM, (g + 1) * HEAD_DIM)
            snk = sink_ref[pl.ds(h * Q_PER_KV + g, 1), :][:, 0:1]
            p_lat, p_ctx, _, den = _attn_probs(q_ref[:, sl], kw, kctx, snk, valid)
            o = (_dot(p_lat.astype(BF16), vw, NN) + _dot(p_ctx.astype(BF16), vctx, NN)) / den
            o_ref[:, sl] = o.astype(BF16)

    return _pcall(
        body, name=name, grid=(N_KV_HEADS, nb),
        in_specs=[qspec] + win + win + [ctx, ctx, sink],
        out_specs=qspec, out_shape=jax.ShapeDtypeStruct(qn.shape, BF16),
        sem=("parallel", "parallel"), args=(qn, kn, kn, kn, vb, vb, vb, kn, vb, sink_rows), ride=ride)


def _attn_bwd(qn, kn, vb, sink_rows, do, n_ctx, name, ride=None):
    T = qn.shape[0]
    nb, qspec, win, ctx, sink = _attn_specs(T, n_ctx)
    scale = HEAD_DIM ** -0.5
    TP = T + 2 * WINDOW

    def body(q_ref, kp, kc, kx, vp, vc, vx, kctx_ref, vctx_ref, sink_ref, do_ref,
             dq_ref, dkw_ref, dvw_ref, dkc_ref, dvc_ref, dsn_ref):
        h, i = pl.program_id(0), pl.program_id(1)

        @pl.when(i == 0)
        def _():
            dkw_ref[...] = jnp.zeros_like(dkw_ref)
            dvw_ref[...] = jnp.zeros_like(dvw_ref)
            dkc_ref[...] = jnp.zeros_like(dkc_ref)
            dvc_ref[...] = jnp.zeros_like(dvc_ref)
            dsn_ref[...] = jnp.zeros_like(dsn_ref)

        kw = jnp.concatenate([kp[...], kc[...], kx[...]], axis=0)
        vw = jnp.concatenate([vp[...], vc[...], vx[...]], axis=0)
        kctx, vctx = kctx_ref[...], vctx_ref[...]
        lane = lax.broadcasted_iota(jnp.int32, (8, HEAD_DIM), 1)
        q, d_o = _stack_heads(q_ref), _stack_heads(do_ref)
        p_lat, p_ctx, p_snk, den = _attn_probs(q, kw, kctx, _stack_sinks(sink_ref, h), _attn_valid(i, T, Q_PER_KV))
        inv = 1.0 / den
        p_lat, p_ctx, p_snk = p_lat * inv, p_ctx * inv, p_snk * inv
        dp_lat = _dot(d_o, vw, NT)
        dp_ctx = _dot(d_o, vctx, NT)
        dr = jnp.sum(p_lat * dp_lat, axis=-1, keepdims=True) + jnp.sum(p_ctx * dp_ctx, axis=-1, keepdims=True)
        ds_lat = (p_lat * (dp_lat - dr) * scale).astype(BF16)
        ds_ctx = (p_ctx * (dp_ctx - dr) * scale).astype(BF16)
        dq = _dot(ds_lat, kw, NN) + _dot(ds_ctx, kctx, NN)
        snk_terms = p_snk * dr
        dsn = jnp.zeros((8, HEAD_DIM), F32)
        for g in range(Q_PER_KV):
            dq_ref[:, g * HEAD_DIM:(g + 1) * HEAD_DIM] = dq[g * WINDOW:(g + 1) * WINDOW]
            dsn = dsn + jnp.where(lane == g, -jnp.sum(snk_terms[g * WINDOW:(g + 1) * WINDOW], axis=0, keepdims=True), 0.0)
        rows = pl.ds(pl.multiple_of(i * WINDOW, WINDOW), 3 * WINDOW)
        dkw_ref[rows, :] += _dot(ds_lat, q, TN)
        dvw_ref[rows, :] += _dot(p_lat.astype(BF16), d_o, TN)
        dkc_ref[...] += _dot(ds_ctx, q, TN)
        dvc_ref[...] += _dot(p_ctx.astype(BF16), d_o, TN)
        dsn_ref[0] += dsn

    wacc = pl.BlockSpec((TP, HEAD_DIM), lambda h, i: (0, h))
    return _pcall(
        body, name=name, grid=(N_KV_HEADS, nb),
        in_specs=[qspec] + win + win + [ctx, ctx, sink, qspec],
        out_specs=[qspec, wacc, wacc, ctx, ctx, pl.BlockSpec((1, 8, HEAD_DIM), lambda h, i: (h, 0, 0))],
        out_shape=[jax.ShapeDtypeStruct(qn.shape, F32),
                   jax.ShapeDtypeStruct((TP, N_KV_HEADS * HEAD_DIM), F32),
                   jax.ShapeDtypeStruct((TP, N_KV_HEADS * HEAD_DIM), F32),
                   jax.ShapeDtypeStruct((n_ctx, N_KV_HEADS * HEAD_DIM), F32),
                   jax.ShapeDtypeStruct((n_ctx, N_KV_HEADS * HEAD_DIM), F32),
                   jax.ShapeDtypeStruct((N_KV_HEADS, 8, HEAD_DIM), F32)],
        sem=("arbitrary", "arbitrary"), args=(qn, kn, kn, kn, vb, vb, vb, kn, vb, sink_rows, do), ride=ride)


def _gla_masks(dirv):
    C = GLA_CHUNK
    r = lax.broadcasted_iota(jnp.int32, (C, C), 0)
    c = lax.broadcasted_iota(jnp.int32, (C, C), 1)
    tt = jnp.where(dirv == 0, r, C - 1 - r)
    ss = jnp.where(dirv == 0, c, C - 1 - c)
    le = (ss <= tt).astype(jnp.int32)
    sums = [le == 1, le == 0]
    blocks = [ss == tt]
    for m in GLA_LEVELS:
        sh = m.bit_length() - 1
        same = (tt >> (sh + 1)) == (ss >> (sh + 1))
        ut = (tt >> sh) & 1
        us = (ss >> sh) & 1
        sums.append(same & (ut == us) & (ut == le))
        blocks.append(same & (ut == 1) & (us == 0))
    mall = jnp.concatenate([jnp.where(s, 1.0, 0.0) for s in sums], axis=0).astype(BF16)
    return mall, blocks


def _split3(x):
    hi = x.astype(BF16)
    r1 = x - hi.astype(F32)
    mid = r1.astype(BF16)
    lo = (r1 - mid.astype(F32)).astype(BF16)
    return hi, mid, lo


def _dot3(m_bf16, x, dims):
    hi, mid, lo = _split3(x)
    return _dot(m_bf16, hi, dims) + _dot(m_bf16, mid, dims) + _dot(m_bf16, lo, dims)


def _gla_chunk_of(dirv, j, lc, nc):
    return jnp.where(dirv == 0, j, jnp.where(j < lc, lc - 1 - j, nc + lc - 1 - j))


def _gla_gate(lr_ref, wg_ref, bg_ref):
    pre = _dot(lr_ref[...].astype(BF16), wg_ref[0].astype(BF16), NN) + bg_ref[0]
    g = (jnp.minimum(pre, 0.0) - jnp.log(1.0 + jnp.exp(-jnp.abs(pre)))) * (1.0 / GLA_GATE_NORM)
    return pre, g


def _gla_fwd(z, qblk, kblk, vblk, lrblk, wg, bg, DV, n_ctx, name):
    R = z.shape[0]
    C = GLA_CHUNK
    DK = wg.shape[2] // GLA_HEADS
    nc, lc = R // C, n_ctx // C
    qscale = DK ** -0.5

    GK, GV = GLA_HEADS * DK, GLA_HEADS * DV

    def body(q_ref, k_ref, v_ref, lr_ref, wg_ref, bg_ref, o_ref, sp_ref, st_ref):
        dirv, j = pl.program_id(0), pl.program_id(1)

        @pl.when(j == 0)
        def _():
            st_ref[...] = jnp.zeros_like(st_ref)

        mall, blocks = _gla_masks(dirv)
        _, g_all = _gla_gate(lr_ref, wg_ref, bg_ref)
        E_all = _dot3(mall, g_all, NN)
        for h in range(GLA_HEADS):
            ks, vs = slice(h * DK, (h + 1) * DK), slice(h * DV, (h + 1) * DV)
            q, k, v = q_ref[:, ks] * qscale, k_ref[:, ks], v_ref[:, vs].astype(BF16)
            g, E = g_all[:, ks], E_all[:, ks]
            st = st_ref[h]
            sp_ref[0, h, 0] = st
            A = jnp.where(blocks[0], _dot(q.astype(BF16), k.astype(BF16), NT), 0.0)
            for l in range(len(GLA_LEVELS)):
                e = jnp.exp(E[(2 + l) * C:(3 + l) * C])
                A = A + jnp.where(blocks[l + 1], _dot((q * e).astype(BF16), (k * e).astype(BF16), NT), 0.0)
            o_ref[0, :, vs] = (_dot((q * jnp.exp(E[0:C])).astype(BF16), st.astype(BF16), NT)
                               + _dot(A.astype(BF16), v, NN))
            decay = jnp.exp(jnp.sum(g, axis=0, keepdims=True))
            st_ref[h] = decay * st + _dot(v, (k * jnp.exp(E[C:2 * C])).astype(BF16), TN)

    chunk = functools.partial(_gla_chunk_of, lc=lc, nc=nc)
    return pl.pallas_call(
        body, name=name, grid=(2, nc),
        in_specs=[pl.BlockSpec((C, GK), lambda d, j: (chunk(d, j), qblk)),
                  pl.BlockSpec((C, GK), lambda d, j: (chunk(d, j), kblk)),
                  pl.BlockSpec((C, GV), lambda d, j: (chunk(d, j), vblk)),
                  pl.BlockSpec((C, LANES), lambda d, j: (chunk(d, j), lrblk)),
                  pl.BlockSpec((1, LANES, GK), lambda d, j: (d, 0, 0)),
                  pl.BlockSpec((1, 1, GK), lambda d, j: (d, 0, 0))],
        out_specs=[pl.BlockSpec((1, C, GV), lambda d, j: (d, chunk(d, j), 0)),
                   pl.BlockSpec((1, GLA_HEADS, 1, DV, DK), lambda d, j: (d, 0, j, 0, 0))],
        out_shape=[jax.ShapeDtypeStruct((2, R, GV), F32),
                   jax.ShapeDtypeStruct((2, GLA_HEADS, nc, DV, DK), F32)],
        scratch_shapes=[pltpu.VMEM((GLA_HEADS, DV, DK), F32)],
        compiler_params=_cparams("parallel", "arbitrary"),
    )(z, z, z, z, wg, bg)


def _gla_bwd(z, qblk, kblk, vblk, lrblk, wg, bg, sprev, do, n_ctx, name, ride=None):
    R = z.shape[0]
    C = GLA_CHUNK
    DK, DV = wg.shape[2] // GLA_HEADS, do.shape[1] // GLA_HEADS
    nc, lc = R // C, n_ctx // C
    qscale = DK ** -0.5
    nl = len(GLA_LEVELS)

    GK, GV = GLA_HEADS * DK, GLA_HEADS * DV

    def body(q_ref, k_ref, v_ref, lr_ref, wg_ref, bg_ref, sp_ref, do_ref,
             dq_ref, dk_ref, dv_ref, dpre_ref, dbg_ref, dst_ref):
        dirv, jr = pl.program_id(0), pl.program_id(1)

        @pl.when(jr == 0)
        def _():
            dst_ref[...] = jnp.zeros_like(dst_ref)
            dbg_ref[...] = jnp.zeros_like(dbg_ref)

        mall, blocks = _gla_masks(dirv)
        pre_all, g_all = _gla_gate(lr_ref, wg_ref, bg_ref)
        E_all = _dot3(mall, g_all, NN)
        for h in range(GLA_HEADS):
            ks, vs = slice(h * DK, (h + 1) * DK), slice(h * DV, (h + 1) * DV)
            q, k, v = q_ref[:, ks] * qscale, k_ref[:, ks], v_ref[:, vs].astype(BF16)
            pre, g, E = pre_all[:, ks], g_all[:, ks], E_all[:, ks]
            eb, er = jnp.exp(E[0:C]), jnp.exp(E[C:2 * C])
            decay = jnp.exp(jnp.sum(g, axis=0, keepdims=True))
            st = sp_ref[0, h, 0]
            dst = dst_ref[h]
            d_o = do_ref[:, vs]
            qe, kd = q * eb, k * er
            qb, kb = q.astype(BF16), k.astype(BF16)
            A = jnp.where(blocks[0], _dot(qb, kb, NT), 0.0)
            for l in range(nl):
                e = jnp.exp(E[(2 + l) * C:(3 + l) * C])
                A = A + jnp.where(blocks[l + 1], _dot((q * e).astype(BF16), (k * e).astype(BF16), NT), 0.0)
            dA = _dot(d_o, v, NT)
            dv_ref[0, :, vs] = _dot(A.astype(BF16), d_o, TN) + _dot(kd.astype(BF16), dst.astype(BF16), NT)
            dqe = _dot(d_o, st.astype(BF16), NN)
            dkd = _dot(v, dst.astype(BF16), NN)
            G = jnp.where(blocks[0], dA, 0.0).astype(BF16)
            dq = dqe * eb + _dot(G, kb, NN)
            dk = dkd * er + _dot(G, qb, TN)
            dE = [dqe * qe, dkd * kd]
            for l in range(nl):
                e = jnp.exp(E[(2 + l) * C:(3 + l) * C])
                ql, kl = q * e, k * e
                G = jnp.where(blocks[l + 1], dA, 0.0).astype(BF16)
                dql = _dot(G, kl.astype(BF16), NN)
                dkl = _dot(G, ql.astype(BF16), TN)
                dq = dq + dql * e
                dk = dk + dkl * e
                dE.append(dql * ql + dkl * kl)
            dlast = jnp.sum(dst * st, axis=0, keepdims=True) * decay
            dg = _dot3(mall, jnp.concatenate(dE, axis=0), TN) + dlast
            dpre = dg * (1.0 / GLA_GATE_NORM) / (1.0 + jnp.exp(pre))
            dq_ref[0, :, ks] = dq * qscale
            dk_ref[0, :, ks] = dk
            dpre_ref[:, ks] = dpre.astype(BF16)
            dbg_ref[0, :, ks] += jnp.sum(dpre, axis=0, keepdims=True)
            dst_ref[h] = decay * dst + _dot(d_o, qe.astype(BF16), TN)

    def chunk(d, jr):
        return _gla_chunk_of(d, nc - 1 - jr, lc, nc)

    return _pcall(
        body, name=name, grid=(2, nc),
        in_specs=[pl.BlockSpec((C, GK), lambda d, j: (chunk(d, j), qblk)),
                  pl.BlockSpec((C, GK), lambda d, j: (chunk(d, j), kblk)),
                  pl.BlockSpec((C, GV), lambda d, j: (chunk(d, j), vblk)),
                  pl.BlockSpec((C, LANES), lambda d, j: (chunk(d, j), lrblk)),
                  pl.BlockSpec((1, LANES, GK), lambda d, j: (d, 0, 0)),
                  pl.BlockSpec((1, 1, GK), lambda d, j: (d, 0, 0)),
                  pl.BlockSpec((1, GLA_HEADS, 1, DV, DK), lambda d, j: (d, 0, nc - 1 - j, 0, 0)),
                  pl.BlockSpec((C, GV), lambda d, j: (chunk(d, j), 0))],
        out_specs=[pl.BlockSpec((1, C, GK), lambda d, j: (d, chunk(d, j), 0)),
                   pl.BlockSpec((1, C, GK), lambda d, j: (d, chunk(d, j), 0)),
                   pl.BlockSpec((1, C, GV), lambda d, j: (d, chunk(d, j), 0)),
                   pl.BlockSpec((C, GK), lambda d, j: (chunk(d, j), d)),
                   pl.BlockSpec((1, 1, GK), lambda d, j: (d, 0, 0))],
        out_shape=[jax.ShapeDtypeStruct((2, R, GK), F32),
                   jax.ShapeDtypeStruct((2, R, GK), F32),
                   jax.ShapeDtypeStruct((2, R, GV), F32),
                   jax.ShapeDtypeStruct((R, 2 * GK), BF16),
                   jax.ShapeDtypeStruct((2, 1, GK), F32)],
        scratch_shapes=[pltpu.VMEM((GLA_HEADS, DV, DK), F32)],
        sem=("arbitrary", "arbitrary"), args=(z, z, z, z, wg, bg, sprev, do), ride=ride)


def _glanorm_fwd(o, z, rbblk, gn, n_ctx, name):
    _, R, GV = o.shape
    T = R - n_ctx
    DV = GV // GLA_HEADS
    tm = _pick(n_ctx, 256, 8)
    ro = n_ctx // tm

    def body(o0_ref, o1_ref, rb_ref, gn_ref, p_ref):
        gnv = gn_ref[...]
        for h in range(GLA_HEADS):
            sl = slice(h * DV, (h + 1) * DV)
            og = o0_ref[0, :, sl] + o1_ref[0, :, sl]
            r = lax.rsqrt(jnp.mean(og * og, axis=-1, keepdims=True) + EPS)
            p_ref[:, sl] = (og * r * gnv * _silu(rb_ref[:, sl])).astype(BF16)

    return pl.pallas_call(
        body, name=name, grid=(T // tm,),
        in_specs=[pl.BlockSpec((1, tm, GV), lambda i: (0, i + ro, 0)), pl.BlockSpec((1, tm, GV), lambda i: (1, i + ro, 0)),
                  pl.BlockSpec((tm, GV), lambda i: (i + ro, rbblk)), pl.BlockSpec((1, DV), lambda i: (0, 0))],
        out_specs=pl.BlockSpec((tm, GV), lambda i: (i, 0)),
        out_shape=jax.ShapeDtypeStruct((T, GV), BF16), compiler_params=_cparams("parallel"))(o, o, z, gn)


def _glanorm_bwd(o, z, rbblk, gn, dp, n_ctx, name):
    _, R, GV = o.shape
    T = R - n_ctx
    DV = GV // GLA_HEADS
    tm = _pick(n_ctx, 256, 8)
    ro = n_ctx // tm

    def body(o0_ref, o1_ref, rb_ref, gn_ref, dp_ref, do_ref, drb_ref, acc_ref):
        i = pl.program_id(0)

        @pl.when(i == 0)
        def _():
            acc_ref[...] = jnp.zeros_like(acc_ref)

        gnv = gn_ref[...]
        dgn = jnp.zeros((1, DV), F32)
        for h in range(GLA_HEADS):
            sl = slice(h * DV, (h + 1) * DV)
            og = o0_ref[0, :, sl] + o1_ref[0, :, sl]
            rb = rb_ref[:, sl]
            d = dp_ref[:, sl]
            r = lax.rsqrt(jnp.mean(og * og, axis=-1, keepdims=True) + EPS)
            xh = og * r
            drb_ref[:, sl] = (d * xh * gnv * _dsilu(rb)).astype(BF16)
            dn = d * _silu(rb)
            dgn = dgn + jnp.sum(dn * xh, axis=0, keepdims=True)
            dxh = dn * gnv
            do_ref[:, sl] = (r * (dxh - xh * jnp.mean(dxh * xh, axis=-1, keepdims=True))).astype(BF16)
        acc_ref[0:1, :] += dgn

    row = pl.BlockSpec((tm, GV), lambda i: (i, 0))
    return pl.pallas_call(
        body, name=name, grid=(T // tm,),
        in_specs=[pl.BlockSpec((1, tm, GV), lambda i: (0, i + ro, 0)), pl.BlockSpec((1, tm, GV), lambda i: (1, i + ro, 0)),
                  pl.BlockSpec((tm, GV), lambda i: (i + ro, rbblk)), pl.BlockSpec((1, DV), lambda i: (0, 0)), row],
        out_specs=[row, row, pl.BlockSpec((8, DV), lambda i: (0, 0))],
        out_shape=[jax.ShapeDtypeStruct((T, GV), BF16), jax.ShapeDtypeStruct((T, GV), BF16),
                   jax.ShapeDtypeStruct((8, DV), F32)],
        compiler_params=_cparams("arbitrary"))(o, o, z, gn, dp)


def _gate_fwd(z, gablk, gbblk, ya, yg, n_ctx, name):
    T, D = ya.shape
    tm = _pick(n_ctx, 256, 8)
    ro = n_ctx // tm

    def body(ga_ref, gb_ref, ya_ref, yg_ref, m_ref):
        m_ref[...] = (_sigmoid(ga_ref[...]) * ya_ref[...] + _sigmoid(gb_ref[...]) * yg_ref[...]).astype(BF16)

    row = pl.BlockSpec((tm, D), lambda i: (i, 0))
    return pl.pallas_call(
        body, name=name, grid=(T // tm,),
        in_specs=[pl.BlockSpec((tm, D), lambda i: (i + ro, gablk)), pl.BlockSpec((tm, D), lambda i: (i + ro, gbblk)), row, row],
        out_specs=row, out_shape=jax.ShapeDtypeStruct((T, D), BF16), compiler_params=_cparams("parallel"))(z, z, ya, yg)


def _gate_bwd(z, gablk, gbblk, ya, yg, dm, n_ctx, name):
    T, D = ya.shape
    tm = _pick(n_ctx, 256, 8)
    ro = n_ctx // tm

    def body(ga_ref, gb_ref, ya_ref, yg_ref, dm_ref, dya_ref, dyg_ref, dga_ref, dgb_ref):
        d = dm_ref[...]
        sa, sb = _sigmoid(ga_ref[...]), _sigmoid(gb_ref[...])
        dya_ref[...] = (d * sa).astype(BF16)
        dyg_ref[...] = (d * sb).astype(BF16)
        dga_ref[...] = (d * ya_ref[...] * sa * (1.0 - sa)).astype(BF16)
        dgb_ref[...] = (d * yg_ref[...] * sb * (1.0 - sb)).astype(BF16)

    row = pl.BlockSpec((tm, D), lambda i: (i, 0))
    sh = jax.ShapeDtypeStruct((T, D), BF16)
    return pl.pallas_call(
        body, name=name, grid=(T // tm,),
        in_specs=[pl.BlockSpec((tm, D), lambda i: (i + ro, gablk)), pl.BlockSpec((tm, D), lambda i: (i + ro, gbblk)), row, row, row],
        out_specs=[row] * 4, out_shape=[sh] * 4, compiler_params=_cparams("parallel"))(z, z, ya, yg, dm)


def _resnorm_fwd(x, mix, gt, g, sc, sh, name):
    T, D = x.shape
    tm = _pick(T, 256, 8)

    def body(x_ref, mix_ref, gt_ref, g_ref, sc_ref, sh_ref, x1_ref, h_ref):
        x1 = x_ref[...] + gt_ref[...] * mix_ref[...]
        x1_ref[...] = x1
        r = lax.rsqrt(jnp.mean(x1 * x1, axis=-1, keepdims=True) + EPS)
        h_ref[...] = (x1 * r * g_ref[...] * (1.0 + sc_ref[...]) + sh_ref[...]).astype(BF16)

    row = pl.BlockSpec((tm, D), lambda i: (i, 0))
    vec = pl.BlockSpec((1, D), lambda i: (0, 0))
    return pl.pallas_call(
        body, name=name, grid=(T // tm,), in_specs=[row, row, vec, vec, vec, vec], out_specs=[row, row],
        out_shape=[jax.ShapeDtypeStruct((T, D), F32), jax.ShapeDtypeStruct((T, D), BF16)],
        compiler_params=_cparams("parallel"))(x, mix, gt, g, sc, sh)


def _loss_head(d, x1, gt, target, name):
    T, D = d.shape
    tm = _pick(T, 256, 8)

    def body(d_ref, x1_ref, gt_ref, t_ref, dy_ref, dd_ref, acc_ref):
        i = pl.program_id(0)

        @pl.when(i == 0)
        def _():
            acc_ref[...] = jnp.zeros_like(acc_ref)

        dv, gtv = d_ref[...], gt_ref[...]
        e = x1_ref[...] + gtv * dv - t_ref[...]
        dy = e * (1.0 / D)
        dy_ref[...] = dy
        dd_ref[...] = (dy * gtv).astype(BF16)
        acc_ref[0:1, :] += jnp.sum(e * e, axis=0, keepdims=True)
        acc_ref[1:2, :] += jnp.sum(dy * dv, axis=0, keepdims=True)

    row = pl.BlockSpec((tm, D), lambda i: (i, 0))
    return pl.pallas_call(
        body, name=name, grid=(T // tm,), in_specs=[row, row, pl.BlockSpec((1, D), lambda i: (0, 0)), row],
        out_specs=[row, row, pl.BlockSpec((8, D), lambda i: (0, 0))],
        out_shape=[jax.ShapeDtypeStruct((T, D), F32), jax.ShapeDtypeStruct((T, D), BF16),
                   jax.ShapeDtypeStruct((8, D), F32)],
        compiler_params=_cparams("arbitrary"))(d, x1, gt, target)


def _halo_specs(T, tm, tw, col_of, order):
    n8 = tm // 8
    if order == "ij":
        mid = lambda i, j: (i, col_of(j))
        prev = lambda i, j: (jnp.maximum(i * n8 - 1, 0), col_of(j))
        nxt = lambda i, j: (jnp.minimum((i + 1) * n8, T // 8 - 1), col_of(j))
    else:
        mid = lambda j, i: (i, col_of(j))
        prev = lambda j, i: (jnp.maximum(i * n8 - 1, 0), col_of(j))
        nxt = lambda j, i: (jnp.minimum((i + 1) * n8, T // 8 - 1), col_of(j))
    return [pl.BlockSpec((tm, tw), mid), pl.BlockSpec((8, tw), prev), pl.BlockSpec((8, tw), nxt)]


def _shifted(u_ref, up_ref, un_ref, i, nt):
    u = u_ref[...]
    tm = u.shape[0]
    row = lax.broadcasted_iota(jnp.int32, u.shape, 0)
    hp = jnp.where(i > 0, up_ref[7:8, :], 0.0)
    hn = jnp.where(i < nt - 1, un_ref[0:1, :], 0.0)
    u_prev = jnp.where(row == 0, hp, pltpu.roll(u, 1, 0))
    u_next = jnp.where(row == tm - 1, hn, pltpu.roll(u, tm - 1, 0))
    return u_prev, u, u_next


def _conv_fwd(u, cw, cb, name):
    T, F2 = u.shape
    F = F2 // 2
    tm, tw = _pick(T, 256, 8), _pick(F, 512)
    nt, nw = T // tm, F // tw

    def body(ua, uap, uan, ug, ugp, ugn, cwa, cwg, cba, cbg, f_ref):
        i = pl.program_id(0)

        def conv(u_ref, up_ref, un_ref, w_ref, b_ref):
            p, m, n = _shifted(u_ref, up_ref, un_ref, i, nt)
            return p * w_ref[0:1, :] + m * w_ref[1:2, :] + n * w_ref[2:3, :] + b_ref[...]

        a = conv(ua, uap, uan, cwa, cba)
        g = conv(ug, ugp, ugn, cwg, cbg)
        f_ref[...] = (_silu(a) * g).astype(BF16)

    wspec = lambda off: pl.BlockSpec((3, tw), lambda i, j: (0, j + off))
    bspec = lambda off: pl.BlockSpec((1, tw), lambda i, j: (0, j + off))
    return pl.pallas_call(
        body, name=name, grid=(nt, nw),
        in_specs=_halo_specs(T, tm, tw, lambda j: j, "ij") + _halo_specs(T, tm, tw, lambda j: j + nw, "ij")
        + [wspec(0), wspec(nw), bspec(0), bspec(nw)],
        out_specs=pl.BlockSpec((tm, tw), lambda i, j: (i, j)),
        out_shape=jax.ShapeDtypeStruct((T, F), BF16), compiler_params=_cparams("parallel", "parallel"),
    )(u, u, u, u, u, u, cw, cw, cb, cb)


def _shift_rows(x, before, after):
    tm = x.shape[0]
    row = lax.broadcasted_iota(jnp.int32, x.shape, 0)
    return (jnp.where(row == 0, before, pltpu.roll(x, 1, 0)),
            jnp.where(row == tm - 1, after, pltpu.roll(x, tm - 1, 0)))


def _conv_bwd(u, df, cw, cb, name):
    T, F2 = u.shape
    F = F2 // 2
    tm, tw = _pick(T, 256, 8), _pick(F, 512)
    nt, nw = T // tm, F // tw

    def body(ua, uap, uan, ug, ugp, ugn, cwa, cwg, cba, cbg, df_ref, dfp, dfn, dua_ref, dug_ref, acca_ref, accg_ref):
        i = pl.program_id(1)

        @pl.when(i == 0)
        def _():
            acca_ref[...] = jnp.zeros_like(acca_ref)
            accg_ref[...] = jnp.zeros_like(accg_ref)

        first, last = i == 0, i == nt - 1
        wa, wg, ba, bg = cwa[...], cwg[...], cba[...], cbg[...]

        def conv(p, m, n, w, b):
            return p * w[0:1] + m * w[1:2] + n * w[2:3] + b

        def grads(a, g, d):
            return d * g * _dsilu(a), d * _silu(a)

        xa, xg, d = ua[...], ug[...], df_ref[...]
        sa = _shift_rows(xa, jnp.where(first, 0.0, uap[7:8, :]), jnp.where(last, 0.0, uan[0:1, :]))
        sg = _shift_rows(xg, jnp.where(first, 0.0, ugp[7:8, :]), jnp.where(last, 0.0, ugn[0:1, :]))
        da, dg = grads(conv(sa[0], xa, sa[1], wa, ba), conv(sg[0], xg, sg[1], wg, bg), d)
        da_p, dg_p = grads(conv(uap[6:7, :], uap[7:8, :], xa[0:1], wa, ba),
                           conv(ugp[6:7, :], ugp[7:8, :], xg[0:1], wg, bg), dfp[7:8, :])
        da_n, dg_n = grads(conv(xa[tm - 1:tm], uan[0:1, :], uan[1:2, :], wa, ba),
                           conv(xg[tm - 1:tm], ugn[0:1, :], ugn[1:2, :], wg, bg), dfn[0:1, :])
        ta = _shift_rows(da, jnp.where(first, 0.0, da_p), jnp.where(last, 0.0, da_n))
        tg = _shift_rows(dg, jnp.where(first, 0.0, dg_p), jnp.where(last, 0.0, dg_n))
        dua_ref[...] = (ta[1] * wa[0:1] + da * wa[1:2] + ta[0] * wa[2:3]).astype(BF16)
        dug_ref[...] = (tg[1] * wg[0:1] + dg * wg[1:2] + tg[0] * wg[2:3]).astype(BF16)
        for t, (va, vg) in enumerate(((sa[0], sg[0]), (xa, xg), (sa[1], sg[1]))):
            acca_ref[t:t + 1, :] += jnp.sum(da * va, axis=0, keepdims=True)
            accg_ref[t:t + 1, :] += jnp.sum(dg * vg, axis=0, keepdims=True)
        acca_ref[3:4, :] += jnp.sum(da, axis=0, keepdims=True)
        accg_ref[3:4, :] += jnp.sum(dg, axis=0, keepdims=True)

    wspec = lambda off: pl.BlockSpec((3, tw), lambda j, i: (0, j + off))
    bspec = lambda off: pl.BlockSpec((1, tw), lambda j, i: (0, j + off))
    row = pl.BlockSpec((tm, tw), lambda j, i: (i, j))
    acc = pl.BlockSpec((8, tw), lambda j, i: (0, j))
    return pl.pallas_call(
        body, name=name, grid=(nw, nt),
        in_specs=_halo_specs(T, tm, tw, lambda j: j, "ji") + _halo_specs(T, tm, tw, lambda j: j + nw, "ji")
        + [wspec(0), wspec(nw), bspec(0), bspec(nw)] + _halo_specs(T, tm, tw, lambda j: j, "ji"),
        out_specs=[row, row, acc, acc],
        out_shape=[jax.ShapeDtypeStruct((T, F), BF16), jax.ShapeDtypeStruct((T, F), BF16),
                   jax.ShapeDtypeStruct((8, F), F32), jax.ShapeDtypeStruct((8, F), F32)],
        compiler_params=_cparams("parallel", "arbitrary"),
    )(u, u, u, u, u, u, cw, cw, cb, cb, df, df, df)


def _assemble_dz(lay, Z, n_ctx, dqa, drb, dga, dgb, dka, dva, dvg, dqg, dkg, dlr, name):
    T = dqa.shape[0]
    R = T + n_ctx
    tm = _pick(n_ctx, 128, 8)
    cb = n_ctx // tm

    def body(dqa_ref, drb_ref, dga_ref, dgb_ref, dka_ref, dva_ref, dvg0, dvg1, dqg0, dqg1, dkg0, dkg1, dlr_ref, o_ref):
        lat = pl.program_id(0) >= cb

        def put(seg, val):
            o_ref[:, lay[seg]:lay[seg] + val.shape[1]] = val.astype(BF16)

        def lat_only(ref):
            v = ref[...]
            return jnp.where(lat, v, jnp.zeros_like(v))

        put("qa", lat_only(dqa_ref))
        put("rb", lat_only(drb_ref))
        put("ga", lat_only(dga_ref))
        put("gb", lat_only(dgb_ref))
        put("ka", dka_ref[...])
        put("va", dva_ref[...])
        put("vb", dvg0[0] + dvg1[0])
        put("qb", dqg0[0] + dqg1[0])
        put("kb", dkg0[0] + dkg1[0])
        put("lr", dlr_ref[...])

    lat_spec = lambda a: pl.BlockSpec((tm, a.shape[1]), lambda i: (jnp.maximum(i - cb, 0), 0))
    all_spec = lambda a: pl.BlockSpec((tm, a.shape[1]), lambda i: (i, 0))
    dir_specs = lambda a: [pl.BlockSpec((1, tm, a.shape[2]), lambda i: (0, i, 0)),
                           pl.BlockSpec((1, tm, a.shape[2]), lambda i: (1, i, 0))]
    return pl.pallas_call(
        body, name=name, grid=(R // tm,),
        in_specs=[lat_spec(dqa), lat_spec(drb), lat_spec(dga), lat_spec(dgb), all_spec(dka), all_spec(dva)]
        + dir_specs(dvg) + dir_specs(dqg) + dir_specs(dkg) + [all_spec(dlr)],
        out_specs=pl.BlockSpec((tm, Z), lambda i: (i, 0)),
        out_shape=jax.ShapeDtypeStruct((R, Z), BF16), compiler_params=_cparams("parallel"),
    )(dqa, drb, dga, dgb, dka, dva, dvg, dvg, dqg, dqg, dkg, dkg, dlr)


def _mod_fwd(ca, w, b, name):
    n, D = ca.shape
    N = w.shape[1]
    tn = _pick(N, 512)

    def body(c_ref, w_ref, b_ref, o_ref, s_ref):
        s = _silu(c_ref[...])
        s_ref[...] = s
        o_ref[...] = _dot(s.astype(BF16), w_ref[...].astype(BF16), NN) + b_ref[...]

    return pl.pallas_call(
        body, name=name, grid=(N // tn,),
        in_specs=[pl.BlockSpec((n, D), lambda j: (0, 0)), pl.BlockSpec((D, tn), lambda j: (0, j)),
                  pl.BlockSpec((1, tn), lambda j: (0, j))],
        out_specs=[pl.BlockSpec((n, tn), lambda j: (0, j)), pl.BlockSpec((n, D), lambda j: (0, 0))],
        out_shape=[jax.ShapeDtypeStruct((n, N), F32), jax.ShapeDtypeStruct((n, D), F32)],
        compiler_params=_cparams("arbitrary"))(ca, w, b)


def _silu_bwd(dsil, ca, name):
    def body(d_ref, c_ref, o_ref):
        o_ref[...] = d_ref[...] * _dsilu(c_ref[...])

    return pl.pallas_call(body, name=name, out_shape=jax.ShapeDtypeStruct(ca.shape, F32))(dsil, ca)


def _adam_math(w, g, m, v):
    c1 = 1.0 - ADAM_B1 ** ADAM_STEP
    c2 = 1.0 - ADAM_B2 ** ADAM_STEP
    mn = ADAM_B1 * m + (1.0 - ADAM_B1) * g
    vn = ADAM_B2 * v + (1.0 - ADAM_B2) * (g * g)
    return -ADAM_LR * ((mn / c1) / (jnp.sqrt(vn / c2) + ADAM_EPS) + ADAM_WD * w), mn, vn


def _adamw(w, g, m, v, name):
    Rw, Cw = w.shape
    tr = _pick(Rw, 128, 8)

    def body(w_ref, g_ref, m_ref, v_ref, d_ref, mo_ref, vo_ref):
        d_ref[...], mo_ref[...], vo_ref[...] = _adam_math(w_ref[...], g_ref[...], m_ref[...], v_ref[...])

    row = pl.BlockSpec((tr, Cw), lambda i: (i, 0))
    sh = jax.ShapeDtypeStruct((Rw, Cw), F32)
    return pl.pallas_call(body, name=name, grid=(Rw // tr,), in_specs=[row] * 4, out_specs=[row] * 3,
                          out_shape=[sh] * 3, compiler_params=_cparams("parallel"))(w, g, m, v)


HBM_SPEC = pl.BlockSpec(memory_space=pltpu.HBM)


def _exchange(inputs, out_shapes, stages, name):
    n_in, n_out = len(inputs), len(out_shapes)
    n = sum(len(s) for s in stages)

    def body(*refs):
        ins, outs = refs[:n_in], refs[n_in:n_in + n_out]
        send_sems, recv_sems = refs[n_in + n_out:]
        k = 0
        for stage in stages:
            copies = _stage_copies(stage, ins, outs, send_sems, recv_sems, k)
            for cp in copies:
                cp.start()
            for cp in copies:
                cp.wait()
            k += len(stage)

    return pl.pallas_call(
        body, name=name, in_specs=[HBM_SPEC] * n_in, out_specs=[HBM_SPEC] * n_out, out_shape=out_shapes,
        scratch_shapes=[pltpu.SemaphoreType.DMA((n,)), pltpu.SemaphoreType.DMA((n,))],
    )(*inputs)


def _stage_copies(stage, ins, outs, send_sems, recv_sems, k0=0):
    me = (lax.axis_index("x"), lax.axis_index("y"), lax.axis_index("c"))
    copies = []
    for k, ((skind, sidx), sfn, didx, dfn, flip) in enumerate(stage):
        src = (ins if skind == "in" else outs)[sidx].at[sfn(*me)]
        dst = outs[didx].at[dfn(*me)]
        if flip == (0, 0, 0):
            copies.append(pltpu.make_async_copy(src, dst, send_sems.at[k0 + k]))
        else:
            peer = tuple(1 - a if f else a for a, f in zip(me, flip))
            copies.append(pltpu.make_async_remote_copy(src, dst, send_sems.at[k0 + k], recv_sems.at[k0 + k],
                                                       device_id=peer, device_id_type=MESH))
    return copies


def _pcall(body, *, name, grid, in_specs, out_specs, out_shape, scratch_shapes=(), sem, args, ride=None):
    many = isinstance(out_shape, (list, tuple))
    out_specs, out_shape = (list(out_specs), list(out_shape)) if many else ([out_specs], [out_shape])
    if ride is None:
        res = pl.pallas_call(body, name=name, grid=grid, in_specs=list(in_specs), out_specs=out_specs,
                             out_shape=out_shape, scratch_shapes=list(scratch_shapes),
                             compiler_params=_cparams(*sem))(*args)
        return res if many else res[0]
    x_in, x_out, stage, aliases = ride
    n_in, n_out, n_scr, n_xin, n_xout = len(in_specs), len(out_specs), len(scratch_shapes), len(x_in), len(x_out)

    def wrapped(*refs):
        ins, xins = refs[:n_in], refs[n_in:n_in + n_xin]
        o0 = n_in + n_xin
        outs, xouts = refs[o0:o0 + n_out], refs[o0 + n_out:o0 + n_out + n_xout]
        s0 = o0 + n_out + n_xout
        scr, (send_sems, recv_sems) = refs[s0:s0 + n_scr], refs[s0 + n_scr:]
        first = functools.reduce(jnp.logical_and, [pl.program_id(d) == 0 for d in range(len(grid))])
        last = functools.reduce(jnp.logical_and, [pl.program_id(d) == grid[d] - 1 for d in range(len(grid))])

        @pl.when(first)
        def _():
            for cp in _stage_copies(stage, xins, xouts, send_sems, recv_sems):
                cp.start()

        body(*ins, *outs, *scr)

        @pl.when(last)
        def _():
            for cp in _stage_copies(stage, xins, xouts, send_sems, recv_sems):
                cp.wait()

    res = pl.pallas_call(
        wrapped, name=name, grid=grid, in_specs=list(in_specs) + [HBM_SPEC] * n_xin,
        out_specs=out_specs + [HBM_SPEC] * n_xout, out_shape=out_shape + list(x_out),
        scratch_shapes=list(scratch_shapes) + [pltpu.SemaphoreType.DMA((len(stage),)),
                                               pltpu.SemaphoreType.DMA((len(stage),))],
        input_output_aliases={n_in + a: n_out + b for a, b in aliases.items()},
        compiler_params=_cparams(*(["arbitrary"] * len(grid))))(*args, *x_in)
    main = res[:n_out]
    return (main if many else main[0]), list(res[n_out:])


FLIPS_ALL = [(0, 0, 1), (0, 1, 0), (0, 1, 1), (1, 0, 0), (1, 0, 1), (1, 1, 0), (1, 1, 1)]
FLIPS_CHIP = [(0, 1, 0), (1, 0, 0), (1, 1, 0)]


def _sum_slots(buf, name):
    n, r, w = buf.shape
    tr = _pick(r, 256, 8)

    def body(b_ref, o_ref):
        acc = b_ref[0]
        for s in range(1, n):
            acc = acc + b_ref[s]
        o_ref[...] = acc

    return pl.pallas_call(
        body, name=name, grid=(r // tr,), in_specs=[pl.BlockSpec((n, tr, w), lambda i: (0, i, 0))],
        out_specs=pl.BlockSpec((tr, w), lambda i: (i, 0)), out_shape=jax.ShapeDtypeStruct((r, w), F32),
        compiler_params=_cparams("parallel"))(buf)


def _allreduce(buf, name):
    r, w = buf.shape
    whole = lambda x, y, c: (slice(None), slice(None))
    slot = lambda x, y, c: (4 * x + 2 * y + c,)
    stage = [(("in", 0), whole, 0, slot, f) for f in [(0, 0, 0)] + FLIPS_ALL]
    (slots,) = _exchange([buf], [jax.ShapeDtypeStruct((8, r, w), F32)], [stage], name + "_x")
    return _sum_slots(slots, name + "_sum")


def _gather_plan(shards, src):
    half = lambda a, c: pl.ds(c * (a.shape[0] // 2), a.shape[0] // 2)
    first, second = [], []
    for n, a in enumerate(shards):
        for f in FLIPS_CHIP:
            first.append((("in", n), lambda x, y, c, a=a: (half(a, c), slice(None)), n,
                          lambda x, y, c, a=a: (2 * x + y, half(a, c), slice(None)), f))
            peer_slot = lambda x, y, c, a=a, f=f: (2 * (x ^ f[0]) + (y ^ f[1]), half(a, c), slice(None))
            second.append(((src, n), peer_slot, n, peer_slot, (0, 0, 1)))
    outs = [jax.ShapeDtypeStruct((4,) + a.shape, a.dtype) for a in shards]
    return first, second, outs


def _allgather_weights(shards, name):
    first, second, outs = _gather_plan(shards, "out")
    return _exchange(shards, outs, [first, second], name)


def _add_pair(G, bufA, cvec, name):
    _, Rs, Cs = G.shape
    Rh = Rs // 2
    tr = _pick(Rh, 128, 16)
    nb = Rh // tr

    def body(c_ref, g_ref, a_ref, o_ref):
        o_ref[...] = (g_ref[...] + a_ref[...]).astype(BF16)

    grid_spec = pltpu.PrefetchScalarGridSpec(
        num_scalar_prefetch=1, grid=(4, nb),
        in_specs=[pl.BlockSpec((1, tr, Cs), lambda s, i, c_ref: (s, c_ref[0] * nb + i, 0)),
                  pl.BlockSpec((1, tr, Cs), lambda s, i, c_ref: (s, i, 0))],
        out_specs=pl.BlockSpec((1, tr, Cs), lambda s, i, c_ref: (s, i, 0)))
    return pl.pallas_call(body, name=name, grid_spec=grid_spec, out_shape=jax.ShapeDtypeStruct((4, Rh, Cs), BF16),
                          compiler_params=_cparams("parallel", "parallel"))(cvec, G, bufA)


def _sum_chips(G, bufA, bufB, cvec, svec, name):
    _, Rs, Cs = G.shape
    Rh = Rs // 2
    tr = _pick(Rh, 128, 16)
    nb = Rh // tr

    def body(c_ref, s_ref, g_ref, a_ref, b_ref, o_ref):
        o_ref[...] = (g_ref[0] + a_ref[0]) + b_ref[0].astype(F32) + b_ref[1].astype(F32) + b_ref[2].astype(F32)

    grid_spec = pltpu.PrefetchScalarGridSpec(
        num_scalar_prefetch=2, grid=(nb,),
        in_specs=[pl.BlockSpec((1, tr, Cs), lambda i, c, s: (s[0], c[0] * nb + i, 0)),
                  pl.BlockSpec((1, tr, Cs), lambda i, c, s: (s[0], i, 0)),
                  pl.BlockSpec((3, tr, Cs), lambda i, c, s: (0, i, 0))],
        out_specs=pl.BlockSpec((tr, Cs), lambda i, c, s: (i, 0)))
    return pl.pallas_call(body, name=name, grid_spec=grid_spec, out_shape=jax.ShapeDtypeStruct((Rh, Cs), F32),
                          compiler_params=_cparams("parallel"))(cvec, svec, G, bufA, bufB)


def _pair_plan(grads):
    Rh = [g.shape[1] // 2 for g in grads]
    whole3 = lambda x, y, c: (slice(None), slice(None), slice(None))
    stage = [(("in", n), lambda x, y, c, n=n: (slice(None), pl.ds((1 - c) * Rh[n], Rh[n]), slice(None)), n,
              whole3, (0, 0, 1)) for n in range(len(grads))]
    return [jax.ShapeDtypeStruct((4, Rh[n], g.shape[2]), F32) for n, g in enumerate(grads)], stage


def _chips_plan(P):
    stage = [(("in", n), lambda x, y, c, f=f: (2 * (x ^ f[0]) + (y ^ f[1]),), n, lambda x, y, c, k=k: (k,), f)
             for n in range(len(P)) for k, f in enumerate(FLIPS_CHIP)]
    return [jax.ShapeDtypeStruct((3,) + p.shape[1:], BF16) for p in P], stage


def _halves_plan(mine):
    whole2 = lambda x, y, c: (slice(None), slice(None))
    stage = [(("in", n), whole2, n, whole2, (0, 0, 1)) for n in range(len(mine))]
    return [jax.ShapeDtypeStruct(r.shape, F32) for r in mine], stage


def _adamw_halves(w, mine, other, m, v, cvec, name):
    Rs, Cs = w.shape
    Rh = Rs // 2
    tr = _pick(Rh, 128, 8)
    nb = Rh // tr

    def body(c_ref, w_ref, a_ref, b_ref, m_ref, v_ref, g_ref, d_ref, mo_ref, vo_ref):
        gv = jnp.where(pl.program_id(0) // nb == c_ref[0], a_ref[...], b_ref[...])
        g_ref[...] = gv
        d_ref[...], mo_ref[...], vo_ref[...] = _adam_math(w_ref[...], gv, m_ref[...], v_ref[...])

    row = pl.BlockSpec((tr, Cs), lambda i, c: (i, 0))
    hrow = pl.BlockSpec((tr, Cs), lambda i, c: (i % nb, 0))
    grid_spec = pltpu.PrefetchScalarGridSpec(num_scalar_prefetch=1, grid=(2 * nb,),
                                             in_specs=[row, hrow, hrow, row, row], out_specs=[row] * 4)
    return pl.pallas_call(body, name=name, grid_spec=grid_spec, out_shape=[jax.ShapeDtypeStruct((Rs, Cs), F32)] * 4,
                          compiler_params=_cparams("parallel"))(cvec, w, mine, other, m, v)


def _pack(arrays):
    flat = [a.reshape(-1).astype(F32) for a in arrays]
    meta, off = [], 0
    for a, f in zip(arrays, flat):
        meta.append((off, a.shape))
        off += f.shape[0]
    total = -(-off // (8 * LANES)) * (8 * LANES)
    flat.append(jnp.zeros((total - off,), F32))
    return jnp.concatenate(flat).reshape(total // LANES, LANES), meta


def _unpack(buf, meta):
    flat = buf.reshape(-1)
    out = []
    for off, shape in meta:
        size = 1
        for s in shape:
            size *= s
        out.append(flat[off:off + size].reshape(shape))
    return out


WEIGHT_NAMES = ["c_ctx", "w_mod", "b_mod", "g_mix", "w_in", "q_norm", "k_norm", "attn_sink", "w_gate_f", "b_gate_f",
                "w_gate_b", "b_gate_b", "gla_norm", "w_attn_o", "w_gla_o", "w_out", "g_ffn", "w_up", "conv_w",
                "conv_b", "w_down"]
BIG_NAMES = ["w_in", "w_attn_o", "w_gla_o", "w_out", "w_up", "w_down"]
SHARDED_SMALL = ["w_gate_f", "w_gate_b", "conv_w"]


def _layouts(D):
    aw, kvw, gk, gv = N_Q_HEADS * HEAD_DIM, N_KV_HEADS * HEAD_DIM, D // 2, D
    widths = {"qa": aw, "ka": kvw, "va": kvw, "qb": gk, "kb": gk, "vb": gv, "rb": gv, "lr": 2 * GLA_LOWRANK,
              "ga": D, "gb": D}
    orig, off = {}, 0
    for s in ["qa", "ka", "va", "qb", "kb", "vb", "rb", "lr", "ga", "gb"]:
        orig[s] = off
        off += widths[s]
    order = ["qa", "vb", "rb", "ga", "gb", "ka", "va", "qb", "kb", "lr"]
    lay, off = {}, 0
    for s in order:
        lay[s] = off
        off += LANES if s == "lr" else widths[s]
    align = {"qa": aw, "vb": D, "rb": D, "ga": D, "gb": D, "ka": kvw, "va": kvw, "qb": gk, "kb": gk,
             "lr": LANES}
    for s in order:
        assert lay[s] % align[s] == 0, (s, lay[s], align[s])
    return widths, orig, order, lay, off


def _rope_tables(T, L):
    t = jnp.arange(T)
    nf = HEAD_DIM // 4
    inv = ROPE_THETA ** (-jnp.arange(nf, dtype=F32) / nf)
    ang = jnp.concatenate([(t // GRID_W)[:, None] * inv, (t % GRID_W)[:, None] * inv], axis=-1)
    cos, sin = jnp.cos(ang), jnp.sin(ang)
    cos2 = jnp.concatenate([jnp.ones((L, HEAD_DIM), F32), jnp.concatenate([cos, cos], axis=-1)], axis=0)
    sin2 = jnp.concatenate([jnp.zeros((L, HEAD_DIM), F32), jnp.concatenate([-sin, sin], axis=-1)], axis=0)
    return cos2, sin2


def _step(x, c, ctx, loss_target, W, M, V):
    xi, yi, ci = lax.axis_index("x"), lax.axis_index("y"), lax.axis_index("c")
    chip = 2 * xi + yi
    dev = 2 * chip + ci
    south = (ci == 0).astype(F32)
    cvec = ci.reshape(1).astype(jnp.int32)
    T, D = x.shape[1], x.shape[2]
    L = ctx.shape[1]
    R = L + T
    F = 4 * W["w_down"].shape[1]
    GK, GV = D // 2, D
    DK, DV = GK // GLA_HEADS, GV // GLA_HEADS
    N6 = 6 * D
    N4 = N6 // 4
    widths, orig, order, lay, Z = _layouts(D)

    def place_cols(shard, full_cols):
        cols = shard.shape[-1]
        full = jnp.zeros(shard.shape[:-1] + (full_cols,), F32)
        return lax.dynamic_update_slice(full, shard * south, (0,) * (shard.ndim - 1) + (chip * cols,))

    c_rows = lax.dynamic_update_slice(jnp.zeros((8, D), F32), c, (dev, 0))
    bufa, meta = _pack([c_rows, place_cols(W["w_gate_f"][0], GK), place_cols(W["w_gate_b"][0], GK),
                        place_cols(W["conv_w"][0], 2 * F)])
    c_all, wgf, wgb, cw = _unpack(_allreduce(bufa, "gather_small"), meta)
    ca = jnp.concatenate([c_all, W["c_ctx"][None, :], jnp.zeros((7, D), F32)], axis=0)
    b_shard = lax.dynamic_slice(W["b_mod"], (0, chip * N4), (1, N4))
    mod_part, sil = _mod_fwd(ca, W["w_mod"][0], b_shard, "mod_fwd")
    slots = lax.dynamic_update_slice(jnp.zeros((4, 16, N4), F32), (mod_part * south)[None], (chip, 0, 0))
    mod_all = _allreduce(slots.reshape(64, N4), "gather_mod").reshape(4, 16, N4).transpose(1, 0, 2).reshape(16, N6)
    mx = lax.dynamic_slice(mod_all, (dev, 0), (1, N6)).reshape(6, 1, D)
    mc = mod_all[8].reshape(6, 1, D)

    sq = lambda a: a.reshape(a.shape[1:])
    shards = [sq(W[n]).astype(BF16) for n in BIG_NAMES]
    own = lambda g, s: lax.dynamic_update_slice(g, s[None], (chip, 0, 0))
    cols = lambda g: g.transpose(1, 0, 2).reshape(g.shape[1], 4 * g.shape[2])
    rows = lambda g: g.reshape(4 * g.shape[1], g.shape[2])
    w_in_f = cols(own(_allgather_weights(shards[:1], "gather_w_in")[0], shards[0]))
    seg = lambda s: w_in_f[:, orig[s]:orig[s] + widths[s]]
    w_cat = jnp.concatenate([jnp.pad(seg(s), ((0, 0), (0, LANES - widths[s]))) if s == "lr" else seg(s)
                             for s in order], axis=1)
    gather1, gather2, gather_outs = _gather_plan(shards[1:], "in")
    wg = jnp.zeros((2, LANES, GK), F32).at[0, :GLA_LOWRANK].set(wgf).at[1, GLA_LOWRANK:2 * GLA_LOWRANK].set(wgb)
    bg = jnp.stack([W["b_gate_f"], W["b_gate_b"]])
    cb = W["conv_b"]
    sink_rows = jnp.broadcast_to(W["attn_sink"][0][:, None], (N_Q_HEADS, HEAD_DIM))
    cos2, sin2 = _rope_tables(T, L)
    blk = lambda s, w: lay[s] // w

    xall = jnp.concatenate([ctx[0], x[0]], axis=0)
    sc1 = jnp.stack([mc[1], mx[1]])
    sh1 = jnp.stack([mc[0], mx[0]])
    h = _modnorm_fwd(xall, W["g_mix"], sc1, sh1, L, "modnorm1")
    z, landed = _matmul(h, w_cat, "nn", F32, "proj_in", ride=(shards[1:], gather_outs, gather1, {}))
    qn = _qknorm_fwd(z, blk("qa", widths["qa"]), T, L, W["q_norm"], cos2, sin2, N_Q_HEADS, "qnorm")
    kn = _qknorm_fwd(z, blk("ka", widths["ka"]), R, 0, W["k_norm"], cos2, sin2, N_KV_HEADS, "knorm")
    vb = _cast_seg(z, blk("va", widths["va"]), widths["va"], "vcast")
    o_attn, landed = _attn_fwd(qn, kn, vb, sink_rows, L, "attn_fwd",
                               ride=(landed, gather_outs, gather2, {n: n for n in range(len(landed))}))
    g_ao, g_go, g_out, g_up, g_dn = [own(g, s) for g, s in zip(landed, shards[1:])]
    w_ao, w_go, w_out, w_up, w_dn = rows(g_ao), rows(g_go), rows(g_out), cols(g_up), rows(g_dn)
    gla_blks = (blk("qb", GK), blk("kb", GK), blk("vb", GV), blk("lr", LANES))
    o_g, sprev = _gla_fwd(z, *gla_blks, wg, bg, DV, L, "gla_fwd")
    p = _glanorm_fwd(o_g, z, blk("rb", D), W["gla_norm"], L, "glanorm")
    ya = _matmul(o_attn, w_ao, "nn", F32, "proj_attn_o")
    yg = _matmul(p, w_go, "nn", F32, "proj_gla_o")
    m = _gate_fwd(z, blk("ga", D), blk("gb", D), ya, yg, L, "gate")
    mix = _matmul(m, w_out, "nn", F32, "proj_out")
    x1, h2 = _resnorm_fwd(x[0], mix, mx[2], W["g_ffn"], mx[4], mx[3], "resnorm2")
    u = _matmul(h2, w_up, "nn", F32, "ffn_up")
    f = _conv_fwd(u, cw, cb, "conv_swiglu")
    d = _matmul(f, w_dn, "nn", F32, "ffn_down")
    dy, dd, lacc = _loss_head(d, x1, mx[5], loss_target[0], "loss_head")
    loss = lax.psum((0.5 / D) * jnp.sum(lacc[0]), ("x", "y", "c"))

    gw_dn = _matmul(f, dd, "tn", F32, "ffn_down_dw")
    df = _matmul(dd, w_dn, "nt", F32, "ffn_down_dx")
    du_a, du_g, acca, accg = _conv_bwd(u, df, cw, cb, "conv_swiglu_bwd")
    du = jnp.concatenate([du_a, du_g], axis=1)
    gw_up = _matmul(h2, du, "tn", F32, "ffn_up_dw")
    dh2 = _matmul(du, w_up, "nt", F32, "ffn_up_dx")
    dx1, dmix, s2 = _resnorm_bwd(x1, dh2, W["g_ffn"], mx[4], dy, mix, mx[2], "resnorm2_bwd")
    gw_out = _matmul(m, dmix, "tn", F32, "proj_out_dw")
    dm = _matmul(dmix, w_out, "nt", F32, "proj_out_dx")
    dya, dyg, dga, dgb = _gate_bwd(z, blk("ga", D), blk("gb", D), ya, yg, dm, L, "gate_bwd")
    gw_ao = _matmul(o_attn, dya, "tn", F32, "proj_attn_o_dw")
    do_attn = _matmul(dya, w_ao, "nt", BF16, "proj_attn_o_dx")
    gw_go = _matmul(p, dyg, "tn", F32, "proj_gla_o_dw")
    dp = _matmul(dyg, w_go, "nt", F32, "proj_gla_o_dx")
    do_gla, drb, s_gn = _glanorm_bwd(o_g, z, blk("rb", D), W["gla_norm"], dp, L, "glanorm_bwd")
    do_pad = jnp.concatenate([jnp.zeros((L, GV), BF16), do_gla], axis=0)
    by_cols = lambda g: g.reshape(g.shape[0], 4, g.shape[1] // 4).transpose(1, 0, 2)
    by_rows = lambda g: g.reshape(4, g.shape[0] // 4, g.shape[1])
    svec = chip.reshape(1).astype(jnp.int32)
    early = [by_rows(gw_ao), by_rows(gw_go), by_rows(gw_out), by_cols(gw_up), by_rows(gw_dn)]
    (dqg, dkg, dvg, dpre, dbg), pair_e = _gla_bwd(z, *gla_blks, wg, bg, sprev, do_pad, L, "gla_bwd",
                                                  ride=(early, *_pair_plan(early), {}))
    sums_e = [_add_pair(g, a, cvec, "reduce_early_add%d" % n) for n, (g, a) in enumerate(zip(early, pair_e))]
    wg_cat = jnp.concatenate([wg[0], wg[1]], axis=1)
    dlr = _matmul(dpre, wg_cat, "nt", BF16, "gla_gate_dx")
    dwg = _matmul(z[:, lay["lr"]:lay["lr"] + LANES], dpre, "tn", F32, "gla_gate_dw")
    (dqn, dkw, dvw, dkc, dvc, dsn), chips_e = _attn_bwd(qn, kn, vb, sink_rows, do_attn, L, "attn_bwd",
                                                        ride=(sums_e, *_chips_plan(sums_e), {}))
    mine_e = [_sum_chips(g, a, b, cvec, svec, "reduce_early_sum%d" % n)
              for n, (g, a, b) in enumerate(zip(early, pair_e, chips_e))]
    dqa, s_qn = _qknorm_bwd(z, blk("qa", widths["qa"]), T, L, W["q_norm"], cos2, sin2, dqn, N_Q_HEADS, "qnorm_bwd")
    dk_all = jnp.concatenate([dkc, dkw[WINDOW:WINDOW + T]], axis=0)
    dv_all = jnp.concatenate([dvc, dvw[WINDOW:WINDOW + T]], axis=0)
    dka, s_kn = _qknorm_bwd(z, blk("ka", widths["ka"]), R, 0, W["k_norm"], cos2, sin2, dk_all, N_KV_HEADS, "knorm_bwd")
    dz = _assemble_dz(lay, Z, L, dqa, drb, dga, dgb, dka, dv_all, dvg, dqg, dkg, dlr, "assemble_dz")
    gw_cat, other_e = _matmul(h, dz, "tn", F32, "proj_in_dw", ride=(mine_e, *_halves_plan(mine_e), {}))
    gw_in = jnp.concatenate([gw_cat[:, lay[s]:lay[s] + widths[s]] for s in ["qa", "ka", "va", "qb", "kb", "vb", "rb",
                                                                           "lr", "ga", "gb"]], axis=1)
    late = [by_cols(gw_in)]
    shapes, stage = _pair_plan(late)
    pair_l = _exchange(late, shapes, [stage], "reduce_late_pair")
    sums_l = [_add_pair(late[0], pair_l[0], cvec, "reduce_late_add")]
    dh, chips_l = _matmul(dz, w_cat, "nt", F32, "proj_in_dx", ride=(sums_l, *_chips_plan(sums_l), {}))
    mine_l = [_sum_chips(late[0], pair_l[0], chips_l[0], cvec, svec, "reduce_late_sum")]
    shapes, stage = _halves_plan(mine_l)
    other_l = _exchange(mine_l, shapes, [stage], "reduce_late_halves")
    mine, other = mine_l + mine_e, list(other_l) + other_e
    grad_x, s1 = _modnorm_bwd(x[0], dh, W["g_mix"], mx[1], dx1, "modnorm1_bwd", dh_roff=L)
    _, s1c = _modnorm_bwd(ctx[0], dh, W["g_mix"], mc[1], None, "modnorm1_ctx_bwd")

    dmod_x = jnp.concatenate([s1[0], s1[1], s2[3], s2[0], s2[1], lacc[1]])
    dmod_c = jnp.concatenate([s1c[0], s1c[1], jnp.zeros((4 * D,), F32)])
    dmod_rows = lax.dynamic_update_slice(jnp.zeros((9, N6), F32).at[8].set(dmod_c), dmod_x[None], (dev, 0))
    small = [dmod_rows, dmod_x + dmod_c, s1[2] + s1c[2], s_qn[0], s_kn[0], dsn[:, 0, :Q_PER_KV].reshape(N_Q_HEADS),
             dwg[:GLA_LOWRANK, :GK], dbg[0].reshape(GK), dwg[GLA_LOWRANK:2 * GLA_LOWRANK, GK:], dbg[1].reshape(GK),
             s_gn[0], s2[2], jnp.concatenate([acca[0:3], accg[0:3]], axis=1), jnp.concatenate([acca[3], accg[3]])]
    bufc, meta = _pack(small)
    (dmod_sum, g_b_mod, g_g_mix, g_q_norm, g_k_norm, g_sink, g_wgf, g_bgf, g_wgb, g_bgb, g_gla_norm, g_g_ffn,
     g_conv_w, g_conv_b) = _unpack(_allreduce(bufc, "reduce_small"), meta)
    dmod16 = lax.dynamic_slice(jnp.concatenate([dmod_sum, jnp.zeros((7, N6), F32)], axis=0), (0, chip * N4), (16, N4))
    g_w_mod = _matmul(sil, dmod16, "tn", F32, "mod_dw")
    dsil = _matmul(dmod16, W["w_mod"][0], "nt", F32, "mod_dx")
    g_c_ctx = _silu_bwd(_allreduce(dsil * south, "reduce_cctx"), ca, "silu_bwd")[8]

    cut = lambda g: lax.dynamic_slice(g, (0, chip * (g.shape[1] // 4)), (g.shape[0], g.shape[1] // 4))
    grads = {"c_ctx": g_c_ctx, "w_mod": g_w_mod[None], "b_mod": g_b_mod[None], "g_mix": g_g_mix[None],
             "q_norm": g_q_norm[None], "k_norm": g_k_norm[None], "attn_sink": g_sink[None],
             "w_gate_f": cut(g_wgf)[None], "b_gate_f": g_bgf[None], "w_gate_b": cut(g_wgb)[None],
             "b_gate_b": g_bgb[None], "gla_norm": g_gla_norm[None], "g_ffn": g_g_ffn[None],
             "conv_w": cut(g_conv_w)[None], "conv_b": g_conv_b[None]}

    delta, new_m, new_v = {}, {}, {}
    dl, mn, vn = _adamw(W["w_mod"][0], g_w_mod, M["w_mod"][0], V["w_mod"][0], "adamw_w_mod")
    delta["w_mod"], new_m["w_mod"], new_v["w_mod"] = dl[None], mn[None], vn[None]
    for n, a, b in zip(BIG_NAMES, mine, other):
        g, dl, mn, vn = _adamw_halves(sq(W[n]), a, b, sq(M[n]), sq(V[n]), cvec, "adamw_" + n)
        grads[n], delta[n], new_m[n], new_v[n] = g[None], dl[None], mn[None], vn[None]
    small_names = [n for n in WEIGHT_NAMES if n not in delta]
    packs = [_pack([src[n] for n in small_names]) for src in (W, grads, M, V)]
    meta = packs[0][1]
    outs = _adamw(packs[0][0], packs[1][0], packs[2][0], packs[3][0], "adamw_small")
    for res, o in zip((delta, new_m, new_v), outs):
        for n, a in zip(small_names, _unpack(o, meta)):
            res[n] = a
    return (loss, grad_x[None], *[grads[n] for n in WEIGHT_NAMES], *[delta[n] for n in WEIGHT_NAMES],
            *[new_m[n] for n in WEIGHT_NAMES], *[new_v[n] for n in WEIGHT_NAMES])


def kernel(x, c, ctx, c_ctx, w_mod, b_mod, g_mix, w_in, q_norm, k_norm, attn_sink, w_gate_f, b_gate_f, w_gate_b, b_gate_b, gla_norm, w_attn_o, w_gla_o, w_out, g_ffn, w_up, conv_w, conv_b, w_down, loss_target, m_c_ctx, m_w_mod, m_b_mod, m_g_mix, m_w_in, m_q_norm, m_k_norm, m_attn_sink, m_w_gate_f, m_b_gate_f, m_w_gate_b, m_b_gate_b, m_gla_norm, m_w_attn_o, m_w_gla_o, m_w_out, m_g_ffn, m_w_up, m_conv_w, m_conv_b, m_w_down, v_c_ctx, v_w_mod, v_b_mod, v_g_mix, v_w_in, v_q_norm, v_k_norm, v_attn_sink, v_w_gate_f, v_b_gate_f, v_w_gate_b, v_b_gate_b, v_gla_norm, v_w_attn_o, v_w_gla_o, v_w_out, v_g_ffn, v_w_up, v_conv_w, v_conv_b, v_w_down):
    W = dict(zip(WEIGHT_NAMES, (c_ctx, w_mod, b_mod, g_mix, w_in, q_norm, k_norm, attn_sink, w_gate_f, b_gate_f,
                                w_gate_b, b_gate_b, gla_norm, w_attn_o, w_gla_o, w_out, g_ffn, w_up, conv_w, conv_b,
                                w_down)))
    M = dict(zip(WEIGHT_NAMES, (m_c_ctx, m_w_mod, m_b_mod, m_g_mix, m_w_in, m_q_norm, m_k_norm, m_attn_sink,
                                m_w_gate_f, m_b_gate_f, m_w_gate_b, m_b_gate_b, m_gla_norm, m_w_attn_o, m_w_gla_o,
                                m_w_out, m_g_ffn, m_w_up, m_conv_w, m_conv_b, m_w_down)))
    V = dict(zip(WEIGHT_NAMES, (v_c_ctx, v_w_mod, v_b_mod, v_g_mix, v_w_in, v_q_norm, v_k_norm, v_attn_sink,
                                v_w_gate_f, v_b_gate_f, v_w_gate_b, v_b_gate_b, v_gla_norm, v_w_attn_o, v_w_gla_o,
                                v_w_out, v_g_ffn, v_w_up, v_conv_w, v_conv_b, v_w_down)))
    return _step(x, c, ctx, loss_target, W, M, V)
```

```python
import functools
import math

import jax
import jax.numpy as jnp
from jax import lax
from jax.experimental import pallas as pl
from jax.experimental.pallas import tpu as pltpu

F32 = jnp.float32
BF16 = jnp.bfloat16
MESH = pl.DeviceIdType.MESH

EPS = 1e-6
HEAD_DIM = 128
N_Q_HEADS = 16
N_KV_HEADS = 4
Q_PER_KV = N_Q_HEADS // N_KV_HEADS
WINDOW = 128
GLA_HEADS = 4
GLA_LOWRANK = 16
GLA_GATE_NORM = 16.0
GLA_CHUNK = 64
GRID_W = 64
ROPE_THETA = 10000.0
GLA_LEVELS = (32, 16, 8, 4, 2, 1)
LANES = 128

ADAM_LR = 0.001
ADAM_B1 = 0.9
ADAM_B2 = 0.999
ADAM_EPS = 1e-08
ADAM_WD = 0.01
ADAM_STEP = 10

VMEM_LIMIT = 52 * 1024 * 1024


def _cparams(*sem):
    return pltpu.CompilerParams(dimension_semantics=sem, vmem_limit_bytes=VMEM_LIMIT)


def _pick(n, target, mult=LANES):
    best = None
    d = mult
    while d <= min(n, target):
        if n % d == 0:
            best = d
        d += mult
    return n if best is None else best


def _sigmoid(x):
    return 1.0 / (1.0 + jnp.exp(-x))


def _silu(x):
    return x * _sigmoid(x)


def _dsilu(x):
    s = _sigmoid(x)
    return s * (1.0 + x * (1.0 - s))


def _dot(a, b, dims):
    return lax.dot_general(a, b, (dims, ((), ())), preferred_element_type=F32)


NN = ((1,), (0,))
NT = ((1,), (1,))
TN = ((0,), (0,))


def _matmul(a, b, mode, out_dtype, name, tm=768, tn=1024, tk=2048, ride=None):
    if mode == "nn":
        (M, K), (K2, N) = a.shape, b.shape
    elif mode == "nt":
        (M, K), (N, K2) = a.shape, b.shape
    else:
        (K, M), (K2, N) = a.shape, b.shape
    assert K == K2, (name, a.shape, b.shape)
    if mode == "tn":
        tm = max(tm, 1024)
    tm, tn, tk = _pick(M, tm), _pick(N, tn), _pick(K, tk)
    nk = K // tk
    dims = {"nn": NN, "nt": NT, "tn": TN}[mode]

    def body(a_ref, b_ref, o_ref, acc_ref):
        k = pl.program_id(2)

        @pl.when(k == 0)
        def _():
            acc_ref[...] = jnp.zeros_like(acc_ref)

        acc_ref[...] += _dot(a_ref[...].astype(BF16), b_ref[...].astype(BF16), dims)

        @pl.when(k == nk - 1)
        def _():
            o_ref[...] = acc_ref[...].astype(out_dtype)

    if mode == "tn":
        a_spec = pl.BlockSpec((tk, tm), lambda i, j, k: (k, i))
    else:
        a_spec = pl.BlockSpec((tm, tk), lambda i, j, k: (i, k))
    if mode == "nt":
        b_spec = pl.BlockSpec((tn, tk), lambda i, j, k: (j, k))
    else:
        b_spec = pl.BlockSpec((tk, tn), lambda i, j, k: (k, j))
    return _pcall(
        body, name=name, grid=(M // tm, N // tn, nk),
        in_specs=[a_spec, b_spec],
        out_specs=pl.BlockSpec((tm, tn), lambda i, j, k: (i, j)),
        out_shape=jax.ShapeDtypeStruct((M, N), out_dtype),
        scratch_shapes=[pltpu.VMEM((tm, tn), F32)],
        sem=("parallel", "parallel", "arbitrary"), args=(a, b), ride=ride)


def _modnorm_fwd(xall, g, sc, sh, n_ctx, name):
    R, D = xall.shape
    tm = _pick(n_ctx, 256, 8)
    cb = n_ctx // tm

    def body(x_ref, g_ref, sc_ref, sh_ref, h_ref):
        x = x_ref[...]
        r = lax.rsqrt(jnp.mean(x * x, axis=-1, keepdims=True) + EPS)
        n = x * r * g_ref[...]
        h_ref[...] = (n * (1.0 + sc_ref[0]) + sh_ref[0]).astype(BF16)

    sel = lambda i: (jnp.where(i < cb, 0, 1), 0, 0)
    return pl.pallas_call(
        body, name=name, grid=(R // tm,),
        in_specs=[pl.BlockSpec((tm, D), lambda i: (i, 0)), pl.BlockSpec((1, D), lambda i: (0, 0)),
                  pl.BlockSpec((1, 1, D), sel), pl.BlockSpec((1, 1, D), sel)],
        out_specs=pl.BlockSpec((tm, D), lambda i: (i, 0)),
        out_shape=jax.ShapeDtypeStruct((R, D), BF16),
        compiler_params=_cparams("parallel"),
    )(xall, g, sc, sh)


def _modnorm_bwd(x, dh, g, sc, resid, name, dh_roff=0):
    N, D = x.shape
    tm = _pick(math.gcd(N, dh_roff), 256, 8)
    ro = dh_roff // tm
    want_dx = resid is not None

    def body(*refs):
        if want_dx:
            x_ref, dh_ref, g_ref, sc_ref, res_ref, dx_ref, acc_ref = refs
        else:
            x_ref, dh_ref, g_ref, sc_ref, acc_ref = refs
        i = pl.program_id(0)

        @pl.when(i == 0)
        def _():
            acc_ref[...] = jnp.zeros_like(acc_ref)

        xv, dhv, gv = x_ref[...], dh_ref[...], g_ref[...]
        r = lax.rsqrt(jnp.mean(xv * xv, axis=-1, keepdims=True) + EPS)
        xh = xv * r
        dn = dhv * (1.0 + sc_ref[...])
        acc_ref[0:1, :] += jnp.sum(dhv, axis=0, keepdims=True)
        acc_ref[1:2, :] += jnp.sum(dhv * xh * gv, axis=0, keepdims=True)
        acc_ref[2:3, :] += jnp.sum(dn * xh, axis=0, keepdims=True)
        if want_dx:
            dxh = dn * gv
            dx_ref[...] = res_ref[...] + r * (dxh - xh * jnp.mean(dxh * xh, axis=-1, keepdims=True))

    row = pl.BlockSpec((tm, D), lambda i: (i, 0))
    drow = pl.BlockSpec((tm, D), lambda i: (i + ro, 0))
    vec = pl.BlockSpec((1, D), lambda i: (0, 0))
    acc = pl.BlockSpec((8, D), lambda i: (0, 0))
    acc_shape = jax.ShapeDtypeStruct((8, D), F32)
    if want_dx:
        return pl.pallas_call(
            body, name=name, grid=(N // tm,), in_specs=[row, drow, vec, vec, row],
            out_specs=[row, acc], out_shape=[jax.ShapeDtypeStruct((N, D), F32), acc_shape],
            compiler_params=_cparams("arbitrary"))(x, dh, g, sc, resid)
    sums = pl.pallas_call(
        body, name=name, grid=(N // tm,), in_specs=[row, drow, vec, vec],
        out_specs=acc, out_shape=acc_shape, compiler_params=_cparams("arbitrary"))(x, dh, g, sc)
    return None, sums


def _resnorm_bwd(x1, dh, g, sc, dy, mix, gt, name):
    N, D = x1.shape
    tm = _pick(N, 256, 8)

    def body(x_ref, dh_ref, g_ref, sc_ref, dy_ref, mix_ref, gt_ref, dx_ref, dm_ref, acc_ref):
        i = pl.program_id(0)

        @pl.when(i == 0)
        def _():
            acc_ref[...] = jnp.zeros_like(acc_ref)

        xv, dhv, gv = x_ref[...], dh_ref[...], g_ref[...]
        r = lax.rsqrt(jnp.mean(xv * xv, axis=-1, keepdims=True) + EPS)
        xh = xv * r
        dn = dhv * (1.0 + sc_ref[...])
        dxh = dn * gv
        dx = dy_ref[...] + r * (dxh - xh * jnp.mean(dxh * xh, axis=-1, keepdims=True))
        dx_ref[...] = dx
        dm_ref[...] = (dx * gt_ref[...]).astype(BF16)
        acc_ref[0:1, :] += jnp.sum(dhv, axis=0, keepdims=True)
        acc_ref[1:2, :] += jnp.sum(dhv * xh * gv, axis=0, keepdims=True)
        acc_ref[2:3, :] += jnp.sum(dn * xh, axis=0, keepdims=True)
        acc_ref[3:4, :] += jnp.sum(dx * mix_ref[...], axis=0, keepdims=True)

    row = pl.BlockSpec((tm, D), lambda i: (i, 0))
    vec = pl.BlockSpec((1, D), lambda i: (0, 0))
    return pl.pallas_call(
        body, name=name, grid=(N // tm,), in_specs=[row, row, vec, vec, row, row, vec],
        out_specs=[row, row, pl.BlockSpec((8, D), lambda i: (0, 0))],
        out_shape=[jax.ShapeDtypeStruct((N, D), F32), jax.ShapeDtypeStruct((N, D), BF16),
                   jax.ShapeDtypeStruct((8, D), F32)],
        compiler_params=_cparams("arbitrary"))(x1, dh, g, sc, dy, mix, gt)


def _qknorm_fwd(z, cblk, nrows, roff, w, cos2, sin2, nh, name):
    W = nh * HEAD_DIM
    tm = _pick(math.gcd(nrows, roff), 256, 8)
    ro = roff // tm
    assert roff % tm == 0

    def body(z_ref, w_ref, c_ref, s_ref, o_ref):
        c, s, wv = c_ref[...], s_ref[...], w_ref[...]
        for h in range(nh):
            x = z_ref[:, h * HEAD_DIM:(h + 1) * HEAD_DIM]
            r = lax.rsqrt(jnp.mean(x * x, axis=-1, keepdims=True) + EPS)
            y = x * r * wv
            o_ref[:, h * HEAD_DIM:(h + 1) * HEAD_DIM] = (y * c + pltpu.roll(y, HEAD_DIM // 2, 1) * s).astype(BF16)

    return pl.pallas_call(
        body, name=name, grid=(nrows // tm,),
        in_specs=[pl.BlockSpec((tm, W), lambda i: (i + ro, cblk)), pl.BlockSpec((1, HEAD_DIM), lambda i: (0, 0)),
                  pl.BlockSpec((tm, HEAD_DIM), lambda i: (i + ro, 0)), pl.BlockSpec((tm, HEAD_DIM), lambda i: (i + ro, 0))],
        out_specs=pl.BlockSpec((tm, W), lambda i: (i, 0)),
        out_shape=jax.ShapeDtypeStruct((nrows, W), BF16),
        compiler_params=_cparams("parallel"),
    )(z, w, cos2, sin2)


def _qknorm_bwd(z, cblk, nrows, roff, w, cos2, sin2, dy, nh, name):
    W = nh * HEAD_DIM
    tm = _pick(math.gcd(nrows, roff), 256, 8)
    ro = roff // tm

    def body(z_ref, w_ref, c_ref, s_ref, dy_ref, dz_ref, acc_ref):
        i = pl.program_id(0)

        @pl.when(i == 0)
        def _():
            acc_ref[...] = jnp.zeros_like(acc_ref)

        c, s, wv = c_ref[...], s_ref[...], w_ref[...]
        dw = jnp.zeros((1, HEAD_DIM), F32)
        for h in range(nh):
            sl = slice(h * HEAD_DIM, (h + 1) * HEAD_DIM)
            x = z_ref[:, sl]
            d = dy_ref[:, sl]
            dyn = d * c + pltpu.roll(d * s, HEAD_DIM // 2, 1)
            r = lax.rsqrt(jnp.mean(x * x, axis=-1, keepdims=True) + EPS)
            xh = x * r
            dw = dw + jnp.sum(dyn * xh, axis=0, keepdims=True)
            dxh = dyn * wv
            dz_ref[:, sl] = (r * (dxh - xh * jnp.mean(dxh * xh, axis=-1, keepdims=True))).astype(BF16)
        acc_ref[0:1, :] += dw

    return pl.pallas_call(
        body, name=name, grid=(nrows // tm,),
        in_specs=[pl.BlockSpec((tm, W), lambda i: (i + ro, cblk)), pl.BlockSpec((1, HEAD_DIM), lambda i: (0, 0)),
                  pl.BlockSpec((tm, HEAD_DIM), lambda i: (i + ro, 0)), pl.BlockSpec((tm, HEAD_DIM), lambda i: (i + ro, 0)),
                  pl.BlockSpec((tm, W), lambda i: (i, 0))],
        out_specs=[pl.BlockSpec((tm, W), lambda i: (i, 0)), pl.BlockSpec((8, HEAD_DIM), lambda i: (0, 0))],
        out_shape=[jax.ShapeDtypeStruct((nrows, W), BF16), jax.ShapeDtypeStruct((8, HEAD_DIM), F32)],
        compiler_params=_cparams("arbitrary"),
    )(z, w, cos2, sin2, dy)


def _cast_seg(z, cblk, width, name):
    R = z.shape[0]
    tm = _pick(R, 512, 8)

    def body(z_ref, o_ref):
        o_ref[...] = z_ref[...].astype(BF16)

    return pl.pallas_call(
        body, name=name, grid=(R // tm,),
        in_specs=[pl.BlockSpec((tm, width), lambda i: (i, cblk))],
        out_specs=pl.BlockSpec((tm, width), lambda i: (i, 0)),
        out_shape=jax.ShapeDtypeStruct((R, width), BF16), compiler_params=_cparams("parallel"))(z)


NEG_BIG = -1e30


def _attn_specs(T, n_ctx):
    nb = T // WINDOW
    lb = n_ctx // WINDOW
    blk = lambda f: pl.BlockSpec((WINDOW, HEAD_DIM), f)
    win = [blk(lambda h, i: (lb + jnp.maximum(i - 1, 0), h)), blk(lambda h, i: (lb + i, h)),
           blk(lambda h, i: (lb + jnp.minimum(i + 1, nb - 1), h))]
    ctx = pl.BlockSpec((n_ctx, HEAD_DIM), lambda h, i: (0, h))
    qspec = pl.BlockSpec((WINDOW, Q_PER_KV * HEAD_DIM), lambda h, i: (i, h))
    sink = pl.BlockSpec((N_Q_HEADS, HEAD_DIM), lambda h, i: (0, 0))
    return nb, qspec, win, ctx, sink


def _attn_probs(q, kw, kctx, snk, valid):
    scale = HEAD_DIM ** -0.5
    s_lat = jnp.where(valid, _dot(q, kw, NT) * scale, NEG_BIG)
    s_ctx = _dot(q, kctx, NT) * scale
    m = jnp.maximum(jnp.maximum(jnp.max(s_lat, axis=-1, keepdims=True), jnp.max(s_ctx, axis=-1, keepdims=True)), snk)
    p_lat = jnp.exp(s_lat - m)
    p_ctx = jnp.exp(s_ctx - m)
    p_snk = jnp.exp(snk - m)
    den = p_snk + jnp.sum(p_lat, axis=-1, keepdims=True) + jnp.sum(p_ctx, axis=-1, keepdims=True)
    return p_lat, p_ctx, p_snk, den


def _attn_valid(i, T, heads):
    rows = heads * WINDOW
    qpos = i * WINDOW + (lax.broadcasted_iota(jnp.int32, (rows, 3 * WINDOW), 0) & (WINDOW - 1))
    kpos = (i - 1) * WINDOW + lax.broadcasted_iota(jnp.int32, (rows, 3 * WINDOW), 1)
    return (jnp.abs(qpos - kpos) <= WINDOW) & (kpos >= 0) & (kpos < T)


def _stack_heads(ref):
    return jnp.concatenate([ref[:, g * HEAD_DIM:(g + 1) * HEAD_DIM] for g in range(Q_PER_KV)], axis=0)


def _stack_sinks(sink_ref, h):
    return jnp.concatenate([jnp.broadcast_to(sink_ref[pl.ds(h * Q_PER_KV + g, 1), :][:, 0:1], (WINDOW, 1))
                            for g in range(Q_PER_KV)], axis=0)


def _attn_fwd(qn, kn, vb, sink_rows, n_ctx, name, ride=None):
    T = qn.shape[0]
    nb, qspec, win, ctx, sink = _attn_specs(T, n_ctx)

    def body(q_ref, kp, kc, kx, vp, vc, vx, kctx_ref, vctx_ref, sink_ref, o_ref):
        h, i = pl.program_id(0), pl.program_id(1)
        kw = jnp.concatenate([kp[...], kc[...], kx[...]], axis=0)
        vw = jnp.concatenate([vp[...], vc[...], vx[...]], axis=0)
        kctx, vctx = kctx_ref[...], vctx_ref[...]
        p_lat, p_ctx, _, den = _attn_probs(_stack_heads(q_ref), kw, kctx, _stack_sinks(sink_ref, h),
                                           _attn_valid(i, T, Q_PER_KV))
        o = ((_dot(p_lat.astype(BF16), vw, NN) + _dot(p_ctx.astype(BF16), vctx, NN)) / den).astype(BF16)
        for g in range(Q_PER_KV):
            o_ref[:, g * HEAD_DIM:(g + 1) * HEAD_DIM] = o[g * WINDOW:(g + 1) * WINDOW]

    return _pcall(
        body, name=name, grid=(N_KV_HEADS, nb),
        in_specs=[qspec] + win + win + [ctx, ctx, sink],
        out_specs=qspec, out_shape=jax.ShapeDtypeStruct(qn.shape, BF16),
        sem=("parallel", "parallel"), args=(qn, kn, kn, kn, vb, vb, vb, kn, vb, sink_rows), ride=ride)


def _attn_bwd(qn, kn, vb, sink_rows, do, n_ctx, name, ride=None):
    T = qn.shape[0]
    nb, qspec, win, ctx, sink = _attn_specs(T, n_ctx)
    scale = HEAD_DIM ** -0.5
    TP = T + 2 * WINDOW

    def body(q_ref, kp, kc, kx, vp, vc, vx, kctx_ref, vctx_ref, sink_ref, do_ref,
             dq_ref, dkw_ref, dvw_ref, dkc_ref, dvc_ref, dsn_ref):
        h, i = pl.program_id(0), pl.program_id(1)

        @pl.when(i == 0)
        def _():
            dkw_ref[...] = jnp.zeros_like(dkw_ref)
            dvw_ref[...] = jnp.zeros_like(dvw_ref)
            dkc_ref[...] = jnp.zeros_like(dkc_ref)
            dvc_ref[...] = jnp.zeros_like(dvc_ref)
            dsn_ref[...] = jnp.zeros_like(dsn_ref)

        kw = jnp.concatenate([kp[...], kc[...], kx[...]], axis=0)
        vw = jnp.concatenate([vp[...], vc[...], vx[...]], axis=0)
        kctx, vctx = kctx_ref[...], vctx_ref[...]
        lane = lax.broadcasted_iota(jnp.int32, (8, HEAD_DIM), 1)
        q, d_o = _stack_heads(q_ref), _stack_heads(do_ref)
        p_lat, p_ctx, p_snk, den = _attn_probs(q, kw, kctx, _stack_sinks(sink_ref, h), _attn_valid(i, T, Q_PER_KV))
        inv = 1.0 / den
        p_lat, p_ctx, p_snk = p_lat * inv, p_ctx * inv, p_snk * inv
        dp_lat = _dot(d_o, vw, NT)
        dp_ctx = _dot(d_o, vctx, NT)
        dr = jnp.sum(p_lat * dp_lat, axis=-1, keepdims=True) + jnp.sum(p_ctx * dp_ctx, axis=-1, keepdims=True)
        ds_lat = (p_lat * (dp_lat - dr) * scale).astype(BF16)
        ds_ctx = (p_ctx * (dp_ctx - dr) * scale).astype(BF16)
        dq = _dot(ds_lat, kw, NN) + _dot(ds_ctx, kctx, NN)
        snk_terms = p_snk * dr
        dsn = jnp.zeros((8, HEAD_DIM), F32)
        for g in range(Q_PER_KV):
            dq_ref[:, g * HEAD_DIM:(g + 1) * HEAD_DIM] = dq[g * WINDOW:(g + 1) * WINDOW]
            dsn = dsn + jnp.where(lane == g, -jnp.sum(snk_terms[g * WINDOW:(g + 1) * WINDOW], axis=0, keepdims=True), 0.0)
        rows = pl.ds(pl.multiple_of(i * WINDOW, WINDOW), 3 * WINDOW)
        dkw_ref[rows, :] += _dot(ds_lat, q, TN)
        dvw_ref[rows, :] += _dot(p_lat.astype(BF16), d_o, TN)
        dkc_ref[...] += _dot(ds_ctx, q, TN)
        dvc_ref[...] += _dot(p_ctx.astype(BF16), d_o, TN)
        dsn_ref[0] += dsn

    wacc = pl.BlockSpec((TP, HEAD_DIM), lambda h, i: (0, h))
    return _pcall(
        body, name=name, grid=(N_KV_HEADS, nb),
        in_specs=[qspec] + win + win + [ctx, ctx, sink, qspec],
        out_specs=[qspec, wacc, wacc, ctx, ctx, pl.BlockSpec((1, 8, HEAD_DIM), lambda h, i: (h, 0, 0))],
        out_shape=[jax.ShapeDtypeStruct(qn.shape, F32),
                   jax.ShapeDtypeStruct((TP, N_KV_HEADS * HEAD_DIM), F32),
                   jax.ShapeDtypeStruct((TP, N_KV_HEADS * HEAD_DIM), F32),
                   jax.ShapeDtypeStruct((n_ctx, N_KV_HEADS * HEAD_DIM), F32),
                   jax.ShapeDtypeStruct((n_ctx, N_KV_HEADS * HEAD_DIM), F32),
                   jax.ShapeDtypeStruct((N_KV_HEADS, 8, HEAD_DIM), F32)],
        sem=("arbitrary", "arbitrary"), args=(qn, kn, kn, kn, vb, vb, vb, kn, vb, sink_rows, do), ride=ride)


def _gla_masks(dirv):
    C = GLA_CHUNK
    r = lax.broadcasted_iota(jnp.int32, (C, C), 0)
    c = lax.broadcasted_iota(jnp.int32, (C, C), 1)
    tt = jnp.where(dirv == 0, r, C - 1 - r)
    ss = jnp.where(dirv == 0, c, C - 1 - c)
    le = (ss <= tt).astype(jnp.int32)
    sums = [le == 1, le == 0]
    blocks = [ss == tt]
    for m in GLA_LEVELS:
        sh = m.bit_length() - 1
        same = (tt >> (sh + 1)) == (ss >> (sh + 1))
        ut = (tt >> sh) & 1
        us = (ss >> sh) & 1
        sums.append(same & (ut == us) & (ut == le))
        blocks.append(same & (ut == 1) & (us == 0))
    mall = jnp.concatenate([jnp.where(s, 1.0, 0.0) for s in sums], axis=0).astype(BF16)
    return mall, blocks


def _split3(x):
    hi = x.astype(BF16)
    r1 = x - hi.astype(F32)
    mid = r1.astype(BF16)
    lo = (r1 - mid.astype(F32)).astype(BF16)
    return hi, mid, lo


def _dot3(m_bf16, x, dims):
    hi, mid, lo = _split3(x)
    return _dot(m_bf16, hi, dims) + _dot(m_bf16, mid, dims) + _dot(m_bf16, lo, dims)


def _gla_chunk_of(dirv, j, lc, nc):
    return jnp.where(dirv == 0, j, jnp.where(j < lc, lc - 1 - j, nc + lc - 1 - j))


def _gla_gate(lr_ref, wg_ref, bg_ref):
    pre = _dot(lr_ref[...].astype(BF16), wg_ref[0].astype(BF16), NN) + bg_ref[0]
    g = (jnp.minimum(pre, 0.0) - jnp.log(1.0 + jnp.exp(-jnp.abs(pre)))) * (1.0 / GLA_GATE_NORM)
    return pre, g


def _gla_fwd(z, qblk, kblk, vblk, lrblk, wg, bg, DV, n_ctx, name):
    R = z.shape[0]
    C = GLA_CHUNK
    DK = wg.shape[2] // GLA_HEADS
    nc, lc = R // C, n_ctx // C
    qscale = DK ** -0.5

    GK, GV = GLA_HEADS * DK, GLA_HEADS * DV

    def body(q_ref, k_ref, v_ref, lr_ref, wg_ref, bg_ref, o_ref, sp_ref, st_ref):
        dirv, j = pl.program_id(0), pl.program_id(1)

        @pl.when(j == 0)
        def _():
            st_ref[...] = jnp.zeros_like(st_ref)

        mall, blocks = _gla_masks(dirv)
        _, g_all = _gla_gate(lr_ref, wg_ref, bg_ref)
        E_all = _dot3(mall, g_all, NN)
        for h in range(GLA_HEADS):
            ks, vs = slice(h * DK, (h + 1) * DK), slice(h * DV, (h + 1) * DV)
            q, k, v = q_ref[:, ks] * qscale, k_ref[:, ks], v_ref[:, vs].astype(BF16)
            g, E = g_all[:, ks], E_all[:, ks]
            st = st_ref[h]
            sp_ref[0, h, 0] = st
            A = jnp.where(blocks[0], _dot(q.astype(BF16), k.astype(BF16), NT), 0.0)
            for l in range(len(GLA_LEVELS)):
                e = jnp.exp(E[(2 + l) * C:(3 + l) * C])
                A = A + jnp.where(blocks[l + 1], _dot((q * e).astype(BF16), (k * e).astype(BF16), NT), 0.0)
            o_ref[0, :, vs] = (_dot((q * jnp.exp(E[0:C])).astype(BF16), st.astype(BF16), NT)
                               + _dot(A.astype(BF16), v, NN))
            decay = jnp.exp(jnp.sum(g, axis=0, keepdims=True))
            st_ref[h] = decay * st + _dot(v, (k * jnp.exp(E[C:2 * C])).astype(BF16), TN)

    chunk = functools.partial(_gla_chunk_of, lc=lc, nc=nc)
    return pl.pallas_call(
        body, name=name, grid=(2, nc),
        in_specs=[pl.BlockSpec((C, GK), lambda d, j: (chunk(d, j), qblk)),
                  pl.BlockSpec((C, GK), lambda d, j: (chunk(d, j), kblk)),
                  pl.BlockSpec((C, GV), lambda d, j: (chunk(d, j), vblk)),
                  pl.BlockSpec((C, LANES), lambda d, j: (chunk(d, j), lrblk)),
                  pl.BlockSpec((1, LANES, GK), lambda d, j: (d, 0, 0)),
                  pl.BlockSpec((1, 1, GK), lambda d, j: (d, 0, 0))],
        out_specs=[pl.BlockSpec((1, C, GV), lambda d, j: (d, chunk(d, j), 0)),
                   pl.BlockSpec((1, GLA_HEADS, 1, DV, DK), lambda d, j: (d, 0, j, 0, 0))],
        out_shape=[jax.ShapeDtypeStruct((2, R, GV), F32),
                   jax.ShapeDtypeStruct((2, GLA_HEADS, nc, DV, DK), F32)],
        scratch_shapes=[pltpu.VMEM((GLA_HEADS, DV, DK), F32)],
        compiler_params=_cparams("parallel", "arbitrary"),
    )(z, z, z, z, wg, bg)


def _gla_bwd(z, qblk, kblk, vblk, lrblk, wg, bg, sprev, do, n_ctx, name, ride=None):
    R = z.shape[0]
    C = GLA_CHUNK
    DK, DV = wg.shape[2] // GLA_HEADS, do.shape[1] // GLA_HEADS
    nc, lc = R // C, n_ctx // C
    qscale = DK ** -0.5
    nl = len(GLA_LEVELS)

    GK, GV = GLA_HEADS * DK, GLA_HEADS * DV

    def body(q_ref, k_ref, v_ref, lr_ref, wg_ref, bg_ref, sp_ref, do_ref,
             dq_ref, dk_ref, dv_ref, dpre_ref, dbg_ref, dst_ref):
        dirv, jr = pl.program_id(0), pl.program_id(1)

        @pl.when(jr == 0)
        def _():
            dst_ref[...] = jnp.zeros_like(dst_ref)
            dbg_ref[...] = jnp.zeros_like(dbg_ref)

        mall, blocks = _gla_masks(dirv)
        pre_all, g_all = _gla_gate(lr_ref, wg_ref, bg_ref)
        E_all = _dot3(mall, g_all, NN)
        for h in range(GLA_HEADS):
            ks, vs = slice(h * DK, (h + 1) * DK), slice(h * DV, (h + 1) * DV)
            q, k, v = q_ref[:, ks] * qscale, k_ref[:, ks], v_ref[:, vs].astype(BF16)
            pre, g, E = pre_all[:, ks], g_all[:, ks], E_all[:, ks]
            eb, er = jnp.exp(E[0:C]), jnp.exp(E[C:2 * C])
            decay = jnp.exp(jnp.sum(g, axis=0, keepdims=True))
            st = sp_ref[0, h, 0]
            dst = dst_ref[h]
            d_o = do_ref[:, vs]
            qe, kd = q * eb, k * er
            qb, kb = q.astype(BF16), k.astype(BF16)
            A = jnp.where(blocks[0], _dot(qb, kb, NT), 0.0)
            for l in range(nl):
                e = jnp.exp(E[(2 + l) * C:(3 + l) * C])
                A = A + jnp.where(blocks[l + 1], _dot((q * e).astype(BF16), (k * e).astype(BF16), NT), 0.0)
            dA = _dot(d_o, v, NT)
            dv_ref[0, :, vs] = _dot(A.astype(BF16), d_o, TN) + _dot(kd.astype(BF16), dst.astype(BF16), NT)
            dqe = _dot(d_o, st.astype(BF16), NN)
            dkd = _dot(v, dst.astype(BF16), NN)
            G = jnp.where(blocks[0], dA, 0.0).astype(BF16)
            dq = dqe * eb + _dot(G, kb, NN)
            dk = dkd * er + _dot(G, qb, TN)
            dE = [dqe * qe, dkd * kd]
            for l in range(nl):
                e = jnp.exp(E[(2 + l) * C:(3 + l) * C])
                ql, kl = q * e, k * e
                G = jnp.where(blocks[l + 1], dA, 0.0).astype(BF16)
                dql = _dot(G, kl.astype(BF16), NN)
                dkl = _dot(G, ql.astype(BF16), TN)
                dq = dq + dql * e
                dk = dk + dkl * e
                dE.append(dql * ql + dkl * kl)
            dlast = jnp.sum(dst * st, axis=0, keepdims=True) * decay
            dg = _dot3(mall, jnp.concatenate(dE, axis=0), TN) + dlast
            dpre = dg * (1.0 / GLA_GATE_NORM) / (1.0 + jnp.exp(pre))
            dq_ref[0, :, ks] = dq * qscale
            dk_ref[0, :, ks] = dk
            dpre_ref[:, ks] = dpre.astype(BF16)
            dbg_ref[0, :, ks] += jnp.sum(dpre, axis=0, keepdims=True)
            dst_ref[h] = decay * dst + _dot(d_o, qe.astype(BF16), TN)

    def chunk(d, jr):
        return _gla_chunk_of(d, nc - 1 - jr, lc, nc)

    return _pcall(
        body, name=name, grid=(2, nc),
        in_specs=[pl.BlockSpec((C, GK), lambda d, j: (chunk(d, j), qblk)),
                  pl.BlockSpec((C, GK), lambda d, j: (chunk(d, j), kblk)),
                  pl.BlockSpec((C, GV), lambda d, j: (chunk(d, j), vblk)),
                  pl.BlockSpec((C, LANES), lambda d, j: (chunk(d, j), lrblk)),
                  pl.BlockSpec((1, LANES, GK), lambda d, j: (d, 0, 0)),
                  pl.BlockSpec((1, 1, GK), lambda d, j: (d, 0, 0)),
                  pl.BlockSpec((1, GLA_HEADS, 1, DV, DK), lambda d, j: (d, 0, nc - 1 - j, 0, 0)),
                  pl.BlockSpec((C, GV), lambda d, j: (chunk(d, j), 0))],
        out_specs=[pl.BlockSpec((1, C, GK), lambda d, j: (d, chunk(d, j), 0)),
                   pl.BlockSpec((1, C, GK), lambda d, j: (d, chunk(d, j), 0)),
                   pl.BlockSpec((1, C, GV), lambda d, j: (d, chunk(d, j), 0)),
                   pl.BlockSpec((C, GK), lambda d, j: (chunk(d, j), d)),
                   pl.BlockSpec((1, 1, GK), lambda d, j: (d, 0, 0))],
        out_shape=[jax.ShapeDtypeStruct((2, R, GK), F32),
                   jax.ShapeDtypeStruct((2, R, GK), F32),
                   jax.ShapeDtypeStruct((2, R, GV), F32),
                   jax.ShapeDtypeStruct((R, 2 * GK), BF16),
                   jax.ShapeDtypeStruct((2, 1, GK), F32)],
        scratch_shapes=[pltpu.VMEM((GLA_HEADS, DV, DK), F32)],
        sem=("arbitrary", "arbitrary"), args=(z, z, z, z, wg, bg, sprev, do), ride=ride)


def _glanorm_fwd(o, z, rbblk, gn, n_ctx, name):
    _, R, GV = o.shape
    T = R - n_ctx
    DV = GV // GLA_HEADS
    tm = _pick(n_ctx, 256, 8)
    ro = n_ctx // tm

    def body(o0_ref, o1_ref, rb_ref, gn_ref, p_ref):
        gnv = gn_ref[...]
        for h in range(GLA_HEADS):
            sl = slice(h * DV, (h + 1) * DV)
            og = o0_ref[0, :, sl] + o1_ref[0, :, sl]
            r = lax.rsqrt(jnp.mean(og * og, axis=-1, keepdims=True) + EPS)
            p_ref[:, sl] = (og * r * gnv * _silu(rb_ref[:, sl])).astype(BF16)

    return pl.pallas_call(
        body, name=name, grid=(T // tm,),
        in_specs=[pl.BlockSpec((1, tm, GV), lambda i: (0, i + ro, 0)), pl.BlockSpec((1, tm, GV), lambda i: (1, i + ro, 0)),
                  pl.BlockSpec((tm, GV), lambda i: (i + ro, rbblk)), pl.BlockSpec((1, DV), lambda i: (0, 0))],
        out_specs=pl.BlockSpec((tm, GV), lambda i: (i, 0)),
        out_shape=jax.ShapeDtypeStruct((T, GV), BF16), compiler_params=_cparams("parallel"))(o, o, z, gn)


def _glanorm_bwd(o, z, rbblk, gn, dp, n_ctx, name):
    _, R, GV = o.shape
    T = R - n_ctx
    DV = GV // GLA_HEADS
    tm = _pick(n_ctx, 256, 8)
    ro = n_ctx // tm

    def body(o0_ref, o1_ref, rb_ref, gn_ref, dp_ref, do_ref, drb_ref, acc_ref):
        i = pl.program_id(0)

        @pl.when(i == 0)
        def _():
            acc_ref[...] = jnp.zeros_like(acc_ref)

        gnv = gn_ref[...]
        dgn = jnp.zeros((1, DV), F32)
        for h in range(GLA_HEADS):
            sl = slice(h * DV, (h + 1) * DV)
            og = o0_ref[0, :, sl] + o1_ref[0, :, sl]
            rb = rb_ref[:, sl]
            d = dp_ref[:, sl]
            r = lax.rsqrt(jnp.mean(og * og, axis=-1, keepdims=True) + EPS)
            xh = og * r
            drb_ref[:, sl] = (d * xh * gnv * _dsilu(rb)).astype(BF16)
            dn = d * _silu(rb)
            dgn = dgn + jnp.sum(dn * xh, axis=0, keepdims=True)
            dxh = dn * gnv
            do_ref[:, sl] = (r * (dxh - xh * jnp.mean(dxh * xh, axis=-1, keepdims=True))).astype(BF16)
        acc_ref[0:1, :] += dgn

    row = pl.BlockSpec((tm, GV), lambda i: (i, 0))
    return pl.pallas_call(
        body, name=name, grid=(T // tm,),
        in_specs=[pl.BlockSpec((1, tm, GV), lambda i: (0, i + ro, 0)), pl.BlockSpec((1, tm, GV), lambda i: (1, i + ro, 0)),
                  pl.BlockSpec((tm, GV), lambda i: (i + ro, rbblk)), pl.BlockSpec((1, DV), lambda i: (0, 0)), row],
        out_specs=[row, row, pl.BlockSpec((8, DV), lambda i: (0, 0))],
        out_shape=[jax.ShapeDtypeStruct((T, GV), BF16), jax.ShapeDtypeStruct((T, GV), BF16),
                   jax.ShapeDtypeStruct((8, DV), F32)],
        compiler_params=_cparams("arbitrary"))(o, o, z, gn, dp)


def _gate_fwd(z, gablk, gbblk, ya, yg, n_ctx, name):
    T, D = ya.shape
    tm = _pick(n_ctx, 256, 8)
    ro = n_ctx // tm

    def body(ga_ref, gb_ref, ya_ref, yg_ref, m_ref):
        m_ref[...] = (_sigmoid(ga_ref[...]) * ya_ref[...] + _sigmoid(gb_ref[...]) * yg_ref[...]).astype(BF16)

    row = pl.BlockSpec((tm, D), lambda i: (i, 0))
    return pl.pallas_call(
        body, name=name, grid=(T // tm,),
        in_specs=[pl.BlockSpec((tm, D), lambda i: (i + ro, gablk)), pl.BlockSpec((tm, D), lambda i: (i + ro, gbblk)), row, row],
        out_specs=row, out_shape=jax.ShapeDtypeStruct((T, D), BF16), compiler_params=_cparams("parallel"))(z, z, ya, yg)


def _gate_bwd(z, gablk, gbblk, ya, yg, dm, n_ctx, name):
    T, D = ya.shape
    tm = _pick(n_ctx, 256, 8)
    ro = n_ctx // tm

    def body(ga_ref, gb_ref, ya_ref, yg_ref, dm_ref, dya_ref, dyg_ref, dga_ref, dgb_ref):
        d = dm_ref[...]
        sa, sb = _sigmoid(ga_ref[...]), _sigmoid(gb_ref[...])
        dya_ref[...] = (d * sa).astype(BF16)
        dyg_ref[...] = (d * sb).astype(BF16)
        dga_ref[...] = (d * ya_ref[...] * sa * (1.0 - sa)).astype(BF16)
        dgb_ref[...] = (d * yg_ref[...] * sb * (1.0 - sb)).astype(BF16)

    row = pl.BlockSpec((tm, D), lambda i: (i, 0))
    sh = jax.ShapeDtypeStruct((T, D), BF16)
    return pl.pallas_call(
        body, name=name, grid=(T // tm,),
        in_specs=[pl.BlockSpec((tm, D), lambda i: (i + ro, gablk)), pl.BlockSpec((tm, D), lambda i: (i + ro, gbblk)), row, row, row],
        out_specs=[row] * 4, out_shape=[sh] * 4, compiler_params=_cparams("parallel"))(z, z, ya, yg, dm)


def _resnorm_fwd(x, mix, gt, g, sc, sh, name):
    T, D = x.shape
    tm = _pick(T, 256, 8)

    def body(x_ref, mix_ref, gt_ref, g_ref, sc_ref, sh_ref, x1_ref, h_ref):
        x1 = x_ref[...] + gt_ref[...] * mix_ref[...]
        x1_ref[...] = x1
        r = lax.rsqrt(jnp.mean(x1 * x1, axis=-1, keepdims=True) + EPS)
        h_ref[...] = (x1 * r * g_ref[...] * (1.0 + sc_ref[...]) + sh_ref[...]).astype(BF16)

    row = pl.BlockSpec((tm, D), lambda i: (i, 0))
    vec = pl.BlockSpec((1, D), lambda i: (0, 0))
    return pl.pallas_call(
        body, name=name, grid=(T // tm,), in_specs=[row, row, vec, vec, vec, vec], out_specs=[row, row],
        out_shape=[jax.ShapeDtypeStruct((T, D), F32), jax.ShapeDtypeStruct((T, D), BF16)],
        compiler_params=_cparams("parallel"))(x, mix, gt, g, sc, sh)


def _loss_head(d, x1, gt, target, name):
    T, D = d.shape
    tm = _pick(T, 256, 8)

    def body(d_ref, x1_ref, gt_ref, t_ref, dy_ref, dd_ref, acc_ref):
        i = pl.program_id(0)

        @pl.when(i == 0)
        def _():
            acc_ref[...] = jnp.zeros_like(acc_ref)

        dv, gtv = d_ref[...], gt_ref[...]
        e = x1_ref[...] + gtv * dv - t_ref[...]
        dy = e * (1.0 / D)
        dy_ref[...] = dy
        dd_ref[...] = (dy * gtv).astype(BF16)
        acc_ref[0:1, :] += jnp.sum(e * e, axis=0, keepdims=True)
        acc_ref[1:2, :] += jnp.sum(dy * dv, axis=0, keepdims=True)

    row = pl.BlockSpec((tm, D), lambda i: (i, 0))
    return pl.pallas_call(
        body, name=name, grid=(T // tm,), in_specs=[row, row, pl.BlockSpec((1, D), lambda i: (0, 0)), row],
        out_specs=[row, row, pl.BlockSpec((8, D), lambda i: (0, 0))],
        out_shape=[jax.ShapeDtypeStruct((T, D), F32), jax.ShapeDtypeStruct((T, D), BF16),
                   jax.ShapeDtypeStruct((8, D), F32)],
        compiler_params=_cparams("arbitrary"))(d, x1, gt, target)


def _halo_specs(T, tm, tw, col_of, order):
    n8 = tm // 8
    if order == "ij":
        mid = lambda i, j: (i, col_of(j))
        prev = lambda i, j: (jnp.maximum(i * n8 - 1, 0), col_of(j))
        nxt = lambda i, j: (jnp.minimum((i + 1) * n8, T // 8 - 1), col_of(j))
    else:
        mid = lambda j, i: (i, col_of(j))
        prev = lambda j, i: (jnp.maximum(i * n8 - 1, 0), col_of(j))
        nxt = lambda j, i: (jnp.minimum((i + 1) * n8, T // 8 - 1), col_of(j))
    return [pl.BlockSpec((tm, tw), mid), pl.BlockSpec((8, tw), prev), pl.BlockSpec((8, tw), nxt)]


def _shift_rows(x, before, after):
    tm = x.shape[0]
    row = lax.broadcasted_iota(jnp.int32, x.shape, 0)
    return (jnp.where(row == 0, before, pltpu.roll(x, 1, 0)),
            jnp.where(row == tm - 1, after, pltpu.roll(x, tm - 1, 0)))


def _conv_fwd(u, cw, cb, name):
    T, F2 = u.shape
    F = F2 // 2
    tm, tw = _pick(T, 256, 8), _pick(F, 512)
    nt, nw = T // tm, F // tw

    def body(ua, uap, uan, ug, ugp, ugn, cwa, cwg, cba, cbg, f_ref):
        i = pl.program_id(0)
        first, last = i == 0, i == nt - 1

        def conv(u_ref, up_ref, un_ref, w_ref, b_ref):
            m = u_ref[...]
            p, n = _shift_rows(m, jnp.where(first, 0.0, up_ref[7:8, :]), jnp.where(last, 0.0, un_ref[0:1, :]))
            return p * w_ref[0:1, :] + m * w_ref[1:2, :] + n * w_ref[2:3, :] + b_ref[...]

        a = conv(ua, uap, uan, cwa, cba)
        g = conv(ug, ugp, ugn, cwg, cbg)
        f_ref[...] = (_silu(a) * g).astype(BF16)

    wspec = lambda off: pl.BlockSpec((3, tw), lambda i, j: (0, j + off))
    bspec = lambda off: pl.BlockSpec((1, tw), lambda i, j: (0, j + off))
    return pl.pallas_call(
        body, name=name, grid=(nt, nw),
        in_specs=_halo_specs(T, tm, tw, lambda j: j, "ij") + _halo_specs(T, tm, tw, lambda j: j + nw, "ij")
        + [wspec(0), wspec(nw), bspec(0), bspec(nw)],
        out_specs=pl.BlockSpec((tm, tw), lambda i, j: (i, j)),
        out_shape=jax.ShapeDtypeStruct((T, F), BF16),
        compiler_params=_cparams("parallel", "parallel"),
    )(u, u, u, u, u, u, cw, cw, cb, cb)


def _conv_bwd(u, df, cw, cb, name):
    T, F2 = u.shape
    F = F2 // 2
    tm, tw = _pick(T, 256, 8), _pick(F, 512)
    nt, nw = T // tm, F // tw

    def body(ua, uap, uan, ug, ugp, ugn, cwa, cwg, cba, cbg, df_ref, dfp, dfn, dua_ref, dug_ref, acca_ref, accg_ref):
        i = pl.program_id(1)

        @pl.when(i == 0)
        def _():
            acca_ref[...] = jnp.zeros_like(acca_ref)
            accg_ref[...] = jnp.zeros_like(accg_ref)

        first, last = i == 0, i == nt - 1
        wa, wg, ba, bg = cwa[...], cwg[...], cba[...], cbg[...]

        def conv(p, m, n, w, b):
            return p * w[0:1] + m * w[1:2] + n * w[2:3] + b

        def grads(a, g, d):
            return d * g * _dsilu(a), d * _silu(a)

        xa, xg, d = ua[...], ug[...], df_ref[...]
        sa = _shift_rows(xa, jnp.where(first, 0.0, uap[7:8, :]), jnp.where(last, 0.0, uan[0:1, :]))
        sg = _shift_rows(xg, jnp.where(first, 0.0, ugp[7:8, :]), jnp.where(last, 0.0, ugn[0:1, :]))
        da, dg = grads(conv(sa[0], xa, sa[1], wa, ba), conv(sg[0], xg, sg[1], wg, bg), d)
        da_p, dg_p = grads(conv(uap[6:7, :], uap[7:8, :], xa[0:1], wa, ba),
                           conv(ugp[6:7, :], ugp[7:8, :], xg[0:1], wg, bg), dfp[7:8, :])
        da_n, dg_n = grads(conv(xa[tm - 1:tm], uan[0:1, :], uan[1:2, :], wa, ba),
                           conv(xg[tm - 1:tm], ugn[0:1, :], ugn[1:2, :], wg, bg), dfn[0:1, :])
        ta = _shift_rows(da, jnp.where(first, 0.0, da_p), jnp.where(last, 0.0, da_n))
        tg = _shift_rows(dg, jnp.where(first, 0.0, dg_p), jnp.where(last, 0.0, dg_n))
        dua_ref[...] = (ta[1] * wa[0:1] + da * wa[1:2] + ta[0] * wa[2:3]).astype(BF16)
        dug_ref[...] = (tg[1] * wg[0:1] + dg * wg[1:2] + tg[0] * wg[2:3]).astype(BF16)
        for t, (va, vg) in enumerate(((sa[0], sg[0]), (xa, xg), (sa[1], sg[1]))):
            acca_ref[t:t + 1, :] += jnp.sum(da * va, axis=0, keepdims=True)
            accg_ref[t:t + 1, :] += jnp.sum(dg * vg, axis=0, keepdims=True)
        acca_ref[3:4, :] += jnp.sum(da, axis=0, keepdims=True)
        accg_ref[3:4, :] += jnp.sum(dg, axis=0, keepdims=True)

    wspec = lambda off: pl.BlockSpec((3, tw), lambda j, i: (0, j + off))
    bspec = lambda off: pl.BlockSpec((1, tw), lambda j, i: (0, j + off))
    row = pl.BlockSpec((tm, tw), lambda j, i: (i, j))
    acc = pl.BlockSpec((8, tw), lambda j, i: (0, j))
    return pl.pallas_call(
        body, name=name, grid=(nw, nt),
        in_specs=_halo_specs(T, tm, tw, lambda j: j, "ji") + _halo_specs(T, tm, tw, lambda j: j + nw, "ji")
        + [wspec(0), wspec(nw), bspec(0), bspec(nw)] + _halo_specs(T, tm, tw, lambda j: j, "ji"),
        out_specs=[row, row, acc, acc],
        out_shape=[jax.ShapeDtypeStruct((T, F), BF16), jax.ShapeDtypeStruct((T, F), BF16),
                   jax.ShapeDtypeStruct((8, F), F32), jax.ShapeDtypeStruct((8, F), F32)],
        compiler_params=_cparams("parallel", "arbitrary"),
    )(u, u, u, u, u, u, cw, cw, cb, cb, df, df, df)


def _assemble_dz(lay, Z, n_ctx, dqa, drb, dga, dgb, dka, dva, dvg, dqg, dkg, dlr, name):
    T = dqa.shape[0]
    R = T + n_ctx
    tm = _pick(n_ctx, 128, 8)
    cb = n_ctx // tm

    def body(dqa_ref, drb_ref, dga_ref, dgb_ref, dka_ref, dva_ref, dvg0, dvg1, dqg0, dqg1, dkg0, dkg1, dlr_ref, o_ref):
        lat = pl.program_id(0) >= cb

        def put(seg, val):
            o_ref[:, lay[seg]:lay[seg] + val.shape[1]] = val.astype(BF16)

        def lat_only(ref):
            v = ref[...]
            return jnp.where(lat, v, jnp.zeros_like(v))

        put("qa", lat_only(dqa_ref))
        put("rb", lat_only(drb_ref))
        put("ga", lat_only(dga_ref))
        put("gb", lat_only(dgb_ref))
        put("ka", dka_ref[...])
        put("va", dva_ref[...])
        put("vb", dvg0[0] + dvg1[0])
        put("qb", dqg0[0] + dqg1[0])
        put("kb", dkg0[0] + dkg1[0])
        put("lr", dlr_ref[...])

    lat_spec = lambda a: pl.BlockSpec((tm, a.shape[1]), lambda i: (jnp.maximum(i - cb, 0), 0))
    all_spec = lambda a: pl.BlockSpec((tm, a.shape[1]), lambda i: (i, 0))
    dir_specs = lambda a: [pl.BlockSpec((1, tm, a.shape[2]), lambda i: (0, i, 0)),
                           pl.BlockSpec((1, tm, a.shape[2]), lambda i: (1, i, 0))]
    return pl.pallas_call(
        body, name=name, grid=(R // tm,),
        in_specs=[lat_spec(dqa), lat_spec(drb), lat_spec(dga), lat_spec(dgb), all_spec(dka), all_spec(dva)]
        + dir_specs(dvg) + dir_specs(dqg) + dir_specs(dkg) + [all_spec(dlr)],
        out_specs=pl.BlockSpec((tm, Z), lambda i: (i, 0)),
        out_shape=jax.ShapeDtypeStruct((R, Z), BF16), compiler_params=_cparams("parallel"),
    )(dqa, drb, dga, dgb, dka, dva, dvg, dvg, dqg, dqg, dkg, dkg, dlr)


def _mod_fwd(ca, w, b, name):
    n, D = ca.shape
    N = w.shape[1]
    tn = _pick(N, 512)

    def body(c_ref, w_ref, b_ref, o_ref, s_ref):
        s = _silu(c_ref[...])
        s_ref[...] = s
        o_ref[...] = _dot(s.astype(BF16), w_ref[...].astype(BF16), NN) + b_ref[...]

    return pl.pallas_call(
        body, name=name, grid=(N // tn,),
        in_specs=[pl.BlockSpec((n, D), lambda j: (0, 0)), pl.BlockSpec((D, tn), lambda j: (0, j)),
                  pl.BlockSpec((1, tn), lambda j: (0, j))],
        out_specs=[pl.BlockSpec((n, tn), lambda j: (0, j)), pl.BlockSpec((n, D), lambda j: (0, 0))],
        out_shape=[jax.ShapeDtypeStruct((n, N), F32), jax.ShapeDtypeStruct((n, D), F32)],
        compiler_params=_cparams("arbitrary"))(ca, w, b)


def _silu_bwd(dsil, ca, name):
    def body(d_ref, c_ref, o_ref):
        o_ref[...] = d_ref[...] * _dsilu(c_ref[...])

    return pl.pallas_call(body, name=name, out_shape=jax.ShapeDtypeStruct(ca.shape, F32))(dsil, ca)


def _adam_math(w, g, m, v):
    c1 = 1.0 - ADAM_B1 ** ADAM_STEP
    c2 = 1.0 - ADAM_B2 ** ADAM_STEP
    mn = ADAM_B1 * m + (1.0 - ADAM_B1) * g
    vn = ADAM_B2 * v + (1.0 - ADAM_B2) * (g * g)
    return -ADAM_LR * ((mn / c1) / (jnp.sqrt(vn / c2) + ADAM_EPS) + ADAM_WD * w), mn, vn


def _adamw(w, g, m, v, name):
    Rw, Cw = w.shape
    tr = _pick(Rw, 128, 8)

    def body(w_ref, g_ref, m_ref, v_ref, d_ref, mo_ref, vo_ref):
        d_ref[...], mo_ref[...], vo_ref[...] = _adam_math(w_ref[...], g_ref[...], m_ref[...], v_ref[...])

    row = pl.BlockSpec((tr, Cw), lambda i: (i, 0))
    sh = jax.ShapeDtypeStruct((Rw, Cw), F32)
    return pl.pallas_call(body, name=name, grid=(Rw // tr,), in_specs=[row] * 4, out_specs=[row] * 3,
                          out_shape=[sh] * 3, compiler_params=_cparams("parallel"))(w, g, m, v)


HBM_SPEC = pl.BlockSpec(memory_space=pltpu.HBM)


def _exchange(inputs, out_shapes, stages, name):
    n_in, n_out = len(inputs), len(out_shapes)
    n = sum(len(s) for s in stages)

    def body(*refs):
        ins, outs = refs[:n_in], refs[n_in:n_in + n_out]
        send_sems, recv_sems = refs[n_in + n_out:]
        k = 0
        for stage in stages:
            copies = _stage_copies(stage, ins, outs, send_sems, recv_sems, k)
            for cp in copies:
                cp.start()
            for cp in copies:
                cp.wait()
            k += len(stage)

    return pl.pallas_call(
        body, name=name, in_specs=[HBM_SPEC] * n_in, out_specs=[HBM_SPEC] * n_out, out_shape=out_shapes,
        scratch_shapes=[pltpu.SemaphoreType.DMA((n,)), pltpu.SemaphoreType.DMA((n,))],
    )(*inputs)


def _stage_copies(stage, ins, outs, send_sems, recv_sems, k0=0):
    me = (lax.axis_index("x"), lax.axis_index("y"), lax.axis_index("c"))
    copies = []
    for k, ((skind, sidx), sfn, didx, dfn, flip) in enumerate(stage):
        src = (ins if skind == "in" else outs)[sidx].at[sfn(*me)]
        dst = outs[didx].at[dfn(*me)]
        if flip == (0, 0, 0):
            copies.append(pltpu.make_async_copy(src, dst, send_sems.at[k0 + k]))
        else:
            peer = tuple(1 - a if f else a for a, f in zip(me, flip))
            copies.append(pltpu.make_async_remote_copy(src, dst, send_sems.at[k0 + k], recv_sems.at[k0 + k],
                                                       device_id=peer, device_id_type=MESH))
    return copies


def _pcall(body, *, name, grid, in_specs, out_specs, out_shape, scratch_shapes=(), sem, args, ride=None):
    many = isinstance(out_shape, (list, tuple))
    out_specs, out_shape = (list(out_specs), list(out_shape)) if many else ([out_specs], [out_shape])
    if ride is None:
        res = pl.pallas_call(body, name=name, grid=grid, in_specs=list(in_specs), out_specs=out_specs,
                             out_shape=out_shape, scratch_shapes=list(scratch_shapes),
                             compiler_params=_cparams(*sem))(*args)
        return res if many else res[0]
    x_in, x_out, stage, aliases = ride
    n_in, n_out, n_scr, n_xin, n_xout = len(in_specs), len(out_specs), len(scratch_shapes), len(x_in), len(x_out)

    def wrapped(*refs):
        ins, xins = refs[:n_in], refs[n_in:n_in + n_xin]
        o0 = n_in + n_xin
        outs, xouts = refs[o0:o0 + n_out], refs[o0 + n_out:o0 + n_out + n_xout]
        s0 = o0 + n_out + n_xout
        scr, (send_sems, recv_sems) = refs[s0:s0 + n_scr], refs[s0 + n_scr:]
        first = functools.reduce(jnp.logical_and, [pl.program_id(d) == 0 for d in range(len(grid))])
        last = functools.reduce(jnp.logical_and, [pl.program_id(d) == grid[d] - 1 for d in range(len(grid))])

        @pl.when(first)
        def _():
            for cp in _stage_copies(stage, xins, xouts, send_sems, recv_sems):
                cp.start()

        body(*ins, *outs, *scr)

        @pl.when(last)
        def _():
            for cp in _stage_copies(stage, xins, xouts, send_sems, recv_sems):
                cp.wait()

    res = pl.pallas_call(
        wrapped, name=name, grid=grid, in_specs=list(in_specs) + [HBM_SPEC] * n_xin,
        out_specs=out_specs + [HBM_SPEC] * n_xout, out_shape=out_shape + list(x_out),
        scratch_shapes=list(scratch_shapes) + [pltpu.SemaphoreType.DMA((len(stage),)),
                                               pltpu.SemaphoreType.DMA((len(stage),))],
        input_output_aliases={n_in + a: n_out + b for a, b in aliases.items()},
        compiler_params=_cparams(*(["arbitrary"] * len(grid))))(*args, *x_in)
    main = res[:n_out]
    return (main if many else main[0]), list(res[n_out:])


FLIPS_ALL = [(0, 0, 1), (0, 1, 0), (0, 1, 1), (1, 0, 0), (1, 0, 1), (1, 1, 0), (1, 1, 1)]
FLIPS_CHIP = [(0, 1, 0), (1, 0, 0), (1, 1, 0)]


def _sum_slots(buf, name):
    n, r, w = buf.shape
    tr = _pick(r, 256, 8)

    def body(b_ref, o_ref):
        acc = b_ref[0]
        for s in range(1, n):
            acc = acc + b_ref[s]
        o_ref[...] = acc

    return pl.pallas_call(
        body, name=name, grid=(r // tr,), in_specs=[pl.BlockSpec((n, tr, w), lambda i: (0, i, 0))],
        out_specs=pl.BlockSpec((tr, w), lambda i: (i, 0)), out_shape=jax.ShapeDtypeStruct((r, w), F32),
        compiler_params=_cparams("parallel"))(buf)


def _allreduce(buf, name):
    r, w = buf.shape
    whole = lambda x, y, c: (slice(None), slice(None))
    slot = lambda x, y, c: (4 * x + 2 * y + c,)
    stage = [(("in", 0), whole, 0, slot, f) for f in [(0, 0, 0)] + FLIPS_ALL]
    (slots,) = _exchange([buf], [jax.ShapeDtypeStruct((8, r, w), F32)], [stage], name + "_x")
    return _sum_slots(slots, name + "_sum")


def _gather_plan(shards, src):
    half = lambda a, c: pl.ds(c * (a.shape[0] // 2), a.shape[0] // 2)
    first, second = [], []
    for n, a in enumerate(shards):
        for f in FLIPS_CHIP:
            first.append((("in", n), lambda x, y, c, a=a: (half(a, c), slice(None)), n,
                          lambda x, y, c, a=a: (2 * x + y, half(a, c), slice(None)), f))
            peer_slot = lambda x, y, c, a=a, f=f: (2 * (x ^ f[0]) + (y ^ f[1]), half(a, c), slice(None))
            second.append(((src, n), peer_slot, n, peer_slot, (0, 0, 1)))
    outs = [jax.ShapeDtypeStruct((4,) + a.shape, a.dtype) for a in shards]
    return first, second, outs


def _allgather_weights(shards, name):
    first, second, outs = _gather_plan(shards, "out")
    return _exchange(shards, outs, [first, second], name)


def _place_own(buf, shard, svec, name):
    _, Rs, Cs = buf.shape
    tr = _pick(Rs, 256, 16)

    def body(s_ref, buf_ref, sh_ref, o_ref):
        o_ref[0] = sh_ref[...]

    grid_spec = pltpu.PrefetchScalarGridSpec(
        num_scalar_prefetch=1, grid=(Rs // tr,),
        in_specs=[pl.BlockSpec(memory_space=pl.ANY), pl.BlockSpec((tr, Cs), lambda i, s: (i, 0))],
        out_specs=pl.BlockSpec((1, tr, Cs), lambda i, s: (s[0], i, 0)))
    return pl.pallas_call(body, name=name, grid_spec=grid_spec, out_shape=jax.ShapeDtypeStruct(buf.shape, buf.dtype),
                          input_output_aliases={1: 0}, compiler_params=_cparams("arbitrary"))(svec, buf, shard)


def _add_pair(G, bufA, cvec, name):
    _, Rs, Cs = G.shape
    Rh = Rs // 2
    tr = _pick(Rh, 128, 16)
    nb = Rh // tr

    def body(c_ref, g_ref, a_ref, o_ref):
        o_ref[...] = (g_ref[...] + a_ref[...]).astype(BF16)

    grid_spec = pltpu.PrefetchScalarGridSpec(
        num_scalar_prefetch=1, grid=(4, nb),
        in_specs=[pl.BlockSpec((1, tr, Cs), lambda s, i, c_ref: (s, c_ref[0] * nb + i, 0)),
                  pl.BlockSpec((1, tr, Cs), lambda s, i, c_ref: (s, i, 0))],
        out_specs=pl.BlockSpec((1, tr, Cs), lambda s, i, c_ref: (s, i, 0)))
    return pl.pallas_call(body, name=name, grid_spec=grid_spec, out_shape=jax.ShapeDtypeStruct((4, Rh, Cs), BF16),
                          compiler_params=_cparams("parallel", "parallel"))(cvec, G, bufA)


def _sum_chips(G, bufA, bufB, cvec, svec, name):
    _, Rs, Cs = G.shape
    Rh = Rs // 2
    tr = _pick(Rh, 128, 16)
    nb = Rh // tr

    def body(c_ref, s_ref, g_ref, a_ref, b_ref, o_ref):
        o_ref[...] = (g_ref[0] + a_ref[0]) + b_ref[0].astype(F32) + b_ref[1].astype(F32) + b_ref[2].astype(F32)

    grid_spec = pltpu.PrefetchScalarGridSpec(
        num_scalar_prefetch=2, grid=(nb,),
        in_specs=[pl.BlockSpec((1, tr, Cs), lambda i, c, s: (s[0], c[0] * nb + i, 0)),
                  pl.BlockSpec((1, tr, Cs), lambda i, c, s: (s[0], i, 0)),
                  pl.BlockSpec((3, tr, Cs), lambda i, c, s: (0, i, 0))],
        out_specs=pl.BlockSpec((tr, Cs), lambda i, c, s: (i, 0)))
    return pl.pallas_call(body, name=name, grid_spec=grid_spec, out_shape=jax.ShapeDtypeStruct((Rh, Cs), F32),
                          compiler_params=_cparams("parallel"))(cvec, svec, G, bufA, bufB)


def _pair_plan(grads):
    Rh = [g.shape[1] // 2 for g in grads]
    whole3 = lambda x, y, c: (slice(None), slice(None), slice(None))
    stage = [(("in", n), lambda x, y, c, n=n: (slice(None), pl.ds((1 - c) * Rh[n], Rh[n]), slice(None)), n,
              whole3, (0, 0, 1)) for n in range(len(grads))]
    return [jax.ShapeDtypeStruct((4, Rh[n], g.shape[2]), F32) for n, g in enumerate(grads)], stage


def _chips_plan(P):
    stage = [(("in", n), lambda x, y, c, f=f: (2 * (x ^ f[0]) + (y ^ f[1]),), n, lambda x, y, c, k=k: (k,), f)
             for n in range(len(P)) for k, f in enumerate(FLIPS_CHIP)]
    return [jax.ShapeDtypeStruct((3,) + p.shape[1:], BF16) for p in P], stage


def _halves_plan(mine):
    whole2 = lambda x, y, c: (slice(None), slice(None))
    stage = [(("in", n), whole2, n, whole2, (0, 0, 1)) for n in range(len(mine))]
    return [jax.ShapeDtypeStruct(r.shape, F32) for r in mine], stage


def _adamw_halves(w, mine, other, m, v, cvec, name):
    Rs, Cs = w.shape
    Rh = Rs // 2
    tr = _pick(Rh, 128, 8)
    nb = Rh // tr

    def body(c_ref, w_ref, a_ref, b_ref, m_ref, v_ref, g_ref, d_ref, mo_ref, vo_ref):
        gv = jnp.where(pl.program_id(0) // nb == c_ref[0], a_ref[...], b_ref[...])
        g_ref[...] = gv
        d_ref[...], mo_ref[...], vo_ref[...] = _adam_math(w_ref[...], gv, m_ref[...], v_ref[...])

    row = pl.BlockSpec((tr, Cs), lambda i, c: (i, 0))
    hrow = pl.BlockSpec((tr, Cs), lambda i, c: (i % nb, 0))
    grid_spec = pltpu.PrefetchScalarGridSpec(num_scalar_prefetch=1, grid=(2 * nb,),
                                             in_specs=[row, hrow, hrow, row, row], out_specs=[row] * 4)
    return pl.pallas_call(body, name=name, grid_spec=grid_spec, out_shape=[jax.ShapeDtypeStruct((Rs, Cs), F32)] * 4,
                          compiler_params=_cparams("parallel"))(cvec, w, mine, other, m, v)


def _pack(arrays):
    flat = [a.reshape(-1).astype(F32) for a in arrays]
    meta, off = [], 0
    for a, f in zip(arrays, flat):
        meta.append((off, a.shape))
        off += f.shape[0]
    total = -(-off // (8 * LANES)) * (8 * LANES)
    flat.append(jnp.zeros((total - off,), F32))
    return jnp.concatenate(flat).reshape(total // LANES, LANES), meta


def _unpack(buf, meta):
    flat = buf.reshape(-1)
    out = []
    for off, shape in meta:
        size = 1
        for s in shape:
            size *= s
        out.append(flat[off:off + size].reshape(shape))
    return out


WEIGHT_NAMES = ["c_ctx", "w_mod", "b_mod", "g_mix", "w_in", "q_norm", "k_norm", "attn_sink", "w_gate_f", "b_gate_f",
                "w_gate_b", "b_gate_b", "gla_norm", "w_attn_o", "w_gla_o", "w_out", "g_ffn", "w_up", "conv_w",
                "conv_b", "w_down"]
BIG_NAMES = ["w_in", "w_attn_o", "w_gla_o", "w_out", "w_up", "w_down"]
SHARDED_SMALL = ["w_gate_f", "w_gate_b", "conv_w"]


def _layouts(D):
    aw, kvw, gk, gv = N_Q_HEADS * HEAD_DIM, N_KV_HEADS * HEAD_DIM, D // 2, D
    widths = {"qa": aw, "ka": kvw, "va": kvw, "qb": gk, "kb": gk, "vb": gv, "rb": gv, "lr": 2 * GLA_LOWRANK,
              "ga": D, "gb": D}
    orig, off = {}, 0
    for s in ["qa", "ka", "va", "qb", "kb", "vb", "rb", "lr", "ga", "gb"]:
        orig[s] = off
        off += widths[s]
    order = ["qa", "vb", "rb", "ga", "gb", "ka", "va", "qb", "kb", "lr"]
    lay, off = {}, 0
    for s in order:
        lay[s] = off
        off += LANES if s == "lr" else widths[s]
    align = {"qa": aw, "vb": D, "rb": D, "ga": D, "gb": D, "ka": kvw, "va": kvw, "qb": gk, "kb": gk,
             "lr": LANES}
    for s in order:
        assert lay[s] % align[s] == 0, (s, lay[s], align[s])
    return widths, orig, order, lay, off


def _rope_tables(T, L):
    t = jnp.arange(T)
    nf = HEAD_DIM // 4
    inv = ROPE_THETA ** (-jnp.arange(nf, dtype=F32) / nf)
    ang = jnp.concatenate([(t // GRID_W)[:, None] * inv, (t % GRID_W)[:, None] * inv], axis=-1)
    cos, sin = jnp.cos(ang), jnp.sin(ang)
    cos2 = jnp.concatenate([jnp.ones((L, HEAD_DIM), F32), jnp.concatenate([cos, cos], axis=-1)], axis=0)
    sin2 = jnp.concatenate([jnp.zeros((L, HEAD_DIM), F32), jnp.concatenate([-sin, sin], axis=-1)], axis=0)
    return cos2, sin2


def _step(x, c, ctx, loss_target, W, M, V):
    xi, yi, ci = lax.axis_index("x"), lax.axis_index("y"), lax.axis_index("c")
    chip = 2 * xi + yi
    dev = 2 * chip + ci
    south = (ci == 0).astype(F32)
    cvec = ci.reshape(1).astype(jnp.int32)
    svec = chip.reshape(1).astype(jnp.int32)
    T, D = x.shape[1], x.shape[2]
    L = ctx.shape[1]
    R = L + T
    F = 4 * W["w_down"].shape[1]
    GK, GV = D // 2, D
    DK, DV = GK // GLA_HEADS, GV // GLA_HEADS
    N6 = 6 * D
    N4 = N6 // 4
    widths, orig, order, lay, Z = _layouts(D)

    def place_cols(shard, full_cols):
        cols = shard.shape[-1]
        full = jnp.zeros(shard.shape[:-1] + (full_cols,), F32)
        return lax.dynamic_update_slice(full, shard * south, (0,) * (shard.ndim - 1) + (chip * cols,))

    c_rows = lax.dynamic_update_slice(jnp.zeros((8, D), F32), c, (dev, 0))
    bufa, meta = _pack([c_rows, place_cols(W["w_gate_f"][0], GK), place_cols(W["w_gate_b"][0], GK),
                        place_cols(W["conv_w"][0], 2 * F)])
    c_all, wgf, wgb, cw = _unpack(_allreduce(bufa, "gather_small"), meta)
    ca = jnp.concatenate([c_all, W["c_ctx"][None, :], jnp.zeros((7, D), F32)], axis=0)
    b_shard = lax.dynamic_slice(W["b_mod"], (0, chip * N4), (1, N4))
    mod_part, sil = _mod_fwd(ca, W["w_mod"][0], b_shard, "mod_fwd")
    slots = lax.dynamic_update_slice(jnp.zeros((4, 16, N4), F32), (mod_part * south)[None], (chip, 0, 0))
    mod_all = _allreduce(slots.reshape(64, N4), "gather_mod").reshape(4, 16, N4).transpose(1, 0, 2).reshape(16, N6)
    mx = lax.dynamic_slice(mod_all, (dev, 0), (1, N6)).reshape(6, 1, D)
    mc = mod_all[8].reshape(6, 1, D)

    sq = lambda a: a.reshape(a.shape[1:])
    shards = [sq(W[n]).astype(BF16) for n in BIG_NAMES]
    own = lambda g, s, n: _place_own(g, s, svec, "place_" + n)
    cols = lambda g: g.transpose(1, 0, 2).reshape(g.shape[1], 4 * g.shape[2])
    rows = lambda g: g.reshape(4 * g.shape[1], g.shape[2])
    w_in_f = cols(own(_allgather_weights(shards[:1], "gather_w_in")[0], shards[0], "w_in"))
    seg = lambda s: w_in_f[:, orig[s]:orig[s] + widths[s]]
    w_cat = jnp.concatenate([jnp.pad(seg(s), ((0, 0), (0, LANES - widths[s]))) if s == "lr" else seg(s)
                             for s in order], axis=1)
    gather1, gather2, gather_outs = _gather_plan(shards[1:], "in")
    wg = jnp.zeros((2, LANES, GK), F32).at[0, :GLA_LOWRANK].set(wgf).at[1, GLA_LOWRANK:2 * GLA_LOWRANK].set(wgb)
    bg = jnp.stack([W["b_gate_f"], W["b_gate_b"]])
    cb = W["conv_b"]
    sink_rows = jnp.broadcast_to(W["attn_sink"][0][:, None], (N_Q_HEADS, HEAD_DIM))
    cos2, sin2 = _rope_tables(T, L)
    blk = lambda s, w: lay[s] // w

    xall = jnp.concatenate([ctx[0], x[0]], axis=0)
    sc1 = jnp.stack([mc[1], mx[1]])
    sh1 = jnp.stack([mc[0], mx[0]])
    h = _modnorm_fwd(xall, W["g_mix"], sc1, sh1, L, "modnorm1")
    z, landed = _matmul(h, w_cat, "nn", F32, "proj_in", ride=(shards[1:], gather_outs, gather1, {}))
    qn = _qknorm_fwd(z, blk("qa", widths["qa"]), T, L, W["q_norm"], cos2, sin2, N_Q_HEADS, "qnorm")
    kn = _qknorm_fwd(z, blk("ka", widths["ka"]), R, 0, W["k_norm"], cos2, sin2, N_KV_HEADS, "knorm")
    vb = _cast_seg(z, blk("va", widths["va"]), widths["va"], "vcast")
    o_attn, landed = _attn_fwd(qn, kn, vb, sink_rows, L, "attn_fwd",
                               ride=(landed, gather_outs, gather2, {n: n for n in range(len(landed))}))
    g_ao, g_go, g_out, g_up, g_dn = [own(g, s, n) for g, s, n in zip(landed, shards[1:], BIG_NAMES[1:])]
    w_ao, w_go, w_out, w_up, w_dn = rows(g_ao), rows(g_go), rows(g_out), cols(g_up), rows(g_dn)
    gla_blks = (blk("qb", GK), blk("kb", GK), blk("vb", GV), blk("lr", LANES))
    o_g, sprev = _gla_fwd(z, *gla_blks, wg, bg, DV, L, "gla_fwd")
    p = _glanorm_fwd(o_g, z, blk("rb", D), W["gla_norm"], L, "glanorm")
    ya = _matmul(o_attn, w_ao, "nn", F32, "proj_attn_o")
    yg = _matmul(p, w_go, "nn", F32, "proj_gla_o")
    m = _gate_fwd(z, blk("ga", D), blk("gb", D), ya, yg, L, "gate")
    mix = _matmul(m, w_out, "nn", F32, "proj_out")
    x1, h2 = _resnorm_fwd(x[0], mix, mx[2], W["g_ffn"], mx[4], mx[3], "resnorm2")
    u = _matmul(h2, w_up, "nn", F32, "ffn_up")
    f = _conv_fwd(u, cw, cb, "conv_swiglu")
    d = _matmul(f, w_dn, "nn", F32, "ffn_down")
    dy, dd, lacc = _loss_head(d, x1, mx[5], loss_target[0], "loss_head")
    loss = lax.psum((0.5 / D) * jnp.sum(lacc[0]), ("x", "y", "c"))

    gw_dn = _matmul(f, dd, "tn", F32, "ffn_down_dw")
    df = _matmul(dd, w_dn, "nt", F32, "ffn_down_dx")
    du_a, du_g, acca, accg = _conv_bwd(u, df, cw, cb, "conv_swiglu_bwd")
    du = jnp.concatenate([du_a, du_g], axis=1)
    gw_up = _matmul(h2, du, "tn", F32, "ffn_up_dw")
    dh2 = _matmul(du, w_up, "nt", F32, "ffn_up_dx")
    dx1, dmix, s2 = _resnorm_bwd(x1, dh2, W["g_ffn"], mx[4], dy, mix, mx[2], "resnorm2_bwd")
    gw_out = _matmul(m, dmix, "tn", F32, "proj_out_dw")
    dm = _matmul(dmix, w_out, "nt", F32, "proj_out_dx")
    dya, dyg, dga, dgb = _gate_bwd(z, blk("ga", D), blk("gb", D), ya, yg, dm, L, "gate_bwd")
    gw_ao = _matmul(o_attn, dya, "tn", F32, "proj_attn_o_dw")
    do_attn = _matmul(dya, w_ao, "nt", BF16, "proj_attn_o_dx")
    gw_go = _matmul(p, dyg, "tn", F32, "proj_gla_o_dw")
    dp = _matmul(dyg, w_go, "nt", F32, "proj_gla_o_dx")
    do_gla, drb, s_gn = _glanorm_bwd(o_g, z, blk("rb", D), W["gla_norm"], dp, L, "glanorm_bwd")
    do_pad = jnp.concatenate([jnp.zeros((L, GV), BF16), do_gla], axis=0)
    by_cols = lambda g: g.reshape(g.shape[0], 4, g.shape[1] // 4).transpose(1, 0, 2)
    by_rows = lambda g: g.reshape(4, g.shape[0] // 4, g.shape[1])
    early = [by_rows(gw_ao), by_rows(gw_go), by_rows(gw_out), by_cols(gw_up), by_rows(gw_dn)]
    (dqg, dkg, dvg, dpre, dbg), pair_e = _gla_bwd(z, *gla_blks, wg, bg, sprev, do_pad, L, "gla_bwd",
                                                  ride=(early, *_pair_plan(early), {}))
    sums_e = [_add_pair(g, a, cvec, "reduce_early_add%d" % n) for n, (g, a) in enumerate(zip(early, pair_e))]
    wg_cat = jnp.concatenate([wg[0], wg[1]], axis=1)
    dlr = _matmul(dpre, wg_cat, "nt", BF16, "gla_gate_dx")
    dwg = _matmul(z[:, lay["lr"]:lay["lr"] + LANES], dpre, "tn", F32, "gla_gate_dw")
    (dqn, dkw, dvw, dkc, dvc, dsn), chips_e = _attn_bwd(qn, kn, vb, sink_rows, do_attn, L, "attn_bwd",
                                                        ride=(sums_e, *_chips_plan(sums_e), {}))
    mine_e = [_sum_chips(g, a, b, cvec, svec, "reduce_early_sum%d" % n)
              for n, (g, a, b) in enumerate(zip(early, pair_e, chips_e))]
    dqa, s_qn = _qknorm_bwd(z, blk("qa", widths["qa"]), T, L, W["q_norm"], cos2, sin2, dqn, N_Q_HEADS, "qnorm_bwd")
    dk_all = jnp.concatenate([dkc, dkw[WINDOW:WINDOW + T]], axis=0)
    dv_all = jnp.concatenate([dvc, dvw[WINDOW:WINDOW + T]], axis=0)
    dka, s_kn = _qknorm_bwd(z, blk("ka", widths["ka"]), R, 0, W["k_norm"], cos2, sin2, dk_all, N_KV_HEADS, "knorm_bwd")
    dz = _assemble_dz(lay, Z, L, dqa, drb, dga, dgb, dka, dv_all, dvg, dqg, dkg, dlr, "assemble_dz")
    gw_cat, other_e = _matmul(h, dz, "tn", F32, "proj_in_dw", ride=(mine_e, *_halves_plan(mine_e), {}))
    gw_in = jnp.concatenate([gw_cat[:, lay[s]:lay[s] + widths[s]] for s in ["qa", "ka", "va", "qb", "kb", "vb", "rb",
                                                                           "lr", "ga", "gb"]], axis=1)
    late = [by_cols(gw_in)]
    shapes, stage = _pair_plan(late)
    pair_l = _exchange(late, shapes, [stage], "reduce_late_pair")
    sums_l = [_add_pair(late[0], pair_l[0], cvec, "reduce_late_add")]
    dh, chips_l = _matmul(dz, w_cat, "nt", F32, "proj_in_dx", ride=(sums_l, *_chips_plan(sums_l), {}))
    mine_l = [_sum_chips(late[0], pair_l[0], chips_l[0], cvec, svec, "reduce_late_sum")]
    shapes, stage = _halves_plan(mine_l)
    other_l = _exchange(mine_l, shapes, [stage], "reduce_late_halves")
    mine, other = mine_l + mine_e, list(other_l) + other_e
    grad_x, s1 = _modnorm_bwd(x[0], dh, W["g_mix"], mx[1], dx1, "modnorm1_bwd", dh_roff=L)
    _, s1c = _modnorm_bwd(ctx[0], dh, W["g_mix"], mc[1], None, "modnorm1_ctx_bwd")

    dmod_x = jnp.concatenate([s1[0], s1[1], s2[3], s2[0], s2[1], lacc[1]])
    dmod_c = jnp.concatenate([s1c[0], s1c[1], jnp.zeros((4 * D,), F32)])
    dmod_rows = lax.dynamic_update_slice(jnp.zeros((9, N6), F32).at[8].set(dmod_c), dmod_x[None], (dev, 0))
    small = [dmod_rows, dmod_x + dmod_c, s1[2] + s1c[2], s_qn[0], s_kn[0], dsn[:, 0, :Q_PER_KV].reshape(N_Q_HEADS),
             dwg[:GLA_LOWRANK, :GK], dbg[0].reshape(GK), dwg[GLA_LOWRANK:2 * GLA_LOWRANK, GK:], dbg[1].reshape(GK),
             s_gn[0], s2[2], jnp.concatenate([acca[0:3], accg[0:3]], axis=1), jnp.concatenate([acca[3], accg[3]])]
    bufc, meta = _pack(small)
    (dmod_sum, g_b_mod, g_g_mix, g_q_norm, g_k_norm, g_sink, g_wgf, g_bgf, g_wgb, g_bgb, g_gla_norm, g_g_ffn,
     g_conv_w, g_conv_b) = _unpack(_allreduce(bufc, "reduce_small"), meta)
    dmod16 = lax.dynamic_slice(jnp.concatenate([dmod_sum, jnp.zeros((7, N6), F32)], axis=0), (0, chip * N4), (16, N4))
    g_w_mod = _matmul(sil, dmod16, "tn", F32, "mod_dw")
    dsil = _matmul(dmod16, W["w_mod"][0], "nt", F32, "mod_dx")
    g_c_ctx = _silu_bwd(_allreduce(dsil * south, "reduce_cctx"), ca, "silu_bwd")[8]

    cut = lambda g: lax.dynamic_slice(g, (0, chip * (g.shape[1] // 4)), (g.shape[0], g.shape[1] // 4))
    grads = {"c_ctx": g_c_ctx, "w_mod": g_w_mod[None], "b_mod": g_b_mod[None], "g_mix": g_g_mix[None],
             "q_norm": g_q_norm[None], "k_norm": g_k_norm[None], "attn_sink": g_sink[None],
             "w_gate_f": cut(g_wgf)[None], "b_gate_f": g_bgf[None], "w_gate_b": cut(g_wgb)[None],
             "b_gate_b": g_bgb[None], "gla_norm": g_gla_norm[None], "g_ffn": g_g_ffn[None],
             "conv_w": cut(g_conv_w)[None], "conv_b": g_conv_b[None]}

    delta, new_m, new_v = {}, {}, {}
    dl, mn, vn = _adamw(W["w_mod"][0], g_w_mod, M["w_mod"][0], V["w_mod"][0], "adamw_w_mod")
    delta["w_mod"], new_m["w_mod"], new_v["w_mod"] = dl[None], mn[None], vn[None]
    for n, a, b in zip(BIG_NAMES, mine, other):
        g, dl, mn, vn = _adamw_halves(sq(W[n]), a, b, sq(M[n]), sq(V[n]), cvec, "adamw_" + n)
        grads[n], delta[n], new_m[n], new_v[n] = g[None], dl[None], mn[None], vn[None]
    small_names = [n for n in WEIGHT_NAMES if n not in delta]
    packs = [_pack([src[n] for n in small_names]) for src in (W, grads, M, V)]
    meta = packs[0][1]
    outs = _adamw(packs[0][0], packs[1][0], packs[2][0], packs[3][0], "adamw_small")
    for res, o in zip((delta, new_m, new_v), outs):
        for n, a in zip(small_names, _unpack(o, meta)):
            res[n] = a
    return (loss, grad_x[None], *[grads[n] for n in WEIGHT_NAMES], *[delta[n] for n in WEIGHT_NAMES],
            *[new_m[n] for n in WEIGHT_NAMES], *[new_v[n] for n in WEIGHT_NAMES])


def kernel(x, c, ctx, c_ctx, w_mod, b_mod, g_mix, w_in, q_norm, k_norm, attn_sink, w_gate_f, b_gate_f, w_gate_b, b_gate_b, gla_norm, w_attn_o, w_gla_o, w_out, g_ffn, w_up, conv_w, conv_b, w_down, loss_target, m_c_ctx, m_w_mod, m_b_mod, m_g_mix, m_w_in, m_q_norm, m_k_norm, m_attn_sink, m_w_gate_f, m_b_gate_f, m_w_gate_b, m_b_gate_b, m_gla_norm, m_w_attn_o, m_w_gla_o, m_w_out, m_g_ffn, m_w_up, m_conv_w, m_conv_b, m_w_down, v_c_ctx, v_w_mod, v_b_mod, v_g_mix, v_w_in, v_q_norm, v_k_norm, v_attn_sink, v_w_gate_f, v_b_gate_f, v_w_gate_b, v_b_gate_b, v_gla_norm, v_w_attn_o, v_w_gla_o, v_w_out, v_g_ffn, v_w_up, v_conv_w, v_conv_b, v_w_down):
    W = dict(zip(WEIGHT_NAMES, (c_ctx, w_mod, b_mod, g_mix, w_in, q_norm, k_norm, attn_sink, w_gate_f, b_gate_f,
                                w_gate_b, b_gate_b, gla_norm, w_attn_o, w_gla_o, w_out, g_ffn, w_up, conv_w, conv_b,
                                w_down)))
    M = dict(zip(WEIGHT_NAMES, (m_c_ctx, m_w_mod, m_b_mod, m_g_mix, m_w_in, m_q_norm, m_k_norm, m_attn_sink,
                                m_w_gate_f, m_b_gate_f, m_w_gate_b, m_b_gate_b, m_gla_norm, m_w_attn_o, m_w_gla_o,
                                m_w_out, m_g_ffn, m_w_up, m_conv_w, m_conv_b, m_w_down)))
    V = dict(zip(WEIGHT_NAMES, (v_c_ctx, v_w_mod, v_b_mod, v_g_mix, v_w_in, v_q_norm, v_k_norm, v_attn_sink,
                                v_w_gate_f, v_b_gate_f, v_w_gate_b, v_b_gate_b, v_gla_norm, v_w_attn_o, v_w_gla_o,
                                v_w_out, v_g_ffn, v_w_up, v_conv_w, v_conv_b, v_w_down)))
    return _step(x, c, ctx, loss_target, W, M, V)
```

```python
import functools
import math

import jax
import jax.numpy as jnp
from jax import lax
from jax.experimental import pallas as pl
from jax.experimental.pallas import tpu as pltpu

F32 = jnp.float32
BF16 = jnp.bfloat16
MESH = pl.DeviceIdType.MESH

EPS = 1e-6
HEAD_DIM = 128
N_Q_HEADS = 16
N_KV_HEADS = 4
Q_PER_KV = N_Q_HEADS // N_KV_HEADS
WINDOW = 128
GLA_HEADS = 4
GLA_LOWRANK = 16
GLA_GATE_NORM = 16.0
GLA_CHUNK = 64
GRID_W = 64
ROPE_THETA = 10000.0
GLA_LEVELS = (32, 16, 8, 4, 2, 1)
LANES = 128

ADAM_LR = 0.001
ADAM_B1 = 0.9
ADAM_B2 = 0.999
ADAM_EPS = 1e-08
ADAM_WD = 0.01
ADAM_STEP = 10

VMEM_LIMIT = 52 * 1024 * 1024


def _cparams(*sem):
    return pltpu.CompilerParams(dimension_semantics=sem, vmem_limit_bytes=VMEM_LIMIT)


def _pick(n, target, mult=LANES):
    best = None
    d = mult
    while d <= min(n, target):
        if n % d == 0:
            best = d
        d += mult
    return n if best is None else best


def _sigmoid(x):
    return 1.0 / (1.0 + jnp.exp(-x))


def _silu(x):
    return x * _sigmoid(x)


def _dsilu(x):
    s = _sigmoid(x)
    return s * (1.0 + x * (1.0 - s))


def _dot(a, b, dims):
    return lax.dot_general(a, b, (dims, ((), ())), preferred_element_type=F32)


NN = ((1,), (0,))
NT = ((1,), (1,))
TN = ((0,), (0,))


def _matmul(a, b, mode, out_dtype, name, tm=768, tn=1024, tk=2048, ride=None):
    if mode == "nn":
        (M, K), (K2, N) = a.shape, b.shape
    elif mode == "nt":
        (M, K), (N, K2) = a.shape, b.shape
    else:
        (K, M), (K2, N) = a.shape, b.shape
    assert K == K2, (name, a.shape, b.shape)
    if mode == "tn":
        tm = max(tm, 1024)
    tm, tn, tk = _pick(M, tm), _pick(N, tn), _pick(K, tk)
    nk = K // tk
    dims = {"nn": NN, "nt": NT, "tn": TN}[mode]

    def body(a_ref, b_ref, o_ref, acc_ref):
        k = pl.program_id(2)

        @pl.when(k == 0)
        def _():
            acc_ref[...] = jnp.zeros_like(acc_ref)

        acc_ref[...] += _dot(a_ref[...].astype(BF16), b_ref[...].astype(BF16), dims)

        @pl.when(k == nk - 1)
        def _():
            o_ref[...] = acc_ref[...].astype(out_dtype)

    if mode == "tn":
        a_spec = pl.BlockSpec((tk, tm), lambda i, j, k: (k, i))
    else:
        a_spec = pl.BlockSpec((tm, tk), lambda i, j, k: (i, k))
    if mode == "nt":
        b_spec = pl.BlockSpec((tn, tk), lambda i, j, k: (j, k))
    else:
        b_spec = pl.BlockSpec((tk, tn), lambda i, j, k: (k, j))
    return _pcall(
        body, name=name, grid=(M // tm, N // tn, nk),
        in_specs=[a_spec, b_spec],
        out_specs=pl.BlockSpec((tm, tn), lambda i, j, k: (i, j)),
        out_shape=jax.ShapeDtypeStruct((M, N), out_dtype),
        scratch_shapes=[pltpu.VMEM((tm, tn), F32)],
        sem=("parallel", "parallel", "arbitrary"), args=(a, b), ride=ride)


def _modnorm_fwd(xall, g, sc, sh, n_ctx, name):
    R, D = xall.shape
    tm = _pick(n_ctx, 256, 8)
    cb = n_ctx // tm

    def body(x_ref, g_ref, sc_ref, sh_ref, h_ref):
        x = x_ref[...]
        r = lax.rsqrt(jnp.mean(x * x, axis=-1, keepdims=True) + EPS)
        n = x * r * g_ref[...]
        h_ref[...] = (n * (1.0 + sc_ref[0]) + sh_ref[0]).astype(BF16)

    sel = lambda i: (jnp.where(i < cb, 0, 1), 0, 0)
    return pl.pallas_call(
        body, name=name, grid=(R // tm,),
        in_specs=[pl.BlockSpec((tm, D), lambda i: (i, 0)), pl.BlockSpec((1, D), lambda i: (0, 0)),
                  pl.BlockSpec((1, 1, D), sel), pl.BlockSpec((1, 1, D), sel)],
        out_specs=pl.BlockSpec((tm, D), lambda i: (i, 0)),
        out_shape=jax.ShapeDtypeStruct((R, D), BF16),
        compiler_params=_cparams("parallel"),
    )(xall, g, sc, sh)


def _modnorm_bwd(x, dh, g, sc, resid, name, dh_roff=0):
    N, D = x.shape
    tm = _pick(math.gcd(N, dh_roff), 256, 8)
    ro = dh_roff // tm
    want_dx = resid is not None

    def body(*refs):
        if want_dx:
            x_ref, dh_ref, g_ref, sc_ref, res_ref, dx_ref, acc_ref = refs
        else:
            x_ref, dh_ref, g_ref, sc_ref, acc_ref = refs
        i = pl.program_id(0)

        @pl.when(i == 0)
        def _():
            acc_ref[...] = jnp.zeros_like(acc_ref)

        xv, dhv, gv = x_ref[...], dh_ref[...], g_ref[...]
        r = lax.rsqrt(jnp.mean(xv * xv, axis=-1, keepdims=True) + EPS)
        xh = xv * r
        dn = dhv * (1.0 + sc_ref[...])
        acc_ref[0:1, :] += jnp.sum(dhv, axis=0, keepdims=True)
        acc_ref[1:2, :] += jnp.sum(dhv * xh * gv, axis=0, keepdims=True)
        acc_ref[2:3, :] += jnp.sum(dn * xh, axis=0, keepdims=True)
        if want_dx:
            dxh = dn * gv
            dx_ref[...] = res_ref[...] + r * (dxh - xh * jnp.mean(dxh * xh, axis=-1, keepdims=True))

    row = pl.BlockSpec((tm, D), lambda i: (i, 0))
    drow = pl.BlockSpec((tm, D), lambda i: (i + ro, 0))
    vec = pl.BlockSpec((1, D), lambda i: (0, 0))
    acc = pl.BlockSpec((8, D), lambda i: (0, 0))
    acc_shape = jax.ShapeDtypeStruct((8, D), F32)
    if want_dx:
        return pl.pallas_call(
            body, name=name, grid=(N // tm,), in_specs=[row, drow, vec, vec, row],
            out_specs=[row, acc], out_shape=[jax.ShapeDtypeStruct((N, D), F32), acc_shape],
            compiler_params=_cparams("arbitrary"))(x, dh, g, sc, resid)
    sums = pl.pallas_call(
        body, name=name, grid=(N // tm,), in_specs=[row, drow, vec, vec],
        out_specs=acc, out_shape=acc_shape, compiler_params=_cparams("arbitrary"))(x, dh, g, sc)
    return None, sums


def _resnorm_bwd(x1, dh, g, sc, dy, mix, gt, name):
    N, D = x1.shape
    tm = _pick(N, 256, 8)

    def body(x_ref, dh_ref, g_ref, sc_ref, dy_ref, mix_ref, gt_ref, dx_ref, dm_ref, acc_ref):
        i = pl.program_id(0)

        @pl.when(i == 0)
        def _():
            acc_ref[...] = jnp.zeros_like(acc_ref)

        xv, dhv, gv = x_ref[...], dh_ref[...], g_ref[...]
        r = lax.rsqrt(jnp.mean(xv * xv, axis=-1, keepdims=True) + EPS)
        xh = xv * r
        dn = dhv * (1.0 + sc_ref[...])
        dxh = dn * gv
        dx = dy_ref[...] + r * (dxh - xh * jnp.mean(dxh * xh, axis=-1, keepdims=True))
        dx_ref[...] = dx
        dm_ref[...] = (dx * gt_ref[...]).astype(BF16)
        acc_ref[0:1, :] += jnp.sum(dhv, axis=0, keepdims=True)
        acc_ref[1:2, :] += jnp.sum(dhv * xh * gv, axis=0, keepdims=True)
        acc_ref[2:3, :] += jnp.sum(dn * xh, axis=0, keepdims=True)
        acc_ref[3:4, :] += jnp.sum(dx * mix_ref[...], axis=0, keepdims=True)

    row = pl.BlockSpec((tm, D), lambda i: (i, 0))
    vec = pl.BlockSpec((1, D), lambda i: (0, 0))
    return pl.pallas_call(
        body, name=name, grid=(N // tm,), in_specs=[row, row, vec, vec, row, row, vec],
        out_specs=[row, row, pl.BlockSpec((8, D), lambda i: (0, 0))],
        out_shape=[jax.ShapeDtypeStruct((N, D), F32), jax.ShapeDtypeStruct((N, D), BF16),
                   jax.ShapeDtypeStruct((8, D), F32)],
        compiler_params=_cparams("arbitrary"))(x1, dh, g, sc, dy, mix, gt)


def _qknorm_fwd(z, cblk, nrows, roff, w, cos2, sin2, nh, name):
    W = nh * HEAD_DIM
    tm = _pick(math.gcd(nrows, roff), 256, 8)
    ro = roff // tm
    assert roff % tm == 0

    def body(z_ref, w_ref, c_ref, s_ref, o_ref):
        c, s, wv = c_ref[...], s_ref[...], w_ref[...]
        for h in range(nh):
            x = z_ref[:, h * HEAD_DIM:(h + 1) * HEAD_DIM]
            r = lax.rsqrt(jnp.mean(x * x, axis=-1, keepdims=True) + EPS)
            y = x * r * wv
            o_ref[:, h * HEAD_DIM:(h + 1) * HEAD_DIM] = (y * c + pltpu.roll(y, HEAD_DIM // 2, 1) * s).astype(BF16)

    return pl.pallas_call(
        body, name=name, grid=(nrows // tm,),
        in_specs=[pl.BlockSpec((tm, W), lambda i: (i + ro, cblk)), pl.BlockSpec((1, HEAD_DIM), lambda i: (0, 0)),
                  pl.BlockSpec((tm, HEAD_DIM), lambda i: (i + ro, 0)), pl.BlockSpec((tm, HEAD_DIM), lambda i: (i + ro, 0))],
        out_specs=pl.BlockSpec((tm, W), lambda i: (i, 0)),
        out_shape=jax.ShapeDtypeStruct((nrows, W), BF16),
        compiler_params=_cparams("parallel"),
    )(z, w, cos2, sin2)


def _qknorm_bwd(z, cblk, nrows, roff, w, cos2, sin2, dy, nh, name):
    W = nh * HEAD_DIM
    tm = _pick(math.gcd(nrows, roff), 256, 8)
    ro = roff // tm

    def body(z_ref, w_ref, c_ref, s_ref, dy_ref, dz_ref, acc_ref):
        i = pl.program_id(0)

        @pl.when(i == 0)
        def _():
            acc_ref[...] = jnp.zeros_like(acc_ref)

        c, s, wv = c_ref[...], s_ref[...], w_ref[...]
        dw = jnp.zeros((1, HEAD_DIM), F32)
        for h in range(nh):
            sl = slice(h * HEAD_DIM, (h + 1) * HEAD_DIM)
            x = z_ref[:, sl]
            d = dy_ref[:, sl]
            dyn = d * c + pltpu.roll(d * s, HEAD_DIM // 2, 1)
            r = lax.rsqrt(jnp.mean(x * x, axis=-1, keepdims=True) + EPS)
            xh = x * r
            dw = dw + jnp.sum(dyn * xh, axis=0, keepdims=True)
            dxh = dyn * wv
            dz_ref[:, sl] = (r * (dxh - xh * jnp.mean(dxh * xh, axis=-1, keepdims=True))).astype(BF16)
        acc_ref[0:1, :] += dw

    return pl.pallas_call(
        body, name=name, grid=(nrows // tm,),
        in_specs=[pl.BlockSpec((tm, W), lambda i: (i + ro, cblk)), pl.BlockSpec((1, HEAD_DIM), lambda i: (0, 0)),
                  pl.BlockSpec((tm, HEAD_DIM), lambda i: (i + ro, 0)), pl.BlockSpec((tm, HEAD_DIM), lambda i: (i + ro, 0)),
                  pl.BlockSpec((tm, W), lambda i: (i, 0))],
        out_specs=[pl.BlockSpec((tm, W), lambda i: (i, 0)), pl.BlockSpec((8, HEAD_DIM), lambda i: (0, 0))],
        out_shape=[jax.ShapeDtypeStruct((nrows, W), BF16), jax.ShapeDtypeStruct((8, HEAD_DIM), F32)],
        compiler_params=_cparams("arbitrary"),
    )(z, w, cos2, sin2, dy)


def _cast_seg(z, cblk, width, name):
    R = z.shape[0]
    tm = _pick(R, 512, 8)

    def body(z_ref, o_ref):
        o_ref[...] = z_ref[...].astype(BF16)

    return pl.pallas_call(
        body, name=name, grid=(R // tm,),
        in_specs=[pl.BlockSpec((tm, width), lambda i: (i, cblk))],
        out_specs=pl.BlockSpec((tm, width), lambda i: (i, 0)),
        out_shape=jax.ShapeDtypeStruct((R, width), BF16), compiler_params=_cparams("parallel"))(z)


NEG_BIG = -1e30


def _attn_specs(T, n_ctx):
    nb = T // WINDOW
    lb = n_ctx // WINDOW
    blk = lambda f: pl.BlockSpec((WINDOW, HEAD_DIM), f)
    win = [blk(lambda h, i: (lb + jnp.maximum(i - 1, 0), h)), blk(lambda h, i: (lb + i, h)),
           blk(lambda h, i: (lb + jnp.minimum(i + 1, nb - 1), h))]
    ctx = pl.BlockSpec((n_ctx, HEAD_DIM), lambda h, i: (0, h))
    qspec = pl.BlockSpec((WINDOW, Q_PER_KV * HEAD_DIM), lambda h, i: (i, h))
    sink = pl.BlockSpec((N_Q_HEADS, HEAD_DIM), lambda h, i: (0, 0))
    return nb, qspec, win, ctx, sink


def _attn_probs(q, kw, kctx, snk, valid):
    scale = HEAD_DIM ** -0.5
    s_lat = jnp.where(valid, _dot(q, kw, NT) * scale, NEG_BIG)
    s_ctx = _dot(q, kctx, NT) * scale
    m = jnp.maximum(jnp.maximum(jnp.max(s_lat, axis=-1, keepdims=True), jnp.max(s_ctx, axis=-1, keepdims=True)), snk)
    p_lat = jnp.exp(s_lat - m)
    p_ctx = jnp.exp(s_ctx - m)
    p_snk = jnp.exp(snk - m)
    den = p_snk + jnp.sum(p_lat, axis=-1, keepdims=True) + jnp.sum(p_ctx, axis=-1, keepdims=True)
    return p_lat, p_ctx, p_snk, den


def _attn_valid(i, T, heads):
    rows = heads * WINDOW
    qpos = i * WINDOW + (lax.broadcasted_iota(jnp.int32, (rows, 3 * WINDOW), 0) & (WINDOW - 1))
    kpos = (i - 1) * WINDOW + lax.broadcasted_iota(jnp.int32, (rows, 3 * WINDOW), 1)
    return (jnp.abs(qpos - kpos) <= WINDOW) & (kpos >= 0) & (kpos < T)


def _stack_heads(ref):
    return jnp.concatenate([ref[:, g * HEAD_DIM:(g + 1) * HEAD_DIM] for g in range(Q_PER_KV)], axis=0)


def _stack_sinks(sink_ref, h):
    return jnp.concatenate([jnp.broadcast_to(sink_ref[pl.ds(h * Q_PER_KV + g, 1), :][:, 0:1], (WINDOW, 1))
                            for g in range(Q_PER_KV)], axis=0)


def _attn_fwd(qn, kn, vb, sink_rows, n_ctx, name, ride=None):
    T = qn.shape[0]
    nb, qspec, win, ctx, sink = _attn_specs(T, n_ctx)

    def body(q_ref, kp, kc, kx, vp, vc, vx, kctx_ref, vctx_ref, sink_ref, o_ref):
        h, i = pl.program_id(0), pl.program_id(1)
        kw = jnp.concatenate([kp[...], kc[...], kx[...]], axis=0)
        vw = jnp.concatenate([vp[...], vc[...], vx[...]], axis=0)
        kctx, vctx = kctx_ref[...], vctx_ref[...]
        p_lat, p_ctx, _, den = _attn_probs(_stack_heads(q_ref), kw, kctx, _stack_sinks(sink_ref, h),
                                           _attn_valid(i, T, Q_PER_KV))
        o = ((_dot(p_lat.astype(BF16), vw, NN) + _dot(p_ctx.astype(BF16), vctx, NN)) / den).astype(BF16)
        for g in range(Q_PER_KV):
            o_ref[:, g * HEAD_DIM:(g + 1) * HEAD_DIM] = o[g * WINDOW:(g + 1) * WINDOW]

    return _pcall(
        body, name=name, grid=(N_KV_HEADS, nb),
        in_specs=[qspec] + win + win + [ctx, ctx, sink],
        out_specs=qspec, out_shape=jax.ShapeDtypeStruct(qn.shape, BF16),
        sem=("parallel", "parallel"), args=(qn, kn, kn, kn, vb, vb, vb, kn, vb, sink_rows), ride=ride)


def _attn_bwd(qn, kn, vb, sink_rows, do, n_ctx, name, ride=None):
    T = qn.shape[0]
    nb, qspec, win, ctx, sink = _attn_specs(T, n_ctx)
    scale = HEAD_DIM ** -0.5
    TP = T + 2 * WINDOW

    def body(q_ref, kp, kc, kx, vp, vc, vx, kctx_ref, vctx_ref, sink_ref, do_ref,
             dq_ref, dkw_ref, dvw_ref, dkc_ref, dvc_ref, dsn_ref):
        h, i = pl.program_id(0), pl.program_id(1)

        @pl.when(i == 0)
        def _():
            dkw_ref[...] = jnp.zeros_like(dkw_ref)
            dvw_ref[...] = jnp.zeros_like(dvw_ref)
            dkc_ref[...] = jnp.zeros_like(dkc_ref)
            dvc_ref[...] = jnp.zeros_like(dvc_ref)
            dsn_ref[...] = jnp.zeros_like(dsn_ref)

        kw = jnp.concatenate([kp[...], kc[...], kx[...]], axis=0)
        vw = jnp.concatenate([vp[...], vc[...], vx[...]], axis=0)
        kctx, vctx = kctx_ref[...], vctx_ref[...]
        lane = lax.broadcasted_iota(jnp.int32, (8, HEAD_DIM), 1)
        q, d_o = _stack_heads(q_ref), _stack_heads(do_ref)
        p_lat, p_ctx, p_snk, den = _attn_probs(q, kw, kctx, _stack_sinks(sink_ref, h), _attn_valid(i, T, Q_PER_KV))
        inv = 1.0 / den
        p_lat, p_ctx, p_snk = p_lat * inv, p_ctx * inv, p_snk * inv
        dp_lat = _dot(d_o, vw, NT)
        dp_ctx = _dot(d_o, vctx, NT)
        dr = jnp.sum(p_lat * dp_lat, axis=-1, keepdims=True) + jnp.sum(p_ctx * dp_ctx, axis=-1, keepdims=True)
        ds_lat = (p_lat * (dp_lat - dr) * scale).astype(BF16)
        ds_ctx = (p_ctx * (dp_ctx - dr) * scale).astype(BF16)
        dq = _dot(ds_lat, kw, NN) + _dot(ds_ctx, kctx, NN)
        snk_terms = p_snk * dr
        dsn = jnp.zeros((8, HEAD_DIM), F32)
        for g in range(Q_PER_KV):
            dq_ref[:, g * HEAD_DIM:(g + 1) * HEAD_DIM] = dq[g * WINDOW:(g + 1) * WINDOW]
            dsn = dsn + jnp.where(lane == g, -jnp.sum(snk_terms[g * WINDOW:(g + 1) * WINDOW], axis=0, keepdims=True), 0.0)
        rows = pl.ds(pl.multiple_of(i * WINDOW, WINDOW), 3 * WINDOW)
        dkw_ref[rows, :] += _dot(ds_lat, q, TN)
        dvw_ref[rows, :] += _dot(p_lat.astype(BF16), d_o, TN)
        dkc_ref[...] += _dot(ds_ctx, q, TN)
        dvc_ref[...] += _dot(p_ctx.astype(BF16), d_o, TN)
        dsn_ref[0] += dsn

    wacc = pl.BlockSpec((TP, HEAD_DIM), lambda h, i: (0, h))
    return _pcall(
        body, name=name, grid=(N_KV_HEADS, nb),
        in_specs=[qspec] + win + win + [ctx, ctx, sink, qspec],
        out_specs=[qspec, wacc, wacc, ctx, ctx, pl.BlockSpec((1, 8, HEAD_DIM), lambda h, i: (h, 0, 0))],
        out_shape=[jax.ShapeDtypeStruct(qn.shape, F32),
                   jax.ShapeDtypeStruct((TP, N_KV_HEADS * HEAD_DIM), F32),
                   jax.ShapeDtypeStruct((TP, N_KV_HEADS * HEAD_DIM), F32),
                   jax.ShapeDtypeStruct((n_ctx, N_KV_HEADS * HEAD_DIM), F32),
                   jax.ShapeDtypeStruct((n_ctx, N_KV_HEADS * HEAD_DIM), F32),
                   jax.ShapeDtypeStruct((N_KV_HEADS, 8, HEAD_DIM), F32)],
        sem=("arbitrary", "arbitrary"), args=(qn, kn, kn, kn, vb, vb, vb, kn, vb, sink_rows, do), ride=ride)


def _gla_masks(dirv):
    C = GLA_CHUNK

    def times(reps):
        r = lax.broadcasted_iota(jnp.int32, (C, reps * C), 0)
        c = lax.broadcasted_iota(jnp.int32, (C, reps * C), 1) & (C - 1)
        return jnp.where(dirv == 0, r, C - 1 - r), jnp.where(dirv == 0, c, C - 1 - c)

    def level(tt, ss, m):
        sh = m.bit_length() - 1
        same = (tt >> (sh + 1)) == (ss >> (sh + 1))
        return same, (tt >> sh) & 1, (ss >> sh) & 1

    tt, ss = times(3)
    le = (ss <= tt).astype(jnp.int32)
    sums = [le == 1]
    for m in GLA_LEVELS:
        same, ut, us = level(tt, ss, m)
        sums.append(same & (ut == us) & (ut == le))
    tt, ss = times(1)
    blocks = [ss == tt]
    for m in GLA_LEVELS:
        same, ut, us = level(tt, ss, m)
        blocks.append(same & (ut == 1) & (us == 0))
    mall3 = jnp.concatenate([jnp.where(s, 1.0, 0.0) for s in sums], axis=0).astype(BF16)
    return mall3, blocks


def _pieces(x):
    hi = x.astype(BF16)
    r1 = x - hi.astype(F32)
    mid = r1.astype(BF16)
    return hi, mid, (r1 - mid.astype(F32)).astype(BF16)


def _sum_f32(mall3, x):
    return _dot(mall3, jnp.concatenate(_pieces(x), axis=0), NN)


def _sum_f32_t(mall3, x):
    m = mall3[:, 0:GLA_CHUNK]
    hi, mid, lo = _pieces(x)
    return _dot(m, hi, TN) + _dot(m, mid, TN) + _dot(m, lo, TN)


def _gla_chunk_of(dirv, j, lc, nc):
    return jnp.where(dirv == 0, j, jnp.where(j < lc, lc - 1 - j, nc + lc - 1 - j))


def _gla_gate(lr_ref, wg_ref, bg_ref):
    pre = _dot(lr_ref[...].astype(BF16), wg_ref[0].astype(BF16), NN) + bg_ref[0]
    g = (jnp.minimum(pre, 0.0) - jnp.log(1.0 + jnp.exp(-jnp.abs(pre)))) * (1.0 / GLA_GATE_NORM)
    return pre, g


def _gla_fwd(z, qblk, kblk, vblk, lrblk, wg, bg, DV, n_ctx, name):
    R = z.shape[0]
    C = GLA_CHUNK
    DK = wg.shape[2] // GLA_HEADS
    nc, lc = R // C, n_ctx // C
    qscale = DK ** -0.5

    GK, GV = GLA_HEADS * DK, GLA_HEADS * DV

    def body(q_ref, k_ref, v_ref, lr_ref, wg_ref, bg_ref, o_ref, sp_ref, st_ref):
        dirv, j = pl.program_id(0), pl.program_id(1)

        @pl.when(j == 0)
        def _():
            st_ref[...] = jnp.zeros_like(st_ref)

        mall, blocks = _gla_masks(dirv)
        _, g_all = _gla_gate(lr_ref, wg_ref, bg_ref)
        E_all = _sum_f32(mall, g_all)
        for h in range(GLA_HEADS):
            ks, vs = slice(h * DK, (h + 1) * DK), slice(h * DV, (h + 1) * DV)
            q, k, v = q_ref[:, ks] * qscale, k_ref[:, ks], v_ref[:, vs].astype(BF16)
            g, E = g_all[:, ks], E_all[:, ks]
            st = st_ref[h]
            sp_ref[0, h, 0] = st
            A = jnp.where(blocks[0], _dot(q.astype(BF16), k.astype(BF16), NT), 0.0)
            for l in range(len(GLA_LEVELS)):
                e = jnp.exp(E[(1 + l) * C:(2 + l) * C])
                A = A + jnp.where(blocks[l + 1], _dot((q * e).astype(BF16), (k * e).astype(BF16), NT), 0.0)
            o_ref[0, :, vs] = (_dot((q * jnp.exp(E[0:C])).astype(BF16), st.astype(BF16), NT)
                               + _dot(A.astype(BF16), v, NN))
            last = jnp.sum(g, axis=0, keepdims=True)
            st_ref[h] = jnp.exp(last) * st + _dot(v, (k * jnp.exp(last - E[0:C])).astype(BF16), TN)

    chunk = functools.partial(_gla_chunk_of, lc=lc, nc=nc)
    return pl.pallas_call(
        body, name=name, grid=(2, nc),
        in_specs=[pl.BlockSpec((C, GK), lambda d, j: (chunk(d, j), qblk)),
                  pl.BlockSpec((C, GK), lambda d, j: (chunk(d, j), kblk)),
                  pl.BlockSpec((C, GV), lambda d, j: (chunk(d, j), vblk)),
                  pl.BlockSpec((C, LANES), lambda d, j: (chunk(d, j), lrblk)),
                  pl.BlockSpec((1, LANES, GK), lambda d, j: (d, 0, 0)),
                  pl.BlockSpec((1, 1, GK), lambda d, j: (d, 0, 0))],
        out_specs=[pl.BlockSpec((1, C, GV), lambda d, j: (d, chunk(d, j), 0)),
                   pl.BlockSpec((1, GLA_HEADS, 1, DV, DK), lambda d, j: (d, 0, j, 0, 0))],
        out_shape=[jax.ShapeDtypeStruct((2, R, GV), F32),
                   jax.ShapeDtypeStruct((2, GLA_HEADS, nc, DV, DK), F32)],
        scratch_shapes=[pltpu.VMEM((GLA_HEADS, DV, DK), F32)],
        compiler_params=_cparams("parallel", "arbitrary"),
    )(z, z, z, z, wg, bg)


def _gla_bwd(z, qblk, kblk, vblk, lrblk, wg, bg, sprev, do, n_ctx, name, ride=None):
    R = z.shape[0]
    C = GLA_CHUNK
    DK, DV = wg.shape[2] // GLA_HEADS, do.shape[1] // GLA_HEADS
    nc, lc = R // C, n_ctx // C
    qscale = DK ** -0.5
    nl = len(GLA_LEVELS)

    GK, GV = GLA_HEADS * DK, GLA_HEADS * DV

    def body(q_ref, k_ref, v_ref, lr_ref, wg_ref, bg_ref, sp_ref, do_ref,
             dq_ref, dk_ref, dv_ref, dpre_ref, dbg_ref, dst_ref):
        dirv, jr = pl.program_id(0), pl.program_id(1)

        @pl.when(jr == 0)
        def _():
            dst_ref[...] = jnp.zeros_like(dst_ref)
            dbg_ref[...] = jnp.zeros_like(dbg_ref)

        mall, blocks = _gla_masks(dirv)
        pre_all, g_all = _gla_gate(lr_ref, wg_ref, bg_ref)
        E_all = _sum_f32(mall, g_all)
        for h in range(GLA_HEADS):
            ks, vs = slice(h * DK, (h + 1) * DK), slice(h * DV, (h + 1) * DV)
            q, k, v = q_ref[:, ks] * qscale, k_ref[:, ks], v_ref[:, vs].astype(BF16)
            pre, g, E = pre_all[:, ks], g_all[:, ks], E_all[:, ks]
            last = jnp.sum(g, axis=0, keepdims=True)
            eb, er, decay = jnp.exp(E[0:C]), jnp.exp(last - E[0:C]), jnp.exp(last)
            st = sp_ref[0, h, 0]
            dst = dst_ref[h]
            d_o = do_ref[:, vs]
            qe, kd = q * eb, k * er
            qb, kb = q.astype(BF16), k.astype(BF16)
            A = jnp.where(blocks[0], _dot(qb, kb, NT), 0.0)
            for l in range(nl):
                e = jnp.exp(E[(1 + l) * C:(2 + l) * C])
                A = A + jnp.where(blocks[l + 1], _dot((q * e).astype(BF16), (k * e).astype(BF16), NT), 0.0)
            dA = _dot(d_o, v, NT)
            dv_ref[0, :, vs] = _dot(A.astype(BF16), d_o, TN) + _dot(kd.astype(BF16), dst.astype(BF16), NT)
            dqe = _dot(d_o, st.astype(BF16), NN)
            dkd = _dot(v, dst.astype(BF16), NN)
            G = jnp.where(blocks[0], dA, 0.0).astype(BF16)
            dq = dqe * eb + _dot(G, kb, NN)
            dk = dkd * er + _dot(G, qb, TN)
            dEr = dkd * kd
            dE = [dqe * qe - dEr]
            for l in range(nl):
                e = jnp.exp(E[(1 + l) * C:(2 + l) * C])
                ql, kl = q * e, k * e
                G = jnp.where(blocks[l + 1], dA, 0.0).astype(BF16)
                dql = _dot(G, kl.astype(BF16), NN)
                dkl = _dot(G, ql.astype(BF16), TN)
                dq = dq + dql * e
                dk = dk + dkl * e
                dE.append(dql * ql + dkl * kl)
            dlast = jnp.sum(dst * st, axis=0, keepdims=True) * decay + jnp.sum(dEr, axis=0, keepdims=True)
            dg = _sum_f32_t(mall, jnp.concatenate(dE, axis=0)) + dlast
            dpre = dg * (1.0 / GLA_GATE_NORM) / (1.0 + jnp.exp(pre))
            dq_ref[0, :, ks] = dq * qscale
            dk_ref[0, :, ks] = dk
            dpre_ref[:, ks] = dpre.astype(BF16)
            dbg_ref[0, :, ks] += jnp.sum(dpre, axis=0, keepdims=True)
            dst_ref[h] = decay * dst + _dot(d_o, qe.astype(BF16), TN)

    def chunk(d, jr):
        return _gla_chunk_of(d, nc - 1 - jr, lc, nc)

    return _pcall(
        body, name=name, grid=(2, nc),
        in_specs=[pl.BlockSpec((C, GK), lambda d, j: (chunk(d, j), qblk)),
                  pl.BlockSpec((C, GK), lambda d, j: (chunk(d, j), kblk)),
                  pl.BlockSpec((C, GV), lambda d, j: (chunk(d, j), vblk)),
                  pl.BlockSpec((C, LANES), lambda d, j: (chunk(d, j), lrblk)),
                  pl.BlockSpec((1, LANES, GK), lambda d, j: (d, 0, 0)),
                  pl.BlockSpec((1, 1, GK), lambda d, j: (d, 0, 0)),
                  pl.BlockSpec((1, GLA_HEADS, 1, DV, DK), lambda d, j: (d, 0, nc - 1 - j, 0, 0)),
                  pl.BlockSpec((C, GV), lambda d, j: (chunk(d, j), 0))],
        out_specs=[pl.BlockSpec((1, C, GK), lambda d, j: (d, chunk(d, j), 0)),
                   pl.BlockSpec((1, C, GK), lambda d, j: (d, chunk(d, j), 0)),
                   pl.BlockSpec((1, C, GV), lambda d, j: (d, chunk(d, j), 0)),
                   pl.BlockSpec((C, GK), lambda d, j: (chunk(d, j), d)),
                   pl.BlockSpec((1, 1, GK), lambda d, j: (d, 0, 0))],
        out_shape=[jax.ShapeDtypeStruct((2, R, GK), F32),
                   jax.ShapeDtypeStruct((2, R, GK), F32),
                   jax.ShapeDtypeStruct((2, R, GV), F32),
                   jax.ShapeDtypeStruct((R, 2 * GK), BF16),
                   jax.ShapeDtypeStruct((2, 1, GK), F32)],
        scratch_shapes=[pltpu.VMEM((GLA_HEADS, DV, DK), F32)],
        sem=("arbitrary", "arbitrary"), args=(z, z, z, z, wg, bg, sprev, do), ride=ride)


def _glanorm_fwd(o, z, rbblk, gn, n_ctx, name):
    _, R, GV = o.shape
    T = R - n_ctx
    DV = GV // GLA_HEADS
    tm = _pick(n_ctx, 256, 8)
    ro = n_ctx // tm

    def body(o0_ref, o1_ref, rb_ref, gn_ref, p_ref):
        gnv = gn_ref[...]
        for h in range(GLA_HEADS):
            sl = slice(h * DV, (h + 1) * DV)
            og = o0_ref[0, :, sl] + o1_ref[0, :, sl]
            r = lax.rsqrt(jnp.mean(og * og, axis=-1, keepdims=True) + EPS)
            p_ref[:, sl] = (og * r * gnv * _silu(rb_ref[:, sl])).astype(BF16)

    return pl.pallas_call(
        body, name=name, grid=(T // tm,),
        in_specs=[pl.BlockSpec((1, tm, GV), lambda i: (0, i + ro, 0)), pl.BlockSpec((1, tm, GV), lambda i: (1, i + ro, 0)),
                  pl.BlockSpec((tm, GV), lambda i: (i + ro, rbblk)), pl.BlockSpec((1, DV), lambda i: (0, 0))],
        out_specs=pl.BlockSpec((tm, GV), lambda i: (i, 0)),
        out_shape=jax.ShapeDtypeStruct((T, GV), BF16), compiler_params=_cparams("parallel"))(o, o, z, gn)


def _glanorm_bwd(o, z, rbblk, gn, dp, n_ctx, name):
    _, R, GV = o.shape
    T = R - n_ctx
    DV = GV // GLA_HEADS
    tm = _pick(n_ctx, 256, 8)
    ro = n_ctx // tm

    def body(o0_ref, o1_ref, rb_ref, gn_ref, dp_ref, do_ref, drb_ref, acc_ref):
        i = pl.program_id(0)

        @pl.when(i == 0)
        def _():
            acc_ref[...] = jnp.zeros_like(acc_ref)

        gnv = gn_ref[...]
        dgn = jnp.zeros((1, DV), F32)
        for h in range(GLA_HEADS):
            sl = slice(h * DV, (h + 1) * DV)
            og = o0_ref[0, :, sl] + o1_ref[0, :, sl]
            rb = rb_ref[:, sl]
            d = dp_ref[:, sl]
            r = lax.rsqrt(jnp.mean(og * og, axis=-1, keepdims=True) + EPS)
            xh = og * r
            drb_ref[:, sl] = (d * xh * gnv * _dsilu(rb)).astype(BF16)
            dn = d * _silu(rb)
            dgn = dgn + jnp.sum(dn * xh, axis=0, keepdims=True)
            dxh = dn * gnv
            do_ref[:, sl] = (r * (dxh - xh * jnp.mean(dxh * xh, axis=-1, keepdims=True))).astype(BF16)
        acc_ref[0:1, :] += dgn

    row = pl.BlockSpec((tm, GV), lambda i: (i, 0))
    return pl.pallas_call(
        body, name=name, grid=(T // tm,),
        in_specs=[pl.BlockSpec((1, tm, GV), lambda i: (0, i + ro, 0)), pl.BlockSpec((1, tm, GV), lambda i: (1, i + ro, 0)),
                  pl.BlockSpec((tm, GV), lambda i: (i + ro, rbblk)), pl.BlockSpec((1, DV), lambda i: (0, 0)), row],
        out_specs=[row, row, pl.BlockSpec((8, DV), lambda i: (0, 0))],
        out_shape=[jax.ShapeDtypeStruct((T, GV), BF16), jax.ShapeDtypeStruct((T, GV), BF16),
                   jax.ShapeDtypeStruct((8, DV), F32)],
        compiler_params=_cparams("arbitrary"))(o, o, z, gn, dp)


def _gate_fwd(z, gablk, gbblk, ya, yg, n_ctx, name):
    T, D = ya.shape
    tm = _pick(n_ctx, 256, 8)
    ro = n_ctx // tm

    def body(ga_ref, gb_ref, ya_ref, yg_ref, m_ref):
        m_ref[...] = (_sigmoid(ga_ref[...]) * ya_ref[...] + _sigmoid(gb_ref[...]) * yg_ref[...]).astype(BF16)

    row = pl.BlockSpec((tm, D), lambda i: (i, 0))
    return pl.pallas_call(
        body, name=name, grid=(T // tm,),
        in_specs=[pl.BlockSpec((tm, D), lambda i: (i + ro, gablk)), pl.BlockSpec((tm, D), lambda i: (i + ro, gbblk)), row, row],
        out_specs=row, out_shape=jax.ShapeDtypeStruct((T, D), BF16), compiler_params=_cparams("parallel"))(z, z, ya, yg)


def _gate_bwd(z, gablk, gbblk, ya, yg, dm, n_ctx, name):
    T, D = ya.shape
    tm = _pick(n_ctx, 256, 8)
    ro = n_ctx // tm

    def body(ga_ref, gb_ref, ya_ref, yg_ref, dm_ref, dya_ref, dyg_ref, dga_ref, dgb_ref):
        d = dm_ref[...]
        sa, sb = _sigmoid(ga_ref[...]), _sigmoid(gb_ref[...])
        dya_ref[...] = (d * sa).astype(BF16)
        dyg_ref[...] = (d * sb).astype(BF16)
        dga_ref[...] = (d * ya_ref[...] * sa * (1.0 - sa)).astype(BF16)
        dgb_ref[...] = (d * yg_ref[...] * sb * (1.0 - sb)).astype(BF16)

    row = pl.BlockSpec((tm, D), lambda i: (i, 0))
    sh = jax.ShapeDtypeStruct((T, D), BF16)
    return pl.pallas_call(
        body, name=name, grid=(T // tm,),
        in_specs=[pl.BlockSpec((tm, D), lambda i: (i + ro, gablk)), pl.BlockSpec((tm, D), lambda i: (i + ro, gbblk)), row, row, row],
        out_specs=[row] * 4, out_shape=[sh] * 4, compiler_params=_cparams("parallel"))(z, z, ya, yg, dm)


def _resnorm_fwd(x, mix, gt, g, sc, sh, name):
    T, D = x.shape
    tm = _pick(T, 256, 8)

    def body(x_ref, mix_ref, gt_ref, g_ref, sc_ref, sh_ref, x1_ref, h_ref):
        x1 = x_ref[...] + gt_ref[...] * mix_ref[...]
        x1_ref[...] = x1
        r = lax.rsqrt(jnp.mean(x1 * x1, axis=-1, keepdims=True) + EPS)
        h_ref[...] = (x1 * r * g_ref[...] * (1.0 + sc_ref[...]) + sh_ref[...]).astype(BF16)

    row = pl.BlockSpec((tm, D), lambda i: (i, 0))
    vec = pl.BlockSpec((1, D), lambda i: (0, 0))
    return pl.pallas_call(
        body, name=name, grid=(T // tm,), in_specs=[row, row, vec, vec, vec, vec], out_specs=[row, row],
        out_shape=[jax.ShapeDtypeStruct((T, D), F32), jax.ShapeDtypeStruct((T, D), BF16)],
        compiler_params=_cparams("parallel"))(x, mix, gt, g, sc, sh)


def _loss_head(d, x1, gt, target, name):
    T, D = d.shape
    tm = _pick(T, 256, 8)

    def body(d_ref, x1_ref, gt_ref, t_ref, dy_ref, dd_ref, acc_ref):
        i = pl.program_id(0)

        @pl.when(i == 0)
        def _():
            acc_ref[...] = jnp.zeros_like(acc_ref)

        dv, gtv = d_ref[...], gt_ref[...]
        e = x1_ref[...] + gtv * dv - t_ref[...]
        dy = e * (1.0 / D)
        dy_ref[...] = dy
        dd_ref[...] = (dy * gtv).astype(BF16)
        acc_ref[0:1, :] += jnp.sum(e * e, axis=0, keepdims=True)
        acc_ref[1:2, :] += jnp.sum(dy * dv, axis=0, keepdims=True)

    row = pl.BlockSpec((tm, D), lambda i: (i, 0))
    return pl.pallas_call(
        body, name=name, grid=(T // tm,), in_specs=[row, row, pl.BlockSpec((1, D), lambda i: (0, 0)), row],
        out_specs=[row, row, pl.BlockSpec((8, D), lambda i: (0, 0))],
        out_shape=[jax.ShapeDtypeStruct((T, D), F32), jax.ShapeDtypeStruct((T, D), BF16),
                   jax.ShapeDtypeStruct((8, D), F32)],
        compiler_params=_cparams("arbitrary"))(d, x1, gt, target)


def _halo_specs(T, tm, tw, col_of, order):
    n8 = tm // 8
    if order == "ij":
        mid = lambda i, j: (i, col_of(j))
        prev = lambda i, j: (jnp.maximum(i * n8 - 1, 0), col_of(j))
        nxt = lambda i, j: (jnp.minimum((i + 1) * n8, T // 8 - 1), col_of(j))
    else:
        mid = lambda j, i: (i, col_of(j))
        prev = lambda j, i: (jnp.maximum(i * n8 - 1, 0), col_of(j))
        nxt = lambda j, i: (jnp.minimum((i + 1) * n8, T // 8 - 1), col_of(j))
    return [pl.BlockSpec((tm, tw), mid), pl.BlockSpec((8, tw), prev), pl.BlockSpec((8, tw), nxt)]


def _shift_rows(x, before, after):
    tm = x.shape[0]
    row = lax.broadcasted_iota(jnp.int32, x.shape, 0)
    return (jnp.where(row == 0, before, pltpu.roll(x, 1, 0)),
            jnp.where(row == tm - 1, after, pltpu.roll(x, tm - 1, 0)))


def _conv_fwd(u, cw, cb, name):
    T, F2 = u.shape
    F = F2 // 2
    tm, tw = _pick(T, 256, 8), _pick(F, 512)
    nt, nw = T // tm, F // tw

    def body(ua, uap, uan, ug, ugp, ugn, cwa, cwg, cba, cbg, f_ref):
        i = pl.program_id(0)
        first, last = i == 0, i == nt - 1

        def conv(u_ref, up_ref, un_ref, w_ref, b_ref):
            m = u_ref[...]
            p, n = _shift_rows(m, jnp.where(first, 0.0, up_ref[7:8, :]), jnp.where(last, 0.0, un_ref[0:1, :]))
            return p * w_ref[0:1, :] + m * w_ref[1:2, :] + n * w_ref[2:3, :] + b_ref[...]

        a = conv(ua, uap, uan, cwa, cba)
        g = conv(ug, ugp, ugn, cwg, cbg)
        f_ref[...] = (_silu(a) * g).astype(BF16)

    wspec = lambda off: pl.BlockSpec((3, tw), lambda i, j: (0, j + off))
    bspec = lambda off: pl.BlockSpec((1, tw), lambda i, j: (0, j + off))
    return pl.pallas_call(
        body, name=name, grid=(nt, nw),
        in_specs=_halo_specs(T, tm, tw, lambda j: j, "ij") + _halo_specs(T, tm, tw, lambda j: j + nw, "ij")
        + [wspec(0), wspec(nw), bspec(0), bspec(nw)],
        out_specs=pl.BlockSpec((tm, tw), lambda i, j: (i, j)),
        out_shape=jax.ShapeDtypeStruct((T, F), BF16),
        compiler_params=_cparams("parallel", "parallel"),
    )(u, u, u, u, u, u, cw, cw, cb, cb)


def _conv_bwd(u, df, cw, cb, name):
    T, F2 = u.shape
    F = F2 // 2
    tm, tw = _pick(T, 256, 8), _pick(F, 512)
    nt, nw = T // tm, F // tw

    def body(ua, uap, uan, ug, ugp, ugn, cwa, cwg, cba, cbg, df_ref, dfp, dfn, dua_ref, dug_ref, acca_ref, accg_ref):
        i = pl.program_id(1)

        @pl.when(i == 0)
        def _():
            acca_ref[...] = jnp.zeros_like(acca_ref)
            accg_ref[...] = jnp.zeros_like(accg_ref)

        first, last = i == 0, i == nt - 1
        wa, wg, ba, bg = cwa[...], cwg[...], cba[...], cbg[...]

        def conv(p, m, n, w, b):
            return p * w[0:1] + m * w[1:2] + n * w[2:3] + b

        def grads(a, g, d):
            return d * g * _dsilu(a), d * _silu(a)

        xa, xg, d = ua[...], ug[...], df_ref[...]
        sa = _shift_rows(xa, jnp.where(first, 0.0, uap[7:8, :]), jnp.where(last, 0.0, uan[0:1, :]))
        sg = _shift_rows(xg, jnp.where(first, 0.0, ugp[7:8, :]), jnp.where(last, 0.0, ugn[0:1, :]))
        da, dg = grads(conv(sa[0], xa, sa[1], wa, ba), conv(sg[0], xg, sg[1], wg, bg), d)
        da_p, dg_p = grads(conv(uap[6:7, :], uap[7:8, :], xa[0:1], wa, ba),
                           conv(ugp[6:7, :], ugp[7:8, :], xg[0:1], wg, bg), dfp[7:8, :])
        da_n, dg_n = grads(conv(xa[tm - 1:tm], uan[0:1, :], uan[1:2, :], wa, ba),
                           conv(xg[tm - 1:tm], ugn[0:1, :], ugn[1:2, :], wg, bg), dfn[0:1, :])
        ta = _shift_rows(da, jnp.where(first, 0.0, da_p), jnp.where(last, 0.0, da_n))
        tg = _shift_rows(dg, jnp.where(first, 0.0, dg_p), jnp.where(last, 0.0, dg_n))
        dua_ref[...] = (ta[1] * wa[0:1] + da * wa[1:2] + ta[0] * wa[2:3]).astype(BF16)
        dug_ref[...] = (tg[1] * wg[0:1] + dg * wg[1:2] + tg[0] * wg[2:3]).astype(BF16)
        for t, (va, vg) in enumerate(((sa[0], sg[0]), (xa, xg), (sa[1], sg[1]))):
            acca_ref[t:t + 1, :] += jnp.sum(da * va, axis=0, keepdims=True)
            accg_ref[t:t + 1, :] += jnp.sum(dg * vg, axis=0, keepdims=True)
        acca_ref[3:4, :] += jnp.sum(da, axis=0, keepdims=True)
        accg_ref[3:4, :] += jnp.sum(dg, axis=0, keepdims=True)

    wspec = lambda off: pl.BlockSpec((3, tw), lambda j, i: (0, j + off))
    bspec = lambda off: pl.BlockSpec((1, tw), lambda j, i: (0, j + off))
    row = pl.BlockSpec((tm, tw), lambda j, i: (i, j))
    acc = pl.BlockSpec((8, tw), lambda j, i: (0, j))
    return pl.pallas_call(
        body, name=name, grid=(nw, nt),
        in_specs=_halo_specs(T, tm, tw, lambda j: j, "ji") + _halo_specs(T, tm, tw, lambda j: j + nw, "ji")
        + [wspec(0), wspec(nw), bspec(0), bspec(nw)] + _halo_specs(T, tm, tw, lambda j: j, "ji"),
        out_specs=[row, row, acc, acc],
        out_shape=[jax.ShapeDtypeStruct((T, F), BF16), jax.ShapeDtypeStruct((T, F), BF16),
                   jax.ShapeDtypeStruct((8, F), F32), jax.ShapeDtypeStruct((8, F), F32)],
        compiler_params=_cparams("parallel", "arbitrary"),
    )(u, u, u, u, u, u, cw, cw, cb, cb, df, df, df)


def _assemble_dz(lay, Z, n_ctx, dqa, drb, dga, dgb, dka, dva, dvg, dqg, dkg, dlr, name):
    T = dqa.shape[0]
    R = T + n_ctx
    tm = _pick(n_ctx, 128, 8)
    cb = n_ctx // tm

    def body(dqa_ref, drb_ref, dga_ref, dgb_ref, dka_ref, dva_ref, dvg0, dvg1, dqg0, dqg1, dkg0, dkg1, dlr_ref, o_ref):
        lat = pl.program_id(0) >= cb

        def put(seg, val):
            o_ref[:, lay[seg]:lay[seg] + val.shape[1]] = val.astype(BF16)

        def lat_only(ref):
            v = ref[...]
            return jnp.where(lat, v, jnp.zeros_like(v))

        put("qa", lat_only(dqa_ref))
        put("rb", lat_only(drb_ref))
        put("ga", lat_only(dga_ref))
        put("gb", lat_only(dgb_ref))
        put("ka", dka_ref[...])
        put("va", dva_ref[...])
        put("vb", dvg0[0] + dvg1[0])
        put("qb", dqg0[0] + dqg1[0])
        put("kb", dkg0[0] + dkg1[0])
        put("lr", dlr_ref[...])

    lat_spec = lambda a: pl.BlockSpec((tm, a.shape[1]), lambda i: (jnp.maximum(i - cb, 0), 0))
    all_spec = lambda a: pl.BlockSpec((tm, a.shape[1]), lambda i: (i, 0))
    dir_specs = lambda a: [pl.BlockSpec((1, tm, a.shape[2]), lambda i: (0, i, 0)),
                           pl.BlockSpec((1, tm, a.shape[2]), lambda i: (1, i, 0))]
    return pl.pallas_call(
        body, name=name, grid=(R // tm,),
        in_specs=[lat_spec(dqa), lat_spec(drb), lat_spec(dga), lat_spec(dgb), all_spec(dka), all_spec(dva)]
        + dir_specs(dvg) + dir_specs(dqg) + dir_specs(dkg) + [all_spec(dlr)],
        out_specs=pl.BlockSpec((tm, Z), lambda i: (i, 0)),
        out_shape=jax.ShapeDtypeStruct((R, Z), BF16), compiler_params=_cparams("parallel"),
    )(dqa, drb, dga, dgb, dka, dva, dvg, dvg, dqg, dqg, dkg, dkg, dlr)


def _mod_fwd(ca, w, b, name):
    n, D = ca.shape
    N = w.shape[1]
    tn = _pick(N, 512)

    def body(c_ref, w_ref, b_ref, o_ref, s_ref):
        s = _silu(c_ref[...])
        s_ref[...] = s
        o_ref[...] = _dot(s.astype(BF16), w_ref[...].astype(BF16), NN) + b_ref[...]

    return pl.pallas_call(
        body, name=name, grid=(N // tn,),
        in_specs=[pl.BlockSpec((n, D), lambda j: (0, 0)), pl.BlockSpec((D, tn), lambda j: (0, j)),
                  pl.BlockSpec((1, tn), lambda j: (0, j))],
        out_specs=[pl.BlockSpec((n, tn), lambda j: (0, j)), pl.BlockSpec((n, D), lambda j: (0, 0))],
        out_shape=[jax.ShapeDtypeStruct((n, N), F32), jax.ShapeDtypeStruct((n, D), F32)],
        compiler_params=_cparams("arbitrary"))(ca, w, b)


def _silu_bwd(dsil, ca, name):
    def body(d_ref, c_ref, o_ref):
        o_ref[...] = d_ref[...] * _dsilu(c_ref[...])

    return pl.pallas_call(body, name=name, out_shape=jax.ShapeDtypeStruct(ca.shape, F32))(dsil, ca)


def _adam_math(w, g, m, v):
    c1 = 1.0 - ADAM_B1 ** ADAM_STEP
    c2 = 1.0 - ADAM_B2 ** ADAM_STEP
    mn = ADAM_B1 * m + (1.0 - ADAM_B1) * g
    vn = ADAM_B2 * v + (1.0 - ADAM_B2) * (g * g)
    return -ADAM_LR * ((mn / c1) / (jnp.sqrt(vn / c2) + ADAM_EPS) + ADAM_WD * w), mn, vn


def _adamw(w, g, m, v, name):
    Rw, Cw = w.shape
    tr = _pick(Rw, 128, 8)

    def body(w_ref, g_ref, m_ref, v_ref, d_ref, mo_ref, vo_ref):
        d_ref[...], mo_ref[...], vo_ref[...] = _adam_math(w_ref[...], g_ref[...], m_ref[...], v_ref[...])

    row = pl.BlockSpec((tr, Cw), lambda i: (i, 0))
    sh = jax.ShapeDtypeStruct((Rw, Cw), F32)
    return pl.pallas_call(body, name=name, grid=(Rw // tr,), in_specs=[row] * 4, out_specs=[row] * 3,
                          out_shape=[sh] * 3, compiler_params=_cparams("parallel"))(w, g, m, v)


HBM_SPEC = pl.BlockSpec(memory_space=pltpu.HBM)


def _exchange(inputs, out_shapes, stages, name):
    n_in, n_out = len(inputs), len(out_shapes)
    n = sum(len(s) for s in stages)

    def body(*refs):
        ins, outs = refs[:n_in], refs[n_in:n_in + n_out]
        send_sems, recv_sems = refs[n_in + n_out:]
        k = 0
        for stage in stages:
            copies = _stage_copies(stage, ins, outs, send_sems, recv_sems, k)
            for cp in copies:
                cp.start()
            for cp in copies:
                cp.wait()
            k += len(stage)

    return pl.pallas_call(
        body, name=name, in_specs=[HBM_SPEC] * n_in, out_specs=[HBM_SPEC] * n_out, out_shape=out_shapes,
        scratch_shapes=[pltpu.SemaphoreType.DMA((n,)), pltpu.SemaphoreType.DMA((n,))],
    )(*inputs)


def _stage_copies(stage, ins, outs, send_sems, recv_sems, k0=0):
    me = (lax.axis_index("x"), lax.axis_index("y"), lax.axis_index("c"))
    copies = []
    for k, ((skind, sidx), sfn, didx, dfn, flip) in enumerate(stage):
        src = (ins if skind == "in" else outs)[sidx].at[sfn(*me)]
        dst = outs[didx].at[dfn(*me)]
        if flip == (0, 0, 0):
            copies.append(pltpu.make_async_copy(src, dst, send_sems.at[k0 + k]))
        else:
            peer = tuple(1 - a if f else a for a, f in zip(me, flip))
            copies.append(pltpu.make_async_remote_copy(src, dst, send_sems.at[k0 + k], recv_sems.at[k0 + k],
                                                       device_id=peer, device_id_type=MESH))
    return copies


def _pcall(body, *, name, grid, in_specs, out_specs, out_shape, scratch_shapes=(), sem, args, ride=None):
    many = isinstance(out_shape, (list, tuple))
    out_specs, out_shape = (list(out_specs), list(out_shape)) if many else ([out_specs], [out_shape])
    if ride is None:
        res = pl.pallas_call(body, name=name, grid=grid, in_specs=list(in_specs), out_specs=out_specs,
                             out_shape=out_shape, scratch_shapes=list(scratch_shapes),
                             compiler_params=_cparams(*sem))(*args)
        return res if many else res[0]
    x_in, x_out, stage, aliases = ride
    n_in, n_out, n_scr, n_xin, n_xout = len(in_specs), len(out_specs), len(scratch_shapes), len(x_in), len(x_out)

    def wrapped(*refs):
        ins, xins = refs[:n_in], refs[n_in:n_in + n_xin]
        o0 = n_in + n_xin
        outs, xouts = refs[o0:o0 + n_out], refs[o0 + n_out:o0 + n_out + n_xout]
        s0 = o0 + n_out + n_xout
        scr, (send_sems, recv_sems) = refs[s0:s0 + n_scr], refs[s0 + n_scr:]
        first = functools.reduce(jnp.logical_and, [pl.program_id(d) == 0 for d in range(len(grid))])
        last = functools.reduce(jnp.logical_and, [pl.program_id(d) == grid[d] - 1 for d in range(len(grid))])

        @pl.when(first)
        def _():
            for cp in _stage_copies(stage, xins, xouts, send_sems, recv_sems):
                cp.start()

        body(*ins, *outs, *scr)

        @pl.when(last)
        def _():
            for cp in _stage_copies(stage, xins, xouts, send_sems, recv_sems):
                cp.wait()

    res = pl.pallas_call(
        wrapped, name=name, grid=grid, in_specs=list(in_specs) + [HBM_SPEC] * n_xin,
        out_specs=out_specs + [HBM_SPEC] * n_xout, out_shape=out_shape + list(x_out),
        scratch_shapes=list(scratch_shapes) + [pltpu.SemaphoreType.DMA((len(stage),)),
                                               pltpu.SemaphoreType.DMA((len(stage),))],
        input_output_aliases={n_in + a: n_out + b for a, b in aliases.items()},
        compiler_params=_cparams(*(["arbitrary"] * len(grid))))(*args, *x_in)
    main = res[:n_out]
    return (main if many else main[0]), list(res[n_out:])


FLIPS_ALL = [(0, 0, 1), (0, 1, 0), (0, 1, 1), (1, 0, 0), (1, 0, 1), (1, 1, 0), (1, 1, 1)]
FLIPS_CHIP = [(0, 1, 0), (1, 0, 0), (1, 1, 0)]


def _sum_slots(buf, name):
    n, r, w = buf.shape
    tr = _pick(r, 256, 8)

    def body(b_ref, o_ref):
        acc = b_ref[0]
        for s in range(1, n):
            acc = acc + b_ref[s]
        o_ref[...] = acc

    return pl.pallas_call(
        body, name=name, grid=(r // tr,), in_specs=[pl.BlockSpec((n, tr, w), lambda i: (0, i, 0))],
        out_specs=pl.BlockSpec((tr, w), lambda i: (i, 0)), out_shape=jax.ShapeDtypeStruct((r, w), F32),
        compiler_params=_cparams("parallel"))(buf)


def _allreduce(buf, name):
    r, w = buf.shape
    whole = lambda x, y, c: (slice(None), slice(None))
    slot = lambda x, y, c: (4 * x + 2 * y + c,)
    stage = [(("in", 0), whole, 0, slot, f) for f in [(0, 0, 0)] + FLIPS_ALL]
    (slots,) = _exchange([buf], [jax.ShapeDtypeStruct((8, r, w), F32)], [stage], name + "_x")
    return _sum_slots(slots, name + "_sum")


def _gather_plan(shards, src):
    half = lambda a, c: pl.ds(c * (a.shape[0] // 2), a.shape[0] // 2)
    first, second = [], []
    for n, a in enumerate(shards):
        for f in FLIPS_CHIP:
            first.append((("in", n), lambda x, y, c, a=a: (half(a, c), slice(None)), n,
                          lambda x, y, c, a=a: (2 * x + y, half(a, c), slice(None)), f))
            peer_slot = lambda x, y, c, a=a, f=f: (2 * (x ^ f[0]) + (y ^ f[1]), half(a, c), slice(None))
            second.append(((src, n), peer_slot, n, peer_slot, (0, 0, 1)))
    outs = [jax.ShapeDtypeStruct((4,) + a.shape, a.dtype) for a in shards]
    return first, second, outs


def _allgather_weights(shards, name):
    first, second, outs = _gather_plan(shards, "out")
    return _exchange(shards, outs, [first, second], name)


def _place_own(buf, shard, svec, name):
    _, Rs, Cs = buf.shape
    tr = _pick(Rs, 256, 16)

    def body(s_ref, buf_ref, sh_ref, o_ref):
        o_ref[0] = sh_ref[...]

    grid_spec = pltpu.PrefetchScalarGridSpec(
        num_scalar_prefetch=1, grid=(Rs // tr,),
        in_specs=[pl.BlockSpec(memory_space=pl.ANY), pl.BlockSpec((tr, Cs), lambda i, s: (i, 0))],
        out_specs=pl.BlockSpec((1, tr, Cs), lambda i, s: (s[0], i, 0)))
    return pl.pallas_call(body, name=name, grid_spec=grid_spec, out_shape=jax.ShapeDtypeStruct(buf.shape, buf.dtype),
                          input_output_aliases={1: 0}, compiler_params=_cparams("arbitrary"))(svec, buf, shard)


def _add_pair(G, bufA, cvec, name):
    _, Rs, Cs = G.shape
    Rh = Rs // 2
    tr = _pick(Rh, 128, 16)
    nb = Rh // tr

    def body(c_ref, g_ref, a_ref, o_ref):
        o_ref[...] = (g_ref[...] + a_ref[...]).astype(BF16)

    grid_spec = pltpu.PrefetchScalarGridSpec(
        num_scalar_prefetch=1, grid=(4, nb),
        in_specs=[pl.BlockSpec((1, tr, Cs), lambda s, i, c_ref: (s, c_ref[0] * nb + i, 0)),
                  pl.BlockSpec((1, tr, Cs), lambda s, i, c_ref: (s, i, 0))],
        out_specs=pl.BlockSpec((1, tr, Cs), lambda s, i, c_ref: (s, i, 0)))
    return pl.pallas_call(body, name=name, grid_spec=grid_spec, out_shape=jax.ShapeDtypeStruct((4, Rh, Cs), BF16),
                          compiler_params=_cparams("parallel", "parallel"))(cvec, G, bufA)


def _sum_chips(G, bufA, bufB, cvec, svec, name):
    _, Rs, Cs = G.shape
    Rh = Rs // 2
    tr = _pick(Rh, 128, 16)
    nb = Rh // tr

    def body(c_ref, s_ref, g_ref, a_ref, b_ref, o_ref):
        o_ref[...] = (g_ref[0] + a_ref[0]) + b_ref[0].astype(F32) + b_ref[1].astype(F32) + b_ref[2].astype(F32)

    grid_spec = pltpu.PrefetchScalarGridSpec(
        num_scalar_prefetch=2, grid=(nb,),
        in_specs=[pl.BlockSpec((1, tr, Cs), lambda i, c, s: (s[0], c[0] * nb + i, 0)),
                  pl.BlockSpec((1, tr, Cs), lambda i, c, s: (s[0], i, 0)),
                  pl.BlockSpec((3, tr, Cs), lambda i, c, s: (0, i, 0))],
        out_specs=pl.BlockSpec((tr, Cs), lambda i, c, s: (i, 0)))
    return pl.pallas_call(body, name=name, grid_spec=grid_spec, out_shape=jax.ShapeDtypeStruct((Rh, Cs), F32),
                          compiler_params=_cparams("parallel"))(cvec, svec, G, bufA, bufB)


def _pair_plan(grads):
    Rh = [g.shape[1] // 2 for g in grads]
    whole3 = lambda x, y, c: (slice(None), slice(None), slice(None))
    stage = [(("in", n), lambda x, y, c, n=n: (slice(None), pl.ds((1 - c) * Rh[n], Rh[n]), slice(None)), n,
              whole3, (0, 0, 1)) for n in range(len(grads))]
    return [jax.ShapeDtypeStruct((4, Rh[n], g.shape[2]), F32) for n, g in enumerate(grads)], stage


def _chips_plan(P):
    stage = [(("in", n), lambda x, y, c, f=f: (2 * (x ^ f[0]) + (y ^ f[1]),), n, lambda x, y, c, k=k: (k,), f)
             for n in range(len(P)) for k, f in enumerate(FLIPS_CHIP)]
    return [jax.ShapeDtypeStruct((3,) + p.shape[1:], BF16) for p in P], stage


def _halves_plan(mine):
    whole2 = lambda x, y, c: (slice(None), slice(None))
    stage = [(("in", n), whole2, n, whole2, (0, 0, 1)) for n in range(len(mine))]
    return [jax.ShapeDtypeStruct(r.shape, F32) for r in mine], stage


def _adamw_halves(w, mine, other, m, v, cvec, name):
    Rs, Cs = w.shape
    Rh = Rs // 2
    tr = _pick(Rh, 128, 8)
    nb = Rh // tr

    def body(c_ref, w_ref, a_ref, b_ref, m_ref, v_ref, g_ref, d_ref, mo_ref, vo_ref):
        gv = jnp.where(pl.program_id(0) // nb == c_ref[0], a_ref[...], b_ref[...])
        g_ref[...] = gv
        d_ref[...], mo_ref[...], vo_ref[...] = _adam_math(w_ref[...], gv, m_ref[...], v_ref[...])

    row = pl.BlockSpec((tr, Cs), lambda i, c: (i, 0))
    hrow = pl.BlockSpec((tr, Cs), lambda i, c: (i % nb, 0))
    grid_spec = pltpu.PrefetchScalarGridSpec(num_scalar_prefetch=1, grid=(2 * nb,),
                                             in_specs=[row, hrow, hrow, row, row], out_specs=[row] * 4)
    return pl.pallas_call(body, name=name, grid_spec=grid_spec, out_shape=[jax.ShapeDtypeStruct((Rs, Cs), F32)] * 4,
                          compiler_params=_cparams("parallel"))(cvec, w, mine, other, m, v)


def _pack(arrays):
    flat = [a.reshape(-1).astype(F32) for a in arrays]
    meta, off = [], 0
    for a, f in zip(arrays, flat):
        meta.append((off, a.shape))
        off += f.shape[0]
    total = -(-off // (8 * LANES)) * (8 * LANES)
    flat.append(jnp.zeros((total - off,), F32))
    return jnp.concatenate(flat).reshape(total // LANES, LANES), meta


def _unpack(buf, meta):
    flat = buf.reshape(-1)
    out = []
    for off, shape in meta:
        size = 1
        for s in shape:
            size *= s
        out.append(flat[off:off + size].reshape(shape))
    return out


WEIGHT_NAMES = ["c_ctx", "w_mod", "b_mod", "g_mix", "w_in", "q_norm", "k_norm", "attn_sink", "w_gate_f", "b_gate_f",
                "w_gate_b", "b_gate_b", "gla_norm", "w_attn_o", "w_gla_o", "w_out", "g_ffn", "w_up", "conv_w",
                "conv_b", "w_down"]
BIG_NAMES = ["w_in", "w_attn_o", "w_gla_o", "w_out", "w_up", "w_down"]
SHARDED_SMALL = ["w_gate_f", "w_gate_b", "conv_w"]


def _layouts(D):
    aw, kvw, gk, gv = N_Q_HEADS * HEAD_DIM, N_KV_HEADS * HEAD_DIM, D // 2, D
    widths = {"qa": aw, "ka": kvw, "va": kvw, "qb": gk, "kb": gk, "vb": gv, "rb": gv, "lr": 2 * GLA_LOWRANK,
              "ga": D, "gb": D}
    orig, off = {}, 0
    for s in ["qa", "ka", "va", "qb", "kb", "vb", "rb", "lr", "ga", "gb"]:
        orig[s] = off
        off += widths[s]
    order = ["qa", "vb", "rb", "ga", "gb", "ka", "va", "qb", "kb", "lr"]
    lay, off = {}, 0
    for s in order:
        lay[s] = off
        off += LANES if s == "lr" else widths[s]
    align = {"qa": aw, "vb": D, "rb": D, "ga": D, "gb": D, "ka": kvw, "va": kvw, "qb": gk, "kb": gk,
             "lr": LANES}
    for s in order:
        assert lay[s] % align[s] == 0, (s, lay[s], align[s])
    return widths, orig, order, lay, off


def _rope_tables(T, L):
    t = jnp.arange(T)
    nf = HEAD_DIM // 4
    inv = ROPE_THETA ** (-jnp.arange(nf, dtype=F32) / nf)
    ang = jnp.concatenate([(t // GRID_W)[:, None] * inv, (t % GRID_W)[:, None] * inv], axis=-1)
    cos, sin = jnp.cos(ang), jnp.sin(ang)
    cos2 = jnp.concatenate([jnp.ones((L, HEAD_DIM), F32), jnp.concatenate([cos, cos], axis=-1)], axis=0)
    sin2 = jnp.concatenate([jnp.zeros((L, HEAD_DIM), F32), jnp.concatenate([-sin, sin], axis=-1)], axis=0)
    return cos2, sin2


def _step(x, c, ctx, loss_target, W, M, V):
    xi, yi, ci = lax.axis_index("x"), lax.axis_index("y"), lax.axis_index("c")
    chip = 2 * xi + yi
    dev = 2 * chip + ci
    south = (ci == 0).astype(F32)
    cvec = ci.reshape(1).astype(jnp.int32)
    svec = chip.reshape(1).astype(jnp.int32)
    T, D = x.shape[1], x.shape[2]
    L = ctx.shape[1]
    R = L + T
    F = 4 * W["w_down"].shape[1]
    GK, GV = D // 2, D
    DK, DV = GK // GLA_HEADS, GV // GLA_HEADS
    N6 = 6 * D
    N4 = N6 // 4
    widths, orig, order, lay, Z = _layouts(D)

    def place_cols(shard, full_cols):
        cols = shard.shape[-1]
        full = jnp.zeros(shard.shape[:-1] + (full_cols,), F32)
        return lax.dynamic_update_slice(full, shard * south, (0,) * (shard.ndim - 1) + (chip * cols,))

    c_rows = lax.dynamic_update_slice(jnp.zeros((8, D), F32), c, (dev, 0))
    bufa, meta = _pack([c_rows, place_cols(W["w_gate_f"][0], GK), place_cols(W["w_gate_b"][0], GK),
                        place_cols(W["conv_w"][0], 2 * F)])
    c_all, wgf, wgb, cw = _unpack(_allreduce(bufa, "gather_small"), meta)
    ca = jnp.concatenate([c_all, W["c_ctx"][None, :], jnp.zeros((7, D), F32)], axis=0)
    b_shard = lax.dynamic_slice(W["b_mod"], (0, chip * N4), (1, N4))
    mod_part, sil = _mod_fwd(ca, W["w_mod"][0], b_shard, "mod_fwd")
    slots = lax.dynamic_update_slice(jnp.zeros((4, 16, N4), F32), (mod_part * south)[None], (chip, 0, 0))
    mod_all = _allreduce(slots.reshape(64, N4), "gather_mod").reshape(4, 16, N4).transpose(1, 0, 2).reshape(16, N6)
    mx = lax.dynamic_slice(mod_all, (dev, 0), (1, N6)).reshape(6, 1, D)
    mc = mod_all[8].reshape(6, 1, D)

    sq = lambda a: a.reshape(a.shape[1:])
    shards = [sq(W[n]).astype(BF16) for n in BIG_NAMES]
    own = lambda g, s, n: _place_own(g, s, svec, "place_" + n)
    cols = lambda g: g.transpose(1, 0, 2).reshape(g.shape[1], 4 * g.shape[2])
    rows = lambda g: g.reshape(4 * g.shape[1], g.shape[2])
    w_in_f = cols(own(_allgather_weights(shards[:1], "gather_w_in")[0], shards[0], "w_in"))
    seg = lambda s: w_in_f[:, orig[s]:orig[s] + widths[s]]
    w_cat = jnp.concatenate([jnp.pad(seg(s), ((0, 0), (0, LANES - widths[s]))) if s == "lr" else seg(s)
                             for s in order], axis=1)
    gather1, gather2, gather_outs = _gather_plan(shards[1:], "in")
    wg = jnp.zeros((2, LANES, GK), F32).at[0, :GLA_LOWRANK].set(wgf).at[1, GLA_LOWRANK:2 * GLA_LOWRANK].set(wgb)
    bg = jnp.stack([W["b_gate_f"], W["b_gate_b"]])
    cb = W["conv_b"]
    sink_rows = jnp.broadcast_to(W["attn_sink"][0][:, None], (N_Q_HEADS, HEAD_DIM))
    cos2, sin2 = _rope_tables(T, L)
    blk = lambda s, w: lay[s] // w

    xall = jnp.concatenate([ctx[0], x[0]], axis=0)
    sc1 = jnp.stack([mc[1], mx[1]])
    sh1 = jnp.stack([mc[0], mx[0]])
    h = _modnorm_fwd(xall, W["g_mix"], sc1, sh1, L, "modnorm1")
    z, landed = _matmul(h, w_cat, "nn", F32, "proj_in", ride=(shards[1:], gather_outs, gather1, {}))
    qn = _qknorm_fwd(z, blk("qa", widths["qa"]), T, L, W["q_norm"], cos2, sin2, N_Q_HEADS, "qnorm")
    kn = _qknorm_fwd(z, blk("ka", widths["ka"]), R, 0, W["k_norm"], cos2, sin2, N_KV_HEADS, "knorm")
    vb = _cast_seg(z, blk("va", widths["va"]), widths["va"], "vcast")
    o_attn, landed = _attn_fwd(qn, kn, vb, sink_rows, L, "attn_fwd",
                               ride=(landed, gather_outs, gather2, {n: n for n in range(len(landed))}))
    g_ao, g_go, g_out, g_up, g_dn = [own(g, s, n) for g, s, n in zip(landed, shards[1:], BIG_NAMES[1:])]
    w_ao, w_go, w_out, w_up, w_dn = rows(g_ao), rows(g_go), rows(g_out), cols(g_up), rows(g_dn)
    gla_blks = (blk("qb", GK), blk("kb", GK), blk("vb", GV), blk("lr", LANES))
    o_g, sprev = _gla_fwd(z, *gla_blks, wg, bg, DV, L, "gla_fwd")
    p = _glanorm_fwd(o_g, z, blk("rb", D), W["gla_norm"], L, "glanorm")
    ya = _matmul(o_attn, w_ao, "nn", F32, "proj_attn_o")
    yg = _matmul(p, w_go, "nn", F32, "proj_gla_o")
    m = _gate_fwd(z, blk("ga", D), blk("gb", D), ya, yg, L, "gate")
    mix = _matmul(m, w_out, "nn", F32, "proj_out")
    x1, h2 = _resnorm_fwd(x[0], mix, mx[2], W["g_ffn"], mx[4], mx[3], "resnorm2")
    u = _matmul(h2, w_up, "nn", F32, "ffn_up")
    f = _conv_fwd(u, cw, cb, "conv_swiglu")
    d = _matmul(f, w_dn, "nn", F32, "ffn_down")
    dy, dd, lacc = _loss_head(d, x1, mx[5], loss_target[0], "loss_head")
    loss = lax.psum((0.5 / D) * jnp.sum(lacc[0]), ("x", "y", "c"))

    gw_dn = _matmul(f, dd, "tn", F32, "ffn_down_dw")
    df = _matmul(dd, w_dn, "nt", F32, "ffn_down_dx")
    du_a, du_g, acca, accg = _conv_bwd(u, df, cw, cb, "conv_swiglu_bwd")
    du = jnp.concatenate([du_a, du_g], axis=1)
    gw_up = _matmul(h2, du, "tn", F32, "ffn_up_dw")
    dh2 = _matmul(du, w_up, "nt", F32, "ffn_up_dx")
    dx1, dmix, s2 = _resnorm_bwd(x1, dh2, W["g_ffn"], mx[4], dy, mix, mx[2], "resnorm2_bwd")
    gw_out = _matmul(m, dmix, "tn", F32, "proj_out_dw")
    dm = _matmul(dmix, w_out, "nt", F32, "proj_out_dx")
    dya, dyg, dga, dgb = _gate_bwd(z, blk("ga", D), blk("gb", D), ya, yg, dm, L, "gate_bwd")
    gw_ao = _matmul(o_attn, dya, "tn", F32, "proj_attn_o_dw")
    do_attn = _matmul(dya, w_ao, "nt", BF16, "proj_attn_o_dx")
    gw_go = _matmul(p, dyg, "tn", F32, "proj_gla_o_dw")
    dp = _matmul(dyg, w_go, "nt", F32, "proj_gla_o_dx")
    do_gla, drb, s_gn = _glanorm_bwd(o_g, z, blk("rb", D), W["gla_norm"], dp, L, "glanorm_bwd")
    do_pad = jnp.concatenate([jnp.zeros((L, GV), BF16), do_gla], axis=0)
    by_cols = lambda g: g.reshape(g.shape[0], 4, g.shape[1] // 4).transpose(1, 0, 2)
    by_rows = lambda g: g.reshape(4, g.shape[0] // 4, g.shape[1])
    early = [by_rows(gw_ao), by_rows(gw_go), by_rows(gw_out), by_cols(gw_up), by_rows(gw_dn)]
    (dqg, dkg, dvg, dpre, dbg), pair_e = _gla_bwd(z, *gla_blks, wg, bg, sprev, do_pad, L, "gla_bwd",
                                                  ride=(early, *_pair_plan(early), {}))
    sums_e = [_add_pair(g, a, cvec, "reduce_early_add%d" % n) for n, (g, a) in enumerate(zip(early, pair_e))]
    wg_cat = jnp.concatenate([wg[0], wg[1]], axis=1)
    dlr = _matmul(dpre, wg_cat, "nt", BF16, "gla_gate_dx")
    dwg = _matmul(z[:, lay["lr"]:lay["lr"] + LANES], dpre, "tn", F32, "gla_gate_dw")
    (dqn, dkw, dvw, dkc, dvc, dsn), chips_e = _attn_bwd(qn, kn, vb, sink_rows, do_attn, L, "attn_bwd",
                                                        ride=(sums_e, *_chips_plan(sums_e), {}))
    mine_e = [_sum_chips(g, a, b, cvec, svec, "reduce_early_sum%d" % n)
              for n, (g, a, b) in enumerate(zip(early, pair_e, chips_e))]
    dqa, s_qn = _qknorm_bwd(z, blk("qa", widths["qa"]), T, L, W["q_norm"], cos2, sin2, dqn, N_Q_HEADS, "qnorm_bwd")
    dk_all = jnp.concatenate([dkc, dkw[WINDOW:WINDOW + T]], axis=0)
    dv_all = jnp.concatenate([dvc, dvw[WINDOW:WINDOW + T]], axis=0)
    dka, s_kn = _qknorm_bwd(z, blk("ka", widths["ka"]), R, 0, W["k_norm"], cos2, sin2, dk_all, N_KV_HEADS, "knorm_bwd")
    dz = _assemble_dz(lay, Z, L, dqa, drb, dga, dgb, dka, dv_all, dvg, dqg, dkg, dlr, "assemble_dz")
    gw_cat, other_e = _matmul(h, dz, "tn", F32, "proj_in_dw", ride=(mine_e, *_halves_plan(mine_e), {}))
    gw_in = jnp.concatenate([gw_cat[:, lay[s]:lay[s] + widths[s]] for s in ["qa", "ka", "va", "qb", "kb", "vb", "rb",
                                                                           "lr", "ga", "gb"]], axis=1)
    late = [by_cols(gw_in)]
    shapes, stage = _pair_plan(late)
    pair_l = _exchange(late, shapes, [stage], "reduce_late_pair")
    sums_l = [_add_pair(late[0], pair_l[0], cvec, "reduce_late_add")]
    dh, chips_l = _matmul(dz, w_cat, "nt", F32, "proj_in_dx", ride=(sums_l, *_chips_plan(sums_l), {}))
    mine_l = [_sum_chips(late[0], pair_l[0], chips_l[0], cvec, svec, "reduce_late_sum")]
    shapes, stage = _halves_plan(mine_l)
    other_l = _exchange(mine_l, shapes, [stage], "reduce_late_halves")
    mine, other = mine_l + mine_e, list(other_l) + other_e
    grad_x, s1 = _modnorm_bwd(x[0], dh, W["g_mix"], mx[1], dx1, "modnorm1_bwd", dh_roff=L)
    _, s1c = _modnorm_bwd(ctx[0], dh, W["g_mix"], mc[1], None, "modnorm1_ctx_bwd")

    dmod_x = jnp.concatenate([s1[0], s1[1], s2[3], s2[0], s2[1], lacc[1]])
    dmod_c = jnp.concatenate([s1c[0], s1c[1], jnp.zeros((4 * D,), F32)])
    dmod_rows = lax.dynamic_update_slice(jnp.zeros((9, N6), F32).at[8].set(dmod_c), dmod_x[None], (dev, 0))
    small = [dmod_rows, dmod_x + dmod_c, s1[2] + s1c[2], s_qn[0], s_kn[0], dsn[:, 0, :Q_PER_KV].reshape(N_Q_HEADS),
             dwg[:GLA_LOWRANK, :GK], dbg[0].reshape(GK), dwg[GLA_LOWRANK:2 * GLA_LOWRANK, GK:], dbg[1].reshape(GK),
             s_gn[0], s2[2], jnp.concatenate([acca[0:3], accg[0:3]], axis=1), jnp.concatenate([acca[3], accg[3]])]
    bufc, meta = _pack(small)
    (dmod_sum, g_b_mod, g_g_mix, g_q_norm, g_k_norm, g_sink, g_wgf, g_bgf, g_wgb, g_bgb, g_gla_norm, g_g_ffn,
     g_conv_w, g_conv_b) = _unpack(_allreduce(bufc, "reduce_small"), meta)
    dmod16 = lax.dynamic_slice(jnp.concatenate([dmod_sum, jnp.zeros((7, N6), F32)], axis=0), (0, chip * N4), (16, N4))
    g_w_mod = _matmul(sil, dmod16, "tn", F32, "mod_dw")
    dsil = _matmul(dmod16, W["w_mod"][0], "nt", F32, "mod_dx")
    g_c_ctx = _silu_bwd(_allreduce(dsil * south, "reduce_cctx"), ca, "silu_bwd")[8]

    cut = lambda g: lax.dynamic_slice(g, (0, chip * (g.shape[1] // 4)), (g.shape[0], g.shape[1] // 4))
    grads = {"c_ctx": g_c_ctx, "w_mod": g_w_mod[None], "b_mod": g_b_mod[None], "g_mix": g_g_mix[None],
             "q_norm": g_q_norm[None], "k_norm": g_k_norm[None], "attn_sink": g_sink[None],
             "w_gate_f": cut(g_wgf)[None], "b_gate_f": g_bgf[None], "w_gate_b": cut(g_wgb)[None],
             "b_gate_b": g_bgb[None], "gla_norm": g_gla_norm[None], "g_ffn": g_g_ffn[None],
             "conv_w": cut(g_conv_w)[None], "conv_b": g_conv_b[None]}

    delta, new_m, new_v = {}, {}, {}
    dl, mn, vn = _adamw(W["w_mod"][0], g_w_mod, M["w_mod"][0], V["w_mod"][0], "adamw_w_mod")
    delta["w_mod"], new_m["w_mod"], new_v["w_mod"] = dl[None], mn[None], vn[None]
    for n, a, b in zip(BIG_NAMES, mine, other):
        g, dl, mn, vn = _adamw_halves(sq(W[n]), a, b, sq(M[n]), sq(V[n]), cvec, "adamw_" + n)
        grads[n], delta[n], new_m[n], new_v[n] = g[None], dl[None], mn[None], vn[None]
    small_names = [n for n in WEIGHT_NAMES if n not in delta]
    packs = [_pack([src[n] for n in small_names]) for src in (W, grads, M, V)]
    meta = packs[0][1]
    outs = _adamw(packs[0][0], packs[1][0], packs[2][0], packs[3][0], "adamw_small")
    for res, o in zip((delta, new_m, new_v), outs):
        for n, a in zip(small_names, _unpack(o, meta)):
            res[n] = a
    return (loss, grad_x[None], *[grads[n] for n in WEIGHT_NAMES], *[delta[n] for n in WEIGHT_NAMES],
            *[new_m[n] for n in WEIGHT_NAMES], *[new_v[n] for n in WEIGHT_NAMES])


def kernel(x, c, ctx, c_ctx, w_mod, b_mod, g_mix, w_in, q_norm, k_norm, attn_sink, w_gate_f, b_gate_f, w_gate_b, b_gate_b, gla_norm, w_attn_o, w_gla_o, w_out, g_ffn, w_up, conv_w, conv_b, w_down, loss_target, m_c_ctx, m_w_mod, m_b_mod, m_g_mix, m_w_in, m_q_norm, m_k_norm, m_attn_sink, m_w_gate_f, m_b_gate_f, m_w_gate_b, m_b_gate_b, m_gla_norm, m_w_attn_o, m_w_gla_o, m_w_out, m_g_ffn, m_w_up, m_conv_w, m_conv_b, m_w_down, v_c_ctx, v_w_mod, v_b_mod, v_g_mix, v_w_in, v_q_norm, v_k_norm, v_attn_sink, v_w_gate_f, v_b_gate_f, v_w_gate_b, v_b_gate_b, v_gla_norm, v_w_attn_o, v_w_gla_o, v_w_out, v_g_ffn, v_w_up, v_conv_w, v_conv_b, v_w_down):
    W = dict(zip(WEIGHT_NAMES, (c_ctx, w_mod, b_mod, g_mix, w_in, q_norm, k_norm, attn_sink, w_gate_f, b_gate_f,
                                w_gate_b, b_gate_b, gla_norm, w_attn_o, w_gla_o, w_out, g_ffn, w_up, conv_w, conv_b,
                                w_down)))
    M = dict(zip(WEIGHT_NAMES, (m_c_ctx, m_w_mod, m_b_mod, m_g_mix, m_w_in, m_q_norm, m_k_norm, m_attn_sink,
                                m_w_gate_f, m_b_gate_f, m_w_gate_b, m_b_gate_b, m_gla_norm, m_w_attn_o, m_w_gla_o,
                                m_w_out, m_g_ffn, m_w_up, m_conv_w, m_conv_b, m_w_down)))
    V = dict(zip(WEIGHT_NAMES, (v_c_ctx, v_w_mod, v_b_mod, v_g_mix, v_w_in, v_q_norm, v_k_norm, v_attn_sink,
                                v_w_gate_f, v_b_gate_f, v_w_gate_b, v_b_gate_b, v_gla_norm, v_w_attn_o, v_w_gla_o,
                                v_w_out, v_g_ffn, v_w_up, v_conv_w, v_conv_b, v_w_down)))
    return _step(x, c, ctx, loss_target, W, M, V)
```

```python
import functools
import math

import jax
import jax.numpy as jnp
from jax import lax
from jax.experimental import pallas as pl
from jax.experimental.pallas import tpu as pltpu

F32 = jnp.float32
BF16 = jnp.bfloat16
MESH = pl.DeviceIdType.MESH

EPS = 1e-6
HEAD_DIM = 128
N_Q_HEADS = 16
N_KV_HEADS = 4
Q_PER_KV = N_Q_HEADS // N_KV_HEADS
WINDOW = 128
GLA_HEADS = 4
GLA_LOWRANK = 16
GLA_GATE_NORM = 16.0
GLA_CHUNK = 64
GRID_W = 64
ROPE_THETA = 10000.0
GLA_LEVELS = (32, 16, 8, 4, 2, 1)
LANES = 128
MXU_TILE = 256

ADAM_LR = 0.001
ADAM_B1 = 0.9
ADAM_B2 = 0.999
ADAM_EPS = 1e-08
ADAM_WD = 0.01
ADAM_STEP = 10

VMEM_LIMIT = 52 * 1024 * 1024


def _cparams(*sem):
    return pltpu.CompilerParams(dimension_semantics=sem, vmem_limit_bytes=VMEM_LIMIT)


def _pick(n, target, mult=LANES):
    best = None
    d = mult
    while d <= min(n, target):
        if n % d == 0:
            best = d
        d += mult
    return n if best is None else best


def _sigmoid(x):
    return 1.0 / (1.0 + jnp.exp(-x))


def _silu(x):
    return x * _sigmoid(x)


def _dsilu(x):
    s = _sigmoid(x)
    return s * (1.0 + x * (1.0 - s))


def _dot(a, b, dims):
    return lax.dot_general(a, b, (dims, ((), ())), preferred_element_type=F32)


NN = ((1,), (0,))
NT = ((1,), (1,))
TN = ((0,), (0,))


def _matmul(a, b, mode, out_dtype, name, tm=1024, tn=1024, tk=2048, ride=None):
    if mode == "nn":
        (M, K), (K2, N) = a.shape, b.shape
    elif mode == "nt":
        (M, K), (N, K2) = a.shape, b.shape
    else:
        (K, M), (K2, N) = a.shape, b.shape
    assert K == K2, (name, a.shape, b.shape)
    tm, tn, tk = [_pick(n, t, MXU_TILE) if n % MXU_TILE == 0 else _pick(n, t) for n, t in ((M, tm), (N, tn), (K, tk))]
    nk = K // tk
    dims = {"nn": NN, "nt": NT, "tn": TN}[mode]

    def body(a_ref, b_ref, o_ref, acc_ref):
        k = pl.program_id(2)

        @pl.when(k == 0)
        def _():
            acc_ref[...] = jnp.zeros_like(acc_ref)

        acc_ref[...] += _dot(a_ref[...].astype(BF16), b_ref[...].astype(BF16), dims)

        @pl.when(k == nk - 1)
        def _():
            o_ref[...] = acc_ref[...].astype(out_dtype)

    if mode == "tn":
        a_spec = pl.BlockSpec((tk, tm), lambda i, j, k: (k, i))
    else:
        a_spec = pl.BlockSpec((tm, tk), lambda i, j, k: (i, k))
    if mode == "nt":
        b_spec = pl.BlockSpec((tn, tk), lambda i, j, k: (j, k))
    else:
        b_spec = pl.BlockSpec((tk, tn), lambda i, j, k: (k, j))
    return _pcall(
        body, name=name, grid=(M // tm, N // tn, nk),
        in_specs=[a_spec, b_spec],
        out_specs=pl.BlockSpec((tm, tn), lambda i, j, k: (i, j)),
        out_shape=jax.ShapeDtypeStruct((M, N), out_dtype),
        scratch_shapes=[pltpu.VMEM((tm, tn), F32)],
        sem=("parallel", "parallel", "arbitrary"), args=(a, b), ride=ride)


def _modnorm_fwd(xall, g, sc, sh, n_ctx, name):
    R, D = xall.shape
    tm = _pick(n_ctx, 256, 8)
    cb = n_ctx // tm

    def body(x_ref, g_ref, sc_ref, sh_ref, h_ref):
        x = x_ref[...]
        r = lax.rsqrt(jnp.mean(x * x, axis=-1, keepdims=True) + EPS)
        n = x * r * g_ref[...]
        h_ref[...] = (n * (1.0 + sc_ref[0]) + sh_ref[0]).astype(BF16)

    sel = lambda i: (jnp.where(i < cb, 0, 1), 0, 0)
    return pl.pallas_call(
        body, name=name, grid=(R // tm,),
        in_specs=[pl.BlockSpec((tm, D), lambda i: (i, 0)), pl.BlockSpec((1, D), lambda i: (0, 0)),
                  pl.BlockSpec((1, 1, D), sel), pl.BlockSpec((1, 1, D), sel)],
        out_specs=pl.BlockSpec((tm, D), lambda i: (i, 0)),
        out_shape=jax.ShapeDtypeStruct((R, D), BF16),
        compiler_params=_cparams("parallel"),
    )(xall, g, sc, sh)


def _modnorm_bwd(x, dh, g, sc, resid, name, dh_roff=0):
    N, D = x.shape
    tm = _pick(math.gcd(N, dh_roff), 256, 8)
    ro = dh_roff // tm
    want_dx = resid is not None

    def body(*refs):
        if want_dx:
            x_ref, dh_ref, g_ref, sc_ref, res_ref, dx_ref, acc_ref = refs
        else:
            x_ref, dh_ref, g_ref, sc_ref, acc_ref = refs
        i = pl.program_id(0)

        @pl.when(i == 0)
        def _():
            acc_ref[...] = jnp.zeros_like(acc_ref)

        xv, dhv, gv = x_ref[...], dh_ref[...], g_ref[...]
        r = lax.rsqrt(jnp.mean(xv * xv, axis=-1, keepdims=True) + EPS)
        xh = xv * r
        dn = dhv * (1.0 + sc_ref[...])
        acc_ref[0:1, :] += jnp.sum(dhv, axis=0, keepdims=True)
        acc_ref[1:2, :] += jnp.sum(dhv * xh * gv, axis=0, keepdims=True)
        acc_ref[2:3, :] += jnp.sum(dn * xh, axis=0, keepdims=True)
        if want_dx:
            dxh = dn * gv
            dx_ref[...] = res_ref[...] + r * (dxh - xh * jnp.mean(dxh * xh, axis=-1, keepdims=True))

    row = pl.BlockSpec((tm, D), lambda i: (i, 0))
    drow = pl.BlockSpec((tm, D), lambda i: (i + ro, 0))
    vec = pl.BlockSpec((1, D), lambda i: (0, 0))
    acc = pl.BlockSpec((8, D), lambda i: (0, 0))
    acc_shape = jax.ShapeDtypeStruct((8, D), F32)
    if want_dx:
        return pl.pallas_call(
            body, name=name, grid=(N // tm,), in_specs=[row, drow, vec, vec, row],
            out_specs=[row, acc], out_shape=[jax.ShapeDtypeStruct((N, D), F32), acc_shape],
            compiler_params=_cparams("arbitrary"))(x, dh, g, sc, resid)
    sums = pl.pallas_call(
        body, name=name, grid=(N // tm,), in_specs=[row, drow, vec, vec],
        out_specs=acc, out_shape=acc_shape, compiler_params=_cparams("arbitrary"))(x, dh, g, sc)
    return None, sums


def _resnorm_bwd(x1, dh, g, sc, dy, mix, gt, name):
    N, D = x1.shape
    tm = _pick(N, 256, 8)

    def body(x_ref, dh_ref, g_ref, sc_ref, dy_ref, mix_ref, gt_ref, dx_ref, dm_ref, acc_ref):
        i = pl.program_id(0)

        @pl.when(i == 0)
        def _():
            acc_ref[...] = jnp.zeros_like(acc_ref)

        xv, dhv, gv = x_ref[...], dh_ref[...], g_ref[...]
        r = lax.rsqrt(jnp.mean(xv * xv, axis=-1, keepdims=True) + EPS)
        xh = xv * r
        dn = dhv * (1.0 + sc_ref[...])
        dxh = dn * gv
        dx = dy_ref[...] + r * (dxh - xh * jnp.mean(dxh * xh, axis=-1, keepdims=True))
        dx_ref[...] = dx
        dm_ref[...] = (dx * gt_ref[...]).astype(BF16)
        acc_ref[0:1, :] += jnp.sum(dhv, axis=0, keepdims=True)
        acc_ref[1:2, :] += jnp.sum(dhv * xh * gv, axis=0, keepdims=True)
        acc_ref[2:3, :] += jnp.sum(dn * xh, axis=0, keepdims=True)
        acc_ref[3:4, :] += jnp.sum(dx * mix_ref[...], axis=0, keepdims=True)

    row = pl.BlockSpec((tm, D), lambda i: (i, 0))
    vec = pl.BlockSpec((1, D), lambda i: (0, 0))
    return pl.pallas_call(
        body, name=name, grid=(N // tm,), in_specs=[row, row, vec, vec, row, row, vec],
        out_specs=[row, row, pl.BlockSpec((8, D), lambda i: (0, 0))],
        out_shape=[jax.ShapeDtypeStruct((N, D), F32), jax.ShapeDtypeStruct((N, D), BF16),
                   jax.ShapeDtypeStruct((8, D), F32)],
        compiler_params=_cparams("arbitrary"))(x1, dh, g, sc, dy, mix, gt)


def _qknorm_fwd(z, cblk, nrows, roff, w, cos2, sin2, nh, name):
    W = nh * HEAD_DIM
    tm = _pick(math.gcd(nrows, roff), 256, 8)
    ro = roff // tm
    assert roff % tm == 0

    def body(z_ref, w_ref, c_ref, s_ref, o_ref):
        c, s, wv = c_ref[...], s_ref[...], w_ref[...]
        for h in range(nh):
            x = z_ref[:, h * HEAD_DIM:(h + 1) * HEAD_DIM]
            r = lax.rsqrt(jnp.mean(x * x, axis=-1, keepdims=True) + EPS)
            y = x * r * wv
            o_ref[:, h * HEAD_DIM:(h + 1) * HEAD_DIM] = (y * c + pltpu.roll(y, HEAD_DIM // 2, 1) * s).astype(BF16)

    return pl.pallas_call(
        body, name=name, grid=(nrows // tm,),
        in_specs=[pl.BlockSpec((tm, W), lambda i: (i + ro, cblk)), pl.BlockSpec((1, HEAD_DIM), lambda i: (0, 0)),
                  pl.BlockSpec((tm, HEAD_DIM), lambda i: (i + ro, 0)), pl.BlockSpec((tm, HEAD_DIM), lambda i: (i + ro, 0))],
        out_specs=pl.BlockSpec((tm, W), lambda i: (i, 0)),
        out_shape=jax.ShapeDtypeStruct((nrows, W), BF16),
        compiler_params=_cparams("parallel"),
    )(z, w, cos2, sin2)


def _qknorm_bwd(z, cblk, nrows, roff, w, cos2, sin2, dy, nh, name):
    W = nh * HEAD_DIM
    tm = _pick(math.gcd(nrows, roff), 256, 8)
    ro = roff // tm

    def body(z_ref, w_ref, c_ref, s_ref, dy_ref, dz_ref, acc_ref):
        i = pl.program_id(0)

        @pl.when(i == 0)
        def _():
            acc_ref[...] = jnp.zeros_like(acc_ref)

        c, s, wv = c_ref[...], s_ref[...], w_ref[...]
        dw = jnp.zeros((1, HEAD_DIM), F32)
        for h in range(nh):
            sl = slice(h * HEAD_DIM, (h + 1) * HEAD_DIM)
            x = z_ref[:, sl]
            d = dy_ref[:, sl]
            dyn = d * c + pltpu.roll(d * s, HEAD_DIM // 2, 1)
            r = lax.rsqrt(jnp.mean(x * x, axis=-1, keepdims=True) + EPS)
            xh = x * r
            dw = dw + jnp.sum(dyn * xh, axis=0, keepdims=True)
            dxh = dyn * wv
            dz_ref[:, sl] = (r * (dxh - xh * jnp.mean(dxh * xh, axis=-1, keepdims=True))).astype(BF16)
        acc_ref[0:1, :] += dw

    return pl.pallas_call(
        body, name=name, grid=(nrows // tm,),
        in_specs=[pl.BlockSpec((tm, W), lambda i: (i + ro, cblk)), pl.BlockSpec((1, HEAD_DIM), lambda i: (0, 0)),
                  pl.BlockSpec((tm, HEAD_DIM), lambda i: (i + ro, 0)), pl.BlockSpec((tm, HEAD_DIM), lambda i: (i + ro, 0)),
                  pl.BlockSpec((tm, W), lambda i: (i, 0))],
        out_specs=[pl.BlockSpec((tm, W), lambda i: (i, 0)), pl.BlockSpec((8, HEAD_DIM), lambda i: (0, 0))],
        out_shape=[jax.ShapeDtypeStruct((nrows, W), BF16), jax.ShapeDtypeStruct((8, HEAD_DIM), F32)],
        compiler_params=_cparams("arbitrary"),
    )(z, w, cos2, sin2, dy)


def _cast_seg(z, cblk, width, name):
    R = z.shape[0]
    tm = _pick(R, 512, 8)

    def body(z_ref, o_ref):
        o_ref[...] = z_ref[...].astype(BF16)

    return pl.pallas_call(
        body, name=name, grid=(R // tm,),
        in_specs=[pl.BlockSpec((tm, width), lambda i: (i, cblk))],
        out_specs=pl.BlockSpec((tm, width), lambda i: (i, 0)),
        out_shape=jax.ShapeDtypeStruct((R, width), BF16), compiler_params=_cparams("parallel"))(z)


NEG_BIG = -1e30


KV_PER_STEP = 2


def _attn_specs(T, n_ctx):
    nb = T // WINDOW
    lb = n_ctx // WINDOW
    kvw = KV_PER_STEP * HEAD_DIM
    blk = lambda f: pl.BlockSpec((WINDOW, kvw), f)
    win = [blk(lambda h, i: (lb + jnp.maximum(i - 1, 0), h)), blk(lambda h, i: (lb + i, h)),
           blk(lambda h, i: (lb + jnp.minimum(i + 1, nb - 1), h))]
    ctx = pl.BlockSpec((n_ctx, kvw), lambda h, i: (0, h))
    qspec = pl.BlockSpec((WINDOW, KV_PER_STEP * Q_PER_KV * HEAD_DIM), lambda h, i: (i, h))
    sink = pl.BlockSpec((N_Q_HEADS, HEAD_DIM), lambda h, i: (0, 0))
    return nb, qspec, win, ctx, sink


def _attn_probs(q, kw, kctx, snk, valid):
    scale = HEAD_DIM ** -0.5
    s_lat = jnp.where(valid, _dot(q, kw, NT) * scale, NEG_BIG)
    s_ctx = _dot(q, kctx, NT) * scale
    m = jnp.maximum(jnp.maximum(jnp.max(s_lat, axis=-1, keepdims=True), jnp.max(s_ctx, axis=-1, keepdims=True)), snk)
    p_lat = jnp.exp(s_lat - m)
    p_ctx = jnp.exp(s_ctx - m)
    p_snk = jnp.exp(snk - m)
    den = p_snk + jnp.sum(p_lat, axis=-1, keepdims=True) + jnp.sum(p_ctx, axis=-1, keepdims=True)
    return p_lat, p_ctx, p_snk, den


def _attn_valid(i, T, heads):
    rows = heads * WINDOW
    qpos = i * WINDOW + (lax.broadcasted_iota(jnp.int32, (rows, 3 * WINDOW), 0) & (WINDOW - 1))
    kpos = (i - 1) * WINDOW + lax.broadcasted_iota(jnp.int32, (rows, 3 * WINDOW), 1)
    return (jnp.abs(qpos - kpos) <= WINDOW) & (kpos >= 0) & (kpos < T)


def _stack_heads(ref, hh):
    c0 = hh * Q_PER_KV * HEAD_DIM
    return jnp.concatenate([ref[:, c0 + g * HEAD_DIM:c0 + (g + 1) * HEAD_DIM] for g in range(Q_PER_KV)], axis=0)


def _stack_sinks(sink_ref, kvh):
    return jnp.concatenate([jnp.broadcast_to(sink_ref[pl.ds(kvh * Q_PER_KV + g, 1), :][:, 0:1], (WINDOW, 1))
                            for g in range(Q_PER_KV)], axis=0)


def _attn_window(refs, hh):
    return jnp.concatenate([r[:, hh * HEAD_DIM:(hh + 1) * HEAD_DIM] for r in refs], axis=0)


def _attn_fwd(qn, kn, vb, sink_rows, n_ctx, name, ride=None):
    T = qn.shape[0]
    nb, qspec, win, ctx, sink = _attn_specs(T, n_ctx)

    def body(q_ref, kp, kc, kx, vp, vc, vx, kctx_ref, vctx_ref, sink_ref, o_ref):
        h, i = pl.program_id(0), pl.program_id(1)
        valid = _attn_valid(i, T, Q_PER_KV)
        for hh in range(KV_PER_STEP):
            sl = slice(hh * HEAD_DIM, (hh + 1) * HEAD_DIM)
            kw, vw = _attn_window((kp, kc, kx), hh), _attn_window((vp, vc, vx), hh)
            kctx, vctx = kctx_ref[:, sl], vctx_ref[:, sl]
            p_lat, p_ctx, _, den = _attn_probs(_stack_heads(q_ref, hh), kw, kctx,
                                               _stack_sinks(sink_ref, h * KV_PER_STEP + hh), valid)
            o = ((_dot(p_lat.astype(BF16), vw, NN) + _dot(p_ctx.astype(BF16), vctx, NN)) / den).astype(BF16)
            for g in range(Q_PER_KV):
                c0 = (hh * Q_PER_KV + g) * HEAD_DIM
                o_ref[:, c0:c0 + HEAD_DIM] = o[g * WINDOW:(g + 1) * WINDOW]

    return _pcall(
        body, name=name, grid=(N_KV_HEADS // KV_PER_STEP, nb),
        in_specs=[qspec] + win + win + [ctx, ctx, sink],
        out_specs=qspec, out_shape=jax.ShapeDtypeStruct(qn.shape, BF16),
        sem=("parallel", "parallel"), args=(qn, kn, kn, kn, vb, vb, vb, kn, vb, sink_rows), ride=ride)


def _attn_bwd(qn, kn, vb, sink_rows, do, n_ctx, name, ride=None):
    T = qn.shape[0]
    nb, qspec, win, ctx, sink = _attn_specs(T, n_ctx)
    scale = HEAD_DIM ** -0.5
    TP = T + 2 * WINDOW

    def body(q_ref, kp, kc, kx, vp, vc, vx, kctx_ref, vctx_ref, sink_ref, do_ref,
             dq_ref, dkw_ref, dvw_ref, dkc_ref, dvc_ref, dsn_ref):
        h, i = pl.program_id(0), pl.program_id(1)

        @pl.when(i == 0)
        def _():
            dkw_ref[...] = jnp.zeros_like(dkw_ref)
            dvw_ref[...] = jnp.zeros_like(dvw_ref)
            dkc_ref[...] = jnp.zeros_like(dkc_ref)
            dvc_ref[...] = jnp.zeros_like(dvc_ref)
            dsn_ref[...] = jnp.zeros_like(dsn_ref)

        lane = lax.broadcasted_iota(jnp.int32, (8, HEAD_DIM), 1)
        valid = _attn_valid(i, T, Q_PER_KV)
        rows = pl.ds(pl.multiple_of(i * WINDOW, WINDOW), 3 * WINDOW)
        for hh in range(KV_PER_STEP):
            sl = slice(hh * HEAD_DIM, (hh + 1) * HEAD_DIM)
            kw, vw = _attn_window((kp, kc, kx), hh), _attn_window((vp, vc, vx), hh)
            kctx, vctx = kctx_ref[:, sl], vctx_ref[:, sl]
            q, d_o = _stack_heads(q_ref, hh), _stack_heads(do_ref, hh)
            p_lat, p_ctx, p_snk, den = _attn_probs(q, kw, kctx, _stack_sinks(sink_ref, h * KV_PER_STEP + hh), valid)
            inv = 1.0 / den
            p_lat, p_ctx, p_snk = p_lat * inv, p_ctx * inv, p_snk * inv
            dp_lat = _dot(d_o, vw, NT)
            dp_ctx = _dot(d_o, vctx, NT)
            dr = jnp.sum(p_lat * dp_lat, axis=-1, keepdims=True) + jnp.sum(p_ctx * dp_ctx, axis=-1, keepdims=True)
            ds_lat = (p_lat * (dp_lat - dr) * scale).astype(BF16)
            ds_ctx = (p_ctx * (dp_ctx - dr) * scale).astype(BF16)
            dq = _dot(ds_lat, kw, NN) + _dot(ds_ctx, kctx, NN)
            snk_terms = p_snk * dr
            dsn = jnp.zeros((8, HEAD_DIM), F32)
            for g in range(Q_PER_KV):
                c0 = (hh * Q_PER_KV + g) * HEAD_DIM
                dq_ref[:, c0:c0 + HEAD_DIM] = dq[g * WINDOW:(g + 1) * WINDOW]
                dsn = dsn + jnp.where(lane == g, -jnp.sum(snk_terms[g * WINDOW:(g + 1) * WINDOW], axis=0, keepdims=True),
                                      0.0)
            dkw_ref[rows, sl] += _dot(ds_lat, q, TN)
            dvw_ref[rows, sl] += _dot(p_lat.astype(BF16), d_o, TN)
            dkc_ref[:, sl] += _dot(ds_ctx, q, TN)
            dvc_ref[:, sl] += _dot(p_ctx.astype(BF16), d_o, TN)
            dsn_ref[hh] += dsn

    wacc = pl.BlockSpec((TP, KV_PER_STEP * HEAD_DIM), lambda h, i: (0, h))
    return _pcall(
        body, name=name, grid=(N_KV_HEADS // KV_PER_STEP, nb),
        in_specs=[qspec] + win + win + [ctx, ctx, sink, qspec],
        out_specs=[qspec, wacc, wacc, ctx, ctx, pl.BlockSpec((KV_PER_STEP, 8, HEAD_DIM), lambda h, i: (h, 0, 0))],
        out_shape=[jax.ShapeDtypeStruct(qn.shape, F32),
                   jax.ShapeDtypeStruct((TP, N_KV_HEADS * HEAD_DIM), F32),
                   jax.ShapeDtypeStruct((TP, N_KV_HEADS * HEAD_DIM), F32),
                   jax.ShapeDtypeStruct((n_ctx, N_KV_HEADS * HEAD_DIM), F32),
                   jax.ShapeDtypeStruct((n_ctx, N_KV_HEADS * HEAD_DIM), F32),
                   jax.ShapeDtypeStruct((N_KV_HEADS, 8, HEAD_DIM), F32)],
        sem=("arbitrary", "arbitrary"), args=(qn, kn, kn, kn, vb, vb, vb, kn, vb, sink_rows, do), ride=ride)


def _gla_masks(dirv):
    C = GLA_CHUNK

    def times(reps):
        r = lax.broadcasted_iota(jnp.int32, (C, reps * C), 0)
        c = lax.broadcasted_iota(jnp.int32, (C, reps * C), 1) & (C - 1)
        return jnp.where(dirv == 0, r, C - 1 - r), jnp.where(dirv == 0, c, C - 1 - c)

    def level(tt, ss, m):
        sh = m.bit_length() - 1
        same = (tt >> (sh + 1)) == (ss >> (sh + 1))
        return same, (tt >> sh) & 1, (ss >> sh) & 1

    tt, ss = times(3)
    le = (ss <= tt).astype(jnp.int32)
    sums = [le == 1]
    for m in GLA_LEVELS:
        same, ut, us = level(tt, ss, m)
        sums.append(same & (ut == us) & (ut == le))
    tt, ss = times(1)
    blocks = [ss == tt]
    for m in GLA_LEVELS:
        same, ut, us = level(tt, ss, m)
        blocks.append(same & (ut == 1) & (us == 0))
    mall3 = jnp.concatenate([jnp.where(s, 1.0, 0.0) for s in sums], axis=0).astype(BF16)
    return mall3, blocks


def _pieces(x):
    hi = x.astype(BF16)
    r1 = x - hi.astype(F32)
    mid = r1.astype(BF16)
    return hi, mid, (r1 - mid.astype(F32)).astype(BF16)


def _sum_f32(mall3, x):
    return _dot(mall3, jnp.concatenate(_pieces(x), axis=0), NN)


def _sum_f32_t(mall3, x):
    m = mall3[:, 0:GLA_CHUNK]
    hi, mid, lo = _pieces(x)
    return _dot(m, hi, TN) + _dot(m, mid, TN) + _dot(m, lo, TN)


def _gla_chunk_of(dirv, j, lc, nc):
    return jnp.where(dirv == 0, j, jnp.where(j < lc, lc - 1 - j, nc + lc - 1 - j))


def _gla_gate(lr_ref, wg_ref, bg_ref):
    pre = _dot(lr_ref[...].astype(BF16), wg_ref[0].astype(BF16), NN) + bg_ref[0]
    g = (jnp.minimum(pre, 0.0) - jnp.log(1.0 + jnp.exp(-jnp.abs(pre)))) * (1.0 / GLA_GATE_NORM)
    return pre, g


def _gla_fwd(z, qblk, kblk, vblk, lrblk, wg, bg, DV, n_ctx, name):
    R = z.shape[0]
    C = GLA_CHUNK
    DK = wg.shape[2] // GLA_HEADS
    nc, lc = R // C, n_ctx // C
    qscale = DK ** -0.5

    GK, GV = GLA_HEADS * DK, GLA_HEADS * DV

    def body(q_ref, k_ref, v_ref, lr_ref, wg_ref, bg_ref, o_ref, sp_ref, st_ref):
        dirv, j = pl.program_id(0), pl.program_id(1)

        @pl.when(j == 0)
        def _():
            st_ref[...] = jnp.zeros_like(st_ref)

        mall, blocks = _gla_masks(dirv)
        _, g_all = _gla_gate(lr_ref, wg_ref, bg_ref)
        E_all = _sum_f32(mall, g_all)
        for h in range(GLA_HEADS):
            ks, vs = slice(h * DK, (h + 1) * DK), slice(h * DV, (h + 1) * DV)
            q, k, v = q_ref[:, ks] * qscale, k_ref[:, ks], v_ref[:, vs].astype(BF16)
            g, E = g_all[:, ks], E_all[:, ks]
            st = st_ref[h]
            sp_ref[0, h, 0] = st
            A = jnp.where(blocks[0], _dot(q.astype(BF16), k.astype(BF16), NT), 0.0)
            for l in range(len(GLA_LEVELS)):
                e = jnp.exp(E[(1 + l) * C:(2 + l) * C])
                A = A + jnp.where(blocks[l + 1], _dot((q * e).astype(BF16), (k * e).astype(BF16), NT), 0.0)
            o_ref[0, :, vs] = (_dot((q * jnp.exp(E[0:C])).astype(BF16), st.astype(BF16), NT)
                               + _dot(A.astype(BF16), v, NN))
            last = jnp.sum(g, axis=0, keepdims=True)
            st_ref[h] = jnp.exp(last) * st + _dot(v, (k * jnp.exp(last - E[0:C])).astype(BF16), TN)

    chunk = functools.partial(_gla_chunk_of, lc=lc, nc=nc)
    return pl.pallas_call(
        body, name=name, grid=(2, nc),
        in_specs=[pl.BlockSpec((C, GK), lambda d, j: (chunk(d, j), qblk)),
                  pl.BlockSpec((C, GK), lambda d, j: (chunk(d, j), kblk)),
                  pl.BlockSpec((C, GV), lambda d, j: (chunk(d, j), vblk)),
                  pl.BlockSpec((C, LANES), lambda d, j: (chunk(d, j), lrblk)),
                  pl.BlockSpec((1, LANES, GK), lambda d, j: (d, 0, 0)),
                  pl.BlockSpec((1, 1, GK), lambda d, j: (d, 0, 0))],
        out_specs=[pl.BlockSpec((1, C, GV), lambda d, j: (d, chunk(d, j), 0)),
                   pl.BlockSpec((1, GLA_HEADS, 1, DV, DK), lambda d, j: (d, 0, j, 0, 0))],
        out_shape=[jax.ShapeDtypeStruct((2, R, GV), F32),
                   jax.ShapeDtypeStruct((2, GLA_HEADS, nc, DV, DK), F32)],
        scratch_shapes=[pltpu.VMEM((GLA_HEADS, DV, DK), F32)],
        compiler_params=_cparams("parallel", "arbitrary"),
    )(z, z, z, z, wg, bg)


def _gla_bwd(z, qblk, kblk, vblk, lrblk, wg, bg, sprev, do, n_ctx, name, ride=None):
    R = z.shape[0]
    C = GLA_CHUNK
    DK, DV = wg.shape[2] // GLA_HEADS, do.shape[1] // GLA_HEADS
    nc, lc = R // C, n_ctx // C
    qscale = DK ** -0.5
    nl = len(GLA_LEVELS)

    GK, GV = GLA_HEADS * DK, GLA_HEADS * DV

    def body(q_ref, k_ref, v_ref, lr_ref, wg_ref, bg_ref, sp_ref, do_ref,
             dq_ref, dk_ref, dv_ref, dpre_ref, dbg_ref, dst_ref):
        dirv, jr = pl.program_id(0), pl.program_id(1)

        @pl.when(jr == 0)
        def _():
            dst_ref[...] = jnp.zeros_like(dst_ref)
            dbg_ref[...] = jnp.zeros_like(dbg_ref)

        mall, blocks = _gla_masks(dirv)
        pre_all, g_all = _gla_gate(lr_ref, wg_ref, bg_ref)
        E_all = _sum_f32(mall, g_all)
        for h in range(GLA_HEADS):
            ks, vs = slice(h * DK, (h + 1) * DK), slice(h * DV, (h + 1) * DV)
            q, k, v = q_ref[:, ks] * qscale, k_ref[:, ks], v_ref[:, vs].astype(BF16)
            pre, g, E = pre_all[:, ks], g_all[:, ks], E_all[:, ks]
            last = jnp.sum(g, axis=0, keepdims=True)
            eb, er, decay = jnp.exp(E[0:C]), jnp.exp(last - E[0:C]), jnp.exp(last)
            st = sp_ref[0, h, 0]
            dst = dst_ref[h]
            d_o = do_ref[:, vs]
            qe, kd = q * eb, k * er
            qb, kb = q.astype(BF16), k.astype(BF16)
            A = jnp.where(blocks[0], _dot(qb, kb, NT), 0.0)
            for l in range(nl):
                e = jnp.exp(E[(1 + l) * C:(2 + l) * C])
                A = A + jnp.where(blocks[l + 1], _dot((q * e).astype(BF16), (k * e).astype(BF16), NT), 0.0)
            dA = _dot(d_o, v, NT)
            dv_ref[0, :, vs] = _dot(A.astype(BF16), d_o, TN) + _dot(kd.astype(BF16), dst.astype(BF16), NT)
            dqe = _dot(d_o, st.astype(BF16), NN)
            dkd = _dot(v, dst.astype(BF16), NN)
            G = jnp.where(blocks[0], dA, 0.0).astype(BF16)
            dq = dqe * eb + _dot(G, kb, NN)
            dk = dkd * er + _dot(G, qb, TN)
            dEr = dkd * kd
            dE = [dqe * qe - dEr]
            for l in range(nl):
                e = jnp.exp(E[(1 + l) * C:(2 + l) * C])
                ql, kl = q * e, k * e
                G = jnp.where(blocks[l + 1], dA, 0.0).astype(BF16)
                dql = _dot(G, kl.astype(BF16), NN)
                dkl = _dot(G, ql.astype(BF16), TN)
                dq = dq + dql * e
                dk = dk + dkl * e
                dE.append(dql * ql + dkl * kl)
            dlast = jnp.sum(dst * st, axis=0, keepdims=True) * decay + jnp.sum(dEr, axis=0, keepdims=True)
            dg = _sum_f32_t(mall, jnp.concatenate(dE, axis=0)) + dlast
            dpre = dg * (1.0 / GLA_GATE_NORM) / (1.0 + jnp.exp(pre))
            dq_ref[0, :, ks] = dq * qscale
            dk_ref[0, :, ks] = dk
            dpre_ref[:, ks] = dpre.astype(BF16)
            dbg_ref[0, :, ks] += jnp.sum(dpre, axis=0, keepdims=True)
            dst_ref[h] = decay * dst + _dot(d_o, qe.astype(BF16), TN)

    def chunk(d, jr):
        return _gla_chunk_of(d, nc - 1 - jr, lc, nc)

    return _pcall(
        body, name=name, grid=(2, nc),
        in_specs=[pl.BlockSpec((C, GK), lambda d, j: (chunk(d, j), qblk)),
                  pl.BlockSpec((C, GK), lambda d, j: (chunk(d, j), kblk)),
                  pl.BlockSpec((C, GV), lambda d, j: (chunk(d, j), vblk)),
                  pl.BlockSpec((C, LANES), lambda d, j: (chunk(d, j), lrblk)),
                  pl.BlockSpec((1, LANES, GK), lambda d, j: (d, 0, 0)),
                  pl.BlockSpec((1, 1, GK), lambda d, j: (d, 0, 0)),
                  pl.BlockSpec((1, GLA_HEADS, 1, DV, DK), lambda d, j: (d, 0, nc - 1 - j, 0, 0)),
                  pl.BlockSpec((C, GV), lambda d, j: (chunk(d, j), 0))],
        out_specs=[pl.BlockSpec((1, C, GK), lambda d, j: (d, chunk(d, j), 0)),
                   pl.BlockSpec((1, C, GK), lambda d, j: (d, chunk(d, j), 0)),
                   pl.BlockSpec((1, C, GV), lambda d, j: (d, chunk(d, j), 0)),
                   pl.BlockSpec((C, GK), lambda d, j: (chunk(d, j), d)),
                   pl.BlockSpec((1, 1, GK), lambda d, j: (d, 0, 0))],
        out_shape=[jax.ShapeDtypeStruct((2, R, GK), F32),
                   jax.ShapeDtypeStruct((2, R, GK), F32),
                   jax.ShapeDtypeStruct((2, R, GV), F32),
                   jax.ShapeDtypeStruct((R, 2 * GK), BF16),
                   jax.ShapeDtypeStruct((2, 1, GK), F32)],
        scratch_shapes=[pltpu.VMEM((GLA_HEADS, DV, DK), F32)],
        sem=("arbitrary", "arbitrary"), args=(z, z, z, z, wg, bg, sprev, do), ride=ride)


def _glanorm_fwd(o, z, rbblk, gn, n_ctx, name):
    _, R, GV = o.shape
    T = R - n_ctx
    DV = GV // GLA_HEADS
    tm = _pick(n_ctx, 256, 8)
    ro = n_ctx // tm

    def body(o0_ref, o1_ref, rb_ref, gn_ref, p_ref):
        gnv = gn_ref[...]
        for h in range(GLA_HEADS):
            sl = slice(h * DV, (h + 1) * DV)
            og = o0_ref[0, :, sl] + o1_ref[0, :, sl]
            r = lax.rsqrt(jnp.mean(og * og, axis=-1, keepdims=True) + EPS)
            p_ref[:, sl] = (og * r * gnv * _silu(rb_ref[:, sl])).astype(BF16)

    return pl.pallas_call(
        body, name=name, grid=(T // tm,),
        in_specs=[pl.BlockSpec((1, tm, GV), lambda i: (0, i + ro, 0)), pl.BlockSpec((1, tm, GV), lambda i: (1, i + ro, 0)),
                  pl.BlockSpec((tm, GV), lambda i: (i + ro, rbblk)), pl.BlockSpec((1, DV), lambda i: (0, 0))],
        out_specs=pl.BlockSpec((tm, GV), lambda i: (i, 0)),
        out_shape=jax.ShapeDtypeStruct((T, GV), BF16), compiler_params=_cparams("parallel"))(o, o, z, gn)


def _glanorm_bwd(o, z, rbblk, gn, dp, n_ctx, name):
    _, R, GV = o.shape
    T = R - n_ctx
    DV = GV // GLA_HEADS
    tm = _pick(n_ctx, 256, 8)
    ro = n_ctx // tm

    def body(o0_ref, o1_ref, rb_ref, gn_ref, dp_ref, do_ref, drb_ref, acc_ref):
        i = pl.program_id(0)

        @pl.when(i == 0)
        def _():
            acc_ref[...] = jnp.zeros_like(acc_ref)

        gnv = gn_ref[...]
        dgn = jnp.zeros((1, DV), F32)
        for h in range(GLA_HEADS):
            sl = slice(h * DV, (h + 1) * DV)
            og = o0_ref[0, :, sl] + o1_ref[0, :, sl]
            rb = rb_ref[:, sl]
            d = dp_ref[:, sl]
            r = lax.rsqrt(jnp.mean(og * og, axis=-1, keepdims=True) + EPS)
            xh = og * r
            drb_ref[:, sl] = (d * xh * gnv * _dsilu(rb)).astype(BF16)
            dn = d * _silu(rb)
            dgn = dgn + jnp.sum(dn * xh, axis=0, keepdims=True)
            dxh = dn * gnv
            do_ref[:, sl] = (r * (dxh - xh * jnp.mean(dxh * xh, axis=-1, keepdims=True))).astype(BF16)
        acc_ref[0:1, :] += dgn

    row = pl.BlockSpec((tm, GV), lambda i: (i, 0))
    return pl.pallas_call(
        body, name=name, grid=(T // tm,),
        in_specs=[pl.BlockSpec((1, tm, GV), lambda i: (0, i + ro, 0)), pl.BlockSpec((1, tm, GV), lambda i: (1, i + ro, 0)),
                  pl.BlockSpec((tm, GV), lambda i: (i + ro, rbblk)), pl.BlockSpec((1, DV), lambda i: (0, 0)), row],
        out_specs=[row, row, pl.BlockSpec((8, DV), lambda i: (0, 0))],
        out_shape=[jax.ShapeDtypeStruct((T, GV), BF16), jax.ShapeDtypeStruct((T, GV), BF16),
                   jax.ShapeDtypeStruct((8, DV), F32)],
        compiler_params=_cparams("arbitrary"))(o, o, z, gn, dp)


def _gate_fwd(z, gablk, gbblk, ya, yg, n_ctx, name):
    T, D = ya.shape
    tm = _pick(n_ctx, 256, 8)
    ro = n_ctx // tm

    def body(ga_ref, gb_ref, ya_ref, yg_ref, m_ref):
        m_ref[...] = (_sigmoid(ga_ref[...]) * ya_ref[...] + _sigmoid(gb_ref[...]) * yg_ref[...]).astype(BF16)

    row = pl.BlockSpec((tm, D), lambda i: (i, 0))
    return pl.pallas_call(
        body, name=name, grid=(T // tm,),
        in_specs=[pl.BlockSpec((tm, D), lambda i: (i + ro, gablk)), pl.BlockSpec((tm, D), lambda i: (i + ro, gbblk)), row, row],
        out_specs=row, out_shape=jax.ShapeDtypeStruct((T, D), BF16), compiler_params=_cparams("parallel"))(z, z, ya, yg)


def _gate_bwd(z, gablk, gbblk, ya, yg, dm, n_ctx, name):
    T, D = ya.shape
    tm = _pick(n_ctx, 256, 8)
    ro = n_ctx // tm

    def body(ga_ref, gb_ref, ya_ref, yg_ref, dm_ref, dya_ref, dyg_ref, dga_ref, dgb_ref):
        d = dm_ref[...]
        sa, sb = _sigmoid(ga_ref[...]), _sigmoid(gb_ref[...])
        dya_ref[...] = (d * sa).astype(BF16)
        dyg_ref[...] = (d * sb).astype(BF16)
        dga_ref[...] = (d * ya_ref[...] * sa * (1.0 - sa)).astype(BF16)
        dgb_ref[...] = (d * yg_ref[...] * sb * (1.0 - sb)).astype(BF16)

    row = pl.BlockSpec((tm, D), lambda i: (i, 0))
    sh = jax.ShapeDtypeStruct((T, D), BF16)
    return pl.pallas_call(
        body, name=name, grid=(T // tm,),
        in_specs=[pl.BlockSpec((tm, D), lambda i: (i + ro, gablk)), pl.BlockSpec((tm, D), lambda i: (i + ro, gbblk)), row, row, row],
        out_specs=[row] * 4, out_shape=[sh] * 4, compiler_params=_cparams("parallel"))(z, z, ya, yg, dm)


def _resnorm_fwd(x, mix, gt, g, sc, sh, name):
    T, D = x.shape
    tm = _pick(T, 256, 8)

    def body(x_ref, mix_ref, gt_ref, g_ref, sc_ref, sh_ref, x1_ref, h_ref):
        x1 = x_ref[...] + gt_ref[...] * mix_ref[...]
        x1_ref[...] = x1
        r = lax.rsqrt(jnp.mean(x1 * x1, axis=-1, keepdims=True) + EPS)
        h_ref[...] = (x1 * r * g_ref[...] * (1.0 + sc_ref[...]) + sh_ref[...]).astype(BF16)

    row = pl.BlockSpec((tm, D), lambda i: (i, 0))
    vec = pl.BlockSpec((1, D), lambda i: (0, 0))
    return pl.pallas_call(
        body, name=name, grid=(T // tm,), in_specs=[row, row, vec, vec, vec, vec], out_specs=[row, row],
        out_shape=[jax.ShapeDtypeStruct((T, D), F32), jax.ShapeDtypeStruct((T, D), BF16)],
        compiler_params=_cparams("parallel"))(x, mix, gt, g, sc, sh)


def _loss_head(d, x1, gt, target, name):
    T, D = d.shape
    tm = _pick(T, 256, 8)

    def body(d_ref, x1_ref, gt_ref, t_ref, dy_ref, dd_ref, acc_ref):
        i = pl.program_id(0)

        @pl.when(i == 0)
        def _():
            acc_ref[...] = jnp.zeros_like(acc_ref)

        dv, gtv = d_ref[...], gt_ref[...]
        e = x1_ref[...] + gtv * dv - t_ref[...]
        dy = e * (1.0 / D)
        dy_ref[...] = dy
        dd_ref[...] = (dy * gtv).astype(BF16)
        acc_ref[0:1, :] += jnp.sum(e * e, axis=0, keepdims=True)
        acc_ref[1:2, :] += jnp.sum(dy * dv, axis=0, keepdims=True)

    row = pl.BlockSpec((tm, D), lambda i: (i, 0))
    return pl.pallas_call(
        body, name=name, grid=(T // tm,), in_specs=[row, row, pl.BlockSpec((1, D), lambda i: (0, 0)), row],
        out_specs=[row, row, pl.BlockSpec((8, D), lambda i: (0, 0))],
        out_shape=[jax.ShapeDtypeStruct((T, D), F32), jax.ShapeDtypeStruct((T, D), BF16),
                   jax.ShapeDtypeStruct((8, D), F32)],
        compiler_params=_cparams("arbitrary"))(d, x1, gt, target)


def _halo_specs(T, tm, tw, col_of, order):
    n8 = tm // 8
    if order == "ij":
        mid = lambda i, j: (i, col_of(j))
        prev = lambda i, j: (jnp.maximum(i * n8 - 1, 0), col_of(j))
        nxt = lambda i, j: (jnp.minimum((i + 1) * n8, T // 8 - 1), col_of(j))
    else:
        mid = lambda j, i: (i, col_of(j))
        prev = lambda j, i: (jnp.maximum(i * n8 - 1, 0), col_of(j))
        nxt = lambda j, i: (jnp.minimum((i + 1) * n8, T // 8 - 1), col_of(j))
    return [pl.BlockSpec((tm, tw), mid), pl.BlockSpec((8, tw), prev), pl.BlockSpec((8, tw), nxt)]


def _shift_rows(x, before, after):
    tm = x.shape[0]
    row = lax.broadcasted_iota(jnp.int32, x.shape, 0)
    return (jnp.where(row == 0, before, pltpu.roll(x, 1, 0)),
            jnp.where(row == tm - 1, after, pltpu.roll(x, tm - 1, 0)))


def _conv_fwd(u, cw, cb, name):
    T, F2 = u.shape
    F = F2 // 2
    tm, tw = _pick(T, 256, 8), _pick(F, 512)
    nt, nw = T // tm, F // tw

    def body(ua, uap, uan, ug, ugp, ugn, cwa, cwg, cba, cbg, f_ref):
        i = pl.program_id(0)
        first, last = i == 0, i == nt - 1

        def conv(u_ref, up_ref, un_ref, w_ref, b_ref):
            m = u_ref[...]
            p, n = _shift_rows(m, jnp.where(first, 0.0, up_ref[7:8, :]), jnp.where(last, 0.0, un_ref[0:1, :]))
            return p * w_ref[0:1, :] + m * w_ref[1:2, :] + n * w_ref[2:3, :] + b_ref[...]

        a = conv(ua, uap, uan, cwa, cba)
        g = conv(ug, ugp, ugn, cwg, cbg)
        f_ref[...] = (_silu(a) * g).astype(BF16)

    wspec = lambda off: pl.BlockSpec((3, tw), lambda i, j: (0, j + off))
    bspec = lambda off: pl.BlockSpec((1, tw), lambda i, j: (0, j + off))
    return pl.pallas_call(
        body, name=name, grid=(nt, nw),
        in_specs=_halo_specs(T, tm, tw, lambda j: j, "ij") + _halo_specs(T, tm, tw, lambda j: j + nw, "ij")
        + [wspec(0), wspec(nw), bspec(0), bspec(nw)],
        out_specs=pl.BlockSpec((tm, tw), lambda i, j: (i, j)),
        out_shape=jax.ShapeDtypeStruct((T, F), BF16),
        compiler_params=_cparams("parallel", "parallel"),
    )(u, u, u, u, u, u, cw, cw, cb, cb)


def _conv_bwd(u, df, cw, cb, name):
    T, F2 = u.shape
    F = F2 // 2
    tm, tw = _pick(T, 256, 8), _pick(F, 512)
    nt, nw = T // tm, F // tw

    def body(ua, uap, uan, ug, ugp, ugn, cwa, cwg, cba, cbg, df_ref, dfp, dfn, dua_ref, dug_ref, acca_ref, accg_ref):
        i = pl.program_id(1)

        @pl.when(i == 0)
        def _():
            acca_ref[...] = jnp.zeros_like(acca_ref)
            accg_ref[...] = jnp.zeros_like(accg_ref)

        first, last = i == 0, i == nt - 1
        wa, wg, ba, bg = cwa[...], cwg[...], cba[...], cbg[...]

        def conv(p, m, n, w, b):
            return p * w[0:1] + m * w[1:2] + n * w[2:3] + b

        def grads(a, g, d):
            return d * g * _dsilu(a), d * _silu(a)

        xa, xg, d = ua[...], ug[...], df_ref[...]
        sa = _shift_rows(xa, jnp.where(first, 0.0, uap[7:8, :]), jnp.where(last, 0.0, uan[0:1, :]))
        sg = _shift_rows(xg, jnp.where(first, 0.0, ugp[7:8, :]), jnp.where(last, 0.0, ugn[0:1, :]))
        da, dg = grads(conv(sa[0], xa, sa[1], wa, ba), conv(sg[0], xg, sg[1], wg, bg), d)
        da_p, dg_p = grads(conv(uap[6:7, :], uap[7:8, :], xa[0:1], wa, ba),
                           conv(ugp[6:7, :], ugp[7:8, :], xg[0:1], wg, bg), dfp[7:8, :])
        da_n, dg_n = grads(conv(xa[tm - 1:tm], uan[0:1, :], uan[1:2, :], wa, ba),
                           conv(xg[tm - 1:tm], ugn[0:1, :], ugn[1:2, :], wg, bg), dfn[0:1, :])
        ta = _shift_rows(da, jnp.where(first, 0.0, da_p), jnp.where(last, 0.0, da_n))
        tg = _shift_rows(dg, jnp.where(first, 0.0, dg_p), jnp.where(last, 0.0, dg_n))
        dua_ref[...] = (ta[1] * wa[0:1] + da * wa[1:2] + ta[0] * wa[2:3]).astype(BF16)
        dug_ref[...] = (tg[1] * wg[0:1] + dg * wg[1:2] + tg[0] * wg[2:3]).astype(BF16)
        for t, (va, vg) in enumerate(((sa[0], sg[0]), (xa, xg), (sa[1], sg[1]))):
            acca_ref[t:t + 1, :] += jnp.sum(da * va, axis=0, keepdims=True)
            accg_ref[t:t + 1, :] += jnp.sum(dg * vg, axis=0, keepdims=True)
        acca_ref[3:4, :] += jnp.sum(da, axis=0, keepdims=True)
        accg_ref[3:4, :] += jnp.sum(dg, axis=0, keepdims=True)

    wspec = lambda off: pl.BlockSpec((3, tw), lambda j, i: (0, j + off))
    bspec = lambda off: pl.BlockSpec((1, tw), lambda j, i: (0, j + off))
    row = pl.BlockSpec((tm, tw), lambda j, i: (i, j))
    acc = pl.BlockSpec((8, tw), lambda j, i: (0, j))
    return pl.pallas_call(
        body, name=name, grid=(nw, nt),
        in_specs=_halo_specs(T, tm, tw, lambda j: j, "ji") + _halo_specs(T, tm, tw, lambda j: j + nw, "ji")
        + [wspec(0), wspec(nw), bspec(0), bspec(nw)] + _halo_specs(T, tm, tw, lambda j: j, "ji"),
        out_specs=[row, row, acc, acc],
        out_shape=[jax.ShapeDtypeStruct((T, F), BF16), jax.ShapeDtypeStruct((T, F), BF16),
                   jax.ShapeDtypeStruct((8, F), F32), jax.ShapeDtypeStruct((8, F), F32)],
        compiler_params=_cparams("parallel", "arbitrary"),
    )(u, u, u, u, u, u, cw, cw, cb, cb, df, df, df)


def _assemble_dz(lay, z_used, Z, n_ctx, dqa, drb, dga, dgb, dka, dva, dvg, dqg, dkg, dlr, name):
    T = dqa.shape[0]
    R = T + n_ctx
    tm = _pick(n_ctx, 128, 8)
    cb = n_ctx // tm

    def body(dqa_ref, drb_ref, dga_ref, dgb_ref, dka_ref, dva_ref, dvg0, dvg1, dqg0, dqg1, dkg0, dkg1, dlr_ref, o_ref):
        lat = pl.program_id(0) >= cb

        def put(seg, val):
            o_ref[:, lay[seg]:lay[seg] + val.shape[1]] = val.astype(BF16)

        def lat_only(ref):
            v = ref[...]
            return jnp.where(lat, v, jnp.zeros_like(v))

        put("qa", lat_only(dqa_ref))
        put("rb", lat_only(drb_ref))
        put("ga", lat_only(dga_ref))
        put("gb", lat_only(dgb_ref))
        put("ka", dka_ref[...])
        put("va", dva_ref[...])
        put("vb", dvg0[0] + dvg1[0])
        put("qb", dqg0[0] + dqg1[0])
        put("kb", dkg0[0] + dkg1[0])
        put("lr", dlr_ref[...])
        if Z > z_used:
            o_ref[:, z_used:] = jnp.zeros((tm, Z - z_used), BF16)

    lat_spec = lambda a: pl.BlockSpec((tm, a.shape[1]), lambda i: (jnp.maximum(i - cb, 0), 0))
    all_spec = lambda a: pl.BlockSpec((tm, a.shape[1]), lambda i: (i, 0))
    dir_specs = lambda a: [pl.BlockSpec((1, tm, a.shape[2]), lambda i: (0, i, 0)),
                           pl.BlockSpec((1, tm, a.shape[2]), lambda i: (1, i, 0))]
    return pl.pallas_call(
        body, name=name, grid=(R // tm,),
        in_specs=[lat_spec(dqa), lat_spec(drb), lat_spec(dga), lat_spec(dgb), all_spec(dka), all_spec(dva)]
        + dir_specs(dvg) + dir_specs(dqg) + dir_specs(dkg) + [all_spec(dlr)],
        out_specs=pl.BlockSpec((tm, Z), lambda i: (i, 0)),
        out_shape=jax.ShapeDtypeStruct((R, Z), BF16), compiler_params=_cparams("parallel"),
    )(dqa, drb, dga, dgb, dka, dva, dvg, dvg, dqg, dqg, dkg, dkg, dlr)


def _mod_fwd(ca, w, b, name):
    n, D = ca.shape
    N = w.shape[1]
    tn = _pick(N, 512)

    def body(c_ref, w_ref, b_ref, o_ref, s_ref):
        s = _silu(c_ref[...])
        s_ref[...] = s
        o_ref[...] = _dot(s.astype(BF16), w_ref[...].astype(BF16), NN) + b_ref[...]

    return pl.pallas_call(
        body, name=name, grid=(N // tn,),
        in_specs=[pl.BlockSpec((n, D), lambda j: (0, 0)), pl.BlockSpec((D, tn), lambda j: (0, j)),
                  pl.BlockSpec((1, tn), lambda j: (0, j))],
        out_specs=[pl.BlockSpec((n, tn), lambda j: (0, j)), pl.BlockSpec((n, D), lambda j: (0, 0))],
        out_shape=[jax.ShapeDtypeStruct((n, N), F32), jax.ShapeDtypeStruct((n, D), F32)],
        compiler_params=_cparams("arbitrary"))(ca, w, b)


def _silu_bwd(dsil, ca, name):
    def body(d_ref, c_ref, o_ref):
        o_ref[...] = d_ref[...] * _dsilu(c_ref[...])

    return pl.pallas_call(body, name=name, out_shape=jax.ShapeDtypeStruct(ca.shape, F32))(dsil, ca)


def _adam_math(w, g, m, v):
    c1 = 1.0 - ADAM_B1 ** ADAM_STEP
    c2 = 1.0 - ADAM_B2 ** ADAM_STEP
    mn = ADAM_B1 * m + (1.0 - ADAM_B1) * g
    vn = ADAM_B2 * v + (1.0 - ADAM_B2) * (g * g)
    return -ADAM_LR * ((mn / c1) / (jnp.sqrt(vn / c2) + ADAM_EPS) + ADAM_WD * w), mn, vn


def _adamw(w, g, m, v, name):
    Rw, Cw = w.shape
    tr = _pick(Rw, 128, 8)

    def body(w_ref, g_ref, m_ref, v_ref, d_ref, mo_ref, vo_ref):
        d_ref[...], mo_ref[...], vo_ref[...] = _adam_math(w_ref[...], g_ref[...], m_ref[...], v_ref[...])

    row = pl.BlockSpec((tr, Cw), lambda i: (i, 0))
    sh = jax.ShapeDtypeStruct((Rw, Cw), F32)
    return pl.pallas_call(body, name=name, grid=(Rw // tr,), in_specs=[row] * 4, out_specs=[row] * 3,
                          out_shape=[sh] * 3, compiler_params=_cparams("parallel"))(w, g, m, v)


HBM_SPEC = pl.BlockSpec(memory_space=pltpu.HBM)


def _exchange(inputs, out_shapes, stages, name):
    n_in, n_out = len(inputs), len(out_shapes)
    n = sum(len(s) for s in stages)

    def body(*refs):
        ins, outs = refs[:n_in], refs[n_in:n_in + n_out]
        send_sems, recv_sems = refs[n_in + n_out:]
        k = 0
        for stage in stages:
            copies = _stage_copies(stage, ins, outs, send_sems, recv_sems, k)
            for cp in copies:
                cp.start()
            for cp in copies:
                cp.wait()
            k += len(stage)

    return pl.pallas_call(
        body, name=name, in_specs=[HBM_SPEC] * n_in, out_specs=[HBM_SPEC] * n_out, out_shape=out_shapes,
        scratch_shapes=[pltpu.SemaphoreType.DMA((n,)), pltpu.SemaphoreType.DMA((n,))],
    )(*inputs)


def _stage_copies(stage, ins, outs, send_sems, recv_sems, k0=0):
    me = (lax.axis_index("x"), lax.axis_index("y"), lax.axis_index("c"))
    copies = []
    for k, ((skind, sidx), sfn, didx, dfn, flip) in enumerate(stage):
        src = (ins if skind == "in" else outs)[sidx].at[sfn(*me)]
        dst = outs[didx].at[dfn(*me)]
        if flip == (0, 0, 0):
            copies.append(pltpu.make_async_copy(src, dst, send_sems.at[k0 + k]))
        else:
            peer = tuple(1 - a if f else a for a, f in zip(me, flip))
            copies.append(pltpu.make_async_remote_copy(src, dst, send_sems.at[k0 + k], recv_sems.at[k0 + k],
                                                       device_id=peer, device_id_type=MESH))
    return copies


def _pcall(body, *, name, grid, in_specs, out_specs, out_shape, scratch_shapes=(), sem, args, ride=None):
    many = isinstance(out_shape, (list, tuple))
    out_specs, out_shape = (list(out_specs), list(out_shape)) if many else ([out_specs], [out_shape])
    if ride is None:
        res = pl.pallas_call(body, name=name, grid=grid, in_specs=list(in_specs), out_specs=out_specs,
                             out_shape=out_shape, scratch_shapes=list(scratch_shapes),
                             compiler_params=_cparams(*sem))(*args)
        return res if many else res[0]
    x_in, x_out, stage, aliases = ride
    n_in, n_out, n_scr, n_xin, n_xout = len(in_specs), len(out_specs), len(scratch_shapes), len(x_in), len(x_out)

    def wrapped(*refs):
        ins, xins = refs[:n_in], refs[n_in:n_in + n_xin]
        o0 = n_in + n_xin
        outs, xouts = refs[o0:o0 + n_out], refs[o0 + n_out:o0 + n_out + n_xout]
        s0 = o0 + n_out + n_xout
        scr, (send_sems, recv_sems) = refs[s0:s0 + n_scr], refs[s0 + n_scr:]
        first = functools.reduce(jnp.logical_and, [pl.program_id(d) == 0 for d in range(len(grid))])
        last = functools.reduce(jnp.logical_and, [pl.program_id(d) == grid[d] - 1 for d in range(len(grid))])

        @pl.when(first)
        def _():
            for cp in _stage_copies(stage, xins, xouts, send_sems, recv_sems):
                cp.start()

        body(*ins, *outs, *scr)

        @pl.when(last)
        def _():
            for cp in _stage_copies(stage, xins, xouts, send_sems, recv_sems):
                cp.wait()

    res = pl.pallas_call(
        wrapped, name=name, grid=grid, in_specs=list(in_specs) + [HBM_SPEC] * n_xin,
        out_specs=out_specs + [HBM_SPEC] * n_xout, out_shape=out_shape + list(x_out),
        scratch_shapes=list(scratch_shapes) + [pltpu.SemaphoreType.DMA((len(stage),)),
                                               pltpu.SemaphoreType.DMA((len(stage),))],
        input_output_aliases={n_in + a: n_out + b for a, b in aliases.items()},
        compiler_params=_cparams(*(["arbitrary"] * len(grid))))(*args, *x_in)
    main = res[:n_out]
    return (main if many else main[0]), list(res[n_out:])


FLIPS_ALL = [(0, 0, 1), (0, 1, 0), (0, 1, 1), (1, 0, 0), (1, 0, 1), (1, 1, 0), (1, 1, 1)]
FLIPS_CHIP = [(0, 1, 0), (1, 0, 0), (1, 1, 0)]


def _sum_slots(buf, name):
    n, r, w = buf.shape
    tr = _pick(r, 256, 8)

    def body(b_ref, o_ref):
        acc = b_ref[0]
        for s in range(1, n):
            acc = acc + b_ref[s]
        o_ref[...] = acc

    return pl.pallas_call(
        body, name=name, grid=(r // tr,), in_specs=[pl.BlockSpec((n, tr, w), lambda i: (0, i, 0))],
        out_specs=pl.BlockSpec((tr, w), lambda i: (i, 0)), out_shape=jax.ShapeDtypeStruct((r, w), F32),
        compiler_params=_cparams("parallel"))(buf)


def _allreduce(buf, name):
    r, w = buf.shape
    whole = lambda x, y, c: (slice(None), slice(None))
    slot = lambda x, y, c: (4 * x + 2 * y + c,)
    stage = [(("in", 0), whole, 0, slot, f) for f in [(0, 0, 0)] + FLIPS_ALL]
    (slots,) = _exchange([buf], [jax.ShapeDtypeStruct((8, r, w), F32)], [stage], name + "_x")
    return _sum_slots(slots, name + "_sum")


def _gather_plan(shards, src):
    half = lambda a, c: pl.ds(c * (a.shape[0] // 2), a.shape[0] // 2)
    first, second = [], []
    for n, a in enumerate(shards):
        for f in FLIPS_CHIP:
            first.append((("in", n), lambda x, y, c, a=a: (half(a, c), slice(None)), n,
                          lambda x, y, c, a=a: (2 * x + y, half(a, c), slice(None)), f))
            peer_slot = lambda x, y, c, a=a, f=f: (2 * (x ^ f[0]) + (y ^ f[1]), half(a, c), slice(None))
            second.append(((src, n), peer_slot, n, peer_slot, (0, 0, 1)))
    outs = [jax.ShapeDtypeStruct((4,) + a.shape, a.dtype) for a in shards]
    return first, second, outs


def _allgather_weights(shards, name):
    first, second, outs = _gather_plan(shards, "out")
    return _exchange(shards, outs, [first, second], name)


def _place_own(buf, shard, svec, name):
    _, Rs, Cs = buf.shape
    tr = _pick(Rs, 256, 16)

    def body(s_ref, buf_ref, sh_ref, o_ref):
        o_ref[0] = sh_ref[...]

    grid_spec = pltpu.PrefetchScalarGridSpec(
        num_scalar_prefetch=1, grid=(Rs // tr,),
        in_specs=[pl.BlockSpec(memory_space=pl.ANY), pl.BlockSpec((tr, Cs), lambda i, s: (i, 0))],
        out_specs=pl.BlockSpec((1, tr, Cs), lambda i, s: (s[0], i, 0)))
    return pl.pallas_call(body, name=name, grid_spec=grid_spec, out_shape=jax.ShapeDtypeStruct(buf.shape, buf.dtype),
                          input_output_aliases={1: 0}, compiler_params=_cparams("arbitrary"))(svec, buf, shard)


def _add_pair(G, bufA, cvec, name):
    _, Rs, Cs = G.shape
    Rh = Rs // 2
    tr = _pick(Rh, 128, 16)
    nb = Rh // tr

    def body(c_ref, g_ref, a_ref, o_ref):
        o_ref[...] = (g_ref[...] + a_ref[...]).astype(BF16)

    grid_spec = pltpu.PrefetchScalarGridSpec(
        num_scalar_prefetch=1, grid=(4, nb),
        in_specs=[pl.BlockSpec((1, tr, Cs), lambda s, i, c_ref: (s, c_ref[0] * nb + i, 0)),
                  pl.BlockSpec((1, tr, Cs), lambda s, i, c_ref: (s, i, 0))],
        out_specs=pl.BlockSpec((1, tr, Cs), lambda s, i, c_ref: (s, i, 0)))
    return pl.pallas_call(body, name=name, grid_spec=grid_spec, out_shape=jax.ShapeDtypeStruct((4, Rh, Cs), BF16),
                          compiler_params=_cparams("parallel", "parallel"))(cvec, G, bufA)


def _sum_chips(G, bufA, bufB, cvec, svec, name):
    _, Rs, Cs = G.shape
    Rh = Rs // 2
    tr = _pick(Rh, 128, 16)
    nb = Rh // tr

    def body(c_ref, s_ref, g_ref, a_ref, b_ref, o_ref):
        o_ref[...] = (g_ref[0] + a_ref[0]) + b_ref[0].astype(F32) + b_ref[1].astype(F32) + b_ref[2].astype(F32)

    grid_spec = pltpu.PrefetchScalarGridSpec(
        num_scalar_prefetch=2, grid=(nb,),
        in_specs=[pl.BlockSpec((1, tr, Cs), lambda i, c, s: (s[0], c[0] * nb + i, 0)),
                  pl.BlockSpec((1, tr, Cs), lambda i, c, s: (s[0], i, 0)),
                  pl.BlockSpec((3, tr, Cs), lambda i, c, s: (0, i, 0))],
        out_specs=pl.BlockSpec((tr, Cs), lambda i, c, s: (i, 0)))
    return pl.pallas_call(body, name=name, grid_spec=grid_spec, out_shape=jax.ShapeDtypeStruct((Rh, Cs), F32),
                          compiler_params=_cparams("parallel"))(cvec, svec, G, bufA, bufB)


def _pair_plan(grads):
    Rh = [g.shape[1] // 2 for g in grads]
    whole3 = lambda x, y, c: (slice(None), slice(None), slice(None))
    stage = [(("in", n), lambda x, y, c, n=n: (slice(None), pl.ds((1 - c) * Rh[n], Rh[n]), slice(None)), n,
              whole3, (0, 0, 1)) for n in range(len(grads))]
    return [jax.ShapeDtypeStruct((4, Rh[n], g.shape[2]), F32) for n, g in enumerate(grads)], stage


def _chips_plan(P):
    stage = [(("in", n), lambda x, y, c, f=f: (2 * (x ^ f[0]) + (y ^ f[1]),), n, lambda x, y, c, k=k: (k,), f)
             for n in range(len(P)) for k, f in enumerate(FLIPS_CHIP)]
    return [jax.ShapeDtypeStruct((3,) + p.shape[1:], BF16) for p in P], stage


def _halves_plan(mine):
    whole2 = lambda x, y, c: (slice(None), slice(None))
    stage = [(("in", n), whole2, n, whole2, (0, 0, 1)) for n in range(len(mine))]
    return [jax.ShapeDtypeStruct(r.shape, F32) for r in mine], stage


def _adamw_halves(w, mine, other, m, v, cvec, name):
    Rs, Cs = w.shape
    Rh = Rs // 2
    tr = _pick(Rh, 128, 8)
    nb = Rh // tr

    def body(c_ref, w_ref, a_ref, b_ref, m_ref, v_ref, g_ref, d_ref, mo_ref, vo_ref):
        gv = jnp.where(pl.program_id(0) // nb == c_ref[0], a_ref[...], b_ref[...])
        g_ref[...] = gv
        d_ref[...], mo_ref[...], vo_ref[...] = _adam_math(w_ref[...], gv, m_ref[...], v_ref[...])

    row = pl.BlockSpec((tr, Cs), lambda i, c: (i, 0))
    hrow = pl.BlockSpec((tr, Cs), lambda i, c: (i % nb, 0))
    grid_spec = pltpu.PrefetchScalarGridSpec(num_scalar_prefetch=1, grid=(2 * nb,),
                                             in_specs=[row, hrow, hrow, row, row], out_specs=[row] * 4)
    return pl.pallas_call(body, name=name, grid_spec=grid_spec, out_shape=[jax.ShapeDtypeStruct((Rs, Cs), F32)] * 4,
                          compiler_params=_cparams("parallel"))(cvec, w, mine, other, m, v)


def _pack(arrays):
    flat = [a.reshape(-1).astype(F32) for a in arrays]
    meta, off = [], 0
    for a, f in zip(arrays, flat):
        meta.append((off, a.shape))
        off += f.shape[0]
    total = -(-off // (8 * LANES)) * (8 * LANES)
    flat.append(jnp.zeros((total - off,), F32))
    return jnp.concatenate(flat).reshape(total // LANES, LANES), meta


def _unpack(buf, meta):
    flat = buf.reshape(-1)
    out = []
    for off, shape in meta:
        size = 1
        for s in shape:
            size *= s
        out.append(flat[off:off + size].reshape(shape))
    return out


WEIGHT_NAMES = ["c_ctx", "w_mod", "b_mod", "g_mix", "w_in", "q_norm", "k_norm", "attn_sink", "w_gate_f", "b_gate_f",
                "w_gate_b", "b_gate_b", "gla_norm", "w_attn_o", "w_gla_o", "w_out", "g_ffn", "w_up", "conv_w",
                "conv_b", "w_down"]
BIG_NAMES = ["w_in", "w_attn_o", "w_gla_o", "w_out", "w_up", "w_down"]
SHARDED_SMALL = ["w_gate_f", "w_gate_b", "conv_w"]


def _layouts(D):
    aw, kvw, gk, gv = N_Q_HEADS * HEAD_DIM, N_KV_HEADS * HEAD_DIM, D // 2, D
    widths = {"qa": aw, "ka": kvw, "va": kvw, "qb": gk, "kb": gk, "vb": gv, "rb": gv, "lr": 2 * GLA_LOWRANK,
              "ga": D, "gb": D}
    orig, off = {}, 0
    for s in ["qa", "ka", "va", "qb", "kb", "vb", "rb", "lr", "ga", "gb"]:
        orig[s] = off
        off += widths[s]
    order = ["qa", "vb", "rb", "ga", "gb", "ka", "va", "qb", "kb", "lr"]
    lay, off = {}, 0
    for s in order:
        lay[s] = off
        off += LANES if s == "lr" else widths[s]
    align = {"qa": aw, "vb": D, "rb": D, "ga": D, "gb": D, "ka": kvw, "va": kvw, "qb": gk, "kb": gk,
             "lr": LANES}
    for s in order:
        assert lay[s] % align[s] == 0, (s, lay[s], align[s])
    return widths, orig, order, lay, off, -(-off // (2 * MXU_TILE)) * (2 * MXU_TILE)


def _rope_tables(T, L):
    t = jnp.arange(T)
    nf = HEAD_DIM // 4
    inv = ROPE_THETA ** (-jnp.arange(nf, dtype=F32) / nf)
    ang = jnp.concatenate([(t // GRID_W)[:, None] * inv, (t % GRID_W)[:, None] * inv], axis=-1)
    cos, sin = jnp.cos(ang), jnp.sin(ang)
    cos2 = jnp.concatenate([jnp.ones((L, HEAD_DIM), F32), jnp.concatenate([cos, cos], axis=-1)], axis=0)
    sin2 = jnp.concatenate([jnp.zeros((L, HEAD_DIM), F32), jnp.concatenate([-sin, sin], axis=-1)], axis=0)
    return cos2, sin2


def _step(x, c, ctx, loss_target, W, M, V):
    xi, yi, ci = lax.axis_index("x"), lax.axis_index("y"), lax.axis_index("c")
    chip = 2 * xi + yi
    dev = 2 * chip + ci
    south = (ci == 0).astype(F32)
    cvec = ci.reshape(1).astype(jnp.int32)
    svec = chip.reshape(1).astype(jnp.int32)
    T, D = x.shape[1], x.shape[2]
    L = ctx.shape[1]
    R = L + T
    F = 4 * W["w_down"].shape[1]
    GK, GV = D // 2, D
    DK, DV = GK // GLA_HEADS, GV // GLA_HEADS
    N6 = 6 * D
    N4 = N6 // 4
    widths, orig, order, lay, z_used, Z = _layouts(D)

    def place_cols(shard, full_cols):
        cols = shard.shape[-1]
        full = jnp.zeros(shard.shape[:-1] + (full_cols,), F32)
        return lax.dynamic_update_slice(full, shard * south, (0,) * (shard.ndim - 1) + (chip * cols,))

    c_rows = lax.dynamic_update_slice(jnp.zeros((8, D), F32), c, (dev, 0))
    bufa, meta = _pack([c_rows, place_cols(W["w_gate_f"][0], GK), place_cols(W["w_gate_b"][0], GK),
                        place_cols(W["conv_w"][0], 2 * F)])
    c_all, wgf, wgb, cw = _unpack(_allreduce(bufa, "gather_small"), meta)
    ca = jnp.concatenate([c_all, W["c_ctx"][None, :], jnp.zeros((7, D), F32)], axis=0)
    b_shard = lax.dynamic_slice(W["b_mod"], (0, chip * N4), (1, N4))
    mod_part, sil = _mod_fwd(ca, W["w_mod"][0], b_shard, "mod_fwd")
    slots = lax.dynamic_update_slice(jnp.zeros((4, 16, N4), F32), (mod_part * south)[None], (chip, 0, 0))
    mod_all = _allreduce(slots.reshape(64, N4), "gather_mod").reshape(4, 16, N4).transpose(1, 0, 2).reshape(16, N6)
    mx = lax.dynamic_slice(mod_all, (dev, 0), (1, N6)).reshape(6, 1, D)
    mc = mod_all[8].reshape(6, 1, D)

    sq = lambda a: a.reshape(a.shape[1:])
    shards = [sq(W[n]).astype(BF16) for n in BIG_NAMES]
    own = lambda g, s, n: _place_own(g, s, svec, "place_" + n)
    cols = lambda g: g.transpose(1, 0, 2).reshape(g.shape[1], 4 * g.shape[2])
    rows = lambda g: g.reshape(4 * g.shape[1], g.shape[2])
    w_in_f = cols(own(_allgather_weights(shards[:1], "gather_w_in")[0], shards[0], "w_in"))
    seg = lambda s: w_in_f[:, orig[s]:orig[s] + widths[s]]
    w_cat = jnp.concatenate([jnp.pad(seg(s), ((0, 0), (0, LANES - widths[s]))) if s == "lr" else seg(s)
                             for s in order] + [jnp.zeros((D, Z - z_used), BF16)], axis=1)
    gather1, gather2, gather_outs = _gather_plan(shards[1:], "in")
    wg = jnp.zeros((2, LANES, GK), F32).at[0, :GLA_LOWRANK].set(wgf).at[1, GLA_LOWRANK:2 * GLA_LOWRANK].set(wgb)
    bg = jnp.stack([W["b_gate_f"], W["b_gate_b"]])
    cb = W["conv_b"]
    sink_rows = jnp.broadcast_to(W["attn_sink"][0][:, None], (N_Q_HEADS, HEAD_DIM))
    cos2, sin2 = _rope_tables(T, L)
    blk = lambda s, w: lay[s] // w

    xall = jnp.concatenate([ctx[0], x[0]], axis=0)
    sc1 = jnp.stack([mc[1], mx[1]])
    sh1 = jnp.stack([mc[0], mx[0]])
    h = _modnorm_fwd(xall, W["g_mix"], sc1, sh1, L, "modnorm1")
    z, landed = _matmul(h, w_cat, "nn", F32, "proj_in", tn=1536, ride=(shards[1:], gather_outs, gather1, {}))
    qn = _qknorm_fwd(z, blk("qa", widths["qa"]), T, L, W["q_norm"], cos2, sin2, N_Q_HEADS, "qnorm")
    kn = _qknorm_fwd(z, blk("ka", widths["ka"]), R, 0, W["k_norm"], cos2, sin2, N_KV_HEADS, "knorm")
    vb = _cast_seg(z, blk("va", widths["va"]), widths["va"], "vcast")
    o_attn, landed = _attn_fwd(qn, kn, vb, sink_rows, L, "attn_fwd",
                               ride=(landed, gather_outs, gather2, {n: n for n in range(len(landed))}))
    g_ao, g_go, g_out, g_up, g_dn = [own(g, s, n) for g, s, n in zip(landed, shards[1:], BIG_NAMES[1:])]
    w_ao, w_go, w_out, w_up, w_dn = rows(g_ao), rows(g_go), rows(g_out), cols(g_up), rows(g_dn)
    gla_blks = (blk("qb", GK), blk("kb", GK), blk("vb", GV), blk("lr", LANES))
    o_g, sprev = _gla_fwd(z, *gla_blks, wg, bg, DV, L, "gla_fwd")
    p = _glanorm_fwd(o_g, z, blk("rb", D), W["gla_norm"], L, "glanorm")
    ya = _matmul(o_attn, w_ao, "nn", F32, "proj_attn_o")
    yg = _matmul(p, w_go, "nn", F32, "proj_gla_o")
    m = _gate_fwd(z, blk("ga", D), blk("gb", D), ya, yg, L, "gate")
    mix = _matmul(m, w_out, "nn", F32, "proj_out")
    x1, h2 = _resnorm_fwd(x[0], mix, mx[2], W["g_ffn"], mx[4], mx[3], "resnorm2")
    u = _matmul(h2, w_up, "nn", F32, "ffn_up")
    f = _conv_fwd(u, cw, cb, "conv_swiglu")
    d = _matmul(f, w_dn, "nn", F32, "ffn_down", tk=2816)
    dy, dd, lacc = _loss_head(d, x1, mx[5], loss_target[0], "loss_head")
    loss = lax.psum((0.5 / D) * jnp.sum(lacc[0]), ("x", "y", "c"))

    gw_dn = _matmul(f, dd, "tn", F32, "ffn_down_dw")
    df = _matmul(dd, w_dn, "nt", F32, "ffn_down_dx")
    du_a, du_g, acca, accg = _conv_bwd(u, df, cw, cb, "conv_swiglu_bwd")
    du = jnp.concatenate([du_a, du_g], axis=1)
    gw_up = _matmul(h2, du, "tn", F32, "ffn_up_dw")
    dh2 = _matmul(du, w_up, "nt", F32, "ffn_up_dx", tk=2816)
    dx1, dmix, s2 = _resnorm_bwd(x1, dh2, W["g_ffn"], mx[4], dy, mix, mx[2], "resnorm2_bwd")
    gw_out = _matmul(m, dmix, "tn", F32, "proj_out_dw")
    dm = _matmul(dmix, w_out, "nt", F32, "proj_out_dx")
    dya, dyg, dga, dgb = _gate_bwd(z, blk("ga", D), blk("gb", D), ya, yg, dm, L, "gate_bwd")
    gw_ao = _matmul(o_attn, dya, "tn", F32, "proj_attn_o_dw")
    do_attn = _matmul(dya, w_ao, "nt", BF16, "proj_attn_o_dx")
    gw_go = _matmul(p, dyg, "tn", F32, "proj_gla_o_dw")
    dp = _matmul(dyg, w_go, "nt", F32, "proj_gla_o_dx")
    do_gla, drb, s_gn = _glanorm_bwd(o_g, z, blk("rb", D), W["gla_norm"], dp, L, "glanorm_bwd")
    do_pad = jnp.concatenate([jnp.zeros((L, GV), BF16), do_gla], axis=0)
    by_cols = lambda g: g.reshape(g.shape[0], 4, g.shape[1] // 4).transpose(1, 0, 2)
    by_rows = lambda g: g.reshape(4, g.shape[0] // 4, g.shape[1])
    early = [by_rows(gw_ao), by_rows(gw_go), by_rows(gw_out), by_cols(gw_up), by_rows(gw_dn)]
    (dqg, dkg, dvg, dpre, dbg), pair_e = _gla_bwd(z, *gla_blks, wg, bg, sprev, do_pad, L, "gla_bwd",
                                                  ride=(early, *_pair_plan(early), {}))
    sums_e = [_add_pair(g, a, cvec, "reduce_early_add%d" % n) for n, (g, a) in enumerate(zip(early, pair_e))]
    wg_cat = jnp.concatenate([wg[0], wg[1]], axis=1)
    dlr = _matmul(dpre, wg_cat, "nt", BF16, "gla_gate_dx")
    dwg = _matmul(z[:, lay["lr"]:lay["lr"] + LANES], dpre, "tn", F32, "gla_gate_dw")
    (dqn, dkw, dvw, dkc, dvc, dsn), chips_e = _attn_bwd(qn, kn, vb, sink_rows, do_attn, L, "attn_bwd",
                                                        ride=(sums_e, *_chips_plan(sums_e), {}))
    mine_e = [_sum_chips(g, a, b, cvec, svec, "reduce_early_sum%d" % n)
              for n, (g, a, b) in enumerate(zip(early, pair_e, chips_e))]
    dqa, s_qn = _qknorm_bwd(z, blk("qa", widths["qa"]), T, L, W["q_norm"], cos2, sin2, dqn, N_Q_HEADS, "qnorm_bwd")
    dk_all = jnp.concatenate([dkc, dkw[WINDOW:WINDOW + T]], axis=0)
    dv_all = jnp.concatenate([dvc, dvw[WINDOW:WINDOW + T]], axis=0)
    dka, s_kn = _qknorm_bwd(z, blk("ka", widths["ka"]), R, 0, W["k_norm"], cos2, sin2, dk_all, N_KV_HEADS, "knorm_bwd")
    dz = _assemble_dz(lay, z_used, Z, L, dqa, drb, dga, dgb, dka, dv_all, dvg, dqg, dkg, dlr, "assemble_dz")
    gw_cat, other_e = _matmul(h, dz, "tn", F32, "proj_in_dw", tn=768, tk=2816,
                              ride=(mine_e, *_halves_plan(mine_e), {}))
    gw_in = jnp.concatenate([gw_cat[:, lay[s]:lay[s] + widths[s]] for s in ["qa", "ka", "va", "qb", "kb", "vb", "rb",
                                                                           "lr", "ga", "gb"]], axis=1)
    late = [by_cols(gw_in)]
    shapes, stage = _pair_plan(late)
    pair_l = _exchange(late, shapes, [stage], "reduce_late_pair")
    sums_l = [_add_pair(late[0], pair_l[0], cvec, "reduce_late_add")]
    dh, chips_l = _matmul(dz, w_cat, "nt", F32, "proj_in_dx", tk=4608, ride=(sums_l, *_chips_plan(sums_l), {}))
    mine_l = [_sum_chips(late[0], pair_l[0], chips_l[0], cvec, svec, "reduce_late_sum")]
    shapes, stage = _halves_plan(mine_l)
    other_l = _exchange(mine_l, shapes, [stage], "reduce_late_halves")
    mine, other = mine_l + mine_e, list(other_l) + other_e
    grad_x, s1 = _modnorm_bwd(x[0], dh, W["g_mix"], mx[1], dx1, "modnorm1_bwd", dh_roff=L)
    _, s1c = _modnorm_bwd(ctx[0], dh, W["g_mix"], mc[1], None, "modnorm1_ctx_bwd")

    dmod_x = jnp.concatenate([s1[0], s1[1], s2[3], s2[0], s2[1], lacc[1]])
    dmod_c = jnp.concatenate([s1c[0], s1c[1], jnp.zeros((4 * D,), F32)])
    dmod_rows = lax.dynamic_update_slice(jnp.zeros((9, N6), F32).at[8].set(dmod_c), dmod_x[None], (dev, 0))
    small = [dmod_rows, dmod_x + dmod_c, s1[2] + s1c[2], s_qn[0], s_kn[0], dsn[:, 0, :Q_PER_KV].reshape(N_Q_HEADS),
             dwg[:GLA_LOWRANK, :GK], dbg[0].reshape(GK), dwg[GLA_LOWRANK:2 * GLA_LOWRANK, GK:], dbg[1].reshape(GK),
             s_gn[0], s2[2], jnp.concatenate([acca[0:3], accg[0:3]], axis=1), jnp.concatenate([acca[3], accg[3]])]
    bufc, meta = _pack(small)
    (dmod_sum, g_b_mod, g_g_mix, g_q_norm, g_k_norm, g_sink, g_wgf, g_bgf, g_wgb, g_bgb, g_gla_norm, g_g_ffn,
     g_conv_w, g_conv_b) = _unpack(_allreduce(bufc, "reduce_small"), meta)
    dmod16 = lax.dynamic_slice(jnp.concatenate([dmod_sum, jnp.zeros((7, N6), F32)], axis=0), (0, chip * N4), (16, N4))
    g_w_mod = _matmul(sil, dmod16, "tn", F32, "mod_dw")
    dsil = _matmul(dmod16, W["w_mod"][0], "nt", F32, "mod_dx")
    g_c_ctx = _silu_bwd(_allreduce(dsil * south, "reduce_cctx"), ca, "silu_bwd")[8]

    cut = lambda g: lax.dynamic_slice(g, (0, chip * (g.shape[1] // 4)), (g.shape[0], g.shape[1] // 4))
    grads = {"c_ctx": g_c_ctx, "w_mod": g_w_mod[None], "b_mod": g_b_mod[None], "g_mix": g_g_mix[None],
             "q_norm": g_q_norm[None], "k_norm": g_k_norm[None], "attn_sink": g_sink[None],
             "w_gate_f": cut(g_wgf)[None], "b_gate_f": g_bgf[None], "w_gate_b": cut(g_wgb)[None],
             "b_gate_b": g_bgb[None], "gla_norm": g_gla_norm[None], "g_ffn": g_g_ffn[None],
             "conv_w": cut(g_conv_w)[None], "conv_b": g_conv_b[None]}

    delta, new_m, new_v = {}, {}, {}
    dl, mn, vn = _adamw(W["w_mod"][0], g_w_mod, M["w_mod"][0], V["w_mod"][0], "adamw_w_mod")
    delta["w_mod"], new_m["w_mod"], new_v["w_mod"] = dl[None], mn[None], vn[None]
    for n, a, b in zip(BIG_NAMES, mine, other):
        g, dl, mn, vn = _adamw_halves(sq(W[n]), a, b, sq(M[n]), sq(V[n]), cvec, "adamw_" + n)
        grads[n], delta[n], new_m[n], new_v[n] = g[None], dl[None], mn[None], vn[None]
    small_names = [n for n in WEIGHT_NAMES if n not in delta]
    packs = [_pack([src[n] for n in small_names]) for src in (W, grads, M, V)]
    meta = packs[0][1]
    outs = _adamw(packs[0][0], packs[1][0], packs[2][0], packs[3][0], "adamw_small")
    for res, o in zip((delta, new_m, new_v), outs):
        for n, a in zip(small_names, _unpack(o, meta)):
            res[n] = a
    return (loss, grad_x[None], *[grads[n] for n in WEIGHT_NAMES], *[delta[n] for n in WEIGHT_NAMES],
            *[new_m[n] for n in WEIGHT_NAMES], *[new_v[n] for n in WEIGHT_NAMES])


def kernel(x, c, ctx, c_ctx, w_mod, b_mod, g_mix, w_in, q_norm, k_norm, attn_sink, w_gate_f, b_gate_f, w_gate_b, b_gate_b, gla_norm, w_attn_o, w_gla_o, w_out, g_ffn, w_up, conv_w, conv_b, w_down, loss_target, m_c_ctx, m_w_mod, m_b_mod, m_g_mix, m_w_in, m_q_norm, m_k_norm, m_attn_sink, m_w_gate_f, m_b_gate_f, m_w_gate_b, m_b_gate_b, m_gla_norm, m_w_attn_o, m_w_gla_o, m_w_out, m_g_ffn, m_w_up, m_conv_w, m_conv_b, m_w_down, v_c_ctx, v_w_mod, v_b_mod, v_g_mix, v_w_in, v_q_norm, v_k_norm, v_attn_sink, v_w_gate_f, v_b_gate_f, v_w_gate_b, v_b_gate_b, v_gla_norm, v_w_attn_o, v_w_gla_o, v_w_out, v_g_ffn, v_w_up, v_conv_w, v_conv_b, v_w_down):
    W = dict(zip(WEIGHT_NAMES, (c_ctx, w_mod, b_mod, g_mix, w_in, q_norm, k_norm, attn_sink, w_gate_f, b_gate_f,
                                w_gate_b, b_gate_b, gla_norm, w_attn_o, w_gla_o, w_out, g_ffn, w_up, conv_w, conv_b,
                                w_down)))
    M = dict(zip(WEIGHT_NAMES, (m_c_ctx, m_w_mod, m_b_mod, m_g_mix, m_w_in, m_q_norm, m_k_norm, m_attn_sink,
                                m_w_gate_f, m_b_gate_f, m_w_gate_b, m_b_gate_b, m_gla_norm, m_w_attn_o, m_w_gla_o,
                                m_w_out, m_g_ffn, m_w_up, m_conv_w, m_conv_b, m_w_down)))
    V = dict(zip(WEIGHT_NAMES, (v_c_ctx, v_w_mod, v_b_mod, v_g_mix, v_w_in, v_q_norm, v_k_norm, v_attn_sink,
                                v_w_gate_f, v_b_gate_f, v_w_gate_b, v_b_gate_b, v_gla_norm, v_w_attn_o, v_w_gla_o,
                                v_w_out, v_g_ffn, v_w_up, v_conv_w, v_conv_b, v_w_down)))
    return _step(x, c, ctx, loss_target, W, M, V)
```

```python
import functools
import math

import jax
import jax.numpy as jnp
from jax import lax
from jax.experimental import pallas as pl
from jax.experimental.pallas import tpu as pltpu

F32 = jnp.float32
BF16 = jnp.bfloat16
MESH = pl.DeviceIdType.MESH

EPS = 1e-6
HEAD_DIM = 128
N_Q_HEADS = 16
N_KV_HEADS = 4
Q_PER_KV = N_Q_HEADS // N_KV_HEADS
WINDOW = 128
GLA_HEADS = 4
GLA_LOWRANK = 16
GLA_GATE_NORM = 16.0
GLA_CHUNK = 64
GRID_W = 64
ROPE_THETA = 10000.0
GLA_LEVELS = (32, 16, 8, 4, 2, 1)
LANES = 128
MXU_TILE = 256

ADAM_LR = 0.001
ADAM_B1 = 0.9
ADAM_B2 = 0.999
ADAM_EPS = 1e-08
ADAM_WD = 0.01
ADAM_STEP = 10

VMEM_LIMIT = 52 * 1024 * 1024


def _cparams(*sem):
    return pltpu.CompilerParams(dimension_semantics=sem, vmem_limit_bytes=VMEM_LIMIT)


def _pick(n, target, mult=LANES):
    best = None
    d = mult
    while d <= min(n, target):
        if n % d == 0:
            best = d
        d += mult
    return n if best is None else best


def _sigmoid(x):
    return 1.0 / (1.0 + jnp.exp(-x))


def _silu(x):
    return x * _sigmoid(x)


def _dsilu(x):
    s = _sigmoid(x)
    return s * (1.0 + x * (1.0 - s))


def _dot(a, b, dims):
    return lax.dot_general(a, b, (dims, ((), ())), preferred_element_type=F32)


NN = ((1,), (0,))
NT = ((1,), (1,))
TN = ((0,), (0,))


def _matmul(a, b, mode, out_dtype, name, tm=1024, tn=1024, tk=2048, ride=None):
    if mode == "nn":
        (M, K), (K2, N) = a.shape, b.shape
    elif mode == "nt":
        (M, K), (N, K2) = a.shape, b.shape
    else:
        (K, M), (K2, N) = a.shape, b.shape
    assert K == K2, (name, a.shape, b.shape)
    tm, tn, tk = [_pick(n, t, MXU_TILE) if n % MXU_TILE == 0 else _pick(n, t) for n, t in ((M, tm), (N, tn), (K, tk))]
    nk = K // tk
    dims = {"nn": NN, "nt": NT, "tn": TN}[mode]

    def body(a_ref, b_ref, o_ref, acc_ref):
        k = pl.program_id(2)

        @pl.when(k == 0)
        def _():
            acc_ref[...] = jnp.zeros_like(acc_ref)

        acc_ref[...] += _dot(a_ref[...].astype(BF16), b_ref[...].astype(BF16), dims)

        @pl.when(k == nk - 1)
        def _():
            o_ref[...] = acc_ref[...].astype(out_dtype)

    if mode == "tn":
        a_spec = pl.BlockSpec((tk, tm), lambda i, j, k: (k, i))
    else:
        a_spec = pl.BlockSpec((tm, tk), lambda i, j, k: (i, k))
    if mode == "nt":
        b_spec = pl.BlockSpec((tn, tk), lambda i, j, k: (j, k))
    else:
        b_spec = pl.BlockSpec((tk, tn), lambda i, j, k: (k, j))
    return _pcall(
        body, name=name, grid=(M // tm, N // tn, nk),
        in_specs=[a_spec, b_spec],
        out_specs=pl.BlockSpec((tm, tn), lambda i, j, k: (i, j)),
        out_shape=jax.ShapeDtypeStruct((M, N), out_dtype),
        scratch_shapes=[pltpu.VMEM((tm, tn), F32)],
        sem=("parallel", "parallel", "arbitrary"), args=(a, b), ride=ride)


def _modnorm_fwd(xall, g, sc, sh, n_ctx, name):
    R, D = xall.shape
    tm = _pick(n_ctx, 256, 8)
    cb = n_ctx // tm

    def body(x_ref, g_ref, sc_ref, sh_ref, h_ref):
        x = x_ref[...]
        r = lax.rsqrt(jnp.mean(x * x, axis=-1, keepdims=True) + EPS)
        n = x * r * g_ref[...]
        h_ref[...] = (n * (1.0 + sc_ref[0]) + sh_ref[0]).astype(BF16)

    sel = lambda i: (jnp.where(i < cb, 0, 1), 0, 0)
    return pl.pallas_call(
        body, name=name, grid=(R // tm,),
        in_specs=[pl.BlockSpec((tm, D), lambda i: (i, 0)), pl.BlockSpec((1, D), lambda i: (0, 0)),
                  pl.BlockSpec((1, 1, D), sel), pl.BlockSpec((1, 1, D), sel)],
        out_specs=pl.BlockSpec((tm, D), lambda i: (i, 0)),
        out_shape=jax.ShapeDtypeStruct((R, D), BF16),
        compiler_params=_cparams("parallel"),
    )(xall, g, sc, sh)


def _modnorm_bwd(x, dh, g, sc, resid, name, dh_roff=0):
    N, D = x.shape
    tm = _pick(math.gcd(N, dh_roff), 256, 8)
    ro = dh_roff // tm
    want_dx = resid is not None

    def body(*refs):
        if want_dx:
            x_ref, dh_ref, g_ref, sc_ref, res_ref, dx_ref, acc_ref = refs
        else:
            x_ref, dh_ref, g_ref, sc_ref, acc_ref = refs
        i = pl.program_id(0)

        @pl.when(i == 0)
        def _():
            acc_ref[...] = jnp.zeros_like(acc_ref)

        xv, dhv, gv = x_ref[...], dh_ref[...], g_ref[...]
        r = lax.rsqrt(jnp.mean(xv * xv, axis=-1, keepdims=True) + EPS)
        xh = xv * r
        dn = dhv * (1.0 + sc_ref[...])
        acc_ref[0:1, :] += jnp.sum(dhv, axis=0, keepdims=True)
        acc_ref[1:2, :] += jnp.sum(dhv * xh * gv, axis=0, keepdims=True)
        acc_ref[2:3, :] += jnp.sum(dn * xh, axis=0, keepdims=True)
        if want_dx:
            dxh = dn * gv
            dx_ref[...] = res_ref[...] + r * (dxh - xh * jnp.mean(dxh * xh, axis=-1, keepdims=True))

    row = pl.BlockSpec((tm, D), lambda i: (i, 0))
    drow = pl.BlockSpec((tm, D), lambda i: (i + ro, 0))
    vec = pl.BlockSpec((1, D), lambda i: (0, 0))
    acc = pl.BlockSpec((8, D), lambda i: (0, 0))
    acc_shape = jax.ShapeDtypeStruct((8, D), F32)
    if want_dx:
        return pl.pallas_call(
            body, name=name, grid=(N // tm,), in_specs=[row, drow, vec, vec, row],
            out_specs=[row, acc], out_shape=[jax.ShapeDtypeStruct((N, D), F32), acc_shape],
            compiler_params=_cparams("arbitrary"))(x, dh, g, sc, resid)
    sums = pl.pallas_call(
        body, name=name, grid=(N // tm,), in_specs=[row, drow, vec, vec],
        out_specs=acc, out_shape=acc_shape, compiler_params=_cparams("arbitrary"))(x, dh, g, sc)
    return None, sums


def _resnorm_bwd(x1, dh, g, sc, dy, mix, gt, name):
    N, D = x1.shape
    tm = _pick(N, 256, 8)

    def body(x_ref, dh_ref, g_ref, sc_ref, dy_ref, mix_ref, gt_ref, dx_ref, dm_ref, acc_ref):
        i = pl.program_id(0)

        @pl.when(i == 0)
        def _():
            acc_ref[...] = jnp.zeros_like(acc_ref)

        xv, dhv, gv = x_ref[...], dh_ref[...], g_ref[...]
        r = lax.rsqrt(jnp.mean(xv * xv, axis=-1, keepdims=True) + EPS)
        xh = xv * r
        dn = dhv * (1.0 + sc_ref[...])
        dxh = dn * gv
        dx = dy_ref[...] + r * (dxh - xh * jnp.mean(dxh * xh, axis=-1, keepdims=True))
        dx_ref[...] = dx
        dm_ref[...] = (dx * gt_ref[...]).astype(BF16)
        acc_ref[0:1, :] += jnp.sum(dhv, axis=0, keepdims=True)
        acc_ref[1:2, :] += jnp.sum(dhv * xh * gv, axis=0, keepdims=True)
        acc_ref[2:3, :] += jnp.sum(dn * xh, axis=0, keepdims=True)
        acc_ref[3:4, :] += jnp.sum(dx * mix_ref[...], axis=0, keepdims=True)

    row = pl.BlockSpec((tm, D), lambda i: (i, 0))
    vec = pl.BlockSpec((1, D), lambda i: (0, 0))
    return pl.pallas_call(
        body, name=name, grid=(N // tm,), in_specs=[row, row, vec, vec, row, row, vec],
        out_specs=[row, row, pl.BlockSpec((8, D), lambda i: (0, 0))],
        out_shape=[jax.ShapeDtypeStruct((N, D), F32), jax.ShapeDtypeStruct((N, D), BF16),
                   jax.ShapeDtypeStruct((8, D), F32)],
        compiler_params=_cparams("arbitrary"))(x1, dh, g, sc, dy, mix, gt)


def _qknorm_fwd(z, cblk, nrows, roff, w, cos2, sin2, nh, name):
    W = nh * HEAD_DIM
    tm = _pick(math.gcd(nrows, roff), 256, 8)
    ro = roff // tm
    assert roff % tm == 0

    def body(z_ref, w_ref, c_ref, s_ref, o_ref):
        c, s, wv = c_ref[...], s_ref[...], w_ref[...]
        for h in range(nh):
            x = z_ref[:, h * HEAD_DIM:(h + 1) * HEAD_DIM]
            r = lax.rsqrt(jnp.mean(x * x, axis=-1, keepdims=True) + EPS)
            y = x * r * wv
            o_ref[:, h * HEAD_DIM:(h + 1) * HEAD_DIM] = (y * c + pltpu.roll(y, HEAD_DIM // 2, 1) * s).astype(BF16)

    return pl.pallas_call(
        body, name=name, grid=(nrows // tm,),
        in_specs=[pl.BlockSpec((tm, W), lambda i: (i + ro, cblk)), pl.BlockSpec((1, HEAD_DIM), lambda i: (0, 0)),
                  pl.BlockSpec((tm, HEAD_DIM), lambda i: (i + ro, 0)), pl.BlockSpec((tm, HEAD_DIM), lambda i: (i + ro, 0))],
        out_specs=pl.BlockSpec((tm, W), lambda i: (i, 0)),
        out_shape=jax.ShapeDtypeStruct((nrows, W), BF16),
        compiler_params=_cparams("parallel"),
    )(z, w, cos2, sin2)


def _qknorm_bwd(z, cblk, nrows, roff, w, cos2, sin2, dy, nh, name):
    W = nh * HEAD_DIM
    tm = _pick(math.gcd(nrows, roff), 256, 8)
    ro = roff // tm

    def body(z_ref, w_ref, c_ref, s_ref, dy_ref, dz_ref, acc_ref):
        i = pl.program_id(0)

        @pl.when(i == 0)
        def _():
            acc_ref[...] = jnp.zeros_like(acc_ref)

        c, s, wv = c_ref[...], s_ref[...], w_ref[...]
        dw = jnp.zeros((1, HEAD_DIM), F32)
        for h in range(nh):
            sl = slice(h * HEAD_DIM, (h + 1) * HEAD_DIM)
            x = z_ref[:, sl]
            d = dy_ref[:, sl]
            dyn = d * c + pltpu.roll(d * s, HEAD_DIM // 2, 1)
            r = lax.rsqrt(jnp.mean(x * x, axis=-1, keepdims=True) + EPS)
            xh = x * r
            dw = dw + jnp.sum(dyn * xh, axis=0, keepdims=True)
            dxh = dyn * wv
            dz_ref[:, sl] = (r * (dxh - xh * jnp.mean(dxh * xh, axis=-1, keepdims=True))).astype(BF16)
        acc_ref[0:1, :] += dw

    return pl.pallas_call(
        body, name=name, grid=(nrows // tm,),
        in_specs=[pl.BlockSpec((tm, W), lambda i: (i + ro, cblk)), pl.BlockSpec((1, HEAD_DIM), lambda i: (0, 0)),
                  pl.BlockSpec((tm, HEAD_DIM), lambda i: (i + ro, 0)), pl.BlockSpec((tm, HEAD_DIM), lambda i: (i + ro, 0)),
                  pl.BlockSpec((tm, W), lambda i: (i, 0))],
        out_specs=[pl.BlockSpec((tm, W), lambda i: (i, 0)), pl.BlockSpec((8, HEAD_DIM), lambda i: (0, 0))],
        out_shape=[jax.ShapeDtypeStruct((nrows, W), BF16), jax.ShapeDtypeStruct((8, HEAD_DIM), F32)],
        compiler_params=_cparams("arbitrary"),
    )(z, w, cos2, sin2, dy)


def _cast_seg(z, cblk, width, name):
    R = z.shape[0]
    tm = _pick(R, 512, 8)

    def body(z_ref, o_ref):
        o_ref[...] = z_ref[...].astype(BF16)

    return pl.pallas_call(
        body, name=name, grid=(R // tm,),
        in_specs=[pl.BlockSpec((tm, width), lambda i: (i, cblk))],
        out_specs=pl.BlockSpec((tm, width), lambda i: (i, 0)),
        out_shape=jax.ShapeDtypeStruct((R, width), BF16), compiler_params=_cparams("parallel"))(z)


NEG_BIG = -1e30


KV_PER_STEP = 2


def _attn_specs(T, n_ctx):
    nb = T // WINDOW
    lb = n_ctx // WINDOW
    kvw = KV_PER_STEP * HEAD_DIM
    blk = lambda f: pl.BlockSpec((WINDOW, kvw), f)
    win = [blk(lambda h, i: (lb + jnp.maximum(i - 1, 0), h)), blk(lambda h, i: (lb + i, h)),
           blk(lambda h, i: (lb + jnp.minimum(i + 1, nb - 1), h))]
    ctx = pl.BlockSpec((n_ctx, kvw), lambda h, i: (0, h))
    qspec = pl.BlockSpec((WINDOW, KV_PER_STEP * Q_PER_KV * HEAD_DIM), lambda h, i: (i, h))
    sink = pl.BlockSpec((N_Q_HEADS, HEAD_DIM), lambda h, i: (0, 0))
    return nb, qspec, win, ctx, sink


def _attn_probs(q, kw, kctx, snk, valid):
    scale = HEAD_DIM ** -0.5
    s_lat = jnp.where(valid, _dot(q, kw, NT) * scale, NEG_BIG)
    s_ctx = _dot(q, kctx, NT) * scale
    m = jnp.maximum(jnp.maximum(jnp.max(s_lat, axis=-1, keepdims=True), jnp.max(s_ctx, axis=-1, keepdims=True)), snk)
    p_lat = jnp.exp(s_lat - m)
    p_ctx = jnp.exp(s_ctx - m)
    p_snk = jnp.exp(snk - m)
    den = p_snk + jnp.sum(p_lat, axis=-1, keepdims=True) + jnp.sum(p_ctx, axis=-1, keepdims=True)
    return p_lat, p_ctx, p_snk, den


def _attn_valid(i, T, heads):
    rows = heads * WINDOW
    qpos = i * WINDOW + (lax.broadcasted_iota(jnp.int32, (rows, 3 * WINDOW), 0) & (WINDOW - 1))
    kpos = (i - 1) * WINDOW + lax.broadcasted_iota(jnp.int32, (rows, 3 * WINDOW), 1)
    return (jnp.abs(qpos - kpos) <= WINDOW) & (kpos >= 0) & (kpos < T)


def _stack_heads(ref, hh):
    c0 = hh * Q_PER_KV * HEAD_DIM
    return jnp.concatenate([ref[:, c0 + g * HEAD_DIM:c0 + (g + 1) * HEAD_DIM] for g in range(Q_PER_KV)], axis=0)


def _stack_sinks(sink_ref, kvh):
    return jnp.concatenate([jnp.broadcast_to(sink_ref[pl.ds(kvh * Q_PER_KV + g, 1), :][:, 0:1], (WINDOW, 1))
                            for g in range(Q_PER_KV)], axis=0)


def _attn_window(refs, hh):
    return jnp.concatenate([r[:, hh * HEAD_DIM:(hh + 1) * HEAD_DIM] for r in refs], axis=0)


def _attn_fwd(qn, kn, vb, sink_rows, n_ctx, name, ride=None):
    T = qn.shape[0]
    nb, qspec, win, ctx, sink = _attn_specs(T, n_ctx)

    def body(q_ref, kp, kc, kx, vp, vc, vx, kctx_ref, vctx_ref, sink_ref, o_ref):
        h, i = pl.program_id(0), pl.program_id(1)
        valid = _attn_valid(i, T, Q_PER_KV)
        for hh in range(KV_PER_STEP):
            sl = slice(hh * HEAD_DIM, (hh + 1) * HEAD_DIM)
            kw, vw = _attn_window((kp, kc, kx), hh), _attn_window((vp, vc, vx), hh)
            kctx, vctx = kctx_ref[:, sl], vctx_ref[:, sl]
            p_lat, p_ctx, _, den = _attn_probs(_stack_heads(q_ref, hh), kw, kctx,
                                               _stack_sinks(sink_ref, h * KV_PER_STEP + hh), valid)
            o = ((_dot(p_lat.astype(BF16), vw, NN) + _dot(p_ctx.astype(BF16), vctx, NN)) / den).astype(BF16)
            for g in range(Q_PER_KV):
                c0 = (hh * Q_PER_KV + g) * HEAD_DIM
                o_ref[:, c0:c0 + HEAD_DIM] = o[g * WINDOW:(g + 1) * WINDOW]

    return _pcall(
        body, name=name, grid=(N_KV_HEADS // KV_PER_STEP, nb),
        in_specs=[qspec] + win + win + [ctx, ctx, sink],
        out_specs=qspec, out_shape=jax.ShapeDtypeStruct(qn.shape, BF16),
        sem=("parallel", "parallel"), args=(qn, kn, kn, kn, vb, vb, vb, kn, vb, sink_rows), ride=ride)


def _attn_bwd(qn, kn, vb, sink_rows, do, n_ctx, name, ride=None):
    T = qn.shape[0]
    nb, qspec, win, ctx, sink = _attn_specs(T, n_ctx)
    scale = HEAD_DIM ** -0.5
    TP = T + 2 * WINDOW

    def body(q_ref, kp, kc, kx, vp, vc, vx, kctx_ref, vctx_ref, sink_ref, do_ref,
             dq_ref, dkw_ref, dvw_ref, dkc_ref, dvc_ref, dsn_ref):
        h, i = pl.program_id(0), pl.program_id(1)

        @pl.when(i == 0)
        def _():
            dkw_ref[...] = jnp.zeros_like(dkw_ref)
            dvw_ref[...] = jnp.zeros_like(dvw_ref)
            dkc_ref[...] = jnp.zeros_like(dkc_ref)
            dvc_ref[...] = jnp.zeros_like(dvc_ref)
            dsn_ref[...] = jnp.zeros_like(dsn_ref)

        lane = lax.broadcasted_iota(jnp.int32, (8, HEAD_DIM), 1)
        valid = _attn_valid(i, T, Q_PER_KV)
        rows = pl.ds(pl.multiple_of(i * WINDOW, WINDOW), 3 * WINDOW)
        for hh in range(KV_PER_STEP):
            sl = slice(hh * HEAD_DIM, (hh + 1) * HEAD_DIM)
            kw, vw = _attn_window((kp, kc, kx), hh), _attn_window((vp, vc, vx), hh)
            kctx, vctx = kctx_ref[:, sl], vctx_ref[:, sl]
            q, d_o = _stack_heads(q_ref, hh), _stack_heads(do_ref, hh)
            p_lat, p_ctx, p_snk, den = _attn_probs(q, kw, kctx, _stack_sinks(sink_ref, h * KV_PER_STEP + hh), valid)
            inv = 1.0 / den
            p_lat, p_ctx, p_snk = p_lat * inv, p_ctx * inv, p_snk * inv
            dp_lat = _dot(d_o, vw, NT)
            dp_ctx = _dot(d_o, vctx, NT)
            dr = jnp.sum(p_lat * dp_lat, axis=-1, keepdims=True) + jnp.sum(p_ctx * dp_ctx, axis=-1, keepdims=True)
            ds_lat = (p_lat * (dp_lat - dr) * scale).astype(BF16)
            ds_ctx = (p_ctx * (dp_ctx - dr) * scale).astype(BF16)
            dq = _dot(ds_lat, kw, NN) + _dot(ds_ctx, kctx, NN)
            snk_terms = p_snk * dr
            dsn = jnp.zeros((8, HEAD_DIM), F32)
            for g in range(Q_PER_KV):
                c0 = (hh * Q_PER_KV + g) * HEAD_DIM
                dq_ref[:, c0:c0 + HEAD_DIM] = dq[g * WINDOW:(g + 1) * WINDOW]
                dsn = dsn + jnp.where(lane == g, -jnp.sum(snk_terms[g * WINDOW:(g + 1) * WINDOW], axis=0, keepdims=True),
                                      0.0)
            dkw_ref[rows, sl] += _dot(ds_lat, q, TN)
            dvw_ref[rows, sl] += _dot(p_lat.astype(BF16), d_o, TN)
            dkc_ref[:, sl] += _dot(ds_ctx, q, TN)
            dvc_ref[:, sl] += _dot(p_ctx.astype(BF16), d_o, TN)
            dsn_ref[hh] += dsn

    wacc = pl.BlockSpec((TP, KV_PER_STEP * HEAD_DIM), lambda h, i: (0, h))
    return _pcall(
        body, name=name, grid=(N_KV_HEADS // KV_PER_STEP, nb),
        in_specs=[qspec] + win + win + [ctx, ctx, sink, qspec],
        out_specs=[qspec, wacc, wacc, ctx, ctx, pl.BlockSpec((KV_PER_STEP, 8, HEAD_DIM), lambda h, i: (h, 0, 0))],
        out_shape=[jax.ShapeDtypeStruct(qn.shape, F32),
                   jax.ShapeDtypeStruct((TP, N_KV_HEADS * HEAD_DIM), F32),
                   jax.ShapeDtypeStruct((TP, N_KV_HEADS * HEAD_DIM), F32),
                   jax.ShapeDtypeStruct((n_ctx, N_KV_HEADS * HEAD_DIM), F32),
                   jax.ShapeDtypeStruct((n_ctx, N_KV_HEADS * HEAD_DIM), F32),
                   jax.ShapeDtypeStruct((N_KV_HEADS, 8, HEAD_DIM), F32)],
        sem=("arbitrary", "arbitrary"), args=(qn, kn, kn, kn, vb, vb, vb, kn, vb, sink_rows, do), ride=ride)


def _gla_masks(dirv):
    C = GLA_CHUNK

    def times(reps):
        r = lax.broadcasted_iota(jnp.int32, (C, reps * C), 0)
        c = lax.broadcasted_iota(jnp.int32, (C, reps * C), 1) & (C - 1)
        return jnp.where(dirv == 0, r, C - 1 - r), jnp.where(dirv == 0, c, C - 1 - c)

    def level(tt, ss, m):
        sh = m.bit_length() - 1
        same = (tt >> (sh + 1)) == (ss >> (sh + 1))
        return same, (tt >> sh) & 1, (ss >> sh) & 1

    tt, ss = times(3)
    le = (ss <= tt).astype(jnp.int32)
    sums = [le == 1]
    for m in GLA_LEVELS:
        same, ut, us = level(tt, ss, m)
        sums.append(same & (ut == us) & (ut == le))
    tt, ss = times(1)
    blocks = [ss == tt]
    for m in GLA_LEVELS:
        same, ut, us = level(tt, ss, m)
        blocks.append(same & (ut == 1) & (us == 0))
    mall3 = jnp.concatenate([jnp.where(s, 1.0, 0.0) for s in sums], axis=0).astype(BF16)
    return mall3, blocks


def _pieces(x):
    hi = x.astype(BF16)
    r1 = x - hi.astype(F32)
    mid = r1.astype(BF16)
    return hi, mid, (r1 - mid.astype(F32)).astype(BF16)


def _sum_f32(mall3, x):
    return _dot(mall3, jnp.concatenate(_pieces(x), axis=0), NN)


def _sum_f32_t(mall3, x):
    m = mall3[:, 0:GLA_CHUNK]
    hi, mid, lo = _pieces(x)
    return _dot(m, hi, TN) + _dot(m, mid, TN) + _dot(m, lo, TN)


def _gla_chunk_of(dirv, j, lc, nc):
    return jnp.where(dirv == 0, j, jnp.where(j < lc, lc - 1 - j, nc + lc - 1 - j))


def _gla_gate(lr_ref, wg_ref, bg_ref):
    pre = _dot(lr_ref[...].astype(BF16), wg_ref[0].astype(BF16), NN) + bg_ref[0]
    g = (jnp.minimum(pre, 0.0) - jnp.log(1.0 + jnp.exp(-jnp.abs(pre)))) * (1.0 / GLA_GATE_NORM)
    return pre, g


def _gla_fwd(z, qblk, kblk, vblk, lrblk, wg, bg, DV, n_ctx, name):
    R = z.shape[0]
    C = GLA_CHUNK
    DK = wg.shape[2] // GLA_HEADS
    nc, lc = R // C, n_ctx // C
    qscale = DK ** -0.5

    GK, GV = GLA_HEADS * DK, GLA_HEADS * DV

    def body(q_ref, k_ref, v_ref, lr_ref, wg_ref, bg_ref, o_ref, sp_ref, st_ref):
        dirv, j = pl.program_id(0), pl.program_id(1)

        @pl.when(j == 0)
        def _():
            st_ref[...] = jnp.zeros_like(st_ref)

        mall, blocks = _gla_masks(dirv)
        _, g_all = _gla_gate(lr_ref, wg_ref, bg_ref)
        E_all = _sum_f32(mall, g_all)
        for h in range(GLA_HEADS):
            ks, vs = slice(h * DK, (h + 1) * DK), slice(h * DV, (h + 1) * DV)
            q, k, v = q_ref[:, ks] * qscale, k_ref[:, ks], v_ref[:, vs].astype(BF16)
            g, E = g_all[:, ks], E_all[:, ks]
            st = st_ref[h]
            sp_ref[0, h, 0] = st
            A = jnp.where(blocks[0], _dot(q.astype(BF16), k.astype(BF16), NT), 0.0)
            for l in range(len(GLA_LEVELS)):
                e = jnp.exp(E[(1 + l) * C:(2 + l) * C])
                A = A + jnp.where(blocks[l + 1], _dot((q * e).astype(BF16), (k * e).astype(BF16), NT), 0.0)
            o_ref[0, :, vs] = (_dot((q * jnp.exp(E[0:C])).astype(BF16), st.astype(BF16), NT)
                               + _dot(A.astype(BF16), v, NN))
            last = jnp.sum(g, axis=0, keepdims=True)
            st_ref[h] = jnp.exp(last) * st + _dot(v, (k * jnp.exp(last - E[0:C])).astype(BF16), TN)

    chunk = functools.partial(_gla_chunk_of, lc=lc, nc=nc)
    return pl.pallas_call(
        body, name=name, grid=(2, nc),
        in_specs=[pl.BlockSpec((C, GK), lambda d, j: (chunk(d, j), qblk)),
                  pl.BlockSpec((C, GK), lambda d, j: (chunk(d, j), kblk)),
                  pl.BlockSpec((C, GV), lambda d, j: (chunk(d, j), vblk)),
                  pl.BlockSpec((C, LANES), lambda d, j: (chunk(d, j), lrblk)),
                  pl.BlockSpec((1, LANES, GK), lambda d, j: (d, 0, 0)),
                  pl.BlockSpec((1, 1, GK), lambda d, j: (d, 0, 0))],
        out_specs=[pl.BlockSpec((1, C, GV), lambda d, j: (d, chunk(d, j), 0)),
                   pl.BlockSpec((1, GLA_HEADS, 1, DV, DK), lambda d, j: (d, 0, j, 0, 0))],
        out_shape=[jax.ShapeDtypeStruct((2, R, GV), F32),
                   jax.ShapeDtypeStruct((2, GLA_HEADS, nc, DV, DK), F32)],
        scratch_shapes=[pltpu.VMEM((GLA_HEADS, DV, DK), F32)],
        compiler_params=_cparams("parallel", "arbitrary"),
    )(z, z, z, z, wg, bg)


def _gla_bwd(z, qblk, kblk, vblk, lrblk, wg, bg, sprev, do, n_ctx, name, ride=None):
    R = z.shape[0]
    C = GLA_CHUNK
    DK, DV = wg.shape[2] // GLA_HEADS, do.shape[1] // GLA_HEADS
    nc, lc = R // C, n_ctx // C
    qscale = DK ** -0.5
    nl = len(GLA_LEVELS)

    GK, GV = GLA_HEADS * DK, GLA_HEADS * DV

    def body(q_ref, k_ref, v_ref, lr_ref, wg_ref, bg_ref, sp_ref, do_ref,
             dq_ref, dk_ref, dv_ref, dpre_ref, dbg_ref, dst_ref):
        dirv, jr = pl.program_id(0), pl.program_id(1)

        @pl.when(jr == 0)
        def _():
            dst_ref[...] = jnp.zeros_like(dst_ref)
            dbg_ref[...] = jnp.zeros_like(dbg_ref)

        mall, blocks = _gla_masks(dirv)
        pre_all, g_all = _gla_gate(lr_ref, wg_ref, bg_ref)
        E_all = _sum_f32(mall, g_all)
        for h in range(GLA_HEADS):
            ks, vs = slice(h * DK, (h + 1) * DK), slice(h * DV, (h + 1) * DV)
            q, k, v = q_ref[:, ks] * qscale, k_ref[:, ks], v_ref[:, vs].astype(BF16)
            pre, g, E = pre_all[:, ks], g_all[:, ks], E_all[:, ks]
            last = jnp.sum(g, axis=0, keepdims=True)
            eb, er, decay = jnp.exp(E[0:C]), jnp.exp(last - E[0:C]), jnp.exp(last)
            st = sp_ref[0, h, 0]
            dst = dst_ref[h]
            d_o = do_ref[:, vs]
            qe, kd = q * eb, k * er
            qb, kb = q.astype(BF16), k.astype(BF16)
            A = jnp.where(blocks[0], _dot(qb, kb, NT), 0.0)
            for l in range(nl):
                e = jnp.exp(E[(1 + l) * C:(2 + l) * C])
                A = A + jnp.where(blocks[l + 1], _dot((q * e).astype(BF16), (k * e).astype(BF16), NT), 0.0)
            dA = _dot(d_o, v, NT)
            dv_ref[0, :, vs] = _dot(A.astype(BF16), d_o, TN) + _dot(kd.astype(BF16), dst.astype(BF16), NT)
            dqe = _dot(d_o, st.astype(BF16), NN)
            dkd = _dot(v, dst.astype(BF16), NN)
            G = jnp.where(blocks[0], dA, 0.0).astype(BF16)
            dq = dqe * eb + _dot(G, kb, NN)
            dk = dkd * er + _dot(G, qb, TN)
            dEr = dkd * kd
            dE = [dqe * qe - dEr]
            for l in range(nl):
                e = jnp.exp(E[(1 + l) * C:(2 + l) * C])
                ql, kl = q * e, k * e
                G = jnp.where(blocks[l + 1], dA, 0.0).astype(BF16)
                dql = _dot(G, kl.astype(BF16), NN)
                dkl = _dot(G, ql.astype(BF16), TN)
                dq = dq + dql * e
                dk = dk + dkl * e
                dE.append(dql * ql + dkl * kl)
            dlast = jnp.sum(dst * st, axis=0, keepdims=True) * decay + jnp.sum(dEr, axis=0, keepdims=True)
            dg = _sum_f32_t(mall, jnp.concatenate(dE, axis=0)) + dlast
            dpre = dg * (1.0 / GLA_GATE_NORM) / (1.0 + jnp.exp(pre))
            dq_ref[0, :, ks] = dq * qscale
            dk_ref[0, :, ks] = dk
            dpre_ref[:, ks] = dpre.astype(BF16)
            dbg_ref[0, :, ks] += jnp.sum(dpre, axis=0, keepdims=True)
            dst_ref[h] = decay * dst + _dot(d_o, qe.astype(BF16), TN)

    def chunk(d, jr):
        return _gla_chunk_of(d, nc - 1 - jr, lc, nc)

    return _pcall(
        body, name=name, grid=(2, nc),
        in_specs=[pl.BlockSpec((C, GK), lambda d, j: (chunk(d, j), qblk)),
                  pl.BlockSpec((C, GK), lambda d, j: (chunk(d, j), kblk)),
                  pl.BlockSpec((C, GV), lambda d, j: (chunk(d, j), vblk)),
                  pl.BlockSpec((C, LANES), lambda d, j: (chunk(d, j), lrblk)),
                  pl.BlockSpec((1, LANES, GK), lambda d, j: (d, 0, 0)),
                  pl.BlockSpec((1, 1, GK), lambda d, j: (d, 0, 0)),
                  pl.BlockSpec((1, GLA_HEADS, 1, DV, DK), lambda d, j: (d, 0, nc - 1 - j, 0, 0)),
                  pl.BlockSpec((C, GV), lambda d, j: (chunk(d, j), 0))],
        out_specs=[pl.BlockSpec((1, C, GK), lambda d, j: (d, chunk(d, j), 0)),
                   pl.BlockSpec((1, C, GK), lambda d, j: (d, chunk(d, j), 0)),
                   pl.BlockSpec((1, C, GV), lambda d, j: (d, chunk(d, j), 0)),
                   pl.BlockSpec((C, GK), lambda d, j: (chunk(d, j), d)),
                   pl.BlockSpec((1, 1, GK), lambda d, j: (d, 0, 0))],
        out_shape=[jax.ShapeDtypeStruct((2, R, GK), F32),
                   jax.ShapeDtypeStruct((2, R, GK), F32),
                   jax.ShapeDtypeStruct((2, R, GV), F32),
                   jax.ShapeDtypeStruct((R, 2 * GK), BF16),
                   jax.ShapeDtypeStruct((2, 1, GK), F32)],
        scratch_shapes=[pltpu.VMEM((GLA_HEADS, DV, DK), F32)],
        sem=("arbitrary", "arbitrary"), args=(z, z, z, z, wg, bg, sprev, do), ride=ride)


def _glanorm_fwd(o, z, rbblk, gn, n_ctx, name):
    _, R, GV = o.shape
    T = R - n_ctx
    DV = GV // GLA_HEADS
    tm = _pick(n_ctx, 256, 8)
    ro = n_ctx // tm

    def body(o0_ref, o1_ref, rb_ref, gn_ref, p_ref):
        gnv = gn_ref[...]
        for h in range(GLA_HEADS):
            sl = slice(h * DV, (h + 1) * DV)
            og = o0_ref[0, :, sl] + o1_ref[0, :, sl]
            r = lax.rsqrt(jnp.mean(og * og, axis=-1, keepdims=True) + EPS)
            p_ref[:, sl] = (og * r * gnv * _silu(rb_ref[:, sl])).astype(BF16)

    return pl.pallas_call(
        body, name=name, grid=(T // tm,),
        in_specs=[pl.BlockSpec((1, tm, GV), lambda i: (0, i + ro, 0)), pl.BlockSpec((1, tm, GV), lambda i: (1, i + ro, 0)),
                  pl.BlockSpec((tm, GV), lambda i: (i + ro, rbblk)), pl.BlockSpec((1, DV), lambda i: (0, 0))],
        out_specs=pl.BlockSpec((tm, GV), lambda i: (i, 0)),
        out_shape=jax.ShapeDtypeStruct((T, GV), BF16), compiler_params=_cparams("parallel"))(o, o, z, gn)


def _glanorm_bwd(o, z, rbblk, gn, dp, n_ctx, name):
    _, R, GV = o.shape
    T = R - n_ctx
    DV = GV // GLA_HEADS
    tm = _pick(n_ctx, 256, 8)
    ro = n_ctx // tm

    def body(o0_ref, o1_ref, rb_ref, gn_ref, dp_ref, do_ref, drb_ref, acc_ref):
        i = pl.program_id(0)

        @pl.when(i == 0)
        def _():
            acc_ref[...] = jnp.zeros_like(acc_ref)

        gnv = gn_ref[...]
        dgn = jnp.zeros((1, DV), F32)
        for h in range(GLA_HEADS):
            sl = slice(h * DV, (h + 1) * DV)
            og = o0_ref[0, :, sl] + o1_ref[0, :, sl]
            rb = rb_ref[:, sl]
            d = dp_ref[:, sl]
            r = lax.rsqrt(jnp.mean(og * og, axis=-1, keepdims=True) + EPS)
            xh = og * r
            drb_ref[:, sl] = (d * xh * gnv * _dsilu(rb)).astype(BF16)
            dn = d * _silu(rb)
            dgn = dgn + jnp.sum(dn * xh, axis=0, keepdims=True)
            dxh = dn * gnv
            do_ref[:, sl] = (r * (dxh - xh * jnp.mean(dxh * xh, axis=-1, keepdims=True))).astype(BF16)
        acc_ref[0:1, :] += dgn

    row = pl.BlockSpec((tm, GV), lambda i: (i, 0))
    return pl.pallas_call(
        body, name=name, grid=(T // tm,),
        in_specs=[pl.BlockSpec((1, tm, GV), lambda i: (0, i + ro, 0)), pl.BlockSpec((1, tm, GV), lambda i: (1, i + ro, 0)),
                  pl.BlockSpec((tm, GV), lambda i: (i + ro, rbblk)), pl.BlockSpec((1, DV), lambda i: (0, 0)), row],
        out_specs=[row, row, pl.BlockSpec((8, DV), lambda i: (0, 0))],
        out_shape=[jax.ShapeDtypeStruct((T, GV), BF16), jax.ShapeDtypeStruct((T, GV), BF16),
                   jax.ShapeDtypeStruct((8, DV), F32)],
        compiler_params=_cparams("arbitrary"))(o, o, z, gn, dp)


def _gate_fwd(z, gablk, gbblk, ya, yg, n_ctx, name):
    T, D = ya.shape
    tm = _pick(n_ctx, 256, 8)
    ro = n_ctx // tm

    def body(ga_ref, gb_ref, ya_ref, yg_ref, m_ref):
        m_ref[...] = (_sigmoid(ga_ref[...]) * ya_ref[...] + _sigmoid(gb_ref[...]) * yg_ref[...]).astype(BF16)

    row = pl.BlockSpec((tm, D), lambda i: (i, 0))
    return pl.pallas_call(
        body, name=name, grid=(T // tm,),
        in_specs=[pl.BlockSpec((tm, D), lambda i: (i + ro, gablk)), pl.BlockSpec((tm, D), lambda i: (i + ro, gbblk)), row, row],
        out_specs=row, out_shape=jax.ShapeDtypeStruct((T, D), BF16), compiler_params=_cparams("parallel"))(z, z, ya, yg)


def _gate_bwd(z, gablk, gbblk, ya, yg, dm, n_ctx, name):
    T, D = ya.shape
    tm = _pick(n_ctx, 256, 8)
    ro = n_ctx // tm

    def body(ga_ref, gb_ref, ya_ref, yg_ref, dm_ref, dya_ref, dyg_ref, dga_ref, dgb_ref):
        d = dm_ref[...]
        sa, sb = _sigmoid(ga_ref[...]), _sigmoid(gb_ref[...])
        dya_ref[...] = (d * sa).astype(BF16)
        dyg_ref[...] = (d * sb).astype(BF16)
        dga_ref[...] = (d * ya_ref[...] * sa * (1.0 - sa)).astype(BF16)
        dgb_ref[...] = (d * yg_ref[...] * sb * (1.0 - sb)).astype(BF16)

    row = pl.BlockSpec((tm, D), lambda i: (i, 0))
    sh = jax.ShapeDtypeStruct((T, D), BF16)
    return pl.pallas_call(
        body, name=name, grid=(T // tm,),
        in_specs=[pl.BlockSpec((tm, D), lambda i: (i + ro, gablk)), pl.BlockSpec((tm, D), lambda i: (i + ro, gbblk)), row, row, row],
        out_specs=[row] * 4, out_shape=[sh] * 4, compiler_params=_cparams("parallel"))(z, z, ya, yg, dm)


def _resnorm_fwd(x, mix, gt, g, sc, sh, name):
    T, D = x.shape
    tm = _pick(T, 256, 8)

    def body(x_ref, mix_ref, gt_ref, g_ref, sc_ref, sh_ref, x1_ref, h_ref):
        x1 = x_ref[...] + gt_ref[...] * mix_ref[...]
        x1_ref[...] = x1
        r = lax.rsqrt(jnp.mean(x1 * x1, axis=-1, keepdims=True) + EPS)
        h_ref[...] = (x1 * r * g_ref[...] * (1.0 + sc_ref[...]) + sh_ref[...]).astype(BF16)

    row = pl.BlockSpec((tm, D), lambda i: (i, 0))
    vec = pl.BlockSpec((1, D), lambda i: (0, 0))
    return pl.pallas_call(
        body, name=name, grid=(T // tm,), in_specs=[row, row, vec, vec, vec, vec], out_specs=[row, row],
        out_shape=[jax.ShapeDtypeStruct((T, D), F32), jax.ShapeDtypeStruct((T, D), BF16)],
        compiler_params=_cparams("parallel"))(x, mix, gt, g, sc, sh)


def _loss_head(d, x1, gt, target, name):
    T, D = d.shape
    tm = _pick(T, 256, 8)

    def body(d_ref, x1_ref, gt_ref, t_ref, dy_ref, dd_ref, acc_ref):
        i = pl.program_id(0)

        @pl.when(i == 0)
        def _():
            acc_ref[...] = jnp.zeros_like(acc_ref)

        dv, gtv = d_ref[...], gt_ref[...]
        e = x1_ref[...] + gtv * dv - t_ref[...]
        dy = e * (1.0 / D)
        dy_ref[...] = dy
        dd_ref[...] = (dy * gtv).astype(BF16)
        acc_ref[0:1, :] += jnp.sum(e * e, axis=0, keepdims=True)
        acc_ref[1:2, :] += jnp.sum(dy * dv, axis=0, keepdims=True)

    row = pl.BlockSpec((tm, D), lambda i: (i, 0))
    return pl.pallas_call(
        body, name=name, grid=(T // tm,), in_specs=[row, row, pl.BlockSpec((1, D), lambda i: (0, 0)), row],
        out_specs=[row, row, pl.BlockSpec((8, D), lambda i: (0, 0))],
        out_shape=[jax.ShapeDtypeStruct((T, D), F32), jax.ShapeDtypeStruct((T, D), BF16),
                   jax.ShapeDtypeStruct((8, D), F32)],
        compiler_params=_cparams("arbitrary"))(d, x1, gt, target)


def _halo_specs(T, tm, tw, col_of, order):
    n8 = tm // 8
    if order == "ij":
        mid = lambda i, j: (i, col_of(j))
        prev = lambda i, j: (jnp.maximum(i * n8 - 1, 0), col_of(j))
        nxt = lambda i, j: (jnp.minimum((i + 1) * n8, T // 8 - 1), col_of(j))
    else:
        mid = lambda j, i: (i, col_of(j))
        prev = lambda j, i: (jnp.maximum(i * n8 - 1, 0), col_of(j))
        nxt = lambda j, i: (jnp.minimum((i + 1) * n8, T // 8 - 1), col_of(j))
    return [pl.BlockSpec((tm, tw), mid), pl.BlockSpec((8, tw), prev), pl.BlockSpec((8, tw), nxt)]


def _shift_rows(x, before, after):
    tm = x.shape[0]
    row = lax.broadcasted_iota(jnp.int32, x.shape, 0)
    return (jnp.where(row == 0, before, pltpu.roll(x, 1, 0)),
            jnp.where(row == tm - 1, after, pltpu.roll(x, tm - 1, 0)))


def _conv_fwd(u, cw, cb, name):
    T, F2 = u.shape
    F = F2 // 2
    tm, tw = _pick(T, 256, 8), _pick(F, 512)
    nt, nw = T // tm, F // tw

    def body(ua, uap, uan, ug, ugp, ugn, cwa, cwg, cba, cbg, f_ref):
        i = pl.program_id(0)
        first, last = i == 0, i == nt - 1

        def conv(u_ref, up_ref, un_ref, w_ref, b_ref):
            m = u_ref[...]
            p, n = _shift_rows(m, jnp.where(first, 0.0, up_ref[7:8, :]), jnp.where(last, 0.0, un_ref[0:1, :]))
            return p * w_ref[0:1, :] + m * w_ref[1:2, :] + n * w_ref[2:3, :] + b_ref[...]

        a = conv(ua, uap, uan, cwa, cba)
        g = conv(ug, ugp, ugn, cwg, cbg)
        f_ref[...] = (_silu(a) * g).astype(BF16)

    wspec = lambda off: pl.BlockSpec((3, tw), lambda i, j: (0, j + off))
    bspec = lambda off: pl.BlockSpec((1, tw), lambda i, j: (0, j + off))
    return pl.pallas_call(
        body, name=name, grid=(nt, nw),
        in_specs=_halo_specs(T, tm, tw, lambda j: j, "ij") + _halo_specs(T, tm, tw, lambda j: j + nw, "ij")
        + [wspec(0), wspec(nw), bspec(0), bspec(nw)],
        out_specs=pl.BlockSpec((tm, tw), lambda i, j: (i, j)),
        out_shape=jax.ShapeDtypeStruct((T, F), BF16),
        compiler_params=_cparams("parallel", "parallel"),
    )(u, u, u, u, u, u, cw, cw, cb, cb)


def _conv_bwd(u, df, cw, cb, name):
    T, F2 = u.shape
    F = F2 // 2
    tm, tw = _pick(T, 256, 8), _pick(F, 512)
    nt, nw = T // tm, F // tw

    def body(ua, uap, uan, ug, ugp, ugn, cwa, cwg, cba, cbg, df_ref, dfp, dfn, dua_ref, dug_ref, acca_ref, accg_ref):
        i = pl.program_id(1)

        @pl.when(i == 0)
        def _():
            acca_ref[...] = jnp.zeros_like(acca_ref)
            accg_ref[...] = jnp.zeros_like(accg_ref)

        first, last = i == 0, i == nt - 1
        wa, wg, ba, bg = cwa[...], cwg[...], cba[...], cbg[...]

        def conv(p, m, n, w, b):
            return p * w[0:1] + m * w[1:2] + n * w[2:3] + b

        def grads(a, g, d):
            return d * g * _dsilu(a), d * _silu(a)

        xa, xg, d = ua[...], ug[...], df_ref[...]
        sa = _shift_rows(xa, jnp.where(first, 0.0, uap[7:8, :]), jnp.where(last, 0.0, uan[0:1, :]))
        sg = _shift_rows(xg, jnp.where(first, 0.0, ugp[7:8, :]), jnp.where(last, 0.0, ugn[0:1, :]))
        da, dg = grads(conv(sa[0], xa, sa[1], wa, ba), conv(sg[0], xg, sg[1], wg, bg), d)
        da_p, dg_p = grads(conv(uap[6:7, :], uap[7:8, :], xa[0:1], wa, ba),
                           conv(ugp[6:7, :], ugp[7:8, :], xg[0:1], wg, bg), dfp[7:8, :])
        da_n, dg_n = grads(conv(xa[tm - 1:tm], uan[0:1, :], uan[1:2, :], wa, ba),
                           conv(xg[tm - 1:tm], ugn[0:1, :], ugn[1:2, :], wg, bg), dfn[0:1, :])
        ta = _shift_rows(da, jnp.where(first, 0.0, da_p), jnp.where(last, 0.0, da_n))
        tg = _shift_rows(dg, jnp.where(first, 0.0, dg_p), jnp.where(last, 0.0, dg_n))
        dua_ref[...] = (ta[1] * wa[0:1] + da * wa[1:2] + ta[0] * wa[2:3]).astype(BF16)
        dug_ref[...] = (tg[1] * wg[0:1] + dg * wg[1:2] + tg[0] * wg[2:3]).astype(BF16)
        for t, (va, vg) in enumerate(((sa[0], sg[0]), (xa, xg), (sa[1], sg[1]))):
            acca_ref[t:t + 1, :] += jnp.sum(da * va, axis=0, keepdims=True)
            accg_ref[t:t + 1, :] += jnp.sum(dg * vg, axis=0, keepdims=True)
        acca_ref[3:4, :] += jnp.sum(da, axis=0, keepdims=True)
        accg_ref[3:4, :] += jnp.sum(dg, axis=0, keepdims=True)

    wspec = lambda off: pl.BlockSpec((3, tw), lambda j, i: (0, j + off))
    bspec = lambda off: pl.BlockSpec((1, tw), lambda j, i: (0, j + off))
    row = pl.BlockSpec((tm, tw), lambda j, i: (i, j))
    acc = pl.BlockSpec((8, tw), lambda j, i: (0, j))
    return pl.pallas_call(
        body, name=name, grid=(nw, nt),
        in_specs=_halo_specs(T, tm, tw, lambda j: j, "ji") + _halo_specs(T, tm, tw, lambda j: j + nw, "ji")
        + [wspec(0), wspec(nw), bspec(0), bspec(nw)] + _halo_specs(T, tm, tw, lambda j: j, "ji"),
        out_specs=[row, row, acc, acc],
        out_shape=[jax.ShapeDtypeStruct((T, F), BF16), jax.ShapeDtypeStruct((T, F), BF16),
                   jax.ShapeDtypeStruct((8, F), F32), jax.ShapeDtypeStruct((8, F), F32)],
        compiler_params=_cparams("parallel", "arbitrary"),
    )(u, u, u, u, u, u, cw, cw, cb, cb, df, df, df)


def _assemble_dz(lay, z_used, Z, n_ctx, dqa, drb, dga, dgb, dka, dva, dvg, dqg, dkg, dlr, name):
    T = dqa.shape[0]
    R = T + n_ctx
    tm = _pick(n_ctx, 128, 8)
    cb = n_ctx // tm

    def body(dqa_ref, drb_ref, dga_ref, dgb_ref, dka_ref, dva_ref, dvg0, dvg1, dqg0, dqg1, dkg0, dkg1, dlr_ref, o_ref):
        lat = pl.program_id(0) >= cb

        def put(seg, val):
            o_ref[:, lay[seg]:lay[seg] + val.shape[1]] = val.astype(BF16)

        def lat_only(ref):
            v = ref[...]
            return jnp.where(lat, v, jnp.zeros_like(v))

        put("qa", lat_only(dqa_ref))
        put("rb", lat_only(drb_ref))
        put("ga", lat_only(dga_ref))
        put("gb", lat_only(dgb_ref))
        put("ka", dka_ref[...])
        put("va", dva_ref[...])
        put("vb", dvg0[0] + dvg1[0])
        put("qb", dqg0[0] + dqg1[0])
        put("kb", dkg0[0] + dkg1[0])
        put("lr", dlr_ref[...])
        if Z > z_used:
            o_ref[:, z_used:] = jnp.zeros((tm, Z - z_used), BF16)

    lat_spec = lambda a: pl.BlockSpec((tm, a.shape[1]), lambda i: (jnp.maximum(i - cb, 0), 0))
    all_spec = lambda a: pl.BlockSpec((tm, a.shape[1]), lambda i: (i, 0))
    dir_specs = lambda a: [pl.BlockSpec((1, tm, a.shape[2]), lambda i: (0, i, 0)),
                           pl.BlockSpec((1, tm, a.shape[2]), lambda i: (1, i, 0))]
    return pl.pallas_call(
        body, name=name, grid=(R // tm,),
        in_specs=[lat_spec(dqa), lat_spec(drb), lat_spec(dga), lat_spec(dgb), all_spec(dka), all_spec(dva)]
        + dir_specs(dvg) + dir_specs(dqg) + dir_specs(dkg) + [all_spec(dlr)],
        out_specs=pl.BlockSpec((tm, Z), lambda i: (i, 0)),
        out_shape=jax.ShapeDtypeStruct((R, Z), BF16), compiler_params=_cparams("parallel"),
    )(dqa, drb, dga, dgb, dka, dva, dvg, dvg, dqg, dqg, dkg, dkg, dlr)


def _mod_fwd(ca, w, b, name):
    n, D = ca.shape
    N = w.shape[1]
    tn = _pick(N, 512)

    def body(c_ref, w_ref, b_ref, o_ref, s_ref):
        s = _silu(c_ref[...])
        s_ref[...] = s
        o_ref[...] = _dot(s.astype(BF16), w_ref[...].astype(BF16), NN) + b_ref[...]

    return pl.pallas_call(
        body, name=name, grid=(N // tn,),
        in_specs=[pl.BlockSpec((n, D), lambda j: (0, 0)), pl.BlockSpec((D, tn), lambda j: (0, j)),
                  pl.BlockSpec((1, tn), lambda j: (0, j))],
        out_specs=[pl.BlockSpec((n, tn), lambda j: (0, j)), pl.BlockSpec((n, D), lambda j: (0, 0))],
        out_shape=[jax.ShapeDtypeStruct((n, N), F32), jax.ShapeDtypeStruct((n, D), F32)],
        compiler_params=_cparams("arbitrary"))(ca, w, b)


def _silu_bwd(dsil, ca, name):
    def body(d_ref, c_ref, o_ref):
        o_ref[...] = d_ref[...] * _dsilu(c_ref[...])

    return pl.pallas_call(body, name=name, out_shape=jax.ShapeDtypeStruct(ca.shape, F32))(dsil, ca)


def _adam_math(w, g, m, v):
    c1 = 1.0 - ADAM_B1 ** ADAM_STEP
    c2 = 1.0 - ADAM_B2 ** ADAM_STEP
    mn = ADAM_B1 * m + (1.0 - ADAM_B1) * g
    vn = ADAM_B2 * v + (1.0 - ADAM_B2) * (g * g)
    return -ADAM_LR * ((mn / c1) / (jnp.sqrt(vn / c2) + ADAM_EPS) + ADAM_WD * w), mn, vn


def _adamw(w, g, m, v, name, ride=None):
    Rw, Cw = w.shape
    tr = _pick(Rw, 128, 8)

    def body(w_ref, g_ref, m_ref, v_ref, d_ref, mo_ref, vo_ref):
        d_ref[...], mo_ref[...], vo_ref[...] = _adam_math(w_ref[...], g_ref[...], m_ref[...], v_ref[...])

    row = pl.BlockSpec((tr, Cw), lambda i: (i, 0))
    sh = jax.ShapeDtypeStruct((Rw, Cw), F32)
    return _pcall(body, name=name, grid=(Rw // tr,), in_specs=[row] * 4, out_specs=[row] * 3, out_shape=[sh] * 3,
                  sem=("parallel",), args=(w, g, m, v), ride=ride)


HBM_SPEC = pl.BlockSpec(memory_space=pltpu.HBM)


def _exchange(inputs, out_shapes, stages, name):
    n_in, n_out = len(inputs), len(out_shapes)
    n = sum(len(s) for s in stages)

    def body(*refs):
        ins, outs = refs[:n_in], refs[n_in:n_in + n_out]
        send_sems, recv_sems = refs[n_in + n_out:]
        k = 0
        for stage in stages:
            copies = _stage_copies(stage, ins, outs, send_sems, recv_sems, k)
            for cp in copies:
                cp.start()
            for cp in copies:
                cp.wait()
            k += len(stage)

    return pl.pallas_call(
        body, name=name, in_specs=[HBM_SPEC] * n_in, out_specs=[HBM_SPEC] * n_out, out_shape=out_shapes,
        scratch_shapes=[pltpu.SemaphoreType.DMA((n,)), pltpu.SemaphoreType.DMA((n,))],
    )(*inputs)


def _stage_copies(stage, ins, outs, send_sems, recv_sems, k0=0):
    me = (lax.axis_index("x"), lax.axis_index("y"), lax.axis_index("c"))
    copies = []
    for k, ((skind, sidx), sfn, didx, dfn, flip) in enumerate(stage):
        src = (ins if skind == "in" else outs)[sidx].at[sfn(*me)]
        dst = outs[didx].at[dfn(*me)]
        if flip == (0, 0, 0):
            copies.append(pltpu.make_async_copy(src, dst, send_sems.at[k0 + k]))
        else:
            peer = tuple(1 - a if f else a for a, f in zip(me, flip))
            copies.append(pltpu.make_async_remote_copy(src, dst, send_sems.at[k0 + k], recv_sems.at[k0 + k],
                                                       device_id=peer, device_id_type=MESH))
    return copies


def _pcall(body, *, name, grid, in_specs, out_specs, out_shape, scratch_shapes=(), sem, args, ride=None):
    many = isinstance(out_shape, (list, tuple))
    out_specs, out_shape = (list(out_specs), list(out_shape)) if many else ([out_specs], [out_shape])
    if ride is None:
        res = pl.pallas_call(body, name=name, grid=grid, in_specs=list(in_specs), out_specs=out_specs,
                             out_shape=out_shape, scratch_shapes=list(scratch_shapes),
                             compiler_params=_cparams(*sem))(*args)
        return res if many else res[0]
    x_in, x_out, stage, aliases = ride
    n_in, n_out, n_scr, n_xin, n_xout = len(in_specs), len(out_specs), len(scratch_shapes), len(x_in), len(x_out)

    def wrapped(*refs):
        ins, xins = refs[:n_in], refs[n_in:n_in + n_xin]
        o0 = n_in + n_xin
        outs, xouts = refs[o0:o0 + n_out], refs[o0 + n_out:o0 + n_out + n_xout]
        s0 = o0 + n_out + n_xout
        scr, (send_sems, recv_sems) = refs[s0:s0 + n_scr], refs[s0 + n_scr:]
        first = functools.reduce(jnp.logical_and, [pl.program_id(d) == 0 for d in range(len(grid))])
        last = functools.reduce(jnp.logical_and, [pl.program_id(d) == grid[d] - 1 for d in range(len(grid))])

        @pl.when(first)
        def _():
            for cp in _stage_copies(stage, xins, xouts, send_sems, recv_sems):
                cp.start()

        body(*ins, *outs, *scr)

        @pl.when(last)
        def _():
            for cp in _stage_copies(stage, xins, xouts, send_sems, recv_sems):
                cp.wait()

    res = pl.pallas_call(
        wrapped, name=name, grid=grid, in_specs=list(in_specs) + [HBM_SPEC] * n_xin,
        out_specs=out_specs + [HBM_SPEC] * n_xout, out_shape=out_shape + list(x_out),
        scratch_shapes=list(scratch_shapes) + [pltpu.SemaphoreType.DMA((len(stage),)),
                                               pltpu.SemaphoreType.DMA((len(stage),))],
        input_output_aliases={n_in + a: n_out + b for a, b in aliases.items()},
        compiler_params=_cparams(*(["arbitrary"] * len(grid))))(*args, *x_in)
    main = res[:n_out]
    return (main if many else main[0]), list(res[n_out:])


FLIPS_ALL = [(0, 0, 1), (0, 1, 0), (0, 1, 1), (1, 0, 0), (1, 0, 1), (1, 1, 0), (1, 1, 1)]
FLIPS_CHIP = [(0, 1, 0), (1, 0, 0), (1, 1, 0)]


def _sum_slots(buf, name):
    n, r, w = buf.shape
    tr = _pick(r, 256, 8)

    def body(b_ref, o_ref):
        acc = b_ref[0]
        for s in range(1, n):
            acc = acc + b_ref[s]
        o_ref[...] = acc

    return pl.pallas_call(
        body, name=name, grid=(r // tr,), in_specs=[pl.BlockSpec((n, tr, w), lambda i: (0, i, 0))],
        out_specs=pl.BlockSpec((tr, w), lambda i: (i, 0)), out_shape=jax.ShapeDtypeStruct((r, w), F32),
        compiler_params=_cparams("parallel"))(buf)


def _allreduce_plan(buf):
    whole = lambda x, y, c: (slice(None), slice(None))
    slot = lambda x, y, c: (4 * x + 2 * y + c,)
    stage = [(("in", 0), whole, 0, slot, f) for f in [(0, 0, 0)] + FLIPS_ALL]
    return [jax.ShapeDtypeStruct((8,) + buf.shape, F32)], stage


def _allreduce(buf, name):
    shapes, stage = _allreduce_plan(buf)
    (slots,) = _exchange([buf], shapes, [stage], name + "_x")
    return _sum_slots(slots, name + "_sum")


def _gather_plan(shards, src):
    half = lambda a, c: pl.ds(c * (a.shape[0] // 2), a.shape[0] // 2)
    first, second = [], []
    for n, a in enumerate(shards):
        for f in FLIPS_CHIP:
            first.append((("in", n), lambda x, y, c, a=a: (half(a, c), slice(None)), n,
                          lambda x, y, c, a=a: (2 * x + y, half(a, c), slice(None)), f))
            peer_slot = lambda x, y, c, a=a, f=f: (2 * (x ^ f[0]) + (y ^ f[1]), half(a, c), slice(None))
            second.append(((src, n), peer_slot, n, peer_slot, (0, 0, 1)))
    outs = [jax.ShapeDtypeStruct((4,) + a.shape, a.dtype) for a in shards]
    return first, second, outs


def _allgather_weights(shards, name):
    first, second, outs = _gather_plan(shards, "out")
    return _exchange(shards, outs, [first, second], name)


def _place_own(buf, shard, svec, name):
    _, Rs, Cs = buf.shape
    tr = _pick(Rs, 256, 16)

    def body(s_ref, buf_ref, sh_ref, o_ref):
        o_ref[0] = sh_ref[...]

    grid_spec = pltpu.PrefetchScalarGridSpec(
        num_scalar_prefetch=1, grid=(Rs // tr,),
        in_specs=[pl.BlockSpec(memory_space=pl.ANY), pl.BlockSpec((tr, Cs), lambda i, s: (i, 0))],
        out_specs=pl.BlockSpec((1, tr, Cs), lambda i, s: (s[0], i, 0)))
    return pl.pallas_call(body, name=name, grid_spec=grid_spec, out_shape=jax.ShapeDtypeStruct(buf.shape, buf.dtype),
                          input_output_aliases={1: 0}, compiler_params=_cparams("arbitrary"))(svec, buf, shard)


def _add_pair(G, bufA, cvec, name):
    _, Rs, Cs = G.shape
    Rh = Rs // 2
    tr = _pick(Rh, 128, 16)
    nb = Rh // tr

    def body(c_ref, g_ref, a_ref, o_ref):
        o_ref[...] = (g_ref[...] + a_ref[...]).astype(BF16)

    grid_spec = pltpu.PrefetchScalarGridSpec(
        num_scalar_prefetch=1, grid=(4, nb),
        in_specs=[pl.BlockSpec((1, tr, Cs), lambda s, i, c_ref: (s, c_ref[0] * nb + i, 0)),
                  pl.BlockSpec((1, tr, Cs), lambda s, i, c_ref: (s, i, 0))],
        out_specs=pl.BlockSpec((1, tr, Cs), lambda s, i, c_ref: (s, i, 0)))
    return pl.pallas_call(body, name=name, grid_spec=grid_spec, out_shape=jax.ShapeDtypeStruct((4, Rh, Cs), BF16),
                          compiler_params=_cparams("parallel", "parallel"))(cvec, G, bufA)


def _sum_chips(G, bufA, bufB, cvec, svec, name):
    _, Rs, Cs = G.shape
    Rh = Rs // 2
    tr = _pick(Rh, 128, 16)
    nb = Rh // tr

    def body(c_ref, s_ref, g_ref, a_ref, b_ref, o_ref):
        o_ref[...] = (g_ref[0] + a_ref[0]) + b_ref[0].astype(F32) + b_ref[1].astype(F32) + b_ref[2].astype(F32)

    grid_spec = pltpu.PrefetchScalarGridSpec(
        num_scalar_prefetch=2, grid=(nb,),
        in_specs=[pl.BlockSpec((1, tr, Cs), lambda i, c, s: (s[0], c[0] * nb + i, 0)),
                  pl.BlockSpec((1, tr, Cs), lambda i, c, s: (s[0], i, 0)),
                  pl.BlockSpec((3, tr, Cs), lambda i, c, s: (0, i, 0))],
        out_specs=pl.BlockSpec((tr, Cs), lambda i, c, s: (i, 0)))
    return pl.pallas_call(body, name=name, grid_spec=grid_spec, out_shape=jax.ShapeDtypeStruct((Rh, Cs), F32),
                          compiler_params=_cparams("parallel"))(cvec, svec, G, bufA, bufB)


def _pair_plan(grads):
    Rh = [g.shape[1] // 2 for g in grads]
    whole3 = lambda x, y, c: (slice(None), slice(None), slice(None))
    stage = [(("in", n), lambda x, y, c, n=n: (slice(None), pl.ds((1 - c) * Rh[n], Rh[n]), slice(None)), n,
              whole3, (0, 0, 1)) for n in range(len(grads))]
    return [jax.ShapeDtypeStruct((4, Rh[n], g.shape[2]), F32) for n, g in enumerate(grads)], stage


def _chips_plan(P):
    stage = [(("in", n), lambda x, y, c, f=f: (2 * (x ^ f[0]) + (y ^ f[1]),), n, lambda x, y, c, k=k: (k,), f)
             for n in range(len(P)) for k, f in enumerate(FLIPS_CHIP)]
    return [jax.ShapeDtypeStruct((3,) + p.shape[1:], BF16) for p in P], stage


def _halves_plan(mine):
    whole2 = lambda x, y, c: (slice(None), slice(None))
    stage = [(("in", n), whole2, n, whole2, (0, 0, 1)) for n in range(len(mine))]
    return [jax.ShapeDtypeStruct(r.shape, F32) for r in mine], stage


def _adamw_halves(w, mine, other, m, v, name, ride=None):
    _, Rs, Cs = w.shape
    Rh = Rs // 2
    tr = _pick(Rh, 128, 8)
    nb = Rh // tr

    def body(w_ref, a_ref, b_ref, m_ref, v_ref, g_ref, d_ref, mo_ref, vo_ref):
        gv = jnp.where(pl.program_id(0) // nb == lax.axis_index("c"), a_ref[...], b_ref[...])
        g_ref[0] = gv
        d_ref[0], mo_ref[0], vo_ref[0] = _adam_math(w_ref[0], gv, m_ref[0], v_ref[0])

    row = pl.BlockSpec((1, tr, Cs), lambda i: (0, i, 0))
    hrow = pl.BlockSpec((tr, Cs), lambda i: (i % nb, 0))
    return _pcall(body, name=name, grid=(2 * nb,), in_specs=[row, hrow, hrow, row, row], out_specs=[row] * 4,
                  out_shape=[jax.ShapeDtypeStruct((1, Rs, Cs), F32)] * 4, sem=("parallel",),
                  args=(w, mine, other, m, v), ride=ride)


def _pack(arrays):
    flat = [a.reshape(-1).astype(F32) for a in arrays]
    meta, off = [], 0
    for a, f in zip(arrays, flat):
        meta.append((off, a.shape))
        off += f.shape[0]
    total = -(-off // (8 * LANES)) * (8 * LANES)
    flat.append(jnp.zeros((total - off,), F32))
    return jnp.concatenate(flat).reshape(total // LANES, LANES), meta


def _unpack(buf, meta):
    flat = buf.reshape(-1)
    out = []
    for off, shape in meta:
        size = 1
        for s in shape:
            size *= s
        out.append(flat[off:off + size].reshape(shape))
    return out


WEIGHT_NAMES = ["c_ctx", "w_mod", "b_mod", "g_mix", "w_in", "q_norm", "k_norm", "attn_sink", "w_gate_f", "b_gate_f",
                "w_gate_b", "b_gate_b", "gla_norm", "w_attn_o", "w_gla_o", "w_out", "g_ffn", "w_up", "conv_w",
                "conv_b", "w_down"]
BIG_NAMES = ["w_in", "w_attn_o", "w_gla_o", "w_out", "w_up", "w_down"]
SHARDED_SMALL = ["w_gate_f", "w_gate_b", "conv_w"]


def _layouts(D):
    aw, kvw, gk, gv = N_Q_HEADS * HEAD_DIM, N_KV_HEADS * HEAD_DIM, D // 2, D
    widths = {"qa": aw, "ka": kvw, "va": kvw, "qb": gk, "kb": gk, "vb": gv, "rb": gv, "lr": 2 * GLA_LOWRANK,
              "ga": D, "gb": D}
    orig, off = {}, 0
    for s in ["qa", "ka", "va", "qb", "kb", "vb", "rb", "lr", "ga", "gb"]:
        orig[s] = off
        off += widths[s]
    order = ["qa", "vb", "rb", "ga", "gb", "ka", "va", "qb", "kb", "lr"]
    lay, off = {}, 0
    for s in order:
        lay[s] = off
        off += LANES if s == "lr" else widths[s]
    align = {"qa": aw, "vb": D, "rb": D, "ga": D, "gb": D, "ka": kvw, "va": kvw, "qb": gk, "kb": gk,
             "lr": LANES}
    for s in order:
        assert lay[s] % align[s] == 0, (s, lay[s], align[s])
    return widths, orig, order, lay, off, -(-off // (2 * MXU_TILE)) * (2 * MXU_TILE)


def _rope_tables(T, L):
    t = jnp.arange(T)
    nf = HEAD_DIM // 4
    inv = ROPE_THETA ** (-jnp.arange(nf, dtype=F32) / nf)
    ang = jnp.concatenate([(t // GRID_W)[:, None] * inv, (t % GRID_W)[:, None] * inv], axis=-1)
    cos, sin = jnp.cos(ang), jnp.sin(ang)
    cos2 = jnp.concatenate([jnp.ones((L, HEAD_DIM), F32), jnp.concatenate([cos, cos], axis=-1)], axis=0)
    sin2 = jnp.concatenate([jnp.zeros((L, HEAD_DIM), F32), jnp.concatenate([-sin, sin], axis=-1)], axis=0)
    return cos2, sin2


def _step(x, c, ctx, loss_target, W, M, V):
    xi, yi, ci = lax.axis_index("x"), lax.axis_index("y"), lax.axis_index("c")
    chip = 2 * xi + yi
    dev = 2 * chip + ci
    south = (ci == 0).astype(F32)
    cvec = ci.reshape(1).astype(jnp.int32)
    svec = chip.reshape(1).astype(jnp.int32)
    T, D = x.shape[1], x.shape[2]
    L = ctx.shape[1]
    R = L + T
    F = 4 * W["w_down"].shape[1]
    GK, GV = D // 2, D
    DK, DV = GK // GLA_HEADS, GV // GLA_HEADS
    N6 = 6 * D
    N4 = N6 // 4
    widths, orig, order, lay, z_used, Z = _layouts(D)

    def place_cols(shard, full_cols):
        cols = shard.shape[-1]
        full = jnp.zeros(shard.shape[:-1] + (full_cols,), F32)
        return lax.dynamic_update_slice(full, shard * south, (0,) * (shard.ndim - 1) + (chip * cols,))

    c_rows = lax.dynamic_update_slice(jnp.zeros((8, D), F32), c, (dev, 0))
    bufa, meta = _pack([c_rows, place_cols(W["w_gate_f"][0], GK), place_cols(W["w_gate_b"][0], GK),
                        place_cols(W["conv_w"][0], 2 * F)])
    c_all, wgf, wgb, cw = _unpack(_allreduce(bufa, "gather_small"), meta)
    ca = jnp.concatenate([c_all, W["c_ctx"][None, :], jnp.zeros((7, D), F32)], axis=0)
    b_shard = lax.dynamic_slice(W["b_mod"], (0, chip * N4), (1, N4))
    mod_part, sil = _mod_fwd(ca, W["w_mod"][0], b_shard, "mod_fwd")
    slots = lax.dynamic_update_slice(jnp.zeros((4, 16, N4), F32), (mod_part * south)[None], (chip, 0, 0))
    mod_all = _allreduce(slots.reshape(64, N4), "gather_mod").reshape(4, 16, N4).transpose(1, 0, 2).reshape(16, N6)
    mx = lax.dynamic_slice(mod_all, (dev, 0), (1, N6)).reshape(6, 1, D)
    mc = mod_all[8].reshape(6, 1, D)

    sq = lambda a: a.reshape(a.shape[1:])
    shards = [sq(W[n]).astype(BF16) for n in BIG_NAMES]
    own = lambda g, s, n: _place_own(g, s, svec, "place_" + n)
    cols = lambda g: g.transpose(1, 0, 2).reshape(g.shape[1], 4 * g.shape[2])
    rows = lambda g: g.reshape(4 * g.shape[1], g.shape[2])
    w_in_f = cols(own(_allgather_weights(shards[:1], "gather_w_in")[0], shards[0], "w_in"))
    seg = lambda s: w_in_f[:, orig[s]:orig[s] + widths[s]]
    w_cat = jnp.concatenate([jnp.pad(seg(s), ((0, 0), (0, LANES - widths[s]))) if s == "lr" else seg(s)
                             for s in order] + [jnp.zeros((D, Z - z_used), BF16)], axis=1)
    gather1, gather2, gather_outs = _gather_plan(shards[1:], "in")
    wg = jnp.zeros((2, LANES, GK), F32).at[0, :GLA_LOWRANK].set(wgf).at[1, GLA_LOWRANK:2 * GLA_LOWRANK].set(wgb)
    bg = jnp.stack([W["b_gate_f"], W["b_gate_b"]])
    cb = W["conv_b"]
    sink_rows = jnp.broadcast_to(W["attn_sink"][0][:, None], (N_Q_HEADS, HEAD_DIM))
    cos2, sin2 = _rope_tables(T, L)
    blk = lambda s, w: lay[s] // w

    xall = jnp.concatenate([ctx[0], x[0]], axis=0)
    sc1 = jnp.stack([mc[1], mx[1]])
    sh1 = jnp.stack([mc[0], mx[0]])
    h = _modnorm_fwd(xall, W["g_mix"], sc1, sh1, L, "modnorm1")
    z, landed = _matmul(h, w_cat, "nn", F32, "proj_in", tn=1536, ride=(shards[1:], gather_outs, gather1, {}))
    qn = _qknorm_fwd(z, blk("qa", widths["qa"]), T, L, W["q_norm"], cos2, sin2, N_Q_HEADS, "qnorm")
    kn = _qknorm_fwd(z, blk("ka", widths["ka"]), R, 0, W["k_norm"], cos2, sin2, N_KV_HEADS, "knorm")
    vb = _cast_seg(z, blk("va", widths["va"]), widths["va"], "vcast")
    o_attn, landed = _attn_fwd(qn, kn, vb, sink_rows, L, "attn_fwd",
                               ride=(landed, gather_outs, gather2, {n: n for n in range(len(landed))}))
    g_ao, g_go, g_out, g_up, g_dn = [own(g, s, n) for g, s, n in zip(landed, shards[1:], BIG_NAMES[1:])]
    w_ao, w_go, w_out, w_up, w_dn = rows(g_ao), rows(g_go), rows(g_out), cols(g_up), rows(g_dn)
    gla_blks = (blk("qb", GK), blk("kb", GK), blk("vb", GV), blk("lr", LANES))
    o_g, sprev = _gla_fwd(z, *gla_blks, wg, bg, DV, L, "gla_fwd")
    p = _glanorm_fwd(o_g, z, blk("rb", D), W["gla_norm"], L, "glanorm")
    ya = _matmul(o_attn, w_ao, "nn", F32, "proj_attn_o")
    yg = _matmul(p, w_go, "nn", F32, "proj_gla_o")
    m = _gate_fwd(z, blk("ga", D), blk("gb", D), ya, yg, L, "gate")
    mix = _matmul(m, w_out, "nn", F32, "proj_out")
    x1, h2 = _resnorm_fwd(x[0], mix, mx[2], W["g_ffn"], mx[4], mx[3], "resnorm2")
    u = _matmul(h2, w_up, "nn", F32, "ffn_up")
    f = _conv_fwd(u, cw, cb, "conv_swiglu")
    d = _matmul(f, w_dn, "nn", F32, "ffn_down", tk=2816)
    dy, dd, lacc = _loss_head(d, x1, mx[5], loss_target[0], "loss_head")
    loss = lax.psum((0.5 / D) * jnp.sum(lacc[0]), ("x", "y", "c"))

    gw_dn = _matmul(f, dd, "tn", F32, "ffn_down_dw")
    df = _matmul(dd, w_dn, "nt", F32, "ffn_down_dx")
    du_a, du_g, acca, accg = _conv_bwd(u, df, cw, cb, "conv_swiglu_bwd")
    du = jnp.concatenate([du_a, du_g], axis=1)
    gw_up = _matmul(h2, du, "tn", F32, "ffn_up_dw")
    dh2 = _matmul(du, w_up, "nt", F32, "ffn_up_dx", tk=2816)
    dx1, dmix, s2 = _resnorm_bwd(x1, dh2, W["g_ffn"], mx[4], dy, mix, mx[2], "resnorm2_bwd")
    gw_out = _matmul(m, dmix, "tn", F32, "proj_out_dw")
    dm = _matmul(dmix, w_out, "nt", F32, "proj_out_dx")
    dya, dyg, dga, dgb = _gate_bwd(z, blk("ga", D), blk("gb", D), ya, yg, dm, L, "gate_bwd")
    gw_ao = _matmul(o_attn, dya, "tn", F32, "proj_attn_o_dw")
    do_attn = _matmul(dya, w_ao, "nt", BF16, "proj_attn_o_dx")
    gw_go = _matmul(p, dyg, "tn", F32, "proj_gla_o_dw")
    dp = _matmul(dyg, w_go, "nt", F32, "proj_gla_o_dx")
    do_gla, drb, s_gn = _glanorm_bwd(o_g, z, blk("rb", D), W["gla_norm"], dp, L, "glanorm_bwd")
    do_pad = jnp.concatenate([jnp.zeros((L, GV), BF16), do_gla], axis=0)
    by_cols = lambda g: g.reshape(g.shape[0], 4, g.shape[1] // 4).transpose(1, 0, 2)
    by_rows = lambda g: g.reshape(4, g.shape[0] // 4, g.shape[1])
    early = [by_rows(gw_ao), by_rows(gw_go), by_rows(gw_out), by_cols(gw_up), by_rows(gw_dn)]
    (dqg, dkg, dvg, dpre, dbg), pair_e = _gla_bwd(z, *gla_blks, wg, bg, sprev, do_pad, L, "gla_bwd",
                                                  ride=(early, *_pair_plan(early), {}))
    sums_e = [_add_pair(g, a, cvec, "reduce_early_add%d" % n) for n, (g, a) in enumerate(zip(early, pair_e))]
    wg_cat = jnp.concatenate([wg[0], wg[1]], axis=1)
    dlr = _matmul(dpre, wg_cat, "nt", BF16, "gla_gate_dx")
    dwg = _matmul(z[:, lay["lr"]:lay["lr"] + LANES], dpre, "tn", F32, "gla_gate_dw")
    (dqn, dkw, dvw, dkc, dvc, dsn), chips_e = _attn_bwd(qn, kn, vb, sink_rows, do_attn, L, "attn_bwd",
                                                        ride=(sums_e, *_chips_plan(sums_e), {}))
    mine_e = [_sum_chips(g, a, b, cvec, svec, "reduce_early_sum%d" % n)
              for n, (g, a, b) in enumerate(zip(early, pair_e, chips_e))]
    dqa, s_qn = _qknorm_bwd(z, blk("qa", widths["qa"]), T, L, W["q_norm"], cos2, sin2, dqn, N_Q_HEADS, "qnorm_bwd")
    dk_all = jnp.concatenate([dkc, dkw[WINDOW:WINDOW + T]], axis=0)
    dv_all = jnp.concatenate([dvc, dvw[WINDOW:WINDOW + T]], axis=0)
    dka, s_kn = _qknorm_bwd(z, blk("ka", widths["ka"]), R, 0, W["k_norm"], cos2, sin2, dk_all, N_KV_HEADS, "knorm_bwd")
    dz = _assemble_dz(lay, z_used, Z, L, dqa, drb, dga, dgb, dka, dv_all, dvg, dqg, dkg, dlr, "assemble_dz")
    gw_cat, other_e = _matmul(h, dz, "tn", F32, "proj_in_dw", tn=768, tk=2816,
                              ride=(mine_e, *_halves_plan(mine_e), {}))
    gw_in = jnp.concatenate([gw_cat[:, lay[s]:lay[s] + widths[s]] for s in ["qa", "ka", "va", "qb", "kb", "vb", "rb",
                                                                           "lr", "ga", "gb"]], axis=1)
    grads, delta, new_m, new_v = {}, {}, {}, {}
    early_halves = dict(zip(BIG_NAMES[1:], zip(mine_e, other_e)))

    def adam_big(n, halves, ride=None):
        res = _adamw_halves(W[n], *halves, M[n], V[n], "adamw_" + n, ride=ride)
        (grads[n], delta[n], new_m[n], new_v[n]), landed = res if ride is not None else (res, None)
        return landed

    late = [by_cols(gw_in)]
    pair_l = adam_big("w_up", early_halves["w_up"], ride=(late, *_pair_plan(late), {}))
    sums_l = [_add_pair(late[0], pair_l[0], cvec, "reduce_late_add")]
    dh, chips_l = _matmul(dz, w_cat, "nt", F32, "proj_in_dx", tk=4608, ride=(sums_l, *_chips_plan(sums_l), {}))
    mine_l = [_sum_chips(late[0], pair_l[0], chips_l[0], cvec, svec, "reduce_late_sum")]
    other_l = adam_big("w_down", early_halves["w_down"], ride=(mine_l, *_halves_plan(mine_l), {}))
    for n in ["w_attn_o", "w_gla_o", "w_out"]:
        adam_big(n, early_halves[n])
    grad_x, s1 = _modnorm_bwd(x[0], dh, W["g_mix"], mx[1], dx1, "modnorm1_bwd", dh_roff=L)
    _, s1c = _modnorm_bwd(ctx[0], dh, W["g_mix"], mc[1], None, "modnorm1_ctx_bwd")

    dmod_x = jnp.concatenate([s1[0], s1[1], s2[3], s2[0], s2[1], lacc[1]])
    dmod_c = jnp.concatenate([s1c[0], s1c[1], jnp.zeros((4 * D,), F32)])
    dmod_rows = lax.dynamic_update_slice(jnp.zeros((9, N6), F32).at[8].set(dmod_c), dmod_x[None], (dev, 0))
    small = [dmod_rows, dmod_x + dmod_c, s1[2] + s1c[2], s_qn[0], s_kn[0], dsn[:, 0, :Q_PER_KV].reshape(N_Q_HEADS),
             dwg[:GLA_LOWRANK, :GK], dbg[0].reshape(GK), dwg[GLA_LOWRANK:2 * GLA_LOWRANK, GK:], dbg[1].reshape(GK),
             s_gn[0], s2[2], jnp.concatenate([acca[0:3], accg[0:3]], axis=1), jnp.concatenate([acca[3], accg[3]])]
    bufc, meta = _pack(small)
    (slots,) = adam_big("w_in", (mine_l[0], other_l[0]), ride=([bufc], *_allreduce_plan(bufc), {}))
    (dmod_sum, g_b_mod, g_g_mix, g_q_norm, g_k_norm, g_sink, g_wgf, g_bgf, g_wgb, g_bgb, g_gla_norm, g_g_ffn,
     g_conv_w, g_conv_b) = _unpack(_sum_slots(slots, "reduce_small_sum"), meta)
    dmod16 = lax.dynamic_slice(jnp.concatenate([dmod_sum, jnp.zeros((7, N6), F32)], axis=0), (0, chip * N4), (16, N4))
    g_w_mod = _matmul(sil, dmod16, "tn", F32, "mod_dw")
    dsil = _matmul(dmod16, W["w_mod"][0], "nt", F32, "mod_dx") * south
    (dl, mn, vn), (slots,) = _adamw(W["w_mod"][0], g_w_mod, M["w_mod"][0], V["w_mod"][0], "adamw_w_mod",
                                    ride=([dsil], *_allreduce_plan(dsil), {}))
    delta["w_mod"], new_m["w_mod"], new_v["w_mod"] = dl[None], mn[None], vn[None]
    g_c_ctx = _silu_bwd(_sum_slots(slots, "reduce_cctx_sum"), ca, "silu_bwd")[8]

    cut = lambda g: lax.dynamic_slice(g, (0, chip * (g.shape[1] // 4)), (g.shape[0], g.shape[1] // 4))
    grads.update({"c_ctx": g_c_ctx, "w_mod": g_w_mod[None], "b_mod": g_b_mod[None], "g_mix": g_g_mix[None],
                  "q_norm": g_q_norm[None], "k_norm": g_k_norm[None], "attn_sink": g_sink[None],
                  "w_gate_f": cut(g_wgf)[None], "b_gate_f": g_bgf[None], "w_gate_b": cut(g_wgb)[None],
                  "b_gate_b": g_bgb[None], "gla_norm": g_gla_norm[None], "g_ffn": g_g_ffn[None],
                  "conv_w": cut(g_conv_w)[None], "conv_b": g_conv_b[None]})

    small_names = [n for n in WEIGHT_NAMES if n not in delta]
    packs = [_pack([src[n] for n in small_names]) for src in (W, grads, M, V)]
    meta = packs[0][1]
    outs = _adamw(packs[0][0], packs[1][0], packs[2][0], packs[3][0], "adamw_small")
    for res, o in zip((delta, new_m, new_v), outs):
        for n, a in zip(small_names, _unpack(o, meta)):
            res[n] = a
    return (loss, grad_x[None], *[grads[n] for n in WEIGHT_NAMES], *[delta[n] for n in WEIGHT_NAMES],
            *[new_m[n] for n in WEIGHT_NAMES], *[new_v[n] for n in WEIGHT_NAMES])


def kernel(x, c, ctx, c_ctx, w_mod, b_mod, g_mix, w_in, q_norm, k_norm, attn_sink, w_gate_f, b_gate_f, w_gate_b, b_gate_b, gla_norm, w_attn_o, w_gla_o, w_out, g_ffn, w_up, conv_w, conv_b, w_down, loss_target, m_c_ctx, m_w_mod, m_b_mod, m_g_mix, m_w_in, m_q_norm, m_k_norm, m_attn_sink, m_w_gate_f, m_b_gate_f, m_w_gate_b, m_b_gate_b, m_gla_norm, m_w_attn_o, m_w_gla_o, m_w_out, m_g_ffn, m_w_up, m_conv_w, m_conv_b, m_w_down, v_c_ctx, v_w_mod, v_b_mod, v_g_mix, v_w_in, v_q_norm, v_k_norm, v_attn_sink, v_w_gate_f, v_b_gate_f, v_w_gate_b, v_b_gate_b, v_gla_norm, v_w_attn_o, v_w_gla_o, v_w_out, v_g_ffn, v_w_up, v_conv_w, v_conv_b, v_w_down):
    W = dict(zip(WEIGHT_NAMES, (c_ctx, w_mod, b_mod, g_mix, w_in, q_norm, k_norm, attn_sink, w_gate_f, b_gate_f,
                                w_gate_b, b_gate_b, gla_norm, w_attn_o, w_gla_o, w_out, g_ffn, w_up, conv_w, conv_b,
                                w_down)))
    M = dict(zip(WEIGHT_NAMES, (m_c_ctx, m_w_mod, m_b_mod, m_g_mix, m_w_in, m_q_norm, m_k_norm, m_attn_sink,
                                m_w_gate_f, m_b_gate_f, m_w_gate_b, m_b_gate_b, m_gla_norm, m_w_attn_o, m_w_gla_o,
                                m_w_out, m_g_ffn, m_w_up, m_conv_w, m_conv_b, m_w_down)))
    V = dict(zip(WEIGHT_NAMES, (v_c_ctx, v_w_mod, v_b_mod, v_g_mix, v_w_in, v_q_norm, v_k_norm, v_attn_sink,
                                v_w_gate_f, v_b_gate_f, v_w_gate_b, v_b_gate_b, v_gla_norm, v_w_attn_o, v_w_gla_o,
                                v_w_out, v_g_ffn, v_w_up, v_conv_w, v_conv_b, v_w_down)))
    return _step(x, c, ctx, loss_target, W, M, V)
```

```python
import functools
import math

import jax
import jax.numpy as jnp
from jax import lax
from jax.experimental import pallas as pl
from jax.experimental.pallas import tpu as pltpu

F32 = jnp.float32
BF16 = jnp.bfloat16
MESH = pl.DeviceIdType.MESH

EPS = 1e-6
HEAD_DIM = 128
N_Q_HEADS = 16
N_KV_HEADS = 4
Q_PER_KV = N_Q_HEADS // N_KV_HEADS
WINDOW = 128
GLA_HEADS = 4
GLA_LOWRANK = 16
GLA_GATE_NORM = 16.0
GLA_CHUNK = 64
GRID_W = 64
ROPE_THETA = 10000.0
GLA_LEVELS = (32, 16, 8, 4, 2, 1)
LANES = 128
MXU_TILE = 256

ADAM_LR = 0.001
ADAM_B1 = 0.9
ADAM_B2 = 0.999
ADAM_EPS = 1e-08
ADAM_WD = 0.01
ADAM_STEP = 10

VMEM_LIMIT = 52 * 1024 * 1024


def _cparams(*sem):
    return pltpu.CompilerParams(dimension_semantics=sem, vmem_limit_bytes=VMEM_LIMIT)


def _pick(n, target, mult=LANES):
    best = None
    d = mult
    while d <= min(n, target):
        if n % d == 0:
            best = d
        d += mult
    return n if best is None else best


def _sigmoid(x):
    return 1.0 / (1.0 + jnp.exp(-x))


def _silu(x):
    return x * _sigmoid(x)


def _dsilu(x):
    s = _sigmoid(x)
    return s * (1.0 + x * (1.0 - s))


def _dot(a, b, dims):
    return lax.dot_general(a, b, (dims, ((), ())), preferred_element_type=F32)


NN = ((1,), (0,))
NT = ((1,), (1,))
TN = ((0,), (0,))


def _matmul(a, b, mode, out_dtype, name, tm=1024, tn=1024, tk=2048, ride=None, halves=None):
    if halves == "a":
        assert mode == "nt"
        (_, M, Kh), (N, K2) = a.shape, b.shape
        K = 2 * Kh
    elif halves == "b":
        assert mode == "tn"
        (K, M), (_, K2, Nh) = a.shape, b.shape
        N = 2 * Nh
    elif mode == "nn":
        (M, K), (K2, N) = a.shape, b.shape
    elif mode == "nt":
        (M, K), (N, K2) = a.shape, b.shape
    else:
        (K, M), (K2, N) = a.shape, b.shape
    assert K == K2, (name, a.shape, b.shape)
    pick = lambda n, t: _pick(n, t, MXU_TILE) if n % MXU_TILE == 0 else _pick(n, t)
    tm, tn, tk = pick(M, tm), pick(N // 2 if halves == "b" else N, tn), pick(K // 2 if halves == "a" else K, tk)
    nk = K // tk
    dims = {"nn": NN, "nt": NT, "tn": TN}[mode]

    def body(a_ref, b_ref, o_ref, acc_ref):
        k = pl.program_id(2)

        @pl.when(k == 0)
        def _():
            acc_ref[...] = jnp.zeros_like(acc_ref)

        av = a_ref[0] if halves == "a" else a_ref[...]
        bv = b_ref[0] if halves == "b" else b_ref[...]
        acc_ref[...] += _dot(av.astype(BF16), bv.astype(BF16), dims)

        @pl.when(k == nk - 1)
        def _():
            o_ref[...] = acc_ref[...].astype(out_dtype)

    if halves == "a":
        per = (K // 2) // tk
        a_spec = pl.BlockSpec((1, tm, tk), lambda i, j, k: (k // per, i, k % per))
    elif mode == "tn":
        a_spec = pl.BlockSpec((tk, tm), lambda i, j, k: (k, i))
    else:
        a_spec = pl.BlockSpec((tm, tk), lambda i, j, k: (i, k))
    if halves == "b":
        per = (N // 2) // tn
        b_spec = pl.BlockSpec((1, tk, tn), lambda i, j, k: (j // per, k, j % per))
    elif mode == "nt":
        b_spec = pl.BlockSpec((tn, tk), lambda i, j, k: (j, k))
    else:
        b_spec = pl.BlockSpec((tk, tn), lambda i, j, k: (k, j))
    return _pcall(
        body, name=name, grid=(M // tm, N // tn, nk),
        in_specs=[a_spec, b_spec],
        out_specs=pl.BlockSpec((tm, tn), lambda i, j, k: (i, j)),
        out_shape=jax.ShapeDtypeStruct((M, N), out_dtype),
        scratch_shapes=[pltpu.VMEM((tm, tn), F32)],
        sem=("parallel", "parallel", "arbitrary"), args=(a, b), ride=ride)


def _modnorm_fwd(xall, g, sc, sh, n_ctx, name):
    R, D = xall.shape
    tm = _pick(n_ctx, 256, 8)
    cb = n_ctx // tm

    def body(x_ref, g_ref, sc_ref, sh_ref, h_ref):
        x = x_ref[...]
        r = lax.rsqrt(jnp.mean(x * x, axis=-1, keepdims=True) + EPS)
        n = x * r * g_ref[...]
        h_ref[...] = (n * (1.0 + sc_ref[0]) + sh_ref[0]).astype(BF16)

    sel = lambda i: (jnp.where(i < cb, 0, 1), 0, 0)
    return pl.pallas_call(
        body, name=name, grid=(R // tm,),
        in_specs=[pl.BlockSpec((tm, D), lambda i: (i, 0)), pl.BlockSpec((1, D), lambda i: (0, 0)),
                  pl.BlockSpec((1, 1, D), sel), pl.BlockSpec((1, 1, D), sel)],
        out_specs=pl.BlockSpec((tm, D), lambda i: (i, 0)),
        out_shape=jax.ShapeDtypeStruct((R, D), BF16),
        compiler_params=_cparams("parallel"),
    )(xall, g, sc, sh)


def _modnorm_bwd(x, dh, g, sc, resid, name, dh_roff=0):
    N, D = x.shape
    tm = _pick(math.gcd(N, dh_roff), 256, 8)
    ro = dh_roff // tm
    want_dx = resid is not None

    def body(*refs):
        if want_dx:
            x_ref, dh_ref, g_ref, sc_ref, res_ref, dx_ref, acc_ref = refs
        else:
            x_ref, dh_ref, g_ref, sc_ref, acc_ref = refs
        i = pl.program_id(0)

        @pl.when(i == 0)
        def _():
            acc_ref[...] = jnp.zeros_like(acc_ref)

        xv, dhv, gv = x_ref[...], dh_ref[...], g_ref[...]
        r = lax.rsqrt(jnp.mean(xv * xv, axis=-1, keepdims=True) + EPS)
        xh = xv * r
        dn = dhv * (1.0 + sc_ref[...])
        acc_ref[0:1, :] += jnp.sum(dhv, axis=0, keepdims=True)
        acc_ref[1:2, :] += jnp.sum(dhv * xh * gv, axis=0, keepdims=True)
        acc_ref[2:3, :] += jnp.sum(dn * xh, axis=0, keepdims=True)
        if want_dx:
            dxh = dn * gv
            dx_ref[...] = res_ref[...] + r * (dxh - xh * jnp.mean(dxh * xh, axis=-1, keepdims=True))

    row = pl.BlockSpec((tm, D), lambda i: (i, 0))
    drow = pl.BlockSpec((tm, D), lambda i: (i + ro, 0))
    vec = pl.BlockSpec((1, D), lambda i: (0, 0))
    acc = pl.BlockSpec((8, D), lambda i: (0, 0))
    acc_shape = jax.ShapeDtypeStruct((8, D), F32)
    if want_dx:
        return pl.pallas_call(
            body, name=name, grid=(N // tm,), in_specs=[row, drow, vec, vec, row],
            out_specs=[row, acc], out_shape=[jax.ShapeDtypeStruct((N, D), F32), acc_shape],
            compiler_params=_cparams("arbitrary"))(x, dh, g, sc, resid)
    sums = pl.pallas_call(
        body, name=name, grid=(N // tm,), in_specs=[row, drow, vec, vec],
        out_specs=acc, out_shape=acc_shape, compiler_params=_cparams("arbitrary"))(x, dh, g, sc)
    return None, sums


def _resnorm_bwd(x1, dh, g, sc, dy, mix, gt, name):
    N, D = x1.shape
    tm = _pick(N, 256, 8)

    def body(x_ref, dh_ref, g_ref, sc_ref, dy_ref, mix_ref, gt_ref, dx_ref, dm_ref, acc_ref):
        i = pl.program_id(0)

        @pl.when(i == 0)
        def _():
            acc_ref[...] = jnp.zeros_like(acc_ref)

        xv, dhv, gv = x_ref[...], dh_ref[...], g_ref[...]
        r = lax.rsqrt(jnp.mean(xv * xv, axis=-1, keepdims=True) + EPS)
        xh = xv * r
        dn = dhv * (1.0 + sc_ref[...])
        dxh = dn * gv
        dx = dy_ref[...] + r * (dxh - xh * jnp.mean(dxh * xh, axis=-1, keepdims=True))
        dx_ref[...] = dx
        dm_ref[...] = (dx * gt_ref[...]).astype(BF16)
        acc_ref[0:1, :] += jnp.sum(dhv, axis=0, keepdims=True)
        acc_ref[1:2, :] += jnp.sum(dhv * xh * gv, axis=0, keepdims=True)
        acc_ref[2:3, :] += jnp.sum(dn * xh, axis=0, keepdims=True)
        acc_ref[3:4, :] += jnp.sum(dx * mix_ref[...], axis=0, keepdims=True)

    row = pl.BlockSpec((tm, D), lambda i: (i, 0))
    vec = pl.BlockSpec((1, D), lambda i: (0, 0))
    return pl.pallas_call(
        body, name=name, grid=(N // tm,), in_specs=[row, row, vec, vec, row, row, vec],
        out_specs=[row, row, pl.BlockSpec((8, D), lambda i: (0, 0))],
        out_shape=[jax.ShapeDtypeStruct((N, D), F32), jax.ShapeDtypeStruct((N, D), BF16),
                   jax.ShapeDtypeStruct((8, D), F32)],
        compiler_params=_cparams("arbitrary"))(x1, dh, g, sc, dy, mix, gt)


def _qknorm_fwd(z, cblk, nrows, roff, w, cos2, sin2, nh, name):
    W = nh * HEAD_DIM
    tm = _pick(math.gcd(nrows, roff), 256, 8)
    ro = roff // tm
    assert roff % tm == 0

    def body(z_ref, w_ref, c_ref, s_ref, o_ref):
        c, s, wv = c_ref[...], s_ref[...], w_ref[...]
        for h in range(nh):
            x = z_ref[:, h * HEAD_DIM:(h + 1) * HEAD_DIM]
            r = lax.rsqrt(jnp.mean(x * x, axis=-1, keepdims=True) + EPS)
            y = x * r * wv
            o_ref[:, h * HEAD_DIM:(h + 1) * HEAD_DIM] = (y * c + pltpu.roll(y, HEAD_DIM // 2, 1) * s).astype(BF16)

    return pl.pallas_call(
        body, name=name, grid=(nrows // tm,),
        in_specs=[pl.BlockSpec((tm, W), lambda i: (i + ro, cblk)), pl.BlockSpec((1, HEAD_DIM), lambda i: (0, 0)),
                  pl.BlockSpec((tm, HEAD_DIM), lambda i: (i + ro, 0)), pl.BlockSpec((tm, HEAD_DIM), lambda i: (i + ro, 0))],
        out_specs=pl.BlockSpec((tm, W), lambda i: (i, 0)),
        out_shape=jax.ShapeDtypeStruct((nrows, W), BF16),
        compiler_params=_cparams("parallel"),
    )(z, w, cos2, sin2)


def _qknorm_bwd(z, cblk, nrows, roff, w, cos2, sin2, dy, nh, name):
    W = nh * HEAD_DIM
    tm = _pick(math.gcd(nrows, roff), 256, 8)
    ro = roff // tm

    def body(z_ref, w_ref, c_ref, s_ref, dy_ref, dz_ref, acc_ref):
        i = pl.program_id(0)

        @pl.when(i == 0)
        def _():
            acc_ref[...] = jnp.zeros_like(acc_ref)

        c, s, wv = c_ref[...], s_ref[...], w_ref[...]
        dw = jnp.zeros((1, HEAD_DIM), F32)
        for h in range(nh):
            sl = slice(h * HEAD_DIM, (h + 1) * HEAD_DIM)
            x = z_ref[:, sl]
            d = dy_ref[:, sl]
            dyn = d * c + pltpu.roll(d * s, HEAD_DIM // 2, 1)
            r = lax.rsqrt(jnp.mean(x * x, axis=-1, keepdims=True) + EPS)
            xh = x * r
            dw = dw + jnp.sum(dyn * xh, axis=0, keepdims=True)
            dxh = dyn * wv
            dz_ref[:, sl] = (r * (dxh - xh * jnp.mean(dxh * xh, axis=-1, keepdims=True))).astype(BF16)
        acc_ref[0:1, :] += dw

    return pl.pallas_call(
        body, name=name, grid=(nrows // tm,),
        in_specs=[pl.BlockSpec((tm, W), lambda i: (i + ro, cblk)), pl.BlockSpec((1, HEAD_DIM), lambda i: (0, 0)),
                  pl.BlockSpec((tm, HEAD_DIM), lambda i: (i + ro, 0)), pl.BlockSpec((tm, HEAD_DIM), lambda i: (i + ro, 0)),
                  pl.BlockSpec((tm, W), lambda i: (i, 0))],
        out_specs=[pl.BlockSpec((tm, W), lambda i: (i, 0)), pl.BlockSpec((8, HEAD_DIM), lambda i: (0, 0))],
        out_shape=[jax.ShapeDtypeStruct((nrows, W), BF16), jax.ShapeDtypeStruct((8, HEAD_DIM), F32)],
        compiler_params=_cparams("arbitrary"),
    )(z, w, cos2, sin2, dy)


def _cast_seg(z, cblk, width, name):
    R = z.shape[0]
    tm = _pick(R, 512, 8)

    def body(z_ref, o_ref):
        o_ref[...] = z_ref[...].astype(BF16)

    return pl.pallas_call(
        body, name=name, grid=(R // tm,),
        in_specs=[pl.BlockSpec((tm, width), lambda i: (i, cblk))],
        out_specs=pl.BlockSpec((tm, width), lambda i: (i, 0)),
        out_shape=jax.ShapeDtypeStruct((R, width), BF16), compiler_params=_cparams("parallel"))(z)


NEG_BIG = -1e30


KV_PER_STEP = 2


def _attn_specs(T, n_ctx):
    nb = T // WINDOW
    lb = n_ctx // WINDOW
    kvw = KV_PER_STEP * HEAD_DIM
    blk = lambda f: pl.BlockSpec((WINDOW, kvw), f)
    win = [blk(lambda h, i: (lb + jnp.maximum(i - 1, 0), h)), blk(lambda h, i: (lb + i, h)),
           blk(lambda h, i: (lb + jnp.minimum(i + 1, nb - 1), h))]
    ctx = pl.BlockSpec((n_ctx, kvw), lambda h, i: (0, h))
    qspec = pl.BlockSpec((WINDOW, KV_PER_STEP * Q_PER_KV * HEAD_DIM), lambda h, i: (i, h))
    sink = pl.BlockSpec((N_Q_HEADS, HEAD_DIM), lambda h, i: (0, 0))
    return nb, qspec, win, ctx, sink


def _attn_probs(q, kw, kctx, snk, valid):
    scale = HEAD_DIM ** -0.5
    s_lat = jnp.where(valid, _dot(q, kw, NT) * scale, NEG_BIG)
    s_ctx = _dot(q, kctx, NT) * scale
    m = jnp.maximum(jnp.maximum(jnp.max(s_lat, axis=-1, keepdims=True), jnp.max(s_ctx, axis=-1, keepdims=True)), snk)
    p_lat = jnp.exp(s_lat - m)
    p_ctx = jnp.exp(s_ctx - m)
    p_snk = jnp.exp(snk - m)
    den = p_snk + jnp.sum(p_lat, axis=-1, keepdims=True) + jnp.sum(p_ctx, axis=-1, keepdims=True)
    return p_lat, p_ctx, p_snk, den


def _attn_valid(i, T, heads):
    rows = heads * WINDOW
    qpos = i * WINDOW + (lax.broadcasted_iota(jnp.int32, (rows, 3 * WINDOW), 0) & (WINDOW - 1))
    kpos = (i - 1) * WINDOW + lax.broadcasted_iota(jnp.int32, (rows, 3 * WINDOW), 1)
    return (jnp.abs(qpos - kpos) <= WINDOW) & (kpos >= 0) & (kpos < T)


def _stack_heads(ref, hh):
    c0 = hh * Q_PER_KV * HEAD_DIM
    return jnp.concatenate([ref[:, c0 + g * HEAD_DIM:c0 + (g + 1) * HEAD_DIM] for g in range(Q_PER_KV)], axis=0)


def _stack_sinks(sink_ref, kvh):
    return jnp.concatenate([jnp.broadcast_to(sink_ref[pl.ds(kvh * Q_PER_KV + g, 1), :][:, 0:1], (WINDOW, 1))
                            for g in range(Q_PER_KV)], axis=0)


def _attn_window(refs, hh):
    return jnp.concatenate([r[:, hh * HEAD_DIM:(hh + 1) * HEAD_DIM] for r in refs], axis=0)


def _attn_fwd(qn, kn, vb, sink_rows, n_ctx, name, ride=None):
    T = qn.shape[0]
    nb, qspec, win, ctx, sink = _attn_specs(T, n_ctx)

    def body(q_ref, kp, kc, kx, vp, vc, vx, kctx_ref, vctx_ref, sink_ref, o_ref):
        h, i = pl.program_id(0), pl.program_id(1)
        valid = _attn_valid(i, T, Q_PER_KV)
        for hh in range(KV_PER_STEP):
            sl = slice(hh * HEAD_DIM, (hh + 1) * HEAD_DIM)
            kw, vw = _attn_window((kp, kc, kx), hh), _attn_window((vp, vc, vx), hh)
            kctx, vctx = kctx_ref[:, sl], vctx_ref[:, sl]
            p_lat, p_ctx, _, den = _attn_probs(_stack_heads(q_ref, hh), kw, kctx,
                                               _stack_sinks(sink_ref, h * KV_PER_STEP + hh), valid)
            o = ((_dot(p_lat.astype(BF16), vw, NN) + _dot(p_ctx.astype(BF16), vctx, NN)) / den).astype(BF16)
            for g in range(Q_PER_KV):
                c0 = (hh * Q_PER_KV + g) * HEAD_DIM
                o_ref[:, c0:c0 + HEAD_DIM] = o[g * WINDOW:(g + 1) * WINDOW]

    return _pcall(
        body, name=name, grid=(N_KV_HEADS // KV_PER_STEP, nb),
        in_specs=[qspec] + win + win + [ctx, ctx, sink],
        out_specs=qspec, out_shape=jax.ShapeDtypeStruct(qn.shape, BF16),
        sem=("parallel", "parallel"), args=(qn, kn, kn, kn, vb, vb, vb, kn, vb, sink_rows), ride=ride)


def _attn_bwd(qn, kn, vb, sink_rows, do, n_ctx, name, ride=None):
    T = qn.shape[0]
    nb, qspec, win, ctx, sink = _attn_specs(T, n_ctx)
    scale = HEAD_DIM ** -0.5
    TP = T + 2 * WINDOW

    def body(q_ref, kp, kc, kx, vp, vc, vx, kctx_ref, vctx_ref, sink_ref, do_ref,
             dq_ref, dkw_ref, dvw_ref, dkc_ref, dvc_ref, dsn_ref):
        h, i = pl.program_id(0), pl.program_id(1)

        @pl.when(i == 0)
        def _():
            dkw_ref[...] = jnp.zeros_like(dkw_ref)
            dvw_ref[...] = jnp.zeros_like(dvw_ref)
            dkc_ref[...] = jnp.zeros_like(dkc_ref)
            dvc_ref[...] = jnp.zeros_like(dvc_ref)
            dsn_ref[...] = jnp.zeros_like(dsn_ref)

        lane = lax.broadcasted_iota(jnp.int32, (8, HEAD_DIM), 1)
        valid = _attn_valid(i, T, Q_PER_KV)
        rows = pl.ds(pl.multiple_of(i * WINDOW, WINDOW), 3 * WINDOW)
        for hh in range(KV_PER_STEP):
            sl = slice(hh * HEAD_DIM, (hh + 1) * HEAD_DIM)
            kw, vw = _attn_window((kp, kc, kx), hh), _attn_window((vp, vc, vx), hh)
            kctx, vctx = kctx_ref[:, sl], vctx_ref[:, sl]
            q, d_o = _stack_heads(q_ref, hh), _stack_heads(do_ref, hh)
            p_lat, p_ctx, p_snk, den = _attn_probs(q, kw, kctx, _stack_sinks(sink_ref, h * KV_PER_STEP + hh), valid)
            inv = 1.0 / den
            p_lat, p_ctx, p_snk = p_lat * inv, p_ctx * inv, p_snk * inv
            dp_lat = _dot(d_o, vw, NT)
            dp_ctx = _dot(d_o, vctx, NT)
            dr = jnp.sum(p_lat * dp_lat, axis=-1, keepdims=True) + jnp.sum(p_ctx * dp_ctx, axis=-1, keepdims=True)
            ds_lat = (p_lat * (dp_lat - dr) * scale).astype(BF16)
            ds_ctx = (p_ctx * (dp_ctx - dr) * scale).astype(BF16)
            dq = _dot(ds_lat, kw, NN) + _dot(ds_ctx, kctx, NN)
            snk_terms = p_snk * dr
            dsn = jnp.zeros((8, HEAD_DIM), F32)
            for g in range(Q_PER_KV):
                c0 = (hh * Q_PER_KV + g) * HEAD_DIM
                dq_ref[:, c0:c0 + HEAD_DIM] = dq[g * WINDOW:(g + 1) * WINDOW]
                dsn = dsn + jnp.where(lane == g, -jnp.sum(snk_terms[g * WINDOW:(g + 1) * WINDOW], axis=0, keepdims=True),
                                      0.0)
            dkw_ref[rows, sl] += _dot(ds_lat, q, TN)
            dvw_ref[rows, sl] += _dot(p_lat.astype(BF16), d_o, TN)
            dkc_ref[:, sl] += _dot(ds_ctx, q, TN)
            dvc_ref[:, sl] += _dot(p_ctx.astype(BF16), d_o, TN)
            dsn_ref[hh] += dsn

    wacc = pl.BlockSpec((TP, KV_PER_STEP * HEAD_DIM), lambda h, i: (0, h))
    return _pcall(
        body, name=name, grid=(N_KV_HEADS // KV_PER_STEP, nb),
        in_specs=[qspec] + win + win + [ctx, ctx, sink, qspec],
        out_specs=[qspec, wacc, wacc, ctx, ctx, pl.BlockSpec((KV_PER_STEP, 8, HEAD_DIM), lambda h, i: (h, 0, 0))],
        out_shape=[jax.ShapeDtypeStruct(qn.shape, F32),
                   jax.ShapeDtypeStruct((TP, N_KV_HEADS * HEAD_DIM), F32),
                   jax.ShapeDtypeStruct((TP, N_KV_HEADS * HEAD_DIM), F32),
                   jax.ShapeDtypeStruct((n_ctx, N_KV_HEADS * HEAD_DIM), F32),
                   jax.ShapeDtypeStruct((n_ctx, N_KV_HEADS * HEAD_DIM), F32),
                   jax.ShapeDtypeStruct((N_KV_HEADS, 8, HEAD_DIM), F32)],
        sem=("arbitrary", "arbitrary"), args=(qn, kn, kn, kn, vb, vb, vb, kn, vb, sink_rows, do), ride=ride)


def _gla_masks(dirv):
    C = GLA_CHUNK

    def times(reps):
        r = lax.broadcasted_iota(jnp.int32, (C, reps * C), 0)
        c = lax.broadcasted_iota(jnp.int32, (C, reps * C), 1) & (C - 1)
        return jnp.where(dirv == 0, r, C - 1 - r), jnp.where(dirv == 0, c, C - 1 - c)

    def level(tt, ss, m):
        sh = m.bit_length() - 1
        same = (tt >> (sh + 1)) == (ss >> (sh + 1))
        return same, (tt >> sh) & 1, (ss >> sh) & 1

    tt, ss = times(3)
    le = (ss <= tt).astype(jnp.int32)
    sums = [le == 1]
    for m in GLA_LEVELS:
        same, ut, us = level(tt, ss, m)
        sums.append(same & (ut == us) & (ut == le))
    tt, ss = times(1)
    blocks = [ss == tt]
    for m in GLA_LEVELS:
        same, ut, us = level(tt, ss, m)
        blocks.append(same & (ut == 1) & (us == 0))
    mall3 = jnp.concatenate([jnp.where(s, 1.0, 0.0) for s in sums], axis=0).astype(BF16)
    return mall3, blocks


def _pieces(x):
    hi = x.astype(BF16)
    r1 = x - hi.astype(F32)
    mid = r1.astype(BF16)
    return hi, mid, (r1 - mid.astype(F32)).astype(BF16)


def _sum_f32(mall3, x):
    return _dot(mall3, jnp.concatenate(_pieces(x), axis=0), NN)


def _sum_f32_t(mall3, x):
    m = mall3[:, 0:GLA_CHUNK]
    hi, mid, lo = _pieces(x)
    return _dot(m, hi, TN) + _dot(m, mid, TN) + _dot(m, lo, TN)


def _gla_chunk_of(dirv, j, lc, nc):
    return jnp.where(dirv == 0, j, jnp.where(j < lc, lc - 1 - j, nc + lc - 1 - j))


def _gla_gate(lr_ref, wg_ref, bg_ref):
    pre = _dot(lr_ref[...].astype(BF16), wg_ref[0].astype(BF16), NN) + bg_ref[0]
    g = (jnp.minimum(pre, 0.0) - jnp.log(1.0 + jnp.exp(-jnp.abs(pre)))) * (1.0 / GLA_GATE_NORM)
    return pre, g


def _gla_fwd(z, qblk, kblk, vblk, lrblk, wg, bg, DV, n_ctx, name):
    R = z.shape[0]
    C = GLA_CHUNK
    DK = wg.shape[2] // GLA_HEADS
    nc, lc = R // C, n_ctx // C
    qscale = DK ** -0.5

    GK, GV = GLA_HEADS * DK, GLA_HEADS * DV

    def body(q_ref, k_ref, v_ref, lr_ref, wg_ref, bg_ref, o_ref, sp_ref, st_ref):
        dirv, j = pl.program_id(0), pl.program_id(1)

        @pl.when(j == 0)
        def _():
            st_ref[...] = jnp.zeros_like(st_ref)

        mall, blocks = _gla_masks(dirv)
        _, g_all = _gla_gate(lr_ref, wg_ref, bg_ref)
        E_all = _sum_f32(mall, g_all)
        for h in range(GLA_HEADS):
            ks, vs = slice(h * DK, (h + 1) * DK), slice(h * DV, (h + 1) * DV)
            q, k, v = q_ref[:, ks] * qscale, k_ref[:, ks], v_ref[:, vs].astype(BF16)
            g, E = g_all[:, ks], E_all[:, ks]
            st = st_ref[h]
            sp_ref[0, h, 0] = st
            A = jnp.where(blocks[0], _dot(q.astype(BF16), k.astype(BF16), NT), 0.0)
            for l in range(len(GLA_LEVELS)):
                e = jnp.exp(E[(1 + l) * C:(2 + l) * C])
                A = A + jnp.where(blocks[l + 1], _dot((q * e).astype(BF16), (k * e).astype(BF16), NT), 0.0)
            o_ref[0, :, vs] = (_dot((q * jnp.exp(E[0:C])).astype(BF16), st.astype(BF16), NT)
                               + _dot(A.astype(BF16), v, NN))
            last = jnp.sum(g, axis=0, keepdims=True)
            st_ref[h] = jnp.exp(last) * st + _dot(v, (k * jnp.exp(last - E[0:C])).astype(BF16), TN)

    chunk = functools.partial(_gla_chunk_of, lc=lc, nc=nc)
    return pl.pallas_call(
        body, name=name, grid=(2, nc),
        in_specs=[pl.BlockSpec((C, GK), lambda d, j: (chunk(d, j), qblk)),
                  pl.BlockSpec((C, GK), lambda d, j: (chunk(d, j), kblk)),
                  pl.BlockSpec((C, GV), lambda d, j: (chunk(d, j), vblk)),
                  pl.BlockSpec((C, LANES), lambda d, j: (chunk(d, j), lrblk)),
                  pl.BlockSpec((1, LANES, GK), lambda d, j: (d, 0, 0)),
                  pl.BlockSpec((1, 1, GK), lambda d, j: (d, 0, 0))],
        out_specs=[pl.BlockSpec((1, C, GV), lambda d, j: (d, chunk(d, j), 0)),
                   pl.BlockSpec((1, GLA_HEADS, 1, DV, DK), lambda d, j: (d, 0, j, 0, 0))],
        out_shape=[jax.ShapeDtypeStruct((2, R, GV), F32),
                   jax.ShapeDtypeStruct((2, GLA_HEADS, nc, DV, DK), F32)],
        scratch_shapes=[pltpu.VMEM((GLA_HEADS, DV, DK), F32)],
        compiler_params=_cparams("parallel", "arbitrary"),
    )(z, z, z, z, wg, bg)


def _gla_bwd(z, qblk, kblk, vblk, lrblk, wg, bg, sprev, do, n_ctx, name, ride=None):
    R = z.shape[0]
    C = GLA_CHUNK
    DK, DV = wg.shape[2] // GLA_HEADS, do.shape[1] // GLA_HEADS
    nc, lc = R // C, n_ctx // C
    qscale = DK ** -0.5
    nl = len(GLA_LEVELS)

    GK, GV = GLA_HEADS * DK, GLA_HEADS * DV

    def body(q_ref, k_ref, v_ref, lr_ref, wg_ref, bg_ref, sp_ref, do_ref,
             dq_ref, dk_ref, dv_ref, dpre_ref, dbg_ref, dst_ref):
        dirv, jr = pl.program_id(0), pl.program_id(1)

        @pl.when(jr == 0)
        def _():
            dst_ref[...] = jnp.zeros_like(dst_ref)
            dbg_ref[...] = jnp.zeros_like(dbg_ref)

        mall, blocks = _gla_masks(dirv)
        pre_all, g_all = _gla_gate(lr_ref, wg_ref, bg_ref)
        E_all = _sum_f32(mall, g_all)
        for h in range(GLA_HEADS):
            ks, vs = slice(h * DK, (h + 1) * DK), slice(h * DV, (h + 1) * DV)
            q, k, v = q_ref[:, ks] * qscale, k_ref[:, ks], v_ref[:, vs].astype(BF16)
            pre, g, E = pre_all[:, ks], g_all[:, ks], E_all[:, ks]
            last = jnp.sum(g, axis=0, keepdims=True)
            eb, er, decay = jnp.exp(E[0:C]), jnp.exp(last - E[0:C]), jnp.exp(last)
            st = sp_ref[0, h, 0]
            dst = dst_ref[h]
            d_o = do_ref[:, vs]
            qe, kd = q * eb, k * er
            qb, kb = q.astype(BF16), k.astype(BF16)
            A = jnp.where(blocks[0], _dot(qb, kb, NT), 0.0)
            levels = []
            for l in range(nl):
                e = jnp.exp(E[(1 + l) * C:(2 + l) * C])
                ql, kl = q * e, k * e
                levels.append((e, ql, kl, ql.astype(BF16), kl.astype(BF16)))
                A = A + jnp.where(blocks[l + 1], _dot(levels[l][3], levels[l][4], NT), 0.0)
            dA = _dot(d_o, v, NT)
            dv_ref[0, :, vs] = _dot(A.astype(BF16), d_o, TN) + _dot(kd.astype(BF16), dst.astype(BF16), NT)
            dqe = _dot(d_o, st.astype(BF16), NN)
            dkd = _dot(v, dst.astype(BF16), NN)
            G = jnp.where(blocks[0], dA, 0.0).astype(BF16)
            dq = dqe * eb + _dot(G, kb, NN)
            dk = dkd * er + _dot(G, qb, TN)
            dEr = dkd * kd
            dE = [dqe * qe - dEr]
            for l in range(nl):
                e, ql, kl, qlb, klb = levels[l]
                G = jnp.where(blocks[l + 1], dA, 0.0).astype(BF16)
                dql = _dot(G, klb, NN)
                dkl = _dot(G, qlb, TN)
                dq = dq + dql * e
                dk = dk + dkl * e
                dE.append(dql * ql + dkl * kl)
            dlast = jnp.sum(dst * st, axis=0, keepdims=True) * decay + jnp.sum(dEr, axis=0, keepdims=True)
            dg = _sum_f32_t(mall, jnp.concatenate(dE, axis=0)) + dlast
            dpre = dg * (1.0 / GLA_GATE_NORM) / (1.0 + jnp.exp(pre))
            dq_ref[0, :, ks] = dq * qscale
            dk_ref[0, :, ks] = dk
            dpre_ref[:, ks] = dpre.astype(BF16)
            dbg_ref[0, :, ks] += jnp.sum(dpre, axis=0, keepdims=True)
            dst_ref[h] = decay * dst + _dot(d_o, qe.astype(BF16), TN)

    def chunk(d, jr):
        return _gla_chunk_of(d, nc - 1 - jr, lc, nc)

    return _pcall(
        body, name=name, grid=(2, nc),
        in_specs=[pl.BlockSpec((C, GK), lambda d, j: (chunk(d, j), qblk)),
                  pl.BlockSpec((C, GK), lambda d, j: (chunk(d, j), kblk)),
                  pl.BlockSpec((C, GV), lambda d, j: (chunk(d, j), vblk)),
                  pl.BlockSpec((C, LANES), lambda d, j: (chunk(d, j), lrblk)),
                  pl.BlockSpec((1, LANES, GK), lambda d, j: (d, 0, 0)),
                  pl.BlockSpec((1, 1, GK), lambda d, j: (d, 0, 0)),
                  pl.BlockSpec((1, GLA_HEADS, 1, DV, DK), lambda d, j: (d, 0, nc - 1 - j, 0, 0)),
                  pl.BlockSpec((C, GV), lambda d, j: (chunk(d, j), 0))],
        out_specs=[pl.BlockSpec((1, C, GK), lambda d, j: (d, chunk(d, j), 0)),
                   pl.BlockSpec((1, C, GK), lambda d, j: (d, chunk(d, j), 0)),
                   pl.BlockSpec((1, C, GV), lambda d, j: (d, chunk(d, j), 0)),
                   pl.BlockSpec((C, GK), lambda d, j: (chunk(d, j), d)),
                   pl.BlockSpec((1, 1, GK), lambda d, j: (d, 0, 0))],
        out_shape=[jax.ShapeDtypeStruct((2, R, GK), F32),
                   jax.ShapeDtypeStruct((2, R, GK), F32),
                   jax.ShapeDtypeStruct((2, R, GV), F32),
                   jax.ShapeDtypeStruct((R, 2 * GK), BF16),
                   jax.ShapeDtypeStruct((2, 1, GK), F32)],
        scratch_shapes=[pltpu.VMEM((GLA_HEADS, DV, DK), F32)],
        sem=("arbitrary", "arbitrary"), args=(z, z, z, z, wg, bg, sprev, do), ride=ride)


def _glanorm_fwd(o, z, rbblk, gn, n_ctx, name):
    _, R, GV = o.shape
    T = R - n_ctx
    DV = GV // GLA_HEADS
    tm = _pick(n_ctx, 256, 8)
    ro = n_ctx // tm

    def body(o0_ref, o1_ref, rb_ref, gn_ref, p_ref):
        gnv = gn_ref[...]
        for h in range(GLA_HEADS):
            sl = slice(h * DV, (h + 1) * DV)
            og = o0_ref[0, :, sl] + o1_ref[0, :, sl]
            r = lax.rsqrt(jnp.mean(og * og, axis=-1, keepdims=True) + EPS)
            p_ref[:, sl] = (og * r * gnv * _silu(rb_ref[:, sl])).astype(BF16)

    return pl.pallas_call(
        body, name=name, grid=(T // tm,),
        in_specs=[pl.BlockSpec((1, tm, GV), lambda i: (0, i + ro, 0)), pl.BlockSpec((1, tm, GV), lambda i: (1, i + ro, 0)),
                  pl.BlockSpec((tm, GV), lambda i: (i + ro, rbblk)), pl.BlockSpec((1, DV), lambda i: (0, 0))],
        out_specs=pl.BlockSpec((tm, GV), lambda i: (i, 0)),
        out_shape=jax.ShapeDtypeStruct((T, GV), BF16), compiler_params=_cparams("parallel"))(o, o, z, gn)


def _glanorm_bwd(o, z, rbblk, gn, dp, n_ctx, name):
    _, R, GV = o.shape
    T = R - n_ctx
    DV = GV // GLA_HEADS
    tm = _pick(n_ctx, 256, 8)
    ro = n_ctx // tm

    def body(o0_ref, o1_ref, rb_ref, gn_ref, dp_ref, do_ref, drb_ref, acc_ref):
        i = pl.program_id(0)

        @pl.when(i == 0)
        def _():
            acc_ref[...] = jnp.zeros_like(acc_ref)

        gnv = gn_ref[...]
        dgn = jnp.zeros((1, DV), F32)
        for h in range(GLA_HEADS):
            sl = slice(h * DV, (h + 1) * DV)
            og = o0_ref[0, :, sl] + o1_ref[0, :, sl]
            rb = rb_ref[:, sl]
            d = dp_ref[:, sl]
            r = lax.rsqrt(jnp.mean(og * og, axis=-1, keepdims=True) + EPS)
            xh = og * r
            drb_ref[:, sl] = (d * xh * gnv * _dsilu(rb)).astype(BF16)
            dn = d * _silu(rb)
            dgn = dgn + jnp.sum(dn * xh, axis=0, keepdims=True)
            dxh = dn * gnv
            do_ref[:, sl] = (r * (dxh - xh * jnp.mean(dxh * xh, axis=-1, keepdims=True))).astype(BF16)
        acc_ref[0:1, :] += dgn

    row = pl.BlockSpec((tm, GV), lambda i: (i, 0))
    return pl.pallas_call(
        body, name=name, grid=(T // tm,),
        in_specs=[pl.BlockSpec((1, tm, GV), lambda i: (0, i + ro, 0)), pl.BlockSpec((1, tm, GV), lambda i: (1, i + ro, 0)),
                  pl.BlockSpec((tm, GV), lambda i: (i + ro, rbblk)), pl.BlockSpec((1, DV), lambda i: (0, 0)), row],
        out_specs=[row, row, pl.BlockSpec((8, DV), lambda i: (0, 0))],
        out_shape=[jax.ShapeDtypeStruct((T, GV), BF16), jax.ShapeDtypeStruct((T, GV), BF16),
                   jax.ShapeDtypeStruct((8, DV), F32)],
        compiler_params=_cparams("arbitrary"))(o, o, z, gn, dp)


def _gate_fwd(z, gablk, gbblk, ya, yg, n_ctx, name):
    T, D = ya.shape
    tm = _pick(n_ctx, 256, 8)
    ro = n_ctx // tm

    def body(ga_ref, gb_ref, ya_ref, yg_ref, m_ref):
        m_ref[...] = (_sigmoid(ga_ref[...]) * ya_ref[...] + _sigmoid(gb_ref[...]) * yg_ref[...]).astype(BF16)

    row = pl.BlockSpec((tm, D), lambda i: (i, 0))
    return pl.pallas_call(
        body, name=name, grid=(T // tm,),
        in_specs=[pl.BlockSpec((tm, D), lambda i: (i + ro, gablk)), pl.BlockSpec((tm, D), lambda i: (i + ro, gbblk)), row, row],
        out_specs=row, out_shape=jax.ShapeDtypeStruct((T, D), BF16), compiler_params=_cparams("parallel"))(z, z, ya, yg)


def _gate_bwd(z, gablk, gbblk, ya, yg, dm, n_ctx, name):
    T, D = ya.shape
    tm = _pick(n_ctx, 256, 8)
    ro = n_ctx // tm

    def body(ga_ref, gb_ref, ya_ref, yg_ref, dm_ref, dya_ref, dyg_ref, dga_ref, dgb_ref):
        d = dm_ref[...]
        sa, sb = _sigmoid(ga_ref[...]), _sigmoid(gb_ref[...])
        dya_ref[...] = (d * sa).astype(BF16)
        dyg_ref[...] = (d * sb).astype(BF16)
        dga_ref[...] = (d * ya_ref[...] * sa * (1.0 - sa)).astype(BF16)
        dgb_ref[...] = (d * yg_ref[...] * sb * (1.0 - sb)).astype(BF16)

    row = pl.BlockSpec((tm, D), lambda i: (i, 0))
    sh = jax.ShapeDtypeStruct((T, D), BF16)
    return pl.pallas_call(
        body, name=name, grid=(T // tm,),
        in_specs=[pl.BlockSpec((tm, D), lambda i: (i + ro, gablk)), pl.BlockSpec((tm, D), lambda i: (i + ro, gbblk)), row, row, row],
        out_specs=[row] * 4, out_shape=[sh] * 4, compiler_params=_cparams("parallel"))(z, z, ya, yg, dm)


def _resnorm_fwd(x, mix, gt, g, sc, sh, name):
    T, D = x.shape
    tm = _pick(T, 256, 8)

    def body(x_ref, mix_ref, gt_ref, g_ref, sc_ref, sh_ref, x1_ref, h_ref):
        x1 = x_ref[...] + gt_ref[...] * mix_ref[...]
        x1_ref[...] = x1
        r = lax.rsqrt(jnp.mean(x1 * x1, axis=-1, keepdims=True) + EPS)
        h_ref[...] = (x1 * r * g_ref[...] * (1.0 + sc_ref[...]) + sh_ref[...]).astype(BF16)

    row = pl.BlockSpec((tm, D), lambda i: (i, 0))
    vec = pl.BlockSpec((1, D), lambda i: (0, 0))
    return pl.pallas_call(
        body, name=name, grid=(T // tm,), in_specs=[row, row, vec, vec, vec, vec], out_specs=[row, row],
        out_shape=[jax.ShapeDtypeStruct((T, D), F32), jax.ShapeDtypeStruct((T, D), BF16)],
        compiler_params=_cparams("parallel"))(x, mix, gt, g, sc, sh)


def _loss_head(d, x1, gt, target, name):
    T, D = d.shape
    tm = _pick(T, 256, 8)

    def body(d_ref, x1_ref, gt_ref, t_ref, dy_ref, dd_ref, acc_ref):
        i = pl.program_id(0)

        @pl.when(i == 0)
        def _():
            acc_ref[...] = jnp.zeros_like(acc_ref)

        dv, gtv = d_ref[...], gt_ref[...]
        e = x1_ref[...] + gtv * dv - t_ref[...]
        dy = e * (1.0 / D)
        dy_ref[...] = dy
        dd_ref[...] = (dy * gtv).astype(BF16)
        acc_ref[0:1, :] += jnp.sum(e * e, axis=0, keepdims=True)
        acc_ref[1:2, :] += jnp.sum(dy * dv, axis=0, keepdims=True)

    row = pl.BlockSpec((tm, D), lambda i: (i, 0))
    return pl.pallas_call(
        body, name=name, grid=(T // tm,), in_specs=[row, row, pl.BlockSpec((1, D), lambda i: (0, 0)), row],
        out_specs=[row, row, pl.BlockSpec((8, D), lambda i: (0, 0))],
        out_shape=[jax.ShapeDtypeStruct((T, D), F32), jax.ShapeDtypeStruct((T, D), BF16),
                   jax.ShapeDtypeStruct((8, D), F32)],
        compiler_params=_cparams("arbitrary"))(d, x1, gt, target)


def _halo_specs(T, tm, tw, col_of, order):
    n8 = tm // 8
    if order == "ij":
        mid = lambda i, j: (i, col_of(j))
        prev = lambda i, j: (jnp.maximum(i * n8 - 1, 0), col_of(j))
        nxt = lambda i, j: (jnp.minimum((i + 1) * n8, T // 8 - 1), col_of(j))
    else:
        mid = lambda j, i: (i, col_of(j))
        prev = lambda j, i: (jnp.maximum(i * n8 - 1, 0), col_of(j))
        nxt = lambda j, i: (jnp.minimum((i + 1) * n8, T // 8 - 1), col_of(j))
    return [pl.BlockSpec((tm, tw), mid), pl.BlockSpec((8, tw), prev), pl.BlockSpec((8, tw), nxt)]


def _shift_rows(x, before, after):
    tm = x.shape[0]
    row = lax.broadcasted_iota(jnp.int32, x.shape, 0)
    return (jnp.where(row == 0, before, pltpu.roll(x, 1, 0)),
            jnp.where(row == tm - 1, after, pltpu.roll(x, tm - 1, 0)))


def _conv_fwd(u, cw, cb, name):
    T, F2 = u.shape
    F = F2 // 2
    tm, tw = _pick(T, 256, 8), _pick(F, 512)
    nt, nw = T // tm, F // tw

    def body(ua, uap, uan, ug, ugp, ugn, cwa, cwg, cba, cbg, f_ref):
        i = pl.program_id(0)
        first, last = i == 0, i == nt - 1

        def conv(u_ref, up_ref, un_ref, w_ref, b_ref):
            m = u_ref[...]
            p, n = _shift_rows(m, jnp.where(first, 0.0, up_ref[7:8, :]), jnp.where(last, 0.0, un_ref[0:1, :]))
            return p * w_ref[0:1, :] + m * w_ref[1:2, :] + n * w_ref[2:3, :] + b_ref[...]

        a = conv(ua, uap, uan, cwa, cba)
        g = conv(ug, ugp, ugn, cwg, cbg)
        f_ref[...] = (_silu(a) * g).astype(BF16)

    wspec = lambda off: pl.BlockSpec((3, tw), lambda i, j: (0, j + off))
    bspec = lambda off: pl.BlockSpec((1, tw), lambda i, j: (0, j + off))
    return pl.pallas_call(
        body, name=name, grid=(nt, nw),
        in_specs=_halo_specs(T, tm, tw, lambda j: j, "ij") + _halo_specs(T, tm, tw, lambda j: j + nw, "ij")
        + [wspec(0), wspec(nw), bspec(0), bspec(nw)],
        out_specs=pl.BlockSpec((tm, tw), lambda i, j: (i, j)),
        out_shape=jax.ShapeDtypeStruct((T, F), BF16),
        compiler_params=_cparams("parallel", "parallel"),
    )(u, u, u, u, u, u, cw, cw, cb, cb)


def _conv_bwd(u, df, cw, cb, name):
    T, F2 = u.shape
    F = F2 // 2
    tm, tw = _pick(T, 256, 8), _pick(F, 512)
    nt, nw = T // tm, F // tw

    def body(ua, uap, uan, ug, ugp, ugn, cwa, cwg, cba, cbg, df_ref, dfp, dfn, du_ref, acca_ref, accg_ref):
        i = pl.program_id(1)

        @pl.when(i == 0)
        def _():
            acca_ref[...] = jnp.zeros_like(acca_ref)
            accg_ref[...] = jnp.zeros_like(accg_ref)

        first, last = i == 0, i == nt - 1
        wa, wg, ba, bg = cwa[...], cwg[...], cba[...], cbg[...]

        def conv(p, m, n, w, b):
            return p * w[0:1] + m * w[1:2] + n * w[2:3] + b

        def grads(a, g, d):
            return d * g * _dsilu(a), d * _silu(a)

        xa, xg, d = ua[...], ug[...], df_ref[...]
        sa = _shift_rows(xa, jnp.where(first, 0.0, uap[7:8, :]), jnp.where(last, 0.0, uan[0:1, :]))
        sg = _shift_rows(xg, jnp.where(first, 0.0, ugp[7:8, :]), jnp.where(last, 0.0, ugn[0:1, :]))
        da, dg = grads(conv(sa[0], xa, sa[1], wa, ba), conv(sg[0], xg, sg[1], wg, bg), d)
        da_p, dg_p = grads(conv(uap[6:7, :], uap[7:8, :], xa[0:1], wa, ba),
                           conv(ugp[6:7, :], ugp[7:8, :], xg[0:1], wg, bg), dfp[7:8, :])
        da_n, dg_n = grads(conv(xa[tm - 1:tm], uan[0:1, :], uan[1:2, :], wa, ba),
                           conv(xg[tm - 1:tm], ugn[0:1, :], ugn[1:2, :], wg, bg), dfn[0:1, :])
        ta = _shift_rows(da, jnp.where(first, 0.0, da_p), jnp.where(last, 0.0, da_n))
        tg = _shift_rows(dg, jnp.where(first, 0.0, dg_p), jnp.where(last, 0.0, dg_n))
        du_ref[0] = (ta[1] * wa[0:1] + da * wa[1:2] + ta[0] * wa[2:3]).astype(BF16)
        du_ref[1] = (tg[1] * wg[0:1] + dg * wg[1:2] + tg[0] * wg[2:3]).astype(BF16)
        for t, (va, vg) in enumerate(((sa[0], sg[0]), (xa, xg), (sa[1], sg[1]))):
            acca_ref[t:t + 1, :] += jnp.sum(da * va, axis=0, keepdims=True)
            accg_ref[t:t + 1, :] += jnp.sum(dg * vg, axis=0, keepdims=True)
        acca_ref[3:4, :] += jnp.sum(da, axis=0, keepdims=True)
        accg_ref[3:4, :] += jnp.sum(dg, axis=0, keepdims=True)

    wspec = lambda off: pl.BlockSpec((3, tw), lambda j, i: (0, j + off))
    bspec = lambda off: pl.BlockSpec((1, tw), lambda j, i: (0, j + off))
    row = pl.BlockSpec((tm, tw), lambda j, i: (i, j))
    acc = pl.BlockSpec((8, tw), lambda j, i: (0, j))
    return pl.pallas_call(
        body, name=name, grid=(nw, nt),
        in_specs=_halo_specs(T, tm, tw, lambda j: j, "ji") + _halo_specs(T, tm, tw, lambda j: j + nw, "ji")
        + [wspec(0), wspec(nw), bspec(0), bspec(nw)] + _halo_specs(T, tm, tw, lambda j: j, "ji"),
        out_specs=[pl.BlockSpec((2, tm, tw), lambda j, i: (0, i, j)), acc, acc],
        out_shape=[jax.ShapeDtypeStruct((2, T, F), BF16),
                   jax.ShapeDtypeStruct((8, F), F32), jax.ShapeDtypeStruct((8, F), F32)],
        compiler_params=_cparams("parallel", "arbitrary"),
    )(u, u, u, u, u, u, cw, cw, cb, cb, df, df, df)


def _assemble_dz(lay, z_used, Z, n_ctx, dqa, drb, dga, dgb, dka, dva, dvg, dqg, dkg, dlr, name):
    T = dqa.shape[0]
    R = T + n_ctx
    tm = _pick(n_ctx, 128, 8)
    cb = n_ctx // tm

    def body(dqa_ref, drb_ref, dga_ref, dgb_ref, dka_ref, dva_ref, dvg0, dvg1, dqg0, dqg1, dkg0, dkg1, dlr_ref, o_ref):
        lat = pl.program_id(0) >= cb

        def put(seg, val):
            o_ref[:, lay[seg]:lay[seg] + val.shape[1]] = val.astype(BF16)

        def lat_only(ref):
            v = ref[...]
            return jnp.where(lat, v, jnp.zeros_like(v))

        put("qa", lat_only(dqa_ref))
        put("rb", lat_only(drb_ref))
        put("ga", lat_only(dga_ref))
        put("gb", lat_only(dgb_ref))
        put("ka", dka_ref[...])
        put("va", dva_ref[...])
        put("vb", dvg0[0] + dvg1[0])
        put("qb", dqg0[0] + dqg1[0])
        put("kb", dkg0[0] + dkg1[0])
        put("lr", dlr_ref[...])
        if Z > z_used:
            o_ref[:, z_used:] = jnp.zeros((tm, Z - z_used), BF16)

    lat_spec = lambda a: pl.BlockSpec((tm, a.shape[1]), lambda i: (jnp.maximum(i - cb, 0), 0))
    all_spec = lambda a: pl.BlockSpec((tm, a.shape[1]), lambda i: (i, 0))
    dir_specs = lambda a: [pl.BlockSpec((1, tm, a.shape[2]), lambda i: (0, i, 0)),
                           pl.BlockSpec((1, tm, a.shape[2]), lambda i: (1, i, 0))]
    return pl.pallas_call(
        body, name=name, grid=(R // tm,),
        in_specs=[lat_spec(dqa), lat_spec(drb), lat_spec(dga), lat_spec(dgb), all_spec(dka), all_spec(dva)]
        + dir_specs(dvg) + dir_specs(dqg) + dir_specs(dkg) + [all_spec(dlr)],
        out_specs=pl.BlockSpec((tm, Z), lambda i: (i, 0)),
        out_shape=jax.ShapeDtypeStruct((R, Z), BF16), compiler_params=_cparams("parallel"),
    )(dqa, drb, dga, dgb, dka, dva, dvg, dvg, dqg, dqg, dkg, dkg, dlr)


def _mod_fwd(ca, w, b, name):
    n, D = ca.shape
    N = w.shape[1]
    tn = _pick(N, 512)

    def body(c_ref, w_ref, b_ref, o_ref, s_ref):
        s = _silu(c_ref[...])
        s_ref[...] = s
        o_ref[...] = _dot(s.astype(BF16), w_ref[...].astype(BF16), NN) + b_ref[...]

    return pl.pallas_call(
        body, name=name, grid=(N // tn,),
        in_specs=[pl.BlockSpec((n, D), lambda j: (0, 0)), pl.BlockSpec((D, tn), lambda j: (0, j)),
                  pl.BlockSpec((1, tn), lambda j: (0, j))],
        out_specs=[pl.BlockSpec((n, tn), lambda j: (0, j)), pl.BlockSpec((n, D), lambda j: (0, 0))],
        out_shape=[jax.ShapeDtypeStruct((n, N), F32), jax.ShapeDtypeStruct((n, D), F32)],
        compiler_params=_cparams("arbitrary"))(ca, w, b)


def _silu_bwd(dsil, ca, name):
    def body(d_ref, c_ref, o_ref):
        o_ref[...] = d_ref[...] * _dsilu(c_ref[...])

    return pl.pallas_call(body, name=name, out_shape=jax.ShapeDtypeStruct(ca.shape, F32))(dsil, ca)


def _adam_math(w, g, m, v):
    c1 = 1.0 - ADAM_B1 ** ADAM_STEP
    c2 = 1.0 - ADAM_B2 ** ADAM_STEP
    mn = ADAM_B1 * m + (1.0 - ADAM_B1) * g
    vn = ADAM_B2 * v + (1.0 - ADAM_B2) * (g * g)
    return -ADAM_LR * ((mn / c1) / (jnp.sqrt(vn / c2) + ADAM_EPS) + ADAM_WD * w), mn, vn


def _adamw(w, g, m, v, name, ride=None):
    Rw, Cw = w.shape
    tr = _pick(Rw, 128, 8)

    def body(w_ref, g_ref, m_ref, v_ref, d_ref, mo_ref, vo_ref):
        d_ref[...], mo_ref[...], vo_ref[...] = _adam_math(w_ref[...], g_ref[...], m_ref[...], v_ref[...])

    row = pl.BlockSpec((tr, Cw), lambda i: (i, 0))
    sh = jax.ShapeDtypeStruct((Rw, Cw), F32)
    return _pcall(body, name=name, grid=(Rw // tr,), in_specs=[row] * 4, out_specs=[row] * 3, out_shape=[sh] * 3,
                  sem=("parallel",), args=(w, g, m, v), ride=ride)


HBM_SPEC = pl.BlockSpec(memory_space=pltpu.HBM)


def _exchange(inputs, out_shapes, stages, name):
    n_in, n_out = len(inputs), len(out_shapes)
    n = sum(len(s) for s in stages)

    def body(*refs):
        ins, outs = refs[:n_in], refs[n_in:n_in + n_out]
        send_sems, recv_sems = refs[n_in + n_out:]
        k = 0
        for stage in stages:
            copies = _stage_copies(stage, ins, outs, send_sems, recv_sems, k)
            for cp in copies:
                cp.start()
            for cp in copies:
                cp.wait()
            k += len(stage)

    return pl.pallas_call(
        body, name=name, in_specs=[HBM_SPEC] * n_in, out_specs=[HBM_SPEC] * n_out, out_shape=out_shapes,
        scratch_shapes=[pltpu.SemaphoreType.DMA((n,)), pltpu.SemaphoreType.DMA((n,))],
    )(*inputs)


def _stage_copies(stage, ins, outs, send_sems, recv_sems, k0=0):
    me = (lax.axis_index("x"), lax.axis_index("y"), lax.axis_index("c"))
    copies = []
    for k, ((skind, sidx), sfn, didx, dfn, flip) in enumerate(stage):
        src = (ins if skind == "in" else outs)[sidx].at[sfn(*me)]
        dst = outs[didx].at[dfn(*me)]
        if flip == (0, 0, 0):
            copies.append(pltpu.make_async_copy(src, dst, send_sems.at[k0 + k]))
        else:
            peer = tuple(1 - a if f else a for a, f in zip(me, flip))
            copies.append(pltpu.make_async_remote_copy(src, dst, send_sems.at[k0 + k], recv_sems.at[k0 + k],
                                                       device_id=peer, device_id_type=MESH))
    return copies


def _pcall(body, *, name, grid, in_specs, out_specs, out_shape, scratch_shapes=(), sem, args, ride=None):
    many = isinstance(out_shape, (list, tuple))
    out_specs, out_shape = (list(out_specs), list(out_shape)) if many else ([out_specs], [out_shape])
    if ride is None:
        res = pl.pallas_call(body, name=name, grid=grid, in_specs=list(in_specs), out_specs=out_specs,
                             out_shape=out_shape, scratch_shapes=list(scratch_shapes),
                             compiler_params=_cparams(*sem))(*args)
        return res if many else res[0]
    x_in, x_out, stage, aliases = ride
    n_in, n_out, n_scr, n_xin, n_xout = len(in_specs), len(out_specs), len(scratch_shapes), len(x_in), len(x_out)

    def wrapped(*refs):
        ins, xins = refs[:n_in], refs[n_in:n_in + n_xin]
        o0 = n_in + n_xin
        outs, xouts = refs[o0:o0 + n_out], refs[o0 + n_out:o0 + n_out + n_xout]
        s0 = o0 + n_out + n_xout
        scr, (send_sems, recv_sems) = refs[s0:s0 + n_scr], refs[s0 + n_scr:]
        first = functools.reduce(jnp.logical_and, [pl.program_id(d) == 0 for d in range(len(grid))])
        last = functools.reduce(jnp.logical_and, [pl.program_id(d) == grid[d] - 1 for d in range(len(grid))])

        @pl.when(first)
        def _():
            for cp in _stage_copies(stage, xins, xouts, send_sems, recv_sems):
                cp.start()

        body(*ins, *outs, *scr)

        @pl.when(last)
        def _():
            for cp in _stage_copies(stage, xins, xouts, send_sems, recv_sems):
                cp.wait()

    res = pl.pallas_call(
        wrapped, name=name, grid=grid, in_specs=list(in_specs) + [HBM_SPEC] * n_xin,
        out_specs=out_specs + [HBM_SPEC] * n_xout, out_shape=out_shape + list(x_out),
        scratch_shapes=list(scratch_shapes) + [pltpu.SemaphoreType.DMA((len(stage),)),
                                               pltpu.SemaphoreType.DMA((len(stage),))],
        input_output_aliases={n_in + a: n_out + b for a, b in aliases.items()},
        compiler_params=_cparams(*(["arbitrary"] * len(grid))))(*args, *x_in)
    main = res[:n_out]
    return (main if many else main[0]), list(res[n_out:])


FLIPS_ALL = [(0, 0, 1), (0, 1, 0), (0, 1, 1), (1, 0, 0), (1, 0, 1), (1, 1, 0), (1, 1, 1)]
FLIPS_CHIP = [(0, 1, 0), (1, 0, 0), (1, 1, 0)]


def _sum_slots(buf, name):
    n, r, w = buf.shape
    tr = _pick(r, 256, 8)

    def body(b_ref, o_ref):
        acc = b_ref[0]
        for s in range(1, n):
            acc = acc + b_ref[s]
        o_ref[...] = acc

    return pl.pallas_call(
        body, name=name, grid=(r // tr,), in_specs=[pl.BlockSpec((n, tr, w), lambda i: (0, i, 0))],
        out_specs=pl.BlockSpec((tr, w), lambda i: (i, 0)), out_shape=jax.ShapeDtypeStruct((r, w), F32),
        compiler_params=_cparams("parallel"))(buf)


def _allreduce_plan(buf):
    whole = lambda x, y, c: (slice(None), slice(None))
    slot = lambda x, y, c: (4 * x + 2 * y + c,)
    stage = [(("in", 0), whole, 0, slot, f) for f in [(0, 0, 0)] + FLIPS_ALL]
    return [jax.ShapeDtypeStruct((8,) + buf.shape, F32)], stage


def _allreduce(buf, name):
    shapes, stage = _allreduce_plan(buf)
    (slots,) = _exchange([buf], shapes, [stage], name + "_x")
    return _sum_slots(slots, name + "_sum")


def _gather_plan(shards, src):
    half = lambda a, c: pl.ds(c * (a.shape[0] // 2), a.shape[0] // 2)
    first, second = [], []
    for n, a in enumerate(shards):
        for f in FLIPS_CHIP:
            first.append((("in", n), lambda x, y, c, a=a: (half(a, c), slice(None)), n,
                          lambda x, y, c, a=a: (2 * x + y, half(a, c), slice(None)), f))
            peer_slot = lambda x, y, c, a=a, f=f: (2 * (x ^ f[0]) + (y ^ f[1]), half(a, c), slice(None))
            second.append(((src, n), peer_slot, n, peer_slot, (0, 0, 1)))
    outs = [jax.ShapeDtypeStruct((4,) + a.shape, a.dtype) for a in shards]
    return first, second, outs


def _allgather_weights(shards, name):
    first, second, outs = _gather_plan(shards, "out")
    return _exchange(shards, outs, [first, second], name)


def _place_own(buf, shard, svec, name):
    _, Rs, Cs = buf.shape
    tr = _pick(Rs, 256, 16)

    def body(s_ref, buf_ref, sh_ref, o_ref):
        o_ref[0] = sh_ref[...]

    grid_spec = pltpu.PrefetchScalarGridSpec(
        num_scalar_prefetch=1, grid=(Rs // tr,),
        in_specs=[pl.BlockSpec(memory_space=pl.ANY), pl.BlockSpec((tr, Cs), lambda i, s: (i, 0))],
        out_specs=pl.BlockSpec((1, tr, Cs), lambda i, s: (s[0], i, 0)))
    return pl.pallas_call(body, name=name, grid_spec=grid_spec, out_shape=jax.ShapeDtypeStruct(buf.shape, buf.dtype),
                          input_output_aliases={1: 0}, compiler_params=_cparams("arbitrary"))(svec, buf, shard)


def _add_pair(G, bufA, cvec, name):
    _, Rs, Cs = G.shape
    Rh = Rs // 2
    tr = _pick(Rh, 128, 16)
    nb = Rh // tr

    def body(c_ref, g_ref, a_ref, o_ref):
        o_ref[...] = (g_ref[...] + a_ref[...]).astype(BF16)

    grid_spec = pltpu.PrefetchScalarGridSpec(
        num_scalar_prefetch=1, grid=(4, nb),
        in_specs=[pl.BlockSpec((1, tr, Cs), lambda s, i, c_ref: (s, c_ref[0] * nb + i, 0)),
                  pl.BlockSpec((1, tr, Cs), lambda s, i, c_ref: (s, i, 0))],
        out_specs=pl.BlockSpec((1, tr, Cs), lambda s, i, c_ref: (s, i, 0)))
    return pl.pallas_call(body, name=name, grid_spec=grid_spec, out_shape=jax.ShapeDtypeStruct((4, Rh, Cs), BF16),
                          compiler_params=_cparams("parallel", "parallel"))(cvec, G, bufA)


def _sum_chips(G, bufA, bufB, cvec, svec, name):
    _, Rs, Cs = G.shape
    Rh = Rs // 2
    tr = _pick(Rh, 128, 16)
    nb = Rh // tr

    def body(c_ref, s_ref, g_ref, a_ref, b_ref, o_ref):
        o_ref[...] = (g_ref[0] + a_ref[0]) + b_ref[0].astype(F32) + b_ref[1].astype(F32) + b_ref[2].astype(F32)

    grid_spec = pltpu.PrefetchScalarGridSpec(
        num_scalar_prefetch=2, grid=(nb,),
        in_specs=[pl.BlockSpec((1, tr, Cs), lambda i, c, s: (s[0], c[0] * nb + i, 0)),
                  pl.BlockSpec((1, tr, Cs), lambda i, c, s: (s[0], i, 0)),
                  pl.BlockSpec((3, tr, Cs), lambda i, c, s: (0, i, 0))],
        out_specs=pl.BlockSpec((tr, Cs), lambda i, c, s: (i, 0)))
    return pl.pallas_call(body, name=name, grid_spec=grid_spec, out_shape=jax.ShapeDtypeStruct((Rh, Cs), F32),
                          compiler_params=_cparams("parallel"))(cvec, svec, G, bufA, bufB)


def _pair_plan(grads):
    Rh = [g.shape[1] // 2 for g in grads]
    whole3 = lambda x, y, c: (slice(None), slice(None), slice(None))
    stage = [(("in", n), lambda x, y, c, n=n: (slice(None), pl.ds((1 - c) * Rh[n], Rh[n]), slice(None)), n,
              whole3, (0, 0, 1)) for n in range(len(grads))]
    return [jax.ShapeDtypeStruct((4, Rh[n], g.shape[2]), F32) for n, g in enumerate(grads)], stage


def _chips_plan(P):
    stage = [(("in", n), lambda x, y, c, f=f: (2 * (x ^ f[0]) + (y ^ f[1]),), n, lambda x, y, c, k=k: (k,), f)
             for n in range(len(P)) for k, f in enumerate(FLIPS_CHIP)]
    return [jax.ShapeDtypeStruct((3,) + p.shape[1:], BF16) for p in P], stage


def _halves_plan(mine):
    whole2 = lambda x, y, c: (slice(None), slice(None))
    stage = [(("in", n), whole2, n, whole2, (0, 0, 1)) for n in range(len(mine))]
    return [jax.ShapeDtypeStruct(r.shape, F32) for r in mine], stage


def _adamw_halves(w, mine, other, m, v, cvec, name):
    Rs, Cs = w.shape
    Rh = Rs // 2
    tr = _pick(Rh, 128, 8)
    nb = Rh // tr

    def body(c_ref, w_ref, a_ref, b_ref, m_ref, v_ref, g_ref, d_ref, mo_ref, vo_ref):
        gv = jnp.where(pl.program_id(0) // nb == c_ref[0], a_ref[...], b_ref[...])
        g_ref[...] = gv
        d_ref[...], mo_ref[...], vo_ref[...] = _adam_math(w_ref[...], gv, m_ref[...], v_ref[...])

    row = pl.BlockSpec((tr, Cs), lambda i, c: (i, 0))
    hrow = pl.BlockSpec((tr, Cs), lambda i, c: (i % nb, 0))
    grid_spec = pltpu.PrefetchScalarGridSpec(num_scalar_prefetch=1, grid=(2 * nb,),
                                             in_specs=[row, hrow, hrow, row, row], out_specs=[row] * 4)
    return pl.pallas_call(body, name=name, grid_spec=grid_spec, out_shape=[jax.ShapeDtypeStruct((Rs, Cs), F32)] * 4,
                          compiler_params=_cparams("parallel"))(cvec, w, mine, other, m, v)


def _pack(arrays):
    flat = [a.reshape(-1).astype(F32) for a in arrays]
    meta, off = [], 0
    for a, f in zip(arrays, flat):
        meta.append((off, a.shape))
        off += f.shape[0]
    total = -(-off // (8 * LANES)) * (8 * LANES)
    flat.append(jnp.zeros((total - off,), F32))
    return jnp.concatenate(flat).reshape(total // LANES, LANES), meta


def _unpack(buf, meta):
    flat = buf.reshape(-1)
    out = []
    for off, shape in meta:
        size = 1
        for s in shape:
            size *= s
        out.append(flat[off:off + size].reshape(shape))
    return out


WEIGHT_NAMES = ["c_ctx", "w_mod", "b_mod", "g_mix", "w_in", "q_norm", "k_norm", "attn_sink", "w_gate_f", "b_gate_f",
                "w_gate_b", "b_gate_b", "gla_norm", "w_attn_o", "w_gla_o", "w_out", "g_ffn", "w_up", "conv_w",
                "conv_b", "w_down"]
BIG_NAMES = ["w_in", "w_attn_o", "w_gla_o", "w_out", "w_up", "w_down"]
SHARDED_SMALL = ["w_gate_f", "w_gate_b", "conv_w"]


def _layouts(D):
    aw, kvw, gk, gv = N_Q_HEADS * HEAD_DIM, N_KV_HEADS * HEAD_DIM, D // 2, D
    widths = {"qa": aw, "ka": kvw, "va": kvw, "qb": gk, "kb": gk, "vb": gv, "rb": gv, "lr": 2 * GLA_LOWRANK,
              "ga": D, "gb": D}
    orig, off = {}, 0
    for s in ["qa", "ka", "va", "qb", "kb", "vb", "rb", "lr", "ga", "gb"]:
        orig[s] = off
        off += widths[s]
    order = ["qa", "vb", "rb", "ga", "gb", "ka", "va", "qb", "kb", "lr"]
    lay, off = {}, 0
    for s in order:
        lay[s] = off
        off += LANES if s == "lr" else widths[s]
    align = {"qa": aw, "vb": D, "rb": D, "ga": D, "gb": D, "ka": kvw, "va": kvw, "qb": gk, "kb": gk,
             "lr": LANES}
    for s in order:
        assert lay[s] % align[s] == 0, (s, lay[s], align[s])
    return widths, orig, order, lay, off, -(-off // (2 * MXU_TILE)) * (2 * MXU_TILE)


def _rope_tables(T, L):
    t = jnp.arange(T)
    nf = HEAD_DIM // 4
    inv = ROPE_THETA ** (-jnp.arange(nf, dtype=F32) / nf)
    ang = jnp.concatenate([(t // GRID_W)[:, None] * inv, (t % GRID_W)[:, None] * inv], axis=-1)
    cos, sin = jnp.cos(ang), jnp.sin(ang)
    cos2 = jnp.concatenate([jnp.ones((L, HEAD_DIM), F32), jnp.concatenate([cos, cos], axis=-1)], axis=0)
    sin2 = jnp.concatenate([jnp.zeros((L, HEAD_DIM), F32), jnp.concatenate([-sin, sin], axis=-1)], axis=0)
    return cos2, sin2


def _step(x, c, ctx, loss_target, W, M, V):
    xi, yi, ci = lax.axis_index("x"), lax.axis_index("y"), lax.axis_index("c")
    chip = 2 * xi + yi
    dev = 2 * chip + ci
    south = (ci == 0).astype(F32)
    cvec = ci.reshape(1).astype(jnp.int32)
    svec = chip.reshape(1).astype(jnp.int32)
    T, D = x.shape[1], x.shape[2]
    L = ctx.shape[1]
    R = L + T
    F = 4 * W["w_down"].shape[1]
    GK, GV = D // 2, D
    DK, DV = GK // GLA_HEADS, GV // GLA_HEADS
    N6 = 6 * D
    N4 = N6 // 4
    widths, orig, order, lay, z_used, Z = _layouts(D)

    def place_cols(shard, full_cols):
        cols = shard.shape[-1]
        full = jnp.zeros(shard.shape[:-1] + (full_cols,), F32)
        return lax.dynamic_update_slice(full, shard * south, (0,) * (shard.ndim - 1) + (chip * cols,))

    c_rows = lax.dynamic_update_slice(jnp.zeros((8, D), F32), c, (dev, 0))
    bufa, meta = _pack([c_rows, place_cols(W["w_gate_f"][0], GK), place_cols(W["w_gate_b"][0], GK),
                        place_cols(W["conv_w"][0], 2 * F)])
    c_all, wgf, wgb, cw = _unpack(_allreduce(bufa, "gather_small"), meta)
    ca = jnp.concatenate([c_all, W["c_ctx"][None, :], jnp.zeros((7, D), F32)], axis=0)
    b_shard = lax.dynamic_slice(W["b_mod"], (0, chip * N4), (1, N4))
    mod_part, sil = _mod_fwd(ca, W["w_mod"][0], b_shard, "mod_fwd")
    slots = lax.dynamic_update_slice(jnp.zeros((4, 16, N4), F32), (mod_part * south)[None], (chip, 0, 0))
    mod_all = _allreduce(slots.reshape(64, N4), "gather_mod").reshape(4, 16, N4).transpose(1, 0, 2).reshape(16, N6)
    mx = lax.dynamic_slice(mod_all, (dev, 0), (1, N6)).reshape(6, 1, D)
    mc = mod_all[8].reshape(6, 1, D)

    sq = lambda a: a.reshape(a.shape[1:])
    shards = [sq(W[n]).astype(BF16) for n in BIG_NAMES]
    own = lambda g, s, n: _place_own(g, s, svec, "place_" + n)
    cols = lambda g: g.transpose(1, 0, 2).reshape(g.shape[1], 4 * g.shape[2])
    rows = lambda g: g.reshape(4 * g.shape[1], g.shape[2])
    w_in_f = cols(own(_allgather_weights(shards[:1], "gather_w_in")[0], shards[0], "w_in"))
    seg = lambda s: w_in_f[:, orig[s]:orig[s] + widths[s]]
    w_cat = jnp.concatenate([jnp.pad(seg(s), ((0, 0), (0, LANES - widths[s]))) if s == "lr" else seg(s)
                             for s in order] + [jnp.zeros((D, Z - z_used), BF16)], axis=1)
    gather1, gather2, gather_outs = _gather_plan(shards[1:], "in")
    wg = jnp.zeros((2, LANES, GK), F32).at[0, :GLA_LOWRANK].set(wgf).at[1, GLA_LOWRANK:2 * GLA_LOWRANK].set(wgb)
    bg = jnp.stack([W["b_gate_f"], W["b_gate_b"]])
    cb = W["conv_b"]
    sink_rows = jnp.broadcast_to(W["attn_sink"][0][:, None], (N_Q_HEADS, HEAD_DIM))
    cos2, sin2 = _rope_tables(T, L)
    blk = lambda s, w: lay[s] // w

    xall = jnp.concatenate([ctx[0], x[0]], axis=0)
    sc1 = jnp.stack([mc[1], mx[1]])
    sh1 = jnp.stack([mc[0], mx[0]])
    h = _modnorm_fwd(xall, W["g_mix"], sc1, sh1, L, "modnorm1")
    z, landed = _matmul(h, w_cat, "nn", F32, "proj_in", tn=1536, ride=(shards[1:], gather_outs, gather1, {}))
    qn = _qknorm_fwd(z, blk("qa", widths["qa"]), T, L, W["q_norm"], cos2, sin2, N_Q_HEADS, "qnorm")
    kn = _qknorm_fwd(z, blk("ka", widths["ka"]), R, 0, W["k_norm"], cos2, sin2, N_KV_HEADS, "knorm")
    vb = _cast_seg(z, blk("va", widths["va"]), widths["va"], "vcast")
    o_attn, landed = _attn_fwd(qn, kn, vb, sink_rows, L, "attn_fwd",
                               ride=(landed, gather_outs, gather2, {n: n for n in range(len(landed))}))
    g_ao, g_go, g_out, g_up, g_dn = [own(g, s, n) for g, s, n in zip(landed, shards[1:], BIG_NAMES[1:])]
    w_ao, w_go, w_out, w_up, w_dn = rows(g_ao), rows(g_go), rows(g_out), cols(g_up), rows(g_dn)
    gla_blks = (blk("qb", GK), blk("kb", GK), blk("vb", GV), blk("lr", LANES))
    o_g, sprev = _gla_fwd(z, *gla_blks, wg, bg, DV, L, "gla_fwd")
    p = _glanorm_fwd(o_g, z, blk("rb", D), W["gla_norm"], L, "glanorm")
    ya = _matmul(o_attn, w_ao, "nn", F32, "proj_attn_o")
    yg = _matmul(p, w_go, "nn", F32, "proj_gla_o")
    m = _gate_fwd(z, blk("ga", D), blk("gb", D), ya, yg, L, "gate")
    mix = _matmul(m, w_out, "nn", F32, "proj_out")
    x1, h2 = _resnorm_fwd(x[0], mix, mx[2], W["g_ffn"], mx[4], mx[3], "resnorm2")
    u = _matmul(h2, w_up, "nn", F32, "ffn_up")
    f = _conv_fwd(u, cw, cb, "conv_swiglu")
    d = _matmul(f, w_dn, "nn", F32, "ffn_down", tk=2816)
    dy, dd, lacc = _loss_head(d, x1, mx[5], loss_target[0], "loss_head")
    loss = lax.psum((0.5 / D) * jnp.sum(lacc[0]), ("x", "y", "c"))

    gw_dn = _matmul(f, dd, "tn", F32, "ffn_down_dw")
    df = _matmul(dd, w_dn, "nt", F32, "ffn_down_dx")
    du, acca, accg = _conv_bwd(u, df, cw, cb, "conv_swiglu_bwd")
    gw_up = _matmul(h2, du, "tn", F32, "ffn_up_dw", tm=512, tn=2816, halves="b")
    dh2 = _matmul(du, w_up, "nt", F32, "ffn_up_dx", tk=2816, halves="a")
    dx1, dmix, s2 = _resnorm_bwd(x1, dh2, W["g_ffn"], mx[4], dy, mix, mx[2], "resnorm2_bwd")
    gw_out = _matmul(m, dmix, "tn", F32, "proj_out_dw")
    dm = _matmul(dmix, w_out, "nt", F32, "proj_out_dx")
    dya, dyg, dga, dgb = _gate_bwd(z, blk("ga", D), blk("gb", D), ya, yg, dm, L, "gate_bwd")
    gw_ao = _matmul(o_attn, dya, "tn", F32, "proj_attn_o_dw")
    do_attn = _matmul(dya, w_ao, "nt", BF16, "proj_attn_o_dx")
    gw_go = _matmul(p, dyg, "tn", F32, "proj_gla_o_dw")
    dp = _matmul(dyg, w_go, "nt", F32, "proj_gla_o_dx")
    do_gla, drb, s_gn = _glanorm_bwd(o_g, z, blk("rb", D), W["gla_norm"], dp, L, "glanorm_bwd")
    do_pad = jnp.concatenate([jnp.zeros((L, GV), BF16), do_gla], axis=0)
    by_cols = lambda g: g.reshape(g.shape[0], 4, g.shape[1] // 4).transpose(1, 0, 2)
    by_rows = lambda g: g.reshape(4, g.shape[0] // 4, g.shape[1])
    early = [by_rows(gw_ao), by_rows(gw_go), by_rows(gw_out), by_cols(gw_up), by_rows(gw_dn)]
    (dqg, dkg, dvg, dpre, dbg), pair_e = _gla_bwd(z, *gla_blks, wg, bg, sprev, do_pad, L, "gla_bwd",
                                                  ride=(early, *_pair_plan(early), {}))
    sums_e = [_add_pair(g, a, cvec, "reduce_early_add%d" % n) for n, (g, a) in enumerate(zip(early, pair_e))]
    wg_cat = jnp.concatenate([wg[0], wg[1]], axis=1)
    dlr = _matmul(dpre, wg_cat, "nt", BF16, "gla_gate_dx")
    dwg = _matmul(z[:, lay["lr"]:lay["lr"] + LANES], dpre, "tn", F32, "gla_gate_dw")
    (dqn, dkw, dvw, dkc, dvc, dsn), chips_e = _attn_bwd(qn, kn, vb, sink_rows, do_attn, L, "attn_bwd",
                                                        ride=(sums_e, *_chips_plan(sums_e), {}))
    mine_e = [_sum_chips(g, a, b, cvec, svec, "reduce_early_sum%d" % n)
              for n, (g, a, b) in enumerate(zip(early, pair_e, chips_e))]
    dqa, s_qn = _qknorm_bwd(z, blk("qa", widths["qa"]), T, L, W["q_norm"], cos2, sin2, dqn, N_Q_HEADS, "qnorm_bwd")
    dk_all = jnp.concatenate([dkc, dkw[WINDOW:WINDOW + T]], axis=0)
    dv_all = jnp.concatenate([dvc, dvw[WINDOW:WINDOW + T]], axis=0)
    dka, s_kn = _qknorm_bwd(z, blk("ka", widths["ka"]), R, 0, W["k_norm"], cos2, sin2, dk_all, N_KV_HEADS, "knorm_bwd")
    dz = _assemble_dz(lay, z_used, Z, L, dqa, drb, dga, dgb, dka, dv_all, dvg, dqg, dkg, dlr, "assemble_dz")
    gw_cat, other_e = _matmul(h, dz, "tn", F32, "proj_in_dw", tn=768, tk=2816,
                              ride=(mine_e, *_halves_plan(mine_e), {}))
    gw_in = jnp.concatenate([gw_cat[:, lay[s]:lay[s] + widths[s]] for s in ["qa", "ka", "va", "qb", "kb", "vb", "rb",
                                                                           "lr", "ga", "gb"]], axis=1)
    late = [by_cols(gw_in)]
    shapes, stage = _pair_plan(late)
    pair_l = _exchange(late, shapes, [stage], "reduce_late_pair")
    sums_l = [_add_pair(late[0], pair_l[0], cvec, "reduce_late_add")]
    dh, chips_l = _matmul(dz, w_cat, "nt", F32, "proj_in_dx", tk=4608, ride=(sums_l, *_chips_plan(sums_l), {}))
    mine_l = [_sum_chips(late[0], pair_l[0], chips_l[0], cvec, svec, "reduce_late_sum")]
    shapes, stage = _halves_plan(mine_l)
    other_l = _exchange(mine_l, shapes, [stage], "reduce_late_halves")
    mine, other = mine_l + mine_e, list(other_l) + other_e
    grad_x, s1 = _modnorm_bwd(x[0], dh, W["g_mix"], mx[1], dx1, "modnorm1_bwd", dh_roff=L)
    _, s1c = _modnorm_bwd(ctx[0], dh, W["g_mix"], mc[1], None, "modnorm1_ctx_bwd")

    dmod_x = jnp.concatenate([s1[0], s1[1], s2[3], s2[0], s2[1], lacc[1]])
    dmod_c = jnp.concatenate([s1c[0], s1c[1], jnp.zeros((4 * D,), F32)])
    dmod_rows = lax.dynamic_update_slice(jnp.zeros((9, N6), F32).at[8].set(dmod_c), dmod_x[None], (dev, 0))
    small = [dmod_rows, dmod_x + dmod_c, s1[2] + s1c[2], s_qn[0], s_kn[0], dsn[:, 0, :Q_PER_KV].reshape(N_Q_HEADS),
             dwg[:GLA_LOWRANK, :GK], dbg[0].reshape(GK), dwg[GLA_LOWRANK:2 * GLA_LOWRANK, GK:], dbg[1].reshape(GK),
             s_gn[0], s2[2], jnp.concatenate([acca[0:3], accg[0:3]], axis=1), jnp.concatenate([acca[3], accg[3]])]
    bufc, meta = _pack(small)
    (dmod_sum, g_b_mod, g_g_mix, g_q_norm, g_k_norm, g_sink, g_wgf, g_bgf, g_wgb, g_bgb, g_gla_norm, g_g_ffn,
     g_conv_w, g_conv_b) = _unpack(_allreduce(bufc, "reduce_small"), meta)
    dmod16 = lax.dynamic_slice(jnp.concatenate([dmod_sum, jnp.zeros((7, N6), F32)], axis=0), (0, chip * N4), (16, N4))
    g_w_mod = _matmul(sil, dmod16, "tn", F32, "mod_dw")
    dsil = _matmul(dmod16, W["w_mod"][0], "nt", F32, "mod_dx")
    g_c_ctx = _silu_bwd(_allreduce(dsil * south, "reduce_cctx"), ca, "silu_bwd")[8]

    cut = lambda g: lax.dynamic_slice(g, (0, chip * (g.shape[1] // 4)), (g.shape[0], g.shape[1] // 4))
    grads = {"c_ctx": g_c_ctx, "w_mod": g_w_mod[None], "b_mod": g_b_mod[None], "g_mix": g_g_mix[None],
             "q_norm": g_q_norm[None], "k_norm": g_k_norm[None], "attn_sink": g_sink[None],
             "w_gate_f": cut(g_wgf)[None], "b_gate_f": g_bgf[None], "w_gate_b": cut(g_wgb)[None],
             "b_gate_b": g_bgb[None], "gla_norm": g_gla_norm[None], "g_ffn": g_g_ffn[None],
             "conv_w": cut(g_conv_w)[None], "conv_b": g_conv_b[None]}

    delta, new_m, new_v = {}, {}, {}
    dl, mn, vn = _adamw(W["w_mod"][0], g_w_mod, M["w_mod"][0], V["w_mod"][0], "adamw_w_mod")
    delta["w_mod"], new_m["w_mod"], new_v["w_mod"] = dl[None], mn[None], vn[None]
    for n, a, b in zip(BIG_NAMES, mine, other):
        g, dl, mn, vn = _adamw_halves(sq(W[n]), a, b, sq(M[n]), sq(V[n]), cvec, "adamw_" + n)
        grads[n], delta[n], new_m[n], new_v[n] = g[None], dl[None], mn[None], vn[None]
    small_names = [n for n in WEIGHT_NAMES if n not in delta]
    packs = [_pack([src[n] for n in small_names]) for src in (W, grads, M, V)]
    meta = packs[0][1]
    outs = _adamw(packs[0][0], packs[1][0], packs[2][0], packs[3][0], "adamw_small")
    for res, o in zip((delta, new_m, new_v), outs):
        for n, a in zip(small_names, _unpack(o, meta)):
            res[n] = a
    return (loss, grad_x[None], *[grads[n] for n in WEIGHT_NAMES], *[delta[n] for n in WEIGHT_NAMES],
            *[new_m[n] for n in WEIGHT_NAMES], *[new_v[n] for n in WEIGHT_NAMES])


def kernel(x, c, ctx, c_ctx, w_mod, b_mod, g_mix, w_in, q_norm, k_norm, attn_sink, w_gate_f, b_gate_f, w_gate_b, b_gate_b, gla_norm, w_attn_o, w_gla_o, w_out, g_ffn, w_up, conv_w, conv_b, w_down, loss_target, m_c_ctx, m_w_mod, m_b_mod, m_g_mix, m_w_in, m_q_norm, m_k_norm, m_attn_sink, m_w_gate_f, m_b_gate_f, m_w_gate_b, m_b_gate_b, m_gla_norm, m_w_attn_o, m_w_gla_o, m_w_out, m_g_ffn, m_w_up, m_conv_w, m_conv_b, m_w_down, v_c_ctx, v_w_mod, v_b_mod, v_g_mix, v_w_in, v_q_norm, v_k_norm, v_attn_sink, v_w_gate_f, v_b_gate_f, v_w_gate_b, v_b_gate_b, v_gla_norm, v_w_attn_o, v_w_gla_o, v_w_out, v_g_ffn, v_w_up, v_conv_w, v_conv_b, v_w_down):
    W = dict(zip(WEIGHT_NAMES, (c_ctx, w_mod, b_mod, g_mix, w_in, q_norm, k_norm, attn_sink, w_gate_f, b_gate_f,
                                w_gate_b, b_gate_b, gla_norm, w_attn_o, w_gla_o, w_out, g_ffn, w_up, conv_w, conv_b,
                                w_down)))
    M = dict(zip(WEIGHT_NAMES, (m_c_ctx, m_w_mod, m_b_mod, m_g_mix, m_w_in, m_q_norm, m_k_norm, m_attn_sink,
                                m_w_gate_f, m_b_gate_f, m_w_gate_b, m_b_gate_b, m_gla_norm, m_w_attn_o, m_w_gla_o,
                                m_w_out, m_g_ffn, m_w_up, m_conv_w, m_conv_b, m_w_down)))
    V = dict(zip(WEIGHT_NAMES, (v_c_ctx, v_w_mod, v_b_mod, v_g_mix, v_w_in, v_q_norm, v_k_norm, v_attn_sink,
                                v_w_gate_f, v_b_gate_f, v_w_gate_b, v_b_gate_b, v_gla_norm, v_w_attn_o, v_w_gla_o,
                                v_w_out, v_g_ffn, v_w_up, v_conv_w, v_conv_b, v_w_down)))
    return _step(x, c, ctx, loss_target, W, M, V)
```

```python
import functools
import math

import jax
import jax.numpy as jnp
from jax import lax
from jax.experimental import pallas as pl
from jax.experimental.pallas import tpu as pltpu

F32 = jnp.float32
BF16 = jnp.bfloat16
MESH = pl.DeviceIdType.MESH

EPS = 1e-6
HEAD_DIM = 128
N_Q_HEADS = 16
N_KV_HEADS = 4
Q_PER_KV = N_Q_HEADS // N_KV_HEADS
WINDOW = 128
GLA_HEADS = 4
GLA_LOWRANK = 16
GLA_GATE_NORM = 16.0
GLA_CHUNK = 64
GRID_W = 64
ROPE_THETA = 10000.0
GLA_LEVELS = (32, 16, 8, 4, 2, 1)
LANES = 128
MXU_TILE = 256

ADAM_LR = 0.001
ADAM_B1 = 0.9
ADAM_B2 = 0.999
ADAM_EPS = 1e-08
ADAM_WD = 0.01
ADAM_STEP = 10

VMEM_LIMIT = 52 * 1024 * 1024


def _cparams(*sem):
    return pltpu.CompilerParams(dimension_semantics=sem, vmem_limit_bytes=VMEM_LIMIT)


def _pick(n, target, mult=LANES):
    best = None
    d = mult
    while d <= min(n, target):
        if n % d == 0:
            best = d
        d += mult
    return n if best is None else best


def _sigmoid(x):
    return 1.0 / (1.0 + jnp.exp(-x))


def _silu(x):
    return x * _sigmoid(x)


def _dsilu(x):
    s = _sigmoid(x)
    return s * (1.0 + x * (1.0 - s))


def _dot(a, b, dims):
    return lax.dot_general(a, b, (dims, ((), ())), preferred_element_type=F32)


NN = ((1,), (0,))
NT = ((1,), (1,))
TN = ((0,), (0,))


def _matmul(a, b, mode, out_dtype, name, tm=1024, tn=1024, tk=2048, ride=None, halves=None, col_shards=None):
    if halves == "a":
        assert mode == "nt"
        (_, M, Kh), (N, K2) = a.shape, b.shape
        K = 2 * Kh
    elif halves == "b":
        assert mode == "tn"
        (K, M), (_, K2, Nh) = a.shape, b.shape
        N = 2 * Nh
    elif mode == "nn":
        (M, K), (K2, N) = a.shape, b.shape
    elif mode == "nt":
        (M, K), (N, K2) = a.shape, b.shape
    else:
        (K, M), (K2, N) = a.shape, b.shape
    assert K == K2, (name, a.shape, b.shape)
    pick = lambda n, t: _pick(n, t, MXU_TILE) if n % MXU_TILE == 0 else _pick(n, t)
    tm, tn, tk = pick(M, tm), pick(N // 2 if halves == "b" else N, tn), pick(K // 2 if halves == "a" else K, tk)
    if col_shards is not None:
        tn = N // col_shards
    nk = K // tk
    dims = {"nn": NN, "nt": NT, "tn": TN}[mode]

    def body(a_ref, b_ref, o_ref, acc_ref):
        k = pl.program_id(2)

        @pl.when(k == 0)
        def _():
            acc_ref[...] = jnp.zeros_like(acc_ref)

        av = a_ref[0] if halves == "a" else a_ref[...]
        bv = b_ref[0] if halves == "b" else b_ref[...]
        acc_ref[...] += _dot(av.astype(BF16), bv.astype(BF16), dims)

        @pl.when(k == nk - 1)
        def _():
            o_ref[...] = acc_ref[...].astype(out_dtype).reshape(o_ref.shape)

    if halves == "a":
        per = (K // 2) // tk
        a_spec = pl.BlockSpec((1, tm, tk), lambda i, j, k: (k // per, i, k % per))
    elif mode == "tn":
        a_spec = pl.BlockSpec((tk, tm), lambda i, j, k: (k, i))
    else:
        a_spec = pl.BlockSpec((tm, tk), lambda i, j, k: (i, k))
    if halves == "b":
        per = (N // 2) // tn
        b_spec = pl.BlockSpec((1, tk, tn), lambda i, j, k: (j // per, k, j % per))
    elif mode == "nt":
        b_spec = pl.BlockSpec((tn, tk), lambda i, j, k: (j, k))
    else:
        b_spec = pl.BlockSpec((tk, tn), lambda i, j, k: (k, j))
    if col_shards is None:
        out_spec, out_shape = pl.BlockSpec((tm, tn), lambda i, j, k: (i, j)), (M, N)
    else:
        assert tn * col_shards == N, (name, tn, N)
        out_spec, out_shape = pl.BlockSpec((1, tm, tn), lambda i, j, k: (j, i, 0)), (col_shards, M, tn)
    return _pcall(
        body, name=name, grid=(M // tm, N // tn, nk),
        in_specs=[a_spec, b_spec],
        out_specs=out_spec,
        out_shape=jax.ShapeDtypeStruct(out_shape, out_dtype),
        scratch_shapes=[pltpu.VMEM((tm, tn), F32)],
        sem=("parallel", "parallel", "arbitrary"), args=(a, b), ride=ride)


def _modnorm_fwd(xc, xl, g, sc, sh, name):
    (L, D), T = xc.shape, xl.shape[0]
    tm = _pick(math.gcd(L, T), 256, 8)
    cb = L // tm

    def body(xc_ref, xl_ref, g_ref, sc_ref, sh_ref, h_ref):
        x = jnp.where(pl.program_id(0) < cb, xc_ref[...], xl_ref[...])
        r = lax.rsqrt(jnp.mean(x * x, axis=-1, keepdims=True) + EPS)
        n = x * r * g_ref[...]
        h_ref[...] = (n * (1.0 + sc_ref[0]) + sh_ref[0]).astype(BF16)

    sel = lambda i: (jnp.where(i < cb, 0, 1), 0, 0)
    return pl.pallas_call(
        body, name=name, grid=((L + T) // tm,),
        in_specs=[pl.BlockSpec((tm, D), lambda i: (jnp.minimum(i, cb - 1), 0)),
                  pl.BlockSpec((tm, D), lambda i: (jnp.maximum(i - cb, 0), 0)),
                  pl.BlockSpec((1, D), lambda i: (0, 0)), pl.BlockSpec((1, 1, D), sel), pl.BlockSpec((1, 1, D), sel)],
        out_specs=pl.BlockSpec((tm, D), lambda i: (i, 0)),
        out_shape=jax.ShapeDtypeStruct((L + T, D), BF16),
        compiler_params=_cparams("parallel"),
    )(xc, xl, g, sc, sh)


def _modnorm_bwd(x, dh, g, sc, resid, name, dh_roff=0):
    N, D = x.shape
    tm = _pick(math.gcd(N, dh_roff), 256, 8)
    ro = dh_roff // tm
    want_dx = resid is not None

    def body(*refs):
        if want_dx:
            x_ref, dh_ref, g_ref, sc_ref, res_ref, dx_ref, acc_ref = refs
        else:
            x_ref, dh_ref, g_ref, sc_ref, acc_ref = refs
        i = pl.program_id(0)

        @pl.when(i == 0)
        def _():
            acc_ref[...] = jnp.zeros_like(acc_ref)

        xv, dhv, gv = x_ref[...], dh_ref[...], g_ref[...]
        r = lax.rsqrt(jnp.mean(xv * xv, axis=-1, keepdims=True) + EPS)
        xh = xv * r
        dn = dhv * (1.0 + sc_ref[...])
        acc_ref[0:1, :] += jnp.sum(dhv, axis=0, keepdims=True)
        acc_ref[1:2, :] += jnp.sum(dhv * xh * gv, axis=0, keepdims=True)
        acc_ref[2:3, :] += jnp.sum(dn * xh, axis=0, keepdims=True)
        if want_dx:
            dxh = dn * gv
            dx_ref[...] = res_ref[...] + r * (dxh - xh * jnp.mean(dxh * xh, axis=-1, keepdims=True))

    row = pl.BlockSpec((tm, D), lambda i: (i, 0))
    drow = pl.BlockSpec((tm, D), lambda i: (i + ro, 0))
    vec = pl.BlockSpec((1, D), lambda i: (0, 0))
    acc = pl.BlockSpec((8, D), lambda i: (0, 0))
    acc_shape = jax.ShapeDtypeStruct((8, D), F32)
    if want_dx:
        return pl.pallas_call(
            body, name=name, grid=(N // tm,), in_specs=[row, drow, vec, vec, row],
            out_specs=[row, acc], out_shape=[jax.ShapeDtypeStruct((N, D), F32), acc_shape],
            compiler_params=_cparams("arbitrary"))(x, dh, g, sc, resid)
    sums = pl.pallas_call(
        body, name=name, grid=(N // tm,), in_specs=[row, drow, vec, vec],
        out_specs=acc, out_shape=acc_shape, compiler_params=_cparams("arbitrary"))(x, dh, g, sc)
    return None, sums


def _resnorm_bwd(x1, dh, g, sc, dy, mix, gt, name):
    N, D = x1.shape
    tm = _pick(N, 256, 8)

    def body(x_ref, dh_ref, g_ref, sc_ref, dy_ref, mix_ref, gt_ref, dx_ref, dm_ref, acc_ref):
        i = pl.program_id(0)

        @pl.when(i == 0)
        def _():
            acc_ref[...] = jnp.zeros_like(acc_ref)

        xv, dhv, gv = x_ref[...], dh_ref[...], g_ref[...]
        r = lax.rsqrt(jnp.mean(xv * xv, axis=-1, keepdims=True) + EPS)
        xh = xv * r
        dn = dhv * (1.0 + sc_ref[...])
        dxh = dn * gv
        dx = dy_ref[...] + r * (dxh - xh * jnp.mean(dxh * xh, axis=-1, keepdims=True))
        dx_ref[...] = dx
        dm_ref[...] = (dx * gt_ref[...]).astype(BF16)
        acc_ref[0:1, :] += jnp.sum(dhv, axis=0, keepdims=True)
        acc_ref[1:2, :] += jnp.sum(dhv * xh * gv, axis=0, keepdims=True)
        acc_ref[2:3, :] += jnp.sum(dn * xh, axis=0, keepdims=True)
        acc_ref[3:4, :] += jnp.sum(dx * mix_ref[...], axis=0, keepdims=True)

    row = pl.BlockSpec((tm, D), lambda i: (i, 0))
    vec = pl.BlockSpec((1, D), lambda i: (0, 0))
    return pl.pallas_call(
        body, name=name, grid=(N // tm,), in_specs=[row, row, vec, vec, row, row, vec],
        out_specs=[row, row, pl.BlockSpec((8, D), lambda i: (0, 0))],
        out_shape=[jax.ShapeDtypeStruct((N, D), F32), jax.ShapeDtypeStruct((N, D), BF16),
                   jax.ShapeDtypeStruct((8, D), F32)],
        compiler_params=_cparams("arbitrary"))(x1, dh, g, sc, dy, mix, gt)


def _qknorm_fwd(z, cblk, nrows, roff, w, cos2, sin2, nh, name):
    W = nh * HEAD_DIM
    tm = _pick(math.gcd(nrows, roff), 256, 8)
    ro = roff // tm
    assert roff % tm == 0

    def body(z_ref, w_ref, c_ref, s_ref, o_ref):
        c, s, wv = c_ref[...], s_ref[...], w_ref[...]
        for h in range(nh):
            x = z_ref[:, h * HEAD_DIM:(h + 1) * HEAD_DIM]
            r = lax.rsqrt(jnp.mean(x * x, axis=-1, keepdims=True) + EPS)
            y = x * r * wv
            o_ref[:, h * HEAD_DIM:(h + 1) * HEAD_DIM] = (y * c + pltpu.roll(y, HEAD_DIM // 2, 1) * s).astype(BF16)

    return pl.pallas_call(
        body, name=name, grid=(nrows // tm,),
        in_specs=[pl.BlockSpec((tm, W), lambda i: (i + ro, cblk)), pl.BlockSpec((1, HEAD_DIM), lambda i: (0, 0)),
                  pl.BlockSpec((tm, HEAD_DIM), lambda i: (i + ro, 0)), pl.BlockSpec((tm, HEAD_DIM), lambda i: (i + ro, 0))],
        out_specs=pl.BlockSpec((tm, W), lambda i: (i, 0)),
        out_shape=jax.ShapeDtypeStruct((nrows, W), BF16),
        compiler_params=_cparams("parallel"),
    )(z, w, cos2, sin2)


def _qknorm_bwd(z, cblk, nrows, roff, w, cos2, sin2, dy, nh, name):
    W = nh * HEAD_DIM
    tm = _pick(math.gcd(nrows, roff), 256, 8)
    ro = roff // tm

    def body(z_ref, w_ref, c_ref, s_ref, dy_ref, dz_ref, acc_ref):
        i = pl.program_id(0)

        @pl.when(i == 0)
        def _():
            acc_ref[...] = jnp.zeros_like(acc_ref)

        c, s, wv = c_ref[...], s_ref[...], w_ref[...]
        dw = jnp.zeros((1, HEAD_DIM), F32)
        for h in range(nh):
            sl = slice(h * HEAD_DIM, (h + 1) * HEAD_DIM)
            x = z_ref[:, sl]
            d = dy_ref[:, sl]
            dyn = d * c + pltpu.roll(d * s, HEAD_DIM // 2, 1)
            r = lax.rsqrt(jnp.mean(x * x, axis=-1, keepdims=True) + EPS)
            xh = x * r
            dw = dw + jnp.sum(dyn * xh, axis=0, keepdims=True)
            dxh = dyn * wv
            dz_ref[:, sl] = (r * (dxh - xh * jnp.mean(dxh * xh, axis=-1, keepdims=True))).astype(BF16)
        acc_ref[0:1, :] += dw

    return pl.pallas_call(
        body, name=name, grid=(nrows // tm,),
        in_specs=[pl.BlockSpec((tm, W), lambda i: (i + ro, cblk)), pl.BlockSpec((1, HEAD_DIM), lambda i: (0, 0)),
                  pl.BlockSpec((tm, HEAD_DIM), lambda i: (i + ro, 0)), pl.BlockSpec((tm, HEAD_DIM), lambda i: (i + ro, 0)),
                  pl.BlockSpec((tm, W), lambda i: (i, 0))],
        out_specs=[pl.BlockSpec((tm, W), lambda i: (i, 0)), pl.BlockSpec((8, HEAD_DIM), lambda i: (0, 0))],
        out_shape=[jax.ShapeDtypeStruct((nrows, W), BF16), jax.ShapeDtypeStruct((8, HEAD_DIM), F32)],
        compiler_params=_cparams("arbitrary"),
    )(z, w, cos2, sin2, dy)


def _cast_seg(z, cblk, width, name):
    R = z.shape[0]
    tm = _pick(R, 512, 8)

    def body(z_ref, o_ref):
        o_ref[...] = z_ref[...].astype(BF16)

    return pl.pallas_call(
        body, name=name, grid=(R // tm,),
        in_specs=[pl.BlockSpec((tm, width), lambda i: (i, cblk))],
        out_specs=pl.BlockSpec((tm, width), lambda i: (i, 0)),
        out_shape=jax.ShapeDtypeStruct((R, width), BF16), compiler_params=_cparams("parallel"))(z)


NEG_BIG = -1e30


KV_PER_STEP = 2


def _attn_specs(T, n_ctx):
    nb = T // WINDOW
    lb = n_ctx // WINDOW
    kvw = KV_PER_STEP * HEAD_DIM
    blk = lambda f: pl.BlockSpec((WINDOW, kvw), f)
    win = [blk(lambda h, i: (lb + jnp.maximum(i - 1, 0), h)), blk(lambda h, i: (lb + i, h)),
           blk(lambda h, i: (lb + jnp.minimum(i + 1, nb - 1), h))]
    ctx = pl.BlockSpec((n_ctx, kvw), lambda h, i: (0, h))
    qspec = pl.BlockSpec((WINDOW, KV_PER_STEP * Q_PER_KV * HEAD_DIM), lambda h, i: (i, h))
    sink = pl.BlockSpec((N_Q_HEADS, HEAD_DIM), lambda h, i: (0, 0))
    return nb, qspec, win, ctx, sink


def _attn_probs(q, kw, kctx, snk, valid):
    scale = HEAD_DIM ** -0.5
    s_lat = jnp.where(valid, _dot(q, kw, NT) * scale, NEG_BIG)
    s_ctx = _dot(q, kctx, NT) * scale
    m = jnp.maximum(jnp.maximum(jnp.max(s_lat, axis=-1, keepdims=True), jnp.max(s_ctx, axis=-1, keepdims=True)), snk)
    p_lat = jnp.exp(s_lat - m)
    p_ctx = jnp.exp(s_ctx - m)
    p_snk = jnp.exp(snk - m)
    den = p_snk + jnp.sum(p_lat, axis=-1, keepdims=True) + jnp.sum(p_ctx, axis=-1, keepdims=True)
    return p_lat, p_ctx, p_snk, den


def _attn_valid(i, T, heads):
    rows = heads * WINDOW
    qpos = i * WINDOW + (lax.broadcasted_iota(jnp.int32, (rows, 3 * WINDOW), 0) & (WINDOW - 1))
    kpos = (i - 1) * WINDOW + lax.broadcasted_iota(jnp.int32, (rows, 3 * WINDOW), 1)
    return (jnp.abs(qpos - kpos) <= WINDOW) & (kpos >= 0) & (kpos < T)


def _stack_heads(ref, hh):
    c0 = hh * Q_PER_KV * HEAD_DIM
    return jnp.concatenate([ref[:, c0 + g * HEAD_DIM:c0 + (g + 1) * HEAD_DIM] for g in range(Q_PER_KV)], axis=0)


def _stack_sinks(sink_ref, kvh):
    return jnp.concatenate([jnp.broadcast_to(sink_ref[pl.ds(kvh * Q_PER_KV + g, 1), :][:, 0:1], (WINDOW, 1))
                            for g in range(Q_PER_KV)], axis=0)


def _attn_window(refs, hh):
    return jnp.concatenate([r[:, hh * HEAD_DIM:(hh + 1) * HEAD_DIM] for r in refs], axis=0)


def _attn_fwd(qn, kn, vb, sink_rows, n_ctx, name, ride=None):
    T = qn.shape[0]
    nb, qspec, win, ctx, sink = _attn_specs(T, n_ctx)

    def body(q_ref, kp, kc, kx, vp, vc, vx, kctx_ref, vctx_ref, sink_ref, o_ref):
        h, i = pl.program_id(0), pl.program_id(1)
        valid = _attn_valid(i, T, Q_PER_KV)
        for hh in range(KV_PER_STEP):
            sl = slice(hh * HEAD_DIM, (hh + 1) * HEAD_DIM)
            kw, vw = _attn_window((kp, kc, kx), hh), _attn_window((vp, vc, vx), hh)
            kctx, vctx = kctx_ref[:, sl], vctx_ref[:, sl]
            p_lat, p_ctx, _, den = _attn_probs(_stack_heads(q_ref, hh), kw, kctx,
                                               _stack_sinks(sink_ref, h * KV_PER_STEP + hh), valid)
            o = ((_dot(p_lat.astype(BF16), vw, NN) + _dot(p_ctx.astype(BF16), vctx, NN)) / den).astype(BF16)
            for g in range(Q_PER_KV):
                c0 = (hh * Q_PER_KV + g) * HEAD_DIM
                o_ref[:, c0:c0 + HEAD_DIM] = o[g * WINDOW:(g + 1) * WINDOW]

    return _pcall(
        body, name=name, grid=(N_KV_HEADS // KV_PER_STEP, nb),
        in_specs=[qspec] + win + win + [ctx, ctx, sink],
        out_specs=qspec, out_shape=jax.ShapeDtypeStruct(qn.shape, BF16),
        sem=("parallel", "parallel"), args=(qn, kn, kn, kn, vb, vb, vb, kn, vb, sink_rows), ride=ride)


def _attn_bwd(qn, kn, vb, sink_rows, do, n_ctx, name, ride=None):
    T = qn.shape[0]
    nb, qspec, win, ctx, sink = _attn_specs(T, n_ctx)
    scale = HEAD_DIM ** -0.5
    TP = T + 2 * WINDOW

    def body(q_ref, kp, kc, kx, vp, vc, vx, kctx_ref, vctx_ref, sink_ref, do_ref,
             dq_ref, dkw_ref, dvw_ref, dkc_ref, dvc_ref, dsn_ref):
        h, i = pl.program_id(0), pl.program_id(1)

        @pl.when(i == 0)
        def _():
            dkw_ref[...] = jnp.zeros_like(dkw_ref)
            dvw_ref[...] = jnp.zeros_like(dvw_ref)
            dkc_ref[...] = jnp.zeros_like(dkc_ref)
            dvc_ref[...] = jnp.zeros_like(dvc_ref)
            dsn_ref[...] = jnp.zeros_like(dsn_ref)

        lane = lax.broadcasted_iota(jnp.int32, (8, HEAD_DIM), 1)
        valid = _attn_valid(i, T, Q_PER_KV)
        rows = pl.ds(pl.multiple_of(i * WINDOW, WINDOW), 3 * WINDOW)
        for hh in range(KV_PER_STEP):
            sl = slice(hh * HEAD_DIM, (hh + 1) * HEAD_DIM)
            kw, vw = _attn_window((kp, kc, kx), hh), _attn_window((vp, vc, vx), hh)
            kctx, vctx = kctx_ref[:, sl], vctx_ref[:, sl]
            q, d_o = _stack_heads(q_ref, hh), _stack_heads(do_ref, hh)
            p_lat, p_ctx, p_snk, den = _attn_probs(q, kw, kctx, _stack_sinks(sink_ref, h * KV_PER_STEP + hh), valid)
            inv = 1.0 / den
            p_lat, p_ctx, p_snk = p_lat * inv, p_ctx * inv, p_snk * inv
            dp_lat = _dot(d_o, vw, NT)
            dp_ctx = _dot(d_o, vctx, NT)
            dr = jnp.sum(p_lat * dp_lat, axis=-1, keepdims=True) + jnp.sum(p_ctx * dp_ctx, axis=-1, keepdims=True)
            ds_lat = (p_lat * (dp_lat - dr) * scale).astype(BF16)
            ds_ctx = (p_ctx * (dp_ctx - dr) * scale).astype(BF16)
            dq = _dot(ds_lat, kw, NN) + _dot(ds_ctx, kctx, NN)
            snk_terms = p_snk * dr
            dsn = jnp.zeros((8, HEAD_DIM), F32)
            for g in range(Q_PER_KV):
                c0 = (hh * Q_PER_KV + g) * HEAD_DIM
                dq_ref[:, c0:c0 + HEAD_DIM] = dq[g * WINDOW:(g + 1) * WINDOW]
                dsn = dsn + jnp.where(lane == g, -jnp.sum(snk_terms[g * WINDOW:(g + 1) * WINDOW], axis=0, keepdims=True),
                                      0.0)
            dkw_ref[rows, sl] += _dot(ds_lat, q, TN)
            dvw_ref[rows, sl] += _dot(p_lat.astype(BF16), d_o, TN)
            dkc_ref[:, sl] += _dot(ds_ctx, q, TN)
            dvc_ref[:, sl] += _dot(p_ctx.astype(BF16), d_o, TN)
            dsn_ref[hh] += dsn

    wacc = pl.BlockSpec((TP, KV_PER_STEP * HEAD_DIM), lambda h, i: (0, h))
    return _pcall(
        body, name=name, grid=(N_KV_HEADS // KV_PER_STEP, nb),
        in_specs=[qspec] + win + win + [ctx, ctx, sink, qspec],
        out_specs=[qspec, wacc, wacc, ctx, ctx, pl.BlockSpec((KV_PER_STEP, 8, HEAD_DIM), lambda h, i: (h, 0, 0))],
        out_shape=[jax.ShapeDtypeStruct(qn.shape, F32),
                   jax.ShapeDtypeStruct((TP, N_KV_HEADS * HEAD_DIM), F32),
                   jax.ShapeDtypeStruct((TP, N_KV_HEADS * HEAD_DIM), F32),
                   jax.ShapeDtypeStruct((n_ctx, N_KV_HEADS * HEAD_DIM), F32),
                   jax.ShapeDtypeStruct((n_ctx, N_KV_HEADS * HEAD_DIM), F32),
                   jax.ShapeDtypeStruct((N_KV_HEADS, 8, HEAD_DIM), F32)],
        sem=("arbitrary", "arbitrary"), args=(qn, kn, kn, kn, vb, vb, vb, kn, vb, sink_rows, do), ride=ride)


def _gla_masks(dirv):
    C = GLA_CHUNK

    def times(reps):
        r = lax.broadcasted_iota(jnp.int32, (C, reps * C), 0)
        c = lax.broadcasted_iota(jnp.int32, (C, reps * C), 1) & (C - 1)
        return jnp.where(dirv == 0, r, C - 1 - r), jnp.where(dirv == 0, c, C - 1 - c)

    def level(tt, ss, m):
        sh = m.bit_length() - 1
        same = (tt >> (sh + 1)) == (ss >> (sh + 1))
        return same, (tt >> sh) & 1, (ss >> sh) & 1

    tt, ss = times(3)
    le = (ss <= tt).astype(jnp.int32)
    sums = [le == 1]
    for m in GLA_LEVELS:
        same, ut, us = level(tt, ss, m)
        sums.append(same & (ut == us) & (ut == le))
    tt, ss = times(1)
    blocks = [ss == tt]
    for m in GLA_LEVELS:
        same, ut, us = level(tt, ss, m)
        blocks.append(same & (ut == 1) & (us == 0))
    mall3 = jnp.concatenate([jnp.where(s, 1.0, 0.0) for s in sums], axis=0).astype(BF16)
    return mall3, blocks


def _pieces(x):
    hi = x.astype(BF16)
    r1 = x - hi.astype(F32)
    mid = r1.astype(BF16)
    return hi, mid, (r1 - mid.astype(F32)).astype(BF16)


def _sum_f32(mall3, x):
    return _dot(mall3, jnp.concatenate(_pieces(x), axis=0), NN)


def _sum_f32_t(mall3, x):
    m = mall3[:, 0:GLA_CHUNK]
    hi, mid, lo = _pieces(x)
    return _dot(m, hi, TN) + _dot(m, mid, TN) + _dot(m, lo, TN)


def _gla_chunk_of(dirv, j, lc, nc):
    return jnp.where(dirv == 0, j, jnp.where(j < lc, lc - 1 - j, nc + lc - 1 - j))


def _gla_gate(lr_ref, wg_ref, bg_ref):
    pre = _dot(lr_ref[...].astype(BF16), wg_ref[0].astype(BF16), NN) + bg_ref[0]
    g = (jnp.minimum(pre, 0.0) - jnp.log(1.0 + jnp.exp(-jnp.abs(pre)))) * (1.0 / GLA_GATE_NORM)
    return pre, g


def _gla_fwd(z, qblk, kblk, vblk, lrblk, wg, bg, DV, n_ctx, name):
    R = z.shape[0]
    C = GLA_CHUNK
    DK = wg.shape[2] // GLA_HEADS
    nc, lc = R // C, n_ctx // C
    qscale = DK ** -0.5

    GK, GV = GLA_HEADS * DK, GLA_HEADS * DV

    def body(q_ref, k_ref, v_ref, lr_ref, wg_ref, bg_ref, o_ref, sp_ref, st_ref):
        dirv, j = pl.program_id(0), pl.program_id(1)

        @pl.when(j == 0)
        def _():
            st_ref[...] = jnp.zeros_like(st_ref)

        mall, blocks = _gla_masks(dirv)
        _, g_all = _gla_gate(lr_ref, wg_ref, bg_ref)
        E_all = _sum_f32(mall, g_all)
        for h in range(GLA_HEADS):
            ks, vs = slice(h * DK, (h + 1) * DK), slice(h * DV, (h + 1) * DV)
            q, k, v = q_ref[:, ks] * qscale, k_ref[:, ks], v_ref[:, vs].astype(BF16)
            g, E = g_all[:, ks], E_all[:, ks]
            st = st_ref[h]
            sp_ref[0, h, 0] = st
            A = jnp.where(blocks[0], _dot(q.astype(BF16), k.astype(BF16), NT), 0.0)
            for l in range(len(GLA_LEVELS)):
                e = jnp.exp(E[(1 + l) * C:(2 + l) * C])
                A = A + jnp.where(blocks[l + 1], _dot((q * e).astype(BF16), (k * e).astype(BF16), NT), 0.0)
            o_ref[0, :, vs] = (_dot((q * jnp.exp(E[0:C])).astype(BF16), st.astype(BF16), NT)
                               + _dot(A.astype(BF16), v, NN))
            last = jnp.sum(g, axis=0, keepdims=True)
            st_ref[h] = jnp.exp(last) * st + _dot(v, (k * jnp.exp(last - E[0:C])).astype(BF16), TN)

    chunk = functools.partial(_gla_chunk_of, lc=lc, nc=nc)
    return pl.pallas_call(
        body, name=name, grid=(2, nc),
        in_specs=[pl.BlockSpec((C, GK), lambda d, j: (chunk(d, j), qblk)),
                  pl.BlockSpec((C, GK), lambda d, j: (chunk(d, j), kblk)),
                  pl.BlockSpec((C, GV), lambda d, j: (chunk(d, j), vblk)),
                  pl.BlockSpec((C, LANES), lambda d, j: (chunk(d, j), lrblk)),
                  pl.BlockSpec((1, LANES, GK), lambda d, j: (d, 0, 0)),
                  pl.BlockSpec((1, 1, GK), lambda d, j: (d, 0, 0))],
        out_specs=[pl.BlockSpec((1, C, GV), lambda d, j: (d, chunk(d, j), 0)),
                   pl.BlockSpec((1, GLA_HEADS, 1, DV, DK), lambda d, j: (d, 0, j, 0, 0))],
        out_shape=[jax.ShapeDtypeStruct((2, R, GV), F32),
                   jax.ShapeDtypeStruct((2, GLA_HEADS, nc, DV, DK), F32)],
        scratch_shapes=[pltpu.VMEM((GLA_HEADS, DV, DK), F32)],
        compiler_params=_cparams("parallel", "arbitrary"),
    )(z, z, z, z, wg, bg)


def _gla_bwd(z, qblk, kblk, vblk, lrblk, wg, bg, sprev, do, n_ctx, name, ride=None):
    R = z.shape[0]
    C = GLA_CHUNK
    DK, DV = wg.shape[2] // GLA_HEADS, do.shape[1] // GLA_HEADS
    nc, lc = R // C, n_ctx // C
    qscale = DK ** -0.5
    nl = len(GLA_LEVELS)

    GK, GV = GLA_HEADS * DK, GLA_HEADS * DV

    def body(q_ref, k_ref, v_ref, lr_ref, wg_ref, bg_ref, sp_ref, do_ref,
             dq_ref, dk_ref, dv_ref, dpre_ref, dbg_ref, dst_ref):
        dirv, jr = pl.program_id(0), pl.program_id(1)

        @pl.when(jr == 0)
        def _():
            dst_ref[...] = jnp.zeros_like(dst_ref)
            dbg_ref[...] = jnp.zeros_like(dbg_ref)

        mall, blocks = _gla_masks(dirv)
        pre_all, g_all = _gla_gate(lr_ref, wg_ref, bg_ref)
        E_all = _sum_f32(mall, g_all)
        for h in range(GLA_HEADS):
            ks, vs = slice(h * DK, (h + 1) * DK), slice(h * DV, (h + 1) * DV)
            q, k, v = q_ref[:, ks] * qscale, k_ref[:, ks], v_ref[:, vs].astype(BF16)
            pre, g, E = pre_all[:, ks], g_all[:, ks], E_all[:, ks]
            last = jnp.sum(g, axis=0, keepdims=True)
            eb, er, decay = jnp.exp(E[0:C]), jnp.exp(last - E[0:C]), jnp.exp(last)
            st = sp_ref[0, h, 0]
            dst = dst_ref[h]
            d_o = do_ref[:, vs]
            qe, kd = q * eb, k * er
            qb, kb = q.astype(BF16), k.astype(BF16)
            A = jnp.where(blocks[0], _dot(qb, kb, NT), 0.0)
            levels = []
            for l in range(nl):
                e = jnp.exp(E[(1 + l) * C:(2 + l) * C])
                ql, kl = q * e, k * e
                levels.append((e, ql, kl, ql.astype(BF16), kl.astype(BF16)))
                A = A + jnp.where(blocks[l + 1], _dot(levels[l][3], levels[l][4], NT), 0.0)
            dA = _dot(d_o, v, NT)
            dv_ref[0, :, vs] = _dot(A.astype(BF16), d_o, TN) + _dot(kd.astype(BF16), dst.astype(BF16), NT)
            dqe = _dot(d_o, st.astype(BF16), NN)
            dkd = _dot(v, dst.astype(BF16), NN)
            G = jnp.where(blocks[0], dA, 0.0).astype(BF16)
            dq = dqe * eb + _dot(G, kb, NN)
            dk = dkd * er + _dot(G, qb, TN)
            dEr = dkd * kd
            dE = [dqe * qe - dEr]
            for l in range(nl):
                e, ql, kl, qlb, klb = levels[l]
                G = jnp.where(blocks[l + 1], dA, 0.0).astype(BF16)
                dql = _dot(G, klb, NN)
                dkl = _dot(G, qlb, TN)
                dq = dq + dql * e
                dk = dk + dkl * e
                dE.append(dql * ql + dkl * kl)
            dlast = jnp.sum(dst * st, axis=0, keepdims=True) * decay + jnp.sum(dEr, axis=0, keepdims=True)
            dg = _sum_f32_t(mall, jnp.concatenate(dE, axis=0)) + dlast
            dpre = dg * (1.0 / GLA_GATE_NORM) / (1.0 + jnp.exp(pre))
            dq_ref[0, :, ks] = dq * qscale
            dk_ref[0, :, ks] = dk
            dpre_ref[:, ks] = dpre.astype(BF16)
            dbg_ref[0, :, ks] += jnp.sum(dpre, axis=0, keepdims=True)
            dst_ref[h] = decay * dst + _dot(d_o, qe.astype(BF16), TN)

    def chunk(d, jr):
        return _gla_chunk_of(d, nc - 1 - jr, lc, nc)

    return _pcall(
        body, name=name, grid=(2, nc),
        in_specs=[pl.BlockSpec((C, GK), lambda d, j: (chunk(d, j), qblk)),
                  pl.BlockSpec((C, GK), lambda d, j: (chunk(d, j), kblk)),
                  pl.BlockSpec((C, GV), lambda d, j: (chunk(d, j), vblk)),
                  pl.BlockSpec((C, LANES), lambda d, j: (chunk(d, j), lrblk)),
                  pl.BlockSpec((1, LANES, GK), lambda d, j: (d, 0, 0)),
                  pl.BlockSpec((1, 1, GK), lambda d, j: (d, 0, 0)),
                  pl.BlockSpec((1, GLA_HEADS, 1, DV, DK), lambda d, j: (d, 0, nc - 1 - j, 0, 0)),
                  pl.BlockSpec((C, GV), lambda d, j: (chunk(d, j), 0))],
        out_specs=[pl.BlockSpec((1, C, GK), lambda d, j: (d, chunk(d, j), 0)),
                   pl.BlockSpec((1, C, GK), lambda d, j: (d, chunk(d, j), 0)),
                   pl.BlockSpec((1, C, GV), lambda d, j: (d, chunk(d, j), 0)),
                   pl.BlockSpec((C, GK), lambda d, j: (chunk(d, j), d)),
                   pl.BlockSpec((1, 1, GK), lambda d, j: (d, 0, 0))],
        out_shape=[jax.ShapeDtypeStruct((2, R, GK), F32),
                   jax.ShapeDtypeStruct((2, R, GK), F32),
                   jax.ShapeDtypeStruct((2, R, GV), F32),
                   jax.ShapeDtypeStruct((R, 2 * GK), BF16),
                   jax.ShapeDtypeStruct((2, 1, GK), F32)],
        scratch_shapes=[pltpu.VMEM((GLA_HEADS, DV, DK), F32)],
        sem=("arbitrary", "arbitrary"), args=(z, z, z, z, wg, bg, sprev, do), ride=ride)


def _glanorm_fwd(o, z, rbblk, gn, n_ctx, name):
    _, R, GV = o.shape
    T = R - n_ctx
    DV = GV // GLA_HEADS
    tm = _pick(n_ctx, 256, 8)
    ro = n_ctx // tm

    def body(o0_ref, o1_ref, rb_ref, gn_ref, p_ref):
        gnv = gn_ref[...]
        for h in range(GLA_HEADS):
            sl = slice(h * DV, (h + 1) * DV)
            og = o0_ref[0, :, sl] + o1_ref[0, :, sl]
            r = lax.rsqrt(jnp.mean(og * og, axis=-1, keepdims=True) + EPS)
            p_ref[:, sl] = (og * r * gnv * _silu(rb_ref[:, sl])).astype(BF16)

    return pl.pallas_call(
        body, name=name, grid=(T // tm,),
        in_specs=[pl.BlockSpec((1, tm, GV), lambda i: (0, i + ro, 0)), pl.BlockSpec((1, tm, GV), lambda i: (1, i + ro, 0)),
                  pl.BlockSpec((tm, GV), lambda i: (i + ro, rbblk)), pl.BlockSpec((1, DV), lambda i: (0, 0))],
        out_specs=pl.BlockSpec((tm, GV), lambda i: (i, 0)),
        out_shape=jax.ShapeDtypeStruct((T, GV), BF16), compiler_params=_cparams("parallel"))(o, o, z, gn)


def _glanorm_bwd(o, z, rbblk, gn, dp, n_ctx, name):
    _, R, GV = o.shape
    T = R - n_ctx
    DV = GV // GLA_HEADS
    tm = _pick(n_ctx, 256, 8)
    ro = n_ctx // tm

    def body(o0_ref, o1_ref, rb_ref, gn_ref, dp_ref, do_ref, drb_ref, acc_ref):
        i = pl.program_id(0)

        @pl.when(i == 0)
        def _():
            acc_ref[...] = jnp.zeros_like(acc_ref)

        gnv = gn_ref[...]
        dgn = jnp.zeros((1, DV), F32)
        for h in range(GLA_HEADS):
            sl = slice(h * DV, (h + 1) * DV)
            og = o0_ref[0, :, sl] + o1_ref[0, :, sl]
            rb = rb_ref[:, sl]
            d = dp_ref[:, sl]
            r = lax.rsqrt(jnp.mean(og * og, axis=-1, keepdims=True) + EPS)
            xh = og * r
            drb_ref[:, sl] = (d * xh * gnv * _dsilu(rb)).astype(BF16)
            dn = d * _silu(rb)
            dgn = dgn + jnp.sum(dn * xh, axis=0, keepdims=True)
            dxh = dn * gnv
            do_ref[:, sl] = (r * (dxh - xh * jnp.mean(dxh * xh, axis=-1, keepdims=True))).astype(BF16)
        acc_ref[0:1, :] += dgn

    row = pl.BlockSpec((tm, GV), lambda i: (i, 0))
    return pl.pallas_call(
        body, name=name, grid=(T // tm,),
        in_specs=[pl.BlockSpec((1, tm, GV), lambda i: (0, i + ro, 0)), pl.BlockSpec((1, tm, GV), lambda i: (1, i + ro, 0)),
                  pl.BlockSpec((tm, GV), lambda i: (i + ro, rbblk)), pl.BlockSpec((1, DV), lambda i: (0, 0)), row],
        out_specs=[row, row, pl.BlockSpec((8, DV), lambda i: (0, 0))],
        out_shape=[jax.ShapeDtypeStruct((T, GV), BF16), jax.ShapeDtypeStruct((T, GV), BF16),
                   jax.ShapeDtypeStruct((8, DV), F32)],
        compiler_params=_cparams("arbitrary"))(o, o, z, gn, dp)


def _gate_fwd(z, gablk, gbblk, ya, yg, n_ctx, name):
    T, D = ya.shape
    tm = _pick(n_ctx, 256, 8)
    ro = n_ctx // tm

    def body(ga_ref, gb_ref, ya_ref, yg_ref, m_ref):
        m_ref[...] = (_sigmoid(ga_ref[...]) * ya_ref[...] + _sigmoid(gb_ref[...]) * yg_ref[...]).astype(BF16)

    row = pl.BlockSpec((tm, D), lambda i: (i, 0))
    return pl.pallas_call(
        body, name=name, grid=(T // tm,),
        in_specs=[pl.BlockSpec((tm, D), lambda i: (i + ro, gablk)), pl.BlockSpec((tm, D), lambda i: (i + ro, gbblk)), row, row],
        out_specs=row, out_shape=jax.ShapeDtypeStruct((T, D), BF16), compiler_params=_cparams("parallel"))(z, z, ya, yg)


def _gate_bwd(z, gablk, gbblk, ya, yg, dm, n_ctx, name):
    T, D = ya.shape
    tm = _pick(n_ctx, 256, 8)
    ro = n_ctx // tm

    def body(ga_ref, gb_ref, ya_ref, yg_ref, dm_ref, dya_ref, dyg_ref, dga_ref, dgb_ref):
        d = dm_ref[...]
        sa, sb = _sigmoid(ga_ref[...]), _sigmoid(gb_ref[...])
        dya_ref[...] = (d * sa).astype(BF16)
        dyg_ref[...] = (d * sb).astype(BF16)
        dga_ref[...] = (d * ya_ref[...] * sa * (1.0 - sa)).astype(BF16)
        dgb_ref[...] = (d * yg_ref[...] * sb * (1.0 - sb)).astype(BF16)

    row = pl.BlockSpec((tm, D), lambda i: (i, 0))
    sh = jax.ShapeDtypeStruct((T, D), BF16)
    return pl.pallas_call(
        body, name=name, grid=(T // tm,),
        in_specs=[pl.BlockSpec((tm, D), lambda i: (i + ro, gablk)), pl.BlockSpec((tm, D), lambda i: (i + ro, gbblk)), row, row, row],
        out_specs=[row] * 4, out_shape=[sh] * 4, compiler_params=_cparams("parallel"))(z, z, ya, yg, dm)


def _resnorm_fwd(x, mix, gt, g, sc, sh, name):
    T, D = x.shape
    tm = _pick(T, 256, 8)

    def body(x_ref, mix_ref, gt_ref, g_ref, sc_ref, sh_ref, x1_ref, h_ref):
        x1 = x_ref[...] + gt_ref[...] * mix_ref[...]
        x1_ref[...] = x1
        r = lax.rsqrt(jnp.mean(x1 * x1, axis=-1, keepdims=True) + EPS)
        h_ref[...] = (x1 * r * g_ref[...] * (1.0 + sc_ref[...]) + sh_ref[...]).astype(BF16)

    row = pl.BlockSpec((tm, D), lambda i: (i, 0))
    vec = pl.BlockSpec((1, D), lambda i: (0, 0))
    return pl.pallas_call(
        body, name=name, grid=(T // tm,), in_specs=[row, row, vec, vec, vec, vec], out_specs=[row, row],
        out_shape=[jax.ShapeDtypeStruct((T, D), F32), jax.ShapeDtypeStruct((T, D), BF16)],
        compiler_params=_cparams("parallel"))(x, mix, gt, g, sc, sh)


def _loss_head(d, x1, gt, target, name):
    T, D = d.shape
    tm = _pick(T, 256, 8)

    def body(d_ref, x1_ref, gt_ref, t_ref, dy_ref, dd_ref, acc_ref):
        i = pl.program_id(0)

        @pl.when(i == 0)
        def _():
            acc_ref[...] = jnp.zeros_like(acc_ref)

        dv, gtv = d_ref[...], gt_ref[...]
        e = x1_ref[...] + gtv * dv - t_ref[...]
        dy = e * (1.0 / D)
        dy_ref[...] = dy
        dd_ref[...] = (dy * gtv).astype(BF16)
        acc_ref[0:1, :] += jnp.sum(e * e, axis=0, keepdims=True)
        acc_ref[1:2, :] += jnp.sum(dy * dv, axis=0, keepdims=True)

    row = pl.BlockSpec((tm, D), lambda i: (i, 0))
    return pl.pallas_call(
        body, name=name, grid=(T // tm,), in_specs=[row, row, pl.BlockSpec((1, D), lambda i: (0, 0)), row],
        out_specs=[row, row, pl.BlockSpec((8, D), lambda i: (0, 0))],
        out_shape=[jax.ShapeDtypeStruct((T, D), F32), jax.ShapeDtypeStruct((T, D), BF16),
                   jax.ShapeDtypeStruct((8, D), F32)],
        compiler_params=_cparams("arbitrary"))(d, x1, gt, target)


def _halo_specs(T, tm, tw, col_of, order):
    n8 = tm // 8
    if order == "ij":
        mid = lambda i, j: (i, col_of(j))
        prev = lambda i, j: (jnp.maximum(i * n8 - 1, 0), col_of(j))
        nxt = lambda i, j: (jnp.minimum((i + 1) * n8, T // 8 - 1), col_of(j))
    else:
        mid = lambda j, i: (i, col_of(j))
        prev = lambda j, i: (jnp.maximum(i * n8 - 1, 0), col_of(j))
        nxt = lambda j, i: (jnp.minimum((i + 1) * n8, T // 8 - 1), col_of(j))
    return [pl.BlockSpec((tm, tw), mid), pl.BlockSpec((8, tw), prev), pl.BlockSpec((8, tw), nxt)]


def _shift_rows(x, before, after):
    tm = x.shape[0]
    row = lax.broadcasted_iota(jnp.int32, x.shape, 0)
    return (jnp.where(row == 0, before, pltpu.roll(x, 1, 0)),
            jnp.where(row == tm - 1, after, pltpu.roll(x, tm - 1, 0)))


def _conv_fwd(u, cw, cb, name):
    T, F2 = u.shape
    F = F2 // 2
    tm, tw = _pick(T, 256, 8), _pick(F, 512)
    nt, nw = T // tm, F // tw

    def body(ua, uap, uan, ug, ugp, ugn, cwa, cwg, cba, cbg, f_ref):
        i = pl.program_id(0)
        first, last = i == 0, i == nt - 1

        def conv(u_ref, up_ref, un_ref, w_ref, b_ref):
            m = u_ref[...]
            p, n = _shift_rows(m, jnp.where(first, 0.0, up_ref[7:8, :]), jnp.where(last, 0.0, un_ref[0:1, :]))
            return p * w_ref[0:1, :] + m * w_ref[1:2, :] + n * w_ref[2:3, :] + b_ref[...]

        a = conv(ua, uap, uan, cwa, cba)
        g = conv(ug, ugp, ugn, cwg, cbg)
        f_ref[...] = (_silu(a) * g).astype(BF16)

    wspec = lambda off: pl.BlockSpec((3, tw), lambda i, j: (0, j + off))
    bspec = lambda off: pl.BlockSpec((1, tw), lambda i, j: (0, j + off))
    return pl.pallas_call(
        body, name=name, grid=(nt, nw),
        in_specs=_halo_specs(T, tm, tw, lambda j: j, "ij") + _halo_specs(T, tm, tw, lambda j: j + nw, "ij")
        + [wspec(0), wspec(nw), bspec(0), bspec(nw)],
        out_specs=pl.BlockSpec((tm, tw), lambda i, j: (i, j)),
        out_shape=jax.ShapeDtypeStruct((T, F), BF16),
        compiler_params=_cparams("parallel", "parallel"),
    )(u, u, u, u, u, u, cw, cw, cb, cb)


def _conv_bwd(u, df, cw, cb, name):
    T, F2 = u.shape
    F = F2 // 2
    tm, tw = _pick(T, 256, 8), _pick(F, 512)
    nt, nw = T // tm, F // tw

    def body(ua, uap, uan, ug, ugp, ugn, cwa, cwg, cba, cbg, df_ref, dfp, dfn, du_ref, acca_ref, accg_ref):
        i = pl.program_id(1)

        @pl.when(i == 0)
        def _():
            acca_ref[...] = jnp.zeros_like(acca_ref)
            accg_ref[...] = jnp.zeros_like(accg_ref)

        first, last = i == 0, i == nt - 1
        wa, wg, ba, bg = cwa[...], cwg[...], cba[...], cbg[...]

        def conv(p, m, n, w, b):
            return p * w[0:1] + m * w[1:2] + n * w[2:3] + b

        def grads(a, g, d):
            return d * g * _dsilu(a), d * _silu(a)

        xa, xg, d = ua[...], ug[...], df_ref[...]
        sa = _shift_rows(xa, jnp.where(first, 0.0, uap[7:8, :]), jnp.where(last, 0.0, uan[0:1, :]))
        sg = _shift_rows(xg, jnp.where(first, 0.0, ugp[7:8, :]), jnp.where(last, 0.0, ugn[0:1, :]))
        da, dg = grads(conv(sa[0], xa, sa[1], wa, ba), conv(sg[0], xg, sg[1], wg, bg), d)
        da_p, dg_p = grads(conv(uap[6:7, :], uap[7:8, :], xa[0:1], wa, ba),
                           conv(ugp[6:7, :], ugp[7:8, :], xg[0:1], wg, bg), dfp[7:8, :])
        da_n, dg_n = grads(conv(xa[tm - 1:tm], uan[0:1, :], uan[1:2, :], wa, ba),
                           conv(xg[tm - 1:tm], ugn[0:1, :], ugn[1:2, :], wg, bg), dfn[0:1, :])
        ta = _shift_rows(da, jnp.where(first, 0.0, da_p), jnp.where(last, 0.0, da_n))
        tg = _shift_rows(dg, jnp.where(first, 0.0, dg_p), jnp.where(last, 0.0, dg_n))
        du_ref[0] = (ta[1] * wa[0:1] + da * wa[1:2] + ta[0] * wa[2:3]).astype(BF16)
        du_ref[1] = (tg[1] * wg[0:1] + dg * wg[1:2] + tg[0] * wg[2:3]).astype(BF16)
        for t, (va, vg) in enumerate(((sa[0], sg[0]), (xa, xg), (sa[1], sg[1]))):
            acca_ref[t:t + 1, :] += jnp.sum(da * va, axis=0, keepdims=True)
            accg_ref[t:t + 1, :] += jnp.sum(dg * vg, axis=0, keepdims=True)
        acca_ref[3:4, :] += jnp.sum(da, axis=0, keepdims=True)
        accg_ref[3:4, :] += jnp.sum(dg, axis=0, keepdims=True)

    wspec = lambda off: pl.BlockSpec((3, tw), lambda j, i: (0, j + off))
    bspec = lambda off: pl.BlockSpec((1, tw), lambda j, i: (0, j + off))
    row = pl.BlockSpec((tm, tw), lambda j, i: (i, j))
    acc = pl.BlockSpec((8, tw), lambda j, i: (0, j))
    return pl.pallas_call(
        body, name=name, grid=(nw, nt),
        in_specs=_halo_specs(T, tm, tw, lambda j: j, "ji") + _halo_specs(T, tm, tw, lambda j: j + nw, "ji")
        + [wspec(0), wspec(nw), bspec(0), bspec(nw)] + _halo_specs(T, tm, tw, lambda j: j, "ji"),
        out_specs=[pl.BlockSpec((2, tm, tw), lambda j, i: (0, i, j)), acc, acc],
        out_shape=[jax.ShapeDtypeStruct((2, T, F), BF16),
                   jax.ShapeDtypeStruct((8, F), F32), jax.ShapeDtypeStruct((8, F), F32)],
        compiler_params=_cparams("parallel", "arbitrary"),
    )(u, u, u, u, u, u, cw, cw, cb, cb, df, df, df)


def _assemble_dz(lay, z_used, Z, n_ctx, dqa, drb, dga, dgb, dka, dva, dvg, dqg, dkg, dlr, name):
    T = dqa.shape[0]
    R = T + n_ctx
    tm = _pick(n_ctx, 128, 8)
    cb = n_ctx // tm

    def body(dqa_ref, drb_ref, dga_ref, dgb_ref, dka_ref, dva_ref, dvg0, dvg1, dqg0, dqg1, dkg0, dkg1, dlr_ref, o_ref):
        lat = pl.program_id(0) >= cb

        def put(seg, val):
            o_ref[:, lay[seg]:lay[seg] + val.shape[1]] = val.astype(BF16)

        def lat_only(ref):
            v = ref[...]
            return jnp.where(lat, v, jnp.zeros_like(v))

        put("qa", lat_only(dqa_ref))
        put("rb", lat_only(drb_ref))
        put("ga", lat_only(dga_ref))
        put("gb", lat_only(dgb_ref))
        put("ka", dka_ref[...])
        put("va", dva_ref[...])
        put("vb", dvg0[0] + dvg1[0])
        put("qb", dqg0[0] + dqg1[0])
        put("kb", dkg0[0] + dkg1[0])
        put("lr", dlr_ref[...])
        if Z > z_used:
            o_ref[:, z_used:] = jnp.zeros((tm, Z - z_used), BF16)

    lat_spec = lambda a: pl.BlockSpec((tm, a.shape[1]), lambda i: (jnp.maximum(i - cb, 0), 0))
    all_spec = lambda a: pl.BlockSpec((tm, a.shape[1]), lambda i: (i, 0))
    dir_specs = lambda a: [pl.BlockSpec((1, tm, a.shape[2]), lambda i: (0, i, 0)),
                           pl.BlockSpec((1, tm, a.shape[2]), lambda i: (1, i, 0))]
    return pl.pallas_call(
        body, name=name, grid=(R // tm,),
        in_specs=[lat_spec(dqa), lat_spec(drb), lat_spec(dga), lat_spec(dgb), all_spec(dka), all_spec(dva)]
        + dir_specs(dvg) + dir_specs(dqg) + dir_specs(dkg) + [all_spec(dlr)],
        out_specs=pl.BlockSpec((tm, Z), lambda i: (i, 0)),
        out_shape=jax.ShapeDtypeStruct((R, Z), BF16), compiler_params=_cparams("parallel"),
    )(dqa, drb, dga, dgb, dka, dva, dvg, dvg, dqg, dqg, dkg, dkg, dlr)


def _mod_fwd(ca, w, b, name):
    n, D = ca.shape
    N = w.shape[1]
    tn = _pick(N, 512)

    def body(c_ref, w_ref, b_ref, o_ref, s_ref):
        s = _silu(c_ref[...])
        s_ref[...] = s
        o_ref[...] = _dot(s.astype(BF16), w_ref[...].astype(BF16), NN) + b_ref[...]

    return pl.pallas_call(
        body, name=name, grid=(N // tn,),
        in_specs=[pl.BlockSpec((n, D), lambda j: (0, 0)), pl.BlockSpec((D, tn), lambda j: (0, j)),
                  pl.BlockSpec((1, tn), lambda j: (0, j))],
        out_specs=[pl.BlockSpec((n, tn), lambda j: (0, j)), pl.BlockSpec((n, D), lambda j: (0, 0))],
        out_shape=[jax.ShapeDtypeStruct((n, N), F32), jax.ShapeDtypeStruct((n, D), F32)],
        compiler_params=_cparams("arbitrary"))(ca, w, b)


def _silu_bwd(dsil, ca, name):
    def body(d_ref, c_ref, o_ref):
        o_ref[...] = d_ref[...] * _dsilu(c_ref[...])

    return pl.pallas_call(body, name=name, out_shape=jax.ShapeDtypeStruct(ca.shape, F32))(dsil, ca)


def _adam_math(w, g, m, v):
    c1 = 1.0 - ADAM_B1 ** ADAM_STEP
    c2 = 1.0 - ADAM_B2 ** ADAM_STEP
    mn = ADAM_B1 * m + (1.0 - ADAM_B1) * g
    vn = ADAM_B2 * v + (1.0 - ADAM_B2) * (g * g)
    return -ADAM_LR * ((mn / c1) / (jnp.sqrt(vn / c2) + ADAM_EPS) + ADAM_WD * w), mn, vn


def _adamw(w, g, m, v, name, ride=None):
    Rw, Cw = w.shape
    tr = _pick(Rw, 128, 8)

    def body(w_ref, g_ref, m_ref, v_ref, d_ref, mo_ref, vo_ref):
        d_ref[...], mo_ref[...], vo_ref[...] = _adam_math(w_ref[...], g_ref[...], m_ref[...], v_ref[...])

    row = pl.BlockSpec((tr, Cw), lambda i: (i, 0))
    sh = jax.ShapeDtypeStruct((Rw, Cw), F32)
    return _pcall(body, name=name, grid=(Rw // tr,), in_specs=[row] * 4, out_specs=[row] * 3, out_shape=[sh] * 3,
                  sem=("parallel",), args=(w, g, m, v), ride=ride)


HBM_SPEC = pl.BlockSpec(memory_space=pltpu.HBM)


def _exchange(inputs, out_shapes, stages, name):
    n_in, n_out = len(inputs), len(out_shapes)
    n = sum(len(s) for s in stages)

    def body(*refs):
        ins, outs = refs[:n_in], refs[n_in:n_in + n_out]
        send_sems, recv_sems = refs[n_in + n_out:]
        k = 0
        for stage in stages:
            copies = _stage_copies(stage, ins, outs, send_sems, recv_sems, k)
            for cp in copies:
                cp.start()
            for cp in copies:
                cp.wait()
            k += len(stage)

    return pl.pallas_call(
        body, name=name, in_specs=[HBM_SPEC] * n_in, out_specs=[HBM_SPEC] * n_out, out_shape=out_shapes,
        scratch_shapes=[pltpu.SemaphoreType.DMA((n,)), pltpu.SemaphoreType.DMA((n,))],
    )(*inputs)


def _stage_copies(stage, ins, outs, send_sems, recv_sems, k0=0):
    me = (lax.axis_index("x"), lax.axis_index("y"), lax.axis_index("c"))
    copies = []
    for k, ((skind, sidx), sfn, didx, dfn, flip) in enumerate(stage):
        src = (ins if skind == "in" else outs)[sidx].at[sfn(*me)]
        dst = outs[didx].at[dfn(*me)]
        if flip == (0, 0, 0):
            copies.append(pltpu.make_async_copy(src, dst, send_sems.at[k0 + k]))
        else:
            peer = tuple(1 - a if f else a for a, f in zip(me, flip))
            copies.append(pltpu.make_async_remote_copy(src, dst, send_sems.at[k0 + k], recv_sems.at[k0 + k],
                                                       device_id=peer, device_id_type=MESH))
    return copies


def _pcall(body, *, name, grid, in_specs, out_specs, out_shape, scratch_shapes=(), sem, args, ride=None):
    many = isinstance(out_shape, (list, tuple))
    out_specs, out_shape = (list(out_specs), list(out_shape)) if many else ([out_specs], [out_shape])
    if ride is None:
        res = pl.pallas_call(body, name=name, grid=grid, in_specs=list(in_specs), out_specs=out_specs,
                             out_shape=out_shape, scratch_shapes=list(scratch_shapes),
                             compiler_params=_cparams(*sem))(*args)
        return res if many else res[0]
    x_in, x_out, stage, aliases = ride
    n_in, n_out, n_scr, n_xin, n_xout = len(in_specs), len(out_specs), len(scratch_shapes), len(x_in), len(x_out)

    def wrapped(*refs):
        ins, xins = refs[:n_in], refs[n_in:n_in + n_xin]
        o0 = n_in + n_xin
        outs, xouts = refs[o0:o0 + n_out], refs[o0 + n_out:o0 + n_out + n_xout]
        s0 = o0 + n_out + n_xout
        scr, (send_sems, recv_sems) = refs[s0:s0 + n_scr], refs[s0 + n_scr:]
        first = functools.reduce(jnp.logical_and, [pl.program_id(d) == 0 for d in range(len(grid))])
        last = functools.reduce(jnp.logical_and, [pl.program_id(d) == grid[d] - 1 for d in range(len(grid))])

        @pl.when(first)
        def _():
            for cp in _stage_copies(stage, xins, xouts, send_sems, recv_sems):
                cp.start()

        body(*ins, *outs, *scr)

        @pl.when(last)
        def _():
            for cp in _stage_copies(stage, xins, xouts, send_sems, recv_sems):
                cp.wait()

    res = pl.pallas_call(
        wrapped, name=name, grid=grid, in_specs=list(in_specs) + [HBM_SPEC] * n_xin,
        out_specs=out_specs + [HBM_SPEC] * n_xout, out_shape=out_shape + list(x_out),
        scratch_shapes=list(scratch_shapes) + [pltpu.SemaphoreType.DMA((len(stage),)),
                                               pltpu.SemaphoreType.DMA((len(stage),))],
        input_output_aliases={n_in + a: n_out + b for a, b in aliases.items()},
        compiler_params=_cparams(*(["arbitrary"] * len(grid))))(*args, *x_in)
    main = res[:n_out]
    return (main if many else main[0]), list(res[n_out:])


FLIPS_ALL = [(0, 0, 1), (0, 1, 0), (0, 1, 1), (1, 0, 0), (1, 0, 1), (1, 1, 0), (1, 1, 1)]
FLIPS_CHIP = [(0, 1, 0), (1, 0, 0), (1, 1, 0)]


def _sum_slots(buf, name):
    n, r, w = buf.shape
    tr = _pick(r, 256, 8)

    def body(b_ref, o_ref):
        acc = b_ref[0]
        for s in range(1, n):
            acc = acc + b_ref[s]
        o_ref[...] = acc

    return pl.pallas_call(
        body, name=name, grid=(r // tr,), in_specs=[pl.BlockSpec((n, tr, w), lambda i: (0, i, 0))],
        out_specs=pl.BlockSpec((tr, w), lambda i: (i, 0)), out_shape=jax.ShapeDtypeStruct((r, w), F32),
        compiler_params=_cparams("parallel"))(buf)


def _allreduce_plan(buf):
    whole = lambda x, y, c: (slice(None), slice(None))
    slot = lambda x, y, c: (4 * x + 2 * y + c,)
    stage = [(("in", 0), whole, 0, slot, f) for f in [(0, 0, 0)] + FLIPS_ALL]
    return [jax.ShapeDtypeStruct((8,) + buf.shape, F32)], stage


def _allreduce(buf, name):
    shapes, stage = _allreduce_plan(buf)
    (slots,) = _exchange([buf], shapes, [stage], name + "_x")
    return _sum_slots(slots, name + "_sum")


def _gather_plan(shards, src):
    half = lambda a, c: pl.ds(c * (a.shape[0] // 2), a.shape[0] // 2)
    first, second = [], []
    for n, a in enumerate(shards):
        for f in FLIPS_CHIP:
            first.append((("in", n), lambda x, y, c, a=a: (half(a, c), slice(None)), n,
                          lambda x, y, c, a=a: (2 * x + y, half(a, c), slice(None)), f))
            peer_slot = lambda x, y, c, a=a, f=f: (2 * (x ^ f[0]) + (y ^ f[1]), half(a, c), slice(None))
            second.append(((src, n), peer_slot, n, peer_slot, (0, 0, 1)))
    outs = [jax.ShapeDtypeStruct((4,) + a.shape, a.dtype) for a in shards]
    return first, second, outs


def _allgather_weights(shards, name):
    first, second, outs = _gather_plan(shards, "out")
    return _exchange(shards, outs, [first, second], name)


def _place_own(buf, shard, svec, name):
    _, Rs, Cs = buf.shape
    tr = _pick(Rs, 256, 16)

    def body(s_ref, buf_ref, sh_ref, o_ref):
        o_ref[0] = sh_ref[...]

    grid_spec = pltpu.PrefetchScalarGridSpec(
        num_scalar_prefetch=1, grid=(Rs // tr,),
        in_specs=[pl.BlockSpec(memory_space=pl.ANY), pl.BlockSpec((tr, Cs), lambda i, s: (i, 0))],
        out_specs=pl.BlockSpec((1, tr, Cs), lambda i, s: (s[0], i, 0)))
    return pl.pallas_call(body, name=name, grid_spec=grid_spec, out_shape=jax.ShapeDtypeStruct(buf.shape, buf.dtype),
                          input_output_aliases={1: 0}, compiler_params=_cparams("arbitrary"))(svec, buf, shard)


def _add_pair(G, bufA, cvec, name):
    _, Rs, Cs = G.shape
    Rh = Rs // 2
    tr = _pick(Rh, 128, 16)
    nb = Rh // tr

    def body(c_ref, g_ref, a_ref, o_ref):
        o_ref[...] = (g_ref[...] + a_ref[...]).astype(BF16)

    grid_spec = pltpu.PrefetchScalarGridSpec(
        num_scalar_prefetch=1, grid=(4, nb),
        in_specs=[pl.BlockSpec((1, tr, Cs), lambda s, i, c_ref: (s, c_ref[0] * nb + i, 0)),
                  pl.BlockSpec((1, tr, Cs), lambda s, i, c_ref: (s, i, 0))],
        out_specs=pl.BlockSpec((1, tr, Cs), lambda s, i, c_ref: (s, i, 0)))
    return pl.pallas_call(body, name=name, grid_spec=grid_spec, out_shape=jax.ShapeDtypeStruct((4, Rh, Cs), BF16),
                          compiler_params=_cparams("parallel", "parallel"))(cvec, G, bufA)


def _sum_chips(G, bufA, bufB, cvec, svec, name):
    _, Rs, Cs = G.shape
    Rh = Rs // 2
    tr = _pick(Rh, 128, 16)
    nb = Rh // tr

    def body(c_ref, s_ref, g_ref, a_ref, b_ref, o_ref):
        o_ref[...] = (g_ref[0] + a_ref[0]) + b_ref[0].astype(F32) + b_ref[1].astype(F32) + b_ref[2].astype(F32)

    grid_spec = pltpu.PrefetchScalarGridSpec(
        num_scalar_prefetch=2, grid=(nb,),
        in_specs=[pl.BlockSpec((1, tr, Cs), lambda i, c, s: (s[0], c[0] * nb + i, 0)),
                  pl.BlockSpec((1, tr, Cs), lambda i, c, s: (s[0], i, 0)),
                  pl.BlockSpec((3, tr, Cs), lambda i, c, s: (0, i, 0))],
        out_specs=pl.BlockSpec((tr, Cs), lambda i, c, s: (i, 0)))
    return pl.pallas_call(body, name=name, grid_spec=grid_spec, out_shape=jax.ShapeDtypeStruct((Rh, Cs), F32),
                          compiler_params=_cparams("parallel"))(cvec, svec, G, bufA, bufB)


def _pair_plan(grads):
    Rh = [g.shape[1] // 2 for g in grads]
    whole3 = lambda x, y, c: (slice(None), slice(None), slice(None))
    stage = [(("in", n), lambda x, y, c, n=n: (slice(None), pl.ds((1 - c) * Rh[n], Rh[n]), slice(None)), n,
              whole3, (0, 0, 1)) for n in range(len(grads))]
    return [jax.ShapeDtypeStruct((4, Rh[n], g.shape[2]), F32) for n, g in enumerate(grads)], stage


def _chips_plan(P):
    stage = [(("in", n), lambda x, y, c, f=f: (2 * (x ^ f[0]) + (y ^ f[1]),), n, lambda x, y, c, k=k: (k,), f)
             for n in range(len(P)) for k, f in enumerate(FLIPS_CHIP)]
    return [jax.ShapeDtypeStruct((3,) + p.shape[1:], BF16) for p in P], stage


def _halves_plan(mine):
    whole2 = lambda x, y, c: (slice(None), slice(None))
    stage = [(("in", n), whole2, n, whole2, (0, 0, 1)) for n in range(len(mine))]
    return [jax.ShapeDtypeStruct(r.shape, F32) for r in mine], stage


def _adamw_halves(w, mine, other, m, v, cvec, name):
    Rs, Cs = w.shape
    Rh = Rs // 2
    tr = _pick(Rh, 128, 8)
    nb = Rh // tr

    def body(c_ref, w_ref, a_ref, b_ref, m_ref, v_ref, g_ref, d_ref, mo_ref, vo_ref):
        gv = jnp.where(pl.program_id(0) // nb == c_ref[0], a_ref[...], b_ref[...])
        g_ref[...] = gv
        d_ref[...], mo_ref[...], vo_ref[...] = _adam_math(w_ref[...], gv, m_ref[...], v_ref[...])

    row = pl.BlockSpec((tr, Cs), lambda i, c: (i, 0))
    hrow = pl.BlockSpec((tr, Cs), lambda i, c: (i % nb, 0))
    grid_spec = pltpu.PrefetchScalarGridSpec(num_scalar_prefetch=1, grid=(2 * nb,),
                                             in_specs=[row, hrow, hrow, row, row], out_specs=[row] * 4)
    return pl.pallas_call(body, name=name, grid_spec=grid_spec, out_shape=[jax.ShapeDtypeStruct((Rs, Cs), F32)] * 4,
                          compiler_params=_cparams("parallel"))(cvec, w, mine, other, m, v)


def _pack(arrays):
    flat = [a.reshape(-1).astype(F32) for a in arrays]
    meta, off = [], 0
    for a, f in zip(arrays, flat):
        meta.append((off, a.shape))
        off += f.shape[0]
    total = -(-off // (8 * LANES)) * (8 * LANES)
    flat.append(jnp.zeros((total - off,), F32))
    return jnp.concatenate(flat).reshape(total // LANES, LANES), meta


def _unpack(buf, meta):
    flat = buf.reshape(-1)
    out = []
    for off, shape in meta:
        size = 1
        for s in shape:
            size *= s
        out.append(flat[off:off + size].reshape(shape))
    return out


WEIGHT_NAMES = ["c_ctx", "w_mod", "b_mod", "g_mix", "w_in", "q_norm", "k_norm", "attn_sink", "w_gate_f", "b_gate_f",
                "w_gate_b", "b_gate_b", "gla_norm", "w_attn_o", "w_gla_o", "w_out", "g_ffn", "w_up", "conv_w",
                "conv_b", "w_down"]
BIG_NAMES = ["w_in", "w_attn_o", "w_gla_o", "w_out", "w_up", "w_down"]
SHARDED_SMALL = ["w_gate_f", "w_gate_b", "conv_w"]


def _layouts(D):
    aw, kvw, gk, gv = N_Q_HEADS * HEAD_DIM, N_KV_HEADS * HEAD_DIM, D // 2, D
    widths = {"qa": aw, "ka": kvw, "va": kvw, "qb": gk, "kb": gk, "vb": gv, "rb": gv, "lr": 2 * GLA_LOWRANK,
              "ga": D, "gb": D}
    orig, off = {}, 0
    for s in ["qa", "ka", "va", "qb", "kb", "vb", "rb", "lr", "ga", "gb"]:
        orig[s] = off
        off += widths[s]
    order = ["qa", "vb", "rb", "ga", "gb", "ka", "va", "qb", "kb", "lr"]
    lay, off = {}, 0
    for s in order:
        lay[s] = off
        off += LANES if s == "lr" else widths[s]
    align = {"qa": aw, "vb": D, "rb": D, "ga": D, "gb": D, "ka": kvw, "va": kvw, "qb": gk, "kb": gk,
             "lr": LANES}
    for s in order:
        assert lay[s] % align[s] == 0, (s, lay[s], align[s])
    return widths, orig, order, lay, off, -(-off // (2 * MXU_TILE)) * (2 * MXU_TILE)


def _rope_tables(T, L):
    t = jnp.arange(T)
    nf = HEAD_DIM // 4
    inv = ROPE_THETA ** (-jnp.arange(nf, dtype=F32) / nf)
    ang = jnp.concatenate([(t // GRID_W)[:, None] * inv, (t % GRID_W)[:, None] * inv], axis=-1)
    cos, sin = jnp.cos(ang), jnp.sin(ang)
    cos2 = jnp.concatenate([jnp.ones((L, HEAD_DIM), F32), jnp.concatenate([cos, cos], axis=-1)], axis=0)
    sin2 = jnp.concatenate([jnp.zeros((L, HEAD_DIM), F32), jnp.concatenate([-sin, sin], axis=-1)], axis=0)
    return cos2, sin2


def _step(x, c, ctx, loss_target, W, M, V):
    xi, yi, ci = lax.axis_index("x"), lax.axis_index("y"), lax.axis_index("c")
    chip = 2 * xi + yi
    dev = 2 * chip + ci
    south = (ci == 0).astype(F32)
    cvec = ci.reshape(1).astype(jnp.int32)
    svec = chip.reshape(1).astype(jnp.int32)
    T, D = x.shape[1], x.shape[2]
    L = ctx.shape[1]
    R = L + T
    F = 4 * W["w_down"].shape[1]
    GK, GV = D // 2, D
    DK, DV = GK // GLA_HEADS, GV // GLA_HEADS
    N6 = 6 * D
    N4 = N6 // 4
    widths, orig, order, lay, z_used, Z = _layouts(D)

    def place_cols(shard, full_cols):
        cols = shard.shape[-1]
        full = jnp.zeros(shard.shape[:-1] + (full_cols,), F32)
        return lax.dynamic_update_slice(full, shard * south, (0,) * (shard.ndim - 1) + (chip * cols,))

    c_rows = lax.dynamic_update_slice(jnp.zeros((8, D), F32), c, (dev, 0))
    bufa, meta = _pack([c_rows, place_cols(W["w_gate_f"][0], GK), place_cols(W["w_gate_b"][0], GK),
                        place_cols(W["conv_w"][0], 2 * F)])
    c_all, wgf, wgb, cw = _unpack(_allreduce(bufa, "gather_small"), meta)
    ca = jnp.concatenate([c_all, W["c_ctx"][None, :], jnp.zeros((7, D), F32)], axis=0)
    b_shard = lax.dynamic_slice(W["b_mod"], (0, chip * N4), (1, N4))
    mod_part, sil = _mod_fwd(ca, W["w_mod"][0], b_shard, "mod_fwd")
    slots = lax.dynamic_update_slice(jnp.zeros((4, 16, N4), F32), (mod_part * south)[None], (chip, 0, 0))
    mod_all = _allreduce(slots.reshape(64, N4), "gather_mod").reshape(4, 16, N4).transpose(1, 0, 2).reshape(16, N6)
    mx = lax.dynamic_slice(mod_all, (dev, 0), (1, N6)).reshape(6, 1, D)
    mc = mod_all[8].reshape(6, 1, D)

    sq = lambda a: a.reshape(a.shape[1:])
    shards = [sq(W[n]).astype(BF16) for n in BIG_NAMES]
    own = lambda g, s, n: _place_own(g, s, svec, "place_" + n)
    cols = lambda g: g.transpose(1, 0, 2).reshape(g.shape[1], 4 * g.shape[2])
    rows = lambda g: g.reshape(4 * g.shape[1], g.shape[2])
    w_in_f = cols(own(_allgather_weights(shards[:1], "gather_w_in")[0], shards[0], "w_in"))
    seg = lambda s: w_in_f[:, orig[s]:orig[s] + widths[s]]
    w_cat = jnp.concatenate([jnp.pad(seg(s), ((0, 0), (0, LANES - widths[s]))) if s == "lr" else seg(s)
                             for s in order] + [jnp.zeros((D, Z - z_used), BF16)], axis=1)
    gather1, gather2, gather_outs = _gather_plan(shards[1:], "in")
    wg = jnp.zeros((2, LANES, GK), F32).at[0, :GLA_LOWRANK].set(wgf).at[1, GLA_LOWRANK:2 * GLA_LOWRANK].set(wgb)
    bg = jnp.stack([W["b_gate_f"], W["b_gate_b"]])
    cb = W["conv_b"]
    sink_rows = jnp.broadcast_to(W["attn_sink"][0][:, None], (N_Q_HEADS, HEAD_DIM))
    cos2, sin2 = _rope_tables(T, L)
    blk = lambda s, w: lay[s] // w

    sc1 = jnp.stack([mc[1], mx[1]])
    sh1 = jnp.stack([mc[0], mx[0]])
    h = _modnorm_fwd(ctx[0], x[0], W["g_mix"], sc1, sh1, "modnorm1")
    z, landed = _matmul(h, w_cat, "nn", F32, "proj_in", tn=1536, ride=(shards[1:], gather_outs, gather1, {}))
    qn = _qknorm_fwd(z, blk("qa", widths["qa"]), T, L, W["q_norm"], cos2, sin2, N_Q_HEADS, "qnorm")
    kn = _qknorm_fwd(z, blk("ka", widths["ka"]), R, 0, W["k_norm"], cos2, sin2, N_KV_HEADS, "knorm")
    vb = _cast_seg(z, blk("va", widths["va"]), widths["va"], "vcast")
    o_attn, landed = _attn_fwd(qn, kn, vb, sink_rows, L, "attn_fwd",
                               ride=(landed, gather_outs, gather2, {n: n for n in range(len(landed))}))
    g_ao, g_go, g_out, g_up, g_dn = [own(g, s, n) for g, s, n in zip(landed, shards[1:], BIG_NAMES[1:])]
    w_ao, w_go, w_out, w_up, w_dn = rows(g_ao), rows(g_go), rows(g_out), cols(g_up), rows(g_dn)
    gla_blks = (blk("qb", GK), blk("kb", GK), blk("vb", GV), blk("lr", LANES))
    o_g, sprev = _gla_fwd(z, *gla_blks, wg, bg, DV, L, "gla_fwd")
    p = _glanorm_fwd(o_g, z, blk("rb", D), W["gla_norm"], L, "glanorm")
    ya = _matmul(o_attn, w_ao, "nn", F32, "proj_attn_o")
    yg = _matmul(p, w_go, "nn", F32, "proj_gla_o")
    m = _gate_fwd(z, blk("ga", D), blk("gb", D), ya, yg, L, "gate")
    mix = _matmul(m, w_out, "nn", F32, "proj_out")
    x1, h2 = _resnorm_fwd(x[0], mix, mx[2], W["g_ffn"], mx[4], mx[3], "resnorm2")
    u = _matmul(h2, w_up, "nn", F32, "ffn_up")
    f = _conv_fwd(u, cw, cb, "conv_swiglu")
    d = _matmul(f, w_dn, "nn", F32, "ffn_down", tk=2816)
    dy, dd, lacc = _loss_head(d, x1, mx[5], loss_target[0], "loss_head")
    loss = lax.psum((0.5 / D) * jnp.sum(lacc[0]), ("x", "y", "c"))

    gw_dn = _matmul(f, dd, "tn", F32, "ffn_down_dw")
    df = _matmul(dd, w_dn, "nt", F32, "ffn_down_dx")
    du, acca, accg = _conv_bwd(u, df, cw, cb, "conv_swiglu_bwd")
    gw_up = _matmul(h2, du, "tn", F32, "ffn_up_dw", tm=512, halves="b", col_shards=4)
    dh2 = _matmul(du, w_up, "nt", F32, "ffn_up_dx", tk=2816, halves="a")
    dx1, dmix, s2 = _resnorm_bwd(x1, dh2, W["g_ffn"], mx[4], dy, mix, mx[2], "resnorm2_bwd")
    gw_out = _matmul(m, dmix, "tn", F32, "proj_out_dw")
    dm = _matmul(dmix, w_out, "nt", F32, "proj_out_dx")
    dya, dyg, dga, dgb = _gate_bwd(z, blk("ga", D), blk("gb", D), ya, yg, dm, L, "gate_bwd")
    gw_ao = _matmul(o_attn, dya, "tn", F32, "proj_attn_o_dw")
    do_attn = _matmul(dya, w_ao, "nt", BF16, "proj_attn_o_dx")
    gw_go = _matmul(p, dyg, "tn", F32, "proj_gla_o_dw")
    dp = _matmul(dyg, w_go, "nt", F32, "proj_gla_o_dx")
    do_gla, drb, s_gn = _glanorm_bwd(o_g, z, blk("rb", D), W["gla_norm"], dp, L, "glanorm_bwd")
    do_pad = jnp.concatenate([jnp.zeros((L, GV), BF16), do_gla], axis=0)
    by_cols = lambda g: g.reshape(g.shape[0], 4, g.shape[1] // 4).transpose(1, 0, 2)
    by_rows = lambda g: g.reshape(4, g.shape[0] // 4, g.shape[1])
    early = [by_rows(gw_ao), by_rows(gw_go), by_rows(gw_out), gw_up, by_rows(gw_dn)]
    (dqg, dkg, dvg, dpre, dbg), pair_e = _gla_bwd(z, *gla_blks, wg, bg, sprev, do_pad, L, "gla_bwd",
                                                  ride=(early, *_pair_plan(early), {}))
    sums_e = [_add_pair(g, a, cvec, "reduce_early_add%d" % n) for n, (g, a) in enumerate(zip(early, pair_e))]
    wg_cat = jnp.concatenate([wg[0], wg[1]], axis=1)
    dlr = _matmul(dpre, wg_cat, "nt", BF16, "gla_gate_dx")
    dwg = _matmul(z[:, lay["lr"]:lay["lr"] + LANES], dpre, "tn", F32, "gla_gate_dw")
    (dqn, dkw, dvw, dkc, dvc, dsn), chips_e = _attn_bwd(qn, kn, vb, sink_rows, do_attn, L, "attn_bwd",
                                                        ride=(sums_e, *_chips_plan(sums_e), {}))
    mine_e = [_sum_chips(g, a, b, cvec, svec, "reduce_early_sum%d" % n)
              for n, (g, a, b) in enumerate(zip(early, pair_e, chips_e))]
    dqa, s_qn = _qknorm_bwd(z, blk("qa", widths["qa"]), T, L, W["q_norm"], cos2, sin2, dqn, N_Q_HEADS, "qnorm_bwd")
    dk_all = jnp.concatenate([dkc, dkw[WINDOW:WINDOW + T]], axis=0)
    dv_all = jnp.concatenate([dvc, dvw[WINDOW:WINDOW + T]], axis=0)
    dka, s_kn = _qknorm_bwd(z, blk("ka", widths["ka"]), R, 0, W["k_norm"], cos2, sin2, dk_all, N_KV_HEADS, "knorm_bwd")
    dz = _assemble_dz(lay, z_used, Z, L, dqa, drb, dga, dgb, dka, dv_all, dvg, dqg, dkg, dlr, "assemble_dz")
    gw_cat, other_e = _matmul(h, dz, "tn", F32, "proj_in_dw", tn=768, tk=2816,
                              ride=(mine_e, *_halves_plan(mine_e), {}))
    gw_in = jnp.concatenate([gw_cat[:, lay[s]:lay[s] + widths[s]] for s in ["qa", "ka", "va", "qb", "kb", "vb", "rb",
                                                                           "lr", "ga", "gb"]], axis=1)
    late = [by_cols(gw_in)]
    shapes, stage = _pair_plan(late)
    pair_l = _exchange(late, shapes, [stage], "reduce_late_pair")
    sums_l = [_add_pair(late[0], pair_l[0], cvec, "reduce_late_add")]
    dh, chips_l = _matmul(dz, w_cat, "nt", F32, "proj_in_dx", tk=4608, ride=(sums_l, *_chips_plan(sums_l), {}))
    mine_l = [_sum_chips(late[0], pair_l[0], chips_l[0], cvec, svec, "reduce_late_sum")]
    shapes, stage = _halves_plan(mine_l)
    other_l = _exchange(mine_l, shapes, [stage], "reduce_late_halves")
    mine, other = mine_l + mine_e, list(other_l) + other_e
    grad_x, s1 = _modnorm_bwd(x[0], dh, W["g_mix"], mx[1], dx1, "modnorm1_bwd", dh_roff=L)
    _, s1c = _modnorm_bwd(ctx[0], dh, W["g_mix"], mc[1], None, "modnorm1_ctx_bwd")

    dmod_x = jnp.concatenate([s1[0], s1[1], s2[3], s2[0], s2[1], lacc[1]])
    dmod_c = jnp.concatenate([s1c[0], s1c[1], jnp.zeros((4 * D,), F32)])
    dmod_rows = lax.dynamic_update_slice(jnp.zeros((9, N6), F32).at[8].set(dmod_c), dmod_x[None], (dev, 0))
    small = [dmod_rows, dmod_x + dmod_c, s1[2] + s1c[2], s_qn[0], s_kn[0], dsn[:, 0, :Q_PER_KV].reshape(N_Q_HEADS),
             dwg[:GLA_LOWRANK, :GK], dbg[0].reshape(GK), dwg[GLA_LOWRANK:2 * GLA_LOWRANK, GK:], dbg[1].reshape(GK),
             s_gn[0], s2[2], jnp.concatenate([acca[0:3], accg[0:3]], axis=1), jnp.concatenate([acca[3], accg[3]])]
    bufc, meta = _pack(small)
    (dmod_sum, g_b_mod, g_g_mix, g_q_norm, g_k_norm, g_sink, g_wgf, g_bgf, g_wgb, g_bgb, g_gla_norm, g_g_ffn,
     g_conv_w, g_conv_b) = _unpack(_allreduce(bufc, "reduce_small"), meta)
    dmod16 = lax.dynamic_slice(jnp.concatenate([dmod_sum, jnp.zeros((7, N6), F32)], axis=0), (0, chip * N4), (16, N4))
    g_w_mod = _matmul(sil, dmod16, "tn", F32, "mod_dw")
    dsil = _matmul(dmod16, W["w_mod"][0], "nt", F32, "mod_dx")
    g_c_ctx = _silu_bwd(_allreduce(dsil * south, "reduce_cctx"), ca, "silu_bwd")[8]

    cut = lambda g: lax.dynamic_slice(g, (0, chip * (g.shape[1] // 4)), (g.shape[0], g.shape[1] // 4))
    grads = {"c_ctx": g_c_ctx, "w_mod": g_w_mod[None], "b_mod": g_b_mod[None], "g_mix": g_g_mix[None],
             "q_norm": g_q_norm[None], "k_norm": g_k_norm[None], "attn_sink": g_sink[None],
             "w_gate_f": cut(g_wgf)[None], "b_gate_f": g_bgf[None], "w_gate_b": cut(g_wgb)[None],
             "b_gate_b": g_bgb[None], "gla_norm": g_gla_norm[None], "g_ffn": g_g_ffn[None],
             "conv_w": cut(g_conv_w)[None], "conv_b": g_conv_b[None]}

    delta, new_m, new_v = {}, {}, {}
    dl, mn, vn = _adamw(W["w_mod"][0], g_w_mod, M["w_mod"][0], V["w_mod"][0], "adamw_w_mod")
    delta["w_mod"], new_m["w_mod"], new_v["w_mod"] = dl[None], mn[None], vn[None]
    for n, a, b in zip(BIG_NAMES, mine, other):
        g, dl, mn, vn = _adamw_halves(sq(W[n]), a, b, sq(M[n]), sq(V[n]), cvec, "adamw_" + n)
        grads[n], delta[n], new_m[n], new_v[n] = g[None], dl[None], mn[None], vn[None]
    small_names = [n for n in WEIGHT_NAMES if n not in delta]
    packs = [_pack([src[n] for n in small_names]) for src in (W, grads, M, V)]
    meta = packs[0][1]
    outs = _adamw(packs[0][0], packs[1][0], packs[2][0], packs[3][0], "adamw_small")
    for res, o in zip((delta, new_m, new_v), outs):
        for n, a in zip(small_names, _unpack(o, meta)):
            res[n] = a
    return (loss, grad_x[None], *[grads[n] for n in WEIGHT_NAMES], *[delta[n] for n in WEIGHT_NAMES],
            *[new_m[n] for n in WEIGHT_NAMES], *[new_v[n] for n in WEIGHT_NAMES])


def kernel(x, c, ctx, c_ctx, w_mod, b_mod, g_mix, w_in, q_norm, k_norm, attn_sink, w_gate_f, b_gate_f, w_gate_b, b_gate_b, gla_norm, w_attn_o, w_gla_o, w_out, g_ffn, w_up, conv_w, conv_b, w_down, loss_target, m_c_ctx, m_w_mod, m_b_mod, m_g_mix, m_w_in, m_q_norm, m_k_norm, m_attn_sink, m_w_gate_f, m_b_gate_f, m_w_gate_b, m_b_gate_b, m_gla_norm, m_w_attn_o, m_w_gla_o, m_w_out, m_g_ffn, m_w_up, m_conv_w, m_conv_b, m_w_down, v_c_ctx, v_w_mod, v_b_mod, v_g_mix, v_w_in, v_q_norm, v_k_norm, v_attn_sink, v_w_gate_f, v_b_gate_f, v_w_gate_b, v_b_gate_b, v_gla_norm, v_w_attn_o, v_w_gla_o, v_w_out, v_g_ffn, v_w_up, v_conv_w, v_conv_b, v_w_down):
    W = dict(zip(WEIGHT_NAMES, (c_ctx, w_mod, b_mod, g_mix, w_in, q_norm, k_norm, attn_sink, w_gate_f, b_gate_f,
                                w_gate_b, b_gate_b, gla_norm, w_attn_o, w_gla_o, w_out, g_ffn, w_up, conv_w, conv_b,
                                w_down)))
    M = dict(zip(WEIGHT_NAMES, (m_c_ctx, m_w_mod, m_b_mod, m_g_mix, m_w_in, m_q_norm, m_k_norm, m_attn_sink,
                                m_w_gate_f, m_b_gate_f, m_w_gate_b, m_b_gate_b, m_gla_norm, m_w_attn_o, m_w_gla_o,
                                m_w_out, m_g_ffn, m_w_up, m_conv_w, m_conv_b, m_w_down)))
    V = dict(zip(WEIGHT_NAMES, (v_c_ctx, v_w_mod, v_b_mod, v_g_mix, v_w_in, v_q_norm, v_k_norm, v_attn_sink,
                                v_w_gate_f, v_b_gate_f, v_w_gate_b, v_b_gate_b, v_gla_norm, v_w_attn_o, v_w_gla_o,
                                v_w_out, v_g_ffn, v_w_up, v_conv_w, v_conv_b, v_w_down)))
    return _step(x, c, ctx, loss_target, W, M, V)
```

```python
import functools
import math

import jax
import jax.numpy as jnp
from jax import lax
from jax.experimental import pallas as pl
from jax.experimental.pallas import tpu as pltpu

F32 = jnp.float32
BF16 = jnp.bfloat16
MESH = pl.DeviceIdType.MESH

EPS = 1e-6
HEAD_DIM = 128
N_Q_HEADS = 16
N_KV_HEADS = 4
Q_PER_KV = N_Q_HEADS // N_KV_HEADS
WINDOW = 128
GLA_HEADS = 4
GLA_LOWRANK = 16
GLA_GATE_NORM = 16.0
GLA_CHUNK = 64
GRID_W = 64
ROPE_THETA = 10000.0
GLA_LEVELS = (32, 16, 8, 4, 2, 1)
LANES = 128
MXU_TILE = 256

ADAM_LR = 0.001
ADAM_B1 = 0.9
ADAM_B2 = 0.999
ADAM_EPS = 1e-08
ADAM_WD = 0.01
ADAM_STEP = 10

VMEM_LIMIT = 52 * 1024 * 1024


def _cparams(*sem):
    return pltpu.CompilerParams(dimension_semantics=sem, vmem_limit_bytes=VMEM_LIMIT)


def _pick(n, target, mult=LANES):
    best = None
    d = mult
    while d <= min(n, target):
        if n % d == 0:
            best = d
        d += mult
    return n if best is None else best


def _sigmoid(x):
    return 1.0 / (1.0 + jnp.exp(-x))


def _silu(x):
    return x * _sigmoid(x)


def _dsilu(x):
    s = _sigmoid(x)
    return s * (1.0 + x * (1.0 - s))


def _dot(a, b, dims):
    return lax.dot_general(a, b, (dims, ((), ())), preferred_element_type=F32)


NN = ((1,), (0,))
NT = ((1,), (1,))
TN = ((0,), (0,))


def _matmul(a, b, mode, out_dtype, name, tm=1024, tn=1024, tk=2048, ride=None, halves=None, col_shards=None):
    if halves == "a":
        assert mode == "nt"
        (_, M, Kh), (N, K2) = a.shape, b.shape
        K = 2 * Kh
    elif halves == "b":
        assert mode == "tn"
        (K, M), (_, K2, Nh) = a.shape, b.shape
        N = 2 * Nh
    elif mode == "nn":
        (M, K), (K2, N) = a.shape, b.shape
    elif mode == "nt":
        (M, K), (N, K2) = a.shape, b.shape
    else:
        (K, M), (K2, N) = a.shape, b.shape
    assert K == K2, (name, a.shape, b.shape)
    pick = lambda n, t: _pick(n, t, MXU_TILE) if n % MXU_TILE == 0 else _pick(n, t)
    tm, tn, tk = pick(M, tm), pick(N // 2 if halves == "b" else N, tn), pick(K // 2 if halves == "a" else K, tk)
    if col_shards is not None:
        tn = N // col_shards
    nk = K // tk
    dims = {"nn": NN, "nt": NT, "tn": TN}[mode]

    def body(a_ref, b_ref, o_ref, acc_ref):
        k = pl.program_id(2)

        @pl.when(k == 0)
        def _():
            acc_ref[...] = jnp.zeros_like(acc_ref)

        av = a_ref[0] if halves == "a" else a_ref[...]
        bv = b_ref[0] if halves == "b" else b_ref[...]
        acc_ref[...] += _dot(av.astype(BF16), bv.astype(BF16), dims)

        @pl.when(k == nk - 1)
        def _():
            o_ref[...] = acc_ref[...].astype(out_dtype).reshape(o_ref.shape)

    if halves == "a":
        per = (K // 2) // tk
        a_spec = pl.BlockSpec((1, tm, tk), lambda i, j, k: (k // per, i, k % per))
    elif mode == "tn":
        a_spec = pl.BlockSpec((tk, tm), lambda i, j, k: (k, i))
    else:
        a_spec = pl.BlockSpec((tm, tk), lambda i, j, k: (i, k))
    if halves == "b":
        per = (N // 2) // tn
        b_spec = pl.BlockSpec((1, tk, tn), lambda i, j, k: (j // per, k, j % per))
    elif mode == "nt":
        b_spec = pl.BlockSpec((tn, tk), lambda i, j, k: (j, k))
    else:
        b_spec = pl.BlockSpec((tk, tn), lambda i, j, k: (k, j))
    if col_shards is None:
        out_spec, out_shape = pl.BlockSpec((tm, tn), lambda i, j, k: (i, j)), (M, N)
    else:
        assert tn * col_shards == N, (name, tn, N)
        out_spec, out_shape = pl.BlockSpec((1, tm, tn), lambda i, j, k: (j, i, 0)), (col_shards, M, tn)
    return _pcall(
        body, name=name, grid=(M // tm, N // tn, nk),
        in_specs=[a_spec, b_spec],
        out_specs=out_spec,
        out_shape=jax.ShapeDtypeStruct(out_shape, out_dtype),
        scratch_shapes=[pltpu.VMEM((tm, tn), F32)],
        sem=("parallel", "parallel", "arbitrary"), args=(a, b), ride=ride)


def _modnorm_fwd(xc, xl, g, sc, sh, name, ride=None):
    (L, D), T = xc.shape, xl.shape[0]
    tm = _pick(math.gcd(L, T), 256, 8)
    cb = L // tm

    def body(xc_ref, xl_ref, g_ref, sc_ref, sh_ref, h_ref):
        x = jnp.where(pl.program_id(0) < cb, xc_ref[...], xl_ref[...])
        r = lax.rsqrt(jnp.mean(x * x, axis=-1, keepdims=True) + EPS)
        n = x * r * g_ref[...]
        h_ref[...] = (n * (1.0 + sc_ref[0]) + sh_ref[0]).astype(BF16)

    sel = lambda i: (jnp.where(i < cb, 0, 1), 0, 0)
    return _pcall(
        body, name=name, grid=((L + T) // tm,),
        in_specs=[pl.BlockSpec((tm, D), lambda i: (jnp.minimum(i, cb - 1), 0)),
                  pl.BlockSpec((tm, D), lambda i: (jnp.maximum(i - cb, 0), 0)),
                  pl.BlockSpec((1, D), lambda i: (0, 0)), pl.BlockSpec((1, 1, D), sel), pl.BlockSpec((1, 1, D), sel)],
        out_specs=pl.BlockSpec((tm, D), lambda i: (i, 0)),
        out_shape=jax.ShapeDtypeStruct((L + T, D), BF16),
        sem=("parallel",), args=(xc, xl, g, sc, sh), ride=ride)


def _modnorm_bwd(x, dh, g, sc, resid, name, dh_roff=0):
    N, D = x.shape
    tm = _pick(math.gcd(N, dh_roff), 256, 8)
    ro = dh_roff // tm
    want_dx = resid is not None

    def body(*refs):
        if want_dx:
            x_ref, dh_ref, g_ref, sc_ref, res_ref, dx_ref, acc_ref = refs
        else:
            x_ref, dh_ref, g_ref, sc_ref, acc_ref = refs
        i = pl.program_id(0)

        @pl.when(i == 0)
        def _():
            acc_ref[...] = jnp.zeros_like(acc_ref)

        xv, dhv, gv = x_ref[...], dh_ref[...], g_ref[...]
        r = lax.rsqrt(jnp.mean(xv * xv, axis=-1, keepdims=True) + EPS)
        xh = xv * r
        dn = dhv * (1.0 + sc_ref[...])
        acc_ref[0:1, :] += jnp.sum(dhv, axis=0, keepdims=True)
        acc_ref[1:2, :] += jnp.sum(dhv * xh * gv, axis=0, keepdims=True)
        acc_ref[2:3, :] += jnp.sum(dn * xh, axis=0, keepdims=True)
        if want_dx:
            dxh = dn * gv
            dx_ref[...] = res_ref[...] + r * (dxh - xh * jnp.mean(dxh * xh, axis=-1, keepdims=True))

    row = pl.BlockSpec((tm, D), lambda i: (i, 0))
    drow = pl.BlockSpec((tm, D), lambda i: (i + ro, 0))
    vec = pl.BlockSpec((1, D), lambda i: (0, 0))
    acc = pl.BlockSpec((8, D), lambda i: (0, 0))
    acc_shape = jax.ShapeDtypeStruct((8, D), F32)
    if want_dx:
        return pl.pallas_call(
            body, name=name, grid=(N // tm,), in_specs=[row, drow, vec, vec, row],
            out_specs=[row, acc], out_shape=[jax.ShapeDtypeStruct((N, D), F32), acc_shape],
            compiler_params=_cparams("arbitrary"))(x, dh, g, sc, resid)
    sums = pl.pallas_call(
        body, name=name, grid=(N // tm,), in_specs=[row, drow, vec, vec],
        out_specs=acc, out_shape=acc_shape, compiler_params=_cparams("arbitrary"))(x, dh, g, sc)
    return None, sums


def _resnorm_bwd(x1, dh, g, sc, dy, mix, gt, name):
    N, D = x1.shape
    tm = _pick(N, 256, 8)

    def body(x_ref, dh_ref, g_ref, sc_ref, dy_ref, mix_ref, gt_ref, dx_ref, dm_ref, acc_ref):
        i = pl.program_id(0)

        @pl.when(i == 0)
        def _():
            acc_ref[...] = jnp.zeros_like(acc_ref)

        xv, dhv, gv = x_ref[...], dh_ref[...], g_ref[...]
        r = lax.rsqrt(jnp.mean(xv * xv, axis=-1, keepdims=True) + EPS)
        xh = xv * r
        dn = dhv * (1.0 + sc_ref[...])
        dxh = dn * gv
        dx = dy_ref[...] + r * (dxh - xh * jnp.mean(dxh * xh, axis=-1, keepdims=True))
        dx_ref[...] = dx
        dm_ref[...] = (dx * gt_ref[...]).astype(BF16)
        acc_ref[0:1, :] += jnp.sum(dhv, axis=0, keepdims=True)
        acc_ref[1:2, :] += jnp.sum(dhv * xh * gv, axis=0, keepdims=True)
        acc_ref[2:3, :] += jnp.sum(dn * xh, axis=0, keepdims=True)
        acc_ref[3:4, :] += jnp.sum(dx * mix_ref[...], axis=0, keepdims=True)

    row = pl.BlockSpec((tm, D), lambda i: (i, 0))
    vec = pl.BlockSpec((1, D), lambda i: (0, 0))
    return pl.pallas_call(
        body, name=name, grid=(N // tm,), in_specs=[row, row, vec, vec, row, row, vec],
        out_specs=[row, row, pl.BlockSpec((8, D), lambda i: (0, 0))],
        out_shape=[jax.ShapeDtypeStruct((N, D), F32), jax.ShapeDtypeStruct((N, D), BF16),
                   jax.ShapeDtypeStruct((8, D), F32)],
        compiler_params=_cparams("arbitrary"))(x1, dh, g, sc, dy, mix, gt)


def _qknorm_fwd(z, cblk, nrows, roff, w, cos2, sin2, nh, name):
    W = nh * HEAD_DIM
    tm = _pick(math.gcd(nrows, roff), 256, 8)
    ro = roff // tm
    assert roff % tm == 0

    def body(z_ref, w_ref, c_ref, s_ref, o_ref):
        c, s, wv = c_ref[...], s_ref[...], w_ref[...]
        for h in range(nh):
            x = z_ref[:, h * HEAD_DIM:(h + 1) * HEAD_DIM]
            r = lax.rsqrt(jnp.mean(x * x, axis=-1, keepdims=True) + EPS)
            y = x * r * wv
            o_ref[:, h * HEAD_DIM:(h + 1) * HEAD_DIM] = (y * c + pltpu.roll(y, HEAD_DIM // 2, 1) * s).astype(BF16)

    return pl.pallas_call(
        body, name=name, grid=(nrows // tm,),
        in_specs=[pl.BlockSpec((tm, W), lambda i: (i + ro, cblk)), pl.BlockSpec((1, HEAD_DIM), lambda i: (0, 0)),
                  pl.BlockSpec((tm, HEAD_DIM), lambda i: (i + ro, 0)), pl.BlockSpec((tm, HEAD_DIM), lambda i: (i + ro, 0))],
        out_specs=pl.BlockSpec((tm, W), lambda i: (i, 0)),
        out_shape=jax.ShapeDtypeStruct((nrows, W), BF16),
        compiler_params=_cparams("parallel"),
    )(z, w, cos2, sin2)


def _qknorm_bwd(z, cblk, nrows, roff, w, cos2, sin2, dy, nh, name):
    W = nh * HEAD_DIM
    tm = _pick(math.gcd(nrows, roff), 256, 8)
    ro = roff // tm

    def body(z_ref, w_ref, c_ref, s_ref, dy_ref, dz_ref, acc_ref):
        i = pl.program_id(0)

        @pl.when(i == 0)
        def _():
            acc_ref[...] = jnp.zeros_like(acc_ref)

        c, s, wv = c_ref[...], s_ref[...], w_ref[...]
        dw = jnp.zeros((1, HEAD_DIM), F32)
        for h in range(nh):
            sl = slice(h * HEAD_DIM, (h + 1) * HEAD_DIM)
            x = z_ref[:, sl]
            d = dy_ref[:, sl]
            dyn = d * c + pltpu.roll(d * s, HEAD_DIM // 2, 1)
            r = lax.rsqrt(jnp.mean(x * x, axis=-1, keepdims=True) + EPS)
            xh = x * r
            dw = dw + jnp.sum(dyn * xh, axis=0, keepdims=True)
            dxh = dyn * wv
            dz_ref[:, sl] = (r * (dxh - xh * jnp.mean(dxh * xh, axis=-1, keepdims=True))).astype(BF16)
        acc_ref[0:1, :] += dw

    return pl.pallas_call(
        body, name=name, grid=(nrows // tm,),
        in_specs=[pl.BlockSpec((tm, W), lambda i: (i + ro, cblk)), pl.BlockSpec((1, HEAD_DIM), lambda i: (0, 0)),
                  pl.BlockSpec((tm, HEAD_DIM), lambda i: (i + ro, 0)), pl.BlockSpec((tm, HEAD_DIM), lambda i: (i + ro, 0)),
                  pl.BlockSpec((tm, W), lambda i: (i, 0))],
        out_specs=[pl.BlockSpec((tm, W), lambda i: (i, 0)), pl.BlockSpec((8, HEAD_DIM), lambda i: (0, 0))],
        out_shape=[jax.ShapeDtypeStruct((nrows, W), BF16), jax.ShapeDtypeStruct((8, HEAD_DIM), F32)],
        compiler_params=_cparams("arbitrary"),
    )(z, w, cos2, sin2, dy)


def _cast_seg(z, cblk, width, name):
    R = z.shape[0]
    tm = _pick(R, 512, 8)

    def body(z_ref, o_ref):
        o_ref[...] = z_ref[...].astype(BF16)

    return pl.pallas_call(
        body, name=name, grid=(R // tm,),
        in_specs=[pl.BlockSpec((tm, width), lambda i: (i, cblk))],
        out_specs=pl.BlockSpec((tm, width), lambda i: (i, 0)),
        out_shape=jax.ShapeDtypeStruct((R, width), BF16), compiler_params=_cparams("parallel"))(z)


NEG_BIG = -1e30


KV_PER_STEP = 2


def _attn_specs(T, n_ctx):
    nb = T // WINDOW
    lb = n_ctx // WINDOW
    kvw = KV_PER_STEP * HEAD_DIM
    blk = lambda f: pl.BlockSpec((WINDOW, kvw), f)
    win = [blk(lambda h, i: (lb + jnp.maximum(i - 1, 0), h)), blk(lambda h, i: (lb + i, h)),
           blk(lambda h, i: (lb + jnp.minimum(i + 1, nb - 1), h))]
    ctx = pl.BlockSpec((n_ctx, kvw), lambda h, i: (0, h))
    qspec = pl.BlockSpec((WINDOW, KV_PER_STEP * Q_PER_KV * HEAD_DIM), lambda h, i: (i, h))
    sink = pl.BlockSpec((N_Q_HEADS, HEAD_DIM), lambda h, i: (0, 0))
    return nb, qspec, win, ctx, sink


def _attn_probs(q, kw, kctx, snk, valid):
    scale = HEAD_DIM ** -0.5
    s_lat = jnp.where(valid, _dot(q, kw, NT) * scale, NEG_BIG)
    s_ctx = _dot(q, kctx, NT) * scale
    m = jnp.maximum(jnp.maximum(jnp.max(s_lat, axis=-1, keepdims=True), jnp.max(s_ctx, axis=-1, keepdims=True)), snk)
    p_lat = jnp.exp(s_lat - m)
    p_ctx = jnp.exp(s_ctx - m)
    p_snk = jnp.exp(snk - m)
    den = p_snk + jnp.sum(p_lat, axis=-1, keepdims=True) + jnp.sum(p_ctx, axis=-1, keepdims=True)
    return p_lat, p_ctx, p_snk, den


def _attn_valid(i, T, heads):
    rows = heads * WINDOW
    qpos = i * WINDOW + (lax.broadcasted_iota(jnp.int32, (rows, 3 * WINDOW), 0) & (WINDOW - 1))
    kpos = (i - 1) * WINDOW + lax.broadcasted_iota(jnp.int32, (rows, 3 * WINDOW), 1)
    return (jnp.abs(qpos - kpos) <= WINDOW) & (kpos >= 0) & (kpos < T)


def _stack_heads(ref, hh):
    c0 = hh * Q_PER_KV * HEAD_DIM
    return jnp.concatenate([ref[:, c0 + g * HEAD_DIM:c0 + (g + 1) * HEAD_DIM] for g in range(Q_PER_KV)], axis=0)


def _stack_sinks(sink_ref, kvh):
    return jnp.concatenate([jnp.broadcast_to(sink_ref[pl.ds(kvh * Q_PER_KV + g, 1), :][:, 0:1], (WINDOW, 1))
                            for g in range(Q_PER_KV)], axis=0)


def _attn_window(refs, hh):
    return jnp.concatenate([r[:, hh * HEAD_DIM:(hh + 1) * HEAD_DIM] for r in refs], axis=0)


def _attn_fwd(qn, kn, vb, sink_rows, n_ctx, name, ride=None):
    T = qn.shape[0]
    nb, qspec, win, ctx, sink = _attn_specs(T, n_ctx)

    def body(q_ref, kp, kc, kx, vp, vc, vx, kctx_ref, vctx_ref, sink_ref, o_ref):
        h, i = pl.program_id(0), pl.program_id(1)
        valid = _attn_valid(i, T, Q_PER_KV)
        for hh in range(KV_PER_STEP):
            sl = slice(hh * HEAD_DIM, (hh + 1) * HEAD_DIM)
            kw, vw = _attn_window((kp, kc, kx), hh), _attn_window((vp, vc, vx), hh)
            kctx, vctx = kctx_ref[:, sl], vctx_ref[:, sl]
            p_lat, p_ctx, _, den = _attn_probs(_stack_heads(q_ref, hh), kw, kctx,
                                               _stack_sinks(sink_ref, h * KV_PER_STEP + hh), valid)
            o = ((_dot(p_lat.astype(BF16), vw, NN) + _dot(p_ctx.astype(BF16), vctx, NN)) / den).astype(BF16)
            for g in range(Q_PER_KV):
                c0 = (hh * Q_PER_KV + g) * HEAD_DIM
                o_ref[:, c0:c0 + HEAD_DIM] = o[g * WINDOW:(g + 1) * WINDOW]

    return _pcall(
        body, name=name, grid=(N_KV_HEADS // KV_PER_STEP, nb),
        in_specs=[qspec] + win + win + [ctx, ctx, sink],
        out_specs=qspec, out_shape=jax.ShapeDtypeStruct(qn.shape, BF16),
        sem=("parallel", "parallel"), args=(qn, kn, kn, kn, vb, vb, vb, kn, vb, sink_rows), ride=ride)


def _attn_bwd(qn, kn, vb, sink_rows, do, n_ctx, name, ride=None):
    T = qn.shape[0]
    nb, qspec, win, ctx, sink = _attn_specs(T, n_ctx)
    scale = HEAD_DIM ** -0.5
    TP = T + 2 * WINDOW

    def body(q_ref, kp, kc, kx, vp, vc, vx, kctx_ref, vctx_ref, sink_ref, do_ref,
             dq_ref, dkw_ref, dvw_ref, dkc_ref, dvc_ref, dsn_ref):
        h, i = pl.program_id(0), pl.program_id(1)

        @pl.when(i == 0)
        def _():
            dkw_ref[...] = jnp.zeros_like(dkw_ref)
            dvw_ref[...] = jnp.zeros_like(dvw_ref)
            dkc_ref[...] = jnp.zeros_like(dkc_ref)
            dvc_ref[...] = jnp.zeros_like(dvc_ref)
            dsn_ref[...] = jnp.zeros_like(dsn_ref)

        lane = lax.broadcasted_iota(jnp.int32, (8, HEAD_DIM), 1)
        valid = _attn_valid(i, T, Q_PER_KV)
        rows = pl.ds(pl.multiple_of(i * WINDOW, WINDOW), 3 * WINDOW)
        for hh in range(KV_PER_STEP):
            sl = slice(hh * HEAD_DIM, (hh + 1) * HEAD_DIM)
            kw, vw = _attn_window((kp, kc, kx), hh), _attn_window((vp, vc, vx), hh)
            kctx, vctx = kctx_ref[:, sl], vctx_ref[:, sl]
            q, d_o = _stack_heads(q_ref, hh), _stack_heads(do_ref, hh)
            p_lat, p_ctx, p_snk, den = _attn_probs(q, kw, kctx, _stack_sinks(sink_ref, h * KV_PER_STEP + hh), valid)
            inv = 1.0 / den
            p_lat, p_ctx, p_snk = p_lat * inv, p_ctx * inv, p_snk * inv
            dp_lat = _dot(d_o, vw, NT)
            dp_ctx = _dot(d_o, vctx, NT)
            dr = jnp.sum(p_lat * dp_lat, axis=-1, keepdims=True) + jnp.sum(p_ctx * dp_ctx, axis=-1, keepdims=True)
            ds_lat = (p_lat * (dp_lat - dr) * scale).astype(BF16)
            ds_ctx = (p_ctx * (dp_ctx - dr) * scale).astype(BF16)
            dq = _dot(ds_lat, kw, NN) + _dot(ds_ctx, kctx, NN)
            snk_terms = p_snk * dr
            dsn = jnp.zeros((8, HEAD_DIM), F32)
            for g in range(Q_PER_KV):
                c0 = (hh * Q_PER_KV + g) * HEAD_DIM
                dq_ref[:, c0:c0 + HEAD_DIM] = dq[g * WINDOW:(g + 1) * WINDOW]
                dsn = dsn + jnp.where(lane == g, -jnp.sum(snk_terms[g * WINDOW:(g + 1) * WINDOW], axis=0, keepdims=True),
                                      0.0)
            dkw_ref[rows, sl] += _dot(ds_lat, q, TN)
            dvw_ref[rows, sl] += _dot(p_lat.astype(BF16), d_o, TN)
            dkc_ref[:, sl] += _dot(ds_ctx, q, TN)
            dvc_ref[:, sl] += _dot(p_ctx.astype(BF16), d_o, TN)
            dsn_ref[hh] += dsn

    wacc = pl.BlockSpec((TP, KV_PER_STEP * HEAD_DIM), lambda h, i: (0, h))
    return _pcall(
        body, name=name, grid=(N_KV_HEADS // KV_PER_STEP, nb),
        in_specs=[qspec] + win + win + [ctx, ctx, sink, qspec],
        out_specs=[qspec, wacc, wacc, ctx, ctx, pl.BlockSpec((KV_PER_STEP, 8, HEAD_DIM), lambda h, i: (h, 0, 0))],
        out_shape=[jax.ShapeDtypeStruct(qn.shape, F32),
                   jax.ShapeDtypeStruct((TP, N_KV_HEADS * HEAD_DIM), F32),
                   jax.ShapeDtypeStruct((TP, N_KV_HEADS * HEAD_DIM), F32),
                   jax.ShapeDtypeStruct((n_ctx, N_KV_HEADS * HEAD_DIM), F32),
                   jax.ShapeDtypeStruct((n_ctx, N_KV_HEADS * HEAD_DIM), F32),
                   jax.ShapeDtypeStruct((N_KV_HEADS, 8, HEAD_DIM), F32)],
        sem=("arbitrary", "arbitrary"), args=(qn, kn, kn, kn, vb, vb, vb, kn, vb, sink_rows, do), ride=ride)


def _gla_masks(dirv):
    C = GLA_CHUNK

    def times(reps):
        r = lax.broadcasted_iota(jnp.int32, (C, reps * C), 0)
        c = lax.broadcasted_iota(jnp.int32, (C, reps * C), 1) & (C - 1)
        return jnp.where(dirv == 0, r, C - 1 - r), jnp.where(dirv == 0, c, C - 1 - c)

    def level(tt, ss, m):
        sh = m.bit_length() - 1
        same = (tt >> (sh + 1)) == (ss >> (sh + 1))
        return same, (tt >> sh) & 1, (ss >> sh) & 1

    tt, ss = times(3)
    le = (ss <= tt).astype(jnp.int32)
    sums = [le == 1]
    for m in GLA_LEVELS:
        same, ut, us = level(tt, ss, m)
        sums.append(same & (ut == us) & (ut == le))
    tt, ss = times(1)
    blocks = [ss == tt]
    for m in GLA_LEVELS:
        same, ut, us = level(tt, ss, m)
        blocks.append(same & (ut == 1) & (us == 0))
    mall3 = jnp.concatenate([jnp.where(s, 1.0, 0.0) for s in sums], axis=0).astype(BF16)
    return mall3, blocks


def _pieces(x):
    hi = x.astype(BF16)
    r1 = x - hi.astype(F32)
    mid = r1.astype(BF16)
    return hi, mid, (r1 - mid.astype(F32)).astype(BF16)


def _sum_f32(mall3, x):
    return _dot(mall3, jnp.concatenate(_pieces(x), axis=0), NN)


def _sum_f32_t(mall3, x):
    m = mall3[:, 0:GLA_CHUNK]
    hi, mid, lo = _pieces(x)
    return _dot(m, hi, TN) + _dot(m, mid, TN) + _dot(m, lo, TN)


def _gla_chunk_of(dirv, j, lc, nc):
    return jnp.where(dirv == 0, j, jnp.where(j < lc, lc - 1 - j, nc + lc - 1 - j))


def _gla_gate(lr_ref, wg_ref, bg_ref):
    pre = _dot(lr_ref[...].astype(BF16), wg_ref[0].astype(BF16), NN) + bg_ref[0]
    g = (jnp.minimum(pre, 0.0) - jnp.log(1.0 + jnp.exp(-jnp.abs(pre)))) * (1.0 / GLA_GATE_NORM)
    return pre, g


def _gla_fwd(z, qblk, kblk, vblk, lrblk, wg, bg, DV, n_ctx, name):
    R = z.shape[0]
    C = GLA_CHUNK
    DK = wg.shape[2] // GLA_HEADS
    nc, lc = R // C, n_ctx // C
    qscale = DK ** -0.5

    GK, GV = GLA_HEADS * DK, GLA_HEADS * DV

    def body(q_ref, k_ref, v_ref, lr_ref, wg_ref, bg_ref, o_ref, sp_ref, st_ref):
        dirv, j = pl.program_id(0), pl.program_id(1)

        @pl.when(j == 0)
        def _():
            st_ref[...] = jnp.zeros_like(st_ref)

        mall, blocks = _gla_masks(dirv)
        _, g_all = _gla_gate(lr_ref, wg_ref, bg_ref)
        E_all = _sum_f32(mall, g_all)
        for h in range(GLA_HEADS):
            ks, vs = slice(h * DK, (h + 1) * DK), slice(h * DV, (h + 1) * DV)
            q, k, v = q_ref[:, ks] * qscale, k_ref[:, ks], v_ref[:, vs].astype(BF16)
            g, E = g_all[:, ks], E_all[:, ks]
            st = st_ref[h]
            sp_ref[0, h, 0] = st
            A = jnp.where(blocks[0], _dot(q.astype(BF16), k.astype(BF16), NT), 0.0)
            for l in range(len(GLA_LEVELS)):
                e = jnp.exp(E[(1 + l) * C:(2 + l) * C])
                A = A + jnp.where(blocks[l + 1], _dot((q * e).astype(BF16), (k * e).astype(BF16), NT), 0.0)
            o_ref[0, :, vs] = (_dot((q * jnp.exp(E[0:C])).astype(BF16), st.astype(BF16), NT)
                               + _dot(A.astype(BF16), v, NN))
            last = jnp.sum(g, axis=0, keepdims=True)
            st_ref[h] = jnp.exp(last) * st + _dot(v, (k * jnp.exp(last - E[0:C])).astype(BF16), TN)

    chunk = functools.partial(_gla_chunk_of, lc=lc, nc=nc)
    return pl.pallas_call(
        body, name=name, grid=(2, nc),
        in_specs=[pl.BlockSpec((C, GK), lambda d, j: (chunk(d, j), qblk)),
                  pl.BlockSpec((C, GK), lambda d, j: (chunk(d, j), kblk)),
                  pl.BlockSpec((C, GV), lambda d, j: (chunk(d, j), vblk)),
                  pl.BlockSpec((C, LANES), lambda d, j: (chunk(d, j), lrblk)),
                  pl.BlockSpec((1, LANES, GK), lambda d, j: (d, 0, 0)),
                  pl.BlockSpec((1, 1, GK), lambda d, j: (d, 0, 0))],
        out_specs=[pl.BlockSpec((1, C, GV), lambda d, j: (d, chunk(d, j), 0)),
                   pl.BlockSpec((1, GLA_HEADS, 1, DV, DK), lambda d, j: (d, 0, j, 0, 0))],
        out_shape=[jax.ShapeDtypeStruct((2, R, GV), F32),
                   jax.ShapeDtypeStruct((2, GLA_HEADS, nc, DV, DK), F32)],
        scratch_shapes=[pltpu.VMEM((GLA_HEADS, DV, DK), F32)],
        compiler_params=_cparams("parallel", "arbitrary"),
    )(z, z, z, z, wg, bg)


def _gla_bwd(z, qblk, kblk, vblk, lrblk, wg, bg, sprev, do, n_ctx, name, ride=None):
    R = z.shape[0]
    C = GLA_CHUNK
    DK, DV = wg.shape[2] // GLA_HEADS, do.shape[1] // GLA_HEADS
    nc, lc = R // C, n_ctx // C
    qscale = DK ** -0.5
    nl = len(GLA_LEVELS)

    GK, GV = GLA_HEADS * DK, GLA_HEADS * DV

    def body(q_ref, k_ref, v_ref, lr_ref, wg_ref, bg_ref, sp_ref, do_ref,
             dq_ref, dk_ref, dv_ref, dpre_ref, dbg_ref, dst_ref):
        dirv, jr = pl.program_id(0), pl.program_id(1)

        @pl.when(jr == 0)
        def _():
            dst_ref[...] = jnp.zeros_like(dst_ref)
            dbg_ref[...] = jnp.zeros_like(dbg_ref)

        mall, blocks = _gla_masks(dirv)
        pre_all, g_all = _gla_gate(lr_ref, wg_ref, bg_ref)
        E_all = _sum_f32(mall, g_all)
        for h in range(GLA_HEADS):
            ks, vs = slice(h * DK, (h + 1) * DK), slice(h * DV, (h + 1) * DV)
            q, k, v = q_ref[:, ks] * qscale, k_ref[:, ks], v_ref[:, vs].astype(BF16)
            pre, g, E = pre_all[:, ks], g_all[:, ks], E_all[:, ks]
            last = jnp.sum(g, axis=0, keepdims=True)
            eb, er, decay = jnp.exp(E[0:C]), jnp.exp(last - E[0:C]), jnp.exp(last)
            st = sp_ref[0, h, 0]
            dst = dst_ref[h]
            d_o = do_ref[:, vs]
            qe, kd = q * eb, k * er
            qb, kb = q.astype(BF16), k.astype(BF16)
            A = jnp.where(blocks[0], _dot(qb, kb, NT), 0.0)
            levels = []
            for l in range(nl):
                e = jnp.exp(E[(1 + l) * C:(2 + l) * C])
                ql, kl = q * e, k * e
                levels.append((e, ql, kl, ql.astype(BF16), kl.astype(BF16)))
                A = A + jnp.where(blocks[l + 1], _dot(levels[l][3], levels[l][4], NT), 0.0)
            dA = _dot(d_o, v, NT)
            dv_ref[0, :, vs] = _dot(A.astype(BF16), d_o, TN) + _dot(kd.astype(BF16), dst.astype(BF16), NT)
            dqe = _dot(d_o, st.astype(BF16), NN)
            dkd = _dot(v, dst.astype(BF16), NN)
            G = jnp.where(blocks[0], dA, 0.0).astype(BF16)
            dq = dqe * eb + _dot(G, kb, NN)
            dk = dkd * er + _dot(G, qb, TN)
            dEr = dkd * kd
            dE = [dqe * qe - dEr]
            for l in range(nl):
                e, ql, kl, qlb, klb = levels[l]
                G = jnp.where(blocks[l + 1], dA, 0.0).astype(BF16)
                dql = _dot(G, klb, NN)
                dkl = _dot(G, qlb, TN)
                dq = dq + dql * e
                dk = dk + dkl * e
                dE.append(dql * ql + dkl * kl)
            dlast = jnp.sum(dst * st, axis=0, keepdims=True) * decay + jnp.sum(dEr, axis=0, keepdims=True)
            dg = _sum_f32_t(mall, jnp.concatenate(dE, axis=0)) + dlast
            dpre = dg * (1.0 / GLA_GATE_NORM) / (1.0 + jnp.exp(pre))
            dq_ref[0, :, ks] = dq * qscale
            dk_ref[0, :, ks] = dk
            dpre_ref[:, ks] = dpre.astype(BF16)
            dbg_ref[0, :, ks] += jnp.sum(dpre, axis=0, keepdims=True)
            dst_ref[h] = decay * dst + _dot(d_o, qe.astype(BF16), TN)

    def chunk(d, jr):
        return _gla_chunk_of(d, nc - 1 - jr, lc, nc)

    return _pcall(
        body, name=name, grid=(2, nc),
        in_specs=[pl.BlockSpec((C, GK), lambda d, j: (chunk(d, j), qblk)),
                  pl.BlockSpec((C, GK), lambda d, j: (chunk(d, j), kblk)),
                  pl.BlockSpec((C, GV), lambda d, j: (chunk(d, j), vblk)),
                  pl.BlockSpec((C, LANES), lambda d, j: (chunk(d, j), lrblk)),
                  pl.BlockSpec((1, LANES, GK), lambda d, j: (d, 0, 0)),
                  pl.BlockSpec((1, 1, GK), lambda d, j: (d, 0, 0)),
                  pl.BlockSpec((1, GLA_HEADS, 1, DV, DK), lambda d, j: (d, 0, nc - 1 - j, 0, 0)),
                  pl.BlockSpec((C, GV), lambda d, j: (chunk(d, j), 0))],
        out_specs=[pl.BlockSpec((1, C, GK), lambda d, j: (d, chunk(d, j), 0)),
                   pl.BlockSpec((1, C, GK), lambda d, j: (d, chunk(d, j), 0)),
                   pl.BlockSpec((1, C, GV), lambda d, j: (d, chunk(d, j), 0)),
                   pl.BlockSpec((C, GK), lambda d, j: (chunk(d, j), d)),
                   pl.BlockSpec((1, 1, GK), lambda d, j: (d, 0, 0))],
        out_shape=[jax.ShapeDtypeStruct((2, R, GK), F32),
                   jax.ShapeDtypeStruct((2, R, GK), F32),
                   jax.ShapeDtypeStruct((2, R, GV), F32),
                   jax.ShapeDtypeStruct((R, 2 * GK), BF16),
                   jax.ShapeDtypeStruct((2, 1, GK), F32)],
        scratch_shapes=[pltpu.VMEM((GLA_HEADS, DV, DK), F32)],
        sem=("arbitrary", "arbitrary"), args=(z, z, z, z, wg, bg, sprev, do), ride=ride)


def _glanorm_fwd(o, z, rbblk, gn, n_ctx, name):
    _, R, GV = o.shape
    T = R - n_ctx
    DV = GV // GLA_HEADS
    tm = _pick(n_ctx, 256, 8)
    ro = n_ctx // tm

    def body(o0_ref, o1_ref, rb_ref, gn_ref, p_ref):
        gnv = gn_ref[...]
        for h in range(GLA_HEADS):
            sl = slice(h * DV, (h + 1) * DV)
            og = o0_ref[0, :, sl] + o1_ref[0, :, sl]
            r = lax.rsqrt(jnp.mean(og * og, axis=-1, keepdims=True) + EPS)
            p_ref[:, sl] = (og * r * gnv * _silu(rb_ref[:, sl])).astype(BF16)

    return pl.pallas_call(
        body, name=name, grid=(T // tm,),
        in_specs=[pl.BlockSpec((1, tm, GV), lambda i: (0, i + ro, 0)), pl.BlockSpec((1, tm, GV), lambda i: (1, i + ro, 0)),
                  pl.BlockSpec((tm, GV), lambda i: (i + ro, rbblk)), pl.BlockSpec((1, DV), lambda i: (0, 0))],
        out_specs=pl.BlockSpec((tm, GV), lambda i: (i, 0)),
        out_shape=jax.ShapeDtypeStruct((T, GV), BF16), compiler_params=_cparams("parallel"))(o, o, z, gn)


def _glanorm_bwd(o, z, rbblk, gn, dp, n_ctx, name):
    _, R, GV = o.shape
    T = R - n_ctx
    DV = GV // GLA_HEADS
    tm = _pick(n_ctx, 256, 8)
    ro = n_ctx // tm

    def body(o0_ref, o1_ref, rb_ref, gn_ref, dp_ref, do_ref, drb_ref, acc_ref):
        i = pl.program_id(0)

        @pl.when(i == 0)
        def _():
            acc_ref[...] = jnp.zeros_like(acc_ref)

        gnv = gn_ref[...]
        dgn = jnp.zeros((1, DV), F32)
        for h in range(GLA_HEADS):
            sl = slice(h * DV, (h + 1) * DV)
            og = o0_ref[0, :, sl] + o1_ref[0, :, sl]
            rb = rb_ref[:, sl]
            d = dp_ref[:, sl]
            r = lax.rsqrt(jnp.mean(og * og, axis=-1, keepdims=True) + EPS)
            xh = og * r
            drb_ref[:, sl] = (d * xh * gnv * _dsilu(rb)).astype(BF16)
            dn = d * _silu(rb)
            dgn = dgn + jnp.sum(dn * xh, axis=0, keepdims=True)
            dxh = dn * gnv
            do_ref[:, sl] = (r * (dxh - xh * jnp.mean(dxh * xh, axis=-1, keepdims=True))).astype(BF16)
        acc_ref[0:1, :] += dgn

    row = pl.BlockSpec((tm, GV), lambda i: (i, 0))
    return pl.pallas_call(
        body, name=name, grid=(T // tm,),
        in_specs=[pl.BlockSpec((1, tm, GV), lambda i: (0, i + ro, 0)), pl.BlockSpec((1, tm, GV), lambda i: (1, i + ro, 0)),
                  pl.BlockSpec((tm, GV), lambda i: (i + ro, rbblk)), pl.BlockSpec((1, DV), lambda i: (0, 0)), row],
        out_specs=[row, row, pl.BlockSpec((8, DV), lambda i: (0, 0))],
        out_shape=[jax.ShapeDtypeStruct((T, GV), BF16), jax.ShapeDtypeStruct((T, GV), BF16),
                   jax.ShapeDtypeStruct((8, DV), F32)],
        compiler_params=_cparams("arbitrary"))(o, o, z, gn, dp)


def _gate_fwd(z, gablk, gbblk, ya, yg, n_ctx, name):
    T, D = ya.shape
    tm = _pick(n_ctx, 256, 8)
    ro = n_ctx // tm

    def body(ga_ref, gb_ref, ya_ref, yg_ref, m_ref):
        m_ref[...] = (_sigmoid(ga_ref[...]) * ya_ref[...] + _sigmoid(gb_ref[...]) * yg_ref[...]).astype(BF16)

    row = pl.BlockSpec((tm, D), lambda i: (i, 0))
    return pl.pallas_call(
        body, name=name, grid=(T // tm,),
        in_specs=[pl.BlockSpec((tm, D), lambda i: (i + ro, gablk)), pl.BlockSpec((tm, D), lambda i: (i + ro, gbblk)), row, row],
        out_specs=row, out_shape=jax.ShapeDtypeStruct((T, D), BF16), compiler_params=_cparams("parallel"))(z, z, ya, yg)


def _gate_bwd(z, gablk, gbblk, ya, yg, dm, n_ctx, name):
    T, D = ya.shape
    tm = _pick(n_ctx, 256, 8)
    ro = n_ctx // tm

    def body(ga_ref, gb_ref, ya_ref, yg_ref, dm_ref, dya_ref, dyg_ref, dga_ref, dgb_ref):
        d = dm_ref[...]
        sa, sb = _sigmoid(ga_ref[...]), _sigmoid(gb_ref[...])
        dya_ref[...] = (d * sa).astype(BF16)
        dyg_ref[...] = (d * sb).astype(BF16)
        dga_ref[...] = (d * ya_ref[...] * sa * (1.0 - sa)).astype(BF16)
        dgb_ref[...] = (d * yg_ref[...] * sb * (1.0 - sb)).astype(BF16)

    row = pl.BlockSpec((tm, D), lambda i: (i, 0))
    sh = jax.ShapeDtypeStruct((T, D), BF16)
    return pl.pallas_call(
        body, name=name, grid=(T // tm,),
        in_specs=[pl.BlockSpec((tm, D), lambda i: (i + ro, gablk)), pl.BlockSpec((tm, D), lambda i: (i + ro, gbblk)), row, row, row],
        out_specs=[row] * 4, out_shape=[sh] * 4, compiler_params=_cparams("parallel"))(z, z, ya, yg, dm)


def _resnorm_fwd(x, mix, gt, g, sc, sh, name):
    T, D = x.shape
    tm = _pick(T, 256, 8)

    def body(x_ref, mix_ref, gt_ref, g_ref, sc_ref, sh_ref, x1_ref, h_ref):
        x1 = x_ref[...] + gt_ref[...] * mix_ref[...]
        x1_ref[...] = x1
        r = lax.rsqrt(jnp.mean(x1 * x1, axis=-1, keepdims=True) + EPS)
        h_ref[...] = (x1 * r * g_ref[...] * (1.0 + sc_ref[...]) + sh_ref[...]).astype(BF16)

    row = pl.BlockSpec((tm, D), lambda i: (i, 0))
    vec = pl.BlockSpec((1, D), lambda i: (0, 0))
    return pl.pallas_call(
        body, name=name, grid=(T // tm,), in_specs=[row, row, vec, vec, vec, vec], out_specs=[row, row],
        out_shape=[jax.ShapeDtypeStruct((T, D), F32), jax.ShapeDtypeStruct((T, D), BF16)],
        compiler_params=_cparams("parallel"))(x, mix, gt, g, sc, sh)


def _loss_head(d, x1, gt, target, name):
    T, D = d.shape
    tm = _pick(T, 256, 8)

    def body(d_ref, x1_ref, gt_ref, t_ref, dy_ref, dd_ref, acc_ref):
        i = pl.program_id(0)

        @pl.when(i == 0)
        def _():
            acc_ref[...] = jnp.zeros_like(acc_ref)

        dv, gtv = d_ref[...], gt_ref[...]
        e = x1_ref[...] + gtv * dv - t_ref[...]
        dy = e * (1.0 / D)
        dy_ref[...] = dy
        dd_ref[...] = (dy * gtv).astype(BF16)
        acc_ref[0:1, :] += jnp.sum(e * e, axis=0, keepdims=True)
        acc_ref[1:2, :] += jnp.sum(dy * dv, axis=0, keepdims=True)

    row = pl.BlockSpec((tm, D), lambda i: (i, 0))
    return pl.pallas_call(
        body, name=name, grid=(T // tm,), in_specs=[row, row, pl.BlockSpec((1, D), lambda i: (0, 0)), row],
        out_specs=[row, row, pl.BlockSpec((8, D), lambda i: (0, 0))],
        out_shape=[jax.ShapeDtypeStruct((T, D), F32), jax.ShapeDtypeStruct((T, D), BF16),
                   jax.ShapeDtypeStruct((8, D), F32)],
        compiler_params=_cparams("arbitrary"))(d, x1, gt, target)


def _halo_specs(T, tm, tw, col_of, order):
    n8 = tm // 8
    if order == "ij":
        mid = lambda i, j: (i, col_of(j))
        prev = lambda i, j: (jnp.maximum(i * n8 - 1, 0), col_of(j))
        nxt = lambda i, j: (jnp.minimum((i + 1) * n8, T // 8 - 1), col_of(j))
    else:
        mid = lambda j, i: (i, col_of(j))
        prev = lambda j, i: (jnp.maximum(i * n8 - 1, 0), col_of(j))
        nxt = lambda j, i: (jnp.minimum((i + 1) * n8, T // 8 - 1), col_of(j))
    return [pl.BlockSpec((tm, tw), mid), pl.BlockSpec((8, tw), prev), pl.BlockSpec((8, tw), nxt)]


def _shift_rows(x, before, after):
    tm = x.shape[0]
    row = lax.broadcasted_iota(jnp.int32, x.shape, 0)
    return (jnp.where(row == 0, before, pltpu.roll(x, 1, 0)),
            jnp.where(row == tm - 1, after, pltpu.roll(x, tm - 1, 0)))


def _conv_fwd(u, cw, cb, name):
    T, F2 = u.shape
    F = F2 // 2
    tm, tw = _pick(T, 256, 8), _pick(F, 512)
    nt, nw = T // tm, F // tw

    def body(ua, uap, uan, ug, ugp, ugn, cwa, cwg, cba, cbg, f_ref):
        i = pl.program_id(0)
        first, last = i == 0, i == nt - 1

        def conv(u_ref, up_ref, un_ref, w_ref, b_ref):
            m = u_ref[...]
            p, n = _shift_rows(m, jnp.where(first, 0.0, up_ref[7:8, :]), jnp.where(last, 0.0, un_ref[0:1, :]))
            return p * w_ref[0:1, :] + m * w_ref[1:2, :] + n * w_ref[2:3, :] + b_ref[...]

        a = conv(ua, uap, uan, cwa, cba)
        g = conv(ug, ugp, ugn, cwg, cbg)
        f_ref[...] = (_silu(a) * g).astype(BF16)

    wspec = lambda off: pl.BlockSpec((3, tw), lambda i, j: (0, j + off))
    bspec = lambda off: pl.BlockSpec((1, tw), lambda i, j: (0, j + off))
    return pl.pallas_call(
        body, name=name, grid=(nt, nw),
        in_specs=_halo_specs(T, tm, tw, lambda j: j, "ij") + _halo_specs(T, tm, tw, lambda j: j + nw, "ij")
        + [wspec(0), wspec(nw), bspec(0), bspec(nw)],
        out_specs=pl.BlockSpec((tm, tw), lambda i, j: (i, j)),
        out_shape=jax.ShapeDtypeStruct((T, F), BF16),
        compiler_params=_cparams("parallel", "parallel"),
    )(u, u, u, u, u, u, cw, cw, cb, cb)


def _conv_bwd(u, df, cw, cb, name):
    T, F2 = u.shape
    F = F2 // 2
    tm, tw = _pick(T, 256, 8), _pick(F, 512)
    nt, nw = T // tm, F // tw

    def body(ua, uap, uan, ug, ugp, ugn, cwa, cwg, cba, cbg, df_ref, dfp, dfn, du_ref, acca_ref, accg_ref):
        i = pl.program_id(1)

        @pl.when(i == 0)
        def _():
            acca_ref[...] = jnp.zeros_like(acca_ref)
            accg_ref[...] = jnp.zeros_like(accg_ref)

        first, last = i == 0, i == nt - 1
        wa, wg, ba, bg = cwa[...], cwg[...], cba[...], cbg[...]

        def conv(p, m, n, w, b):
            return p * w[0:1] + m * w[1:2] + n * w[2:3] + b

        def grads(a, g, d):
            return d * g * _dsilu(a), d * _silu(a)

        xa, xg, d = ua[...], ug[...], df_ref[...]
        sa = _shift_rows(xa, jnp.where(first, 0.0, uap[7:8, :]), jnp.where(last, 0.0, uan[0:1, :]))
        sg = _shift_rows(xg, jnp.where(first, 0.0, ugp[7:8, :]), jnp.where(last, 0.0, ugn[0:1, :]))
        da, dg = grads(conv(sa[0], xa, sa[1], wa, ba), conv(sg[0], xg, sg[1], wg, bg), d)
        da_p, dg_p = grads(conv(uap[6:7, :], uap[7:8, :], xa[0:1], wa, ba),
                           conv(ugp[6:7, :], ugp[7:8, :], xg[0:1], wg, bg), dfp[7:8, :])
        da_n, dg_n = grads(conv(xa[tm - 1:tm], uan[0:1, :], uan[1:2, :], wa, ba),
                           conv(xg[tm - 1:tm], ugn[0:1, :], ugn[1:2, :], wg, bg), dfn[0:1, :])
        ta = _shift_rows(da, jnp.where(first, 0.0, da_p), jnp.where(last, 0.0, da_n))
        tg = _shift_rows(dg, jnp.where(first, 0.0, dg_p), jnp.where(last, 0.0, dg_n))
        du_ref[0] = (ta[1] * wa[0:1] + da * wa[1:2] + ta[0] * wa[2:3]).astype(BF16)
        du_ref[1] = (tg[1] * wg[0:1] + dg * wg[1:2] + tg[0] * wg[2:3]).astype(BF16)
        for t, (va, vg) in enumerate(((sa[0], sg[0]), (xa, xg), (sa[1], sg[1]))):
            acca_ref[t:t + 1, :] += jnp.sum(da * va, axis=0, keepdims=True)
            accg_ref[t:t + 1, :] += jnp.sum(dg * vg, axis=0, keepdims=True)
        acca_ref[3:4, :] += jnp.sum(da, axis=0, keepdims=True)
        accg_ref[3:4, :] += jnp.sum(dg, axis=0, keepdims=True)

    wspec = lambda off: pl.BlockSpec((3, tw), lambda j, i: (0, j + off))
    bspec = lambda off: pl.BlockSpec((1, tw), lambda j, i: (0, j + off))
    row = pl.BlockSpec((tm, tw), lambda j, i: (i, j))
    acc = pl.BlockSpec((8, tw), lambda j, i: (0, j))
    return pl.pallas_call(
        body, name=name, grid=(nw, nt),
        in_specs=_halo_specs(T, tm, tw, lambda j: j, "ji") + _halo_specs(T, tm, tw, lambda j: j + nw, "ji")
        + [wspec(0), wspec(nw), bspec(0), bspec(nw)] + _halo_specs(T, tm, tw, lambda j: j, "ji"),
        out_specs=[pl.BlockSpec((2, tm, tw), lambda j, i: (0, i, j)), acc, acc],
        out_shape=[jax.ShapeDtypeStruct((2, T, F), BF16),
                   jax.ShapeDtypeStruct((8, F), F32), jax.ShapeDtypeStruct((8, F), F32)],
        compiler_params=_cparams("parallel", "arbitrary"),
    )(u, u, u, u, u, u, cw, cw, cb, cb, df, df, df)


def _assemble_dz(lay, z_used, Z, n_ctx, dqa, drb, dga, dgb, dka, dva, dvg, dqg, dkg, dlr, name):
    T = dqa.shape[0]
    R = T + n_ctx
    tm = _pick(n_ctx, 128, 8)
    cb = n_ctx // tm

    def body(dqa_ref, drb_ref, dga_ref, dgb_ref, dka_ref, dva_ref, dvg0, dvg1, dqg0, dqg1, dkg0, dkg1, dlr_ref, o_ref):
        lat = pl.program_id(0) >= cb

        def put(seg, val):
            o_ref[:, lay[seg]:lay[seg] + val.shape[1]] = val.astype(BF16)

        def lat_only(ref):
            v = ref[...]
            return jnp.where(lat, v, jnp.zeros_like(v))

        put("qa", lat_only(dqa_ref))
        put("rb", lat_only(drb_ref))
        put("ga", lat_only(dga_ref))
        put("gb", lat_only(dgb_ref))
        put("ka", dka_ref[...])
        put("va", dva_ref[...])
        put("vb", dvg0[0] + dvg1[0])
        put("qb", dqg0[0] + dqg1[0])
        put("kb", dkg0[0] + dkg1[0])
        put("lr", dlr_ref[...])
        if Z > z_used:
            o_ref[:, z_used:] = jnp.zeros((tm, Z - z_used), BF16)

    lat_spec = lambda a: pl.BlockSpec((tm, a.shape[1]), lambda i: (jnp.maximum(i - cb, 0), 0))
    all_spec = lambda a: pl.BlockSpec((tm, a.shape[1]), lambda i: (i, 0))
    dir_specs = lambda a: [pl.BlockSpec((1, tm, a.shape[2]), lambda i: (0, i, 0)),
                           pl.BlockSpec((1, tm, a.shape[2]), lambda i: (1, i, 0))]
    return pl.pallas_call(
        body, name=name, grid=(R // tm,),
        in_specs=[lat_spec(dqa), lat_spec(drb), lat_spec(dga), lat_spec(dgb), all_spec(dka), all_spec(dva)]
        + dir_specs(dvg) + dir_specs(dqg) + dir_specs(dkg) + [all_spec(dlr)],
        out_specs=pl.BlockSpec((tm, Z), lambda i: (i, 0)),
        out_shape=jax.ShapeDtypeStruct((R, Z), BF16), compiler_params=_cparams("parallel"),
    )(dqa, drb, dga, dgb, dka, dva, dvg, dvg, dqg, dqg, dkg, dkg, dlr)


def _mod_fwd(ca, w, b, name):
    n, D = ca.shape
    N = w.shape[1]
    tn = _pick(N, 512)

    def body(c_ref, w_ref, b_ref, o_ref, s_ref):
        s = _silu(c_ref[...])
        s_ref[...] = s
        o_ref[...] = _dot(s.astype(BF16), w_ref[...].astype(BF16), NN) + b_ref[...]

    return pl.pallas_call(
        body, name=name, grid=(N // tn,),
        in_specs=[pl.BlockSpec((n, D), lambda j: (0, 0)), pl.BlockSpec((D, tn), lambda j: (0, j)),
                  pl.BlockSpec((1, tn), lambda j: (0, j))],
        out_specs=[pl.BlockSpec((n, tn), lambda j: (0, j)), pl.BlockSpec((n, D), lambda j: (0, 0))],
        out_shape=[jax.ShapeDtypeStruct((n, N), F32), jax.ShapeDtypeStruct((n, D), F32)],
        compiler_params=_cparams("arbitrary"))(ca, w, b)


def _silu_bwd(dsil, ca, name):
    def body(d_ref, c_ref, o_ref):
        o_ref[...] = d_ref[...] * _dsilu(c_ref[...])

    return pl.pallas_call(body, name=name, out_shape=jax.ShapeDtypeStruct(ca.shape, F32))(dsil, ca)


def _adam_math(w, g, m, v):
    c1 = 1.0 - ADAM_B1 ** ADAM_STEP
    c2 = 1.0 - ADAM_B2 ** ADAM_STEP
    mn = ADAM_B1 * m + (1.0 - ADAM_B1) * g
    vn = ADAM_B2 * v + (1.0 - ADAM_B2) * (g * g)
    return -ADAM_LR * ((mn / c1) / (jnp.sqrt(vn / c2) + ADAM_EPS) + ADAM_WD * w), mn, vn


def _adamw(w, g, m, v, name, ride=None):
    Rw, Cw = w.shape
    tr = _pick(Rw, 128, 8)

    def body(w_ref, g_ref, m_ref, v_ref, d_ref, mo_ref, vo_ref):
        d_ref[...], mo_ref[...], vo_ref[...] = _adam_math(w_ref[...], g_ref[...], m_ref[...], v_ref[...])

    row = pl.BlockSpec((tr, Cw), lambda i: (i, 0))
    sh = jax.ShapeDtypeStruct((Rw, Cw), F32)
    return _pcall(body, name=name, grid=(Rw // tr,), in_specs=[row] * 4, out_specs=[row] * 3, out_shape=[sh] * 3,
                  sem=("parallel",), args=(w, g, m, v), ride=ride)


HBM_SPEC = pl.BlockSpec(memory_space=pltpu.HBM)


def _exchange(inputs, out_shapes, stages, name):
    n_in, n_out = len(inputs), len(out_shapes)
    n = sum(len(s) for s in stages)

    def body(*refs):
        ins, outs = refs[:n_in], refs[n_in:n_in + n_out]
        send_sems, recv_sems = refs[n_in + n_out:]
        k = 0
        for stage in stages:
            copies = _stage_copies(stage, ins, outs, send_sems, recv_sems, k)
            for cp in copies:
                cp.start()
            for cp in copies:
                cp.wait()
            k += len(stage)

    return pl.pallas_call(
        body, name=name, in_specs=[HBM_SPEC] * n_in, out_specs=[HBM_SPEC] * n_out, out_shape=out_shapes,
        scratch_shapes=[pltpu.SemaphoreType.DMA((n,)), pltpu.SemaphoreType.DMA((n,))],
    )(*inputs)


def _stage_copies(stage, ins, outs, send_sems, recv_sems, k0=0):
    me = (lax.axis_index("x"), lax.axis_index("y"), lax.axis_index("c"))
    copies = []
    for k, ((skind, sidx), sfn, didx, dfn, flip) in enumerate(stage):
        src = (ins if skind == "in" else outs)[sidx].at[sfn(*me)]
        dst = outs[didx].at[dfn(*me)]
        if flip == (0, 0, 0):
            copies.append(pltpu.make_async_copy(src, dst, send_sems.at[k0 + k]))
        else:
            peer = tuple(1 - a if f else a for a, f in zip(me, flip))
            copies.append(pltpu.make_async_remote_copy(src, dst, send_sems.at[k0 + k], recv_sems.at[k0 + k],
                                                       device_id=peer, device_id_type=MESH))
    return copies


def _pcall(body, *, name, grid, in_specs, out_specs, out_shape, scratch_shapes=(), sem, args, ride=None):
    many = isinstance(out_shape, (list, tuple))
    out_specs, out_shape = (list(out_specs), list(out_shape)) if many else ([out_specs], [out_shape])
    if ride is None:
        res = pl.pallas_call(body, name=name, grid=grid, in_specs=list(in_specs), out_specs=out_specs,
                             out_shape=out_shape, scratch_shapes=list(scratch_shapes),
                             compiler_params=_cparams(*sem))(*args)
        return res if many else res[0]
    x_in, x_out, stage, aliases = ride
    n_in, n_out, n_scr, n_xin, n_xout = len(in_specs), len(out_specs), len(scratch_shapes), len(x_in), len(x_out)

    def wrapped(*refs):
        ins, xins = refs[:n_in], refs[n_in:n_in + n_xin]
        o0 = n_in + n_xin
        outs, xouts = refs[o0:o0 + n_out], refs[o0 + n_out:o0 + n_out + n_xout]
        s0 = o0 + n_out + n_xout
        scr, (send_sems, recv_sems) = refs[s0:s0 + n_scr], refs[s0 + n_scr:]
        first = functools.reduce(jnp.logical_and, [pl.program_id(d) == 0 for d in range(len(grid))])
        last = functools.reduce(jnp.logical_and, [pl.program_id(d) == grid[d] - 1 for d in range(len(grid))])

        @pl.when(first)
        def _():
            for cp in _stage_copies(stage, xins, xouts, send_sems, recv_sems):
                cp.start()

        body(*ins, *outs, *scr)

        @pl.when(last)
        def _():
            for cp in _stage_copies(stage, xins, xouts, send_sems, recv_sems):
                cp.wait()

    res = pl.pallas_call(
        wrapped, name=name, grid=grid, in_specs=list(in_specs) + [HBM_SPEC] * n_xin,
        out_specs=out_specs + [HBM_SPEC] * n_xout, out_shape=out_shape + list(x_out),
        scratch_shapes=list(scratch_shapes) + [pltpu.SemaphoreType.DMA((len(stage),)),
                                               pltpu.SemaphoreType.DMA((len(stage),))],
        input_output_aliases={n_in + a: n_out + b for a, b in aliases.items()},
        compiler_params=_cparams(*(["arbitrary"] * len(grid))))(*args, *x_in)
    main = res[:n_out]
    return (main if many else main[0]), list(res[n_out:])


FLIPS_ALL = [(0, 0, 1), (0, 1, 0), (0, 1, 1), (1, 0, 0), (1, 0, 1), (1, 1, 0), (1, 1, 1)]
FLIPS_CHIP = [(0, 1, 0), (1, 0, 0), (1, 1, 0)]


def _sum_slots(buf, name):
    n, r, w = buf.shape
    tr = _pick(r, 256, 8)

    def body(b_ref, o_ref):
        acc = b_ref[0]
        for s in range(1, n):
            acc = acc + b_ref[s]
        o_ref[...] = acc

    return pl.pallas_call(
        body, name=name, grid=(r // tr,), in_specs=[pl.BlockSpec((n, tr, w), lambda i: (0, i, 0))],
        out_specs=pl.BlockSpec((tr, w), lambda i: (i, 0)), out_shape=jax.ShapeDtypeStruct((r, w), F32),
        compiler_params=_cparams("parallel"))(buf)


def _allreduce_plan(buf):
    whole = lambda x, y, c: (slice(None), slice(None))
    slot = lambda x, y, c: (4 * x + 2 * y + c,)
    stage = [(("in", 0), whole, 0, slot, f) for f in [(0, 0, 0)] + FLIPS_ALL]
    return [jax.ShapeDtypeStruct((8,) + buf.shape, F32)], stage


def _allreduce(buf, name):
    shapes, stage = _allreduce_plan(buf)
    (slots,) = _exchange([buf], shapes, [stage], name + "_x")
    return _sum_slots(slots, name + "_sum")


def _gather_plan(shards, src):
    half = lambda a, c: pl.ds(c * (a.shape[0] // 2), a.shape[0] // 2)
    first, second = [], []
    for n, a in enumerate(shards):
        for f in FLIPS_CHIP:
            first.append((("in", n), lambda x, y, c, a=a: (half(a, c), slice(None)), n,
                          lambda x, y, c, a=a: (2 * x + y, half(a, c), slice(None)), f))
            peer_slot = lambda x, y, c, a=a, f=f: (2 * (x ^ f[0]) + (y ^ f[1]), half(a, c), slice(None))
            second.append(((src, n), peer_slot, n, peer_slot, (0, 0, 1)))
    outs = [jax.ShapeDtypeStruct((4,) + a.shape, a.dtype) for a in shards]
    return first, second, outs


SEM_SPEC = pl.BlockSpec(memory_space=pltpu.SEMAPHORE)


def _gather_start(shard, name):
    first, _, (land,) = _gather_plan([shard], "in")
    sems = pltpu.SemaphoreType.DMA((len(first),))

    def body(s_ref, land_ref, send_sems, recv_sems, s_thru, land_thru, token_ref):
        for cp in _stage_copies(first, [s_ref], [land_ref], send_sems, recv_sems):
            cp.start()
        token_ref[...] = jnp.zeros_like(token_ref)

    outs = pl.pallas_call(
        body, name=name,
        out_shape=(sems, sems, pltpu.HBM(shard.shape, shard.dtype), pltpu.HBM(land.shape, land.dtype),
                   jax.ShapeDtypeStruct((8, LANES), F32)),
        in_specs=(HBM_SPEC, HBM_SPEC),
        out_specs=(SEM_SPEC, SEM_SPEC, HBM_SPEC, HBM_SPEC, pl.BlockSpec(memory_space=pltpu.VMEM)),
        input_output_aliases={0: 2, 1: 3},
        compiler_params=pltpu.CompilerParams(has_side_effects=pltpu.SideEffectType.DATAFLOW_SIDE_EFFECTING),
    )(pltpu.with_memory_space_constraint(shard, pltpu.HBM),
      pltpu.with_memory_space_constraint(lax.empty(land.shape, land.dtype), pltpu.HBM))
    return outs[4], outs[:4]


def _gather_wait(pending, after, name):
    send_sems, recv_sems, s_thru, land_thru = pending
    first, _, _ = _gather_plan([s_thru], "in")

    def body(s_ref, land_ref, send_sems, recv_sems, after_ref, s_dead, got_ref):
        for cp in _stage_copies(first, [s_ref], [land_ref], send_sems, recv_sems):
            cp.wait()

    return pl.pallas_call(
        body, name=name,
        out_shape=(pltpu.HBM(s_thru.shape, s_thru.dtype), pltpu.HBM(land_thru.shape, land_thru.dtype)),
        in_specs=(HBM_SPEC, HBM_SPEC, SEM_SPEC, SEM_SPEC, pl.BlockSpec(memory_space=pl.ANY)),
        out_specs=(HBM_SPEC, HBM_SPEC), input_output_aliases={0: 0, 1: 1},
        compiler_params=pltpu.CompilerParams(has_side_effects=pltpu.SideEffectType.DATAFLOW_SIDE_EFFECTING),
    )(s_thru, land_thru, send_sems, recv_sems, after)[1]


def _place_own(buf, shard, svec, name):
    _, Rs, Cs = buf.shape
    tr = _pick(Rs, 256, 16)

    def body(s_ref, buf_ref, sh_ref, o_ref):
        o_ref[0] = sh_ref[...]

    grid_spec = pltpu.PrefetchScalarGridSpec(
        num_scalar_prefetch=1, grid=(Rs // tr,),
        in_specs=[pl.BlockSpec(memory_space=pl.ANY), pl.BlockSpec((tr, Cs), lambda i, s: (i, 0))],
        out_specs=pl.BlockSpec((1, tr, Cs), lambda i, s: (s[0], i, 0)))
    return pl.pallas_call(body, name=name, grid_spec=grid_spec, out_shape=jax.ShapeDtypeStruct(buf.shape, buf.dtype),
                          input_output_aliases={1: 0}, compiler_params=_cparams("arbitrary"))(svec, buf, shard)


def _add_pair(G, bufA, cvec, name):
    _, Rs, Cs = G.shape
    Rh = Rs // 2
    tr = _pick(Rh, 128, 16)
    nb = Rh // tr

    def body(c_ref, g_ref, a_ref, o_ref):
        o_ref[...] = (g_ref[...] + a_ref[...]).astype(BF16)

    grid_spec = pltpu.PrefetchScalarGridSpec(
        num_scalar_prefetch=1, grid=(4, nb),
        in_specs=[pl.BlockSpec((1, tr, Cs), lambda s, i, c_ref: (s, c_ref[0] * nb + i, 0)),
                  pl.BlockSpec((1, tr, Cs), lambda s, i, c_ref: (s, i, 0))],
        out_specs=pl.BlockSpec((1, tr, Cs), lambda s, i, c_ref: (s, i, 0)))
    return pl.pallas_call(body, name=name, grid_spec=grid_spec, out_shape=jax.ShapeDtypeStruct((4, Rh, Cs), BF16),
                          compiler_params=_cparams("parallel", "parallel"))(cvec, G, bufA)


def _sum_chips(G, bufA, bufB, cvec, svec, name):
    _, Rs, Cs = G.shape
    Rh = Rs // 2
    tr = _pick(Rh, 128, 16)
    nb = Rh // tr

    def body(c_ref, s_ref, g_ref, a_ref, b_ref, o_ref):
        o_ref[...] = (g_ref[0] + a_ref[0]) + b_ref[0].astype(F32) + b_ref[1].astype(F32) + b_ref[2].astype(F32)

    grid_spec = pltpu.PrefetchScalarGridSpec(
        num_scalar_prefetch=2, grid=(nb,),
        in_specs=[pl.BlockSpec((1, tr, Cs), lambda i, c, s: (s[0], c[0] * nb + i, 0)),
                  pl.BlockSpec((1, tr, Cs), lambda i, c, s: (s[0], i, 0)),
                  pl.BlockSpec((3, tr, Cs), lambda i, c, s: (0, i, 0))],
        out_specs=pl.BlockSpec((tr, Cs), lambda i, c, s: (i, 0)))
    return pl.pallas_call(body, name=name, grid_spec=grid_spec, out_shape=jax.ShapeDtypeStruct((Rh, Cs), F32),
                          compiler_params=_cparams("parallel"))(cvec, svec, G, bufA, bufB)


def _pair_plan(grads):
    Rh = [g.shape[1] // 2 for g in grads]
    whole3 = lambda x, y, c: (slice(None), slice(None), slice(None))
    stage = [(("in", n), lambda x, y, c, n=n: (slice(None), pl.ds((1 - c) * Rh[n], Rh[n]), slice(None)), n,
              whole3, (0, 0, 1)) for n in range(len(grads))]
    return [jax.ShapeDtypeStruct((4, Rh[n], g.shape[2]), F32) for n, g in enumerate(grads)], stage


def _chips_plan(P):
    stage = [(("in", n), lambda x, y, c, f=f: (2 * (x ^ f[0]) + (y ^ f[1]),), n, lambda x, y, c, k=k: (k,), f)
             for n in range(len(P)) for k, f in enumerate(FLIPS_CHIP)]
    return [jax.ShapeDtypeStruct((3,) + p.shape[1:], BF16) for p in P], stage


def _halves_plan(mine):
    whole2 = lambda x, y, c: (slice(None), slice(None))
    stage = [(("in", n), whole2, n, whole2, (0, 0, 1)) for n in range(len(mine))]
    return [jax.ShapeDtypeStruct(r.shape, F32) for r in mine], stage


def _adamw_halves(w, mine, other, m, v, cvec, name):
    Rs, Cs = w.shape
    Rh = Rs // 2
    tr = _pick(Rh, 128, 8)
    nb = Rh // tr

    def body(c_ref, w_ref, a_ref, b_ref, m_ref, v_ref, g_ref, d_ref, mo_ref, vo_ref):
        gv = jnp.where(pl.program_id(0) // nb == c_ref[0], a_ref[...], b_ref[...])
        g_ref[...] = gv
        d_ref[...], mo_ref[...], vo_ref[...] = _adam_math(w_ref[...], gv, m_ref[...], v_ref[...])

    row = pl.BlockSpec((tr, Cs), lambda i, c: (i, 0))
    hrow = pl.BlockSpec((tr, Cs), lambda i, c: (i % nb, 0))
    grid_spec = pltpu.PrefetchScalarGridSpec(num_scalar_prefetch=1, grid=(2 * nb,),
                                             in_specs=[row, hrow, hrow, row, row], out_specs=[row] * 4)
    return pl.pallas_call(body, name=name, grid_spec=grid_spec, out_shape=[jax.ShapeDtypeStruct((Rs, Cs), F32)] * 4,
                          compiler_params=_cparams("parallel"))(cvec, w, mine, other, m, v)


def _pack(arrays):
    flat = [a.reshape(-1).astype(F32) for a in arrays]
    meta, off = [], 0
    for a, f in zip(arrays, flat):
        meta.append((off, a.shape))
        off += f.shape[0]
    total = -(-off // (8 * LANES)) * (8 * LANES)
    flat.append(jnp.zeros((total - off,), F32))
    return jnp.concatenate(flat).reshape(total // LANES, LANES), meta


def _unpack(buf, meta):
    flat = buf.reshape(-1)
    out = []
    for off, shape in meta:
        size = 1
        for s in shape:
            size *= s
        out.append(flat[off:off + size].reshape(shape))
    return out


WEIGHT_NAMES = ["c_ctx", "w_mod", "b_mod", "g_mix", "w_in", "q_norm", "k_norm", "attn_sink", "w_gate_f", "b_gate_f",
                "w_gate_b", "b_gate_b", "gla_norm", "w_attn_o", "w_gla_o", "w_out", "g_ffn", "w_up", "conv_w",
                "conv_b", "w_down"]
BIG_NAMES = ["w_in", "w_attn_o", "w_gla_o", "w_out", "w_up", "w_down"]
SHARDED_SMALL = ["w_gate_f", "w_gate_b", "conv_w"]


def _layouts(D):
    aw, kvw, gk, gv = N_Q_HEADS * HEAD_DIM, N_KV_HEADS * HEAD_DIM, D // 2, D
    widths = {"qa": aw, "ka": kvw, "va": kvw, "qb": gk, "kb": gk, "vb": gv, "rb": gv, "lr": 2 * GLA_LOWRANK,
              "ga": D, "gb": D}
    orig, off = {}, 0
    for s in ["qa", "ka", "va", "qb", "kb", "vb", "rb", "lr", "ga", "gb"]:
        orig[s] = off
        off += widths[s]
    order = ["qa", "vb", "rb", "ga", "gb", "ka", "va", "qb", "kb", "lr"]
    lay, off = {}, 0
    for s in order:
        lay[s] = off
        off += LANES if s == "lr" else widths[s]
    align = {"qa": aw, "vb": D, "rb": D, "ga": D, "gb": D, "ka": kvw, "va": kvw, "qb": gk, "kb": gk,
             "lr": LANES}
    for s in order:
        assert lay[s] % align[s] == 0, (s, lay[s], align[s])
    return widths, orig, order, lay, off, -(-off // (2 * MXU_TILE)) * (2 * MXU_TILE)


def _rope_tables(T, L):
    t = jnp.arange(T)
    nf = HEAD_DIM // 4
    inv = ROPE_THETA ** (-jnp.arange(nf, dtype=F32) / nf)
    ang = jnp.concatenate([(t // GRID_W)[:, None] * inv, (t % GRID_W)[:, None] * inv], axis=-1)
    cos, sin = jnp.cos(ang), jnp.sin(ang)
    cos2 = jnp.concatenate([jnp.ones((L, HEAD_DIM), F32), jnp.concatenate([cos, cos], axis=-1)], axis=0)
    sin2 = jnp.concatenate([jnp.zeros((L, HEAD_DIM), F32), jnp.concatenate([-sin, sin], axis=-1)], axis=0)
    return cos2, sin2


def _step(x, c, ctx, loss_target, W, M, V):
    xi, yi, ci = lax.axis_index("x"), lax.axis_index("y"), lax.axis_index("c")
    chip = 2 * xi + yi
    dev = 2 * chip + ci
    south = (ci == 0).astype(F32)
    cvec = ci.reshape(1).astype(jnp.int32)
    svec = chip.reshape(1).astype(jnp.int32)
    T, D = x.shape[1], x.shape[2]
    L = ctx.shape[1]
    R = L + T
    F = 4 * W["w_down"].shape[1]
    GK, GV = D // 2, D
    DK, DV = GK // GLA_HEADS, GV // GLA_HEADS
    N6 = 6 * D
    N4 = N6 // 4
    widths, orig, order, lay, z_used, Z = _layouts(D)

    def place_cols(shard, full_cols):
        cols = shard.shape[-1]
        full = jnp.zeros(shard.shape[:-1] + (full_cols,), F32)
        return lax.dynamic_update_slice(full, shard * south, (0,) * (shard.ndim - 1) + (chip * cols,))

    sq = lambda a: a.reshape(a.shape[1:])
    shards = [sq(W[n]).astype(BF16) for n in BIG_NAMES]
    token, pending = _gather_start(shards[0], "gather_w_in_start")
    c_rows = lax.dynamic_update_slice(jnp.zeros((8, D), F32), c, (dev, 0)) + token[0, 0]
    bufa, meta = _pack([c_rows, place_cols(W["w_gate_f"][0], GK), place_cols(W["w_gate_b"][0], GK),
                        place_cols(W["conv_w"][0], 2 * F)])
    c_all, wgf, wgb, cw = _unpack(_allreduce(bufa, "gather_small"), meta)
    ca = jnp.concatenate([c_all, W["c_ctx"][None, :], jnp.zeros((7, D), F32)], axis=0)
    b_shard = lax.dynamic_slice(W["b_mod"], (0, chip * N4), (1, N4))
    mod_part, sil = _mod_fwd(ca, W["w_mod"][0], b_shard, "mod_fwd")
    slots = lax.dynamic_update_slice(jnp.zeros((4, 16, N4), F32), (mod_part * south)[None], (chip, 0, 0))
    mod_all = _allreduce(slots.reshape(64, N4), "gather_mod").reshape(4, 16, N4).transpose(1, 0, 2).reshape(16, N6)
    mx = lax.dynamic_slice(mod_all, (dev, 0), (1, N6)).reshape(6, 1, D)
    mc = mod_all[8].reshape(6, 1, D)

    own = lambda g, s, n: _place_own(g, s, svec, "place_" + n)
    cols = lambda g: g.transpose(1, 0, 2).reshape(g.shape[1], 4 * g.shape[2])
    rows = lambda g: g.reshape(4 * g.shape[1], g.shape[2])
    land = _gather_wait(pending, mod_all, "gather_w_in_wait")
    _, forward, land_shape = _gather_plan(shards[:1], "in")
    sc1 = jnp.stack([mc[1], mx[1]])
    sh1 = jnp.stack([mc[0], mx[0]])
    h, (land,) = _modnorm_fwd(ctx[0], x[0], W["g_mix"], sc1, sh1, "modnorm1", ride=([land], land_shape, forward, {0: 0}))
    w_in_f = cols(own(land, shards[0], "w_in"))
    seg = lambda s: w_in_f[:, orig[s]:orig[s] + widths[s]]
    w_cat = jnp.concatenate([jnp.pad(seg(s), ((0, 0), (0, LANES - widths[s]))) if s == "lr" else seg(s)
                             for s in order] + [jnp.zeros((D, Z - z_used), BF16)], axis=1)
    gather1, gather2, gather_outs = _gather_plan(shards[1:], "in")
    wg = jnp.zeros((2, LANES, GK), F32).at[0, :GLA_LOWRANK].set(wgf).at[1, GLA_LOWRANK:2 * GLA_LOWRANK].set(wgb)
    bg = jnp.stack([W["b_gate_f"], W["b_gate_b"]])
    cb = W["conv_b"]
    sink_rows = jnp.broadcast_to(W["attn_sink"][0][:, None], (N_Q_HEADS, HEAD_DIM))
    cos2, sin2 = _rope_tables(T, L)
    blk = lambda s, w: lay[s] // w

    z, landed = _matmul(h, w_cat, "nn", F32, "proj_in", tn=1536, ride=(shards[1:], gather_outs, gather1, {}))
    qn = _qknorm_fwd(z, blk("qa", widths["qa"]), T, L, W["q_norm"], cos2, sin2, N_Q_HEADS, "qnorm")
    kn = _qknorm_fwd(z, blk("ka", widths["ka"]), R, 0, W["k_norm"], cos2, sin2, N_KV_HEADS, "knorm")
    vb = _cast_seg(z, blk("va", widths["va"]), widths["va"], "vcast")
    o_attn, landed = _attn_fwd(qn, kn, vb, sink_rows, L, "attn_fwd",
                               ride=(landed, gather_outs, gather2, {n: n for n in range(len(landed))}))
    g_ao, g_go, g_out, g_up, g_dn = [own(g, s, n) for g, s, n in zip(landed, shards[1:], BIG_NAMES[1:])]
    w_ao, w_go, w_out, w_up, w_dn = rows(g_ao), rows(g_go), rows(g_out), cols(g_up), rows(g_dn)
    gla_blks = (blk("qb", GK), blk("kb", GK), blk("vb", GV), blk("lr", LANES))
    o_g, sprev = _gla_fwd(z, *gla_blks, wg, bg, DV, L, "gla_fwd")
    p = _glanorm_fwd(o_g, z, blk("rb", D), W["gla_norm"], L, "glanorm")
    ya = _matmul(o_attn, w_ao, "nn", F32, "proj_attn_o")
    yg = _matmul(p, w_go, "nn", F32, "proj_gla_o")
    m = _gate_fwd(z, blk("ga", D), blk("gb", D), ya, yg, L, "gate")
    mix = _matmul(m, w_out, "nn", F32, "proj_out")
    x1, h2 = _resnorm_fwd(x[0], mix, mx[2], W["g_ffn"], mx[4], mx[3], "resnorm2")
    u = _matmul(h2, w_up, "nn", F32, "ffn_up")
    f = _conv_fwd(u, cw, cb, "conv_swiglu")
    d = _matmul(f, w_dn, "nn", F32, "ffn_down", tk=2816)
    dy, dd, lacc = _loss_head(d, x1, mx[5], loss_target[0], "loss_head")
    loss = lax.psum((0.5 / D) * jnp.sum(lacc[0]), ("x", "y", "c"))

    gw_dn = _matmul(f, dd, "tn", F32, "ffn_down_dw")
    df = _matmul(dd, w_dn, "nt", F32, "ffn_down_dx")
    du, acca, accg = _conv_bwd(u, df, cw, cb, "conv_swiglu_bwd")
    gw_up = _matmul(h2, du, "tn", F32, "ffn_up_dw", tm=512, halves="b", col_shards=4)
    dh2 = _matmul(du, w_up, "nt", F32, "ffn_up_dx", tk=2816, halves="a")
    dx1, dmix, s2 = _resnorm_bwd(x1, dh2, W["g_ffn"], mx[4], dy, mix, mx[2], "resnorm2_bwd")
    gw_out = _matmul(m, dmix, "tn", F32, "proj_out_dw")
    dm = _matmul(dmix, w_out, "nt", F32, "proj_out_dx")
    dya, dyg, dga, dgb = _gate_bwd(z, blk("ga", D), blk("gb", D), ya, yg, dm, L, "gate_bwd")
    gw_ao = _matmul(o_attn, dya, "tn", F32, "proj_attn_o_dw")
    do_attn = _matmul(dya, w_ao, "nt", BF16, "proj_attn_o_dx")
    gw_go = _matmul(p, dyg, "tn", F32, "proj_gla_o_dw")
    dp = _matmul(dyg, w_go, "nt", F32, "proj_gla_o_dx")
    do_gla, drb, s_gn = _glanorm_bwd(o_g, z, blk("rb", D), W["gla_norm"], dp, L, "glanorm_bwd")
    do_pad = jnp.concatenate([jnp.zeros((L, GV), BF16), do_gla], axis=0)
    by_cols = lambda g: g.reshape(g.shape[0], 4, g.shape[1] // 4).transpose(1, 0, 2)
    by_rows = lambda g: g.reshape(4, g.shape[0] // 4, g.shape[1])
    early = [by_rows(gw_ao), by_rows(gw_go), by_rows(gw_out), gw_up, by_rows(gw_dn)]
    (dqg, dkg, dvg, dpre, dbg), pair_e = _gla_bwd(z, *gla_blks, wg, bg, sprev, do_pad, L, "gla_bwd",
                                                  ride=(early, *_pair_plan(early), {}))
    sums_e = [_add_pair(g, a, cvec, "reduce_early_add%d" % n) for n, (g, a) in enumerate(zip(early, pair_e))]
    wg_cat = jnp.concatenate([wg[0], wg[1]], axis=1)
    dlr = _matmul(dpre, wg_cat, "nt", BF16, "gla_gate_dx")
    dwg = _matmul(z[:, lay["lr"]:lay["lr"] + LANES], dpre, "tn", F32, "gla_gate_dw")
    (dqn, dkw, dvw, dkc, dvc, dsn), chips_e = _attn_bwd(qn, kn, vb, sink_rows, do_attn, L, "attn_bwd",
                                                        ride=(sums_e, *_chips_plan(sums_e), {}))
    mine_e = [_sum_chips(g, a, b, cvec, svec, "reduce_early_sum%d" % n)
              for n, (g, a, b) in enumerate(zip(early, pair_e, chips_e))]
    dqa, s_qn = _qknorm_bwd(z, blk("qa", widths["qa"]), T, L, W["q_norm"], cos2, sin2, dqn, N_Q_HEADS, "qnorm_bwd")
    dk_all = jnp.concatenate([dkc, dkw[WINDOW:WINDOW + T]], axis=0)
    dv_all = jnp.concatenate([dvc, dvw[WINDOW:WINDOW + T]], axis=0)
    dka, s_kn = _qknorm_bwd(z, blk("ka", widths["ka"]), R, 0, W["k_norm"], cos2, sin2, dk_all, N_KV_HEADS, "knorm_bwd")
    dz = _assemble_dz(lay, z_used, Z, L, dqa, drb, dga, dgb, dka, dv_all, dvg, dqg, dkg, dlr, "assemble_dz")
    gw_cat, other_e = _matmul(h, dz, "tn", F32, "proj_in_dw", tn=768, tk=2816,
                              ride=(mine_e, *_halves_plan(mine_e), {}))
    gw_in = jnp.concatenate([gw_cat[:, lay[s]:lay[s] + widths[s]] for s in ["qa", "ka", "va", "qb", "kb", "vb", "rb",
                                                                           "lr", "ga", "gb"]], axis=1)
    late = [by_cols(gw_in)]
    shapes, stage = _pair_plan(late)
    pair_l = _exchange(late, shapes, [stage], "reduce_late_pair")
    sums_l = [_add_pair(late[0], pair_l[0], cvec, "reduce_late_add")]
    dh, chips_l = _matmul(dz, w_cat, "nt", F32, "proj_in_dx", tk=4608, ride=(sums_l, *_chips_plan(sums_l), {}))
    mine_l = [_sum_chips(late[0], pair_l[0], chips_l[0], cvec, svec, "reduce_late_sum")]
    shapes, stage = _halves_plan(mine_l)
    other_l = _exchange(mine_l, shapes, [stage], "reduce_late_halves")
    mine, other = mine_l + mine_e, list(other_l) + other_e
    grad_x, s1 = _modnorm_bwd(x[0], dh, W["g_mix"], mx[1], dx1, "modnorm1_bwd", dh_roff=L)
    _, s1c = _modnorm_bwd(ctx[0], dh, W["g_mix"], mc[1], None, "modnorm1_ctx_bwd")

    dmod_x = jnp.concatenate([s1[0], s1[1], s2[3], s2[0], s2[1], lacc[1]])
    dmod_c = jnp.concatenate([s1c[0], s1c[1], jnp.zeros((4 * D,), F32)])
    dmod_rows = lax.dynamic_update_slice(jnp.zeros((9, N6), F32).at[8].set(dmod_c), dmod_x[None], (dev, 0))
    small = [dmod_rows, dmod_x + dmod_c, s1[2] + s1c[2], s_qn[0], s_kn[0], dsn[:, 0, :Q_PER_KV].reshape(N_Q_HEADS),
             dwg[:GLA_LOWRANK, :GK], dbg[0].reshape(GK), dwg[GLA_LOWRANK:2 * GLA_LOWRANK, GK:], dbg[1].reshape(GK),
             s_gn[0], s2[2], jnp.concatenate([acca[0:3], accg[0:3]], axis=1), jnp.concatenate([acca[3], accg[3]])]
    bufc, meta = _pack(small)
    (dmod_sum, g_b_mod, g_g_mix, g_q_norm, g_k_norm, g_sink, g_wgf, g_bgf, g_wgb, g_bgb, g_gla_norm, g_g_ffn,
     g_conv_w, g_conv_b) = _unpack(_allreduce(bufc, "reduce_small"), meta)
    dmod16 = lax.dynamic_slice(jnp.concatenate([dmod_sum, jnp.zeros((7, N6), F32)], axis=0), (0, chip * N4), (16, N4))
    g_w_mod = _matmul(sil, dmod16, "tn", F32, "mod_dw")
    dsil = _matmul(dmod16, W["w_mod"][0], "nt", F32, "mod_dx")
    g_c_ctx = _silu_bwd(_allreduce(dsil * south, "reduce_cctx"), ca, "silu_bwd")[8]

    cut = lambda g: lax.dynamic_slice(g, (0, chip * (g.shape[1] // 4)), (g.shape[0], g.shape[1] // 4))
    grads = {"c_ctx": g_c_ctx, "w_mod": g_w_mod[None], "b_mod": g_b_mod[None], "g_mix": g_g_mix[None],
             "q_norm": g_q_norm[None], "k_norm": g_k_norm[None], "attn_sink": g_sink[None],
             "w_gate_f": cut(g_wgf)[None], "b_gate_f": g_bgf[None], "w_gate_b": cut(g_wgb)[None],
             "b_gate_b": g_bgb[None], "gla_norm": g_gla_norm[None], "g_ffn": g_g_ffn[None],
             "conv_w": cut(g_conv_w)[None], "conv_b": g_conv_b[None]}

    delta, new_m, new_v = {}, {}, {}
    dl, mn, vn = _adamw(W["w_mod"][0], g_w_mod, M["w_mod"][0], V["w_mod"][0], "adamw_w_mod")
    delta["w_mod"], new_m["w_mod"], new_v["w_mod"] = dl[None], mn[None], vn[None]
    for n, a, b in zip(BIG_NAMES, mine, other):
        g, dl, mn, vn = _adamw_halves(sq(W[n]), a, b, sq(M[n]), sq(V[n]), cvec, "adamw_" + n)
        grads[n], delta[n], new_m[n], new_v[n] = g[None], dl[None], mn[None], vn[None]
    small_names = [n for n in WEIGHT_NAMES if n not in delta]
    packs = [_pack([src[n] for n in small_names]) for src in (W, grads, M, V)]
    meta = packs[0][1]
    outs = _adamw(packs[0][0], packs[1][0], packs[2][0], packs[3][0], "adamw_small")
    for res, o in zip((delta, new_m, new_v), outs):
        for n, a in zip(small_names, _unpack(o, meta)):
            res[n] = a
    return (loss, grad_x[None], *[grads[n] for n in WEIGHT_NAMES], *[delta[n] for n in WEIGHT_NAMES],
            *[new_m[n] for n in WEIGHT_NAMES], *[new_v[n] for n in WEIGHT_NAMES])


def kernel(x, c, ctx, c_ctx, w_mod, b_mod, g_mix, w_in, q_norm, k_norm, attn_sink, w_gate_f, b_gate_f, w_gate_b, b_gate_b, gla_norm, w_attn_o, w_gla_o, w_out, g_ffn, w_up, conv_w, conv_b, w_down, loss_target, m_c_ctx, m_w_mod, m_b_mod, m_g_mix, m_w_in, m_q_norm, m_k_norm, m_attn_sink, m_w_gate_f, m_b_gate_f, m_w_gate_b, m_b_gate_b, m_gla_norm, m_w_attn_o, m_w_gla_o, m_w_out, m_g_ffn, m_w_up, m_conv_w, m_conv_b, m_w_down, v_c_ctx, v_w_mod, v_b_mod, v_g_mix, v_w_in, v_q_norm, v_k_norm, v_attn_sink, v_w_gate_f, v_b_gate_f, v_w_gate_b, v_b_gate_b, v_gla_norm, v_w_attn_o, v_w_gla_o, v_w_out, v_g_ffn, v_w_up, v_conv_w, v_conv_b, v_w_down):
    W = dict(zip(WEIGHT_NAMES, (c_ctx, w_mod, b_mod, g_mix, w_in, q_norm, k_norm, attn_sink, w_gate_f, b_gate_f,
                                w_gate_b, b_gate_b, gla_norm, w_attn_o, w_gla_o, w_out, g_ffn, w_up, conv_w, conv_b,
                                w_down)))
    M = dict(zip(WEIGHT_NAMES, (m_c_ctx, m_w_mod, m_b_mod, m_g_mix, m_w_in, m_q_norm, m_k_norm, m_attn_sink,
                                m_w_gate_f, m_b_gate_f, m_w_gate_b, m_b_gate_b, m_gla_norm, m_w_attn_o, m_w_gla_o,
                                m_w_out, m_g_ffn, m_w_up, m_conv_w, m_conv_b, m_w_down)))
    V = dict(zip(WEIGHT_NAMES, (v_c_ctx, v_w_mod, v_b_mod, v_g_mix, v_w_in, v_q_norm, v_k_norm, v_attn_sink,
                                v_w_gate_f, v_b_gate_f, v_w_gate_b, v_b_gate_b, v_gla_norm, v_w_attn_o, v_w_gla_o,
                                v_w_out, v_g_ffn, v_w_up, v_conv_w, v_conv_b, v_w_down)))
    return _step(x, c, ctx, loss_target, W, M, V)
```

```python
import functools
import math

import jax
import jax.numpy as jnp
from jax import lax
from jax.experimental import pallas as pl
from jax.experimental.pallas import tpu as pltpu

F32 = jnp.float32
BF16 = jnp.bfloat16
MESH = pl.DeviceIdType.MESH

EPS = 1e-6
HEAD_DIM = 128
N_Q_HEADS = 16
N_KV_HEADS = 4
Q_PER_KV = N_Q_HEADS // N_KV_HEADS
WINDOW = 128
GLA_HEADS = 4
GLA_LOWRANK = 16
GLA_GATE_NORM = 16.0
GLA_CHUNK = 64
GRID_W = 64
ROPE_THETA = 10000.0
GLA_LEVELS = (32, 16, 8, 4, 2, 1)
LANES = 128
MXU_TILE = 256

ADAM_LR = 0.001
ADAM_B1 = 0.9
ADAM_B2 = 0.999
ADAM_EPS = 1e-08
ADAM_WD = 0.01
ADAM_STEP = 10

VMEM_LIMIT = 52 * 1024 * 1024


def _cparams(*sem):
    return pltpu.CompilerParams(dimension_semantics=sem, vmem_limit_bytes=VMEM_LIMIT)


def _pick(n, target, mult=LANES):
    best = None
    d = mult
    while d <= min(n, target):
        if n % d == 0:
            best = d
        d += mult
    return n if best is None else best


def _sigmoid(x):
    return 1.0 / (1.0 + jnp.exp(-x))


def _silu(x):
    return x * _sigmoid(x)


def _dsilu(x):
    s = _sigmoid(x)
    return s * (1.0 + x * (1.0 - s))


def _dot(a, b, dims):
    return lax.dot_general(a, b, (dims, ((), ())), preferred_element_type=F32)


NN = ((1,), (0,))
NT = ((1,), (1,))
TN = ((0,), (0,))


def _matmul(a, b, mode, out_dtype, name, tm=1024, tn=1024, tk=2048, ride=None, halves=None, col_shards=None):
    if halves == "a":
        assert mode == "nt"
        (_, M, Kh), (N, K2) = a.shape, b.shape
        K = 2 * Kh
    elif halves == "b":
        assert mode == "tn"
        (K, M), (_, K2, Nh) = a.shape, b.shape
        N = 2 * Nh
    elif mode == "nn":
        (M, K), (K2, N) = a.shape, b.shape
    elif mode == "nt":
        (M, K), (N, K2) = a.shape, b.shape
    else:
        (K, M), (K2, N) = a.shape, b.shape
    assert K == K2, (name, a.shape, b.shape)
    pick = lambda n, t: _pick(n, t, MXU_TILE) if n % MXU_TILE == 0 else _pick(n, t)
    tm, tn, tk = pick(M, tm), pick(N // 2 if halves == "b" else N, tn), pick(K // 2 if halves == "a" else K, tk)
    if col_shards is not None:
        tn = N // col_shards
    nk = K // tk
    dims = {"nn": NN, "nt": NT, "tn": TN}[mode]

    def body(a_ref, b_ref, o_ref, acc_ref):
        k = pl.program_id(2)

        @pl.when(k == 0)
        def _():
            acc_ref[...] = jnp.zeros_like(acc_ref)

        av = a_ref[0] if halves == "a" else a_ref[...]
        bv = b_ref[0] if halves == "b" else b_ref[...]
        acc_ref[...] += _dot(av.astype(BF16), bv.astype(BF16), dims)

        @pl.when(k == nk - 1)
        def _():
            o_ref[...] = acc_ref[...].astype(out_dtype).reshape(o_ref.shape)

    if halves == "a":
        per = (K // 2) // tk
        a_spec = pl.BlockSpec((1, tm, tk), lambda i, j, k: (k // per, i, k % per))
    elif mode == "tn":
        a_spec = pl.BlockSpec((tk, tm), lambda i, j, k: (k, i))
    else:
        a_spec = pl.BlockSpec((tm, tk), lambda i, j, k: (i, k))
    if halves == "b":
        per = (N // 2) // tn
        b_spec = pl.BlockSpec((1, tk, tn), lambda i, j, k: (j // per, k, j % per))
    elif mode == "nt":
        b_spec = pl.BlockSpec((tn, tk), lambda i, j, k: (j, k))
    else:
        b_spec = pl.BlockSpec((tk, tn), lambda i, j, k: (k, j))
    if col_shards is None:
        out_spec, out_shape = pl.BlockSpec((tm, tn), lambda i, j, k: (i, j)), (M, N)
    else:
        assert tn * col_shards == N, (name, tn, N)
        out_spec, out_shape = pl.BlockSpec((1, tm, tn), lambda i, j, k: (j, i, 0)), (col_shards, M, tn)
    return _pcall(
        body, name=name, grid=(M // tm, N // tn, nk),
        in_specs=[a_spec, b_spec],
        out_specs=out_spec,
        out_shape=jax.ShapeDtypeStruct(out_shape, out_dtype),
        scratch_shapes=[pltpu.VMEM((tm, tn), F32)],
        sem=("parallel", "parallel", "arbitrary"), args=(a, b), ride=ride)


def _modnorm_fwd(xc, xl, g, sc, sh, name, ride=None):
    (L, D), T = xc.shape, xl.shape[0]
    tm = _pick(math.gcd(L, T), 256, 8)
    cb = L // tm

    def body(xc_ref, xl_ref, g_ref, sc_ref, sh_ref, h_ref):
        x = jnp.where(pl.program_id(0) < cb, xc_ref[...], xl_ref[...])
        r = lax.rsqrt(jnp.mean(x * x, axis=-1, keepdims=True) + EPS)
        n = x * r * g_ref[...]
        h_ref[...] = (n * (1.0 + sc_ref[0]) + sh_ref[0]).astype(BF16)

    sel = lambda i: (jnp.where(i < cb, 0, 1), 0, 0)
    return _pcall(
        body, name=name, grid=((L + T) // tm,),
        in_specs=[pl.BlockSpec((tm, D), lambda i: (jnp.minimum(i, cb - 1), 0)),
                  pl.BlockSpec((tm, D), lambda i: (jnp.maximum(i - cb, 0), 0)),
                  pl.BlockSpec((1, D), lambda i: (0, 0)), pl.BlockSpec((1, 1, D), sel), pl.BlockSpec((1, 1, D), sel)],
        out_specs=pl.BlockSpec((tm, D), lambda i: (i, 0)),
        out_shape=jax.ShapeDtypeStruct((L + T, D), BF16),
        sem=("parallel",), args=(xc, xl, g, sc, sh), ride=ride)


def _modnorm_bwd(x, dh, g, sc, resid, name, dh_roff=0):
    N, D = x.shape
    tm = _pick(math.gcd(N, dh_roff), 256, 8)
    ro = dh_roff // tm
    want_dx = resid is not None

    def body(*refs):
        if want_dx:
            x_ref, dh_ref, g_ref, sc_ref, res_ref, dx_ref, acc_ref = refs
        else:
            x_ref, dh_ref, g_ref, sc_ref, acc_ref = refs
        i = pl.program_id(0)

        @pl.when(i == 0)
        def _():
            acc_ref[...] = jnp.zeros_like(acc_ref)

        xv, dhv, gv = x_ref[...], dh_ref[...], g_ref[...]
        r = lax.rsqrt(jnp.mean(xv * xv, axis=-1, keepdims=True) + EPS)
        xh = xv * r
        dn = dhv * (1.0 + sc_ref[...])
        acc_ref[0:1, :] += jnp.sum(dhv, axis=0, keepdims=True)
        acc_ref[1:2, :] += jnp.sum(dhv * xh * gv, axis=0, keepdims=True)
        acc_ref[2:3, :] += jnp.sum(dn * xh, axis=0, keepdims=True)
        if want_dx:
            dxh = dn * gv
            dx_ref[...] = res_ref[...] + r * (dxh - xh * jnp.mean(dxh * xh, axis=-1, keepdims=True))

    row = pl.BlockSpec((tm, D), lambda i: (i, 0))
    drow = pl.BlockSpec((tm, D), lambda i: (i + ro, 0))
    vec = pl.BlockSpec((1, D), lambda i: (0, 0))
    acc = pl.BlockSpec((8, D), lambda i: (0, 0))
    acc_shape = jax.ShapeDtypeStruct((8, D), F32)
    if want_dx:
        return pl.pallas_call(
            body, name=name, grid=(N // tm,), in_specs=[row, drow, vec, vec, row],
            out_specs=[row, acc], out_shape=[jax.ShapeDtypeStruct((N, D), F32), acc_shape],
            compiler_params=_cparams("arbitrary"))(x, dh, g, sc, resid)
    sums = pl.pallas_call(
        body, name=name, grid=(N // tm,), in_specs=[row, drow, vec, vec],
        out_specs=acc, out_shape=acc_shape, compiler_params=_cparams("arbitrary"))(x, dh, g, sc)
    return None, sums


def _resnorm_bwd(x1, dh, g, sc, dy, mix, gt, name):
    N, D = x1.shape
    tm = _pick(N, 256, 8)

    def body(x_ref, dh_ref, g_ref, sc_ref, dy_ref, mix_ref, gt_ref, dx_ref, dm_ref, acc_ref):
        i = pl.program_id(0)

        @pl.when(i == 0)
        def _():
            acc_ref[...] = jnp.zeros_like(acc_ref)

        xv, dhv, gv = x_ref[...], dh_ref[...], g_ref[...]
        r = lax.rsqrt(jnp.mean(xv * xv, axis=-1, keepdims=True) + EPS)
        xh = xv * r
        dn = dhv * (1.0 + sc_ref[...])
        dxh = dn * gv
        dx = dy_ref[...] + r * (dxh - xh * jnp.mean(dxh * xh, axis=-1, keepdims=True))
        dx_ref[...] = dx
        dm_ref[...] = (dx * gt_ref[...]).astype(BF16)
        acc_ref[0:1, :] += jnp.sum(dhv, axis=0, keepdims=True)
        acc_ref[1:2, :] += jnp.sum(dhv * xh * gv, axis=0, keepdims=True)
        acc_ref[2:3, :] += jnp.sum(dn * xh, axis=0, keepdims=True)
        acc_ref[3:4, :] += jnp.sum(dx * mix_ref[...], axis=0, keepdims=True)

    row = pl.BlockSpec((tm, D), lambda i: (i, 0))
    vec = pl.BlockSpec((1, D), lambda i: (0, 0))
    return pl.pallas_call(
        body, name=name, grid=(N // tm,), in_specs=[row, row, vec, vec, row, row, vec],
        out_specs=[row, row, pl.BlockSpec((8, D), lambda i: (0, 0))],
        out_shape=[jax.ShapeDtypeStruct((N, D), F32), jax.ShapeDtypeStruct((N, D), BF16),
                   jax.ShapeDtypeStruct((8, D), F32)],
        compiler_params=_cparams("arbitrary"))(x1, dh, g, sc, dy, mix, gt)


def _qknorm_fwd(z, cblk, nrows, roff, w, cos2, sin2, nh, name):
    W = nh * HEAD_DIM
    tm = _pick(math.gcd(nrows, roff), 256, 8)
    ro = roff // tm
    assert roff % tm == 0

    def body(z_ref, w_ref, c_ref, s_ref, o_ref):
        c, s, wv = c_ref[...], s_ref[...], w_ref[...]
        for h in range(nh):
            x = z_ref[:, h * HEAD_DIM:(h + 1) * HEAD_DIM]
            r = lax.rsqrt(jnp.mean(x * x, axis=-1, keepdims=True) + EPS)
            y = x * r * wv
            o_ref[:, h * HEAD_DIM:(h + 1) * HEAD_DIM] = (y * c + pltpu.roll(y, HEAD_DIM // 2, 1) * s).astype(BF16)

    return pl.pallas_call(
        body, name=name, grid=(nrows // tm,),
        in_specs=[pl.BlockSpec((tm, W), lambda i: (i + ro, cblk)), pl.BlockSpec((1, HEAD_DIM), lambda i: (0, 0)),
                  pl.BlockSpec((tm, HEAD_DIM), lambda i: (i + ro, 0)), pl.BlockSpec((tm, HEAD_DIM), lambda i: (i + ro, 0))],
        out_specs=pl.BlockSpec((tm, W), lambda i: (i, 0)),
        out_shape=jax.ShapeDtypeStruct((nrows, W), BF16),
        compiler_params=_cparams("parallel"),
    )(z, w, cos2, sin2)


def _qknorm_bwd(z, cblk, nrows, roff, w, cos2, sin2, dy, nh, name):
    W = nh * HEAD_DIM
    tm = _pick(math.gcd(nrows, roff), 256, 8)
    ro = roff // tm

    def body(z_ref, w_ref, c_ref, s_ref, dy_ref, dz_ref, acc_ref):
        i = pl.program_id(0)

        @pl.when(i == 0)
        def _():
            acc_ref[...] = jnp.zeros_like(acc_ref)

        c, s, wv = c_ref[...], s_ref[...], w_ref[...]
        dw = jnp.zeros((1, HEAD_DIM), F32)
        for h in range(nh):
            sl = slice(h * HEAD_DIM, (h + 1) * HEAD_DIM)
            x = z_ref[:, sl]
            d = dy_ref[:, sl]
            dyn = d * c + pltpu.roll(d * s, HEAD_DIM // 2, 1)
            r = lax.rsqrt(jnp.mean(x * x, axis=-1, keepdims=True) + EPS)
            xh = x * r
            dw = dw + jnp.sum(dyn * xh, axis=0, keepdims=True)
            dxh = dyn * wv
            dz_ref[:, sl] = (r * (dxh - xh * jnp.mean(dxh * xh, axis=-1, keepdims=True))).astype(BF16)
        acc_ref[0:1, :] += dw

    return pl.pallas_call(
        body, name=name, grid=(nrows // tm,),
        in_specs=[pl.BlockSpec((tm, W), lambda i: (i + ro, cblk)), pl.BlockSpec((1, HEAD_DIM), lambda i: (0, 0)),
                  pl.BlockSpec((tm, HEAD_DIM), lambda i: (i + ro, 0)), pl.BlockSpec((tm, HEAD_DIM), lambda i: (i + ro, 0)),
                  pl.BlockSpec((tm, W), lambda i: (i, 0))],
        out_specs=[pl.BlockSpec((tm, W), lambda i: (i, 0)), pl.BlockSpec((8, HEAD_DIM), lambda i: (0, 0))],
        out_shape=[jax.ShapeDtypeStruct((nrows, W), BF16), jax.ShapeDtypeStruct((8, HEAD_DIM), F32)],
        compiler_params=_cparams("arbitrary"),
    )(z, w, cos2, sin2, dy)


def _cast_seg(z, cblk, width, name):
    R = z.shape[0]
    tm = _pick(R, 512, 8)

    def body(z_ref, o_ref):
        o_ref[...] = z_ref[...].astype(BF16)

    return pl.pallas_call(
        body, name=name, grid=(R // tm,),
        in_specs=[pl.BlockSpec((tm, width), lambda i: (i, cblk))],
        out_specs=pl.BlockSpec((tm, width), lambda i: (i, 0)),
        out_shape=jax.ShapeDtypeStruct((R, width), BF16), compiler_params=_cparams("parallel"))(z)


NEG_BIG = -1e30


KV_PER_STEP = 2


def _attn_specs(T, n_ctx):
    nb = T // WINDOW
    lb = n_ctx // WINDOW
    kvw = KV_PER_STEP * HEAD_DIM
    blk = lambda f: pl.BlockSpec((WINDOW, kvw), f)
    win = [blk(lambda h, i: (lb + jnp.maximum(i - 1, 0), h)), blk(lambda h, i: (lb + i, h)),
           blk(lambda h, i: (lb + jnp.minimum(i + 1, nb - 1), h))]
    ctx = pl.BlockSpec((n_ctx, kvw), lambda h, i: (0, h))
    qspec = pl.BlockSpec((WINDOW, KV_PER_STEP * Q_PER_KV * HEAD_DIM), lambda h, i: (i, h))
    sink = pl.BlockSpec((N_Q_HEADS, HEAD_DIM), lambda h, i: (0, 0))
    return nb, qspec, win, ctx, sink


def _attn_probs(q, kw, kctx, snk, valid):
    scale = HEAD_DIM ** -0.5
    s_lat = jnp.where(valid, _dot(q, kw, NT) * scale, NEG_BIG)
    s_ctx = _dot(q, kctx, NT) * scale
    m = jnp.maximum(jnp.maximum(jnp.max(s_lat, axis=-1, keepdims=True), jnp.max(s_ctx, axis=-1, keepdims=True)), snk)
    p_lat = jnp.exp(s_lat - m)
    p_ctx = jnp.exp(s_ctx - m)
    p_snk = jnp.exp(snk - m)
    den = p_snk + jnp.sum(p_lat, axis=-1, keepdims=True) + jnp.sum(p_ctx, axis=-1, keepdims=True)
    return p_lat, p_ctx, p_snk, den


def _attn_valid(i, T, heads):
    rows = heads * WINDOW
    qpos = i * WINDOW + (lax.broadcasted_iota(jnp.int32, (rows, 3 * WINDOW), 0) & (WINDOW - 1))
    kpos = (i - 1) * WINDOW + lax.broadcasted_iota(jnp.int32, (rows, 3 * WINDOW), 1)
    return (jnp.abs(qpos - kpos) <= WINDOW) & (kpos >= 0) & (kpos < T)


def _stack_heads(ref, hh):
    c0 = hh * Q_PER_KV * HEAD_DIM
    return jnp.concatenate([ref[:, c0 + g * HEAD_DIM:c0 + (g + 1) * HEAD_DIM] for g in range(Q_PER_KV)], axis=0)


def _stack_sinks(sink_ref, kvh):
    return jnp.concatenate([jnp.broadcast_to(sink_ref[pl.ds(kvh * Q_PER_KV + g, 1), :][:, 0:1], (WINDOW, 1))
                            for g in range(Q_PER_KV)], axis=0)


def _attn_window(refs, hh):
    return jnp.concatenate([r[:, hh * HEAD_DIM:(hh + 1) * HEAD_DIM] for r in refs], axis=0)


def _attn_fwd(qn, kn, vb, sink_rows, n_ctx, name, ride=None):
    T = qn.shape[0]
    nb, qspec, win, ctx, sink = _attn_specs(T, n_ctx)

    def body(q_ref, kp, kc, kx, vp, vc, vx, kctx_ref, vctx_ref, sink_ref, o_ref):
        h, i = pl.program_id(0), pl.program_id(1)
        valid = _attn_valid(i, T, Q_PER_KV)
        for hh in range(KV_PER_STEP):
            sl = slice(hh * HEAD_DIM, (hh + 1) * HEAD_DIM)
            kw, vw = _attn_window((kp, kc, kx), hh), _attn_window((vp, vc, vx), hh)
            kctx, vctx = kctx_ref[:, sl], vctx_ref[:, sl]
            p_lat, p_ctx, _, den = _attn_probs(_stack_heads(q_ref, hh), kw, kctx,
                                               _stack_sinks(sink_ref, h * KV_PER_STEP + hh), valid)
            o = ((_dot(p_lat.astype(BF16), vw, NN) + _dot(p_ctx.astype(BF16), vctx, NN)) / den).astype(BF16)
            for g in range(Q_PER_KV):
                c0 = (hh * Q_PER_KV + g) * HEAD_DIM
                o_ref[:, c0:c0 + HEAD_DIM] = o[g * WINDOW:(g + 1) * WINDOW]

    return _pcall(
        body, name=name, grid=(N_KV_HEADS // KV_PER_STEP, nb),
        in_specs=[qspec] + win + win + [ctx, ctx, sink],
        out_specs=qspec, out_shape=jax.ShapeDtypeStruct(qn.shape, BF16),
        sem=("parallel", "parallel"), args=(qn, kn, kn, kn, vb, vb, vb, kn, vb, sink_rows), ride=ride)


def _attn_bwd(qn, kn, vb, sink_rows, do, n_ctx, name, ride=None):
    T = qn.shape[0]
    nb, qspec, win, ctx, sink = _attn_specs(T, n_ctx)
    scale = HEAD_DIM ** -0.5
    TP = T + 2 * WINDOW

    def body(q_ref, kp, kc, kx, vp, vc, vx, kctx_ref, vctx_ref, sink_ref, do_ref,
             dq_ref, dkw_ref, dvw_ref, dkc_ref, dvc_ref, dsn_ref):
        h, i = pl.program_id(0), pl.program_id(1)

        @pl.when(i == 0)
        def _():
            dkw_ref[...] = jnp.zeros_like(dkw_ref)
            dvw_ref[...] = jnp.zeros_like(dvw_ref)
            dkc_ref[...] = jnp.zeros_like(dkc_ref)
            dvc_ref[...] = jnp.zeros_like(dvc_ref)
            dsn_ref[...] = jnp.zeros_like(dsn_ref)

        lane = lax.broadcasted_iota(jnp.int32, (8, HEAD_DIM), 1)
        valid = _attn_valid(i, T, Q_PER_KV)
        rows = pl.ds(pl.multiple_of(i * WINDOW, WINDOW), 3 * WINDOW)
        for hh in range(KV_PER_STEP):
            sl = slice(hh * HEAD_DIM, (hh + 1) * HEAD_DIM)
            kw, vw = _attn_window((kp, kc, kx), hh), _attn_window((vp, vc, vx), hh)
            kctx, vctx = kctx_ref[:, sl], vctx_ref[:, sl]
            q, d_o = _stack_heads(q_ref, hh), _stack_heads(do_ref, hh)
            p_lat, p_ctx, p_snk, den = _attn_probs(q, kw, kctx, _stack_sinks(sink_ref, h * KV_PER_STEP + hh), valid)
            inv = 1.0 / den
            p_lat, p_ctx, p_snk = p_lat * inv, p_ctx * inv, p_snk * inv
            dp_lat = _dot(d_o, vw, NT)
            dp_ctx = _dot(d_o, vctx, NT)
            dr = jnp.sum(p_lat * dp_lat, axis=-1, keepdims=True) + jnp.sum(p_ctx * dp_ctx, axis=-1, keepdims=True)
            ds_lat = (p_lat * (dp_lat - dr) * scale).astype(BF16)
            ds_ctx = (p_ctx * (dp_ctx - dr) * scale).astype(BF16)
            dq = _dot(ds_lat, kw, NN) + _dot(ds_ctx, kctx, NN)
            snk_terms = p_snk * dr
            dsn = jnp.zeros((8, HEAD_DIM), F32)
            for g in range(Q_PER_KV):
                c0 = (hh * Q_PER_KV + g) * HEAD_DIM
                dq_ref[:, c0:c0 + HEAD_DIM] = dq[g * WINDOW:(g + 1) * WINDOW]
                dsn = dsn + jnp.where(lane == g, -jnp.sum(snk_terms[g * WINDOW:(g + 1) * WINDOW], axis=0, keepdims=True),
                                      0.0)
            dkw_ref[rows, sl] += _dot(ds_lat, q, TN)
            dvw_ref[rows, sl] += _dot(p_lat.astype(BF16), d_o, TN)
            dkc_ref[:, sl] += _dot(ds_ctx, q, TN)
            dvc_ref[:, sl] += _dot(p_ctx.astype(BF16), d_o, TN)
            dsn_ref[hh] += dsn

    wacc = pl.BlockSpec((TP, KV_PER_STEP * HEAD_DIM), lambda h, i: (0, h))
    return _pcall(
        body, name=name, grid=(N_KV_HEADS // KV_PER_STEP, nb),
        in_specs=[qspec] + win + win + [ctx, ctx, sink, qspec],
        out_specs=[qspec, wacc, wacc, ctx, ctx, pl.BlockSpec((KV_PER_STEP, 8, HEAD_DIM), lambda h, i: (h, 0, 0))],
        out_shape=[jax.ShapeDtypeStruct(qn.shape, F32),
                   jax.ShapeDtypeStruct((TP, N_KV_HEADS * HEAD_DIM), F32),
                   jax.ShapeDtypeStruct((TP, N_KV_HEADS * HEAD_DIM), F32),
                   jax.ShapeDtypeStruct((n_ctx, N_KV_HEADS * HEAD_DIM), F32),
                   jax.ShapeDtypeStruct((n_ctx, N_KV_HEADS * HEAD_DIM), F32),
                   jax.ShapeDtypeStruct((N_KV_HEADS, 8, HEAD_DIM), F32)],
        sem=("arbitrary", "arbitrary"), args=(qn, kn, kn, kn, vb, vb, vb, kn, vb, sink_rows, do), ride=ride)


def _gla_masks(dirv):
    C = GLA_CHUNK

    def times(reps):
        r = lax.broadcasted_iota(jnp.int32, (C, reps * C), 0)
        c = lax.broadcasted_iota(jnp.int32, (C, reps * C), 1) & (C - 1)
        return jnp.where(dirv == 0, r, C - 1 - r), jnp.where(dirv == 0, c, C - 1 - c)

    def level(tt, ss, m):
        sh = m.bit_length() - 1
        same = (tt >> (sh + 1)) == (ss >> (sh + 1))
        return same, (tt >> sh) & 1, (ss >> sh) & 1

    tt, ss = times(3)
    le = (ss <= tt).astype(jnp.int32)
    sums = [le == 1]
    for m in GLA_LEVELS:
        same, ut, us = level(tt, ss, m)
        sums.append(same & (ut == us) & (ut == le))
    tt, ss = times(1)
    blocks = [ss == tt]
    for m in GLA_LEVELS:
        same, ut, us = level(tt, ss, m)
        blocks.append(same & (ut == 1) & (us == 0))
    mall3 = jnp.concatenate([jnp.where(s, 1.0, 0.0) for s in sums], axis=0).astype(BF16)
    return mall3, blocks


def _pieces(x):
    hi = x.astype(BF16)
    r1 = x - hi.astype(F32)
    mid = r1.astype(BF16)
    return hi, mid, (r1 - mid.astype(F32)).astype(BF16)


def _sum_f32(mall3, x):
    return _dot(mall3, jnp.concatenate(_pieces(x), axis=0), NN)


def _sum_f32_t(mall3, x):
    m = mall3[:, 0:GLA_CHUNK]
    hi, mid, lo = _pieces(x)
    return _dot(m, hi, TN) + _dot(m, mid, TN) + _dot(m, lo, TN)


def _gla_chunk_of(dirv, j, lc, nc):
    return jnp.where(dirv == 0, j, jnp.where(j < lc, lc - 1 - j, nc + lc - 1 - j))


def _gla_gate(lr_ref, wg_ref, bg_ref):
    pre = _dot(lr_ref[...].astype(BF16), wg_ref[0].astype(BF16), NN) + bg_ref[0]
    g = (jnp.minimum(pre, 0.0) - jnp.log(1.0 + jnp.exp(-jnp.abs(pre)))) * (1.0 / GLA_GATE_NORM)
    return pre, g


def _gla_fwd(z, qblk, kblk, vblk, lrblk, wg, bg, DV, n_ctx, name):
    R = z.shape[0]
    C = GLA_CHUNK
    DK = wg.shape[2] // GLA_HEADS
    nc, lc = R // C, n_ctx // C
    qscale = DK ** -0.5

    GK, GV = GLA_HEADS * DK, GLA_HEADS * DV

    def body(q_ref, k_ref, v_ref, lr_ref, wg_ref, bg_ref, o_ref, sp_ref, st_ref):
        dirv, j = pl.program_id(0), pl.program_id(1)

        @pl.when(j == 0)
        def _():
            st_ref[...] = jnp.zeros_like(st_ref)

        mall, blocks = _gla_masks(dirv)
        _, g_all = _gla_gate(lr_ref, wg_ref, bg_ref)
        E_all = _sum_f32(mall, g_all)
        for h in range(GLA_HEADS):
            ks, vs = slice(h * DK, (h + 1) * DK), slice(h * DV, (h + 1) * DV)
            q, k, v = q_ref[:, ks] * qscale, k_ref[:, ks], v_ref[:, vs].astype(BF16)
            g, E = g_all[:, ks], E_all[:, ks]
            st = st_ref[h]
            sp_ref[0, h, 0] = st
            A = jnp.where(blocks[0], _dot(q.astype(BF16), k.astype(BF16), NT), 0.0)
            for l in range(len(GLA_LEVELS)):
                e = jnp.exp(E[(1 + l) * C:(2 + l) * C])
                A = A + jnp.where(blocks[l + 1], _dot((q * e).astype(BF16), (k * e).astype(BF16), NT), 0.0)
            o_ref[0, :, vs] = (_dot((q * jnp.exp(E[0:C])).astype(BF16), st.astype(BF16), NT)
                               + _dot(A.astype(BF16), v, NN))
            last = jnp.sum(g, axis=0, keepdims=True)
            st_ref[h] = jnp.exp(last) * st + _dot(v, (k * jnp.exp(last - E[0:C])).astype(BF16), TN)

    chunk = functools.partial(_gla_chunk_of, lc=lc, nc=nc)
    return pl.pallas_call(
        body, name=name, grid=(2, nc),
        in_specs=[pl.BlockSpec((C, GK), lambda d, j: (chunk(d, j), qblk)),
                  pl.BlockSpec((C, GK), lambda d, j: (chunk(d, j), kblk)),
                  pl.BlockSpec((C, GV), lambda d, j: (chunk(d, j), vblk)),
                  pl.BlockSpec((C, LANES), lambda d, j: (chunk(d, j), lrblk)),
                  pl.BlockSpec((1, LANES, GK), lambda d, j: (d, 0, 0)),
                  pl.BlockSpec((1, 1, GK), lambda d, j: (d, 0, 0))],
        out_specs=[pl.BlockSpec((1, C, GV), lambda d, j: (d, chunk(d, j), 0)),
                   pl.BlockSpec((1, GLA_HEADS, 1, DV, DK), lambda d, j: (d, 0, j, 0, 0))],
        out_shape=[jax.ShapeDtypeStruct((2, R, GV), F32),
                   jax.ShapeDtypeStruct((2, GLA_HEADS, nc, DV, DK), F32)],
        scratch_shapes=[pltpu.VMEM((GLA_HEADS, DV, DK), F32)],
        compiler_params=_cparams("parallel", "arbitrary"),
    )(z, z, z, z, wg, bg)


def _gla_bwd(z, qblk, kblk, vblk, lrblk, wg, bg, sprev, do, n_ctx, name, ride=None):
    R = z.shape[0]
    C = GLA_CHUNK
    DK, DV = wg.shape[2] // GLA_HEADS, do.shape[1] // GLA_HEADS
    nc, lc = R // C, n_ctx // C
    qscale = DK ** -0.5
    nl = len(GLA_LEVELS)

    GK, GV = GLA_HEADS * DK, GLA_HEADS * DV

    def body(q_ref, k_ref, v_ref, lr_ref, wg_ref, bg_ref, sp_ref, do_ref,
             dq_ref, dk_ref, dv_ref, dpre_ref, dbg_ref, dst_ref):
        dirv, jr = pl.program_id(0), pl.program_id(1)

        @pl.when(jr == 0)
        def _():
            dst_ref[...] = jnp.zeros_like(dst_ref)
            dbg_ref[...] = jnp.zeros_like(dbg_ref)

        mall, blocks = _gla_masks(dirv)
        pre_all, g_all = _gla_gate(lr_ref, wg_ref, bg_ref)
        E_all = _sum_f32(mall, g_all)
        for h in range(GLA_HEADS):
            ks, vs = slice(h * DK, (h + 1) * DK), slice(h * DV, (h + 1) * DV)
            q, k, v = q_ref[:, ks] * qscale, k_ref[:, ks], v_ref[:, vs].astype(BF16)
            pre, g, E = pre_all[:, ks], g_all[:, ks], E_all[:, ks]
            last = jnp.sum(g, axis=0, keepdims=True)
            eb, er, decay = jnp.exp(E[0:C]), jnp.exp(last - E[0:C]), jnp.exp(last)
            st = sp_ref[0, h, 0]
            dst = dst_ref[h]
            d_o = do_ref[:, vs]
            qe, kd = q * eb, k * er
            qb, kb = q.astype(BF16), k.astype(BF16)
            A = jnp.where(blocks[0], _dot(qb, kb, NT), 0.0)
            levels = []
            for l in range(nl):
                e = jnp.exp(E[(1 + l) * C:(2 + l) * C])
                ql, kl = q * e, k * e
                levels.append((e, ql, kl, ql.astype(BF16), kl.astype(BF16)))
                A = A + jnp.where(blocks[l + 1], _dot(levels[l][3], levels[l][4], NT), 0.0)
            dA = _dot(d_o, v, NT)
            dv_ref[0, :, vs] = (_dot(A.astype(BF16), d_o, TN) + _dot(kd.astype(BF16), dst.astype(BF16), NT)).astype(BF16)
            dqe = _dot(d_o, st.astype(BF16), NN)
            dkd = _dot(v, dst.astype(BF16), NN)
            G = jnp.where(blocks[0], dA, 0.0).astype(BF16)
            dq = dqe * eb + _dot(G, kb, NN)
            dk = dkd * er + _dot(G, qb, TN)
            dEr = dkd * kd
            dE = [dqe * qe - dEr]
            for l in range(nl):
                e, ql, kl, qlb, klb = levels[l]
                G = jnp.where(blocks[l + 1], dA, 0.0).astype(BF16)
                dql = _dot(G, klb, NN)
                dkl = _dot(G, qlb, TN)
                dq = dq + dql * e
                dk = dk + dkl * e
                dE.append(dql * ql + dkl * kl)
            dlast = jnp.sum(dst * st, axis=0, keepdims=True) * decay + jnp.sum(dEr, axis=0, keepdims=True)
            dg = _sum_f32_t(mall, jnp.concatenate(dE, axis=0)) + dlast
            dpre = dg * (1.0 / GLA_GATE_NORM) / (1.0 + jnp.exp(pre))
            dq_ref[0, :, ks] = (dq * qscale).astype(BF16)
            dk_ref[0, :, ks] = dk.astype(BF16)
            dpre_ref[:, ks] = dpre.astype(BF16)
            dbg_ref[0, :, ks] += jnp.sum(dpre, axis=0, keepdims=True)
            dst_ref[h] = decay * dst + _dot(d_o, qe.astype(BF16), TN)

    def chunk(d, jr):
        return _gla_chunk_of(d, nc - 1 - jr, lc, nc)

    return _pcall(
        body, name=name, grid=(2, nc),
        in_specs=[pl.BlockSpec((C, GK), lambda d, j: (chunk(d, j), qblk)),
                  pl.BlockSpec((C, GK), lambda d, j: (chunk(d, j), kblk)),
                  pl.BlockSpec((C, GV), lambda d, j: (chunk(d, j), vblk)),
                  pl.BlockSpec((C, LANES), lambda d, j: (chunk(d, j), lrblk)),
                  pl.BlockSpec((1, LANES, GK), lambda d, j: (d, 0, 0)),
                  pl.BlockSpec((1, 1, GK), lambda d, j: (d, 0, 0)),
                  pl.BlockSpec((1, GLA_HEADS, 1, DV, DK), lambda d, j: (d, 0, nc - 1 - j, 0, 0)),
                  pl.BlockSpec((C, GV), lambda d, j: (chunk(d, j), 0))],
        out_specs=[pl.BlockSpec((1, C, GK), lambda d, j: (d, chunk(d, j), 0)),
                   pl.BlockSpec((1, C, GK), lambda d, j: (d, chunk(d, j), 0)),
                   pl.BlockSpec((1, C, GV), lambda d, j: (d, chunk(d, j), 0)),
                   pl.BlockSpec((C, GK), lambda d, j: (chunk(d, j), d)),
                   pl.BlockSpec((1, 1, GK), lambda d, j: (d, 0, 0))],
        out_shape=[jax.ShapeDtypeStruct((2, R, GK), BF16),
                   jax.ShapeDtypeStruct((2, R, GK), BF16),
                   jax.ShapeDtypeStruct((2, R, GV), BF16),
                   jax.ShapeDtypeStruct((R, 2 * GK), BF16),
                   jax.ShapeDtypeStruct((2, 1, GK), F32)],
        scratch_shapes=[pltpu.VMEM((GLA_HEADS, DV, DK), F32)],
        sem=("arbitrary", "arbitrary"), args=(z, z, z, z, wg, bg, sprev, do), ride=ride)


def _glanorm_fwd(o, z, rbblk, gn, n_ctx, name):
    _, R, GV = o.shape
    T = R - n_ctx
    DV = GV // GLA_HEADS
    tm = _pick(n_ctx, 256, 8)
    ro = n_ctx // tm

    def body(o0_ref, o1_ref, rb_ref, gn_ref, p_ref):
        gnv = gn_ref[...]
        for h in range(GLA_HEADS):
            sl = slice(h * DV, (h + 1) * DV)
            og = o0_ref[0, :, sl] + o1_ref[0, :, sl]
            r = lax.rsqrt(jnp.mean(og * og, axis=-1, keepdims=True) + EPS)
            p_ref[:, sl] = (og * r * gnv * _silu(rb_ref[:, sl])).astype(BF16)

    return pl.pallas_call(
        body, name=name, grid=(T // tm,),
        in_specs=[pl.BlockSpec((1, tm, GV), lambda i: (0, i + ro, 0)), pl.BlockSpec((1, tm, GV), lambda i: (1, i + ro, 0)),
                  pl.BlockSpec((tm, GV), lambda i: (i + ro, rbblk)), pl.BlockSpec((1, DV), lambda i: (0, 0))],
        out_specs=pl.BlockSpec((tm, GV), lambda i: (i, 0)),
        out_shape=jax.ShapeDtypeStruct((T, GV), BF16), compiler_params=_cparams("parallel"))(o, o, z, gn)


def _glanorm_bwd(o, z, rbblk, gn, dp, n_ctx, name):
    _, R, GV = o.shape
    T = R - n_ctx
    DV = GV // GLA_HEADS
    tm = _pick(n_ctx, 256, 8)
    ro = n_ctx // tm

    def body(o0_ref, o1_ref, rb_ref, gn_ref, dp_ref, do_ref, drb_ref, acc_ref):
        i = pl.program_id(0)

        @pl.when(i == 0)
        def _():
            acc_ref[...] = jnp.zeros_like(acc_ref)

        gnv = gn_ref[...]
        dgn = jnp.zeros((1, DV), F32)
        for h in range(GLA_HEADS):
            sl = slice(h * DV, (h + 1) * DV)
            og = o0_ref[0, :, sl] + o1_ref[0, :, sl]
            rb = rb_ref[:, sl]
            d = dp_ref[:, sl]
            r = lax.rsqrt(jnp.mean(og * og, axis=-1, keepdims=True) + EPS)
            xh = og * r
            drb_ref[:, sl] = (d * xh * gnv * _dsilu(rb)).astype(BF16)
            dn = d * _silu(rb)
            dgn = dgn + jnp.sum(dn * xh, axis=0, keepdims=True)
            dxh = dn * gnv
            do_ref[:, sl] = (r * (dxh - xh * jnp.mean(dxh * xh, axis=-1, keepdims=True))).astype(BF16)
        acc_ref[0:1, :] += dgn

    row = pl.BlockSpec((tm, GV), lambda i: (i, 0))
    return pl.pallas_call(
        body, name=name, grid=(T // tm,),
        in_specs=[pl.BlockSpec((1, tm, GV), lambda i: (0, i + ro, 0)), pl.BlockSpec((1, tm, GV), lambda i: (1, i + ro, 0)),
                  pl.BlockSpec((tm, GV), lambda i: (i + ro, rbblk)), pl.BlockSpec((1, DV), lambda i: (0, 0)), row],
        out_specs=[row, row, pl.BlockSpec((8, DV), lambda i: (0, 0))],
        out_shape=[jax.ShapeDtypeStruct((T, GV), BF16), jax.ShapeDtypeStruct((T, GV), BF16),
                   jax.ShapeDtypeStruct((8, DV), F32)],
        compiler_params=_cparams("arbitrary"))(o, o, z, gn, dp)


def _gate_fwd(z, gablk, gbblk, ya, yg, n_ctx, name):
    T, D = ya.shape
    tm = _pick(n_ctx, 256, 8)
    ro = n_ctx // tm

    def body(ga_ref, gb_ref, ya_ref, yg_ref, m_ref):
        m_ref[...] = (_sigmoid(ga_ref[...]) * ya_ref[...] + _sigmoid(gb_ref[...]) * yg_ref[...]).astype(BF16)

    row = pl.BlockSpec((tm, D), lambda i: (i, 0))
    return pl.pallas_call(
        body, name=name, grid=(T // tm,),
        in_specs=[pl.BlockSpec((tm, D), lambda i: (i + ro, gablk)), pl.BlockSpec((tm, D), lambda i: (i + ro, gbblk)), row, row],
        out_specs=row, out_shape=jax.ShapeDtypeStruct((T, D), BF16), compiler_params=_cparams("parallel"))(z, z, ya, yg)


def _gate_bwd(z, gablk, gbblk, ya, yg, dm, n_ctx, name):
    T, D = ya.shape
    tm = _pick(n_ctx, 256, 8)
    ro = n_ctx // tm

    def body(ga_ref, gb_ref, ya_ref, yg_ref, dm_ref, dya_ref, dyg_ref, dga_ref, dgb_ref):
        d = dm_ref[...]
        sa, sb = _sigmoid(ga_ref[...]), _sigmoid(gb_ref[...])
        dya_ref[...] = (d * sa).astype(BF16)
        dyg_ref[...] = (d * sb).astype(BF16)
        dga_ref[...] = (d * ya_ref[...] * sa * (1.0 - sa)).astype(BF16)
        dgb_ref[...] = (d * yg_ref[...] * sb * (1.0 - sb)).astype(BF16)

    row = pl.BlockSpec((tm, D), lambda i: (i, 0))
    sh = jax.ShapeDtypeStruct((T, D), BF16)
    return pl.pallas_call(
        body, name=name, grid=(T // tm,),
        in_specs=[pl.BlockSpec((tm, D), lambda i: (i + ro, gablk)), pl.BlockSpec((tm, D), lambda i: (i + ro, gbblk)), row, row, row],
        out_specs=[row] * 4, out_shape=[sh] * 4, compiler_params=_cparams("parallel"))(z, z, ya, yg, dm)


def _resnorm_fwd(x, mix, gt, g, sc, sh, name):
    T, D = x.shape
    tm = _pick(T, 256, 8)

    def body(x_ref, mix_ref, gt_ref, g_ref, sc_ref, sh_ref, x1_ref, h_ref):
        x1 = x_ref[...] + gt_ref[...] * mix_ref[...]
        x1_ref[...] = x1
        r = lax.rsqrt(jnp.mean(x1 * x1, axis=-1, keepdims=True) + EPS)
        h_ref[...] = (x1 * r * g_ref[...] * (1.0 + sc_ref[...]) + sh_ref[...]).astype(BF16)

    row = pl.BlockSpec((tm, D), lambda i: (i, 0))
    vec = pl.BlockSpec((1, D), lambda i: (0, 0))
    return pl.pallas_call(
        body, name=name, grid=(T // tm,), in_specs=[row, row, vec, vec, vec, vec], out_specs=[row, row],
        out_shape=[jax.ShapeDtypeStruct((T, D), F32), jax.ShapeDtypeStruct((T, D), BF16)],
        compiler_params=_cparams("parallel"))(x, mix, gt, g, sc, sh)


def _loss_head(d, x1, gt, target, name):
    T, D = d.shape
    tm = _pick(T, 256, 8)

    def body(d_ref, x1_ref, gt_ref, t_ref, dy_ref, dd_ref, acc_ref):
        i = pl.program_id(0)

        @pl.when(i == 0)
        def _():
            acc_ref[...] = jnp.zeros_like(acc_ref)

        dv, gtv = d_ref[...], gt_ref[...]
        e = x1_ref[...] + gtv * dv - t_ref[...]
        dy = e * (1.0 / D)
        dy_ref[...] = dy
        dd_ref[...] = (dy * gtv).astype(BF16)
        acc_ref[0:1, :] += jnp.sum(e * e, axis=0, keepdims=True)
        acc_ref[1:2, :] += jnp.sum(dy * dv, axis=0, keepdims=True)

    row = pl.BlockSpec((tm, D), lambda i: (i, 0))
    return pl.pallas_call(
        body, name=name, grid=(T // tm,), in_specs=[row, row, pl.BlockSpec((1, D), lambda i: (0, 0)), row],
        out_specs=[row, row, pl.BlockSpec((8, D), lambda i: (0, 0))],
        out_shape=[jax.ShapeDtypeStruct((T, D), F32), jax.ShapeDtypeStruct((T, D), BF16),
                   jax.ShapeDtypeStruct((8, D), F32)],
        compiler_params=_cparams("arbitrary"))(d, x1, gt, target)


def _halo_specs(T, tm, tw, col_of, order):
    n8 = tm // 8
    if order == "ij":
        mid = lambda i, j: (i, col_of(j))
        prev = lambda i, j: (jnp.maximum(i * n8 - 1, 0), col_of(j))
        nxt = lambda i, j: (jnp.minimum((i + 1) * n8, T // 8 - 1), col_of(j))
    else:
        mid = lambda j, i: (i, col_of(j))
        prev = lambda j, i: (jnp.maximum(i * n8 - 1, 0), col_of(j))
        nxt = lambda j, i: (jnp.minimum((i + 1) * n8, T // 8 - 1), col_of(j))
    return [pl.BlockSpec((tm, tw), mid), pl.BlockSpec((8, tw), prev), pl.BlockSpec((8, tw), nxt)]


def _shift_rows(x, before, after):
    tm = x.shape[0]
    row = lax.broadcasted_iota(jnp.int32, x.shape, 0)
    return (jnp.where(row == 0, before, pltpu.roll(x, 1, 0)),
            jnp.where(row == tm - 1, after, pltpu.roll(x, tm - 1, 0)))


def _conv_fwd(u, cw, cb, name):
    T, F2 = u.shape
    F = F2 // 2
    tm, tw = _pick(T, 256, 8), _pick(F, 512)
    nt, nw = T // tm, F // tw

    def body(ua, uap, uan, ug, ugp, ugn, cwa, cwg, cba, cbg, f_ref):
        i = pl.program_id(0)
        first, last = i == 0, i == nt - 1

        def conv(u_ref, up_ref, un_ref, w_ref, b_ref):
            m = u_ref[...]
            p, n = _shift_rows(m, jnp.where(first, 0.0, up_ref[7:8, :]), jnp.where(last, 0.0, un_ref[0:1, :]))
            return p * w_ref[0:1, :] + m * w_ref[1:2, :] + n * w_ref[2:3, :] + b_ref[...]

        a = conv(ua, uap, uan, cwa, cba)
        g = conv(ug, ugp, ugn, cwg, cbg)
        f_ref[...] = (_silu(a) * g).astype(BF16)

    wspec = lambda off: pl.BlockSpec((3, tw), lambda i, j: (0, j + off))
    bspec = lambda off: pl.BlockSpec((1, tw), lambda i, j: (0, j + off))
    return pl.pallas_call(
        body, name=name, grid=(nt, nw),
        in_specs=_halo_specs(T, tm, tw, lambda j: j, "ij") + _halo_specs(T, tm, tw, lambda j: j + nw, "ij")
        + [wspec(0), wspec(nw), bspec(0), bspec(nw)],
        out_specs=pl.BlockSpec((tm, tw), lambda i, j: (i, j)),
        out_shape=jax.ShapeDtypeStruct((T, F), BF16),
        compiler_params=_cparams("parallel", "parallel"),
    )(u, u, u, u, u, u, cw, cw, cb, cb)


def _conv_bwd(u, df, cw, cb, name):
    T, F2 = u.shape
    F = F2 // 2
    tm, tw = _pick(T, 256, 8), _pick(F, 512)
    nt, nw = T // tm, F // tw

    def body(ua, uap, uan, ug, ugp, ugn, cwa, cwg, cba, cbg, df_ref, dfp, dfn, du_ref, acca_ref, accg_ref):
        i = pl.program_id(1)

        @pl.when(i == 0)
        def _():
            acca_ref[...] = jnp.zeros_like(acca_ref)
            accg_ref[...] = jnp.zeros_like(accg_ref)

        first, last = i == 0, i == nt - 1
        wa, wg, ba, bg = cwa[...], cwg[...], cba[...], cbg[...]

        def conv(p, m, n, w, b):
            return p * w[0:1] + m * w[1:2] + n * w[2:3] + b

        def grads(a, g, d):
            return d * g * _dsilu(a), d * _silu(a)

        xa, xg, d = ua[...], ug[...], df_ref[...]
        sa = _shift_rows(xa, jnp.where(first, 0.0, uap[7:8, :]), jnp.where(last, 0.0, uan[0:1, :]))
        sg = _shift_rows(xg, jnp.where(first, 0.0, ugp[7:8, :]), jnp.where(last, 0.0, ugn[0:1, :]))
        da, dg = grads(conv(sa[0], xa, sa[1], wa, ba), conv(sg[0], xg, sg[1], wg, bg), d)
        da_p, dg_p = grads(conv(uap[6:7, :], uap[7:8, :], xa[0:1], wa, ba),
                           conv(ugp[6:7, :], ugp[7:8, :], xg[0:1], wg, bg), dfp[7:8, :])
        da_n, dg_n = grads(conv(xa[tm - 1:tm], uan[0:1, :], uan[1:2, :], wa, ba),
                           conv(xg[tm - 1:tm], ugn[0:1, :], ugn[1:2, :], wg, bg), dfn[0:1, :])
        ta = _shift_rows(da, jnp.where(first, 0.0, da_p), jnp.where(last, 0.0, da_n))
        tg = _shift_rows(dg, jnp.where(first, 0.0, dg_p), jnp.where(last, 0.0, dg_n))
        du_ref[0] = (ta[1] * wa[0:1] + da * wa[1:2] + ta[0] * wa[2:3]).astype(BF16)
        du_ref[1] = (tg[1] * wg[0:1] + dg * wg[1:2] + tg[0] * wg[2:3]).astype(BF16)
        for t, (va, vg) in enumerate(((sa[0], sg[0]), (xa, xg), (sa[1], sg[1]))):
            acca_ref[t:t + 1, :] += jnp.sum(da * va, axis=0, keepdims=True)
            accg_ref[t:t + 1, :] += jnp.sum(dg * vg, axis=0, keepdims=True)
        acca_ref[3:4, :] += jnp.sum(da, axis=0, keepdims=True)
        accg_ref[3:4, :] += jnp.sum(dg, axis=0, keepdims=True)

    wspec = lambda off: pl.BlockSpec((3, tw), lambda j, i: (0, j + off))
    bspec = lambda off: pl.BlockSpec((1, tw), lambda j, i: (0, j + off))
    row = pl.BlockSpec((tm, tw), lambda j, i: (i, j))
    acc = pl.BlockSpec((8, tw), lambda j, i: (0, j))
    return pl.pallas_call(
        body, name=name, grid=(nw, nt),
        in_specs=_halo_specs(T, tm, tw, lambda j: j, "ji") + _halo_specs(T, tm, tw, lambda j: j + nw, "ji")
        + [wspec(0), wspec(nw), bspec(0), bspec(nw)] + _halo_specs(T, tm, tw, lambda j: j, "ji"),
        out_specs=[pl.BlockSpec((2, tm, tw), lambda j, i: (0, i, j)), acc, acc],
        out_shape=[jax.ShapeDtypeStruct((2, T, F), BF16),
                   jax.ShapeDtypeStruct((8, F), F32), jax.ShapeDtypeStruct((8, F), F32)],
        compiler_params=_cparams("parallel", "arbitrary"),
    )(u, u, u, u, u, u, cw, cw, cb, cb, df, df, df)


def _assemble_dz(lay, z_used, Z, n_ctx, dqa, drb, dga, dgb, dka, dva, dvg, dqg, dkg, dlr, name):
    T = dqa.shape[0]
    R = T + n_ctx
    tm = _pick(n_ctx, 128, 8)
    cb = n_ctx // tm

    def body(dqa_ref, drb_ref, dga_ref, dgb_ref, dka_ref, dva_ref, dvg0, dvg1, dqg0, dqg1, dkg0, dkg1, dlr_ref, o_ref):
        lat = pl.program_id(0) >= cb

        def put(seg, val):
            o_ref[:, lay[seg]:lay[seg] + val.shape[1]] = val.astype(BF16)

        def lat_only(ref):
            v = ref[...]
            return jnp.where(lat, v, jnp.zeros_like(v))

        put("qa", lat_only(dqa_ref))
        put("rb", lat_only(drb_ref))
        put("ga", lat_only(dga_ref))
        put("gb", lat_only(dgb_ref))
        put("ka", dka_ref[...])
        put("va", dva_ref[...])
        put("vb", dvg0[0].astype(F32) + dvg1[0].astype(F32))
        put("qb", dqg0[0].astype(F32) + dqg1[0].astype(F32))
        put("kb", dkg0[0].astype(F32) + dkg1[0].astype(F32))
        put("lr", dlr_ref[...])
        if Z > z_used:
            o_ref[:, z_used:] = jnp.zeros((tm, Z - z_used), BF16)

    lat_spec = lambda a: pl.BlockSpec((tm, a.shape[1]), lambda i: (jnp.maximum(i - cb, 0), 0))
    all_spec = lambda a: pl.BlockSpec((tm, a.shape[1]), lambda i: (i, 0))
    dir_specs = lambda a: [pl.BlockSpec((1, tm, a.shape[2]), lambda i: (0, i, 0)),
                           pl.BlockSpec((1, tm, a.shape[2]), lambda i: (1, i, 0))]
    return pl.pallas_call(
        body, name=name, grid=(R // tm,),
        in_specs=[lat_spec(dqa), lat_spec(drb), lat_spec(dga), lat_spec(dgb), all_spec(dka), all_spec(dva)]
        + dir_specs(dvg) + dir_specs(dqg) + dir_specs(dkg) + [all_spec(dlr)],
        out_specs=pl.BlockSpec((tm, Z), lambda i: (i, 0)),
        out_shape=jax.ShapeDtypeStruct((R, Z), BF16), compiler_params=_cparams("parallel"),
    )(dqa, drb, dga, dgb, dka, dva, dvg, dvg, dqg, dqg, dkg, dkg, dlr)


def _mod_fwd(ca, w, b, name):
    n, D = ca.shape
    N = w.shape[1]
    tn = _pick(N, 512)

    def body(c_ref, w_ref, b_ref, o_ref, s_ref):
        s = _silu(c_ref[...])
        s_ref[...] = s
        o_ref[...] = _dot(s.astype(BF16), w_ref[...].astype(BF16), NN) + b_ref[...]

    return pl.pallas_call(
        body, name=name, grid=(N // tn,),
        in_specs=[pl.BlockSpec((n, D), lambda j: (0, 0)), pl.BlockSpec((D, tn), lambda j: (0, j)),
                  pl.BlockSpec((1, tn), lambda j: (0, j))],
        out_specs=[pl.BlockSpec((n, tn), lambda j: (0, j)), pl.BlockSpec((n, D), lambda j: (0, 0))],
        out_shape=[jax.ShapeDtypeStruct((n, N), F32), jax.ShapeDtypeStruct((n, D), F32)],
        compiler_params=_cparams("arbitrary"))(ca, w, b)


def _silu_bwd(dsil, ca, name):
    def body(d_ref, c_ref, o_ref):
        o_ref[...] = d_ref[...] * _dsilu(c_ref[...])

    return pl.pallas_call(body, name=name, out_shape=jax.ShapeDtypeStruct(ca.shape, F32))(dsil, ca)


def _adam_math(w, g, m, v):
    c1 = 1.0 - ADAM_B1 ** ADAM_STEP
    c2 = 1.0 - ADAM_B2 ** ADAM_STEP
    mn = ADAM_B1 * m + (1.0 - ADAM_B1) * g
    vn = ADAM_B2 * v + (1.0 - ADAM_B2) * (g * g)
    return -ADAM_LR * ((mn / c1) / (jnp.sqrt(vn / c2) + ADAM_EPS) + ADAM_WD * w), mn, vn


def _adamw(w, g, m, v, name, ride=None):
    Rw, Cw = w.shape
    tr = _pick(Rw, 128, 8)

    def body(w_ref, g_ref, m_ref, v_ref, d_ref, mo_ref, vo_ref):
        d_ref[...], mo_ref[...], vo_ref[...] = _adam_math(w_ref[...], g_ref[...], m_ref[...], v_ref[...])

    row = pl.BlockSpec((tr, Cw), lambda i: (i, 0))
    sh = jax.ShapeDtypeStruct((Rw, Cw), F32)
    return _pcall(body, name=name, grid=(Rw // tr,), in_specs=[row] * 4, out_specs=[row] * 3, out_shape=[sh] * 3,
                  sem=("parallel",), args=(w, g, m, v), ride=ride)


HBM_SPEC = pl.BlockSpec(memory_space=pltpu.HBM)


def _exchange(inputs, out_shapes, stages, name):
    n_in, n_out = len(inputs), len(out_shapes)
    n = sum(len(s) for s in stages)

    def body(*refs):
        ins, outs = refs[:n_in], refs[n_in:n_in + n_out]
        send_sems, recv_sems = refs[n_in + n_out:]
        k = 0
        for stage in stages:
            copies = _stage_copies(stage, ins, outs, send_sems, recv_sems, k)
            for cp in copies:
                cp.start()
            for cp in copies:
                cp.wait()
            k += len(stage)

    return pl.pallas_call(
        body, name=name, in_specs=[HBM_SPEC] * n_in, out_specs=[HBM_SPEC] * n_out, out_shape=out_shapes,
        scratch_shapes=[pltpu.SemaphoreType.DMA((n,)), pltpu.SemaphoreType.DMA((n,))],
    )(*inputs)


def _stage_copies(stage, ins, outs, send_sems, recv_sems, k0=0):
    me = (lax.axis_index("x"), lax.axis_index("y"), lax.axis_index("c"))
    copies = []
    for k, ((skind, sidx), sfn, didx, dfn, flip) in enumerate(stage):
        src = (ins if skind == "in" else outs)[sidx].at[sfn(*me)]
        dst = outs[didx].at[dfn(*me)]
        if flip == (0, 0, 0):
            copies.append(pltpu.make_async_copy(src, dst, send_sems.at[k0 + k]))
        else:
            peer = tuple(1 - a if f else a for a, f in zip(me, flip))
            copies.append(pltpu.make_async_remote_copy(src, dst, send_sems.at[k0 + k], recv_sems.at[k0 + k],
                                                       device_id=peer, device_id_type=MESH))
    return copies


def _pcall(body, *, name, grid, in_specs, out_specs, out_shape, scratch_shapes=(), sem, args, ride=None):
    many = isinstance(out_shape, (list, tuple))
    out_specs, out_shape = (list(out_specs), list(out_shape)) if many else ([out_specs], [out_shape])
    if ride is None:
        res = pl.pallas_call(body, name=name, grid=grid, in_specs=list(in_specs), out_specs=out_specs,
                             out_shape=out_shape, scratch_shapes=list(scratch_shapes),
                             compiler_params=_cparams(*sem))(*args)
        return res if many else res[0]
    x_in, x_out, stage, aliases = ride
    n_in, n_out, n_scr, n_xin, n_xout = len(in_specs), len(out_specs), len(scratch_shapes), len(x_in), len(x_out)

    def wrapped(*refs):
        ins, xins = refs[:n_in], refs[n_in:n_in + n_xin]
        o0 = n_in + n_xin
        outs, xouts = refs[o0:o0 + n_out], refs[o0 + n_out:o0 + n_out + n_xout]
        s0 = o0 + n_out + n_xout
        scr, (send_sems, recv_sems) = refs[s0:s0 + n_scr], refs[s0 + n_scr:]
        first = functools.reduce(jnp.logical_and, [pl.program_id(d) == 0 for d in range(len(grid))])
        last = functools.reduce(jnp.logical_and, [pl.program_id(d) == grid[d] - 1 for d in range(len(grid))])

        @pl.when(first)
        def _():
            for cp in _stage_copies(stage, xins, xouts, send_sems, recv_sems):
                cp.start()

        body(*ins, *outs, *scr)

        @pl.when(last)
        def _():
            for cp in _stage_copies(stage, xins, xouts, send_sems, recv_sems):
                cp.wait()

    res = pl.pallas_call(
        wrapped, name=name, grid=grid, in_specs=list(in_specs) + [HBM_SPEC] * n_xin,
        out_specs=out_specs + [HBM_SPEC] * n_xout, out_shape=out_shape + list(x_out),
        scratch_shapes=list(scratch_shapes) + [pltpu.SemaphoreType.DMA((len(stage),)),
                                               pltpu.SemaphoreType.DMA((len(stage),))],
        input_output_aliases={n_in + a: n_out + b for a, b in aliases.items()},
        compiler_params=_cparams(*(["arbitrary"] * len(grid))))(*args, *x_in)
    main = res[:n_out]
    return (main if many else main[0]), list(res[n_out:])


FLIPS_ALL = [(0, 0, 1), (0, 1, 0), (0, 1, 1), (1, 0, 0), (1, 0, 1), (1, 1, 0), (1, 1, 1)]
FLIPS_CHIP = [(0, 1, 0), (1, 0, 0), (1, 1, 0)]


def _sum_slots(buf, name):
    n, r, w = buf.shape
    tr = _pick(r, 256, 8)

    def body(b_ref, o_ref):
        acc = b_ref[0]
        for s in range(1, n):
            acc = acc + b_ref[s]
        o_ref[...] = acc

    return pl.pallas_call(
        body, name=name, grid=(r // tr,), in_specs=[pl.BlockSpec((n, tr, w), lambda i: (0, i, 0))],
        out_specs=pl.BlockSpec((tr, w), lambda i: (i, 0)), out_shape=jax.ShapeDtypeStruct((r, w), F32),
        compiler_params=_cparams("parallel"))(buf)


def _allreduce_plan(buf):
    whole = lambda x, y, c: (slice(None), slice(None))
    slot = lambda x, y, c: (4 * x + 2 * y + c,)
    stage = [(("in", 0), whole, 0, slot, f) for f in [(0, 0, 0)] + FLIPS_ALL]
    return [jax.ShapeDtypeStruct((8,) + buf.shape, F32)], stage


def _allreduce(buf, name):
    shapes, stage = _allreduce_plan(buf)
    (slots,) = _exchange([buf], shapes, [stage], name + "_x")
    return _sum_slots(slots, name + "_sum")


def _gather_plan(shards, src):
    half = lambda a, c: pl.ds(c * (a.shape[0] // 2), a.shape[0] // 2)
    first, second = [], []
    for n, a in enumerate(shards):
        for f in FLIPS_CHIP:
            first.append((("in", n), lambda x, y, c, a=a: (half(a, c), slice(None)), n,
                          lambda x, y, c, a=a: (2 * x + y, half(a, c), slice(None)), f))
            peer_slot = lambda x, y, c, a=a, f=f: (2 * (x ^ f[0]) + (y ^ f[1]), half(a, c), slice(None))
            second.append(((src, n), peer_slot, n, peer_slot, (0, 0, 1)))
    outs = [jax.ShapeDtypeStruct((4,) + a.shape, a.dtype) for a in shards]
    return first, second, outs


def _allgather_weights(shards, name):
    first, second, outs = _gather_plan(shards, "out")
    return _exchange(shards, outs, [first, second], name)


def _place_own(buf, shard, svec, name):
    _, Rs, Cs = buf.shape
    tr = _pick(Rs, 256, 16)

    def body(s_ref, buf_ref, sh_ref, o_ref):
        o_ref[0] = sh_ref[...]

    grid_spec = pltpu.PrefetchScalarGridSpec(
        num_scalar_prefetch=1, grid=(Rs // tr,),
        in_specs=[pl.BlockSpec(memory_space=pl.ANY), pl.BlockSpec((tr, Cs), lambda i, s: (i, 0))],
        out_specs=pl.BlockSpec((1, tr, Cs), lambda i, s: (s[0], i, 0)))
    return pl.pallas_call(body, name=name, grid_spec=grid_spec, out_shape=jax.ShapeDtypeStruct(buf.shape, buf.dtype),
                          input_output_aliases={1: 0}, compiler_params=_cparams("arbitrary"))(svec, buf, shard)


def _add_pair(G, bufA, cvec, name):
    _, Rs, Cs = G.shape
    Rh = Rs // 2
    tr = _pick(Rh, 128, 16)
    nb = Rh // tr

    def body(c_ref, g_ref, a_ref, o_ref):
        o_ref[...] = (g_ref[...] + a_ref[...]).astype(BF16)

    grid_spec = pltpu.PrefetchScalarGridSpec(
        num_scalar_prefetch=1, grid=(4, nb),
        in_specs=[pl.BlockSpec((1, tr, Cs), lambda s, i, c_ref: (s, c_ref[0] * nb + i, 0)),
                  pl.BlockSpec((1, tr, Cs), lambda s, i, c_ref: (s, i, 0))],
        out_specs=pl.BlockSpec((1, tr, Cs), lambda s, i, c_ref: (s, i, 0)))
    return pl.pallas_call(body, name=name, grid_spec=grid_spec, out_shape=jax.ShapeDtypeStruct((4, Rh, Cs), BF16),
                          compiler_params=_cparams("parallel", "parallel"))(cvec, G, bufA)


def _sum_chips(G, bufA, bufB, cvec, svec, name):
    _, Rs, Cs = G.shape
    Rh = Rs // 2
    tr = _pick(Rh, 128, 16)
    nb = Rh // tr

    def body(c_ref, s_ref, g_ref, a_ref, b_ref, o_ref):
        o_ref[...] = (g_ref[0] + a_ref[0]) + b_ref[0].astype(F32) + b_ref[1].astype(F32) + b_ref[2].astype(F32)

    grid_spec = pltpu.PrefetchScalarGridSpec(
        num_scalar_prefetch=2, grid=(nb,),
        in_specs=[pl.BlockSpec((1, tr, Cs), lambda i, c, s: (s[0], c[0] * nb + i, 0)),
                  pl.BlockSpec((1, tr, Cs), lambda i, c, s: (s[0], i, 0)),
                  pl.BlockSpec((3, tr, Cs), lambda i, c, s: (0, i, 0))],
        out_specs=pl.BlockSpec((tr, Cs), lambda i, c, s: (i, 0)))
    return pl.pallas_call(body, name=name, grid_spec=grid_spec, out_shape=jax.ShapeDtypeStruct((Rh, Cs), F32),
                          compiler_params=_cparams("parallel"))(cvec, svec, G, bufA, bufB)


def _pair_plan(grads):
    Rh = [g.shape[1] // 2 for g in grads]
    whole3 = lambda x, y, c: (slice(None), slice(None), slice(None))
    stage = [(("in", n), lambda x, y, c, n=n: (slice(None), pl.ds((1 - c) * Rh[n], Rh[n]), slice(None)), n,
              whole3, (0, 0, 1)) for n in range(len(grads))]
    return [jax.ShapeDtypeStruct((4, Rh[n], g.shape[2]), F32) for n, g in enumerate(grads)], stage


def _chips_plan(P):
    stage = [(("in", n), lambda x, y, c, f=f: (2 * (x ^ f[0]) + (y ^ f[1]),), n, lambda x, y, c, k=k: (k,), f)
             for n in range(len(P)) for k, f in enumerate(FLIPS_CHIP)]
    return [jax.ShapeDtypeStruct((3,) + p.shape[1:], BF16) for p in P], stage


def _halves_plan(mine):
    whole2 = lambda x, y, c: (slice(None), slice(None))
    stage = [(("in", n), whole2, n, whole2, (0, 0, 1)) for n in range(len(mine))]
    return [jax.ShapeDtypeStruct(r.shape, F32) for r in mine], stage


def _adamw_halves(w, mine, other, m, v, cvec, name):
    Rs, Cs = w.shape
    Rh = Rs // 2
    tr = _pick(Rh, 128, 8)
    nb = Rh // tr

    def body(c_ref, w_ref, a_ref, b_ref, m_ref, v_ref, g_ref, d_ref, mo_ref, vo_ref):
        gv = jnp.where(pl.program_id(0) // nb == c_ref[0], a_ref[...], b_ref[...])
        g_ref[...] = gv
        d_ref[...], mo_ref[...], vo_ref[...] = _adam_math(w_ref[...], gv, m_ref[...], v_ref[...])

    row = pl.BlockSpec((tr, Cs), lambda i, c: (i, 0))
    hrow = pl.BlockSpec((tr, Cs), lambda i, c: (i % nb, 0))
    grid_spec = pltpu.PrefetchScalarGridSpec(num_scalar_prefetch=1, grid=(2 * nb,),
                                             in_specs=[row, hrow, hrow, row, row], out_specs=[row] * 4)
    return pl.pallas_call(body, name=name, grid_spec=grid_spec, out_shape=[jax.ShapeDtypeStruct((Rs, Cs), F32)] * 4,
                          compiler_params=_cparams("parallel"))(cvec, w, mine, other, m, v)


def _pack(arrays):
    flat = [a.reshape(-1).astype(F32) for a in arrays]
    meta, off = [], 0
    for a, f in zip(arrays, flat):
        meta.append((off, a.shape))
        off += f.shape[0]
    total = -(-off // (8 * LANES)) * (8 * LANES)
    flat.append(jnp.zeros((total - off,), F32))
    return jnp.concatenate(flat).reshape(total // LANES, LANES), meta


def _unpack(buf, meta):
    flat = buf.reshape(-1)
    out = []
    for off, shape in meta:
        size = 1
        for s in shape:
            size *= s
        out.append(flat[off:off + size].reshape(shape))
    return out


WEIGHT_NAMES = ["c_ctx", "w_mod", "b_mod", "g_mix", "w_in", "q_norm", "k_norm", "attn_sink", "w_gate_f", "b_gate_f",
                "w_gate_b", "b_gate_b", "gla_norm", "w_attn_o", "w_gla_o", "w_out", "g_ffn", "w_up", "conv_w",
                "conv_b", "w_down"]
BIG_NAMES = ["w_in", "w_attn_o", "w_gla_o", "w_out", "w_up", "w_down"]
SHARDED_SMALL = ["w_gate_f", "w_gate_b", "conv_w"]


def _layouts(D):
    aw, kvw, gk, gv = N_Q_HEADS * HEAD_DIM, N_KV_HEADS * HEAD_DIM, D // 2, D
    widths = {"qa": aw, "ka": kvw, "va": kvw, "qb": gk, "kb": gk, "vb": gv, "rb": gv, "lr": 2 * GLA_LOWRANK,
              "ga": D, "gb": D}
    orig, off = {}, 0
    for s in ["qa", "ka", "va", "qb", "kb", "vb", "rb", "lr", "ga", "gb"]:
        orig[s] = off
        off += widths[s]
    order = ["qa", "vb", "rb", "ga", "gb", "ka", "va", "qb", "kb", "lr"]
    lay, off = {}, 0
    for s in order:
        lay[s] = off
        off += LANES if s == "lr" else widths[s]
    align = {"qa": aw, "vb": D, "rb": D, "ga": D, "gb": D, "ka": kvw, "va": kvw, "qb": gk, "kb": gk,
             "lr": LANES}
    for s in order:
        assert lay[s] % align[s] == 0, (s, lay[s], align[s])
    return widths, orig, order, lay, off, -(-off // (2 * MXU_TILE)) * (2 * MXU_TILE)


def _rope_tables(T, L):
    t = jnp.arange(T)
    nf = HEAD_DIM // 4
    inv = ROPE_THETA ** (-jnp.arange(nf, dtype=F32) / nf)
    ang = jnp.concatenate([(t // GRID_W)[:, None] * inv, (t % GRID_W)[:, None] * inv], axis=-1)
    cos, sin = jnp.cos(ang), jnp.sin(ang)
    cos2 = jnp.concatenate([jnp.ones((L, HEAD_DIM), F32), jnp.concatenate([cos, cos], axis=-1)], axis=0)
    sin2 = jnp.concatenate([jnp.zeros((L, HEAD_DIM), F32), jnp.concatenate([-sin, sin], axis=-1)], axis=0)
    return cos2, sin2


def _step(x, c, ctx, loss_target, W, M, V):
    xi, yi, ci = lax.axis_index("x"), lax.axis_index("y"), lax.axis_index("c")
    chip = 2 * xi + yi
    dev = 2 * chip + ci
    south = (ci == 0).astype(F32)
    cvec = ci.reshape(1).astype(jnp.int32)
    svec = chip.reshape(1).astype(jnp.int32)
    T, D = x.shape[1], x.shape[2]
    L = ctx.shape[1]
    R = L + T
    F = 4 * W["w_down"].shape[1]
    GK, GV = D // 2, D
    DK, DV = GK // GLA_HEADS, GV // GLA_HEADS
    N6 = 6 * D
    N4 = N6 // 4
    widths, orig, order, lay, z_used, Z = _layouts(D)

    def place_cols(shard, full_cols):
        cols = shard.shape[-1]
        full = jnp.zeros(shard.shape[:-1] + (full_cols,), F32)
        return lax.dynamic_update_slice(full, shard * south, (0,) * (shard.ndim - 1) + (chip * cols,))

    c_rows = lax.dynamic_update_slice(jnp.zeros((8, D), F32), c, (dev, 0))
    bufa, meta = _pack([c_rows, place_cols(W["w_gate_f"][0], GK), place_cols(W["w_gate_b"][0], GK),
                        place_cols(W["conv_w"][0], 2 * F)])
    c_all, wgf, wgb, cw = _unpack(_allreduce(bufa, "gather_small"), meta)
    ca = jnp.concatenate([c_all, W["c_ctx"][None, :], jnp.zeros((7, D), F32)], axis=0)
    b_shard = lax.dynamic_slice(W["b_mod"], (0, chip * N4), (1, N4))
    mod_part, sil = _mod_fwd(ca, W["w_mod"][0], b_shard, "mod_fwd")
    slots = lax.dynamic_update_slice(jnp.zeros((4, 16, N4), F32), (mod_part * south)[None], (chip, 0, 0))
    mod_all = _allreduce(slots.reshape(64, N4), "gather_mod").reshape(4, 16, N4).transpose(1, 0, 2).reshape(16, N6)
    mx = lax.dynamic_slice(mod_all, (dev, 0), (1, N6)).reshape(6, 1, D)
    mc = mod_all[8].reshape(6, 1, D)

    sq = lambda a: a.reshape(a.shape[1:])
    shards = [sq(W[n]).astype(BF16) for n in BIG_NAMES]
    own = lambda g, s, n: _place_own(g, s, svec, "place_" + n)
    cols = lambda g: g.transpose(1, 0, 2).reshape(g.shape[1], 4 * g.shape[2])
    rows = lambda g: g.reshape(4 * g.shape[1], g.shape[2])
    w_in_f = cols(own(_allgather_weights(shards[:1], "gather_w_in")[0], shards[0], "w_in"))
    sc1 = jnp.stack([mc[1], mx[1]])
    sh1 = jnp.stack([mc[0], mx[0]])
    h = _modnorm_fwd(ctx[0], x[0], W["g_mix"], sc1, sh1, "modnorm1")
    seg = lambda s: w_in_f[:, orig[s]:orig[s] + widths[s]]
    w_cat = jnp.concatenate([jnp.pad(seg(s), ((0, 0), (0, LANES - widths[s]))) if s == "lr" else seg(s)
                             for s in order] + [jnp.zeros((D, Z - z_used), BF16)], axis=1)
    gather1, gather2, gather_outs = _gather_plan(shards[1:], "in")
    wg = jnp.zeros((2, LANES, GK), F32).at[0, :GLA_LOWRANK].set(wgf).at[1, GLA_LOWRANK:2 * GLA_LOWRANK].set(wgb)
    bg = jnp.stack([W["b_gate_f"], W["b_gate_b"]])
    cb = W["conv_b"]
    sink_rows = jnp.broadcast_to(W["attn_sink"][0][:, None], (N_Q_HEADS, HEAD_DIM))
    cos2, sin2 = _rope_tables(T, L)
    blk = lambda s, w: lay[s] // w

    z, landed = _matmul(h, w_cat, "nn", F32, "proj_in", tn=1536, ride=(shards[1:], gather_outs, gather1, {}))
    qn = _qknorm_fwd(z, blk("qa", widths["qa"]), T, L, W["q_norm"], cos2, sin2, N_Q_HEADS, "qnorm")
    kn = _qknorm_fwd(z, blk("ka", widths["ka"]), R, 0, W["k_norm"], cos2, sin2, N_KV_HEADS, "knorm")
    vb = _cast_seg(z, blk("va", widths["va"]), widths["va"], "vcast")
    o_attn, landed = _attn_fwd(qn, kn, vb, sink_rows, L, "attn_fwd",
                               ride=(landed, gather_outs, gather2, {n: n for n in range(len(landed))}))
    g_ao, g_go, g_out, g_up, g_dn = [own(g, s, n) for g, s, n in zip(landed, shards[1:], BIG_NAMES[1:])]
    w_ao, w_go, w_out, w_up, w_dn = rows(g_ao), rows(g_go), rows(g_out), cols(g_up), rows(g_dn)
    gla_blks = (blk("qb", GK), blk("kb", GK), blk("vb", GV), blk("lr", LANES))
    o_g, sprev = _gla_fwd(z, *gla_blks, wg, bg, DV, L, "gla_fwd")
    p = _glanorm_fwd(o_g, z, blk("rb", D), W["gla_norm"], L, "glanorm")
    ya = _matmul(o_attn, w_ao, "nn", BF16, "proj_attn_o")
    yg = _matmul(p, w_go, "nn", BF16, "proj_gla_o")
    m = _gate_fwd(z, blk("ga", D), blk("gb", D), ya, yg, L, "gate")
    mix = _matmul(m, w_out, "nn", F32, "proj_out")
    x1, h2 = _resnorm_fwd(x[0], mix, mx[2], W["g_ffn"], mx[4], mx[3], "resnorm2")
    u = _matmul(h2, w_up, "nn", F32, "ffn_up")
    f = _conv_fwd(u, cw, cb, "conv_swiglu")
    d = _matmul(f, w_dn, "nn", F32, "ffn_down", tk=2816)
    dy, dd, lacc = _loss_head(d, x1, mx[5], loss_target[0], "loss_head")
    loss = lax.psum((0.5 / D) * jnp.sum(lacc[0]), ("x", "y", "c"))

    gw_dn = _matmul(f, dd, "tn", F32, "ffn_down_dw")
    df = _matmul(dd, w_dn, "nt", F32, "ffn_down_dx")
    du, acca, accg = _conv_bwd(u, df, cw, cb, "conv_swiglu_bwd")
    gw_up = _matmul(h2, du, "tn", F32, "ffn_up_dw", tm=512, halves="b", col_shards=4)
    dh2 = _matmul(du, w_up, "nt", F32, "ffn_up_dx", tk=2816, halves="a")
    dx1, dmix, s2 = _resnorm_bwd(x1, dh2, W["g_ffn"], mx[4], dy, mix, mx[2], "resnorm2_bwd")
    gw_out = _matmul(m, dmix, "tn", F32, "proj_out_dw")
    dm = _matmul(dmix, w_out, "nt", BF16, "proj_out_dx")
    dya, dyg, dga, dgb = _gate_bwd(z, blk("ga", D), blk("gb", D), ya, yg, dm, L, "gate_bwd")
    gw_ao = _matmul(o_attn, dya, "tn", F32, "proj_attn_o_dw")
    do_attn = _matmul(dya, w_ao, "nt", BF16, "proj_attn_o_dx")
    gw_go = _matmul(p, dyg, "tn", F32, "proj_gla_o_dw")
    dp = _matmul(dyg, w_go, "nt", BF16, "proj_gla_o_dx")
    do_gla, drb, s_gn = _glanorm_bwd(o_g, z, blk("rb", D), W["gla_norm"], dp, L, "glanorm_bwd")
    do_pad = jnp.concatenate([jnp.zeros((L, GV), BF16), do_gla], axis=0)
    by_cols = lambda g: g.reshape(g.shape[0], 4, g.shape[1] // 4).transpose(1, 0, 2)
    by_rows = lambda g: g.reshape(4, g.shape[0] // 4, g.shape[1])
    early = [by_rows(gw_ao), by_rows(gw_go), by_rows(gw_out), gw_up, by_rows(gw_dn)]
    (dqg, dkg, dvg, dpre, dbg), pair_e = _gla_bwd(z, *gla_blks, wg, bg, sprev, do_pad, L, "gla_bwd",
                                                  ride=(early, *_pair_plan(early), {}))
    sums_e = [_add_pair(g, a, cvec, "reduce_early_add%d" % n) for n, (g, a) in enumerate(zip(early, pair_e))]
    wg_cat = jnp.concatenate([wg[0], wg[1]], axis=1)
    dlr = _matmul(dpre, wg_cat, "nt", BF16, "gla_gate_dx")
    dwg = _matmul(z[:, lay["lr"]:lay["lr"] + LANES], dpre, "tn", F32, "gla_gate_dw")
    (dqn, dkw, dvw, dkc, dvc, dsn), chips_e = _attn_bwd(qn, kn, vb, sink_rows, do_attn, L, "attn_bwd",
                                                        ride=(sums_e, *_chips_plan(sums_e), {}))
    mine_e = [_sum_chips(g, a, b, cvec, svec, "reduce_early_sum%d" % n)
              for n, (g, a, b) in enumerate(zip(early, pair_e, chips_e))]
    dqa, s_qn = _qknorm_bwd(z, blk("qa", widths["qa"]), T, L, W["q_norm"], cos2, sin2, dqn, N_Q_HEADS, "qnorm_bwd")
    dk_all = jnp.concatenate([dkc, dkw[WINDOW:WINDOW + T]], axis=0)
    dv_all = jnp.concatenate([dvc, dvw[WINDOW:WINDOW + T]], axis=0)
    dka, s_kn = _qknorm_bwd(z, blk("ka", widths["ka"]), R, 0, W["k_norm"], cos2, sin2, dk_all, N_KV_HEADS, "knorm_bwd")
    dz = _assemble_dz(lay, z_used, Z, L, dqa, drb, dga, dgb, dka, dv_all, dvg, dqg, dkg, dlr, "assemble_dz")
    gw_cat, other_e = _matmul(h, dz, "tn", F32, "proj_in_dw", tn=768, tk=2816,
                              ride=(mine_e, *_halves_plan(mine_e), {}))
    gw_in = jnp.concatenate([gw_cat[:, lay[s]:lay[s] + widths[s]] for s in ["qa", "ka", "va", "qb", "kb", "vb", "rb",
                                                                           "lr", "ga", "gb"]], axis=1)
    late = [by_cols(gw_in)]
    shapes, stage = _pair_plan(late)
    pair_l = _exchange(late, shapes, [stage], "reduce_late_pair")
    sums_l = [_add_pair(late[0], pair_l[0], cvec, "reduce_late_add")]
    dh, chips_l = _matmul(dz, w_cat, "nt", F32, "proj_in_dx", tk=4608, ride=(sums_l, *_chips_plan(sums_l), {}))
    mine_l = [_sum_chips(late[0], pair_l[0], chips_l[0], cvec, svec, "reduce_late_sum")]
    shapes, stage = _halves_plan(mine_l)
    other_l = _exchange(mine_l, shapes, [stage], "reduce_late_halves")
    mine, other = mine_l + mine_e, list(other_l) + other_e
    grad_x, s1 = _modnorm_bwd(x[0], dh, W["g_mix"], mx[1], dx1, "modnorm1_bwd", dh_roff=L)
    _, s1c = _modnorm_bwd(ctx[0], dh, W["g_mix"], mc[1], None, "modnorm1_ctx_bwd")

    dmod_x = jnp.concatenate([s1[0], s1[1], s2[3], s2[0], s2[1], lacc[1]])
    dmod_c = jnp.concatenate([s1c[0], s1c[1], jnp.zeros((4 * D,), F32)])
    dmod_rows = lax.dynamic_update_slice(jnp.zeros((9, N6), F32).at[8].set(dmod_c), dmod_x[None], (dev, 0))
    small = [dmod_rows, dmod_x + dmod_c, s1[2] + s1c[2], s_qn[0], s_kn[0], dsn[:, 0, :Q_PER_KV].reshape(N_Q_HEADS),
             dwg[:GLA_LOWRANK, :GK], dbg[0].reshape(GK), dwg[GLA_LOWRANK:2 * GLA_LOWRANK, GK:], dbg[1].reshape(GK),
             s_gn[0], s2[2], jnp.concatenate([acca[0:3], accg[0:3]], axis=1), jnp.concatenate([acca[3], accg[3]])]
    bufc, meta = _pack(small)
    (dmod_sum, g_b_mod, g_g_mix, g_q_norm, g_k_norm, g_sink, g_wgf, g_bgf, g_wgb, g_bgb, g_gla_norm, g_g_ffn,
     g_conv_w, g_conv_b) = _unpack(_allreduce(bufc, "reduce_small"), meta)
    dmod16 = lax.dynamic_slice(jnp.concatenate([dmod_sum, jnp.zeros((7, N6), F32)], axis=0), (0, chip * N4), (16, N4))
    g_w_mod = _matmul(sil, dmod16, "tn", F32, "mod_dw")
    dsil = _matmul(dmod16, W["w_mod"][0], "nt", F32, "mod_dx")
    g_c_ctx = _silu_bwd(_allreduce(dsil * south, "reduce_cctx"), ca, "silu_bwd")[8]

    cut = lambda g: lax.dynamic_slice(g, (0, chip * (g.shape[1] // 4)), (g.shape[0], g.shape[1] // 4))
    grads = {"c_ctx": g_c_ctx, "w_mod": g_w_mod[None], "b_mod": g_b_mod[None], "g_mix": g_g_mix[None],
             "q_norm": g_q_norm[None], "k_norm": g_k_norm[None], "attn_sink": g_sink[None],
             "w_gate_f": cut(g_wgf)[None], "b_gate_f": g_bgf[None], "w_gate_b": cut(g_wgb)[None],
             "b_gate_b": g_bgb[None], "gla_norm": g_gla_norm[None], "g_ffn": g_g_ffn[None],
             "conv_w": cut(g_conv_w)[None], "conv_b": g_conv_b[None]}

    delta, new_m, new_v = {}, {}, {}
    dl, mn, vn = _adamw(W["w_mod"][0], g_w_mod, M["w_mod"][0], V["w_mod"][0], "adamw_w_mod")
    delta["w_mod"], new_m["w_mod"], new_v["w_mod"] = dl[None], mn[None], vn[None]
    for n, a, b in zip(BIG_NAMES, mine, other):
        g, dl, mn, vn = _adamw_halves(sq(W[n]), a, b, sq(M[n]), sq(V[n]), cvec, "adamw_" + n)
        grads[n], delta[n], new_m[n], new_v[n] = g[None], dl[None], mn[None], vn[None]
    small_names = [n for n in WEIGHT_NAMES if n not in delta]
    packs = [_pack([src[n] for n in small_names]) for src in (W, grads, M, V)]
    meta = packs[0][1]
    outs = _adamw(packs[0][0], packs[1][0], packs[2][0], packs[3][0], "adamw_small")
    for res, o in zip((delta, new_m, new_v), outs):
        for n, a in zip(small_names, _unpack(o, meta)):
            res[n] = a
    return (loss, grad_x[None], *[grads[n] for n in WEIGHT_NAMES], *[delta[n] for n in WEIGHT_NAMES],
            *[new_m[n] for n in WEIGHT_NAMES], *[new_v[n] for n in WEIGHT_NAMES])


def kernel(x, c, ctx, c_ctx, w_mod, b_mod, g_mix, w_in, q_norm, k_norm, attn_sink, w_gate_f, b_gate_f, w_gate_b, b_gate_b, gla_norm, w_attn_o, w_gla_o, w_out, g_ffn, w_up, conv_w, conv_b, w_down, loss_target, m_c_ctx, m_w_mod, m_b_mod, m_g_mix, m_w_in, m_q_norm, m_k_norm, m_attn_sink, m_w_gate_f, m_b_gate_f, m_w_gate_b, m_b_gate_b, m_gla_norm, m_w_attn_o, m_w_gla_o, m_w_out, m_g_ffn, m_w_up, m_conv_w, m_conv_b, m_w_down, v_c_ctx, v_w_mod, v_b_mod, v_g_mix, v_w_in, v_q_norm, v_k_norm, v_attn_sink, v_w_gate_f, v_b_gate_f, v_w_gate_b, v_b_gate_b, v_gla_norm, v_w_attn_o, v_w_gla_o, v_w_out, v_g_ffn, v_w_up, v_conv_w, v_conv_b, v_w_down):
    W = dict(zip(WEIGHT_NAMES, (c_ctx, w_mod, b_mod, g_mix, w_in, q_norm, k_norm, attn_sink, w_gate_f, b_gate_f,
                                w_gate_b, b_gate_b, gla_norm, w_attn_o, w_gla_o, w_out, g_ffn, w_up, conv_w, conv_b,
                                w_down)))
    M = dict(zip(WEIGHT_NAMES, (m_c_ctx, m_w_mod, m_b_mod, m_g_mix, m_w_in, m_q_norm, m_k_norm, m_attn_sink,
                                m_w_gate_f, m_b_gate_f, m_w_gate_b, m_b_gate_b, m_gla_norm, m_w_attn_o, m_w_gla_o,
                                m_w_out, m_g_ffn, m_w_up, m_conv_w, m_conv_b, m_w_down)))
    V = dict(zip(WEIGHT_NAMES, (v_c_ctx, v_w_mod, v_b_mod, v_g_mix, v_w_in, v_q_norm, v_k_norm, v_attn_sink,
                                v_w_gate_f, v_b_gate_f, v_w_gate_b, v_b_gate_b, v_gla_norm, v_w_attn_o, v_w_gla_o,
                                v_w_out, v_g_ffn, v_w_up, v_conv_w, v_conv_b, v_w_down)))
    return _step(x, c, ctx, loss_target, W, M, V)
```

```python
import functools
import math

import jax
import jax.numpy as jnp
from jax import lax
from jax.experimental import pallas as pl
from jax.experimental.pallas import tpu as pltpu

F32 = jnp.float32
BF16 = jnp.bfloat16
MESH = pl.DeviceIdType.MESH

EPS = 1e-6
HEAD_DIM = 128
N_Q_HEADS = 16
N_KV_HEADS = 4
Q_PER_KV = N_Q_HEADS // N_KV_HEADS
WINDOW = 128
GLA_HEADS = 4
GLA_LOWRANK = 16
GLA_GATE_NORM = 16.0
GLA_CHUNK = 64
GRID_W = 64
ROPE_THETA = 10000.0
GLA_LEVELS = (32, 16, 8, 4, 2, 1)
LANES = 128
MXU_TILE = 256

ADAM_LR = 0.001
ADAM_B1 = 0.9
ADAM_B2 = 0.999
ADAM_EPS = 1e-08
ADAM_WD = 0.01
ADAM_STEP = 10

VMEM_LIMIT = 52 * 1024 * 1024


def _cparams(*sem):
    return pltpu.CompilerParams(dimension_semantics=sem, vmem_limit_bytes=VMEM_LIMIT)


def _pick(n, target, mult=LANES):
    best = None
    d = mult
    while d <= min(n, target):
        if n % d == 0:
            best = d
        d += mult
    return n if best is None else best


def _sigmoid(x):
    return 1.0 / (1.0 + jnp.exp(-x))


def _silu(x):
    return x * _sigmoid(x)


def _dsilu(x):
    s = _sigmoid(x)
    return s * (1.0 + x * (1.0 - s))


def _dot(a, b, dims):
    return lax.dot_general(a, b, (dims, ((), ())), preferred_element_type=F32)


NN = ((1,), (0,))
NT = ((1,), (1,))
TN = ((0,), (0,))


def _matmul(a, b, mode, out_dtype, name, tm=1024, tn=1024, tk=2048, ride=None, halves=None, col_shards=None):
    if halves == "a":
        assert mode == "nt"
        (_, M, Kh), (N, K2) = a.shape, b.shape
        K = 2 * Kh
    elif halves == "b":
        assert mode == "tn"
        (K, M), (_, K2, Nh) = a.shape, b.shape
        N = 2 * Nh
    elif mode == "nn":
        (M, K), (K2, N) = a.shape, b.shape
    elif mode == "nt":
        (M, K), (N, K2) = a.shape, b.shape
    else:
        (K, M), (K2, N) = a.shape, b.shape
    assert K == K2, (name, a.shape, b.shape)
    pick = lambda n, t: _pick(n, t, MXU_TILE) if n % MXU_TILE == 0 else _pick(n, t)
    tm, tn, tk = pick(M, tm), pick(N // 2 if halves == "b" else N, tn), pick(K // 2 if halves == "a" else K, tk)
    if col_shards is not None:
        tn = N // col_shards
    nk = K // tk
    dims = {"nn": NN, "nt": NT, "tn": TN}[mode]

    def body(a_ref, b_ref, o_ref, acc_ref):
        k = pl.program_id(2)

        @pl.when(k == 0)
        def _():
            acc_ref[...] = jnp.zeros_like(acc_ref)

        av = a_ref[0] if halves == "a" else a_ref[...]
        bv = b_ref[0] if halves == "b" else b_ref[...]
        acc_ref[...] += _dot(av.astype(BF16), bv.astype(BF16), dims)

        @pl.when(k == nk - 1)
        def _():
            o_ref[...] = acc_ref[...].astype(out_dtype).reshape(o_ref.shape)

    if halves == "a":
        per = (K // 2) // tk
        a_spec = pl.BlockSpec((1, tm, tk), lambda i, j, k: (k // per, i, k % per))
    elif mode == "tn":
        a_spec = pl.BlockSpec((tk, tm), lambda i, j, k: (k, i))
    else:
        a_spec = pl.BlockSpec((tm, tk), lambda i, j, k: (i, k))
    if halves == "b":
        per = (N // 2) // tn
        b_spec = pl.BlockSpec((1, tk, tn), lambda i, j, k: (j // per, k, j % per))
    elif mode == "nt":
        b_spec = pl.BlockSpec((tn, tk), lambda i, j, k: (j, k))
    else:
        b_spec = pl.BlockSpec((tk, tn), lambda i, j, k: (k, j))
    if col_shards is None:
        out_spec, out_shape = pl.BlockSpec((tm, tn), lambda i, j, k: (i, j)), (M, N)
    else:
        assert tn * col_shards == N, (name, tn, N)
        out_spec, out_shape = pl.BlockSpec((1, tm, tn), lambda i, j, k: (j, i, 0)), (col_shards, M, tn)
    return _pcall(
        body, name=name, grid=(M // tm, N // tn, nk),
        in_specs=[a_spec, b_spec],
        out_specs=out_spec,
        out_shape=jax.ShapeDtypeStruct(out_shape, out_dtype),
        scratch_shapes=[pltpu.VMEM((tm, tn), F32)],
        sem=("parallel", "parallel", "arbitrary"), args=(a, b), ride=ride)


def _modnorm_fwd(xc, xl, g, sc, sh, name, ride=None):
    (L, D), T = xc.shape, xl.shape[0]
    tm = _pick(math.gcd(L, T), 256, 8)
    cb = L // tm

    def body(xc_ref, xl_ref, g_ref, sc_ref, sh_ref, h_ref):
        x = jnp.where(pl.program_id(0) < cb, xc_ref[...], xl_ref[...])
        r = lax.rsqrt(jnp.mean(x * x, axis=-1, keepdims=True) + EPS)
        n = x * r * g_ref[...]
        h_ref[...] = (n * (1.0 + sc_ref[0]) + sh_ref[0]).astype(BF16)

    sel = lambda i: (jnp.where(i < cb, 0, 1), 0, 0)
    return _pcall(
        body, name=name, grid=((L + T) // tm,),
        in_specs=[pl.BlockSpec((tm, D), lambda i: (jnp.minimum(i, cb - 1), 0)),
                  pl.BlockSpec((tm, D), lambda i: (jnp.maximum(i - cb, 0), 0)),
                  pl.BlockSpec((1, D), lambda i: (0, 0)), pl.BlockSpec((1, 1, D), sel), pl.BlockSpec((1, 1, D), sel)],
        out_specs=pl.BlockSpec((tm, D), lambda i: (i, 0)),
        out_shape=jax.ShapeDtypeStruct((L + T, D), BF16),
        sem=("parallel",), args=(xc, xl, g, sc, sh), ride=ride)


def _modnorm_bwd(x, dh, g, sc, resid, name, dh_roff=0):
    N, D = x.shape
    tm = _pick(math.gcd(N, dh_roff), 256, 8)
    ro = dh_roff // tm
    want_dx = resid is not None

    def body(*refs):
        if want_dx:
            x_ref, dh_ref, g_ref, sc_ref, res_ref, dx_ref, acc_ref = refs
        else:
            x_ref, dh_ref, g_ref, sc_ref, acc_ref = refs
        i = pl.program_id(0)

        @pl.when(i == 0)
        def _():
            acc_ref[...] = jnp.zeros_like(acc_ref)

        xv, dhv, gv = x_ref[...], dh_ref[...], g_ref[...]
        r = lax.rsqrt(jnp.mean(xv * xv, axis=-1, keepdims=True) + EPS)
        xh = xv * r
        dn = dhv * (1.0 + sc_ref[...])
        acc_ref[0:1, :] += jnp.sum(dhv, axis=0, keepdims=True)
        acc_ref[1:2, :] += jnp.sum(dhv * xh * gv, axis=0, keepdims=True)
        acc_ref[2:3, :] += jnp.sum(dn * xh, axis=0, keepdims=True)
        if want_dx:
            dxh = dn * gv
            dx_ref[...] = res_ref[...] + r * (dxh - xh * jnp.mean(dxh * xh, axis=-1, keepdims=True))

    row = pl.BlockSpec((tm, D), lambda i: (i, 0))
    drow = pl.BlockSpec((tm, D), lambda i: (i + ro, 0))
    vec = pl.BlockSpec((1, D), lambda i: (0, 0))
    acc = pl.BlockSpec((8, D), lambda i: (0, 0))
    acc_shape = jax.ShapeDtypeStruct((8, D), F32)
    if want_dx:
        return pl.pallas_call(
            body, name=name, grid=(N // tm,), in_specs=[row, drow, vec, vec, row],
            out_specs=[row, acc], out_shape=[jax.ShapeDtypeStruct((N, D), F32), acc_shape],
            compiler_params=_cparams("arbitrary"))(x, dh, g, sc, resid)
    sums = pl.pallas_call(
        body, name=name, grid=(N // tm,), in_specs=[row, drow, vec, vec],
        out_specs=acc, out_shape=acc_shape, compiler_params=_cparams("arbitrary"))(x, dh, g, sc)
    return None, sums


def _resnorm_bwd(x1, dh, g, sc, dy, mix, gt, name):
    N, D = x1.shape
    tm = _pick(N, 256, 8)

    def body(x_ref, dh_ref, g_ref, sc_ref, dy_ref, mix_ref, gt_ref, dx_ref, dm_ref, acc_ref):
        i = pl.program_id(0)

        @pl.when(i == 0)
        def _():
            acc_ref[...] = jnp.zeros_like(acc_ref)

        xv, dhv, gv = x_ref[...], dh_ref[...], g_ref[...]
        r = lax.rsqrt(jnp.mean(xv * xv, axis=-1, keepdims=True) + EPS)
        xh = xv * r
        dn = dhv * (1.0 + sc_ref[...])
        dxh = dn * gv
        dx = dy_ref[...] + r * (dxh - xh * jnp.mean(dxh * xh, axis=-1, keepdims=True))
        dx_ref[...] = dx
        dm_ref[...] = (dx * gt_ref[...]).astype(BF16)
        acc_ref[0:1, :] += jnp.sum(dhv, axis=0, keepdims=True)
        acc_ref[1:2, :] += jnp.sum(dhv * xh * gv, axis=0, keepdims=True)
        acc_ref[2:3, :] += jnp.sum(dn * xh, axis=0, keepdims=True)
        acc_ref[3:4, :] += jnp.sum(dx * mix_ref[...], axis=0, keepdims=True)

    row = pl.BlockSpec((tm, D), lambda i: (i, 0))
    vec = pl.BlockSpec((1, D), lambda i: (0, 0))
    return pl.pallas_call(
        body, name=name, grid=(N // tm,), in_specs=[row, row, vec, vec, row, row, vec],
        out_specs=[row, row, pl.BlockSpec((8, D), lambda i: (0, 0))],
        out_shape=[jax.ShapeDtypeStruct((N, D), F32), jax.ShapeDtypeStruct((N, D), BF16),
                   jax.ShapeDtypeStruct((8, D), F32)],
        compiler_params=_cparams("arbitrary"))(x1, dh, g, sc, dy, mix, gt)


def _qknorm_fwd(z, cblk, nrows, roff, w, cos2, sin2, nh, name):
    W = nh * HEAD_DIM
    tm = _pick(math.gcd(nrows, roff), 256, 8)
    ro = roff // tm
    assert roff % tm == 0

    def body(z_ref, w_ref, c_ref, s_ref, o_ref):
        c, s, wv = c_ref[...], s_ref[...], w_ref[...]
        for h in range(nh):
            x = z_ref[:, h * HEAD_DIM:(h + 1) * HEAD_DIM]
            r = lax.rsqrt(jnp.mean(x * x, axis=-1, keepdims=True) + EPS)
            y = x * r * wv
            o_ref[:, h * HEAD_DIM:(h + 1) * HEAD_DIM] = (y * c + pltpu.roll(y, HEAD_DIM // 2, 1) * s).astype(BF16)

    return pl.pallas_call(
        body, name=name, grid=(nrows // tm,),
        in_specs=[pl.BlockSpec((tm, W), lambda i: (i + ro, cblk)), pl.BlockSpec((1, HEAD_DIM), lambda i: (0, 0)),
                  pl.BlockSpec((tm, HEAD_DIM), lambda i: (i + ro, 0)), pl.BlockSpec((tm, HEAD_DIM), lambda i: (i + ro, 0))],
        out_specs=pl.BlockSpec((tm, W), lambda i: (i, 0)),
        out_shape=jax.ShapeDtypeStruct((nrows, W), BF16),
        compiler_params=_cparams("parallel"),
    )(z, w, cos2, sin2)


def _qknorm_bwd(z, cblk, nrows, roff, w, cos2, sin2, dy, nh, name):
    W = nh * HEAD_DIM
    tm = _pick(math.gcd(nrows, roff), 256, 8)
    ro = roff // tm

    def body(z_ref, w_ref, c_ref, s_ref, dy_ref, dz_ref, acc_ref):
        i = pl.program_id(0)

        @pl.when(i == 0)
        def _():
            acc_ref[...] = jnp.zeros_like(acc_ref)

        c, s, wv = c_ref[...], s_ref[...], w_ref[...]
        dw = jnp.zeros((1, HEAD_DIM), F32)
        for h in range(nh):
            sl = slice(h * HEAD_DIM, (h + 1) * HEAD_DIM)
            x = z_ref[:, sl]
            d = dy_ref[:, sl]
            dyn = d * c + pltpu.roll(d * s, HEAD_DIM // 2, 1)
            r = lax.rsqrt(jnp.mean(x * x, axis=-1, keepdims=True) + EPS)
            xh = x * r
            dw = dw + jnp.sum(dyn * xh, axis=0, keepdims=True)
            dxh = dyn * wv
            dz_ref[:, sl] = (r * (dxh - xh * jnp.mean(dxh * xh, axis=-1, keepdims=True))).astype(BF16)
        acc_ref[0:1, :] += dw

    return pl.pallas_call(
        body, name=name, grid=(nrows // tm,),
        in_specs=[pl.BlockSpec((tm, W), lambda i: (i + ro, cblk)), pl.BlockSpec((1, HEAD_DIM), lambda i: (0, 0)),
                  pl.BlockSpec((tm, HEAD_DIM), lambda i: (i + ro, 0)), pl.BlockSpec((tm, HEAD_DIM), lambda i: (i + ro, 0)),
                  pl.BlockSpec((tm, W), lambda i: (i, 0))],
        out_specs=[pl.BlockSpec((tm, W), lambda i: (i, 0)), pl.BlockSpec((8, HEAD_DIM), lambda i: (0, 0))],
        out_shape=[jax.ShapeDtypeStruct((nrows, W), BF16), jax.ShapeDtypeStruct((8, HEAD_DIM), F32)],
        compiler_params=_cparams("arbitrary"),
    )(z, w, cos2, sin2, dy)


def _cast_seg(z, cblk, width, name):
    R = z.shape[0]
    tm = _pick(R, 512, 8)

    def body(z_ref, o_ref):
        o_ref[...] = z_ref[...].astype(BF16)

    return pl.pallas_call(
        body, name=name, grid=(R // tm,),
        in_specs=[pl.BlockSpec((tm, width), lambda i: (i, cblk))],
        out_specs=pl.BlockSpec((tm, width), lambda i: (i, 0)),
        out_shape=jax.ShapeDtypeStruct((R, width), BF16), compiler_params=_cparams("parallel"))(z)


NEG_BIG = -1e30


KV_PER_STEP = 2
KV_PER_STEP_FWD = 4


def _attn_specs(T, n_ctx, kv_per_step=KV_PER_STEP):
    nb = T // WINDOW
    lb = n_ctx // WINDOW
    kvw = kv_per_step * HEAD_DIM
    blk = lambda f: pl.BlockSpec((WINDOW, kvw), f)
    win = [blk(lambda h, i: (lb + jnp.maximum(i - 1, 0), h)), blk(lambda h, i: (lb + i, h)),
           blk(lambda h, i: (lb + jnp.minimum(i + 1, nb - 1), h))]
    ctx = pl.BlockSpec((n_ctx, kvw), lambda h, i: (0, h))
    qspec = pl.BlockSpec((WINDOW, kv_per_step * Q_PER_KV * HEAD_DIM), lambda h, i: (i, h))
    sink = pl.BlockSpec((N_Q_HEADS, HEAD_DIM), lambda h, i: (0, 0))
    return nb, qspec, win, ctx, sink


def _attn_probs(q, kw, kctx, snk, valid):
    scale = HEAD_DIM ** -0.5
    s_lat = jnp.where(valid, _dot(q, kw, NT) * scale, NEG_BIG)
    s_ctx = _dot(q, kctx, NT) * scale
    m = jnp.maximum(jnp.maximum(jnp.max(s_lat, axis=-1, keepdims=True), jnp.max(s_ctx, axis=-1, keepdims=True)), snk)
    p_lat = jnp.exp(s_lat - m)
    p_ctx = jnp.exp(s_ctx - m)
    p_snk = jnp.exp(snk - m)
    den = p_snk + jnp.sum(p_lat, axis=-1, keepdims=True) + jnp.sum(p_ctx, axis=-1, keepdims=True)
    return p_lat, p_ctx, p_snk, den


def _attn_valid(i, T, heads):
    rows = heads * WINDOW
    qpos = i * WINDOW + (lax.broadcasted_iota(jnp.int32, (rows, 3 * WINDOW), 0) & (WINDOW - 1))
    kpos = (i - 1) * WINDOW + lax.broadcasted_iota(jnp.int32, (rows, 3 * WINDOW), 1)
    return (jnp.abs(qpos - kpos) <= WINDOW) & (kpos >= 0) & (kpos < T)


def _stack_heads(ref, hh):
    c0 = hh * Q_PER_KV * HEAD_DIM
    return jnp.concatenate([ref[:, c0 + g * HEAD_DIM:c0 + (g + 1) * HEAD_DIM] for g in range(Q_PER_KV)], axis=0)


def _stack_sinks(sink_ref, kvh):
    return jnp.concatenate([jnp.broadcast_to(sink_ref[pl.ds(kvh * Q_PER_KV + g, 1), :][:, 0:1], (WINDOW, 1))
                            for g in range(Q_PER_KV)], axis=0)


def _attn_window(refs, hh):
    return jnp.concatenate([r[:, hh * HEAD_DIM:(hh + 1) * HEAD_DIM] for r in refs], axis=0)


def _attn_fwd(qn, kn, vb, sink_rows, n_ctx, name, ride=None):
    T = qn.shape[0]
    nb, qspec, win, ctx, sink = _attn_specs(T, n_ctx, KV_PER_STEP_FWD)

    def body(q_ref, kp, kc, kx, vp, vc, vx, kctx_ref, vctx_ref, sink_ref, o_ref):
        h, i = pl.program_id(0), pl.program_id(1)
        valid = _attn_valid(i, T, Q_PER_KV)
        for hh in range(KV_PER_STEP_FWD):
            sl = slice(hh * HEAD_DIM, (hh + 1) * HEAD_DIM)
            kw, vw = _attn_window((kp, kc, kx), hh), _attn_window((vp, vc, vx), hh)
            kctx, vctx = kctx_ref[:, sl], vctx_ref[:, sl]
            p_lat, p_ctx, _, den = _attn_probs(_stack_heads(q_ref, hh), kw, kctx,
                                               _stack_sinks(sink_ref, h * KV_PER_STEP_FWD + hh), valid)
            o = ((_dot(p_lat.astype(BF16), vw, NN) + _dot(p_ctx.astype(BF16), vctx, NN)) / den).astype(BF16)
            for g in range(Q_PER_KV):
                c0 = (hh * Q_PER_KV + g) * HEAD_DIM
                o_ref[:, c0:c0 + HEAD_DIM] = o[g * WINDOW:(g + 1) * WINDOW]

    return _pcall(
        body, name=name, grid=(N_KV_HEADS // KV_PER_STEP_FWD, nb),
        in_specs=[qspec] + win + win + [ctx, ctx, sink],
        out_specs=qspec, out_shape=jax.ShapeDtypeStruct(qn.shape, BF16),
        sem=("parallel", "parallel"), args=(qn, kn, kn, kn, vb, vb, vb, kn, vb, sink_rows), ride=ride)


def _attn_bwd(qn, kn, vb, sink_rows, do, n_ctx, name, ride=None):
    T = qn.shape[0]
    nb, qspec, win, ctx, sink = _attn_specs(T, n_ctx)
    scale = HEAD_DIM ** -0.5
    TP = T + 2 * WINDOW

    def body(q_ref, kp, kc, kx, vp, vc, vx, kctx_ref, vctx_ref, sink_ref, do_ref,
             dq_ref, dkw_ref, dvw_ref, dkc_ref, dvc_ref, dsn_ref):
        h, i = pl.program_id(0), pl.program_id(1)

        @pl.when(i == 0)
        def _():
            dkw_ref[...] = jnp.zeros_like(dkw_ref)
            dvw_ref[...] = jnp.zeros_like(dvw_ref)
            dkc_ref[...] = jnp.zeros_like(dkc_ref)
            dvc_ref[...] = jnp.zeros_like(dvc_ref)
            dsn_ref[...] = jnp.zeros_like(dsn_ref)

        lane = lax.broadcasted_iota(jnp.int32, (8, HEAD_DIM), 1)
        valid = _attn_valid(i, T, Q_PER_KV)
        rows = pl.ds(pl.multiple_of(i * WINDOW, WINDOW), 3 * WINDOW)
        for hh in range(KV_PER_STEP):
            sl = slice(hh * HEAD_DIM, (hh + 1) * HEAD_DIM)
            kw, vw = _attn_window((kp, kc, kx), hh), _attn_window((vp, vc, vx), hh)
            kctx, vctx = kctx_ref[:, sl], vctx_ref[:, sl]
            q, d_o = _stack_heads(q_ref, hh), _stack_heads(do_ref, hh)
            p_lat, p_ctx, p_snk, den = _attn_probs(q, kw, kctx, _stack_sinks(sink_ref, h * KV_PER_STEP + hh), valid)
            inv = 1.0 / den
            p_lat, p_ctx, p_snk = p_lat * inv, p_ctx * inv, p_snk * inv
            dp_lat = _dot(d_o, vw, NT)
            dp_ctx = _dot(d_o, vctx, NT)
            dr = jnp.sum(p_lat * dp_lat, axis=-1, keepdims=True) + jnp.sum(p_ctx * dp_ctx, axis=-1, keepdims=True)
            ds_lat = (p_lat * (dp_lat - dr) * scale).astype(BF16)
            ds_ctx = (p_ctx * (dp_ctx - dr) * scale).astype(BF16)
            dq = _dot(ds_lat, kw, NN) + _dot(ds_ctx, kctx, NN)
            snk_terms = p_snk * dr
            dsn = jnp.zeros((8, HEAD_DIM), F32)
            for g in range(Q_PER_KV):
                c0 = (hh * Q_PER_KV + g) * HEAD_DIM
                dq_ref[:, c0:c0 + HEAD_DIM] = dq[g * WINDOW:(g + 1) * WINDOW]
                dsn = dsn + jnp.where(lane == g, -jnp.sum(snk_terms[g * WINDOW:(g + 1) * WINDOW], axis=0, keepdims=True),
                                      0.0)
            dkw_ref[rows, sl] += _dot(ds_lat, q, TN)
            dvw_ref[rows, sl] += _dot(p_lat.astype(BF16), d_o, TN)
            dkc_ref[:, sl] += _dot(ds_ctx, q, TN)
            dvc_ref[:, sl] += _dot(p_ctx.astype(BF16), d_o, TN)
            dsn_ref[hh] += dsn

    wacc = pl.BlockSpec((TP, KV_PER_STEP * HEAD_DIM), lambda h, i: (0, h))
    return _pcall(
        body, name=name, grid=(N_KV_HEADS // KV_PER_STEP, nb),
        in_specs=[qspec] + win + win + [ctx, ctx, sink, qspec],
        out_specs=[qspec, wacc, wacc, ctx, ctx, pl.BlockSpec((KV_PER_STEP, 8, HEAD_DIM), lambda h, i: (h, 0, 0))],
        out_shape=[jax.ShapeDtypeStruct(qn.shape, F32),
                   jax.ShapeDtypeStruct((TP, N_KV_HEADS * HEAD_DIM), F32),
                   jax.ShapeDtypeStruct((TP, N_KV_HEADS * HEAD_DIM), F32),
                   jax.ShapeDtypeStruct((n_ctx, N_KV_HEADS * HEAD_DIM), F32),
                   jax.ShapeDtypeStruct((n_ctx, N_KV_HEADS * HEAD_DIM), F32),
                   jax.ShapeDtypeStruct((N_KV_HEADS, 8, HEAD_DIM), F32)],
        sem=("arbitrary", "arbitrary"), args=(qn, kn, kn, kn, vb, vb, vb, kn, vb, sink_rows, do), ride=ride)


def _gla_masks(dirv):
    C = GLA_CHUNK

    def times(reps):
        r = lax.broadcasted_iota(jnp.int32, (C, reps * C), 0)
        c = lax.broadcasted_iota(jnp.int32, (C, reps * C), 1) & (C - 1)
        return jnp.where(dirv == 0, r, C - 1 - r), jnp.where(dirv == 0, c, C - 1 - c)

    def level(tt, ss, m):
        sh = m.bit_length() - 1
        same = (tt >> (sh + 1)) == (ss >> (sh + 1))
        return same, (tt >> sh) & 1, (ss >> sh) & 1

    tt, ss = times(3)
    le = (ss <= tt).astype(jnp.int32)
    sums = [le == 1]
    for m in GLA_LEVELS:
        same, ut, us = level(tt, ss, m)
        sums.append(same & (ut == us) & (ut == le))
    tt, ss = times(1)
    blocks = [ss == tt]
    for m in GLA_LEVELS:
        same, ut, us = level(tt, ss, m)
        blocks.append(same & (ut == 1) & (us == 0))
    mall3 = jnp.concatenate([jnp.where(s, 1.0, 0.0) for s in sums], axis=0).astype(BF16)
    return mall3, blocks


def _pieces(x):
    hi = x.astype(BF16)
    r1 = x - hi.astype(F32)
    mid = r1.astype(BF16)
    return hi, mid, (r1 - mid.astype(F32)).astype(BF16)


def _sum_f32(mall3, x):
    return _dot(mall3, jnp.concatenate(_pieces(x), axis=0), NN)


def _sum_f32_t(mall3, x):
    m = mall3[:, 0:GLA_CHUNK]
    hi, mid, lo = _pieces(x)
    return _dot(m, hi, TN) + _dot(m, mid, TN) + _dot(m, lo, TN)


def _gla_chunk_of(dirv, j, lc, nc):
    return jnp.where(dirv == 0, j, jnp.where(j < lc, lc - 1 - j, nc + lc - 1 - j))


def _gla_gate(lr_ref, wg_ref, bg_ref):
    pre = _dot(lr_ref[...].astype(BF16), wg_ref[0].astype(BF16), NN) + bg_ref[0]
    g = (jnp.minimum(pre, 0.0) - jnp.log(1.0 + jnp.exp(-jnp.abs(pre)))) * (1.0 / GLA_GATE_NORM)
    return pre, g


def _gla_fwd(z, qblk, kblk, vblk, lrblk, wg, bg, DV, n_ctx, name):
    R = z.shape[0]
    C = GLA_CHUNK
    DK = wg.shape[2] // GLA_HEADS
    nc, lc = R // C, n_ctx // C
    qscale = DK ** -0.5

    GK, GV = GLA_HEADS * DK, GLA_HEADS * DV

    def body(q_ref, k_ref, v_ref, lr_ref, wg_ref, bg_ref, o_ref, sp_ref, st_ref):
        dirv, j = pl.program_id(0), pl.program_id(1)

        @pl.when(j == 0)
        def _():
            st_ref[...] = jnp.zeros_like(st_ref)

        mall, blocks = _gla_masks(dirv)
        _, g_all = _gla_gate(lr_ref, wg_ref, bg_ref)
        E_all = _sum_f32(mall, g_all)
        for h in range(GLA_HEADS):
            ks, vs = slice(h * DK, (h + 1) * DK), slice(h * DV, (h + 1) * DV)
            q, k, v = q_ref[:, ks] * qscale, k_ref[:, ks], v_ref[:, vs].astype(BF16)
            g, E = g_all[:, ks], E_all[:, ks]
            st = st_ref[h]
            sp_ref[0, h, 0] = st
            A = jnp.where(blocks[0], _dot(q.astype(BF16), k.astype(BF16), NT), 0.0)
            for l in range(len(GLA_LEVELS)):
                e = jnp.exp(E[(1 + l) * C:(2 + l) * C])
                A = A + jnp.where(blocks[l + 1], _dot((q * e).astype(BF16), (k * e).astype(BF16), NT), 0.0)
            o_ref[0, :, vs] = (_dot((q * jnp.exp(E[0:C])).astype(BF16), st.astype(BF16), NT)
                               + _dot(A.astype(BF16), v, NN))
            last = jnp.sum(g, axis=0, keepdims=True)
            st_ref[h] = jnp.exp(last) * st + _dot(v, (k * jnp.exp(last - E[0:C])).astype(BF16), TN)

    chunk = functools.partial(_gla_chunk_of, lc=lc, nc=nc)
    return pl.pallas_call(
        body, name=name, grid=(2, nc),
        in_specs=[pl.BlockSpec((C, GK), lambda d, j: (chunk(d, j), qblk)),
                  pl.BlockSpec((C, GK), lambda d, j: (chunk(d, j), kblk)),
                  pl.BlockSpec((C, GV), lambda d, j: (chunk(d, j), vblk)),
                  pl.BlockSpec((C, LANES), lambda d, j: (chunk(d, j), lrblk)),
                  pl.BlockSpec((1, LANES, GK), lambda d, j: (d, 0, 0)),
                  pl.BlockSpec((1, 1, GK), lambda d, j: (d, 0, 0))],
        out_specs=[pl.BlockSpec((1, C, GV), lambda d, j: (d, chunk(d, j), 0)),
                   pl.BlockSpec((1, GLA_HEADS, 1, DV, DK), lambda d, j: (d, 0, j, 0, 0))],
        out_shape=[jax.ShapeDtypeStruct((2, R, GV), F32),
                   jax.ShapeDtypeStruct((2, GLA_HEADS, nc, DV, DK), F32)],
        scratch_shapes=[pltpu.VMEM((GLA_HEADS, DV, DK), F32)],
        compiler_params=_cparams("parallel", "arbitrary"),
    )(z, z, z, z, wg, bg)


def _gla_bwd(z, qblk, kblk, vblk, lrblk, wg, bg, sprev, do, n_ctx, name, ride=None):
    R = z.shape[0]
    C = GLA_CHUNK
    DK, DV = wg.shape[2] // GLA_HEADS, do.shape[1] // GLA_HEADS
    nc, lc = R // C, n_ctx // C
    qscale = DK ** -0.5
    nl = len(GLA_LEVELS)

    GK, GV = GLA_HEADS * DK, GLA_HEADS * DV

    def body(q_ref, k_ref, v_ref, lr_ref, wg_ref, bg_ref, sp_ref, do_ref,
             dq_ref, dk_ref, dv_ref, dpre_ref, dbg_ref, dst_ref):
        dirv, jr = pl.program_id(0), pl.program_id(1)

        @pl.when(jr == 0)
        def _():
            dst_ref[...] = jnp.zeros_like(dst_ref)
            dbg_ref[...] = jnp.zeros_like(dbg_ref)

        mall, blocks = _gla_masks(dirv)
        pre_all, g_all = _gla_gate(lr_ref, wg_ref, bg_ref)
        E_all = _sum_f32(mall, g_all)
        for h in range(GLA_HEADS):
            ks, vs = slice(h * DK, (h + 1) * DK), slice(h * DV, (h + 1) * DV)
            q, k, v = q_ref[:, ks] * qscale, k_ref[:, ks], v_ref[:, vs].astype(BF16)
            pre, g, E = pre_all[:, ks], g_all[:, ks], E_all[:, ks]
            last = jnp.sum(g, axis=0, keepdims=True)
            eb, er, decay = jnp.exp(E[0:C]), jnp.exp(last - E[0:C]), jnp.exp(last)
            st = sp_ref[0, h, 0]
            dst = dst_ref[h]
            d_o = do_ref[:, vs]
            qe, kd = q * eb, k * er
            qb, kb = q.astype(BF16), k.astype(BF16)
            A = jnp.where(blocks[0], _dot(qb, kb, NT), 0.0)
            levels = []
            for l in range(nl):
                e = jnp.exp(E[(1 + l) * C:(2 + l) * C])
                ql, kl = q * e, k * e
                levels.append((e, ql, kl, ql.astype(BF16), kl.astype(BF16)))
                A = A + jnp.where(blocks[l + 1], _dot(levels[l][3], levels[l][4], NT), 0.0)
            dA = _dot(d_o, v, NT)
            dv_ref[0, :, vs] = (_dot(A.astype(BF16), d_o, TN) + _dot(kd.astype(BF16), dst.astype(BF16), NT)).astype(BF16)
            dqe = _dot(d_o, st.astype(BF16), NN)
            dkd = _dot(v, dst.astype(BF16), NN)
            G = jnp.where(blocks[0], dA, 0.0).astype(BF16)
            dq = dqe * eb + _dot(G, kb, NN)
            dk = dkd * er + _dot(G, qb, TN)
            dEr = dkd * kd
            dE = [dqe * qe - dEr]
            for l in range(nl):
                e, ql, kl, qlb, klb = levels[l]
                G = jnp.where(blocks[l + 1], dA, 0.0).astype(BF16)
                dql = _dot(G, klb, NN)
                dkl = _dot(G, qlb, TN)
                dq = dq + dql * e
                dk = dk + dkl * e
                dE.append(dql * ql + dkl * kl)
            dlast = jnp.sum(dst * st, axis=0, keepdims=True) * decay + jnp.sum(dEr, axis=0, keepdims=True)
            dg = _sum_f32_t(mall, jnp.concatenate(dE, axis=0)) + dlast
            dpre = dg * (1.0 / GLA_GATE_NORM) / (1.0 + jnp.exp(pre))
            dq_ref[0, :, ks] = (dq * qscale).astype(BF16)
            dk_ref[0, :, ks] = dk.astype(BF16)
            dpre_ref[:, ks] = dpre.astype(BF16)
            dbg_ref[0, :, ks] += jnp.sum(dpre, axis=0, keepdims=True)
            dst_ref[h] = decay * dst + _dot(d_o, qe.astype(BF16), TN)

    def chunk(d, jr):
        return _gla_chunk_of(d, nc - 1 - jr, lc, nc)

    return _pcall(
        body, name=name, grid=(2, nc),
        in_specs=[pl.BlockSpec((C, GK), lambda d, j: (chunk(d, j), qblk)),
                  pl.BlockSpec((C, GK), lambda d, j: (chunk(d, j), kblk)),
                  pl.BlockSpec((C, GV), lambda d, j: (chunk(d, j), vblk)),
                  pl.BlockSpec((C, LANES), lambda d, j: (chunk(d, j), lrblk)),
                  pl.BlockSpec((1, LANES, GK), lambda d, j: (d, 0, 0)),
                  pl.BlockSpec((1, 1, GK), lambda d, j: (d, 0, 0)),
                  pl.BlockSpec((1, GLA_HEADS, 1, DV, DK), lambda d, j: (d, 0, nc - 1 - j, 0, 0)),
                  pl.BlockSpec((C, GV), lambda d, j: (chunk(d, j), 0))],
        out_specs=[pl.BlockSpec((1, C, GK), lambda d, j: (d, chunk(d, j), 0)),
                   pl.BlockSpec((1, C, GK), lambda d, j: (d, chunk(d, j), 0)),
                   pl.BlockSpec((1, C, GV), lambda d, j: (d, chunk(d, j), 0)),
                   pl.BlockSpec((C, GK), lambda d, j: (chunk(d, j), d)),
                   pl.BlockSpec((1, 1, GK), lambda d, j: (d, 0, 0))],
        out_shape=[jax.ShapeDtypeStruct((2, R, GK), BF16),
                   jax.ShapeDtypeStruct((2, R, GK), BF16),
                   jax.ShapeDtypeStruct((2, R, GV), BF16),
                   jax.ShapeDtypeStruct((R, 2 * GK), BF16),
                   jax.ShapeDtypeStruct((2, 1, GK), F32)],
        scratch_shapes=[pltpu.VMEM((GLA_HEADS, DV, DK), F32)],
        sem=("arbitrary", "arbitrary"), args=(z, z, z, z, wg, bg, sprev, do), ride=ride)


def _glanorm_fwd(o, z, rbblk, gn, n_ctx, name):
    _, R, GV = o.shape
    T = R - n_ctx
    DV = GV // GLA_HEADS
    tm = _pick(n_ctx, 256, 8)
    ro = n_ctx // tm

    def body(o0_ref, o1_ref, rb_ref, gn_ref, p_ref):
        gnv = gn_ref[...]
        for h in range(GLA_HEADS):
            sl = slice(h * DV, (h + 1) * DV)
            og = o0_ref[0, :, sl] + o1_ref[0, :, sl]
            r = lax.rsqrt(jnp.mean(og * og, axis=-1, keepdims=True) + EPS)
            p_ref[:, sl] = (og * r * gnv * _silu(rb_ref[:, sl])).astype(BF16)

    return pl.pallas_call(
        body, name=name, grid=(T // tm,),
        in_specs=[pl.BlockSpec((1, tm, GV), lambda i: (0, i + ro, 0)), pl.BlockSpec((1, tm, GV), lambda i: (1, i + ro, 0)),
                  pl.BlockSpec((tm, GV), lambda i: (i + ro, rbblk)), pl.BlockSpec((1, DV), lambda i: (0, 0))],
        out_specs=pl.BlockSpec((tm, GV), lambda i: (i, 0)),
        out_shape=jax.ShapeDtypeStruct((T, GV), BF16), compiler_params=_cparams("parallel"))(o, o, z, gn)


def _glanorm_bwd(o, z, rbblk, gn, dp, n_ctx, name):
    _, R, GV = o.shape
    T = R - n_ctx
    DV = GV // GLA_HEADS
    tm = _pick(n_ctx, 256, 8)
    ro = n_ctx // tm

    def body(o0_ref, o1_ref, rb_ref, gn_ref, dp_ref, do_ref, drb_ref, acc_ref):
        i = pl.program_id(0)

        @pl.when(i == 0)
        def _():
            acc_ref[...] = jnp.zeros_like(acc_ref)

        gnv = gn_ref[...]
        dgn = jnp.zeros((1, DV), F32)
        for h in range(GLA_HEADS):
            sl = slice(h * DV, (h + 1) * DV)
            og = o0_ref[0, :, sl] + o1_ref[0, :, sl]
            rb = rb_ref[:, sl]
            d = dp_ref[:, sl]
            r = lax.rsqrt(jnp.mean(og * og, axis=-1, keepdims=True) + EPS)
            xh = og * r
            drb_ref[:, sl] = (d * xh * gnv * _dsilu(rb)).astype(BF16)
            dn = d * _silu(rb)
            dgn = dgn + jnp.sum(dn * xh, axis=0, keepdims=True)
            dxh = dn * gnv
            do_ref[:, sl] = (r * (dxh - xh * jnp.mean(dxh * xh, axis=-1, keepdims=True))).astype(BF16)
        acc_ref[0:1, :] += dgn

    row = pl.BlockSpec((tm, GV), lambda i: (i, 0))
    return pl.pallas_call(
        body, name=name, grid=(T // tm,),
        in_specs=[pl.BlockSpec((1, tm, GV), lambda i: (0, i + ro, 0)), pl.BlockSpec((1, tm, GV), lambda i: (1, i + ro, 0)),
                  pl.BlockSpec((tm, GV), lambda i: (i + ro, rbblk)), pl.BlockSpec((1, DV), lambda i: (0, 0)), row],
        out_specs=[row, row, pl.BlockSpec((8, DV), lambda i: (0, 0))],
        out_shape=[jax.ShapeDtypeStruct((T, GV), BF16), jax.ShapeDtypeStruct((T, GV), BF16),
                   jax.ShapeDtypeStruct((8, DV), F32)],
        compiler_params=_cparams("arbitrary"))(o, o, z, gn, dp)


def _gate_fwd(z, gablk, gbblk, ya, yg, n_ctx, name):
    T, D = ya.shape
    tm = _pick(n_ctx, 256, 8)
    ro = n_ctx // tm

    def body(ga_ref, gb_ref, ya_ref, yg_ref, m_ref):
        m_ref[...] = (_sigmoid(ga_ref[...]) * ya_ref[...] + _sigmoid(gb_ref[...]) * yg_ref[...]).astype(BF16)

    row = pl.BlockSpec((tm, D), lambda i: (i, 0))
    return pl.pallas_call(
        body, name=name, grid=(T // tm,),
        in_specs=[pl.BlockSpec((tm, D), lambda i: (i + ro, gablk)), pl.BlockSpec((tm, D), lambda i: (i + ro, gbblk)), row, row],
        out_specs=row, out_shape=jax.ShapeDtypeStruct((T, D), BF16), compiler_params=_cparams("parallel"))(z, z, ya, yg)


def _gate_bwd(z, gablk, gbblk, ya, yg, dm, n_ctx, name):
    T, D = ya.shape
    tm = _pick(n_ctx, 256, 8)
    ro = n_ctx // tm

    def body(ga_ref, gb_ref, ya_ref, yg_ref, dm_ref, dya_ref, dyg_ref, dga_ref, dgb_ref):
        d = dm_ref[...]
        sa, sb = _sigmoid(ga_ref[...]), _sigmoid(gb_ref[...])
        dya_ref[...] = (d * sa).astype(BF16)
        dyg_ref[...] = (d * sb).astype(BF16)
        dga_ref[...] = (d * ya_ref[...] * sa * (1.0 - sa)).astype(BF16)
        dgb_ref[...] = (d * yg_ref[...] * sb * (1.0 - sb)).astype(BF16)

    row = pl.BlockSpec((tm, D), lambda i: (i, 0))
    sh = jax.ShapeDtypeStruct((T, D), BF16)
    return pl.pallas_call(
        body, name=name, grid=(T // tm,),
        in_specs=[pl.BlockSpec((tm, D), lambda i: (i + ro, gablk)), pl.BlockSpec((tm, D), lambda i: (i + ro, gbblk)), row, row, row],
        out_specs=[row] * 4, out_shape=[sh] * 4, compiler_params=_cparams("parallel"))(z, z, ya, yg, dm)


def _resnorm_fwd(x, mix, gt, g, sc, sh, name):
    T, D = x.shape
    tm = _pick(T, 256, 8)

    def body(x_ref, mix_ref, gt_ref, g_ref, sc_ref, sh_ref, x1_ref, h_ref):
        x1 = x_ref[...] + gt_ref[...] * mix_ref[...]
        x1_ref[...] = x1
        r = lax.rsqrt(jnp.mean(x1 * x1, axis=-1, keepdims=True) + EPS)
        h_ref[...] = (x1 * r * g_ref[...] * (1.0 + sc_ref[...]) + sh_ref[...]).astype(BF16)

    row = pl.BlockSpec((tm, D), lambda i: (i, 0))
    vec = pl.BlockSpec((1, D), lambda i: (0, 0))
    return pl.pallas_call(
        body, name=name, grid=(T // tm,), in_specs=[row, row, vec, vec, vec, vec], out_specs=[row, row],
        out_shape=[jax.ShapeDtypeStruct((T, D), F32), jax.ShapeDtypeStruct((T, D), BF16)],
        compiler_params=_cparams("parallel"))(x, mix, gt, g, sc, sh)


def _loss_head(d, x1, gt, target, name):
    T, D = d.shape
    tm = _pick(T, 256, 8)

    def body(d_ref, x1_ref, gt_ref, t_ref, dy_ref, dd_ref, acc_ref):
        i = pl.program_id(0)

        @pl.when(i == 0)
        def _():
            acc_ref[...] = jnp.zeros_like(acc_ref)

        dv, gtv = d_ref[...], gt_ref[...]
        e = x1_ref[...] + gtv * dv - t_ref[...]
        dy = e * (1.0 / D)
        dy_ref[...] = dy
        dd_ref[...] = (dy * gtv).astype(BF16)
        acc_ref[0:1, :] += jnp.sum(e * e, axis=0, keepdims=True)
        acc_ref[1:2, :] += jnp.sum(dy * dv, axis=0, keepdims=True)

    row = pl.BlockSpec((tm, D), lambda i: (i, 0))
    return pl.pallas_call(
        body, name=name, grid=(T // tm,), in_specs=[row, row, pl.BlockSpec((1, D), lambda i: (0, 0)), row],
        out_specs=[row, row, pl.BlockSpec((8, D), lambda i: (0, 0))],
        out_shape=[jax.ShapeDtypeStruct((T, D), F32), jax.ShapeDtypeStruct((T, D), BF16),
                   jax.ShapeDtypeStruct((8, D), F32)],
        compiler_params=_cparams("arbitrary"))(d, x1, gt, target)


def _halo_specs(T, tm, tw, col_of, order):
    n8 = tm // 8
    if order == "ij":
        mid = lambda i, j: (i, col_of(j))
        prev = lambda i, j: (jnp.maximum(i * n8 - 1, 0), col_of(j))
        nxt = lambda i, j: (jnp.minimum((i + 1) * n8, T // 8 - 1), col_of(j))
    else:
        mid = lambda j, i: (i, col_of(j))
        prev = lambda j, i: (jnp.maximum(i * n8 - 1, 0), col_of(j))
        nxt = lambda j, i: (jnp.minimum((i + 1) * n8, T // 8 - 1), col_of(j))
    return [pl.BlockSpec((tm, tw), mid), pl.BlockSpec((8, tw), prev), pl.BlockSpec((8, tw), nxt)]


def _shift_rows(x, before, after):
    tm = x.shape[0]
    row = lax.broadcasted_iota(jnp.int32, x.shape, 0)
    return (jnp.where(row == 0, before, pltpu.roll(x, 1, 0)),
            jnp.where(row == tm - 1, after, pltpu.roll(x, tm - 1, 0)))


def _conv_fwd(u, cw, cb, name):
    T, F2 = u.shape
    F = F2 // 2
    tm, tw = _pick(T, 256, 8), _pick(F, 512)
    nt, nw = T // tm, F // tw

    def body(ua, uap, uan, ug, ugp, ugn, cwa, cwg, cba, cbg, f_ref):
        i = pl.program_id(0)
        first, last = i == 0, i == nt - 1

        def conv(u_ref, up_ref, un_ref, w_ref, b_ref):
            m = u_ref[...]
            p, n = _shift_rows(m, jnp.where(first, 0.0, up_ref[7:8, :]), jnp.where(last, 0.0, un_ref[0:1, :]))
            return p * w_ref[0:1, :] + m * w_ref[1:2, :] + n * w_ref[2:3, :] + b_ref[...]

        a = conv(ua, uap, uan, cwa, cba)
        g = conv(ug, ugp, ugn, cwg, cbg)
        f_ref[...] = (_silu(a) * g).astype(BF16)

    wspec = lambda off: pl.BlockSpec((3, tw), lambda i, j: (0, j + off))
    bspec = lambda off: pl.BlockSpec((1, tw), lambda i, j: (0, j + off))
    return pl.pallas_call(
        body, name=name, grid=(nt, nw),
        in_specs=_halo_specs(T, tm, tw, lambda j: j, "ij") + _halo_specs(T, tm, tw, lambda j: j + nw, "ij")
        + [wspec(0), wspec(nw), bspec(0), bspec(nw)],
        out_specs=pl.BlockSpec((tm, tw), lambda i, j: (i, j)),
        out_shape=jax.ShapeDtypeStruct((T, F), BF16),
        compiler_params=_cparams("parallel", "parallel"),
    )(u, u, u, u, u, u, cw, cw, cb, cb)


def _conv_bwd(u, df, cw, cb, name):
    T, F2 = u.shape
    F = F2 // 2
    tm, tw = _pick(T, 256, 8), _pick(F, 512)
    nt, nw = T // tm, F // tw

    def body(ua, uap, uan, ug, ugp, ugn, cwa, cwg, cba, cbg, df_ref, dfp, dfn, du_ref, acca_ref, accg_ref):
        i = pl.program_id(1)

        @pl.when(i == 0)
        def _():
            acca_ref[...] = jnp.zeros_like(acca_ref)
            accg_ref[...] = jnp.zeros_like(accg_ref)

        first, last = i == 0, i == nt - 1
        wa, wg, ba, bg = cwa[...], cwg[...], cba[...], cbg[...]

        def conv(p, m, n, w, b):
            return p * w[0:1] + m * w[1:2] + n * w[2:3] + b

        def grads(a, g, d):
            return d * g * _dsilu(a), d * _silu(a)

        xa, xg, d = ua[...], ug[...], df_ref[...]
        sa = _shift_rows(xa, jnp.where(first, 0.0, uap[7:8, :]), jnp.where(last, 0.0, uan[0:1, :]))
        sg = _shift_rows(xg, jnp.where(first, 0.0, ugp[7:8, :]), jnp.where(last, 0.0, ugn[0:1, :]))
        da, dg = grads(conv(sa[0], xa, sa[1], wa, ba), conv(sg[0], xg, sg[1], wg, bg), d)
        da_p, dg_p = grads(conv(uap[6:7, :], uap[7:8, :], xa[0:1], wa, ba),
                           conv(ugp[6:7, :], ugp[7:8, :], xg[0:1], wg, bg), dfp[7:8, :])
        da_n, dg_n = grads(conv(xa[tm - 1:tm], uan[0:1, :], uan[1:2, :], wa, ba),
                           conv(xg[tm - 1:tm], ugn[0:1, :], ugn[1:2, :], wg, bg), dfn[0:1, :])
        ta = _shift_rows(da, jnp.where(first, 0.0, da_p), jnp.where(last, 0.0, da_n))
        tg = _shift_rows(dg, jnp.where(first, 0.0, dg_p), jnp.where(last, 0.0, dg_n))
        du_ref[0] = (ta[1] * wa[0:1] + da * wa[1:2] + ta[0] * wa[2:3]).astype(BF16)
        du_ref[1] = (tg[1] * wg[0:1] + dg * wg[1:2] + tg[0] * wg[2:3]).astype(BF16)
        for t, (va, vg) in enumerate(((sa[0], sg[0]), (xa, xg), (sa[1], sg[1]))):
            acca_ref[t:t + 1, :] += jnp.sum(da * va, axis=0, keepdims=True)
            accg_ref[t:t + 1, :] += jnp.sum(dg * vg, axis=0, keepdims=True)
        acca_ref[3:4, :] += jnp.sum(da, axis=0, keepdims=True)
        accg_ref[3:4, :] += jnp.sum(dg, axis=0, keepdims=True)

    wspec = lambda off: pl.BlockSpec((3, tw), lambda j, i: (0, j + off))
    bspec = lambda off: pl.BlockSpec((1, tw), lambda j, i: (0, j + off))
    row = pl.BlockSpec((tm, tw), lambda j, i: (i, j))
    acc = pl.BlockSpec((8, tw), lambda j, i: (0, j))
    return pl.pallas_call(
        body, name=name, grid=(nw, nt),
        in_specs=_halo_specs(T, tm, tw, lambda j: j, "ji") + _halo_specs(T, tm, tw, lambda j: j + nw, "ji")
        + [wspec(0), wspec(nw), bspec(0), bspec(nw)] + _halo_specs(T, tm, tw, lambda j: j, "ji"),
        out_specs=[pl.BlockSpec((2, tm, tw), lambda j, i: (0, i, j)), acc, acc],
        out_shape=[jax.ShapeDtypeStruct((2, T, F), BF16),
                   jax.ShapeDtypeStruct((8, F), F32), jax.ShapeDtypeStruct((8, F), F32)],
        compiler_params=_cparams("parallel", "arbitrary"),
    )(u, u, u, u, u, u, cw, cw, cb, cb, df, df, df)


def _assemble_dz(lay, z_used, Z, n_ctx, dqa, drb, dga, dgb, dka, dva, dvg, dqg, dkg, dlr, name):
    T = dqa.shape[0]
    R = T + n_ctx
    tm = _pick(n_ctx, 128, 8)
    cb = n_ctx // tm

    def body(dqa_ref, drb_ref, dga_ref, dgb_ref, dka_ref, dva_ref, dvg0, dvg1, dqg0, dqg1, dkg0, dkg1, dlr_ref, o_ref):
        lat = pl.program_id(0) >= cb

        def put(seg, val):
            o_ref[:, lay[seg]:lay[seg] + val.shape[1]] = val.astype(BF16)

        def lat_only(ref):
            v = ref[...]
            return jnp.where(lat, v, jnp.zeros_like(v))

        put("qa", lat_only(dqa_ref))
        put("rb", lat_only(drb_ref))
        put("ga", lat_only(dga_ref))
        put("gb", lat_only(dgb_ref))
        put("ka", dka_ref[...])
        put("va", dva_ref[...])
        put("vb", dvg0[0].astype(F32) + dvg1[0].astype(F32))
        put("qb", dqg0[0].astype(F32) + dqg1[0].astype(F32))
        put("kb", dkg0[0].astype(F32) + dkg1[0].astype(F32))
        put("lr", dlr_ref[...])
        if Z > z_used:
            o_ref[:, z_used:] = jnp.zeros((tm, Z - z_used), BF16)

    lat_spec = lambda a: pl.BlockSpec((tm, a.shape[1]), lambda i: (jnp.maximum(i - cb, 0), 0))
    all_spec = lambda a: pl.BlockSpec((tm, a.shape[1]), lambda i: (i, 0))
    dir_specs = lambda a: [pl.BlockSpec((1, tm, a.shape[2]), lambda i: (0, i, 0)),
                           pl.BlockSpec((1, tm, a.shape[2]), lambda i: (1, i, 0))]
    return pl.pallas_call(
        body, name=name, grid=(R // tm,),
        in_specs=[lat_spec(dqa), lat_spec(drb), lat_spec(dga), lat_spec(dgb), all_spec(dka), all_spec(dva)]
        + dir_specs(dvg) + dir_specs(dqg) + dir_specs(dkg) + [all_spec(dlr)],
        out_specs=pl.BlockSpec((tm, Z), lambda i: (i, 0)),
        out_shape=jax.ShapeDtypeStruct((R, Z), BF16), compiler_params=_cparams("parallel"),
    )(dqa, drb, dga, dgb, dka, dva, dvg, dvg, dqg, dqg, dkg, dkg, dlr)


def _mod_fwd(ca, w, b, name):
    n, D = ca.shape
    N = w.shape[1]
    tn = _pick(N, 512)

    def body(c_ref, w_ref, b_ref, o_ref, s_ref):
        s = _silu(c_ref[...])
        s_ref[...] = s
        o_ref[...] = _dot(s.astype(BF16), w_ref[...].astype(BF16), NN) + b_ref[...]

    return pl.pallas_call(
        body, name=name, grid=(N // tn,),
        in_specs=[pl.BlockSpec((n, D), lambda j: (0, 0)), pl.BlockSpec((D, tn), lambda j: (0, j)),
                  pl.BlockSpec((1, tn), lambda j: (0, j))],
        out_specs=[pl.BlockSpec((n, tn), lambda j: (0, j)), pl.BlockSpec((n, D), lambda j: (0, 0))],
        out_shape=[jax.ShapeDtypeStruct((n, N), F32), jax.ShapeDtypeStruct((n, D), F32)],
        compiler_params=_cparams("arbitrary"))(ca, w, b)


def _silu_bwd(dsil, ca, name):
    def body(d_ref, c_ref, o_ref):
        o_ref[...] = d_ref[...] * _dsilu(c_ref[...])

    return pl.pallas_call(body, name=name, out_shape=jax.ShapeDtypeStruct(ca.shape, F32))(dsil, ca)


def _adam_math(w, g, m, v):
    c1 = 1.0 - ADAM_B1 ** ADAM_STEP
    c2 = 1.0 - ADAM_B2 ** ADAM_STEP
    mn = ADAM_B1 * m + (1.0 - ADAM_B1) * g
    vn = ADAM_B2 * v + (1.0 - ADAM_B2) * (g * g)
    return -ADAM_LR * ((mn / c1) / (jnp.sqrt(vn / c2) + ADAM_EPS) + ADAM_WD * w), mn, vn


def _adamw(w, g, m, v, name, ride=None):
    Rw, Cw = w.shape
    tr = _pick(Rw, 128, 8)

    def body(w_ref, g_ref, m_ref, v_ref, d_ref, mo_ref, vo_ref):
        d_ref[...], mo_ref[...], vo_ref[...] = _adam_math(w_ref[...], g_ref[...], m_ref[...], v_ref[...])

    row = pl.BlockSpec((tr, Cw), lambda i: (i, 0))
    sh = jax.ShapeDtypeStruct((Rw, Cw), F32)
    return _pcall(body, name=name, grid=(Rw // tr,), in_specs=[row] * 4, out_specs=[row] * 3, out_shape=[sh] * 3,
                  sem=("parallel",), args=(w, g, m, v), ride=ride)


HBM_SPEC = pl.BlockSpec(memory_space=pltpu.HBM)


def _exchange(inputs, out_shapes, stages, name):
    n_in, n_out = len(inputs), len(out_shapes)
    n = sum(len(s) for s in stages)

    def body(*refs):
        ins, outs = refs[:n_in], refs[n_in:n_in + n_out]
        send_sems, recv_sems = refs[n_in + n_out:]
        k = 0
        for stage in stages:
            copies = _stage_copies(stage, ins, outs, send_sems, recv_sems, k)
            for cp in copies:
                cp.start()
            for cp in copies:
                cp.wait()
            k += len(stage)

    return pl.pallas_call(
        body, name=name, in_specs=[HBM_SPEC] * n_in, out_specs=[HBM_SPEC] * n_out, out_shape=out_shapes,
        scratch_shapes=[pltpu.SemaphoreType.DMA((n,)), pltpu.SemaphoreType.DMA((n,))],
    )(*inputs)


def _stage_copies(stage, ins, outs, send_sems, recv_sems, k0=0):
    me = (lax.axis_index("x"), lax.axis_index("y"), lax.axis_index("c"))
    copies = []
    for k, ((skind, sidx), sfn, didx, dfn, flip) in enumerate(stage):
        src = (ins if skind == "in" else outs)[sidx].at[sfn(*me)]
        dst = outs[didx].at[dfn(*me)]
        if flip == (0, 0, 0):
            copies.append(pltpu.make_async_copy(src, dst, send_sems.at[k0 + k]))
        else:
            peer = tuple(1 - a if f else a for a, f in zip(me, flip))
            copies.append(pltpu.make_async_remote_copy(src, dst, send_sems.at[k0 + k], recv_sems.at[k0 + k],
                                                       device_id=peer, device_id_type=MESH))
    return copies


def _pcall(body, *, name, grid, in_specs, out_specs, out_shape, scratch_shapes=(), sem, args, ride=None):
    many = isinstance(out_shape, (list, tuple))
    out_specs, out_shape = (list(out_specs), list(out_shape)) if many else ([out_specs], [out_shape])
    if ride is None:
        res = pl.pallas_call(body, name=name, grid=grid, in_specs=list(in_specs), out_specs=out_specs,
                             out_shape=out_shape, scratch_shapes=list(scratch_shapes),
                             compiler_params=_cparams(*sem))(*args)
        return res if many else res[0]
    x_in, x_out, stage, aliases = ride
    n_in, n_out, n_scr, n_xin, n_xout = len(in_specs), len(out_specs), len(scratch_shapes), len(x_in), len(x_out)

    def wrapped(*refs):
        ins, xins = refs[:n_in], refs[n_in:n_in + n_xin]
        o0 = n_in + n_xin
        outs, xouts = refs[o0:o0 + n_out], refs[o0 + n_out:o0 + n_out + n_xout]
        s0 = o0 + n_out + n_xout
        scr, (send_sems, recv_sems) = refs[s0:s0 + n_scr], refs[s0 + n_scr:]
        first = functools.reduce(jnp.logical_and, [pl.program_id(d) == 0 for d in range(len(grid))])
        last = functools.reduce(jnp.logical_and, [pl.program_id(d) == grid[d] - 1 for d in range(len(grid))])

        @pl.when(first)
        def _():
            for cp in _stage_copies(stage, xins, xouts, send_sems, recv_sems):
                cp.start()

        body(*ins, *outs, *scr)

        @pl.when(last)
        def _():
            for cp in _stage_copies(stage, xins, xouts, send_sems, recv_sems):
                cp.wait()

    res = pl.pallas_call(
        wrapped, name=name, grid=grid, in_specs=list(in_specs) + [HBM_SPEC] * n_xin,
        out_specs=out_specs + [HBM_SPEC] * n_xout, out_shape=out_shape + list(x_out),
        scratch_shapes=list(scratch_shapes) + [pltpu.SemaphoreType.DMA((len(stage),)),
                                               pltpu.SemaphoreType.DMA((len(stage),))],
        input_output_aliases={n_in + a: n_out + b for a, b in aliases.items()},
        compiler_params=_cparams(*(["arbitrary"] * len(grid))))(*args, *x_in)
    main = res[:n_out]
    return (main if many else main[0]), list(res[n_out:])


FLIPS_ALL = [(0, 0, 1), (0, 1, 0), (0, 1, 1), (1, 0, 0), (1, 0, 1), (1, 1, 0), (1, 1, 1)]
FLIPS_CHIP = [(0, 1, 0), (1, 0, 0), (1, 1, 0)]


def _sum_slots(buf, name):
    n, r, w = buf.shape
    tr = _pick(r, 256, 8)

    def body(b_ref, o_ref):
        acc = b_ref[0]
        for s in range(1, n):
            acc = acc + b_ref[s]
        o_ref[...] = acc

    return pl.pallas_call(
        body, name=name, grid=(r // tr,), in_specs=[pl.BlockSpec((n, tr, w), lambda i: (0, i, 0))],
        out_specs=pl.BlockSpec((tr, w), lambda i: (i, 0)), out_shape=jax.ShapeDtypeStruct((r, w), F32),
        compiler_params=_cparams("parallel"))(buf)


def _allreduce_plan(buf):
    whole = lambda x, y, c: (slice(None), slice(None))
    slot = lambda x, y, c: (4 * x + 2 * y + c,)
    stage = [(("in", 0), whole, 0, slot, f) for f in [(0, 0, 0)] + FLIPS_ALL]
    return [jax.ShapeDtypeStruct((8,) + buf.shape, F32)], stage


def _allreduce(buf, name):
    shapes, stage = _allreduce_plan(buf)
    (slots,) = _exchange([buf], shapes, [stage], name + "_x")
    return _sum_slots(slots, name + "_sum")


def _gather_plan(shards, src):
    half = lambda a, c: pl.ds(c * (a.shape[0] // 2), a.shape[0] // 2)
    first, second = [], []
    for n, a in enumerate(shards):
        for f in FLIPS_CHIP:
            first.append((("in", n), lambda x, y, c, a=a: (half(a, c), slice(None)), n,
                          lambda x, y, c, a=a: (2 * x + y, half(a, c), slice(None)), f))
            peer_slot = lambda x, y, c, a=a, f=f: (2 * (x ^ f[0]) + (y ^ f[1]), half(a, c), slice(None))
            second.append(((src, n), peer_slot, n, peer_slot, (0, 0, 1)))
    outs = [jax.ShapeDtypeStruct((4,) + a.shape, a.dtype) for a in shards]
    return first, second, outs


def _allgather_weights(shards, name):
    first, second, outs = _gather_plan(shards, "out")
    return _exchange(shards, outs, [first, second], name)


def _place_own(buf, shard, svec, name):
    _, Rs, Cs = buf.shape
    tr = _pick(Rs, 256, 16)

    def body(s_ref, buf_ref, sh_ref, o_ref):
        o_ref[0] = sh_ref[...]

    grid_spec = pltpu.PrefetchScalarGridSpec(
        num_scalar_prefetch=1, grid=(Rs // tr,),
        in_specs=[pl.BlockSpec(memory_space=pl.ANY), pl.BlockSpec((tr, Cs), lambda i, s: (i, 0))],
        out_specs=pl.BlockSpec((1, tr, Cs), lambda i, s: (s[0], i, 0)))
    return pl.pallas_call(body, name=name, grid_spec=grid_spec, out_shape=jax.ShapeDtypeStruct(buf.shape, buf.dtype),
                          input_output_aliases={1: 0}, compiler_params=_cparams("arbitrary"))(svec, buf, shard)


def _add_pair(G, bufA, cvec, name):
    _, Rs, Cs = G.shape
    Rh = Rs // 2
    tr = _pick(Rh, 128, 16)
    nb = Rh // tr

    def body(c_ref, g_ref, a_ref, o_ref):
        o_ref[...] = (g_ref[...] + a_ref[...]).astype(BF16)

    grid_spec = pltpu.PrefetchScalarGridSpec(
        num_scalar_prefetch=1, grid=(4, nb),
        in_specs=[pl.BlockSpec((1, tr, Cs), lambda s, i, c_ref: (s, c_ref[0] * nb + i, 0)),
                  pl.BlockSpec((1, tr, Cs), lambda s, i, c_ref: (s, i, 0))],
        out_specs=pl.BlockSpec((1, tr, Cs), lambda s, i, c_ref: (s, i, 0)))
    return pl.pallas_call(body, name=name, grid_spec=grid_spec, out_shape=jax.ShapeDtypeStruct((4, Rh, Cs), BF16),
                          compiler_params=_cparams("parallel", "parallel"))(cvec, G, bufA)


def _sum_chips(G, bufA, bufB, cvec, svec, name):
    _, Rs, Cs = G.shape
    Rh = Rs // 2
    tr = _pick(Rh, 128, 16)
    nb = Rh // tr

    def body(c_ref, s_ref, g_ref, a_ref, b_ref, o_ref):
        o_ref[...] = (g_ref[0] + a_ref[0]) + b_ref[0].astype(F32) + b_ref[1].astype(F32) + b_ref[2].astype(F32)

    grid_spec = pltpu.PrefetchScalarGridSpec(
        num_scalar_prefetch=2, grid=(nb,),
        in_specs=[pl.BlockSpec((1, tr, Cs), lambda i, c, s: (s[0], c[0] * nb + i, 0)),
                  pl.BlockSpec((1, tr, Cs), lambda i, c, s: (s[0], i, 0)),
                  pl.BlockSpec((3, tr, Cs), lambda i, c, s: (0, i, 0))],
        out_specs=pl.BlockSpec((tr, Cs), lambda i, c, s: (i, 0)))
    return pl.pallas_call(body, name=name, grid_spec=grid_spec, out_shape=jax.ShapeDtypeStruct((Rh, Cs), F32),
                          compiler_params=_cparams("parallel"))(cvec, svec, G, bufA, bufB)


def _pair_plan(grads):
    Rh = [g.shape[1] // 2 for g in grads]
    whole3 = lambda x, y, c: (slice(None), slice(None), slice(None))
    stage = [(("in", n), lambda x, y, c, n=n: (slice(None), pl.ds((1 - c) * Rh[n], Rh[n]), slice(None)), n,
              whole3, (0, 0, 1)) for n in range(len(grads))]
    return [jax.ShapeDtypeStruct((4, Rh[n], g.shape[2]), F32) for n, g in enumerate(grads)], stage


def _chips_plan(P):
    stage = [(("in", n), lambda x, y, c, f=f: (2 * (x ^ f[0]) + (y ^ f[1]),), n, lambda x, y, c, k=k: (k,), f)
             for n in range(len(P)) for k, f in enumerate(FLIPS_CHIP)]
    return [jax.ShapeDtypeStruct((3,) + p.shape[1:], BF16) for p in P], stage


def _halves_plan(mine):
    whole2 = lambda x, y, c: (slice(None), slice(None))
    stage = [(("in", n), whole2, n, whole2, (0, 0, 1)) for n in range(len(mine))]
    return [jax.ShapeDtypeStruct(r.shape, F32) for r in mine], stage


def _adamw_halves(w, mine, other, m, v, cvec, name):
    Rs, Cs = w.shape
    Rh = Rs // 2
    tr = _pick(Rh, 128, 8)
    nb = Rh // tr

    def body(c_ref, w_ref, a_ref, b_ref, m_ref, v_ref, g_ref, d_ref, mo_ref, vo_ref):
        gv = jnp.where(pl.program_id(0) // nb == c_ref[0], a_ref[...], b_ref[...])
        g_ref[...] = gv
        d_ref[...], mo_ref[...], vo_ref[...] = _adam_math(w_ref[...], gv, m_ref[...], v_ref[...])

    row = pl.BlockSpec((tr, Cs), lambda i, c: (i, 0))
    hrow = pl.BlockSpec((tr, Cs), lambda i, c: (i % nb, 0))
    grid_spec = pltpu.PrefetchScalarGridSpec(num_scalar_prefetch=1, grid=(2 * nb,),
                                             in_specs=[row, hrow, hrow, row, row], out_specs=[row] * 4)
    return pl.pallas_call(body, name=name, grid_spec=grid_spec, out_shape=[jax.ShapeDtypeStruct((Rs, Cs), F32)] * 4,
                          compiler_params=_cparams("parallel"))(cvec, w, mine, other, m, v)


def _pack(arrays):
    flat = [a.reshape(-1).astype(F32) for a in arrays]
    meta, off = [], 0
    for a, f in zip(arrays, flat):
        meta.append((off, a.shape))
        off += f.shape[0]
    total = -(-off // (8 * LANES)) * (8 * LANES)
    flat.append(jnp.zeros((total - off,), F32))
    return jnp.concatenate(flat).reshape(total // LANES, LANES), meta


def _unpack(buf, meta):
    flat = buf.reshape(-1)
    out = []
    for off, shape in meta:
        size = 1
        for s in shape:
            size *= s
        out.append(flat[off:off + size].reshape(shape))
    return out


WEIGHT_NAMES = ["c_ctx", "w_mod", "b_mod", "g_mix", "w_in", "q_norm", "k_norm", "attn_sink", "w_gate_f", "b_gate_f",
                "w_gate_b", "b_gate_b", "gla_norm", "w_attn_o", "w_gla_o", "w_out", "g_ffn", "w_up", "conv_w",
                "conv_b", "w_down"]
BIG_NAMES = ["w_in", "w_attn_o", "w_gla_o", "w_out", "w_up", "w_down"]
SHARDED_SMALL = ["w_gate_f", "w_gate_b", "conv_w"]


def _layouts(D):
    aw, kvw, gk, gv = N_Q_HEADS * HEAD_DIM, N_KV_HEADS * HEAD_DIM, D // 2, D
    widths = {"qa": aw, "ka": kvw, "va": kvw, "qb": gk, "kb": gk, "vb": gv, "rb": gv, "lr": 2 * GLA_LOWRANK,
              "ga": D, "gb": D}
    orig, off = {}, 0
    for s in ["qa", "ka", "va", "qb", "kb", "vb", "rb", "lr", "ga", "gb"]:
        orig[s] = off
        off += widths[s]
    order = ["qa", "vb", "rb", "ga", "gb", "ka", "va", "qb", "kb", "lr"]
    lay, off = {}, 0
    for s in order:
        lay[s] = off
        off += LANES if s == "lr" else widths[s]
    align = {"qa": aw, "vb": D, "rb": D, "ga": D, "gb": D, "ka": kvw, "va": kvw, "qb": gk, "kb": gk,
             "lr": LANES}
    for s in order:
        assert lay[s] % align[s] == 0, (s, lay[s], align[s])
    return widths, orig, order, lay, off, -(-off // (2 * MXU_TILE)) * (2 * MXU_TILE)


def _rope_tables(T, L):
    t = jnp.arange(T)
    nf = HEAD_DIM // 4
    inv = ROPE_THETA ** (-jnp.arange(nf, dtype=F32) / nf)
    ang = jnp.concatenate([(t // GRID_W)[:, None] * inv, (t % GRID_W)[:, None] * inv], axis=-1)
    cos, sin = jnp.cos(ang), jnp.sin(ang)
    cos2 = jnp.concatenate([jnp.ones((L, HEAD_DIM), F32), jnp.concatenate([cos, cos], axis=-1)], axis=0)
    sin2 = jnp.concatenate([jnp.zeros((L, HEAD_DIM), F32), jnp.concatenate([-sin, sin], axis=-1)], axis=0)
    return cos2, sin2


def _step(x, c, ctx, loss_target, W, M, V):
    xi, yi, ci = lax.axis_index("x"), lax.axis_index("y"), lax.axis_index("c")
    chip = 2 * xi + yi
    dev = 2 * chip + ci
    south = (ci == 0).astype(F32)
    cvec = ci.reshape(1).astype(jnp.int32)
    svec = chip.reshape(1).astype(jnp.int32)
    T, D = x.shape[1], x.shape[2]
    L = ctx.shape[1]
    R = L + T
    F = 4 * W["w_down"].shape[1]
    GK, GV = D // 2, D
    DK, DV = GK // GLA_HEADS, GV // GLA_HEADS
    N6 = 6 * D
    N4 = N6 // 4
    widths, orig, order, lay, z_used, Z = _layouts(D)

    def place_cols(shard, full_cols):
        cols = shard.shape[-1]
        full = jnp.zeros(shard.shape[:-1] + (full_cols,), F32)
        return lax.dynamic_update_slice(full, shard * south, (0,) * (shard.ndim - 1) + (chip * cols,))

    c_rows = lax.dynamic_update_slice(jnp.zeros((8, D), F32), c, (dev, 0))
    bufa, meta = _pack([c_rows, place_cols(W["w_gate_f"][0], GK), place_cols(W["w_gate_b"][0], GK),
                        place_cols(W["conv_w"][0], 2 * F)])
    c_all, wgf, wgb, cw = _unpack(_allreduce(bufa, "gather_small"), meta)
    ca = jnp.concatenate([c_all, W["c_ctx"][None, :], jnp.zeros((7, D), F32)], axis=0)
    b_shard = lax.dynamic_slice(W["b_mod"], (0, chip * N4), (1, N4))
    mod_part, sil = _mod_fwd(ca, W["w_mod"][0], b_shard, "mod_fwd")
    slots = lax.dynamic_update_slice(jnp.zeros((4, 16, N4), F32), (mod_part * south)[None], (chip, 0, 0))
    mod_all = _allreduce(slots.reshape(64, N4), "gather_mod").reshape(4, 16, N4).transpose(1, 0, 2).reshape(16, N6)
    mx = lax.dynamic_slice(mod_all, (dev, 0), (1, N6)).reshape(6, 1, D)
    mc = mod_all[8].reshape(6, 1, D)

    sq = lambda a: a.reshape(a.shape[1:])
    shards = [sq(W[n]).astype(BF16) for n in BIG_NAMES]
    own = lambda g, s, n: _place_own(g, s, svec, "place_" + n)
    cols = lambda g: g.transpose(1, 0, 2).reshape(g.shape[1], 4 * g.shape[2])
    rows = lambda g: g.reshape(4 * g.shape[1], g.shape[2])
    w_in_f = cols(own(_allgather_weights(shards[:1], "gather_w_in")[0], shards[0], "w_in"))
    sc1 = jnp.stack([mc[1], mx[1]])
    sh1 = jnp.stack([mc[0], mx[0]])
    h = _modnorm_fwd(ctx[0], x[0], W["g_mix"], sc1, sh1, "modnorm1")
    seg = lambda s: w_in_f[:, orig[s]:orig[s] + widths[s]]
    w_cat = jnp.concatenate([jnp.pad(seg(s), ((0, 0), (0, LANES - widths[s]))) if s == "lr" else seg(s)
                             for s in order] + [jnp.zeros((D, Z - z_used), BF16)], axis=1)
    gather1, gather2, gather_outs = _gather_plan(shards[1:], "in")
    wg = jnp.zeros((2, LANES, GK), F32).at[0, :GLA_LOWRANK].set(wgf).at[1, GLA_LOWRANK:2 * GLA_LOWRANK].set(wgb)
    bg = jnp.stack([W["b_gate_f"], W["b_gate_b"]])
    cb = W["conv_b"]
    sink_rows = jnp.broadcast_to(W["attn_sink"][0][:, None], (N_Q_HEADS, HEAD_DIM))
    cos2, sin2 = _rope_tables(T, L)
    blk = lambda s, w: lay[s] // w

    z, landed = _matmul(h, w_cat, "nn", F32, "proj_in", tn=1536, ride=(shards[1:], gather_outs, gather1, {}))
    qn = _qknorm_fwd(z, blk("qa", widths["qa"]), T, L, W["q_norm"], cos2, sin2, N_Q_HEADS, "qnorm")
    kn = _qknorm_fwd(z, blk("ka", widths["ka"]), R, 0, W["k_norm"], cos2, sin2, N_KV_HEADS, "knorm")
    vb = _cast_seg(z, blk("va", widths["va"]), widths["va"], "vcast")
    o_attn, landed = _attn_fwd(qn, kn, vb, sink_rows, L, "attn_fwd",
                               ride=(landed, gather_outs, gather2, {n: n for n in range(len(landed))}))
    g_ao, g_go, g_out, g_up, g_dn = [own(g, s, n) for g, s, n in zip(landed, shards[1:], BIG_NAMES[1:])]
    w_ao, w_go, w_out, w_up, w_dn = rows(g_ao), rows(g_go), rows(g_out), cols(g_up), rows(g_dn)
    gla_blks = (blk("qb", GK), blk("kb", GK), blk("vb", GV), blk("lr", LANES))
    o_g, sprev = _gla_fwd(z, *gla_blks, wg, bg, DV, L, "gla_fwd")
    p = _glanorm_fwd(o_g, z, blk("rb", D), W["gla_norm"], L, "glanorm")
    ya = _matmul(o_attn, w_ao, "nn", BF16, "proj_attn_o")
    yg = _matmul(p, w_go, "nn", BF16, "proj_gla_o")
    m = _gate_fwd(z, blk("ga", D), blk("gb", D), ya, yg, L, "gate")
    mix = _matmul(m, w_out, "nn", F32, "proj_out")
    x1, h2 = _resnorm_fwd(x[0], mix, mx[2], W["g_ffn"], mx[4], mx[3], "resnorm2")
    u = _matmul(h2, w_up, "nn", F32, "ffn_up")
    f = _conv_fwd(u, cw, cb, "conv_swiglu")
    d = _matmul(f, w_dn, "nn", F32, "ffn_down", tk=2816)
    dy, dd, lacc = _loss_head(d, x1, mx[5], loss_target[0], "loss_head")
    loss = lax.psum((0.5 / D) * jnp.sum(lacc[0]), ("x", "y", "c"))

    gw_dn = _matmul(f, dd, "tn", F32, "ffn_down_dw")
    df = _matmul(dd, w_dn, "nt", F32, "ffn_down_dx")
    du, acca, accg = _conv_bwd(u, df, cw, cb, "conv_swiglu_bwd")
    gw_up = _matmul(h2, du, "tn", F32, "ffn_up_dw", tm=512, halves="b", col_shards=4)
    dh2 = _matmul(du, w_up, "nt", F32, "ffn_up_dx", tk=2816, halves="a")
    dx1, dmix, s2 = _resnorm_bwd(x1, dh2, W["g_ffn"], mx[4], dy, mix, mx[2], "resnorm2_bwd")
    gw_out = _matmul(m, dmix, "tn", F32, "proj_out_dw")
    dm = _matmul(dmix, w_out, "nt", BF16, "proj_out_dx")
    dya, dyg, dga, dgb = _gate_bwd(z, blk("ga", D), blk("gb", D), ya, yg, dm, L, "gate_bwd")
    gw_ao = _matmul(o_attn, dya, "tn", F32, "proj_attn_o_dw")
    do_attn = _matmul(dya, w_ao, "nt", BF16, "proj_attn_o_dx")
    gw_go = _matmul(p, dyg, "tn", F32, "proj_gla_o_dw")
    dp = _matmul(dyg, w_go, "nt", BF16, "proj_gla_o_dx")
    do_gla, drb, s_gn = _glanorm_bwd(o_g, z, blk("rb", D), W["gla_norm"], dp, L, "glanorm_bwd")
    do_pad = jnp.concatenate([jnp.zeros((L, GV), BF16), do_gla], axis=0)
    by_cols = lambda g: g.reshape(g.shape[0], 4, g.shape[1] // 4).transpose(1, 0, 2)
    by_rows = lambda g: g.reshape(4, g.shape[0] // 4, g.shape[1])
    early = [by_rows(gw_ao), by_rows(gw_go), by_rows(gw_out), gw_up, by_rows(gw_dn)]
    (dqg, dkg, dvg, dpre, dbg), pair_e = _gla_bwd(z, *gla_blks, wg, bg, sprev, do_pad, L, "gla_bwd",
                                                  ride=(early, *_pair_plan(early), {}))
    sums_e = [_add_pair(g, a, cvec, "reduce_early_add%d" % n) for n, (g, a) in enumerate(zip(early, pair_e))]
    wg_cat = jnp.concatenate([wg[0], wg[1]], axis=1)
    dlr = _matmul(dpre, wg_cat, "nt", BF16, "gla_gate_dx")
    dwg = _matmul(z[:, lay["lr"]:lay["lr"] + LANES], dpre, "tn", F32, "gla_gate_dw")
    (dqn, dkw, dvw, dkc, dvc, dsn), chips_e = _attn_bwd(qn, kn, vb, sink_rows, do_attn, L, "attn_bwd",
                                                        ride=(sums_e, *_chips_plan(sums_e), {}))
    mine_e = [_sum_chips(g, a, b, cvec, svec, "reduce_early_sum%d" % n)
              for n, (g, a, b) in enumerate(zip(early, pair_e, chips_e))]
    dqa, s_qn = _qknorm_bwd(z, blk("qa", widths["qa"]), T, L, W["q_norm"], cos2, sin2, dqn, N_Q_HEADS, "qnorm_bwd")
    dk_all = jnp.concatenate([dkc, dkw[WINDOW:WINDOW + T]], axis=0)
    dv_all = jnp.concatenate([dvc, dvw[WINDOW:WINDOW + T]], axis=0)
    dka, s_kn = _qknorm_bwd(z, blk("ka", widths["ka"]), R, 0, W["k_norm"], cos2, sin2, dk_all, N_KV_HEADS, "knorm_bwd")
    dz = _assemble_dz(lay, z_used, Z, L, dqa, drb, dga, dgb, dka, dv_all, dvg, dqg, dkg, dlr, "assemble_dz")
    gw_cat, other_e = _matmul(h, dz, "tn", F32, "proj_in_dw", tn=768, tk=2816,
                              ride=(mine_e, *_halves_plan(mine_e), {}))
    gw_in = jnp.concatenate([gw_cat[:, lay[s]:lay[s] + widths[s]] for s in ["qa", "ka", "va", "qb", "kb", "vb", "rb",
                                                                           "lr", "ga", "gb"]], axis=1)
    late = [by_cols(gw_in)]
    shapes, stage = _pair_plan(late)
    pair_l = _exchange(late, shapes, [stage], "reduce_late_pair")
    sums_l = [_add_pair(late[0], pair_l[0], cvec, "reduce_late_add")]
    dh, chips_l = _matmul(dz, w_cat, "nt", F32, "proj_in_dx", tk=4608, ride=(sums_l, *_chips_plan(sums_l), {}))
    mine_l = [_sum_chips(late[0], pair_l[0], chips_l[0], cvec, svec, "reduce_late_sum")]
    shapes, stage = _halves_plan(mine_l)
    other_l = _exchange(mine_l, shapes, [stage], "reduce_late_halves")
    mine, other = mine_l + mine_e, list(other_l) + other_e
    grad_x, s1 = _modnorm_bwd(x[0], dh, W["g_mix"], mx[1], dx1, "modnorm1_bwd", dh_roff=L)
    _, s1c = _modnorm_bwd(ctx[0], dh, W["g_mix"], mc[1], None, "modnorm1_ctx_bwd")

    dmod_x = jnp.concatenate([s1[0], s1[1], s2[3], s2[0], s2[1], lacc[1]])
    dmod_c = jnp.concatenate([s1c[0], s1c[1], jnp.zeros((4 * D,), F32)])
    dmod_rows = lax.dynamic_update_slice(jnp.zeros((9, N6), F32).at[8].set(dmod_c), dmod_x[None], (dev, 0))
    small = [dmod_rows, dmod_x + dmod_c, s1[2] + s1c[2], s_qn[0], s_kn[0], dsn[:, 0, :Q_PER_KV].reshape(N_Q_HEADS),
             dwg[:GLA_LOWRANK, :GK], dbg[0].reshape(GK), dwg[GLA_LOWRANK:2 * GLA_LOWRANK, GK:], dbg[1].reshape(GK),
             s_gn[0], s2[2], jnp.concatenate([acca[0:3], accg[0:3]], axis=1), jnp.concatenate([acca[3], accg[3]])]
    bufc, meta = _pack(small)
    (dmod_sum, g_b_mod, g_g_mix, g_q_norm, g_k_norm, g_sink, g_wgf, g_bgf, g_wgb, g_bgb, g_gla_norm, g_g_ffn,
     g_conv_w, g_conv_b) = _unpack(_allreduce(bufc, "reduce_small"), meta)
    dmod16 = lax.dynamic_slice(jnp.concatenate([dmod_sum, jnp.zeros((7, N6), F32)], axis=0), (0, chip * N4), (16, N4))
    g_w_mod = _matmul(sil, dmod16, "tn", F32, "mod_dw")
    dsil = _matmul(dmod16, W["w_mod"][0], "nt", F32, "mod_dx")
    g_c_ctx = _silu_bwd(_allreduce(dsil * south, "reduce_cctx"), ca, "silu_bwd")[8]

    cut = lambda g: lax.dynamic_slice(g, (0, chip * (g.shape[1] // 4)), (g.shape[0], g.shape[1] // 4))
    grads = {"c_ctx": g_c_ctx, "w_mod": g_w_mod[None], "b_mod": g_b_mod[None], "g_mix": g_g_mix[None],
             "q_norm": g_q_norm[None], "k_norm": g_k_norm[None], "attn_sink": g_sink[None],
             "w_gate_f": cut(g_wgf)[None], "b_gate_f": g_bgf[None], "w_gate_b": cut(g_wgb)[None],
             "b_gate_b": g_bgb[None], "gla_norm": g_gla_norm[None], "g_ffn": g_g_ffn[None],
             "conv_w": cut(g_conv_w)[None], "conv_b": g_conv_b[None]}

    delta, new_m, new_v = {}, {}, {}
    dl, mn, vn = _adamw(W["w_mod"][0], g_w_mod, M["w_mod"][0], V["w_mod"][0], "adamw_w_mod")
    delta["w_mod"], new_m["w_mod"], new_v["w_mod"] = dl[None], mn[None], vn[None]
    for n, a, b in zip(BIG_NAMES, mine, other):
        g, dl, mn, vn = _adamw_halves(sq(W[n]), a, b, sq(M[n]), sq(V[n]), cvec, "adamw_" + n)
        grads[n], delta[n], new_m[n], new_v[n] = g[None], dl[None], mn[None], vn[None]
    small_names = [n for n in WEIGHT_NAMES if n not in delta]
    packs = [_pack([src[n] for n in small_names]) for src in (W, grads, M, V)]
    meta = packs[0][1]
    outs = _adamw(packs[0][0], packs[1][0], packs[2][0], packs[3][0], "adamw_small")
    for res, o in zip((delta, new_m, new_v), outs):
        for n, a in zip(small_names, _unpack(o, meta)):
            res[n] = a
    return (loss, grad_x[None], *[grads[n] for n in WEIGHT_NAMES], *[delta[n] for n in WEIGHT_NAMES],
            *[new_m[n] for n in WEIGHT_NAMES], *[new_v[n] for n in WEIGHT_NAMES])


def kernel(x, c, ctx, c_ctx, w_mod, b_mod, g_mix, w_in, q_norm, k_norm, attn_sink, w_gate_f, b_gate_f, w_gate_b, b_gate_b, gla_norm, w_attn_o, w_gla_o, w_out, g_ffn, w_up, conv_w, conv_b, w_down, loss_target, m_c_ctx, m_w_mod, m_b_mod, m_g_mix, m_w_in, m_q_norm, m_k_norm, m_attn_sink, m_w_gate_f, m_b_gate_f, m_w_gate_b, m_b_gate_b, m_gla_norm, m_w_attn_o, m_w_gla_o, m_w_out, m_g_ffn, m_w_up, m_conv_w, m_conv_b, m_w_down, v_c_ctx, v_w_mod, v_b_mod, v_g_mix, v_w_in, v_q_norm, v_k_norm, v_attn_sink, v_w_gate_f, v_b_gate_f, v_w_gate_b, v_b_gate_b, v_gla_norm, v_w_attn_o, v_w_gla_o, v_w_out, v_g_ffn, v_w_up, v_conv_w, v_conv_b, v_w_down):
    W = dict(zip(WEIGHT_NAMES, (c_ctx, w_mod, b_mod, g_mix, w_in, q_norm, k_norm, attn_sink, w_gate_f, b_gate_f,
                                w_gate_b, b_gate_b, gla_norm, w_attn_o, w_gla_o, w_out, g_ffn, w_up, conv_w, conv_b,
                                w_down)))
    M = dict(zip(WEIGHT_NAMES, (m_c_ctx, m_w_mod, m_b_mod, m_g_mix, m_w_in, m_q_norm, m_k_norm, m_attn_sink,
                                m_w_gate_f, m_b_gate_f, m_w_gate_b, m_b_gate_b, m_gla_norm, m_w_attn_o, m_w_gla_o,
                                m_w_out, m_g_ffn, m_w_up, m_conv_w, m_conv_b, m_w_down)))
    V = dict(zip(WEIGHT_NAMES, (v_c_ctx, v_w_mod, v_b_mod, v_g_mix, v_w_in, v_q_norm, v_k_norm, v_attn_sink,
                                v_w_gate_f, v_b_gate_f, v_w_gate_b, v_b_gate_b, v_gla_norm, v_w_attn_o, v_w_gla_o,
                                v_w_out, v_g_ffn, v_w_up, v_conv_w, v_conv_b, v_w_down)))
    return _step(x, c, ctx, loss_target, W, M, V)
```

```python
import functools
import math

import jax
import jax.numpy as jnp
from jax import lax
from jax.experimental import pallas as pl
from jax.experimental.pallas import tpu as pltpu

F32 = jnp.float32
BF16 = jnp.bfloat16
MESH = pl.DeviceIdType.MESH

EPS = 1e-6
HEAD_DIM = 128
N_Q_HEADS = 16
N_KV_HEADS = 4
Q_PER_KV = N_Q_HEADS // N_KV_HEADS
WINDOW = 128
GLA_HEADS = 4
GLA_LOWRANK = 16
GLA_GATE_NORM = 16.0
GLA_CHUNK = 64
GRID_W = 64
ROPE_THETA = 10000.0
GLA_LEVELS = (32, 16, 8, 4, 2, 1)
LANES = 128
MXU_TILE = 256

ADAM_LR = 0.001
ADAM_B1 = 0.9
ADAM_B2 = 0.999
ADAM_EPS = 1e-08
ADAM_WD = 0.01
ADAM_STEP = 10

VMEM_LIMIT = 52 * 1024 * 1024


def _cparams(*sem):
    return pltpu.CompilerParams(dimension_semantics=sem, vmem_limit_bytes=VMEM_LIMIT)


def _pick(n, target, mult=LANES):
    best = None
    d = mult
    while d <= min(n, target):
        if n % d == 0:
            best = d
        d += mult
    return n if best is None else best


def _sigmoid(x):
    return 1.0 / (1.0 + jnp.exp(-x))


def _silu(x):
    return x * _sigmoid(x)


def _dsilu(x):
    s = _sigmoid(x)
    return s * (1.0 + x * (1.0 - s))


def _dot(a, b, dims):
    return lax.dot_general(a, b, (dims, ((), ())), preferred_element_type=F32)


NN = ((1,), (0,))
NT = ((1,), (1,))
TN = ((0,), (0,))


def _matmul(a, b, mode, out_dtype, name, tm=1024, tn=1024, tk=2048, ride=None, halves=None, col_shards=None):
    if halves == "a":
        assert mode == "nt"
        (_, M, Kh), (N, K2) = a.shape, b.shape
        K = 2 * Kh
    elif halves == "b":
        assert mode == "tn"
        (K, M), (_, K2, Nh) = a.shape, b.shape
        N = 2 * Nh
    elif mode == "nn":
        (M, K), (K2, N) = a.shape, b.shape
    elif mode == "nt":
        (M, K), (N, K2) = a.shape, b.shape
    else:
        (K, M), (K2, N) = a.shape, b.shape
    assert K == K2, (name, a.shape, b.shape)
    pick = lambda n, t: _pick(n, t, MXU_TILE) if n % MXU_TILE == 0 else _pick(n, t)
    tm, tn, tk = pick(M, tm), pick(N // 2 if halves == "b" else N, tn), pick(K // 2 if halves == "a" else K, tk)
    if col_shards is not None:
        tn = N // col_shards
    nk = K // tk
    dims = {"nn": NN, "nt": NT, "tn": TN}[mode]

    def body(a_ref, b_ref, o_ref, acc_ref):
        k = pl.program_id(2)

        @pl.when(k == 0)
        def _():
            acc_ref[...] = jnp.zeros_like(acc_ref)

        av = a_ref[0] if halves == "a" else a_ref[...]
        bv = b_ref[0] if halves == "b" else b_ref[...]
        acc_ref[...] += _dot(av.astype(BF16), bv.astype(BF16), dims)

        @pl.when(k == nk - 1)
        def _():
            o_ref[...] = acc_ref[...].astype(out_dtype).reshape(o_ref.shape)

    if halves == "a":
        per = (K // 2) // tk
        a_spec = pl.BlockSpec((1, tm, tk), lambda i, j, k: (k // per, i, k % per))
    elif mode == "tn":
        a_spec = pl.BlockSpec((tk, tm), lambda i, j, k: (k, i))
    else:
        a_spec = pl.BlockSpec((tm, tk), lambda i, j, k: (i, k))
    if halves == "b":
        per = (N // 2) // tn
        b_spec = pl.BlockSpec((1, tk, tn), lambda i, j, k: (j // per, k, j % per))
    elif mode == "nt":
        b_spec = pl.BlockSpec((tn, tk), lambda i, j, k: (j, k))
    else:
        b_spec = pl.BlockSpec((tk, tn), lambda i, j, k: (k, j))
    if col_shards is None:
        out_spec, out_shape = pl.BlockSpec((tm, tn), lambda i, j, k: (i, j)), (M, N)
    else:
        assert tn * col_shards == N, (name, tn, N)
        out_spec, out_shape = pl.BlockSpec((1, tm, tn), lambda i, j, k: (j, i, 0)), (col_shards, M, tn)
    return _pcall(
        body, name=name, grid=(M // tm, N // tn, nk),
        in_specs=[a_spec, b_spec],
        out_specs=out_spec,
        out_shape=jax.ShapeDtypeStruct(out_shape, out_dtype),
        scratch_shapes=[pltpu.VMEM((tm, tn), F32)],
        sem=("parallel", "parallel", "arbitrary"), args=(a, b), ride=ride)


def _modnorm_fwd(xc, xl, g, sc, sh, name, ride=None):
    (L, D), T = xc.shape, xl.shape[0]
    tm = _pick(math.gcd(L, T), 256, 8)
    cb = L // tm

    def body(xc_ref, xl_ref, g_ref, sc_ref, sh_ref, h_ref):
        x = jnp.where(pl.program_id(0) < cb, xc_ref[...], xl_ref[...])
        r = lax.rsqrt(jnp.mean(x * x, axis=-1, keepdims=True) + EPS)
        n = x * r * g_ref[...]
        h_ref[...] = (n * (1.0 + sc_ref[0]) + sh_ref[0]).astype(BF16)

    sel = lambda i: (jnp.where(i < cb, 0, 1), 0, 0)
    return _pcall(
        body, name=name, grid=((L + T) // tm,),
        in_specs=[pl.BlockSpec((tm, D), lambda i: (jnp.minimum(i, cb - 1), 0)),
                  pl.BlockSpec((tm, D), lambda i: (jnp.maximum(i - cb, 0), 0)),
                  pl.BlockSpec((1, D), lambda i: (0, 0)), pl.BlockSpec((1, 1, D), sel), pl.BlockSpec((1, 1, D), sel)],
        out_specs=pl.BlockSpec((tm, D), lambda i: (i, 0)),
        out_shape=jax.ShapeDtypeStruct((L + T, D), BF16),
        sem=("parallel",), args=(xc, xl, g, sc, sh), ride=ride)


def _modnorm_bwd(x, dh, g, sc, resid, name, dh_roff=0):
    N, D = x.shape
    tm = _pick(math.gcd(N, dh_roff), 256, 8)
    ro = dh_roff // tm
    want_dx = resid is not None

    def body(*refs):
        if want_dx:
            x_ref, dh_ref, g_ref, sc_ref, res_ref, dx_ref, acc_ref = refs
        else:
            x_ref, dh_ref, g_ref, sc_ref, acc_ref = refs
        i = pl.program_id(0)

        @pl.when(i == 0)
        def _():
            acc_ref[...] = jnp.zeros_like(acc_ref)

        xv, dhv, gv = x_ref[...], dh_ref[...], g_ref[...]
        r = lax.rsqrt(jnp.mean(xv * xv, axis=-1, keepdims=True) + EPS)
        xh = xv * r
        dn = dhv * (1.0 + sc_ref[...])
        acc_ref[0:1, :] += jnp.sum(dhv, axis=0, keepdims=True)
        acc_ref[1:2, :] += jnp.sum(dhv * xh * gv, axis=0, keepdims=True)
        acc_ref[2:3, :] += jnp.sum(dn * xh, axis=0, keepdims=True)
        if want_dx:
            dxh = dn * gv
            dx_ref[...] = res_ref[...] + r * (dxh - xh * jnp.mean(dxh * xh, axis=-1, keepdims=True))

    row = pl.BlockSpec((tm, D), lambda i: (i, 0))
    drow = pl.BlockSpec((tm, D), lambda i: (i + ro, 0))
    vec = pl.BlockSpec((1, D), lambda i: (0, 0))
    acc = pl.BlockSpec((8, D), lambda i: (0, 0))
    acc_shape = jax.ShapeDtypeStruct((8, D), F32)
    if want_dx:
        return pl.pallas_call(
            body, name=name, grid=(N // tm,), in_specs=[row, drow, vec, vec, row],
            out_specs=[row, acc], out_shape=[jax.ShapeDtypeStruct((N, D), F32), acc_shape],
            compiler_params=_cparams("arbitrary"))(x, dh, g, sc, resid)
    sums = pl.pallas_call(
        body, name=name, grid=(N // tm,), in_specs=[row, drow, vec, vec],
        out_specs=acc, out_shape=acc_shape, compiler_params=_cparams("arbitrary"))(x, dh, g, sc)
    return None, sums


def _resnorm_bwd(x1, dh, g, sc, dy, mix, gt, name):
    N, D = x1.shape
    tm = _pick(N, 256, 8)

    def body(x_ref, dh_ref, g_ref, sc_ref, dy_ref, mix_ref, gt_ref, dx_ref, dm_ref, acc_ref):
        i = pl.program_id(0)

        @pl.when(i == 0)
        def _():
            acc_ref[...] = jnp.zeros_like(acc_ref)

        xv, dhv, gv = x_ref[...], dh_ref[...], g_ref[...]
        r = lax.rsqrt(jnp.mean(xv * xv, axis=-1, keepdims=True) + EPS)
        xh = xv * r
        dn = dhv * (1.0 + sc_ref[...])
        dxh = dn * gv
        dx = dy_ref[...] + r * (dxh - xh * jnp.mean(dxh * xh, axis=-1, keepdims=True))
        dx_ref[...] = dx
        dm_ref[...] = (dx * gt_ref[...]).astype(BF16)
        acc_ref[0:1, :] += jnp.sum(dhv, axis=0, keepdims=True)
        acc_ref[1:2, :] += jnp.sum(dhv * xh * gv, axis=0, keepdims=True)
        acc_ref[2:3, :] += jnp.sum(dn * xh, axis=0, keepdims=True)
        acc_ref[3:4, :] += jnp.sum(dx * mix_ref[...], axis=0, keepdims=True)

    row = pl.BlockSpec((tm, D), lambda i: (i, 0))
    vec = pl.BlockSpec((1, D), lambda i: (0, 0))
    return pl.pallas_call(
        body, name=name, grid=(N // tm,), in_specs=[row, row, vec, vec, row, row, vec],
        out_specs=[row, row, pl.BlockSpec((8, D), lambda i: (0, 0))],
        out_shape=[jax.ShapeDtypeStruct((N, D), F32), jax.ShapeDtypeStruct((N, D), BF16),
                   jax.ShapeDtypeStruct((8, D), F32)],
        compiler_params=_cparams("arbitrary"))(x1, dh, g, sc, dy, mix, gt)


def _qknorm_fwd(z, cblk, nrows, roff, w, cos2, sin2, nh, name):
    W = nh * HEAD_DIM
    tm = _pick(math.gcd(nrows, roff), 256, 8)
    ro = roff // tm
    assert roff % tm == 0

    def body(z_ref, w_ref, c_ref, s_ref, o_ref):
        c, s, wv = c_ref[...], s_ref[...], w_ref[...]
        for h in range(nh):
            x = z_ref[:, h * HEAD_DIM:(h + 1) * HEAD_DIM]
            r = lax.rsqrt(jnp.mean(x * x, axis=-1, keepdims=True) + EPS)
            y = x * r * wv
            o_ref[:, h * HEAD_DIM:(h + 1) * HEAD_DIM] = (y * c + pltpu.roll(y, HEAD_DIM // 2, 1) * s).astype(BF16)

    return pl.pallas_call(
        body, name=name, grid=(nrows // tm,),
        in_specs=[pl.BlockSpec((tm, W), lambda i: (i + ro, cblk)), pl.BlockSpec((1, HEAD_DIM), lambda i: (0, 0)),
                  pl.BlockSpec((tm, HEAD_DIM), lambda i: (i + ro, 0)), pl.BlockSpec((tm, HEAD_DIM), lambda i: (i + ro, 0))],
        out_specs=pl.BlockSpec((tm, W), lambda i: (i, 0)),
        out_shape=jax.ShapeDtypeStruct((nrows, W), BF16),
        compiler_params=_cparams("parallel"),
    )(z, w, cos2, sin2)


def _qknorm_bwd(z, cblk, nrows, roff, w, cos2, sin2, dy, nh, name):
    W = nh * HEAD_DIM
    tm = _pick(math.gcd(nrows, roff), 256, 8)
    ro = roff // tm

    def body(z_ref, w_ref, c_ref, s_ref, dy_ref, dz_ref, acc_ref):
        i = pl.program_id(0)

        @pl.when(i == 0)
        def _():
            acc_ref[...] = jnp.zeros_like(acc_ref)

        c, s, wv = c_ref[...], s_ref[...], w_ref[...]
        dw = jnp.zeros((1, HEAD_DIM), F32)
        for h in range(nh):
            sl = slice(h * HEAD_DIM, (h + 1) * HEAD_DIM)
            x = z_ref[:, sl]
            d = dy_ref[:, sl]
            dyn = d * c + pltpu.roll(d * s, HEAD_DIM // 2, 1)
            r = lax.rsqrt(jnp.mean(x * x, axis=-1, keepdims=True) + EPS)
            xh = x * r
            dw = dw + jnp.sum(dyn * xh, axis=0, keepdims=True)
            dxh = dyn * wv
            dz_ref[:, sl] = (r * (dxh - xh * jnp.mean(dxh * xh, axis=-1, keepdims=True))).astype(BF16)
        acc_ref[0:1, :] += dw

    return pl.pallas_call(
        body, name=name, grid=(nrows // tm,),
        in_specs=[pl.BlockSpec((tm, W), lambda i: (i + ro, cblk)), pl.BlockSpec((1, HEAD_DIM), lambda i: (0, 0)),
                  pl.BlockSpec((tm, HEAD_DIM), lambda i: (i + ro, 0)), pl.BlockSpec((tm, HEAD_DIM), lambda i: (i + ro, 0)),
                  pl.BlockSpec((tm, W), lambda i: (i, 0))],
        out_specs=[pl.BlockSpec((tm, W), lambda i: (i, 0)), pl.BlockSpec((8, HEAD_DIM), lambda i: (0, 0))],
        out_shape=[jax.ShapeDtypeStruct((nrows, W), BF16), jax.ShapeDtypeStruct((8, HEAD_DIM), F32)],
        compiler_params=_cparams("arbitrary"),
    )(z, w, cos2, sin2, dy)


def _cast_seg(z, cblk, width, name):
    R = z.shape[0]
    tm = _pick(R, 512, 8)

    def body(z_ref, o_ref):
        o_ref[...] = z_ref[...].astype(BF16)

    return pl.pallas_call(
        body, name=name, grid=(R // tm,),
        in_specs=[pl.BlockSpec((tm, width), lambda i: (i, cblk))],
        out_specs=pl.BlockSpec((tm, width), lambda i: (i, 0)),
        out_shape=jax.ShapeDtypeStruct((R, width), BF16), compiler_params=_cparams("parallel"))(z)


NEG_BIG = -1e30


KV_PER_STEP = 2
KV_PER_STEP_FWD = 4


def _attn_specs(T, n_ctx, kv_per_step=KV_PER_STEP):
    nb = T // WINDOW
    lb = n_ctx // WINDOW
    kvw = kv_per_step * HEAD_DIM
    blk = lambda f: pl.BlockSpec((WINDOW, kvw), f)
    win = [blk(lambda h, i: (lb + jnp.maximum(i - 1, 0), h)), blk(lambda h, i: (lb + i, h)),
           blk(lambda h, i: (lb + jnp.minimum(i + 1, nb - 1), h))]
    ctx = pl.BlockSpec((n_ctx, kvw), lambda h, i: (0, h))
    qspec = pl.BlockSpec((WINDOW, kv_per_step * Q_PER_KV * HEAD_DIM), lambda h, i: (i, h))
    sink = pl.BlockSpec((N_Q_HEADS, HEAD_DIM), lambda h, i: (0, 0))
    return nb, qspec, win, ctx, sink


def _attn_probs(q, kw, kctx, snk, valid):
    scale = HEAD_DIM ** -0.5
    s_lat = jnp.where(valid, _dot(q, kw, NT) * scale, NEG_BIG)
    s_ctx = _dot(q, kctx, NT) * scale
    m = jnp.maximum(jnp.maximum(jnp.max(s_lat, axis=-1, keepdims=True), jnp.max(s_ctx, axis=-1, keepdims=True)), snk)
    p_lat = jnp.exp(s_lat - m)
    p_ctx = jnp.exp(s_ctx - m)
    p_snk = jnp.exp(snk - m)
    den = p_snk + jnp.sum(p_lat, axis=-1, keepdims=True) + jnp.sum(p_ctx, axis=-1, keepdims=True)
    return p_lat, p_ctx, p_snk, den


def _attn_valid(i, T, heads):
    rows = heads * WINDOW
    qpos = i * WINDOW + (lax.broadcasted_iota(jnp.int32, (rows, 3 * WINDOW), 0) & (WINDOW - 1))
    kpos = (i - 1) * WINDOW + lax.broadcasted_iota(jnp.int32, (rows, 3 * WINDOW), 1)
    return (jnp.abs(qpos - kpos) <= WINDOW) & (kpos >= 0) & (kpos < T)


def _stack_heads(ref, hh):
    c0 = hh * Q_PER_KV * HEAD_DIM
    return jnp.concatenate([ref[:, c0 + g * HEAD_DIM:c0 + (g + 1) * HEAD_DIM] for g in range(Q_PER_KV)], axis=0)


def _stack_sinks(sink_ref, kvh):
    return jnp.concatenate([jnp.broadcast_to(sink_ref[pl.ds(kvh * Q_PER_KV + g, 1), :][:, 0:1], (WINDOW, 1))
                            for g in range(Q_PER_KV)], axis=0)


def _attn_window(refs, hh):
    return jnp.concatenate([r[:, hh * HEAD_DIM:(hh + 1) * HEAD_DIM] for r in refs], axis=0)


def _attn_fwd(qn, kn, vb, sink_rows, n_ctx, name, ride=None):
    T = qn.shape[0]
    nb, qspec, win, ctx, sink = _attn_specs(T, n_ctx, KV_PER_STEP_FWD)

    def body(q_ref, kp, kc, kx, vp, vc, vx, kctx_ref, vctx_ref, sink_ref, o_ref):
        h, i = pl.program_id(0), pl.program_id(1)
        valid = _attn_valid(i, T, Q_PER_KV)
        for hh in range(KV_PER_STEP_FWD):
            sl = slice(hh * HEAD_DIM, (hh + 1) * HEAD_DIM)
            kw, vw = _attn_window((kp, kc, kx), hh), _attn_window((vp, vc, vx), hh)
            kctx, vctx = kctx_ref[:, sl], vctx_ref[:, sl]
            p_lat, p_ctx, _, den = _attn_probs(_stack_heads(q_ref, hh), kw, kctx,
                                               _stack_sinks(sink_ref, h * KV_PER_STEP_FWD + hh), valid)
            o = ((_dot(p_lat.astype(BF16), vw, NN) + _dot(p_ctx.astype(BF16), vctx, NN)) / den).astype(BF16)
            for g in range(Q_PER_KV):
                c0 = (hh * Q_PER_KV + g) * HEAD_DIM
                o_ref[:, c0:c0 + HEAD_DIM] = o[g * WINDOW:(g + 1) * WINDOW]

    return _pcall(
        body, name=name, grid=(N_KV_HEADS // KV_PER_STEP_FWD, nb),
        in_specs=[qspec] + win + win + [ctx, ctx, sink],
        out_specs=qspec, out_shape=jax.ShapeDtypeStruct(qn.shape, BF16),
        sem=("parallel", "parallel"), args=(qn, kn, kn, kn, vb, vb, vb, kn, vb, sink_rows), ride=ride)


def _attn_bwd(qn, kn, vb, sink_rows, do, n_ctx, name, ride=None):
    T = qn.shape[0]
    nb, qspec, win, ctx, sink = _attn_specs(T, n_ctx)
    scale = HEAD_DIM ** -0.5
    TP = T + 2 * WINDOW

    def body(q_ref, kp, kc, kx, vp, vc, vx, kctx_ref, vctx_ref, sink_ref, do_ref,
             dq_ref, dkw_ref, dvw_ref, dkc_ref, dvc_ref, dsn_ref):
        h, i = pl.program_id(0), pl.program_id(1)

        @pl.when(i == 0)
        def _():
            dkw_ref[...] = jnp.zeros_like(dkw_ref)
            dvw_ref[...] = jnp.zeros_like(dvw_ref)
            dkc_ref[...] = jnp.zeros_like(dkc_ref)
            dvc_ref[...] = jnp.zeros_like(dvc_ref)
            dsn_ref[...] = jnp.zeros_like(dsn_ref)

        lane = lax.broadcasted_iota(jnp.int32, (8, HEAD_DIM), 1)
        valid = _attn_valid(i, T, Q_PER_KV)
        rows = pl.ds(pl.multiple_of(i * WINDOW, WINDOW), 3 * WINDOW)
        for hh in range(KV_PER_STEP):
            sl = slice(hh * HEAD_DIM, (hh + 1) * HEAD_DIM)
            kw, vw = _attn_window((kp, kc, kx), hh), _attn_window((vp, vc, vx), hh)
            kctx, vctx = kctx_ref[:, sl], vctx_ref[:, sl]
            q, d_o = _stack_heads(q_ref, hh), _stack_heads(do_ref, hh)
            p_lat, p_ctx, p_snk, den = _attn_probs(q, kw, kctx, _stack_sinks(sink_ref, h * KV_PER_STEP + hh), valid)
            inv = 1.0 / den
            p_lat, p_ctx, p_snk = p_lat * inv, p_ctx * inv, p_snk * inv
            dp_lat = _dot(d_o, vw, NT)
            dp_ctx = _dot(d_o, vctx, NT)
            dr = jnp.sum(p_lat * dp_lat, axis=-1, keepdims=True) + jnp.sum(p_ctx * dp_ctx, axis=-1, keepdims=True)
            ds_lat = (p_lat * (dp_lat - dr) * scale).astype(BF16)
            ds_ctx = (p_ctx * (dp_ctx - dr) * scale).astype(BF16)
            dq = _dot(ds_lat, kw, NN) + _dot(ds_ctx, kctx, NN)
            snk_terms = p_snk * dr
            dsn = jnp.zeros((8, HEAD_DIM), F32)
            for g in range(Q_PER_KV):
                c0 = (hh * Q_PER_KV + g) * HEAD_DIM
                dq_ref[:, c0:c0 + HEAD_DIM] = dq[g * WINDOW:(g + 1) * WINDOW]
                dsn = dsn + jnp.where(lane == g, -jnp.sum(snk_terms[g * WINDOW:(g + 1) * WINDOW], axis=0, keepdims=True),
                                      0.0)
            dkw_ref[rows, sl] += _dot(ds_lat, q, TN)
            dvw_ref[rows, sl] += _dot(p_lat.astype(BF16), d_o, TN)
            dkc_ref[:, sl] += _dot(ds_ctx, q, TN)
            dvc_ref[:, sl] += _dot(p_ctx.astype(BF16), d_o, TN)
            dsn_ref[hh] += dsn

    wacc = pl.BlockSpec((TP, KV_PER_STEP * HEAD_DIM), lambda h, i: (0, h))
    return _pcall(
        body, name=name, grid=(N_KV_HEADS // KV_PER_STEP, nb),
        in_specs=[qspec] + win + win + [ctx, ctx, sink, qspec],
        out_specs=[qspec, wacc, wacc, ctx, ctx, pl.BlockSpec((KV_PER_STEP, 8, HEAD_DIM), lambda h, i: (h, 0, 0))],
        out_shape=[jax.ShapeDtypeStruct(qn.shape, F32),
                   jax.ShapeDtypeStruct((TP, N_KV_HEADS * HEAD_DIM), F32),
                   jax.ShapeDtypeStruct((TP, N_KV_HEADS * HEAD_DIM), F32),
                   jax.ShapeDtypeStruct((n_ctx, N_KV_HEADS * HEAD_DIM), F32),
                   jax.ShapeDtypeStruct((n_ctx, N_KV_HEADS * HEAD_DIM), F32),
                   jax.ShapeDtypeStruct((N_KV_HEADS, 8, HEAD_DIM), F32)],
        sem=("arbitrary", "arbitrary"), args=(qn, kn, kn, kn, vb, vb, vb, kn, vb, sink_rows, do), ride=ride)


def _gla_masks(dirv):
    C = GLA_CHUNK

    def times(reps):
        r = lax.broadcasted_iota(jnp.int32, (C, reps * C), 0)
        c = lax.broadcasted_iota(jnp.int32, (C, reps * C), 1) & (C - 1)
        return jnp.where(dirv == 0, r, C - 1 - r), jnp.where(dirv == 0, c, C - 1 - c)

    def level(tt, ss, m):
        sh = m.bit_length() - 1
        same = (tt >> (sh + 1)) == (ss >> (sh + 1))
        return same, (tt >> sh) & 1, (ss >> sh) & 1

    tt, ss = times(3)
    le = (ss <= tt).astype(jnp.int32)
    sums = [le == 1]
    for m in GLA_LEVELS:
        same, ut, us = level(tt, ss, m)
        sums.append(same & (ut == us) & (ut == le))
    tt, ss = times(1)
    blocks = [ss == tt]
    for m in GLA_LEVELS:
        same, ut, us = level(tt, ss, m)
        blocks.append(same & (ut == 1) & (us == 0))
    mall3 = jnp.concatenate([jnp.where(s, 1.0, 0.0) for s in sums], axis=0).astype(BF16)
    return mall3, blocks


def _pieces(x):
    hi = x.astype(BF16)
    r1 = x - hi.astype(F32)
    mid = r1.astype(BF16)
    return hi, mid, (r1 - mid.astype(F32)).astype(BF16)


def _sum_f32(mall3, x):
    return _dot(mall3, jnp.concatenate(_pieces(x), axis=0), NN)


def _sum_f32_t(mall3, x):
    m = mall3[:, 0:GLA_CHUNK]
    hi, mid, lo = _pieces(x)
    return _dot(m, hi, TN) + _dot(m, mid, TN) + _dot(m, lo, TN)


def _gla_chunk_of(dirv, j, lc, nc):
    return jnp.where(dirv == 0, j, jnp.where(j < lc, lc - 1 - j, nc + lc - 1 - j))


def _gla_gate(lr_ref, wg_ref, bg_ref, d=0):
    pre = _dot(lr_ref[...].astype(BF16), wg_ref[d].astype(BF16), NN) + bg_ref[d]
    g = (jnp.minimum(pre, 0.0) - jnp.log(1.0 + jnp.exp(-jnp.abs(pre)))) * (1.0 / GLA_GATE_NORM)
    return pre, g


def _gla_fwd(z, qblk, kblk, vblk, lrblk, wg, bg, DV, n_ctx, name):
    R = z.shape[0]
    C = GLA_CHUNK
    DK = wg.shape[2] // GLA_HEADS
    nc, lc = R // C, n_ctx // C
    qscale = DK ** -0.5

    GK, GV = GLA_HEADS * DK, GLA_HEADS * DV

    def body(qf, kf, vf, lrf, qb, kb, vb, lrb, wg_ref, bg_ref, of_ref, ob_ref, sp_ref, st_ref):
        @pl.when(pl.program_id(0) == 0)
        def _():
            st_ref[...] = jnp.zeros_like(st_ref)

        for d, (q_ref, k_ref, v_ref, lr_ref, o_ref) in enumerate(((qf, kf, vf, lrf, of_ref), (qb, kb, vb, lrb, ob_ref))):
            mall, blocks = _gla_masks(d)
            _, g_all = _gla_gate(lr_ref, wg_ref, bg_ref, d)
            E_all = _sum_f32(mall, g_all)
            for h in range(GLA_HEADS):
                ks, vs = slice(h * DK, (h + 1) * DK), slice(h * DV, (h + 1) * DV)
                q, k, v = q_ref[:, ks] * qscale, k_ref[:, ks], v_ref[:, vs].astype(BF16)
                g, E = g_all[:, ks], E_all[:, ks]
                st = st_ref[d, h]
                sp_ref[d, h, 0] = st
                A = jnp.where(blocks[0], _dot(q.astype(BF16), k.astype(BF16), NT), 0.0)
                for l in range(len(GLA_LEVELS)):
                    e = jnp.exp(E[(1 + l) * C:(2 + l) * C])
                    A = A + jnp.where(blocks[l + 1], _dot((q * e).astype(BF16), (k * e).astype(BF16), NT), 0.0)
                o_ref[:, vs] = (_dot((q * jnp.exp(E[0:C])).astype(BF16), st.astype(BF16), NT)
                                + _dot(A.astype(BF16), v, NN))
                last = jnp.sum(g, axis=0, keepdims=True)
                st_ref[d, h] = jnp.exp(last) * st + _dot(v, (k * jnp.exp(last - E[0:C])).astype(BF16), TN)

    def ins(d):
        chunk = lambda j: _gla_chunk_of(d, j, lc, nc)
        return [pl.BlockSpec((C, GK), lambda j: (chunk(j), qblk)), pl.BlockSpec((C, GK), lambda j: (chunk(j), kblk)),
                pl.BlockSpec((C, GV), lambda j: (chunk(j), vblk)), pl.BlockSpec((C, LANES), lambda j: (chunk(j), lrblk))]

    return pl.pallas_call(
        body, name=name, grid=(nc,),
        in_specs=ins(0) + ins(1) + [pl.BlockSpec((2, LANES, GK), lambda j: (0, 0, 0)),
                                    pl.BlockSpec((2, 1, GK), lambda j: (0, 0, 0))],
        out_specs=[pl.BlockSpec((C, GV), lambda j: (_gla_chunk_of(0, j, lc, nc), 0)),
                   pl.BlockSpec((C, GV), lambda j: (_gla_chunk_of(1, j, lc, nc), 0)),
                   pl.BlockSpec((2, GLA_HEADS, 1, DV, DK), lambda j: (0, 0, j, 0, 0))],
        out_shape=[jax.ShapeDtypeStruct((R, GV), F32), jax.ShapeDtypeStruct((R, GV), F32),
                   jax.ShapeDtypeStruct((2, GLA_HEADS, nc, DV, DK), F32)],
        scratch_shapes=[pltpu.VMEM((2, GLA_HEADS, DV, DK), F32)],
        compiler_params=_cparams("arbitrary"),
    )(z, z, z, z, z, z, z, z, wg, bg)


def _gla_bwd(z, qblk, kblk, vblk, lrblk, wg, bg, sprev, do, n_ctx, name, ride=None):
    R = z.shape[0]
    C = GLA_CHUNK
    DK, DV = wg.shape[2] // GLA_HEADS, do.shape[1] // GLA_HEADS
    nc, lc = R // C, n_ctx // C
    qscale = DK ** -0.5
    nl = len(GLA_LEVELS)

    GK, GV = GLA_HEADS * DK, GLA_HEADS * DV

    def body(qf, kf, vf, lrf, dof, qb_, kb_, vb_, lrb, dob, wg_ref, bg_ref, sp_ref,
             dqf, dkf, dvf, dpf, dqb, dkb, dvb, dpb, dbg_ref, dst_ref):
        @pl.when(pl.program_id(0) == 0)
        def _():
            dst_ref[...] = jnp.zeros_like(dst_ref)
            dbg_ref[...] = jnp.zeros_like(dbg_ref)

        sides = ((qf, kf, vf, lrf, dof, dqf, dkf, dvf, dpf), (qb_, kb_, vb_, lrb, dob, dqb, dkb, dvb, dpb))
        for d, (q_ref, k_ref, v_ref, lr_ref, do_ref, dq_ref, dk_ref, dv_ref, dpre_ref) in enumerate(sides):
            mall, blocks = _gla_masks(d)
            pre_all, g_all = _gla_gate(lr_ref, wg_ref, bg_ref, d)
            E_all = _sum_f32(mall, g_all)
            for h in range(GLA_HEADS):
                ks, vs = slice(h * DK, (h + 1) * DK), slice(h * DV, (h + 1) * DV)
                q, k, v = q_ref[:, ks] * qscale, k_ref[:, ks], v_ref[:, vs].astype(BF16)
                pre, g, E = pre_all[:, ks], g_all[:, ks], E_all[:, ks]
                last = jnp.sum(g, axis=0, keepdims=True)
                eb, er, decay = jnp.exp(E[0:C]), jnp.exp(last - E[0:C]), jnp.exp(last)
                st = sp_ref[d, h, 0]
                dst = dst_ref[d, h]
                d_o = do_ref[:, vs]
                qe, kd = q * eb, k * er
                qb, kb = q.astype(BF16), k.astype(BF16)
                A = jnp.where(blocks[0], _dot(qb, kb, NT), 0.0)
                levels = []
                for l in range(nl):
                    e = jnp.exp(E[(1 + l) * C:(2 + l) * C])
                    ql, kl = q * e, k * e
                    levels.append((e, ql, kl, ql.astype(BF16), kl.astype(BF16)))
                    A = A + jnp.where(blocks[l + 1], _dot(levels[l][3], levels[l][4], NT), 0.0)
                dA = _dot(d_o, v, NT)
                dv_ref[:, vs] = (_dot(A.astype(BF16), d_o, TN)
                                 + _dot(kd.astype(BF16), dst.astype(BF16), NT)).astype(BF16)
                dqe = _dot(d_o, st.astype(BF16), NN)
                dkd = _dot(v, dst.astype(BF16), NN)
                G = jnp.where(blocks[0], dA, 0.0).astype(BF16)
                dq = dqe * eb + _dot(G, kb, NN)
                dk = dkd * er + _dot(G, qb, TN)
                dEr = dkd * kd
                dE = [dqe * qe - dEr]
                for l in range(nl):
                    e, ql, kl, qlb, klb = levels[l]
                    G = jnp.where(blocks[l + 1], dA, 0.0).astype(BF16)
                    dql = _dot(G, klb, NN)
                    dkl = _dot(G, qlb, TN)
                    dq = dq + dql * e
                    dk = dk + dkl * e
                    dE.append(dql * ql + dkl * kl)
                dlast = jnp.sum(dst * st, axis=0, keepdims=True) * decay + jnp.sum(dEr, axis=0, keepdims=True)
                dg = _sum_f32_t(mall, jnp.concatenate(dE, axis=0)) + dlast
                dpre = dg * (1.0 / GLA_GATE_NORM) / (1.0 + jnp.exp(pre))
                dq_ref[:, ks] = (dq * qscale).astype(BF16)
                dk_ref[:, ks] = dk.astype(BF16)
                dpre_ref[:, ks] = dpre.astype(BF16)
                dbg_ref[d, :, ks] += jnp.sum(dpre, axis=0, keepdims=True)
                dst_ref[d, h] = decay * dst + _dot(d_o, qe.astype(BF16), TN)

    def ins(d):
        chunk = lambda j: _gla_chunk_of(d, nc - 1 - j, lc, nc)
        return [pl.BlockSpec((C, GK), lambda j: (chunk(j), qblk)), pl.BlockSpec((C, GK), lambda j: (chunk(j), kblk)),
                pl.BlockSpec((C, GV), lambda j: (chunk(j), vblk)), pl.BlockSpec((C, LANES), lambda j: (chunk(j), lrblk)),
                pl.BlockSpec((C, GV), lambda j: (chunk(j), 0))]

    def outs(d):
        chunk = lambda j: _gla_chunk_of(d, nc - 1 - j, lc, nc)
        return [pl.BlockSpec((C, GK), lambda j: (chunk(j), 0)), pl.BlockSpec((C, GK), lambda j: (chunk(j), 0)),
                pl.BlockSpec((C, GV), lambda j: (chunk(j), 0)), pl.BlockSpec((C, GK), lambda j: (chunk(j), 0))]

    side_shapes = [jax.ShapeDtypeStruct((R, GK), BF16), jax.ShapeDtypeStruct((R, GK), BF16),
                   jax.ShapeDtypeStruct((R, GV), BF16), jax.ShapeDtypeStruct((R, GK), BF16)]
    return _pcall(
        body, name=name, grid=(nc,),
        in_specs=ins(0) + ins(1) + [pl.BlockSpec((2, LANES, GK), lambda j: (0, 0, 0)),
                                    pl.BlockSpec((2, 1, GK), lambda j: (0, 0, 0)),
                                    pl.BlockSpec((2, GLA_HEADS, 1, DV, DK), lambda j: (0, 0, nc - 1 - j, 0, 0))],
        out_specs=outs(0) + outs(1) + [pl.BlockSpec((2, 1, GK), lambda j: (0, 0, 0))],
        out_shape=side_shapes + side_shapes + [jax.ShapeDtypeStruct((2, 1, GK), F32)],
        scratch_shapes=[pltpu.VMEM((2, GLA_HEADS, DV, DK), F32)],
        sem=("arbitrary",), args=(z, z, z, z, do, z, z, z, z, do, wg, bg, sprev), ride=ride)


def _glanorm_fwd(of, ob, z, rbblk, gn, n_ctx, name):
    R, GV = of.shape
    T = R - n_ctx
    DV = GV // GLA_HEADS
    tm = _pick(n_ctx, 256, 8)
    ro = n_ctx // tm

    def body(o0_ref, o1_ref, rb_ref, gn_ref, p_ref):
        gnv = gn_ref[...]
        for h in range(GLA_HEADS):
            sl = slice(h * DV, (h + 1) * DV)
            og = o0_ref[:, sl] + o1_ref[:, sl]
            r = lax.rsqrt(jnp.mean(og * og, axis=-1, keepdims=True) + EPS)
            p_ref[:, sl] = (og * r * gnv * _silu(rb_ref[:, sl])).astype(BF16)

    return pl.pallas_call(
        body, name=name, grid=(T // tm,),
        in_specs=[pl.BlockSpec((tm, GV), lambda i: (i + ro, 0)), pl.BlockSpec((tm, GV), lambda i: (i + ro, 0)),
                  pl.BlockSpec((tm, GV), lambda i: (i + ro, rbblk)), pl.BlockSpec((1, DV), lambda i: (0, 0))],
        out_specs=pl.BlockSpec((tm, GV), lambda i: (i, 0)),
        out_shape=jax.ShapeDtypeStruct((T, GV), BF16), compiler_params=_cparams("parallel"))(of, ob, z, gn)


def _glanorm_bwd(of, ob, z, rbblk, gn, dp, n_ctx, name):
    R, GV = of.shape
    T = R - n_ctx
    DV = GV // GLA_HEADS
    tm = _pick(n_ctx, 256, 8)
    ro = n_ctx // tm

    def body(o0_ref, o1_ref, rb_ref, gn_ref, dp_ref, do_ref, drb_ref, acc_ref):
        i = pl.program_id(0)

        @pl.when(i == 0)
        def _():
            acc_ref[...] = jnp.zeros_like(acc_ref)

        gnv = gn_ref[...]
        dgn = jnp.zeros((1, DV), F32)
        for h in range(GLA_HEADS):
            sl = slice(h * DV, (h + 1) * DV)
            og = o0_ref[:, sl] + o1_ref[:, sl]
            rb = rb_ref[:, sl]
            d = dp_ref[:, sl]
            r = lax.rsqrt(jnp.mean(og * og, axis=-1, keepdims=True) + EPS)
            xh = og * r
            drb_ref[:, sl] = (d * xh * gnv * _dsilu(rb)).astype(BF16)
            dn = d * _silu(rb)
            dgn = dgn + jnp.sum(dn * xh, axis=0, keepdims=True)
            dxh = dn * gnv
            do_ref[:, sl] = (r * (dxh - xh * jnp.mean(dxh * xh, axis=-1, keepdims=True))).astype(BF16)
        acc_ref[0:1, :] += dgn

    row = pl.BlockSpec((tm, GV), lambda i: (i, 0))
    return pl.pallas_call(
        body, name=name, grid=(T // tm,),
        in_specs=[pl.BlockSpec((tm, GV), lambda i: (i + ro, 0)), pl.BlockSpec((tm, GV), lambda i: (i + ro, 0)),
                  pl.BlockSpec((tm, GV), lambda i: (i + ro, rbblk)), pl.BlockSpec((1, DV), lambda i: (0, 0)), row],
        out_specs=[row, row, pl.BlockSpec((8, DV), lambda i: (0, 0))],
        out_shape=[jax.ShapeDtypeStruct((T, GV), BF16), jax.ShapeDtypeStruct((T, GV), BF16),
                   jax.ShapeDtypeStruct((8, DV), F32)],
        compiler_params=_cparams("arbitrary"))(of, ob, z, gn, dp)


def _gate_fwd(z, gablk, gbblk, ya, yg, n_ctx, name):
    T, D = ya.shape
    tm = _pick(n_ctx, 256, 8)
    ro = n_ctx // tm

    def body(ga_ref, gb_ref, ya_ref, yg_ref, m_ref):
        m_ref[...] = (_sigmoid(ga_ref[...]) * ya_ref[...] + _sigmoid(gb_ref[...]) * yg_ref[...]).astype(BF16)

    row = pl.BlockSpec((tm, D), lambda i: (i, 0))
    return pl.pallas_call(
        body, name=name, grid=(T // tm,),
        in_specs=[pl.BlockSpec((tm, D), lambda i: (i + ro, gablk)), pl.BlockSpec((tm, D), lambda i: (i + ro, gbblk)), row, row],
        out_specs=row, out_shape=jax.ShapeDtypeStruct((T, D), BF16), compiler_params=_cparams("parallel"))(z, z, ya, yg)


def _gate_bwd(z, gablk, gbblk, ya, yg, dm, n_ctx, name):
    T, D = ya.shape
    tm = _pick(n_ctx, 256, 8)
    ro = n_ctx // tm

    def body(ga_ref, gb_ref, ya_ref, yg_ref, dm_ref, dya_ref, dyg_ref, dga_ref, dgb_ref):
        d = dm_ref[...]
        sa, sb = _sigmoid(ga_ref[...]), _sigmoid(gb_ref[...])
        dya_ref[...] = (d * sa).astype(BF16)
        dyg_ref[...] = (d * sb).astype(BF16)
        dga_ref[...] = (d * ya_ref[...] * sa * (1.0 - sa)).astype(BF16)
        dgb_ref[...] = (d * yg_ref[...] * sb * (1.0 - sb)).astype(BF16)

    row = pl.BlockSpec((tm, D), lambda i: (i, 0))
    sh = jax.ShapeDtypeStruct((T, D), BF16)
    return pl.pallas_call(
        body, name=name, grid=(T // tm,),
        in_specs=[pl.BlockSpec((tm, D), lambda i: (i + ro, gablk)), pl.BlockSpec((tm, D), lambda i: (i + ro, gbblk)), row, row, row],
        out_specs=[row] * 4, out_shape=[sh] * 4, compiler_params=_cparams("parallel"))(z, z, ya, yg, dm)


def _resnorm_fwd(x, mix, gt, g, sc, sh, name):
    T, D = x.shape
    tm = _pick(T, 256, 8)

    def body(x_ref, mix_ref, gt_ref, g_ref, sc_ref, sh_ref, x1_ref, h_ref):
        x1 = x_ref[...] + gt_ref[...] * mix_ref[...]
        x1_ref[...] = x1
        r = lax.rsqrt(jnp.mean(x1 * x1, axis=-1, keepdims=True) + EPS)
        h_ref[...] = (x1 * r * g_ref[...] * (1.0 + sc_ref[...]) + sh_ref[...]).astype(BF16)

    row = pl.BlockSpec((tm, D), lambda i: (i, 0))
    vec = pl.BlockSpec((1, D), lambda i: (0, 0))
    return pl.pallas_call(
        body, name=name, grid=(T // tm,), in_specs=[row, row, vec, vec, vec, vec], out_specs=[row, row],
        out_shape=[jax.ShapeDtypeStruct((T, D), F32), jax.ShapeDtypeStruct((T, D), BF16)],
        compiler_params=_cparams("parallel"))(x, mix, gt, g, sc, sh)


def _loss_head(d, x1, gt, target, name):
    T, D = d.shape
    tm = _pick(T, 256, 8)

    def body(d_ref, x1_ref, gt_ref, t_ref, dy_ref, dd_ref, acc_ref):
        i = pl.program_id(0)

        @pl.when(i == 0)
        def _():
            acc_ref[...] = jnp.zeros_like(acc_ref)

        dv, gtv = d_ref[...], gt_ref[...]
        e = x1_ref[...] + gtv * dv - t_ref[...]
        dy = e * (1.0 / D)
        dy_ref[...] = dy
        dd_ref[...] = (dy * gtv).astype(BF16)
        acc_ref[0:1, :] += jnp.sum(e * e, axis=0, keepdims=True)
        acc_ref[1:2, :] += jnp.sum(dy * dv, axis=0, keepdims=True)

    row = pl.BlockSpec((tm, D), lambda i: (i, 0))
    return pl.pallas_call(
        body, name=name, grid=(T // tm,), in_specs=[row, row, pl.BlockSpec((1, D), lambda i: (0, 0)), row],
        out_specs=[row, row, pl.BlockSpec((8, D), lambda i: (0, 0))],
        out_shape=[jax.ShapeDtypeStruct((T, D), F32), jax.ShapeDtypeStruct((T, D), BF16),
                   jax.ShapeDtypeStruct((8, D), F32)],
        compiler_params=_cparams("arbitrary"))(d, x1, gt, target)


def _halo_specs(T, tm, tw, col_of, order):
    n8 = tm // 8
    if order == "ij":
        mid = lambda i, j: (i, col_of(j))
        prev = lambda i, j: (jnp.maximum(i * n8 - 1, 0), col_of(j))
        nxt = lambda i, j: (jnp.minimum((i + 1) * n8, T // 8 - 1), col_of(j))
    else:
        mid = lambda j, i: (i, col_of(j))
        prev = lambda j, i: (jnp.maximum(i * n8 - 1, 0), col_of(j))
        nxt = lambda j, i: (jnp.minimum((i + 1) * n8, T // 8 - 1), col_of(j))
    return [pl.BlockSpec((tm, tw), mid), pl.BlockSpec((8, tw), prev), pl.BlockSpec((8, tw), nxt)]


def _shift_rows(x, before, after):
    tm = x.shape[0]
    row = lax.broadcasted_iota(jnp.int32, x.shape, 0)
    return (jnp.where(row == 0, before, pltpu.roll(x, 1, 0)),
            jnp.where(row == tm - 1, after, pltpu.roll(x, tm - 1, 0)))


def _conv_fwd(u, cw, cb, name):
    T, F2 = u.shape
    F = F2 // 2
    tm, tw = _pick(T, 256, 8), _pick(F, 512)
    nt, nw = T // tm, F // tw

    def body(ua, uap, uan, ug, ugp, ugn, cwa, cwg, cba, cbg, f_ref):
        i = pl.program_id(0)
        first, last = i == 0, i == nt - 1

        def conv(u_ref, up_ref, un_ref, w_ref, b_ref):
            m = u_ref[...]
            p, n = _shift_rows(m, jnp.where(first, 0.0, up_ref[7:8, :]), jnp.where(last, 0.0, un_ref[0:1, :]))
            return p * w_ref[0:1, :] + m * w_ref[1:2, :] + n * w_ref[2:3, :] + b_ref[...]

        a = conv(ua, uap, uan, cwa, cba)
        g = conv(ug, ugp, ugn, cwg, cbg)
        f_ref[...] = (_silu(a) * g).astype(BF16)

    wspec = lambda off: pl.BlockSpec((3, tw), lambda i, j: (0, j + off))
    bspec = lambda off: pl.BlockSpec((1, tw), lambda i, j: (0, j + off))
    return pl.pallas_call(
        body, name=name, grid=(nt, nw),
        in_specs=_halo_specs(T, tm, tw, lambda j: j, "ij") + _halo_specs(T, tm, tw, lambda j: j + nw, "ij")
        + [wspec(0), wspec(nw), bspec(0), bspec(nw)],
        out_specs=pl.BlockSpec((tm, tw), lambda i, j: (i, j)),
        out_shape=jax.ShapeDtypeStruct((T, F), BF16),
        compiler_params=_cparams("parallel", "parallel"),
    )(u, u, u, u, u, u, cw, cw, cb, cb)


def _conv_bwd(u, df, cw, cb, name):
    T, F2 = u.shape
    F = F2 // 2
    tm, tw = _pick(T, 256, 8), _pick(F, 512)
    nt, nw = T // tm, F // tw

    def body(ua, uap, uan, ug, ugp, ugn, cwa, cwg, cba, cbg, df_ref, dfp, dfn, du_ref, acca_ref, accg_ref):
        i = pl.program_id(1)

        @pl.when(i == 0)
        def _():
            acca_ref[...] = jnp.zeros_like(acca_ref)
            accg_ref[...] = jnp.zeros_like(accg_ref)

        first, last = i == 0, i == nt - 1
        wa, wg, ba, bg = cwa[...], cwg[...], cba[...], cbg[...]

        def conv(p, m, n, w, b):
            return p * w[0:1] + m * w[1:2] + n * w[2:3] + b

        def grads(a, g, d):
            return d * g * _dsilu(a), d * _silu(a)

        xa, xg, d = ua[...], ug[...], df_ref[...]
        sa = _shift_rows(xa, jnp.where(first, 0.0, uap[7:8, :]), jnp.where(last, 0.0, uan[0:1, :]))
        sg = _shift_rows(xg, jnp.where(first, 0.0, ugp[7:8, :]), jnp.where(last, 0.0, ugn[0:1, :]))
        da, dg = grads(conv(sa[0], xa, sa[1], wa, ba), conv(sg[0], xg, sg[1], wg, bg), d)
        da_p, dg_p = grads(conv(uap[6:7, :], uap[7:8, :], xa[0:1], wa, ba),
                           conv(ugp[6:7, :], ugp[7:8, :], xg[0:1], wg, bg), dfp[7:8, :])
        da_n, dg_n = grads(conv(xa[tm - 1:tm], uan[0:1, :], uan[1:2, :], wa, ba),
                           conv(xg[tm - 1:tm], ugn[0:1, :], ugn[1:2, :], wg, bg), dfn[0:1, :])
        ta = _shift_rows(da, jnp.where(first, 0.0, da_p), jnp.where(last, 0.0, da_n))
        tg = _shift_rows(dg, jnp.where(first, 0.0, dg_p), jnp.where(last, 0.0, dg_n))
        du_ref[0] = (ta[1] * wa[0:1] + da * wa[1:2] + ta[0] * wa[2:3]).astype(BF16)
        du_ref[1] = (tg[1] * wg[0:1] + dg * wg[1:2] + tg[0] * wg[2:3]).astype(BF16)
        for t, (va, vg) in enumerate(((sa[0], sg[0]), (xa, xg), (sa[1], sg[1]))):
            acca_ref[t:t + 1, :] += jnp.sum(da * va, axis=0, keepdims=True)
            accg_ref[t:t + 1, :] += jnp.sum(dg * vg, axis=0, keepdims=True)
        acca_ref[3:4, :] += jnp.sum(da, axis=0, keepdims=True)
        accg_ref[3:4, :] += jnp.sum(dg, axis=0, keepdims=True)

    wspec = lambda off: pl.BlockSpec((3, tw), lambda j, i: (0, j + off))
    bspec = lambda off: pl.BlockSpec((1, tw), lambda j, i: (0, j + off))
    row = pl.BlockSpec((tm, tw), lambda j, i: (i, j))
    acc = pl.BlockSpec((8, tw), lambda j, i: (0, j))
    return pl.pallas_call(
        body, name=name, grid=(nw, nt),
        in_specs=_halo_specs(T, tm, tw, lambda j: j, "ji") + _halo_specs(T, tm, tw, lambda j: j + nw, "ji")
        + [wspec(0), wspec(nw), bspec(0), bspec(nw)] + _halo_specs(T, tm, tw, lambda j: j, "ji"),
        out_specs=[pl.BlockSpec((2, tm, tw), lambda j, i: (0, i, j)), acc, acc],
        out_shape=[jax.ShapeDtypeStruct((2, T, F), BF16),
                   jax.ShapeDtypeStruct((8, F), F32), jax.ShapeDtypeStruct((8, F), F32)],
        compiler_params=_cparams("parallel", "arbitrary"),
    )(u, u, u, u, u, u, cw, cw, cb, cb, df, df, df)


def _assemble_dz(lay, z_used, Z, n_ctx, dqa, drb, dga, dgb, dka, dva, dvg, dqg, dkg, dlr, name):
    T = dqa.shape[0]
    R = T + n_ctx
    tm = _pick(n_ctx, 128, 8)
    cb = n_ctx // tm

    def body(dqa_ref, drb_ref, dga_ref, dgb_ref, dka_ref, dva_ref, dvg0, dvg1, dqg0, dqg1, dkg0, dkg1, dlr_ref, o_ref):
        lat = pl.program_id(0) >= cb

        def put(seg, val):
            o_ref[:, lay[seg]:lay[seg] + val.shape[1]] = val.astype(BF16)

        def lat_only(ref):
            v = ref[...]
            return jnp.where(lat, v, jnp.zeros_like(v))

        put("qa", lat_only(dqa_ref))
        put("rb", lat_only(drb_ref))
        put("ga", lat_only(dga_ref))
        put("gb", lat_only(dgb_ref))
        put("ka", dka_ref[...])
        put("va", dva_ref[...])
        put("vb", dvg0[...].astype(F32) + dvg1[...].astype(F32))
        put("qb", dqg0[...].astype(F32) + dqg1[...].astype(F32))
        put("kb", dkg0[...].astype(F32) + dkg1[...].astype(F32))
        put("lr", dlr_ref[...])
        if Z > z_used:
            o_ref[:, z_used:] = jnp.zeros((tm, Z - z_used), BF16)

    lat_spec = lambda a: pl.BlockSpec((tm, a.shape[1]), lambda i: (jnp.maximum(i - cb, 0), 0))
    all_spec = lambda a: pl.BlockSpec((tm, a.shape[1]), lambda i: (i, 0))
    dir_specs = lambda pair: [all_spec(pair[0]), all_spec(pair[1])]
    return pl.pallas_call(
        body, name=name, grid=(R // tm,),
        in_specs=[lat_spec(dqa), lat_spec(drb), lat_spec(dga), lat_spec(dgb), all_spec(dka), all_spec(dva)]
        + dir_specs(dvg) + dir_specs(dqg) + dir_specs(dkg) + [all_spec(dlr)],
        out_specs=pl.BlockSpec((tm, Z), lambda i: (i, 0)),
        out_shape=jax.ShapeDtypeStruct((R, Z), BF16), compiler_params=_cparams("parallel"),
    )(dqa, drb, dga, dgb, dka, dva, *dvg, *dqg, *dkg, dlr)


def _mod_fwd(ca, w, b, name):
    n, D = ca.shape
    N = w.shape[1]
    tn = _pick(N, 512)

    def body(c_ref, w_ref, b_ref, o_ref, s_ref):
        s = _silu(c_ref[...])
        s_ref[...] = s
        o_ref[...] = _dot(s.astype(BF16), w_ref[...].astype(BF16), NN) + b_ref[...]

    return pl.pallas_call(
        body, name=name, grid=(N // tn,),
        in_specs=[pl.BlockSpec((n, D), lambda j: (0, 0)), pl.BlockSpec((D, tn), lambda j: (0, j)),
                  pl.BlockSpec((1, tn), lambda j: (0, j))],
        out_specs=[pl.BlockSpec((n, tn), lambda j: (0, j)), pl.BlockSpec((n, D), lambda j: (0, 0))],
        out_shape=[jax.ShapeDtypeStruct((n, N), F32), jax.ShapeDtypeStruct((n, D), F32)],
        compiler_params=_cparams("arbitrary"))(ca, w, b)


def _silu_bwd(dsil, ca, name):
    def body(d_ref, c_ref, o_ref):
        o_ref[...] = d_ref[...] * _dsilu(c_ref[...])

    return pl.pallas_call(body, name=name, out_shape=jax.ShapeDtypeStruct(ca.shape, F32))(dsil, ca)


def _adam_math(w, g, m, v):
    c1 = 1.0 - ADAM_B1 ** ADAM_STEP
    c2 = 1.0 - ADAM_B2 ** ADAM_STEP
    mn = ADAM_B1 * m + (1.0 - ADAM_B1) * g
    vn = ADAM_B2 * v + (1.0 - ADAM_B2) * (g * g)
    return -ADAM_LR * ((mn / c1) / (jnp.sqrt(vn / c2) + ADAM_EPS) + ADAM_WD * w), mn, vn


def _adamw(w, g, m, v, name, ride=None):
    Rw, Cw = w.shape
    tr = _pick(Rw, 128, 8)

    def body(w_ref, g_ref, m_ref, v_ref, d_ref, mo_ref, vo_ref):
        d_ref[...], mo_ref[...], vo_ref[...] = _adam_math(w_ref[...], g_ref[...], m_ref[...], v_ref[...])

    row = pl.BlockSpec((tr, Cw), lambda i: (i, 0))
    sh = jax.ShapeDtypeStruct((Rw, Cw), F32)
    return _pcall(body, name=name, grid=(Rw // tr,), in_specs=[row] * 4, out_specs=[row] * 3, out_shape=[sh] * 3,
                  sem=("parallel",), args=(w, g, m, v), ride=ride)


HBM_SPEC = pl.BlockSpec(memory_space=pltpu.HBM)


def _exchange(inputs, out_shapes, stages, name):
    n_in, n_out = len(inputs), len(out_shapes)
    n = sum(len(s) for s in stages)

    def body(*refs):
        ins, outs = refs[:n_in], refs[n_in:n_in + n_out]
        send_sems, recv_sems = refs[n_in + n_out:]
        k = 0
        for stage in stages:
            copies = _stage_copies(stage, ins, outs, send_sems, recv_sems, k)
            for cp in copies:
                cp.start()
            for cp in copies:
                cp.wait()
            k += len(stage)

    return pl.pallas_call(
        body, name=name, in_specs=[HBM_SPEC] * n_in, out_specs=[HBM_SPEC] * n_out, out_shape=out_shapes,
        scratch_shapes=[pltpu.SemaphoreType.DMA((n,)), pltpu.SemaphoreType.DMA((n,))],
    )(*inputs)


def _stage_copies(stage, ins, outs, send_sems, recv_sems, k0=0):
    me = (lax.axis_index("x"), lax.axis_index("y"), lax.axis_index("c"))
    copies = []
    for k, ((skind, sidx), sfn, didx, dfn, flip) in enumerate(stage):
        src = (ins if skind == "in" else outs)[sidx].at[sfn(*me)]
        dst = outs[didx].at[dfn(*me)]
        if flip == (0, 0, 0):
            copies.append(pltpu.make_async_copy(src, dst, send_sems.at[k0 + k]))
        else:
            peer = tuple(1 - a if f else a for a, f in zip(me, flip))
            copies.append(pltpu.make_async_remote_copy(src, dst, send_sems.at[k0 + k], recv_sems.at[k0 + k],
                                                       device_id=peer, device_id_type=MESH))
    return copies


def _pcall(body, *, name, grid, in_specs, out_specs, out_shape, scratch_shapes=(), sem, args, ride=None):
    many = isinstance(out_shape, (list, tuple))
    out_specs, out_shape = (list(out_specs), list(out_shape)) if many else ([out_specs], [out_shape])
    if ride is None:
        res = pl.pallas_call(body, name=name, grid=grid, in_specs=list(in_specs), out_specs=out_specs,
                             out_shape=out_shape, scratch_shapes=list(scratch_shapes),
                             compiler_params=_cparams(*sem))(*args)
        return res if many else res[0]
    x_in, x_out, stage, aliases = ride
    n_in, n_out, n_scr, n_xin, n_xout = len(in_specs), len(out_specs), len(scratch_shapes), len(x_in), len(x_out)

    def wrapped(*refs):
        ins, xins = refs[:n_in], refs[n_in:n_in + n_xin]
        o0 = n_in + n_xin
        outs, xouts = refs[o0:o0 + n_out], refs[o0 + n_out:o0 + n_out + n_xout]
        s0 = o0 + n_out + n_xout
        scr, (send_sems, recv_sems) = refs[s0:s0 + n_scr], refs[s0 + n_scr:]
        first = functools.reduce(jnp.logical_and, [pl.program_id(d) == 0 for d in range(len(grid))])
        last = functools.reduce(jnp.logical_and, [pl.program_id(d) == grid[d] - 1 for d in range(len(grid))])

        @pl.when(first)
        def _():
            for cp in _stage_copies(stage, xins, xouts, send_sems, recv_sems):
                cp.start()

        body(*ins, *outs, *scr)

        @pl.when(last)
        def _():
            for cp in _stage_copies(stage, xins, xouts, send_sems, recv_sems):
                cp.wait()

    res = pl.pallas_call(
        wrapped, name=name, grid=grid, in_specs=list(in_specs) + [HBM_SPEC] * n_xin,
        out_specs=out_specs + [HBM_SPEC] * n_xout, out_shape=out_shape + list(x_out),
        scratch_shapes=list(scratch_shapes) + [pltpu.SemaphoreType.DMA((len(stage),)),
                                               pltpu.SemaphoreType.DMA((len(stage),))],
        input_output_aliases={n_in + a: n_out + b for a, b in aliases.items()},
        compiler_params=_cparams(*(["arbitrary"] * len(grid))))(*args, *x_in)
    main = res[:n_out]
    return (main if many else main[0]), list(res[n_out:])


FLIPS_ALL = [(0, 0, 1), (0, 1, 0), (0, 1, 1), (1, 0, 0), (1, 0, 1), (1, 1, 0), (1, 1, 1)]
FLIPS_CHIP = [(0, 1, 0), (1, 0, 0), (1, 1, 0)]


def _sum_slots(buf, name):
    n, r, w = buf.shape
    tr = _pick(r, 256, 8)

    def body(b_ref, o_ref):
        acc = b_ref[0]
        for s in range(1, n):
            acc = acc + b_ref[s]
        o_ref[...] = acc

    return pl.pallas_call(
        body, name=name, grid=(r // tr,), in_specs=[pl.BlockSpec((n, tr, w), lambda i: (0, i, 0))],
        out_specs=pl.BlockSpec((tr, w), lambda i: (i, 0)), out_shape=jax.ShapeDtypeStruct((r, w), F32),
        compiler_params=_cparams("parallel"))(buf)


def _allreduce_plan(buf):
    whole = lambda x, y, c: (slice(None), slice(None))
    slot = lambda x, y, c: (4 * x + 2 * y + c,)
    stage = [(("in", 0), whole, 0, slot, f) for f in [(0, 0, 0)] + FLIPS_ALL]
    return [jax.ShapeDtypeStruct((8,) + buf.shape, F32)], stage


def _allreduce(buf, name):
    shapes, stage = _allreduce_plan(buf)
    (slots,) = _exchange([buf], shapes, [stage], name + "_x")
    return _sum_slots(slots, name + "_sum")


def _gather_plan(shards, src):
    half = lambda a, c: pl.ds(c * (a.shape[0] // 2), a.shape[0] // 2)
    first, second = [], []
    for n, a in enumerate(shards):
        for f in FLIPS_CHIP:
            first.append((("in", n), lambda x, y, c, a=a: (half(a, c), slice(None)), n,
                          lambda x, y, c, a=a: (2 * x + y, half(a, c), slice(None)), f))
            peer_slot = lambda x, y, c, a=a, f=f: (2 * (x ^ f[0]) + (y ^ f[1]), half(a, c), slice(None))
            second.append(((src, n), peer_slot, n, peer_slot, (0, 0, 1)))
    outs = [jax.ShapeDtypeStruct((4,) + a.shape, a.dtype) for a in shards]
    return first, second, outs


def _allgather_weights(shards, name):
    first, second, outs = _gather_plan(shards, "out")
    return _exchange(shards, outs, [first, second], name)


def _place_own(buf, shard, svec, name):
    _, Rs, Cs = buf.shape
    tr = _pick(Rs, 256, 16)

    def body(s_ref, buf_ref, sh_ref, o_ref):
        o_ref[0] = sh_ref[...]

    grid_spec = pltpu.PrefetchScalarGridSpec(
        num_scalar_prefetch=1, grid=(Rs // tr,),
        in_specs=[pl.BlockSpec(memory_space=pl.ANY), pl.BlockSpec((tr, Cs), lambda i, s: (i, 0))],
        out_specs=pl.BlockSpec((1, tr, Cs), lambda i, s: (s[0], i, 0)))
    return pl.pallas_call(body, name=name, grid_spec=grid_spec, out_shape=jax.ShapeDtypeStruct(buf.shape, buf.dtype),
                          input_output_aliases={1: 0}, compiler_params=_cparams("arbitrary"))(svec, buf, shard)


def _add_pair(G, bufA, cvec, name):
    _, Rs, Cs = G.shape
    Rh = Rs // 2
    tr = _pick(Rh, 128, 16)
    nb = Rh // tr

    def body(c_ref, g_ref, a_ref, o_ref):
        o_ref[...] = (g_ref[...] + a_ref[...]).astype(BF16)

    grid_spec = pltpu.PrefetchScalarGridSpec(
        num_scalar_prefetch=1, grid=(4, nb),
        in_specs=[pl.BlockSpec((1, tr, Cs), lambda s, i, c_ref: (s, c_ref[0] * nb + i, 0)),
                  pl.BlockSpec((1, tr, Cs), lambda s, i, c_ref: (s, i, 0))],
        out_specs=pl.BlockSpec((1, tr, Cs), lambda s, i, c_ref: (s, i, 0)))
    return pl.pallas_call(body, name=name, grid_spec=grid_spec, out_shape=jax.ShapeDtypeStruct((4, Rh, Cs), BF16),
                          compiler_params=_cparams("parallel", "parallel"))(cvec, G, bufA)


def _sum_chips(G, bufA, bufB, cvec, svec, name):
    _, Rs, Cs = G.shape
    Rh = Rs // 2
    tr = _pick(Rh, 128, 16)
    nb = Rh // tr

    def body(c_ref, s_ref, g_ref, a_ref, b_ref, o_ref):
        o_ref[...] = (g_ref[0] + a_ref[0]) + b_ref[0].astype(F32) + b_ref[1].astype(F32) + b_ref[2].astype(F32)

    grid_spec = pltpu.PrefetchScalarGridSpec(
        num_scalar_prefetch=2, grid=(nb,),
        in_specs=[pl.BlockSpec((1, tr, Cs), lambda i, c, s: (s[0], c[0] * nb + i, 0)),
                  pl.BlockSpec((1, tr, Cs), lambda i, c, s: (s[0], i, 0)),
                  pl.BlockSpec((3, tr, Cs), lambda i, c, s: (0, i, 0))],
        out_specs=pl.BlockSpec((tr, Cs), lambda i, c, s: (i, 0)))
    return pl.pallas_call(body, name=name, grid_spec=grid_spec, out_shape=jax.ShapeDtypeStruct((Rh, Cs), F32),
                          compiler_params=_cparams("parallel"))(cvec, svec, G, bufA, bufB)


def _pair_plan(grads):
    Rh = [g.shape[1] // 2 for g in grads]
    whole3 = lambda x, y, c: (slice(None), slice(None), slice(None))
    stage = [(("in", n), lambda x, y, c, n=n: (slice(None), pl.ds((1 - c) * Rh[n], Rh[n]), slice(None)), n,
              whole3, (0, 0, 1)) for n in range(len(grads))]
    return [jax.ShapeDtypeStruct((4, Rh[n], g.shape[2]), F32) for n, g in enumerate(grads)], stage


def _chips_plan(P):
    stage = [(("in", n), lambda x, y, c, f=f: (2 * (x ^ f[0]) + (y ^ f[1]),), n, lambda x, y, c, k=k: (k,), f)
             for n in range(len(P)) for k, f in enumerate(FLIPS_CHIP)]
    return [jax.ShapeDtypeStruct((3,) + p.shape[1:], BF16) for p in P], stage


def _halves_plan(mine):
    whole2 = lambda x, y, c: (slice(None), slice(None))
    stage = [(("in", n), whole2, n, whole2, (0, 0, 1)) for n in range(len(mine))]
    return [jax.ShapeDtypeStruct(r.shape, F32) for r in mine], stage


def _adamw_halves(w, mine, other, m, v, cvec, name):
    Rs, Cs = w.shape
    Rh = Rs // 2
    tr = _pick(Rh, 128, 8)
    nb = Rh // tr

    def body(c_ref, w_ref, a_ref, b_ref, m_ref, v_ref, g_ref, d_ref, mo_ref, vo_ref):
        gv = jnp.where(pl.program_id(0) // nb == c_ref[0], a_ref[...], b_ref[...])
        g_ref[...] = gv
        d_ref[...], mo_ref[...], vo_ref[...] = _adam_math(w_ref[...], gv, m_ref[...], v_ref[...])

    row = pl.BlockSpec((tr, Cs), lambda i, c: (i, 0))
    hrow = pl.BlockSpec((tr, Cs), lambda i, c: (i % nb, 0))
    grid_spec = pltpu.PrefetchScalarGridSpec(num_scalar_prefetch=1, grid=(2 * nb,),
                                             in_specs=[row, hrow, hrow, row, row], out_specs=[row] * 4)
    return pl.pallas_call(body, name=name, grid_spec=grid_spec, out_shape=[jax.ShapeDtypeStruct((Rs, Cs), F32)] * 4,
                          compiler_params=_cparams("parallel"))(cvec, w, mine, other, m, v)


def _pack(arrays):
    flat = [a.reshape(-1).astype(F32) for a in arrays]
    meta, off = [], 0
    for a, f in zip(arrays, flat):
        meta.append((off, a.shape))
        off += f.shape[0]
    total = -(-off // (8 * LANES)) * (8 * LANES)
    flat.append(jnp.zeros((total - off,), F32))
    return jnp.concatenate(flat).reshape(total // LANES, LANES), meta


def _unpack(buf, meta):
    flat = buf.reshape(-1)
    out = []
    for off, shape in meta:
        size = 1
        for s in shape:
            size *= s
        out.append(flat[off:off + size].reshape(shape))
    return out


WEIGHT_NAMES = ["c_ctx", "w_mod", "b_mod", "g_mix", "w_in", "q_norm", "k_norm", "attn_sink", "w_gate_f", "b_gate_f",
                "w_gate_b", "b_gate_b", "gla_norm", "w_attn_o", "w_gla_o", "w_out", "g_ffn", "w_up", "conv_w",
                "conv_b", "w_down"]
BIG_NAMES = ["w_in", "w_attn_o", "w_gla_o", "w_out", "w_up", "w_down"]
SHARDED_SMALL = ["w_gate_f", "w_gate_b", "conv_w"]


def _layouts(D):
    aw, kvw, gk, gv = N_Q_HEADS * HEAD_DIM, N_KV_HEADS * HEAD_DIM, D // 2, D
    widths = {"qa": aw, "ka": kvw, "va": kvw, "qb": gk, "kb": gk, "vb": gv, "rb": gv, "lr": 2 * GLA_LOWRANK,
              "ga": D, "gb": D}
    orig, off = {}, 0
    for s in ["qa", "ka", "va", "qb", "kb", "vb", "rb", "lr", "ga", "gb"]:
        orig[s] = off
        off += widths[s]
    order = ["qa", "vb", "rb", "ga", "gb", "ka", "va", "qb", "kb", "lr"]
    lay, off = {}, 0
    for s in order:
        lay[s] = off
        off += LANES if s == "lr" else widths[s]
    align = {"qa": aw, "vb": D, "rb": D, "ga": D, "gb": D, "ka": kvw, "va": kvw, "qb": gk, "kb": gk,
             "lr": LANES}
    for s in order:
        assert lay[s] % align[s] == 0, (s, lay[s], align[s])
    return widths, orig, order, lay, off, -(-off // (2 * MXU_TILE)) * (2 * MXU_TILE)


def _rope_tables(T, L):
    t = jnp.arange(T)
    nf = HEAD_DIM // 4
    inv = ROPE_THETA ** (-jnp.arange(nf, dtype=F32) / nf)
    ang = jnp.concatenate([(t // GRID_W)[:, None] * inv, (t % GRID_W)[:, None] * inv], axis=-1)
    cos, sin = jnp.cos(ang), jnp.sin(ang)
    cos2 = jnp.concatenate([jnp.ones((L, HEAD_DIM), F32), jnp.concatenate([cos, cos], axis=-1)], axis=0)
    sin2 = jnp.concatenate([jnp.zeros((L, HEAD_DIM), F32), jnp.concatenate([-sin, sin], axis=-1)], axis=0)
    return cos2, sin2


def _step(x, c, ctx, loss_target, W, M, V):
    xi, yi, ci = lax.axis_index("x"), lax.axis_index("y"), lax.axis_index("c")
    chip = 2 * xi + yi
    dev = 2 * chip + ci
    south = (ci == 0).astype(F32)
    cvec = ci.reshape(1).astype(jnp.int32)
    svec = chip.reshape(1).astype(jnp.int32)
    T, D = x.shape[1], x.shape[2]
    L = ctx.shape[1]
    R = L + T
    F = 4 * W["w_down"].shape[1]
    GK, GV = D // 2, D
    DK, DV = GK // GLA_HEADS, GV // GLA_HEADS
    N6 = 6 * D
    N4 = N6 // 4
    widths, orig, order, lay, z_used, Z = _layouts(D)

    def place_cols(shard, full_cols):
        cols = shard.shape[-1]
        full = jnp.zeros(shard.shape[:-1] + (full_cols,), F32)
        return lax.dynamic_update_slice(full, shard * south, (0,) * (shard.ndim - 1) + (chip * cols,))

    c_rows = lax.dynamic_update_slice(jnp.zeros((8, D), F32), c, (dev, 0))
    bufa, meta = _pack([c_rows, place_cols(W["w_gate_f"][0], GK), place_cols(W["w_gate_b"][0], GK),
                        place_cols(W["conv_w"][0], 2 * F)])
    c_all, wgf, wgb, cw = _unpack(_allreduce(bufa, "gather_small"), meta)
    ca = jnp.concatenate([c_all, W["c_ctx"][None, :], jnp.zeros((7, D), F32)], axis=0)
    b_shard = lax.dynamic_slice(W["b_mod"], (0, chip * N4), (1, N4))
    mod_part, sil = _mod_fwd(ca, W["w_mod"][0], b_shard, "mod_fwd")
    slots = lax.dynamic_update_slice(jnp.zeros((4, 16, N4), F32), (mod_part * south)[None], (chip, 0, 0))
    mod_all = _allreduce(slots.reshape(64, N4), "gather_mod").reshape(4, 16, N4).transpose(1, 0, 2).reshape(16, N6)
    mx = lax.dynamic_slice(mod_all, (dev, 0), (1, N6)).reshape(6, 1, D)
    mc = mod_all[8].reshape(6, 1, D)

    sq = lambda a: a.reshape(a.shape[1:])
    shards = [sq(W[n]).astype(BF16) for n in BIG_NAMES]
    own = lambda g, s, n: _place_own(g, s, svec, "place_" + n)
    cols = lambda g: g.transpose(1, 0, 2).reshape(g.shape[1], 4 * g.shape[2])
    rows = lambda g: g.reshape(4 * g.shape[1], g.shape[2])
    w_in_f = cols(own(_allgather_weights(shards[:1], "gather_w_in")[0], shards[0], "w_in"))
    sc1 = jnp.stack([mc[1], mx[1]])
    sh1 = jnp.stack([mc[0], mx[0]])
    h = _modnorm_fwd(ctx[0], x[0], W["g_mix"], sc1, sh1, "modnorm1")
    seg = lambda s: w_in_f[:, orig[s]:orig[s] + widths[s]]
    w_cat = jnp.concatenate([jnp.pad(seg(s), ((0, 0), (0, LANES - widths[s]))) if s == "lr" else seg(s)
                             for s in order] + [jnp.zeros((D, Z - z_used), BF16)], axis=1)
    gather1, gather2, gather_outs = _gather_plan(shards[1:], "in")
    wg = jnp.zeros((2, LANES, GK), F32).at[0, :GLA_LOWRANK].set(wgf).at[1, GLA_LOWRANK:2 * GLA_LOWRANK].set(wgb)
    bg = jnp.stack([W["b_gate_f"], W["b_gate_b"]])
    cb = W["conv_b"]
    sink_rows = jnp.broadcast_to(W["attn_sink"][0][:, None], (N_Q_HEADS, HEAD_DIM))
    cos2, sin2 = _rope_tables(T, L)
    blk = lambda s, w: lay[s] // w

    z, landed = _matmul(h, w_cat, "nn", F32, "proj_in", tn=1536, ride=(shards[1:], gather_outs, gather1, {}))
    qn = _qknorm_fwd(z, blk("qa", widths["qa"]), T, L, W["q_norm"], cos2, sin2, N_Q_HEADS, "qnorm")
    kn = _qknorm_fwd(z, blk("ka", widths["ka"]), R, 0, W["k_norm"], cos2, sin2, N_KV_HEADS, "knorm")
    vb = _cast_seg(z, blk("va", widths["va"]), widths["va"], "vcast")
    o_attn, landed = _attn_fwd(qn, kn, vb, sink_rows, L, "attn_fwd",
                               ride=(landed, gather_outs, gather2, {n: n for n in range(len(landed))}))
    g_ao, g_go, g_out, g_up, g_dn = [own(g, s, n) for g, s, n in zip(landed, shards[1:], BIG_NAMES[1:])]
    w_ao, w_go, w_out, w_up, w_dn = rows(g_ao), rows(g_go), rows(g_out), cols(g_up), rows(g_dn)
    gla_blks = (blk("qb", GK), blk("kb", GK), blk("vb", GV), blk("lr", LANES))
    o_f, o_b, sprev = _gla_fwd(z, *gla_blks, wg, bg, DV, L, "gla_fwd")
    p = _glanorm_fwd(o_f, o_b, z, blk("rb", D), W["gla_norm"], L, "glanorm")
    ya = _matmul(o_attn, w_ao, "nn", BF16, "proj_attn_o")
    yg = _matmul(p, w_go, "nn", BF16, "proj_gla_o")
    m = _gate_fwd(z, blk("ga", D), blk("gb", D), ya, yg, L, "gate")
    mix = _matmul(m, w_out, "nn", F32, "proj_out")
    x1, h2 = _resnorm_fwd(x[0], mix, mx[2], W["g_ffn"], mx[4], mx[3], "resnorm2")
    u = _matmul(h2, w_up, "nn", F32, "ffn_up")
    f = _conv_fwd(u, cw, cb, "conv_swiglu")
    d = _matmul(f, w_dn, "nn", F32, "ffn_down", tk=2816)
    dy, dd, lacc = _loss_head(d, x1, mx[5], loss_target[0], "loss_head")
    loss = lax.psum((0.5 / D) * jnp.sum(lacc[0]), ("x", "y", "c"))

    gw_dn = _matmul(f, dd, "tn", F32, "ffn_down_dw")
    df = _matmul(dd, w_dn, "nt", F32, "ffn_down_dx")
    du, acca, accg = _conv_bwd(u, df, cw, cb, "conv_swiglu_bwd")
    gw_up = _matmul(h2, du, "tn", F32, "ffn_up_dw", tm=512, halves="b", col_shards=4)
    dh2 = _matmul(du, w_up, "nt", F32, "ffn_up_dx", tk=2816, halves="a")
    dx1, dmix, s2 = _resnorm_bwd(x1, dh2, W["g_ffn"], mx[4], dy, mix, mx[2], "resnorm2_bwd")
    gw_out = _matmul(m, dmix, "tn", F32, "proj_out_dw")
    dm = _matmul(dmix, w_out, "nt", BF16, "proj_out_dx")
    dya, dyg, dga, dgb = _gate_bwd(z, blk("ga", D), blk("gb", D), ya, yg, dm, L, "gate_bwd")
    gw_ao = _matmul(o_attn, dya, "tn", F32, "proj_attn_o_dw")
    do_attn = _matmul(dya, w_ao, "nt", BF16, "proj_attn_o_dx")
    gw_go = _matmul(p, dyg, "tn", F32, "proj_gla_o_dw")
    dp = _matmul(dyg, w_go, "nt", BF16, "proj_gla_o_dx")
    do_gla, drb, s_gn = _glanorm_bwd(o_f, o_b, z, blk("rb", D), W["gla_norm"], dp, L, "glanorm_bwd")
    do_pad = jnp.concatenate([jnp.zeros((L, GV), BF16), do_gla], axis=0)
    by_cols = lambda g: g.reshape(g.shape[0], 4, g.shape[1] // 4).transpose(1, 0, 2)
    by_rows = lambda g: g.reshape(4, g.shape[0] // 4, g.shape[1])
    early = [by_rows(gw_ao), by_rows(gw_go), by_rows(gw_out), gw_up, by_rows(gw_dn)]
    (dq_f, dk_f, dv_f, dpre_f, dq_b, dk_b, dv_b, dpre_b, dbg), pair_e = _gla_bwd(
        z, *gla_blks, wg, bg, sprev, do_pad, L, "gla_bwd", ride=(early, *_pair_plan(early), {}))
    dqg, dkg, dvg = (dq_f, dq_b), (dk_f, dk_b), (dv_f, dv_b)
    sums_e = [_add_pair(g, a, cvec, "reduce_early_add%d" % n) for n, (g, a) in enumerate(zip(early, pair_e))]
    wg_cat = jnp.concatenate([wg[0], wg[1]], axis=1)
    dpre = jnp.stack([dpre_f, dpre_b])
    dlr = _matmul(dpre, wg_cat, "nt", BF16, "gla_gate_dx", halves="a")
    dwg = _matmul(z[:, lay["lr"]:lay["lr"] + LANES], dpre, "tn", F32, "gla_gate_dw", halves="b")
    (dqn, dkw, dvw, dkc, dvc, dsn), chips_e = _attn_bwd(qn, kn, vb, sink_rows, do_attn, L, "attn_bwd",
                                                        ride=(sums_e, *_chips_plan(sums_e), {}))
    mine_e = [_sum_chips(g, a, b, cvec, svec, "reduce_early_sum%d" % n)
              for n, (g, a, b) in enumerate(zip(early, pair_e, chips_e))]
    dqa, s_qn = _qknorm_bwd(z, blk("qa", widths["qa"]), T, L, W["q_norm"], cos2, sin2, dqn, N_Q_HEADS, "qnorm_bwd")
    dk_all = jnp.concatenate([dkc, dkw[WINDOW:WINDOW + T]], axis=0)
    dv_all = jnp.concatenate([dvc, dvw[WINDOW:WINDOW + T]], axis=0)
    dka, s_kn = _qknorm_bwd(z, blk("ka", widths["ka"]), R, 0, W["k_norm"], cos2, sin2, dk_all, N_KV_HEADS, "knorm_bwd")
    dz = _assemble_dz(lay, z_used, Z, L, dqa, drb, dga, dgb, dka, dv_all, dvg, dqg, dkg, dlr, "assemble_dz")
    gw_cat, other_e = _matmul(h, dz, "tn", F32, "proj_in_dw", tn=768, tk=2816,
                              ride=(mine_e, *_halves_plan(mine_e), {}))
    gw_in = jnp.concatenate([gw_cat[:, lay[s]:lay[s] + widths[s]] for s in ["qa", "ka", "va", "qb", "kb", "vb", "rb",
                                                                           "lr", "ga", "gb"]], axis=1)
    late = [by_cols(gw_in)]
    shapes, stage = _pair_plan(late)
    pair_l = _exchange(late, shapes, [stage], "reduce_late_pair")
    sums_l = [_add_pair(late[0], pair_l[0], cvec, "reduce_late_add")]
    dh, chips_l = _matmul(dz, w_cat, "nt", F32, "proj_in_dx", tk=4608, ride=(sums_l, *_chips_plan(sums_l), {}))
    mine_l = [_sum_chips(late[0], pair_l[0], chips_l[0], cvec, svec, "reduce_late_sum")]
    shapes, stage = _halves_plan(mine_l)
    other_l = _exchange(mine_l, shapes, [stage], "reduce_late_halves")
    mine, other = mine_l + mine_e, list(other_l) + other_e
    grad_x, s1 = _modnorm_bwd(x[0], dh, W["g_mix"], mx[1], dx1, "modnorm1_bwd", dh_roff=L)
    _, s1c = _modnorm_bwd(ctx[0], dh, W["g_mix"], mc[1], None, "modnorm1_ctx_bwd")

    dmod_x = jnp.concatenate([s1[0], s1[1], s2[3], s2[0], s2[1], lacc[1]])
    dmod_c = jnp.concatenate([s1c[0], s1c[1], jnp.zeros((4 * D,), F32)])
    dmod_rows = lax.dynamic_update_slice(jnp.zeros((9, N6), F32).at[8].set(dmod_c), dmod_x[None], (dev, 0))
    small = [dmod_rows, dmod_x + dmod_c, s1[2] + s1c[2], s_qn[0], s_kn[0], dsn[:, 0, :Q_PER_KV].reshape(N_Q_HEADS),
             dwg[:GLA_LOWRANK, :GK], dbg[0].reshape(GK), dwg[GLA_LOWRANK:2 * GLA_LOWRANK, GK:], dbg[1].reshape(GK),
             s_gn[0], s2[2], jnp.concatenate([acca[0:3], accg[0:3]], axis=1), jnp.concatenate([acca[3], accg[3]])]
    bufc, meta = _pack(small)
    (dmod_sum, g_b_mod, g_g_mix, g_q_norm, g_k_norm, g_sink, g_wgf, g_bgf, g_wgb, g_bgb, g_gla_norm, g_g_ffn,
     g_conv_w, g_conv_b) = _unpack(_allreduce(bufc, "reduce_small"), meta)
    dmod16 = lax.dynamic_slice(jnp.concatenate([dmod_sum, jnp.zeros((7, N6), F32)], axis=0), (0, chip * N4), (16, N4))
    g_w_mod = _matmul(sil, dmod16, "tn", F32, "mod_dw")
    dsil = _matmul(dmod16, W["w_mod"][0], "nt", F32, "mod_dx")
    g_c_ctx = _silu_bwd(_allreduce(dsil * south, "reduce_cctx"), ca, "silu_bwd")[8]

    cut = lambda g: lax.dynamic_slice(g, (0, chip * (g.shape[1] // 4)), (g.shape[0], g.shape[1] // 4))
    grads = {"c_ctx": g_c_ctx, "w_mod": g_w_mod[None], "b_mod": g_b_mod[None], "g_mix": g_g_mix[None],
             "q_norm": g_q_norm[None], "k_norm": g_k_norm[None], "attn_sink": g_sink[None],
             "w_gate_f": cut(g_wgf)[None], "b_gate_f": g_bgf[None], "w_gate_b": cut(g_wgb)[None],
             "b_gate_b": g_bgb[None], "gla_norm": g_gla_norm[None], "g_ffn": g_g_ffn[None],
             "conv_w": cut(g_conv_w)[None], "conv_b": g_conv_b[None]}

    delta, new_m, new_v = {}, {}, {}
    dl, mn, vn = _adamw(W["w_mod"][0], g_w_mod, M["w_mod"][0], V["w_mod"][0], "adamw_w_mod")
    delta["w_mod"], new_m["w_mod"], new_v["w_mod"] = dl[None], mn[None], vn[None]
    for n, a, b in zip(BIG_NAMES, mine, other):
        g, dl, mn, vn = _adamw_halves(sq(W[n]), a, b, sq(M[n]), sq(V[n]), cvec, "adamw_" + n)
        grads[n], delta[n], new_m[n], new_v[n] = g[None], dl[None], mn[None], vn[None]
    small_names = [n for n in WEIGHT_NAMES if n not in delta]
    packs = [_pack([src[n] for n in small_names]) for src in (W, grads, M, V)]
    meta = packs[0][1]
    outs = _adamw(packs[0][0], packs[1][0], packs[2][0], packs[3][0], "adamw_small")
    for res, o in zip((delta, new_m, new_v), outs):
        for n, a in zip(small_names, _unpack(o, meta)):
            res[n] = a
    return (loss, grad_x[None], *[grads[n] for n in WEIGHT_NAMES], *[delta[n] for n in WEIGHT_NAMES],
            *[new_m[n] for n in WEIGHT_NAMES], *[new_v[n] for n in WEIGHT_NAMES])


def kernel(x, c, ctx, c_ctx, w_mod, b_mod, g_mix, w_in, q_norm, k_norm, attn_sink, w_gate_f, b_gate_f, w_gate_b, b_gate_b, gla_norm, w_attn_o, w_gla_o, w_out, g_ffn, w_up, conv_w, conv_b, w_down, loss_target, m_c_ctx, m_w_mod, m_b_mod, m_g_mix, m_w_in, m_q_norm, m_k_norm, m_attn_sink, m_w_gate_f, m_b_gate_f, m_w_gate_b, m_b_gate_b, m_gla_norm, m_w_attn_o, m_w_gla_o, m_w_out, m_g_ffn, m_w_up, m_conv_w, m_conv_b, m_w_down, v_c_ctx, v_w_mod, v_b_mod, v_g_mix, v_w_in, v_q_norm, v_k_norm, v_attn_sink, v_w_gate_f, v_b_gate_f, v_w_gate_b, v_b_gate_b, v_gla_norm, v_w_attn_o, v_w_gla_o, v_w_out, v_g_ffn, v_w_up, v_conv_w, v_conv_b, v_w_down):
    W = dict(zip(WEIGHT_NAMES, (c_ctx, w_mod, b_mod, g_mix, w_in, q_norm, k_norm, attn_sink, w_gate_f, b_gate_f,
                                w_gate_b, b_gate_b, gla_norm, w_attn_o, w_gla_o, w_out, g_ffn, w_up, conv_w, conv_b,
                                w_down)))
    M = dict(zip(WEIGHT_NAMES, (m_c_ctx, m_w_mod, m_b_mod, m_g_mix, m_w_in, m_q_norm, m_k_norm, m_attn_sink,
                                m_w_gate_f, m_b_gate_f, m_w_gate_b, m_b_gate_b, m_gla_norm, m_w_attn_o, m_w_gla_o,
                                m_w_out, m_g_ffn, m_w_up, m_conv_w, m_conv_b, m_w_down)))
    V = dict(zip(WEIGHT_NAMES, (v_c_ctx, v_w_mod, v_b_mod, v_g_mix, v_w_in, v_q_norm, v_k_norm, v_attn_sink,
                                v_w_gate_f, v_b_gate_f, v_w_gate_b, v_b_gate_b, v_gla_norm, v_w_attn_o, v_w_gla_o,
                                v_w_out, v_g_ffn, v_w_up, v_conv_w, v_conv_b, v_w_down)))
    return _step(x, c, ctx, loss_target, W, M, V)
```

```python
import functools
import math

import jax
import jax.numpy as jnp
from jax import lax
from jax.experimental import pallas as pl
from jax.experimental.pallas import tpu as pltpu

F32 = jnp.float32
BF16 = jnp.bfloat16
MESH = pl.DeviceIdType.MESH

EPS = 1e-6
HEAD_DIM = 128
N_Q_HEADS = 16
N_KV_HEADS = 4
Q_PER_KV = N_Q_HEADS // N_KV_HEADS
WINDOW = 128
GLA_HEADS = 4
GLA_LOWRANK = 16
GLA_GATE_NORM = 16.0
GLA_CHUNK = 64
GRID_W = 64
ROPE_THETA = 10000.0
GLA_LEVELS = (32, 16, 8, 4, 2, 1)
LANES = 128
MXU_TILE = 256

ADAM_LR = 0.001
ADAM_B1 = 0.9
ADAM_B2 = 0.999
ADAM_EPS = 1e-08
ADAM_WD = 0.01
ADAM_STEP = 10

VMEM_LIMIT = 52 * 1024 * 1024


def _cparams(*sem):
    return pltpu.CompilerParams(dimension_semantics=sem, vmem_limit_bytes=VMEM_LIMIT)


def _pick(n, target, mult=LANES):
    best = None
    d = mult
    while d <= min(n, target):
        if n % d == 0:
            best = d
        d += mult
    return n if best is None else best


def _sigmoid(x):
    return 1.0 / (1.0 + jnp.exp(-x))


def _silu(x):
    return x * _sigmoid(x)


def _dsilu(x):
    s = _sigmoid(x)
    return s * (1.0 + x * (1.0 - s))


def _dot(a, b, dims):
    return lax.dot_general(a, b, (dims, ((), ())), preferred_element_type=F32)


NN = ((1,), (0,))
NT = ((1,), (1,))
TN = ((0,), (0,))


def _matmul(a, b, mode, out_dtype, name, tm=1024, tn=1024, tk=2048, ride=None, halves=None, col_shards=None):
    if halves == "a":
        assert mode == "nt"
        (_, M, Kh), (N, K2) = a.shape, b.shape
        K = 2 * Kh
    elif halves == "b":
        assert mode == "tn"
        (K, M), (_, K2, Nh) = a.shape, b.shape
        N = 2 * Nh
    elif mode == "nn":
        (M, K), (K2, N) = a.shape, b.shape
    elif mode == "nt":
        (M, K), (N, K2) = a.shape, b.shape
    else:
        (K, M), (K2, N) = a.shape, b.shape
    assert K == K2, (name, a.shape, b.shape)
    pick = lambda n, t: _pick(n, t, MXU_TILE) if n % MXU_TILE == 0 else _pick(n, t)
    tm, tn, tk = pick(M, tm), pick(N // 2 if halves == "b" else N, tn), pick(K // 2 if halves == "a" else K, tk)
    if col_shards is not None:
        tn = N // col_shards
    nk = K // tk
    dims = {"nn": NN, "nt": NT, "tn": TN}[mode]

    def body(a_ref, b_ref, o_ref, acc_ref):
        k = pl.program_id(2)

        @pl.when(k == 0)
        def _():
            acc_ref[...] = jnp.zeros_like(acc_ref)

        av = a_ref[0] if halves == "a" else a_ref[...]
        bv = b_ref[0] if halves == "b" else b_ref[...]
        acc_ref[...] += _dot(av.astype(BF16), bv.astype(BF16), dims)

        @pl.when(k == nk - 1)
        def _():
            o_ref[...] = acc_ref[...].astype(out_dtype).reshape(o_ref.shape)

    if halves == "a":
        per = (K // 2) // tk
        a_spec = pl.BlockSpec((1, tm, tk), lambda i, j, k: (k // per, i, k % per))
    elif mode == "tn":
        a_spec = pl.BlockSpec((tk, tm), lambda i, j, k: (k, i))
    else:
        a_spec = pl.BlockSpec((tm, tk), lambda i, j, k: (i, k))
    if halves == "b":
        per = (N // 2) // tn
        b_spec = pl.BlockSpec((1, tk, tn), lambda i, j, k: (j // per, k, j % per))
    elif mode == "nt":
        b_spec = pl.BlockSpec((tn, tk), lambda i, j, k: (j, k))
    else:
        b_spec = pl.BlockSpec((tk, tn), lambda i, j, k: (k, j))
    if col_shards is None:
        out_spec, out_shape = pl.BlockSpec((tm, tn), lambda i, j, k: (i, j)), (M, N)
    else:
        assert tn * col_shards == N, (name, tn, N)
        out_spec, out_shape = pl.BlockSpec((1, tm, tn), lambda i, j, k: (j, i, 0)), (col_shards, M, tn)
    return _pcall(
        body, name=name, grid=(M // tm, N // tn, nk),
        in_specs=[a_spec, b_spec],
        out_specs=out_spec,
        out_shape=jax.ShapeDtypeStruct(out_shape, out_dtype),
        scratch_shapes=[pltpu.VMEM((tm, tn), F32)],
        sem=("parallel", "parallel", "arbitrary"), args=(a, b), ride=ride)


def _modnorm_fwd(xc, xl, g, sc, sh, name, ride=None):
    (L, D), T = xc.shape, xl.shape[0]
    tm = _pick(math.gcd(L, T), 256, 8)
    cb = L // tm

    def body(xc_ref, xl_ref, g_ref, sc_ref, sh_ref, h_ref):
        x = jnp.where(pl.program_id(0) < cb, xc_ref[...], xl_ref[...])
        r = lax.rsqrt(jnp.mean(x * x, axis=-1, keepdims=True) + EPS)
        n = x * r * g_ref[...]
        h_ref[...] = (n * (1.0 + sc_ref[0]) + sh_ref[0]).astype(BF16)

    sel = lambda i: (jnp.where(i < cb, 0, 1), 0, 0)
    return _pcall(
        body, name=name, grid=((L + T) // tm,),
        in_specs=[pl.BlockSpec((tm, D), lambda i: (jnp.minimum(i, cb - 1), 0)),
                  pl.BlockSpec((tm, D), lambda i: (jnp.maximum(i - cb, 0), 0)),
                  pl.BlockSpec((1, D), lambda i: (0, 0)), pl.BlockSpec((1, 1, D), sel), pl.BlockSpec((1, 1, D), sel)],
        out_specs=pl.BlockSpec((tm, D), lambda i: (i, 0)),
        out_shape=jax.ShapeDtypeStruct((L + T, D), BF16),
        sem=("parallel",), args=(xc, xl, g, sc, sh), ride=ride)


def _modnorm_bwd(x, dh, g, sc, resid, name, dh_roff=0):
    N, D = x.shape
    tm = _pick(math.gcd(N, dh_roff), 256, 8)
    ro = dh_roff // tm
    want_dx = resid is not None

    def body(*refs):
        if want_dx:
            x_ref, dh_ref, g_ref, sc_ref, res_ref, dx_ref, acc_ref = refs
        else:
            x_ref, dh_ref, g_ref, sc_ref, acc_ref = refs
        i = pl.program_id(0)

        @pl.when(i == 0)
        def _():
            acc_ref[...] = jnp.zeros_like(acc_ref)

        xv, dhv, gv = x_ref[...], dh_ref[...], g_ref[...]
        r = lax.rsqrt(jnp.mean(xv * xv, axis=-1, keepdims=True) + EPS)
        xh = xv * r
        dn = dhv * (1.0 + sc_ref[...])
        acc_ref[0:1, :] += jnp.sum(dhv, axis=0, keepdims=True)
        acc_ref[1:2, :] += jnp.sum(dhv * xh * gv, axis=0, keepdims=True)
        acc_ref[2:3, :] += jnp.sum(dn * xh, axis=0, keepdims=True)
        if want_dx:
            dxh = dn * gv
            dx_ref[...] = res_ref[...] + r * (dxh - xh * jnp.mean(dxh * xh, axis=-1, keepdims=True))

    row = pl.BlockSpec((tm, D), lambda i: (i, 0))
    drow = pl.BlockSpec((tm, D), lambda i: (i + ro, 0))
    vec = pl.BlockSpec((1, D), lambda i: (0, 0))
    acc = pl.BlockSpec((8, D), lambda i: (0, 0))
    acc_shape = jax.ShapeDtypeStruct((8, D), F32)
    if want_dx:
        return pl.pallas_call(
            body, name=name, grid=(N // tm,), in_specs=[row, drow, vec, vec, row],
            out_specs=[row, acc], out_shape=[jax.ShapeDtypeStruct((N, D), F32), acc_shape],
            compiler_params=_cparams("arbitrary"))(x, dh, g, sc, resid)
    sums = pl.pallas_call(
        body, name=name, grid=(N // tm,), in_specs=[row, drow, vec, vec],
        out_specs=acc, out_shape=acc_shape, compiler_params=_cparams("arbitrary"))(x, dh, g, sc)
    return None, sums


def _resnorm_bwd(x1, dh, g, sc, dy, mix, gt, name):
    N, D = x1.shape
    tm = _pick(N, 256, 8)

    def body(x_ref, dh_ref, g_ref, sc_ref, dy_ref, mix_ref, gt_ref, dx_ref, dm_ref, acc_ref):
        i = pl.program_id(0)

        @pl.when(i == 0)
        def _():
            acc_ref[...] = jnp.zeros_like(acc_ref)

        xv, dhv, gv = x_ref[...], dh_ref[...], g_ref[...]
        r = lax.rsqrt(jnp.mean(xv * xv, axis=-1, keepdims=True) + EPS)
        xh = xv * r
        dn = dhv * (1.0 + sc_ref[...])
        dxh = dn * gv
        dx = dy_ref[...] + r * (dxh - xh * jnp.mean(dxh * xh, axis=-1, keepdims=True))
        dx_ref[...] = dx
        dm_ref[...] = (dx * gt_ref[...]).astype(BF16)
        acc_ref[0:1, :] += jnp.sum(dhv, axis=0, keepdims=True)
        acc_ref[1:2, :] += jnp.sum(dhv * xh * gv, axis=0, keepdims=True)
        acc_ref[2:3, :] += jnp.sum(dn * xh, axis=0, keepdims=True)
        acc_ref[3:4, :] += jnp.sum(dx * mix_ref[...], axis=0, keepdims=True)

    row = pl.BlockSpec((tm, D), lambda i: (i, 0))
    vec = pl.BlockSpec((1, D), lambda i: (0, 0))
    return pl.pallas_call(
        body, name=name, grid=(N // tm,), in_specs=[row, row, vec, vec, row, row, vec],
        out_specs=[row, row, pl.BlockSpec((8, D), lambda i: (0, 0))],
        out_shape=[jax.ShapeDtypeStruct((N, D), F32), jax.ShapeDtypeStruct((N, D), BF16),
                   jax.ShapeDtypeStruct((8, D), F32)],
        compiler_params=_cparams("arbitrary"))(x1, dh, g, sc, dy, mix, gt)


def _qknorm_fwd(z, cblk, nrows, roff, w, cos2, sin2, nh, name):
    W = nh * HEAD_DIM
    tm = _pick(math.gcd(nrows, roff), 256, 8)
    ro = roff // tm
    assert roff % tm == 0

    def body(z_ref, w_ref, c_ref, s_ref, o_ref):
        c, s, wv = c_ref[...], s_ref[...], w_ref[...]
        for h in range(nh):
            x = z_ref[:, h * HEAD_DIM:(h + 1) * HEAD_DIM]
            r = lax.rsqrt(jnp.mean(x * x, axis=-1, keepdims=True) + EPS)
            y = x * r * wv
            o_ref[:, h * HEAD_DIM:(h + 1) * HEAD_DIM] = (y * c + pltpu.roll(y, HEAD_DIM // 2, 1) * s).astype(BF16)

    return pl.pallas_call(
        body, name=name, grid=(nrows // tm,),
        in_specs=[pl.BlockSpec((tm, W), lambda i: (i + ro, cblk)), pl.BlockSpec((1, HEAD_DIM), lambda i: (0, 0)),
                  pl.BlockSpec((tm, HEAD_DIM), lambda i: (i + ro, 0)), pl.BlockSpec((tm, HEAD_DIM), lambda i: (i + ro, 0))],
        out_specs=pl.BlockSpec((tm, W), lambda i: (i, 0)),
        out_shape=jax.ShapeDtypeStruct((nrows, W), BF16),
        compiler_params=_cparams("parallel"),
    )(z, w, cos2, sin2)


def _qknorm_bwd(z, cblk, nrows, roff, w, cos2, sin2, dy, nh, name):
    W = nh * HEAD_DIM
    tm = _pick(math.gcd(nrows, roff), 256, 8)
    ro = roff // tm

    def body(z_ref, w_ref, c_ref, s_ref, dy_ref, dz_ref, acc_ref):
        i = pl.program_id(0)

        @pl.when(i == 0)
        def _():
            acc_ref[...] = jnp.zeros_like(acc_ref)

        c, s, wv = c_ref[...], s_ref[...], w_ref[...]
        dw = jnp.zeros((1, HEAD_DIM), F32)
        for h in range(nh):
            sl = slice(h * HEAD_DIM, (h + 1) * HEAD_DIM)
            x = z_ref[:, sl]
            d = dy_ref[:, sl]
            dyn = d * c + pltpu.roll(d * s, HEAD_DIM // 2, 1)
            r = lax.rsqrt(jnp.mean(x * x, axis=-1, keepdims=True) + EPS)
            xh = x * r
            dw = dw + jnp.sum(dyn * xh, axis=0, keepdims=True)
            dxh = dyn * wv
            dz_ref[:, sl] = (r * (dxh - xh * jnp.mean(dxh * xh, axis=-1, keepdims=True))).astype(BF16)
        acc_ref[0:1, :] += dw

    return pl.pallas_call(
        body, name=name, grid=(nrows // tm,),
        in_specs=[pl.BlockSpec((tm, W), lambda i: (i + ro, cblk)), pl.BlockSpec((1, HEAD_DIM), lambda i: (0, 0)),
                  pl.BlockSpec((tm, HEAD_DIM), lambda i: (i + ro, 0)), pl.BlockSpec((tm, HEAD_DIM), lambda i: (i + ro, 0)),
                  pl.BlockSpec((tm, W), lambda i: (i, 0))],
        out_specs=[pl.BlockSpec((tm, W), lambda i: (i, 0)), pl.BlockSpec((8, HEAD_DIM), lambda i: (0, 0))],
        out_shape=[jax.ShapeDtypeStruct((nrows, W), BF16), jax.ShapeDtypeStruct((8, HEAD_DIM), F32)],
        compiler_params=_cparams("arbitrary"),
    )(z, w, cos2, sin2, dy)


def _cast_seg(z, cblk, width, name):
    R = z.shape[0]
    tm = _pick(R, 512, 8)

    def body(z_ref, o_ref):
        o_ref[...] = z_ref[...].astype(BF16)

    return pl.pallas_call(
        body, name=name, grid=(R // tm,),
        in_specs=[pl.BlockSpec((tm, width), lambda i: (i, cblk))],
        out_specs=pl.BlockSpec((tm, width), lambda i: (i, 0)),
        out_shape=jax.ShapeDtypeStruct((R, width), BF16), compiler_params=_cparams("parallel"))(z)


NEG_BIG = -1e30


KV_PER_STEP = 2
KV_PER_STEP_FWD = 4


def _attn_specs(T, n_ctx, kv_per_step=KV_PER_STEP):
    nb = T // WINDOW
    lb = n_ctx // WINDOW
    kvw = kv_per_step * HEAD_DIM
    blk = lambda f: pl.BlockSpec((WINDOW, kvw), f)
    win = [blk(lambda h, i: (lb + jnp.maximum(i - 1, 0), h)), blk(lambda h, i: (lb + i, h)),
           blk(lambda h, i: (lb + jnp.minimum(i + 1, nb - 1), h))]
    ctx = pl.BlockSpec((n_ctx, kvw), lambda h, i: (0, h))
    qspec = pl.BlockSpec((WINDOW, kv_per_step * Q_PER_KV * HEAD_DIM), lambda h, i: (i, h))
    sink = pl.BlockSpec((N_Q_HEADS, HEAD_DIM), lambda h, i: (0, 0))
    return nb, qspec, win, ctx, sink


def _attn_probs(q, kw, kctx, snk, valid):
    scale = HEAD_DIM ** -0.5
    s_lat = jnp.where(valid, _dot(q, kw, NT) * scale, NEG_BIG)
    s_ctx = _dot(q, kctx, NT) * scale
    m = jnp.maximum(jnp.maximum(jnp.max(s_lat, axis=-1, keepdims=True), jnp.max(s_ctx, axis=-1, keepdims=True)), snk)
    p_lat = jnp.exp(s_lat - m)
    p_ctx = jnp.exp(s_ctx - m)
    p_snk = jnp.exp(snk - m)
    den = p_snk + jnp.sum(p_lat, axis=-1, keepdims=True) + jnp.sum(p_ctx, axis=-1, keepdims=True)
    return p_lat, p_ctx, den, m


def _attn_probs_lse(q, kw, kctx, snk, valid, lse):
    scale = HEAD_DIM ** -0.5
    s_lat = jnp.where(valid, _dot(q, kw, NT) * scale, NEG_BIG)
    s_ctx = _dot(q, kctx, NT) * scale
    return jnp.exp(s_lat - lse), jnp.exp(s_ctx - lse), jnp.exp(snk - lse)


def _attn_valid(i, T, heads):
    rows = heads * WINDOW
    qpos = i * WINDOW + (lax.broadcasted_iota(jnp.int32, (rows, 3 * WINDOW), 0) & (WINDOW - 1))
    kpos = (i - 1) * WINDOW + lax.broadcasted_iota(jnp.int32, (rows, 3 * WINDOW), 1)
    return (jnp.abs(qpos - kpos) <= WINDOW) & (kpos >= 0) & (kpos < T)


def _stack_heads(ref, hh):
    c0 = hh * Q_PER_KV * HEAD_DIM
    return jnp.concatenate([ref[:, c0 + g * HEAD_DIM:c0 + (g + 1) * HEAD_DIM] for g in range(Q_PER_KV)], axis=0)


def _stack_sinks(sink_ref, kvh):
    return jnp.concatenate([jnp.broadcast_to(sink_ref[pl.ds(kvh * Q_PER_KV + g, 1), :][:, 0:1], (WINDOW, 1))
                            for g in range(Q_PER_KV)], axis=0)


def _attn_window(refs, hh):
    return jnp.concatenate([r[:, hh * HEAD_DIM:(hh + 1) * HEAD_DIM] for r in refs], axis=0)


def _attn_fwd(qn, kn, vb, sink_rows, n_ctx, name, ride=None):
    T = qn.shape[0]
    assert KV_PER_STEP_FWD == N_KV_HEADS
    nb, qspec, win, ctx, sink = _attn_specs(T, n_ctx, KV_PER_STEP_FWD)

    def body(q_ref, kp, kc, kx, vp, vc, vx, kctx_ref, vctx_ref, sink_ref, o_ref, lse_ref):
        i = pl.program_id(1)
        valid = _attn_valid(i, T, Q_PER_KV)
        lane = lax.broadcasted_iota(jnp.int32, (WINDOW, HEAD_DIM), 1)
        lse_tile = jnp.zeros((WINDOW, HEAD_DIM), F32)
        for hh in range(KV_PER_STEP_FWD):
            sl = slice(hh * HEAD_DIM, (hh + 1) * HEAD_DIM)
            kw, vw = _attn_window((kp, kc, kx), hh), _attn_window((vp, vc, vx), hh)
            kctx, vctx = kctx_ref[:, sl], vctx_ref[:, sl]
            p_lat, p_ctx, den, m = _attn_probs(_stack_heads(q_ref, hh), kw, kctx, _stack_sinks(sink_ref, hh), valid)
            o = ((_dot(p_lat.astype(BF16), vw, NN) + _dot(p_ctx.astype(BF16), vctx, NN)) / den).astype(BF16)
            lse = m + jnp.log(den)
            for g in range(Q_PER_KV):
                c0 = (hh * Q_PER_KV + g) * HEAD_DIM
                o_ref[:, c0:c0 + HEAD_DIM] = o[g * WINDOW:(g + 1) * WINDOW]
                lse_tile = jnp.where(lane == hh * Q_PER_KV + g, lse[g * WINDOW:(g + 1) * WINDOW], lse_tile)
        lse_ref[...] = lse_tile

    return _pcall(
        body, name=name, grid=(1, nb),
        in_specs=[qspec] + win + win + [ctx, ctx, sink],
        out_specs=[qspec, pl.BlockSpec((WINDOW, HEAD_DIM), lambda h, i: (i, 0))],
        out_shape=[jax.ShapeDtypeStruct(qn.shape, BF16), jax.ShapeDtypeStruct((T, HEAD_DIM), F32)],
        sem=("parallel", "parallel"), args=(qn, kn, kn, kn, vb, vb, vb, kn, vb, sink_rows), ride=ride)


def _attn_bwd(qn, kn, vb, sink_rows, lse, do, n_ctx, name, ride=None):
    T = qn.shape[0]
    nb, qspec, win, ctx, sink = _attn_specs(T, n_ctx)
    scale = HEAD_DIM ** -0.5
    TP = T + 2 * WINDOW

    def body(q_ref, kp, kc, kx, vp, vc, vx, kctx_ref, vctx_ref, sink_ref, do_ref, lse_ref,
             dq_ref, dkw_ref, dvw_ref, dkc_ref, dvc_ref, dsn_ref):
        h, i = pl.program_id(0), pl.program_id(1)

        @pl.when(i == 0)
        def _():
            dkw_ref[...] = jnp.zeros_like(dkw_ref)
            dvw_ref[...] = jnp.zeros_like(dvw_ref)
            dkc_ref[...] = jnp.zeros_like(dkc_ref)
            dvc_ref[...] = jnp.zeros_like(dvc_ref)
            dsn_ref[...] = jnp.zeros_like(dsn_ref)

        lane = lax.broadcasted_iota(jnp.int32, (8, HEAD_DIM), 1)
        lane_q = lax.broadcasted_iota(jnp.int32, (WINDOW, HEAD_DIM), 1)
        lse_tile = lse_ref[...]
        valid = _attn_valid(i, T, Q_PER_KV)
        rows = pl.ds(pl.multiple_of(i * WINDOW, WINDOW), 3 * WINDOW)
        for hh in range(KV_PER_STEP):
            sl = slice(hh * HEAD_DIM, (hh + 1) * HEAD_DIM)
            kw, vw = _attn_window((kp, kc, kx), hh), _attn_window((vp, vc, vx), hh)
            kctx, vctx = kctx_ref[:, sl], vctx_ref[:, sl]
            q, d_o = _stack_heads(q_ref, hh), _stack_heads(do_ref, hh)
            head0 = (h * KV_PER_STEP + hh) * Q_PER_KV
            lse4 = jnp.concatenate([jnp.sum(jnp.where(lane_q == head0 + g, lse_tile, 0.0), axis=1, keepdims=True)
                                    for g in range(Q_PER_KV)], axis=0)
            p_lat, p_ctx, p_snk = _attn_probs_lse(q, kw, kctx, _stack_sinks(sink_ref, h * KV_PER_STEP + hh), valid, lse4)
            dp_lat = _dot(d_o, vw, NT)
            dp_ctx = _dot(d_o, vctx, NT)
            dr = jnp.sum(p_lat * dp_lat, axis=-1, keepdims=True) + jnp.sum(p_ctx * dp_ctx, axis=-1, keepdims=True)
            ds_lat = (p_lat * (dp_lat - dr) * scale).astype(BF16)
            ds_ctx = (p_ctx * (dp_ctx - dr) * scale).astype(BF16)
            dq = _dot(ds_lat, kw, NN) + _dot(ds_ctx, kctx, NN)
            snk_terms = p_snk * dr
            dsn = jnp.zeros((8, HEAD_DIM), F32)
            for g in range(Q_PER_KV):
                c0 = (hh * Q_PER_KV + g) * HEAD_DIM
                dq_ref[:, c0:c0 + HEAD_DIM] = dq[g * WINDOW:(g + 1) * WINDOW]
                dsn = dsn + jnp.where(lane == g, -jnp.sum(snk_terms[g * WINDOW:(g + 1) * WINDOW], axis=0, keepdims=True),
                                      0.0)
            dkw_ref[rows, sl] += _dot(ds_lat, q, TN)
            dvw_ref[rows, sl] += _dot(p_lat.astype(BF16), d_o, TN)
            dkc_ref[:, sl] += _dot(ds_ctx, q, TN)
            dvc_ref[:, sl] += _dot(p_ctx.astype(BF16), d_o, TN)
            dsn_ref[hh] += dsn

    wacc = pl.BlockSpec((TP, KV_PER_STEP * HEAD_DIM), lambda h, i: (0, h))
    return _pcall(
        body, name=name, grid=(N_KV_HEADS // KV_PER_STEP, nb),
        in_specs=[qspec] + win + win + [ctx, ctx, sink, qspec, pl.BlockSpec((WINDOW, HEAD_DIM), lambda h, i: (i, 0))],
        out_specs=[qspec, wacc, wacc, ctx, ctx, pl.BlockSpec((KV_PER_STEP, 8, HEAD_DIM), lambda h, i: (h, 0, 0))],
        out_shape=[jax.ShapeDtypeStruct(qn.shape, F32),
                   jax.ShapeDtypeStruct((TP, N_KV_HEADS * HEAD_DIM), F32),
                   jax.ShapeDtypeStruct((TP, N_KV_HEADS * HEAD_DIM), F32),
                   jax.ShapeDtypeStruct((n_ctx, N_KV_HEADS * HEAD_DIM), F32),
                   jax.ShapeDtypeStruct((n_ctx, N_KV_HEADS * HEAD_DIM), F32),
                   jax.ShapeDtypeStruct((N_KV_HEADS, 8, HEAD_DIM), F32)],
        sem=("arbitrary", "arbitrary"), args=(qn, kn, kn, kn, vb, vb, vb, kn, vb, sink_rows, do, lse), ride=ride)


def _gla_masks(dirv):
    C = GLA_CHUNK

    def times(reps):
        r = lax.broadcasted_iota(jnp.int32, (C, reps * C), 0)
        c = lax.broadcasted_iota(jnp.int32, (C, reps * C), 1) & (C - 1)
        return jnp.where(dirv == 0, r, C - 1 - r), jnp.where(dirv == 0, c, C - 1 - c)

    def level(tt, ss, m):
        sh = m.bit_length() - 1
        same = (tt >> (sh + 1)) == (ss >> (sh + 1))
        return same, (tt >> sh) & 1, (ss >> sh) & 1

    tt, ss = times(3)
    le = (ss <= tt).astype(jnp.int32)
    sums = [le == 1]
    for m in GLA_LEVELS:
        same, ut, us = level(tt, ss, m)
        sums.append(same & (ut == us) & (ut == le))
    tt, ss = times(1)
    blocks = [ss == tt]
    for m in GLA_LEVELS:
        same, ut, us = level(tt, ss, m)
        blocks.append(same & (ut == 1) & (us == 0))
    mall3 = jnp.concatenate([jnp.where(s, 1.0, 0.0) for s in sums], axis=0).astype(BF16)
    return mall3, blocks


def _pieces(x):
    hi = x.astype(BF16)
    r1 = x - hi.astype(F32)
    mid = r1.astype(BF16)
    return hi, mid, (r1 - mid.astype(F32)).astype(BF16)


def _sum_f32(mall3, x):
    return _dot(mall3, jnp.concatenate(_pieces(x), axis=0), NN)


def _sum_f32_t(mall3, x):
    m = mall3[:, 0:GLA_CHUNK]
    hi, mid, lo = _pieces(x)
    return _dot(m, hi, TN) + _dot(m, mid, TN) + _dot(m, lo, TN)


def _gla_chunk_of(dirv, j, lc, nc):
    return jnp.where(dirv == 0, j, jnp.where(j < lc, lc - 1 - j, nc + lc - 1 - j))


def _gla_gate(lr_ref, wg_ref, bg_ref, d=0):
    pre = _dot(lr_ref[...].astype(BF16), wg_ref[d].astype(BF16), NN) + bg_ref[d]
    g = (jnp.minimum(pre, 0.0) - jnp.log(1.0 + jnp.exp(-jnp.abs(pre)))) * (1.0 / GLA_GATE_NORM)
    return pre, g


def _gla_fwd(z, qblk, kblk, vblk, lrblk, wg, bg, DV, n_ctx, name):
    R = z.shape[0]
    C = GLA_CHUNK
    DK = wg.shape[2] // GLA_HEADS
    nc, lc = R // C, n_ctx // C
    qscale = DK ** -0.5

    GK, GV = GLA_HEADS * DK, GLA_HEADS * DV

    def body(qf, kf, vf, lrf, qb, kb, vb, lrb, wg_ref, bg_ref, of_ref, ob_ref, sp_ref, st_ref):
        @pl.when(pl.program_id(0) == 0)
        def _():
            st_ref[...] = jnp.zeros_like(st_ref)

        for d, (q_ref, k_ref, v_ref, lr_ref, o_ref) in enumerate(((qf, kf, vf, lrf, of_ref), (qb, kb, vb, lrb, ob_ref))):
            mall, blocks = _gla_masks(d)
            _, g_all = _gla_gate(lr_ref, wg_ref, bg_ref, d)
            E_all = _sum_f32(mall, g_all)
            for h in range(GLA_HEADS):
                ks, vs = slice(h * DK, (h + 1) * DK), slice(h * DV, (h + 1) * DV)
                q, k, v = q_ref[:, ks] * qscale, k_ref[:, ks], v_ref[:, vs].astype(BF16)
                g, E = g_all[:, ks], E_all[:, ks]
                st = st_ref[d, h]
                sp_ref[d, h, 0] = st
                A = jnp.where(blocks[0], _dot(q.astype(BF16), k.astype(BF16), NT), 0.0)
                for l in range(len(GLA_LEVELS)):
                    e = jnp.exp(E[(1 + l) * C:(2 + l) * C])
                    A = A + jnp.where(blocks[l + 1], _dot((q * e).astype(BF16), (k * e).astype(BF16), NT), 0.0)
                o_ref[:, vs] = (_dot((q * jnp.exp(E[0:C])).astype(BF16), st.astype(BF16), NT)
                                + _dot(A.astype(BF16), v, NN))
                last = jnp.sum(g, axis=0, keepdims=True)
                st_ref[d, h] = jnp.exp(last) * st + _dot(v, (k * jnp.exp(last - E[0:C])).astype(BF16), TN)

    def ins(d):
        chunk = lambda j: _gla_chunk_of(d, j, lc, nc)
        return [pl.BlockSpec((C, GK), lambda j: (chunk(j), qblk)), pl.BlockSpec((C, GK), lambda j: (chunk(j), kblk)),
                pl.BlockSpec((C, GV), lambda j: (chunk(j), vblk)), pl.BlockSpec((C, LANES), lambda j: (chunk(j), lrblk))]

    return pl.pallas_call(
        body, name=name, grid=(nc,),
        in_specs=ins(0) + ins(1) + [pl.BlockSpec((2, LANES, GK), lambda j: (0, 0, 0)),
                                    pl.BlockSpec((2, 1, GK), lambda j: (0, 0, 0))],
        out_specs=[pl.BlockSpec((C, GV), lambda j: (_gla_chunk_of(0, j, lc, nc), 0)),
                   pl.BlockSpec((C, GV), lambda j: (_gla_chunk_of(1, j, lc, nc), 0)),
                   pl.BlockSpec((2, GLA_HEADS, 1, DV, DK), lambda j: (0, 0, j, 0, 0))],
        out_shape=[jax.ShapeDtypeStruct((R, GV), F32), jax.ShapeDtypeStruct((R, GV), F32),
                   jax.ShapeDtypeStruct((2, GLA_HEADS, nc, DV, DK), F32)],
        scratch_shapes=[pltpu.VMEM((2, GLA_HEADS, DV, DK), F32)],
        compiler_params=_cparams("arbitrary"),
    )(z, z, z, z, z, z, z, z, wg, bg)


def _gla_bwd(z, qblk, kblk, vblk, lrblk, wg, bg, sprev, do, n_ctx, name, ride=None):
    R = z.shape[0]
    C = GLA_CHUNK
    DK, DV = wg.shape[2] // GLA_HEADS, do.shape[1] // GLA_HEADS
    nc, lc = R // C, n_ctx // C
    qscale = DK ** -0.5
    nl = len(GLA_LEVELS)

    GK, GV = GLA_HEADS * DK, GLA_HEADS * DV

    def body(qf, kf, vf, lrf, dof, qb_, kb_, vb_, lrb, dob, wg_ref, bg_ref, sp_ref,
             dqf, dkf, dvf, dpf, dqb, dkb, dvb, dpb, dbg_ref, dst_ref):
        @pl.when(pl.program_id(0) == 0)
        def _():
            dst_ref[...] = jnp.zeros_like(dst_ref)
            dbg_ref[...] = jnp.zeros_like(dbg_ref)

        sides = ((qf, kf, vf, lrf, dof, dqf, dkf, dvf, dpf), (qb_, kb_, vb_, lrb, dob, dqb, dkb, dvb, dpb))
        for d, (q_ref, k_ref, v_ref, lr_ref, do_ref, dq_ref, dk_ref, dv_ref, dpre_ref) in enumerate(sides):
            mall, blocks = _gla_masks(d)
            pre_all, g_all = _gla_gate(lr_ref, wg_ref, bg_ref, d)
            E_all = _sum_f32(mall, g_all)
            for h in range(GLA_HEADS):
                ks, vs = slice(h * DK, (h + 1) * DK), slice(h * DV, (h + 1) * DV)
                q, k, v = q_ref[:, ks] * qscale, k_ref[:, ks], v_ref[:, vs].astype(BF16)
                pre, g, E = pre_all[:, ks], g_all[:, ks], E_all[:, ks]
                last = jnp.sum(g, axis=0, keepdims=True)
                eb, er, decay = jnp.exp(E[0:C]), jnp.exp(last - E[0:C]), jnp.exp(last)
                st = sp_ref[d, h, 0]
                dst = dst_ref[d, h]
                d_o = do_ref[:, vs]
                qe, kd = q * eb, k * er
                qb, kb = q.astype(BF16), k.astype(BF16)
                A = jnp.where(blocks[0], _dot(qb, kb, NT), 0.0)
                levels = []
                for l in range(nl):
                    e = jnp.exp(E[(1 + l) * C:(2 + l) * C])
                    ql, kl = q * e, k * e
                    levels.append((e, ql, kl, ql.astype(BF16), kl.astype(BF16)))
                    A = A + jnp.where(blocks[l + 1], _dot(levels[l][3], levels[l][4], NT), 0.0)
                dA = _dot(d_o, v, NT)
                dv_ref[:, vs] = (_dot(A.astype(BF16), d_o, TN)
                                 + _dot(kd.astype(BF16), dst.astype(BF16), NT)).astype(BF16)
                dqe = _dot(d_o, st.astype(BF16), NN)
                dkd = _dot(v, dst.astype(BF16), NN)
                G = jnp.where(blocks[0], dA, 0.0).astype(BF16)
                dq = dqe * eb + _dot(G, kb, NN)
                dk = dkd * er + _dot(G, qb, TN)
                dEr = dkd * kd
                dE = [dqe * qe - dEr]
                for l in range(nl):
                    e, ql, kl, qlb, klb = levels[l]
                    G = jnp.where(blocks[l + 1], dA, 0.0).astype(BF16)
                    dql = _dot(G, klb, NN)
                    dkl = _dot(G, qlb, TN)
                    dq = dq + dql * e
                    dk = dk + dkl * e
                    dE.append(dql * ql + dkl * kl)
                dlast = jnp.sum(dst * st, axis=0, keepdims=True) * decay + jnp.sum(dEr, axis=0, keepdims=True)
                dg = _sum_f32_t(mall, jnp.concatenate(dE, axis=0)) + dlast
                dpre = dg * (1.0 / GLA_GATE_NORM) / (1.0 + jnp.exp(pre))
                dq_ref[:, ks] = (dq * qscale).astype(BF16)
                dk_ref[:, ks] = dk.astype(BF16)
                dpre_ref[:, ks] = dpre.astype(BF16)
                dbg_ref[d, :, ks] += jnp.sum(dpre, axis=0, keepdims=True)
                dst_ref[d, h] = decay * dst + _dot(d_o, qe.astype(BF16), TN)

    def ins(d):
        chunk = lambda j: _gla_chunk_of(d, nc - 1 - j, lc, nc)
        return [pl.BlockSpec((C, GK), lambda j: (chunk(j), qblk)), pl.BlockSpec((C, GK), lambda j: (chunk(j), kblk)),
                pl.BlockSpec((C, GV), lambda j: (chunk(j), vblk)), pl.BlockSpec((C, LANES), lambda j: (chunk(j), lrblk)),
                pl.BlockSpec((C, GV), lambda j: (chunk(j), 0))]

    def outs(d):
        chunk = lambda j: _gla_chunk_of(d, nc - 1 - j, lc, nc)
        return [pl.BlockSpec((C, GK), lambda j: (chunk(j), 0)), pl.BlockSpec((C, GK), lambda j: (chunk(j), 0)),
                pl.BlockSpec((C, GV), lambda j: (chunk(j), 0)), pl.BlockSpec((C, GK), lambda j: (chunk(j), 0))]

    side_shapes = [jax.ShapeDtypeStruct((R, GK), BF16), jax.ShapeDtypeStruct((R, GK), BF16),
                   jax.ShapeDtypeStruct((R, GV), BF16), jax.ShapeDtypeStruct((R, GK), BF16)]
    return _pcall(
        body, name=name, grid=(nc,),
        in_specs=ins(0) + ins(1) + [pl.BlockSpec((2, LANES, GK), lambda j: (0, 0, 0)),
                                    pl.BlockSpec((2, 1, GK), lambda j: (0, 0, 0)),
                                    pl.BlockSpec((2, GLA_HEADS, 1, DV, DK), lambda j: (0, 0, nc - 1 - j, 0, 0))],
        out_specs=outs(0) + outs(1) + [pl.BlockSpec((2, 1, GK), lambda j: (0, 0, 0))],
        out_shape=side_shapes + side_shapes + [jax.ShapeDtypeStruct((2, 1, GK), F32)],
        scratch_shapes=[pltpu.VMEM((2, GLA_HEADS, DV, DK), F32)],
        sem=("arbitrary",), args=(z, z, z, z, do, z, z, z, z, do, wg, bg, sprev), ride=ride)


def _glanorm_fwd(of, ob, z, rbblk, gn, n_ctx, name):
    R, GV = of.shape
    T = R - n_ctx
    DV = GV // GLA_HEADS
    tm = _pick(n_ctx, 256, 8)
    ro = n_ctx // tm

    def body(o0_ref, o1_ref, rb_ref, gn_ref, p_ref):
        gnv = gn_ref[...]
        for h in range(GLA_HEADS):
            sl = slice(h * DV, (h + 1) * DV)
            og = o0_ref[:, sl] + o1_ref[:, sl]
            r = lax.rsqrt(jnp.mean(og * og, axis=-1, keepdims=True) + EPS)
            p_ref[:, sl] = (og * r * gnv * _silu(rb_ref[:, sl])).astype(BF16)

    return pl.pallas_call(
        body, name=name, grid=(T // tm,),
        in_specs=[pl.BlockSpec((tm, GV), lambda i: (i + ro, 0)), pl.BlockSpec((tm, GV), lambda i: (i + ro, 0)),
                  pl.BlockSpec((tm, GV), lambda i: (i + ro, rbblk)), pl.BlockSpec((1, DV), lambda i: (0, 0))],
        out_specs=pl.BlockSpec((tm, GV), lambda i: (i, 0)),
        out_shape=jax.ShapeDtypeStruct((T, GV), BF16), compiler_params=_cparams("parallel"))(of, ob, z, gn)


def _glanorm_bwd(of, ob, z, rbblk, gn, dp, n_ctx, name):
    R, GV = of.shape
    T = R - n_ctx
    DV = GV // GLA_HEADS
    tm = _pick(n_ctx, 256, 8)
    ro = n_ctx // tm

    def body(o0_ref, o1_ref, rb_ref, gn_ref, dp_ref, do_ref, drb_ref, acc_ref):
        i = pl.program_id(0)

        @pl.when(i == 0)
        def _():
            acc_ref[...] = jnp.zeros_like(acc_ref)

        gnv = gn_ref[...]
        dgn = jnp.zeros((1, DV), F32)
        for h in range(GLA_HEADS):
            sl = slice(h * DV, (h + 1) * DV)
            og = o0_ref[:, sl] + o1_ref[:, sl]
            rb = rb_ref[:, sl]
            d = dp_ref[:, sl]
            r = lax.rsqrt(jnp.mean(og * og, axis=-1, keepdims=True) + EPS)
            xh = og * r
            drb_ref[:, sl] = (d * xh * gnv * _dsilu(rb)).astype(BF16)
            dn = d * _silu(rb)
            dgn = dgn + jnp.sum(dn * xh, axis=0, keepdims=True)
            dxh = dn * gnv
            do_ref[:, sl] = (r * (dxh - xh * jnp.mean(dxh * xh, axis=-1, keepdims=True))).astype(BF16)
        acc_ref[0:1, :] += dgn

    row = pl.BlockSpec((tm, GV), lambda i: (i, 0))
    return pl.pallas_call(
        body, name=name, grid=(T // tm,),
        in_specs=[pl.BlockSpec((tm, GV), lambda i: (i + ro, 0)), pl.BlockSpec((tm, GV), lambda i: (i + ro, 0)),
                  pl.BlockSpec((tm, GV), lambda i: (i + ro, rbblk)), pl.BlockSpec((1, DV), lambda i: (0, 0)), row],
        out_specs=[row, row, pl.BlockSpec((8, DV), lambda i: (0, 0))],
        out_shape=[jax.ShapeDtypeStruct((T, GV), BF16), jax.ShapeDtypeStruct((T, GV), BF16),
                   jax.ShapeDtypeStruct((8, DV), F32)],
        compiler_params=_cparams("arbitrary"))(of, ob, z, gn, dp)


def _gate_fwd(z, gablk, gbblk, ya, yg, n_ctx, name):
    T, D = ya.shape
    tm = _pick(n_ctx, 256, 8)
    ro = n_ctx // tm

    def body(ga_ref, gb_ref, ya_ref, yg_ref, m_ref):
        m_ref[...] = (_sigmoid(ga_ref[...]) * ya_ref[...] + _sigmoid(gb_ref[...]) * yg_ref[...]).astype(BF16)

    row = pl.BlockSpec((tm, D), lambda i: (i, 0))
    return pl.pallas_call(
        body, name=name, grid=(T // tm,),
        in_specs=[pl.BlockSpec((tm, D), lambda i: (i + ro, gablk)), pl.BlockSpec((tm, D), lambda i: (i + ro, gbblk)), row, row],
        out_specs=row, out_shape=jax.ShapeDtypeStruct((T, D), BF16), compiler_params=_cparams("parallel"))(z, z, ya, yg)


def _gate_bwd(z, gablk, gbblk, ya, yg, dm, n_ctx, name):
    T, D = ya.shape
    tm = _pick(n_ctx, 256, 8)
    ro = n_ctx // tm

    def body(ga_ref, gb_ref, ya_ref, yg_ref, dm_ref, dya_ref, dyg_ref, dga_ref, dgb_ref):
        d = dm_ref[...]
        sa, sb = _sigmoid(ga_ref[...]), _sigmoid(gb_ref[...])
        dya_ref[...] = (d * sa).astype(BF16)
        dyg_ref[...] = (d * sb).astype(BF16)
        dga_ref[...] = (d * ya_ref[...] * sa * (1.0 - sa)).astype(BF16)
        dgb_ref[...] = (d * yg_ref[...] * sb * (1.0 - sb)).astype(BF16)

    row = pl.BlockSpec((tm, D), lambda i: (i, 0))
    sh = jax.ShapeDtypeStruct((T, D), BF16)
    return pl.pallas_call(
        body, name=name, grid=(T // tm,),
        in_specs=[pl.BlockSpec((tm, D), lambda i: (i + ro, gablk)), pl.BlockSpec((tm, D), lambda i: (i + ro, gbblk)), row, row, row],
        out_specs=[row] * 4, out_shape=[sh] * 4, compiler_params=_cparams("parallel"))(z, z, ya, yg, dm)


def _resnorm_fwd(x, mix, gt, g, sc, sh, name):
    T, D = x.shape
    tm = _pick(T, 256, 8)

    def body(x_ref, mix_ref, gt_ref, g_ref, sc_ref, sh_ref, x1_ref, h_ref):
        x1 = x_ref[...] + gt_ref[...] * mix_ref[...]
        x1_ref[...] = x1
        r = lax.rsqrt(jnp.mean(x1 * x1, axis=-1, keepdims=True) + EPS)
        h_ref[...] = (x1 * r * g_ref[...] * (1.0 + sc_ref[...]) + sh_ref[...]).astype(BF16)

    row = pl.BlockSpec((tm, D), lambda i: (i, 0))
    vec = pl.BlockSpec((1, D), lambda i: (0, 0))
    return pl.pallas_call(
        body, name=name, grid=(T // tm,), in_specs=[row, row, vec, vec, vec, vec], out_specs=[row, row],
        out_shape=[jax.ShapeDtypeStruct((T, D), F32), jax.ShapeDtypeStruct((T, D), BF16)],
        compiler_params=_cparams("parallel"))(x, mix, gt, g, sc, sh)


def _loss_head(d, x1, gt, target, name):
    T, D = d.shape
    tm = _pick(T, 256, 8)

    def body(d_ref, x1_ref, gt_ref, t_ref, dy_ref, dd_ref, acc_ref):
        i = pl.program_id(0)

        @pl.when(i == 0)
        def _():
            acc_ref[...] = jnp.zeros_like(acc_ref)

        dv, gtv = d_ref[...], gt_ref[...]
        e = x1_ref[...] + gtv * dv - t_ref[...]
        dy = e * (1.0 / D)
        dy_ref[...] = dy
        dd_ref[...] = (dy * gtv).astype(BF16)
        acc_ref[0:1, :] += jnp.sum(e * e, axis=0, keepdims=True)
        acc_ref[1:2, :] += jnp.sum(dy * dv, axis=0, keepdims=True)

    row = pl.BlockSpec((tm, D), lambda i: (i, 0))
    return pl.pallas_call(
        body, name=name, grid=(T // tm,), in_specs=[row, row, pl.BlockSpec((1, D), lambda i: (0, 0)), row],
        out_specs=[row, row, pl.BlockSpec((8, D), lambda i: (0, 0))],
        out_shape=[jax.ShapeDtypeStruct((T, D), F32), jax.ShapeDtypeStruct((T, D), BF16),
                   jax.ShapeDtypeStruct((8, D), F32)],
        compiler_params=_cparams("arbitrary"))(d, x1, gt, target)


def _halo_specs(T, tm, tw, col_of, order):
    n8 = tm // 8
    if order == "ij":
        mid = lambda i, j: (i, col_of(j))
        prev = lambda i, j: (jnp.maximum(i * n8 - 1, 0), col_of(j))
        nxt = lambda i, j: (jnp.minimum((i + 1) * n8, T // 8 - 1), col_of(j))
    else:
        mid = lambda j, i: (i, col_of(j))
        prev = lambda j, i: (jnp.maximum(i * n8 - 1, 0), col_of(j))
        nxt = lambda j, i: (jnp.minimum((i + 1) * n8, T // 8 - 1), col_of(j))
    return [pl.BlockSpec((tm, tw), mid), pl.BlockSpec((8, tw), prev), pl.BlockSpec((8, tw), nxt)]


def _shift_rows(x, before, after):
    tm = x.shape[0]
    row = lax.broadcasted_iota(jnp.int32, x.shape, 0)
    return (jnp.where(row == 0, before, pltpu.roll(x, 1, 0)),
            jnp.where(row == tm - 1, after, pltpu.roll(x, tm - 1, 0)))


def _conv_fwd(u, cw, cb, name):
    T, F2 = u.shape
    F = F2 // 2
    tm, tw = _pick(T, 256, 8), _pick(F, 512)
    nt, nw = T // tm, F // tw

    def body(ua, uap, uan, ug, ugp, ugn, cwa, cwg, cba, cbg, f_ref):
        i = pl.program_id(0)
        first, last = i == 0, i == nt - 1

        def conv(u_ref, up_ref, un_ref, w_ref, b_ref):
            m = u_ref[...]
            p, n = _shift_rows(m, jnp.where(first, 0.0, up_ref[7:8, :]), jnp.where(last, 0.0, un_ref[0:1, :]))
            return p * w_ref[0:1, :] + m * w_ref[1:2, :] + n * w_ref[2:3, :] + b_ref[...]

        a = conv(ua, uap, uan, cwa, cba)
        g = conv(ug, ugp, ugn, cwg, cbg)
        f_ref[...] = (_silu(a) * g).astype(BF16)

    wspec = lambda off: pl.BlockSpec((3, tw), lambda i, j: (0, j + off))
    bspec = lambda off: pl.BlockSpec((1, tw), lambda i, j: (0, j + off))
    return pl.pallas_call(
        body, name=name, grid=(nt, nw),
        in_specs=_halo_specs(T, tm, tw, lambda j: j, "ij") + _halo_specs(T, tm, tw, lambda j: j + nw, "ij")
        + [wspec(0), wspec(nw), bspec(0), bspec(nw)],
        out_specs=pl.BlockSpec((tm, tw), lambda i, j: (i, j)),
        out_shape=jax.ShapeDtypeStruct((T, F), BF16),
        compiler_params=_cparams("parallel", "parallel"),
    )(u, u, u, u, u, u, cw, cw, cb, cb)


def _conv_bwd(u, df, cw, cb, name):
    T, F2 = u.shape
    F = F2 // 2
    tm, tw = _pick(T, 256, 8), _pick(F, 512)
    nt, nw = T // tm, F // tw

    def body(ua, uap, uan, ug, ugp, ugn, cwa, cwg, cba, cbg, df_ref, dfp, dfn, du_ref, acca_ref, accg_ref):
        i = pl.program_id(1)

        @pl.when(i == 0)
        def _():
            acca_ref[...] = jnp.zeros_like(acca_ref)
            accg_ref[...] = jnp.zeros_like(accg_ref)

        first, last = i == 0, i == nt - 1
        wa, wg, ba, bg = cwa[...], cwg[...], cba[...], cbg[...]

        def conv(p, m, n, w, b):
            return p * w[0:1] + m * w[1:2] + n * w[2:3] + b

        def grads(a, g, d):
            return d * g * _dsilu(a), d * _silu(a)

        xa, xg, d = ua[...], ug[...], df_ref[...]
        sa = _shift_rows(xa, jnp.where(first, 0.0, uap[7:8, :]), jnp.where(last, 0.0, uan[0:1, :]))
        sg = _shift_rows(xg, jnp.where(first, 0.0, ugp[7:8, :]), jnp.where(last, 0.0, ugn[0:1, :]))
        da, dg = grads(conv(sa[0], xa, sa[1], wa, ba), conv(sg[0], xg, sg[1], wg, bg), d)
        da_p, dg_p = grads(conv(uap[6:7, :], uap[7:8, :], xa[0:1], wa, ba),
                           conv(ugp[6:7, :], ugp[7:8, :], xg[0:1], wg, bg), dfp[7:8, :])
        da_n, dg_n = grads(conv(xa[tm - 1:tm], uan[0:1, :], uan[1:2, :], wa, ba),
                           conv(xg[tm - 1:tm], ugn[0:1, :], ugn[1:2, :], wg, bg), dfn[0:1, :])
        ta = _shift_rows(da, jnp.where(first, 0.0, da_p), jnp.where(last, 0.0, da_n))
        tg = _shift_rows(dg, jnp.where(first, 0.0, dg_p), jnp.where(last, 0.0, dg_n))
        du_ref[0] = (ta[1] * wa[0:1] + da * wa[1:2] + ta[0] * wa[2:3]).astype(BF16)
        du_ref[1] = (tg[1] * wg[0:1] + dg * wg[1:2] + tg[0] * wg[2:3]).astype(BF16)
        for t, (va, vg) in enumerate(((sa[0], sg[0]), (xa, xg), (sa[1], sg[1]))):
            acca_ref[t:t + 1, :] += jnp.sum(da * va, axis=0, keepdims=True)
            accg_ref[t:t + 1, :] += jnp.sum(dg * vg, axis=0, keepdims=True)
        acca_ref[3:4, :] += jnp.sum(da, axis=0, keepdims=True)
        accg_ref[3:4, :] += jnp.sum(dg, axis=0, keepdims=True)

    wspec = lambda off: pl.BlockSpec((3, tw), lambda j, i: (0, j + off))
    bspec = lambda off: pl.BlockSpec((1, tw), lambda j, i: (0, j + off))
    row = pl.BlockSpec((tm, tw), lambda j, i: (i, j))
    acc = pl.BlockSpec((8, tw), lambda j, i: (0, j))
    return pl.pallas_call(
        body, name=name, grid=(nw, nt),
        in_specs=_halo_specs(T, tm, tw, lambda j: j, "ji") + _halo_specs(T, tm, tw, lambda j: j + nw, "ji")
        + [wspec(0), wspec(nw), bspec(0), bspec(nw)] + _halo_specs(T, tm, tw, lambda j: j, "ji"),
        out_specs=[pl.BlockSpec((2, tm, tw), lambda j, i: (0, i, j)), acc, acc],
        out_shape=[jax.ShapeDtypeStruct((2, T, F), BF16),
                   jax.ShapeDtypeStruct((8, F), F32), jax.ShapeDtypeStruct((8, F), F32)],
        compiler_params=_cparams("parallel", "arbitrary"),
    )(u, u, u, u, u, u, cw, cw, cb, cb, df, df, df)


def _assemble_dz(lay, z_used, Z, n_ctx, dqa, drb, dga, dgb, dka, dva, dvg, dqg, dkg, dlr, name):
    T = dqa.shape[0]
    R = T + n_ctx
    tm = _pick(n_ctx, 128, 8)
    cb = n_ctx // tm

    def body(dqa_ref, drb_ref, dga_ref, dgb_ref, dka_ref, dva_ref, dvg0, dvg1, dqg0, dqg1, dkg0, dkg1, dlr_ref, o_ref):
        lat = pl.program_id(0) >= cb

        def put(seg, val):
            o_ref[:, lay[seg]:lay[seg] + val.shape[1]] = val.astype(BF16)

        def lat_only(ref):
            v = ref[...]
            return jnp.where(lat, v, jnp.zeros_like(v))

        put("qa", lat_only(dqa_ref))
        put("rb", lat_only(drb_ref))
        put("ga", lat_only(dga_ref))
        put("gb", lat_only(dgb_ref))
        put("ka", dka_ref[...])
        put("va", dva_ref[...])
        put("vb", dvg0[...].astype(F32) + dvg1[...].astype(F32))
        put("qb", dqg0[...].astype(F32) + dqg1[...].astype(F32))
        put("kb", dkg0[...].astype(F32) + dkg1[...].astype(F32))
        put("lr", dlr_ref[...])
        if Z > z_used:
            o_ref[:, z_used:] = jnp.zeros((tm, Z - z_used), BF16)

    lat_spec = lambda a: pl.BlockSpec((tm, a.shape[1]), lambda i: (jnp.maximum(i - cb, 0), 0))
    all_spec = lambda a: pl.BlockSpec((tm, a.shape[1]), lambda i: (i, 0))
    dir_specs = lambda pair: [all_spec(pair[0]), all_spec(pair[1])]
    return pl.pallas_call(
        body, name=name, grid=(R // tm,),
        in_specs=[lat_spec(dqa), lat_spec(drb), lat_spec(dga), lat_spec(dgb), all_spec(dka), all_spec(dva)]
        + dir_specs(dvg) + dir_specs(dqg) + dir_specs(dkg) + [all_spec(dlr)],
        out_specs=pl.BlockSpec((tm, Z), lambda i: (i, 0)),
        out_shape=jax.ShapeDtypeStruct((R, Z), BF16), compiler_params=_cparams("parallel"),
    )(dqa, drb, dga, dgb, dka, dva, *dvg, *dqg, *dkg, dlr)


def _mod_fwd(ca, w, b, name):
    n, D = ca.shape
    N = w.shape[1]
    tn = _pick(N, 512)

    def body(c_ref, w_ref, b_ref, o_ref, s_ref):
        s = _silu(c_ref[...])
        s_ref[...] = s
        o_ref[...] = _dot(s.astype(BF16), w_ref[...].astype(BF16), NN) + b_ref[...]

    return pl.pallas_call(
        body, name=name, grid=(N // tn,),
        in_specs=[pl.BlockSpec((n, D), lambda j: (0, 0)), pl.BlockSpec((D, tn), lambda j: (0, j)),
                  pl.BlockSpec((1, tn), lambda j: (0, j))],
        out_specs=[pl.BlockSpec((n, tn), lambda j: (0, j)), pl.BlockSpec((n, D), lambda j: (0, 0))],
        out_shape=[jax.ShapeDtypeStruct((n, N), F32), jax.ShapeDtypeStruct((n, D), F32)],
        compiler_params=_cparams("arbitrary"))(ca, w, b)


def _silu_bwd(dsil, ca, name):
    def body(d_ref, c_ref, o_ref):
        o_ref[...] = d_ref[...] * _dsilu(c_ref[...])

    return pl.pallas_call(body, name=name, out_shape=jax.ShapeDtypeStruct(ca.shape, F32))(dsil, ca)


def _adam_math(w, g, m, v):
    c1 = 1.0 - ADAM_B1 ** ADAM_STEP
    c2 = 1.0 - ADAM_B2 ** ADAM_STEP
    mn = ADAM_B1 * m + (1.0 - ADAM_B1) * g
    vn = ADAM_B2 * v + (1.0 - ADAM_B2) * (g * g)
    return -ADAM_LR * ((mn / c1) / (jnp.sqrt(vn / c2) + ADAM_EPS) + ADAM_WD * w), mn, vn


def _adamw(w, g, m, v, name, ride=None):
    Rw, Cw = w.shape
    tr = _pick(Rw, 128, 8)

    def body(w_ref, g_ref, m_ref, v_ref, d_ref, mo_ref, vo_ref):
        d_ref[...], mo_ref[...], vo_ref[...] = _adam_math(w_ref[...], g_ref[...], m_ref[...], v_ref[...])

    row = pl.BlockSpec((tr, Cw), lambda i: (i, 0))
    sh = jax.ShapeDtypeStruct((Rw, Cw), F32)
    return _pcall(body, name=name, grid=(Rw // tr,), in_specs=[row] * 4, out_specs=[row] * 3, out_shape=[sh] * 3,
                  sem=("parallel",), args=(w, g, m, v), ride=ride)


HBM_SPEC = pl.BlockSpec(memory_space=pltpu.HBM)


def _exchange(inputs, out_shapes, stages, name):
    n_in, n_out = len(inputs), len(out_shapes)
    n = sum(len(s) for s in stages)

    def body(*refs):
        ins, outs = refs[:n_in], refs[n_in:n_in + n_out]
        send_sems, recv_sems = refs[n_in + n_out:]
        k = 0
        for stage in stages:
            copies = _stage_copies(stage, ins, outs, send_sems, recv_sems, k)
            for cp in copies:
                cp.start()
            for cp in copies:
                cp.wait()
            k += len(stage)

    return pl.pallas_call(
        body, name=name, in_specs=[HBM_SPEC] * n_in, out_specs=[HBM_SPEC] * n_out, out_shape=out_shapes,
        scratch_shapes=[pltpu.SemaphoreType.DMA((n,)), pltpu.SemaphoreType.DMA((n,))],
    )(*inputs)


def _stage_copies(stage, ins, outs, send_sems, recv_sems, k0=0):
    me = (lax.axis_index("x"), lax.axis_index("y"), lax.axis_index("c"))
    copies = []
    for k, ((skind, sidx), sfn, didx, dfn, flip) in enumerate(stage):
        src = (ins if skind == "in" else outs)[sidx].at[sfn(*me)]
        dst = outs[didx].at[dfn(*me)]
        if flip == (0, 0, 0):
            copies.append(pltpu.make_async_copy(src, dst, send_sems.at[k0 + k]))
        else:
            peer = tuple(1 - a if f else a for a, f in zip(me, flip))
            copies.append(pltpu.make_async_remote_copy(src, dst, send_sems.at[k0 + k], recv_sems.at[k0 + k],
                                                       device_id=peer, device_id_type=MESH))
    return copies


def _pcall(body, *, name, grid, in_specs, out_specs, out_shape, scratch_shapes=(), sem, args, ride=None):
    many = isinstance(out_shape, (list, tuple))
    out_specs, out_shape = (list(out_specs), list(out_shape)) if many else ([out_specs], [out_shape])
    if ride is None:
        res = pl.pallas_call(body, name=name, grid=grid, in_specs=list(in_specs), out_specs=out_specs,
                             out_shape=out_shape, scratch_shapes=list(scratch_shapes),
                             compiler_params=_cparams(*sem))(*args)
        return res if many else res[0]
    x_in, x_out, stage, aliases = ride
    n_in, n_out, n_scr, n_xin, n_xout = len(in_specs), len(out_specs), len(scratch_shapes), len(x_in), len(x_out)

    def wrapped(*refs):
        ins, xins = refs[:n_in], refs[n_in:n_in + n_xin]
        o0 = n_in + n_xin
        outs, xouts = refs[o0:o0 + n_out], refs[o0 + n_out:o0 + n_out + n_xout]
        s0 = o0 + n_out + n_xout
        scr, (send_sems, recv_sems) = refs[s0:s0 + n_scr], refs[s0 + n_scr:]
        first = functools.reduce(jnp.logical_and, [pl.program_id(d) == 0 for d in range(len(grid))])
        last = functools.reduce(jnp.logical_and, [pl.program_id(d) == grid[d] - 1 for d in range(len(grid))])

        @pl.when(first)
        def _():
            for cp in _stage_copies(stage, xins, xouts, send_sems, recv_sems):
                cp.start()

        body(*ins, *outs, *scr)

        @pl.when(last)
        def _():
            for cp in _stage_copies(stage, xins, xouts, send_sems, recv_sems):
                cp.wait()

    res = pl.pallas_call(
        wrapped, name=name, grid=grid, in_specs=list(in_specs) + [HBM_SPEC] * n_xin,
        out_specs=out_specs + [HBM_SPEC] * n_xout, out_shape=out_shape + list(x_out),
        scratch_shapes=list(scratch_shapes) + [pltpu.SemaphoreType.DMA((len(stage),)),
                                               pltpu.SemaphoreType.DMA((len(stage),))],
        input_output_aliases={n_in + a: n_out + b for a, b in aliases.items()},
        compiler_params=_cparams(*(["arbitrary"] * len(grid))))(*args, *x_in)
    main = res[:n_out]
    return (main if many else main[0]), list(res[n_out:])


FLIPS_ALL = [(0, 0, 1), (0, 1, 0), (0, 1, 1), (1, 0, 0), (1, 0, 1), (1, 1, 0), (1, 1, 1)]
FLIPS_CHIP = [(0, 1, 0), (1, 0, 0), (1, 1, 0)]


def _sum_slots(buf, name):
    n, r, w = buf.shape
    tr = _pick(r, 256, 8)

    def body(b_ref, o_ref):
        acc = b_ref[0]
        for s in range(1, n):
            acc = acc + b_ref[s]
        o_ref[...] = acc

    return pl.pallas_call(
        body, name=name, grid=(r // tr,), in_specs=[pl.BlockSpec((n, tr, w), lambda i: (0, i, 0))],
        out_specs=pl.BlockSpec((tr, w), lambda i: (i, 0)), out_shape=jax.ShapeDtypeStruct((r, w), F32),
        compiler_params=_cparams("parallel"))(buf)


def _allreduce_plan(buf):
    whole = lambda x, y, c: (slice(None), slice(None))
    slot = lambda x, y, c: (4 * x + 2 * y + c,)
    stage = [(("in", 0), whole, 0, slot, f) for f in [(0, 0, 0)] + FLIPS_ALL]
    return [jax.ShapeDtypeStruct((8,) + buf.shape, F32)], stage


def _allreduce(buf, name):
    shapes, stage = _allreduce_plan(buf)
    (slots,) = _exchange([buf], shapes, [stage], name + "_x")
    return _sum_slots(slots, name + "_sum")


def _gather_plan(shards, src):
    half = lambda a, c: pl.ds(c * (a.shape[0] // 2), a.shape[0] // 2)
    first, second = [], []
    for n, a in enumerate(shards):
        for f in FLIPS_CHIP:
            first.append((("in", n), lambda x, y, c, a=a: (half(a, c), slice(None)), n,
                          lambda x, y, c, a=a: (2 * x + y, half(a, c), slice(None)), f))
            peer_slot = lambda x, y, c, a=a, f=f: (2 * (x ^ f[0]) + (y ^ f[1]), half(a, c), slice(None))
            second.append(((src, n), peer_slot, n, peer_slot, (0, 0, 1)))
    outs = [jax.ShapeDtypeStruct((4,) + a.shape, a.dtype) for a in shards]
    return first, second, outs


def _allgather_weights(shards, name):
    first, second, outs = _gather_plan(shards, "out")
    return _exchange(shards, outs, [first, second], name)


def _place_own(buf, shard, svec, name):
    _, Rs, Cs = buf.shape
    tr = _pick(Rs, 256, 16)

    def body(s_ref, buf_ref, sh_ref, o_ref):
        o_ref[0] = sh_ref[...]

    grid_spec = pltpu.PrefetchScalarGridSpec(
        num_scalar_prefetch=1, grid=(Rs // tr,),
        in_specs=[pl.BlockSpec(memory_space=pl.ANY), pl.BlockSpec((tr, Cs), lambda i, s: (i, 0))],
        out_specs=pl.BlockSpec((1, tr, Cs), lambda i, s: (s[0], i, 0)))
    return pl.pallas_call(body, name=name, grid_spec=grid_spec, out_shape=jax.ShapeDtypeStruct(buf.shape, buf.dtype),
                          input_output_aliases={1: 0}, compiler_params=_cparams("arbitrary"))(svec, buf, shard)


def _add_pair(G, bufA, cvec, name):
    _, Rs, Cs = G.shape
    Rh = Rs // 2
    tr = _pick(Rh, 128, 16)
    nb = Rh // tr

    def body(c_ref, g_ref, a_ref, o_ref):
        o_ref[...] = (g_ref[...] + a_ref[...]).astype(BF16)

    grid_spec = pltpu.PrefetchScalarGridSpec(
        num_scalar_prefetch=1, grid=(4, nb),
        in_specs=[pl.BlockSpec((1, tr, Cs), lambda s, i, c_ref: (s, c_ref[0] * nb + i, 0)),
                  pl.BlockSpec((1, tr, Cs), lambda s, i, c_ref: (s, i, 0))],
        out_specs=pl.BlockSpec((1, tr, Cs), lambda s, i, c_ref: (s, i, 0)))
    return pl.pallas_call(body, name=name, grid_spec=grid_spec, out_shape=jax.ShapeDtypeStruct((4, Rh, Cs), BF16),
                          compiler_params=_cparams("parallel", "parallel"))(cvec, G, bufA)


def _sum_chips(G, bufA, bufB, cvec, svec, name):
    _, Rs, Cs = G.shape
    Rh = Rs // 2
    tr = _pick(Rh, 128, 16)
    nb = Rh // tr

    def body(c_ref, s_ref, g_ref, a_ref, b_ref, o_ref):
        o_ref[...] = (g_ref[0] + a_ref[0]) + b_ref[0].astype(F32) + b_ref[1].astype(F32) + b_ref[2].astype(F32)

    grid_spec = pltpu.PrefetchScalarGridSpec(
        num_scalar_prefetch=2, grid=(nb,),
        in_specs=[pl.BlockSpec((1, tr, Cs), lambda i, c, s: (s[0], c[0] * nb + i, 0)),
                  pl.BlockSpec((1, tr, Cs), lambda i, c, s: (s[0], i, 0)),
                  pl.BlockSpec((3, tr, Cs), lambda i, c, s: (0, i, 0))],
        out_specs=pl.BlockSpec((tr, Cs), lambda i, c, s: (i, 0)))
    return pl.pallas_call(body, name=name, grid_spec=grid_spec, out_shape=jax.ShapeDtypeStruct((Rh, Cs), F32),
                          compiler_params=_cparams("parallel"))(cvec, svec, G, bufA, bufB)


def _pair_plan(grads):
    Rh = [g.shape[1] // 2 for g in grads]
    whole3 = lambda x, y, c: (slice(None), slice(None), slice(None))
    stage = [(("in", n), lambda x, y, c, n=n: (slice(None), pl.ds((1 - c) * Rh[n], Rh[n]), slice(None)), n,
              whole3, (0, 0, 1)) for n in range(len(grads))]
    return [jax.ShapeDtypeStruct((4, Rh[n], g.shape[2]), F32) for n, g in enumerate(grads)], stage


def _chips_plan(P):
    stage = [(("in", n), lambda x, y, c, f=f: (2 * (x ^ f[0]) + (y ^ f[1]),), n, lambda x, y, c, k=k: (k,), f)
             for n in range(len(P)) for k, f in enumerate(FLIPS_CHIP)]
    return [jax.ShapeDtypeStruct((3,) + p.shape[1:], BF16) for p in P], stage


def _halves_plan(mine):
    whole2 = lambda x, y, c: (slice(None), slice(None))
    stage = [(("in", n), whole2, n, whole2, (0, 0, 1)) for n in range(len(mine))]
    return [jax.ShapeDtypeStruct(r.shape, F32) for r in mine], stage


def _adamw_halves(w, mine, other, m, v, cvec, name):
    Rs, Cs = w.shape
    Rh = Rs // 2
    tr = _pick(Rh, 128, 8)
    nb = Rh // tr

    def body(c_ref, w_ref, a_ref, b_ref, m_ref, v_ref, g_ref, d_ref, mo_ref, vo_ref):
        gv = jnp.where(pl.program_id(0) // nb == c_ref[0], a_ref[...], b_ref[...])
        g_ref[...] = gv
        d_ref[...], mo_ref[...], vo_ref[...] = _adam_math(w_ref[...], gv, m_ref[...], v_ref[...])

    row = pl.BlockSpec((tr, Cs), lambda i, c: (i, 0))
    hrow = pl.BlockSpec((tr, Cs), lambda i, c: (i % nb, 0))
    grid_spec = pltpu.PrefetchScalarGridSpec(num_scalar_prefetch=1, grid=(2 * nb,),
                                             in_specs=[row, hrow, hrow, row, row], out_specs=[row] * 4)
    return pl.pallas_call(body, name=name, grid_spec=grid_spec, out_shape=[jax.ShapeDtypeStruct((Rs, Cs), F32)] * 4,
                          compiler_params=_cparams("parallel"))(cvec, w, mine, other, m, v)


def _pack(arrays):
    flat = [a.reshape(-1).astype(F32) for a in arrays]
    meta, off = [], 0
    for a, f in zip(arrays, flat):
        meta.append((off, a.shape))
        off += f.shape[0]
    total = -(-off // (8 * LANES)) * (8 * LANES)
    flat.append(jnp.zeros((total - off,), F32))
    return jnp.concatenate(flat).reshape(total // LANES, LANES), meta


def _unpack(buf, meta):
    flat = buf.reshape(-1)
    out = []
    for off, shape in meta:
        size = 1
        for s in shape:
            size *= s
        out.append(flat[off:off + size].reshape(shape))
    return out


WEIGHT_NAMES = ["c_ctx", "w_mod", "b_mod", "g_mix", "w_in", "q_norm", "k_norm", "attn_sink", "w_gate_f", "b_gate_f",
                "w_gate_b", "b_gate_b", "gla_norm", "w_attn_o", "w_gla_o", "w_out", "g_ffn", "w_up", "conv_w",
                "conv_b", "w_down"]
BIG_NAMES = ["w_in", "w_attn_o", "w_gla_o", "w_out", "w_up", "w_down"]
SHARDED_SMALL = ["w_gate_f", "w_gate_b", "conv_w"]


def _layouts(D):
    aw, kvw, gk, gv = N_Q_HEADS * HEAD_DIM, N_KV_HEADS * HEAD_DIM, D // 2, D
    widths = {"qa": aw, "ka": kvw, "va": kvw, "qb": gk, "kb": gk, "vb": gv, "rb": gv, "lr": 2 * GLA_LOWRANK,
              "ga": D, "gb": D}
    orig, off = {}, 0
    for s in ["qa", "ka", "va", "qb", "kb", "vb", "rb", "lr", "ga", "gb"]:
        orig[s] = off
        off += widths[s]
    order = ["qa", "vb", "rb", "ga", "gb", "ka", "va", "qb", "kb", "lr"]
    lay, off = {}, 0
    for s in order:
        lay[s] = off
        off += LANES if s == "lr" else widths[s]
    align = {"qa": aw, "vb": D, "rb": D, "ga": D, "gb": D, "ka": kvw, "va": kvw, "qb": gk, "kb": gk,
             "lr": LANES}
    for s in order:
        assert lay[s] % align[s] == 0, (s, lay[s], align[s])
    return widths, orig, order, lay, off, -(-off // (2 * MXU_TILE)) * (2 * MXU_TILE)


def _rope_tables(T, L):
    t = jnp.arange(T)
    nf = HEAD_DIM // 4
    inv = ROPE_THETA ** (-jnp.arange(nf, dtype=F32) / nf)
    ang = jnp.concatenate([(t // GRID_W)[:, None] * inv, (t % GRID_W)[:, None] * inv], axis=-1)
    cos, sin = jnp.cos(ang), jnp.sin(ang)
    cos2 = jnp.concatenate([jnp.ones((L, HEAD_DIM), F32), jnp.concatenate([cos, cos], axis=-1)], axis=0)
    sin2 = jnp.concatenate([jnp.zeros((L, HEAD_DIM), F32), jnp.concatenate([-sin, sin], axis=-1)], axis=0)
    return cos2, sin2


def _step(x, c, ctx, loss_target, W, M, V):
    xi, yi, ci = lax.axis_index("x"), lax.axis_index("y"), lax.axis_index("c")
    chip = 2 * xi + yi
    dev = 2 * chip + ci
    south = (ci == 0).astype(F32)
    cvec = ci.reshape(1).astype(jnp.int32)
    svec = chip.reshape(1).astype(jnp.int32)
    T, D = x.shape[1], x.shape[2]
    L = ctx.shape[1]
    R = L + T
    F = 4 * W["w_down"].shape[1]
    GK, GV = D // 2, D
    DK, DV = GK // GLA_HEADS, GV // GLA_HEADS
    N6 = 6 * D
    N4 = N6 // 4
    widths, orig, order, lay, z_used, Z = _layouts(D)

    def place_cols(shard, full_cols):
        cols = shard.shape[-1]
        full = jnp.zeros(shard.shape[:-1] + (full_cols,), F32)
        return lax.dynamic_update_slice(full, shard * south, (0,) * (shard.ndim - 1) + (chip * cols,))

    c_rows = lax.dynamic_update_slice(jnp.zeros((8, D), F32), c, (dev, 0))
    bufa, meta = _pack([c_rows, place_cols(W["w_gate_f"][0], GK), place_cols(W["w_gate_b"][0], GK),
                        place_cols(W["conv_w"][0], 2 * F)])
    c_all, wgf, wgb, cw = _unpack(_allreduce(bufa, "gather_small"), meta)
    ca = jnp.concatenate([c_all, W["c_ctx"][None, :], jnp.zeros((7, D), F32)], axis=0)
    b_shard = lax.dynamic_slice(W["b_mod"], (0, chip * N4), (1, N4))
    mod_part, sil = _mod_fwd(ca, W["w_mod"][0], b_shard, "mod_fwd")
    slots = lax.dynamic_update_slice(jnp.zeros((4, 16, N4), F32), (mod_part * south)[None], (chip, 0, 0))
    mod_all = _allreduce(slots.reshape(64, N4), "gather_mod").reshape(4, 16, N4).transpose(1, 0, 2).reshape(16, N6)
    mx = lax.dynamic_slice(mod_all, (dev, 0), (1, N6)).reshape(6, 1, D)
    mc = mod_all[8].reshape(6, 1, D)

    sq = lambda a: a.reshape(a.shape[1:])
    shards = [sq(W[n]).astype(BF16) for n in BIG_NAMES]
    own = lambda g, s, n: _place_own(g, s, svec, "place_" + n)
    cols = lambda g: g.transpose(1, 0, 2).reshape(g.shape[1], 4 * g.shape[2])
    rows = lambda g: g.reshape(4 * g.shape[1], g.shape[2])
    w_in_f = cols(own(_allgather_weights(shards[:1], "gather_w_in")[0], shards[0], "w_in"))
    sc1 = jnp.stack([mc[1], mx[1]])
    sh1 = jnp.stack([mc[0], mx[0]])
    h = _modnorm_fwd(ctx[0], x[0], W["g_mix"], sc1, sh1, "modnorm1")
    seg = lambda s: w_in_f[:, orig[s]:orig[s] + widths[s]]
    w_cat = jnp.concatenate([jnp.pad(seg(s), ((0, 0), (0, LANES - widths[s]))) if s == "lr" else seg(s)
                             for s in order] + [jnp.zeros((D, Z - z_used), BF16)], axis=1)
    gather1, gather2, gather_outs = _gather_plan(shards[1:], "in")
    wg = jnp.zeros((2, LANES, GK), F32).at[0, :GLA_LOWRANK].set(wgf).at[1, GLA_LOWRANK:2 * GLA_LOWRANK].set(wgb)
    bg = jnp.stack([W["b_gate_f"], W["b_gate_b"]])
    cb = W["conv_b"]
    sink_rows = jnp.broadcast_to(W["attn_sink"][0][:, None], (N_Q_HEADS, HEAD_DIM))
    cos2, sin2 = _rope_tables(T, L)
    blk = lambda s, w: lay[s] // w

    z, landed = _matmul(h, w_cat, "nn", F32, "proj_in", tn=1536, ride=(shards[1:], gather_outs, gather1, {}))
    qn = _qknorm_fwd(z, blk("qa", widths["qa"]), T, L, W["q_norm"], cos2, sin2, N_Q_HEADS, "qnorm")
    kn = _qknorm_fwd(z, blk("ka", widths["ka"]), R, 0, W["k_norm"], cos2, sin2, N_KV_HEADS, "knorm")
    vb = _cast_seg(z, blk("va", widths["va"]), widths["va"], "vcast")
    (o_attn, lse), landed = _attn_fwd(qn, kn, vb, sink_rows, L, "attn_fwd",
                               ride=(landed, gather_outs, gather2, {n: n for n in range(len(landed))}))
    g_ao, g_go, g_out, g_up, g_dn = [own(g, s, n) for g, s, n in zip(landed, shards[1:], BIG_NAMES[1:])]
    w_ao, w_go, w_out, w_up, w_dn = rows(g_ao), rows(g_go), rows(g_out), cols(g_up), rows(g_dn)
    gla_blks = (blk("qb", GK), blk("kb", GK), blk("vb", GV), blk("lr", LANES))
    o_f, o_b, sprev = _gla_fwd(z, *gla_blks, wg, bg, DV, L, "gla_fwd")
    p = _glanorm_fwd(o_f, o_b, z, blk("rb", D), W["gla_norm"], L, "glanorm")
    ya = _matmul(o_attn, w_ao, "nn", BF16, "proj_attn_o")
    yg = _matmul(p, w_go, "nn", BF16, "proj_gla_o")
    m = _gate_fwd(z, blk("ga", D), blk("gb", D), ya, yg, L, "gate")
    mix = _matmul(m, w_out, "nn", F32, "proj_out")
    x1, h2 = _resnorm_fwd(x[0], mix, mx[2], W["g_ffn"], mx[4], mx[3], "resnorm2")
    u = _matmul(h2, w_up, "nn", F32, "ffn_up")
    f = _conv_fwd(u, cw, cb, "conv_swiglu")
    d = _matmul(f, w_dn, "nn", F32, "ffn_down", tk=2816)
    dy, dd, lacc = _loss_head(d, x1, mx[5], loss_target[0], "loss_head")
    loss = lax.psum((0.5 / D) * jnp.sum(lacc[0]), ("x", "y", "c"))

    gw_dn = _matmul(f, dd, "tn", F32, "ffn_down_dw")
    df = _matmul(dd, w_dn, "nt", F32, "ffn_down_dx")
    du, acca, accg = _conv_bwd(u, df, cw, cb, "conv_swiglu_bwd")
    gw_up = _matmul(h2, du, "tn", F32, "ffn_up_dw", tm=512, halves="b", col_shards=4)
    dh2 = _matmul(du, w_up, "nt", F32, "ffn_up_dx", tk=2816, halves="a")
    dx1, dmix, s2 = _resnorm_bwd(x1, dh2, W["g_ffn"], mx[4], dy, mix, mx[2], "resnorm2_bwd")
    gw_out = _matmul(m, dmix, "tn", F32, "proj_out_dw")
    dm = _matmul(dmix, w_out, "nt", BF16, "proj_out_dx")
    dya, dyg, dga, dgb = _gate_bwd(z, blk("ga", D), blk("gb", D), ya, yg, dm, L, "gate_bwd")
    gw_ao = _matmul(o_attn, dya, "tn", F32, "proj_attn_o_dw")
    do_attn = _matmul(dya, w_ao, "nt", BF16, "proj_attn_o_dx")
    gw_go = _matmul(p, dyg, "tn", F32, "proj_gla_o_dw")
    dp = _matmul(dyg, w_go, "nt", BF16, "proj_gla_o_dx")
    do_gla, drb, s_gn = _glanorm_bwd(o_f, o_b, z, blk("rb", D), W["gla_norm"], dp, L, "glanorm_bwd")
    do_pad = jnp.concatenate([jnp.zeros((L, GV), BF16), do_gla], axis=0)
    by_cols = lambda g: g.reshape(g.shape[0], 4, g.shape[1] // 4).transpose(1, 0, 2)
    by_rows = lambda g: g.reshape(4, g.shape[0] // 4, g.shape[1])
    early = [by_rows(gw_ao), by_rows(gw_go), by_rows(gw_out), gw_up, by_rows(gw_dn)]
    (dq_f, dk_f, dv_f, dpre_f, dq_b, dk_b, dv_b, dpre_b, dbg), pair_e = _gla_bwd(
        z, *gla_blks, wg, bg, sprev, do_pad, L, "gla_bwd", ride=(early, *_pair_plan(early), {}))
    dqg, dkg, dvg = (dq_f, dq_b), (dk_f, dk_b), (dv_f, dv_b)
    sums_e = [_add_pair(g, a, cvec, "reduce_early_add%d" % n) for n, (g, a) in enumerate(zip(early, pair_e))]
    wg_cat = jnp.concatenate([wg[0], wg[1]], axis=1)
    dpre = jnp.stack([dpre_f, dpre_b])
    dlr = _matmul(dpre, wg_cat, "nt", BF16, "gla_gate_dx", halves="a")
    dwg = _matmul(z[:, lay["lr"]:lay["lr"] + LANES], dpre, "tn", F32, "gla_gate_dw", halves="b")
    (dqn, dkw, dvw, dkc, dvc, dsn), chips_e = _attn_bwd(qn, kn, vb, sink_rows, lse, do_attn, L, "attn_bwd",
                                                        ride=(sums_e, *_chips_plan(sums_e), {}))
    mine_e = [_sum_chips(g, a, b, cvec, svec, "reduce_early_sum%d" % n)
              for n, (g, a, b) in enumerate(zip(early, pair_e, chips_e))]
    dqa, s_qn = _qknorm_bwd(z, blk("qa", widths["qa"]), T, L, W["q_norm"], cos2, sin2, dqn, N_Q_HEADS, "qnorm_bwd")
    dk_all = jnp.concatenate([dkc, dkw[WINDOW:WINDOW + T]], axis=0)
    dv_all = jnp.concatenate([dvc, dvw[WINDOW:WINDOW + T]], axis=0)
    dka, s_kn = _qknorm_bwd(z, blk("ka", widths["ka"]), R, 0, W["k_norm"], cos2, sin2, dk_all, N_KV_HEADS, "knorm_bwd")
    dz = _assemble_dz(lay, z_used, Z, L, dqa, drb, dga, dgb, dka, dv_all, dvg, dqg, dkg, dlr, "assemble_dz")
    gw_cat, other_e = _matmul(h, dz, "tn", F32, "proj_in_dw", tn=768, tk=2816,
                              ride=(mine_e, *_halves_plan(mine_e), {}))
    gw_in = jnp.concatenate([gw_cat[:, lay[s]:lay[s] + widths[s]] for s in ["qa", "ka", "va", "qb", "kb", "vb", "rb",
                                                                           "lr", "ga", "gb"]], axis=1)
    late = [by_cols(gw_in)]
    shapes, stage = _pair_plan(late)
    pair_l = _exchange(late, shapes, [stage], "reduce_late_pair")
    sums_l = [_add_pair(late[0], pair_l[0], cvec, "reduce_late_add")]
    dh, chips_l = _matmul(dz, w_cat, "nt", F32, "proj_in_dx", tk=4608, ride=(sums_l, *_chips_plan(sums_l), {}))
    mine_l = [_sum_chips(late[0], pair_l[0], chips_l[0], cvec, svec, "reduce_late_sum")]
    shapes, stage = _halves_plan(mine_l)
    other_l = _exchange(mine_l, shapes, [stage], "reduce_late_halves")
    mine, other = mine_l + mine_e, list(other_l) + other_e
    grad_x, s1 = _modnorm_bwd(x[0], dh, W["g_mix"], mx[1], dx1, "modnorm1_bwd", dh_roff=L)
    _, s1c = _modnorm_bwd(ctx[0], dh, W["g_mix"], mc[1], None, "modnorm1_ctx_bwd")

    dmod_x = jnp.concatenate([s1[0], s1[1], s2[3], s2[0], s2[1], lacc[1]])
    dmod_c = jnp.concatenate([s1c[0], s1c[1], jnp.zeros((4 * D,), F32)])
    dmod_rows = lax.dynamic_update_slice(jnp.zeros((9, N6), F32).at[8].set(dmod_c), dmod_x[None], (dev, 0))
    small = [dmod_rows, dmod_x + dmod_c, s1[2] + s1c[2], s_qn[0], s_kn[0], dsn[:, 0, :Q_PER_KV].reshape(N_Q_HEADS),
             dwg[:GLA_LOWRANK, :GK], dbg[0].reshape(GK), dwg[GLA_LOWRANK:2 * GLA_LOWRANK, GK:], dbg[1].reshape(GK),
             s_gn[0], s2[2], jnp.concatenate([acca[0:3], accg[0:3]], axis=1), jnp.concatenate([acca[3], accg[3]])]
    bufc, meta = _pack(small)
    (dmod_sum, g_b_mod, g_g_mix, g_q_norm, g_k_norm, g_sink, g_wgf, g_bgf, g_wgb, g_bgb, g_gla_norm, g_g_ffn,
     g_conv_w, g_conv_b) = _unpack(_allreduce(bufc, "reduce_small"), meta)
    dmod16 = lax.dynamic_slice(jnp.concatenate([dmod_sum, jnp.zeros((7, N6), F32)], axis=0), (0, chip * N4), (16, N4))
    g_w_mod = _matmul(sil, dmod16, "tn", F32, "mod_dw")
    dsil = _matmul(dmod16, W["w_mod"][0], "nt", F32, "mod_dx")
    g_c_ctx = _silu_bwd(_allreduce(dsil * south, "reduce_cctx"), ca, "silu_bwd")[8]

    cut = lambda g: lax.dynamic_slice(g, (0, chip * (g.shape[1] // 4)), (g.shape[0], g.shape[1] // 4))
    grads = {"c_ctx": g_c_ctx, "w_mod": g_w_mod[None], "b_mod": g_b_mod[None], "g_mix": g_g_mix[None],
             "q_norm": g_q_norm[None], "k_norm": g_k_norm[None], "attn_sink": g_sink[None],
             "w_gate_f": cut(g_wgf)[None], "b_gate_f": g_bgf[None], "w_gate_b": cut(g_wgb)[None],
             "b_gate_b": g_bgb[None], "gla_norm": g_gla_norm[None], "g_ffn": g_g_ffn[None],
             "conv_w": cut(g_conv_w)[None], "conv_b": g_conv_b[None]}

    delta, new_m, new_v = {}, {}, {}
    dl, mn, vn = _adamw(W["w_mod"][0], g_w_mod, M["w_mod"][0], V["w_mod"][0], "adamw_w_mod")
    delta["w_mod"], new_m["w_mod"], new_v["w_mod"] = dl[None], mn[None], vn[None]
    for n, a, b in zip(BIG_NAMES, mine, other):
        g, dl, mn, vn = _adamw_halves(sq(W[n]), a, b, sq(M[n]), sq(V[n]), cvec, "adamw_" + n)
        grads[n], delta[n], new_m[n], new_v[n] = g[None], dl[None], mn[None], vn[None]
    small_names = [n for n in WEIGHT_NAMES if n not in delta]
    packs = [_pack([src[n] for n in small_names]) for src in (W, grads, M, V)]
    meta = packs[0][1]
    outs = _adamw(packs[0][0], packs[1][0], packs[2][0], packs[3][0], "adamw_small")
    for res, o in zip((delta, new_m, new_v), outs):
        for n, a in zip(small_names, _unpack(o, meta)):
            res[n] = a
    return (loss, grad_x[None], *[grads[n] for n in WEIGHT_NAMES], *[delta[n] for n in WEIGHT_NAMES],
            *[new_m[n] for n in WEIGHT_NAMES], *[new_v[n] for n in WEIGHT_NAMES])


def kernel(x, c, ctx, c_ctx, w_mod, b_mod, g_mix, w_in, q_norm, k_norm, attn_sink, w_gate_f, b_gate_f, w_gate_b, b_gate_b, gla_norm, w_attn_o, w_gla_o, w_out, g_ffn, w_up, conv_w, conv_b, w_down, loss_target, m_c_ctx, m_w_mod, m_b_mod, m_g_mix, m_w_in, m_q_norm, m_k_norm, m_attn_sink, m_w_gate_f, m_b_gate_f, m_w_gate_b, m_b_gate_b, m_gla_norm, m_w_attn_o, m_w_gla_o, m_w_out, m_g_ffn, m_w_up, m_conv_w, m_conv_b, m_w_down, v_c_ctx, v_w_mod, v_b_mod, v_g_mix, v_w_in, v_q_norm, v_k_norm, v_attn_sink, v_w_gate_f, v_b_gate_f, v_w_gate_b, v_b_gate_b, v_gla_norm, v_w_attn_o, v_w_gla_o, v_w_out, v_g_ffn, v_w_up, v_conv_w, v_conv_b, v_w_down):
    W = dict(zip(WEIGHT_NAMES, (c_ctx, w_mod, b_mod, g_mix, w_in, q_norm, k_norm, attn_sink, w_gate_f, b_gate_f,
                                w_gate_b, b_gate_b, gla_norm, w_attn_o, w_gla_o, w_out, g_ffn, w_up, conv_w, conv_b,
                                w_down)))
    M = dict(zip(WEIGHT_NAMES, (m_c_ctx, m_w_mod, m_b_mod, m_g_mix, m_w_in, m_q_norm, m_k_norm, m_attn_sink,
                                m_w_gate_f, m_b_gate_f, m_w_gate_b, m_b_gate_b, m_gla_norm, m_w_attn_o, m_w_gla_o,
                                m_w_out, m_g_ffn, m_w_up, m_conv_w, m_conv_b, m_w_down)))
    V = dict(zip(WEIGHT_NAMES, (v_c_ctx, v_w_mod, v_b_mod, v_g_mix, v_w_in, v_q_norm, v_k_norm, v_attn_sink,
                                v_w_gate_f, v_b_gate_f, v_w_gate_b, v_b_gate_b, v_gla_norm, v_w_attn_o, v_w_gla_o,
                                v_w_out, v_g_ffn, v_w_up, v_conv_w, v_conv_b, v_w_down)))
    return _step(x, c, ctx, loss_target, W, M, V)
```

```python
import functools
import math

import jax
import jax.numpy as jnp
from jax import lax
from jax.experimental import pallas as pl
from jax.experimental.pallas import tpu as pltpu

F32 = jnp.float32
BF16 = jnp.bfloat16
MESH = pl.DeviceIdType.MESH

EPS = 1e-6
HEAD_DIM = 128
N_Q_HEADS = 16
N_KV_HEADS = 4
Q_PER_KV = N_Q_HEADS // N_KV_HEADS
WINDOW = 128
GLA_HEADS = 4
GLA_LOWRANK = 16
GLA_GATE_NORM = 16.0
GLA_CHUNK = 64
GRID_W = 64
ROPE_THETA = 10000.0
GLA_LEVELS = (32, 16, 8, 4, 2, 1)
LANES = 128
MXU_TILE = 256

ADAM_LR = 0.001
ADAM_B1 = 0.9
ADAM_B2 = 0.999
ADAM_EPS = 1e-08
ADAM_WD = 0.01
ADAM_STEP = 10

VMEM_LIMIT = 52 * 1024 * 1024


def _cparams(*sem):
    return pltpu.CompilerParams(dimension_semantics=sem, vmem_limit_bytes=VMEM_LIMIT)


def _pick(n, target, mult=LANES):
    best = None
    d = mult
    while d <= min(n, target):
        if n % d == 0:
            best = d
        d += mult
    return n if best is None else best


def _sigmoid(x):
    return 1.0 / (1.0 + jnp.exp(-x))


def _silu(x):
    return x * _sigmoid(x)


def _dsilu(x):
    s = _sigmoid(x)
    return s * (1.0 + x * (1.0 - s))


def _dot(a, b, dims):
    return lax.dot_general(a, b, (dims, ((), ())), preferred_element_type=F32)


NN = ((1,), (0,))
NT = ((1,), (1,))
TN = ((0,), (0,))


def _matmul(a, b, mode, out_dtype, name, tm=1024, tn=1024, tk=2048, ride=None, halves=None, col_shards=None):
    if halves == "a":
        assert mode == "nt"
        (_, M, Kh), (N, K2) = a.shape, b.shape
        K = 2 * Kh
    elif halves == "b":
        assert mode == "tn"
        (K, M), (_, K2, Nh) = a.shape, b.shape
        N = 2 * Nh
    elif mode == "nn":
        (M, K), (K2, N) = a.shape, b.shape
    elif mode == "nt":
        (M, K), (N, K2) = a.shape, b.shape
    else:
        (K, M), (K2, N) = a.shape, b.shape
    assert K == K2, (name, a.shape, b.shape)
    pick = lambda n, t: _pick(n, t, MXU_TILE) if n % MXU_TILE == 0 else _pick(n, t)
    tm, tn, tk = pick(M, tm), pick(N // 2 if halves == "b" else N, tn), pick(K // 2 if halves == "a" else K, tk)
    if col_shards is not None:
        tn = N // col_shards
    nk = K // tk
    dims = {"nn": NN, "nt": NT, "tn": TN}[mode]

    def body(a_ref, b_ref, o_ref, acc_ref):
        k = pl.program_id(2)

        @pl.when(k == 0)
        def _():
            acc_ref[...] = jnp.zeros_like(acc_ref)

        av = a_ref[0] if halves == "a" else a_ref[...]
        bv = b_ref[0] if halves == "b" else b_ref[...]
        acc_ref[...] += _dot(av.astype(BF16), bv.astype(BF16), dims)

        @pl.when(k == nk - 1)
        def _():
            o_ref[...] = acc_ref[...].astype(out_dtype).reshape(o_ref.shape)

    if halves == "a":
        per = (K // 2) // tk
        a_spec = pl.BlockSpec((1, tm, tk), lambda i, j, k: (k // per, i, k % per))
    elif mode == "tn":
        a_spec = pl.BlockSpec((tk, tm), lambda i, j, k: (k, i))
    else:
        a_spec = pl.BlockSpec((tm, tk), lambda i, j, k: (i, k))
    if halves == "b":
        per = (N // 2) // tn
        b_spec = pl.BlockSpec((1, tk, tn), lambda i, j, k: (j // per, k, j % per))
    elif mode == "nt":
        b_spec = pl.BlockSpec((tn, tk), lambda i, j, k: (j, k))
    else:
        b_spec = pl.BlockSpec((tk, tn), lambda i, j, k: (k, j))
    if col_shards is None:
        out_spec, out_shape = pl.BlockSpec((tm, tn), lambda i, j, k: (i, j)), (M, N)
    else:
        assert tn * col_shards == N, (name, tn, N)
        out_spec, out_shape = pl.BlockSpec((1, tm, tn), lambda i, j, k: (j, i, 0)), (col_shards, M, tn)
    return _pcall(
        body, name=name, grid=(M // tm, N // tn, nk),
        in_specs=[a_spec, b_spec],
        out_specs=out_spec,
        out_shape=jax.ShapeDtypeStruct(out_shape, out_dtype),
        scratch_shapes=[pltpu.VMEM((tm, tn), F32)],
        sem=("parallel", "parallel", "arbitrary"), args=(a, b), ride=ride)


def _modnorm_fwd(xc, xl, g, sc, sh, name, ride=None):
    (L, D), T = xc.shape, xl.shape[0]
    tm = _pick(math.gcd(L, T), 256, 8)
    cb = L // tm

    def body(xc_ref, xl_ref, g_ref, sc_ref, sh_ref, h_ref):
        x = jnp.where(pl.program_id(0) < cb, xc_ref[...], xl_ref[...])
        r = lax.rsqrt(jnp.mean(x * x, axis=-1, keepdims=True) + EPS)
        n = x * r * g_ref[...]
        h_ref[...] = (n * (1.0 + sc_ref[0]) + sh_ref[0]).astype(BF16)

    sel = lambda i: (jnp.where(i < cb, 0, 1), 0, 0)
    return _pcall(
        body, name=name, grid=((L + T) // tm,),
        in_specs=[pl.BlockSpec((tm, D), lambda i: (jnp.minimum(i, cb - 1), 0)),
                  pl.BlockSpec((tm, D), lambda i: (jnp.maximum(i - cb, 0), 0)),
                  pl.BlockSpec((1, D), lambda i: (0, 0)), pl.BlockSpec((1, 1, D), sel), pl.BlockSpec((1, 1, D), sel)],
        out_specs=pl.BlockSpec((tm, D), lambda i: (i, 0)),
        out_shape=jax.ShapeDtypeStruct((L + T, D), BF16),
        sem=("parallel",), args=(xc, xl, g, sc, sh), ride=ride)


def _modnorm_bwd(x, dh, g, sc, resid, name, dh_roff=0):
    N, D = x.shape
    tm = _pick(math.gcd(N, dh_roff), 256, 8)
    ro = dh_roff // tm
    want_dx = resid is not None

    def body(*refs):
        if want_dx:
            x_ref, dh_ref, g_ref, sc_ref, res_ref, dx_ref, acc_ref = refs
        else:
            x_ref, dh_ref, g_ref, sc_ref, acc_ref = refs
        i = pl.program_id(0)

        @pl.when(i == 0)
        def _():
            acc_ref[...] = jnp.zeros_like(acc_ref)

        xv, dhv, gv = x_ref[...], dh_ref[...], g_ref[...]
        r = lax.rsqrt(jnp.mean(xv * xv, axis=-1, keepdims=True) + EPS)
        xh = xv * r
        dn = dhv * (1.0 + sc_ref[...])
        acc_ref[0:1, :] += jnp.sum(dhv, axis=0, keepdims=True)
        acc_ref[1:2, :] += jnp.sum(dhv * xh * gv, axis=0, keepdims=True)
        acc_ref[2:3, :] += jnp.sum(dn * xh, axis=0, keepdims=True)
        if want_dx:
            dxh = dn * gv
            dx_ref[...] = res_ref[...] + r * (dxh - xh * jnp.mean(dxh * xh, axis=-1, keepdims=True))

    row = pl.BlockSpec((tm, D), lambda i: (i, 0))
    drow = pl.BlockSpec((tm, D), lambda i: (i + ro, 0))
    vec = pl.BlockSpec((1, D), lambda i: (0, 0))
    acc = pl.BlockSpec((8, D), lambda i: (0, 0))
    acc_shape = jax.ShapeDtypeStruct((8, D), F32)
    if want_dx:
        return pl.pallas_call(
            body, name=name, grid=(N // tm,), in_specs=[row, drow, vec, vec, row],
            out_specs=[row, acc], out_shape=[jax.ShapeDtypeStruct((N, D), F32), acc_shape],
            compiler_params=_cparams("arbitrary"))(x, dh, g, sc, resid)
    sums = pl.pallas_call(
        body, name=name, grid=(N // tm,), in_specs=[row, drow, vec, vec],
        out_specs=acc, out_shape=acc_shape, compiler_params=_cparams("arbitrary"))(x, dh, g, sc)
    return None, sums


def _resnorm_bwd(x1, dh, g, sc, dy, mix, gt, name):
    N, D = x1.shape
    tm = _pick(N, 256, 8)

    def body(x_ref, dh_ref, g_ref, sc_ref, dy_ref, mix_ref, gt_ref, dx_ref, dm_ref, acc_ref):
        i = pl.program_id(0)

        @pl.when(i == 0)
        def _():
            acc_ref[...] = jnp.zeros_like(acc_ref)

        xv, dhv, gv = x_ref[...], dh_ref[...], g_ref[...]
        r = lax.rsqrt(jnp.mean(xv * xv, axis=-1, keepdims=True) + EPS)
        xh = xv * r
        dn = dhv * (1.0 + sc_ref[...])
        dxh = dn * gv
        dx = dy_ref[...] + r * (dxh - xh * jnp.mean(dxh * xh, axis=-1, keepdims=True))
        dx_ref[...] = dx
        dm_ref[...] = (dx * gt_ref[...]).astype(BF16)
        acc_ref[0:1, :] += jnp.sum(dhv, axis=0, keepdims=True)
        acc_ref[1:2, :] += jnp.sum(dhv * xh * gv, axis=0, keepdims=True)
        acc_ref[2:3, :] += jnp.sum(dn * xh, axis=0, keepdims=True)
        acc_ref[3:4, :] += jnp.sum(dx * mix_ref[...], axis=0, keepdims=True)

    row = pl.BlockSpec((tm, D), lambda i: (i, 0))
    vec = pl.BlockSpec((1, D), lambda i: (0, 0))
    return pl.pallas_call(
        body, name=name, grid=(N // tm,), in_specs=[row, row, vec, vec, row, row, vec],
        out_specs=[row, row, pl.BlockSpec((8, D), lambda i: (0, 0))],
        out_shape=[jax.ShapeDtypeStruct((N, D), F32), jax.ShapeDtypeStruct((N, D), BF16),
                   jax.ShapeDtypeStruct((8, D), F32)],
        compiler_params=_cparams("arbitrary"))(x1, dh, g, sc, dy, mix, gt)


def _qknorm_fwd(z, cblk, nrows, roff, w, cos2, sin2, nh, name):
    W = nh * HEAD_DIM
    tm = _pick(math.gcd(nrows, roff), 256, 8)
    ro = roff // tm
    assert roff % tm == 0

    def body(z_ref, w_ref, c_ref, s_ref, o_ref):
        c, s, wv = c_ref[...], s_ref[...], w_ref[...]
        for h in range(nh):
            x = z_ref[:, h * HEAD_DIM:(h + 1) * HEAD_DIM]
            r = lax.rsqrt(jnp.mean(x * x, axis=-1, keepdims=True) + EPS)
            y = x * r * wv
            o_ref[:, h * HEAD_DIM:(h + 1) * HEAD_DIM] = (y * c + pltpu.roll(y, HEAD_DIM // 2, 1) * s).astype(BF16)

    return pl.pallas_call(
        body, name=name, grid=(nrows // tm,),
        in_specs=[pl.BlockSpec((tm, W), lambda i: (i + ro, cblk)), pl.BlockSpec((1, HEAD_DIM), lambda i: (0, 0)),
                  pl.BlockSpec((tm, HEAD_DIM), lambda i: (i + ro, 0)), pl.BlockSpec((tm, HEAD_DIM), lambda i: (i + ro, 0))],
        out_specs=pl.BlockSpec((tm, W), lambda i: (i, 0)),
        out_shape=jax.ShapeDtypeStruct((nrows, W), BF16),
        compiler_params=_cparams("parallel"),
    )(z, w, cos2, sin2)


def _qknorm_bwd(z, cblk, nrows, roff, w, cos2, sin2, dy, nh, name):
    W = nh * HEAD_DIM
    tm = _pick(math.gcd(nrows, roff), 256, 8)
    ro = roff // tm

    def body(z_ref, w_ref, c_ref, s_ref, dy_ref, dz_ref, acc_ref):
        i = pl.program_id(0)

        @pl.when(i == 0)
        def _():
            acc_ref[...] = jnp.zeros_like(acc_ref)

        c, s, wv = c_ref[...], s_ref[...], w_ref[...]
        dw = jnp.zeros((1, HEAD_DIM), F32)
        for h in range(nh):
            sl = slice(h * HEAD_DIM, (h + 1) * HEAD_DIM)
            x = z_ref[:, sl]
            d = dy_ref[:, sl]
            dyn = d * c + pltpu.roll(d * s, HEAD_DIM // 2, 1)
            r = lax.rsqrt(jnp.mean(x * x, axis=-1, keepdims=True) + EPS)
            xh = x * r
            dw = dw + jnp.sum(dyn * xh, axis=0, keepdims=True)
            dxh = dyn * wv
            dz_ref[:, sl] = (r * (dxh - xh * jnp.mean(dxh * xh, axis=-1, keepdims=True))).astype(BF16)
        acc_ref[0:1, :] += dw

    return pl.pallas_call(
        body, name=name, grid=(nrows // tm,),
        in_specs=[pl.BlockSpec((tm, W), lambda i: (i + ro, cblk)), pl.BlockSpec((1, HEAD_DIM), lambda i: (0, 0)),
                  pl.BlockSpec((tm, HEAD_DIM), lambda i: (i + ro, 0)), pl.BlockSpec((tm, HEAD_DIM), lambda i: (i + ro, 0)),
                  pl.BlockSpec((tm, W), lambda i: (i, 0))],
        out_specs=[pl.BlockSpec((tm, W), lambda i: (i, 0)), pl.BlockSpec((8, HEAD_DIM), lambda i: (0, 0))],
        out_shape=[jax.ShapeDtypeStruct((nrows, W), BF16), jax.ShapeDtypeStruct((8, HEAD_DIM), F32)],
        compiler_params=_cparams("arbitrary"),
    )(z, w, cos2, sin2, dy)


def _cast_seg(z, cblk, width, name):
    R = z.shape[0]
    tm = _pick(R, 512, 8)

    def body(z_ref, o_ref):
        o_ref[...] = z_ref[...].astype(BF16)

    return pl.pallas_call(
        body, name=name, grid=(R // tm,),
        in_specs=[pl.BlockSpec((tm, width), lambda i: (i, cblk))],
        out_specs=pl.BlockSpec((tm, width), lambda i: (i, 0)),
        out_shape=jax.ShapeDtypeStruct((R, width), BF16), compiler_params=_cparams("parallel"))(z)


NEG_BIG = -1e30


KV_PER_STEP = 2
KV_PER_STEP_FWD = 4


def _attn_specs(T, n_ctx, kv_per_step=KV_PER_STEP):
    nb = T // WINDOW
    lb = n_ctx // WINDOW
    kvw = kv_per_step * HEAD_DIM
    blk = lambda f: pl.BlockSpec((WINDOW, kvw), f)
    win = [blk(lambda h, i: (lb + jnp.maximum(i - 1, 0), h)), blk(lambda h, i: (lb + i, h)),
           blk(lambda h, i: (lb + jnp.minimum(i + 1, nb - 1), h))]
    ctx = pl.BlockSpec((n_ctx, kvw), lambda h, i: (0, h))
    qspec = pl.BlockSpec((WINDOW, kv_per_step * Q_PER_KV * HEAD_DIM), lambda h, i: (i, h))
    sink = pl.BlockSpec((N_Q_HEADS, HEAD_DIM), lambda h, i: (0, 0))
    return nb, qspec, win, ctx, sink


def _attn_probs(q, kw, kctx, snk, valid):
    scale = HEAD_DIM ** -0.5
    s_lat = jnp.where(valid, _dot(q, kw, NT) * scale, NEG_BIG)
    s_ctx = _dot(q, kctx, NT) * scale
    m = jnp.maximum(jnp.maximum(jnp.max(s_lat, axis=-1, keepdims=True), jnp.max(s_ctx, axis=-1, keepdims=True)), snk)
    p_lat = jnp.exp(s_lat - m)
    p_ctx = jnp.exp(s_ctx - m)
    p_snk = jnp.exp(snk - m)
    den = p_snk + jnp.sum(p_lat, axis=-1, keepdims=True) + jnp.sum(p_ctx, axis=-1, keepdims=True)
    return p_lat, p_ctx, den, m


def _attn_probs_lse(q, kw, kctx, snk, valid, lse):
    scale = HEAD_DIM ** -0.5
    s_lat = jnp.where(valid, _dot(q, kw, NT) * scale, NEG_BIG)
    s_ctx = _dot(q, kctx, NT) * scale
    return jnp.exp(s_lat - lse), jnp.exp(s_ctx - lse), jnp.exp(snk - lse)


def _attn_valid(i, T, heads):
    rows = heads * WINDOW
    qpos = i * WINDOW + (lax.broadcasted_iota(jnp.int32, (rows, 3 * WINDOW), 0) & (WINDOW - 1))
    kpos = (i - 1) * WINDOW + lax.broadcasted_iota(jnp.int32, (rows, 3 * WINDOW), 1)
    return (jnp.abs(qpos - kpos) <= WINDOW) & (kpos >= 0) & (kpos < T)


def _stack_heads(ref, hh):
    c0 = hh * Q_PER_KV * HEAD_DIM
    return jnp.concatenate([ref[:, c0 + g * HEAD_DIM:c0 + (g + 1) * HEAD_DIM] for g in range(Q_PER_KV)], axis=0)


def _stack_sinks(sink_ref, kvh):
    return jnp.concatenate([jnp.broadcast_to(sink_ref[pl.ds(kvh * Q_PER_KV + g, 1), :][:, 0:1], (WINDOW, 1))
                            for g in range(Q_PER_KV)], axis=0)


def _attn_window(refs, hh):
    return jnp.concatenate([r[:, hh * HEAD_DIM:(hh + 1) * HEAD_DIM] for r in refs], axis=0)


def _attn_fwd(qn, kn, vb, sink_rows, n_ctx, name, ride=None):
    T = qn.shape[0]
    assert KV_PER_STEP_FWD == N_KV_HEADS
    nb, qspec, win, ctx, sink = _attn_specs(T, n_ctx, KV_PER_STEP_FWD)

    def body(q_ref, kp, kc, kx, vp, vc, vx, kctx_ref, vctx_ref, sink_ref, o_ref, lse_ref):
        i = pl.program_id(1)
        valid = _attn_valid(i, T, Q_PER_KV)
        for hh in range(KV_PER_STEP_FWD):
            sl = slice(hh * HEAD_DIM, (hh + 1) * HEAD_DIM)
            kw, vw = _attn_window((kp, kc, kx), hh), _attn_window((vp, vc, vx), hh)
            kctx, vctx = kctx_ref[:, sl], vctx_ref[:, sl]
            p_lat, p_ctx, den, m = _attn_probs(_stack_heads(q_ref, hh), kw, kctx, _stack_sinks(sink_ref, hh), valid)
            o = ((_dot(p_lat.astype(BF16), vw, NN) + _dot(p_ctx.astype(BF16), vctx, NN)) / den).astype(BF16)
            lse_ref[0, hh] = m + jnp.log(den)
            for g in range(Q_PER_KV):
                c0 = (hh * Q_PER_KV + g) * HEAD_DIM
                o_ref[:, c0:c0 + HEAD_DIM] = o[g * WINDOW:(g + 1) * WINDOW]

    return _pcall(
        body, name=name, grid=(1, nb),
        in_specs=[qspec] + win + win + [ctx, ctx, sink],
        out_specs=[qspec, pl.BlockSpec((1, N_KV_HEADS, Q_PER_KV * WINDOW, 1), lambda h, i: (i, 0, 0, 0))],
        out_shape=[jax.ShapeDtypeStruct(qn.shape, BF16),
                   jax.ShapeDtypeStruct((nb, N_KV_HEADS, Q_PER_KV * WINDOW, 1), F32)],
        sem=("parallel", "parallel"), args=(qn, kn, kn, kn, vb, vb, vb, kn, vb, sink_rows), ride=ride)


def _attn_bwd(qn, kn, vb, sink_rows, lse, do, n_ctx, name, ride=None):
    T = qn.shape[0]
    nb, qspec, win, ctx, sink = _attn_specs(T, n_ctx)
    scale = HEAD_DIM ** -0.5
    TP = T + 2 * WINDOW

    def body(q_ref, kp, kc, kx, vp, vc, vx, kctx_ref, vctx_ref, sink_ref, do_ref, lse_ref,
             dq_ref, dkw_ref, dvw_ref, dkc_ref, dvc_ref, dsn_ref):
        h, i = pl.program_id(0), pl.program_id(1)

        @pl.when(i == 0)
        def _():
            dkw_ref[...] = jnp.zeros_like(dkw_ref)
            dvw_ref[...] = jnp.zeros_like(dvw_ref)
            dkc_ref[...] = jnp.zeros_like(dkc_ref)
            dvc_ref[...] = jnp.zeros_like(dvc_ref)
            dsn_ref[...] = jnp.zeros_like(dsn_ref)

        lane = lax.broadcasted_iota(jnp.int32, (8, HEAD_DIM), 1)
        valid = _attn_valid(i, T, Q_PER_KV)
        rows = pl.ds(pl.multiple_of(i * WINDOW, WINDOW), 3 * WINDOW)
        for hh in range(KV_PER_STEP):
            sl = slice(hh * HEAD_DIM, (hh + 1) * HEAD_DIM)
            kw, vw = _attn_window((kp, kc, kx), hh), _attn_window((vp, vc, vx), hh)
            kctx, vctx = kctx_ref[:, sl], vctx_ref[:, sl]
            q, d_o = _stack_heads(q_ref, hh), _stack_heads(do_ref, hh)
            p_lat, p_ctx, p_snk = _attn_probs_lse(q, kw, kctx, _stack_sinks(sink_ref, h * KV_PER_STEP + hh), valid,
                                                  lse_ref[0, hh])
            dp_lat = _dot(d_o, vw, NT)
            dp_ctx = _dot(d_o, vctx, NT)
            dr = jnp.sum(p_lat * dp_lat, axis=-1, keepdims=True) + jnp.sum(p_ctx * dp_ctx, axis=-1, keepdims=True)
            ds_lat = (p_lat * (dp_lat - dr) * scale).astype(BF16)
            ds_ctx = (p_ctx * (dp_ctx - dr) * scale).astype(BF16)
            dq = _dot(ds_lat, kw, NN) + _dot(ds_ctx, kctx, NN)
            snk_terms = p_snk * dr
            dsn = jnp.zeros((8, HEAD_DIM), F32)
            for g in range(Q_PER_KV):
                c0 = (hh * Q_PER_KV + g) * HEAD_DIM
                dq_ref[:, c0:c0 + HEAD_DIM] = dq[g * WINDOW:(g + 1) * WINDOW]
                dsn = dsn + jnp.where(lane == g, -jnp.sum(snk_terms[g * WINDOW:(g + 1) * WINDOW], axis=0, keepdims=True),
                                      0.0)
            dkw_ref[rows, sl] += _dot(ds_lat, q, TN)
            dvw_ref[rows, sl] += _dot(p_lat.astype(BF16), d_o, TN)
            dkc_ref[:, sl] += _dot(ds_ctx, q, TN)
            dvc_ref[:, sl] += _dot(p_ctx.astype(BF16), d_o, TN)
            dsn_ref[hh] += dsn

    wacc = pl.BlockSpec((TP, KV_PER_STEP * HEAD_DIM), lambda h, i: (0, h))
    return _pcall(
        body, name=name, grid=(N_KV_HEADS // KV_PER_STEP, nb),
        in_specs=[qspec] + win + win + [ctx, ctx, sink, qspec,
                                        pl.BlockSpec((1, KV_PER_STEP, Q_PER_KV * WINDOW, 1), lambda h, i: (i, h, 0, 0))],
        out_specs=[qspec, wacc, wacc, ctx, ctx, pl.BlockSpec((KV_PER_STEP, 8, HEAD_DIM), lambda h, i: (h, 0, 0))],
        out_shape=[jax.ShapeDtypeStruct(qn.shape, F32),
                   jax.ShapeDtypeStruct((TP, N_KV_HEADS * HEAD_DIM), F32),
                   jax.ShapeDtypeStruct((TP, N_KV_HEADS * HEAD_DIM), F32),
                   jax.ShapeDtypeStruct((n_ctx, N_KV_HEADS * HEAD_DIM), F32),
                   jax.ShapeDtypeStruct((n_ctx, N_KV_HEADS * HEAD_DIM), F32),
                   jax.ShapeDtypeStruct((N_KV_HEADS, 8, HEAD_DIM), F32)],
        sem=("arbitrary", "arbitrary"), args=(qn, kn, kn, kn, vb, vb, vb, kn, vb, sink_rows, do, lse), ride=ride)


def _gla_masks(dirv):
    C = GLA_CHUNK

    def times(reps):
        r = lax.broadcasted_iota(jnp.int32, (C, reps * C), 0)
        c = lax.broadcasted_iota(jnp.int32, (C, reps * C), 1) & (C - 1)
        return jnp.where(dirv == 0, r, C - 1 - r), jnp.where(dirv == 0, c, C - 1 - c)

    def level(tt, ss, m):
        sh = m.bit_length() - 1
        same = (tt >> (sh + 1)) == (ss >> (sh + 1))
        return same, (tt >> sh) & 1, (ss >> sh) & 1

    tt, ss = times(3)
    le = (ss <= tt).astype(jnp.int32)
    sums = [le == 1]
    for m in GLA_LEVELS:
        same, ut, us = level(tt, ss, m)
        sums.append(same & (ut == us) & (ut == le))
    tt, ss = times(1)
    blocks = [ss == tt]
    for m in GLA_LEVELS:
        same, ut, us = level(tt, ss, m)
        blocks.append(same & (ut == 1) & (us == 0))
    mall3 = jnp.concatenate([jnp.where(s, 1.0, 0.0) for s in sums], axis=0).astype(BF16)
    return mall3, blocks


def _pieces(x):
    hi = x.astype(BF16)
    r1 = x - hi.astype(F32)
    mid = r1.astype(BF16)
    return hi, mid, (r1 - mid.astype(F32)).astype(BF16)


def _sum_f32(mall3, x):
    return _dot(mall3, jnp.concatenate(_pieces(x), axis=0), NN)


def _sum_f32_t(mall3, x):
    m = mall3[:, 0:GLA_CHUNK]
    hi, mid, lo = _pieces(x)
    return _dot(m, hi, TN) + _dot(m, mid, TN) + _dot(m, lo, TN)


def _gla_chunk_of(dirv, j, lc, nc):
    return jnp.where(dirv == 0, j, jnp.where(j < lc, lc - 1 - j, nc + lc - 1 - j))


def _gla_gate(lr_ref, wg_ref, bg_ref, d=0):
    pre = _dot(lr_ref[...].astype(BF16), wg_ref[d].astype(BF16), NN) + bg_ref[d]
    g = (jnp.minimum(pre, 0.0) - jnp.log(1.0 + jnp.exp(-jnp.abs(pre)))) * (1.0 / GLA_GATE_NORM)
    return pre, g


def _gla_fwd(z, qblk, kblk, vblk, lrblk, wg, bg, DV, n_ctx, name):
    R = z.shape[0]
    C = GLA_CHUNK
    DK = wg.shape[2] // GLA_HEADS
    nc, lc = R // C, n_ctx // C
    qscale = DK ** -0.5

    GK, GV = GLA_HEADS * DK, GLA_HEADS * DV

    def body(qf, kf, vf, lrf, qb, kb, vb, lrb, wg_ref, bg_ref, of_ref, ob_ref, sp_ref, st_ref):
        @pl.when(pl.program_id(0) == 0)
        def _():
            st_ref[...] = jnp.zeros_like(st_ref)

        for d, (q_ref, k_ref, v_ref, lr_ref, o_ref) in enumerate(((qf, kf, vf, lrf, of_ref), (qb, kb, vb, lrb, ob_ref))):
            mall, blocks = _gla_masks(d)
            _, g_all = _gla_gate(lr_ref, wg_ref, bg_ref, d)
            E_all = _sum_f32(mall, g_all)
            for h in range(GLA_HEADS):
                ks, vs = slice(h * DK, (h + 1) * DK), slice(h * DV, (h + 1) * DV)
                q, k, v = q_ref[:, ks] * qscale, k_ref[:, ks], v_ref[:, vs].astype(BF16)
                g, E = g_all[:, ks], E_all[:, ks]
                st = st_ref[d, h]
                sp_ref[d, h, 0] = st
                A = jnp.where(blocks[0], _dot(q.astype(BF16), k.astype(BF16), NT), 0.0)
                for l in range(len(GLA_LEVELS)):
                    e = jnp.exp(E[(1 + l) * C:(2 + l) * C])
                    A = A + jnp.where(blocks[l + 1], _dot((q * e).astype(BF16), (k * e).astype(BF16), NT), 0.0)
                o_ref[:, vs] = (_dot((q * jnp.exp(E[0:C])).astype(BF16), st.astype(BF16), NT)
                                + _dot(A.astype(BF16), v, NN))
                last = jnp.sum(g, axis=0, keepdims=True)
                st_ref[d, h] = jnp.exp(last) * st + _dot(v, (k * jnp.exp(last - E[0:C])).astype(BF16), TN)

    def ins(d):
        chunk = lambda j: _gla_chunk_of(d, j, lc, nc)
        return [pl.BlockSpec((C, GK), lambda j: (chunk(j), qblk)), pl.BlockSpec((C, GK), lambda j: (chunk(j), kblk)),
                pl.BlockSpec((C, GV), lambda j: (chunk(j), vblk)), pl.BlockSpec((C, LANES), lambda j: (chunk(j), lrblk))]

    return pl.pallas_call(
        body, name=name, grid=(nc,),
        in_specs=ins(0) + ins(1) + [pl.BlockSpec((2, LANES, GK), lambda j: (0, 0, 0)),
                                    pl.BlockSpec((2, 1, GK), lambda j: (0, 0, 0))],
        out_specs=[pl.BlockSpec((C, GV), lambda j: (_gla_chunk_of(0, j, lc, nc), 0)),
                   pl.BlockSpec((C, GV), lambda j: (_gla_chunk_of(1, j, lc, nc), 0)),
                   pl.BlockSpec((2, GLA_HEADS, 1, DV, DK), lambda j: (0, 0, j, 0, 0))],
        out_shape=[jax.ShapeDtypeStruct((R, GV), F32), jax.ShapeDtypeStruct((R, GV), F32),
                   jax.ShapeDtypeStruct((2, GLA_HEADS, nc, DV, DK), F32)],
        scratch_shapes=[pltpu.VMEM((2, GLA_HEADS, DV, DK), F32)],
        compiler_params=_cparams("arbitrary"),
    )(z, z, z, z, z, z, z, z, wg, bg)


def _gla_bwd(z, qblk, kblk, vblk, lrblk, wg, bg, sprev, do, n_ctx, name, ride=None):
    R = z.shape[0]
    C = GLA_CHUNK
    DK, DV = wg.shape[2] // GLA_HEADS, do.shape[1] // GLA_HEADS
    nc, lc = R // C, n_ctx // C
    qscale = DK ** -0.5
    nl = len(GLA_LEVELS)

    GK, GV = GLA_HEADS * DK, GLA_HEADS * DV

    def body(qf, kf, vf, lrf, dof, qb_, kb_, vb_, lrb, dob, wg_ref, bg_ref, sp_ref,
             dqf, dkf, dvf, dpf, dqb, dkb, dvb, dpb, dbg_ref, dst_ref):
        @pl.when(pl.program_id(0) == 0)
        def _():
            dst_ref[...] = jnp.zeros_like(dst_ref)
            dbg_ref[...] = jnp.zeros_like(dbg_ref)

        sides = ((qf, kf, vf, lrf, dof, dqf, dkf, dvf, dpf), (qb_, kb_, vb_, lrb, dob, dqb, dkb, dvb, dpb))
        for d, (q_ref, k_ref, v_ref, lr_ref, do_ref, dq_ref, dk_ref, dv_ref, dpre_ref) in enumerate(sides):
            mall, blocks = _gla_masks(d)
            pre_all, g_all = _gla_gate(lr_ref, wg_ref, bg_ref, d)
            E_all = _sum_f32(mall, g_all)
            for h in range(GLA_HEADS):
                ks, vs = slice(h * DK, (h + 1) * DK), slice(h * DV, (h + 1) * DV)
                q, k, v = q_ref[:, ks] * qscale, k_ref[:, ks], v_ref[:, vs].astype(BF16)
                pre, g, E = pre_all[:, ks], g_all[:, ks], E_all[:, ks]
                last = jnp.sum(g, axis=0, keepdims=True)
                eb, er, decay = jnp.exp(E[0:C]), jnp.exp(last - E[0:C]), jnp.exp(last)
                st = sp_ref[d, h, 0]
                dst = dst_ref[d, h]
                d_o = do_ref[:, vs]
                qe, kd = q * eb, k * er
                qb, kb = q.astype(BF16), k.astype(BF16)
                A = jnp.where(blocks[0], _dot(qb, kb, NT), 0.0)
                levels = []
                for l in range(nl):
                    e = jnp.exp(E[(1 + l) * C:(2 + l) * C])
                    ql, kl = q * e, k * e
                    levels.append((e, ql, kl, ql.astype(BF16), kl.astype(BF16)))
                    A = A + jnp.where(blocks[l + 1], _dot(levels[l][3], levels[l][4], NT), 0.0)
                dA = _dot(d_o, v, NT)
                dv_ref[:, vs] = (_dot(A.astype(BF16), d_o, TN)
                                 + _dot(kd.astype(BF16), dst.astype(BF16), NT)).astype(BF16)
                dqe = _dot(d_o, st.astype(BF16), NN)
                dkd = _dot(v, dst.astype(BF16), NN)
                G = jnp.where(blocks[0], dA, 0.0).astype(BF16)
                dq = dqe * eb + _dot(G, kb, NN)
                dk = dkd * er + _dot(G, qb, TN)
                dEr = dkd * kd
                dE = [dqe * qe - dEr]
                for l in range(nl):
                    e, ql, kl, qlb, klb = levels[l]
                    G = jnp.where(blocks[l + 1], dA, 0.0).astype(BF16)
                    dql = _dot(G, klb, NN)
                    dkl = _dot(G, qlb, TN)
                    dq = dq + dql * e
                    dk = dk + dkl * e
                    dE.append(dql * ql + dkl * kl)
                dlast = jnp.sum(dst * st, axis=0, keepdims=True) * decay + jnp.sum(dEr, axis=0, keepdims=True)
                dg = _sum_f32_t(mall, jnp.concatenate(dE, axis=0)) + dlast
                dpre = dg * (1.0 / GLA_GATE_NORM) / (1.0 + jnp.exp(pre))
                dq_ref[:, ks] = (dq * qscale).astype(BF16)
                dk_ref[:, ks] = dk.astype(BF16)
                dpre_ref[:, ks] = dpre.astype(BF16)
                dbg_ref[d, :, ks] += jnp.sum(dpre, axis=0, keepdims=True)
                dst_ref[d, h] = decay * dst + _dot(d_o, qe.astype(BF16), TN)

    def ins(d):
        chunk = lambda j: _gla_chunk_of(d, nc - 1 - j, lc, nc)
        return [pl.BlockSpec((C, GK), lambda j: (chunk(j), qblk)), pl.BlockSpec((C, GK), lambda j: (chunk(j), kblk)),
                pl.BlockSpec((C, GV), lambda j: (chunk(j), vblk)), pl.BlockSpec((C, LANES), lambda j: (chunk(j), lrblk)),
                pl.BlockSpec((C, GV), lambda j: (chunk(j), 0))]

    def outs(d):
        chunk = lambda j: _gla_chunk_of(d, nc - 1 - j, lc, nc)
        return [pl.BlockSpec((C, GK), lambda j: (chunk(j), 0)), pl.BlockSpec((C, GK), lambda j: (chunk(j), 0)),
                pl.BlockSpec((C, GV), lambda j: (chunk(j), 0)), pl.BlockSpec((C, GK), lambda j: (chunk(j), 0))]

    side_shapes = [jax.ShapeDtypeStruct((R, GK), BF16), jax.ShapeDtypeStruct((R, GK), BF16),
                   jax.ShapeDtypeStruct((R, GV), BF16), jax.ShapeDtypeStruct((R, GK), BF16)]
    return _pcall(
        body, name=name, grid=(nc,),
        in_specs=ins(0) + ins(1) + [pl.BlockSpec((2, LANES, GK), lambda j: (0, 0, 0)),
                                    pl.BlockSpec((2, 1, GK), lambda j: (0, 0, 0)),
                                    pl.BlockSpec((2, GLA_HEADS, 1, DV, DK), lambda j: (0, 0, nc - 1 - j, 0, 0))],
        out_specs=outs(0) + outs(1) + [pl.BlockSpec((2, 1, GK), lambda j: (0, 0, 0))],
        out_shape=side_shapes + side_shapes + [jax.ShapeDtypeStruct((2, 1, GK), F32)],
        scratch_shapes=[pltpu.VMEM((2, GLA_HEADS, DV, DK), F32)],
        sem=("arbitrary",), args=(z, z, z, z, do, z, z, z, z, do, wg, bg, sprev), ride=ride)


def _glanorm_fwd(of, ob, z, rbblk, gn, n_ctx, name):
    R, GV = of.shape
    T = R - n_ctx
    DV = GV // GLA_HEADS
    tm = _pick(n_ctx, 256, 8)
    ro = n_ctx // tm

    def body(o0_ref, o1_ref, rb_ref, gn_ref, p_ref):
        gnv = gn_ref[...]
        for h in range(GLA_HEADS):
            sl = slice(h * DV, (h + 1) * DV)
            og = o0_ref[:, sl] + o1_ref[:, sl]
            r = lax.rsqrt(jnp.mean(og * og, axis=-1, keepdims=True) + EPS)
            p_ref[:, sl] = (og * r * gnv * _silu(rb_ref[:, sl])).astype(BF16)

    return pl.pallas_call(
        body, name=name, grid=(T // tm,),
        in_specs=[pl.BlockSpec((tm, GV), lambda i: (i + ro, 0)), pl.BlockSpec((tm, GV), lambda i: (i + ro, 0)),
                  pl.BlockSpec((tm, GV), lambda i: (i + ro, rbblk)), pl.BlockSpec((1, DV), lambda i: (0, 0))],
        out_specs=pl.BlockSpec((tm, GV), lambda i: (i, 0)),
        out_shape=jax.ShapeDtypeStruct((T, GV), BF16), compiler_params=_cparams("parallel"))(of, ob, z, gn)


def _glanorm_bwd(of, ob, z, rbblk, gn, dp, n_ctx, name):
    R, GV = of.shape
    T = R - n_ctx
    DV = GV // GLA_HEADS
    tm = _pick(n_ctx, 256, 8)
    ro = n_ctx // tm

    def body(o0_ref, o1_ref, rb_ref, gn_ref, dp_ref, do_ref, drb_ref, acc_ref):
        i = pl.program_id(0)

        @pl.when(i == 0)
        def _():
            acc_ref[...] = jnp.zeros_like(acc_ref)

        gnv = gn_ref[...]
        dgn = jnp.zeros((1, DV), F32)
        for h in range(GLA_HEADS):
            sl = slice(h * DV, (h + 1) * DV)
            og = o0_ref[:, sl] + o1_ref[:, sl]
            rb = rb_ref[:, sl]
            d = dp_ref[:, sl]
            r = lax.rsqrt(jnp.mean(og * og, axis=-1, keepdims=True) + EPS)
            xh = og * r
            drb_ref[:, sl] = (d * xh * gnv * _dsilu(rb)).astype(BF16)
            dn = d * _silu(rb)
            dgn = dgn + jnp.sum(dn * xh, axis=0, keepdims=True)
            dxh = dn * gnv
            do_ref[:, sl] = (r * (dxh - xh * jnp.mean(dxh * xh, axis=-1, keepdims=True))).astype(BF16)
        acc_ref[0:1, :] += dgn

    row = pl.BlockSpec((tm, GV), lambda i: (i, 0))
    return pl.pallas_call(
        body, name=name, grid=(T // tm,),
        in_specs=[pl.BlockSpec((tm, GV), lambda i: (i + ro, 0)), pl.BlockSpec((tm, GV), lambda i: (i + ro, 0)),
                  pl.BlockSpec((tm, GV), lambda i: (i + ro, rbblk)), pl.BlockSpec((1, DV), lambda i: (0, 0)), row],
        out_specs=[row, row, pl.BlockSpec((8, DV), lambda i: (0, 0))],
        out_shape=[jax.ShapeDtypeStruct((T, GV), BF16), jax.ShapeDtypeStruct((T, GV), BF16),
                   jax.ShapeDtypeStruct((8, DV), F32)],
        compiler_params=_cparams("arbitrary"))(of, ob, z, gn, dp)


def _gate_fwd(z, gablk, gbblk, ya, yg, n_ctx, name):
    T, D = ya.shape
    tm = _pick(n_ctx, 256, 8)
    ro = n_ctx // tm

    def body(ga_ref, gb_ref, ya_ref, yg_ref, m_ref):
        m_ref[...] = (_sigmoid(ga_ref[...]) * ya_ref[...] + _sigmoid(gb_ref[...]) * yg_ref[...]).astype(BF16)

    row = pl.BlockSpec((tm, D), lambda i: (i, 0))
    return pl.pallas_call(
        body, name=name, grid=(T // tm,),
        in_specs=[pl.BlockSpec((tm, D), lambda i: (i + ro, gablk)), pl.BlockSpec((tm, D), lambda i: (i + ro, gbblk)), row, row],
        out_specs=row, out_shape=jax.ShapeDtypeStruct((T, D), BF16), compiler_params=_cparams("parallel"))(z, z, ya, yg)


def _gate_bwd(z, gablk, gbblk, ya, yg, dm, n_ctx, name):
    T, D = ya.shape
    tm = _pick(n_ctx, 256, 8)
    ro = n_ctx // tm

    def body(ga_ref, gb_ref, ya_ref, yg_ref, dm_ref, dya_ref, dyg_ref, dga_ref, dgb_ref):
        d = dm_ref[...]
        sa, sb = _sigmoid(ga_ref[...]), _sigmoid(gb_ref[...])
        dya_ref[...] = (d * sa).astype(BF16)
        dyg_ref[...] = (d * sb).astype(BF16)
        dga_ref[...] = (d * ya_ref[...] * sa * (1.0 - sa)).astype(BF16)
        dgb_ref[...] = (d * yg_ref[...] * sb * (1.0 - sb)).astype(BF16)

    row = pl.BlockSpec((tm, D), lambda i: (i, 0))
    sh = jax.ShapeDtypeStruct((T, D), BF16)
    return pl.pallas_call(
        body, name=name, grid=(T // tm,),
        in_specs=[pl.BlockSpec((tm, D), lambda i: (i + ro, gablk)), pl.BlockSpec((tm, D), lambda i: (i + ro, gbblk)), row, row, row],
        out_specs=[row] * 4, out_shape=[sh] * 4, compiler_params=_cparams("parallel"))(z, z, ya, yg, dm)


def _resnorm_fwd(x, mix, gt, g, sc, sh, name):
    T, D = x.shape
    tm = _pick(T, 256, 8)

    def body(x_ref, mix_ref, gt_ref, g_ref, sc_ref, sh_ref, x1_ref, h_ref):
        x1 = x_ref[...] + gt_ref[...] * mix_ref[...]
        x1_ref[...] = x1
        r = lax.rsqrt(jnp.mean(x1 * x1, axis=-1, keepdims=True) + EPS)
        h_ref[...] = (x1 * r * g_ref[...] * (1.0 + sc_ref[...]) + sh_ref[...]).astype(BF16)

    row = pl.BlockSpec((tm, D), lambda i: (i, 0))
    vec = pl.BlockSpec((1, D), lambda i: (0, 0))
    return pl.pallas_call(
        body, name=name, grid=(T // tm,), in_specs=[row, row, vec, vec, vec, vec], out_specs=[row, row],
        out_shape=[jax.ShapeDtypeStruct((T, D), F32), jax.ShapeDtypeStruct((T, D), BF16)],
        compiler_params=_cparams("parallel"))(x, mix, gt, g, sc, sh)


def _loss_head(d, x1, gt, target, name):
    T, D = d.shape
    tm = _pick(T, 256, 8)

    def body(d_ref, x1_ref, gt_ref, t_ref, dy_ref, dd_ref, acc_ref):
        i = pl.program_id(0)

        @pl.when(i == 0)
        def _():
            acc_ref[...] = jnp.zeros_like(acc_ref)

        dv, gtv = d_ref[...], gt_ref[...]
        e = x1_ref[...] + gtv * dv - t_ref[...]
        dy = e * (1.0 / D)
        dy_ref[...] = dy
        dd_ref[...] = (dy * gtv).astype(BF16)
        acc_ref[0:1, :] += jnp.sum(e * e, axis=0, keepdims=True)
        acc_ref[1:2, :] += jnp.sum(dy * dv, axis=0, keepdims=True)

    row = pl.BlockSpec((tm, D), lambda i: (i, 0))
    return pl.pallas_call(
        body, name=name, grid=(T // tm,), in_specs=[row, row, pl.BlockSpec((1, D), lambda i: (0, 0)), row],
        out_specs=[row, row, pl.BlockSpec((8, D), lambda i: (0, 0))],
        out_shape=[jax.ShapeDtypeStruct((T, D), F32), jax.ShapeDtypeStruct((T, D), BF16),
                   jax.ShapeDtypeStruct((8, D), F32)],
        compiler_params=_cparams("arbitrary"))(d, x1, gt, target)


def _halo_specs(T, tm, tw, col_of, order):
    n8 = tm // 8
    if order == "ij":
        mid = lambda i, j: (i, col_of(j))
        prev = lambda i, j: (jnp.maximum(i * n8 - 1, 0), col_of(j))
        nxt = lambda i, j: (jnp.minimum((i + 1) * n8, T // 8 - 1), col_of(j))
    else:
        mid = lambda j, i: (i, col_of(j))
        prev = lambda j, i: (jnp.maximum(i * n8 - 1, 0), col_of(j))
        nxt = lambda j, i: (jnp.minimum((i + 1) * n8, T // 8 - 1), col_of(j))
    return [pl.BlockSpec((tm, tw), mid), pl.BlockSpec((8, tw), prev), pl.BlockSpec((8, tw), nxt)]


def _shift_rows(x, before, after):
    tm = x.shape[0]
    row = lax.broadcasted_iota(jnp.int32, x.shape, 0)
    return (jnp.where(row == 0, before, pltpu.roll(x, 1, 0)),
            jnp.where(row == tm - 1, after, pltpu.roll(x, tm - 1, 0)))


def _conv_fwd(u, cw, cb, name):
    T, F2 = u.shape
    F = F2 // 2
    tm, tw = _pick(T, 256, 8), _pick(F, 512)
    nt, nw = T // tm, F // tw

    def body(ua, uap, uan, ug, ugp, ugn, cwa, cwg, cba, cbg, f_ref):
        i = pl.program_id(0)
        first, last = i == 0, i == nt - 1

        def conv(u_ref, up_ref, un_ref, w_ref, b_ref):
            m = u_ref[...]
            p, n = _shift_rows(m, jnp.where(first, 0.0, up_ref[7:8, :]), jnp.where(last, 0.0, un_ref[0:1, :]))
            return p * w_ref[0:1, :] + m * w_ref[1:2, :] + n * w_ref[2:3, :] + b_ref[...]

        a = conv(ua, uap, uan, cwa, cba)
        g = conv(ug, ugp, ugn, cwg, cbg)
        f_ref[...] = (_silu(a) * g).astype(BF16)

    wspec = lambda off: pl.BlockSpec((3, tw), lambda i, j: (0, j + off))
    bspec = lambda off: pl.BlockSpec((1, tw), lambda i, j: (0, j + off))
    return pl.pallas_call(
        body, name=name, grid=(nt, nw),
        in_specs=_halo_specs(T, tm, tw, lambda j: j, "ij") + _halo_specs(T, tm, tw, lambda j: j + nw, "ij")
        + [wspec(0), wspec(nw), bspec(0), bspec(nw)],
        out_specs=pl.BlockSpec((tm, tw), lambda i, j: (i, j)),
        out_shape=jax.ShapeDtypeStruct((T, F), BF16),
        compiler_params=_cparams("parallel", "parallel"),
    )(u, u, u, u, u, u, cw, cw, cb, cb)


def _conv_bwd(u, df, cw, cb, name):
    T, F2 = u.shape
    F = F2 // 2
    tm, tw = _pick(T, 256, 8), _pick(F, 512)
    nt, nw = T // tm, F // tw

    def body(ua, uap, uan, ug, ugp, ugn, cwa, cwg, cba, cbg, df_ref, dfp, dfn, du_ref, acca_ref, accg_ref):
        i = pl.program_id(1)

        @pl.when(i == 0)
        def _():
            acca_ref[...] = jnp.zeros_like(acca_ref)
            accg_ref[...] = jnp.zeros_like(accg_ref)

        first, last = i == 0, i == nt - 1
        wa, wg, ba, bg = cwa[...], cwg[...], cba[...], cbg[...]

        def conv(p, m, n, w, b):
            return p * w[0:1] + m * w[1:2] + n * w[2:3] + b

        def grads(a, g, d):
            return d * g * _dsilu(a), d * _silu(a)

        xa, xg, d = ua[...], ug[...], df_ref[...]
        sa = _shift_rows(xa, jnp.where(first, 0.0, uap[7:8, :]), jnp.where(last, 0.0, uan[0:1, :]))
        sg = _shift_rows(xg, jnp.where(first, 0.0, ugp[7:8, :]), jnp.where(last, 0.0, ugn[0:1, :]))
        da, dg = grads(conv(sa[0], xa, sa[1], wa, ba), conv(sg[0], xg, sg[1], wg, bg), d)
        da_p, dg_p = grads(conv(uap[6:7, :], uap[7:8, :], xa[0:1], wa, ba),
                           conv(ugp[6:7, :], ugp[7:8, :], xg[0:1], wg, bg), dfp[7:8, :])
        da_n, dg_n = grads(conv(xa[tm - 1:tm], uan[0:1, :], uan[1:2, :], wa, ba),
                           conv(xg[tm - 1:tm], ugn[0:1, :], ugn[1:2, :], wg, bg), dfn[0:1, :])
        ta = _shift_rows(da, jnp.where(first, 0.0, da_p), jnp.where(last, 0.0, da_n))
        tg = _shift_rows(dg, jnp.where(first, 0.0, dg_p), jnp.where(last, 0.0, dg_n))
        du_ref[0] = (ta[1] * wa[0:1] + da * wa[1:2] + ta[0] * wa[2:3]).astype(BF16)
        du_ref[1] = (tg[1] * wg[0:1] + dg * wg[1:2] + tg[0] * wg[2:3]).astype(BF16)
        for t, (va, vg) in enumerate(((sa[0], sg[0]), (xa, xg), (sa[1], sg[1]))):
            acca_ref[t:t + 1, :] += jnp.sum(da * va, axis=0, keepdims=True)
            accg_ref[t:t + 1, :] += jnp.sum(dg * vg, axis=0, keepdims=True)
        acca_ref[3:4, :] += jnp.sum(da, axis=0, keepdims=True)
        accg_ref[3:4, :] += jnp.sum(dg, axis=0, keepdims=True)

    wspec = lambda off: pl.BlockSpec((3, tw), lambda j, i: (0, j + off))
    bspec = lambda off: pl.BlockSpec((1, tw), lambda j, i: (0, j + off))
    row = pl.BlockSpec((tm, tw), lambda j, i: (i, j))
    acc = pl.BlockSpec((8, tw), lambda j, i: (0, j))
    return pl.pallas_call(
        body, name=name, grid=(nw, nt),
        in_specs=_halo_specs(T, tm, tw, lambda j: j, "ji") + _halo_specs(T, tm, tw, lambda j: j + nw, "ji")
        + [wspec(0), wspec(nw), bspec(0), bspec(nw)] + _halo_specs(T, tm, tw, lambda j: j, "ji"),
        out_specs=[pl.BlockSpec((2, tm, tw), lambda j, i: (0, i, j)), acc, acc],
        out_shape=[jax.ShapeDtypeStruct((2, T, F), BF16),
                   jax.ShapeDtypeStruct((8, F), F32), jax.ShapeDtypeStruct((8, F), F32)],
        compiler_params=_cparams("parallel", "arbitrary"),
    )(u, u, u, u, u, u, cw, cw, cb, cb, df, df, df)


def _assemble_dz(lay, z_used, Z, n_ctx, dqa, drb, dga, dgb, dka, dva, dvg, dqg, dkg, dlr, name):
    T = dqa.shape[0]
    R = T + n_ctx
    tm = _pick(n_ctx, 128, 8)
    cb = n_ctx // tm

    def body(dqa_ref, drb_ref, dga_ref, dgb_ref, dka_ref, dva_ref, dvg0, dvg1, dqg0, dqg1, dkg0, dkg1, dlr_ref, o_ref):
        lat = pl.program_id(0) >= cb

        def put(seg, val):
            o_ref[:, lay[seg]:lay[seg] + val.shape[1]] = val.astype(BF16)

        def lat_only(ref):
            v = ref[...]
            return jnp.where(lat, v, jnp.zeros_like(v))

        put("qa", lat_only(dqa_ref))
        put("rb", lat_only(drb_ref))
        put("ga", lat_only(dga_ref))
        put("gb", lat_only(dgb_ref))
        put("ka", dka_ref[...])
        put("va", dva_ref[...])
        put("vb", dvg0[...].astype(F32) + dvg1[...].astype(F32))
        put("qb", dqg0[...].astype(F32) + dqg1[...].astype(F32))
        put("kb", dkg0[...].astype(F32) + dkg1[...].astype(F32))
        put("lr", dlr_ref[...])
        if Z > z_used:
            o_ref[:, z_used:] = jnp.zeros((tm, Z - z_used), BF16)

    lat_spec = lambda a: pl.BlockSpec((tm, a.shape[1]), lambda i: (jnp.maximum(i - cb, 0), 0))
    all_spec = lambda a: pl.BlockSpec((tm, a.shape[1]), lambda i: (i, 0))
    dir_specs = lambda pair: [all_spec(pair[0]), all_spec(pair[1])]
    return pl.pallas_call(
        body, name=name, grid=(R // tm,),
        in_specs=[lat_spec(dqa), lat_spec(drb), lat_spec(dga), lat_spec(dgb), all_spec(dka), all_spec(dva)]
        + dir_specs(dvg) + dir_specs(dqg) + dir_specs(dkg) + [all_spec(dlr)],
        out_specs=pl.BlockSpec((tm, Z), lambda i: (i, 0)),
        out_shape=jax.ShapeDtypeStruct((R, Z), BF16), compiler_params=_cparams("parallel"),
    )(dqa, drb, dga, dgb, dka, dva, *dvg, *dqg, *dkg, dlr)


def _mod_fwd(ca, w, b, name):
    n, D = ca.shape
    N = w.shape[1]
    tn = _pick(N, 512)

    def body(c_ref, w_ref, b_ref, o_ref, s_ref):
        s = _silu(c_ref[...])
        s_ref[...] = s
        o_ref[...] = _dot(s.astype(BF16), w_ref[...].astype(BF16), NN) + b_ref[...]

    return pl.pallas_call(
        body, name=name, grid=(N // tn,),
        in_specs=[pl.BlockSpec((n, D), lambda j: (0, 0)), pl.BlockSpec((D, tn), lambda j: (0, j)),
                  pl.BlockSpec((1, tn), lambda j: (0, j))],
        out_specs=[pl.BlockSpec((n, tn), lambda j: (0, j)), pl.BlockSpec((n, D), lambda j: (0, 0))],
        out_shape=[jax.ShapeDtypeStruct((n, N), F32), jax.ShapeDtypeStruct((n, D), F32)],
        compiler_params=_cparams("arbitrary"))(ca, w, b)


def _silu_bwd(dsil, ca, name):
    def body(d_ref, c_ref, o_ref):
        o_ref[...] = d_ref[...] * _dsilu(c_ref[...])

    return pl.pallas_call(body, name=name, out_shape=jax.ShapeDtypeStruct(ca.shape, F32))(dsil, ca)


def _adam_math(w, g, m, v):
    c1 = 1.0 - ADAM_B1 ** ADAM_STEP
    c2 = 1.0 - ADAM_B2 ** ADAM_STEP
    mn = ADAM_B1 * m + (1.0 - ADAM_B1) * g
    vn = ADAM_B2 * v + (1.0 - ADAM_B2) * (g * g)
    return -ADAM_LR * ((mn / c1) / (jnp.sqrt(vn / c2) + ADAM_EPS) + ADAM_WD * w), mn, vn


def _adamw(w, g, m, v, name, ride=None):
    Rw, Cw = w.shape
    tr = _pick(Rw, 128, 8)

    def body(w_ref, g_ref, m_ref, v_ref, d_ref, mo_ref, vo_ref):
        d_ref[...], mo_ref[...], vo_ref[...] = _adam_math(w_ref[...], g_ref[...], m_ref[...], v_ref[...])

    row = pl.BlockSpec((tr, Cw), lambda i: (i, 0))
    sh = jax.ShapeDtypeStruct((Rw, Cw), F32)
    return _pcall(body, name=name, grid=(Rw // tr,), in_specs=[row] * 4, out_specs=[row] * 3, out_shape=[sh] * 3,
                  sem=("parallel",), args=(w, g, m, v), ride=ride)


HBM_SPEC = pl.BlockSpec(memory_space=pltpu.HBM)


def _exchange(inputs, out_shapes, stages, name):
    n_in, n_out = len(inputs), len(out_shapes)
    n = sum(len(s) for s in stages)

    def body(*refs):
        ins, outs = refs[:n_in], refs[n_in:n_in + n_out]
        send_sems, recv_sems = refs[n_in + n_out:]
        k = 0
        for stage in stages:
            copies = _stage_copies(stage, ins, outs, send_sems, recv_sems, k)
            for cp in copies:
                cp.start()
            for cp in copies:
                cp.wait()
            k += len(stage)

    return pl.pallas_call(
        body, name=name, in_specs=[HBM_SPEC] * n_in, out_specs=[HBM_SPEC] * n_out, out_shape=out_shapes,
        scratch_shapes=[pltpu.SemaphoreType.DMA((n,)), pltpu.SemaphoreType.DMA((n,))],
    )(*inputs)


def _stage_copies(stage, ins, outs, send_sems, recv_sems, k0=0):
    me = (lax.axis_index("x"), lax.axis_index("y"), lax.axis_index("c"))
    copies = []
    for k, ((skind, sidx), sfn, didx, dfn, flip) in enumerate(stage):
        src = (ins if skind == "in" else outs)[sidx].at[sfn(*me)]
        dst = outs[didx].at[dfn(*me)]
        if flip == (0, 0, 0):
            copies.append(pltpu.make_async_copy(src, dst, send_sems.at[k0 + k]))
        else:
            peer = tuple(1 - a if f else a for a, f in zip(me, flip))
            copies.append(pltpu.make_async_remote_copy(src, dst, send_sems.at[k0 + k], recv_sems.at[k0 + k],
                                                       device_id=peer, device_id_type=MESH))
    return copies


def _pcall(body, *, name, grid, in_specs, out_specs, out_shape, scratch_shapes=(), sem, args, ride=None):
    many = isinstance(out_shape, (list, tuple))
    out_specs, out_shape = (list(out_specs), list(out_shape)) if many else ([out_specs], [out_shape])
    if ride is None:
        res = pl.pallas_call(body, name=name, grid=grid, in_specs=list(in_specs), out_specs=out_specs,
                             out_shape=out_shape, scratch_shapes=list(scratch_shapes),
                             compiler_params=_cparams(*sem))(*args)
        return res if many else res[0]
    x_in, x_out, stage, aliases = ride
    n_in, n_out, n_scr, n_xin, n_xout = len(in_specs), len(out_specs), len(scratch_shapes), len(x_in), len(x_out)

    def wrapped(*refs):
        ins, xins = refs[:n_in], refs[n_in:n_in + n_xin]
        o0 = n_in + n_xin
        outs, xouts = refs[o0:o0 + n_out], refs[o0 + n_out:o0 + n_out + n_xout]
        s0 = o0 + n_out + n_xout
        scr, (send_sems, recv_sems) = refs[s0:s0 + n_scr], refs[s0 + n_scr:]
        first = functools.reduce(jnp.logical_and, [pl.program_id(d) == 0 for d in range(len(grid))])
        last = functools.reduce(jnp.logical_and, [pl.program_id(d) == grid[d] - 1 for d in range(len(grid))])

        @pl.when(first)
        def _():
            for cp in _stage_copies(stage, xins, xouts, send_sems, recv_sems):
                cp.start()

        body(*ins, *outs, *scr)

        @pl.when(last)
        def _():
            for cp in _stage_copies(stage, xins, xouts, send_sems, recv_sems):
                cp.wait()

    res = pl.pallas_call(
        wrapped, name=name, grid=grid, in_specs=list(in_specs) + [HBM_SPEC] * n_xin,
        out_specs=out_specs + [HBM_SPEC] * n_xout, out_shape=out_shape + list(x_out),
        scratch_shapes=list(scratch_shapes) + [pltpu.SemaphoreType.DMA((len(stage),)),
                                               pltpu.SemaphoreType.DMA((len(stage),))],
        input_output_aliases={n_in + a: n_out + b for a, b in aliases.items()},
        compiler_params=_cparams(*(["arbitrary"] * len(grid))))(*args, *x_in)
    main = res[:n_out]
    return (main if many else main[0]), list(res[n_out:])


FLIPS_ALL = [(0, 0, 1), (0, 1, 0), (0, 1, 1), (1, 0, 0), (1, 0, 1), (1, 1, 0), (1, 1, 1)]
FLIPS_CHIP = [(0, 1, 0), (1, 0, 0), (1, 1, 0)]


def _sum_slots(buf, name):
    n, r, w = buf.shape
    tr = _pick(r, 256, 8)

    def body(b_ref, o_ref):
        acc = b_ref[0]
        for s in range(1, n):
            acc = acc + b_ref[s]
        o_ref[...] = acc

    return pl.pallas_call(
        body, name=name, grid=(r // tr,), in_specs=[pl.BlockSpec((n, tr, w), lambda i: (0, i, 0))],
        out_specs=pl.BlockSpec((tr, w), lambda i: (i, 0)), out_shape=jax.ShapeDtypeStruct((r, w), F32),
        compiler_params=_cparams("parallel"))(buf)


def _allreduce_plan(buf):
    whole = lambda x, y, c: (slice(None), slice(None))
    slot = lambda x, y, c: (4 * x + 2 * y + c,)
    stage = [(("in", 0), whole, 0, slot, f) for f in [(0, 0, 0)] + FLIPS_ALL]
    return [jax.ShapeDtypeStruct((8,) + buf.shape, F32)], stage


def _allreduce(buf, name):
    shapes, stage = _allreduce_plan(buf)
    (slots,) = _exchange([buf], shapes, [stage], name + "_x")
    return _sum_slots(slots, name + "_sum")


def _gather_plan(shards, src):
    half = lambda a, c: pl.ds(c * (a.shape[0] // 2), a.shape[0] // 2)
    first, second = [], []
    for n, a in enumerate(shards):
        for f in FLIPS_CHIP:
            first.append((("in", n), lambda x, y, c, a=a: (half(a, c), slice(None)), n,
                          lambda x, y, c, a=a: (2 * x + y, half(a, c), slice(None)), f))
            peer_slot = lambda x, y, c, a=a, f=f: (2 * (x ^ f[0]) + (y ^ f[1]), half(a, c), slice(None))
            second.append(((src, n), peer_slot, n, peer_slot, (0, 0, 1)))
    outs = [jax.ShapeDtypeStruct((4,) + a.shape, a.dtype) for a in shards]
    return first, second, outs


def _allgather_weights(shards, name):
    first, second, outs = _gather_plan(shards, "out")
    return _exchange(shards, outs, [first, second], name)


def _place_own(buf, shard, svec, name):
    _, Rs, Cs = buf.shape
    tr = _pick(Rs, 256, 16)

    def body(s_ref, buf_ref, sh_ref, o_ref):
        o_ref[0] = sh_ref[...]

    grid_spec = pltpu.PrefetchScalarGridSpec(
        num_scalar_prefetch=1, grid=(Rs // tr,),
        in_specs=[pl.BlockSpec(memory_space=pl.ANY), pl.BlockSpec((tr, Cs), lambda i, s: (i, 0))],
        out_specs=pl.BlockSpec((1, tr, Cs), lambda i, s: (s[0], i, 0)))
    return pl.pallas_call(body, name=name, grid_spec=grid_spec, out_shape=jax.ShapeDtypeStruct(buf.shape, buf.dtype),
                          input_output_aliases={1: 0}, compiler_params=_cparams("arbitrary"))(svec, buf, shard)


def _add_pair(G, bufA, cvec, name):
    _, Rs, Cs = G.shape
    Rh = Rs // 2
    tr = _pick(Rh, 128, 16)
    nb = Rh // tr

    def body(c_ref, g_ref, a_ref, o_ref):
        o_ref[...] = (g_ref[...] + a_ref[...]).astype(BF16)

    grid_spec = pltpu.PrefetchScalarGridSpec(
        num_scalar_prefetch=1, grid=(4, nb),
        in_specs=[pl.BlockSpec((1, tr, Cs), lambda s, i, c_ref: (s, c_ref[0] * nb + i, 0)),
                  pl.BlockSpec((1, tr, Cs), lambda s, i, c_ref: (s, i, 0))],
        out_specs=pl.BlockSpec((1, tr, Cs), lambda s, i, c_ref: (s, i, 0)))
    return pl.pallas_call(body, name=name, grid_spec=grid_spec, out_shape=jax.ShapeDtypeStruct((4, Rh, Cs), BF16),
                          compiler_params=_cparams("parallel", "parallel"))(cvec, G, bufA)


def _sum_chips(G, bufA, bufB, cvec, svec, name):
    _, Rs, Cs = G.shape
    Rh = Rs // 2
    tr = _pick(Rh, 128, 16)
    nb = Rh // tr

    def body(c_ref, s_ref, g_ref, a_ref, b_ref, o_ref):
        o_ref[...] = (g_ref[0] + a_ref[0]) + b_ref[0].astype(F32) + b_ref[1].astype(F32) + b_ref[2].astype(F32)

    grid_spec = pltpu.PrefetchScalarGridSpec(
        num_scalar_prefetch=2, grid=(nb,),
        in_specs=[pl.BlockSpec((1, tr, Cs), lambda i, c, s: (s[0], c[0] * nb + i, 0)),
                  pl.BlockSpec((1, tr, Cs), lambda i, c, s: (s[0], i, 0)),
                  pl.BlockSpec((3, tr, Cs), lambda i, c, s: (0, i, 0))],
        out_specs=pl.BlockSpec((tr, Cs), lambda i, c, s: (i, 0)))
    return pl.pallas_call(body, name=name, grid_spec=grid_spec, out_shape=jax.ShapeDtypeStruct((Rh, Cs), F32),
                          compiler_params=_cparams("parallel"))(cvec, svec, G, bufA, bufB)


def _pair_plan(grads):
    Rh = [g.shape[1] // 2 for g in grads]
    whole3 = lambda x, y, c: (slice(None), slice(None), slice(None))
    stage = [(("in", n), lambda x, y, c, n=n: (slice(None), pl.ds((1 - c) * Rh[n], Rh[n]), slice(None)), n,
              whole3, (0, 0, 1)) for n in range(len(grads))]
    return [jax.ShapeDtypeStruct((4, Rh[n], g.shape[2]), F32) for n, g in enumerate(grads)], stage


def _chips_plan(P):
    stage = [(("in", n), lambda x, y, c, f=f: (2 * (x ^ f[0]) + (y ^ f[1]),), n, lambda x, y, c, k=k: (k,), f)
             for n in range(len(P)) for k, f in enumerate(FLIPS_CHIP)]
    return [jax.ShapeDtypeStruct((3,) + p.shape[1:], BF16) for p in P], stage


def _halves_plan(mine):
    whole2 = lambda x, y, c: (slice(None), slice(None))
    stage = [(("in", n), whole2, n, whole2, (0, 0, 1)) for n in range(len(mine))]
    return [jax.ShapeDtypeStruct(r.shape, F32) for r in mine], stage


def _adamw_halves(w, mine, other, m, v, cvec, name):
    Rs, Cs = w.shape
    Rh = Rs // 2
    tr = _pick(Rh, 128, 8)
    nb = Rh // tr

    def body(c_ref, w_ref, a_ref, b_ref, m_ref, v_ref, g_ref, d_ref, mo_ref, vo_ref):
        gv = jnp.where(pl.program_id(0) // nb == c_ref[0], a_ref[...], b_ref[...])
        g_ref[...] = gv
        d_ref[...], mo_ref[...], vo_ref[...] = _adam_math(w_ref[...], gv, m_ref[...], v_ref[...])

    row = pl.BlockSpec((tr, Cs), lambda i, c: (i, 0))
    hrow = pl.BlockSpec((tr, Cs), lambda i, c: (i % nb, 0))
    grid_spec = pltpu.PrefetchScalarGridSpec(num_scalar_prefetch=1, grid=(2 * nb,),
                                             in_specs=[row, hrow, hrow, row, row], out_specs=[row] * 4)
    return pl.pallas_call(body, name=name, grid_spec=grid_spec, out_shape=[jax.ShapeDtypeStruct((Rs, Cs), F32)] * 4,
                          compiler_params=_cparams("parallel"))(cvec, w, mine, other, m, v)


def _pack(arrays):
    flat = [a.reshape(-1).astype(F32) for a in arrays]
    meta, off = [], 0
    for a, f in zip(arrays, flat):
        meta.append((off, a.shape))
        off += f.shape[0]
    total = -(-off // (8 * LANES)) * (8 * LANES)
    flat.append(jnp.zeros((total - off,), F32))
    return jnp.concatenate(flat).reshape(total // LANES, LANES), meta


def _unpack(buf, meta):
    flat = buf.reshape(-1)
    out = []
    for off, shape in meta:
        size = 1
        for s in shape:
            size *= s
        out.append(flat[off:off + size].reshape(shape))
    return out


WEIGHT_NAMES = ["c_ctx", "w_mod", "b_mod", "g_mix", "w_in", "q_norm", "k_norm", "attn_sink", "w_gate_f", "b_gate_f",
                "w_gate_b", "b_gate_b", "gla_norm", "w_attn_o", "w_gla_o", "w_out", "g_ffn", "w_up", "conv_w",
                "conv_b", "w_down"]
BIG_NAMES = ["w_in", "w_attn_o", "w_gla_o", "w_out", "w_up", "w_down"]
SHARDED_SMALL = ["w_gate_f", "w_gate_b", "conv_w"]


def _layouts(D):
    aw, kvw, gk, gv = N_Q_HEADS * HEAD_DIM, N_KV_HEADS * HEAD_DIM, D // 2, D
    widths = {"qa": aw, "ka": kvw, "va": kvw, "qb": gk, "kb": gk, "vb": gv, "rb": gv, "lr": 2 * GLA_LOWRANK,
              "ga": D, "gb": D}
    orig, off = {}, 0
    for s in ["qa", "ka", "va", "qb", "kb", "vb", "rb", "lr", "ga", "gb"]:
        orig[s] = off
        off += widths[s]
    order = ["qa", "vb", "rb", "ga", "gb", "ka", "va", "qb", "kb", "lr"]
    lay, off = {}, 0
    for s in order:
        lay[s] = off
        off += LANES if s == "lr" else widths[s]
    align = {"qa": aw, "vb": D, "rb": D, "ga": D, "gb": D, "ka": kvw, "va": kvw, "qb": gk, "kb": gk,
             "lr": LANES}
    for s in order:
        assert lay[s] % align[s] == 0, (s, lay[s], align[s])
    return widths, orig, order, lay, off, -(-off // (2 * MXU_TILE)) * (2 * MXU_TILE)


def _rope_tables(T, L):
    t = jnp.arange(T)
    nf = HEAD_DIM // 4
    inv = ROPE_THETA ** (-jnp.arange(nf, dtype=F32) / nf)
    ang = jnp.concatenate([(t // GRID_W)[:, None] * inv, (t % GRID_W)[:, None] * inv], axis=-1)
    cos, sin = jnp.cos(ang), jnp.sin(ang)
    cos2 = jnp.concatenate([jnp.ones((L, HEAD_DIM), F32), jnp.concatenate([cos, cos], axis=-1)], axis=0)
    sin2 = jnp.concatenate([jnp.zeros((L, HEAD_DIM), F32), jnp.concatenate([-sin, sin], axis=-1)], axis=0)
    return cos2, sin2


def _step(x, c, ctx, loss_target, W, M, V):
    xi, yi, ci = lax.axis_index("x"), lax.axis_index("y"), lax.axis_index("c")
    chip = 2 * xi + yi
    dev = 2 * chip + ci
    south = (ci == 0).astype(F32)
    cvec = ci.reshape(1).astype(jnp.int32)
    svec = chip.reshape(1).astype(jnp.int32)
    T, D = x.shape[1], x.shape[2]
    L = ctx.shape[1]
    R = L + T
    F = 4 * W["w_down"].shape[1]
    GK, GV = D // 2, D
    DK, DV = GK // GLA_HEADS, GV // GLA_HEADS
    N6 = 6 * D
    N4 = N6 // 4
    widths, orig, order, lay, z_used, Z = _layouts(D)

    def place_cols(shard, full_cols):
        cols = shard.shape[-1]
        full = jnp.zeros(shard.shape[:-1] + (full_cols,), F32)
        return lax.dynamic_update_slice(full, shard * south, (0,) * (shard.ndim - 1) + (chip * cols,))

    c_rows = lax.dynamic_update_slice(jnp.zeros((8, D), F32), c, (dev, 0))
    bufa, meta = _pack([c_rows, place_cols(W["w_gate_f"][0], GK), place_cols(W["w_gate_b"][0], GK),
                        place_cols(W["conv_w"][0], 2 * F)])
    c_all, wgf, wgb, cw = _unpack(_allreduce(bufa, "gather_small"), meta)
    ca = jnp.concatenate([c_all, W["c_ctx"][None, :], jnp.zeros((7, D), F32)], axis=0)
    b_shard = lax.dynamic_slice(W["b_mod"], (0, chip * N4), (1, N4))
    mod_part, sil = _mod_fwd(ca, W["w_mod"][0], b_shard, "mod_fwd")
    slots = lax.dynamic_update_slice(jnp.zeros((4, 16, N4), F32), (mod_part * south)[None], (chip, 0, 0))
    mod_all = _allreduce(slots.reshape(64, N4), "gather_mod").reshape(4, 16, N4).transpose(1, 0, 2).reshape(16, N6)
    mx = lax.dynamic_slice(mod_all, (dev, 0), (1, N6)).reshape(6, 1, D)
    mc = mod_all[8].reshape(6, 1, D)

    sq = lambda a: a.reshape(a.shape[1:])
    shards = [sq(W[n]).astype(BF16) for n in BIG_NAMES]
    own = lambda g, s, n: _place_own(g, s, svec, "place_" + n)
    cols = lambda g: g.transpose(1, 0, 2).reshape(g.shape[1], 4 * g.shape[2])
    rows = lambda g: g.reshape(4 * g.shape[1], g.shape[2])
    w_in_f = cols(own(_allgather_weights(shards[:1], "gather_w_in")[0], shards[0], "w_in"))
    sc1 = jnp.stack([mc[1], mx[1]])
    sh1 = jnp.stack([mc[0], mx[0]])
    h = _modnorm_fwd(ctx[0], x[0], W["g_mix"], sc1, sh1, "modnorm1")
    seg = lambda s: w_in_f[:, orig[s]:orig[s] + widths[s]]
    w_cat = jnp.concatenate([jnp.pad(seg(s), ((0, 0), (0, LANES - widths[s]))) if s == "lr" else seg(s)
                             for s in order] + [jnp.zeros((D, Z - z_used), BF16)], axis=1)
    gather1, gather2, gather_outs = _gather_plan(shards[1:], "in")
    wg = jnp.zeros((2, LANES, GK), F32).at[0, :GLA_LOWRANK].set(wgf).at[1, GLA_LOWRANK:2 * GLA_LOWRANK].set(wgb)
    bg = jnp.stack([W["b_gate_f"], W["b_gate_b"]])
    cb = W["conv_b"]
    sink_rows = jnp.broadcast_to(W["attn_sink"][0][:, None], (N_Q_HEADS, HEAD_DIM))
    cos2, sin2 = _rope_tables(T, L)
    blk = lambda s, w: lay[s] // w

    z, landed = _matmul(h, w_cat, "nn", F32, "proj_in", tn=1536, ride=(shards[1:], gather_outs, gather1, {}))
    qn = _qknorm_fwd(z, blk("qa", widths["qa"]), T, L, W["q_norm"], cos2, sin2, N_Q_HEADS, "qnorm")
    kn = _qknorm_fwd(z, blk("ka", widths["ka"]), R, 0, W["k_norm"], cos2, sin2, N_KV_HEADS, "knorm")
    vb = _cast_seg(z, blk("va", widths["va"]), widths["va"], "vcast")
    (o_attn, lse), landed = _attn_fwd(qn, kn, vb, sink_rows, L, "attn_fwd",
                               ride=(landed, gather_outs, gather2, {n: n for n in range(len(landed))}))
    g_ao, g_go, g_out, g_up, g_dn = [own(g, s, n) for g, s, n in zip(landed, shards[1:], BIG_NAMES[1:])]
    w_ao, w_go, w_out, w_up, w_dn = rows(g_ao), rows(g_go), rows(g_out), cols(g_up), rows(g_dn)
    gla_blks = (blk("qb", GK), blk("kb", GK), blk("vb", GV), blk("lr", LANES))
    o_f, o_b, sprev = _gla_fwd(z, *gla_blks, wg, bg, DV, L, "gla_fwd")
    p = _glanorm_fwd(o_f, o_b, z, blk("rb", D), W["gla_norm"], L, "glanorm")
    ya = _matmul(o_attn, w_ao, "nn", BF16, "proj_attn_o")
    yg = _matmul(p, w_go, "nn", BF16, "proj_gla_o")
    m = _gate_fwd(z, blk("ga", D), blk("gb", D), ya, yg, L, "gate")
    mix = _matmul(m, w_out, "nn", F32, "proj_out")
    x1, h2 = _resnorm_fwd(x[0], mix, mx[2], W["g_ffn"], mx[4], mx[3], "resnorm2")
    u = _matmul(h2, w_up, "nn", F32, "ffn_up")
    f = _conv_fwd(u, cw, cb, "conv_swiglu")
    d = _matmul(f, w_dn, "nn", F32, "ffn_down", tk=2816)
    dy, dd, lacc = _loss_head(d, x1, mx[5], loss_target[0], "loss_head")
    loss = lax.psum((0.5 / D) * jnp.sum(lacc[0]), ("x", "y", "c"))

    gw_dn = _matmul(f, dd, "tn", F32, "ffn_down_dw")
    df = _matmul(dd, w_dn, "nt", F32, "ffn_down_dx")
    du, acca, accg = _conv_bwd(u, df, cw, cb, "conv_swiglu_bwd")
    gw_up = _matmul(h2, du, "tn", F32, "ffn_up_dw", tm=512, halves="b", col_shards=4)
    dh2 = _matmul(du, w_up, "nt", F32, "ffn_up_dx", tk=2816, halves="a")
    dx1, dmix, s2 = _resnorm_bwd(x1, dh2, W["g_ffn"], mx[4], dy, mix, mx[2], "resnorm2_bwd")
    gw_out = _matmul(m, dmix, "tn", F32, "proj_out_dw")
    dm = _matmul(dmix, w_out, "nt", BF16, "proj_out_dx")
    dya, dyg, dga, dgb = _gate_bwd(z, blk("ga", D), blk("gb", D), ya, yg, dm, L, "gate_bwd")
    gw_ao = _matmul(o_attn, dya, "tn", F32, "proj_attn_o_dw")
    do_attn = _matmul(dya, w_ao, "nt", BF16, "proj_attn_o_dx")
    gw_go = _matmul(p, dyg, "tn", F32, "proj_gla_o_dw")
    dp = _matmul(dyg, w_go, "nt", BF16, "proj_gla_o_dx")
    do_gla, drb, s_gn = _glanorm_bwd(o_f, o_b, z, blk("rb", D), W["gla_norm"], dp, L, "glanorm_bwd")
    do_pad = jnp.concatenate([jnp.zeros((L, GV), BF16), do_gla], axis=0)
    by_cols = lambda g: g.reshape(g.shape[0], 4, g.shape[1] // 4).transpose(1, 0, 2)
    by_rows = lambda g: g.reshape(4, g.shape[0] // 4, g.shape[1])
    early = [by_rows(gw_ao), by_rows(gw_go), by_rows(gw_out), gw_up, by_rows(gw_dn)]
    (dq_f, dk_f, dv_f, dpre_f, dq_b, dk_b, dv_b, dpre_b, dbg), pair_e = _gla_bwd(
        z, *gla_blks, wg, bg, sprev, do_pad, L, "gla_bwd", ride=(early, *_pair_plan(early), {}))
    dqg, dkg, dvg = (dq_f, dq_b), (dk_f, dk_b), (dv_f, dv_b)
    sums_e = [_add_pair(g, a, cvec, "reduce_early_add%d" % n) for n, (g, a) in enumerate(zip(early, pair_e))]
    wg_cat = jnp.concatenate([wg[0], wg[1]], axis=1)
    dpre = jnp.stack([dpre_f, dpre_b])
    dlr = _matmul(dpre, wg_cat, "nt", BF16, "gla_gate_dx", halves="a")
    dwg = _matmul(z[:, lay["lr"]:lay["lr"] + LANES], dpre, "tn", F32, "gla_gate_dw", halves="b")
    (dqn, dkw, dvw, dkc, dvc, dsn), chips_e = _attn_bwd(qn, kn, vb, sink_rows, lse, do_attn, L, "attn_bwd",
                                                        ride=(sums_e, *_chips_plan(sums_e), {}))
    mine_e = [_sum_chips(g, a, b, cvec, svec, "reduce_early_sum%d" % n)
              for n, (g, a, b) in enumerate(zip(early, pair_e, chips_e))]
    dqa, s_qn = _qknorm_bwd(z, blk("qa", widths["qa"]), T, L, W["q_norm"], cos2, sin2, dqn, N_Q_HEADS, "qnorm_bwd")
    dk_all = jnp.concatenate([dkc, dkw[WINDOW:WINDOW + T]], axis=0)
    dv_all = jnp.concatenate([dvc, dvw[WINDOW:WINDOW + T]], axis=0)
    dka, s_kn = _qknorm_bwd(z, blk("ka", widths["ka"]), R, 0, W["k_norm"], cos2, sin2, dk_all, N_KV_HEADS, "knorm_bwd")
    dz = _assemble_dz(lay, z_used, Z, L, dqa, drb, dga, dgb, dka, dv_all, dvg, dqg, dkg, dlr, "assemble_dz")
    gw_cat, other_e = _matmul(h, dz, "tn", F32, "proj_in_dw", tn=768, tk=2816,
                              ride=(mine_e, *_halves_plan(mine_e), {}))
    gw_in = jnp.concatenate([gw_cat[:, lay[s]:lay[s] + widths[s]] for s in ["qa", "ka", "va", "qb", "kb", "vb", "rb",
                                                                           "lr", "ga", "gb"]], axis=1)
    late = [by_cols(gw_in)]
    shapes, stage = _pair_plan(late)
    pair_l = _exchange(late, shapes, [stage], "reduce_late_pair")
    sums_l = [_add_pair(late[0], pair_l[0], cvec, "reduce_late_add")]
    dh, chips_l = _matmul(dz, w_cat, "nt", F32, "proj_in_dx", tk=4608, ride=(sums_l, *_chips_plan(sums_l), {}))
    mine_l = [_sum_chips(late[0], pair_l[0], chips_l[0], cvec, svec, "reduce_late_sum")]
    shapes, stage = _halves_plan(mine_l)
    other_l = _exchange(mine_l, shapes, [stage], "reduce_late_halves")
    mine, other = mine_l + mine_e, list(other_l) + other_e
    grad_x, s1 = _modnorm_bwd(x[0], dh, W["g_mix"], mx[1], dx1, "modnorm1_bwd", dh_roff=L)
    _, s1c = _modnorm_bwd(ctx[0], dh, W["g_mix"], mc[1], None, "modnorm1_ctx_bwd")

    dmod_x = jnp.concatenate([s1[0], s1[1], s2[3], s2[0], s2[1], lacc[1]])
    dmod_c = jnp.concatenate([s1c[0], s1c[1], jnp.zeros((4 * D,), F32)])
    dmod_rows = lax.dynamic_update_slice(jnp.zeros((9, N6), F32).at[8].set(dmod_c), dmod_x[None], (dev, 0))
    small = [dmod_rows, dmod_x + dmod_c, s1[2] + s1c[2], s_qn[0], s_kn[0], dsn[:, 0, :Q_PER_KV].reshape(N_Q_HEADS),
             dwg[:GLA_LOWRANK, :GK], dbg[0].reshape(GK), dwg[GLA_LOWRANK:2 * GLA_LOWRANK, GK:], dbg[1].reshape(GK),
             s_gn[0], s2[2], jnp.concatenate([acca[0:3], accg[0:3]], axis=1), jnp.concatenate([acca[3], accg[3]])]
    bufc, meta = _pack(small)
    (dmod_sum, g_b_mod, g_g_mix, g_q_norm, g_k_norm, g_sink, g_wgf, g_bgf, g_wgb, g_bgb, g_gla_norm, g_g_ffn,
     g_conv_w, g_conv_b) = _unpack(_allreduce(bufc, "reduce_small"), meta)
    dmod16 = lax.dynamic_slice(jnp.concatenate([dmod_sum, jnp.zeros((7, N6), F32)], axis=0), (0, chip * N4), (16, N4))
    g_w_mod = _matmul(sil, dmod16, "tn", F32, "mod_dw")
    dsil = _matmul(dmod16, W["w_mod"][0], "nt", F32, "mod_dx")
    g_c_ctx = _silu_bwd(_allreduce(dsil * south, "reduce_cctx"), ca, "silu_bwd")[8]

    cut = lambda g: lax.dynamic_slice(g, (0, chip * (g.shape[1] // 4)), (g.shape[0], g.shape[1] // 4))
    grads = {"c_ctx": g_c_ctx, "w_mod": g_w_mod[None], "b_mod": g_b_mod[None], "g_mix": g_g_mix[None],
             "q_norm": g_q_norm[None], "k_norm": g_k_norm[None], "attn_sink": g_sink[None],
             "w_gate_f": cut(g_wgf)[None], "b_gate_f": g_bgf[None], "w_gate_b": cut(g_wgb)[None],
             "b_gate_b": g_bgb[None], "gla_norm": g_gla_norm[None], "g_ffn": g_g_ffn[None],
             "conv_w": cut(g_conv_w)[None], "conv_b": g_conv_b[None]}

    delta, new_m, new_v = {}, {}, {}
    dl, mn, vn = _adamw(W["w_mod"][0], g_w_mod, M["w_mod"][0], V["w_mod"][0], "adamw_w_mod")
    delta["w_mod"], new_m["w_mod"], new_v["w_mod"] = dl[None], mn[None], vn[None]
    for n, a, b in zip(BIG_NAMES, mine, other):
        g, dl, mn, vn = _adamw_halves(sq(W[n]), a, b, sq(M[n]), sq(V[n]), cvec, "adamw_" + n)
        grads[n], delta[n], new_m[n], new_v[n] = g[None], dl[None], mn[None], vn[None]
    small_names = [n for n in WEIGHT_NAMES if n not in delta]
    packs = [_pack([src[n] for n in small_names]) for src in (W, grads, M, V)]
    meta = packs[0][1]
    outs = _adamw(packs[0][0], packs[1][0], packs[2][0], packs[3][0], "adamw_small")
    for res, o in zip((delta, new_m, new_v), outs):
        for n, a in zip(small_names, _unpack(o, meta)):
            res[n] = a
    return (loss, grad_x[None], *[grads[n] for n in WEIGHT_NAMES], *[delta[n] for n in WEIGHT_NAMES],
            *[new_m[n] for n in WEIGHT_NAMES], *[new_v[n] for n in WEIGHT_NAMES])


def kernel(x, c, ctx, c_ctx, w_mod, b_mod, g_mix, w_in, q_norm, k_norm, attn_sink, w_gate_f, b_gate_f, w_gate_b, b_gate_b, gla_norm, w_attn_o, w_gla_o, w_out, g_ffn, w_up, conv_w, conv_b, w_down, loss_target, m_c_ctx, m_w_mod, m_b_mod, m_g_mix, m_w_in, m_q_norm, m_k_norm, m_attn_sink, m_w_gate_f, m_b_gate_f, m_w_gate_b, m_b_gate_b, m_gla_norm, m_w_attn_o, m_w_gla_o, m_w_out, m_g_ffn, m_w_up, m_conv_w, m_conv_b, m_w_down, v_c_ctx, v_w_mod, v_b_mod, v_g_mix, v_w_in, v_q_norm, v_k_norm, v_attn_sink, v_w_gate_f, v_b_gate_f, v_w_gate_b, v_b_gate_b, v_gla_norm, v_w_attn_o, v_w_gla_o, v_w_out, v_g_ffn, v_w_up, v_conv_w, v_conv_b, v_w_down):
    W = dict(zip(WEIGHT_NAMES, (c_ctx, w_mod, b_mod, g_mix, w_in, q_norm, k_norm, attn_sink, w_gate_f, b_gate_f,
                                w_gate_b, b_gate_b, gla_norm, w_attn_o, w_gla_o, w_out, g_ffn, w_up, conv_w, conv_b,
                                w_down)))
    M = dict(zip(WEIGHT_NAMES, (m_c_ctx, m_w_mod, m_b_mod, m_g_mix, m_w_in, m_q_norm, m_k_norm, m_attn_sink,
                                m_w_gate_f, m_b_gate_f, m_w_gate_b, m_b_gate_b, m_gla_norm, m_w_attn_o, m_w_gla_o,
                                m_w_out, m_g_ffn, m_w_up, m_conv_w, m_conv_b, m_w_down)))
    V = dict(zip(WEIGHT_NAMES, (v_c_ctx, v_w_mod, v_b_mod, v_g_mix, v_w_in, v_q_norm, v_k_norm, v_attn_sink,
                                v_w_gate_f, v_b_gate_f, v_w_gate_b, v_b_gate_b, v_gla_norm, v_w_attn_o, v_w_gla_o,
                                v_w_out, v_g_ffn, v_w_up, v_conv_w, v_conv_b, v_w_down)))
    return _step(x, c, ctx, loss_target, W, M, V)
```

```python
import functools
import math

import jax
import jax.numpy as jnp
from jax import lax
from jax.experimental import pallas as pl
from jax.experimental.pallas import tpu as pltpu

F32 = jnp.float32
BF16 = jnp.bfloat16
MESH = pl.DeviceIdType.MESH

EPS = 1e-6
HEAD_DIM = 128
N_Q_HEADS = 16
N_KV_HEADS = 4
Q_PER_KV = N_Q_HEADS // N_KV_HEADS
WINDOW = 128
GLA_HEADS = 4
GLA_LOWRANK = 16
GLA_GATE_NORM = 16.0
GLA_CHUNK = 64
GRID_W = 64
ROPE_THETA = 10000.0
GLA_LEVELS = (32, 16, 8, 4, 2, 1)
LANES = 128
MXU_TILE = 256

ADAM_LR = 0.001
ADAM_B1 = 0.9
ADAM_B2 = 0.999
ADAM_EPS = 1e-08
ADAM_WD = 0.01
ADAM_STEP = 10

VMEM_LIMIT = 52 * 1024 * 1024


def _cparams(*sem):
    return pltpu.CompilerParams(dimension_semantics=sem, vmem_limit_bytes=VMEM_LIMIT)


def _pick(n, target, mult=LANES):
    best = None
    d = mult
    while d <= min(n, target):
        if n % d == 0:
            best = d
        d += mult
    return n if best is None else best


def _sigmoid(x):
    return 1.0 / (1.0 + jnp.exp(-x))


def _silu(x):
    return x * _sigmoid(x)


def _dsilu(x):
    s = _sigmoid(x)
    return s * (1.0 + x * (1.0 - s))


def _dot(a, b, dims):
    return lax.dot_general(a, b, (dims, ((), ())), preferred_element_type=F32)


NN = ((1,), (0,))
NT = ((1,), (1,))
TN = ((0,), (0,))


def _matmul(a, b, mode, out_dtype, name, tm=1024, tn=1024, tk=2048, ride=None, halves=None, col_shards=None):
    if halves == "a":
        assert mode == "nt"
        (_, M, Kh), (N, K2) = a.shape, b.shape
        K = 2 * Kh
    elif halves == "b":
        assert mode == "tn"
        (K, M), (_, K2, Nh) = a.shape, b.shape
        N = 2 * Nh
    elif mode == "nn":
        (M, K), (K2, N) = a.shape, b.shape
    elif mode == "nt":
        (M, K), (N, K2) = a.shape, b.shape
    else:
        (K, M), (K2, N) = a.shape, b.shape
    assert K == K2, (name, a.shape, b.shape)
    pick = lambda n, t: _pick(n, t, MXU_TILE) if n % MXU_TILE == 0 else _pick(n, t)
    tm, tn, tk = pick(M, tm), pick(N // 2 if halves == "b" else N, tn), pick(K // 2 if halves == "a" else K, tk)
    if col_shards is not None:
        tn = N // col_shards
    nk = K // tk
    dims = {"nn": NN, "nt": NT, "tn": TN}[mode]

    def body(a_ref, b_ref, o_ref, acc_ref):
        k = pl.program_id(2)

        @pl.when(k == 0)
        def _():
            acc_ref[...] = jnp.zeros_like(acc_ref)

        av = a_ref[0] if halves == "a" else a_ref[...]
        bv = b_ref[0] if halves == "b" else b_ref[...]
        acc_ref[...] += _dot(av.astype(BF16), bv.astype(BF16), dims)

        @pl.when(k == nk - 1)
        def _():
            o_ref[...] = acc_ref[...].astype(out_dtype).reshape(o_ref.shape)

    if halves == "a":
        per = (K // 2) // tk
        a_spec = pl.BlockSpec((1, tm, tk), lambda i, j, k: (k // per, i, k % per))
    elif mode == "tn":
        a_spec = pl.BlockSpec((tk, tm), lambda i, j, k: (k, i))
    else:
        a_spec = pl.BlockSpec((tm, tk), lambda i, j, k: (i, k))
    if halves == "b":
        per = (N // 2) // tn
        b_spec = pl.BlockSpec((1, tk, tn), lambda i, j, k: (j // per, k, j % per))
    elif mode == "nt":
        b_spec = pl.BlockSpec((tn, tk), lambda i, j, k: (j, k))
    else:
        b_spec = pl.BlockSpec((tk, tn), lambda i, j, k: (k, j))
    if col_shards is None:
        out_spec, out_shape = pl.BlockSpec((tm, tn), lambda i, j, k: (i, j)), (M, N)
    else:
        assert tn * col_shards == N, (name, tn, N)
        out_spec, out_shape = pl.BlockSpec((1, tm, tn), lambda i, j, k: (j, i, 0)), (col_shards, M, tn)
    return _pcall(
        body, name=name, grid=(M // tm, N // tn, nk),
        in_specs=[a_spec, b_spec],
        out_specs=out_spec,
        out_shape=jax.ShapeDtypeStruct(out_shape, out_dtype),
        scratch_shapes=[pltpu.VMEM((tm, tn), F32)],
        sem=("parallel", "parallel", "arbitrary"), args=(a, b), ride=ride)


def _modnorm_fwd(xc, xl, g, sc, sh, name, ride=None):
    (L, D), T = xc.shape, xl.shape[0]
    tm = _pick(math.gcd(L, T), 256, 8)
    cb = L // tm

    def body(xc_ref, xl_ref, g_ref, sc_ref, sh_ref, h_ref):
        x = jnp.where(pl.program_id(0) < cb, xc_ref[...], xl_ref[...])
        r = lax.rsqrt(jnp.mean(x * x, axis=-1, keepdims=True) + EPS)
        n = x * r * g_ref[...]
        h_ref[...] = (n * (1.0 + sc_ref[0]) + sh_ref[0]).astype(BF16)

    sel = lambda i: (jnp.where(i < cb, 0, 1), 0, 0)
    return _pcall(
        body, name=name, grid=((L + T) // tm,),
        in_specs=[pl.BlockSpec((tm, D), lambda i: (jnp.minimum(i, cb - 1), 0)),
                  pl.BlockSpec((tm, D), lambda i: (jnp.maximum(i - cb, 0), 0)),
                  pl.BlockSpec((1, D), lambda i: (0, 0)), pl.BlockSpec((1, 1, D), sel), pl.BlockSpec((1, 1, D), sel)],
        out_specs=pl.BlockSpec((tm, D), lambda i: (i, 0)),
        out_shape=jax.ShapeDtypeStruct((L + T, D), BF16),
        sem=("parallel",), args=(xc, xl, g, sc, sh), ride=ride)


def _modnorm_bwd(x, dh, g, sc, resid, name, dh_roff=0):
    N, D = x.shape
    tm = _pick(math.gcd(N, dh_roff), 256, 8)
    ro = dh_roff // tm
    want_dx = resid is not None

    def body(*refs):
        if want_dx:
            x_ref, dh_ref, g_ref, sc_ref, res_ref, dx_ref, acc_ref = refs
        else:
            x_ref, dh_ref, g_ref, sc_ref, acc_ref = refs
        i = pl.program_id(0)

        @pl.when(i == 0)
        def _():
            acc_ref[...] = jnp.zeros_like(acc_ref)

        xv, dhv, gv = x_ref[...], dh_ref[...], g_ref[...]
        r = lax.rsqrt(jnp.mean(xv * xv, axis=-1, keepdims=True) + EPS)
        xh = xv * r
        dn = dhv * (1.0 + sc_ref[...])
        acc_ref[0:1, :] += jnp.sum(dhv, axis=0, keepdims=True)
        acc_ref[1:2, :] += jnp.sum(dhv * xh * gv, axis=0, keepdims=True)
        acc_ref[2:3, :] += jnp.sum(dn * xh, axis=0, keepdims=True)
        if want_dx:
            dxh = dn * gv
            dx_ref[...] = res_ref[...] + r * (dxh - xh * jnp.mean(dxh * xh, axis=-1, keepdims=True))

    row = pl.BlockSpec((tm, D), lambda i: (i, 0))
    drow = pl.BlockSpec((tm, D), lambda i: (i + ro, 0))
    vec = pl.BlockSpec((1, D), lambda i: (0, 0))
    acc = pl.BlockSpec((8, D), lambda i: (0, 0))
    acc_shape = jax.ShapeDtypeStruct((8, D), F32)
    if want_dx:
        return pl.pallas_call(
            body, name=name, grid=(N // tm,), in_specs=[row, drow, vec, vec, row],
            out_specs=[row, acc], out_shape=[jax.ShapeDtypeStruct((N, D), F32), acc_shape],
            compiler_params=_cparams("arbitrary"))(x, dh, g, sc, resid)
    sums = pl.pallas_call(
        body, name=name, grid=(N // tm,), in_specs=[row, drow, vec, vec],
        out_specs=acc, out_shape=acc_shape, compiler_params=_cparams("arbitrary"))(x, dh, g, sc)
    return None, sums


def _resnorm_bwd(x1, dh, g, sc, dy, mix, gt, name):
    N, D = x1.shape
    tm = _pick(N, 256, 8)

    def body(x_ref, dh_ref, g_ref, sc_ref, dy_ref, mix_ref, gt_ref, dx_ref, dm_ref, acc_ref):
        i = pl.program_id(0)

        @pl.when(i == 0)
        def _():
            acc_ref[...] = jnp.zeros_like(acc_ref)

        xv, dhv, gv = x_ref[...], dh_ref[...], g_ref[...]
        r = lax.rsqrt(jnp.mean(xv * xv, axis=-1, keepdims=True) + EPS)
        xh = xv * r
        dn = dhv * (1.0 + sc_ref[...])
        dxh = dn * gv
        dx = dy_ref[...] + r * (dxh - xh * jnp.mean(dxh * xh, axis=-1, keepdims=True))
        dx_ref[...] = dx
        dm_ref[...] = (dx * gt_ref[...]).astype(BF16)
        acc_ref[0:1, :] += jnp.sum(dhv, axis=0, keepdims=True)
        acc_ref[1:2, :] += jnp.sum(dhv * xh * gv, axis=0, keepdims=True)
        acc_ref[2:3, :] += jnp.sum(dn * xh, axis=0, keepdims=True)
        acc_ref[3:4, :] += jnp.sum(dx * mix_ref[...], axis=0, keepdims=True)

    row = pl.BlockSpec((tm, D), lambda i: (i, 0))
    vec = pl.BlockSpec((1, D), lambda i: (0, 0))
    return pl.pallas_call(
        body, name=name, grid=(N // tm,), in_specs=[row, row, vec, vec, row, row, vec],
        out_specs=[row, row, pl.BlockSpec((8, D), lambda i: (0, 0))],
        out_shape=[jax.ShapeDtypeStruct((N, D), F32), jax.ShapeDtypeStruct((N, D), BF16),
                   jax.ShapeDtypeStruct((8, D), F32)],
        compiler_params=_cparams("arbitrary"))(x1, dh, g, sc, dy, mix, gt)


def _qknorm_fwd(z, cblk, nrows, roff, w, cos2, sin2, nh, name):
    W = nh * HEAD_DIM
    tm = _pick(math.gcd(nrows, roff), 256, 8)
    ro = roff // tm
    assert roff % tm == 0

    def body(z_ref, w_ref, c_ref, s_ref, o_ref):
        c, s, wv = c_ref[...], s_ref[...], w_ref[...]
        for h in range(nh):
            x = z_ref[:, h * HEAD_DIM:(h + 1) * HEAD_DIM]
            r = lax.rsqrt(jnp.mean(x * x, axis=-1, keepdims=True) + EPS)
            y = x * r * wv
            o_ref[:, h * HEAD_DIM:(h + 1) * HEAD_DIM] = (y * c + pltpu.roll(y, HEAD_DIM // 2, 1) * s).astype(BF16)

    return pl.pallas_call(
        body, name=name, grid=(nrows // tm,),
        in_specs=[pl.BlockSpec((tm, W), lambda i: (i + ro, cblk)), pl.BlockSpec((1, HEAD_DIM), lambda i: (0, 0)),
                  pl.BlockSpec((tm, HEAD_DIM), lambda i: (i + ro, 0)), pl.BlockSpec((tm, HEAD_DIM), lambda i: (i + ro, 0))],
        out_specs=pl.BlockSpec((tm, W), lambda i: (i, 0)),
        out_shape=jax.ShapeDtypeStruct((nrows, W), BF16),
        compiler_params=_cparams("parallel"),
    )(z, w, cos2, sin2)


def _qknorm_bwd(z, cblk, nrows, roff, w, cos2, sin2, dy, nh, name):
    W = nh * HEAD_DIM
    tm = _pick(math.gcd(nrows, roff), 256, 8)
    ro = roff // tm

    def body(z_ref, w_ref, c_ref, s_ref, dy_ref, dz_ref, acc_ref):
        i = pl.program_id(0)

        @pl.when(i == 0)
        def _():
            acc_ref[...] = jnp.zeros_like(acc_ref)

        c, s, wv = c_ref[...], s_ref[...], w_ref[...]
        dw = jnp.zeros((1, HEAD_DIM), F32)
        for h in range(nh):
            sl = slice(h * HEAD_DIM, (h + 1) * HEAD_DIM)
            x = z_ref[:, sl]
            d = dy_ref[:, sl]
            dyn = d * c + pltpu.roll(d * s, HEAD_DIM // 2, 1)
            r = lax.rsqrt(jnp.mean(x * x, axis=-1, keepdims=True) + EPS)
            xh = x * r
            dw = dw + jnp.sum(dyn * xh, axis=0, keepdims=True)
            dxh = dyn * wv
            dz_ref[:, sl] = (r * (dxh - xh * jnp.mean(dxh * xh, axis=-1, keepdims=True))).astype(BF16)
        acc_ref[0:1, :] += dw

    return pl.pallas_call(
        body, name=name, grid=(nrows // tm,),
        in_specs=[pl.BlockSpec((tm, W), lambda i: (i + ro, cblk)), pl.BlockSpec((1, HEAD_DIM), lambda i: (0, 0)),
                  pl.BlockSpec((tm, HEAD_DIM), lambda i: (i + ro, 0)), pl.BlockSpec((tm, HEAD_DIM), lambda i: (i + ro, 0)),
                  pl.BlockSpec((tm, W), lambda i: (i, 0))],
        out_specs=[pl.BlockSpec((tm, W), lambda i: (i, 0)), pl.BlockSpec((8, HEAD_DIM), lambda i: (0, 0))],
        out_shape=[jax.ShapeDtypeStruct((nrows, W), BF16), jax.ShapeDtypeStruct((8, HEAD_DIM), F32)],
        compiler_params=_cparams("arbitrary"),
    )(z, w, cos2, sin2, dy)


def _cast_seg(z, cblk, width, name):
    R = z.shape[0]
    tm = _pick(R, 512, 8)

    def body(z_ref, o_ref):
        o_ref[...] = z_ref[...].astype(BF16)

    return pl.pallas_call(
        body, name=name, grid=(R // tm,),
        in_specs=[pl.BlockSpec((tm, width), lambda i: (i, cblk))],
        out_specs=pl.BlockSpec((tm, width), lambda i: (i, 0)),
        out_shape=jax.ShapeDtypeStruct((R, width), BF16), compiler_params=_cparams("parallel"))(z)


NEG_BIG = -1e30


KV_PER_STEP = 2
KV_PER_STEP_FWD = 4
ATTN_BWD_STACK = 2


def _attn_specs(T, n_ctx, kv_per_step=KV_PER_STEP):
    nb = T // WINDOW
    lb = n_ctx // WINDOW
    kvw = kv_per_step * HEAD_DIM
    blk = lambda f: pl.BlockSpec((WINDOW, kvw), f)
    win = [blk(lambda h, i: (lb + jnp.maximum(i - 1, 0), h)), blk(lambda h, i: (lb + i, h)),
           blk(lambda h, i: (lb + jnp.minimum(i + 1, nb - 1), h))]
    ctx = pl.BlockSpec((n_ctx, kvw), lambda h, i: (0, h))
    qspec = pl.BlockSpec((WINDOW, kv_per_step * Q_PER_KV * HEAD_DIM), lambda h, i: (i, h))
    sink = pl.BlockSpec((N_Q_HEADS, HEAD_DIM), lambda h, i: (0, 0))
    return nb, qspec, win, ctx, sink


def _attn_probs(q, kw, kctx, snk, valid):
    scale = HEAD_DIM ** -0.5
    s_lat = jnp.where(valid, _dot(q, kw, NT) * scale, NEG_BIG)
    s_ctx = _dot(q, kctx, NT) * scale
    m = jnp.maximum(jnp.maximum(jnp.max(s_lat, axis=-1, keepdims=True), jnp.max(s_ctx, axis=-1, keepdims=True)), snk)
    p_lat = jnp.exp(s_lat - m)
    p_ctx = jnp.exp(s_ctx - m)
    p_snk = jnp.exp(snk - m)
    den = p_snk + jnp.sum(p_lat, axis=-1, keepdims=True) + jnp.sum(p_ctx, axis=-1, keepdims=True)
    return p_lat, p_ctx, den, m


def _attn_probs_lse(q, kw, kctx, snk, valid, lse):
    scale = HEAD_DIM ** -0.5
    s_lat = jnp.where(valid, _dot(q, kw, NT) * scale, NEG_BIG)
    s_ctx = _dot(q, kctx, NT) * scale
    return jnp.exp(s_lat - lse), jnp.exp(s_ctx - lse), jnp.exp(snk - lse)


def _attn_valid(i, T, heads):
    rows = heads * WINDOW
    qpos = i * WINDOW + (lax.broadcasted_iota(jnp.int32, (rows, 3 * WINDOW), 0) & (WINDOW - 1))
    kpos = (i - 1) * WINDOW + lax.broadcasted_iota(jnp.int32, (rows, 3 * WINDOW), 1)
    return (jnp.abs(qpos - kpos) <= WINDOW) & (kpos >= 0) & (kpos < T)


def _stack_heads(ref, hh, heads=range(Q_PER_KV)):
    c0 = hh * Q_PER_KV * HEAD_DIM
    return jnp.concatenate([ref[:, c0 + g * HEAD_DIM:c0 + (g + 1) * HEAD_DIM] for g in heads], axis=0)


def _stack_sinks(sink_ref, kvh, heads=range(Q_PER_KV)):
    return jnp.concatenate([jnp.broadcast_to(sink_ref[pl.ds(kvh * Q_PER_KV + g, 1), :][:, 0:1], (WINDOW, 1))
                            for g in heads], axis=0)


def _attn_window(refs, hh):
    return jnp.concatenate([r[:, hh * HEAD_DIM:(hh + 1) * HEAD_DIM] for r in refs], axis=0)


def _attn_fwd(qn, kn, vb, sink_rows, n_ctx, name, ride=None):
    T = qn.shape[0]
    assert KV_PER_STEP_FWD == N_KV_HEADS
    nb, qspec, win, ctx, sink = _attn_specs(T, n_ctx, KV_PER_STEP_FWD)

    def body(q_ref, kp, kc, kx, vp, vc, vx, kctx_ref, vctx_ref, sink_ref, o_ref, lse_ref):
        i = pl.program_id(1)
        valid = _attn_valid(i, T, Q_PER_KV)
        for hh in range(KV_PER_STEP_FWD):
            sl = slice(hh * HEAD_DIM, (hh + 1) * HEAD_DIM)
            kw, vw = _attn_window((kp, kc, kx), hh), _attn_window((vp, vc, vx), hh)
            kctx, vctx = kctx_ref[:, sl], vctx_ref[:, sl]
            p_lat, p_ctx, den, m = _attn_probs(_stack_heads(q_ref, hh), kw, kctx, _stack_sinks(sink_ref, hh), valid)
            o = ((_dot(p_lat.astype(BF16), vw, NN) + _dot(p_ctx.astype(BF16), vctx, NN)) / den).astype(BF16)
            lse_ref[0, hh] = m + jnp.log(den)
            for g in range(Q_PER_KV):
                c0 = (hh * Q_PER_KV + g) * HEAD_DIM
                o_ref[:, c0:c0 + HEAD_DIM] = o[g * WINDOW:(g + 1) * WINDOW]

    return _pcall(
        body, name=name, grid=(1, nb),
        in_specs=[qspec] + win + win + [ctx, ctx, sink],
        out_specs=[qspec, pl.BlockSpec((1, N_KV_HEADS, Q_PER_KV * WINDOW, 1), lambda h, i: (i, 0, 0, 0))],
        out_shape=[jax.ShapeDtypeStruct(qn.shape, BF16),
                   jax.ShapeDtypeStruct((nb, N_KV_HEADS, Q_PER_KV * WINDOW, 1), F32)],
        sem=("parallel", "parallel"), args=(qn, kn, kn, kn, vb, vb, vb, kn, vb, sink_rows), ride=ride)


def _attn_bwd(qn, kn, vb, sink_rows, lse, do, n_ctx, name, ride=None):
    T = qn.shape[0]
    nb, qspec, win, ctx, sink = _attn_specs(T, n_ctx)
    scale = HEAD_DIM ** -0.5
    TP = T + 2 * WINDOW

    def body(q_ref, kp, kc, kx, vp, vc, vx, kctx_ref, vctx_ref, sink_ref, do_ref, lse_ref,
             dq_ref, dkw_ref, dvw_ref, dkc_ref, dvc_ref, dsn_ref):
        h, i = pl.program_id(0), pl.program_id(1)

        @pl.when(i == 0)
        def _():
            dkw_ref[...] = jnp.zeros_like(dkw_ref)
            dvw_ref[...] = jnp.zeros_like(dvw_ref)
            dkc_ref[...] = jnp.zeros_like(dkc_ref)
            dvc_ref[...] = jnp.zeros_like(dvc_ref)
            dsn_ref[...] = jnp.zeros_like(dsn_ref)

        lane = lax.broadcasted_iota(jnp.int32, (8, HEAD_DIM), 1)
        valid = _attn_valid(i, T, ATTN_BWD_STACK)
        rows = pl.ds(pl.multiple_of(i * WINDOW, WINDOW), 3 * WINDOW)
        for hh in range(KV_PER_STEP):
            sl = slice(hh * HEAD_DIM, (hh + 1) * HEAD_DIM)
            kw, vw = _attn_window((kp, kc, kx), hh), _attn_window((vp, vc, vx), hh)
            kctx, vctx = kctx_ref[:, sl], vctx_ref[:, sl]
            dsn = jnp.zeros((8, HEAD_DIM), F32)
            acc = [0.0, 0.0, 0.0, 0.0]
            for g0 in range(0, Q_PER_KV, ATTN_BWD_STACK):
                heads = range(g0, g0 + ATTN_BWD_STACK)
                q, d_o = _stack_heads(q_ref, hh, heads), _stack_heads(do_ref, hh, heads)
                p_lat, p_ctx, p_snk = _attn_probs_lse(
                    q, kw, kctx, _stack_sinks(sink_ref, h * KV_PER_STEP + hh, heads), valid,
                    lse_ref[0, hh, g0 * WINDOW:(g0 + ATTN_BWD_STACK) * WINDOW, :])
                dp_lat = _dot(d_o, vw, NT)
                dp_ctx = _dot(d_o, vctx, NT)
                dr = jnp.sum(p_lat * dp_lat, axis=-1, keepdims=True) + jnp.sum(p_ctx * dp_ctx, axis=-1, keepdims=True)
                ds_lat = (p_lat * (dp_lat - dr) * scale).astype(BF16)
                ds_ctx = (p_ctx * (dp_ctx - dr) * scale).astype(BF16)
                dq = _dot(ds_lat, kw, NN) + _dot(ds_ctx, kctx, NN)
                snk_terms = p_snk * dr
                for n, g in enumerate(heads):
                    c0 = (hh * Q_PER_KV + g) * HEAD_DIM
                    dq_ref[:, c0:c0 + HEAD_DIM] = dq[n * WINDOW:(n + 1) * WINDOW]
                    dsn = dsn + jnp.where(lane == g, -jnp.sum(snk_terms[n * WINDOW:(n + 1) * WINDOW], axis=0,
                                                               keepdims=True), 0.0)
                parts = (_dot(ds_lat, q, TN), _dot(p_lat.astype(BF16), d_o, TN), _dot(ds_ctx, q, TN),
                         _dot(p_ctx.astype(BF16), d_o, TN))
                acc = [a + b for a, b in zip(acc, parts)]
            dkw_ref[rows, sl] += acc[0]
            dvw_ref[rows, sl] += acc[1]
            dkc_ref[:, sl] += acc[2]
            dvc_ref[:, sl] += acc[3]
            dsn_ref[hh] += dsn

    wacc = pl.BlockSpec((TP, KV_PER_STEP * HEAD_DIM), lambda h, i: (0, h))
    return _pcall(
        body, name=name, grid=(N_KV_HEADS // KV_PER_STEP, nb),
        in_specs=[qspec] + win + win + [ctx, ctx, sink, qspec,
                                        pl.BlockSpec((1, KV_PER_STEP, Q_PER_KV * WINDOW, 1), lambda h, i: (i, h, 0, 0))],
        out_specs=[qspec, wacc, wacc, ctx, ctx, pl.BlockSpec((KV_PER_STEP, 8, HEAD_DIM), lambda h, i: (h, 0, 0))],
        out_shape=[jax.ShapeDtypeStruct(qn.shape, F32),
                   jax.ShapeDtypeStruct((TP, N_KV_HEADS * HEAD_DIM), F32),
                   jax.ShapeDtypeStruct((TP, N_KV_HEADS * HEAD_DIM), F32),
                   jax.ShapeDtypeStruct((n_ctx, N_KV_HEADS * HEAD_DIM), F32),
                   jax.ShapeDtypeStruct((n_ctx, N_KV_HEADS * HEAD_DIM), F32),
                   jax.ShapeDtypeStruct((N_KV_HEADS, 8, HEAD_DIM), F32)],
        sem=("arbitrary", "arbitrary"), args=(qn, kn, kn, kn, vb, vb, vb, kn, vb, sink_rows, do, lse), ride=ride)


def _gla_masks(dirv):
    C = GLA_CHUNK

    def times(reps):
        r = lax.broadcasted_iota(jnp.int32, (C, reps * C), 0)
        c = lax.broadcasted_iota(jnp.int32, (C, reps * C), 1) & (C - 1)
        return jnp.where(dirv == 0, r, C - 1 - r), jnp.where(dirv == 0, c, C - 1 - c)

    def level(tt, ss, m):
        sh = m.bit_length() - 1
        same = (tt >> (sh + 1)) == (ss >> (sh + 1))
        return same, (tt >> sh) & 1, (ss >> sh) & 1

    tt, ss = times(3)
    le = (ss <= tt).astype(jnp.int32)
    sums = [le == 1]
    for m in GLA_LEVELS:
        same, ut, us = level(tt, ss, m)
        sums.append(same & (ut == us) & (ut == le))
    tt, ss = times(1)
    blocks = [ss == tt]
    for m in GLA_LEVELS:
        same, ut, us = level(tt, ss, m)
        blocks.append(same & (ut == 1) & (us == 0))
    mall3 = jnp.concatenate([jnp.where(s, 1.0, 0.0) for s in sums], axis=0).astype(BF16)
    return mall3, blocks


def _pieces(x):
    hi = x.astype(BF16)
    r1 = x - hi.astype(F32)
    mid = r1.astype(BF16)
    return hi, mid, (r1 - mid.astype(F32)).astype(BF16)


def _sum_f32(mall3, x):
    return _dot(mall3, jnp.concatenate(_pieces(x), axis=0), NN)


def _sum_f32_t(mall3, x):
    m = mall3[:, 0:GLA_CHUNK]
    hi, mid, lo = _pieces(x)
    return _dot(m, hi, TN) + _dot(m, mid, TN) + _dot(m, lo, TN)


def _gla_chunk_of(dirv, j, lc, nc):
    return jnp.where(dirv == 0, j, jnp.where(j < lc, lc - 1 - j, nc + lc - 1 - j))


def _gla_gate(lr_ref, wg_ref, bg_ref, d=0):
    pre = _dot(lr_ref[...].astype(BF16), wg_ref[d].astype(BF16), NN) + bg_ref[d]
    g = (jnp.minimum(pre, 0.0) - jnp.log(1.0 + jnp.exp(-jnp.abs(pre)))) * (1.0 / GLA_GATE_NORM)
    return pre, g


def _gla_fwd(z, qblk, kblk, vblk, lrblk, wg, bg, DV, n_ctx, name):
    R = z.shape[0]
    C = GLA_CHUNK
    DK = wg.shape[2] // GLA_HEADS
    nc, lc = R // C, n_ctx // C
    qscale = DK ** -0.5

    GK, GV = GLA_HEADS * DK, GLA_HEADS * DV

    def body(qf, kf, vf, lrf, qb, kb, vb, lrb, wg_ref, bg_ref, of_ref, ob_ref, sp_ref, st_ref):
        @pl.when(pl.program_id(0) == 0)
        def _():
            st_ref[...] = jnp.zeros_like(st_ref)

        for d, (q_ref, k_ref, v_ref, lr_ref, o_ref) in enumerate(((qf, kf, vf, lrf, of_ref), (qb, kb, vb, lrb, ob_ref))):
            mall, blocks = _gla_masks(d)
            _, g_all = _gla_gate(lr_ref, wg_ref, bg_ref, d)
            E_all = _sum_f32(mall, g_all)
            for h in range(GLA_HEADS):
                ks, vs = slice(h * DK, (h + 1) * DK), slice(h * DV, (h + 1) * DV)
                q, k, v = q_ref[:, ks] * qscale, k_ref[:, ks], v_ref[:, vs].astype(BF16)
                g, E = g_all[:, ks], E_all[:, ks]
                st = st_ref[d, h]
                sp_ref[d, h, 0] = st
                A = jnp.where(blocks[0], _dot(q.astype(BF16), k.astype(BF16), NT), 0.0)
                for l in range(len(GLA_LEVELS)):
                    e = jnp.exp(E[(1 + l) * C:(2 + l) * C])
                    A = A + jnp.where(blocks[l + 1], _dot((q * e).astype(BF16), (k * e).astype(BF16), NT), 0.0)
                o_ref[:, vs] = (_dot((q * jnp.exp(E[0:C])).astype(BF16), st.astype(BF16), NT)
                                + _dot(A.astype(BF16), v, NN))
                last = jnp.sum(g, axis=0, keepdims=True)
                st_ref[d, h] = jnp.exp(last) * st + _dot(v, (k * jnp.exp(last - E[0:C])).astype(BF16), TN)

    def ins(d):
        chunk = lambda j: _gla_chunk_of(d, j, lc, nc)
        return [pl.BlockSpec((C, GK), lambda j: (chunk(j), qblk)), pl.BlockSpec((C, GK), lambda j: (chunk(j), kblk)),
                pl.BlockSpec((C, GV), lambda j: (chunk(j), vblk)), pl.BlockSpec((C, LANES), lambda j: (chunk(j), lrblk))]

    return pl.pallas_call(
        body, name=name, grid=(nc,),
        in_specs=ins(0) + ins(1) + [pl.BlockSpec((2, LANES, GK), lambda j: (0, 0, 0)),
                                    pl.BlockSpec((2, 1, GK), lambda j: (0, 0, 0))],
        out_specs=[pl.BlockSpec((C, GV), lambda j: (_gla_chunk_of(0, j, lc, nc), 0)),
                   pl.BlockSpec((C, GV), lambda j: (_gla_chunk_of(1, j, lc, nc), 0)),
                   pl.BlockSpec((2, GLA_HEADS, 1, DV, DK), lambda j: (0, 0, j, 0, 0))],
        out_shape=[jax.ShapeDtypeStruct((R, GV), F32), jax.ShapeDtypeStruct((R, GV), F32),
                   jax.ShapeDtypeStruct((2, GLA_HEADS, nc, DV, DK), F32)],
        scratch_shapes=[pltpu.VMEM((2, GLA_HEADS, DV, DK), F32)],
        compiler_params=_cparams("arbitrary"),
    )(z, z, z, z, z, z, z, z, wg, bg)


def _gla_bwd(z, qblk, kblk, vblk, lrblk, wg, bg, sprev, do, n_ctx, name, ride=None):
    R = z.shape[0]
    C = GLA_CHUNK
    DK, DV = wg.shape[2] // GLA_HEADS, do.shape[1] // GLA_HEADS
    nc, lc = R // C, n_ctx // C
    qscale = DK ** -0.5
    nl = len(GLA_LEVELS)

    GK, GV = GLA_HEADS * DK, GLA_HEADS * DV

    def body(qf, kf, vf, lrf, dof, qb_, kb_, vb_, lrb, dob, wg_ref, bg_ref, sp_ref,
             dqf, dkf, dvf, dpf, dqb, dkb, dvb, dpb, dbg_ref, dst_ref):
        @pl.when(pl.program_id(0) == 0)
        def _():
            dst_ref[...] = jnp.zeros_like(dst_ref)
            dbg_ref[...] = jnp.zeros_like(dbg_ref)

        sides = ((qf, kf, vf, lrf, dof, dqf, dkf, dvf, dpf), (qb_, kb_, vb_, lrb, dob, dqb, dkb, dvb, dpb))
        for d, (q_ref, k_ref, v_ref, lr_ref, do_ref, dq_ref, dk_ref, dv_ref, dpre_ref) in enumerate(sides):
            mall, blocks = _gla_masks(d)
            pre_all, g_all = _gla_gate(lr_ref, wg_ref, bg_ref, d)
            E_all = _sum_f32(mall, g_all)
            for h in range(GLA_HEADS):
                ks, vs = slice(h * DK, (h + 1) * DK), slice(h * DV, (h + 1) * DV)
                q, k, v = q_ref[:, ks] * qscale, k_ref[:, ks], v_ref[:, vs].astype(BF16)
                pre, g, E = pre_all[:, ks], g_all[:, ks], E_all[:, ks]
                last = jnp.sum(g, axis=0, keepdims=True)
                eb, er, decay = jnp.exp(E[0:C]), jnp.exp(last - E[0:C]), jnp.exp(last)
                st = sp_ref[d, h, 0]
                dst = dst_ref[d, h]
                d_o = do_ref[:, vs]
                qe, kd = q * eb, k * er
                qb, kb = q.astype(BF16), k.astype(BF16)
                A = jnp.where(blocks[0], _dot(qb, kb, NT), 0.0)
                levels = []
                for l in range(nl):
                    e = jnp.exp(E[(1 + l) * C:(2 + l) * C])
                    ql, kl = q * e, k * e
                    levels.append((e, ql, kl, ql.astype(BF16), kl.astype(BF16)))
                    A = A + jnp.where(blocks[l + 1], _dot(levels[l][3], levels[l][4], NT), 0.0)
                dA = _dot(d_o, v, NT)
                dv_ref[:, vs] = (_dot(A.astype(BF16), d_o, TN)
                                 + _dot(kd.astype(BF16), dst.astype(BF16), NT)).astype(BF16)
                dqe = _dot(d_o, st.astype(BF16), NN)
                dkd = _dot(v, dst.astype(BF16), NN)
                G = jnp.where(blocks[0], dA, 0.0).astype(BF16)
                dq = dqe * eb + _dot(G, kb, NN)
                dk = dkd * er + _dot(G, qb, TN)
                dEr = dkd * kd
                dE = [dqe * qe - dEr]
                for l in range(nl):
                    e, ql, kl, qlb, klb = levels[l]
                    G = jnp.where(blocks[l + 1], dA, 0.0).astype(BF16)
                    dql = _dot(G, klb, NN)
                    dkl = _dot(G, qlb, TN)
                    dq = dq + dql * e
                    dk = dk + dkl * e
                    dE.append(dql * ql + dkl * kl)
                dlast = jnp.sum(dst * st, axis=0, keepdims=True) * decay + jnp.sum(dEr, axis=0, keepdims=True)
                dg = _sum_f32_t(mall, jnp.concatenate(dE, axis=0)) + dlast
                dpre = dg * (1.0 / GLA_GATE_NORM) / (1.0 + jnp.exp(pre))
                dq_ref[:, ks] = (dq * qscale).astype(BF16)
                dk_ref[:, ks] = dk.astype(BF16)
                dpre_ref[:, ks] = dpre.astype(BF16)
                dbg_ref[d, :, ks] += jnp.sum(dpre, axis=0, keepdims=True)
                dst_ref[d, h] = decay * dst + _dot(d_o, qe.astype(BF16), TN)

    def ins(d):
        chunk = lambda j: _gla_chunk_of(d, nc - 1 - j, lc, nc)
        return [pl.BlockSpec((C, GK), lambda j: (chunk(j), qblk)), pl.BlockSpec((C, GK), lambda j: (chunk(j), kblk)),
                pl.BlockSpec((C, GV), lambda j: (chunk(j), vblk)), pl.BlockSpec((C, LANES), lambda j: (chunk(j), lrblk)),
                pl.BlockSpec((C, GV), lambda j: (chunk(j), 0))]

    def outs(d):
        chunk = lambda j: _gla_chunk_of(d, nc - 1 - j, lc, nc)
        return [pl.BlockSpec((C, GK), lambda j: (chunk(j), 0)), pl.BlockSpec((C, GK), lambda j: (chunk(j), 0)),
                pl.BlockSpec((C, GV), lambda j: (chunk(j), 0)), pl.BlockSpec((C, GK), lambda j: (chunk(j), 0))]

    side_shapes = [jax.ShapeDtypeStruct((R, GK), BF16), jax.ShapeDtypeStruct((R, GK), BF16),
                   jax.ShapeDtypeStruct((R, GV), BF16), jax.ShapeDtypeStruct((R, GK), BF16)]
    return _pcall(
        body, name=name, grid=(nc,),
        in_specs=ins(0) + ins(1) + [pl.BlockSpec((2, LANES, GK), lambda j: (0, 0, 0)),
                                    pl.BlockSpec((2, 1, GK), lambda j: (0, 0, 0)),
                                    pl.BlockSpec((2, GLA_HEADS, 1, DV, DK), lambda j: (0, 0, nc - 1 - j, 0, 0))],
        out_specs=outs(0) + outs(1) + [pl.BlockSpec((2, 1, GK), lambda j: (0, 0, 0))],
        out_shape=side_shapes + side_shapes + [jax.ShapeDtypeStruct((2, 1, GK), F32)],
        scratch_shapes=[pltpu.VMEM((2, GLA_HEADS, DV, DK), F32)],
        sem=("arbitrary",), args=(z, z, z, z, do, z, z, z, z, do, wg, bg, sprev), ride=ride)


def _glanorm_fwd(of, ob, z, rbblk, gn, n_ctx, name):
    R, GV = of.shape
    T = R - n_ctx
    DV = GV // GLA_HEADS
    tm = _pick(n_ctx, 256, 8)
    ro = n_ctx // tm

    def body(o0_ref, o1_ref, rb_ref, gn_ref, p_ref):
        gnv = gn_ref[...]
        for h in range(GLA_HEADS):
            sl = slice(h * DV, (h + 1) * DV)
            og = o0_ref[:, sl] + o1_ref[:, sl]
            r = lax.rsqrt(jnp.mean(og * og, axis=-1, keepdims=True) + EPS)
            p_ref[:, sl] = (og * r * gnv * _silu(rb_ref[:, sl])).astype(BF16)

    return pl.pallas_call(
        body, name=name, grid=(T // tm,),
        in_specs=[pl.BlockSpec((tm, GV), lambda i: (i + ro, 0)), pl.BlockSpec((tm, GV), lambda i: (i + ro, 0)),
                  pl.BlockSpec((tm, GV), lambda i: (i + ro, rbblk)), pl.BlockSpec((1, DV), lambda i: (0, 0))],
        out_specs=pl.BlockSpec((tm, GV), lambda i: (i, 0)),
        out_shape=jax.ShapeDtypeStruct((T, GV), BF16), compiler_params=_cparams("parallel"))(of, ob, z, gn)


def _glanorm_bwd(of, ob, z, rbblk, gn, dp, n_ctx, name):
    R, GV = of.shape
    T = R - n_ctx
    DV = GV // GLA_HEADS
    tm = _pick(n_ctx, 256, 8)
    ro = n_ctx // tm

    def body(o0_ref, o1_ref, rb_ref, gn_ref, dp_ref, do_ref, drb_ref, acc_ref):
        i = pl.program_id(0)

        @pl.when(i == 0)
        def _():
            acc_ref[...] = jnp.zeros_like(acc_ref)

        gnv = gn_ref[...]
        dgn = jnp.zeros((1, DV), F32)
        for h in range(GLA_HEADS):
            sl = slice(h * DV, (h + 1) * DV)
            og = o0_ref[:, sl] + o1_ref[:, sl]
            rb = rb_ref[:, sl]
            d = dp_ref[:, sl]
            r = lax.rsqrt(jnp.mean(og * og, axis=-1, keepdims=True) + EPS)
            xh = og * r
            drb_ref[:, sl] = (d * xh * gnv * _dsilu(rb)).astype(BF16)
            dn = d * _silu(rb)
            dgn = dgn + jnp.sum(dn * xh, axis=0, keepdims=True)
            dxh = dn * gnv
            do_ref[:, sl] = (r * (dxh - xh * jnp.mean(dxh * xh, axis=-1, keepdims=True))).astype(BF16)
        acc_ref[0:1, :] += dgn

    row = pl.BlockSpec((tm, GV), lambda i: (i, 0))
    return pl.pallas_call(
        body, name=name, grid=(T // tm,),
        in_specs=[pl.BlockSpec((tm, GV), lambda i: (i + ro, 0)), pl.BlockSpec((tm, GV), lambda i: (i + ro, 0)),
                  pl.BlockSpec((tm, GV), lambda i: (i + ro, rbblk)), pl.BlockSpec((1, DV), lambda i: (0, 0)), row],
        out_specs=[row, row, pl.BlockSpec((8, DV), lambda i: (0, 0))],
        out_shape=[jax.ShapeDtypeStruct((T, GV), BF16), jax.ShapeDtypeStruct((T, GV), BF16),
                   jax.ShapeDtypeStruct((8, DV), F32)],
        compiler_params=_cparams("arbitrary"))(of, ob, z, gn, dp)


def _gate_fwd(z, gablk, gbblk, ya, yg, n_ctx, name):
    T, D = ya.shape
    tm = _pick(n_ctx, 256, 8)
    ro = n_ctx // tm

    def body(ga_ref, gb_ref, ya_ref, yg_ref, m_ref):
        m_ref[...] = (_sigmoid(ga_ref[...]) * ya_ref[...] + _sigmoid(gb_ref[...]) * yg_ref[...]).astype(BF16)

    row = pl.BlockSpec((tm, D), lambda i: (i, 0))
    return pl.pallas_call(
        body, name=name, grid=(T // tm,),
        in_specs=[pl.BlockSpec((tm, D), lambda i: (i + ro, gablk)), pl.BlockSpec((tm, D), lambda i: (i + ro, gbblk)), row, row],
        out_specs=row, out_shape=jax.ShapeDtypeStruct((T, D), BF16), compiler_params=_cparams("parallel"))(z, z, ya, yg)


def _gate_bwd(z, gablk, gbblk, ya, yg, dm, n_ctx, name):
    T, D = ya.shape
    tm = _pick(n_ctx, 256, 8)
    ro = n_ctx // tm

    def body(ga_ref, gb_ref, ya_ref, yg_ref, dm_ref, dya_ref, dyg_ref, dga_ref, dgb_ref):
        d = dm_ref[...]
        sa, sb = _sigmoid(ga_ref[...]), _sigmoid(gb_ref[...])
        dya_ref[...] = (d * sa).astype(BF16)
        dyg_ref[...] = (d * sb).astype(BF16)
        dga_ref[...] = (d * ya_ref[...] * sa * (1.0 - sa)).astype(BF16)
        dgb_ref[...] = (d * yg_ref[...] * sb * (1.0 - sb)).astype(BF16)

    row = pl.BlockSpec((tm, D), lambda i: (i, 0))
    sh = jax.ShapeDtypeStruct((T, D), BF16)
    return pl.pallas_call(
        body, name=name, grid=(T // tm,),
        in_specs=[pl.BlockSpec((tm, D), lambda i: (i + ro, gablk)), pl.BlockSpec((tm, D), lambda i: (i + ro, gbblk)), row, row, row],
        out_specs=[row] * 4, out_shape=[sh] * 4, compiler_params=_cparams("parallel"))(z, z, ya, yg, dm)


def _resnorm_fwd(x, mix, gt, g, sc, sh, name):
    T, D = x.shape
    tm = _pick(T, 256, 8)

    def body(x_ref, mix_ref, gt_ref, g_ref, sc_ref, sh_ref, x1_ref, h_ref):
        x1 = x_ref[...] + gt_ref[...] * mix_ref[...]
        x1_ref[...] = x1
        r = lax.rsqrt(jnp.mean(x1 * x1, axis=-1, keepdims=True) + EPS)
        h_ref[...] = (x1 * r * g_ref[...] * (1.0 + sc_ref[...]) + sh_ref[...]).astype(BF16)

    row = pl.BlockSpec((tm, D), lambda i: (i, 0))
    vec = pl.BlockSpec((1, D), lambda i: (0, 0))
    return pl.pallas_call(
        body, name=name, grid=(T // tm,), in_specs=[row, row, vec, vec, vec, vec], out_specs=[row, row],
        out_shape=[jax.ShapeDtypeStruct((T, D), F32), jax.ShapeDtypeStruct((T, D), BF16)],
        compiler_params=_cparams("parallel"))(x, mix, gt, g, sc, sh)


def _loss_head(d, x1, gt, target, name):
    T, D = d.shape
    tm = _pick(T, 256, 8)

    def body(d_ref, x1_ref, gt_ref, t_ref, dy_ref, dd_ref, acc_ref):
        i = pl.program_id(0)

        @pl.when(i == 0)
        def _():
            acc_ref[...] = jnp.zeros_like(acc_ref)

        dv, gtv = d_ref[...], gt_ref[...]
        e = x1_ref[...] + gtv * dv - t_ref[...]
        dy = e * (1.0 / D)
        dy_ref[...] = dy
        dd_ref[...] = (dy * gtv).astype(BF16)
        acc_ref[0:1, :] += jnp.sum(e * e, axis=0, keepdims=True)
        acc_ref[1:2, :] += jnp.sum(dy * dv, axis=0, keepdims=True)

    row = pl.BlockSpec((tm, D), lambda i: (i, 0))
    return pl.pallas_call(
        body, name=name, grid=(T // tm,), in_specs=[row, row, pl.BlockSpec((1, D), lambda i: (0, 0)), row],
        out_specs=[row, row, pl.BlockSpec((8, D), lambda i: (0, 0))],
        out_shape=[jax.ShapeDtypeStruct((T, D), F32), jax.ShapeDtypeStruct((T, D), BF16),
                   jax.ShapeDtypeStruct((8, D), F32)],
        compiler_params=_cparams("arbitrary"))(d, x1, gt, target)


def _halo_specs(T, tm, tw, col_of, order):
    n8 = tm // 8
    if order == "ij":
        mid = lambda i, j: (i, col_of(j))
        prev = lambda i, j: (jnp.maximum(i * n8 - 1, 0), col_of(j))
        nxt = lambda i, j: (jnp.minimum((i + 1) * n8, T // 8 - 1), col_of(j))
    else:
        mid = lambda j, i: (i, col_of(j))
        prev = lambda j, i: (jnp.maximum(i * n8 - 1, 0), col_of(j))
        nxt = lambda j, i: (jnp.minimum((i + 1) * n8, T // 8 - 1), col_of(j))
    return [pl.BlockSpec((tm, tw), mid), pl.BlockSpec((8, tw), prev), pl.BlockSpec((8, tw), nxt)]


def _shift_rows(x, before, after):
    tm = x.shape[0]
    row = lax.broadcasted_iota(jnp.int32, x.shape, 0)
    return (jnp.where(row == 0, before, pltpu.roll(x, 1, 0)),
            jnp.where(row == tm - 1, after, pltpu.roll(x, tm - 1, 0)))


def _conv_fwd(u, cw, cb, name):
    T, F2 = u.shape
    F = F2 // 2
    tm, tw = _pick(T, 256, 8), _pick(F, 512)
    nt, nw = T // tm, F // tw

    def body(ua, uap, uan, ug, ugp, ugn, cwa, cwg, cba, cbg, f_ref):
        i = pl.program_id(0)
        first, last = i == 0, i == nt - 1

        def conv(u_ref, up_ref, un_ref, w_ref, b_ref):
            m = u_ref[...]
            p, n = _shift_rows(m, jnp.where(first, 0.0, up_ref[7:8, :]), jnp.where(last, 0.0, un_ref[0:1, :]))
            return p * w_ref[0:1, :] + m * w_ref[1:2, :] + n * w_ref[2:3, :] + b_ref[...]

        a = conv(ua, uap, uan, cwa, cba)
        g = conv(ug, ugp, ugn, cwg, cbg)
        f_ref[...] = (_silu(a) * g).astype(BF16)

    wspec = lambda off: pl.BlockSpec((3, tw), lambda i, j: (0, j + off))
    bspec = lambda off: pl.BlockSpec((1, tw), lambda i, j: (0, j + off))
    return pl.pallas_call(
        body, name=name, grid=(nt, nw),
        in_specs=_halo_specs(T, tm, tw, lambda j: j, "ij") + _halo_specs(T, tm, tw, lambda j: j + nw, "ij")
        + [wspec(0), wspec(nw), bspec(0), bspec(nw)],
        out_specs=pl.BlockSpec((tm, tw), lambda i, j: (i, j)),
        out_shape=jax.ShapeDtypeStruct((T, F), BF16),
        compiler_params=_cparams("parallel", "parallel"),
    )(u, u, u, u, u, u, cw, cw, cb, cb)


def _conv_bwd(u, df, cw, cb, name):
    T, F2 = u.shape
    F = F2 // 2
    tm, tw = _pick(T, 256, 8), _pick(F, 512)
    nt, nw = T // tm, F // tw

    def body(ua, uap, uan, ug, ugp, ugn, cwa, cwg, cba, cbg, df_ref, dfp, dfn, du_ref, acca_ref, accg_ref):
        i = pl.program_id(1)

        @pl.when(i == 0)
        def _():
            acca_ref[...] = jnp.zeros_like(acca_ref)
            accg_ref[...] = jnp.zeros_like(accg_ref)

        first, last = i == 0, i == nt - 1
        wa, wg, ba, bg = cwa[...], cwg[...], cba[...], cbg[...]

        def conv(p, m, n, w, b):
            return p * w[0:1] + m * w[1:2] + n * w[2:3] + b

        def grads(a, g, d):
            return d * g * _dsilu(a), d * _silu(a)

        xa, xg, d = ua[...], ug[...], df_ref[...]
        sa = _shift_rows(xa, jnp.where(first, 0.0, uap[7:8, :]), jnp.where(last, 0.0, uan[0:1, :]))
        sg = _shift_rows(xg, jnp.where(first, 0.0, ugp[7:8, :]), jnp.where(last, 0.0, ugn[0:1, :]))
        da, dg = grads(conv(sa[0], xa, sa[1], wa, ba), conv(sg[0], xg, sg[1], wg, bg), d)
        da_p, dg_p = grads(conv(uap[6:7, :], uap[7:8, :], xa[0:1], wa, ba),
                           conv(ugp[6:7, :], ugp[7:8, :], xg[0:1], wg, bg), dfp[7:8, :])
        da_n, dg_n = grads(conv(xa[tm - 1:tm], uan[0:1, :], uan[1:2, :], wa, ba),
                           conv(xg[tm - 1:tm], ugn[0:1, :], ugn[1:2, :], wg, bg), dfn[0:1, :])
        ta = _shift_rows(da, jnp.where(first, 0.0, da_p), jnp.where(last, 0.0, da_n))
        tg = _shift_rows(dg, jnp.where(first, 0.0, dg_p), jnp.where(last, 0.0, dg_n))
        du_ref[0] = (ta[1] * wa[0:1] + da * wa[1:2] + ta[0] * wa[2:3]).astype(BF16)
        du_ref[1] = (tg[1] * wg[0:1] + dg * wg[1:2] + tg[0] * wg[2:3]).astype(BF16)
        for t, (va, vg) in enumerate(((sa[0], sg[0]), (xa, xg), (sa[1], sg[1]))):
            acca_ref[t:t + 1, :] += jnp.sum(da * va, axis=0, keepdims=True)
            accg_ref[t:t + 1, :] += jnp.sum(dg * vg, axis=0, keepdims=True)
        acca_ref[3:4, :] += jnp.sum(da, axis=0, keepdims=True)
        accg_ref[3:4, :] += jnp.sum(dg, axis=0, keepdims=True)

    wspec = lambda off: pl.BlockSpec((3, tw), lambda j, i: (0, j + off))
    bspec = lambda off: pl.BlockSpec((1, tw), lambda j, i: (0, j + off))
    row = pl.BlockSpec((tm, tw), lambda j, i: (i, j))
    acc = pl.BlockSpec((8, tw), lambda j, i: (0, j))
    return pl.pallas_call(
        body, name=name, grid=(nw, nt),
        in_specs=_halo_specs(T, tm, tw, lambda j: j, "ji") + _halo_specs(T, tm, tw, lambda j: j + nw, "ji")
        + [wspec(0), wspec(nw), bspec(0), bspec(nw)] + _halo_specs(T, tm, tw, lambda j: j, "ji"),
        out_specs=[pl.BlockSpec((2, tm, tw), lambda j, i: (0, i, j)), acc, acc],
        out_shape=[jax.ShapeDtypeStruct((2, T, F), BF16),
                   jax.ShapeDtypeStruct((8, F), F32), jax.ShapeDtypeStruct((8, F), F32)],
        compiler_params=_cparams("parallel", "arbitrary"),
    )(u, u, u, u, u, u, cw, cw, cb, cb, df, df, df)


def _assemble_dz(lay, z_used, Z, n_ctx, dqa, drb, dga, dgb, dka, dva, dvg, dqg, dkg, dlr, name):
    T = dqa.shape[0]
    R = T + n_ctx
    tm = _pick(n_ctx, 128, 8)
    cb = n_ctx // tm

    def body(dqa_ref, drb_ref, dga_ref, dgb_ref, dka_ref, dva_ref, dvg0, dvg1, dqg0, dqg1, dkg0, dkg1, dlr_ref, o_ref):
        lat = pl.program_id(0) >= cb

        def put(seg, val):
            o_ref[:, lay[seg]:lay[seg] + val.shape[1]] = val.astype(BF16)

        def lat_only(ref):
            v = ref[...]
            return jnp.where(lat, v, jnp.zeros_like(v))

        put("qa", lat_only(dqa_ref))
        put("rb", lat_only(drb_ref))
        put("ga", lat_only(dga_ref))
        put("gb", lat_only(dgb_ref))
        put("ka", dka_ref[...])
        put("va", dva_ref[...])
        put("vb", dvg0[...].astype(F32) + dvg1[...].astype(F32))
        put("qb", dqg0[...].astype(F32) + dqg1[...].astype(F32))
        put("kb", dkg0[...].astype(F32) + dkg1[...].astype(F32))
        put("lr", dlr_ref[...])
        if Z > z_used:
            o_ref[:, z_used:] = jnp.zeros((tm, Z - z_used), BF16)

    lat_spec = lambda a: pl.BlockSpec((tm, a.shape[1]), lambda i: (jnp.maximum(i - cb, 0), 0))
    all_spec = lambda a: pl.BlockSpec((tm, a.shape[1]), lambda i: (i, 0))
    dir_specs = lambda pair: [all_spec(pair[0]), all_spec(pair[1])]
    return pl.pallas_call(
        body, name=name, grid=(R // tm,),
        in_specs=[lat_spec(dqa), lat_spec(drb), lat_spec(dga), lat_spec(dgb), all_spec(dka), all_spec(dva)]
        + dir_specs(dvg) + dir_specs(dqg) + dir_specs(dkg) + [all_spec(dlr)],
        out_specs=pl.BlockSpec((tm, Z), lambda i: (i, 0)),
        out_shape=jax.ShapeDtypeStruct((R, Z), BF16), compiler_params=_cparams("parallel"),
    )(dqa, drb, dga, dgb, dka, dva, *dvg, *dqg, *dkg, dlr)


def _mod_fwd(ca, w, b, name):
    n, D = ca.shape
    N = w.shape[1]
    tn = _pick(N, 512)

    def body(c_ref, w_ref, b_ref, o_ref, s_ref):
        s = _silu(c_ref[...])
        s_ref[...] = s
        o_ref[...] = _dot(s.astype(BF16), w_ref[...].astype(BF16), NN) + b_ref[...]

    return pl.pallas_call(
        body, name=name, grid=(N // tn,),
        in_specs=[pl.BlockSpec((n, D), lambda j: (0, 0)), pl.BlockSpec((D, tn), lambda j: (0, j)),
                  pl.BlockSpec((1, tn), lambda j: (0, j))],
        out_specs=[pl.BlockSpec((n, tn), lambda j: (0, j)), pl.BlockSpec((n, D), lambda j: (0, 0))],
        out_shape=[jax.ShapeDtypeStruct((n, N), F32), jax.ShapeDtypeStruct((n, D), F32)],
        compiler_params=_cparams("arbitrary"))(ca, w, b)


def _silu_bwd(dsil, ca, name):
    def body(d_ref, c_ref, o_ref):
        o_ref[...] = d_ref[...] * _dsilu(c_ref[...])

    return pl.pallas_call(body, name=name, out_shape=jax.ShapeDtypeStruct(ca.shape, F32))(dsil, ca)


def _adam_math(w, g, m, v):
    c1 = 1.0 - ADAM_B1 ** ADAM_STEP
    c2 = 1.0 - ADAM_B2 ** ADAM_STEP
    mn = ADAM_B1 * m + (1.0 - ADAM_B1) * g
    vn = ADAM_B2 * v + (1.0 - ADAM_B2) * (g * g)
    return -ADAM_LR * ((mn / c1) / (jnp.sqrt(vn / c2) + ADAM_EPS) + ADAM_WD * w), mn, vn


def _adamw(w, g, m, v, name, ride=None):
    Rw, Cw = w.shape
    tr = _pick(Rw, 128, 8)

    def body(w_ref, g_ref, m_ref, v_ref, d_ref, mo_ref, vo_ref):
        d_ref[...], mo_ref[...], vo_ref[...] = _adam_math(w_ref[...], g_ref[...], m_ref[...], v_ref[...])

    row = pl.BlockSpec((tr, Cw), lambda i: (i, 0))
    sh = jax.ShapeDtypeStruct((Rw, Cw), F32)
    return _pcall(body, name=name, grid=(Rw // tr,), in_specs=[row] * 4, out_specs=[row] * 3, out_shape=[sh] * 3,
                  sem=("parallel",), args=(w, g, m, v), ride=ride)


HBM_SPEC = pl.BlockSpec(memory_space=pltpu.HBM)


def _exchange(inputs, out_shapes, stages, name):
    n_in, n_out = len(inputs), len(out_shapes)
    n = sum(len(s) for s in stages)

    def body(*refs):
        ins, outs = refs[:n_in], refs[n_in:n_in + n_out]
        send_sems, recv_sems = refs[n_in + n_out:]
        k = 0
        for stage in stages:
            copies = _stage_copies(stage, ins, outs, send_sems, recv_sems, k)
            for cp in copies:
                cp.start()
            for cp in copies:
                cp.wait()
            k += len(stage)

    return pl.pallas_call(
        body, name=name, in_specs=[HBM_SPEC] * n_in, out_specs=[HBM_SPEC] * n_out, out_shape=out_shapes,
        scratch_shapes=[pltpu.SemaphoreType.DMA((n,)), pltpu.SemaphoreType.DMA((n,))],
    )(*inputs)


def _stage_copies(stage, ins, outs, send_sems, recv_sems, k0=0):
    me = (lax.axis_index("x"), lax.axis_index("y"), lax.axis_index("c"))
    copies = []
    for k, ((skind, sidx), sfn, didx, dfn, flip) in enumerate(stage):
        src = (ins if skind == "in" else outs)[sidx].at[sfn(*me)]
        dst = outs[didx].at[dfn(*me)]
        if flip == (0, 0, 0):
            copies.append(pltpu.make_async_copy(src, dst, send_sems.at[k0 + k]))
        else:
            peer = tuple(1 - a if f else a for a, f in zip(me, flip))
            copies.append(pltpu.make_async_remote_copy(src, dst, send_sems.at[k0 + k], recv_sems.at[k0 + k],
                                                       device_id=peer, device_id_type=MESH))
    return copies


def _pcall(body, *, name, grid, in_specs, out_specs, out_shape, scratch_shapes=(), sem, args, ride=None):
    many = isinstance(out_shape, (list, tuple))
    out_specs, out_shape = (list(out_specs), list(out_shape)) if many else ([out_specs], [out_shape])
    if ride is None:
        res = pl.pallas_call(body, name=name, grid=grid, in_specs=list(in_specs), out_specs=out_specs,
                             out_shape=out_shape, scratch_shapes=list(scratch_shapes),
                             compiler_params=_cparams(*sem))(*args)
        return res if many else res[0]
    x_in, x_out, stage, aliases = ride
    n_in, n_out, n_scr, n_xin, n_xout = len(in_specs), len(out_specs), len(scratch_shapes), len(x_in), len(x_out)

    def wrapped(*refs):
        ins, xins = refs[:n_in], refs[n_in:n_in + n_xin]
        o0 = n_in + n_xin
        outs, xouts = refs[o0:o0 + n_out], refs[o0 + n_out:o0 + n_out + n_xout]
        s0 = o0 + n_out + n_xout
        scr, (send_sems, recv_sems) = refs[s0:s0 + n_scr], refs[s0 + n_scr:]
        first = functools.reduce(jnp.logical_and, [pl.program_id(d) == 0 for d in range(len(grid))])
        last = functools.reduce(jnp.logical_and, [pl.program_id(d) == grid[d] - 1 for d in range(len(grid))])

        @pl.when(first)
        def _():
            for cp in _stage_copies(stage, xins, xouts, send_sems, recv_sems):
                cp.start()

        body(*ins, *outs, *scr)

        @pl.when(last)
        def _():
            for cp in _stage_copies(stage, xins, xouts, send_sems, recv_sems):
                cp.wait()

    res = pl.pallas_call(
        wrapped, name=name, grid=grid, in_specs=list(in_specs) + [HBM_SPEC] * n_xin,
        out_specs=out_specs + [HBM_SPEC] * n_xout, out_shape=out_shape + list(x_out),
        scratch_shapes=list(scratch_shapes) + [pltpu.SemaphoreType.DMA((len(stage),)),
                                               pltpu.SemaphoreType.DMA((len(stage),))],
        input_output_aliases={n_in + a: n_out + b for a, b in aliases.items()},
        compiler_params=_cparams(*(["arbitrary"] * len(grid))))(*args, *x_in)
    main = res[:n_out]
    return (main if many else main[0]), list(res[n_out:])


FLIPS_ALL = [(0, 0, 1), (0, 1, 0), (0, 1, 1), (1, 0, 0), (1, 0, 1), (1, 1, 0), (1, 1, 1)]
FLIPS_CHIP = [(0, 1, 0), (1, 0, 0), (1, 1, 0)]


def _sum_slots(buf, name):
    n, r, w = buf.shape
    tr = _pick(r, 256, 8)

    def body(b_ref, o_ref):
        acc = b_ref[0]
        for s in range(1, n):
            acc = acc + b_ref[s]
        o_ref[...] = acc

    return pl.pallas_call(
        body, name=name, grid=(r // tr,), in_specs=[pl.BlockSpec((n, tr, w), lambda i: (0, i, 0))],
        out_specs=pl.BlockSpec((tr, w), lambda i: (i, 0)), out_shape=jax.ShapeDtypeStruct((r, w), F32),
        compiler_params=_cparams("parallel"))(buf)


def _allreduce_plan(buf):
    whole = lambda x, y, c: (slice(None), slice(None))
    slot = lambda x, y, c: (4 * x + 2 * y + c,)
    stage = [(("in", 0), whole, 0, slot, f) for f in [(0, 0, 0)] + FLIPS_ALL]
    return [jax.ShapeDtypeStruct((8,) + buf.shape, F32)], stage


def _allreduce(buf, name):
    shapes, stage = _allreduce_plan(buf)
    (slots,) = _exchange([buf], shapes, [stage], name + "_x")
    return _sum_slots(slots, name + "_sum")


def _gather_plan(shards, src):
    half = lambda a, c: pl.ds(c * (a.shape[0] // 2), a.shape[0] // 2)
    first, second = [], []
    for n, a in enumerate(shards):
        for f in FLIPS_CHIP:
            first.append((("in", n), lambda x, y, c, a=a: (half(a, c), slice(None)), n,
                          lambda x, y, c, a=a: (2 * x + y, half(a, c), slice(None)), f))
            peer_slot = lambda x, y, c, a=a, f=f: (2 * (x ^ f[0]) + (y ^ f[1]), half(a, c), slice(None))
            second.append(((src, n), peer_slot, n, peer_slot, (0, 0, 1)))
    outs = [jax.ShapeDtypeStruct((4,) + a.shape, a.dtype) for a in shards]
    return first, second, outs


def _allgather_weights(shards, name):
    first, second, outs = _gather_plan(shards, "out")
    return _exchange(shards, outs, [first, second], name)


def _place_own(buf, shard, svec, name):
    _, Rs, Cs = buf.shape
    tr = _pick(Rs, 256, 16)

    def body(s_ref, buf_ref, sh_ref, o_ref):
        o_ref[0] = sh_ref[...]

    grid_spec = pltpu.PrefetchScalarGridSpec(
        num_scalar_prefetch=1, grid=(Rs // tr,),
        in_specs=[pl.BlockSpec(memory_space=pl.ANY), pl.BlockSpec((tr, Cs), lambda i, s: (i, 0))],
        out_specs=pl.BlockSpec((1, tr, Cs), lambda i, s: (s[0], i, 0)))
    return pl.pallas_call(body, name=name, grid_spec=grid_spec, out_shape=jax.ShapeDtypeStruct(buf.shape, buf.dtype),
                          input_output_aliases={1: 0}, compiler_params=_cparams("arbitrary"))(svec, buf, shard)


def _add_pair(G, bufA, cvec, name):
    _, Rs, Cs = G.shape
    Rh = Rs // 2
    tr = _pick(Rh, 128, 16)
    nb = Rh // tr

    def body(c_ref, g_ref, a_ref, o_ref):
        o_ref[...] = (g_ref[...] + a_ref[...]).astype(BF16)

    grid_spec = pltpu.PrefetchScalarGridSpec(
        num_scalar_prefetch=1, grid=(4, nb),
        in_specs=[pl.BlockSpec((1, tr, Cs), lambda s, i, c_ref: (s, c_ref[0] * nb + i, 0)),
                  pl.BlockSpec((1, tr, Cs), lambda s, i, c_ref: (s, i, 0))],
        out_specs=pl.BlockSpec((1, tr, Cs), lambda s, i, c_ref: (s, i, 0)))
    return pl.pallas_call(body, name=name, grid_spec=grid_spec, out_shape=jax.ShapeDtypeStruct((4, Rh, Cs), BF16),
                          compiler_params=_cparams("parallel", "parallel"))(cvec, G, bufA)


def _sum_chips(G, bufA, bufB, cvec, svec, name):
    _, Rs, Cs = G.shape
    Rh = Rs // 2
    tr = _pick(Rh, 128, 16)
    nb = Rh // tr

    def body(c_ref, s_ref, g_ref, a_ref, b_ref, o_ref):
        o_ref[...] = (g_ref[0] + a_ref[0]) + b_ref[0].astype(F32) + b_ref[1].astype(F32) + b_ref[2].astype(F32)

    grid_spec = pltpu.PrefetchScalarGridSpec(
        num_scalar_prefetch=2, grid=(nb,),
        in_specs=[pl.BlockSpec((1, tr, Cs), lambda i, c, s: (s[0], c[0] * nb + i, 0)),
                  pl.BlockSpec((1, tr, Cs), lambda i, c, s: (s[0], i, 0)),
                  pl.BlockSpec((3, tr, Cs), lambda i, c, s: (0, i, 0))],
        out_specs=pl.BlockSpec((tr, Cs), lambda i, c, s: (i, 0)))
    return pl.pallas_call(body, name=name, grid_spec=grid_spec, out_shape=jax.ShapeDtypeStruct((Rh, Cs), F32),
                          compiler_params=_cparams("parallel"))(cvec, svec, G, bufA, bufB)


def _pair_plan(grads):
    Rh = [g.shape[1] // 2 for g in grads]
    whole3 = lambda x, y, c: (slice(None), slice(None), slice(None))
    stage = [(("in", n), lambda x, y, c, n=n: (slice(None), pl.ds((1 - c) * Rh[n], Rh[n]), slice(None)), n,
              whole3, (0, 0, 1)) for n in range(len(grads))]
    return [jax.ShapeDtypeStruct((4, Rh[n], g.shape[2]), F32) for n, g in enumerate(grads)], stage


def _chips_plan(P):
    stage = [(("in", n), lambda x, y, c, f=f: (2 * (x ^ f[0]) + (y ^ f[1]),), n, lambda x, y, c, k=k: (k,), f)
             for n in range(len(P)) for k, f in enumerate(FLIPS_CHIP)]
    return [jax.ShapeDtypeStruct((3,) + p.shape[1:], BF16) for p in P], stage


def _halves_plan(mine):
    whole2 = lambda x, y, c: (slice(None), slice(None))
    stage = [(("in", n), whole2, n, whole2, (0, 0, 1)) for n in range(len(mine))]
    return [jax.ShapeDtypeStruct(r.shape, F32) for r in mine], stage


def _adamw_halves(w, mine, other, m, v, cvec, name):
    Rs, Cs = w.shape
    Rh = Rs // 2
    tr = _pick(Rh, 128, 8)
    nb = Rh // tr

    def body(c_ref, w_ref, a_ref, b_ref, m_ref, v_ref, g_ref, d_ref, mo_ref, vo_ref):
        gv = jnp.where(pl.program_id(0) // nb == c_ref[0], a_ref[...], b_ref[...])
        g_ref[...] = gv
        d_ref[...], mo_ref[...], vo_ref[...] = _adam_math(w_ref[...], gv, m_ref[...], v_ref[...])

    row = pl.BlockSpec((tr, Cs), lambda i, c: (i, 0))
    hrow = pl.BlockSpec((tr, Cs), lambda i, c: (i % nb, 0))
    grid_spec = pltpu.PrefetchScalarGridSpec(num_scalar_prefetch=1, grid=(2 * nb,),
                                             in_specs=[row, hrow, hrow, row, row], out_specs=[row] * 4)
    return pl.pallas_call(body, name=name, grid_spec=grid_spec, out_shape=[jax.ShapeDtypeStruct((Rs, Cs), F32)] * 4,
                          compiler_params=_cparams("parallel"))(cvec, w, mine, other, m, v)


def _pack(arrays):
    flat = [a.reshape(-1).astype(F32) for a in arrays]
    meta, off = [], 0
    for a, f in zip(arrays, flat):
        meta.append((off, a.shape))
        off += f.shape[0]
    total = -(-off // (8 * LANES)) * (8 * LANES)
    flat.append(jnp.zeros((total - off,), F32))
    return jnp.concatenate(flat).reshape(total // LANES, LANES), meta


def _unpack(buf, meta):
    flat = buf.reshape(-1)
    out = []
    for off, shape in meta:
        size = 1
        for s in shape:
            size *= s
        out.append(flat[off:off + size].reshape(shape))
    return out


WEIGHT_NAMES = ["c_ctx", "w_mod", "b_mod", "g_mix", "w_in", "q_norm", "k_norm", "attn_sink", "w_gate_f", "b_gate_f",
                "w_gate_b", "b_gate_b", "gla_norm", "w_attn_o", "w_gla_o", "w_out", "g_ffn", "w_up", "conv_w",
                "conv_b", "w_down"]
BIG_NAMES = ["w_in", "w_attn_o", "w_gla_o", "w_out", "w_up", "w_down"]
SHARDED_SMALL = ["w_gate_f", "w_gate_b", "conv_w"]


def _layouts(D):
    aw, kvw, gk, gv = N_Q_HEADS * HEAD_DIM, N_KV_HEADS * HEAD_DIM, D // 2, D
    widths = {"qa": aw, "ka": kvw, "va": kvw, "qb": gk, "kb": gk, "vb": gv, "rb": gv, "lr": 2 * GLA_LOWRANK,
              "ga": D, "gb": D}
    orig, off = {}, 0
    for s in ["qa", "ka", "va", "qb", "kb", "vb", "rb", "lr", "ga", "gb"]:
        orig[s] = off
        off += widths[s]
    order = ["qa", "vb", "rb", "ga", "gb", "ka", "va", "qb", "kb", "lr"]
    lay, off = {}, 0
    for s in order:
        lay[s] = off
        off += LANES if s == "lr" else widths[s]
    align = {"qa": aw, "vb": D, "rb": D, "ga": D, "gb": D, "ka": kvw, "va": kvw, "qb": gk, "kb": gk,
             "lr": LANES}
    for s in order:
        assert lay[s] % align[s] == 0, (s, lay[s], align[s])
    return widths, orig, order, lay, off, -(-off // (2 * MXU_TILE)) * (2 * MXU_TILE)


def _rope_tables(T, L):
    t = jnp.arange(T)
    nf = HEAD_DIM // 4
    inv = ROPE_THETA ** (-jnp.arange(nf, dtype=F32) / nf)
    ang = jnp.concatenate([(t // GRID_W)[:, None] * inv, (t % GRID_W)[:, None] * inv], axis=-1)
    cos, sin = jnp.cos(ang), jnp.sin(ang)
    cos2 = jnp.concatenate([jnp.ones((L, HEAD_DIM), F32), jnp.concatenate([cos, cos], axis=-1)], axis=0)
    sin2 = jnp.concatenate([jnp.zeros((L, HEAD_DIM), F32), jnp.concatenate([-sin, sin], axis=-1)], axis=0)
    return cos2, sin2


def _step(x, c, ctx, loss_target, W, M, V):
    xi, yi, ci = lax.axis_index("x"), lax.axis_index("y"), lax.axis_index("c")
    chip = 2 * xi + yi
    dev = 2 * chip + ci
    south = (ci == 0).astype(F32)
    cvec = ci.reshape(1).astype(jnp.int32)
    svec = chip.reshape(1).astype(jnp.int32)
    T, D = x.shape[1], x.shape[2]
    L = ctx.shape[1]
    R = L + T
    F = 4 * W["w_down"].shape[1]
    GK, GV = D // 2, D
    DK, DV = GK // GLA_HEADS, GV // GLA_HEADS
    N6 = 6 * D
    N4 = N6 // 4
    widths, orig, order, lay, z_used, Z = _layouts(D)

    def place_cols(shard, full_cols):
        cols = shard.shape[-1]
        full = jnp.zeros(shard.shape[:-1] + (full_cols,), F32)
        return lax.dynamic_update_slice(full, shard * south, (0,) * (shard.ndim - 1) + (chip * cols,))

    c_rows = lax.dynamic_update_slice(jnp.zeros((8, D), F32), c, (dev, 0))
    bufa, meta = _pack([c_rows, place_cols(W["w_gate_f"][0], GK), place_cols(W["w_gate_b"][0], GK),
                        place_cols(W["conv_w"][0], 2 * F)])
    c_all, wgf, wgb, cw = _unpack(_allreduce(bufa, "gather_small"), meta)
    ca = jnp.concatenate([c_all, W["c_ctx"][None, :], jnp.zeros((7, D), F32)], axis=0)
    b_shard = lax.dynamic_slice(W["b_mod"], (0, chip * N4), (1, N4))
    mod_part, sil = _mod_fwd(ca, W["w_mod"][0], b_shard, "mod_fwd")
    slots = lax.dynamic_update_slice(jnp.zeros((4, 16, N4), F32), (mod_part * south)[None], (chip, 0, 0))
    mod_all = _allreduce(slots.reshape(64, N4), "gather_mod").reshape(4, 16, N4).transpose(1, 0, 2).reshape(16, N6)
    mx = lax.dynamic_slice(mod_all, (dev, 0), (1, N6)).reshape(6, 1, D)
    mc = mod_all[8].reshape(6, 1, D)

    sq = lambda a: a.reshape(a.shape[1:])
    shards = [sq(W[n]).astype(BF16) for n in BIG_NAMES]
    own = lambda g, s, n: _place_own(g, s, svec, "place_" + n)
    cols = lambda g: g.transpose(1, 0, 2).reshape(g.shape[1], 4 * g.shape[2])
    rows = lambda g: g.reshape(4 * g.shape[1], g.shape[2])
    w_in_f = cols(own(_allgather_weights(shards[:1], "gather_w_in")[0], shards[0], "w_in"))
    sc1 = jnp.stack([mc[1], mx[1]])
    sh1 = jnp.stack([mc[0], mx[0]])
    h = _modnorm_fwd(ctx[0], x[0], W["g_mix"], sc1, sh1, "modnorm1")
    seg = lambda s: w_in_f[:, orig[s]:orig[s] + widths[s]]
    w_cat = jnp.concatenate([jnp.pad(seg(s), ((0, 0), (0, LANES - widths[s]))) if s == "lr" else seg(s)
                             for s in order] + [jnp.zeros((D, Z - z_used), BF16)], axis=1)
    gather1, gather2, gather_outs = _gather_plan(shards[1:], "in")
    wg = jnp.zeros((2, LANES, GK), F32).at[0, :GLA_LOWRANK].set(wgf).at[1, GLA_LOWRANK:2 * GLA_LOWRANK].set(wgb)
    bg = jnp.stack([W["b_gate_f"], W["b_gate_b"]])
    cb = W["conv_b"]
    sink_rows = jnp.broadcast_to(W["attn_sink"][0][:, None], (N_Q_HEADS, HEAD_DIM))
    cos2, sin2 = _rope_tables(T, L)
    blk = lambda s, w: lay[s] // w

    z, landed = _matmul(h, w_cat, "nn", F32, "proj_in", tn=1536, ride=(shards[1:], gather_outs, gather1, {}))
    qn = _qknorm_fwd(z, blk("qa", widths["qa"]), T, L, W["q_norm"], cos2, sin2, N_Q_HEADS, "qnorm")
    kn = _qknorm_fwd(z, blk("ka", widths["ka"]), R, 0, W["k_norm"], cos2, sin2, N_KV_HEADS, "knorm")
    vb = _cast_seg(z, blk("va", widths["va"]), widths["va"], "vcast")
    (o_attn, lse), landed = _attn_fwd(qn, kn, vb, sink_rows, L, "attn_fwd",
                               ride=(landed, gather_outs, gather2, {n: n for n in range(len(landed))}))
    g_ao, g_go, g_out, g_up, g_dn = [own(g, s, n) for g, s, n in zip(landed, shards[1:], BIG_NAMES[1:])]
    w_ao, w_go, w_out, w_up, w_dn = rows(g_ao), rows(g_go), rows(g_out), cols(g_up), rows(g_dn)
    gla_blks = (blk("qb", GK), blk("kb", GK), blk("vb", GV), blk("lr", LANES))
    o_f, o_b, sprev = _gla_fwd(z, *gla_blks, wg, bg, DV, L, "gla_fwd")
    p = _glanorm_fwd(o_f, o_b, z, blk("rb", D), W["gla_norm"], L, "glanorm")
    ya = _matmul(o_attn, w_ao, "nn", BF16, "proj_attn_o")
    yg = _matmul(p, w_go, "nn", BF16, "proj_gla_o")
    m = _gate_fwd(z, blk("ga", D), blk("gb", D), ya, yg, L, "gate")
    mix = _matmul(m, w_out, "nn", F32, "proj_out")
    x1, h2 = _resnorm_fwd(x[0], mix, mx[2], W["g_ffn"], mx[4], mx[3], "resnorm2")
    u = _matmul(h2, w_up, "nn", F32, "ffn_up")
    f = _conv_fwd(u, cw, cb, "conv_swiglu")
    d = _matmul(f, w_dn, "nn", F32, "ffn_down", tk=2816)
    dy, dd, lacc = _loss_head(d, x1, mx[5], loss_target[0], "loss_head")
    loss = lax.psum((0.5 / D) * jnp.sum(lacc[0]), ("x", "y", "c"))

    gw_dn = _matmul(f, dd, "tn", F32, "ffn_down_dw")
    df = _matmul(dd, w_dn, "nt", F32, "ffn_down_dx")
    du, acca, accg = _conv_bwd(u, df, cw, cb, "conv_swiglu_bwd")
    gw_up = _matmul(h2, du, "tn", F32, "ffn_up_dw", tm=512, halves="b", col_shards=4)
    dh2 = _matmul(du, w_up, "nt", F32, "ffn_up_dx", tk=2816, halves="a")
    dx1, dmix, s2 = _resnorm_bwd(x1, dh2, W["g_ffn"], mx[4], dy, mix, mx[2], "resnorm2_bwd")
    gw_out = _matmul(m, dmix, "tn", F32, "proj_out_dw")
    dm = _matmul(dmix, w_out, "nt", BF16, "proj_out_dx")
    dya, dyg, dga, dgb = _gate_bwd(z, blk("ga", D), blk("gb", D), ya, yg, dm, L, "gate_bwd")
    gw_ao = _matmul(o_attn, dya, "tn", F32, "proj_attn_o_dw")
    do_attn = _matmul(dya, w_ao, "nt", BF16, "proj_attn_o_dx")
    gw_go = _matmul(p, dyg, "tn", F32, "proj_gla_o_dw")
    dp = _matmul(dyg, w_go, "nt", BF16, "proj_gla_o_dx")
    do_gla, drb, s_gn = _glanorm_bwd(o_f, o_b, z, blk("rb", D), W["gla_norm"], dp, L, "glanorm_bwd")
    do_pad = jnp.concatenate([jnp.zeros((L, GV), BF16), do_gla], axis=0)
    by_cols = lambda g: g.reshape(g.shape[0], 4, g.shape[1] // 4).transpose(1, 0, 2)
    by_rows = lambda g: g.reshape(4, g.shape[0] // 4, g.shape[1])
    early = [by_rows(gw_ao), by_rows(gw_go), by_rows(gw_out), gw_up, by_rows(gw_dn)]
    (dq_f, dk_f, dv_f, dpre_f, dq_b, dk_b, dv_b, dpre_b, dbg), pair_e = _gla_bwd(
        z, *gla_blks, wg, bg, sprev, do_pad, L, "gla_bwd", ride=(early, *_pair_plan(early), {}))
    dqg, dkg, dvg = (dq_f, dq_b), (dk_f, dk_b), (dv_f, dv_b)
    sums_e = [_add_pair(g, a, cvec, "reduce_early_add%d" % n) for n, (g, a) in enumerate(zip(early, pair_e))]
    wg_cat = jnp.concatenate([wg[0], wg[1]], axis=1)
    dpre = jnp.stack([dpre_f, dpre_b])
    dlr = _matmul(dpre, wg_cat, "nt", BF16, "gla_gate_dx", halves="a")
    dwg = _matmul(z[:, lay["lr"]:lay["lr"] + LANES], dpre, "tn", F32, "gla_gate_dw", halves="b")
    (dqn, dkw, dvw, dkc, dvc, dsn), chips_e = _attn_bwd(qn, kn, vb, sink_rows, lse, do_attn, L, "attn_bwd",
                                                        ride=(sums_e, *_chips_plan(sums_e), {}))
    mine_e = [_sum_chips(g, a, b, cvec, svec, "reduce_early_sum%d" % n)
              for n, (g, a, b) in enumerate(zip(early, pair_e, chips_e))]
    dqa, s_qn = _qknorm_bwd(z, blk("qa", widths["qa"]), T, L, W["q_norm"], cos2, sin2, dqn, N_Q_HEADS, "qnorm_bwd")
    dk_all = jnp.concatenate([dkc, dkw[WINDOW:WINDOW + T]], axis=0)
    dv_all = jnp.concatenate([dvc, dvw[WINDOW:WINDOW + T]], axis=0)
    dka, s_kn = _qknorm_bwd(z, blk("ka", widths["ka"]), R, 0, W["k_norm"], cos2, sin2, dk_all, N_KV_HEADS, "knorm_bwd")
    dz = _assemble_dz(lay, z_used, Z, L, dqa, drb, dga, dgb, dka, dv_all, dvg, dqg, dkg, dlr, "assemble_dz")
    gw_cat, other_e = _matmul(h, dz, "tn", F32, "proj_in_dw", tn=768, tk=2816,
                              ride=(mine_e, *_halves_plan(mine_e), {}))
    gw_in = jnp.concatenate([gw_cat[:, lay[s]:lay[s] + widths[s]] for s in ["qa", "ka", "va", "qb", "kb", "vb", "rb",
                                                                           "lr", "ga", "gb"]], axis=1)
    late = [by_cols(gw_in)]
    shapes, stage = _pair_plan(late)
    pair_l = _exchange(late, shapes, [stage], "reduce_late_pair")
    sums_l = [_add_pair(late[0], pair_l[0], cvec, "reduce_late_add")]
    dh, chips_l = _matmul(dz, w_cat, "nt", F32, "proj_in_dx", tk=4608, ride=(sums_l, *_chips_plan(sums_l), {}))
    mine_l = [_sum_chips(late[0], pair_l[0], chips_l[0], cvec, svec, "reduce_late_sum")]
    shapes, stage = _halves_plan(mine_l)
    other_l = _exchange(mine_l, shapes, [stage], "reduce_late_halves")
    mine, other = mine_l + mine_e, list(other_l) + other_e
    grad_x, s1 = _modnorm_bwd(x[0], dh, W["g_mix"], mx[1], dx1, "modnorm1_bwd", dh_roff=L)
    _, s1c = _modnorm_bwd(ctx[0], dh, W["g_mix"], mc[1], None, "modnorm1_ctx_bwd")

    dmod_x = jnp.concatenate([s1[0], s1[1], s2[3], s2[0], s2[1], lacc[1]])
    dmod_c = jnp.concatenate([s1c[0], s1c[1], jnp.zeros((4 * D,), F32)])
    dmod_rows = lax.dynamic_update_slice(jnp.zeros((9, N6), F32).at[8].set(dmod_c), dmod_x[None], (dev, 0))
    small = [dmod_rows, dmod_x + dmod_c, s1[2] + s1c[2], s_qn[0], s_kn[0], dsn[:, 0, :Q_PER_KV].reshape(N_Q_HEADS),
             dwg[:GLA_LOWRANK, :GK], dbg[0].reshape(GK), dwg[GLA_LOWRANK:2 * GLA_LOWRANK, GK:], dbg[1].reshape(GK),
             s_gn[0], s2[2], jnp.concatenate([acca[0:3], accg[0:3]], axis=1), jnp.concatenate([acca[3], accg[3]])]
    bufc, meta = _pack(small)
    (dmod_sum, g_b_mod, g_g_mix, g_q_norm, g_k_norm, g_sink, g_wgf, g_bgf, g_wgb, g_bgb, g_gla_norm, g_g_ffn,
     g_conv_w, g_conv_b) = _unpack(_allreduce(bufc, "reduce_small"), meta)
    dmod16 = lax.dynamic_slice(jnp.concatenate([dmod_sum, jnp.zeros((7, N6), F32)], axis=0), (0, chip * N4), (16, N4))
    g_w_mod = _matmul(sil, dmod16, "tn", F32, "mod_dw")
    dsil = _matmul(dmod16, W["w_mod"][0], "nt", F32, "mod_dx")
    g_c_ctx = _silu_bwd(_allreduce(dsil * south, "reduce_cctx"), ca, "silu_bwd")[8]

    cut = lambda g: lax.dynamic_slice(g, (0, chip * (g.shape[1] // 4)), (g.shape[0], g.shape[1] // 4))
    grads = {"c_ctx": g_c_ctx, "w_mod": g_w_mod[None], "b_mod": g_b_mod[None], "g_mix": g_g_mix[None],
             "q_norm": g_q_norm[None], "k_norm": g_k_norm[None], "attn_sink": g_sink[None],
             "w_gate_f": cut(g_wgf)[None], "b_gate_f": g_bgf[None], "w_gate_b": cut(g_wgb)[None],
             "b_gate_b": g_bgb[None], "gla_norm": g_gla_norm[None], "g_ffn": g_g_ffn[None],
             "conv_w": cut(g_conv_w)[None], "conv_b": g_conv_b[None]}

    delta, new_m, new_v = {}, {}, {}
    dl, mn, vn = _adamw(W["w_mod"][0], g_w_mod, M["w_mod"][0], V["w_mod"][0], "adamw_w_mod")
    delta["w_mod"], new_m["w_mod"], new_v["w_mod"] = dl[None], mn[None], vn[None]
    for n, a, b in zip(BIG_NAMES, mine, other):
        g, dl, mn, vn = _adamw_halves(sq(W[n]), a, b, sq(M[n]), sq(V[n]), cvec, "adamw_" + n)
        grads[n], delta[n], new_m[n], new_v[n] = g[None], dl[None], mn[None], vn[None]
    small_names = [n for n in WEIGHT_NAMES if n not in delta]
    packs = [_pack([src[n] for n in small_names]) for src in (W, grads, M, V)]
    meta = packs[0][1]
    outs = _adamw(packs[0][0], packs[1][0], packs[2][0], packs[3][0], "adamw_small")
    for res, o in zip((delta, new_m, new_v), outs):
        for n, a in zip(small_names, _unpack(o, meta)):
            res[n] = a
    return (loss, grad_x[None], *[grads[n] for n in WEIGHT_NAMES], *[delta[n] for n in WEIGHT_NAMES],
            *[new_m[n] for n in WEIGHT_NAMES], *[new_v[n] for n in WEIGHT_NAMES])


def kernel(x, c, ctx, c_ctx, w_mod, b_mod, g_mix, w_in, q_norm, k_norm, attn_sink, w_gate_f, b_gate_f, w_gate_b, b_gate_b, gla_norm, w_attn_o, w_gla_o, w_out, g_ffn, w_up, conv_w, conv_b, w_down, loss_target, m_c_ctx, m_w_mod, m_b_mod, m_g_mix, m_w_in, m_q_norm, m_k_norm, m_attn_sink, m_w_gate_f, m_b_gate_f, m_w_gate_b, m_b_gate_b, m_gla_norm, m_w_attn_o, m_w_gla_o, m_w_out, m_g_ffn, m_w_up, m_conv_w, m_conv_b, m_w_down, v_c_ctx, v_w_mod, v_b_mod, v_g_mix, v_w_in, v_q_norm, v_k_norm, v_attn_sink, v_w_gate_f, v_b_gate_f, v_w_gate_b, v_b_gate_b, v_gla_norm, v_w_attn_o, v_w_gla_o, v_w_out, v_g_ffn, v_w_up, v_conv_w, v_conv_b, v_w_down):
    W = dict(zip(WEIGHT_NAMES, (c_ctx, w_mod, b_mod, g_mix, w_in, q_norm, k_norm, attn_sink, w_gate_f, b_gate_f,
                                w_gate_b, b_gate_b, gla_norm, w_attn_o, w_gla_o, w_out, g_ffn, w_up, conv_w, conv_b,
                                w_down)))
    M = dict(zip(WEIGHT_NAMES, (m_c_ctx, m_w_mod, m_b_mod, m_g_mix, m_w_in, m_q_norm, m_k_norm, m_attn_sink,
                                m_w_gate_f, m_b_gate_f, m_w_gate_b, m_b_gate_b, m_gla_norm, m_w_attn_o, m_w_gla_o,
                                m_w_out, m_g_ffn, m_w_up, m_conv_w, m_conv_b, m_w_down)))
    V = dict(zip(WEIGHT_NAMES, (v_c_ctx, v_w_mod, v_b_mod, v_g_mix, v_w_in, v_q_norm, v_k_norm, v_attn_sink,
                                v_w_gate_f, v_b_gate_f, v_w_gate_b, v_b_gate_b, v_gla_norm, v_w_attn_o, v_w_gla_o,
                                v_w_out, v_g_ffn, v_w_up, v_conv_w, v_conv_b, v_w_down)))
    return _step(x, c, ctx, loss_target, W, M, V)
```
